```python
import jax, jax.numpy as jnp
from jax import lax
import numpy as np

D_MODEL = 1024
BATCH = 8
SEQ = 4096
DEPTH = 2

EPS = 1e-6
CHUNK = 128
HEAD_DIM = 128
A_HEADS = 4
B_HEADS = 4
D_A = A_HEADS * HEAD_DIM
D_B = B_HEADS * HEAD_DIM
D_AB = D_A + D_B
B_CONV = 31
D_C = D_MODEL
C_CONV = 3
D_FF = 2816
F_CONV = 3
N_EVEN = (DEPTH + 1) // 2
N_ODD = DEPTH // 2

kernel_name = "hybrid_sgu_conformer_shortconv_convffn"


def _rmsnorm(x, g):
    xf = x.astype(jnp.float32)
    y = xf * lax.rsqrt(jnp.mean(xf * xf, axis=-1, keepdims=True) + EPS)
    return (y * g.astype(jnp.float32)).astype(x.dtype)


def _layernorm(x, g, b):
    xf = x.astype(jnp.float32)
    mu = jnp.mean(xf, axis=-1, keepdims=True)
    xc = xf - mu
    var = jnp.mean(xc * xc, axis=-1, keepdims=True)
    y = xc * lax.rsqrt(var + EPS) * g.astype(jnp.float32) + b.astype(jnp.float32)
    return y.astype(x.dtype)


def _causal_dwconv(x, w):
    k, c = w.shape
    return lax.conv_general_dilated(
        x, w[:, None, :].astype(x.dtype), window_strides=(1,),
        padding=[(k - 1, 0)], dimension_numbers=("NWC", "WIO", "NWC"),
        feature_group_count=c)


def _chunked_sgu(u, v, ln_g, ln_b, w_s, b_s):
    bsz, s, _ = v.shape
    v = _layernorm(v, ln_g, ln_b)
    vc = v.reshape(bsz, s // CHUNK, CHUNK, A_HEADS, HEAD_DIM)
    causal = jnp.tril(jnp.ones((CHUNK, CHUNK), dtype=w_s.dtype))
    mixed = jnp.einsum("hts,bnshd->bnthd", w_s * causal, vc)
    mixed = mixed + b_s.T[None, None, :, :, None]
    return u * mixed.reshape(bsz, s, D_A)


def _even_mixer(h, w_in, a_ln_g, a_ln_b, a_w_s, a_b_s, b_conv_w, b_conv_b, b_ln_g, b_ln_b, w_out):
    z = h @ w_in
    ua, va, xb, gb = jnp.split(z, [D_A, 2 * D_A, 2 * D_A + D_B], axis=-1)
    ya = _chunked_sgu(jax.nn.gelu(ua, approximate=False), jax.nn.gelu(va, approximate=False),
                      a_ln_g, a_ln_b, a_w_s, a_b_s)
    yb = xb * jax.nn.sigmoid(gb)
    yb = _causal_dwconv(yb, b_conv_w) + b_conv_b
    yb = jax.nn.silu(_layernorm(yb, b_ln_g, b_ln_b))
    return jnp.concatenate([ya, yb], axis=-1) @ w_out


def _odd_mixer(h, w_in, conv_w, w_out):
    b_gate, c_gate, xv = jnp.split(h @ w_in, 3, axis=-1)
    return (b_gate * _causal_dwconv(c_gate * xv, conv_w)) @ w_out


def _conv_ffn(h, w_up, conv_w, w_down):
    up = _causal_dwconv(h @ w_up, conv_w)
    g, v = jnp.split(up, 2, axis=-1)
    return (jax.nn.silu(g) * v) @ w_down


def _fwd_setup_inputs(seed: int = 0) -> dict:
    key = jax.random.key(seed)
    ks = jax.random.split(key, 24)
    f32 = jnp.float32

    def nrm(k, shape, scale):
        return jax.random.normal(k, shape, f32) * scale

    def gain(k, shape):
        return 1.0 + 0.02 * jax.random.normal(k, shape, f32)

    return {
        "x": jax.random.normal(ks[0], (BATCH, SEQ, D_MODEL), f32),
        "norm_mix": gain(ks[1], (DEPTH, D_MODEL)),
        "norm_ffn": gain(ks[2], (DEPTH, D_MODEL)),
        "norm_final": gain(ks[3], (D_MODEL,)),
        "ab_w_in": nrm(ks[4], (N_EVEN, D_MODEL, 2 * D_AB), D_MODEL ** -0.5),
        "a_ln_g": gain(ks[5], (N_EVEN, D_A)),
        "a_ln_b": nrm(ks[6], (N_EVEN, D_A), 0.02),
        "a_w_s": nrm(ks[7], (N_EVEN, A_HEADS, CHUNK, CHUNK), CHUNK ** -0.5),
        "a_b_s": gain(ks[8], (N_EVEN, A_HEADS, CHUNK)),
        "b_conv_w": nrm(ks[9], (N_EVEN, B_CONV, D_B), B_CONV ** -0.5),
        "b_conv_b": nrm(ks[10], (N_EVEN, D_B), 0.02),
        "b_ln_g": gain(ks[11], (N_EVEN, D_B)),
        "b_ln_b": nrm(ks[12], (N_EVEN, D_B), 0.02),
        "ab_w_out": nrm(ks[13], (N_EVEN, D_AB, D_MODEL), D_AB ** -0.5),
        "c_w_in": nrm(ks[14], (N_ODD, D_MODEL, 3 * D_C), D_MODEL ** -0.5),
        "c_conv_w": nrm(ks[15], (N_ODD, C_CONV, D_C), C_CONV ** -0.5),
        "c_w_out": nrm(ks[16], (N_ODD, D_C, D_MODEL), D_C ** -0.5),
        "f_w_up": nrm(ks[17], (DEPTH, D_MODEL, 2 * D_FF), D_MODEL ** -0.5),
        "f_conv_w": nrm(ks[18], (DEPTH, F_CONV, 2 * D_FF), F_CONV ** -0.5),
        "f_w_down": nrm(ks[19], (DEPTH, D_FF, D_MODEL), D_FF ** -0.5),
    }


def _fwd_reference(x, norm_mix, norm_ffn, norm_final, ab_w_in, a_ln_g, a_ln_b, a_w_s, a_b_s,
              b_conv_w, b_conv_b, b_ln_g, b_ln_b, ab_w_out, c_w_in, c_conv_w, c_w_out,
              f_w_up, f_conv_w, f_w_down):
    for layer in range(DEPTH):
        h = _rmsnorm(x, norm_mix[layer])
        if layer % 2 == 0:
            i = layer // 2
            y = _even_mixer(h, ab_w_in[i], a_ln_g[i], a_ln_b[i], a_w_s[i], a_b_s[i],
                            b_conv_w[i], b_conv_b[i], b_ln_g[i], b_ln_b[i], ab_w_out[i])
        else:
            i = layer // 2
            y = _odd_mixer(h, c_w_in[i], c_conv_w[i], c_w_out[i])
        x = x + y
        h = _rmsnorm(x, norm_ffn[layer])
        x = x + _conv_ffn(h, f_w_up[layer], f_conv_w[layer], f_w_down[layer])
    return _rmsnorm(x, norm_final)


import jax as _jax
import jax.numpy as _jnp

TWIN_FORMAT = 'train_step'
FWD_PARAMS = ['x', 'norm_mix', 'norm_ffn', 'norm_final', 'ab_w_in', 'a_ln_g', 'a_ln_b', 'a_w_s', 'a_b_s', 'b_conv_w', 'b_conv_b', 'b_ln_g', 'b_ln_b', 'ab_w_out', 'c_w_in', 'c_conv_w', 'c_w_out', 'f_w_up', 'f_conv_w', 'f_w_down']
TWIN_WEIGHTS = ['norm_mix', 'norm_ffn', 'norm_final', 'ab_w_in', 'a_ln_g', 'a_ln_b', 'a_w_s', 'a_b_s', 'b_conv_w', 'b_conv_b', 'b_ln_g', 'b_ln_b', 'ab_w_out', 'c_w_in', 'c_conv_w', 'c_w_out', 'f_w_up', 'f_conv_w', 'f_w_down']
TWIN_DIFF_INPUT = 'x'
TWIN_INPUTS = ['x', 'norm_mix', 'norm_ffn', 'norm_final', 'ab_w_in', 'a_ln_g', 'a_ln_b', 'a_w_s', 'a_b_s', 'b_conv_w', 'b_conv_b', 'b_ln_g', 'b_ln_b', 'ab_w_out', 'c_w_in', 'c_conv_w', 'c_w_out', 'f_w_up', 'f_conv_w', 'f_w_down', 'loss_target', 'm_norm_mix', 'm_norm_ffn', 'm_norm_final', 'm_ab_w_in', 'm_a_ln_g', 'm_a_ln_b', 'm_a_w_s', 'm_a_b_s', 'm_b_conv_w', 'm_b_conv_b', 'm_b_ln_g', 'm_b_ln_b', 'm_ab_w_out', 'm_c_w_in', 'm_c_conv_w', 'm_c_w_out', 'm_f_w_up', 'm_f_conv_w', 'm_f_w_down', 'v_norm_mix', 'v_norm_ffn', 'v_norm_final', 'v_ab_w_in', 'v_a_ln_g', 'v_a_ln_b', 'v_a_w_s', 'v_a_b_s', 'v_b_conv_w', 'v_b_conv_b', 'v_b_ln_g', 'v_b_ln_b', 'v_ab_w_out', 'v_c_w_in', 'v_c_conv_w', 'v_c_w_out', 'v_f_w_up', 'v_f_conv_w', 'v_f_w_down']
TWIN_OUTPUTS = ['loss', 'grad_x', 'grad_norm_mix', 'grad_norm_ffn', 'grad_norm_final', 'grad_ab_w_in', 'grad_a_ln_g', 'grad_a_ln_b', 'grad_a_w_s', 'grad_a_b_s', 'grad_b_conv_w', 'grad_b_conv_b', 'grad_b_ln_g', 'grad_b_ln_b', 'grad_ab_w_out', 'grad_c_w_in', 'grad_c_conv_w', 'grad_c_w_out', 'grad_f_w_up', 'grad_f_conv_w', 'grad_f_w_down', 'delta_norm_mix', 'delta_norm_ffn', 'delta_norm_final', 'delta_ab_w_in', 'delta_a_ln_g', 'delta_a_ln_b', 'delta_a_w_s', 'delta_a_b_s', 'delta_b_conv_w', 'delta_b_conv_b', 'delta_b_ln_g', 'delta_b_ln_b', 'delta_ab_w_out', 'delta_c_w_in', 'delta_c_conv_w', 'delta_c_w_out', 'delta_f_w_up', 'delta_f_conv_w', 'delta_f_w_down', 'new_m_norm_mix', 'new_m_norm_ffn', 'new_m_norm_final', 'new_m_ab_w_in', 'new_m_a_ln_g', 'new_m_a_ln_b', 'new_m_a_w_s', 'new_m_a_b_s', 'new_m_b_conv_w', 'new_m_b_conv_b', 'new_m_b_ln_g', 'new_m_b_ln_b', 'new_m_ab_w_out', 'new_m_c_w_in', 'new_m_c_conv_w', 'new_m_c_w_out', 'new_m_f_w_up', 'new_m_f_conv_w', 'new_m_f_w_down', 'new_v_norm_mix', 'new_v_norm_ffn', 'new_v_norm_final', 'new_v_ab_w_in', 'new_v_a_ln_g', 'new_v_a_ln_b', 'new_v_a_w_s', 'new_v_a_b_s', 'new_v_b_conv_w', 'new_v_b_conv_b', 'new_v_b_ln_g', 'new_v_b_ln_b', 'new_v_ab_w_out', 'new_v_c_w_in', 'new_v_c_conv_w', 'new_v_c_w_out', 'new_v_f_w_up', 'new_v_f_conv_w', 'new_v_f_w_down']
TWIN_LEAF_KINDS = {'loss': 'loss', 'grad_x': 'grad_x', 'grad_norm_mix': 'grad_w', 'grad_norm_ffn': 'grad_w', 'grad_norm_final': 'grad_w', 'grad_ab_w_in': 'grad_w', 'grad_a_ln_g': 'grad_w', 'grad_a_ln_b': 'grad_w', 'grad_a_w_s': 'grad_w', 'grad_a_b_s': 'grad_w', 'grad_b_conv_w': 'grad_w', 'grad_b_conv_b': 'grad_w', 'grad_b_ln_g': 'grad_w', 'grad_b_ln_b': 'grad_w', 'grad_ab_w_out': 'grad_w', 'grad_c_w_in': 'grad_w', 'grad_c_conv_w': 'grad_w', 'grad_c_w_out': 'grad_w', 'grad_f_w_up': 'grad_w', 'grad_f_conv_w': 'grad_w', 'grad_f_w_down': 'grad_w', 'delta_norm_mix': 'delta_w', 'delta_norm_ffn': 'delta_w', 'delta_norm_final': 'delta_w', 'delta_ab_w_in': 'delta_w', 'delta_a_ln_g': 'delta_w', 'delta_a_ln_b': 'delta_w', 'delta_a_w_s': 'delta_w', 'delta_a_b_s': 'delta_w', 'delta_b_conv_w': 'delta_w', 'delta_b_conv_b': 'delta_w', 'delta_b_ln_g': 'delta_w', 'delta_b_ln_b': 'delta_w', 'delta_ab_w_out': 'delta_w', 'delta_c_w_in': 'delta_w', 'delta_c_conv_w': 'delta_w', 'delta_c_w_out': 'delta_w', 'delta_f_w_up': 'delta_w', 'delta_f_conv_w': 'delta_w', 'delta_f_w_down': 'delta_w', 'new_m_norm_mix': 'new_m', 'new_m_norm_ffn': 'new_m', 'new_m_norm_final': 'new_m', 'new_m_ab_w_in': 'new_m', 'new_m_a_ln_g': 'new_m', 'new_m_a_ln_b': 'new_m', 'new_m_a_w_s': 'new_m', 'new_m_a_b_s': 'new_m', 'new_m_b_conv_w': 'new_m', 'new_m_b_conv_b': 'new_m', 'new_m_b_ln_g': 'new_m', 'new_m_b_ln_b': 'new_m', 'new_m_ab_w_out': 'new_m', 'new_m_c_w_in': 'new_m', 'new_m_c_conv_w': 'new_m', 'new_m_c_w_out': 'new_m', 'new_m_f_w_up': 'new_m', 'new_m_f_conv_w': 'new_m', 'new_m_f_w_down': 'new_m', 'new_v_norm_mix': 'new_v', 'new_v_norm_ffn': 'new_v', 'new_v_norm_final': 'new_v', 'new_v_ab_w_in': 'new_v', 'new_v_a_ln_g': 'new_v', 'new_v_a_ln_b': 'new_v', 'new_v_a_w_s': 'new_v', 'new_v_a_b_s': 'new_v', 'new_v_b_conv_w': 'new_v', 'new_v_b_conv_b': 'new_v', 'new_v_b_ln_g': 'new_v', 'new_v_b_ln_b': 'new_v', 'new_v_ab_w_out': 'new_v', 'new_v_c_w_in': 'new_v', 'new_v_c_conv_w': 'new_v', 'new_v_c_w_out': 'new_v', 'new_v_f_w_up': 'new_v', 'new_v_f_conv_w': 'new_v', 'new_v_f_w_down': 'new_v'}


def _forward(args):
    return _fwd_reference(*[args[k] for k in FWD_PARAMS])


def _output_shape():
    def fwd():
        inp = _fwd_setup_inputs(0)
        return _fwd_reference(*[inp[k] for k in FWD_PARAMS])
    out = _jax.eval_shape(fwd)
    return out.shape, out.dtype

N_MICROBATCH = 1
ADAM_LR = 0.001
ADAM_B1 = 0.9
ADAM_B2 = 0.999
ADAM_EPS = 1e-08
ADAM_WD = 0.01
ADAM_STEP = 10
PER_EXAMPLE_BATCH_AXIS = {'x': 0, 'loss_target': 0}
SHARED_INPUTS = []
_WEIGHT_DTYPES = {'norm_mix': _jnp.float32, 'norm_ffn': _jnp.float32, 'norm_final': _jnp.float32, 'ab_w_in': _jnp.float32, 'a_ln_g': _jnp.float32, 'a_ln_b': _jnp.float32, 'a_w_s': _jnp.float32, 'a_b_s': _jnp.float32, 'b_conv_w': _jnp.float32, 'b_conv_b': _jnp.float32, 'b_ln_g': _jnp.float32, 'b_ln_b': _jnp.float32, 'ab_w_out': _jnp.float32, 'c_w_in': _jnp.float32, 'c_conv_w': _jnp.float32, 'c_w_out': _jnp.float32, 'f_w_up': _jnp.float32, 'f_conv_w': _jnp.float32, 'f_w_down': _jnp.float32}
MOMENT_SCALE = {'norm_mix': 1.898258e-01, 'norm_ffn': 1.263863e-01, 'norm_final': 3.199297e+01, 'ab_w_in': 1.300047e-01, 'a_ln_g': 1.004669e-01, 'a_ln_b': 1.123781e-01, 'a_w_s': 9.937841e-02, 'a_b_s': 1.454300e-01, 'b_conv_w': 1.362496e-01, 'b_conv_b': 2.894476e-01, 'b_ln_g': 1.588724e-01, 'b_ln_b': 1.540432e-01, 'ab_w_out': 1.559634e-01, 'c_w_in': 1.115113e-01, 'c_conv_w': 1.107553e-01, 'c_w_out': 1.115303e-01, 'f_w_up': 5.304851e-02, 'f_conv_w': 5.389155e-02, 'f_w_down': 8.697905e-02}


def _to_microbatches(a, axis):
    t = _jnp.moveaxis(a, axis, 0)
    t = t.reshape((N_MICROBATCH, t.shape[0] // N_MICROBATCH) + t.shape[1:])
    return _jnp.moveaxis(t, 1, axis + 1)


def setup_inputs(seed: int = 0) -> dict:
    inp = _fwd_setup_inputs(seed)
    key = _jax.random.fold_in(_jax.random.key(seed), 7919)
    shape, _ = _output_shape()
    out = dict(inp)
    out["loss_target"] = _jax.random.normal(_jax.random.fold_in(key, 0), shape, _jnp.float32)
    for i, name in enumerate(TWIN_WEIGHTS):
        w = inp[name].astype(_jnp.float32)
        if MOMENT_SCALE is None:
            s = _jnp.sqrt(_jnp.mean(_jnp.square(w)) + 1e-30)
        else:
            s = MOMENT_SCALE[name]
        km, kv = _jax.random.split(_jax.random.fold_in(key, i + 1))
        out[name] = w
        out["m_" + name] = s * _jax.random.normal(km, w.shape, _jnp.float32)
        out["v_" + name] = (s * s) * _jax.random.uniform(kv, w.shape, _jnp.float32, 0.5, 1.5)
    if N_MICROBATCH > 1:
        for name, axis in PER_EXAMPLE_BATCH_AXIS.items():
            out[name] = _to_microbatches(out[name], axis)
    return {'x': out['x'], 'norm_mix': out['norm_mix'], 'norm_ffn': out['norm_ffn'], 'norm_final': out['norm_final'], 'ab_w_in': out['ab_w_in'], 'a_ln_g': out['a_ln_g'], 'a_ln_b': out['a_ln_b'], 'a_w_s': out['a_w_s'], 'a_b_s': out['a_b_s'], 'b_conv_w': out['b_conv_w'], 'b_conv_b': out['b_conv_b'], 'b_ln_g': out['b_ln_g'], 'b_ln_b': out['b_ln_b'], 'ab_w_out': out['ab_w_out'], 'c_w_in': out['c_w_in'], 'c_conv_w': out['c_conv_w'], 'c_w_out': out['c_w_out'], 'f_w_up': out['f_w_up'], 'f_conv_w': out['f_conv_w'], 'f_w_down': out['f_w_down'], 'loss_target': out['loss_target'], 'm_norm_mix': out['m_norm_mix'], 'm_norm_ffn': out['m_norm_ffn'], 'm_norm_final': out['m_norm_final'], 'm_ab_w_in': out['m_ab_w_in'], 'm_a_ln_g': out['m_a_ln_g'], 'm_a_ln_b': out['m_a_ln_b'], 'm_a_w_s': out['m_a_w_s'], 'm_a_b_s': out['m_a_b_s'], 'm_b_conv_w': out['m_b_conv_w'], 'm_b_conv_b': out['m_b_conv_b'], 'm_b_ln_g': out['m_b_ln_g'], 'm_b_ln_b': out['m_b_ln_b'], 'm_ab_w_out': out['m_ab_w_out'], 'm_c_w_in': out['m_c_w_in'], 'm_c_conv_w': out['m_c_conv_w'], 'm_c_w_out': out['m_c_w_out'], 'm_f_w_up': out['m_f_w_up'], 'm_f_conv_w': out['m_f_conv_w'], 'm_f_w_down': out['m_f_w_down'], 'v_norm_mix': out['v_norm_mix'], 'v_norm_ffn': out['v_norm_ffn'], 'v_norm_final': out['v_norm_final'], 'v_ab_w_in': out['v_ab_w_in'], 'v_a_ln_g': out['v_a_ln_g'], 'v_a_ln_b': out['v_a_ln_b'], 'v_a_w_s': out['v_a_w_s'], 'v_a_b_s': out['v_a_b_s'], 'v_b_conv_w': out['v_b_conv_w'], 'v_b_conv_b': out['v_b_conv_b'], 'v_b_ln_g': out['v_b_ln_g'], 'v_b_ln_b': out['v_b_ln_b'], 'v_ab_w_out': out['v_ab_w_out'], 'v_c_w_in': out['v_c_w_in'], 'v_c_conv_w': out['v_c_conv_w'], 'v_c_w_out': out['v_c_w_out'], 'v_f_w_up': out['v_f_w_up'], 'v_f_conv_w': out['v_f_conv_w'], 'v_f_w_down': out['v_f_w_down']}


def _loss(weights, diff, rest, loss_target):
    with _jax.named_scope("forward"):
        args = {**rest, TWIN_DIFF_INPUT: diff, **{k: w.astype(_WEIGHT_DTYPES[k]) for k, w in weights.items()}}
        y = _forward(args)
    with _jax.named_scope("loss_head"):
        err = _jnp.square(y.astype(_jnp.float32) - loss_target)
        return 0.5 * _jnp.sum(_jnp.mean(err, axis=-1)) if err.ndim else 0.5 * err


def _adamw(w, g, m, v):
    m = ADAM_B1 * m + (1.0 - ADAM_B1) * g
    v = ADAM_B2 * v + (1.0 - ADAM_B2) * _jnp.square(g)
    m_hat = m / (1.0 - ADAM_B1 ** ADAM_STEP)
    v_hat = v / (1.0 - ADAM_B2 ** ADAM_STEP)
    delta = -ADAM_LR * (m_hat / (_jnp.sqrt(v_hat) + ADAM_EPS) + ADAM_WD * w)
    return delta, m, v


def reference(x, norm_mix, norm_ffn, norm_final, ab_w_in, a_ln_g, a_ln_b, a_w_s, a_b_s, b_conv_w, b_conv_b, b_ln_g, b_ln_b, ab_w_out, c_w_in, c_conv_w, c_w_out, f_w_up, f_conv_w, f_w_down, loss_target, m_norm_mix, m_norm_ffn, m_norm_final, m_ab_w_in, m_a_ln_g, m_a_ln_b, m_a_w_s, m_a_b_s, m_b_conv_w, m_b_conv_b, m_b_ln_g, m_b_ln_b, m_ab_w_out, m_c_w_in, m_c_conv_w, m_c_w_out, m_f_w_up, m_f_conv_w, m_f_w_down, v_norm_mix, v_norm_ffn, v_norm_final, v_ab_w_in, v_a_ln_g, v_a_ln_b, v_a_w_s, v_a_b_s, v_b_conv_w, v_b_conv_b, v_b_ln_g, v_b_ln_b, v_ab_w_out, v_c_w_in, v_c_conv_w, v_c_w_out, v_f_w_up, v_f_conv_w, v_f_w_down):
    given = dict(x=x, norm_mix=norm_mix, norm_ffn=norm_ffn, norm_final=norm_final, ab_w_in=ab_w_in, a_ln_g=a_ln_g, a_ln_b=a_ln_b, a_w_s=a_w_s, a_b_s=a_b_s, b_conv_w=b_conv_w, b_conv_b=b_conv_b, b_ln_g=b_ln_g, b_ln_b=b_ln_b, ab_w_out=ab_w_out, c_w_in=c_w_in, c_conv_w=c_conv_w, c_w_out=c_w_out, f_w_up=f_w_up, f_conv_w=f_conv_w, f_w_down=f_w_down, loss_target=loss_target, m_norm_mix=m_norm_mix, m_norm_ffn=m_norm_ffn, m_norm_final=m_norm_final, m_ab_w_in=m_ab_w_in, m_a_ln_g=m_a_ln_g, m_a_ln_b=m_a_ln_b, m_a_w_s=m_a_w_s, m_a_b_s=m_a_b_s, m_b_conv_w=m_b_conv_w, m_b_conv_b=m_b_conv_b, m_b_ln_g=m_b_ln_g, m_b_ln_b=m_b_ln_b, m_ab_w_out=m_ab_w_out, m_c_w_in=m_c_w_in, m_c_conv_w=m_c_conv_w, m_c_w_out=m_c_w_out, m_f_w_up=m_f_w_up, m_f_conv_w=m_f_conv_w, m_f_w_down=m_f_w_down, v_norm_mix=v_norm_mix, v_norm_ffn=v_norm_ffn, v_norm_final=v_norm_final, v_ab_w_in=v_ab_w_in, v_a_ln_g=v_a_ln_g, v_a_ln_b=v_a_ln_b, v_a_w_s=v_a_w_s, v_a_b_s=v_a_b_s, v_b_conv_w=v_b_conv_w, v_b_conv_b=v_b_conv_b, v_b_ln_g=v_b_ln_g, v_b_ln_b=v_b_ln_b, v_ab_w_out=v_ab_w_out, v_c_w_in=v_c_w_in, v_c_conv_w=v_c_conv_w, v_c_w_out=v_c_w_out, v_f_w_up=v_f_w_up, v_f_conv_w=v_f_conv_w, v_f_w_down=v_f_w_down)
    weights = {n: given[n] for n in TWIN_WEIGHTS}
    shared = {n: given[n] for n in SHARED_INPUTS}
    per_example = {n: given[n] for n in ['x']}
    grad_fn = _jax.value_and_grad(_loss, argnums=(0, 1))

    def one_microbatch(ex, loss_target):
        ex = dict(ex)
        diff = ex.pop(TWIN_DIFF_INPUT)
        return grad_fn(weights, diff, {**shared, **ex}, loss_target)

    if N_MICROBATCH == 1:
        loss, (grad_w, grad_x) = one_microbatch(per_example, given["loss_target"])
    else:
        def body(carry, xs):
            loss_sum, grad_sum = carry
            l_k, (gw_k, gx_k) = one_microbatch(xs[0], xs[1])
            with _jax.named_scope("update"):
                return (loss_sum + l_k, _jax.tree.map(_jnp.add, grad_sum, gw_k)), gx_k

        init = (_jnp.zeros((), _jnp.float32), _jax.tree.map(_jnp.zeros_like, weights))
        (loss, grad_w), grad_x = _jax.lax.scan(body, init, (per_example, given["loss_target"]))
    with _jax.named_scope("update"):
        delta_w, new_m, new_v = {}, {}, {}
        for n in TWIN_WEIGHTS:
            delta_w[n], new_m[n], new_v[n] = _adamw(weights[n], grad_w[n], given["m_" + n], given["v_" + n])
    return (loss, grad_x, *[grad_w[n] for n in TWIN_WEIGHTS], *[delta_w[n] for n in TWIN_WEIGHTS],
            *[new_m[n] for n in TWIN_WEIGHTS], *[new_v[n] for n in TWIN_WEIGHTS])
```

```python
import functools
import math

import jax
import jax.numpy as jnp
from jax import lax
from jax.experimental import pallas as pl
from jax.experimental.pallas import tpu as pltpu

F32 = jnp.float32
BF16 = jnp.bfloat16

EPS = 1e-6
D_MODEL = 1024
CHUNK = 128
HEAD_DIM = 128
A_HEADS = 4
D_A = 512
D_B = 512
B_CONV = 31
C_CONV = 3
D_FF = 2816
F_CONV = 3
N_CHIPS = 4

ADAM_LR = 0.001
ADAM_B1 = 0.9
ADAM_B2 = 0.999
ADAM_EPS = 1e-08
ADAM_WD = 0.01
ADAM_STEP = 10

SUBLANES = 8
LANES = 128
HALO_SHORT = 8
HALO_LONG = 32
VMEM_BYTES_MAX = 56 * 1024 * 1024

INV_SQRT2 = 1.0 / math.sqrt(2.0)
INV_SQRT_2PI = 1.0 / math.sqrt(2.0 * math.pi)

MESH = pl.DeviceIdType.MESH


def _cparams(sem, vmem_mb):
    return pltpu.CompilerParams(dimension_semantics=sem,
                                vmem_limit_bytes=min(vmem_mb * 1024 * 1024, VMEM_BYTES_MAX))


def _pick(total, pref):
    for c in (2048, 1024, 512, 256, 128):
        if c <= pref and total % c == 0:
            return c
    raise ValueError(f"no tile for {total}")


def _sigmoid(x):
    return jax.nn.sigmoid(x)


def _silu(x):
    return x * _sigmoid(x)


def _dsilu(x):
    s = _sigmoid(x)
    return s * (1.0 + x * (1.0 - s))


def _gelu(x):
    return 0.5 * x * (1.0 + lax.erf(x * INV_SQRT2))


def _dgelu(x):
    return 0.5 * (1.0 + lax.erf(x * INV_SQRT2)) + x * jnp.exp(-0.5 * x * x) * INV_SQRT_2PI


def _ln_stats(x):
    mu = jnp.mean(x, axis=-1, keepdims=True)
    xc = x - mu
    var = jnp.mean(xc * xc, axis=-1, keepdims=True)
    r = lax.rsqrt(var + EPS)
    return xc * r, r


def _ln_bwd(dy, xh, r, g):
    dxh = dy * g
    m1 = jnp.mean(dxh, axis=-1, keepdims=True)
    m2 = jnp.mean(dxh * xh, axis=-1, keepdims=True)
    return r * (dxh - m1 - xh * m2)


def _rowsum(x):
    return jnp.sum(x, axis=0, keepdims=True)


def _mm_nn(a, w, *, layer, tm, tn, residual=None, out_dtype=F32, name):
    T, K = a.shape
    if w.ndim == 4:
        _, S, _, n4 = w.shape
        N = S * n4
        bps = n4 // tn
        w_spec = pl.BlockSpec((None, None, K, tn), lambda i, j: (layer, j // bps, 0, j % bps))
    else:
        N = w.shape[2]
        w_spec = pl.BlockSpec((None, K, tn), lambda i, j: (layer, 0, j))
    in_specs = [pl.BlockSpec((tm, K), lambda i, j: (i, 0)), w_spec]
    args = [a, w]
    if residual is not None:
        in_specs.append(pl.BlockSpec((tm, tn), lambda i, j: (i, j)))
        args.append(residual)

    def body(*refs):
        a_ref, w_ref, o_ref = refs[0], refs[1], refs[-1]
        acc = jnp.dot(a_ref[...].astype(BF16), w_ref[...], preferred_element_type=F32)
        if residual is not None:
            acc = refs[2][...] + acc
        o_ref[...] = acc.astype(out_dtype)

    return pl.pallas_call(
        body, name=name, grid=(T // tm, N // tn), in_specs=in_specs,
        out_specs=pl.BlockSpec((tm, tn), lambda i, j: (i, j)),
        out_shape=jax.ShapeDtypeStruct((T, N), out_dtype),
        compiler_params=_cparams(("parallel", "parallel"), 48),
    )(*args)


def _mm_nt(dy, w, *, layer, tm, tn, name):
    T = dy.shape[0]
    nt_dims = (((1,), (1,)), ((), ()))
    if w.ndim == 4:
        _, S, K, n4 = w.shape

        def body(dy_ref, w_ref, o_ref):
            @pl.when(pl.program_id(1) == 0)
            def _():
                o_ref[...] = jnp.zeros_like(o_ref)
            o_ref[...] += lax.dot_general(dy_ref[...].astype(BF16), w_ref[...], nt_dims,
                                          preferred_element_type=F32)

        return pl.pallas_call(
            body, name=name, grid=(T // tm, S),
            in_specs=[pl.BlockSpec((tm, n4), lambda i, s: (i, s)),
                      pl.BlockSpec((None, None, K, n4), lambda i, s: (layer, s, 0, 0))],
            out_specs=pl.BlockSpec((tm, K), lambda i, s: (i, 0)),
            out_shape=jax.ShapeDtypeStruct((T, K), F32),
            compiler_params=_cparams(("parallel", "arbitrary"), 48),
        )(dy, w)
    _, R, N = w.shape

    def body2(dy_ref, w_ref, o_ref):
        o_ref[...] = lax.dot_general(dy_ref[...].astype(BF16), w_ref[...], nt_dims,
                                     preferred_element_type=F32)

    return pl.pallas_call(
        body2, name=name, grid=(T // tm, R // tn),
        in_specs=[pl.BlockSpec((tm, N), lambda i, j: (i, 0)),
                  pl.BlockSpec((None, tn, N), lambda i, j: (layer, j, 0))],
        out_specs=pl.BlockSpec((tm, tn), lambda i, j: (i, j)),
        out_shape=jax.ShapeDtypeStruct((T, R), F32),
        compiler_params=_cparams(("parallel", "parallel"), 48),
    )(dy, w)


def _mm_tn(a, dy, *, shards, tk, tn, tt, name):
    T, K = a.shape
    N = dy.shape[1]
    tn_dims = (((0,), (0,)), ((), ()))

    def body(a_ref, dy_ref, o_ref):
        @pl.when(pl.program_id(2) == 0)
        def _():
            o_ref[...] = jnp.zeros_like(o_ref)
        o_ref[...] += lax.dot_general(a_ref[...].astype(BF16), dy_ref[...].astype(BF16), tn_dims,
                                      preferred_element_type=F32)

    if shards is None:
        out_spec = pl.BlockSpec((tk, tn), lambda k, n, t: (k, n))
        out_shape = jax.ShapeDtypeStruct((K, N), F32)
    else:
        n4 = N // shards
        bps = n4 // tn
        out_spec = pl.BlockSpec((None, tk, tn), lambda k, n, t: (n // bps, k, n % bps))
        out_shape = jax.ShapeDtypeStruct((shards, K, n4), F32)
    return pl.pallas_call(
        body, name=name, grid=(K // tk, N // tn, T // tt),
        in_specs=[pl.BlockSpec((tt, tk), lambda k, n, t: (t, k)),
                  pl.BlockSpec((tt, tn), lambda k, n, t: (t, n))],
        out_specs=out_spec, out_shape=out_shape,
        compiler_params=_cparams(("parallel", "parallel", "arbitrary"), 48),
    )(a, dy)


def _rmsnorm_fwd(x, g, *, layer, tm, name):
    T, D = x.shape

    def body(x_ref, g_ref, h_ref):
        xf = x_ref[...]
        r = lax.rsqrt(jnp.mean(xf * xf, axis=-1, keepdims=True) + EPS)
        h_ref[...] = (xf * r * g_ref[...]).astype(BF16)

    return pl.pallas_call(
        body, name=name, grid=(T // tm,),
        in_specs=[pl.BlockSpec((tm, D), lambda i: (i, 0)),
                  pl.BlockSpec((None, 1, D), lambda i: (layer, 0, 0))],
        out_specs=pl.BlockSpec((tm, D), lambda i: (i, 0)),
        out_shape=jax.ShapeDtypeStruct((T, D), BF16),
        compiler_params=_cparams(("parallel",), 32),
    )(x, g)


def _rmsnorm_bwd(x, g, dh, dres, *, layer, tm, name):
    T, D = x.shape

    def body(x_ref, g_ref, dh_ref, dres_ref, dx_ref, dg_ref):
        @pl.when(pl.program_id(0) == 0)
        def _():
            dg_ref[...] = jnp.zeros_like(dg_ref)
        xf = x_ref[...]
        r = lax.rsqrt(jnp.mean(xf * xf, axis=-1, keepdims=True) + EPS)
        xh = xf * r
        dh = dh_ref[...]
        dg_ref[...] += _rowsum(dh * xh)
        dxh = dh * g_ref[...]
        dx_ref[...] = dres_ref[...] + r * (dxh - xh * jnp.mean(dxh * xh, axis=-1, keepdims=True))

    return pl.pallas_call(
        body, name=name, grid=(T // tm,),
        in_specs=[pl.BlockSpec((tm, D), lambda i: (i, 0)),
                  pl.BlockSpec((None, 1, D), lambda i: (layer, 0, 0)),
                  pl.BlockSpec((tm, D), lambda i: (i, 0)),
                  pl.BlockSpec((tm, D), lambda i: (i, 0))],
        out_specs=[pl.BlockSpec((tm, D), lambda i: (i, 0)),
                   pl.BlockSpec((1, D), lambda i: (0, 0))],
        out_shape=[jax.ShapeDtypeStruct((T, D), F32), jax.ShapeDtypeStruct((1, D), F32)],
        compiler_params=_cparams(("arbitrary",), 40),
    )(x, g, dh, dres)


def _loss_head(x, tgt, g, *, tm, name):
    T, D = x.shape

    def body(x_ref, t_ref, g_ref, loss_ref, dx_ref, dg_ref):
        @pl.when(pl.program_id(0) == 0)
        def _():
            dg_ref[...] = jnp.zeros_like(dg_ref)
            loss_ref[...] = jnp.zeros_like(loss_ref)
        xf = x_ref[...]
        gg = g_ref[...]
        r = lax.rsqrt(jnp.mean(xf * xf, axis=-1, keepdims=True) + EPS)
        xh = xf * r
        err = xh * gg - t_ref[...]
        row = jnp.mean(err * err, axis=-1, keepdims=True)
        loss_ref[...] += 0.5 * jnp.sum(row, axis=0, keepdims=True)
        dy = err * (1.0 / D)
        dg_ref[...] += _rowsum(dy * xh)
        dxh = dy * gg
        dx_ref[...] = r * (dxh - xh * jnp.mean(dxh * xh, axis=-1, keepdims=True))

    return pl.pallas_call(
        body, name=name, grid=(T // tm,),
        in_specs=[pl.BlockSpec((tm, D), lambda i: (i, 0)),
                  pl.BlockSpec((tm, D), lambda i: (i, 0)),
                  pl.BlockSpec((1, D), lambda i: (0, 0))],
        out_specs=[pl.BlockSpec((1, 1), lambda i: (0, 0)),
                   pl.BlockSpec((tm, D), lambda i: (i, 0)),
                   pl.BlockSpec((1, D), lambda i: (0, 0))],
        out_shape=[jax.ShapeDtypeStruct((1, 1), F32), jax.ShapeDtypeStruct((T, D), F32),
                   jax.ShapeDtypeStruct((1, D), F32)],
        compiler_params=_cparams(("arbitrary",), 40),
    )(x, tgt, g)


CONV_ROWS = 64
CONV_COLS = 256


def _halo_prev_index(tm, halo):
    per = tm // halo
    return lambda i: jnp.maximum(i * per - 1, 0)


def _halo_next_index(tm, halo, total):
    per = tm // halo
    last = total // halo - 1
    return lambda i: jnp.minimum((i + 1) * per, last)


def _causal_mask():
    t = lax.broadcasted_iota(jnp.int32, (CHUNK, CHUNK), 0)
    s = lax.broadcasted_iota(jnp.int32, (CHUNK, CHUNK), 1)
    return s <= t


def _mixer_ab_fwd(z, a_ln_g, a_ln_b, w_s, b_s, conv_w, conv_b, b_ln_g, b_ln_b, *, tm, name):
    T = z.shape[0]
    nchunk = tm // CHUNK
    halo = HALO_LONG

    def body(za_ref, zb_ref, zh_ref, alg_ref, alb_ref, ws_ref, bs_ref, cw_ref, cbias_ref,
             blg_ref, blb_ref, y_ref, cb_ref, ext_ref):
        i = pl.program_id(0)
        gu = _gelu(za_ref[:, :D_A])
        gv = _gelu(za_ref[:, D_A:])
        xh, _ = _ln_stats(gv)
        lv = (xh * alg_ref[...] + alb_ref[...]).astype(BF16)
        mask = _causal_mask()
        for h in range(A_HEADS):
            wm = jnp.where(mask, ws_ref[h], 0.0).astype(BF16)
            cols = slice(h * HEAD_DIM, (h + 1) * HEAD_DIM)
            for c in range(nchunk):
                rows = slice(c * CHUNK, (c + 1) * CHUNK)
                mixed = jnp.dot(wm, lv[rows, cols], preferred_element_type=F32) + bs_ref[h]
                y_ref[rows, cols] = (gu[rows, cols] * mixed).astype(BF16)
        ext_ref[halo:halo + tm, :] = zb_ref[:, :D_B] * _sigmoid(zb_ref[:, D_B:])
        prev = zh_ref[:, :D_B] * _sigmoid(zh_ref[:, D_B:])
        ext_ref[0:halo, :] = jnp.where(i > 0, prev, 0.0)
        for rb in range(tm // CONV_ROWS):
            for cb in range(D_B // CONV_COLS):
                cs = slice(cb * CONV_COLS, (cb + 1) * CONV_COLS)
                acc = jnp.zeros((CONV_ROWS, CONV_COLS), F32)
                for k in range(B_CONV):
                    off = rb * CONV_ROWS + halo - (B_CONV - 1) + k
                    acc = acc + cw_ref[k:k + 1, cs] * ext_ref[off:off + CONV_ROWS, cs]
                cb_ref[rb * CONV_ROWS:(rb + 1) * CONV_ROWS, cs] = acc + cbias_ref[:, cs]
        xhb, _ = _ln_stats(cb_ref[...])
        y_ref[:, D_A:] = _silu(xhb * blg_ref[...] + blb_ref[...]).astype(BF16)

    row = lambda i: (i, 0)
    par = lambda i: (0, 0)
    return pl.pallas_call(
        body, name=name, grid=(T // tm,),
        in_specs=[pl.BlockSpec((tm, 2 * D_A), lambda i: (i, 0)),
                  pl.BlockSpec((tm, 2 * D_B), lambda i: (i, 1)),
                  pl.BlockSpec((halo, 2 * D_B), lambda i: (_halo_prev_index(tm, halo)(i), 1)),
                  pl.BlockSpec((1, D_A), par), pl.BlockSpec((1, D_A), par),
                  pl.BlockSpec((A_HEADS, CHUNK, CHUNK), lambda i: (0, 0, 0)),
                  pl.BlockSpec((A_HEADS, CHUNK, 1), lambda i: (0, 0, 0)),
                  pl.BlockSpec((B_CONV, D_B), par), pl.BlockSpec((1, D_B), par),
                  pl.BlockSpec((1, D_B), par), pl.BlockSpec((1, D_B), par)],
        out_specs=[pl.BlockSpec((tm, D_A + D_B), row), pl.BlockSpec((tm, D_B), row)],
        out_shape=[jax.ShapeDtypeStruct((T, D_A + D_B), BF16), jax.ShapeDtypeStruct((T, D_B), F32)],
        scratch_shapes=[pltpu.VMEM((halo + tm, D_B), F32)],
        compiler_params=_cparams(("parallel",), 40),
    )(z, z, z, a_ln_g, a_ln_b, w_s, b_s, conv_w, conv_b, b_ln_g, b_ln_b)


def _mixer_ab_bwd_pre(z, cb, dy, a_ln_g, a_ln_b, w_s, b_s, b_ln_g, b_ln_b, *, tm, name):
    T = z.shape[0]
    nchunk = tm // CHUNK
    tn_dims = (((0,), (0,)), ((), ()))
    nt_dims = (((1,), (1,)), ((), ()))

    def body(za_ref, cb_ref, dy_ref, alg_ref, alb_ref, ws_ref, bs_ref, blg_ref, blb_ref,
             dza_ref, dcb_ref, dalg_ref, dalb_ref, dws_ref, dbs_ref, dblg_ref, dblb_ref,
             dlv_ref):
        @pl.when(pl.program_id(0) == 0)
        def _():
            for ref in (dalg_ref, dalb_ref, dws_ref, dbs_ref, dblg_ref, dblb_ref):
                ref[...] = jnp.zeros_like(ref)
        ua = za_ref[:, :D_A]
        va = za_ref[:, D_A:]
        gu = _gelu(ua)
        gv = _gelu(va)
        xh, r = _ln_stats(gv)
        alg = alg_ref[...]
        lv = (xh * alg + alb_ref[...]).astype(BF16)
        dya = dy_ref[:, :D_A]
        mask = _causal_mask()
        for h in range(A_HEADS):
            wm = jnp.where(mask, ws_ref[h], 0.0).astype(BF16)
            cols = slice(h * HEAD_DIM, (h + 1) * HEAD_DIM)
            dwm = jnp.zeros((CHUNK, CHUNK), F32)
            dbs = jnp.zeros((CHUNK, 1), F32)
            for c in range(nchunk):
                rows = slice(c * CHUNK, (c + 1) * CHUNK)
                lvb = lv[rows, cols]
                mixed = jnp.dot(wm, lvb, preferred_element_type=F32) + bs_ref[h]
                dyb = dya[rows, cols]
                dza_ref[rows, cols] = (dyb * mixed * _dgelu(ua[rows, cols])).astype(BF16)
                dmixed = dyb * gu[rows, cols]
                dmb = dmixed.astype(BF16)
                dlv_ref[rows, cols] = lax.dot_general(wm, dmb, tn_dims, preferred_element_type=F32)
                dwm = dwm + lax.dot_general(dmb, lvb, nt_dims, preferred_element_type=F32)
                dbs = dbs + jnp.sum(dmixed, axis=1, keepdims=True)
            dws_ref[h] += jnp.where(mask, dwm, 0.0)
            dbs_ref[h] += dbs
        dlv = dlv_ref[...]
        dalg_ref[...] += _rowsum(dlv * xh)
        dalb_ref[...] += _rowsum(dlv)
        dgv = _ln_bwd(dlv, xh, r, alg)
        dza_ref[:, D_A:] = (dgv * _dgelu(va)).astype(BF16)
        xhb, rb = _ln_stats(cb_ref[...])
        blg = blg_ref[...]
        lb = xhb * blg + blb_ref[...]
        dlb = dy_ref[:, D_A:] * _dsilu(lb)
        dblg_ref[...] += _rowsum(dlb * xhb)
        dblb_ref[...] += _rowsum(dlb)
        dcb_ref[...] = _ln_bwd(dlb, xhb, rb, blg)

    row = lambda i: (i, 0)
    par = lambda i: (0, 0)
    par3 = lambda i: (0, 0, 0)
    return pl.pallas_call(
        body, name=name, grid=(T // tm,),
        in_specs=[pl.BlockSpec((tm, 2 * D_A), row), pl.BlockSpec((tm, D_B), row),
                  pl.BlockSpec((tm, D_A + D_B), row),
                  pl.BlockSpec((1, D_A), par), pl.BlockSpec((1, D_A), par),
                  pl.BlockSpec((A_HEADS, CHUNK, CHUNK), par3),
                  pl.BlockSpec((A_HEADS, CHUNK, 1), par3),
                  pl.BlockSpec((1, D_B), par), pl.BlockSpec((1, D_B), par)],
        out_specs=[pl.BlockSpec((tm, 2 * D_A), row), pl.BlockSpec((tm, D_B), row),
                   pl.BlockSpec((1, D_A), par), pl.BlockSpec((1, D_A), par),
                   pl.BlockSpec((A_HEADS, CHUNK, CHUNK), par3),
                   pl.BlockSpec((A_HEADS, CHUNK, 1), par3),
                   pl.BlockSpec((1, D_B), par), pl.BlockSpec((1, D_B), par)],
        out_shape=[jax.ShapeDtypeStruct((T, 2 * D_A), BF16), jax.ShapeDtypeStruct((T, D_B), F32),
                   jax.ShapeDtypeStruct((1, D_A), F32), jax.ShapeDtypeStruct((1, D_A), F32),
                   jax.ShapeDtypeStruct((A_HEADS, CHUNK, CHUNK), F32),
                   jax.ShapeDtypeStruct((A_HEADS, CHUNK, 1), F32),
                   jax.ShapeDtypeStruct((1, D_B), F32), jax.ShapeDtypeStruct((1, D_B), F32)],
        scratch_shapes=[pltpu.VMEM((tm, D_A), F32)],
        compiler_params=_cparams(("arbitrary",), 40),
    )(z, cb, dy, a_ln_g, a_ln_b, w_s, b_s, b_ln_g, b_ln_b)


def _mixer_b_conv_bwd(z, dcb, conv_w, *, tm, name):
    T = z.shape[0]
    halo = HALO_LONG

    def body(zb_ref, zh_ref, dcb_ref, dcn_ref, cw_ref, dzb_ref, dcw_ref, dbias_ref, ext_ref, dext_ref):
        i = pl.program_id(0)
        last = pl.num_programs(0) - 1

        @pl.when(i == 0)
        def _():
            dcw_ref[...] = jnp.zeros_like(dcw_ref)
            dbias_ref[...] = jnp.zeros_like(dbias_ref)
        xb = zb_ref[:, :D_B]
        sg = _sigmoid(zb_ref[:, D_B:])
        ext_ref[halo:halo + tm, :] = xb * sg
        prev = zh_ref[:, :D_B] * _sigmoid(zh_ref[:, D_B:])
        ext_ref[0:halo, :] = jnp.where(i > 0, prev, 0.0)
        dcb = dcb_ref[...]
        dext_ref[0:tm, :] = dcb
        dext_ref[tm:tm + halo, :] = jnp.where(i < last, dcn_ref[...], 0.0)
        dbias_ref[...] += _rowsum(dcb)
        for k in range(B_CONV):
            off = halo - (B_CONV - 1) + k
            dcw_ref[k:k + 1, :] += _rowsum(dcb * ext_ref[off:off + tm, :])
        for rb in range(tm // CONV_ROWS):
            for cb in range(D_B // CONV_COLS):
                cs = slice(cb * CONV_COLS, (cb + 1) * CONV_COLS)
                rs = slice(rb * CONV_ROWS, (rb + 1) * CONV_ROWS)
                acc = jnp.zeros((CONV_ROWS, CONV_COLS), F32)
                for k in range(B_CONV):
                    off = rb * CONV_ROWS + (B_CONV - 1) - k
                    acc = acc + cw_ref[k:k + 1, cs] * dext_ref[off:off + CONV_ROWS, cs]
                xbb = xb[rs, cs]
                sgb = sg[rs, cs]
                dzb_ref[rs, cs] = (acc * sgb).astype(BF16)
                dzb_ref[rs, D_B + cb * CONV_COLS:D_B + (cb + 1) * CONV_COLS] = (
                    acc * xbb * sgb * (1.0 - sgb)).astype(BF16)

    row = lambda i: (i, 0)
    par = lambda i: (0, 0)
    return pl.pallas_call(
        body, name=name, grid=(T // tm,),
        in_specs=[pl.BlockSpec((tm, 2 * D_B), lambda i: (i, 1)),
                  pl.BlockSpec((halo, 2 * D_B), lambda i: (_halo_prev_index(tm, halo)(i), 1)),
                  pl.BlockSpec((tm, D_B), row),
                  pl.BlockSpec((halo, D_B), lambda i: (_halo_next_index(tm, halo, T)(i), 0)),
                  pl.BlockSpec((B_CONV, D_B), par)],
        out_specs=[pl.BlockSpec((tm, 2 * D_B), row), pl.BlockSpec((B_CONV, D_B), par),
                   pl.BlockSpec((1, D_B), par)],
        out_shape=[jax.ShapeDtypeStruct((T, 2 * D_B), BF16), jax.ShapeDtypeStruct((B_CONV, D_B), F32),
                   jax.ShapeDtypeStruct((1, D_B), F32)],
        scratch_shapes=[pltpu.VMEM((halo + tm, D_B), F32), pltpu.VMEM((tm + halo, D_B), F32)],
        compiler_params=_cparams(("arbitrary",), 40),
    )(z, z, dcb, dcb, conv_w)


def _conv3(w_ref, ext_ref, base, rows, cs):
    acc = w_ref[0:1, cs] * ext_ref[base - 2:base - 2 + rows, cs]
    acc = acc + w_ref[1:2, cs] * ext_ref[base - 1:base - 1 + rows, cs]
    return acc + w_ref[2:3, cs] * ext_ref[base:base + rows, cs]


def _conv3_t(w_ref, ext_ref, base, rows, cs):
    acc = w_ref[0:1, cs] * ext_ref[base + 2:base + 2 + rows, cs]
    acc = acc + w_ref[1:2, cs] * ext_ref[base + 1:base + 1 + rows, cs]
    return acc + w_ref[2:3, cs] * ext_ref[base:base + rows, cs]


def _mixer_c_fwd(z, conv_w, *, tm, name):
    T = z.shape[0]
    D = D_MODEL
    halo = HALO_SHORT
    full = slice(0, D)

    def body(bg_ref, cg_ref, xv_ref, cgh_ref, xvh_ref, w_ref, r_ref, ext_ref):
        i = pl.program_id(0)
        ext_ref[halo:halo + tm, :] = cg_ref[...] * xv_ref[...]
        ext_ref[0:halo, :] = jnp.where(i > 0, cgh_ref[...] * xvh_ref[...], 0.0)
        q = _conv3(w_ref, ext_ref, halo, tm, full)
        r_ref[...] = (bg_ref[...] * q).astype(BF16)

    hp = _halo_prev_index(tm, halo)
    return pl.pallas_call(
        body, name=name, grid=(T // tm,),
        in_specs=[pl.BlockSpec((tm, D), lambda i: (i, 0)), pl.BlockSpec((tm, D), lambda i: (i, 1)),
                  pl.BlockSpec((tm, D), lambda i: (i, 2)),
                  pl.BlockSpec((halo, D), lambda i: (hp(i), 1)),
                  pl.BlockSpec((halo, D), lambda i: (hp(i), 2)),
                  pl.BlockSpec((None, C_CONV, D), lambda i: (0, 0, 0))],
        out_specs=pl.BlockSpec((tm, D), lambda i: (i, 0)),
        out_shape=jax.ShapeDtypeStruct((T, D), BF16),
        scratch_shapes=[pltpu.VMEM((halo + tm, D), F32)],
        compiler_params=_cparams(("parallel",), 40),
    )(z, z, z, z, z, conv_w)


def _mixer_c_bwd(z, dr, conv_w, *, tm, name):
    T = z.shape[0]
    D = D_MODEL
    halo = HALO_SHORT
    full = slice(0, D)

    def body(bg_ref, cg_ref, xv_ref, cgh_ref, xvh_ref, bgn_ref, dr_ref, drn_ref, w_ref,
             dz_ref, dw_ref, ext_ref, dext_ref):
        i = pl.program_id(0)
        last = pl.num_programs(0) - 1

        @pl.when(i == 0)
        def _():
            dw_ref[...] = jnp.zeros_like(dw_ref)
        cg = cg_ref[...]
        xv = xv_ref[...]
        dr = dr_ref[...]
        ext_ref[halo:halo + tm, :] = cg * xv
        ext_ref[0:halo, :] = jnp.where(i > 0, cgh_ref[...] * xvh_ref[...], 0.0)
        dq = dr * bg_ref[...]
        dext_ref[0:tm, :] = dq
        dext_ref[tm:tm + halo, :] = jnp.where(i < last, drn_ref[...] * bgn_ref[...], 0.0)
        q = _conv3(w_ref, ext_ref, halo, tm, full)
        dz_ref[:, 0:D] = (dr * q).astype(BF16)
        for k in range(C_CONV):
            dw_ref[k:k + 1, :] += _rowsum(dq * ext_ref[halo - 2 + k:halo - 2 + k + tm, :])
        dp = _conv3_t(w_ref, dext_ref, 0, tm, full)
        dz_ref[:, D:2 * D] = (dp * xv).astype(BF16)
        dz_ref[:, 2 * D:3 * D] = (dp * cg).astype(BF16)

    hp = _halo_prev_index(tm, halo)
    hn = _halo_next_index(tm, halo, T)
    return pl.pallas_call(
        body, name=name, grid=(T // tm,),
        in_specs=[pl.BlockSpec((tm, D), lambda i: (i, 0)), pl.BlockSpec((tm, D), lambda i: (i, 1)),
                  pl.BlockSpec((tm, D), lambda i: (i, 2)),
                  pl.BlockSpec((halo, D), lambda i: (hp(i), 1)),
                  pl.BlockSpec((halo, D), lambda i: (hp(i), 2)),
                  pl.BlockSpec((halo, D), lambda i: (hn(i), 0)),
                  pl.BlockSpec((tm, D), lambda i: (i, 0)),
                  pl.BlockSpec((halo, D), lambda i: (hn(i), 0)),
                  pl.BlockSpec((None, C_CONV, D), lambda i: (0, 0, 0))],
        out_specs=[pl.BlockSpec((tm, 3 * D), lambda i: (i, 0)),
                   pl.BlockSpec((C_CONV, D), lambda i: (0, 0))],
        out_shape=[jax.ShapeDtypeStruct((T, 3 * D), BF16), jax.ShapeDtypeStruct((C_CONV, D), F32)],
        scratch_shapes=[pltpu.VMEM((halo + tm, D), F32), pltpu.VMEM((tm + halo, D), F32)],
        compiler_params=_cparams(("arbitrary",), 48),
    )(z, z, z, z, z, z, dr, dr, conv_w)


FFN_COLS = 256


def _ffn_act_fwd(up, conv_w, *, layer, tm, name):
    T = up.shape[0]
    halo = HALO_SHORT
    W = FFN_COLS

    def body(up_ref, uph_ref, w_ref, a_ref, ext_ref):
        i = pl.program_id(0)
        ext_ref[halo:halo + tm, :] = up_ref[...]
        ext_ref[0:halo, :] = jnp.where(i > 0, uph_ref[...], 0.0)
        for cb in range(D_FF // W):
            gs = slice(cb * W, (cb + 1) * W)
            vs = slice(D_FF + cb * W, D_FF + (cb + 1) * W)
            g = _conv3(w_ref, ext_ref, halo, tm, gs)
            v = _conv3(w_ref, ext_ref, halo, tm, vs)
            a_ref[:, gs] = (_silu(g) * v).astype(BF16)

    return pl.pallas_call(
        body, name=name, grid=(T // tm,),
        in_specs=[pl.BlockSpec((tm, 2 * D_FF), lambda i: (i, 0)),
                  pl.BlockSpec((halo, 2 * D_FF), lambda i: (_halo_prev_index(tm, halo)(i), 0)),
                  pl.BlockSpec((None, F_CONV, 2 * D_FF), lambda i: (layer, 0, 0))],
        out_specs=pl.BlockSpec((tm, D_FF), lambda i: (i, 0)),
        out_shape=jax.ShapeDtypeStruct((T, D_FF), BF16),
        scratch_shapes=[pltpu.VMEM((halo + tm, 2 * D_FF), F32)],
        compiler_params=_cparams(("parallel",), 48),
    )(up, up, conv_w)


def _ffn_act_bwd(up, da, conv_w, *, layer, tm, name):
    T = up.shape[0]
    halo = HALO_SHORT
    W = FFN_COLS

    def body(up_ref, uph_ref, upn_ref, da_ref, dan_ref, w_ref, dup_ref, dw_ref, ext_ref, dext_ref):
        i = pl.program_id(0)
        last = pl.num_programs(0) - 1

        @pl.when(i == 0)
        def _():
            dw_ref[...] = jnp.zeros_like(dw_ref)
        ext_ref[halo:halo + tm, :] = up_ref[...]
        ext_ref[0:halo, :] = jnp.where(i > 0, uph_ref[...], 0.0)
        ext_ref[halo + tm:2 * halo + tm, :] = upn_ref[...]
        rows = tm + halo
        live = jnp.where(i < last, 1.0, 0.0)
        for cb in range(D_FF // W):
            gs = slice(cb * W, (cb + 1) * W)
            vs = slice(D_FF + cb * W, D_FF + (cb + 1) * W)
            g = _conv3(w_ref, ext_ref, halo, rows, gs)
            v = _conv3(w_ref, ext_ref, halo, rows, vs)
            da = jnp.concatenate([da_ref[:, gs], dan_ref[:, gs] * live], axis=0)
            dext_ref[:, gs] = da * v * _dsilu(g)
            dext_ref[:, vs] = da * _silu(g)
            for cs in (gs, vs):
                dup_ref[:, cs] = _conv3_t(w_ref, dext_ref, 0, tm, cs).astype(BF16)
                d = dext_ref[0:tm, cs]
                for k in range(F_CONV):
                    dw_ref[k:k + 1, cs] += _rowsum(d * ext_ref[halo - 2 + k:halo - 2 + k + tm, cs])

    hp = _halo_prev_index(tm, halo)
    hn = _halo_next_index(tm, halo, T)
    return pl.pallas_call(
        body, name=name, grid=(T // tm,),
        in_specs=[pl.BlockSpec((tm, 2 * D_FF), lambda i: (i, 0)),
                  pl.BlockSpec((halo, 2 * D_FF), lambda i: (hp(i), 0)),
                  pl.BlockSpec((halo, 2 * D_FF), lambda i: (hn(i), 0)),
                  pl.BlockSpec((tm, D_FF), lambda i: (i, 0)),
                  pl.BlockSpec((halo, D_FF), lambda i: (hn(i), 0)),
                  pl.BlockSpec((None, F_CONV, 2 * D_FF), lambda i: (layer, 0, 0))],
        out_specs=[pl.BlockSpec((tm, 2 * D_FF), lambda i: (i, 0)),
                   pl.BlockSpec((F_CONV, 2 * D_FF), lambda i: (0, 0))],
        out_shape=[jax.ShapeDtypeStruct((T, 2 * D_FF), BF16),
                   jax.ShapeDtypeStruct((F_CONV, 2 * D_FF), F32)],
        scratch_shapes=[pltpu.VMEM((2 * halo + tm, 2 * D_FF), F32),
                        pltpu.VMEM((tm + halo, 2 * D_FF), F32)],
        compiler_params=_cparams(("arbitrary",), 56),
    )(up, up, up, da, da, conv_w)


def _local_step(x, tgt, small, big):
    T = x.shape[0]
    tm_e = _pick(T, 256)
    tm_n = _pick(T, 512)
    tm = _pick(T, 512)
    tt = _pick(T, 512)
    nm = small["norm_mix"].reshape(2, 1, D_MODEL)
    nf = small["norm_ffn"].reshape(2, 1, D_MODEL)
    ngf = small["norm_final"].reshape(1, D_MODEL)
    b_s = small["a_b_s"].reshape(A_HEADS, CHUNK, 1)
    w_s = small["a_w_s"].reshape(A_HEADS, CHUNK, CHUNK)
    b_conv_w = small["b_conv_w"].reshape(B_CONV, D_B)
    sg, bg = {}, {}

    h_m0 = _rmsnorm_fwd(x, nm, layer=0, tm=tm_n, name="norm_mix0")
    z_ab = _mm_nn(h_m0, big["ab_w_in"], layer=0, tm=tm, tn=512, name="ab_in")
    yab, cb = _mixer_ab_fwd(z_ab, small["a_ln_g"], small["a_ln_b"], w_s, b_s, b_conv_w,
                            small["b_conv_b"], small["b_ln_g"], small["b_ln_b"], tm=tm_e, name="mixer_ab")
    x1 = _mm_nn(yab, big["ab_w_out"], layer=0, tm=tm, tn=512, residual=x, name="ab_out")

    def ffn_fwd(xin, layer):
        h = _rmsnorm_fwd(xin, nf, layer=layer, tm=tm_n, name=f"norm_ffn{layer}")
        up = _mm_nn(h, big["f_w_up"], layer=layer, tm=tm, tn=1408, name=f"ffn_up{layer}")
        a = _ffn_act_fwd(up, small["f_conv_w"], layer=layer, tm=tm_e, name=f"ffn_act{layer}")
        xout = _mm_nn(a, big["f_w_down"], layer=layer, tm=tm, tn=512, residual=xin, name=f"ffn_down{layer}")
        return h, up, a, xout

    h_f0, up0, a0, x2 = ffn_fwd(x1, 0)
    h_m1 = _rmsnorm_fwd(x2, nm, layer=1, tm=tm_n, name="norm_mix1")
    z_c = _mm_nn(h_m1, big["c_w_in"], layer=0, tm=tm, tn=768, name="c_in")
    r = _mixer_c_fwd(z_c, small["c_conv_w"], tm=tm_e, name="mixer_c")
    x3 = _mm_nn(r, big["c_w_out"], layer=0, tm=tm, tn=512, residual=x2, name="c_out")
    h_f1, up1, a1, x4 = ffn_fwd(x3, 1)
    loss, dx, sg["norm_final"] = _loss_head(x4, tgt, ngf, tm=tm_n, name="loss_head")

    def ffn_bwd(dx, xin, h, up, a, layer):
        da = _mm_nt(dx, big["f_w_down"], layer=layer, tm=tm, tn=1408, name=f"ffn_down_dx{layer}")
        dwd = _mm_tn(a, dx, shards=None, tk=1408, tn=1024, tt=tt, name=f"ffn_down_dw{layer}")
        dup, dcw = _ffn_act_bwd(up, da, small["f_conv_w"], layer=layer, tm=tm_e, name=f"ffn_act_bwd{layer}")
        dh = _mm_nt(dup, big["f_w_up"], layer=layer, tm=tm, tn=None, name=f"ffn_up_dx{layer}")
        dwu = _mm_tn(h, dup, shards=N_CHIPS, tk=1024, tn=1408, tt=tt, name=f"ffn_up_dw{layer}")
        dxin, dg = _rmsnorm_bwd(xin, nf, dh, dx, layer=layer, tm=tm_n, name=f"norm_ffn_bwd{layer}")
        return dxin, dg, dcw, dwu, dwd

    dx, dnf1, dfc1, dwu1, dwd1 = ffn_bwd(dx, x3, h_f1, up1, a1, 1)
    dr = _mm_nt(dx, big["c_w_out"], layer=0, tm=tm, tn=512, name="c_out_dx")
    bg["c_w_out"] = _mm_tn(r, dx, shards=None, tk=1024, tn=1024, tt=tt, name="c_out_dw")
    dz_c, dccw = _mixer_c_bwd(z_c, dr, small["c_conv_w"], tm=tm_e, name="mixer_c_bwd")
    sg["c_conv_w"] = dccw.reshape(1, C_CONV, D_MODEL)
    dh = _mm_nt(dz_c, big["c_w_in"], layer=0, tm=tm, tn=None, name="c_in_dx")
    bg["c_w_in"] = _mm_tn(h_m1, dz_c, shards=N_CHIPS, tk=1024, tn=768, tt=tt, name="c_in_dw")
    dx, dnm1 = _rmsnorm_bwd(x2, nm, dh, dx, layer=1, tm=tm_n, name="norm_mix_bwd1")
    dx, dnf0, dfc0, dwu0, dwd0 = ffn_bwd(dx, x1, h_f0, up0, a0, 0)
    dyab = _mm_nt(dx, big["ab_w_out"], layer=0, tm=tm, tn=512, name="ab_out_dx")
    bg["ab_w_out"] = _mm_tn(yab, dx, shards=None, tk=1024, tn=1024, tt=tt, name="ab_out_dw")
    (dza, dcb, sg["a_ln_g"], sg["a_ln_b"], dws, dbs, sg["b_ln_g"], sg["b_ln_b"]) = _mixer_ab_bwd_pre(
        z_ab, cb, dyab, small["a_ln_g"], small["a_ln_b"], w_s, b_s, small["b_ln_g"], small["b_ln_b"],
        tm=tm_e, name="mixer_ab_bwd")
    dzb, dbcw, sg["b_conv_b"] = _mixer_b_conv_bwd(z_ab, dcb, b_conv_w, tm=tm_e, name="mixer_b_conv_bwd")
    sg["a_w_s"] = dws.reshape(1, A_HEADS, CHUNK, CHUNK)
    sg["a_b_s"] = dbs.reshape(1, A_HEADS, CHUNK)
    sg["b_conv_w"] = dbcw.reshape(1, B_CONV, D_B)
    dz_ab = jnp.concatenate([dza, dzb], axis=1)
    dh = _mm_nt(dz_ab, big["ab_w_in"], layer=0, tm=tm, tn=None, name="ab_in_dx")
    bg["ab_w_in"] = _mm_tn(h_m0, dz_ab, shards=N_CHIPS, tk=1024, tn=512, tt=tt, name="ab_in_dw")
    dx, dnm0 = _rmsnorm_bwd(x, nm, dh, dx, layer=0, tm=tm_n, name="norm_mix_bwd0")

    sg["norm_mix"] = [dnm0, dnm1]
    sg["norm_ffn"] = [dnf0, dnf1]
    sg["f_conv_w"] = [dfc0, dfc1]
    bg["f_w_up"] = [dwu0, dwu1]
    bg["f_w_down"] = [dwd0, dwd1]
    return loss, dx, sg, bg


BLOCK_BYTES = 3 * 1024 * 1024


def _row_tile(rows, row_bytes):
    best = None
    for tr in range(SUBLANES, rows + 1, SUBLANES):
        if rows % tr == 0 and tr * row_bytes <= BLOCK_BYTES:
            best = tr
    if best is None:
        raise ValueError(f"no row tile for {rows}")
    return best


def _cast_bf16(w, *, name):
    L, rows, cols = w.shape
    tr = _row_tile(rows, cols * 4)
    if tr % 16:
        tr = rows

    def body(w_ref, o_ref):
        o_ref[...] = w_ref[...].astype(BF16)

    spec = pl.BlockSpec((None, tr, cols), lambda l, i: (l, i, 0))
    return pl.pallas_call(
        body, name=name, grid=(L, rows // tr), in_specs=[spec], out_specs=spec,
        out_shape=jax.ShapeDtypeStruct(w.shape, BF16),
        compiler_params=_cparams(("parallel", "parallel"), 32),
    )(w)


def _add_pair(a, b, *, name):
    S, rows, cols = a.shape
    tr = _row_tile(rows, cols * 4)

    def body(a_ref, b_ref, o_ref):
        o_ref[...] = a_ref[...] + b_ref[...]

    spec = pl.BlockSpec((None, tr, cols), lambda s, i: (s, i, 0))
    return pl.pallas_call(
        body, name=name, grid=(S, rows // tr), in_specs=[spec, spec], out_specs=spec,
        out_shape=jax.ShapeDtypeStruct(a.shape, F32),
        compiler_params=_cparams(("parallel", "parallel"), 32),
    )(a, b)


def _sum_slots(r, *, name):
    S, rows, cols = r.shape
    tr = _row_tile(rows, cols * 4 * S)

    def body(r_ref, o_ref):
        acc = r_ref[0]
        for s in range(1, S):
            acc = acc + r_ref[s]
        o_ref[...] = acc

    return pl.pallas_call(
        body, name=name, grid=(rows // tr,),
        in_specs=[pl.BlockSpec((S, tr, cols), lambda i: (0, i, 0))],
        out_specs=pl.BlockSpec((tr, cols), lambda i: (i, 0)),
        out_shape=jax.ShapeDtypeStruct((rows, cols), F32),
        compiler_params=_cparams(("parallel",), 32),
    )(r)


def _adamw_math(w, g, m, v):
    m2 = ADAM_B1 * m + (1.0 - ADAM_B1) * g
    v2 = ADAM_B2 * v + (1.0 - ADAM_B2) * (g * g)
    m_hat = m2 / (1.0 - ADAM_B1 ** ADAM_STEP)
    v_hat = v2 / (1.0 - ADAM_B2 ** ADAM_STEP)
    delta = -ADAM_LR * (m_hat / (jnp.sqrt(v_hat) + ADAM_EPS) + ADAM_WD * w)
    return delta, m2, v2


def _adamw(w, g, m, v, *, name):
    L, rows, cols = w.shape
    tr = _row_tile(rows, cols * 4)

    def body(w_ref, g_ref, m_ref, v_ref, d_ref, m2_ref, v2_ref):
        d, m2, v2 = _adamw_math(w_ref[...], g_ref[...], m_ref[...], v_ref[...])
        d_ref[...] = d
        m2_ref[...] = m2
        v2_ref[...] = v2

    spec = pl.BlockSpec((None, tr, cols), lambda l, i: (l, i, 0))
    shape = jax.ShapeDtypeStruct(w.shape, F32)
    return pl.pallas_call(
        body, name=name, grid=(L, rows // tr), in_specs=[spec] * 4, out_specs=[spec] * 3,
        out_shape=[shape] * 3,
        compiler_params=_cparams(("parallel", "parallel"), 48),
    )(w, g, m, v)


ANY = pl.BlockSpec(memory_space=pl.ANY)


def _place():
    x, y, c = lax.axis_index("x"), lax.axis_index("y"), lax.axis_index("c")
    peers = [(1 - x, y), (x, 1 - y), (1 - x, 1 - y)]
    return x, y, c, 2 * x + y, (x, y, 1 - c), peers


def _half(rows, which):
    return pl.ds(which * (rows // 2), rows // 2)


def _remote(src, dst, send_sem, recv_sem, device):
    return pltpu.make_async_remote_copy(src_ref=src, dst_ref=dst, send_sem=send_sem, recv_sem=recv_sem,
                                        device_id=device, device_id_type=MESH)


def _gather_weights(shards):
    nw = len(shards)
    items = [(wi, l) for wi, w in enumerate(shards) for l in range(w.shape[0])]
    n = len(items)

    def body(*refs):
        ws, wg = refs[:nw], refs[nw:2 * nw]
        loc, isend, irecv, dsend, drecv = refs[2 * nw:]
        x, y, c, k, sib, peers = _place()
        local, sends = [], []
        for t, (wi, l) in enumerate(items):
            rows = shards[wi].shape[1]
            cp = pltpu.make_async_copy(ws[wi].at[l], wg[wi].at[l, k], loc.at[t])
            cp.start()
            local.append(cp)
            for j, (px, py) in enumerate(peers):
                rc = _remote(ws[wi].at[l, _half(rows, c)], wg[wi].at[l, k, _half(rows, c)],
                             isend.at[t, j], irecv.at[t, j], (px, py, c))
                rc.start()
                sends.append(rc)
        for t, (wi, l) in enumerate(items):
            rows = shards[wi].shape[1]
            for j, (px, py) in enumerate(peers):
                landed = wg[wi].at[l, 2 * px + py, _half(rows, c)]
                _remote(landed, landed, isend.at[t, j], irecv.at[t, j], (px, py, c)).wait_recv()
                fw = _remote(landed, landed, dsend.at[t, j], drecv.at[t, j], sib)
                fw.start()
                sends.append(fw)
        for t, (wi, l) in enumerate(items):
            rows = shards[wi].shape[1]
            for j, (px, py) in enumerate(peers):
                other = wg[wi].at[l, 2 * px + py, _half(rows, 1 - c)]
                _remote(other, other, dsend.at[t, j], drecv.at[t, j], sib).wait_recv()
        for rc in sends:
            rc.wait_send()
        for cp in local:
            cp.wait()

    return pl.pallas_call(
        body, name="gather_weights", in_specs=[ANY] * nw, out_specs=[ANY] * nw,
        out_shape=[jax.ShapeDtypeStruct((w.shape[0], N_CHIPS) + w.shape[1:], BF16) for w in shards],
        scratch_shapes=[pltpu.SemaphoreType.DMA((n,))] + [pltpu.SemaphoreType.DMA((n, 3))] * 4,
    )(*shards)


def _reduce_pair_exchange(grads):
    n = len(grads)

    def body(*refs):
        gs, mine, theirs = refs[:n], refs[n:2 * n], refs[2 * n:3 * n]
        loc, send, recv = refs[3 * n:]
        x, y, c, k, sib, peers = _place()
        copies = []
        for t in range(n):
            rows = grads[t].shape[1]
            cp = pltpu.make_async_copy(gs[t].at[:, _half(rows, c), :], mine[t], loc.at[t])
            cp.start()
            copies.append(cp)
            rc = _remote(gs[t].at[:, _half(rows, 1 - c), :], theirs[t], send.at[t], recv.at[t], sib)
            rc.start()
            copies.append(rc)
        for cp in copies:
            cp.wait()

    shapes = [jax.ShapeDtypeStruct((g.shape[0], g.shape[1] // 2, g.shape[2]), F32) for g in grads]
    outs = pl.pallas_call(
        body, name="reduce_pair_exchange", in_specs=[ANY] * n, out_specs=[ANY] * (2 * n),
        out_shape=shapes + shapes,
        scratch_shapes=[pltpu.SemaphoreType.DMA((n,))] * 3,
    )(*grads)
    return outs[:n], outs[n:]


def _reduce_chip_exchange(parts):
    n = len(parts)

    def body(*refs):
        ps, rb = refs[:n], refs[n:2 * n]
        loc, send, recv = refs[2 * n:]
        x, y, c, k, sib, peers = _place()
        local, sends = [], []
        for t in range(n):
            cp = pltpu.make_async_copy(ps[t].at[k], rb[t].at[k], loc.at[t])
            cp.start()
            local.append(cp)
            for j, (px, py) in enumerate(peers):
                rc = _remote(ps[t].at[2 * px + py], rb[t].at[k], send.at[t, j], recv.at[t, j], (px, py, c))
                rc.start()
                sends.append(rc)
        for t in range(n):
            for j, (px, py) in enumerate(peers):
                slot = rb[t].at[2 * px + py]
                _remote(slot, slot, send.at[t, j], recv.at[t, j], (px, py, c)).wait_recv()
        for rc in sends:
            rc.wait_send()
        for cp in local:
            cp.wait()

    return pl.pallas_call(
        body, name="reduce_chip_exchange", in_specs=[ANY] * n, out_specs=[ANY] * n,
        out_shape=[jax.ShapeDtypeStruct(p.shape, F32) for p in parts],
        scratch_shapes=[pltpu.SemaphoreType.DMA((n,)), pltpu.SemaphoreType.DMA((n, 3)),
                        pltpu.SemaphoreType.DMA((n, 3))],
    )(*parts)


def _reduce_pair_share(halves, items, shapes):
    n = len(halves)
    nw = len(shapes)

    def body(*refs):
        hs, gs = refs[:n], refs[n:n + nw]
        loc, send, recv = refs[n + nw:]
        x, y, c, k, sib, peers = _place()
        copies = []
        for t, (wi, l) in enumerate(items):
            rows = shapes[wi][1]
            cp = pltpu.make_async_copy(hs[t], gs[wi].at[l, _half(rows, c)], loc.at[t])
            cp.start()
            copies.append(cp)
            rc = _remote(hs[t], gs[wi].at[l, _half(rows, c)], send.at[t], recv.at[t], sib)
            rc.start()
            copies.append(rc)
        for t, (wi, l) in enumerate(items):
            rows = shapes[wi][1]
            other = gs[wi].at[l, _half(rows, 1 - c)]
            _remote(other, other, send.at[t], recv.at[t], sib).wait_recv()
        for t in range(n):
            copies[2 * t].wait()
            copies[2 * t + 1].wait_send()

    return pl.pallas_call(
        body, name="reduce_pair_share", in_specs=[ANY] * n, out_specs=[ANY] * nw,
        out_shape=[jax.ShapeDtypeStruct(s, F32) for s in shapes],
        scratch_shapes=[pltpu.SemaphoreType.DMA((n,))] * 3,
    )(*halves)


def _exchange_packs(pack, *, reduce, name):
    R = pack.shape[0]
    ndev = 2 * N_CHIPS

    def body(p_ref, o_ref, *scratch):
        if reduce:
            buf, send, recv = scratch
        else:
            buf = o_ref
            send, recv = scratch
        x, y, c = lax.axis_index("x"), lax.axis_index("y"), lax.axis_index("c")
        me = 4 * x + 2 * y + c
        buf[me] = p_ref[...]
        sends = []
        for q in range(1, ndev):
            qx, qy, qc = (q >> 2) & 1, (q >> 1) & 1, q & 1
            peer = (x ^ qx, y ^ qy, c ^ qc)
            rc = _remote(p_ref, buf.at[me], send.at[q - 1], recv.at[q - 1], peer)
            rc.start()
            sends.append(rc)
        for q in range(1, ndev):
            qx, qy, qc = (q >> 2) & 1, (q >> 1) & 1, q & 1
            slot = buf.at[4 * (x ^ qx) + 2 * (y ^ qy) + (c ^ qc)]
            _remote(slot, slot, send.at[q - 1], recv.at[q - 1], (x ^ qx, y ^ qy, c ^ qc)).wait_recv()
        for rc in sends:
            rc.wait_send()
        if reduce:
            acc = buf[0]
            for d in range(1, ndev):
                acc = acc + buf[d]
            o_ref[...] = acc

    vm = pl.BlockSpec(memory_space=pltpu.VMEM)
    sems = [pltpu.SemaphoreType.DMA((ndev - 1,)), pltpu.SemaphoreType.DMA((ndev - 1,))]
    if reduce:
        out_shape = jax.ShapeDtypeStruct((R, LANES), F32)
        scratch = [pltpu.VMEM((ndev, R, LANES), F32)] + sems
    else:
        out_shape = jax.ShapeDtypeStruct((ndev, R, LANES), F32)
        scratch = sems
    return pl.pallas_call(
        body, name=name, in_specs=[vm], out_specs=vm, out_shape=out_shape, scratch_shapes=scratch,
        compiler_params=pltpu.CompilerParams(vmem_limit_bytes=32 * 1024 * 1024),
    )(pack)


PACK_UNIT = SUBLANES * LANES


def _pack(arrays):
    flat, sizes = [], []
    for a in arrays:
        pieces = a if isinstance(a, (list, tuple)) else [a]
        v = jnp.concatenate([p.reshape(-1) for p in pieces]) if len(pieces) > 1 else pieces[0].reshape(-1)
        size = v.shape[0]
        padded = -(-size // PACK_UNIT) * PACK_UNIT
        flat.append(jnp.pad(v, (0, padded - size)))
        sizes.append((size, padded))
    return jnp.concatenate(flat).reshape(-1, LANES), sizes


def _unpack(pack, sizes, shapes):
    v = pack.reshape(-1)
    out, off = [], 0
    for (size, padded), shape in zip(sizes, shapes):
        out.append(v[off:off + size].reshape(shape))
        off += padded
    return out


BIG = ("ab_w_in", "ab_w_out", "c_w_in", "c_w_out", "f_w_up", "f_w_down")
COL_SHARDED = ("ab_w_in", "c_w_in", "f_w_up")
SMALL_REPLICATED = ("norm_mix", "norm_ffn", "norm_final", "a_ln_g", "a_ln_b", "a_w_s", "a_b_s",
                    "b_conv_b", "b_ln_g", "b_ln_b")
SMALL_SHARDED = ("b_conv_w", "c_conv_w", "f_conv_w")
SMALL = SMALL_REPLICATED + SMALL_SHARDED
ALL_WEIGHTS = ("norm_mix", "norm_ffn", "norm_final", "ab_w_in", "a_ln_g", "a_ln_b", "a_w_s", "a_b_s",
               "b_conv_w", "b_conv_b", "b_ln_g", "b_ln_b", "ab_w_out", "c_w_in", "c_conv_w", "c_w_out",
               "f_w_up", "f_conv_w", "f_w_down")


def _step(x, tgt, w, m, v):
    chip = 2 * lax.axis_index("x") + lax.axis_index("y")

    gathered = _gather_weights([_cast_bf16(w[n], name=f"cast_{n}") for n in BIG])
    big = {}
    for n, g in zip(BIG, gathered):
        L, S, rows, cols = g.shape
        big[n] = g if n in COL_SHARDED else g.reshape(L, S * rows, cols)
    conv_pack, conv_sizes = _pack([w[n] for n in SMALL_SHARDED])
    conv_all = _exchange_packs(conv_pack, reduce=False, name="gather_conv_weights")
    conv_shapes = [w[n].shape for n in SMALL_SHARDED]
    per_chip = [_unpack(conv_all[2 * s], conv_sizes, conv_shapes) for s in range(N_CHIPS)]
    small = {n: w[n] for n in SMALL_REPLICATED}
    for idx, n in enumerate(SMALL_SHARDED):
        small[n] = jnp.concatenate([per_chip[s][idx] for s in range(N_CHIPS)], axis=-1)

    loss, dx, sg, bg = _local_step(x, tgt, small, big)

    items, full = [], []
    for wi, n in enumerate(BIG):
        for l in range(w[n].shape[0]):
            g = bg[n][l] if isinstance(bg[n], list) else bg[n]
            rows, cols = w[n].shape[1:]
            items.append((wi, l))
            full.append(g.reshape(N_CHIPS, rows, cols))
    mine, theirs = _reduce_pair_exchange(full)
    parts = [_add_pair(a, b, name=f"pair_sum{t}") for t, (a, b) in enumerate(zip(mine, theirs))]
    slots = _reduce_chip_exchange(parts)
    halves = [_sum_slots(r, name=f"chip_sum{t}") for t, r in enumerate(slots)]
    grads_big = _reduce_pair_share(halves, items, [w[n].shape for n in BIG])

    g_pack, g_sizes = _pack([sg[n] for n in SMALL])
    g_sum = _exchange_packs(g_pack, reduce=True, name="allreduce_small_grads")
    full_shapes = [small[n].shape for n in SMALL]
    g_small = dict(zip(SMALL, _unpack(g_sum, g_sizes, full_shapes)))
    for n in SMALL_SHARDED:
        width = w[n].shape[-1]
        g_small[n] = lax.dynamic_slice_in_dim(g_small[n], chip * width, width, axis=g_small[n].ndim - 1)

    grad, delta, new_m, new_v = {}, {}, {}, {}
    for n, g in zip(BIG, grads_big):
        grad[n] = g
        delta[n], new_m[n], new_v[n] = _adamw(w[n], g, m[n], v[n], name=f"adamw_{n}")
    shapes = [w[n].shape for n in SMALL]
    wp, sizes = _pack([w[n] for n in SMALL])
    gp, _ = _pack([g_small[n] for n in SMALL])
    mp, _ = _pack([m[n] for n in SMALL])
    vp, _ = _pack([v[n] for n in SMALL])
    R = wp.shape[0]
    dp, m2p, v2p = _adamw(wp.reshape(1, R, LANES), gp.reshape(1, R, LANES), mp.reshape(1, R, LANES),
                          vp.reshape(1, R, LANES), name="adamw_small")
    for n, d_, m_, v_ in zip(SMALL, _unpack(dp, sizes, shapes), _unpack(m2p, sizes, shapes),
                             _unpack(v2p, sizes, shapes)):
        grad[n] = g_small[n]
        delta[n], new_m[n], new_v[n] = d_, m_, v_
    return loss, dx, grad, delta, new_m, new_v


def kernel(x, norm_mix, norm_ffn, norm_final, ab_w_in, a_ln_g, a_ln_b, a_w_s, a_b_s, b_conv_w, b_conv_b, b_ln_g, b_ln_b, ab_w_out, c_w_in, c_conv_w, c_w_out, f_w_up, f_conv_w, f_w_down, loss_target, m_norm_mix, m_norm_ffn, m_norm_final, m_ab_w_in, m_a_ln_g, m_a_ln_b, m_a_w_s, m_a_b_s, m_b_conv_w, m_b_conv_b, m_b_ln_g, m_b_ln_b, m_ab_w_out, m_c_w_in, m_c_conv_w, m_c_w_out, m_f_w_up, m_f_conv_w, m_f_w_down, v_norm_mix, v_norm_ffn, v_norm_final, v_ab_w_in, v_a_ln_g, v_a_ln_b, v_a_w_s, v_a_b_s, v_b_conv_w, v_b_conv_b, v_b_ln_g, v_b_ln_b, v_ab_w_out, v_c_w_in, v_c_conv_w, v_c_w_out, v_f_w_up, v_f_conv_w, v_f_w_down):
    given = dict(locals())
    w = {n: given[n] for n in ALL_WEIGHTS}
    m = {n: given["m_" + n] for n in ALL_WEIGHTS}
    v = {n: given["v_" + n] for n in ALL_WEIGHTS}
    T = x.shape[1]
    loss, dx, grad, delta, new_m, new_v = _step(x.reshape(T, D_MODEL), loss_target.reshape(T, D_MODEL), w, m, v)
    loss = lax.psum(loss[0, 0], ("x", "y", "c"))
    out = [loss, dx.reshape(x.shape)]
    for d in (grad, delta, new_m, new_v):
        out += [d[n] for n in ALL_WEIGHTS]
    return tuple(out)
```

```python
import functools
import math

import jax
import jax.numpy as jnp
from jax import lax
from jax.experimental import pallas as pl
from jax.experimental.pallas import tpu as pltpu

F32 = jnp.float32
BF16 = jnp.bfloat16

EPS = 1e-6
D_MODEL = 1024
CHUNK = 128
HEAD_DIM = 128
A_HEADS = 4
D_A = 512
D_B = 512
B_CONV = 31
C_CONV = 3
D_FF = 2816
F_CONV = 3
N_CHIPS = 4

ADAM_LR = 0.001
ADAM_B1 = 0.9
ADAM_B2 = 0.999
ADAM_EPS = 1e-08
ADAM_WD = 0.01
ADAM_STEP = 10

SUBLANES = 8
LANES = 128
HALO_SHORT = 8
HALO_LONG = 32
VMEM_BYTES_MAX = 56 * 1024 * 1024

INV_SQRT2 = 1.0 / math.sqrt(2.0)
INV_SQRT_2PI = 1.0 / math.sqrt(2.0 * math.pi)

MESH = pl.DeviceIdType.MESH


def _cparams(sem, vmem_mb):
    return pltpu.CompilerParams(dimension_semantics=sem,
                                vmem_limit_bytes=min(vmem_mb * 1024 * 1024, VMEM_BYTES_MAX))


def _pick(total, pref):
    for c in (2048, 1024, 512, 256, 128):
        if c <= pref and total % c == 0:
            return c
    raise ValueError(f"no tile for {total}")


def _sigmoid(x):
    return jax.nn.sigmoid(x)


def _silu(x):
    return x * _sigmoid(x)


def _dsilu(x):
    s = _sigmoid(x)
    return s * (1.0 + x * (1.0 - s))


def _gelu(x):
    return 0.5 * x * (1.0 + lax.erf(x * INV_SQRT2))


def _dgelu(x):
    return 0.5 * (1.0 + lax.erf(x * INV_SQRT2)) + x * jnp.exp(-0.5 * x * x) * INV_SQRT_2PI


def _ln_stats(x):
    mu = jnp.mean(x, axis=-1, keepdims=True)
    xc = x - mu
    var = jnp.mean(xc * xc, axis=-1, keepdims=True)
    r = lax.rsqrt(var + EPS)
    return xc * r, r


def _ln_bwd(dy, xh, r, g):
    dxh = dy * g
    m1 = jnp.mean(dxh, axis=-1, keepdims=True)
    m2 = jnp.mean(dxh * xh, axis=-1, keepdims=True)
    return r * (dxh - m1 - xh * m2)


def _rowsum(x):
    return jnp.sum(x, axis=0, keepdims=True)


def _mm_nn(a, w, *, layer, tm, tn, residual=None, out_dtype=F32, name):
    T, K = a.shape
    if w.ndim == 4:
        _, S, _, n4 = w.shape
        N = S * n4
        bps = n4 // tn
        w_spec = pl.BlockSpec((None, None, K, tn), lambda i, j: (layer, j // bps, 0, j % bps))
    else:
        N = w.shape[2]
        w_spec = pl.BlockSpec((None, K, tn), lambda i, j: (layer, 0, j))
    in_specs = [pl.BlockSpec((tm, K), lambda i, j: (i, 0)), w_spec]
    args = [a, w]
    if residual is not None:
        in_specs.append(pl.BlockSpec((tm, tn), lambda i, j: (i, j)))
        args.append(residual)

    def body(*refs):
        a_ref, w_ref, o_ref = refs[0], refs[1], refs[-1]
        acc = jnp.dot(a_ref[...].astype(BF16), w_ref[...], preferred_element_type=F32)
        if residual is not None:
            acc = refs[2][...] + acc
        o_ref[...] = acc.astype(out_dtype)

    return pl.pallas_call(
        body, name=name, grid=(T // tm, N // tn), in_specs=in_specs,
        out_specs=pl.BlockSpec((tm, tn), lambda i, j: (i, j)),
        out_shape=jax.ShapeDtypeStruct((T, N), out_dtype),
        compiler_params=_cparams(("parallel", "parallel"), 48),
    )(*args)


def _mm_nt(dy, w, *, layer, tm, tn, name):
    T = dy.shape[0]
    nt_dims = (((1,), (1,)), ((), ()))
    if w.ndim == 4:
        _, S, K, n4 = w.shape

        def body(dy_ref, w_ref, o_ref):
            @pl.when(pl.program_id(1) == 0)
            def _():
                o_ref[...] = jnp.zeros_like(o_ref)
            o_ref[...] += lax.dot_general(dy_ref[...].astype(BF16), w_ref[...], nt_dims,
                                          preferred_element_type=F32)

        return pl.pallas_call(
            body, name=name, grid=(T // tm, S),
            in_specs=[pl.BlockSpec((tm, n4), lambda i, s: (i, s)),
                      pl.BlockSpec((None, None, K, n4), lambda i, s: (layer, s, 0, 0))],
            out_specs=pl.BlockSpec((tm, K), lambda i, s: (i, 0)),
            out_shape=jax.ShapeDtypeStruct((T, K), F32),
            compiler_params=_cparams(("parallel", "arbitrary"), 48),
        )(dy, w)
    _, R, N = w.shape

    def body2(dy_ref, w_ref, o_ref):
        o_ref[...] = lax.dot_general(dy_ref[...].astype(BF16), w_ref[...], nt_dims,
                                     preferred_element_type=F32)

    return pl.pallas_call(
        body2, name=name, grid=(T // tm, R // tn),
        in_specs=[pl.BlockSpec((tm, N), lambda i, j: (i, 0)),
                  pl.BlockSpec((None, tn, N), lambda i, j: (layer, j, 0))],
        out_specs=pl.BlockSpec((tm, tn), lambda i, j: (i, j)),
        out_shape=jax.ShapeDtypeStruct((T, R), F32),
        compiler_params=_cparams(("parallel", "parallel"), 48),
    )(dy, w)


def _mm_tn(a, dy, *, shards, tk, tn, tt, name):
    T, K = a.shape
    N = dy.shape[1]
    tn_dims = (((0,), (0,)), ((), ()))

    def body(a_ref, dy_ref, o_ref):
        @pl.when(pl.program_id(2) == 0)
        def _():
            o_ref[...] = jnp.zeros_like(o_ref)
        o_ref[...] += lax.dot_general(a_ref[...].astype(BF16), dy_ref[...].astype(BF16), tn_dims,
                                      preferred_element_type=F32)

    if shards is None:
        out_spec = pl.BlockSpec((tk, tn), lambda k, n, t: (k, n))
        out_shape = jax.ShapeDtypeStruct((K, N), F32)
    else:
        n4 = N // shards
        bps = n4 // tn
        out_spec = pl.BlockSpec((None, tk, tn), lambda k, n, t: (n // bps, k, n % bps))
        out_shape = jax.ShapeDtypeStruct((shards, K, n4), F32)
    return pl.pallas_call(
        body, name=name, grid=(K // tk, N // tn, T // tt),
        in_specs=[pl.BlockSpec((tt, tk), lambda k, n, t: (t, k)),
                  pl.BlockSpec((tt, tn), lambda k, n, t: (t, n))],
        out_specs=out_spec, out_shape=out_shape,
        compiler_params=_cparams(("parallel", "parallel", "arbitrary"), 48),
    )(a, dy)


def _rmsnorm_fwd(x, g, *, layer, tm, name):
    T, D = x.shape

    def body(x_ref, g_ref, h_ref):
        xf = x_ref[...]
        r = lax.rsqrt(jnp.mean(xf * xf, axis=-1, keepdims=True) + EPS)
        h_ref[...] = (xf * r * g_ref[...]).astype(BF16)

    return pl.pallas_call(
        body, name=name, grid=(T // tm,),
        in_specs=[pl.BlockSpec((tm, D), lambda i: (i, 0)),
                  pl.BlockSpec((None, 1, D), lambda i: (layer, 0, 0))],
        out_specs=pl.BlockSpec((tm, D), lambda i: (i, 0)),
        out_shape=jax.ShapeDtypeStruct((T, D), BF16),
        compiler_params=_cparams(("parallel",), 32),
    )(x, g)


def _rmsnorm_bwd(x, g, dh, dres, *, layer, tm, name):
    T, D = x.shape

    def body(x_ref, g_ref, dh_ref, dres_ref, dx_ref, dg_ref):
        @pl.when(pl.program_id(0) == 0)
        def _():
            dg_ref[...] = jnp.zeros_like(dg_ref)
        xf = x_ref[...]
        r = lax.rsqrt(jnp.mean(xf * xf, axis=-1, keepdims=True) + EPS)
        xh = xf * r
        dh = dh_ref[...]
        dg_ref[...] += _rowsum(dh * xh)
        dxh = dh * g_ref[...]
        dx_ref[...] = dres_ref[...] + r * (dxh - xh * jnp.mean(dxh * xh, axis=-1, keepdims=True))

    return pl.pallas_call(
        body, name=name, grid=(T // tm,),
        in_specs=[pl.BlockSpec((tm, D), lambda i: (i, 0)),
                  pl.BlockSpec((None, 1, D), lambda i: (layer, 0, 0)),
                  pl.BlockSpec((tm, D), lambda i: (i, 0)),
                  pl.BlockSpec((tm, D), lambda i: (i, 0))],
        out_specs=[pl.BlockSpec((tm, D), lambda i: (i, 0)),
                   pl.BlockSpec((1, D), lambda i: (0, 0))],
        out_shape=[jax.ShapeDtypeStruct((T, D), F32), jax.ShapeDtypeStruct((1, D), F32)],
        compiler_params=_cparams(("arbitrary",), 40),
    )(x, g, dh, dres)


def _loss_head(x, tgt, g, *, tm, name):
    T, D = x.shape

    def body(x_ref, t_ref, g_ref, loss_ref, dx_ref, dg_ref):
        @pl.when(pl.program_id(0) == 0)
        def _():
            dg_ref[...] = jnp.zeros_like(dg_ref)
            loss_ref[...] = jnp.zeros_like(loss_ref)
        xf = x_ref[...]
        gg = g_ref[...]
        r = lax.rsqrt(jnp.mean(xf * xf, axis=-1, keepdims=True) + EPS)
        xh = xf * r
        err = xh * gg - t_ref[...]
        row = jnp.mean(err * err, axis=-1, keepdims=True)
        loss_ref[...] += 0.5 * jnp.sum(row, axis=0, keepdims=True)
        dy = err * (1.0 / D)
        dg_ref[...] += _rowsum(dy * xh)
        dxh = dy * gg
        dx_ref[...] = r * (dxh - xh * jnp.mean(dxh * xh, axis=-1, keepdims=True))

    return pl.pallas_call(
        body, name=name, grid=(T // tm,),
        in_specs=[pl.BlockSpec((tm, D), lambda i: (i, 0)),
                  pl.BlockSpec((tm, D), lambda i: (i, 0)),
                  pl.BlockSpec((1, D), lambda i: (0, 0))],
        out_specs=[pl.BlockSpec((1, 1), lambda i: (0, 0)),
                   pl.BlockSpec((tm, D), lambda i: (i, 0)),
                   pl.BlockSpec((1, D), lambda i: (0, 0))],
        out_shape=[jax.ShapeDtypeStruct((1, 1), F32), jax.ShapeDtypeStruct((T, D), F32),
                   jax.ShapeDtypeStruct((1, D), F32)],
        compiler_params=_cparams(("arbitrary",), 40),
    )(x, tgt, g)


CONV_ROWS = 64
CONV_COLS = 256


def _halo_prev_index(tm, halo):
    per = tm // halo
    return lambda i: jnp.maximum(i * per - 1, 0)


def _halo_next_index(tm, halo, total):
    per = tm // halo
    last = total // halo - 1
    return lambda i: jnp.minimum((i + 1) * per, last)


def _causal_mask():
    t = lax.broadcasted_iota(jnp.int32, (CHUNK, CHUNK), 0)
    s = lax.broadcasted_iota(jnp.int32, (CHUNK, CHUNK), 1)
    return s <= t


def _mixer_ab_fwd(z, a_ln_g, a_ln_b, w_s, b_s, conv_w, conv_b, b_ln_g, b_ln_b, *, tm, name):
    T = z.shape[0]
    nchunk = tm // CHUNK
    halo = HALO_LONG

    def body(za_ref, zb_ref, zh_ref, alg_ref, alb_ref, ws_ref, bs_ref, cw_ref, cbias_ref,
             blg_ref, blb_ref, y_ref, cb_ref, ext_ref):
        i = pl.program_id(0)
        gu = _gelu(za_ref[:, :D_A])
        gv = _gelu(za_ref[:, D_A:])
        xh, _ = _ln_stats(gv)
        lv = (xh * alg_ref[...] + alb_ref[...]).astype(BF16)
        mask = _causal_mask()
        for h in range(A_HEADS):
            wm = jnp.where(mask, ws_ref[h], 0.0).astype(BF16)
            cols = slice(h * HEAD_DIM, (h + 1) * HEAD_DIM)
            for c in range(nchunk):
                rows = slice(c * CHUNK, (c + 1) * CHUNK)
                mixed = jnp.dot(wm, lv[rows, cols], preferred_element_type=F32) + bs_ref[h]
                y_ref[rows, cols] = (gu[rows, cols] * mixed).astype(BF16)
        ext_ref[halo:halo + tm, :] = zb_ref[:, :D_B] * _sigmoid(zb_ref[:, D_B:])
        prev = zh_ref[:, :D_B] * _sigmoid(zh_ref[:, D_B:])
        ext_ref[0:halo, :] = jnp.where(i > 0, prev, 0.0)
        for rb in range(tm // CONV_ROWS):
            for cb in range(D_B // CONV_COLS):
                cs = slice(cb * CONV_COLS, (cb + 1) * CONV_COLS)
                acc = jnp.zeros((CONV_ROWS, CONV_COLS), F32)
                for k in range(B_CONV):
                    off = rb * CONV_ROWS + halo - (B_CONV - 1) + k
                    acc = acc + cw_ref[k:k + 1, cs] * ext_ref[off:off + CONV_ROWS, cs]
                cb_ref[rb * CONV_ROWS:(rb + 1) * CONV_ROWS, cs] = acc + cbias_ref[:, cs]
        xhb, _ = _ln_stats(cb_ref[...])
        y_ref[:, D_A:] = _silu(xhb * blg_ref[...] + blb_ref[...]).astype(BF16)

    row = lambda i: (i, 0)
    par = lambda i: (0, 0)
    return pl.pallas_call(
        body, name=name, grid=(T // tm,),
        in_specs=[pl.BlockSpec((tm, 2 * D_A), lambda i: (i, 0)),
                  pl.BlockSpec((tm, 2 * D_B), lambda i: (i, 1)),
                  pl.BlockSpec((halo, 2 * D_B), lambda i: (_halo_prev_index(tm, halo)(i), 1)),
                  pl.BlockSpec((1, D_A), par), pl.BlockSpec((1, D_A), par),
                  pl.BlockSpec((A_HEADS, CHUNK, CHUNK), lambda i: (0, 0, 0)),
                  pl.BlockSpec((A_HEADS, CHUNK, 1), lambda i: (0, 0, 0)),
                  pl.BlockSpec((B_CONV, D_B), par), pl.BlockSpec((1, D_B), par),
                  pl.BlockSpec((1, D_B), par), pl.BlockSpec((1, D_B), par)],
        out_specs=[pl.BlockSpec((tm, D_A + D_B), row), pl.BlockSpec((tm, D_B), row)],
        out_shape=[jax.ShapeDtypeStruct((T, D_A + D_B), BF16), jax.ShapeDtypeStruct((T, D_B), F32)],
        scratch_shapes=[pltpu.VMEM((halo + tm, D_B), F32)],
        compiler_params=_cparams(("parallel",), 40),
    )(z, z, z, a_ln_g, a_ln_b, w_s, b_s, conv_w, conv_b, b_ln_g, b_ln_b)


def _mixer_ab_bwd_pre(z, cb, dy, a_ln_g, a_ln_b, w_s, b_s, b_ln_g, b_ln_b, *, tm, name):
    T = z.shape[0]
    nchunk = tm // CHUNK
    tn_dims = (((0,), (0,)), ((), ()))
    nt_dims = (((1,), (1,)), ((), ()))

    def body(za_ref, cb_ref, dy_ref, alg_ref, alb_ref, ws_ref, bs_ref, blg_ref, blb_ref,
             dza_ref, dcb_ref, dalg_ref, dalb_ref, dws_ref, dbs_ref, dblg_ref, dblb_ref,
             dlv_ref):
        @pl.when(pl.program_id(0) == 0)
        def _():
            for ref in (dalg_ref, dalb_ref, dws_ref, dbs_ref, dblg_ref, dblb_ref):
                ref[...] = jnp.zeros_like(ref)
        ua = za_ref[:, :D_A]
        va = za_ref[:, D_A:]
        gu = _gelu(ua)
        gv = _gelu(va)
        xh, r = _ln_stats(gv)
        alg = alg_ref[...]
        lv = (xh * alg + alb_ref[...]).astype(BF16)
        dya = dy_ref[:, :D_A]
        mask = _causal_mask()
        for h in range(A_HEADS):
            wm = jnp.where(mask, ws_ref[h], 0.0).astype(BF16)
            cols = slice(h * HEAD_DIM, (h + 1) * HEAD_DIM)
            dwm = jnp.zeros((CHUNK, CHUNK), F32)
            dbs = jnp.zeros((CHUNK, 1), F32)
            for c in range(nchunk):
                rows = slice(c * CHUNK, (c + 1) * CHUNK)
                lvb = lv[rows, cols]
                mixed = jnp.dot(wm, lvb, preferred_element_type=F32) + bs_ref[h]
                dyb = dya[rows, cols]
                dza_ref[rows, cols] = (dyb * mixed * _dgelu(ua[rows, cols])).astype(BF16)
                dmixed = dyb * gu[rows, cols]
                dmb = dmixed.astype(BF16)
                dlv_ref[rows, cols] = lax.dot_general(wm, dmb, tn_dims, preferred_element_type=F32)
                dwm = dwm + lax.dot_general(dmb, lvb, nt_dims, preferred_element_type=F32)
                dbs = dbs + jnp.sum(dmixed, axis=1, keepdims=True)
            dws_ref[h] += jnp.where(mask, dwm, 0.0)
            dbs_ref[h] += dbs
        dlv = dlv_ref[...]
        dalg_ref[...] += _rowsum(dlv * xh)
        dalb_ref[...] += _rowsum(dlv)
        dgv = _ln_bwd(dlv, xh, r, alg)
        dza_ref[:, D_A:] = (dgv * _dgelu(va)).astype(BF16)
        xhb, rb = _ln_stats(cb_ref[...])
        blg = blg_ref[...]
        lb = xhb * blg + blb_ref[...]
        dlb = dy_ref[:, D_A:] * _dsilu(lb)
        dblg_ref[...] += _rowsum(dlb * xhb)
        dblb_ref[...] += _rowsum(dlb)
        dcb_ref[...] = _ln_bwd(dlb, xhb, rb, blg)

    row = lambda i: (i, 0)
    par = lambda i: (0, 0)
    par3 = lambda i: (0, 0, 0)
    return pl.pallas_call(
        body, name=name, grid=(T // tm,),
        in_specs=[pl.BlockSpec((tm, 2 * D_A), row), pl.BlockSpec((tm, D_B), row),
                  pl.BlockSpec((tm, D_A + D_B), row),
                  pl.BlockSpec((1, D_A), par), pl.BlockSpec((1, D_A), par),
                  pl.BlockSpec((A_HEADS, CHUNK, CHUNK), par3),
                  pl.BlockSpec((A_HEADS, CHUNK, 1), par3),
                  pl.BlockSpec((1, D_B), par), pl.BlockSpec((1, D_B), par)],
        out_specs=[pl.BlockSpec((tm, 2 * D_A), row), pl.BlockSpec((tm, D_B), row),
                   pl.BlockSpec((1, D_A), par), pl.BlockSpec((1, D_A), par),
                   pl.BlockSpec((A_HEADS, CHUNK, CHUNK), par3),
                   pl.BlockSpec((A_HEADS, CHUNK, 1), par3),
                   pl.BlockSpec((1, D_B), par), pl.BlockSpec((1, D_B), par)],
        out_shape=[jax.ShapeDtypeStruct((T, 2 * D_A), BF16), jax.ShapeDtypeStruct((T, D_B), F32),
                   jax.ShapeDtypeStruct((1, D_A), F32), jax.ShapeDtypeStruct((1, D_A), F32),
                   jax.ShapeDtypeStruct((A_HEADS, CHUNK, CHUNK), F32),
                   jax.ShapeDtypeStruct((A_HEADS, CHUNK, 1), F32),
                   jax.ShapeDtypeStruct((1, D_B), F32), jax.ShapeDtypeStruct((1, D_B), F32)],
        scratch_shapes=[pltpu.VMEM((tm, D_A), F32)],
        compiler_params=_cparams(("arbitrary",), 40),
    )(z, cb, dy, a_ln_g, a_ln_b, w_s, b_s, b_ln_g, b_ln_b)


def _mixer_b_conv_bwd(z, dcb, conv_w, *, tm, name):
    T = z.shape[0]
    halo = HALO_LONG

    def body(zb_ref, zh_ref, dcb_ref, dcn_ref, cw_ref, dzb_ref, dcw_ref, dbias_ref, ext_ref, dext_ref):
        i = pl.program_id(0)
        last = pl.num_programs(0) - 1

        @pl.when(i == 0)
        def _():
            dcw_ref[...] = jnp.zeros_like(dcw_ref)
            dbias_ref[...] = jnp.zeros_like(dbias_ref)
        xb = zb_ref[:, :D_B]
        sg = _sigmoid(zb_ref[:, D_B:])
        ext_ref[halo:halo + tm, :] = xb * sg
        prev = zh_ref[:, :D_B] * _sigmoid(zh_ref[:, D_B:])
        ext_ref[0:halo, :] = jnp.where(i > 0, prev, 0.0)
        dcb = dcb_ref[...]
        dext_ref[0:tm, :] = dcb
        dext_ref[tm:tm + halo, :] = jnp.where(i < last, dcn_ref[...], 0.0)
        dbias_ref[...] += _rowsum(dcb)
        for k in range(B_CONV):
            off = halo - (B_CONV - 1) + k
            dcw_ref[k:k + 1, :] += _rowsum(dcb * ext_ref[off:off + tm, :])
        for rb in range(tm // CONV_ROWS):
            for cb in range(D_B // CONV_COLS):
                cs = slice(cb * CONV_COLS, (cb + 1) * CONV_COLS)
                rs = slice(rb * CONV_ROWS, (rb + 1) * CONV_ROWS)
                acc = jnp.zeros((CONV_ROWS, CONV_COLS), F32)
                for k in range(B_CONV):
                    off = rb * CONV_ROWS + (B_CONV - 1) - k
                    acc = acc + cw_ref[k:k + 1, cs] * dext_ref[off:off + CONV_ROWS, cs]
                xbb = xb[rs, cs]
                sgb = sg[rs, cs]
                dzb_ref[rs, cs] = (acc * sgb).astype(BF16)
                dzb_ref[rs, D_B + cb * CONV_COLS:D_B + (cb + 1) * CONV_COLS] = (
                    acc * xbb * sgb * (1.0 - sgb)).astype(BF16)

    row = lambda i: (i, 0)
    par = lambda i: (0, 0)
    return pl.pallas_call(
        body, name=name, grid=(T // tm,),
        in_specs=[pl.BlockSpec((tm, 2 * D_B), lambda i: (i, 1)),
                  pl.BlockSpec((halo, 2 * D_B), lambda i: (_halo_prev_index(tm, halo)(i), 1)),
                  pl.BlockSpec((tm, D_B), row),
                  pl.BlockSpec((halo, D_B), lambda i: (_halo_next_index(tm, halo, T)(i), 0)),
                  pl.BlockSpec((B_CONV, D_B), par)],
        out_specs=[pl.BlockSpec((tm, 2 * D_B), row), pl.BlockSpec((B_CONV, D_B), par),
                   pl.BlockSpec((1, D_B), par)],
        out_shape=[jax.ShapeDtypeStruct((T, 2 * D_B), BF16), jax.ShapeDtypeStruct((B_CONV, D_B), F32),
                   jax.ShapeDtypeStruct((1, D_B), F32)],
        scratch_shapes=[pltpu.VMEM((halo + tm, D_B), F32), pltpu.VMEM((tm + halo, D_B), F32)],
        compiler_params=_cparams(("arbitrary",), 40),
    )(z, z, dcb, dcb, conv_w)


def _conv3(w_ref, ext_ref, base, rows, cs):
    acc = w_ref[0:1, cs] * ext_ref[base - 2:base - 2 + rows, cs]
    acc = acc + w_ref[1:2, cs] * ext_ref[base - 1:base - 1 + rows, cs]
    return acc + w_ref[2:3, cs] * ext_ref[base:base + rows, cs]


def _conv3_t(w_ref, ext_ref, base, rows, cs):
    acc = w_ref[0:1, cs] * ext_ref[base + 2:base + 2 + rows, cs]
    acc = acc + w_ref[1:2, cs] * ext_ref[base + 1:base + 1 + rows, cs]
    return acc + w_ref[2:3, cs] * ext_ref[base:base + rows, cs]


def _mixer_c_fwd(z, conv_w, *, tm, name):
    T = z.shape[0]
    D = D_MODEL
    halo = HALO_SHORT
    full = slice(0, D)

    def body(bg_ref, cg_ref, xv_ref, cgh_ref, xvh_ref, w_ref, r_ref, ext_ref):
        i = pl.program_id(0)
        ext_ref[halo:halo + tm, :] = cg_ref[...] * xv_ref[...]
        ext_ref[0:halo, :] = jnp.where(i > 0, cgh_ref[...] * xvh_ref[...], 0.0)
        q = _conv3(w_ref, ext_ref, halo, tm, full)
        r_ref[...] = (bg_ref[...] * q).astype(BF16)

    hp = _halo_prev_index(tm, halo)
    return pl.pallas_call(
        body, name=name, grid=(T // tm,),
        in_specs=[pl.BlockSpec((tm, D), lambda i: (i, 0)), pl.BlockSpec((tm, D), lambda i: (i, 1)),
                  pl.BlockSpec((tm, D), lambda i: (i, 2)),
                  pl.BlockSpec((halo, D), lambda i: (hp(i), 1)),
                  pl.BlockSpec((halo, D), lambda i: (hp(i), 2)),
                  pl.BlockSpec((None, C_CONV, D), lambda i: (0, 0, 0))],
        out_specs=pl.BlockSpec((tm, D), lambda i: (i, 0)),
        out_shape=jax.ShapeDtypeStruct((T, D), BF16),
        scratch_shapes=[pltpu.VMEM((halo + tm, D), F32)],
        compiler_params=_cparams(("parallel",), 40),
    )(z, z, z, z, z, conv_w)


def _mixer_c_bwd(z, dr, conv_w, *, tm, name):
    T = z.shape[0]
    D = D_MODEL
    halo = HALO_SHORT
    full = slice(0, D)

    def body(bg_ref, cg_ref, xv_ref, cgh_ref, xvh_ref, bgn_ref, dr_ref, drn_ref, w_ref,
             dz_ref, dw_ref, ext_ref, dext_ref):
        i = pl.program_id(0)
        last = pl.num_programs(0) - 1

        @pl.when(i == 0)
        def _():
            dw_ref[...] = jnp.zeros_like(dw_ref)
        cg = cg_ref[...]
        xv = xv_ref[...]
        dr = dr_ref[...]
        ext_ref[halo:halo + tm, :] = cg * xv
        ext_ref[0:halo, :] = jnp.where(i > 0, cgh_ref[...] * xvh_ref[...], 0.0)
        dq = dr * bg_ref[...]
        dext_ref[0:tm, :] = dq
        dext_ref[tm:tm + halo, :] = jnp.where(i < last, drn_ref[...] * bgn_ref[...], 0.0)
        q = _conv3(w_ref, ext_ref, halo, tm, full)
        dz_ref[:, 0:D] = (dr * q).astype(BF16)
        for k in range(C_CONV):
            dw_ref[k:k + 1, :] += _rowsum(dq * ext_ref[halo - 2 + k:halo - 2 + k + tm, :])
        dp = _conv3_t(w_ref, dext_ref, 0, tm, full)
        dz_ref[:, D:2 * D] = (dp * xv).astype(BF16)
        dz_ref[:, 2 * D:3 * D] = (dp * cg).astype(BF16)

    hp = _halo_prev_index(tm, halo)
    hn = _halo_next_index(tm, halo, T)
    return pl.pallas_call(
        body, name=name, grid=(T // tm,),
        in_specs=[pl.BlockSpec((tm, D), lambda i: (i, 0)), pl.BlockSpec((tm, D), lambda i: (i, 1)),
                  pl.BlockSpec((tm, D), lambda i: (i, 2)),
                  pl.BlockSpec((halo, D), lambda i: (hp(i), 1)),
                  pl.BlockSpec((halo, D), lambda i: (hp(i), 2)),
                  pl.BlockSpec((halo, D), lambda i: (hn(i), 0)),
                  pl.BlockSpec((tm, D), lambda i: (i, 0)),
                  pl.BlockSpec((halo, D), lambda i: (hn(i), 0)),
                  pl.BlockSpec((None, C_CONV, D), lambda i: (0, 0, 0))],
        out_specs=[pl.BlockSpec((tm, 3 * D), lambda i: (i, 0)),
                   pl.BlockSpec((C_CONV, D), lambda i: (0, 0))],
        out_shape=[jax.ShapeDtypeStruct((T, 3 * D), BF16), jax.ShapeDtypeStruct((C_CONV, D), F32)],
        scratch_shapes=[pltpu.VMEM((halo + tm, D), F32), pltpu.VMEM((tm + halo, D), F32)],
        compiler_params=_cparams(("arbitrary",), 48),
    )(z, z, z, z, z, z, dr, dr, conv_w)


FFN_COLS = 256


def _ffn_act_fwd(up, conv_w, *, layer, tm, name):
    T = up.shape[0]
    halo = HALO_SHORT
    W = FFN_COLS

    def body(up_ref, uph_ref, w_ref, a_ref, ext_ref):
        i = pl.program_id(0)
        ext_ref[halo:halo + tm, :] = up_ref[...]
        ext_ref[0:halo, :] = jnp.where(i > 0, uph_ref[...], 0.0)
        for cb in range(D_FF // W):
            gs = slice(cb * W, (cb + 1) * W)
            vs = slice(D_FF + cb * W, D_FF + (cb + 1) * W)
            g = _conv3(w_ref, ext_ref, halo, tm, gs)
            v = _conv3(w_ref, ext_ref, halo, tm, vs)
            a_ref[:, gs] = (_silu(g) * v).astype(BF16)

    return pl.pallas_call(
        body, name=name, grid=(T // tm,),
        in_specs=[pl.BlockSpec((tm, 2 * D_FF), lambda i: (i, 0)),
                  pl.BlockSpec((halo, 2 * D_FF), lambda i: (_halo_prev_index(tm, halo)(i), 0)),
                  pl.BlockSpec((None, F_CONV, 2 * D_FF), lambda i: (layer, 0, 0))],
        out_specs=pl.BlockSpec((tm, D_FF), lambda i: (i, 0)),
        out_shape=jax.ShapeDtypeStruct((T, D_FF), BF16),
        scratch_shapes=[pltpu.VMEM((halo + tm, 2 * D_FF), F32)],
        compiler_params=_cparams(("parallel",), 48),
    )(up, up, conv_w)


def _ffn_act_bwd(up, da, conv_w, *, layer, tm, name):
    T = up.shape[0]
    halo = HALO_SHORT
    W = FFN_COLS

    def body(up_ref, uph_ref, upn_ref, da_ref, dan_ref, w_ref, dup_ref, dw_ref, ext_ref, dext_ref):
        i = pl.program_id(0)
        last = pl.num_programs(0) - 1

        @pl.when(i == 0)
        def _():
            dw_ref[...] = jnp.zeros_like(dw_ref)
        ext_ref[halo:halo + tm, :] = up_ref[...]
        ext_ref[0:halo, :] = jnp.where(i > 0, uph_ref[...], 0.0)
        ext_ref[halo + tm:2 * halo + tm, :] = upn_ref[...]
        rows = tm + halo
        live = jnp.where(i < last, 1.0, 0.0)
        for cb in range(D_FF // W):
            gs = slice(cb * W, (cb + 1) * W)
            vs = slice(D_FF + cb * W, D_FF + (cb + 1) * W)
            g = _conv3(w_ref, ext_ref, halo, rows, gs)
            v = _conv3(w_ref, ext_ref, halo, rows, vs)
            da = jnp.concatenate([da_ref[:, gs], dan_ref[:, gs] * live], axis=0)
            dext_ref[:, gs] = da * v * _dsilu(g)
            dext_ref[:, vs] = da * _silu(g)
            for cs in (gs, vs):
                dup_ref[:, cs] = _conv3_t(w_ref, dext_ref, 0, tm, cs).astype(BF16)
                d = dext_ref[0:tm, cs]
                for k in range(F_CONV):
                    dw_ref[k:k + 1, cs] += _rowsum(d * ext_ref[halo - 2 + k:halo - 2 + k + tm, cs])

    hp = _halo_prev_index(tm, halo)
    hn = _halo_next_index(tm, halo, T)
    return pl.pallas_call(
        body, name=name, grid=(T // tm,),
        in_specs=[pl.BlockSpec((tm, 2 * D_FF), lambda i: (i, 0)),
                  pl.BlockSpec((halo, 2 * D_FF), lambda i: (hp(i), 0)),
                  pl.BlockSpec((halo, 2 * D_FF), lambda i: (hn(i), 0)),
                  pl.BlockSpec((tm, D_FF), lambda i: (i, 0)),
                  pl.BlockSpec((halo, D_FF), lambda i: (hn(i), 0)),
                  pl.BlockSpec((None, F_CONV, 2 * D_FF), lambda i: (layer, 0, 0))],
        out_specs=[pl.BlockSpec((tm, 2 * D_FF), lambda i: (i, 0)),
                   pl.BlockSpec((F_CONV, 2 * D_FF), lambda i: (0, 0))],
        out_shape=[jax.ShapeDtypeStruct((T, 2 * D_FF), BF16),
                   jax.ShapeDtypeStruct((F_CONV, 2 * D_FF), F32)],
        scratch_shapes=[pltpu.VMEM((2 * halo + tm, 2 * D_FF), F32),
                        pltpu.VMEM((tm + halo, 2 * D_FF), F32)],
        compiler_params=_cparams(("arbitrary",), 56),
    )(up, up, up, da, da, conv_w)


def _local_step(x, tgt, small, big):
    T = x.shape[0]
    tm_e = _pick(T, 256)
    tm_n = _pick(T, 512)
    tm = _pick(T, 512)
    tt = _pick(T, 512)
    nm = small["norm_mix"].reshape(2, 1, D_MODEL)
    nf = small["norm_ffn"].reshape(2, 1, D_MODEL)
    ngf = small["norm_final"].reshape(1, D_MODEL)
    b_s = small["a_b_s"].reshape(A_HEADS, CHUNK, 1)
    w_s = small["a_w_s"].reshape(A_HEADS, CHUNK, CHUNK)
    b_conv_w = small["b_conv_w"].reshape(B_CONV, D_B)
    sg, bg = {}, {}

    h_m0 = _rmsnorm_fwd(x, nm, layer=0, tm=tm_n, name="norm_mix0")
    z_ab = _mm_nn(h_m0, big["ab_w_in"], layer=0, tm=tm, tn=512, name="ab_in")
    yab, cb = _mixer_ab_fwd(z_ab, small["a_ln_g"], small["a_ln_b"], w_s, b_s, b_conv_w,
                            small["b_conv_b"], small["b_ln_g"], small["b_ln_b"], tm=tm_e, name="mixer_ab")
    x1 = _mm_nn(yab, big["ab_w_out"], layer=0, tm=tm, tn=512, residual=x, name="ab_out")

    def ffn_fwd(xin, layer):
        h = _rmsnorm_fwd(xin, nf, layer=layer, tm=tm_n, name=f"norm_ffn{layer}")
        up = _mm_nn(h, big["f_w_up"], layer=layer, tm=tm, tn=1408, name=f"ffn_up{layer}")
        a = _ffn_act_fwd(up, small["f_conv_w"], layer=layer, tm=tm_e, name=f"ffn_act{layer}")
        xout = _mm_nn(a, big["f_w_down"], layer=layer, tm=tm, tn=512, residual=xin, name=f"ffn_down{layer}")
        return h, up, a, xout

    h_f0, up0, a0, x2 = ffn_fwd(x1, 0)
    h_m1 = _rmsnorm_fwd(x2, nm, layer=1, tm=tm_n, name="norm_mix1")
    z_c = _mm_nn(h_m1, big["c_w_in"], layer=0, tm=tm, tn=768, name="c_in")
    r = _mixer_c_fwd(z_c, small["c_conv_w"], tm=tm_e, name="mixer_c")
    x3 = _mm_nn(r, big["c_w_out"], layer=0, tm=tm, tn=512, residual=x2, name="c_out")
    h_f1, up1, a1, x4 = ffn_fwd(x3, 1)
    loss, dx, sg["norm_final"] = _loss_head(x4, tgt, ngf, tm=tm_n, name="loss_head")

    def ffn_bwd(dx, xin, h, up, a, layer):
        da = _mm_nt(dx, big["f_w_down"], layer=layer, tm=tm, tn=1408, name=f"ffn_down_dx{layer}")
        dwd = _mm_tn(a, dx, shards=None, tk=1408, tn=1024, tt=tt, name=f"ffn_down_dw{layer}")
        dup, dcw = _ffn_act_bwd(up, da, small["f_conv_w"], layer=layer, tm=tm_e, name=f"ffn_act_bwd{layer}")
        dh = _mm_nt(dup, big["f_w_up"], layer=layer, tm=tm, tn=None, name=f"ffn_up_dx{layer}")
        dwu = _mm_tn(h, dup, shards=N_CHIPS, tk=1024, tn=1408, tt=tt, name=f"ffn_up_dw{layer}")
        dxin, dg = _rmsnorm_bwd(xin, nf, dh, dx, layer=layer, tm=tm_n, name=f"norm_ffn_bwd{layer}")
        return dxin, dg, dcw, dwu, dwd

    dx, dnf1, dfc1, dwu1, dwd1 = ffn_bwd(dx, x3, h_f1, up1, a1, 1)
    dr = _mm_nt(dx, big["c_w_out"], layer=0, tm=tm, tn=512, name="c_out_dx")
    bg["c_w_out"] = _mm_tn(r, dx, shards=None, tk=1024, tn=1024, tt=tt, name="c_out_dw")
    dz_c, dccw = _mixer_c_bwd(z_c, dr, small["c_conv_w"], tm=tm_e, name="mixer_c_bwd")
    sg["c_conv_w"] = dccw.reshape(1, C_CONV, D_MODEL)
    dh = _mm_nt(dz_c, big["c_w_in"], layer=0, tm=tm, tn=None, name="c_in_dx")
    bg["c_w_in"] = _mm_tn(h_m1, dz_c, shards=N_CHIPS, tk=1024, tn=768, tt=tt, name="c_in_dw")
    dx, dnm1 = _rmsnorm_bwd(x2, nm, dh, dx, layer=1, tm=tm_n, name="norm_mix_bwd1")
    dx, dnf0, dfc0, dwu0, dwd0 = ffn_bwd(dx, x1, h_f0, up0, a0, 0)
    dyab = _mm_nt(dx, big["ab_w_out"], layer=0, tm=tm, tn=512, name="ab_out_dx")
    bg["ab_w_out"] = _mm_tn(yab, dx, shards=None, tk=1024, tn=1024, tt=tt, name="ab_out_dw")
    (dza, dcb, sg["a_ln_g"], sg["a_ln_b"], dws, dbs, sg["b_ln_g"], sg["b_ln_b"]) = _mixer_ab_bwd_pre(
        z_ab, cb, dyab, small["a_ln_g"], small["a_ln_b"], w_s, b_s, small["b_ln_g"], small["b_ln_b"],
        tm=tm_e, name="mixer_ab_bwd")
    dzb, dbcw, sg["b_conv_b"] = _mixer_b_conv_bwd(z_ab, dcb, b_conv_w, tm=tm_e, name="mixer_b_conv_bwd")
    sg["a_w_s"] = dws.reshape(1, A_HEADS, CHUNK, CHUNK)
    sg["a_b_s"] = dbs.reshape(1, A_HEADS, CHUNK)
    sg["b_conv_w"] = dbcw.reshape(1, B_CONV, D_B)
    dz_ab = jnp.concatenate([dza, dzb], axis=1)
    dh = _mm_nt(dz_ab, big["ab_w_in"], layer=0, tm=tm, tn=None, name="ab_in_dx")
    bg["ab_w_in"] = _mm_tn(h_m0, dz_ab, shards=N_CHIPS, tk=1024, tn=512, tt=tt, name="ab_in_dw")
    dx, dnm0 = _rmsnorm_bwd(x, nm, dh, dx, layer=0, tm=tm_n, name="norm_mix_bwd0")

    sg["norm_mix"] = [dnm0, dnm1]
    sg["norm_ffn"] = [dnf0, dnf1]
    sg["f_conv_w"] = [dfc0, dfc1]
    bg["f_w_up"] = [dwu0, dwu1]
    bg["f_w_down"] = [dwd0, dwd1]
    return loss, dx, sg, bg


BLOCK_BYTES = 3 * 1024 * 1024


BF16_SUBLANES = 16


def _row_tile(rows, row_bytes, step=SUBLANES):
    best = None
    for tr in range(step, rows + 1, step):
        if rows % tr == 0 and tr * row_bytes <= BLOCK_BYTES:
            best = tr
    if best is None:
        raise ValueError(f"no row tile for {rows}")
    return best


def _place_scalars():
    x, y, c = lax.axis_index("x"), lax.axis_index("y"), lax.axis_index("c")
    return jnp.stack([c, 2 * x + y, 2 * (1 - x) + y, 2 * x + (1 - y), 2 * (1 - x) + (1 - y)]).astype(jnp.int32)


def _cast_into_slot(w, place, *, name):
    L, rows, cols = w.shape
    tr = _row_tile(rows, cols * 4, BF16_SUBLANES)

    def body(place_ref, w_ref, o_ref):
        o_ref[...] = w_ref[...].astype(BF16)

    return pl.pallas_call(
        body, name=name,
        grid_spec=pltpu.PrefetchScalarGridSpec(
            num_scalar_prefetch=1, grid=(L, rows // tr),
            in_specs=[pl.BlockSpec((None, tr, cols), lambda l, i, p: (l, i, 0))],
            out_specs=pl.BlockSpec((None, None, tr, cols), lambda l, i, p: (l, p[1], i, 0))),
        out_shape=jax.ShapeDtypeStruct((L, N_CHIPS, rows, cols), BF16),
        compiler_params=_cparams(("parallel", "parallel"), 32),
    )(place, w)


def _pair_sum(g, theirs, place, *, name):
    S, rows, cols = g.shape
    half = rows // 2
    tr = _row_tile(half, cols * 4, BF16_SUBLANES)
    nb = half // tr

    def body(place_ref, g_ref, t_ref, o_ref):
        o_ref[...] = (g_ref[...] + t_ref[...]).astype(BF16)

    spec = pl.BlockSpec((None, tr, cols), lambda s, i, p: (s, i, 0))
    return pl.pallas_call(
        body, name=name,
        grid_spec=pltpu.PrefetchScalarGridSpec(
            num_scalar_prefetch=1, grid=(S, nb),
            in_specs=[pl.BlockSpec((None, tr, cols), lambda s, i, p: (s, p[0] * nb + i, 0)), spec],
            out_specs=spec),
        out_shape=jax.ShapeDtypeStruct((S, half, cols), BF16),
        compiler_params=_cparams(("parallel", "parallel"), 32),
    )(place, g, theirs)


def _chip_sum(p, r, g_prev, place, *, layer, shape, name):
    L, rows, cols = shape
    half = rows // 2
    tr = _row_tile(half, cols * 4, BF16_SUBLANES)
    nb = half // tr

    def body(place_ref, p_ref, r_ref, *rest):
        o_ref = rest[-1]
        mine = p_ref[...].astype(F32)
        peers = [r_ref[j].astype(F32) for j in range(3)]
        acc = None
        for s in range(N_CHIPS):
            term = jnp.where(place_ref[1] == s, mine,
                             jnp.where(place_ref[2] == s, peers[0],
                                       jnp.where(place_ref[3] == s, peers[1], peers[2])))
            acc = term if acc is None else acc + term
        o_ref[...] = acc

    in_specs = [pl.BlockSpec((None, tr, cols), lambda i, pr: (pr[1], i, 0)),
                pl.BlockSpec((3, tr, cols), lambda i, pr: (0, i, 0))]
    args = [place, p, r]
    aliases = {}
    if g_prev is not None:
        in_specs.append(ANY)
        args.append(g_prev)
        aliases = {3: 0}
    return pl.pallas_call(
        body, name=name,
        grid_spec=pltpu.PrefetchScalarGridSpec(
            num_scalar_prefetch=1, grid=(nb,), in_specs=in_specs,
            out_specs=pl.BlockSpec((None, tr, cols), lambda i, pr: (layer, pr[0] * nb + i, 0))),
        out_shape=jax.ShapeDtypeStruct(shape, F32), input_output_aliases=aliases,
        compiler_params=_cparams(("parallel",), 32),
    )(*args)


def _adamw_math(w, g, m, v):
    m2 = ADAM_B1 * m + (1.0 - ADAM_B1) * g
    v2 = ADAM_B2 * v + (1.0 - ADAM_B2) * (g * g)
    m_hat = m2 / (1.0 - ADAM_B1 ** ADAM_STEP)
    v_hat = v2 / (1.0 - ADAM_B2 ** ADAM_STEP)
    delta = -ADAM_LR * (m_hat / (jnp.sqrt(v_hat) + ADAM_EPS) + ADAM_WD * w)
    return delta, m2, v2


def _adamw(w, g, m, v, *, name):
    L, rows, cols = w.shape
    tr = _row_tile(rows, cols * 4)

    def body(w_ref, g_ref, m_ref, v_ref, d_ref, m2_ref, v2_ref):
        d, m2, v2 = _adamw_math(w_ref[...], g_ref[...], m_ref[...], v_ref[...])
        d_ref[...] = d
        m2_ref[...] = m2
        v2_ref[...] = v2

    spec = pl.BlockSpec((None, tr, cols), lambda l, i: (l, i, 0))
    shape = jax.ShapeDtypeStruct(w.shape, F32)
    return pl.pallas_call(
        body, name=name, grid=(L, rows // tr), in_specs=[spec] * 4, out_specs=[spec] * 3,
        out_shape=[shape] * 3,
        compiler_params=_cparams(("parallel", "parallel"), 48),
    )(w, g, m, v)


ANY = pl.BlockSpec(memory_space=pl.ANY)


def _place():
    x, y, c = lax.axis_index("x"), lax.axis_index("y"), lax.axis_index("c")
    peers = [(1 - x, y), (x, 1 - y), (1 - x, 1 - y)]
    return x, y, c, 2 * x + y, (x, y, 1 - c), peers


def _half(rows, which):
    return pl.ds(which * (rows // 2), rows // 2)


def _remote(src, dst, send_sem, recv_sem, device):
    return pltpu.make_async_remote_copy(src_ref=src, dst_ref=dst, send_sem=send_sem, recv_sem=recv_sem,
                                        device_id=device, device_id_type=MESH)


def _gather_weights(slots):
    nw = len(slots)
    items = [(wi, l) for wi, w in enumerate(slots) for l in range(w.shape[0])]
    n = len(items)

    def body(*refs):
        ws, wg = refs[:nw], refs[nw:2 * nw]
        isend, irecv, dsend, drecv = refs[2 * nw:]
        x, y, c, k, sib, peers = _place()
        sends = []
        for t, (wi, l) in enumerate(items):
            rows = slots[wi].shape[2]
            for j, (px, py) in enumerate(peers):
                rc = _remote(ws[wi].at[l, k, _half(rows, c)], wg[wi].at[l, k, _half(rows, c)],
                             isend.at[t, j], irecv.at[t, j], (px, py, c))
                rc.start()
                sends.append(rc)
        for t, (wi, l) in enumerate(items):
            rows = slots[wi].shape[2]
            for j, (px, py) in enumerate(peers):
                landed = wg[wi].at[l, 2 * px + py, _half(rows, c)]
                _remote(landed, landed, isend.at[t, j], irecv.at[t, j], (px, py, c)).wait_recv()
                fw = _remote(landed, landed, dsend.at[t, j], drecv.at[t, j], sib)
                fw.start()
                sends.append(fw)
        for t, (wi, l) in enumerate(items):
            rows = slots[wi].shape[2]
            for j, (px, py) in enumerate(peers):
                other = wg[wi].at[l, 2 * px + py, _half(rows, 1 - c)]
                _remote(other, other, dsend.at[t, j], drecv.at[t, j], sib).wait_recv()
        for rc in sends:
            rc.wait_send()

    return pl.pallas_call(
        body, name="gather_weights", in_specs=[ANY] * nw, out_specs=[ANY] * nw,
        out_shape=[jax.ShapeDtypeStruct(w.shape, BF16) for w in slots],
        input_output_aliases={i: i for i in range(nw)},
        scratch_shapes=[pltpu.SemaphoreType.DMA((n, 3))] * 4,
    )(*slots)


def _reduce_pair_exchange(grads):
    n = len(grads)

    def body(*refs):
        gs, theirs = refs[:n], refs[n:2 * n]
        send, recv = refs[2 * n:]
        x, y, c, k, sib, peers = _place()
        copies = []
        for t in range(n):
            rows = grads[t].shape[1]
            rc = _remote(gs[t].at[:, _half(rows, 1 - c), :], theirs[t], send.at[t], recv.at[t], sib)
            rc.start()
            copies.append(rc)
        for rc in copies:
            rc.wait()

    return pl.pallas_call(
        body, name="reduce_pair_exchange", in_specs=[ANY] * n, out_specs=[ANY] * n,
        out_shape=[jax.ShapeDtypeStruct((g.shape[0], g.shape[1] // 2, g.shape[2]), F32) for g in grads],
        scratch_shapes=[pltpu.SemaphoreType.DMA((n,))] * 2,
    )(*grads)


def _reduce_chip_exchange(parts):
    n = len(parts)

    def body(*refs):
        ps, rb = refs[:n], refs[n:2 * n]
        send, recv = refs[2 * n:]
        x, y, c, k, sib, peers = _place()
        sends = []
        for t in range(n):
            for j, (px, py) in enumerate(peers):
                rc = _remote(ps[t].at[2 * px + py], rb[t].at[j], send.at[t, j], recv.at[t, j], (px, py, c))
                rc.start()
                sends.append(rc)
        for rc in sends:
            rc.wait()

    return pl.pallas_call(
        body, name="reduce_chip_exchange", in_specs=[ANY] * n, out_specs=[ANY] * n,
        out_shape=[jax.ShapeDtypeStruct((3,) + p.shape[1:], p.dtype) for p in parts],
        scratch_shapes=[pltpu.SemaphoreType.DMA((n, 3)), pltpu.SemaphoreType.DMA((n, 3))],
    )(*parts)


def _reduce_pair_share(grads):
    nw = len(grads)
    items = [(wi, l) for wi, g in enumerate(grads) for l in range(g.shape[0])]
    n = len(items)

    def body(*refs):
        gin, gout = refs[:nw], refs[nw:2 * nw]
        send, recv = refs[2 * nw:]
        x, y, c, k, sib, peers = _place()
        copies = []
        for t, (wi, l) in enumerate(items):
            rows = grads[wi].shape[1]
            rc = _remote(gin[wi].at[l, _half(rows, c)], gout[wi].at[l, _half(rows, c)],
                         send.at[t], recv.at[t], sib)
            rc.start()
            copies.append(rc)
        for t, (wi, l) in enumerate(items):
            rows = grads[wi].shape[1]
            other = gout[wi].at[l, _half(rows, 1 - c)]
            _remote(other, other, send.at[t], recv.at[t], sib).wait_recv()
        for rc in copies:
            rc.wait_send()

    return pl.pallas_call(
        body, name="reduce_pair_share", in_specs=[ANY] * nw, out_specs=[ANY] * nw,
        out_shape=[jax.ShapeDtypeStruct(g.shape, F32) for g in grads],
        input_output_aliases={i: i for i in range(nw)},
        scratch_shapes=[pltpu.SemaphoreType.DMA((n,))] * 2,
    )(*grads)


def _exchange_packs(pack, *, reduce, name):
    R = pack.shape[0]
    ndev = 2 * N_CHIPS

    def body(p_ref, o_ref, *scratch):
        if reduce:
            buf, send, recv = scratch
        else:
            buf = o_ref
            send, recv = scratch
        x, y, c = lax.axis_index("x"), lax.axis_index("y"), lax.axis_index("c")
        me = 4 * x + 2 * y + c
        buf[me] = p_ref[...]
        sends = []
        for q in range(1, ndev):
            qx, qy, qc = (q >> 2) & 1, (q >> 1) & 1, q & 1
            peer = (x ^ qx, y ^ qy, c ^ qc)
            rc = _remote(p_ref, buf.at[me], send.at[q - 1], recv.at[q - 1], peer)
            rc.start()
            sends.append(rc)
        for q in range(1, ndev):
            qx, qy, qc = (q >> 2) & 1, (q >> 1) & 1, q & 1
            slot = buf.at[4 * (x ^ qx) + 2 * (y ^ qy) + (c ^ qc)]
            _remote(slot, slot, send.at[q - 1], recv.at[q - 1], (x ^ qx, y ^ qy, c ^ qc)).wait_recv()
        for rc in sends:
            rc.wait_send()
        if reduce:
            acc = buf[0]
            for d in range(1, ndev):
                acc = acc + buf[d]
            o_ref[...] = acc

    vm = pl.BlockSpec(memory_space=pltpu.VMEM)
    sems = [pltpu.SemaphoreType.DMA((ndev - 1,)), pltpu.SemaphoreType.DMA((ndev - 1,))]
    if reduce:
        out_shape = jax.ShapeDtypeStruct((R, LANES), F32)
        scratch = [pltpu.VMEM((ndev, R, LANES), F32)] + sems
    else:
        out_shape = jax.ShapeDtypeStruct((ndev, R, LANES), F32)
        scratch = sems
    return pl.pallas_call(
        body, name=name, in_specs=[vm], out_specs=vm, out_shape=out_shape, scratch_shapes=scratch,
        compiler_params=pltpu.CompilerParams(vmem_limit_bytes=32 * 1024 * 1024),
    )(pack)


PACK_UNIT = SUBLANES * LANES


def _pack(arrays):
    flat, sizes = [], []
    for a in arrays:
        pieces = a if isinstance(a, (list, tuple)) else [a]
        v = jnp.concatenate([p.reshape(-1) for p in pieces]) if len(pieces) > 1 else pieces[0].reshape(-1)
        size = v.shape[0]
        padded = -(-size // PACK_UNIT) * PACK_UNIT
        flat.append(jnp.pad(v, (0, padded - size)))
        sizes.append((size, padded))
    return jnp.concatenate(flat).reshape(-1, LANES), sizes


def _unpack(pack, sizes, shapes):
    v = pack.reshape(-1)
    out, off = [], 0
    for (size, padded), shape in zip(sizes, shapes):
        out.append(v[off:off + size].reshape(shape))
        off += padded
    return out


BIG = ("ab_w_in", "ab_w_out", "c_w_in", "c_w_out", "f_w_up", "f_w_down")
COL_SHARDED = ("ab_w_in", "c_w_in", "f_w_up")
SMALL_REPLICATED = ("norm_mix", "norm_ffn", "norm_final", "a_ln_g", "a_ln_b", "a_w_s", "a_b_s",
                    "b_conv_b", "b_ln_g", "b_ln_b")
SMALL_SHARDED = ("b_conv_w", "c_conv_w", "f_conv_w")
SMALL = SMALL_REPLICATED + SMALL_SHARDED
ALL_WEIGHTS = ("norm_mix", "norm_ffn", "norm_final", "ab_w_in", "a_ln_g", "a_ln_b", "a_w_s", "a_b_s",
               "b_conv_w", "b_conv_b", "b_ln_g", "b_ln_b", "ab_w_out", "c_w_in", "c_conv_w", "c_w_out",
               "f_w_up", "f_conv_w", "f_w_down")


def _step(x, tgt, w, m, v):
    chip = 2 * lax.axis_index("x") + lax.axis_index("y")
    place = _place_scalars()

    gathered = _gather_weights([_cast_into_slot(w[n], place, name=f"cast_{n}") for n in BIG])
    big = {}
    for n, g in zip(BIG, gathered):
        L, S, rows, cols = g.shape
        big[n] = g if n in COL_SHARDED else g.reshape(L, S * rows, cols)
    conv_pack, conv_sizes = _pack([w[n] for n in SMALL_SHARDED])
    conv_all = _exchange_packs(conv_pack, reduce=False, name="gather_conv_weights")
    conv_shapes = [w[n].shape for n in SMALL_SHARDED]
    per_chip = [_unpack(conv_all[2 * s], conv_sizes, conv_shapes) for s in range(N_CHIPS)]
    small = {n: w[n] for n in SMALL_REPLICATED}
    for idx, n in enumerate(SMALL_SHARDED):
        small[n] = jnp.concatenate([per_chip[s][idx] for s in range(N_CHIPS)], axis=-1)

    loss, dx, sg, bg = _local_step(x, tgt, small, big)

    items, full = [], []
    for wi, n in enumerate(BIG):
        for l in range(w[n].shape[0]):
            g = bg[n][l] if isinstance(bg[n], list) else bg[n]
            rows, cols = w[n].shape[1:]
            items.append((wi, l))
            full.append(g.reshape(N_CHIPS, rows, cols))
    theirs = _reduce_pair_exchange(full)
    parts = [_pair_sum(g, o, place, name=f"pair_sum{t}") for t, (g, o) in enumerate(zip(full, theirs))]
    landed = _reduce_chip_exchange(parts)
    summed = [None] * len(BIG)
    for t, (wi, l) in enumerate(items):
        summed[wi] = _chip_sum(parts[t], landed[t], summed[wi], place, layer=l, shape=w[BIG[wi]].shape,
                               name=f"chip_sum{t}")
    grads_big = _reduce_pair_share(summed)

    g_pack, g_sizes = _pack([sg[n] for n in SMALL])
    g_sum = _exchange_packs(g_pack, reduce=True, name="allreduce_small_grads")
    full_shapes = [small[n].shape for n in SMALL]
    g_small = dict(zip(SMALL, _unpack(g_sum, g_sizes, full_shapes)))
    for n in SMALL_SHARDED:
        width = w[n].shape[-1]
        g_small[n] = lax.dynamic_slice_in_dim(g_small[n], chip * width, width, axis=g_small[n].ndim - 1)

    grad, delta, new_m, new_v = {}, {}, {}, {}
    for n, g in zip(BIG, grads_big):
        grad[n] = g
        delta[n], new_m[n], new_v[n] = _adamw(w[n], g, m[n], v[n], name=f"adamw_{n}")
    shapes = [w[n].shape for n in SMALL]
    wp, sizes = _pack([w[n] for n in SMALL])
    gp, _ = _pack([g_small[n] for n in SMALL])
    mp, _ = _pack([m[n] for n in SMALL])
    vp, _ = _pack([v[n] for n in SMALL])
    R = wp.shape[0]
    dp, m2p, v2p = _adamw(wp.reshape(1, R, LANES), gp.reshape(1, R, LANES), mp.reshape(1, R, LANES),
                          vp.reshape(1, R, LANES), name="adamw_small")
    for n, d_, m_, v_ in zip(SMALL, _unpack(dp, sizes, shapes), _unpack(m2p, sizes, shapes),
                             _unpack(v2p, sizes, shapes)):
        grad[n] = g_small[n]
        delta[n], new_m[n], new_v[n] = d_, m_, v_
    return loss, dx, grad, delta, new_m, new_v


def kernel(x, norm_mix, norm_ffn, norm_final, ab_w_in, a_ln_g, a_ln_b, a_w_s, a_b_s, b_conv_w, b_conv_b, b_ln_g, b_ln_b, ab_w_out, c_w_in, c_conv_w, c_w_out, f_w_up, f_conv_w, f_w_down, loss_target, m_norm_mix, m_norm_ffn, m_norm_final, m_ab_w_in, m_a_ln_g, m_a_ln_b, m_a_w_s, m_a_b_s, m_b_conv_w, m_b_conv_b, m_b_ln_g, m_b_ln_b, m_ab_w_out, m_c_w_in, m_c_conv_w, m_c_w_out, m_f_w_up, m_f_conv_w, m_f_w_down, v_norm_mix, v_norm_ffn, v_norm_final, v_ab_w_in, v_a_ln_g, v_a_ln_b, v_a_w_s, v_a_b_s, v_b_conv_w, v_b_conv_b, v_b_ln_g, v_b_ln_b, v_ab_w_out, v_c_w_in, v_c_conv_w, v_c_w_out, v_f_w_up, v_f_conv_w, v_f_w_down):
    given = dict(locals())
    w = {n: given[n] for n in ALL_WEIGHTS}
    m = {n: given["m_" + n] for n in ALL_WEIGHTS}
    v = {n: given["v_" + n] for n in ALL_WEIGHTS}
    T = x.shape[1]
    loss, dx, grad, delta, new_m, new_v = _step(x.reshape(T, D_MODEL), loss_target.reshape(T, D_MODEL), w, m, v)
    loss = lax.psum(loss[0, 0], ("x", "y", "c"))
    out = [loss, dx.reshape(x.shape)]
    for d in (grad, delta, new_m, new_v):
        out += [d[n] for n in ALL_WEIGHTS]
    return tuple(out)
```

```python
import functools
import math

import jax
import jax.numpy as jnp
from jax import lax
from jax.experimental import pallas as pl
from jax.experimental.pallas import tpu as pltpu

F32 = jnp.float32
BF16 = jnp.bfloat16

EPS = 1e-6
D_MODEL = 1024
CHUNK = 128
HEAD_DIM = 128
A_HEADS = 4
D_A = 512
D_B = 512
B_CONV = 31
C_CONV = 3
D_FF = 2816
F_CONV = 3
N_CHIPS = 4

ADAM_LR = 0.001
ADAM_B1 = 0.9
ADAM_B2 = 0.999
ADAM_EPS = 1e-08
ADAM_WD = 0.01
ADAM_STEP = 10

SUBLANES = 8
LANES = 128
HALO_SHORT = 16
HALO_LONG = 32
VMEM_BYTES_MAX = 56 * 1024 * 1024

INV_SQRT2 = 1.0 / math.sqrt(2.0)
INV_SQRT_2PI = 1.0 / math.sqrt(2.0 * math.pi)

MESH = pl.DeviceIdType.MESH


def _cparams(sem, vmem_mb):
    return pltpu.CompilerParams(dimension_semantics=sem,
                                vmem_limit_bytes=min(vmem_mb * 1024 * 1024, VMEM_BYTES_MAX))


def _pick(total, pref):
    for c in (2048, 1024, 512, 256, 128):
        if c <= pref and total % c == 0:
            return c
    raise ValueError(f"no tile for {total}")


def _sigmoid(x):
    return jax.nn.sigmoid(x)


def _silu(x):
    return x * _sigmoid(x)


def _dsilu(x):
    s = _sigmoid(x)
    return s * (1.0 + x * (1.0 - s))


def _gelu(x):
    return 0.5 * x * (1.0 + lax.erf(x * INV_SQRT2))


def _dgelu(x):
    return 0.5 * (1.0 + lax.erf(x * INV_SQRT2)) + x * jnp.exp(-0.5 * x * x) * INV_SQRT_2PI


def _ln_stats(x):
    mu = jnp.mean(x, axis=-1, keepdims=True)
    xc = x - mu
    var = jnp.mean(xc * xc, axis=-1, keepdims=True)
    r = lax.rsqrt(var + EPS)
    return xc * r, r


def _ln_bwd(dy, xh, r, g):
    dxh = dy * g
    m1 = jnp.mean(dxh, axis=-1, keepdims=True)
    m2 = jnp.mean(dxh * xh, axis=-1, keepdims=True)
    return r * (dxh - m1 - xh * m2)


def _rowsum(x):
    return jnp.sum(x, axis=0, keepdims=True)


def _mm_nn(a, w, *, layer, tm, tn, residual=None, out_dtype=F32, name):
    T, K = a.shape
    if w.ndim == 4:
        _, S, _, n4 = w.shape
        N = S * n4
        bps = n4 // tn
        w_spec = pl.BlockSpec((None, None, K, tn), lambda j, i: (layer, j // bps, 0, j % bps))
    else:
        N = w.shape[2]
        w_spec = pl.BlockSpec((None, K, tn), lambda j, i: (layer, 0, j))
    in_specs = [pl.BlockSpec((tm, K), lambda j, i: (i, 0)), w_spec]
    args = [a, w]
    if residual is not None:
        in_specs.append(pl.BlockSpec((tm, tn), lambda j, i: (i, j)))
        args.append(residual)

    def body(*refs):
        a_ref, w_ref, o_ref = refs[0], refs[1], refs[-1]
        acc = jnp.dot(a_ref[...].astype(BF16), w_ref[...], preferred_element_type=F32)
        if residual is not None:
            acc = refs[2][...] + acc
        o_ref[...] = acc.astype(out_dtype)

    return pl.pallas_call(
        body, name=name, grid=(N // tn, T // tm), in_specs=in_specs,
        out_specs=pl.BlockSpec((tm, tn), lambda j, i: (i, j)),
        out_shape=jax.ShapeDtypeStruct((T, N), out_dtype),
        compiler_params=_cparams(("parallel", "parallel"), 48),
    )(*args)


def _mm_nt(dy, w, *, layer, tm, tn, name, out_dtype=F32):
    T = dy.shape[0]
    nt_dims = (((1,), (1,)), ((), ()))
    if w.ndim == 4:
        _, S, K, n4 = w.shape

        def body(dy_ref, w_ref, o_ref):
            @pl.when(pl.program_id(1) == 0)
            def _():
                o_ref[...] = jnp.zeros_like(o_ref)
            o_ref[...] += lax.dot_general(dy_ref[...].astype(BF16), w_ref[...], nt_dims,
                                          preferred_element_type=F32)

        return pl.pallas_call(
            body, name=name, grid=(T // tm, S),
            in_specs=[pl.BlockSpec((tm, n4), lambda i, s: (i, s)),
                      pl.BlockSpec((None, None, K, n4), lambda i, s: (layer, s, 0, 0))],
            out_specs=pl.BlockSpec((tm, K), lambda i, s: (i, 0)),
            out_shape=jax.ShapeDtypeStruct((T, K), F32),
            compiler_params=_cparams(("parallel", "arbitrary"), 48),
        )(dy, w)
    _, R, N = w.shape

    def body2(dy_ref, w_ref, o_ref):
        o_ref[...] = lax.dot_general(dy_ref[...].astype(BF16), w_ref[...], nt_dims,
                                     preferred_element_type=F32).astype(out_dtype)

    return pl.pallas_call(
        body2, name=name, grid=(R // tn, T // tm),
        in_specs=[pl.BlockSpec((tm, N), lambda j, i: (i, 0)),
                  pl.BlockSpec((None, tn, N), lambda j, i: (layer, j, 0))],
        out_specs=pl.BlockSpec((tm, tn), lambda j, i: (i, j)),
        out_shape=jax.ShapeDtypeStruct((T, R), out_dtype),
        compiler_params=_cparams(("parallel", "parallel"), 48),
    )(dy, w)


def _mm_tn(a, dy, *, shards, tk, tn, tt, name):
    T, K = a.shape
    N = dy.shape[1]
    tn_dims = (((0,), (0,)), ((), ()))

    def body(a_ref, dy_ref, o_ref):
        @pl.when(pl.program_id(2) == 0)
        def _():
            o_ref[...] = jnp.zeros_like(o_ref)
        o_ref[...] += lax.dot_general(a_ref[...].astype(BF16), dy_ref[...].astype(BF16), tn_dims,
                                      preferred_element_type=F32)

    if shards is None:
        out_spec = pl.BlockSpec((tk, tn), lambda k, n, t: (k, n))
        out_shape = jax.ShapeDtypeStruct((K, N), F32)
    else:
        n4 = N // shards
        bps = n4 // tn
        out_spec = pl.BlockSpec((None, tk, tn), lambda k, n, t: (n // bps, k, n % bps))
        out_shape = jax.ShapeDtypeStruct((shards, K, n4), F32)
    return pl.pallas_call(
        body, name=name, grid=(K // tk, N // tn, T // tt),
        in_specs=[pl.BlockSpec((tt, tk), lambda k, n, t: (t, k)),
                  pl.BlockSpec((tt, tn), lambda k, n, t: (t, n))],
        out_specs=out_spec, out_shape=out_shape,
        compiler_params=_cparams(("parallel", "parallel", "arbitrary"), 48),
    )(a, dy)


def _rmsnorm_fwd(x, g, *, layer, tm, name):
    T, D = x.shape

    def body(x_ref, g_ref, h_ref):
        xf = x_ref[...]
        r = lax.rsqrt(jnp.mean(xf * xf, axis=-1, keepdims=True) + EPS)
        h_ref[...] = (xf * r * g_ref[...]).astype(BF16)

    return pl.pallas_call(
        body, name=name, grid=(T // tm,),
        in_specs=[pl.BlockSpec((tm, D), lambda i: (i, 0)),
                  pl.BlockSpec((None, 1, D), lambda i: (layer, 0, 0))],
        out_specs=pl.BlockSpec((tm, D), lambda i: (i, 0)),
        out_shape=jax.ShapeDtypeStruct((T, D), BF16),
        compiler_params=_cparams(("parallel",), 32),
    )(x, g)


def _rmsnorm_bwd(x, g, dh, dres, *, layer, tm, name):
    T, D = x.shape

    def body(x_ref, g_ref, dh_ref, dres_ref, dx_ref, dg_ref):
        @pl.when(pl.program_id(0) == 0)
        def _():
            dg_ref[...] = jnp.zeros_like(dg_ref)
        xf = x_ref[...]
        r = lax.rsqrt(jnp.mean(xf * xf, axis=-1, keepdims=True) + EPS)
        xh = xf * r
        dh = dh_ref[...]
        dg_ref[...] += _rowsum(dh * xh)
        dxh = dh * g_ref[...]
        dx_ref[...] = dres_ref[...] + r * (dxh - xh * jnp.mean(dxh * xh, axis=-1, keepdims=True))

    return pl.pallas_call(
        body, name=name, grid=(T // tm,),
        in_specs=[pl.BlockSpec((tm, D), lambda i: (i, 0)),
                  pl.BlockSpec((None, 1, D), lambda i: (layer, 0, 0)),
                  pl.BlockSpec((tm, D), lambda i: (i, 0)),
                  pl.BlockSpec((tm, D), lambda i: (i, 0))],
        out_specs=[pl.BlockSpec((tm, D), lambda i: (i, 0)),
                   pl.BlockSpec((1, D), lambda i: (0, 0))],
        out_shape=[jax.ShapeDtypeStruct((T, D), F32), jax.ShapeDtypeStruct((1, D), F32)],
        compiler_params=_cparams(("arbitrary",), 40),
    )(x, g, dh, dres)


def _loss_head(x, tgt, g, *, tm, name):
    T, D = x.shape

    def body(x_ref, t_ref, g_ref, loss_ref, dx_ref, dg_ref):
        @pl.when(pl.program_id(0) == 0)
        def _():
            dg_ref[...] = jnp.zeros_like(dg_ref)
            loss_ref[...] = jnp.zeros_like(loss_ref)
        xf = x_ref[...]
        gg = g_ref[...]
        r = lax.rsqrt(jnp.mean(xf * xf, axis=-1, keepdims=True) + EPS)
        xh = xf * r
        err = xh * gg - t_ref[...]
        row = jnp.mean(err * err, axis=-1, keepdims=True)
        loss_ref[...] += 0.5 * jnp.sum(row, axis=0, keepdims=True)
        dy = err * (1.0 / D)
        dg_ref[...] += _rowsum(dy * xh)
        dxh = dy * gg
        dx_ref[...] = r * (dxh - xh * jnp.mean(dxh * xh, axis=-1, keepdims=True))

    return pl.pallas_call(
        body, name=name, grid=(T // tm,),
        in_specs=[pl.BlockSpec((tm, D), lambda i: (i, 0)),
                  pl.BlockSpec((tm, D), lambda i: (i, 0)),
                  pl.BlockSpec((1, D), lambda i: (0, 0))],
        out_specs=[pl.BlockSpec((1, 1), lambda i: (0, 0)),
                   pl.BlockSpec((tm, D), lambda i: (i, 0)),
                   pl.BlockSpec((1, D), lambda i: (0, 0))],
        out_shape=[jax.ShapeDtypeStruct((1, 1), F32), jax.ShapeDtypeStruct((T, D), F32),
                   jax.ShapeDtypeStruct((1, D), F32)],
        compiler_params=_cparams(("arbitrary",), 40),
    )(x, tgt, g)


CONV_ROWS = 64
CONV_COLS = 256


def _halo_prev_index(tm, halo):
    per = tm // halo
    return lambda i: jnp.maximum(i * per - 1, 0)


def _halo_next_index(tm, halo, total):
    per = tm // halo
    last = total // halo - 1
    return lambda i: jnp.minimum((i + 1) * per, last)


def _causal_mask():
    t = lax.broadcasted_iota(jnp.int32, (CHUNK, CHUNK), 0)
    s = lax.broadcasted_iota(jnp.int32, (CHUNK, CHUNK), 1)
    return s <= t


def _mixer_ab_fwd(z, a_ln_g, a_ln_b, w_s, b_s, conv_w, conv_b, b_ln_g, b_ln_b, *, tm, name):
    T = z.shape[0]
    nchunk = tm // CHUNK
    halo = HALO_LONG

    def body(za_ref, zb_ref, zh_ref, alg_ref, alb_ref, ws_ref, bs_ref, cw_ref, cbias_ref,
             blg_ref, blb_ref, y_ref, cb_ref, ext_ref):
        i = pl.program_id(0)
        gu = _gelu(za_ref[:, :D_A].astype(F32))
        gv = _gelu(za_ref[:, D_A:].astype(F32))
        xh, _ = _ln_stats(gv)
        lv = (xh * alg_ref[...] + alb_ref[...]).astype(BF16)
        mask = _causal_mask()
        for h in range(A_HEADS):
            wm = jnp.where(mask, ws_ref[h], 0.0).astype(BF16)
            cols = slice(h * HEAD_DIM, (h + 1) * HEAD_DIM)
            for c in range(nchunk):
                rows = slice(c * CHUNK, (c + 1) * CHUNK)
                mixed = jnp.dot(wm, lv[rows, cols], preferred_element_type=F32) + bs_ref[h]
                y_ref[rows, cols] = (gu[rows, cols] * mixed).astype(BF16)
        ext_ref[halo:halo + tm, :] = zb_ref[:, :D_B].astype(F32) * _sigmoid(zb_ref[:, D_B:].astype(F32))
        prev = zh_ref[:, :D_B].astype(F32) * _sigmoid(zh_ref[:, D_B:].astype(F32))
        ext_ref[0:halo, :] = jnp.where(i > 0, prev, 0.0)
        for rb in range(tm // CONV_ROWS):
            for cb in range(D_B // CONV_COLS):
                cs = slice(cb * CONV_COLS, (cb + 1) * CONV_COLS)
                acc = jnp.zeros((CONV_ROWS, CONV_COLS), F32)
                for k in range(B_CONV):
                    off = rb * CONV_ROWS + halo - (B_CONV - 1) + k
                    acc = acc + cw_ref[k:k + 1, cs] * ext_ref[off:off + CONV_ROWS, cs]
                cb_ref[rb * CONV_ROWS:(rb + 1) * CONV_ROWS, cs] = acc + cbias_ref[:, cs]
        xhb, _ = _ln_stats(cb_ref[...])
        y_ref[:, D_A:] = _silu(xhb * blg_ref[...] + blb_ref[...]).astype(BF16)

    row = lambda i: (i, 0)
    par = lambda i: (0, 0)
    return pl.pallas_call(
        body, name=name, grid=(T // tm,),
        in_specs=[pl.BlockSpec((tm, 2 * D_A), lambda i: (i, 0)),
                  pl.BlockSpec((tm, 2 * D_B), lambda i: (i, 1)),
                  pl.BlockSpec((halo, 2 * D_B), lambda i: (_halo_prev_index(tm, halo)(i), 1)),
                  pl.BlockSpec((1, D_A), par), pl.BlockSpec((1, D_A), par),
                  pl.BlockSpec((A_HEADS, CHUNK, CHUNK), lambda i: (0, 0, 0)),
                  pl.BlockSpec((A_HEADS, CHUNK, 1), lambda i: (0, 0, 0)),
                  pl.BlockSpec((B_CONV, D_B), par), pl.BlockSpec((1, D_B), par),
                  pl.BlockSpec((1, D_B), par), pl.BlockSpec((1, D_B), par)],
        out_specs=[pl.BlockSpec((tm, D_A + D_B), row), pl.BlockSpec((tm, D_B), row)],
        out_shape=[jax.ShapeDtypeStruct((T, D_A + D_B), BF16), jax.ShapeDtypeStruct((T, D_B), F32)],
        scratch_shapes=[pltpu.VMEM((halo + tm, D_B), F32)],
        compiler_params=_cparams(("parallel",), 40),
    )(z, z, z, a_ln_g, a_ln_b, w_s, b_s, conv_w, conv_b, b_ln_g, b_ln_b)


def _mixer_ab_bwd_pre(z, cb, dy, a_ln_g, a_ln_b, w_s, b_s, b_ln_g, b_ln_b, *, tm, name):
    T = z.shape[0]
    nchunk = tm // CHUNK
    tn_dims = (((0,), (0,)), ((), ()))
    nt_dims = (((1,), (1,)), ((), ()))

    def body(za_ref, cb_ref, dy_ref, alg_ref, alb_ref, ws_ref, bs_ref, blg_ref, blb_ref,
             dza_ref, dcb_ref, dalg_ref, dalb_ref, dws_ref, dbs_ref, dblg_ref, dblb_ref,
             dlv_ref):
        @pl.when(pl.program_id(0) == 0)
        def _():
            for ref in (dalg_ref, dalb_ref, dws_ref, dbs_ref, dblg_ref, dblb_ref):
                ref[...] = jnp.zeros_like(ref)
        ua = za_ref[:, :D_A].astype(F32)
        va = za_ref[:, D_A:].astype(F32)
        gu = _gelu(ua)
        gv = _gelu(va)
        xh, r = _ln_stats(gv)
        alg = alg_ref[...]
        lv = (xh * alg + alb_ref[...]).astype(BF16)
        dya = dy_ref[:, :D_A].astype(F32)
        mask = _causal_mask()
        for h in range(A_HEADS):
            wm = jnp.where(mask, ws_ref[h], 0.0).astype(BF16)
            cols = slice(h * HEAD_DIM, (h + 1) * HEAD_DIM)
            dwm = jnp.zeros((CHUNK, CHUNK), F32)
            dbs = jnp.zeros((CHUNK, 1), F32)
            for c in range(nchunk):
                rows = slice(c * CHUNK, (c + 1) * CHUNK)
                lvb = lv[rows, cols]
                mixed = jnp.dot(wm, lvb, preferred_element_type=F32) + bs_ref[h]
                dyb = dya[rows, cols]
                dza_ref[rows, cols] = (dyb * mixed * _dgelu(ua[rows, cols])).astype(BF16)
                dmixed = dyb * gu[rows, cols]
                dmb = dmixed.astype(BF16)
                dlv_ref[rows, cols] = lax.dot_general(wm, dmb, tn_dims, preferred_element_type=F32)
                dwm = dwm + lax.dot_general(dmb, lvb, nt_dims, preferred_element_type=F32)
                dbs = dbs + jnp.sum(dmixed, axis=1, keepdims=True)
            dws_ref[h] += jnp.where(mask, dwm, 0.0)
            dbs_ref[h] += dbs
        dlv = dlv_ref[...]
        dalg_ref[...] += _rowsum(dlv * xh)
        dalb_ref[...] += _rowsum(dlv)
        dgv = _ln_bwd(dlv, xh, r, alg)
        dza_ref[:, D_A:] = (dgv * _dgelu(va)).astype(BF16)
        xhb, rb = _ln_stats(cb_ref[...])
        blg = blg_ref[...]
        lb = xhb * blg + blb_ref[...]
        dlb = dy_ref[:, D_A:].astype(F32) * _dsilu(lb)
        dblg_ref[...] += _rowsum(dlb * xhb)
        dblb_ref[...] += _rowsum(dlb)
        dcb_ref[...] = _ln_bwd(dlb, xhb, rb, blg)

    row = lambda i: (i, 0)
    par = lambda i: (0, 0)
    par3 = lambda i: (0, 0, 0)
    return pl.pallas_call(
        body, name=name, grid=(T // tm,),
        in_specs=[pl.BlockSpec((tm, 2 * D_A), row), pl.BlockSpec((tm, D_B), row),
                  pl.BlockSpec((tm, D_A + D_B), row),
                  pl.BlockSpec((1, D_A), par), pl.BlockSpec((1, D_A), par),
                  pl.BlockSpec((A_HEADS, CHUNK, CHUNK), par3),
                  pl.BlockSpec((A_HEADS, CHUNK, 1), par3),
                  pl.BlockSpec((1, D_B), par), pl.BlockSpec((1, D_B), par)],
        out_specs=[pl.BlockSpec((tm, 2 * D_A), row), pl.BlockSpec((tm, D_B), row),
                   pl.BlockSpec((1, D_A), par), pl.BlockSpec((1, D_A), par),
                   pl.BlockSpec((A_HEADS, CHUNK, CHUNK), par3),
                   pl.BlockSpec((A_HEADS, CHUNK, 1), par3),
                   pl.BlockSpec((1, D_B), par), pl.BlockSpec((1, D_B), par)],
        out_shape=[jax.ShapeDtypeStruct((T, 2 * D_A), BF16), jax.ShapeDtypeStruct((T, D_B), F32),
                   jax.ShapeDtypeStruct((1, D_A), F32), jax.ShapeDtypeStruct((1, D_A), F32),
                   jax.ShapeDtypeStruct((A_HEADS, CHUNK, CHUNK), F32),
                   jax.ShapeDtypeStruct((A_HEADS, CHUNK, 1), F32),
                   jax.ShapeDtypeStruct((1, D_B), F32), jax.ShapeDtypeStruct((1, D_B), F32)],
        scratch_shapes=[pltpu.VMEM((tm, D_A), F32)],
        compiler_params=_cparams(("arbitrary",), 40),
    )(z, cb, dy, a_ln_g, a_ln_b, w_s, b_s, b_ln_g, b_ln_b)


def _mixer_b_conv_bwd(z, dcb, conv_w, *, tm, name):
    T = z.shape[0]
    halo = HALO_LONG

    def body(zb_ref, dcb_ref, dcn_ref, cw_ref, dzb_ref, dcw_ref, dbias_ref, dext_ref):
        i = pl.program_id(0)
        last = pl.num_programs(0) - 1

        @pl.when(i == 0)
        def _():
            dcw_ref[...] = jnp.zeros_like(dcw_ref)
            dbias_ref[...] = jnp.zeros_like(dbias_ref)
        dcb = dcb_ref[...]
        dext_ref[0:tm, :] = dcb
        dext_ref[tm:tm + halo, :] = jnp.where(i < last, dcn_ref[...], 0.0)
        dbias_ref[...] += _rowsum(dcb)
        for rb in range(tm // CONV_ROWS):
            for cb in range(D_B // CONV_COLS):
                cs = slice(cb * CONV_COLS, (cb + 1) * CONV_COLS)
                gcs = slice(D_B + cb * CONV_COLS, D_B + (cb + 1) * CONV_COLS)
                rs = slice(rb * CONV_ROWS, (rb + 1) * CONV_ROWS)
                xbb = zb_ref[rs, cs].astype(F32)
                sgb = _sigmoid(zb_ref[rs, gcs].astype(F32))
                yb0 = xbb * sgb
                acc = jnp.zeros((CONV_ROWS, CONV_COLS), F32)
                for k in range(B_CONV):
                    off = rb * CONV_ROWS + (B_CONV - 1) - k
                    shifted = dext_ref[off:off + CONV_ROWS, cs]
                    acc = acc + cw_ref[k:k + 1, cs] * shifted
                    dcw_ref[k:k + 1, cs] += _rowsum(shifted * yb0)
                dzb_ref[rs, cs] = (acc * sgb).astype(BF16)
                dzb_ref[rs, gcs] = (acc * xbb * sgb * (1.0 - sgb)).astype(BF16)

    row = lambda i: (i, 0)
    par = lambda i: (0, 0)
    return pl.pallas_call(
        body, name=name, grid=(T // tm,),
        in_specs=[pl.BlockSpec((tm, 2 * D_B), lambda i: (i, 1)),
                  pl.BlockSpec((tm, D_B), row),
                  pl.BlockSpec((halo, D_B), lambda i: (_halo_next_index(tm, halo, T)(i), 0)),
                  pl.BlockSpec((B_CONV, D_B), par)],
        out_specs=[pl.BlockSpec((tm, 2 * D_B), row), pl.BlockSpec((B_CONV, D_B), par),
                   pl.BlockSpec((1, D_B), par)],
        out_shape=[jax.ShapeDtypeStruct((T, 2 * D_B), BF16), jax.ShapeDtypeStruct((B_CONV, D_B), F32),
                   jax.ShapeDtypeStruct((1, D_B), F32)],
        scratch_shapes=[pltpu.VMEM((tm + halo, D_B), F32)],
        compiler_params=_cparams(("arbitrary",), 40),
    )(z, dcb, dcb, conv_w)


def _conv3(w_ref, ext_ref, base, rows, cs):
    acc = w_ref[0:1, cs] * ext_ref[base - 2:base - 2 + rows, cs]
    acc = acc + w_ref[1:2, cs] * ext_ref[base - 1:base - 1 + rows, cs]
    return acc + w_ref[2:3, cs] * ext_ref[base:base + rows, cs]


def _conv3_t(w_ref, ext_ref, base, rows, cs):
    acc = w_ref[0:1, cs] * ext_ref[base + 2:base + 2 + rows, cs]
    acc = acc + w_ref[1:2, cs] * ext_ref[base + 1:base + 1 + rows, cs]
    return acc + w_ref[2:3, cs] * ext_ref[base:base + rows, cs]


def _mixer_c_fwd(z, conv_w, *, tm, name):
    T = z.shape[0]
    D = D_MODEL
    halo = HALO_SHORT
    full = slice(0, D)

    def body(bg_ref, cg_ref, xv_ref, cgh_ref, xvh_ref, w_ref, r_ref, ext_ref):
        i = pl.program_id(0)
        ext_ref[halo:halo + tm, :] = cg_ref[...].astype(F32) * xv_ref[...].astype(F32)
        ext_ref[0:halo, :] = jnp.where(i > 0, cgh_ref[...].astype(F32) * xvh_ref[...].astype(F32), 0.0)
        q = _conv3(w_ref, ext_ref, halo, tm, full)
        r_ref[...] = (bg_ref[...].astype(F32) * q).astype(BF16)

    hp = _halo_prev_index(tm, halo)
    return pl.pallas_call(
        body, name=name, grid=(T // tm,),
        in_specs=[pl.BlockSpec((tm, D), lambda i: (i, 0)), pl.BlockSpec((tm, D), lambda i: (i, 1)),
                  pl.BlockSpec((tm, D), lambda i: (i, 2)),
                  pl.BlockSpec((halo, D), lambda i: (hp(i), 1)),
                  pl.BlockSpec((halo, D), lambda i: (hp(i), 2)),
                  pl.BlockSpec((None, C_CONV, D), lambda i: (0, 0, 0))],
        out_specs=pl.BlockSpec((tm, D), lambda i: (i, 0)),
        out_shape=jax.ShapeDtypeStruct((T, D), BF16),
        scratch_shapes=[pltpu.VMEM((halo + tm, D), F32)],
        compiler_params=_cparams(("parallel",), 40),
    )(z, z, z, z, z, conv_w)


def _mixer_c_bwd(z, dr, conv_w, *, tm, name):
    T = z.shape[0]
    D = D_MODEL
    halo = HALO_SHORT
    full = slice(0, D)

    def body(bg_ref, cg_ref, xv_ref, cgh_ref, xvh_ref, bgn_ref, dr_ref, drn_ref, w_ref,
             dz_ref, dw_ref, ext_ref, dext_ref):
        i = pl.program_id(0)
        last = pl.num_programs(0) - 1

        @pl.when(i == 0)
        def _():
            dw_ref[...] = jnp.zeros_like(dw_ref)
        cg = cg_ref[...].astype(F32)
        xv = xv_ref[...].astype(F32)
        dr = dr_ref[...].astype(F32)
        p = cg * xv
        ext_ref[halo:halo + tm, :] = p
        ext_ref[0:halo, :] = jnp.where(i > 0, cgh_ref[...].astype(F32) * xvh_ref[...].astype(F32), 0.0)
        dext_ref[0:tm, :] = dr * bg_ref[...].astype(F32)
        dext_ref[tm:tm + halo, :] = jnp.where(i < last, drn_ref[...].astype(F32) * bgn_ref[...].astype(F32), 0.0)
        q = _conv3(w_ref, ext_ref, halo, tm, full)
        dz_ref[:, 0:D] = (dr * q).astype(BF16)
        dp = jnp.zeros((tm, D), F32)
        for k in range(C_CONV):
            shifted = dext_ref[2 - k:2 - k + tm, :]
            dp = dp + w_ref[k:k + 1, :] * shifted
            dw_ref[k:k + 1, :] += _rowsum(shifted * p)
        dz_ref[:, D:2 * D] = (dp * xv).astype(BF16)
        dz_ref[:, 2 * D:3 * D] = (dp * cg).astype(BF16)

    hp = _halo_prev_index(tm, halo)
    hn = _halo_next_index(tm, halo, T)
    return pl.pallas_call(
        body, name=name, grid=(T // tm,),
        in_specs=[pl.BlockSpec((tm, D), lambda i: (i, 0)), pl.BlockSpec((tm, D), lambda i: (i, 1)),
                  pl.BlockSpec((tm, D), lambda i: (i, 2)),
                  pl.BlockSpec((halo, D), lambda i: (hp(i), 1)),
                  pl.BlockSpec((halo, D), lambda i: (hp(i), 2)),
                  pl.BlockSpec((halo, D), lambda i: (hn(i), 0)),
                  pl.BlockSpec((tm, D), lambda i: (i, 0)),
                  pl.BlockSpec((halo, D), lambda i: (hn(i), 0)),
                  pl.BlockSpec((None, C_CONV, D), lambda i: (0, 0, 0))],
        out_specs=[pl.BlockSpec((tm, 3 * D), lambda i: (i, 0)),
                   pl.BlockSpec((C_CONV, D), lambda i: (0, 0))],
        out_shape=[jax.ShapeDtypeStruct((T, 3 * D), BF16), jax.ShapeDtypeStruct((C_CONV, D), F32)],
        scratch_shapes=[pltpu.VMEM((halo + tm, D), F32), pltpu.VMEM((tm + halo, D), F32)],
        compiler_params=_cparams(("arbitrary",), 48),
    )(z, z, z, z, z, z, dr, dr, conv_w)


FFN_COLS = 256


def _ffn_act_fwd(up, conv_w, *, layer, tm, name):
    T = up.shape[0]
    halo = HALO_SHORT
    W = FFN_COLS

    def body(up_ref, uph_ref, w_ref, a_ref, upc_ref, ext_ref):
        i = pl.program_id(0)
        ext_ref[halo:halo + tm, :] = up_ref[...].astype(F32)
        ext_ref[0:halo, :] = jnp.where(i > 0, uph_ref[...].astype(F32), 0.0)
        for cb in range(D_FF // W):
            gs = slice(cb * W, (cb + 1) * W)
            vs = slice(D_FF + cb * W, D_FF + (cb + 1) * W)
            g = _conv3(w_ref, ext_ref, halo, tm, gs)
            v = _conv3(w_ref, ext_ref, halo, tm, vs)
            upc_ref[:, gs] = g.astype(BF16)
            upc_ref[:, vs] = v.astype(BF16)
            a_ref[:, gs] = (_silu(g) * v).astype(BF16)

    return pl.pallas_call(
        body, name=name, grid=(T // tm,),
        in_specs=[pl.BlockSpec((tm, 2 * D_FF), lambda i: (i, 0)),
                  pl.BlockSpec((halo, 2 * D_FF), lambda i: (_halo_prev_index(tm, halo)(i), 0)),
                  pl.BlockSpec((None, F_CONV, 2 * D_FF), lambda i: (layer, 0, 0))],
        out_specs=[pl.BlockSpec((tm, D_FF), lambda i: (i, 0)),
                   pl.BlockSpec((tm, 2 * D_FF), lambda i: (i, 0))],
        out_shape=[jax.ShapeDtypeStruct((T, D_FF), BF16), jax.ShapeDtypeStruct((T, 2 * D_FF), BF16)],
        scratch_shapes=[pltpu.VMEM((halo + tm, 2 * D_FF), F32)],
        compiler_params=_cparams(("parallel",), 48),
    )(up, up, conv_w)


def _ffn_act_bwd(up, upc, da, conv_w, *, layer, tm, name):
    T = up.shape[0]
    halo = HALO_SHORT
    W = FFN_COLS

    def body(up_ref, upc_ref, upcn_ref, da_ref, dan_ref, w_ref, dup_ref, dw_ref, dext_ref):
        i = pl.program_id(0)
        last = pl.num_programs(0) - 1

        @pl.when(i == 0)
        def _():
            dw_ref[...] = jnp.zeros_like(dw_ref)
        live = jnp.where(i < last, 1.0, 0.0)
        for cb in range(D_FF // W):
            gs = slice(cb * W, (cb + 1) * W)
            vs = slice(D_FF + cb * W, D_FF + (cb + 1) * W)
            g = jnp.concatenate([upc_ref[:, gs], upcn_ref[:, gs]], axis=0).astype(F32)
            v = jnp.concatenate([upc_ref[:, vs], upcn_ref[:, vs]], axis=0).astype(F32)
            da = jnp.concatenate([da_ref[:, gs].astype(F32), dan_ref[:, gs].astype(F32) * live], axis=0)
            s = _sigmoid(g)
            silu = g * s
            dext_ref[:, gs] = da * v * (s * (1.0 + g * (1.0 - s)))
            dext_ref[:, vs] = da * silu
            for cs in (gs, vs):
                u = up_ref[:, cs].astype(F32)
                acc = jnp.zeros((tm, W), F32)
                for k in range(F_CONV):
                    shifted = dext_ref[2 - k:2 - k + tm, cs]
                    acc = acc + w_ref[k:k + 1, cs] * shifted
                    dw_ref[k:k + 1, cs] += _rowsum(shifted * u)
                dup_ref[:, cs] = acc.astype(BF16)

    hn = _halo_next_index(tm, halo, T)
    return pl.pallas_call(
        body, name=name, grid=(T // tm,),
        in_specs=[pl.BlockSpec((tm, 2 * D_FF), lambda i: (i, 0)),
                  pl.BlockSpec((tm, 2 * D_FF), lambda i: (i, 0)),
                  pl.BlockSpec((halo, 2 * D_FF), lambda i: (hn(i), 0)),
                  pl.BlockSpec((tm, D_FF), lambda i: (i, 0)),
                  pl.BlockSpec((halo, D_FF), lambda i: (hn(i), 0)),
                  pl.BlockSpec((None, F_CONV, 2 * D_FF), lambda i: (layer, 0, 0))],
        out_specs=[pl.BlockSpec((tm, 2 * D_FF), lambda i: (i, 0)),
                   pl.BlockSpec((F_CONV, 2 * D_FF), lambda i: (0, 0))],
        out_shape=[jax.ShapeDtypeStruct((T, 2 * D_FF), BF16),
                   jax.ShapeDtypeStruct((F_CONV, 2 * D_FF), F32)],
        scratch_shapes=[pltpu.VMEM((tm + halo, 2 * D_FF), F32)],
        compiler_params=_cparams(("arbitrary",), 56),
    )(up, upc, upc, da, da, conv_w)


def _local_step(x, tgt, small, big):
    T = x.shape[0]
    tm_e = _pick(T, 256)
    tm_n = _pick(T, 512)
    tm = _pick(T, 1024)
    tm_r = _pick(T, 2048)
    tt = _pick(T, 512)
    nm = small["norm_mix"].reshape(2, 1, D_MODEL)
    nf = small["norm_ffn"].reshape(2, 1, D_MODEL)
    ngf = small["norm_final"].reshape(1, D_MODEL)
    b_s = small["a_b_s"].reshape(A_HEADS, CHUNK, 1)
    w_s = small["a_w_s"].reshape(A_HEADS, CHUNK, CHUNK)
    b_conv_w = small["b_conv_w"].reshape(B_CONV, D_B)
    sg, bg = {}, {}

    h_m0 = _rmsnorm_fwd(x, nm, layer=0, tm=tm_n, name="norm_mix0")
    z_ab = _mm_nn(h_m0, big["ab_w_in"], layer=0, tm=tm, tn=512, out_dtype=BF16, name="ab_in")
    yab, cb = _mixer_ab_fwd(z_ab, small["a_ln_g"], small["a_ln_b"], w_s, b_s, b_conv_w,
                            small["b_conv_b"], small["b_ln_g"], small["b_ln_b"], tm=tm_e, name="mixer_ab")
    x1 = _mm_nn(yab, big["ab_w_out"], layer=0, tm=tm, tn=512, residual=x, name="ab_out")

    def ffn_fwd(xin, layer):
        h = _rmsnorm_fwd(xin, nf, layer=layer, tm=tm_n, name=f"norm_ffn{layer}")
        up = _mm_nn(h, big["f_w_up"], layer=layer, tm=tm, tn=1408, out_dtype=BF16, name=f"ffn_up{layer}")
        a, upc = _ffn_act_fwd(up, small["f_conv_w"], layer=layer, tm=tm_e, name=f"ffn_act{layer}")
        xout = _mm_nn(a, big["f_w_down"], layer=layer, tm=tm, tn=512, residual=xin, name=f"ffn_down{layer}")
        return h, up, upc, a, xout

    h_f0, up0, upc0, a0, x2 = ffn_fwd(x1, 0)
    h_m1 = _rmsnorm_fwd(x2, nm, layer=1, tm=tm_n, name="norm_mix1")
    z_c = _mm_nn(h_m1, big["c_w_in"], layer=0, tm=tm, tn=768, out_dtype=BF16, name="c_in")
    r = _mixer_c_fwd(z_c, small["c_conv_w"], tm=tm_e, name="mixer_c")
    x3 = _mm_nn(r, big["c_w_out"], layer=0, tm=tm, tn=512, residual=x2, name="c_out")
    h_f1, up1, upc1, a1, x4 = ffn_fwd(x3, 1)
    loss, dx, sg["norm_final"] = _loss_head(x4, tgt, ngf, tm=tm_n, name="loss_head")

    def ffn_bwd(dx, xin, h, up, upc, a, layer):
        da = _mm_nt(dx, big["f_w_down"], layer=layer, tm=tm, tn=1408, out_dtype=BF16, name=f"ffn_down_dx{layer}")
        dwd = _mm_tn(a, dx, shards=None, tk=1408, tn=1024, tt=tt, name=f"ffn_down_dw{layer}")
        dup, dcw = _ffn_act_bwd(up, upc, da, small["f_conv_w"], layer=layer, tm=tm_e, name=f"ffn_act_bwd{layer}")
        dh = _mm_nt(dup, big["f_w_up"], layer=layer, tm=tm_r, tn=None, name=f"ffn_up_dx{layer}")
        dwu = _mm_tn(h, dup, shards=N_CHIPS, tk=1024, tn=1408, tt=tt, name=f"ffn_up_dw{layer}")
        dxin, dg = _rmsnorm_bwd(xin, nf, dh, dx, layer=layer, tm=tm_n, name=f"norm_ffn_bwd{layer}")
        return dxin, dg, dcw, dwu, dwd

    dx, dnf1, dfc1, dwu1, dwd1 = ffn_bwd(dx, x3, h_f1, up1, upc1, a1, 1)
    dr = _mm_nt(dx, big["c_w_out"], layer=0, tm=tm, tn=512, out_dtype=BF16, name="c_out_dx")
    bg["c_w_out"] = _mm_tn(r, dx, shards=None, tk=1024, tn=1024, tt=tt, name="c_out_dw")
    dz_c, dccw = _mixer_c_bwd(z_c, dr, small["c_conv_w"], tm=tm_e, name="mixer_c_bwd")
    sg["c_conv_w"] = dccw.reshape(1, C_CONV, D_MODEL)
    dh = _mm_nt(dz_c, big["c_w_in"], layer=0, tm=tm_r, tn=None, name="c_in_dx")
    bg["c_w_in"] = _mm_tn(h_m1, dz_c, shards=N_CHIPS, tk=1024, tn=768, tt=tt, name="c_in_dw")
    dx, dnm1 = _rmsnorm_bwd(x2, nm, dh, dx, layer=1, tm=tm_n, name="norm_mix_bwd1")
    dx, dnf0, dfc0, dwu0, dwd0 = ffn_bwd(dx, x1, h_f0, up0, upc0, a0, 0)
    dyab = _mm_nt(dx, big["ab_w_out"], layer=0, tm=tm, tn=512, out_dtype=BF16, name="ab_out_dx")
    bg["ab_w_out"] = _mm_tn(yab, dx, shards=None, tk=1024, tn=1024, tt=tt, name="ab_out_dw")
    (dza, dcb, sg["a_ln_g"], sg["a_ln_b"], dws, dbs, sg["b_ln_g"], sg["b_ln_b"]) = _mixer_ab_bwd_pre(
        z_ab, cb, dyab, small["a_ln_g"], small["a_ln_b"], w_s, b_s, small["b_ln_g"], small["b_ln_b"],
        tm=tm_e, name="mixer_ab_bwd")
    dzb, dbcw, sg["b_conv_b"] = _mixer_b_conv_bwd(z_ab, dcb, b_conv_w, tm=tm_e, name="mixer_b_conv_bwd")
    sg["a_w_s"] = dws.reshape(1, A_HEADS, CHUNK, CHUNK)
    sg["a_b_s"] = dbs.reshape(1, A_HEADS, CHUNK)
    sg["b_conv_w"] = dbcw.reshape(1, B_CONV, D_B)
    dz_ab = jnp.concatenate([dza, dzb], axis=1)
    dh = _mm_nt(dz_ab, big["ab_w_in"], layer=0, tm=tm_r, tn=None, name="ab_in_dx")
    bg["ab_w_in"] = _mm_tn(h_m0, dz_ab, shards=N_CHIPS, tk=1024, tn=512, tt=tt, name="ab_in_dw")
    dx, dnm0 = _rmsnorm_bwd(x, nm, dh, dx, layer=0, tm=tm_n, name="norm_mix_bwd0")

    sg["norm_mix"] = [dnm0, dnm1]
    sg["norm_ffn"] = [dnf0, dnf1]
    sg["f_conv_w"] = [dfc0, dfc1]
    bg["f_w_up"] = [dwu0, dwu1]
    bg["f_w_down"] = [dwd0, dwd1]
    return loss, dx, sg, bg


BLOCK_BYTES = 3 * 1024 * 1024


BF16_SUBLANES = 16


def _row_tile(rows, row_bytes, step=SUBLANES):
    best = None
    for tr in range(step, rows + 1, step):
        if rows % tr == 0 and tr * row_bytes <= BLOCK_BYTES:
            best = tr
    if best is None:
        raise ValueError(f"no row tile for {rows}")
    return best


def _place_scalars():
    x, y, c = lax.axis_index("x"), lax.axis_index("y"), lax.axis_index("c")
    return jnp.stack([c, 2 * x + y, 2 * (1 - x) + y, 2 * x + (1 - y), 2 * (1 - x) + (1 - y)]).astype(jnp.int32)


def _cast_into_slot(w, place, *, name):
    L, rows, cols = w.shape
    tr = _row_tile(rows, cols * 4, BF16_SUBLANES)

    def body(place_ref, w_ref, o_ref):
        o_ref[...] = w_ref[...].astype(BF16)

    return pl.pallas_call(
        body, name=name,
        grid_spec=pltpu.PrefetchScalarGridSpec(
            num_scalar_prefetch=1, grid=(L, rows // tr),
            in_specs=[pl.BlockSpec((None, tr, cols), lambda l, i, p: (l, i, 0))],
            out_specs=pl.BlockSpec((None, None, tr, cols), lambda l, i, p: (l, p[1], i, 0))),
        out_shape=jax.ShapeDtypeStruct((L, N_CHIPS, rows, cols), BF16),
        compiler_params=_cparams(("parallel", "parallel"), 32),
    )(place, w)


def _pair_sum(g, theirs, place, *, name):
    S, rows, cols = g.shape
    half = rows // 2
    tr = _row_tile(half, cols * 4, BF16_SUBLANES)
    nb = half // tr

    def body(place_ref, g_ref, t_ref, o_ref):
        o_ref[...] = (g_ref[...] + t_ref[...]).astype(BF16)

    spec = pl.BlockSpec((None, tr, cols), lambda s, i, p: (s, i, 0))
    return pl.pallas_call(
        body, name=name,
        grid_spec=pltpu.PrefetchScalarGridSpec(
            num_scalar_prefetch=1, grid=(S, nb),
            in_specs=[pl.BlockSpec((None, tr, cols), lambda s, i, p: (s, p[0] * nb + i, 0)), spec],
            out_specs=spec),
        out_shape=jax.ShapeDtypeStruct((S, half, cols), BF16),
        compiler_params=_cparams(("parallel", "parallel"), 32),
    )(place, g, theirs)


def _chip_sum(p, r, g_prev, place, *, layer, shape, name):
    L, rows, cols = shape
    half = rows // 2
    tr = _row_tile(half, cols * 4, BF16_SUBLANES)
    nb = half // tr

    def body(place_ref, p_ref, r_ref, *rest):
        o_ref = rest[-1]
        mine = p_ref[...].astype(F32)
        peers = [r_ref[j].astype(F32) for j in range(3)]
        acc = None
        for s in range(N_CHIPS):
            term = jnp.where(place_ref[1] == s, mine,
                             jnp.where(place_ref[2] == s, peers[0],
                                       jnp.where(place_ref[3] == s, peers[1], peers[2])))
            acc = term if acc is None else acc + term
        o_ref[...] = acc

    in_specs = [pl.BlockSpec((None, tr, cols), lambda i, pr: (pr[1], i, 0)),
                pl.BlockSpec((3, tr, cols), lambda i, pr: (0, i, 0))]
    args = [place, p, r]
    aliases = {}
    if g_prev is not None:
        in_specs.append(ANY)
        args.append(g_prev)
        aliases = {3: 0}
    return pl.pallas_call(
        body, name=name,
        grid_spec=pltpu.PrefetchScalarGridSpec(
            num_scalar_prefetch=1, grid=(nb,), in_specs=in_specs,
            out_specs=pl.BlockSpec((None, tr, cols), lambda i, pr: (layer, pr[0] * nb + i, 0))),
        out_shape=jax.ShapeDtypeStruct(shape, F32), input_output_aliases=aliases,
        compiler_params=_cparams(("parallel",), 32),
    )(*args)


def _adamw_math(w, g, m, v):
    m2 = ADAM_B1 * m + (1.0 - ADAM_B1) * g
    v2 = ADAM_B2 * v + (1.0 - ADAM_B2) * (g * g)
    m_hat = m2 / (1.0 - ADAM_B1 ** ADAM_STEP)
    v_hat = v2 / (1.0 - ADAM_B2 ** ADAM_STEP)
    delta = -ADAM_LR * (m_hat / (jnp.sqrt(v_hat) + ADAM_EPS) + ADAM_WD * w)
    return delta, m2, v2


def _adamw(w, g, m, v, *, name):
    L, rows, cols = w.shape
    tr = _row_tile(rows, cols * 4)

    def body(w_ref, g_ref, m_ref, v_ref, d_ref, m2_ref, v2_ref):
        d, m2, v2 = _adamw_math(w_ref[...], g_ref[...], m_ref[...], v_ref[...])
        d_ref[...] = d
        m2_ref[...] = m2
        v2_ref[...] = v2

    spec = pl.BlockSpec((None, tr, cols), lambda l, i: (l, i, 0))
    shape = jax.ShapeDtypeStruct(w.shape, F32)
    return pl.pallas_call(
        body, name=name, grid=(L, rows // tr), in_specs=[spec] * 4, out_specs=[spec] * 3,
        out_shape=[shape] * 3,
        compiler_params=_cparams(("parallel", "parallel"), 48),
    )(w, g, m, v)


ANY = pl.BlockSpec(memory_space=pl.ANY)


def _place():
    x, y, c = lax.axis_index("x"), lax.axis_index("y"), lax.axis_index("c")
    peers = [(1 - x, y), (x, 1 - y), (1 - x, 1 - y)]
    return x, y, c, 2 * x + y, (x, y, 1 - c), peers


def _half(rows, which):
    return pl.ds(which * (rows // 2), rows // 2)


def _remote(src, dst, send_sem, recv_sem, device):
    return pltpu.make_async_remote_copy(src_ref=src, dst_ref=dst, send_sem=send_sem, recv_sem=recv_sem,
                                        device_id=device, device_id_type=MESH)


def _gather_weights(slots):
    nw = len(slots)
    items = [(wi, l) for wi, w in enumerate(slots) for l in range(w.shape[0])]
    n = len(items)

    def body(*refs):
        ws, wg = refs[:nw], refs[nw:2 * nw]
        isend, irecv, dsend, drecv = refs[2 * nw:]
        x, y, c, k, sib, peers = _place()
        sends = []
        for t, (wi, l) in enumerate(items):
            rows = slots[wi].shape[2]
            for j, (px, py) in enumerate(peers):
                rc = _remote(ws[wi].at[l, k, _half(rows, c)], wg[wi].at[l, k, _half(rows, c)],
                             isend.at[t, j], irecv.at[t, j], (px, py, c))
                rc.start()
                sends.append(rc)
        for t, (wi, l) in enumerate(items):
            rows = slots[wi].shape[2]
            for j, (px, py) in enumerate(peers):
                landed = wg[wi].at[l, 2 * px + py, _half(rows, c)]
                _remote(landed, landed, isend.at[t, j], irecv.at[t, j], (px, py, c)).wait_recv()
                fw = _remote(landed, landed, dsend.at[t, j], drecv.at[t, j], sib)
                fw.start()
                sends.append(fw)
        for t, (wi, l) in enumerate(items):
            rows = slots[wi].shape[2]
            for j, (px, py) in enumerate(peers):
                other = wg[wi].at[l, 2 * px + py, _half(rows, 1 - c)]
                _remote(other, other, dsend.at[t, j], drecv.at[t, j], sib).wait_recv()
        for rc in sends:
            rc.wait_send()

    return pl.pallas_call(
        body, name="gather_weights", in_specs=[ANY] * nw, out_specs=[ANY] * nw,
        out_shape=[jax.ShapeDtypeStruct(w.shape, BF16) for w in slots],
        input_output_aliases={i: i for i in range(nw)},
        scratch_shapes=[pltpu.SemaphoreType.DMA((n, 3))] * 4,
    )(*slots)


def _reduce_pair_exchange(grads):
    n = len(grads)

    def body(*refs):
        gs, theirs = refs[:n], refs[n:2 * n]
        send, recv = refs[2 * n:]
        x, y, c, k, sib, peers = _place()
        copies = []
        for t in range(n):
            rows = grads[t].shape[1]
            rc = _remote(gs[t].at[:, _half(rows, 1 - c), :], theirs[t], send.at[t], recv.at[t], sib)
            rc.start()
            copies.append(rc)
        for rc in copies:
            rc.wait()

    return pl.pallas_call(
        body, name="reduce_pair_exchange", in_specs=[ANY] * n, out_specs=[ANY] * n,
        out_shape=[jax.ShapeDtypeStruct((g.shape[0], g.shape[1] // 2, g.shape[2]), F32) for g in grads],
        scratch_shapes=[pltpu.SemaphoreType.DMA((n,))] * 2,
    )(*grads)


def _reduce_chip_exchange(parts):
    n = len(parts)

    def body(*refs):
        ps, rb = refs[:n], refs[n:2 * n]
        send, recv = refs[2 * n:]
        x, y, c, k, sib, peers = _place()
        sends = []
        for t in range(n):
            for j, (px, py) in enumerate(peers):
                rc = _remote(ps[t].at[2 * px + py], rb[t].at[j], send.at[t, j], recv.at[t, j], (px, py, c))
                rc.start()
                sends.append(rc)
        for rc in sends:
            rc.wait()

    return pl.pallas_call(
        body, name="reduce_chip_exchange", in_specs=[ANY] * n, out_specs=[ANY] * n,
        out_shape=[jax.ShapeDtypeStruct((3,) + p.shape[1:], p.dtype) for p in parts],
        scratch_shapes=[pltpu.SemaphoreType.DMA((n, 3)), pltpu.SemaphoreType.DMA((n, 3))],
    )(*parts)


def _reduce_pair_share(grads):
    nw = len(grads)
    items = [(wi, l) for wi, g in enumerate(grads) for l in range(g.shape[0])]
    n = len(items)

    def body(*refs):
        gin, gout = refs[:nw], refs[nw:2 * nw]
        send, recv = refs[2 * nw:]
        x, y, c, k, sib, peers = _place()
        copies = []
        for t, (wi, l) in enumerate(items):
            rows = grads[wi].shape[1]
            rc = _remote(gin[wi].at[l, _half(rows, c)], gout[wi].at[l, _half(rows, c)],
                         send.at[t], recv.at[t], sib)
            rc.start()
            copies.append(rc)
        for t, (wi, l) in enumerate(items):
            rows = grads[wi].shape[1]
            other = gout[wi].at[l, _half(rows, 1 - c)]
            _remote(other, other, send.at[t], recv.at[t], sib).wait_recv()
        for rc in copies:
            rc.wait_send()

    return pl.pallas_call(
        body, name="reduce_pair_share", in_specs=[ANY] * nw, out_specs=[ANY] * nw,
        out_shape=[jax.ShapeDtypeStruct(g.shape, F32) for g in grads],
        input_output_aliases={i: i for i in range(nw)},
        scratch_shapes=[pltpu.SemaphoreType.DMA((n,))] * 2,
    )(*grads)


def _exchange_packs(pack, *, reduce, name):
    R = pack.shape[0]
    ndev = 2 * N_CHIPS

    def body(p_ref, o_ref, *scratch):
        if reduce:
            buf, send, recv = scratch
        else:
            buf = o_ref
            send, recv = scratch
        x, y, c = lax.axis_index("x"), lax.axis_index("y"), lax.axis_index("c")
        me = 4 * x + 2 * y + c
        buf[me] = p_ref[...]
        sends = []
        for q in range(1, ndev):
            qx, qy, qc = (q >> 2) & 1, (q >> 1) & 1, q & 1
            peer = (x ^ qx, y ^ qy, c ^ qc)
            rc = _remote(p_ref, buf.at[me], send.at[q - 1], recv.at[q - 1], peer)
            rc.start()
            sends.append(rc)
        for q in range(1, ndev):
            qx, qy, qc = (q >> 2) & 1, (q >> 1) & 1, q & 1
            slot = buf.at[4 * (x ^ qx) + 2 * (y ^ qy) + (c ^ qc)]
            _remote(slot, slot, send.at[q - 1], recv.at[q - 1], (x ^ qx, y ^ qy, c ^ qc)).wait_recv()
        for rc in sends:
            rc.wait_send()
        if reduce:
            acc = buf[0]
            for d in range(1, ndev):
                acc = acc + buf[d]
            o_ref[...] = acc

    vm = pl.BlockSpec(memory_space=pltpu.VMEM)
    sems = [pltpu.SemaphoreType.DMA((ndev - 1,)), pltpu.SemaphoreType.DMA((ndev - 1,))]
    if reduce:
        out_shape = jax.ShapeDtypeStruct((R, LANES), F32)
        scratch = [pltpu.VMEM((ndev, R, LANES), F32)] + sems
    else:
        out_shape = jax.ShapeDtypeStruct((ndev, R, LANES), F32)
        scratch = sems
    return pl.pallas_call(
        body, name=name, in_specs=[vm], out_specs=vm, out_shape=out_shape, scratch_shapes=scratch,
        compiler_params=pltpu.CompilerParams(vmem_limit_bytes=32 * 1024 * 1024),
    )(pack)


PACK_UNIT = SUBLANES * LANES


def _pack(arrays):
    flat, sizes = [], []
    for a in arrays:
        pieces = a if isinstance(a, (list, tuple)) else [a]
        v = jnp.concatenate([p.reshape(-1) for p in pieces]) if len(pieces) > 1 else pieces[0].reshape(-1)
        size = v.shape[0]
        padded = -(-size // PACK_UNIT) * PACK_UNIT
        flat.append(jnp.pad(v, (0, padded - size)))
        sizes.append((size, padded))
    return jnp.concatenate(flat).reshape(-1, LANES), sizes


def _unpack(pack, sizes, shapes):
    v = pack.reshape(-1)
    out, off = [], 0
    for (size, padded), shape in zip(sizes, shapes):
        out.append(v[off:off + size].reshape(shape))
        off += padded
    return out


BIG = ("ab_w_in", "ab_w_out", "c_w_in", "c_w_out", "f_w_up", "f_w_down")
COL_SHARDED = ("ab_w_in", "c_w_in", "f_w_up")
SMALL_REPLICATED = ("norm_mix", "norm_ffn", "norm_final", "a_ln_g", "a_ln_b", "a_w_s", "a_b_s",
                    "b_conv_b", "b_ln_g", "b_ln_b")
SMALL_SHARDED = ("b_conv_w", "c_conv_w", "f_conv_w")
SMALL = SMALL_REPLICATED + SMALL_SHARDED
ALL_WEIGHTS = ("norm_mix", "norm_ffn", "norm_final", "ab_w_in", "a_ln_g", "a_ln_b", "a_w_s", "a_b_s",
               "b_conv_w", "b_conv_b", "b_ln_g", "b_ln_b", "ab_w_out", "c_w_in", "c_conv_w", "c_w_out",
               "f_w_up", "f_conv_w", "f_w_down")


def _step(x, tgt, w, m, v):
    chip = 2 * lax.axis_index("x") + lax.axis_index("y")
    place = _place_scalars()

    gathered = _gather_weights([_cast_into_slot(w[n], place, name=f"cast_{n}") for n in BIG])
    big = {}
    for n, g in zip(BIG, gathered):
        L, S, rows, cols = g.shape
        big[n] = g if n in COL_SHARDED else g.reshape(L, S * rows, cols)
    conv_pack, conv_sizes = _pack([w[n] for n in SMALL_SHARDED])
    conv_all = _exchange_packs(conv_pack, reduce=False, name="gather_conv_weights")
    conv_shapes = [w[n].shape for n in SMALL_SHARDED]
    per_chip = [_unpack(conv_all[2 * s], conv_sizes, conv_shapes) for s in range(N_CHIPS)]
    small = {n: w[n] for n in SMALL_REPLICATED}
    for idx, n in enumerate(SMALL_SHARDED):
        small[n] = jnp.concatenate([per_chip[s][idx] for s in range(N_CHIPS)], axis=-1)

    loss, dx, sg, bg = _local_step(x, tgt, small, big)

    items, full = [], []
    for wi, n in enumerate(BIG):
        for l in range(w[n].shape[0]):
            g = bg[n][l] if isinstance(bg[n], list) else bg[n]
            rows, cols = w[n].shape[1:]
            items.append((wi, l))
            full.append(g.reshape(N_CHIPS, rows, cols))
    theirs = _reduce_pair_exchange(full)
    parts = [_pair_sum(g, o, place, name=f"pair_sum{t}") for t, (g, o) in enumerate(zip(full, theirs))]
    landed = _reduce_chip_exchange(parts)
    summed = [None] * len(BIG)
    for t, (wi, l) in enumerate(items):
        summed[wi] = _chip_sum(parts[t], landed[t], summed[wi], place, layer=l, shape=w[BIG[wi]].shape,
                               name=f"chip_sum{t}")
    grads_big = _reduce_pair_share(summed)

    g_pack, g_sizes = _pack([sg[n] for n in SMALL])
    g_sum = _exchange_packs(g_pack, reduce=True, name="allreduce_small_grads")
    full_shapes = [small[n].shape for n in SMALL]
    g_small = dict(zip(SMALL, _unpack(g_sum, g_sizes, full_shapes)))
    for n in SMALL_SHARDED:
        width = w[n].shape[-1]
        g_small[n] = lax.dynamic_slice_in_dim(g_small[n], chip * width, width, axis=g_small[n].ndim - 1)

    grad, delta, new_m, new_v = {}, {}, {}, {}
    for n, g in zip(BIG, grads_big):
        grad[n] = g
        delta[n], new_m[n], new_v[n] = _adamw(w[n], g, m[n], v[n], name=f"adamw_{n}")
    shapes = [w[n].shape for n in SMALL]
    wp, sizes = _pack([w[n] for n in SMALL])
    gp, _ = _pack([g_small[n] for n in SMALL])
    mp, _ = _pack([m[n] for n in SMALL])
    vp, _ = _pack([v[n] for n in SMALL])
    R = wp.shape[0]
    dp, m2p, v2p = _adamw(wp.reshape(1, R, LANES), gp.reshape(1, R, LANES), mp.reshape(1, R, LANES),
                          vp.reshape(1, R, LANES), name="adamw_small")
    for n, d_, m_, v_ in zip(SMALL, _unpack(dp, sizes, shapes), _unpack(m2p, sizes, shapes),
                             _unpack(v2p, sizes, shapes)):
        grad[n] = g_small[n]
        delta[n], new_m[n], new_v[n] = d_, m_, v_
    return loss, dx, grad, delta, new_m, new_v


def kernel(x, norm_mix, norm_ffn, norm_final, ab_w_in, a_ln_g, a_ln_b, a_w_s, a_b_s, b_conv_w, b_conv_b, b_ln_g, b_ln_b, ab_w_out, c_w_in, c_conv_w, c_w_out, f_w_up, f_conv_w, f_w_down, loss_target, m_norm_mix, m_norm_ffn, m_norm_final, m_ab_w_in, m_a_ln_g, m_a_ln_b, m_a_w_s, m_a_b_s, m_b_conv_w, m_b_conv_b, m_b_ln_g, m_b_ln_b, m_ab_w_out, m_c_w_in, m_c_conv_w, m_c_w_out, m_f_w_up, m_f_conv_w, m_f_w_down, v_norm_mix, v_norm_ffn, v_norm_final, v_ab_w_in, v_a_ln_g, v_a_ln_b, v_a_w_s, v_a_b_s, v_b_conv_w, v_b_conv_b, v_b_ln_g, v_b_ln_b, v_ab_w_out, v_c_w_in, v_c_conv_w, v_c_w_out, v_f_w_up, v_f_conv_w, v_f_w_down):
    given = dict(locals())
    w = {n: given[n] for n in ALL_WEIGHTS}
    m = {n: given["m_" + n] for n in ALL_WEIGHTS}
    v = {n: given["v_" + n] for n in ALL_WEIGHTS}
    T = x.shape[1]
    loss, dx, grad, delta, new_m, new_v = _step(x.reshape(T, D_MODEL), loss_target.reshape(T, D_MODEL), w, m, v)
    loss = lax.psum(loss[0, 0], ("x", "y", "c"))
    out = [loss, dx.reshape(x.shape)]
    for d in (grad, delta, new_m, new_v):
        out += [d[n] for n in ALL_WEIGHTS]
    return tuple(out)
```

```python
import functools
import math

import jax
import jax.numpy as jnp
from jax import lax
from jax.experimental import pallas as pl
from jax.experimental.pallas import tpu as pltpu

F32 = jnp.float32
BF16 = jnp.bfloat16

EPS = 1e-6
D_MODEL = 1024
CHUNK = 128
HEAD_DIM = 128
A_HEADS = 4
D_A = 512
D_B = 512
B_CONV = 31
C_CONV = 3
D_FF = 2816
F_CONV = 3
N_CHIPS = 4

ADAM_LR = 0.001
ADAM_B1 = 0.9
ADAM_B2 = 0.999
ADAM_EPS = 1e-08
ADAM_WD = 0.01
ADAM_STEP = 10

SUBLANES = 8
LANES = 128
HALO_SHORT = 16
HALO_LONG = 32
VMEM_BYTES_MAX = 56 * 1024 * 1024

INV_SQRT2 = 1.0 / math.sqrt(2.0)
INV_SQRT_2PI = 1.0 / math.sqrt(2.0 * math.pi)

MESH = pl.DeviceIdType.MESH


def _cparams(sem, vmem_mb):
    return pltpu.CompilerParams(dimension_semantics=sem,
                                vmem_limit_bytes=min(vmem_mb * 1024 * 1024, VMEM_BYTES_MAX))


def _pick(total, pref):
    for c in (2048, 1024, 512, 256, 128):
        if c <= pref and total % c == 0:
            return c
    raise ValueError(f"no tile for {total}")


def _sigmoid(x):
    return jax.nn.sigmoid(x)


def _silu(x):
    return x * _sigmoid(x)


def _dsilu(x):
    s = _sigmoid(x)
    return s * (1.0 + x * (1.0 - s))


def _gelu(x):
    return 0.5 * x * (1.0 + lax.erf(x * INV_SQRT2))


def _dgelu(x):
    return 0.5 * (1.0 + lax.erf(x * INV_SQRT2)) + x * jnp.exp(-0.5 * x * x) * INV_SQRT_2PI


def _ln_stats(x):
    mu = jnp.mean(x, axis=-1, keepdims=True)
    xc = x - mu
    var = jnp.mean(xc * xc, axis=-1, keepdims=True)
    r = lax.rsqrt(var + EPS)
    return xc * r, r


def _ln_bwd(dy, xh, r, g):
    dxh = dy * g
    m1 = jnp.mean(dxh, axis=-1, keepdims=True)
    m2 = jnp.mean(dxh * xh, axis=-1, keepdims=True)
    return r * (dxh - m1 - xh * m2)


def _rowsum(x):
    return jnp.sum(x, axis=0, keepdims=True)


ANY = pl.BlockSpec(memory_space=pl.ANY)


def _place():
    x, y, c = lax.axis_index("x"), lax.axis_index("y"), lax.axis_index("c")
    peers = [(1 - x, y), (x, 1 - y), (1 - x, 1 - y)]
    return x, y, c, 2 * x + y, (x, y, 1 - c), peers


def _half(rows, which):
    return pl.ds(which * (rows // 2), rows // 2)


def _remote(src, dst, send_sem, recv_sem, device):
    return pltpu.make_async_remote_copy(src_ref=src, dst_ref=dst, send_sem=send_sem, recv_sem=recv_sem,
                                        device_id=device, device_id_type=MESH)


class _Job:
    def __init__(self, reads, writes, ncopies, copies):
        self.reads, self.writes, self.ncopies, self.copies = reads, writes, ncopies, copies


def _job_gather_ici(name, rows):
    def copies(src, dst, sem):
        x, y, c, k, sib, peers = _place()
        out = []
        for j, (px, py) in enumerate(peers):
            mine = src[name].at[0, k, _half(rows, c)]
            out.append((_remote(mine, dst[name].at[0, k, _half(rows, c)], sem(j, 0), sem(j, 1), (px, py, c)),
                        _remote(mine, dst[name].at[0, 2 * px + py, _half(rows, c)], sem(j, 0), sem(j, 1), (px, py, c))))
        return out
    return _Job([], [name], 3, copies)


def _job_gather_d2d(name, rows):
    def copies(src, dst, sem):
        x, y, c, k, sib, peers = _place()
        out = []
        for j, (px, py) in enumerate(peers):
            landed = src[name].at[0, 2 * px + py, _half(rows, c)]
            out.append((_remote(landed, dst[name].at[0, 2 * px + py, _half(rows, c)], sem(j, 0), sem(j, 1), sib),
                        _remote(landed, dst[name].at[0, 2 * px + py, _half(rows, 1 - c)], sem(j, 0), sem(j, 1), sib)))
        return out
    return _Job([], [name], 3, copies)


def _job_pair_exchange(gname, tname, rows):
    def copies(src, dst, sem):
        x, y, c, k, sib, peers = _place()
        cp = _remote(src[gname].at[:, _half(rows, 1 - c), :], dst[tname], sem(0, 0), sem(0, 1), sib)
        return [(cp, cp)]
    return _Job([gname], [tname], 1, copies)


def _job_chip_exchange(pname, lname, r0, nr):
    def copies(src, dst, sem):
        x, y, c, k, sib, peers = _place()
        out = []
        for j, (px, py) in enumerate(peers):
            cp = _remote(src[pname].at[2 * px + py, pl.ds(r0, nr)], dst[lname].at[j, pl.ds(r0, nr)],
                         sem(j, 0), sem(j, 1), (px, py, c))
            out.append((cp, cp))
        return out
    return _Job([pname], [lname], 3, copies)


def _job_pair_share(name, layer, rows):
    def copies(src, dst, sem):
        x, y, c, k, sib, peers = _place()
        mine = src[name].at[layer, _half(rows, c)]
        return [(_remote(mine, dst[name].at[layer, _half(rows, c)], sem(0, 0), sem(0, 1), sib),
                 _remote(mine, dst[name].at[layer, _half(rows, 1 - c)], sem(0, 0), sem(0, 1), sib))]
    return _Job([], [name], 1, copies)


class _Comm:
    def __init__(self, plan, jobs):
        self.plan, self.jobs = plan, jobs
        self.writes, self.reads = [], []
        for job in jobs:
            for n in job.writes:
                if n not in self.writes:
                    self.writes.append(n)
        for job in jobs:
            for n in job.reads:
                if n not in self.writes and n not in self.reads:
                    self.reads.append(n)
        self.ncopies = sum(job.ncopies for job in jobs)

    def descriptors(self, src, dst, sems, base):
        out = []
        for job in self.jobs:
            sem = lambda j, which, base=base: sems.at[base + j, which]
            out += job.copies(src, dst, sem)
            base += job.ncopies
        return out

    def start(self, src, dst, sems, base=0):
        for first, _ in self.descriptors(src, dst, sems, base):
            first.start()

    def finish(self, src, dst, sems, base=0):
        for _, landed in self.descriptors(src, dst, sems, base):
            landed.wait()


def _comm_operands(comm):
    bufs = comm.plan.bufs
    shapes = [jax.ShapeDtypeStruct(bufs[n].shape, bufs[n].dtype) for n in comm.writes]
    return [bufs[n] for n in comm.reads] + [bufs[n] for n in comm.writes], shapes


def _pallas(comm, body, *, name, grid, in_specs, out_specs, out_shape, compiler_params, scratch_shapes=()):
    if comm is None:
        return pl.pallas_call(body, name=name, grid=grid, in_specs=in_specs, out_specs=out_specs,
                              out_shape=out_shape, scratch_shapes=list(scratch_shapes),
                              compiler_params=compiler_params)
    single = not isinstance(out_shape, (list, tuple))
    base_specs = [out_specs] if single else list(out_specs)
    base_shape = [out_shape] if single else list(out_shape)
    nb, nr, nw, nbo, nsc = len(in_specs), len(comm.reads), len(comm.writes), len(base_specs), len(scratch_shapes)

    def wrapped(*refs):
        base_in, rd, wr_in = refs[:nb], refs[nb:nb + nr], refs[nb + nr:nb + nr + nw]
        o0 = nb + nr + nw
        base_out, wr_out = refs[o0:o0 + nbo], refs[o0 + nbo:o0 + nbo + nw]
        scratch, sems = refs[o0 + nbo + nw:o0 + nbo + nw + nsc], refs[-1]
        src = dict(zip(comm.reads, rd))
        src.update(zip(comm.writes, wr_in))
        dst = dict(zip(comm.writes, wr_out))
        first = functools.reduce(jnp.logical_and, [pl.program_id(a) == 0 for a in range(len(grid))])
        last = functools.reduce(jnp.logical_and,
                                [pl.program_id(a) == pl.num_programs(a) - 1 for a in range(len(grid))])

        @pl.when(first)
        def _():
            comm.start(src, dst, sems)
        body(*base_in, *base_out, *scratch)

        @pl.when(last)
        def _():
            comm.finish(src, dst, sems)

    operands, shapes = _comm_operands(comm)
    call = pl.pallas_call(
        wrapped, name=name, grid=grid, in_specs=list(in_specs) + [ANY] * (nr + nw),
        out_specs=base_specs + [ANY] * nw, out_shape=base_shape + shapes,
        input_output_aliases={nb + nr + q: nbo + q for q in range(nw)},
        scratch_shapes=list(scratch_shapes) + [pltpu.SemaphoreType.DMA((comm.ncopies, 2))],
        compiler_params=compiler_params)

    def run(*args):
        outs = call(*args, *operands)
        for q, n in enumerate(comm.writes):
            comm.plan.bufs[n] = outs[nbo + q]
        return outs[0] if single else list(outs[:nbo])

    return run


def _comm_only(plan, phases, *, name):
    comms = [_Comm(plan, jobs) for jobs in phases]
    both = _Comm(plan, [job for jobs in phases for job in jobs])
    nr, nw = len(both.reads), len(both.writes)

    def body(*refs):
        rd, wr_in, wr_out, sems = refs[:nr], refs[nr:nr + nw], refs[nr + nw:nr + 2 * nw], refs[-1]
        src = dict(zip(both.reads, rd))
        src.update(zip(both.writes, wr_in))
        dst = dict(zip(both.writes, wr_out))
        base = 0
        for comm in comms:
            comm.start(src, dst, sems, base)
            comm.finish(src, dst, sems, base)
            base += comm.ncopies

    operands, shapes = _comm_operands(both)
    outs = pl.pallas_call(
        body, name=name, in_specs=[ANY] * (nr + nw), out_specs=[ANY] * nw, out_shape=shapes,
        input_output_aliases={nr + q: q for q in range(nw)},
        scratch_shapes=[pltpu.SemaphoreType.DMA((both.ncopies, 2))],
    )(*operands)
    for q, n in enumerate(both.writes):
        plan.bufs[n] = outs[q]


def _mm_nn(a, w, *, layer, tm, tn, residual=None, out_dtype=F32, name, comm=None):
    T, K = a.shape
    if w.ndim == 4:
        _, S, _, n4 = w.shape
        N = S * n4
        bps = n4 // tn
        w_spec = pl.BlockSpec((None, None, K, tn), lambda j, i: (layer, j // bps, 0, j % bps))
    else:
        N = w.shape[2]
        w_spec = pl.BlockSpec((None, K, tn), lambda j, i: (layer, 0, j))
    in_specs = [pl.BlockSpec((tm, K), lambda j, i: (i, 0)), w_spec]
    args = [a, w]
    if residual is not None:
        in_specs.append(pl.BlockSpec((tm, tn), lambda j, i: (i, j)))
        args.append(residual)

    def body(*refs):
        a_ref, w_ref, o_ref = refs[0], refs[1], refs[-1]
        acc = jnp.dot(a_ref[...].astype(BF16), w_ref[...], preferred_element_type=F32)
        if residual is not None:
            acc = refs[2][...] + acc
        o_ref[...] = acc.astype(out_dtype)

    return _pallas(
        comm, body, name=name, grid=(N // tn, T // tm), in_specs=in_specs,
        out_specs=pl.BlockSpec((tm, tn), lambda j, i: (i, j)),
        out_shape=jax.ShapeDtypeStruct((T, N), out_dtype),
        compiler_params=_cparams(("parallel", "parallel"), 48),
    )(*args)


def _mm_nt(dy, w, *, layer, tm, tn, name, out_dtype=F32, comm=None):
    T = dy.shape[0]
    nt_dims = (((1,), (1,)), ((), ()))
    if w.ndim == 4:
        _, S, K, n4 = w.shape

        def body(dy_ref, w_ref, o_ref):
            @pl.when(pl.program_id(1) == 0)
            def _():
                o_ref[...] = jnp.zeros_like(o_ref)
            o_ref[...] += lax.dot_general(dy_ref[...].astype(BF16), w_ref[...], nt_dims,
                                          preferred_element_type=F32)

        return _pallas(
            comm, body, name=name, grid=(T // tm, S),
            in_specs=[pl.BlockSpec((tm, n4), lambda i, s: (i, s)),
                      pl.BlockSpec((None, None, K, n4), lambda i, s: (layer, s, 0, 0))],
            out_specs=pl.BlockSpec((tm, K), lambda i, s: (i, 0)),
            out_shape=jax.ShapeDtypeStruct((T, K), F32),
            compiler_params=_cparams(("parallel", "arbitrary"), 48),
        )(dy, w)
    _, R, N = w.shape

    def body2(dy_ref, w_ref, o_ref):
        o_ref[...] = lax.dot_general(dy_ref[...].astype(BF16), w_ref[...], nt_dims,
                                     preferred_element_type=F32).astype(out_dtype)

    return _pallas(
        comm, body2, name=name, grid=(R // tn, T // tm),
        in_specs=[pl.BlockSpec((tm, N), lambda j, i: (i, 0)),
                  pl.BlockSpec((None, tn, N), lambda j, i: (layer, j, 0))],
        out_specs=pl.BlockSpec((tm, tn), lambda j, i: (i, j)),
        out_shape=jax.ShapeDtypeStruct((T, R), out_dtype),
        compiler_params=_cparams(("parallel", "parallel"), 48),
    )(dy, w)


def _mm_tn(a, dy, *, shards, tk, tn, tt, name, comm=None):
    T, K = a.shape
    N = dy.shape[1]
    tn_dims = (((0,), (0,)), ((), ()))

    def body(a_ref, dy_ref, o_ref):
        @pl.when(pl.program_id(2) == 0)
        def _():
            o_ref[...] = jnp.zeros_like(o_ref)
        o_ref[...] += lax.dot_general(a_ref[...].astype(BF16), dy_ref[...].astype(BF16), tn_dims,
                                      preferred_element_type=F32)

    if shards is None:
        out_spec = pl.BlockSpec((tk, tn), lambda k, n, t: (k, n))
        out_shape = jax.ShapeDtypeStruct((K, N), F32)
    else:
        n4 = N // shards
        bps = n4 // tn
        out_spec = pl.BlockSpec((None, tk, tn), lambda k, n, t: (n // bps, k, n % bps))
        out_shape = jax.ShapeDtypeStruct((shards, K, n4), F32)
    return _pallas(
        comm, body, name=name, grid=(K // tk, N // tn, T // tt),
        in_specs=[pl.BlockSpec((tt, tk), lambda k, n, t: (t, k)),
                  pl.BlockSpec((tt, tn), lambda k, n, t: (t, n))],
        out_specs=out_spec, out_shape=out_shape,
        compiler_params=_cparams(("parallel", "parallel", "arbitrary"), 48),
    )(a, dy)


def _rmsnorm_fwd(x, g, *, layer, tm, name, comm=None):
    T, D = x.shape

    def body(x_ref, g_ref, h_ref):
        xf = x_ref[...]
        r = lax.rsqrt(jnp.mean(xf * xf, axis=-1, keepdims=True) + EPS)
        h_ref[...] = (xf * r * g_ref[...]).astype(BF16)

    return _pallas(
        comm, body, name=name, grid=(T // tm,),
        in_specs=[pl.BlockSpec((tm, D), lambda i: (i, 0)),
                  pl.BlockSpec((None, 1, D), lambda i: (layer, 0, 0))],
        out_specs=pl.BlockSpec((tm, D), lambda i: (i, 0)),
        out_shape=jax.ShapeDtypeStruct((T, D), BF16),
        compiler_params=_cparams(("parallel",), 32),
    )(x, g)


def _rmsnorm_bwd(x, g, dh, dres, *, layer, tm, name, comm=None):
    T, D = x.shape

    def body(x_ref, g_ref, dh_ref, dres_ref, dx_ref, dg_ref):
        @pl.when(pl.program_id(0) == 0)
        def _():
            dg_ref[...] = jnp.zeros_like(dg_ref)
        xf = x_ref[...]
        r = lax.rsqrt(jnp.mean(xf * xf, axis=-1, keepdims=True) + EPS)
        xh = xf * r
        dh = dh_ref[...]
        dg_ref[...] += _rowsum(dh * xh)
        dxh = dh * g_ref[...]
        dx_ref[...] = dres_ref[...] + r * (dxh - xh * jnp.mean(dxh * xh, axis=-1, keepdims=True))

    return _pallas(
        comm, body, name=name, grid=(T // tm,),
        in_specs=[pl.BlockSpec((tm, D), lambda i: (i, 0)),
                  pl.BlockSpec((None, 1, D), lambda i: (layer, 0, 0)),
                  pl.BlockSpec((tm, D), lambda i: (i, 0)),
                  pl.BlockSpec((tm, D), lambda i: (i, 0))],
        out_specs=[pl.BlockSpec((tm, D), lambda i: (i, 0)),
                   pl.BlockSpec((1, D), lambda i: (0, 0))],
        out_shape=[jax.ShapeDtypeStruct((T, D), F32), jax.ShapeDtypeStruct((1, D), F32)],
        compiler_params=_cparams(("arbitrary",), 40),
    )(x, g, dh, dres)


def _loss_head(x, tgt, g, *, tm, name, comm=None):
    T, D = x.shape

    def body(x_ref, t_ref, g_ref, loss_ref, dx_ref, dg_ref):
        @pl.when(pl.program_id(0) == 0)
        def _():
            dg_ref[...] = jnp.zeros_like(dg_ref)
            loss_ref[...] = jnp.zeros_like(loss_ref)
        xf = x_ref[...]
        gg = g_ref[...]
        r = lax.rsqrt(jnp.mean(xf * xf, axis=-1, keepdims=True) + EPS)
        xh = xf * r
        err = xh * gg - t_ref[...]
        row = jnp.mean(err * err, axis=-1, keepdims=True)
        loss_ref[...] += 0.5 * jnp.sum(row, axis=0, keepdims=True)
        dy = err * (1.0 / D)
        dg_ref[...] += _rowsum(dy * xh)
        dxh = dy * gg
        dx_ref[...] = r * (dxh - xh * jnp.mean(dxh * xh, axis=-1, keepdims=True))

    return _pallas(
        comm, body, name=name, grid=(T // tm,),
        in_specs=[pl.BlockSpec((tm, D), lambda i: (i, 0)),
                  pl.BlockSpec((tm, D), lambda i: (i, 0)),
                  pl.BlockSpec((1, D), lambda i: (0, 0))],
        out_specs=[pl.BlockSpec((1, 1), lambda i: (0, 0)),
                   pl.BlockSpec((tm, D), lambda i: (i, 0)),
                   pl.BlockSpec((1, D), lambda i: (0, 0))],
        out_shape=[jax.ShapeDtypeStruct((1, 1), F32), jax.ShapeDtypeStruct((T, D), F32),
                   jax.ShapeDtypeStruct((1, D), F32)],
        compiler_params=_cparams(("arbitrary",), 40),
    )(x, tgt, g)


CONV_ROWS = 64
CONV_COLS = 256


def _halo_prev_index(tm, halo):
    per = tm // halo
    return lambda i: jnp.maximum(i * per - 1, 0)


def _halo_next_index(tm, halo, total):
    per = tm // halo
    last = total // halo - 1
    return lambda i: jnp.minimum((i + 1) * per, last)


def _causal_mask():
    t = lax.broadcasted_iota(jnp.int32, (CHUNK, CHUNK), 0)
    s = lax.broadcasted_iota(jnp.int32, (CHUNK, CHUNK), 1)
    return s <= t


def _mixer_ab_fwd(z, a_ln_g, a_ln_b, w_s, b_s, conv_w, conv_b, b_ln_g, b_ln_b, *, tm, name, comm=None):
    T = z.shape[0]
    nchunk = tm // CHUNK
    halo = HALO_LONG

    def body(za_ref, zb_ref, zh_ref, alg_ref, alb_ref, ws_ref, bs_ref, cw_ref, cbias_ref,
             blg_ref, blb_ref, y_ref, cb_ref, ext_ref):
        i = pl.program_id(0)
        gu = _gelu(za_ref[:, :D_A].astype(F32))
        gv = _gelu(za_ref[:, D_A:].astype(F32))
        xh, _ = _ln_stats(gv)
        lv = (xh * alg_ref[...] + alb_ref[...]).astype(BF16)
        mask = _causal_mask()
        for h in range(A_HEADS):
            wm = jnp.where(mask, ws_ref[h], 0.0).astype(BF16)
            cols = slice(h * HEAD_DIM, (h + 1) * HEAD_DIM)
            for c in range(nchunk):
                rows = slice(c * CHUNK, (c + 1) * CHUNK)
                mixed = jnp.dot(wm, lv[rows, cols], preferred_element_type=F32) + bs_ref[h]
                y_ref[rows, cols] = (gu[rows, cols] * mixed).astype(BF16)
        ext_ref[halo:halo + tm, :] = zb_ref[:, :D_B].astype(F32) * _sigmoid(zb_ref[:, D_B:].astype(F32))
        prev = zh_ref[:, :D_B].astype(F32) * _sigmoid(zh_ref[:, D_B:].astype(F32))
        ext_ref[0:halo, :] = jnp.where(i > 0, prev, 0.0)
        for rb in range(tm // CONV_ROWS):
            for cb in range(D_B // CONV_COLS):
                cs = slice(cb * CONV_COLS, (cb + 1) * CONV_COLS)
                acc = jnp.zeros((CONV_ROWS, CONV_COLS), F32)
                for k in range(B_CONV):
                    off = rb * CONV_ROWS + halo - (B_CONV - 1) + k
                    acc = acc + cw_ref[k:k + 1, cs] * ext_ref[off:off + CONV_ROWS, cs]
                cb_ref[rb * CONV_ROWS:(rb + 1) * CONV_ROWS, cs] = acc + cbias_ref[:, cs]
        xhb, _ = _ln_stats(cb_ref[...])
        y_ref[:, D_A:] = _silu(xhb * blg_ref[...] + blb_ref[...]).astype(BF16)

    row = lambda i: (i, 0)
    par = lambda i: (0, 0)
    return _pallas(
        comm, body, name=name, grid=(T // tm,),
        in_specs=[pl.BlockSpec((tm, 2 * D_A), lambda i: (i, 0)),
                  pl.BlockSpec((tm, 2 * D_B), lambda i: (i, 1)),
                  pl.BlockSpec((halo, 2 * D_B), lambda i: (_halo_prev_index(tm, halo)(i), 1)),
                  pl.BlockSpec((1, D_A), par), pl.BlockSpec((1, D_A), par),
                  pl.BlockSpec((A_HEADS, CHUNK, CHUNK), lambda i: (0, 0, 0)),
                  pl.BlockSpec((A_HEADS, CHUNK, 1), lambda i: (0, 0, 0)),
                  pl.BlockSpec((B_CONV, D_B), par), pl.BlockSpec((1, D_B), par),
                  pl.BlockSpec((1, D_B), par), pl.BlockSpec((1, D_B), par)],
        out_specs=[pl.BlockSpec((tm, D_A + D_B), row), pl.BlockSpec((tm, D_B), row)],
        out_shape=[jax.ShapeDtypeStruct((T, D_A + D_B), BF16), jax.ShapeDtypeStruct((T, D_B), F32)],
        scratch_shapes=[pltpu.VMEM((halo + tm, D_B), F32)],
        compiler_params=_cparams(("parallel",), 40),
    )(z, z, z, a_ln_g, a_ln_b, w_s, b_s, conv_w, conv_b, b_ln_g, b_ln_b)


def _mixer_ab_bwd_pre(z, cb, dy, a_ln_g, a_ln_b, w_s, b_s, b_ln_g, b_ln_b, *, tm, name, comm=None):
    T = z.shape[0]
    nchunk = tm // CHUNK
    tn_dims = (((0,), (0,)), ((), ()))
    nt_dims = (((1,), (1,)), ((), ()))

    def body(za_ref, cb_ref, dy_ref, alg_ref, alb_ref, ws_ref, bs_ref, blg_ref, blb_ref,
             dza_ref, dcb_ref, dalg_ref, dalb_ref, dws_ref, dbs_ref, dblg_ref, dblb_ref,
             dlv_ref):
        @pl.when(pl.program_id(0) == 0)
        def _():
            for ref in (dalg_ref, dalb_ref, dws_ref, dbs_ref, dblg_ref, dblb_ref):
                ref[...] = jnp.zeros_like(ref)
        ua = za_ref[:, :D_A].astype(F32)
        va = za_ref[:, D_A:].astype(F32)
        gu = _gelu(ua)
        gv = _gelu(va)
        xh, r = _ln_stats(gv)
        alg = alg_ref[...]
        lv = (xh * alg + alb_ref[...]).astype(BF16)
        dya = dy_ref[:, :D_A].astype(F32)
        mask = _causal_mask()
        for h in range(A_HEADS):
            wm = jnp.where(mask, ws_ref[h], 0.0).astype(BF16)
            cols = slice(h * HEAD_DIM, (h + 1) * HEAD_DIM)
            dwm = jnp.zeros((CHUNK, CHUNK), F32)
            dbs = jnp.zeros((CHUNK, 1), F32)
            for c in range(nchunk):
                rows = slice(c * CHUNK, (c + 1) * CHUNK)
                lvb = lv[rows, cols]
                mixed = jnp.dot(wm, lvb, preferred_element_type=F32) + bs_ref[h]
                dyb = dya[rows, cols]
                dza_ref[rows, cols] = (dyb * mixed * _dgelu(ua[rows, cols])).astype(BF16)
                dmixed = dyb * gu[rows, cols]
                dmb = dmixed.astype(BF16)
                dlv_ref[rows, cols] = lax.dot_general(wm, dmb, tn_dims, preferred_element_type=F32)
                dwm = dwm + lax.dot_general(dmb, lvb, nt_dims, preferred_element_type=F32)
                dbs = dbs + jnp.sum(dmixed, axis=1, keepdims=True)
            dws_ref[h] += jnp.where(mask, dwm, 0.0)
            dbs_ref[h] += dbs
        dlv = dlv_ref[...]
        dalg_ref[...] += _rowsum(dlv * xh)
        dalb_ref[...] += _rowsum(dlv)
        dgv = _ln_bwd(dlv, xh, r, alg)
        dza_ref[:, D_A:] = (dgv * _dgelu(va)).astype(BF16)
        xhb, rb = _ln_stats(cb_ref[...])
        blg = blg_ref[...]
        lb = xhb * blg + blb_ref[...]
        dlb = dy_ref[:, D_A:].astype(F32) * _dsilu(lb)
        dblg_ref[...] += _rowsum(dlb * xhb)
        dblb_ref[...] += _rowsum(dlb)
        dcb_ref[...] = _ln_bwd(dlb, xhb, rb, blg)

    row = lambda i: (i, 0)
    par = lambda i: (0, 0)
    par3 = lambda i: (0, 0, 0)
    return _pallas(
        comm, body, name=name, grid=(T // tm,),
        in_specs=[pl.BlockSpec((tm, 2 * D_A), row), pl.BlockSpec((tm, D_B), row),
                  pl.BlockSpec((tm, D_A + D_B), row),
                  pl.BlockSpec((1, D_A), par), pl.BlockSpec((1, D_A), par),
                  pl.BlockSpec((A_HEADS, CHUNK, CHUNK), par3),
                  pl.BlockSpec((A_HEADS, CHUNK, 1), par3),
                  pl.BlockSpec((1, D_B), par), pl.BlockSpec((1, D_B), par)],
        out_specs=[pl.BlockSpec((tm, 2 * D_A), row), pl.BlockSpec((tm, D_B), row),
                   pl.BlockSpec((1, D_A), par), pl.BlockSpec((1, D_A), par),
                   pl.BlockSpec((A_HEADS, CHUNK, CHUNK), par3),
                   pl.BlockSpec((A_HEADS, CHUNK, 1), par3),
                   pl.BlockSpec((1, D_B), par), pl.BlockSpec((1, D_B), par)],
        out_shape=[jax.ShapeDtypeStruct((T, 2 * D_A), BF16), jax.ShapeDtypeStruct((T, D_B), F32),
                   jax.ShapeDtypeStruct((1, D_A), F32), jax.ShapeDtypeStruct((1, D_A), F32),
                   jax.ShapeDtypeStruct((A_HEADS, CHUNK, CHUNK), F32),
                   jax.ShapeDtypeStruct((A_HEADS, CHUNK, 1), F32),
                   jax.ShapeDtypeStruct((1, D_B), F32), jax.ShapeDtypeStruct((1, D_B), F32)],
        scratch_shapes=[pltpu.VMEM((tm, D_A), F32)],
        compiler_params=_cparams(("arbitrary",), 40),
    )(z, cb, dy, a_ln_g, a_ln_b, w_s, b_s, b_ln_g, b_ln_b)


def _mixer_b_conv_bwd(z, dcb, conv_w, *, tm, name, comm=None):
    T = z.shape[0]
    halo = HALO_LONG

    def body(zb_ref, dcb_ref, dcn_ref, cw_ref, dzb_ref, dcw_ref, dbias_ref, dext_ref):
        i = pl.program_id(0)
        last = pl.num_programs(0) - 1

        @pl.when(i == 0)
        def _():
            dcw_ref[...] = jnp.zeros_like(dcw_ref)
            dbias_ref[...] = jnp.zeros_like(dbias_ref)
        dcb = dcb_ref[...]
        dext_ref[0:tm, :] = dcb
        dext_ref[tm:tm + halo, :] = jnp.where(i < last, dcn_ref[...], 0.0)
        dbias_ref[...] += _rowsum(dcb)
        for rb in range(tm // CONV_ROWS):
            for cb in range(D_B // CONV_COLS):
                cs = slice(cb * CONV_COLS, (cb + 1) * CONV_COLS)
                gcs = slice(D_B + cb * CONV_COLS, D_B + (cb + 1) * CONV_COLS)
                rs = slice(rb * CONV_ROWS, (rb + 1) * CONV_ROWS)
                xbb = zb_ref[rs, cs].astype(F32)
                sgb = _sigmoid(zb_ref[rs, gcs].astype(F32))
                yb0 = xbb * sgb
                acc = jnp.zeros((CONV_ROWS, CONV_COLS), F32)
                for k in range(B_CONV):
                    off = rb * CONV_ROWS + (B_CONV - 1) - k
                    shifted = dext_ref[off:off + CONV_ROWS, cs]
                    acc = acc + cw_ref[k:k + 1, cs] * shifted
                    dcw_ref[k:k + 1, cs] += _rowsum(shifted * yb0)
                dzb_ref[rs, cs] = (acc * sgb).astype(BF16)
                dzb_ref[rs, gcs] = (acc * xbb * sgb * (1.0 - sgb)).astype(BF16)

    row = lambda i: (i, 0)
    par = lambda i: (0, 0)
    return _pallas(
        comm, body, name=name, grid=(T // tm,),
        in_specs=[pl.BlockSpec((tm, 2 * D_B), lambda i: (i, 1)),
                  pl.BlockSpec((tm, D_B), row),
                  pl.BlockSpec((halo, D_B), lambda i: (_halo_next_index(tm, halo, T)(i), 0)),
                  pl.BlockSpec((B_CONV, D_B), par)],
        out_specs=[pl.BlockSpec((tm, 2 * D_B), row), pl.BlockSpec((B_CONV, D_B), par),
                   pl.BlockSpec((1, D_B), par)],
        out_shape=[jax.ShapeDtypeStruct((T, 2 * D_B), BF16), jax.ShapeDtypeStruct((B_CONV, D_B), F32),
                   jax.ShapeDtypeStruct((1, D_B), F32)],
        scratch_shapes=[pltpu.VMEM((tm + halo, D_B), F32)],
        compiler_params=_cparams(("arbitrary",), 40),
    )(z, dcb, dcb, conv_w)


def _conv3(w_ref, ext_ref, base, rows, cs):
    acc = w_ref[0:1, cs] * ext_ref[base - 2:base - 2 + rows, cs]
    acc = acc + w_ref[1:2, cs] * ext_ref[base - 1:base - 1 + rows, cs]
    return acc + w_ref[2:3, cs] * ext_ref[base:base + rows, cs]


def _conv3_t(w_ref, ext_ref, base, rows, cs):
    acc = w_ref[0:1, cs] * ext_ref[base + 2:base + 2 + rows, cs]
    acc = acc + w_ref[1:2, cs] * ext_ref[base + 1:base + 1 + rows, cs]
    return acc + w_ref[2:3, cs] * ext_ref[base:base + rows, cs]


def _mixer_c_fwd(z, conv_w, *, tm, name, comm=None):
    T = z.shape[0]
    D = D_MODEL
    halo = HALO_SHORT
    full = slice(0, D)

    def body(bg_ref, cg_ref, xv_ref, cgh_ref, xvh_ref, w_ref, r_ref, ext_ref):
        i = pl.program_id(0)
        ext_ref[halo:halo + tm, :] = cg_ref[...].astype(F32) * xv_ref[...].astype(F32)
        ext_ref[0:halo, :] = jnp.where(i > 0, cgh_ref[...].astype(F32) * xvh_ref[...].astype(F32), 0.0)
        q = _conv3(w_ref, ext_ref, halo, tm, full)
        r_ref[...] = (bg_ref[...].astype(F32) * q).astype(BF16)

    hp = _halo_prev_index(tm, halo)
    return _pallas(
        comm, body, name=name, grid=(T // tm,),
        in_specs=[pl.BlockSpec((tm, D), lambda i: (i, 0)), pl.BlockSpec((tm, D), lambda i: (i, 1)),
                  pl.BlockSpec((tm, D), lambda i: (i, 2)),
                  pl.BlockSpec((halo, D), lambda i: (hp(i), 1)),
                  pl.BlockSpec((halo, D), lambda i: (hp(i), 2)),
                  pl.BlockSpec((None, C_CONV, D), lambda i: (0, 0, 0))],
        out_specs=pl.BlockSpec((tm, D), lambda i: (i, 0)),
        out_shape=jax.ShapeDtypeStruct((T, D), BF16),
        scratch_shapes=[pltpu.VMEM((halo + tm, D), F32)],
        compiler_params=_cparams(("parallel",), 40),
    )(z, z, z, z, z, conv_w)


def _mixer_c_bwd(z, dr, conv_w, *, tm, name, comm=None):
    T = z.shape[0]
    D = D_MODEL
    halo = HALO_SHORT
    full = slice(0, D)

    def body(bg_ref, cg_ref, xv_ref, cgh_ref, xvh_ref, bgn_ref, dr_ref, drn_ref, w_ref,
             dz_ref, dw_ref, ext_ref, dext_ref):
        i = pl.program_id(0)
        last = pl.num_programs(0) - 1

        @pl.when(i == 0)
        def _():
            dw_ref[...] = jnp.zeros_like(dw_ref)
        cg = cg_ref[...].astype(F32)
        xv = xv_ref[...].astype(F32)
        dr = dr_ref[...].astype(F32)
        p = cg * xv
        ext_ref[halo:halo + tm, :] = p
        ext_ref[0:halo, :] = jnp.where(i > 0, cgh_ref[...].astype(F32) * xvh_ref[...].astype(F32), 0.0)
        dext_ref[0:tm, :] = dr * bg_ref[...].astype(F32)
        dext_ref[tm:tm + halo, :] = jnp.where(i < last, drn_ref[...].astype(F32) * bgn_ref[...].astype(F32), 0.0)
        q = _conv3(w_ref, ext_ref, halo, tm, full)
        dz_ref[:, 0:D] = (dr * q).astype(BF16)
        dp = jnp.zeros((tm, D), F32)
        for k in range(C_CONV):
            shifted = dext_ref[2 - k:2 - k + tm, :]
            dp = dp + w_ref[k:k + 1, :] * shifted
            dw_ref[k:k + 1, :] += _rowsum(shifted * p)
        dz_ref[:, D:2 * D] = (dp * xv).astype(BF16)
        dz_ref[:, 2 * D:3 * D] = (dp * cg).astype(BF16)

    hp = _halo_prev_index(tm, halo)
    hn = _halo_next_index(tm, halo, T)
    return _pallas(
        comm, body, name=name, grid=(T // tm,),
        in_specs=[pl.BlockSpec((tm, D), lambda i: (i, 0)), pl.BlockSpec((tm, D), lambda i: (i, 1)),
                  pl.BlockSpec((tm, D), lambda i: (i, 2)),
                  pl.BlockSpec((halo, D), lambda i: (hp(i), 1)),
                  pl.BlockSpec((halo, D), lambda i: (hp(i), 2)),
                  pl.BlockSpec((halo, D), lambda i: (hn(i), 0)),
                  pl.BlockSpec((tm, D), lambda i: (i, 0)),
                  pl.BlockSpec((halo, D), lambda i: (hn(i), 0)),
                  pl.BlockSpec((None, C_CONV, D), lambda i: (0, 0, 0))],
        out_specs=[pl.BlockSpec((tm, 3 * D), lambda i: (i, 0)),
                   pl.BlockSpec((C_CONV, D), lambda i: (0, 0))],
        out_shape=[jax.ShapeDtypeStruct((T, 3 * D), BF16), jax.ShapeDtypeStruct((C_CONV, D), F32)],
        scratch_shapes=[pltpu.VMEM((halo + tm, D), F32), pltpu.VMEM((tm + halo, D), F32)],
        compiler_params=_cparams(("arbitrary",), 48),
    )(z, z, z, z, z, z, dr, dr, conv_w)


FFN_COLS = 256


def _ffn_act_fwd(up, conv_w, *, layer, tm, name, comm=None):
    T = up.shape[0]
    halo = HALO_SHORT
    W = FFN_COLS

    def body(up_ref, uph_ref, w_ref, a_ref, upc_ref, ext_ref):
        i = pl.program_id(0)
        ext_ref[halo:halo + tm, :] = up_ref[...].astype(F32)
        ext_ref[0:halo, :] = jnp.where(i > 0, uph_ref[...].astype(F32), 0.0)
        for cb in range(D_FF // W):
            gs = slice(cb * W, (cb + 1) * W)
            vs = slice(D_FF + cb * W, D_FF + (cb + 1) * W)
            g = _conv3(w_ref, ext_ref, halo, tm, gs)
            v = _conv3(w_ref, ext_ref, halo, tm, vs)
            upc_ref[:, gs] = g.astype(BF16)
            upc_ref[:, vs] = v.astype(BF16)
            a_ref[:, gs] = (_silu(g) * v).astype(BF16)

    return _pallas(
        comm, body, name=name, grid=(T // tm,),
        in_specs=[pl.BlockSpec((tm, 2 * D_FF), lambda i: (i, 0)),
                  pl.BlockSpec((halo, 2 * D_FF), lambda i: (_halo_prev_index(tm, halo)(i), 0)),
                  pl.BlockSpec((None, F_CONV, 2 * D_FF), lambda i: (layer, 0, 0))],
        out_specs=[pl.BlockSpec((tm, D_FF), lambda i: (i, 0)),
                   pl.BlockSpec((tm, 2 * D_FF), lambda i: (i, 0))],
        out_shape=[jax.ShapeDtypeStruct((T, D_FF), BF16), jax.ShapeDtypeStruct((T, 2 * D_FF), BF16)],
        scratch_shapes=[pltpu.VMEM((halo + tm, 2 * D_FF), F32)],
        compiler_params=_cparams(("parallel",), 48),
    )(up, up, conv_w)


def _ffn_act_bwd(up, upc, da, conv_w, *, layer, tm, name, comm=None):
    T = up.shape[0]
    halo = HALO_SHORT
    W = FFN_COLS

    def body(up_ref, upc_ref, upcn_ref, da_ref, dan_ref, w_ref, dup_ref, dw_ref, dext_ref):
        i = pl.program_id(0)
        last = pl.num_programs(0) - 1

        @pl.when(i == 0)
        def _():
            dw_ref[...] = jnp.zeros_like(dw_ref)
        live = jnp.where(i < last, 1.0, 0.0)
        for cb in range(D_FF // W):
            gs = slice(cb * W, (cb + 1) * W)
            vs = slice(D_FF + cb * W, D_FF + (cb + 1) * W)
            g = jnp.concatenate([upc_ref[:, gs], upcn_ref[:, gs]], axis=0).astype(F32)
            v = jnp.concatenate([upc_ref[:, vs], upcn_ref[:, vs]], axis=0).astype(F32)
            da = jnp.concatenate([da_ref[:, gs].astype(F32), dan_ref[:, gs].astype(F32) * live], axis=0)
            s = _sigmoid(g)
            silu = g * s
            dext_ref[:, gs] = da * v * (s * (1.0 + g * (1.0 - s)))
            dext_ref[:, vs] = da * silu
            for cs in (gs, vs):
                u = up_ref[:, cs].astype(F32)
                acc = jnp.zeros((tm, W), F32)
                for k in range(F_CONV):
                    shifted = dext_ref[2 - k:2 - k + tm, cs]
                    acc = acc + w_ref[k:k + 1, cs] * shifted
                    dw_ref[k:k + 1, cs] += _rowsum(shifted * u)
                dup_ref[:, cs] = acc.astype(BF16)

    hn = _halo_next_index(tm, halo, T)
    return _pallas(
        comm, body, name=name, grid=(T // tm,),
        in_specs=[pl.BlockSpec((tm, 2 * D_FF), lambda i: (i, 0)),
                  pl.BlockSpec((tm, 2 * D_FF), lambda i: (i, 0)),
                  pl.BlockSpec((halo, 2 * D_FF), lambda i: (hn(i), 0)),
                  pl.BlockSpec((tm, D_FF), lambda i: (i, 0)),
                  pl.BlockSpec((halo, D_FF), lambda i: (hn(i), 0)),
                  pl.BlockSpec((None, F_CONV, 2 * D_FF), lambda i: (layer, 0, 0))],
        out_specs=[pl.BlockSpec((tm, 2 * D_FF), lambda i: (i, 0)),
                   pl.BlockSpec((F_CONV, 2 * D_FF), lambda i: (0, 0))],
        out_shape=[jax.ShapeDtypeStruct((T, 2 * D_FF), BF16),
                   jax.ShapeDtypeStruct((F_CONV, 2 * D_FF), F32)],
        scratch_shapes=[pltpu.VMEM((tm + halo, 2 * D_FF), F32)],
        compiler_params=_cparams(("arbitrary",), 56),
    )(up, upc, upc, da, da, conv_w)


def _local_step(x, tgt, small, plan):
    T = x.shape[0]
    tm_e = _pick(T, 256)
    tm_n = _pick(T, 512)
    tm = _pick(T, 1024)
    tm_r = _pick(T, 2048)
    tt = _pick(T, 512)
    nm = small["norm_mix"].reshape(2, 1, D_MODEL)
    nf = small["norm_ffn"].reshape(2, 1, D_MODEL)
    ngf = small["norm_final"].reshape(1, D_MODEL)
    b_s = small["a_b_s"].reshape(A_HEADS, CHUNK, 1)
    w_s = small["a_w_s"].reshape(A_HEADS, CHUNK, CHUNK)
    b_conv_w = small["b_conv_w"].reshape(B_CONV, D_B)
    sg = {}
    wt, cm = plan.weight, plan.comm

    h_m0 = _rmsnorm_fwd(x, nm, layer=0, tm=tm_n, name="norm_mix0")
    z_ab = _mm_nn(h_m0, wt("ab_w_in", 0), layer=0, tm=tm, tn=512, out_dtype=BF16, name="ab_in", comm=cm("ab_in"))
    yab, cb = _mixer_ab_fwd(z_ab, small["a_ln_g"], small["a_ln_b"], w_s, b_s, b_conv_w, small["b_conv_b"],
                            small["b_ln_g"], small["b_ln_b"], tm=tm_e, name="mixer_ab", comm=cm("mixer_ab"))
    x1 = _mm_nn(yab, wt("ab_w_out", 0), layer=0, tm=tm, tn=512, residual=x, name="ab_out", comm=cm("ab_out"))

    def ffn_fwd(xin, layer):
        h = _rmsnorm_fwd(xin, nf, layer=layer, tm=tm_n, name=f"norm_ffn{layer}")
        up = _mm_nn(h, wt("f_w_up", layer), layer=0, tm=tm, tn=1408, out_dtype=BF16, name=f"ffn_up{layer}",
                    comm=cm(f"ffn_up{layer}"))
        a, upc = _ffn_act_fwd(up, small["f_conv_w"], layer=layer, tm=tm_e, name=f"ffn_act{layer}",
                              comm=cm(f"ffn_act{layer}"))
        xout = _mm_nn(a, wt("f_w_down", layer), layer=0, tm=tm, tn=512, residual=xin, name=f"ffn_down{layer}",
                      comm=cm(f"ffn_down{layer}"))
        return h, up, upc, a, xout

    h_f0, up0, upc0, a0, x2 = ffn_fwd(x1, 0)
    h_m1 = _rmsnorm_fwd(x2, nm, layer=1, tm=tm_n, name="norm_mix1")
    z_c = _mm_nn(h_m1, wt("c_w_in", 0), layer=0, tm=tm, tn=768, out_dtype=BF16, name="c_in", comm=cm("c_in"))
    r = _mixer_c_fwd(z_c, small["c_conv_w"], tm=tm_e, name="mixer_c", comm=cm("mixer_c"))
    x3 = _mm_nn(r, wt("c_w_out", 0), layer=0, tm=tm, tn=512, residual=x2, name="c_out", comm=cm("c_out"))
    h_f1, up1, upc1, a1, x4 = ffn_fwd(x3, 1)
    loss, dx, sg["norm_final"] = _loss_head(x4, tgt, ngf, tm=tm_n, name="loss_head")

    def ffn_bwd(dx, xin, h, up, upc, a, layer):
        da = _mm_nt(dx, wt("f_w_down", layer), layer=0, tm=tm, tn=1408, out_dtype=BF16,
                    name=f"ffn_down_dx{layer}", comm=cm(f"ffn_down_dx{layer}"))
        plan.grad_ready("f_w_down", layer, _mm_tn(a, dx, shards=None, tk=1408, tn=1024, tt=tt,
                                                  name=f"ffn_down_dw{layer}", comm=cm(f"ffn_down_dw{layer}")))
        dup, dcw = _ffn_act_bwd(up, upc, da, small["f_conv_w"], layer=layer, tm=tm_e, name=f"ffn_act_bwd{layer}",
                                comm=cm(f"ffn_act_bwd{layer}"))
        dh = _mm_nt(dup, wt("f_w_up", layer), layer=0, tm=tm_r, tn=None, name=f"ffn_up_dx{layer}",
                    comm=cm(f"ffn_up_dx{layer}"))
        plan.grad_ready("f_w_up", layer, _mm_tn(h, dup, shards=N_CHIPS, tk=1024, tn=1408, tt=tt,
                                                name=f"ffn_up_dw{layer}", comm=cm(f"ffn_up_dw{layer}")))
        dxin, dg = _rmsnorm_bwd(xin, nf, dh, dx, layer=layer, tm=tm_n, name=f"norm_ffn_bwd{layer}",
                                comm=cm(f"norm_ffn_bwd{layer}"))
        return dxin, dg, dcw

    dx, dnf1, dfc1 = ffn_bwd(dx, x3, h_f1, up1, upc1, a1, 1)
    dr = _mm_nt(dx, wt("c_w_out", 0), layer=0, tm=tm, tn=512, out_dtype=BF16, name="c_out_dx", comm=cm("c_out_dx"))
    plan.grad_ready("c_w_out", 0, _mm_tn(r, dx, shards=None, tk=1024, tn=1024, tt=tt, name="c_out_dw",
                                         comm=cm("c_out_dw")))
    dz_c, dccw = _mixer_c_bwd(z_c, dr, small["c_conv_w"], tm=tm_e, name="mixer_c_bwd", comm=cm("mixer_c_bwd"))
    sg["c_conv_w"] = dccw.reshape(1, C_CONV, D_MODEL)
    dh = _mm_nt(dz_c, wt("c_w_in", 0), layer=0, tm=tm_r, tn=None, name="c_in_dx", comm=cm("c_in_dx"))
    plan.grad_ready("c_w_in", 0, _mm_tn(h_m1, dz_c, shards=N_CHIPS, tk=1024, tn=768, tt=tt, name="c_in_dw",
                                        comm=cm("c_in_dw")))
    dx, dnm1 = _rmsnorm_bwd(x2, nm, dh, dx, layer=1, tm=tm_n, name="norm_mix_bwd1", comm=cm("norm_mix_bwd1"))
    dx, dnf0, dfc0 = ffn_bwd(dx, x1, h_f0, up0, upc0, a0, 0)
    dyab = _mm_nt(dx, wt("ab_w_out", 0), layer=0, tm=tm, tn=512, out_dtype=BF16, name="ab_out_dx",
                  comm=cm("ab_out_dx"))
    plan.grad_ready("ab_w_out", 0, _mm_tn(yab, dx, shards=None, tk=1024, tn=1024, tt=tt, name="ab_out_dw",
                                          comm=cm("ab_out_dw")))
    (dza, dcb, sg["a_ln_g"], sg["a_ln_b"], dws, dbs, sg["b_ln_g"], sg["b_ln_b"]) = _mixer_ab_bwd_pre(
        z_ab, cb, dyab, small["a_ln_g"], small["a_ln_b"], w_s, b_s, small["b_ln_g"], small["b_ln_b"],
        tm=tm_e, name="mixer_ab_bwd", comm=cm("mixer_ab_bwd"))
    dzb, dbcw, sg["b_conv_b"] = _mixer_b_conv_bwd(z_ab, dcb, b_conv_w, tm=tm_e, name="mixer_b_conv_bwd",
                                                  comm=cm("mixer_b_conv_bwd"))
    sg["a_w_s"] = dws.reshape(1, A_HEADS, CHUNK, CHUNK)
    sg["a_b_s"] = dbs.reshape(1, A_HEADS, CHUNK)
    sg["b_conv_w"] = dbcw.reshape(1, B_CONV, D_B)
    dz_ab = jnp.concatenate([dza, dzb], axis=1)
    dh = _mm_nt(dz_ab, wt("ab_w_in", 0), layer=0, tm=tm_r, tn=None, name="ab_in_dx", comm=cm("ab_in_dx"))
    plan.grad_ready("ab_w_in", 0, _mm_tn(h_m0, dz_ab, shards=N_CHIPS, tk=1024, tn=512, tt=tt, name="ab_in_dw",
                                         comm=cm("ab_in_dw")))
    dx, dnm0 = _rmsnorm_bwd(x, nm, dh, dx, layer=0, tm=tm_n, name="norm_mix_bwd0", comm=cm("norm_mix_bwd0"))

    sg["norm_mix"] = [dnm0, dnm1]
    sg["norm_ffn"] = [dnf0, dnf1]
    sg["f_conv_w"] = [dfc0, dfc1]
    return loss, dx, sg


BLOCK_BYTES = 3 * 1024 * 1024


BF16_SUBLANES = 16


def _row_tile(rows, row_bytes, step=SUBLANES):
    best = None
    for tr in range(step, rows + 1, step):
        if rows % tr == 0 and tr * row_bytes <= BLOCK_BYTES:
            best = tr
    if best is None:
        raise ValueError(f"no row tile for {rows}")
    return best


def _place_scalars():
    x, y, c = lax.axis_index("x"), lax.axis_index("y"), lax.axis_index("c")
    return jnp.stack([c, 2 * x + y, 2 * (1 - x) + y, 2 * x + (1 - y), 2 * (1 - x) + (1 - y)]).astype(jnp.int32)


def _cast_into_slot(w, place, *, layer, name):
    L, rows, cols = w.shape
    tr = _row_tile(rows, cols * 4, BF16_SUBLANES)

    def body(place_ref, w_ref, o_ref):
        o_ref[...] = w_ref[...].astype(BF16)

    return pl.pallas_call(
        body, name=name,
        grid_spec=pltpu.PrefetchScalarGridSpec(
            num_scalar_prefetch=1, grid=(rows // tr,),
            in_specs=[pl.BlockSpec((None, tr, cols), lambda i, p: (layer, i, 0))],
            out_specs=pl.BlockSpec((None, None, tr, cols), lambda i, p: (0, p[1], i, 0))),
        out_shape=jax.ShapeDtypeStruct((1, N_CHIPS, rows, cols), BF16),
        compiler_params=_cparams(("parallel",), 32),
    )(place, w)


def _pair_sum(g, theirs, place, *, name):
    S, rows, cols = g.shape
    half = rows // 2
    tr = _row_tile(half, cols * 4, BF16_SUBLANES)
    nb = half // tr

    def body(place_ref, g_ref, t_ref, o_ref):
        o_ref[...] = (g_ref[...] + t_ref[...]).astype(BF16)

    spec = pl.BlockSpec((None, tr, cols), lambda s, i, p: (s, i, 0))
    return pl.pallas_call(
        body, name=name,
        grid_spec=pltpu.PrefetchScalarGridSpec(
            num_scalar_prefetch=1, grid=(S, nb),
            in_specs=[pl.BlockSpec((None, tr, cols), lambda s, i, p: (s, p[0] * nb + i, 0)), spec],
            out_specs=spec),
        out_shape=jax.ShapeDtypeStruct((S, half, cols), BF16),
        compiler_params=_cparams(("parallel", "parallel"), 32),
    )(place, g, theirs)


def _chip_sum(p, r, g_prev, place, *, layer, shape, name):
    L, rows, cols = shape
    half = rows // 2
    tr = _row_tile(half, cols * 4, BF16_SUBLANES)
    nb = half // tr

    def body(place_ref, p_ref, r_ref, *rest):
        o_ref = rest[-1]
        mine = p_ref[...].astype(F32)
        peers = [r_ref[j].astype(F32) for j in range(3)]
        acc = None
        for s in range(N_CHIPS):
            term = jnp.where(place_ref[1] == s, mine,
                             jnp.where(place_ref[2] == s, peers[0],
                                       jnp.where(place_ref[3] == s, peers[1], peers[2])))
            acc = term if acc is None else acc + term
        o_ref[...] = acc

    in_specs = [pl.BlockSpec((None, tr, cols), lambda i, pr: (pr[1], i, 0)),
                pl.BlockSpec((3, tr, cols), lambda i, pr: (0, i, 0))]
    args = [place, p, r]
    aliases = {}
    if g_prev is not None:
        in_specs.append(ANY)
        args.append(g_prev)
        aliases = {3: 0}
    return pl.pallas_call(
        body, name=name,
        grid_spec=pltpu.PrefetchScalarGridSpec(
            num_scalar_prefetch=1, grid=(nb,), in_specs=in_specs,
            out_specs=pl.BlockSpec((None, tr, cols), lambda i, pr: (layer, pr[0] * nb + i, 0))),
        out_shape=jax.ShapeDtypeStruct(shape, F32), input_output_aliases=aliases,
        compiler_params=_cparams(("parallel",), 32),
    )(*args)


def _adamw_math(w, g, m, v):
    m2 = ADAM_B1 * m + (1.0 - ADAM_B1) * g
    v2 = ADAM_B2 * v + (1.0 - ADAM_B2) * (g * g)
    m_hat = m2 / (1.0 - ADAM_B1 ** ADAM_STEP)
    v_hat = v2 / (1.0 - ADAM_B2 ** ADAM_STEP)
    delta = -ADAM_LR * (m_hat / (jnp.sqrt(v_hat) + ADAM_EPS) + ADAM_WD * w)
    return delta, m2, v2


def _adamw(w, g, m, v, *, name):
    L, rows, cols = w.shape
    tr = _row_tile(rows, cols * 4)

    def body(w_ref, g_ref, m_ref, v_ref, d_ref, m2_ref, v2_ref):
        d, m2, v2 = _adamw_math(w_ref[...], g_ref[...], m_ref[...], v_ref[...])
        d_ref[...] = d
        m2_ref[...] = m2
        v2_ref[...] = v2

    spec = pl.BlockSpec((None, tr, cols), lambda l, i: (l, i, 0))
    shape = jax.ShapeDtypeStruct(w.shape, F32)
    return pl.pallas_call(
        body, name=name, grid=(L, rows // tr), in_specs=[spec] * 4, out_specs=[spec] * 3,
        out_shape=[shape] * 3,
        compiler_params=_cparams(("parallel", "parallel"), 48),
    )(w, g, m, v)


def _exchange_packs(pack, *, reduce, name):
    R = pack.shape[0]
    ndev = 2 * N_CHIPS

    def body(p_ref, o_ref, *scratch):
        if reduce:
            buf, send, recv = scratch
        else:
            buf = o_ref
            send, recv = scratch
        x, y, c = lax.axis_index("x"), lax.axis_index("y"), lax.axis_index("c")
        me = 4 * x + 2 * y + c
        buf[me] = p_ref[...]
        sends = []
        for q in range(1, ndev):
            qx, qy, qc = (q >> 2) & 1, (q >> 1) & 1, q & 1
            peer = (x ^ qx, y ^ qy, c ^ qc)
            rc = _remote(p_ref, buf.at[me], send.at[q - 1], recv.at[q - 1], peer)
            rc.start()
            sends.append(rc)
        for q in range(1, ndev):
            qx, qy, qc = (q >> 2) & 1, (q >> 1) & 1, q & 1
            slot = buf.at[4 * (x ^ qx) + 2 * (y ^ qy) + (c ^ qc)]
            _remote(slot, slot, send.at[q - 1], recv.at[q - 1], (x ^ qx, y ^ qy, c ^ qc)).wait_recv()
        for rc in sends:
            rc.wait_send()
        if reduce:
            acc = buf[0]
            for d in range(1, ndev):
                acc = acc + buf[d]
            o_ref[...] = acc

    vm = pl.BlockSpec(memory_space=pltpu.VMEM)
    sems = [pltpu.SemaphoreType.DMA((ndev - 1,)), pltpu.SemaphoreType.DMA((ndev - 1,))]
    if reduce:
        out_shape = jax.ShapeDtypeStruct((R, LANES), F32)
        scratch = [pltpu.VMEM((ndev, R, LANES), F32)] + sems
    else:
        out_shape = jax.ShapeDtypeStruct((ndev, R, LANES), F32)
        scratch = sems
    return pl.pallas_call(
        body, name=name, in_specs=[vm], out_specs=vm, out_shape=out_shape, scratch_shapes=scratch,
        compiler_params=pltpu.CompilerParams(vmem_limit_bytes=32 * 1024 * 1024),
    )(pack)


PACK_UNIT = SUBLANES * LANES


def _pack(arrays):
    flat, sizes = [], []
    for a in arrays:
        pieces = a if isinstance(a, (list, tuple)) else [a]
        v = jnp.concatenate([p.reshape(-1) for p in pieces]) if len(pieces) > 1 else pieces[0].reshape(-1)
        size = v.shape[0]
        padded = -(-size // PACK_UNIT) * PACK_UNIT
        flat.append(jnp.pad(v, (0, padded - size)))
        sizes.append((size, padded))
    return jnp.concatenate(flat).reshape(-1, LANES), sizes


def _unpack(pack, sizes, shapes):
    v = pack.reshape(-1)
    out, off = [], 0
    for (size, padded), shape in zip(sizes, shapes):
        out.append(v[off:off + size].reshape(shape))
        off += padded
    return out


BIG = ("ab_w_in", "ab_w_out", "c_w_in", "c_w_out", "f_w_up", "f_w_down")
COL_SHARDED = ("ab_w_in", "c_w_in", "f_w_up")
SMALL_REPLICATED = ("norm_mix", "norm_ffn", "norm_final", "a_ln_g", "a_ln_b", "a_w_s", "a_b_s",
                    "b_conv_b", "b_ln_g", "b_ln_b")
SMALL_SHARDED = ("b_conv_w", "c_conv_w", "f_conv_w")
SMALL = SMALL_REPLICATED + SMALL_SHARDED
ALL_WEIGHTS = ("norm_mix", "norm_ffn", "norm_final", "ab_w_in", "a_ln_g", "a_ln_b", "a_w_s", "a_b_s",
               "b_conv_w", "b_conv_b", "b_ln_g", "b_ln_b", "ab_w_out", "c_w_in", "c_conv_w", "c_w_out",
               "f_w_up", "f_conv_w", "f_w_down")


SCHEDULE = {
    "mixer_ab": [("gi", "f_w_up", 0)],
    "ab_out": [("gd", "f_w_up", 0)],
    "ffn_up0": [("gi", "f_w_down", 0), ("gi", "c_w_in", 0)],
    "ffn_act0": [("gd", "f_w_down", 0), ("gd", "c_w_in", 0), ("gi", "f_w_up", 1)],
    "ffn_down0": [("gd", "f_w_up", 1), ("gi", "c_w_out", 0)],
    "c_in": [("gd", "c_w_out", 0), ("gi", "f_w_down", 1)],
    "mixer_c": [("gd", "f_w_down", 1)],
    "ffn_act_bwd1": [("px", "f_w_down", 1)],
    "ffn_up_dx1": [("cx", "f_w_down", 1, 0, 1)],
    "norm_ffn_bwd1": [("px", "f_w_up", 1)],
    "mixer_c_bwd": [("cx", "f_w_up", 1, 0, 2), ("px", "c_w_out", 0)],
    "c_in_dx": [("cx", "c_w_out", 0, 0, 1)],
    "c_in_dw": [("cx", "f_w_up", 1, 1, 2)],
    "norm_mix_bwd1": [("px", "c_w_in", 0)],
    "ffn_act_bwd0": [("cx", "c_w_in", 0, 0, 1), ("px", "f_w_down", 0)],
    "ffn_up_dx0": [("cx", "f_w_down", 0, 0, 1)],
    "norm_ffn_bwd0": [("px", "f_w_up", 0)],
    "mixer_ab_bwd": [("px", "ab_w_out", 0)],
    "mixer_b_conv_bwd": [("cx", "f_w_up", 0, 0, 1)],
    "ab_in_dx": [("cx", "ab_w_out", 0, 0, 1)],
    "norm_mix_bwd0": [("px", "ab_w_in", 0)],
}


class _Plan:
    def __init__(self, shapes, place):
        self.shapes, self.place, self.bufs = shapes, place, {}

    def weight(self, name, layer):
        g = self.bufs[f"w:{name}:{layer}"]
        if name in COL_SHARDED:
            return g
        _, S, rows, cols = g.shape
        return g.reshape(1, S * rows, cols)

    def grad_ready(self, name, layer, g):
        _, rows, cols = self.shapes[name]
        self.bufs[f"g:{name}:{layer}"] = g.reshape(N_CHIPS, rows, cols)
        self.bufs[f"t:{name}:{layer}"] = lax.empty((N_CHIPS, rows // 2, cols), F32)
        self.bufs[f"l:{name}:{layer}"] = lax.empty((3, rows // 2, cols), BF16)

    def job(self, kind, name, layer, part=0, parts=1):
        _, rows, cols = self.shapes[name]
        key = f"{name}:{layer}"
        if kind == "gi":
            return _job_gather_ici("w:" + key, rows)
        if kind == "gd":
            return _job_gather_d2d("w:" + key, rows)
        if kind == "px":
            return _job_pair_exchange("g:" + key, "t:" + key, rows)
        if kind == "cx":
            if "p:" + key not in self.bufs:
                self.bufs["p:" + key] = _pair_sum(self.bufs["g:" + key], self.bufs["t:" + key], self.place,
                                                  name=f"pair_sum_{name}{layer}")
            nr = rows // 2 // parts
            return _job_chip_exchange("p:" + key, "l:" + key, part * nr, nr)
        if kind == "ps":
            return _job_pair_share("G:" + name, layer, rows)
        raise ValueError(kind)

    def comm(self, call):
        specs = SCHEDULE.get(call)
        return None if specs is None else _Comm(self, [self.job(*spec) for spec in specs])


def _step(x, tgt, w, m, v):
    chip = 2 * lax.axis_index("x") + lax.axis_index("y")
    place = _place_scalars()
    plan = _Plan({n: w[n].shape for n in BIG}, place)
    items = [(n, l) for n in BIG for l in range(w[n].shape[0])]

    for n, l in items:
        plan.bufs[f"w:{n}:{l}"] = _cast_into_slot(w[n], place, layer=l, name=f"cast_{n}{l}")
    first = [("ab_w_in", 0), ("ab_w_out", 0)]
    _comm_only(plan, [[plan.job("gi", n, l) for n, l in first], [plan.job("gd", n, l) for n, l in first]],
               name="gather_first")
    conv_pack, conv_sizes = _pack([w[n] for n in SMALL_SHARDED])
    conv_all = _exchange_packs(conv_pack, reduce=False, name="gather_conv_weights")
    conv_shapes = [w[n].shape for n in SMALL_SHARDED]
    per_chip = [_unpack(conv_all[2 * s], conv_sizes, conv_shapes) for s in range(N_CHIPS)]
    small = {n: w[n] for n in SMALL_REPLICATED}
    for idx, n in enumerate(SMALL_SHARDED):
        small[n] = jnp.concatenate([per_chip[s][idx] for s in range(N_CHIPS)], axis=-1)

    loss, dx, sg = _local_step(x, tgt, small, plan)

    _comm_only(plan, [[plan.job("cx", "ab_w_in", 0)]], name="reduce_last")
    for n, l in items:
        plan.bufs["G:" + n] = _chip_sum(plan.bufs[f"p:{n}:{l}"], plan.bufs[f"l:{n}:{l}"], plan.bufs.get("G:" + n),
                                        place, layer=l, shape=w[n].shape, name=f"chip_sum_{n}{l}")
    _comm_only(plan, [[plan.job("ps", n, l) for n, l in items]], name="reduce_pair_share")
    grads_big = [plan.bufs["G:" + n] for n in BIG]

    g_pack, g_sizes = _pack([sg[n] for n in SMALL])
    g_sum = _exchange_packs(g_pack, reduce=True, name="allreduce_small_grads")
    full_shapes = [small[n].shape for n in SMALL]
    g_small = dict(zip(SMALL, _unpack(g_sum, g_sizes, full_shapes)))
    for n in SMALL_SHARDED:
        width = w[n].shape[-1]
        g_small[n] = lax.dynamic_slice_in_dim(g_small[n], chip * width, width, axis=g_small[n].ndim - 1)

    grad, delta, new_m, new_v = {}, {}, {}, {}
    for n, g in zip(BIG, grads_big):
        grad[n] = g
        delta[n], new_m[n], new_v[n] = _adamw(w[n], g, m[n], v[n], name=f"adamw_{n}")
    shapes = [w[n].shape for n in SMALL]
    wp, sizes = _pack([w[n] for n in SMALL])
    gp, _ = _pack([g_small[n] for n in SMALL])
    mp, _ = _pack([m[n] for n in SMALL])
    vp, _ = _pack([v[n] for n in SMALL])
    R = wp.shape[0]
    dp, m2p, v2p = _adamw(wp.reshape(1, R, LANES), gp.reshape(1, R, LANES), mp.reshape(1, R, LANES),
                          vp.reshape(1, R, LANES), name="adamw_small")
    for n, d_, m_, v_ in zip(SMALL, _unpack(dp, sizes, shapes), _unpack(m2p, sizes, shapes),
                             _unpack(v2p, sizes, shapes)):
        grad[n] = g_small[n]
        delta[n], new_m[n], new_v[n] = d_, m_, v_
    return loss, dx, grad, delta, new_m, new_v


def kernel(x, norm_mix, norm_ffn, norm_final, ab_w_in, a_ln_g, a_ln_b, a_w_s, a_b_s, b_conv_w, b_conv_b, b_ln_g, b_ln_b, ab_w_out, c_w_in, c_conv_w, c_w_out, f_w_up, f_conv_w, f_w_down, loss_target, m_norm_mix, m_norm_ffn, m_norm_final, m_ab_w_in, m_a_ln_g, m_a_ln_b, m_a_w_s, m_a_b_s, m_b_conv_w, m_b_conv_b, m_b_ln_g, m_b_ln_b, m_ab_w_out, m_c_w_in, m_c_conv_w, m_c_w_out, m_f_w_up, m_f_conv_w, m_f_w_down, v_norm_mix, v_norm_ffn, v_norm_final, v_ab_w_in, v_a_ln_g, v_a_ln_b, v_a_w_s, v_a_b_s, v_b_conv_w, v_b_conv_b, v_b_ln_g, v_b_ln_b, v_ab_w_out, v_c_w_in, v_c_conv_w, v_c_w_out, v_f_w_up, v_f_conv_w, v_f_w_down):
    given = dict(locals())
    w = {n: given[n] for n in ALL_WEIGHTS}
    m = {n: given["m_" + n] for n in ALL_WEIGHTS}
    v = {n: given["v_" + n] for n in ALL_WEIGHTS}
    T = x.shape[1]
    loss, dx, grad, delta, new_m, new_v = _step(x.reshape(T, D_MODEL), loss_target.reshape(T, D_MODEL), w, m, v)
    loss = lax.psum(loss[0, 0], ("x", "y", "c"))
    out = [loss, dx.reshape(x.shape)]
    for d in (grad, delta, new_m, new_v):
        out += [d[n] for n in ALL_WEIGHTS]
    return tuple(out)
```

```python
import functools
import math

import jax
import jax.numpy as jnp
from jax import lax
from jax.experimental import pallas as pl
from jax.experimental.pallas import tpu as pltpu

F32 = jnp.float32
BF16 = jnp.bfloat16

EPS = 1e-6
D_MODEL = 1024
CHUNK = 128
HEAD_DIM = 128
A_HEADS = 4
D_A = 512
D_B = 512
B_CONV = 31
C_CONV = 3
D_FF = 2816
F_CONV = 3
N_CHIPS = 4

ADAM_LR = 0.001
ADAM_B1 = 0.9
ADAM_B2 = 0.999
ADAM_EPS = 1e-08
ADAM_WD = 0.01
ADAM_STEP = 10

SUBLANES = 8
LANES = 128
HALO_SHORT = 16
HALO_LONG = 32
VMEM_BYTES_MAX = 60000 * 1024

INV_SQRT2 = 1.0 / math.sqrt(2.0)
INV_SQRT_2PI = 1.0 / math.sqrt(2.0 * math.pi)

MESH = pl.DeviceIdType.MESH


def _cparams(sem, vmem_mb):
    del vmem_mb
    return pltpu.CompilerParams(dimension_semantics=sem, vmem_limit_bytes=VMEM_BYTES_MAX)


def _pick(total, pref):
    for c in (2048, 1024, 512, 256, 128):
        if c <= pref and total % c == 0:
            return c
    raise ValueError(f"no tile for {total}")


def _sigmoid(x):
    return jax.nn.sigmoid(x)


def _silu(x):
    return x * _sigmoid(x)


def _dsilu(x):
    s = _sigmoid(x)
    return s * (1.0 + x * (1.0 - s))


def _gelu(x):
    return 0.5 * x * (1.0 + lax.erf(x * INV_SQRT2))


def _dgelu(x):
    return 0.5 * (1.0 + lax.erf(x * INV_SQRT2)) + x * jnp.exp(-0.5 * x * x) * INV_SQRT_2PI


def _ln_stats(x):
    mu = jnp.mean(x, axis=-1, keepdims=True)
    xc = x - mu
    var = jnp.mean(xc * xc, axis=-1, keepdims=True)
    r = lax.rsqrt(var + EPS)
    return xc * r, r


def _ln_bwd(dy, xh, r, g):
    dxh = dy * g
    m1 = jnp.mean(dxh, axis=-1, keepdims=True)
    m2 = jnp.mean(dxh * xh, axis=-1, keepdims=True)
    return r * (dxh - m1 - xh * m2)


def _rowsum(x):
    return jnp.sum(x, axis=0, keepdims=True)


ANY = pl.BlockSpec(memory_space=pltpu.HBM)


def _place():
    x, y, c = lax.axis_index("x"), lax.axis_index("y"), lax.axis_index("c")
    peers = [(1 - x, y), (x, 1 - y), (1 - x, 1 - y)]
    return x, y, c, 2 * x + y, (x, y, 1 - c), peers


def _half(rows, which):
    return pl.ds(which * (rows // 2), rows // 2)


def _remote(src, dst, send_sem, recv_sem, device):
    return pltpu.make_async_remote_copy(src_ref=src, dst_ref=dst, send_sem=send_sem, recv_sem=recv_sem,
                                        device_id=device, device_id_type=MESH)


class _Job:
    def __init__(self, reads, writes, ncopies, copies):
        self.reads, self.writes, self.ncopies, self.copies = reads, writes, ncopies, copies


def _job_gather_ici(name, rows):
    def copies(src, dst, sem):
        x, y, c, k, sib, peers = _place()
        out = []
        for j, (px, py) in enumerate(peers):
            mine = src[name].at[0, k, _half(rows, c)]
            out.append((_remote(mine, dst[name].at[0, k, _half(rows, c)], sem(j, 0), sem(j, 1), (px, py, c)),
                        _remote(mine, dst[name].at[0, 2 * px + py, _half(rows, c)], sem(j, 0), sem(j, 1), (px, py, c))))
        return out
    return _Job([], [name], 3, copies)


def _job_gather_d2d(name, rows):
    def copies(src, dst, sem):
        x, y, c, k, sib, peers = _place()
        out = []
        for j, (px, py) in enumerate(peers):
            landed = src[name].at[0, 2 * px + py, _half(rows, c)]
            out.append((_remote(landed, dst[name].at[0, 2 * px + py, _half(rows, c)], sem(j, 0), sem(j, 1), sib),
                        _remote(landed, dst[name].at[0, 2 * px + py, _half(rows, 1 - c)], sem(j, 0), sem(j, 1), sib)))
        return out
    return _Job([], [name], 3, copies)


def _job_pair_exchange(gname, tname, rows):
    def copies(src, dst, sem):
        x, y, c, k, sib, peers = _place()
        cp = _remote(src[gname].at[:, _half(rows, 1 - c), :], dst[tname], sem(0, 0), sem(0, 1), sib)
        return [(cp, cp)]
    return _Job([gname], [tname], 1, copies)


def _job_chip_exchange(pname, lname, r0, nr):
    def copies(src, dst, sem):
        x, y, c, k, sib, peers = _place()
        out = []
        for j, (px, py) in enumerate(peers):
            cp = _remote(src[pname].at[2 * px + py, pl.ds(r0, nr)], dst[lname].at[j, pl.ds(r0, nr)],
                         sem(j, 0), sem(j, 1), (px, py, c))
            out.append((cp, cp))
        return out
    return _Job([pname], [lname], 3, copies)


def _job_pair_share(name, layer, rows):
    def copies(src, dst, sem):
        x, y, c, k, sib, peers = _place()
        mine = src[name].at[layer, _half(rows, c)]
        return [(_remote(mine, dst[name].at[layer, _half(rows, c)], sem(0, 0), sem(0, 1), sib),
                 _remote(mine, dst[name].at[layer, _half(rows, 1 - c)], sem(0, 0), sem(0, 1), sib))]
    return _Job([], [name], 1, copies)


class _Comm:
    def __init__(self, plan, jobs):
        self.plan, self.jobs = plan, jobs
        self.writes, self.reads = [], []
        for job in jobs:
            for n in job.writes:
                if n not in self.writes:
                    self.writes.append(n)
        for job in jobs:
            for n in job.reads:
                if n not in self.writes and n not in self.reads:
                    self.reads.append(n)
        self.ncopies = sum(job.ncopies for job in jobs)

    def descriptors(self, src, dst, sems, base):
        out = []
        for job in self.jobs:
            sem = lambda j, which, base=base: sems.at[base + j, which]
            out += job.copies(src, dst, sem)
            base += job.ncopies
        return out

    def start(self, src, dst, sems, base=0):
        for first, _ in self.descriptors(src, dst, sems, base):
            first.start()

    def finish(self, src, dst, sems, base=0):
        for _, landed in self.descriptors(src, dst, sems, base):
            landed.wait()


def _comm_operands(comm):
    bufs = comm.plan.bufs
    shapes = [jax.ShapeDtypeStruct(bufs[n].shape, bufs[n].dtype) for n in comm.writes]
    return [bufs[n] for n in comm.reads] + [bufs[n] for n in comm.writes], shapes


def _pallas(comm, body, *, name, grid, in_specs, out_specs, out_shape, compiler_params, scratch_shapes=()):
    if comm is None:
        return pl.pallas_call(body, name=name, grid=grid, in_specs=in_specs, out_specs=out_specs,
                              out_shape=out_shape, scratch_shapes=list(scratch_shapes),
                              compiler_params=compiler_params)
    single = not isinstance(out_shape, (list, tuple))
    base_specs = [out_specs] if single else list(out_specs)
    base_shape = [out_shape] if single else list(out_shape)
    nb, nr, nw, nbo, nsc = len(in_specs), len(comm.reads), len(comm.writes), len(base_specs), len(scratch_shapes)

    def wrapped(*refs):
        base_in, rd, wr_in = refs[:nb], refs[nb:nb + nr], refs[nb + nr:nb + nr + nw]
        o0 = nb + nr + nw
        base_out, wr_out = refs[o0:o0 + nbo], refs[o0 + nbo:o0 + nbo + nw]
        scratch, sems = refs[o0 + nbo + nw:o0 + nbo + nw + nsc], refs[-1]
        src = dict(zip(comm.reads, rd))
        src.update(zip(comm.writes, wr_in))
        dst = dict(zip(comm.writes, wr_out))
        first = functools.reduce(jnp.logical_and, [pl.program_id(a) == 0 for a in range(len(grid))])
        last = functools.reduce(jnp.logical_and,
                                [pl.program_id(a) == pl.num_programs(a) - 1 for a in range(len(grid))])

        @pl.when(first)
        def _():
            comm.start(src, dst, sems)
        body(*base_in, *base_out, *scratch)

        @pl.when(last)
        def _():
            comm.finish(src, dst, sems)

    operands, shapes = _comm_operands(comm)
    call = pl.pallas_call(
        wrapped, name=name, grid=grid, in_specs=list(in_specs) + [ANY] * (nr + nw),
        out_specs=base_specs + [ANY] * nw, out_shape=base_shape + shapes,
        input_output_aliases={nb + nr + q: nbo + q for q in range(nw)},
        scratch_shapes=list(scratch_shapes) + [pltpu.SemaphoreType.DMA((comm.ncopies, 2))],
        compiler_params=compiler_params)

    def run(*args):
        outs = call(*args, *operands)
        for q, n in enumerate(comm.writes):
            comm.plan.bufs[n] = outs[nbo + q]
        return outs[0] if single else list(outs[:nbo])

    return run


def _comm_only(plan, phases, *, name):
    comms = [_Comm(plan, jobs) for jobs in phases]
    both = _Comm(plan, [job for jobs in phases for job in jobs])
    nr, nw = len(both.reads), len(both.writes)

    def body(*refs):
        rd, wr_in, wr_out, sems = refs[:nr], refs[nr:nr + nw], refs[nr + nw:nr + 2 * nw], refs[-1]
        src = dict(zip(both.reads, rd))
        src.update(zip(both.writes, wr_in))
        dst = dict(zip(both.writes, wr_out))
        base = 0
        for comm in comms:
            comm.start(src, dst, sems, base)
            comm.finish(src, dst, sems, base)
            base += comm.ncopies

    operands, shapes = _comm_operands(both)
    outs = pl.pallas_call(
        body, name=name, in_specs=[ANY] * (nr + nw), out_specs=[ANY] * nw, out_shape=shapes,
        input_output_aliases={nr + q: q for q in range(nw)},
        scratch_shapes=[pltpu.SemaphoreType.DMA((both.ncopies, 2))],
    )(*operands)
    for q, n in enumerate(both.writes):
        plan.bufs[n] = outs[q]


def _mm_nn(a, w, *, layer, tm, tn, residual=None, out_dtype=F32, name, comm=None):
    T, K = a.shape
    if w.ndim == 4:
        _, S, _, n4 = w.shape
        N = S * n4
        bps = n4 // tn
        w_spec = pl.BlockSpec((None, None, K, tn), lambda j, i: (layer, j // bps, 0, j % bps))
    else:
        N = w.shape[2]
        w_spec = pl.BlockSpec((None, K, tn), lambda j, i: (layer, 0, j))
    in_specs = [pl.BlockSpec((tm, K), lambda j, i: (i, 0)), w_spec]
    args = [a, w]
    if residual is not None:
        in_specs.append(pl.BlockSpec((tm, tn), lambda j, i: (i, j)))
        args.append(residual)

    def body(*refs):
        a_ref, w_ref, o_ref = refs[0], refs[1], refs[-1]
        acc = jnp.dot(a_ref[...].astype(BF16), w_ref[...], preferred_element_type=F32)
        if residual is not None:
            acc = refs[2][...] + acc
        o_ref[...] = acc.astype(out_dtype)

    return _pallas(
        comm, body, name=name, grid=(N // tn, T // tm), in_specs=in_specs,
        out_specs=pl.BlockSpec((tm, tn), lambda j, i: (i, j)),
        out_shape=jax.ShapeDtypeStruct((T, N), out_dtype),
        compiler_params=_cparams(("parallel", "parallel"), 48),
    )(*args)


def _mm_nt(dy, w, *, layer, tm, tn, name, out_dtype=F32, comm=None):
    T = dy.shape[0]
    nt_dims = (((1,), (1,)), ((), ()))
    if w.ndim == 4:
        _, S, K, n4 = w.shape

        def body(dy_ref, w_ref, o_ref):
            @pl.when(pl.program_id(1) == 0)
            def _():
                o_ref[...] = jnp.zeros_like(o_ref)
            o_ref[...] += lax.dot_general(dy_ref[...].astype(BF16), w_ref[...], nt_dims,
                                          preferred_element_type=F32)

        return _pallas(
            comm, body, name=name, grid=(T // tm, S),
            in_specs=[pl.BlockSpec((tm, n4), lambda i, s: (i, s)),
                      pl.BlockSpec((None, None, K, n4), lambda i, s: (layer, s, 0, 0))],
            out_specs=pl.BlockSpec((tm, K), lambda i, s: (i, 0)),
            out_shape=jax.ShapeDtypeStruct((T, K), F32),
            compiler_params=_cparams(("parallel", "arbitrary"), 48),
        )(dy, w)
    _, R, N = w.shape

    def body2(dy_ref, w_ref, o_ref):
        o_ref[...] = lax.dot_general(dy_ref[...].astype(BF16), w_ref[...], nt_dims,
                                     preferred_element_type=F32).astype(out_dtype)

    return _pallas(
        comm, body2, name=name, grid=(R // tn, T // tm),
        in_specs=[pl.BlockSpec((tm, N), lambda j, i: (i, 0)),
                  pl.BlockSpec((None, tn, N), lambda j, i: (layer, j, 0))],
        out_specs=pl.BlockSpec((tm, tn), lambda j, i: (i, j)),
        out_shape=jax.ShapeDtypeStruct((T, R), out_dtype),
        compiler_params=_cparams(("parallel", "parallel"), 48),
    )(dy, w)


def _mm_tn(a, dy, *, shards, tk, tn, tt, name, comm=None):
    T, K = a.shape
    N = dy.shape[1]
    tn_dims = (((0,), (0,)), ((), ()))

    def body(a_ref, dy_ref, o_ref):
        @pl.when(pl.program_id(2) == 0)
        def _():
            o_ref[...] = jnp.zeros_like(o_ref)
        o_ref[...] += lax.dot_general(a_ref[...].astype(BF16), dy_ref[...].astype(BF16), tn_dims,
                                      preferred_element_type=F32)

    if shards is None:
        out_spec = pl.BlockSpec((tk, tn), lambda k, n, t: (k, n))
        out_shape = jax.ShapeDtypeStruct((K, N), F32)
    else:
        n4 = N // shards
        bps = n4 // tn
        out_spec = pl.BlockSpec((None, tk, tn), lambda k, n, t: (n // bps, k, n % bps))
        out_shape = jax.ShapeDtypeStruct((shards, K, n4), F32)
    return _pallas(
        comm, body, name=name, grid=(K // tk, N // tn, T // tt),
        in_specs=[pl.BlockSpec((tt, tk), lambda k, n, t: (t, k)),
                  pl.BlockSpec((tt, tn), lambda k, n, t: (t, n))],
        out_specs=out_spec, out_shape=out_shape,
        compiler_params=_cparams(("parallel", "parallel", "arbitrary"), 48),
    )(a, dy)


def _rmsnorm_fwd(x, g, *, layer, tm, name, comm=None):
    T, D = x.shape

    def body(x_ref, g_ref, h_ref):
        xf = x_ref[...]
        r = lax.rsqrt(jnp.mean(xf * xf, axis=-1, keepdims=True) + EPS)
        h_ref[...] = (xf * r * g_ref[...]).astype(BF16)

    return _pallas(
        comm, body, name=name, grid=(T // tm,),
        in_specs=[pl.BlockSpec((tm, D), lambda i: (i, 0)),
                  pl.BlockSpec((None, 1, D), lambda i: (layer, 0, 0))],
        out_specs=pl.BlockSpec((tm, D), lambda i: (i, 0)),
        out_shape=jax.ShapeDtypeStruct((T, D), BF16),
        compiler_params=_cparams(("parallel",), 32),
    )(x, g)


def _rmsnorm_bwd(x, g, dh, dres, *, layer, tm, name, comm=None):
    T, D = x.shape

    def body(x_ref, g_ref, dh_ref, dres_ref, dx_ref, dg_ref):
        @pl.when(pl.program_id(0) == 0)
        def _():
            dg_ref[...] = jnp.zeros_like(dg_ref)
        xf = x_ref[...]
        r = lax.rsqrt(jnp.mean(xf * xf, axis=-1, keepdims=True) + EPS)
        xh = xf * r
        dh = dh_ref[...]
        dg_ref[...] += _rowsum(dh * xh)
        dxh = dh * g_ref[...]
        dx_ref[...] = dres_ref[...] + r * (dxh - xh * jnp.mean(dxh * xh, axis=-1, keepdims=True))

    return _pallas(
        comm, body, name=name, grid=(T // tm,),
        in_specs=[pl.BlockSpec((tm, D), lambda i: (i, 0)),
                  pl.BlockSpec((None, 1, D), lambda i: (layer, 0, 0)),
                  pl.BlockSpec((tm, D), lambda i: (i, 0)),
                  pl.BlockSpec((tm, D), lambda i: (i, 0))],
        out_specs=[pl.BlockSpec((tm, D), lambda i: (i, 0)),
                   pl.BlockSpec((1, D), lambda i: (0, 0))],
        out_shape=[jax.ShapeDtypeStruct((T, D), F32), jax.ShapeDtypeStruct((1, D), F32)],
        compiler_params=_cparams(("arbitrary",), 40),
    )(x, g, dh, dres)


def _loss_head(x, tgt, g, *, tm, name, comm=None):
    T, D = x.shape

    def body(x_ref, t_ref, g_ref, loss_ref, dx_ref, dg_ref):
        @pl.when(pl.program_id(0) == 0)
        def _():
            dg_ref[...] = jnp.zeros_like(dg_ref)
            loss_ref[...] = jnp.zeros_like(loss_ref)
        xf = x_ref[...]
        gg = g_ref[...]
        r = lax.rsqrt(jnp.mean(xf * xf, axis=-1, keepdims=True) + EPS)
        xh = xf * r
        err = xh * gg - t_ref[...]
        row = jnp.mean(err * err, axis=-1, keepdims=True)
        loss_ref[...] += 0.5 * jnp.sum(row, axis=0, keepdims=True)
        dy = err * (1.0 / D)
        dg_ref[...] += _rowsum(dy * xh)
        dxh = dy * gg
        dx_ref[...] = r * (dxh - xh * jnp.mean(dxh * xh, axis=-1, keepdims=True))

    return _pallas(
        comm, body, name=name, grid=(T // tm,),
        in_specs=[pl.BlockSpec((tm, D), lambda i: (i, 0)),
                  pl.BlockSpec((tm, D), lambda i: (i, 0)),
                  pl.BlockSpec((1, D), lambda i: (0, 0))],
        out_specs=[pl.BlockSpec((1, 1), lambda i: (0, 0)),
                   pl.BlockSpec((tm, D), lambda i: (i, 0)),
                   pl.BlockSpec((1, D), lambda i: (0, 0))],
        out_shape=[jax.ShapeDtypeStruct((1, 1), F32), jax.ShapeDtypeStruct((T, D), F32),
                   jax.ShapeDtypeStruct((1, D), F32)],
        compiler_params=_cparams(("arbitrary",), 40),
    )(x, tgt, g)


CONV_ROWS = 64
CONV_COLS = 256


def _halo_prev_index(tm, halo):
    per = tm // halo
    return lambda i: jnp.maximum(i * per - 1, 0)


def _halo_next_index(tm, halo, total):
    per = tm // halo
    last = total // halo - 1
    return lambda i: jnp.minimum((i + 1) * per, last)


def _causal_mask():
    t = lax.broadcasted_iota(jnp.int32, (CHUNK, CHUNK), 0)
    s = lax.broadcasted_iota(jnp.int32, (CHUNK, CHUNK), 1)
    return s <= t


def _mixer_ab_fwd(z, a_ln_g, a_ln_b, w_s, b_s, conv_w, conv_b, b_ln_g, b_ln_b, *, tm, name, comm=None):
    T = z.shape[0]
    nchunk = tm // CHUNK
    halo = HALO_LONG

    def body(za_ref, zb_ref, zh_ref, alg_ref, alb_ref, ws_ref, bs_ref, cw_ref, cbias_ref,
             blg_ref, blb_ref, y_ref, cb_ref, ext_ref):
        i = pl.program_id(0)
        gu = _gelu(za_ref[:, :D_A].astype(F32))
        gv = _gelu(za_ref[:, D_A:].astype(F32))
        xh, _ = _ln_stats(gv)
        lv = (xh * alg_ref[...] + alb_ref[...]).astype(BF16)
        mask = _causal_mask()
        for h in range(A_HEADS):
            wm = jnp.where(mask, ws_ref[h], 0.0).astype(BF16)
            cols = slice(h * HEAD_DIM, (h + 1) * HEAD_DIM)
            for c in range(nchunk):
                rows = slice(c * CHUNK, (c + 1) * CHUNK)
                mixed = jnp.dot(wm, lv[rows, cols], preferred_element_type=F32) + bs_ref[h]
                y_ref[rows, cols] = (gu[rows, cols] * mixed).astype(BF16)
        ext_ref[halo:halo + tm, :] = zb_ref[:, :D_B].astype(F32) * _sigmoid(zb_ref[:, D_B:].astype(F32))
        prev = zh_ref[:, :D_B].astype(F32) * _sigmoid(zh_ref[:, D_B:].astype(F32))
        ext_ref[0:halo, :] = jnp.where(i > 0, prev, 0.0)
        for rb in range(tm // CONV_ROWS):
            for cb in range(D_B // CONV_COLS):
                cs = slice(cb * CONV_COLS, (cb + 1) * CONV_COLS)
                acc = jnp.zeros((CONV_ROWS, CONV_COLS), F32)
                for k in range(B_CONV):
                    off = rb * CONV_ROWS + halo - (B_CONV - 1) + k
                    acc = acc + cw_ref[k:k + 1, cs] * ext_ref[off:off + CONV_ROWS, cs]
                cb_ref[rb * CONV_ROWS:(rb + 1) * CONV_ROWS, cs] = acc + cbias_ref[:, cs]
        xhb, _ = _ln_stats(cb_ref[...])
        y_ref[:, D_A:] = _silu(xhb * blg_ref[...] + blb_ref[...]).astype(BF16)

    row = lambda i: (i, 0)
    par = lambda i: (0, 0)
    return _pallas(
        comm, body, name=name, grid=(T // tm,),
        in_specs=[pl.BlockSpec((tm, 2 * D_A), lambda i: (i, 0)),
                  pl.BlockSpec((tm, 2 * D_B), lambda i: (i, 1)),
                  pl.BlockSpec((halo, 2 * D_B), lambda i: (_halo_prev_index(tm, halo)(i), 1)),
                  pl.BlockSpec((1, D_A), par), pl.BlockSpec((1, D_A), par),
                  pl.BlockSpec((A_HEADS, CHUNK, CHUNK), lambda i: (0, 0, 0)),
                  pl.BlockSpec((A_HEADS, CHUNK, 1), lambda i: (0, 0, 0)),
                  pl.BlockSpec((B_CONV, D_B), par), pl.BlockSpec((1, D_B), par),
                  pl.BlockSpec((1, D_B), par), pl.BlockSpec((1, D_B), par)],
        out_specs=[pl.BlockSpec((tm, D_A + D_B), row), pl.BlockSpec((tm, D_B), row)],
        out_shape=[jax.ShapeDtypeStruct((T, D_A + D_B), BF16), jax.ShapeDtypeStruct((T, D_B), F32)],
        scratch_shapes=[pltpu.VMEM((halo + tm, D_B), F32)],
        compiler_params=_cparams(("parallel",), 40),
    )(z, z, z, a_ln_g, a_ln_b, w_s, b_s, conv_w, conv_b, b_ln_g, b_ln_b)


def _mixer_ab_bwd_pre(z, cb, dy, a_ln_g, a_ln_b, w_s, b_s, b_ln_g, b_ln_b, *, tm, name, comm=None):
    T = z.shape[0]
    nchunk = tm // CHUNK
    tn_dims = (((0,), (0,)), ((), ()))
    nt_dims = (((1,), (1,)), ((), ()))

    def body(za_ref, cb_ref, dy_ref, alg_ref, alb_ref, ws_ref, bs_ref, blg_ref, blb_ref,
             dza_ref, dcb_ref, dalg_ref, dalb_ref, dws_ref, dbs_ref, dblg_ref, dblb_ref,
             dlv_ref):
        @pl.when(pl.program_id(0) == 0)
        def _():
            for ref in (dalg_ref, dalb_ref, dws_ref, dbs_ref, dblg_ref, dblb_ref):
                ref[...] = jnp.zeros_like(ref)
        ua = za_ref[:, :D_A].astype(F32)
        va = za_ref[:, D_A:].astype(F32)
        gu = _gelu(ua)
        gv = _gelu(va)
        xh, r = _ln_stats(gv)
        alg = alg_ref[...]
        lv = (xh * alg + alb_ref[...]).astype(BF16)
        dya = dy_ref[:, :D_A].astype(F32)
        mask = _causal_mask()
        for h in range(A_HEADS):
            wm = jnp.where(mask, ws_ref[h], 0.0).astype(BF16)
            cols = slice(h * HEAD_DIM, (h + 1) * HEAD_DIM)
            dwm = jnp.zeros((CHUNK, CHUNK), F32)
            dbs = jnp.zeros((CHUNK, 1), F32)
            for c in range(nchunk):
                rows = slice(c * CHUNK, (c + 1) * CHUNK)
                lvb = lv[rows, cols]
                mixed = jnp.dot(wm, lvb, preferred_element_type=F32) + bs_ref[h]
                dyb = dya[rows, cols]
                dza_ref[rows, cols] = (dyb * mixed * _dgelu(ua[rows, cols])).astype(BF16)
                dmixed = dyb * gu[rows, cols]
                dmb = dmixed.astype(BF16)
                dlv_ref[rows, cols] = lax.dot_general(wm, dmb, tn_dims, preferred_element_type=F32)
                dwm = dwm + lax.dot_general(dmb, lvb, nt_dims, preferred_element_type=F32)
                dbs = dbs + jnp.sum(dmixed, axis=1, keepdims=True)
            dws_ref[h] += jnp.where(mask, dwm, 0.0)
            dbs_ref[h] += dbs
        dlv = dlv_ref[...]
        dalg_ref[...] += _rowsum(dlv * xh)
        dalb_ref[...] += _rowsum(dlv)
        dgv = _ln_bwd(dlv, xh, r, alg)
        dza_ref[:, D_A:] = (dgv * _dgelu(va)).astype(BF16)
        xhb, rb = _ln_stats(cb_ref[...])
        blg = blg_ref[...]
        lb = xhb * blg + blb_ref[...]
        dlb = dy_ref[:, D_A:].astype(F32) * _dsilu(lb)
        dblg_ref[...] += _rowsum(dlb * xhb)
        dblb_ref[...] += _rowsum(dlb)
        dcb_ref[...] = _ln_bwd(dlb, xhb, rb, blg)

    row = lambda i: (i, 0)
    par = lambda i: (0, 0)
    par3 = lambda i: (0, 0, 0)
    return _pallas(
        comm, body, name=name, grid=(T // tm,),
        in_specs=[pl.BlockSpec((tm, 2 * D_A), row), pl.BlockSpec((tm, D_B), row),
                  pl.BlockSpec((tm, D_A + D_B), row),
                  pl.BlockSpec((1, D_A), par), pl.BlockSpec((1, D_A), par),
                  pl.BlockSpec((A_HEADS, CHUNK, CHUNK), par3),
                  pl.BlockSpec((A_HEADS, CHUNK, 1), par3),
                  pl.BlockSpec((1, D_B), par), pl.BlockSpec((1, D_B), par)],
        out_specs=[pl.BlockSpec((tm, 2 * D_A), row), pl.BlockSpec((tm, D_B), row),
                   pl.BlockSpec((1, D_A), par), pl.BlockSpec((1, D_A), par),
                   pl.BlockSpec((A_HEADS, CHUNK, CHUNK), par3),
                   pl.BlockSpec((A_HEADS, CHUNK, 1), par3),
                   pl.BlockSpec((1, D_B), par), pl.BlockSpec((1, D_B), par)],
        out_shape=[jax.ShapeDtypeStruct((T, 2 * D_A), BF16), jax.ShapeDtypeStruct((T, D_B), F32),
                   jax.ShapeDtypeStruct((1, D_A), F32), jax.ShapeDtypeStruct((1, D_A), F32),
                   jax.ShapeDtypeStruct((A_HEADS, CHUNK, CHUNK), F32),
                   jax.ShapeDtypeStruct((A_HEADS, CHUNK, 1), F32),
                   jax.ShapeDtypeStruct((1, D_B), F32), jax.ShapeDtypeStruct((1, D_B), F32)],
        scratch_shapes=[pltpu.VMEM((tm, D_A), F32)],
        compiler_params=_cparams(("arbitrary",), 40),
    )(z, cb, dy, a_ln_g, a_ln_b, w_s, b_s, b_ln_g, b_ln_b)


def _mixer_b_conv_bwd(z, dcb, conv_w, *, tm, name, comm=None):
    T = z.shape[0]
    halo = HALO_LONG

    def body(zb_ref, dcb_ref, dcn_ref, cw_ref, dzb_ref, dcw_ref, dbias_ref, dext_ref):
        i = pl.program_id(0)
        last = pl.num_programs(0) - 1

        @pl.when(i == 0)
        def _():
            dcw_ref[...] = jnp.zeros_like(dcw_ref)
            dbias_ref[...] = jnp.zeros_like(dbias_ref)
        dcb = dcb_ref[...]
        dext_ref[0:tm, :] = dcb
        dext_ref[tm:tm + halo, :] = jnp.where(i < last, dcn_ref[...], 0.0)
        dbias_ref[...] += _rowsum(dcb)
        for rb in range(tm // CONV_ROWS):
            for cb in range(D_B // CONV_COLS):
                cs = slice(cb * CONV_COLS, (cb + 1) * CONV_COLS)
                gcs = slice(D_B + cb * CONV_COLS, D_B + (cb + 1) * CONV_COLS)
                rs = slice(rb * CONV_ROWS, (rb + 1) * CONV_ROWS)
                xbb = zb_ref[rs, cs].astype(F32)
                sgb = _sigmoid(zb_ref[rs, gcs].astype(F32))
                yb0 = xbb * sgb
                acc = jnp.zeros((CONV_ROWS, CONV_COLS), F32)
                for k in range(B_CONV):
                    off = rb * CONV_ROWS + (B_CONV - 1) - k
                    shifted = dext_ref[off:off + CONV_ROWS, cs]
                    acc = acc + cw_ref[k:k + 1, cs] * shifted
                    dcw_ref[k:k + 1, cs] += _rowsum(shifted * yb0)
                dzb_ref[rs, cs] = (acc * sgb).astype(BF16)
                dzb_ref[rs, gcs] = (acc * xbb * sgb * (1.0 - sgb)).astype(BF16)

    row = lambda i: (i, 0)
    par = lambda i: (0, 0)
    return _pallas(
        comm, body, name=name, grid=(T // tm,),
        in_specs=[pl.BlockSpec((tm, 2 * D_B), lambda i: (i, 1)),
                  pl.BlockSpec((tm, D_B), row),
                  pl.BlockSpec((halo, D_B), lambda i: (_halo_next_index(tm, halo, T)(i), 0)),
                  pl.BlockSpec((B_CONV, D_B), par)],
        out_specs=[pl.BlockSpec((tm, 2 * D_B), row), pl.BlockSpec((B_CONV, D_B), par),
                   pl.BlockSpec((1, D_B), par)],
        out_shape=[jax.ShapeDtypeStruct((T, 2 * D_B), BF16), jax.ShapeDtypeStruct((B_CONV, D_B), F32),
                   jax.ShapeDtypeStruct((1, D_B), F32)],
        scratch_shapes=[pltpu.VMEM((tm + halo, D_B), F32)],
        compiler_params=_cparams(("arbitrary",), 40),
    )(z, dcb, dcb, conv_w)


def _conv3(w_ref, ext_ref, base, rows, cs):
    acc = w_ref[0:1, cs] * ext_ref[base - 2:base - 2 + rows, cs]
    acc = acc + w_ref[1:2, cs] * ext_ref[base - 1:base - 1 + rows, cs]
    return acc + w_ref[2:3, cs] * ext_ref[base:base + rows, cs]


def _conv3_t(w_ref, ext_ref, base, rows, cs):
    acc = w_ref[0:1, cs] * ext_ref[base + 2:base + 2 + rows, cs]
    acc = acc + w_ref[1:2, cs] * ext_ref[base + 1:base + 1 + rows, cs]
    return acc + w_ref[2:3, cs] * ext_ref[base:base + rows, cs]


def _mixer_c_fwd(z, conv_w, *, tm, name, comm=None):
    T = z.shape[0]
    D = D_MODEL
    halo = HALO_SHORT
    full = slice(0, D)

    def body(bg_ref, cg_ref, xv_ref, cgh_ref, xvh_ref, w_ref, r_ref, ext_ref):
        i = pl.program_id(0)
        ext_ref[halo:halo + tm, :] = cg_ref[...].astype(F32) * xv_ref[...].astype(F32)
        ext_ref[0:halo, :] = jnp.where(i > 0, cgh_ref[...].astype(F32) * xvh_ref[...].astype(F32), 0.0)
        q = _conv3(w_ref, ext_ref, halo, tm, full)
        r_ref[...] = (bg_ref[...].astype(F32) * q).astype(BF16)

    hp = _halo_prev_index(tm, halo)
    return _pallas(
        comm, body, name=name, grid=(T // tm,),
        in_specs=[pl.BlockSpec((tm, D), lambda i: (i, 0)), pl.BlockSpec((tm, D), lambda i: (i, 1)),
                  pl.BlockSpec((tm, D), lambda i: (i, 2)),
                  pl.BlockSpec((halo, D), lambda i: (hp(i), 1)),
                  pl.BlockSpec((halo, D), lambda i: (hp(i), 2)),
                  pl.BlockSpec((None, C_CONV, D), lambda i: (0, 0, 0))],
        out_specs=pl.BlockSpec((tm, D), lambda i: (i, 0)),
        out_shape=jax.ShapeDtypeStruct((T, D), BF16),
        scratch_shapes=[pltpu.VMEM((halo + tm, D), F32)],
        compiler_params=_cparams(("parallel",), 40),
    )(z, z, z, z, z, conv_w)


def _mixer_c_bwd(z, dr, conv_w, *, tm, name, comm=None):
    T = z.shape[0]
    D = D_MODEL
    halo = HALO_SHORT
    full = slice(0, D)

    def body(bg_ref, cg_ref, xv_ref, cgh_ref, xvh_ref, bgn_ref, dr_ref, drn_ref, w_ref,
             dz_ref, dw_ref, ext_ref, dext_ref):
        i = pl.program_id(0)
        last = pl.num_programs(0) - 1

        @pl.when(i == 0)
        def _():
            dw_ref[...] = jnp.zeros_like(dw_ref)
        cg = cg_ref[...].astype(F32)
        xv = xv_ref[...].astype(F32)
        dr = dr_ref[...].astype(F32)
        p = cg * xv
        ext_ref[halo:halo + tm, :] = p
        ext_ref[0:halo, :] = jnp.where(i > 0, cgh_ref[...].astype(F32) * xvh_ref[...].astype(F32), 0.0)
        dext_ref[0:tm, :] = dr * bg_ref[...].astype(F32)
        dext_ref[tm:tm + halo, :] = jnp.where(i < last, drn_ref[...].astype(F32) * bgn_ref[...].astype(F32), 0.0)
        q = _conv3(w_ref, ext_ref, halo, tm, full)
        dz_ref[:, 0:D] = (dr * q).astype(BF16)
        dp = jnp.zeros((tm, D), F32)
        for k in range(C_CONV):
            shifted = dext_ref[2 - k:2 - k + tm, :]
            dp = dp + w_ref[k:k + 1, :] * shifted
            dw_ref[k:k + 1, :] += _rowsum(shifted * p)
        dz_ref[:, D:2 * D] = (dp * xv).astype(BF16)
        dz_ref[:, 2 * D:3 * D] = (dp * cg).astype(BF16)

    hp = _halo_prev_index(tm, halo)
    hn = _halo_next_index(tm, halo, T)
    return _pallas(
        comm, body, name=name, grid=(T // tm,),
        in_specs=[pl.BlockSpec((tm, D), lambda i: (i, 0)), pl.BlockSpec((tm, D), lambda i: (i, 1)),
                  pl.BlockSpec((tm, D), lambda i: (i, 2)),
                  pl.BlockSpec((halo, D), lambda i: (hp(i), 1)),
                  pl.BlockSpec((halo, D), lambda i: (hp(i), 2)),
                  pl.BlockSpec((halo, D), lambda i: (hn(i), 0)),
                  pl.BlockSpec((tm, D), lambda i: (i, 0)),
                  pl.BlockSpec((halo, D), lambda i: (hn(i), 0)),
                  pl.BlockSpec((None, C_CONV, D), lambda i: (0, 0, 0))],
        out_specs=[pl.BlockSpec((tm, 3 * D), lambda i: (i, 0)),
                   pl.BlockSpec((C_CONV, D), lambda i: (0, 0))],
        out_shape=[jax.ShapeDtypeStruct((T, 3 * D), BF16), jax.ShapeDtypeStruct((C_CONV, D), F32)],
        scratch_shapes=[pltpu.VMEM((halo + tm, D), F32), pltpu.VMEM((tm + halo, D), F32)],
        compiler_params=_cparams(("arbitrary",), 48),
    )(z, z, z, z, z, z, dr, dr, conv_w)


FFN_COLS = 256


def _ffn_act_fwd(up, conv_w, *, layer, tm, name, comm=None):
    T = up.shape[0]
    halo = HALO_SHORT
    W = FFN_COLS

    def body(up_ref, uph_ref, w_ref, a_ref, upc_ref, ext_ref):
        i = pl.program_id(0)
        ext_ref[halo:halo + tm, :] = up_ref[...].astype(F32)
        ext_ref[0:halo, :] = jnp.where(i > 0, uph_ref[...].astype(F32), 0.0)
        for cb in range(D_FF // W):
            gs = slice(cb * W, (cb + 1) * W)
            vs = slice(D_FF + cb * W, D_FF + (cb + 1) * W)
            g = _conv3(w_ref, ext_ref, halo, tm, gs)
            v = _conv3(w_ref, ext_ref, halo, tm, vs)
            upc_ref[:, gs] = g.astype(BF16)
            upc_ref[:, vs] = v.astype(BF16)
            a_ref[:, gs] = (_silu(g) * v).astype(BF16)

    return _pallas(
        comm, body, name=name, grid=(T // tm,),
        in_specs=[pl.BlockSpec((tm, 2 * D_FF), lambda i: (i, 0)),
                  pl.BlockSpec((halo, 2 * D_FF), lambda i: (_halo_prev_index(tm, halo)(i), 0)),
                  pl.BlockSpec((None, F_CONV, 2 * D_FF), lambda i: (layer, 0, 0))],
        out_specs=[pl.BlockSpec((tm, D_FF), lambda i: (i, 0)),
                   pl.BlockSpec((tm, 2 * D_FF), lambda i: (i, 0))],
        out_shape=[jax.ShapeDtypeStruct((T, D_FF), BF16), jax.ShapeDtypeStruct((T, 2 * D_FF), BF16)],
        scratch_shapes=[pltpu.VMEM((halo + tm, 2 * D_FF), F32)],
        compiler_params=_cparams(("parallel",), 48),
    )(up, up, conv_w)


def _ffn_act_bwd(up, upc, da, conv_w, *, layer, tm, name, comm=None):
    T = up.shape[0]
    halo = HALO_SHORT
    W = FFN_COLS

    def body(up_ref, upc_ref, upcn_ref, da_ref, dan_ref, w_ref, dup_ref, dw_ref, dext_ref):
        i = pl.program_id(0)
        last = pl.num_programs(0) - 1

        @pl.when(i == 0)
        def _():
            dw_ref[...] = jnp.zeros_like(dw_ref)
        live = jnp.where(i < last, 1.0, 0.0)
        for cb in range(D_FF // W):
            gs = slice(cb * W, (cb + 1) * W)
            vs = slice(D_FF + cb * W, D_FF + (cb + 1) * W)
            g = jnp.concatenate([upc_ref[:, gs], upcn_ref[:, gs]], axis=0).astype(F32)
            v = jnp.concatenate([upc_ref[:, vs], upcn_ref[:, vs]], axis=0).astype(F32)
            da = jnp.concatenate([da_ref[:, gs].astype(F32), dan_ref[:, gs].astype(F32) * live], axis=0)
            s = _sigmoid(g)
            silu = g * s
            dext_ref[:, gs] = da * v * (s * (1.0 + g * (1.0 - s)))
            dext_ref[:, vs] = da * silu
            for cs in (gs, vs):
                u = up_ref[:, cs].astype(F32)
                acc = jnp.zeros((tm, W), F32)
                for k in range(F_CONV):
                    shifted = dext_ref[2 - k:2 - k + tm, cs]
                    acc = acc + w_ref[k:k + 1, cs] * shifted
                    dw_ref[k:k + 1, cs] += _rowsum(shifted * u)
                dup_ref[:, cs] = acc.astype(BF16)

    hn = _halo_next_index(tm, halo, T)
    return _pallas(
        comm, body, name=name, grid=(T // tm,),
        in_specs=[pl.BlockSpec((tm, 2 * D_FF), lambda i: (i, 0)),
                  pl.BlockSpec((tm, 2 * D_FF), lambda i: (i, 0)),
                  pl.BlockSpec((halo, 2 * D_FF), lambda i: (hn(i), 0)),
                  pl.BlockSpec((tm, D_FF), lambda i: (i, 0)),
                  pl.BlockSpec((halo, D_FF), lambda i: (hn(i), 0)),
                  pl.BlockSpec((None, F_CONV, 2 * D_FF), lambda i: (layer, 0, 0))],
        out_specs=[pl.BlockSpec((tm, 2 * D_FF), lambda i: (i, 0)),
                   pl.BlockSpec((F_CONV, 2 * D_FF), lambda i: (0, 0))],
        out_shape=[jax.ShapeDtypeStruct((T, 2 * D_FF), BF16),
                   jax.ShapeDtypeStruct((F_CONV, 2 * D_FF), F32)],
        scratch_shapes=[pltpu.VMEM((tm + halo, 2 * D_FF), F32)],
        compiler_params=_cparams(("arbitrary",), 56),
    )(up, upc, upc, da, da, conv_w)


def _local_step(x, tgt, small, plan):
    T = x.shape[0]
    tm_e = _pick(T, 256)
    tm_n = _pick(T, 512)
    tm = _pick(T, 1024)
    tm_r = _pick(T, 2048)
    tt = _pick(T, 512)
    nm = small["norm_mix"].reshape(2, 1, D_MODEL)
    nf = small["norm_ffn"].reshape(2, 1, D_MODEL)
    ngf = small["norm_final"].reshape(1, D_MODEL)
    b_s = small["a_b_s"].reshape(A_HEADS, CHUNK, 1)
    w_s = small["a_w_s"].reshape(A_HEADS, CHUNK, CHUNK)
    b_conv_w = small["b_conv_w"].reshape(B_CONV, D_B)
    sg = {}
    wt, cm = plan.weight, plan.comm

    h_m0 = _rmsnorm_fwd(x, nm, layer=0, tm=tm_n, name="norm_mix0")
    z_ab = _mm_nn(h_m0, wt("ab_w_in", 0), layer=0, tm=tm, tn=512, out_dtype=BF16, name="ab_in", comm=cm("ab_in"))
    yab, cb = _mixer_ab_fwd(z_ab, small["a_ln_g"], small["a_ln_b"], w_s, b_s, b_conv_w, small["b_conv_b"],
                            small["b_ln_g"], small["b_ln_b"], tm=tm_e, name="mixer_ab", comm=cm("mixer_ab"))
    x1 = _mm_nn(yab, wt("ab_w_out", 0), layer=0, tm=tm, tn=512, residual=x, name="ab_out", comm=cm("ab_out"))

    def ffn_fwd(xin, layer):
        h = _rmsnorm_fwd(xin, nf, layer=layer, tm=tm_n, name=f"norm_ffn{layer}")
        up = _mm_nn(h, wt("f_w_up", layer), layer=0, tm=tm, tn=1408, out_dtype=BF16, name=f"ffn_up{layer}",
                    comm=cm(f"ffn_up{layer}"))
        a, upc = _ffn_act_fwd(up, small["f_conv_w"], layer=layer, tm=tm_e, name=f"ffn_act{layer}",
                              comm=cm(f"ffn_act{layer}"))
        xout = _mm_nn(a, wt("f_w_down", layer), layer=0, tm=tm, tn=512, residual=xin, name=f"ffn_down{layer}",
                      comm=cm(f"ffn_down{layer}"))
        return h, up, upc, a, xout

    h_f0, up0, upc0, a0, x2 = ffn_fwd(x1, 0)
    h_m1 = _rmsnorm_fwd(x2, nm, layer=1, tm=tm_n, name="norm_mix1")
    z_c = _mm_nn(h_m1, wt("c_w_in", 0), layer=0, tm=tm, tn=768, out_dtype=BF16, name="c_in", comm=cm("c_in"))
    r = _mixer_c_fwd(z_c, small["c_conv_w"], tm=tm_e, name="mixer_c", comm=cm("mixer_c"))
    x3 = _mm_nn(r, wt("c_w_out", 0), layer=0, tm=tm, tn=512, residual=x2, name="c_out", comm=cm("c_out"))
    h_f1, up1, upc1, a1, x4 = ffn_fwd(x3, 1)
    loss, dx, sg["norm_final"] = _loss_head(x4, tgt, ngf, tm=tm_n, name="loss_head")

    def ffn_bwd(dx, xin, h, up, upc, a, layer):
        da = _mm_nt(dx, wt("f_w_down", layer), layer=0, tm=tm, tn=1408, out_dtype=BF16,
                    name=f"ffn_down_dx{layer}", comm=cm(f"ffn_down_dx{layer}"))
        plan.grad_ready("f_w_down", layer, _mm_tn(a, dx, shards=None, tk=1408, tn=1024, tt=tt,
                                                  name=f"ffn_down_dw{layer}", comm=cm(f"ffn_down_dw{layer}")))
        dup, dcw = _ffn_act_bwd(up, upc, da, small["f_conv_w"], layer=layer, tm=tm_e, name=f"ffn_act_bwd{layer}",
                                comm=cm(f"ffn_act_bwd{layer}"))
        dh = _mm_nt(dup, wt("f_w_up", layer), layer=0, tm=tm_r, tn=None, name=f"ffn_up_dx{layer}",
                    comm=cm(f"ffn_up_dx{layer}"))
        plan.grad_ready("f_w_up", layer, _mm_tn(h, dup, shards=N_CHIPS, tk=1024, tn=1408, tt=tt,
                                                name=f"ffn_up_dw{layer}", comm=cm(f"ffn_up_dw{layer}")))
        dxin, dg = _rmsnorm_bwd(xin, nf, dh, dx, layer=layer, tm=tm_n, name=f"norm_ffn_bwd{layer}",
                                comm=cm(f"norm_ffn_bwd{layer}"))
        return dxin, dg, dcw

    dx, dnf1, dfc1 = ffn_bwd(dx, x3, h_f1, up1, upc1, a1, 1)
    dr = _mm_nt(dx, wt("c_w_out", 0), layer=0, tm=tm, tn=512, out_dtype=BF16, name="c_out_dx", comm=cm("c_out_dx"))
    plan.grad_ready("c_w_out", 0, _mm_tn(r, dx, shards=None, tk=1024, tn=1024, tt=tt, name="c_out_dw",
                                         comm=cm("c_out_dw")))
    dz_c, dccw = _mixer_c_bwd(z_c, dr, small["c_conv_w"], tm=tm_e, name="mixer_c_bwd", comm=cm("mixer_c_bwd"))
    sg["c_conv_w"] = dccw.reshape(1, C_CONV, D_MODEL)
    dh = _mm_nt(dz_c, wt("c_w_in", 0), layer=0, tm=tm_r, tn=None, name="c_in_dx", comm=cm("c_in_dx"))
    plan.grad_ready("c_w_in", 0, _mm_tn(h_m1, dz_c, shards=N_CHIPS, tk=1024, tn=768, tt=tt, name="c_in_dw",
                                        comm=cm("c_in_dw")))
    dx, dnm1 = _rmsnorm_bwd(x2, nm, dh, dx, layer=1, tm=tm_n, name="norm_mix_bwd1", comm=cm("norm_mix_bwd1"))
    dx, dnf0, dfc0 = ffn_bwd(dx, x1, h_f0, up0, upc0, a0, 0)
    dyab = _mm_nt(dx, wt("ab_w_out", 0), layer=0, tm=tm, tn=512, out_dtype=BF16, name="ab_out_dx",
                  comm=cm("ab_out_dx"))
    plan.grad_ready("ab_w_out", 0, _mm_tn(yab, dx, shards=None, tk=1024, tn=1024, tt=tt, name="ab_out_dw",
                                          comm=cm("ab_out_dw")))
    (dza, dcb, sg["a_ln_g"], sg["a_ln_b"], dws, dbs, sg["b_ln_g"], sg["b_ln_b"]) = _mixer_ab_bwd_pre(
        z_ab, cb, dyab, small["a_ln_g"], small["a_ln_b"], w_s, b_s, small["b_ln_g"], small["b_ln_b"],
        tm=tm_e, name="mixer_ab_bwd", comm=cm("mixer_ab_bwd"))
    dzb, dbcw, sg["b_conv_b"] = _mixer_b_conv_bwd(z_ab, dcb, b_conv_w, tm=tm_e, name="mixer_b_conv_bwd",
                                                  comm=cm("mixer_b_conv_bwd"))
    sg["a_w_s"] = dws.reshape(1, A_HEADS, CHUNK, CHUNK)
    sg["a_b_s"] = dbs.reshape(1, A_HEADS, CHUNK)
    sg["b_conv_w"] = dbcw.reshape(1, B_CONV, D_B)
    dz_ab = jnp.concatenate([dza, dzb], axis=1)
    dh = _mm_nt(dz_ab, wt("ab_w_in", 0), layer=0, tm=tm_r, tn=None, name="ab_in_dx", comm=cm("ab_in_dx"))
    plan.grad_ready("ab_w_in", 0, _mm_tn(h_m0, dz_ab, shards=N_CHIPS, tk=1024, tn=512, tt=tt, name="ab_in_dw",
                                         comm=cm("ab_in_dw")))
    dx, dnm0 = _rmsnorm_bwd(x, nm, dh, dx, layer=0, tm=tm_n, name="norm_mix_bwd0", comm=cm("norm_mix_bwd0"))

    sg["norm_mix"] = [dnm0, dnm1]
    sg["norm_ffn"] = [dnf0, dnf1]
    sg["f_conv_w"] = [dfc0, dfc1]
    return loss, dx, sg


BLOCK_BYTES = 3 * 1024 * 1024


BF16_SUBLANES = 16


def _row_tile(rows, row_bytes, step=SUBLANES):
    best = None
    for tr in range(step, rows + 1, step):
        if rows % tr == 0 and tr * row_bytes <= BLOCK_BYTES:
            best = tr
    if best is None:
        raise ValueError(f"no row tile for {rows}")
    return best


def _place_scalars():
    x, y, c = lax.axis_index("x"), lax.axis_index("y"), lax.axis_index("c")
    return jnp.stack([c, 2 * x + y, 2 * (1 - x) + y, 2 * x + (1 - y), 2 * (1 - x) + (1 - y)]).astype(jnp.int32)


def _cast_into_slot(w, place, *, layer, name):
    L, rows, cols = w.shape
    tr = _row_tile(rows, cols * 4, BF16_SUBLANES)

    def body(place_ref, w_ref, o_ref):
        o_ref[...] = w_ref[...].astype(BF16)

    return pl.pallas_call(
        body, name=name,
        grid_spec=pltpu.PrefetchScalarGridSpec(
            num_scalar_prefetch=1, grid=(rows // tr,),
            in_specs=[pl.BlockSpec((None, tr, cols), lambda i, p: (layer, i, 0))],
            out_specs=pl.BlockSpec((None, None, tr, cols), lambda i, p: (0, p[1], i, 0))),
        out_shape=jax.ShapeDtypeStruct((1, N_CHIPS, rows, cols), BF16),
        compiler_params=_cparams(("parallel",), 32),
    )(place, w)


def _pair_sum(g, theirs, place, *, name):
    S, rows, cols = g.shape
    half = rows // 2
    tr = _row_tile(half, cols * 4, BF16_SUBLANES)
    nb = half // tr

    def body(place_ref, g_ref, t_ref, o_ref):
        o_ref[...] = (g_ref[...] + t_ref[...]).astype(BF16)

    spec = pl.BlockSpec((None, tr, cols), lambda s, i, p: (s, i, 0))
    return pl.pallas_call(
        body, name=name,
        grid_spec=pltpu.PrefetchScalarGridSpec(
            num_scalar_prefetch=1, grid=(S, nb),
            in_specs=[pl.BlockSpec((None, tr, cols), lambda s, i, p: (s, p[0] * nb + i, 0)), spec],
            out_specs=spec),
        out_shape=jax.ShapeDtypeStruct((S, half, cols), BF16),
        compiler_params=_cparams(("parallel", "parallel"), 32),
    )(place, g, theirs)


def _chip_sum(p, r, g_prev, place, *, layer, shape, name):
    L, rows, cols = shape
    half = rows // 2
    tr = _row_tile(half, cols * 4, BF16_SUBLANES)
    nb = half // tr

    def body(place_ref, p_ref, r_ref, *rest):
        o_ref = rest[-1]
        mine = p_ref[...].astype(F32)
        peers = [r_ref[j].astype(F32) for j in range(3)]
        acc = None
        for s in range(N_CHIPS):
            term = jnp.where(place_ref[1] == s, mine,
                             jnp.where(place_ref[2] == s, peers[0],
                                       jnp.where(place_ref[3] == s, peers[1], peers[2])))
            acc = term if acc is None else acc + term
        o_ref[...] = acc

    in_specs = [pl.BlockSpec((None, tr, cols), lambda i, pr: (pr[1], i, 0)),
                pl.BlockSpec((3, tr, cols), lambda i, pr: (0, i, 0))]
    args = [place, p, r]
    aliases = {}
    if g_prev is not None:
        in_specs.append(ANY)
        args.append(g_prev)
        aliases = {3: 0}
    return pl.pallas_call(
        body, name=name,
        grid_spec=pltpu.PrefetchScalarGridSpec(
            num_scalar_prefetch=1, grid=(nb,), in_specs=in_specs,
            out_specs=pl.BlockSpec((None, tr, cols), lambda i, pr: (layer, pr[0] * nb + i, 0))),
        out_shape=jax.ShapeDtypeStruct(shape, F32), input_output_aliases=aliases,
        compiler_params=_cparams(("parallel",), 32),
    )(*args)


def _adamw_math(w, g, m, v):
    m2 = ADAM_B1 * m + (1.0 - ADAM_B1) * g
    v2 = ADAM_B2 * v + (1.0 - ADAM_B2) * (g * g)
    m_hat = m2 / (1.0 - ADAM_B1 ** ADAM_STEP)
    v_hat = v2 / (1.0 - ADAM_B2 ** ADAM_STEP)
    delta = -ADAM_LR * (m_hat / (jnp.sqrt(v_hat) + ADAM_EPS) + ADAM_WD * w)
    return delta, m2, v2


def _adamw(w, g, m, v, *, name):
    L, rows, cols = w.shape
    tr = _row_tile(rows, cols * 4)

    def body(w_ref, g_ref, m_ref, v_ref, d_ref, m2_ref, v2_ref):
        d, m2, v2 = _adamw_math(w_ref[...], g_ref[...], m_ref[...], v_ref[...])
        d_ref[...] = d
        m2_ref[...] = m2
        v2_ref[...] = v2

    spec = pl.BlockSpec((None, tr, cols), lambda l, i: (l, i, 0))
    shape = jax.ShapeDtypeStruct(w.shape, F32)
    return pl.pallas_call(
        body, name=name, grid=(L, rows // tr), in_specs=[spec] * 4, out_specs=[spec] * 3,
        out_shape=[shape] * 3,
        compiler_params=_cparams(("parallel", "parallel"), 48),
    )(w, g, m, v)


def _exchange_packs(pack, *, reduce, name):
    R = pack.shape[0]
    ndev = 2 * N_CHIPS

    def body(p_ref, o_ref, *scratch):
        if reduce:
            buf, send, recv = scratch
        else:
            buf = o_ref
            send, recv = scratch
        x, y, c = lax.axis_index("x"), lax.axis_index("y"), lax.axis_index("c")
        me = 4 * x + 2 * y + c
        buf[me] = p_ref[...]
        sends = []
        for q in range(1, ndev):
            qx, qy, qc = (q >> 2) & 1, (q >> 1) & 1, q & 1
            peer = (x ^ qx, y ^ qy, c ^ qc)
            rc = _remote(p_ref, buf.at[me], send.at[q - 1], recv.at[q - 1], peer)
            rc.start()
            sends.append(rc)
        for q in range(1, ndev):
            qx, qy, qc = (q >> 2) & 1, (q >> 1) & 1, q & 1
            slot = buf.at[4 * (x ^ qx) + 2 * (y ^ qy) + (c ^ qc)]
            _remote(slot, slot, send.at[q - 1], recv.at[q - 1], (x ^ qx, y ^ qy, c ^ qc)).wait_recv()
        for rc in sends:
            rc.wait_send()
        if reduce:
            acc = buf[0]
            for d in range(1, ndev):
                acc = acc + buf[d]
            o_ref[...] = acc

    vm = pl.BlockSpec(memory_space=pltpu.VMEM)
    sems = [pltpu.SemaphoreType.DMA((ndev - 1,)), pltpu.SemaphoreType.DMA((ndev - 1,))]
    if reduce:
        out_shape = jax.ShapeDtypeStruct((R, LANES), F32)
        scratch = [pltpu.VMEM((ndev, R, LANES), F32)] + sems
    else:
        out_shape = jax.ShapeDtypeStruct((ndev, R, LANES), F32)
        scratch = sems
    return pl.pallas_call(
        body, name=name, in_specs=[vm], out_specs=vm, out_shape=out_shape, scratch_shapes=scratch,
        compiler_params=pltpu.CompilerParams(vmem_limit_bytes=VMEM_BYTES_MAX),
    )(pack)


PACK_UNIT = SUBLANES * LANES


def _pack(arrays):
    flat, sizes = [], []
    for a in arrays:
        pieces = a if isinstance(a, (list, tuple)) else [a]
        v = jnp.concatenate([p.reshape(-1) for p in pieces]) if len(pieces) > 1 else pieces[0].reshape(-1)
        size = v.shape[0]
        padded = -(-size // PACK_UNIT) * PACK_UNIT
        flat.append(jnp.pad(v, (0, padded - size)))
        sizes.append((size, padded))
    return jnp.concatenate(flat).reshape(-1, LANES), sizes


def _unpack(pack, sizes, shapes):
    v = pack.reshape(-1)
    out, off = [], 0
    for (size, padded), shape in zip(sizes, shapes):
        out.append(v[off:off + size].reshape(shape))
        off += padded
    return out


BIG = ("ab_w_in", "ab_w_out", "c_w_in", "c_w_out", "f_w_up", "f_w_down")
COL_SHARDED = ("ab_w_in", "c_w_in", "f_w_up")
SMALL_REPLICATED = ("norm_mix", "norm_ffn", "norm_final", "a_ln_g", "a_ln_b", "a_w_s", "a_b_s",
                    "b_conv_b", "b_ln_g", "b_ln_b")
SMALL_SHARDED = ("b_conv_w", "c_conv_w", "f_conv_w")
SMALL = SMALL_REPLICATED + SMALL_SHARDED
ALL_WEIGHTS = ("norm_mix", "norm_ffn", "norm_final", "ab_w_in", "a_ln_g", "a_ln_b", "a_w_s", "a_b_s",
               "b_conv_w", "b_conv_b", "b_ln_g", "b_ln_b", "ab_w_out", "c_w_in", "c_conv_w", "c_w_out",
               "f_w_up", "f_conv_w", "f_w_down")


SCHEDULE = {
    "mixer_ab": [("gi", "f_w_up", 0)],
    "ab_out": [("gd", "f_w_up", 0)],
    "ffn_up0": [("gi", "f_w_down", 0), ("gi", "c_w_in", 0)],
    "ffn_act0": [("gd", "f_w_down", 0), ("gd", "c_w_in", 0), ("gi", "f_w_up", 1)],
    "ffn_down0": [("gd", "f_w_up", 1), ("gi", "c_w_out", 0)],
    "c_in": [("gd", "c_w_out", 0), ("gi", "f_w_down", 1)],
    "mixer_c": [("gd", "f_w_down", 1)],
    "ffn_act_bwd1": [("px", "f_w_down", 1)],
    "ffn_up_dx1": [("cx", "f_w_down", 1, 0, 1)],
    "norm_ffn_bwd1": [("px", "f_w_up", 1)],
    "mixer_c_bwd": [("cx", "f_w_up", 1, 0, 2), ("px", "c_w_out", 0)],
    "c_in_dx": [("cx", "c_w_out", 0, 0, 1)],
    "c_in_dw": [("cx", "f_w_up", 1, 1, 2)],
    "norm_mix_bwd1": [("px", "c_w_in", 0)],
    "ffn_act_bwd0": [("cx", "c_w_in", 0, 0, 1), ("px", "f_w_down", 0)],
    "ffn_up_dx0": [("cx", "f_w_down", 0, 0, 1)],
    "norm_ffn_bwd0": [("px", "f_w_up", 0)],
    "mixer_ab_bwd": [("px", "ab_w_out", 0)],
    "mixer_b_conv_bwd": [("cx", "f_w_up", 0, 0, 1)],
    "ab_in_dx": [("cx", "ab_w_out", 0, 0, 1)],
    "norm_mix_bwd0": [("px", "ab_w_in", 0)],
}


class _Plan:
    def __init__(self, shapes, place):
        self.shapes, self.place, self.bufs = shapes, place, {}

    def weight(self, name, layer):
        g = self.bufs[f"w:{name}:{layer}"]
        if name in COL_SHARDED:
            return g
        _, S, rows, cols = g.shape
        return g.reshape(1, S * rows, cols)

    def grad_ready(self, name, layer, g):
        _, rows, cols = self.shapes[name]
        hbm = lambda a: pltpu.with_memory_space_constraint(a, pltpu.HBM)
        self.bufs[f"g:{name}:{layer}"] = g.reshape(N_CHIPS, rows, cols)
        self.bufs[f"t:{name}:{layer}"] = hbm(lax.empty((N_CHIPS, rows // 2, cols), F32))
        self.bufs[f"l:{name}:{layer}"] = hbm(lax.empty((3, rows // 2, cols), BF16))

    def job(self, kind, name, layer, part=0, parts=1):
        _, rows, cols = self.shapes[name]
        key = f"{name}:{layer}"
        if kind == "gi":
            return _job_gather_ici("w:" + key, rows)
        if kind == "gd":
            return _job_gather_d2d("w:" + key, rows)
        if kind == "px":
            return _job_pair_exchange("g:" + key, "t:" + key, rows)
        if kind == "cx":
            if "p:" + key not in self.bufs:
                self.bufs["p:" + key] = _pair_sum(self.bufs["g:" + key], self.bufs["t:" + key], self.place,
                                                  name=f"pair_sum_{name}{layer}")
            nr = rows // 2 // parts
            return _job_chip_exchange("p:" + key, "l:" + key, part * nr, nr)
        if kind == "ps":
            return _job_pair_share("G:" + name, layer, rows)
        raise ValueError(kind)

    def comm(self, call):
        specs = SCHEDULE.get(call)
        return None if specs is None else _Comm(self, [self.job(*spec) for spec in specs])


def _step(x, tgt, w, m, v):
    chip = 2 * lax.axis_index("x") + lax.axis_index("y")
    place = _place_scalars()
    plan = _Plan({n: w[n].shape for n in BIG}, place)
    items = [(n, l) for n in BIG for l in range(w[n].shape[0])]

    for n, l in items:
        plan.bufs[f"w:{n}:{l}"] = _cast_into_slot(w[n], place, layer=l, name=f"cast_{n}{l}")
    first = [("ab_w_in", 0), ("ab_w_out", 0)]
    _comm_only(plan, [[plan.job("gi", n, l) for n, l in first], [plan.job("gd", n, l) for n, l in first]],
               name="gather_first")
    conv_pack, conv_sizes = _pack([w[n] for n in SMALL_SHARDED])
    conv_all = _exchange_packs(conv_pack, reduce=False, name="gather_conv_weights")
    conv_shapes = [w[n].shape for n in SMALL_SHARDED]
    per_chip = [_unpack(conv_all[2 * s], conv_sizes, conv_shapes) for s in range(N_CHIPS)]
    small = {n: w[n] for n in SMALL_REPLICATED}
    for idx, n in enumerate(SMALL_SHARDED):
        small[n] = jnp.concatenate([per_chip[s][idx] for s in range(N_CHIPS)], axis=-1)

    loss, dx, sg = _local_step(x, tgt, small, plan)

    _comm_only(plan, [[plan.job("cx", "ab_w_in", 0)]], name="reduce_last")
    for n, l in items:
        plan.bufs["G:" + n] = _chip_sum(plan.bufs[f"p:{n}:{l}"], plan.bufs[f"l:{n}:{l}"], plan.bufs.get("G:" + n),
                                        place, layer=l, shape=w[n].shape, name=f"chip_sum_{n}{l}")
    _comm_only(plan, [[plan.job("ps", n, l) for n, l in items]], name="reduce_pair_share")
    grads_big = [plan.bufs["G:" + n] for n in BIG]

    g_pack, g_sizes = _pack([sg[n] for n in SMALL])
    g_sum = _exchange_packs(g_pack, reduce=True, name="allreduce_small_grads")
    full_shapes = [small[n].shape for n in SMALL]
    g_small = dict(zip(SMALL, _unpack(g_sum, g_sizes, full_shapes)))
    for n in SMALL_SHARDED:
        width = w[n].shape[-1]
        g_small[n] = lax.dynamic_slice_in_dim(g_small[n], chip * width, width, axis=g_small[n].ndim - 1)

    grad, delta, new_m, new_v = {}, {}, {}, {}
    for n, g in zip(BIG, grads_big):
        grad[n] = g
        delta[n], new_m[n], new_v[n] = _adamw(w[n], g, m[n], v[n], name=f"adamw_{n}")
    shapes = [w[n].shape for n in SMALL]
    wp, sizes = _pack([w[n] for n in SMALL])
    gp, _ = _pack([g_small[n] for n in SMALL])
    mp, _ = _pack([m[n] for n in SMALL])
    vp, _ = _pack([v[n] for n in SMALL])
    R = wp.shape[0]
    dp, m2p, v2p = _adamw(wp.reshape(1, R, LANES), gp.reshape(1, R, LANES), mp.reshape(1, R, LANES),
                          vp.reshape(1, R, LANES), name="adamw_small")
    for n, d_, m_, v_ in zip(SMALL, _unpack(dp, sizes, shapes), _unpack(m2p, sizes, shapes),
                             _unpack(v2p, sizes, shapes)):
        grad[n] = g_small[n]
        delta[n], new_m[n], new_v[n] = d_, m_, v_
    return loss, dx, grad, delta, new_m, new_v


def kernel(x, norm_mix, norm_ffn, norm_final, ab_w_in, a_ln_g, a_ln_b, a_w_s, a_b_s, b_conv_w, b_conv_b, b_ln_g, b_ln_b, ab_w_out, c_w_in, c_conv_w, c_w_out, f_w_up, f_conv_w, f_w_down, loss_target, m_norm_mix, m_norm_ffn, m_norm_final, m_ab_w_in, m_a_ln_g, m_a_ln_b, m_a_w_s, m_a_b_s, m_b_conv_w, m_b_conv_b, m_b_ln_g, m_b_ln_b, m_ab_w_out, m_c_w_in, m_c_conv_w, m_c_w_out, m_f_w_up, m_f_conv_w, m_f_w_down, v_norm_mix, v_norm_ffn, v_norm_final, v_ab_w_in, v_a_ln_g, v_a_ln_b, v_a_w_s, v_a_b_s, v_b_conv_w, v_b_conv_b, v_b_ln_g, v_b_ln_b, v_ab_w_out, v_c_w_in, v_c_conv_w, v_c_w_out, v_f_w_up, v_f_conv_w, v_f_w_down):
    given = dict(locals())
    w = {n: given[n] for n in ALL_WEIGHTS}
    m = {n: given["m_" + n] for n in ALL_WEIGHTS}
    v = {n: given["v_" + n] for n in ALL_WEIGHTS}
    T = x.shape[1]
    loss, dx, grad, delta, new_m, new_v = _step(x.reshape(T, D_MODEL), loss_target.reshape(T, D_MODEL), w, m, v)
    loss = lax.psum(loss[0, 0], ("x", "y", "c"))
    out = [loss, dx.reshape(x.shape)]
    for d in (grad, delta, new_m, new_v):
        out += [d[n] for n in ALL_WEIGHTS]
    return tuple(out)
```

```python
import functools
import math

import jax
import jax.numpy as jnp
from jax import lax
from jax.experimental import pallas as pl
from jax.experimental.pallas import tpu as pltpu

F32 = jnp.float32
BF16 = jnp.bfloat16

EPS = 1e-6
D_MODEL = 1024
CHUNK = 128
HEAD_DIM = 128
A_HEADS = 4
D_A = 512
D_B = 512
B_CONV = 31
C_CONV = 3
D_FF = 2816
F_CONV = 3
N_CHIPS = 4

ADAM_LR = 0.001
ADAM_B1 = 0.9
ADAM_B2 = 0.999
ADAM_EPS = 1e-08
ADAM_WD = 0.01
ADAM_STEP = 10

SUBLANES = 8
LANES = 128
HALO_SHORT = 16
HALO_LONG = 32
VMEM_BYTES_MAX = 60000 * 1024

INV_SQRT2 = 1.0 / math.sqrt(2.0)
INV_SQRT_2PI = 1.0 / math.sqrt(2.0 * math.pi)

MESH = pl.DeviceIdType.MESH


def _cparams(sem, vmem_mb):
    del vmem_mb
    return pltpu.CompilerParams(dimension_semantics=sem, vmem_limit_bytes=VMEM_BYTES_MAX)


def _pick(total, pref):
    for c in (2048, 1024, 512, 256, 128):
        if c <= pref and total % c == 0:
            return c
    raise ValueError(f"no tile for {total}")


def _sigmoid(x):
    return jax.nn.sigmoid(x)


def _silu(x):
    return x * _sigmoid(x)


def _dsilu(x):
    s = _sigmoid(x)
    return s * (1.0 + x * (1.0 - s))


def _gelu(x):
    return 0.5 * x * (1.0 + lax.erf(x * INV_SQRT2))


def _dgelu(x):
    return 0.5 * (1.0 + lax.erf(x * INV_SQRT2)) + x * jnp.exp(-0.5 * x * x) * INV_SQRT_2PI


def _ln_stats(x):
    mu = jnp.mean(x, axis=-1, keepdims=True)
    xc = x - mu
    var = jnp.mean(xc * xc, axis=-1, keepdims=True)
    r = lax.rsqrt(var + EPS)
    return xc * r, r


def _ln_bwd(dy, xh, r, g):
    dxh = dy * g
    m1 = jnp.mean(dxh, axis=-1, keepdims=True)
    m2 = jnp.mean(dxh * xh, axis=-1, keepdims=True)
    return r * (dxh - m1 - xh * m2)


def _rowsum(x):
    return jnp.sum(x, axis=0, keepdims=True)


ANY = pl.BlockSpec(memory_space=pltpu.HBM)


def _place():
    x, y, c = lax.axis_index("x"), lax.axis_index("y"), lax.axis_index("c")
    peers = [(1 - x, y), (x, 1 - y), (1 - x, 1 - y)]
    return x, y, c, 2 * x + y, (x, y, 1 - c), peers


def _half(rows, which):
    return pl.ds(which * (rows // 2), rows // 2)


def _remote(src, dst, send_sem, recv_sem, device):
    return pltpu.make_async_remote_copy(src_ref=src, dst_ref=dst, send_sem=send_sem, recv_sem=recv_sem,
                                        device_id=device, device_id_type=MESH)


class _Job:
    def __init__(self, reads, writes, ncopies, copies):
        self.reads, self.writes, self.ncopies, self.copies = reads, writes, ncopies, copies


def _job_gather_ici(name, rows):
    def copies(src, dst, sem):
        x, y, c, k, sib, peers = _place()
        out = []
        for j, (px, py) in enumerate(peers):
            mine = src[name].at[0, k, _half(rows, c)]
            out.append((_remote(mine, dst[name].at[0, k, _half(rows, c)], sem(j, 0), sem(j, 1), (px, py, c)),
                        _remote(mine, dst[name].at[0, 2 * px + py, _half(rows, c)], sem(j, 0), sem(j, 1), (px, py, c))))
        return out
    return _Job([], [name], 3, copies)


def _job_gather_d2d(name, rows):
    def copies(src, dst, sem):
        x, y, c, k, sib, peers = _place()
        out = []
        for j, (px, py) in enumerate(peers):
            landed = src[name].at[0, 2 * px + py, _half(rows, c)]
            out.append((_remote(landed, dst[name].at[0, 2 * px + py, _half(rows, c)], sem(j, 0), sem(j, 1), sib),
                        _remote(landed, dst[name].at[0, 2 * px + py, _half(rows, 1 - c)], sem(j, 0), sem(j, 1), sib)))
        return out
    return _Job([], [name], 3, copies)


def _job_pair_exchange(gname, tname, rows):
    def copies(src, dst, sem):
        x, y, c, k, sib, peers = _place()
        cp = _remote(src[gname].at[:, _half(rows, 1 - c), :], dst[tname], sem(0, 0), sem(0, 1), sib)
        return [(cp, cp)]
    return _Job([gname], [tname], 1, copies)


def _job_chip_exchange(pname, lname, r0, nr):
    def copies(src, dst, sem):
        x, y, c, k, sib, peers = _place()
        out = []
        for j, (px, py) in enumerate(peers):
            cp = _remote(src[pname].at[2 * px + py, pl.ds(r0, nr)], dst[lname].at[j, pl.ds(r0, nr)],
                         sem(j, 0), sem(j, 1), (px, py, c))
            out.append((cp, cp))
        return out
    return _Job([pname], [lname], 3, copies)


def _job_pair_share(name, layer, rows):
    def copies(src, dst, sem):
        x, y, c, k, sib, peers = _place()
        mine = src[name].at[layer, _half(rows, c)]
        return [(_remote(mine, dst[name].at[layer, _half(rows, c)], sem(0, 0), sem(0, 1), sib),
                 _remote(mine, dst[name].at[layer, _half(rows, 1 - c)], sem(0, 0), sem(0, 1), sib))]
    return _Job([], [name], 1, copies)


class _Comm:
    def __init__(self, plan, jobs):
        self.plan, self.jobs = plan, jobs
        self.writes, self.reads = [], []
        for job in jobs:
            for n in job.writes:
                if n not in self.writes:
                    self.writes.append(n)
        for job in jobs:
            for n in job.reads:
                if n not in self.writes and n not in self.reads:
                    self.reads.append(n)
        self.ncopies = sum(job.ncopies for job in jobs)

    def descriptors(self, src, dst, sems, base):
        out = []
        for job in self.jobs:
            sem = lambda j, which, base=base: sems.at[base + j, which]
            out += job.copies(src, dst, sem)
            base += job.ncopies
        return out

    def start(self, src, dst, sems, base=0):
        for first, _ in self.descriptors(src, dst, sems, base):
            first.start()

    def finish(self, src, dst, sems, base=0):
        for _, landed in self.descriptors(src, dst, sems, base):
            landed.wait()


def _comm_operands(comm):
    bufs = comm.plan.bufs
    shapes = [jax.ShapeDtypeStruct(bufs[n].shape, bufs[n].dtype) for n in comm.writes]
    return [bufs[n] for n in comm.reads] + [bufs[n] for n in comm.writes], shapes


def _pallas(comm, body, *, name, grid, in_specs, out_specs, out_shape, compiler_params, scratch_shapes=()):
    if comm is None:
        return pl.pallas_call(body, name=name, grid=grid, in_specs=in_specs, out_specs=out_specs,
                              out_shape=out_shape, scratch_shapes=list(scratch_shapes),
                              compiler_params=compiler_params)
    single = not isinstance(out_shape, (list, tuple))
    base_specs = [out_specs] if single else list(out_specs)
    base_shape = [out_shape] if single else list(out_shape)
    nb, nr, nw, nbo, nsc = len(in_specs), len(comm.reads), len(comm.writes), len(base_specs), len(scratch_shapes)

    def wrapped(*refs):
        base_in, rd, wr_in = refs[:nb], refs[nb:nb + nr], refs[nb + nr:nb + nr + nw]
        o0 = nb + nr + nw
        base_out, wr_out = refs[o0:o0 + nbo], refs[o0 + nbo:o0 + nbo + nw]
        scratch, sems = refs[o0 + nbo + nw:o0 + nbo + nw + nsc], refs[-1]
        src = dict(zip(comm.reads, rd))
        src.update(zip(comm.writes, wr_in))
        dst = dict(zip(comm.writes, wr_out))
        first = functools.reduce(jnp.logical_and, [pl.program_id(a) == 0 for a in range(len(grid))])
        last = functools.reduce(jnp.logical_and,
                                [pl.program_id(a) == pl.num_programs(a) - 1 for a in range(len(grid))])

        @pl.when(first)
        def _():
            comm.start(src, dst, sems)
        body(*base_in, *base_out, *scratch)

        @pl.when(last)
        def _():
            comm.finish(src, dst, sems)

    operands, shapes = _comm_operands(comm)
    call = pl.pallas_call(
        wrapped, name=name, grid=grid, in_specs=list(in_specs) + [ANY] * (nr + nw),
        out_specs=base_specs + [ANY] * nw, out_shape=base_shape + shapes,
        input_output_aliases={nb + nr + q: nbo + q for q in range(nw)},
        scratch_shapes=list(scratch_shapes) + [pltpu.SemaphoreType.DMA((comm.ncopies, 2))],
        compiler_params=compiler_params)

    def run(*args):
        outs = call(*args, *operands)
        for q, n in enumerate(comm.writes):
            comm.plan.bufs[n] = outs[nbo + q]
        return outs[0] if single else list(outs[:nbo])

    return run


def _comm_only(plan, phases, *, name):
    comms = [_Comm(plan, jobs) for jobs in phases]
    both = _Comm(plan, [job for jobs in phases for job in jobs])
    nr, nw = len(both.reads), len(both.writes)

    def body(*refs):
        rd, wr_in, wr_out, sems = refs[:nr], refs[nr:nr + nw], refs[nr + nw:nr + 2 * nw], refs[-1]
        src = dict(zip(both.reads, rd))
        src.update(zip(both.writes, wr_in))
        dst = dict(zip(both.writes, wr_out))
        base = 0
        for comm in comms:
            comm.start(src, dst, sems, base)
            comm.finish(src, dst, sems, base)
            base += comm.ncopies

    operands, shapes = _comm_operands(both)
    outs = pl.pallas_call(
        body, name=name, in_specs=[ANY] * (nr + nw), out_specs=[ANY] * nw, out_shape=shapes,
        input_output_aliases={nr + q: q for q in range(nw)},
        scratch_shapes=[pltpu.SemaphoreType.DMA((both.ncopies, 2))],
    )(*operands)
    for q, n in enumerate(both.writes):
        plan.bufs[n] = outs[q]


def _mm_nn(a, w, *, layer, tm, tn, residual=None, out_dtype=F32, name, comm=None):
    T, K = a.shape
    if w.ndim == 4:
        _, S, _, n4 = w.shape
        N = S * n4
        bps = n4 // tn
        w_spec = pl.BlockSpec((None, None, K, tn), lambda j, i: (layer, j // bps, 0, j % bps))
    else:
        N = w.shape[2]
        w_spec = pl.BlockSpec((None, K, tn), lambda j, i: (layer, 0, j))
    in_specs = [pl.BlockSpec((tm, K), lambda j, i: (i, 0)), w_spec]
    args = [a, w]
    if residual is not None:
        in_specs.append(pl.BlockSpec((tm, tn), lambda j, i: (i, j)))
        args.append(residual)

    def body(*refs):
        a_ref, w_ref, o_ref = refs[0], refs[1], refs[-1]
        acc = jnp.dot(a_ref[...].astype(BF16), w_ref[...], preferred_element_type=F32)
        if residual is not None:
            acc = refs[2][...] + acc
        o_ref[...] = acc.astype(out_dtype)

    return _pallas(
        comm, body, name=name, grid=(N // tn, T // tm), in_specs=in_specs,
        out_specs=pl.BlockSpec((tm, tn), lambda j, i: (i, j)),
        out_shape=jax.ShapeDtypeStruct((T, N), out_dtype),
        compiler_params=_cparams(("parallel", "parallel"), 48),
    )(*args)


def _mm_nt(dy, w, *, layer, tm, tn, name, out_dtype=F32, comm=None):
    T = dy.shape[0]
    nt_dims = (((1,), (1,)), ((), ()))
    if w.ndim == 4:
        _, S, K, n4 = w.shape

        def body(dy_ref, w_ref, o_ref):
            @pl.when(pl.program_id(1) == 0)
            def _():
                o_ref[...] = jnp.zeros_like(o_ref)
            o_ref[...] += lax.dot_general(dy_ref[...].astype(BF16), w_ref[...], nt_dims,
                                          preferred_element_type=F32)

        return _pallas(
            comm, body, name=name, grid=(T // tm, S),
            in_specs=[pl.BlockSpec((tm, n4), lambda i, s: (i, s)),
                      pl.BlockSpec((None, None, K, n4), lambda i, s: (layer, s, 0, 0))],
            out_specs=pl.BlockSpec((tm, K), lambda i, s: (i, 0)),
            out_shape=jax.ShapeDtypeStruct((T, K), F32),
            compiler_params=_cparams(("parallel", "arbitrary"), 48),
        )(dy, w)
    _, R, N = w.shape

    def body2(dy_ref, w_ref, o_ref):
        o_ref[...] = lax.dot_general(dy_ref[...].astype(BF16), w_ref[...], nt_dims,
                                     preferred_element_type=F32).astype(out_dtype)

    return _pallas(
        comm, body2, name=name, grid=(R // tn, T // tm),
        in_specs=[pl.BlockSpec((tm, N), lambda j, i: (i, 0)),
                  pl.BlockSpec((None, tn, N), lambda j, i: (layer, j, 0))],
        out_specs=pl.BlockSpec((tm, tn), lambda j, i: (i, j)),
        out_shape=jax.ShapeDtypeStruct((T, R), out_dtype),
        compiler_params=_cparams(("parallel", "parallel"), 48),
    )(dy, w)


def _mm_tn(a, dy, *, shards, tk, tn, tt, name, comm=None):
    T, K = a.shape
    N = dy.shape[1]
    tn_dims = (((0,), (0,)), ((), ()))

    def body(a_ref, dy_ref, o_ref):
        @pl.when(pl.program_id(2) == 0)
        def _():
            o_ref[...] = jnp.zeros_like(o_ref)
        o_ref[...] += lax.dot_general(a_ref[...].astype(BF16), dy_ref[...].astype(BF16), tn_dims,
                                      preferred_element_type=F32)

    if shards is None:
        out_spec = pl.BlockSpec((tk, tn), lambda k, n, t: (k, n))
        out_shape = jax.ShapeDtypeStruct((K, N), F32)
    else:
        n4 = N // shards
        bps = n4 // tn
        out_spec = pl.BlockSpec((None, tk, tn), lambda k, n, t: (n // bps, k, n % bps))
        out_shape = jax.ShapeDtypeStruct((shards, K, n4), F32)
    return _pallas(
        comm, body, name=name, grid=(K // tk, N // tn, T // tt),
        in_specs=[pl.BlockSpec((tt, tk), lambda k, n, t: (t, k)),
                  pl.BlockSpec((tt, tn), lambda k, n, t: (t, n))],
        out_specs=out_spec, out_shape=out_shape,
        compiler_params=_cparams(("parallel", "parallel", "arbitrary"), 48),
    )(a, dy)


def _rmsnorm_fwd(x, g, *, layer, tm, name, comm=None):
    T, D = x.shape

    def body(x_ref, g_ref, h_ref):
        xf = x_ref[...]
        r = lax.rsqrt(jnp.mean(xf * xf, axis=-1, keepdims=True) + EPS)
        h_ref[...] = (xf * r * g_ref[...]).astype(BF16)

    return _pallas(
        comm, body, name=name, grid=(T // tm,),
        in_specs=[pl.BlockSpec((tm, D), lambda i: (i, 0)),
                  pl.BlockSpec((None, 1, D), lambda i: (layer, 0, 0))],
        out_specs=pl.BlockSpec((tm, D), lambda i: (i, 0)),
        out_shape=jax.ShapeDtypeStruct((T, D), BF16),
        compiler_params=_cparams(("parallel",), 32),
    )(x, g)


def _rmsnorm_bwd(x, g, dh, dres, *, layer, tm, name, comm=None):
    T, D = x.shape

    def body(x_ref, g_ref, dh_ref, dres_ref, dx_ref, dg_ref):
        @pl.when(pl.program_id(0) == 0)
        def _():
            dg_ref[...] = jnp.zeros_like(dg_ref)
        xf = x_ref[...]
        r = lax.rsqrt(jnp.mean(xf * xf, axis=-1, keepdims=True) + EPS)
        xh = xf * r
        dh = dh_ref[...]
        dg_ref[...] += _rowsum(dh * xh)
        dxh = dh * g_ref[...]
        dx_ref[...] = dres_ref[...] + r * (dxh - xh * jnp.mean(dxh * xh, axis=-1, keepdims=True))

    return _pallas(
        comm, body, name=name, grid=(T // tm,),
        in_specs=[pl.BlockSpec((tm, D), lambda i: (i, 0)),
                  pl.BlockSpec((None, 1, D), lambda i: (layer, 0, 0)),
                  pl.BlockSpec((tm, D), lambda i: (i, 0)),
                  pl.BlockSpec((tm, D), lambda i: (i, 0))],
        out_specs=[pl.BlockSpec((tm, D), lambda i: (i, 0)),
                   pl.BlockSpec((1, D), lambda i: (0, 0))],
        out_shape=[jax.ShapeDtypeStruct((T, D), F32), jax.ShapeDtypeStruct((1, D), F32)],
        compiler_params=_cparams(("arbitrary",), 40),
    )(x, g, dh, dres)


def _loss_head(x, tgt, g, *, tm, name, comm=None):
    T, D = x.shape

    def body(x_ref, t_ref, g_ref, loss_ref, dx_ref, dg_ref):
        @pl.when(pl.program_id(0) == 0)
        def _():
            dg_ref[...] = jnp.zeros_like(dg_ref)
            loss_ref[...] = jnp.zeros_like(loss_ref)
        xf = x_ref[...]
        gg = g_ref[...]
        r = lax.rsqrt(jnp.mean(xf * xf, axis=-1, keepdims=True) + EPS)
        xh = xf * r
        err = xh * gg - t_ref[...]
        row = jnp.mean(err * err, axis=-1, keepdims=True)
        loss_ref[...] += 0.5 * jnp.sum(row, axis=0, keepdims=True)
        dy = err * (1.0 / D)
        dg_ref[...] += _rowsum(dy * xh)
        dxh = dy * gg
        dx_ref[...] = r * (dxh - xh * jnp.mean(dxh * xh, axis=-1, keepdims=True))

    return _pallas(
        comm, body, name=name, grid=(T // tm,),
        in_specs=[pl.BlockSpec((tm, D), lambda i: (i, 0)),
                  pl.BlockSpec((tm, D), lambda i: (i, 0)),
                  pl.BlockSpec((1, D), lambda i: (0, 0))],
        out_specs=[pl.BlockSpec((1, 1), lambda i: (0, 0)),
                   pl.BlockSpec((tm, D), lambda i: (i, 0)),
                   pl.BlockSpec((1, D), lambda i: (0, 0))],
        out_shape=[jax.ShapeDtypeStruct((1, 1), F32), jax.ShapeDtypeStruct((T, D), F32),
                   jax.ShapeDtypeStruct((1, D), F32)],
        compiler_params=_cparams(("arbitrary",), 40),
    )(x, tgt, g)


CONV_ROWS = 64
CONV_COLS = 256


def _halo_prev_index(tm, halo):
    per = tm // halo
    return lambda i: jnp.maximum(i * per - 1, 0)


def _halo_next_index(tm, halo, total):
    per = tm // halo
    last = total // halo - 1
    return lambda i: jnp.minimum((i + 1) * per, last)


def _causal_mask():
    t = lax.broadcasted_iota(jnp.int32, (CHUNK, CHUNK), 0)
    s = lax.broadcasted_iota(jnp.int32, (CHUNK, CHUNK), 1)
    return s <= t


def _mixer_ab_fwd(z, a_ln_g, a_ln_b, w_s, b_s, conv_w, conv_b, b_ln_g, b_ln_b, *, tm, name, comm=None):
    T = z.shape[0]
    nchunk = tm // CHUNK
    halo = HALO_LONG

    def body(za_ref, zb_ref, zh_ref, alg_ref, alb_ref, ws_ref, bs_ref, cw_ref, cbias_ref,
             blg_ref, blb_ref, y_ref, cb_ref, ext_ref):
        i = pl.program_id(0)
        gu = _gelu(za_ref[:, :D_A].astype(F32))
        gv = _gelu(za_ref[:, D_A:].astype(F32))
        xh, _ = _ln_stats(gv)
        lv = (xh * alg_ref[...] + alb_ref[...]).astype(BF16)
        mask = _causal_mask()
        for h in range(A_HEADS):
            wm = jnp.where(mask, ws_ref[h], 0.0).astype(BF16)
            cols = slice(h * HEAD_DIM, (h + 1) * HEAD_DIM)
            for c in range(nchunk):
                rows = slice(c * CHUNK, (c + 1) * CHUNK)
                mixed = jnp.dot(wm, lv[rows, cols], preferred_element_type=F32) + bs_ref[h]
                y_ref[rows, cols] = (gu[rows, cols] * mixed).astype(BF16)
        ext_ref[halo:halo + tm, :] = zb_ref[:, :D_B].astype(F32) * _sigmoid(zb_ref[:, D_B:].astype(F32))
        prev = zh_ref[:, :D_B].astype(F32) * _sigmoid(zh_ref[:, D_B:].astype(F32))
        ext_ref[0:halo, :] = jnp.where(i > 0, prev, 0.0)
        for rb in range(tm // CONV_ROWS):
            for cb in range(D_B // CONV_COLS):
                cs = slice(cb * CONV_COLS, (cb + 1) * CONV_COLS)
                window = ext_ref[rb * CONV_ROWS:rb * CONV_ROWS + CONV_ROWS + halo, cs]
                acc = jnp.zeros((CONV_ROWS, CONV_COLS), F32)
                for k in range(B_CONV):
                    shifted = _rows_after(window, halo - (B_CONV - 1) + k)[:CONV_ROWS]
                    acc = acc + cw_ref[k:k + 1, cs] * shifted
                cb_ref[rb * CONV_ROWS:(rb + 1) * CONV_ROWS, cs] = acc + cbias_ref[:, cs]
        xhb, _ = _ln_stats(cb_ref[...])
        y_ref[:, D_A:] = _silu(xhb * blg_ref[...] + blb_ref[...]).astype(BF16)

    row = lambda i: (i, 0)
    par = lambda i: (0, 0)
    return _pallas(
        comm, body, name=name, grid=(T // tm,),
        in_specs=[pl.BlockSpec((tm, 2 * D_A), lambda i: (i, 0)),
                  pl.BlockSpec((tm, 2 * D_B), lambda i: (i, 1)),
                  pl.BlockSpec((halo, 2 * D_B), lambda i: (_halo_prev_index(tm, halo)(i), 1)),
                  pl.BlockSpec((1, D_A), par), pl.BlockSpec((1, D_A), par),
                  pl.BlockSpec((A_HEADS, CHUNK, CHUNK), lambda i: (0, 0, 0)),
                  pl.BlockSpec((A_HEADS, CHUNK, 1), lambda i: (0, 0, 0)),
                  pl.BlockSpec((B_CONV, D_B), par), pl.BlockSpec((1, D_B), par),
                  pl.BlockSpec((1, D_B), par), pl.BlockSpec((1, D_B), par)],
        out_specs=[pl.BlockSpec((tm, D_A + D_B), row), pl.BlockSpec((tm, D_B), row)],
        out_shape=[jax.ShapeDtypeStruct((T, D_A + D_B), BF16), jax.ShapeDtypeStruct((T, D_B), F32)],
        scratch_shapes=[pltpu.VMEM((halo + tm, D_B), F32)],
        compiler_params=_cparams(("parallel",), 40),
    )(z, z, z, a_ln_g, a_ln_b, w_s, b_s, conv_w, conv_b, b_ln_g, b_ln_b)


def _mixer_ab_bwd_pre(z, cb, dy, a_ln_g, a_ln_b, w_s, b_s, b_ln_g, b_ln_b, *, tm, name, comm=None):
    T = z.shape[0]
    nchunk = tm // CHUNK
    tn_dims = (((0,), (0,)), ((), ()))
    nt_dims = (((1,), (1,)), ((), ()))

    def body(za_ref, cb_ref, dy_ref, alg_ref, alb_ref, ws_ref, bs_ref, blg_ref, blb_ref,
             dza_ref, dcb_ref, dalg_ref, dalb_ref, dws_ref, dbs_ref, dblg_ref, dblb_ref,
             dlv_ref):
        @pl.when(pl.program_id(0) == 0)
        def _():
            for ref in (dalg_ref, dalb_ref, dws_ref, dbs_ref, dblg_ref, dblb_ref):
                ref[...] = jnp.zeros_like(ref)
        ua = za_ref[:, :D_A].astype(F32)
        va = za_ref[:, D_A:].astype(F32)
        gu = _gelu(ua)
        gv = _gelu(va)
        xh, r = _ln_stats(gv)
        alg = alg_ref[...]
        lv = (xh * alg + alb_ref[...]).astype(BF16)
        dya = dy_ref[:, :D_A].astype(F32)
        mask = _causal_mask()
        for h in range(A_HEADS):
            wm = jnp.where(mask, ws_ref[h], 0.0).astype(BF16)
            cols = slice(h * HEAD_DIM, (h + 1) * HEAD_DIM)
            dwm = jnp.zeros((CHUNK, CHUNK), F32)
            dbs = jnp.zeros((CHUNK, 1), F32)
            for c in range(nchunk):
                rows = slice(c * CHUNK, (c + 1) * CHUNK)
                lvb = lv[rows, cols]
                mixed = jnp.dot(wm, lvb, preferred_element_type=F32) + bs_ref[h]
                dyb = dya[rows, cols]
                dza_ref[rows, cols] = (dyb * mixed * _dgelu(ua[rows, cols])).astype(BF16)
                dmixed = dyb * gu[rows, cols]
                dmb = dmixed.astype(BF16)
                dlv_ref[rows, cols] = lax.dot_general(wm, dmb, tn_dims, preferred_element_type=F32)
                dwm = dwm + lax.dot_general(dmb, lvb, nt_dims, preferred_element_type=F32)
                dbs = dbs + jnp.sum(dmixed, axis=1, keepdims=True)
            dws_ref[h] += jnp.where(mask, dwm, 0.0)
            dbs_ref[h] += dbs
        dlv = dlv_ref[...]
        dalg_ref[...] += _rowsum(dlv * xh)
        dalb_ref[...] += _rowsum(dlv)
        dgv = _ln_bwd(dlv, xh, r, alg)
        dza_ref[:, D_A:] = (dgv * _dgelu(va)).astype(BF16)
        xhb, rb = _ln_stats(cb_ref[...])
        blg = blg_ref[...]
        lb = xhb * blg + blb_ref[...]
        dlb = dy_ref[:, D_A:].astype(F32) * _dsilu(lb)
        dblg_ref[...] += _rowsum(dlb * xhb)
        dblb_ref[...] += _rowsum(dlb)
        dcb_ref[...] = _ln_bwd(dlb, xhb, rb, blg)

    row = lambda i: (i, 0)
    par = lambda i: (0, 0)
    par3 = lambda i: (0, 0, 0)
    return _pallas(
        comm, body, name=name, grid=(T // tm,),
        in_specs=[pl.BlockSpec((tm, 2 * D_A), row), pl.BlockSpec((tm, D_B), row),
                  pl.BlockSpec((tm, D_A + D_B), row),
                  pl.BlockSpec((1, D_A), par), pl.BlockSpec((1, D_A), par),
                  pl.BlockSpec((A_HEADS, CHUNK, CHUNK), par3),
                  pl.BlockSpec((A_HEADS, CHUNK, 1), par3),
                  pl.BlockSpec((1, D_B), par), pl.BlockSpec((1, D_B), par)],
        out_specs=[pl.BlockSpec((tm, 2 * D_A), row), pl.BlockSpec((tm, D_B), row),
                   pl.BlockSpec((1, D_A), par), pl.BlockSpec((1, D_A), par),
                   pl.BlockSpec((A_HEADS, CHUNK, CHUNK), par3),
                   pl.BlockSpec((A_HEADS, CHUNK, 1), par3),
                   pl.BlockSpec((1, D_B), par), pl.BlockSpec((1, D_B), par)],
        out_shape=[jax.ShapeDtypeStruct((T, 2 * D_A), BF16), jax.ShapeDtypeStruct((T, D_B), F32),
                   jax.ShapeDtypeStruct((1, D_A), F32), jax.ShapeDtypeStruct((1, D_A), F32),
                   jax.ShapeDtypeStruct((A_HEADS, CHUNK, CHUNK), F32),
                   jax.ShapeDtypeStruct((A_HEADS, CHUNK, 1), F32),
                   jax.ShapeDtypeStruct((1, D_B), F32), jax.ShapeDtypeStruct((1, D_B), F32)],
        scratch_shapes=[pltpu.VMEM((tm, D_A), F32)],
        compiler_params=_cparams(("arbitrary",), 40),
    )(z, cb, dy, a_ln_g, a_ln_b, w_s, b_s, b_ln_g, b_ln_b)


def _mixer_b_conv_bwd(z, dcb, conv_w, *, tm, name, comm=None):
    T = z.shape[0]
    halo = HALO_LONG

    def body(zb_ref, dcb_ref, dcn_ref, cw_ref, dzb_ref, dcw_ref, dbias_ref, dext_ref):
        i = pl.program_id(0)
        last = pl.num_programs(0) - 1

        @pl.when(i == 0)
        def _():
            dcw_ref[...] = jnp.zeros_like(dcw_ref)
            dbias_ref[...] = jnp.zeros_like(dbias_ref)
        dcb = dcb_ref[...]
        dext_ref[0:tm, :] = dcb
        dext_ref[tm:tm + halo, :] = jnp.where(i < last, dcn_ref[...], 0.0)
        dbias_ref[...] += _rowsum(dcb)
        for rb in range(tm // CONV_ROWS):
            for cb in range(D_B // CONV_COLS):
                cs = slice(cb * CONV_COLS, (cb + 1) * CONV_COLS)
                gcs = slice(D_B + cb * CONV_COLS, D_B + (cb + 1) * CONV_COLS)
                rs = slice(rb * CONV_ROWS, (rb + 1) * CONV_ROWS)
                xbb = zb_ref[rs, cs].astype(F32)
                sgb = _sigmoid(zb_ref[rs, gcs].astype(F32))
                yb0 = xbb * sgb
                window = dext_ref[rb * CONV_ROWS:rb * CONV_ROWS + CONV_ROWS + halo, cs]
                acc = jnp.zeros((CONV_ROWS, CONV_COLS), F32)
                for k in range(B_CONV):
                    shifted = _rows_after(window, (B_CONV - 1) - k)[:CONV_ROWS]
                    acc = acc + cw_ref[k:k + 1, cs] * shifted
                    dcw_ref[k:k + 1, cs] += _rowsum(shifted * yb0)
                dzb_ref[rs, cs] = (acc * sgb).astype(BF16)
                dzb_ref[rs, gcs] = (acc * xbb * sgb * (1.0 - sgb)).astype(BF16)

    row = lambda i: (i, 0)
    par = lambda i: (0, 0)
    return _pallas(
        comm, body, name=name, grid=(T // tm,),
        in_specs=[pl.BlockSpec((tm, 2 * D_B), lambda i: (i, 1)),
                  pl.BlockSpec((tm, D_B), row),
                  pl.BlockSpec((halo, D_B), lambda i: (_halo_next_index(tm, halo, T)(i), 0)),
                  pl.BlockSpec((B_CONV, D_B), par)],
        out_specs=[pl.BlockSpec((tm, 2 * D_B), row), pl.BlockSpec((B_CONV, D_B), par),
                   pl.BlockSpec((1, D_B), par)],
        out_shape=[jax.ShapeDtypeStruct((T, 2 * D_B), BF16), jax.ShapeDtypeStruct((B_CONV, D_B), F32),
                   jax.ShapeDtypeStruct((1, D_B), F32)],
        scratch_shapes=[pltpu.VMEM((tm + halo, D_B), F32)],
        compiler_params=_cparams(("arbitrary",), 40),
    )(z, dcb, dcb, conv_w)


def _rows_before(x, a):
    return x if a == 0 else pltpu.roll(x, a, axis=0)


def _rows_after(x, a):
    return x if a == 0 else pltpu.roll(x, x.shape[0] - a, axis=0)


def _conv3(w_ref, x, halo, cs):
    acc = w_ref[2:3, cs] * x[halo:]
    acc = acc + w_ref[1:2, cs] * _rows_before(x, 1)[halo:]
    return acc + w_ref[0:1, cs] * _rows_before(x, 2)[halo:]


def _mixer_c_fwd(z, conv_w, *, tm, name, comm=None):
    T = z.shape[0]
    D = D_MODEL
    halo = HALO_SHORT
    W = CONV_COLS

    def body(bg_ref, cg_ref, xv_ref, cgh_ref, xvh_ref, w_ref, r_ref):
        i = pl.program_id(0)
        for cb in range(D // W):
            cs = slice(cb * W, (cb + 1) * W)
            prev = jnp.where(i > 0, cgh_ref[:, cs].astype(F32) * xvh_ref[:, cs].astype(F32), 0.0)
            p = jnp.concatenate([prev, cg_ref[:, cs].astype(F32) * xv_ref[:, cs].astype(F32)], axis=0)
            r_ref[:, cs] = (bg_ref[:, cs].astype(F32) * _conv3(w_ref, p, halo, cs)).astype(BF16)

    hp = _halo_prev_index(tm, halo)
    return _pallas(
        comm, body, name=name, grid=(T // tm,),
        in_specs=[pl.BlockSpec((tm, D), lambda i: (i, 0)), pl.BlockSpec((tm, D), lambda i: (i, 1)),
                  pl.BlockSpec((tm, D), lambda i: (i, 2)),
                  pl.BlockSpec((halo, D), lambda i: (hp(i), 1)),
                  pl.BlockSpec((halo, D), lambda i: (hp(i), 2)),
                  pl.BlockSpec((None, C_CONV, D), lambda i: (0, 0, 0))],
        out_specs=pl.BlockSpec((tm, D), lambda i: (i, 0)),
        out_shape=jax.ShapeDtypeStruct((T, D), BF16),
        compiler_params=_cparams(("parallel",), 40),
    )(z, z, z, z, z, conv_w)


def _mixer_c_bwd(z, dr, conv_w, *, tm, name, comm=None):
    T = z.shape[0]
    D = D_MODEL
    halo = HALO_SHORT
    W = CONV_COLS

    def body(bg_ref, cg_ref, xv_ref, cgh_ref, xvh_ref, bgn_ref, dr_ref, drn_ref, w_ref, dz_ref, dw_ref):
        i = pl.program_id(0)
        last = pl.num_programs(0) - 1

        @pl.when(i == 0)
        def _():
            dw_ref[...] = jnp.zeros_like(dw_ref)
        for cb in range(D // W):
            cs = slice(cb * W, (cb + 1) * W)
            cg = cg_ref[:, cs].astype(F32)
            xv = xv_ref[:, cs].astype(F32)
            dr = dr_ref[:, cs].astype(F32)
            p = cg * xv
            prev = jnp.where(i > 0, cgh_ref[:, cs].astype(F32) * xvh_ref[:, cs].astype(F32), 0.0)
            q = _conv3(w_ref, jnp.concatenate([prev, p], axis=0), halo, cs)
            dz_ref[:, cs] = (dr * q).astype(BF16)
            nxt = jnp.where(i < last, drn_ref[:, cs].astype(F32) * bgn_ref[:, cs].astype(F32), 0.0)
            dq = jnp.concatenate([dr * bg_ref[:, cs].astype(F32), nxt], axis=0)
            dp = None
            for k in range(C_CONV):
                shifted = _rows_after(dq, 2 - k)[:tm]
                term = w_ref[k:k + 1, cs] * shifted
                dp = term if dp is None else dp + term
                dw_ref[k:k + 1, cs] += _rowsum(shifted * p)
            dz_ref[:, D + cb * W:D + (cb + 1) * W] = (dp * xv).astype(BF16)
            dz_ref[:, 2 * D + cb * W:2 * D + (cb + 1) * W] = (dp * cg).astype(BF16)

    hp = _halo_prev_index(tm, halo)
    hn = _halo_next_index(tm, halo, T)
    return _pallas(
        comm, body, name=name, grid=(T // tm,),
        in_specs=[pl.BlockSpec((tm, D), lambda i: (i, 0)), pl.BlockSpec((tm, D), lambda i: (i, 1)),
                  pl.BlockSpec((tm, D), lambda i: (i, 2)),
                  pl.BlockSpec((halo, D), lambda i: (hp(i), 1)),
                  pl.BlockSpec((halo, D), lambda i: (hp(i), 2)),
                  pl.BlockSpec((halo, D), lambda i: (hn(i), 0)),
                  pl.BlockSpec((tm, D), lambda i: (i, 0)),
                  pl.BlockSpec((halo, D), lambda i: (hn(i), 0)),
                  pl.BlockSpec((None, C_CONV, D), lambda i: (0, 0, 0))],
        out_specs=[pl.BlockSpec((tm, 3 * D), lambda i: (i, 0)),
                   pl.BlockSpec((C_CONV, D), lambda i: (0, 0))],
        out_shape=[jax.ShapeDtypeStruct((T, 3 * D), BF16), jax.ShapeDtypeStruct((C_CONV, D), F32)],
        compiler_params=_cparams(("arbitrary",), 48),
    )(z, z, z, z, z, z, dr, dr, conv_w)


FFN_COLS = 128


def _ffn_act_fwd(up, conv_w, *, layer, tm, name, comm=None):
    T = up.shape[0]
    halo = HALO_SHORT
    W = FFN_COLS

    def body(up_ref, uph_ref, w_ref, a_ref, upc_ref):
        i = pl.program_id(0)
        def conv(cs):
            prev = jnp.where(i > 0, uph_ref[:, cs], jnp.zeros((halo, W), BF16))
            return _conv3(w_ref, jnp.concatenate([prev, up_ref[:, cs]], axis=0).astype(F32), halo, cs)

        for cb in range(D_FF // W):
            gs = slice(cb * W, (cb + 1) * W)
            vs = slice(D_FF + cb * W, D_FF + (cb + 1) * W)
            g = conv(gs)
            v = conv(vs)
            upc_ref[:, gs] = g.astype(BF16)
            upc_ref[:, vs] = v.astype(BF16)
            a_ref[:, gs] = (_silu(g) * v).astype(BF16)

    return _pallas(
        comm, body, name=name, grid=(T // tm,),
        in_specs=[pl.BlockSpec((tm, 2 * D_FF), lambda i: (i, 0)),
                  pl.BlockSpec((halo, 2 * D_FF), lambda i: (_halo_prev_index(tm, halo)(i), 0)),
                  pl.BlockSpec((None, F_CONV, 2 * D_FF), lambda i: (layer, 0, 0))],
        out_specs=[pl.BlockSpec((tm, D_FF), lambda i: (i, 0)),
                   pl.BlockSpec((tm, 2 * D_FF), lambda i: (i, 0))],
        out_shape=[jax.ShapeDtypeStruct((T, D_FF), BF16), jax.ShapeDtypeStruct((T, 2 * D_FF), BF16)],
        compiler_params=_cparams(("parallel",), 48),
    )(up, up, conv_w)


def _ffn_act_bwd(up, upc, da, conv_w, *, layer, tm, name, comm=None):
    T = up.shape[0]
    halo = HALO_SHORT
    W = FFN_COLS

    def body(up_ref, upc_ref, upcn_ref, da_ref, dan_ref, w_ref, dup_ref, dw_ref):
        i = pl.program_id(0)
        last = pl.num_programs(0) - 1

        @pl.when(i == 0)
        def _():
            dw_ref[...] = jnp.zeros_like(dw_ref)
        live = jnp.where(i < last, 1.0, 0.0)
        for cb in range(D_FF // W):
            gs = slice(cb * W, (cb + 1) * W)
            vs = slice(D_FF + cb * W, D_FF + (cb + 1) * W)
            g = jnp.concatenate([upc_ref[:, gs], upcn_ref[:, gs]], axis=0).astype(F32)
            v = jnp.concatenate([upc_ref[:, vs], upcn_ref[:, vs]], axis=0).astype(F32)
            da = jnp.concatenate([da_ref[:, gs].astype(F32), dan_ref[:, gs].astype(F32) * live], axis=0)
            s = _sigmoid(g)
            silu = g * s
            grads = (da * v * (s * (1.0 + g * (1.0 - s))), da * silu)
            for cs, d in zip((gs, vs), grads):
                u = up_ref[:, cs].astype(F32)
                acc = None
                for k in range(F_CONV):
                    shifted = _rows_after(d, 2 - k)[:tm]
                    term = w_ref[k:k + 1, cs] * shifted
                    acc = term if acc is None else acc + term
                    dw_ref[k:k + 1, cs] += _rowsum(shifted * u)
                dup_ref[:, cs] = acc.astype(BF16)

    hn = _halo_next_index(tm, halo, T)
    return _pallas(
        comm, body, name=name, grid=(T // tm,),
        in_specs=[pl.BlockSpec((tm, 2 * D_FF), lambda i: (i, 0)),
                  pl.BlockSpec((tm, 2 * D_FF), lambda i: (i, 0)),
                  pl.BlockSpec((halo, 2 * D_FF), lambda i: (hn(i), 0)),
                  pl.BlockSpec((tm, D_FF), lambda i: (i, 0)),
                  pl.BlockSpec((halo, D_FF), lambda i: (hn(i), 0)),
                  pl.BlockSpec((None, F_CONV, 2 * D_FF), lambda i: (layer, 0, 0))],
        out_specs=[pl.BlockSpec((tm, 2 * D_FF), lambda i: (i, 0)),
                   pl.BlockSpec((F_CONV, 2 * D_FF), lambda i: (0, 0))],
        out_shape=[jax.ShapeDtypeStruct((T, 2 * D_FF), BF16),
                   jax.ShapeDtypeStruct((F_CONV, 2 * D_FF), F32)],
        compiler_params=_cparams(("arbitrary",), 56),
    )(up, upc, upc, da, da, conv_w)


def _local_step(x, tgt, small, plan):
    T = x.shape[0]
    tm_e = _pick(T, 256)
    tm_n = _pick(T, 512)
    tm = _pick(T, 1024)
    tm_r = _pick(T, 2048)
    tt = _pick(T, 512)
    nm = small["norm_mix"].reshape(2, 1, D_MODEL)
    nf = small["norm_ffn"].reshape(2, 1, D_MODEL)
    ngf = small["norm_final"].reshape(1, D_MODEL)
    b_s = small["a_b_s"].reshape(A_HEADS, CHUNK, 1)
    w_s = small["a_w_s"].reshape(A_HEADS, CHUNK, CHUNK)
    b_conv_w = small["b_conv_w"].reshape(B_CONV, D_B)
    sg = {}
    wt, cm = plan.weight, plan.comm

    h_m0 = _rmsnorm_fwd(x, nm, layer=0, tm=tm_n, name="norm_mix0")
    z_ab = _mm_nn(h_m0, wt("ab_w_in", 0), layer=0, tm=tm, tn=512, out_dtype=BF16, name="ab_in", comm=cm("ab_in"))
    yab, cb = _mixer_ab_fwd(z_ab, small["a_ln_g"], small["a_ln_b"], w_s, b_s, b_conv_w, small["b_conv_b"],
                            small["b_ln_g"], small["b_ln_b"], tm=tm_e, name="mixer_ab", comm=cm("mixer_ab"))
    x1 = _mm_nn(yab, wt("ab_w_out", 0), layer=0, tm=tm, tn=512, residual=x, name="ab_out", comm=cm("ab_out"))

    def ffn_fwd(xin, layer):
        h = _rmsnorm_fwd(xin, nf, layer=layer, tm=tm_n, name=f"norm_ffn{layer}")
        up = _mm_nn(h, wt("f_w_up", layer), layer=0, tm=tm, tn=1408, out_dtype=BF16, name=f"ffn_up{layer}",
                    comm=cm(f"ffn_up{layer}"))
        a, upc = _ffn_act_fwd(up, small["f_conv_w"], layer=layer, tm=tm_e, name=f"ffn_act{layer}",
                              comm=cm(f"ffn_act{layer}"))
        xout = _mm_nn(a, wt("f_w_down", layer), layer=0, tm=tm, tn=512, residual=xin, name=f"ffn_down{layer}",
                      comm=cm(f"ffn_down{layer}"))
        return h, up, upc, a, xout

    h_f0, up0, upc0, a0, x2 = ffn_fwd(x1, 0)
    h_m1 = _rmsnorm_fwd(x2, nm, layer=1, tm=tm_n, name="norm_mix1")
    z_c = _mm_nn(h_m1, wt("c_w_in", 0), layer=0, tm=tm, tn=768, out_dtype=BF16, name="c_in", comm=cm("c_in"))
    r = _mixer_c_fwd(z_c, small["c_conv_w"], tm=tm_e, name="mixer_c", comm=cm("mixer_c"))
    x3 = _mm_nn(r, wt("c_w_out", 0), layer=0, tm=tm, tn=512, residual=x2, name="c_out", comm=cm("c_out"))
    h_f1, up1, upc1, a1, x4 = ffn_fwd(x3, 1)
    loss, dx, sg["norm_final"] = _loss_head(x4, tgt, ngf, tm=tm_n, name="loss_head")

    def ffn_bwd(dx, xin, h, up, upc, a, layer):
        da = _mm_nt(dx, wt("f_w_down", layer), layer=0, tm=tm, tn=1408, out_dtype=BF16,
                    name=f"ffn_down_dx{layer}", comm=cm(f"ffn_down_dx{layer}"))
        plan.grad_ready("f_w_down", layer, _mm_tn(a, dx, shards=None, tk=1408, tn=1024, tt=tt,
                                                  name=f"ffn_down_dw{layer}", comm=cm(f"ffn_down_dw{layer}")))
        dup, dcw = _ffn_act_bwd(up, upc, da, small["f_conv_w"], layer=layer, tm=tm_e, name=f"ffn_act_bwd{layer}",
                                comm=cm(f"ffn_act_bwd{layer}"))
        dh = _mm_nt(dup, wt("f_w_up", layer), layer=0, tm=tm_r, tn=None, name=f"ffn_up_dx{layer}",
                    comm=cm(f"ffn_up_dx{layer}"))
        plan.grad_ready("f_w_up", layer, _mm_tn(h, dup, shards=N_CHIPS, tk=1024, tn=1408, tt=tt,
                                                name=f"ffn_up_dw{layer}", comm=cm(f"ffn_up_dw{layer}")))
        dxin, dg = _rmsnorm_bwd(xin, nf, dh, dx, layer=layer, tm=tm_n, name=f"norm_ffn_bwd{layer}",
                                comm=cm(f"norm_ffn_bwd{layer}"))
        return dxin, dg, dcw

    dx, dnf1, dfc1 = ffn_bwd(dx, x3, h_f1, up1, upc1, a1, 1)
    dr = _mm_nt(dx, wt("c_w_out", 0), layer=0, tm=tm, tn=512, out_dtype=BF16, name="c_out_dx", comm=cm("c_out_dx"))
    plan.grad_ready("c_w_out", 0, _mm_tn(r, dx, shards=None, tk=1024, tn=1024, tt=tt, name="c_out_dw",
                                         comm=cm("c_out_dw")))
    dz_c, dccw = _mixer_c_bwd(z_c, dr, small["c_conv_w"], tm=tm_e, name="mixer_c_bwd", comm=cm("mixer_c_bwd"))
    sg["c_conv_w"] = dccw.reshape(1, C_CONV, D_MODEL)
    dh = _mm_nt(dz_c, wt("c_w_in", 0), layer=0, tm=tm_r, tn=None, name="c_in_dx", comm=cm("c_in_dx"))
    plan.grad_ready("c_w_in", 0, _mm_tn(h_m1, dz_c, shards=N_CHIPS, tk=1024, tn=768, tt=tt, name="c_in_dw",
                                        comm=cm("c_in_dw")))
    dx, dnm1 = _rmsnorm_bwd(x2, nm, dh, dx, layer=1, tm=tm_n, name="norm_mix_bwd1", comm=cm("norm_mix_bwd1"))
    dx, dnf0, dfc0 = ffn_bwd(dx, x1, h_f0, up0, upc0, a0, 0)
    dyab = _mm_nt(dx, wt("ab_w_out", 0), layer=0, tm=tm, tn=512, out_dtype=BF16, name="ab_out_dx",
                  comm=cm("ab_out_dx"))
    plan.grad_ready("ab_w_out", 0, _mm_tn(yab, dx, shards=None, tk=1024, tn=1024, tt=tt, name="ab_out_dw",
                                          comm=cm("ab_out_dw")))
    (dza, dcb, sg["a_ln_g"], sg["a_ln_b"], dws, dbs, sg["b_ln_g"], sg["b_ln_b"]) = _mixer_ab_bwd_pre(
        z_ab, cb, dyab, small["a_ln_g"], small["a_ln_b"], w_s, b_s, small["b_ln_g"], small["b_ln_b"],
        tm=tm_e, name="mixer_ab_bwd", comm=cm("mixer_ab_bwd"))
    dzb, dbcw, sg["b_conv_b"] = _mixer_b_conv_bwd(z_ab, dcb, b_conv_w, tm=tm_e, name="mixer_b_conv_bwd",
                                                  comm=cm("mixer_b_conv_bwd"))
    sg["a_w_s"] = dws.reshape(1, A_HEADS, CHUNK, CHUNK)
    sg["a_b_s"] = dbs.reshape(1, A_HEADS, CHUNK)
    sg["b_conv_w"] = dbcw.reshape(1, B_CONV, D_B)
    dz_ab = jnp.concatenate([dza, dzb], axis=1)
    dh = _mm_nt(dz_ab, wt("ab_w_in", 0), layer=0, tm=tm_r, tn=None, name="ab_in_dx", comm=cm("ab_in_dx"))
    plan.grad_ready("ab_w_in", 0, _mm_tn(h_m0, dz_ab, shards=N_CHIPS, tk=1024, tn=512, tt=tt, name="ab_in_dw",
                                         comm=cm("ab_in_dw")))
    dx, dnm0 = _rmsnorm_bwd(x, nm, dh, dx, layer=0, tm=tm_n, name="norm_mix_bwd0", comm=cm("norm_mix_bwd0"))

    sg["norm_mix"] = [dnm0, dnm1]
    sg["norm_ffn"] = [dnf0, dnf1]
    sg["f_conv_w"] = [dfc0, dfc1]
    return loss, dx, sg


BLOCK_BYTES = 3 * 1024 * 1024


BF16_SUBLANES = 16


def _row_tile(rows, row_bytes, step=SUBLANES):
    best = None
    for tr in range(step, rows + 1, step):
        if rows % tr == 0 and tr * row_bytes <= BLOCK_BYTES:
            best = tr
    if best is None:
        raise ValueError(f"no row tile for {rows}")
    return best


def _place_scalars():
    x, y, c = lax.axis_index("x"), lax.axis_index("y"), lax.axis_index("c")
    return jnp.stack([c, 2 * x + y, 2 * (1 - x) + y, 2 * x + (1 - y), 2 * (1 - x) + (1 - y)]).astype(jnp.int32)


def _cast_into_slot(w, place, *, layer, name):
    L, rows, cols = w.shape
    tr = _row_tile(rows, cols * 4, BF16_SUBLANES)

    def body(place_ref, w_ref, o_ref):
        o_ref[...] = w_ref[...].astype(BF16)

    return pl.pallas_call(
        body, name=name,
        grid_spec=pltpu.PrefetchScalarGridSpec(
            num_scalar_prefetch=1, grid=(rows // tr,),
            in_specs=[pl.BlockSpec((None, tr, cols), lambda i, p: (layer, i, 0))],
            out_specs=pl.BlockSpec((None, None, tr, cols), lambda i, p: (0, p[1], i, 0))),
        out_shape=jax.ShapeDtypeStruct((1, N_CHIPS, rows, cols), BF16),
        compiler_params=_cparams(("parallel",), 32),
    )(place, w)


def _pair_sum(g, theirs, place, *, name):
    S, rows, cols = g.shape
    half = rows // 2
    tr = _row_tile(half, cols * 4, BF16_SUBLANES)
    nb = half // tr

    def body(place_ref, g_ref, t_ref, o_ref):
        o_ref[...] = (g_ref[...] + t_ref[...]).astype(BF16)

    spec = pl.BlockSpec((None, tr, cols), lambda s, i, p: (s, i, 0))
    return pl.pallas_call(
        body, name=name,
        grid_spec=pltpu.PrefetchScalarGridSpec(
            num_scalar_prefetch=1, grid=(S, nb),
            in_specs=[pl.BlockSpec((None, tr, cols), lambda s, i, p: (s, p[0] * nb + i, 0)), spec],
            out_specs=spec),
        out_shape=jax.ShapeDtypeStruct((S, half, cols), BF16),
        compiler_params=_cparams(("parallel", "parallel"), 32),
    )(place, g, theirs)


def _chip_sum(p, r, g_prev, place, *, layer, shape, name):
    L, rows, cols = shape
    half = rows // 2
    tr = _row_tile(half, cols * 4, BF16_SUBLANES)
    nb = half // tr

    def body(place_ref, p_ref, r_ref, *rest):
        o_ref = rest[-1]
        mine = p_ref[...].astype(F32)
        peers = [r_ref[j].astype(F32) for j in range(3)]
        acc = None
        for s in range(N_CHIPS):
            term = jnp.where(place_ref[1] == s, mine,
                             jnp.where(place_ref[2] == s, peers[0],
                                       jnp.where(place_ref[3] == s, peers[1], peers[2])))
            acc = term if acc is None else acc + term
        o_ref[...] = acc

    in_specs = [pl.BlockSpec((None, tr, cols), lambda i, pr: (pr[1], i, 0)),
                pl.BlockSpec((3, tr, cols), lambda i, pr: (0, i, 0))]
    args = [place, p, r]
    aliases = {}
    if g_prev is not None:
        in_specs.append(ANY)
        args.append(g_prev)
        aliases = {3: 0}
    return pl.pallas_call(
        body, name=name,
        grid_spec=pltpu.PrefetchScalarGridSpec(
            num_scalar_prefetch=1, grid=(nb,), in_specs=in_specs,
            out_specs=pl.BlockSpec((None, tr, cols), lambda i, pr: (layer, pr[0] * nb + i, 0))),
        out_shape=jax.ShapeDtypeStruct(shape, F32), input_output_aliases=aliases,
        compiler_params=_cparams(("parallel",), 32),
    )(*args)


def _adamw_math(w, g, m, v):
    m2 = ADAM_B1 * m + (1.0 - ADAM_B1) * g
    v2 = ADAM_B2 * v + (1.0 - ADAM_B2) * (g * g)
    m_hat = m2 / (1.0 - ADAM_B1 ** ADAM_STEP)
    v_hat = v2 / (1.0 - ADAM_B2 ** ADAM_STEP)
    delta = -ADAM_LR * (m_hat / (jnp.sqrt(v_hat) + ADAM_EPS) + ADAM_WD * w)
    return delta, m2, v2


def _adamw(w, g, m, v, *, name):
    L, rows, cols = w.shape
    tr = _row_tile(rows, cols * 4)

    def body(w_ref, g_ref, m_ref, v_ref, d_ref, m2_ref, v2_ref):
        d, m2, v2 = _adamw_math(w_ref[...], g_ref[...], m_ref[...], v_ref[...])
        d_ref[...] = d
        m2_ref[...] = m2
        v2_ref[...] = v2

    spec = pl.BlockSpec((None, tr, cols), lambda l, i: (l, i, 0))
    shape = jax.ShapeDtypeStruct(w.shape, F32)
    return pl.pallas_call(
        body, name=name, grid=(L, rows // tr), in_specs=[spec] * 4, out_specs=[spec] * 3,
        out_shape=[shape] * 3,
        compiler_params=_cparams(("parallel", "parallel"), 48),
    )(w, g, m, v)


def _exchange_packs(pack, *, reduce, name):
    R = pack.shape[0]
    ndev = 2 * N_CHIPS

    def body(p_ref, o_ref, *scratch):
        if reduce:
            buf, send, recv = scratch
        else:
            buf = o_ref
            send, recv = scratch
        x, y, c = lax.axis_index("x"), lax.axis_index("y"), lax.axis_index("c")
        me = 4 * x + 2 * y + c
        buf[me] = p_ref[...]
        sends = []
        for q in range(1, ndev):
            qx, qy, qc = (q >> 2) & 1, (q >> 1) & 1, q & 1
            peer = (x ^ qx, y ^ qy, c ^ qc)
            rc = _remote(p_ref, buf.at[me], send.at[q - 1], recv.at[q - 1], peer)
            rc.start()
            sends.append(rc)
        for q in range(1, ndev):
            qx, qy, qc = (q >> 2) & 1, (q >> 1) & 1, q & 1
            slot = buf.at[4 * (x ^ qx) + 2 * (y ^ qy) + (c ^ qc)]
            _remote(slot, slot, send.at[q - 1], recv.at[q - 1], (x ^ qx, y ^ qy, c ^ qc)).wait_recv()
        for rc in sends:
            rc.wait_send()
        if reduce:
            acc = buf[0]
            for d in range(1, ndev):
                acc = acc + buf[d]
            o_ref[...] = acc

    vm = pl.BlockSpec(memory_space=pltpu.VMEM)
    sems = [pltpu.SemaphoreType.DMA((ndev - 1,)), pltpu.SemaphoreType.DMA((ndev - 1,))]
    if reduce:
        out_shape = jax.ShapeDtypeStruct((R, LANES), F32)
        scratch = [pltpu.VMEM((ndev, R, LANES), F32)] + sems
    else:
        out_shape = jax.ShapeDtypeStruct((ndev, R, LANES), F32)
        scratch = sems
    return pl.pallas_call(
        body, name=name, in_specs=[vm], out_specs=vm, out_shape=out_shape, scratch_shapes=scratch,
        compiler_params=pltpu.CompilerParams(vmem_limit_bytes=VMEM_BYTES_MAX),
    )(pack)


PACK_UNIT = SUBLANES * LANES


def _pack(arrays):
    flat, sizes = [], []
    for a in arrays:
        pieces = a if isinstance(a, (list, tuple)) else [a]
        v = jnp.concatenate([p.reshape(-1) for p in pieces]) if len(pieces) > 1 else pieces[0].reshape(-1)
        size = v.shape[0]
        padded = -(-size // PACK_UNIT) * PACK_UNIT
        flat.append(jnp.pad(v, (0, padded - size)))
        sizes.append((size, padded))
    return jnp.concatenate(flat).reshape(-1, LANES), sizes


def _unpack(pack, sizes, shapes):
    v = pack.reshape(-1)
    out, off = [], 0
    for (size, padded), shape in zip(sizes, shapes):
        out.append(v[off:off + size].reshape(shape))
        off += padded
    return out


BIG = ("ab_w_in", "ab_w_out", "c_w_in", "c_w_out", "f_w_up", "f_w_down")
COL_SHARDED = ("ab_w_in", "c_w_in", "f_w_up")
SMALL_REPLICATED = ("norm_mix", "norm_ffn", "norm_final", "a_ln_g", "a_ln_b", "a_w_s", "a_b_s",
                    "b_conv_b", "b_ln_g", "b_ln_b")
SMALL_SHARDED = ("b_conv_w", "c_conv_w", "f_conv_w")
SMALL = SMALL_REPLICATED + SMALL_SHARDED
ALL_WEIGHTS = ("norm_mix", "norm_ffn", "norm_final", "ab_w_in", "a_ln_g", "a_ln_b", "a_w_s", "a_b_s",
               "b_conv_w", "b_conv_b", "b_ln_g", "b_ln_b", "ab_w_out", "c_w_in", "c_conv_w", "c_w_out",
               "f_w_up", "f_conv_w", "f_w_down")


SCHEDULE = {
    "mixer_ab": [("gi", "f_w_up", 0)],
    "ab_out": [("gd", "f_w_up", 0)],
    "ffn_up0": [("gi", "f_w_down", 0), ("gi", "c_w_in", 0)],
    "ffn_act0": [("gd", "f_w_down", 0), ("gd", "c_w_in", 0), ("gi", "f_w_up", 1)],
    "ffn_down0": [("gd", "f_w_up", 1), ("gi", "c_w_out", 0)],
    "c_in": [("gd", "c_w_out", 0), ("gi", "f_w_down", 1)],
    "mixer_c": [("gd", "f_w_down", 1)],
    "ffn_act_bwd1": [("px", "f_w_down", 1)],
    "ffn_up_dx1": [("cx", "f_w_down", 1, 0, 1)],
    "norm_ffn_bwd1": [("px", "f_w_up", 1)],
    "mixer_c_bwd": [("cx", "f_w_up", 1, 0, 2), ("px", "c_w_out", 0)],
    "c_in_dx": [("cx", "c_w_out", 0, 0, 1)],
    "c_in_dw": [("cx", "f_w_up", 1, 1, 2)],
    "norm_mix_bwd1": [("px", "c_w_in", 0)],
    "ffn_act_bwd0": [("cx", "c_w_in", 0, 0, 1), ("px", "f_w_down", 0)],
    "ffn_up_dx0": [("cx", "f_w_down", 0, 0, 1)],
    "norm_ffn_bwd0": [("px", "f_w_up", 0)],
    "mixer_ab_bwd": [("px", "ab_w_out", 0)],
    "mixer_b_conv_bwd": [("cx", "f_w_up", 0, 0, 1)],
    "ab_in_dx": [("cx", "ab_w_out", 0, 0, 1)],
    "norm_mix_bwd0": [("px", "ab_w_in", 0)],
}


class _Plan:
    def __init__(self, shapes, place):
        self.shapes, self.place, self.bufs = shapes, place, {}

    def weight(self, name, layer):
        g = self.bufs[f"w:{name}:{layer}"]
        if name in COL_SHARDED:
            return g
        _, S, rows, cols = g.shape
        return g.reshape(1, S * rows, cols)

    def grad_ready(self, name, layer, g):
        _, rows, cols = self.shapes[name]
        hbm = lambda a: pltpu.with_memory_space_constraint(a, pltpu.HBM)
        self.bufs[f"g:{name}:{layer}"] = g.reshape(N_CHIPS, rows, cols)
        self.bufs[f"t:{name}:{layer}"] = hbm(lax.empty((N_CHIPS, rows // 2, cols), F32))
        self.bufs[f"l:{name}:{layer}"] = hbm(lax.empty((3, rows // 2, cols), BF16))

    def job(self, kind, name, layer, part=0, parts=1):
        _, rows, cols = self.shapes[name]
        key = f"{name}:{layer}"
        if kind == "gi":
            return _job_gather_ici("w:" + key, rows)
        if kind == "gd":
            return _job_gather_d2d("w:" + key, rows)
        if kind == "px":
            return _job_pair_exchange("g:" + key, "t:" + key, rows)
        if kind == "cx":
            if "p:" + key not in self.bufs:
                self.bufs["p:" + key] = _pair_sum(self.bufs["g:" + key], self.bufs["t:" + key], self.place,
                                                  name=f"pair_sum_{name}{layer}")
            nr = rows // 2 // parts
            return _job_chip_exchange("p:" + key, "l:" + key, part * nr, nr)
        if kind == "ps":
            return _job_pair_share("G:" + name, layer, rows)
        raise ValueError(kind)

    def comm(self, call):
        specs = SCHEDULE.get(call)
        return None if specs is None else _Comm(self, [self.job(*spec) for spec in specs])


def _step(x, tgt, w, m, v):
    chip = 2 * lax.axis_index("x") + lax.axis_index("y")
    place = _place_scalars()
    plan = _Plan({n: w[n].shape for n in BIG}, place)
    items = [(n, l) for n in BIG for l in range(w[n].shape[0])]

    for n, l in items:
        plan.bufs[f"w:{n}:{l}"] = _cast_into_slot(w[n], place, layer=l, name=f"cast_{n}{l}")
    first = [("ab_w_in", 0), ("ab_w_out", 0)]
    _comm_only(plan, [[plan.job("gi", n, l) for n, l in first], [plan.job("gd", n, l) for n, l in first]],
               name="gather_first")
    conv_pack, conv_sizes = _pack([w[n] for n in SMALL_SHARDED])
    conv_all = _exchange_packs(conv_pack, reduce=False, name="gather_conv_weights")
    conv_shapes = [w[n].shape for n in SMALL_SHARDED]
    per_chip = [_unpack(conv_all[2 * s], conv_sizes, conv_shapes) for s in range(N_CHIPS)]
    small = {n: w[n] for n in SMALL_REPLICATED}
    for idx, n in enumerate(SMALL_SHARDED):
        small[n] = jnp.concatenate([per_chip[s][idx] for s in range(N_CHIPS)], axis=-1)

    loss, dx, sg = _local_step(x, tgt, small, plan)

    _comm_only(plan, [[plan.job("cx", "ab_w_in", 0)]], name="reduce_last")
    for n, l in items:
        plan.bufs["G:" + n] = _chip_sum(plan.bufs[f"p:{n}:{l}"], plan.bufs[f"l:{n}:{l}"], plan.bufs.get("G:" + n),
                                        place, layer=l, shape=w[n].shape, name=f"chip_sum_{n}{l}")
    _comm_only(plan, [[plan.job("ps", n, l) for n, l in items]], name="reduce_pair_share")
    grads_big = [plan.bufs["G:" + n] for n in BIG]

    g_pack, g_sizes = _pack([sg[n] for n in SMALL])
    g_sum = _exchange_packs(g_pack, reduce=True, name="allreduce_small_grads")
    full_shapes = [small[n].shape for n in SMALL]
    g_small = dict(zip(SMALL, _unpack(g_sum, g_sizes, full_shapes)))
    for n in SMALL_SHARDED:
        width = w[n].shape[-1]
        g_small[n] = lax.dynamic_slice_in_dim(g_small[n], chip * width, width, axis=g_small[n].ndim - 1)

    grad, delta, new_m, new_v = {}, {}, {}, {}
    for n, g in zip(BIG, grads_big):
        grad[n] = g
        delta[n], new_m[n], new_v[n] = _adamw(w[n], g, m[n], v[n], name=f"adamw_{n}")
    shapes = [w[n].shape for n in SMALL]
    wp, sizes = _pack([w[n] for n in SMALL])
    gp, _ = _pack([g_small[n] for n in SMALL])
    mp, _ = _pack([m[n] for n in SMALL])
    vp, _ = _pack([v[n] for n in SMALL])
    R = wp.shape[0]
    dp, m2p, v2p = _adamw(wp.reshape(1, R, LANES), gp.reshape(1, R, LANES), mp.reshape(1, R, LANES),
                          vp.reshape(1, R, LANES), name="adamw_small")
    for n, d_, m_, v_ in zip(SMALL, _unpack(dp, sizes, shapes), _unpack(m2p, sizes, shapes),
                             _unpack(v2p, sizes, shapes)):
        grad[n] = g_small[n]
        delta[n], new_m[n], new_v[n] = d_, m_, v_
    return loss, dx, grad, delta, new_m, new_v


def kernel(x, norm_mix, norm_ffn, norm_final, ab_w_in, a_ln_g, a_ln_b, a_w_s, a_b_s, b_conv_w, b_conv_b, b_ln_g, b_ln_b, ab_w_out, c_w_in, c_conv_w, c_w_out, f_w_up, f_conv_w, f_w_down, loss_target, m_norm_mix, m_norm_ffn, m_norm_final, m_ab_w_in, m_a_ln_g, m_a_ln_b, m_a_w_s, m_a_b_s, m_b_conv_w, m_b_conv_b, m_b_ln_g, m_b_ln_b, m_ab_w_out, m_c_w_in, m_c_conv_w, m_c_w_out, m_f_w_up, m_f_conv_w, m_f_w_down, v_norm_mix, v_norm_ffn, v_norm_final, v_ab_w_in, v_a_ln_g, v_a_ln_b, v_a_w_s, v_a_b_s, v_b_conv_w, v_b_conv_b, v_b_ln_g, v_b_ln_b, v_ab_w_out, v_c_w_in, v_c_conv_w, v_c_w_out, v_f_w_up, v_f_conv_w, v_f_w_down):
    given = dict(locals())
    w = {n: given[n] for n in ALL_WEIGHTS}
    m = {n: given["m_" + n] for n in ALL_WEIGHTS}
    v = {n: given["v_" + n] for n in ALL_WEIGHTS}
    T = x.shape[1]
    loss, dx, grad, delta, new_m, new_v = _step(x.reshape(T, D_MODEL), loss_target.reshape(T, D_MODEL), w, m, v)
    loss = lax.psum(loss[0, 0], ("x", "y", "c"))
    out = [loss, dx.reshape(x.shape)]
    for d in (grad, delta, new_m, new_v):
        out += [d[n] for n in ALL_WEIGHTS]
    return tuple(out)
```

```python
import functools
import math

import jax
import jax.numpy as jnp
from jax import lax
from jax.experimental import pallas as pl
from jax.experimental.pallas import tpu as pltpu

F32 = jnp.float32
BF16 = jnp.bfloat16

EPS = 1e-6
D_MODEL = 1024
CHUNK = 128
HEAD_DIM = 128
A_HEADS = 4
D_A = 512
D_B = 512
B_CONV = 31
C_CONV = 3
D_FF = 2816
F_CONV = 3
N_CHIPS = 4

ADAM_LR = 0.001
ADAM_B1 = 0.9
ADAM_B2 = 0.999
ADAM_EPS = 1e-08
ADAM_WD = 0.01
ADAM_STEP = 10

SUBLANES = 8
LANES = 128
HALO_SHORT = 16
HALO_LONG = 32
VMEM_BYTES_MAX = 60000 * 1024

INV_SQRT2 = 1.0 / math.sqrt(2.0)
INV_SQRT_2PI = 1.0 / math.sqrt(2.0 * math.pi)

MESH = pl.DeviceIdType.MESH


def _cparams(sem, vmem_mb):
    del vmem_mb
    return pltpu.CompilerParams(dimension_semantics=sem, vmem_limit_bytes=VMEM_BYTES_MAX)


def _pick(total, pref):
    for c in (2048, 1024, 512, 256, 128):
        if c <= pref and total % c == 0:
            return c
    raise ValueError(f"no tile for {total}")


def _sigmoid(x):
    return jax.nn.sigmoid(x)


def _silu(x):
    return x * _sigmoid(x)


def _dsilu(x):
    s = _sigmoid(x)
    return s * (1.0 + x * (1.0 - s))


def _gelu(x):
    return 0.5 * x * (1.0 + lax.erf(x * INV_SQRT2))


def _dgelu(x):
    return 0.5 * (1.0 + lax.erf(x * INV_SQRT2)) + x * jnp.exp(-0.5 * x * x) * INV_SQRT_2PI


def _ln_stats(x):
    mu = jnp.mean(x, axis=-1, keepdims=True)
    xc = x - mu
    var = jnp.mean(xc * xc, axis=-1, keepdims=True)
    r = lax.rsqrt(var + EPS)
    return xc * r, r


def _ln_bwd(dy, xh, r, g):
    dxh = dy * g
    m1 = jnp.mean(dxh, axis=-1, keepdims=True)
    m2 = jnp.mean(dxh * xh, axis=-1, keepdims=True)
    return r * (dxh - m1 - xh * m2)


def _rowsum(x):
    return jnp.sum(x, axis=0, keepdims=True)


ANY = pl.BlockSpec(memory_space=pltpu.HBM)


def _place():
    x, y, c = lax.axis_index("x"), lax.axis_index("y"), lax.axis_index("c")
    peers = [(1 - x, y), (x, 1 - y), (1 - x, 1 - y)]
    return x, y, c, 2 * x + y, (x, y, 1 - c), peers


def _half(rows, which):
    return pl.ds(which * (rows // 2), rows // 2)


def _remote(src, dst, send_sem, recv_sem, device):
    return pltpu.make_async_remote_copy(src_ref=src, dst_ref=dst, send_sem=send_sem, recv_sem=recv_sem,
                                        device_id=device, device_id_type=MESH)


class _Job:
    def __init__(self, reads, writes, ncopies, copies):
        self.reads, self.writes, self.ncopies, self.copies = reads, writes, ncopies, copies


def _share(rows, which, part, parts):
    nr = rows // 2 // parts
    return pl.ds(which * (rows // 2) + part * nr, nr)


def _job_gather_ici(name, rows, part, parts):
    def copies(src, dst, sem):
        x, y, c, k, sib, peers = _place()
        mine_rows = _share(rows, c, part, parts)
        out = []
        for j, (px, py) in enumerate(peers):
            mine = src[name].at[0, k, mine_rows]
            out.append((_remote(mine, dst[name].at[0, k, mine_rows], sem(j, 0), sem(j, 1), (px, py, c)),
                        _remote(mine, dst[name].at[0, 2 * px + py, mine_rows], sem(j, 0), sem(j, 1), (px, py, c))))
        return out
    return _Job([], [name], 3, copies)


def _job_gather_d2d(name, rows, part, parts):
    def copies(src, dst, sem):
        x, y, c, k, sib, peers = _place()
        out = []
        for j, (px, py) in enumerate(peers):
            landed = src[name].at[0, 2 * px + py, _share(rows, c, part, parts)]
            out.append((_remote(landed, dst[name].at[0, 2 * px + py, _share(rows, c, part, parts)],
                                sem(j, 0), sem(j, 1), sib),
                        _remote(landed, dst[name].at[0, 2 * px + py, _share(rows, 1 - c, part, parts)],
                                sem(j, 0), sem(j, 1), sib)))
        return out
    return _Job([], [name], 3, copies)


def _job_pair_exchange(gname, tname, rows):
    def copies(src, dst, sem):
        x, y, c, k, sib, peers = _place()
        cp = _remote(src[gname].at[:, _half(rows, 1 - c), :], dst[tname], sem(0, 0), sem(0, 1), sib)
        return [(cp, cp)]
    return _Job([gname], [tname], 1, copies)


def _job_chip_exchange(pname, lname, r0, nr):
    def copies(src, dst, sem):
        x, y, c, k, sib, peers = _place()
        out = []
        for j, (px, py) in enumerate(peers):
            cp = _remote(src[pname].at[2 * px + py, pl.ds(r0, nr)], dst[lname].at[j, pl.ds(r0, nr)],
                         sem(j, 0), sem(j, 1), (px, py, c))
            out.append((cp, cp))
        return out
    return _Job([pname], [lname], 3, copies)


def _job_pair_share(name, layer, rows):
    def copies(src, dst, sem):
        x, y, c, k, sib, peers = _place()
        mine = src[name].at[layer, _half(rows, c)]
        return [(_remote(mine, dst[name].at[layer, _half(rows, c)], sem(0, 0), sem(0, 1), sib),
                 _remote(mine, dst[name].at[layer, _half(rows, 1 - c)], sem(0, 0), sem(0, 1), sib))]
    return _Job([], [name], 1, copies)


class _Comm:
    def __init__(self, plan, jobs):
        self.plan, self.jobs = plan, jobs
        self.writes, self.reads = [], []
        for job in jobs:
            for n in job.writes:
                if n not in self.writes:
                    self.writes.append(n)
        for job in jobs:
            for n in job.reads:
                if n not in self.writes and n not in self.reads:
                    self.reads.append(n)
        self.ncopies = sum(job.ncopies for job in jobs)

    def descriptors(self, src, dst, sems, base):
        out = []
        for job in self.jobs:
            sem = lambda j, which, base=base: sems.at[base + j, which]
            out += job.copies(src, dst, sem)
            base += job.ncopies
        return out

    def start(self, src, dst, sems, base=0):
        for first, _ in self.descriptors(src, dst, sems, base):
            first.start()

    def finish(self, src, dst, sems, base=0):
        for _, landed in self.descriptors(src, dst, sems, base):
            landed.wait()


def _comm_operands(comm):
    bufs = comm.plan.bufs
    shapes = [jax.ShapeDtypeStruct(bufs[n].shape, bufs[n].dtype) for n in comm.writes]
    return [bufs[n] for n in comm.reads] + [bufs[n] for n in comm.writes], shapes


def _pallas(comm, body, *, name, grid, in_specs, out_specs, out_shape, compiler_params, scratch_shapes=()):
    if comm is None:
        return pl.pallas_call(body, name=name, grid=grid, in_specs=in_specs, out_specs=out_specs,
                              out_shape=out_shape, scratch_shapes=list(scratch_shapes),
                              compiler_params=compiler_params)
    single = not isinstance(out_shape, (list, tuple))
    base_specs = [out_specs] if single else list(out_specs)
    base_shape = [out_shape] if single else list(out_shape)
    nb, nr, nw, nbo, nsc = len(in_specs), len(comm.reads), len(comm.writes), len(base_specs), len(scratch_shapes)

    def wrapped(*refs):
        base_in, rd, wr_in = refs[:nb], refs[nb:nb + nr], refs[nb + nr:nb + nr + nw]
        o0 = nb + nr + nw
        base_out, wr_out = refs[o0:o0 + nbo], refs[o0 + nbo:o0 + nbo + nw]
        scratch, sems = refs[o0 + nbo + nw:o0 + nbo + nw + nsc], refs[-1]
        src = dict(zip(comm.reads, rd))
        src.update(zip(comm.writes, wr_in))
        dst = dict(zip(comm.writes, wr_out))
        first = functools.reduce(jnp.logical_and, [pl.program_id(a) == 0 for a in range(len(grid))])
        last = functools.reduce(jnp.logical_and,
                                [pl.program_id(a) == pl.num_programs(a) - 1 for a in range(len(grid))])

        @pl.when(first)
        def _():
            comm.start(src, dst, sems)
        body(*base_in, *base_out, *scratch)

        @pl.when(last)
        def _():
            comm.finish(src, dst, sems)

    operands, shapes = _comm_operands(comm)
    call = pl.pallas_call(
        wrapped, name=name, grid=grid, in_specs=list(in_specs) + [ANY] * (nr + nw),
        out_specs=base_specs + [ANY] * nw, out_shape=base_shape + shapes,
        input_output_aliases={nb + nr + q: nbo + q for q in range(nw)},
        scratch_shapes=list(scratch_shapes) + [pltpu.SemaphoreType.DMA((comm.ncopies, 2))],
        compiler_params=compiler_params)

    def run(*args):
        outs = call(*args, *operands)
        for q, n in enumerate(comm.writes):
            comm.plan.bufs[n] = outs[nbo + q]
        return outs[0] if single else list(outs[:nbo])

    return run


def _comm_only(plan, phases, *, name):
    comms = [_Comm(plan, jobs) for jobs in phases]
    both = _Comm(plan, [job for jobs in phases for job in jobs])
    nr, nw = len(both.reads), len(both.writes)

    def body(*refs):
        rd, wr_in, wr_out, sems = refs[:nr], refs[nr:nr + nw], refs[nr + nw:nr + 2 * nw], refs[-1]
        src = dict(zip(both.reads, rd))
        src.update(zip(both.writes, wr_in))
        dst = dict(zip(both.writes, wr_out))
        base = 0
        for comm in comms:
            comm.start(src, dst, sems, base)
            comm.finish(src, dst, sems, base)
            base += comm.ncopies

    operands, shapes = _comm_operands(both)
    outs = pl.pallas_call(
        body, name=name, in_specs=[ANY] * (nr + nw), out_specs=[ANY] * nw, out_shape=shapes,
        input_output_aliases={nr + q: q for q in range(nw)},
        scratch_shapes=[pltpu.SemaphoreType.DMA((both.ncopies, 2))],
    )(*operands)
    for q, n in enumerate(both.writes):
        plan.bufs[n] = outs[q]


def _mm_nn(a, w, *, layer, tm, tn, residual=None, out_dtype=F32, name, comm=None):
    T, K = a.shape
    if w.ndim == 4:
        _, S, _, n4 = w.shape
        N = S * n4
        bps = n4 // tn
        w_spec = pl.BlockSpec((None, None, K, tn), lambda j, i: (layer, j // bps, 0, j % bps))
    else:
        N = w.shape[2]
        w_spec = pl.BlockSpec((None, K, tn), lambda j, i: (layer, 0, j))
    in_specs = [pl.BlockSpec((tm, K), lambda j, i: (i, 0)), w_spec]
    args = [a, w]
    if residual is not None:
        in_specs.append(pl.BlockSpec((tm, tn), lambda j, i: (i, j)))
        args.append(residual)

    def body(*refs):
        a_ref, w_ref, o_ref = refs[0], refs[1], refs[-1]
        acc = jnp.dot(a_ref[...].astype(BF16), w_ref[...], preferred_element_type=F32)
        if residual is not None:
            acc = refs[2][...] + acc
        o_ref[...] = acc.astype(out_dtype)

    return _pallas(
        comm, body, name=name, grid=(N // tn, T // tm), in_specs=in_specs,
        out_specs=pl.BlockSpec((tm, tn), lambda j, i: (i, j)),
        out_shape=jax.ShapeDtypeStruct((T, N), out_dtype),
        compiler_params=_cparams(("parallel", "parallel"), 48),
    )(*args)


def _mm_nt(dy, w, *, layer, tm, tn, name, out_dtype=F32, comm=None):
    T = dy.shape[0]
    nt_dims = (((1,), (1,)), ((), ()))
    if w.ndim == 4:
        _, S, K, n4 = w.shape

        def body(dy_ref, w_ref, o_ref):
            @pl.when(pl.program_id(1) == 0)
            def _():
                o_ref[...] = jnp.zeros_like(o_ref)
            o_ref[...] += lax.dot_general(dy_ref[...].astype(BF16), w_ref[...], nt_dims,
                                          preferred_element_type=F32)

        return _pallas(
            comm, body, name=name, grid=(T // tm, S),
            in_specs=[pl.BlockSpec((tm, n4), lambda i, s: (i, s)),
                      pl.BlockSpec((None, None, K, n4), lambda i, s: (layer, s, 0, 0))],
            out_specs=pl.BlockSpec((tm, K), lambda i, s: (i, 0)),
            out_shape=jax.ShapeDtypeStruct((T, K), F32),
            compiler_params=_cparams(("parallel", "arbitrary"), 48),
        )(dy, w)
    _, R, N = w.shape

    def body2(dy_ref, w_ref, o_ref):
        o_ref[...] = lax.dot_general(dy_ref[...].astype(BF16), w_ref[...], nt_dims,
                                     preferred_element_type=F32).astype(out_dtype)

    return _pallas(
        comm, body2, name=name, grid=(R // tn, T // tm),
        in_specs=[pl.BlockSpec((tm, N), lambda j, i: (i, 0)),
                  pl.BlockSpec((None, tn, N), lambda j, i: (layer, j, 0))],
        out_specs=pl.BlockSpec((tm, tn), lambda j, i: (i, j)),
        out_shape=jax.ShapeDtypeStruct((T, R), out_dtype),
        compiler_params=_cparams(("parallel", "parallel"), 48),
    )(dy, w)


def _mm_tn(a, dy, *, shards, tk, tn, tt, name, comm=None):
    T, K = a.shape
    N = dy.shape[1]
    tn_dims = (((0,), (0,)), ((), ()))

    def body(a_ref, dy_ref, o_ref):
        @pl.when(pl.program_id(2) == 0)
        def _():
            o_ref[...] = jnp.zeros_like(o_ref)
        o_ref[...] += lax.dot_general(a_ref[...].astype(BF16), dy_ref[...].astype(BF16), tn_dims,
                                      preferred_element_type=F32)

    if shards is None:
        out_spec = pl.BlockSpec((tk, tn), lambda k, n, t: (k, n))
        out_shape = jax.ShapeDtypeStruct((K, N), F32)
    else:
        n4 = N // shards
        bps = n4 // tn
        out_spec = pl.BlockSpec((None, tk, tn), lambda k, n, t: (n // bps, k, n % bps))
        out_shape = jax.ShapeDtypeStruct((shards, K, n4), F32)
    return _pallas(
        comm, body, name=name, grid=(K // tk, N // tn, T // tt),
        in_specs=[pl.BlockSpec((tt, tk), lambda k, n, t: (t, k)),
                  pl.BlockSpec((tt, tn), lambda k, n, t: (t, n))],
        out_specs=out_spec, out_shape=out_shape,
        compiler_params=_cparams(("parallel", "parallel", "arbitrary"), 48),
    )(a, dy)


def _rmsnorm_fwd(x, g, *, layer, tm, name, comm=None):
    T, D = x.shape

    def body(x_ref, g_ref, h_ref):
        xf = x_ref[...]
        r = lax.rsqrt(jnp.mean(xf * xf, axis=-1, keepdims=True) + EPS)
        h_ref[...] = (xf * r * g_ref[...]).astype(BF16)

    return _pallas(
        comm, body, name=name, grid=(T // tm,),
        in_specs=[pl.BlockSpec((tm, D), lambda i: (i, 0)),
                  pl.BlockSpec((None, 1, D), lambda i: (layer, 0, 0))],
        out_specs=pl.BlockSpec((tm, D), lambda i: (i, 0)),
        out_shape=jax.ShapeDtypeStruct((T, D), BF16),
        compiler_params=_cparams(("parallel",), 32),
    )(x, g)


def _rmsnorm_bwd(x, g, dh, dres, *, layer, tm, name, comm=None):
    T, D = x.shape

    def body(x_ref, g_ref, dh_ref, dres_ref, dx_ref, dg_ref):
        @pl.when(pl.program_id(0) == 0)
        def _():
            dg_ref[...] = jnp.zeros_like(dg_ref)
        xf = x_ref[...]
        r = lax.rsqrt(jnp.mean(xf * xf, axis=-1, keepdims=True) + EPS)
        xh = xf * r
        dh = dh_ref[...]
        dg_ref[...] += _rowsum(dh * xh)
        dxh = dh * g_ref[...]
        dx_ref[...] = dres_ref[...] + r * (dxh - xh * jnp.mean(dxh * xh, axis=-1, keepdims=True))

    return _pallas(
        comm, body, name=name, grid=(T // tm,),
        in_specs=[pl.BlockSpec((tm, D), lambda i: (i, 0)),
                  pl.BlockSpec((None, 1, D), lambda i: (layer, 0, 0)),
                  pl.BlockSpec((tm, D), lambda i: (i, 0)),
                  pl.BlockSpec((tm, D), lambda i: (i, 0))],
        out_specs=[pl.BlockSpec((tm, D), lambda i: (i, 0)),
                   pl.BlockSpec((1, D), lambda i: (0, 0))],
        out_shape=[jax.ShapeDtypeStruct((T, D), F32), jax.ShapeDtypeStruct((1, D), F32)],
        compiler_params=_cparams(("arbitrary",), 40),
    )(x, g, dh, dres)


def _loss_head(x, tgt, g, *, tm, name, comm=None):
    T, D = x.shape

    def body(x_ref, t_ref, g_ref, loss_ref, dx_ref, dg_ref):
        @pl.when(pl.program_id(0) == 0)
        def _():
            dg_ref[...] = jnp.zeros_like(dg_ref)
            loss_ref[...] = jnp.zeros_like(loss_ref)
        xf = x_ref[...]
        gg = g_ref[...]
        r = lax.rsqrt(jnp.mean(xf * xf, axis=-1, keepdims=True) + EPS)
        xh = xf * r
        err = xh * gg - t_ref[...]
        row = jnp.mean(err * err, axis=-1, keepdims=True)
        loss_ref[...] += 0.5 * jnp.sum(row, axis=0, keepdims=True)
        dy = err * (1.0 / D)
        dg_ref[...] += _rowsum(dy * xh)
        dxh = dy * gg
        dx_ref[...] = r * (dxh - xh * jnp.mean(dxh * xh, axis=-1, keepdims=True))

    return _pallas(
        comm, body, name=name, grid=(T // tm,),
        in_specs=[pl.BlockSpec((tm, D), lambda i: (i, 0)),
                  pl.BlockSpec((tm, D), lambda i: (i, 0)),
                  pl.BlockSpec((1, D), lambda i: (0, 0))],
        out_specs=[pl.BlockSpec((1, 1), lambda i: (0, 0)),
                   pl.BlockSpec((tm, D), lambda i: (i, 0)),
                   pl.BlockSpec((1, D), lambda i: (0, 0))],
        out_shape=[jax.ShapeDtypeStruct((1, 1), F32), jax.ShapeDtypeStruct((T, D), F32),
                   jax.ShapeDtypeStruct((1, D), F32)],
        compiler_params=_cparams(("arbitrary",), 40),
    )(x, tgt, g)


CONV_ROWS = 64
CONV_COLS = 256


def _halo_prev_index(tm, halo):
    per = tm // halo
    return lambda i: jnp.maximum(i * per - 1, 0)


def _halo_next_index(tm, halo, total):
    per = tm // halo
    last = total // halo - 1
    return lambda i: jnp.minimum((i + 1) * per, last)


def _causal_mask():
    t = lax.broadcasted_iota(jnp.int32, (CHUNK, CHUNK), 0)
    s = lax.broadcasted_iota(jnp.int32, (CHUNK, CHUNK), 1)
    return s <= t


def _mixer_ab_fwd(z, a_ln_g, a_ln_b, w_s, b_s, conv_w, conv_b, b_ln_g, b_ln_b, *, tm, name, comm=None):
    T = z.shape[0]
    nchunk = tm // CHUNK
    halo = HALO_LONG

    def body(za_ref, zb_ref, zh_ref, alg_ref, alb_ref, ws_ref, bs_ref, cw_ref, cbias_ref,
             blg_ref, blb_ref, y_ref, cb_ref, ext_ref):
        i = pl.program_id(0)
        gu = _gelu(za_ref[:, :D_A].astype(F32))
        gv = _gelu(za_ref[:, D_A:].astype(F32))
        xh, _ = _ln_stats(gv)
        lv = (xh * alg_ref[...] + alb_ref[...]).astype(BF16)
        mask = _causal_mask()
        for h in range(A_HEADS):
            wm = jnp.where(mask, ws_ref[h], 0.0).astype(BF16)
            cols = slice(h * HEAD_DIM, (h + 1) * HEAD_DIM)
            for c in range(nchunk):
                rows = slice(c * CHUNK, (c + 1) * CHUNK)
                mixed = jnp.dot(wm, lv[rows, cols], preferred_element_type=F32) + bs_ref[h]
                y_ref[rows, cols] = (gu[rows, cols] * mixed).astype(BF16)
        ext_ref[halo:halo + tm, :] = zb_ref[:, :D_B].astype(F32) * _sigmoid(zb_ref[:, D_B:].astype(F32))
        prev = zh_ref[:, :D_B].astype(F32) * _sigmoid(zh_ref[:, D_B:].astype(F32))
        ext_ref[0:halo, :] = jnp.where(i > 0, prev, 0.0)
        for rb in range(tm // CONV_ROWS):
            for cb in range(D_B // CONV_COLS):
                cs = slice(cb * CONV_COLS, (cb + 1) * CONV_COLS)
                window = ext_ref[rb * CONV_ROWS:rb * CONV_ROWS + CONV_ROWS + halo, cs]
                acc = jnp.zeros((CONV_ROWS, CONV_COLS), F32)
                for k in range(B_CONV):
                    shifted = _rows_after(window, halo - (B_CONV - 1) + k)[:CONV_ROWS]
                    acc = acc + cw_ref[k:k + 1, cs] * shifted
                cb_ref[rb * CONV_ROWS:(rb + 1) * CONV_ROWS, cs] = acc + cbias_ref[:, cs]
        xhb, _ = _ln_stats(cb_ref[...])
        y_ref[:, D_A:] = _silu(xhb * blg_ref[...] + blb_ref[...]).astype(BF16)

    row = lambda i: (i, 0)
    par = lambda i: (0, 0)
    return _pallas(
        comm, body, name=name, grid=(T // tm,),
        in_specs=[pl.BlockSpec((tm, 2 * D_A), lambda i: (i, 0)),
                  pl.BlockSpec((tm, 2 * D_B), lambda i: (i, 1)),
                  pl.BlockSpec((halo, 2 * D_B), lambda i: (_halo_prev_index(tm, halo)(i), 1)),
                  pl.BlockSpec((1, D_A), par), pl.BlockSpec((1, D_A), par),
                  pl.BlockSpec((A_HEADS, CHUNK, CHUNK), lambda i: (0, 0, 0)),
                  pl.BlockSpec((A_HEADS, CHUNK, 1), lambda i: (0, 0, 0)),
                  pl.BlockSpec((B_CONV, D_B), par), pl.BlockSpec((1, D_B), par),
                  pl.BlockSpec((1, D_B), par), pl.BlockSpec((1, D_B), par)],
        out_specs=[pl.BlockSpec((tm, D_A + D_B), row), pl.BlockSpec((tm, D_B), row)],
        out_shape=[jax.ShapeDtypeStruct((T, D_A + D_B), BF16), jax.ShapeDtypeStruct((T, D_B), F32)],
        scratch_shapes=[pltpu.VMEM((halo + tm, D_B), F32)],
        compiler_params=_cparams(("parallel",), 40),
    )(z, z, z, a_ln_g, a_ln_b, w_s, b_s, conv_w, conv_b, b_ln_g, b_ln_b)


def _mixer_ab_bwd_pre(z, cb, dy, a_ln_g, a_ln_b, w_s, b_s, b_ln_g, b_ln_b, *, tm, name, comm=None):
    T = z.shape[0]
    nchunk = tm // CHUNK
    tn_dims = (((0,), (0,)), ((), ()))
    nt_dims = (((1,), (1,)), ((), ()))

    def body(za_ref, cb_ref, dy_ref, alg_ref, alb_ref, ws_ref, bs_ref, blg_ref, blb_ref,
             dza_ref, dcb_ref, dalg_ref, dalb_ref, dws_ref, dbs_ref, dblg_ref, dblb_ref,
             dlv_ref):
        @pl.when(pl.program_id(0) == 0)
        def _():
            for ref in (dalg_ref, dalb_ref, dws_ref, dbs_ref, dblg_ref, dblb_ref):
                ref[...] = jnp.zeros_like(ref)
        ua = za_ref[:, :D_A].astype(F32)
        va = za_ref[:, D_A:].astype(F32)
        gu = _gelu(ua)
        gv = _gelu(va)
        xh, r = _ln_stats(gv)
        alg = alg_ref[...]
        lv = (xh * alg + alb_ref[...]).astype(BF16)
        dya = dy_ref[:, :D_A].astype(F32)
        mask = _causal_mask()
        for h in range(A_HEADS):
            wm = jnp.where(mask, ws_ref[h], 0.0).astype(BF16)
            cols = slice(h * HEAD_DIM, (h + 1) * HEAD_DIM)
            dwm = jnp.zeros((CHUNK, CHUNK), F32)
            dbs = jnp.zeros((CHUNK, 1), F32)
            for c in range(nchunk):
                rows = slice(c * CHUNK, (c + 1) * CHUNK)
                lvb = lv[rows, cols]
                mixed = jnp.dot(wm, lvb, preferred_element_type=F32) + bs_ref[h]
                dyb = dya[rows, cols]
                dza_ref[rows, cols] = (dyb * mixed * _dgelu(ua[rows, cols])).astype(BF16)
                dmixed = dyb * gu[rows, cols]
                dmb = dmixed.astype(BF16)
                dlv_ref[rows, cols] = lax.dot_general(wm, dmb, tn_dims, preferred_element_type=F32)
                dwm = dwm + lax.dot_general(dmb, lvb, nt_dims, preferred_element_type=F32)
                dbs = dbs + jnp.sum(dmixed, axis=1, keepdims=True)
            dws_ref[h] += jnp.where(mask, dwm, 0.0)
            dbs_ref[h] += dbs
        dlv = dlv_ref[...]
        dalg_ref[...] += _rowsum(dlv * xh)
        dalb_ref[...] += _rowsum(dlv)
        dgv = _ln_bwd(dlv, xh, r, alg)
        dza_ref[:, D_A:] = (dgv * _dgelu(va)).astype(BF16)
        xhb, rb = _ln_stats(cb_ref[...])
        blg = blg_ref[...]
        lb = xhb * blg + blb_ref[...]
        dlb = dy_ref[:, D_A:].astype(F32) * _dsilu(lb)
        dblg_ref[...] += _rowsum(dlb * xhb)
        dblb_ref[...] += _rowsum(dlb)
        dcb_ref[...] = _ln_bwd(dlb, xhb, rb, blg)

    row = lambda i: (i, 0)
    par = lambda i: (0, 0)
    par3 = lambda i: (0, 0, 0)
    return _pallas(
        comm, body, name=name, grid=(T // tm,),
        in_specs=[pl.BlockSpec((tm, 2 * D_A), row), pl.BlockSpec((tm, D_B), row),
                  pl.BlockSpec((tm, D_A + D_B), row),
                  pl.BlockSpec((1, D_A), par), pl.BlockSpec((1, D_A), par),
                  pl.BlockSpec((A_HEADS, CHUNK, CHUNK), par3),
                  pl.BlockSpec((A_HEADS, CHUNK, 1), par3),
                  pl.BlockSpec((1, D_B), par), pl.BlockSpec((1, D_B), par)],
        out_specs=[pl.BlockSpec((tm, 2 * D_A), row), pl.BlockSpec((tm, D_B), row),
                   pl.BlockSpec((1, D_A), par), pl.BlockSpec((1, D_A), par),
                   pl.BlockSpec((A_HEADS, CHUNK, CHUNK), par3),
                   pl.BlockSpec((A_HEADS, CHUNK, 1), par3),
                   pl.BlockSpec((1, D_B), par), pl.BlockSpec((1, D_B), par)],
        out_shape=[jax.ShapeDtypeStruct((T, 2 * D_A), BF16), jax.ShapeDtypeStruct((T, D_B), F32),
                   jax.ShapeDtypeStruct((1, D_A), F32), jax.ShapeDtypeStruct((1, D_A), F32),
                   jax.ShapeDtypeStruct((A_HEADS, CHUNK, CHUNK), F32),
                   jax.ShapeDtypeStruct((A_HEADS, CHUNK, 1), F32),
                   jax.ShapeDtypeStruct((1, D_B), F32), jax.ShapeDtypeStruct((1, D_B), F32)],
        scratch_shapes=[pltpu.VMEM((tm, D_A), F32)],
        compiler_params=_cparams(("arbitrary",), 40),
    )(z, cb, dy, a_ln_g, a_ln_b, w_s, b_s, b_ln_g, b_ln_b)


def _mixer_b_conv_bwd(z, dcb, conv_w, *, tm, name, comm=None):
    T = z.shape[0]
    halo = HALO_LONG

    def body(zb_ref, dcb_ref, dcn_ref, cw_ref, dzb_ref, dcw_ref, dbias_ref, dext_ref):
        i = pl.program_id(0)
        last = pl.num_programs(0) - 1

        @pl.when(i == 0)
        def _():
            dcw_ref[...] = jnp.zeros_like(dcw_ref)
            dbias_ref[...] = jnp.zeros_like(dbias_ref)
        dcb = dcb_ref[...]
        dext_ref[0:tm, :] = dcb
        dext_ref[tm:tm + halo, :] = jnp.where(i < last, dcn_ref[...], 0.0)
        dbias_ref[...] += _rowsum(dcb)
        for rb in range(tm // CONV_ROWS):
            for cb in range(D_B // CONV_COLS):
                cs = slice(cb * CONV_COLS, (cb + 1) * CONV_COLS)
                gcs = slice(D_B + cb * CONV_COLS, D_B + (cb + 1) * CONV_COLS)
                rs = slice(rb * CONV_ROWS, (rb + 1) * CONV_ROWS)
                xbb = zb_ref[rs, cs].astype(F32)
                sgb = _sigmoid(zb_ref[rs, gcs].astype(F32))
                yb0 = xbb * sgb
                window = dext_ref[rb * CONV_ROWS:rb * CONV_ROWS + CONV_ROWS + halo, cs]
                acc = jnp.zeros((CONV_ROWS, CONV_COLS), F32)
                for k in range(B_CONV):
                    shifted = _rows_after(window, (B_CONV - 1) - k)[:CONV_ROWS]
                    acc = acc + cw_ref[k:k + 1, cs] * shifted
                    dcw_ref[k:k + 1, cs] += _rowsum(shifted * yb0)
                dzb_ref[rs, cs] = (acc * sgb).astype(BF16)
                dzb_ref[rs, gcs] = (acc * xbb * sgb * (1.0 - sgb)).astype(BF16)

    row = lambda i: (i, 0)
    par = lambda i: (0, 0)
    return _pallas(
        comm, body, name=name, grid=(T // tm,),
        in_specs=[pl.BlockSpec((tm, 2 * D_B), lambda i: (i, 1)),
                  pl.BlockSpec((tm, D_B), row),
                  pl.BlockSpec((halo, D_B), lambda i: (_halo_next_index(tm, halo, T)(i), 0)),
                  pl.BlockSpec((B_CONV, D_B), par)],
        out_specs=[pl.BlockSpec((tm, 2 * D_B), row), pl.BlockSpec((B_CONV, D_B), par),
                   pl.BlockSpec((1, D_B), par)],
        out_shape=[jax.ShapeDtypeStruct((T, 2 * D_B), BF16), jax.ShapeDtypeStruct((B_CONV, D_B), F32),
                   jax.ShapeDtypeStruct((1, D_B), F32)],
        scratch_shapes=[pltpu.VMEM((tm + halo, D_B), F32)],
        compiler_params=_cparams(("arbitrary",), 40),
    )(z, dcb, dcb, conv_w)


def _rows_before(x, a):
    return x if a == 0 else pltpu.roll(x, a, axis=0)


def _rows_after(x, a):
    return x if a == 0 else pltpu.roll(x, x.shape[0] - a, axis=0)


def _conv3(w_ref, x, halo, cs):
    acc = w_ref[2:3, cs] * x[halo:]
    acc = acc + w_ref[1:2, cs] * _rows_before(x, 1)[halo:]
    return acc + w_ref[0:1, cs] * _rows_before(x, 2)[halo:]


def _mixer_c_fwd(z, conv_w, *, tm, name, comm=None):
    T = z.shape[0]
    D = D_MODEL
    halo = HALO_SHORT
    W = CONV_COLS

    def body(bg_ref, cg_ref, xv_ref, cgh_ref, xvh_ref, w_ref, r_ref):
        i = pl.program_id(0)
        for cb in range(D // W):
            cs = slice(cb * W, (cb + 1) * W)
            prev = jnp.where(i > 0, cgh_ref[:, cs].astype(F32) * xvh_ref[:, cs].astype(F32), 0.0)
            p = jnp.concatenate([prev, cg_ref[:, cs].astype(F32) * xv_ref[:, cs].astype(F32)], axis=0)
            r_ref[:, cs] = (bg_ref[:, cs].astype(F32) * _conv3(w_ref, p, halo, cs)).astype(BF16)

    hp = _halo_prev_index(tm, halo)
    return _pallas(
        comm, body, name=name, grid=(T // tm,),
        in_specs=[pl.BlockSpec((tm, D), lambda i: (i, 0)), pl.BlockSpec((tm, D), lambda i: (i, 1)),
                  pl.BlockSpec((tm, D), lambda i: (i, 2)),
                  pl.BlockSpec((halo, D), lambda i: (hp(i), 1)),
                  pl.BlockSpec((halo, D), lambda i: (hp(i), 2)),
                  pl.BlockSpec((None, C_CONV, D), lambda i: (0, 0, 0))],
        out_specs=pl.BlockSpec((tm, D), lambda i: (i, 0)),
        out_shape=jax.ShapeDtypeStruct((T, D), BF16),
        compiler_params=_cparams(("parallel",), 40),
    )(z, z, z, z, z, conv_w)


def _mixer_c_bwd(z, dr, conv_w, *, tm, name, comm=None):
    T = z.shape[0]
    D = D_MODEL
    halo = HALO_SHORT
    W = CONV_COLS

    def body(bg_ref, cg_ref, xv_ref, cgh_ref, xvh_ref, bgn_ref, dr_ref, drn_ref, w_ref, dz_ref, dw_ref):
        i = pl.program_id(0)
        last = pl.num_programs(0) - 1

        @pl.when(i == 0)
        def _():
            dw_ref[...] = jnp.zeros_like(dw_ref)
        for cb in range(D // W):
            cs = slice(cb * W, (cb + 1) * W)
            cg = cg_ref[:, cs].astype(F32)
            xv = xv_ref[:, cs].astype(F32)
            dr = dr_ref[:, cs].astype(F32)
            p = cg * xv
            prev = jnp.where(i > 0, cgh_ref[:, cs].astype(F32) * xvh_ref[:, cs].astype(F32), 0.0)
            q = _conv3(w_ref, jnp.concatenate([prev, p], axis=0), halo, cs)
            dz_ref[:, cs] = (dr * q).astype(BF16)
            nxt = jnp.where(i < last, drn_ref[:, cs].astype(F32) * bgn_ref[:, cs].astype(F32), 0.0)
            dq = jnp.concatenate([dr * bg_ref[:, cs].astype(F32), nxt], axis=0)
            dp = None
            for k in range(C_CONV):
                shifted = _rows_after(dq, 2 - k)[:tm]
                term = w_ref[k:k + 1, cs] * shifted
                dp = term if dp is None else dp + term
                dw_ref[k:k + 1, cs] += _rowsum(shifted * p)
            dz_ref[:, D + cb * W:D + (cb + 1) * W] = (dp * xv).astype(BF16)
            dz_ref[:, 2 * D + cb * W:2 * D + (cb + 1) * W] = (dp * cg).astype(BF16)

    hp = _halo_prev_index(tm, halo)
    hn = _halo_next_index(tm, halo, T)
    return _pallas(
        comm, body, name=name, grid=(T // tm,),
        in_specs=[pl.BlockSpec((tm, D), lambda i: (i, 0)), pl.BlockSpec((tm, D), lambda i: (i, 1)),
                  pl.BlockSpec((tm, D), lambda i: (i, 2)),
                  pl.BlockSpec((halo, D), lambda i: (hp(i), 1)),
                  pl.BlockSpec((halo, D), lambda i: (hp(i), 2)),
                  pl.BlockSpec((halo, D), lambda i: (hn(i), 0)),
                  pl.BlockSpec((tm, D), lambda i: (i, 0)),
                  pl.BlockSpec((halo, D), lambda i: (hn(i), 0)),
                  pl.BlockSpec((None, C_CONV, D), lambda i: (0, 0, 0))],
        out_specs=[pl.BlockSpec((tm, 3 * D), lambda i: (i, 0)),
                   pl.BlockSpec((C_CONV, D), lambda i: (0, 0))],
        out_shape=[jax.ShapeDtypeStruct((T, 3 * D), BF16), jax.ShapeDtypeStruct((C_CONV, D), F32)],
        compiler_params=_cparams(("arbitrary",), 48),
    )(z, z, z, z, z, z, dr, dr, conv_w)


FFN_COLS = 128


def _ffn_act_fwd(up, conv_w, *, layer, tm, name, comm=None):
    T = up.shape[0]
    halo = HALO_SHORT
    W = FFN_COLS

    def body(up_ref, uph_ref, w_ref, a_ref, upc_ref):
        i = pl.program_id(0)
        def conv(cs):
            prev = jnp.where(i > 0, uph_ref[:, cs], jnp.zeros((halo, W), BF16))
            return _conv3(w_ref, jnp.concatenate([prev, up_ref[:, cs]], axis=0).astype(F32), halo, cs)

        for cb in range(D_FF // W):
            gs = slice(cb * W, (cb + 1) * W)
            vs = slice(D_FF + cb * W, D_FF + (cb + 1) * W)
            g = conv(gs)
            v = conv(vs)
            upc_ref[:, gs] = g.astype(BF16)
            upc_ref[:, vs] = v.astype(BF16)
            a_ref[:, gs] = (_silu(g) * v).astype(BF16)

    return _pallas(
        comm, body, name=name, grid=(T // tm,),
        in_specs=[pl.BlockSpec((tm, 2 * D_FF), lambda i: (i, 0)),
                  pl.BlockSpec((halo, 2 * D_FF), lambda i: (_halo_prev_index(tm, halo)(i), 0)),
                  pl.BlockSpec((None, F_CONV, 2 * D_FF), lambda i: (layer, 0, 0))],
        out_specs=[pl.BlockSpec((tm, D_FF), lambda i: (i, 0)),
                   pl.BlockSpec((tm, 2 * D_FF), lambda i: (i, 0))],
        out_shape=[jax.ShapeDtypeStruct((T, D_FF), BF16), jax.ShapeDtypeStruct((T, 2 * D_FF), BF16)],
        compiler_params=_cparams(("parallel",), 48),
    )(up, up, conv_w)


def _ffn_act_bwd(up, upc, da, conv_w, *, layer, tm, name, comm=None):
    T = up.shape[0]
    halo = HALO_SHORT
    W = FFN_COLS

    def body(up_ref, upc_ref, upcn_ref, da_ref, dan_ref, w_ref, dup_ref, dw_ref):
        i = pl.program_id(0)
        last = pl.num_programs(0) - 1

        @pl.when(i == 0)
        def _():
            dw_ref[...] = jnp.zeros_like(dw_ref)
        live = jnp.where(i < last, 1.0, 0.0)
        for cb in range(D_FF // W):
            gs = slice(cb * W, (cb + 1) * W)
            vs = slice(D_FF + cb * W, D_FF + (cb + 1) * W)
            g = jnp.concatenate([upc_ref[:, gs], upcn_ref[:, gs]], axis=0).astype(F32)
            v = jnp.concatenate([upc_ref[:, vs], upcn_ref[:, vs]], axis=0).astype(F32)
            da = jnp.concatenate([da_ref[:, gs].astype(F32), dan_ref[:, gs].astype(F32) * live], axis=0)
            s = _sigmoid(g)
            silu = g * s
            grads = (da * v * (s * (1.0 + g * (1.0 - s))), da * silu)
            for cs, d in zip((gs, vs), grads):
                u = up_ref[:, cs].astype(F32)
                acc = None
                for k in range(F_CONV):
                    shifted = _rows_after(d, 2 - k)[:tm]
                    term = w_ref[k:k + 1, cs] * shifted
                    acc = term if acc is None else acc + term
                    dw_ref[k:k + 1, cs] += _rowsum(shifted * u)
                dup_ref[:, cs] = acc.astype(BF16)

    hn = _halo_next_index(tm, halo, T)
    return _pallas(
        comm, body, name=name, grid=(T // tm,),
        in_specs=[pl.BlockSpec((tm, 2 * D_FF), lambda i: (i, 0)),
                  pl.BlockSpec((tm, 2 * D_FF), lambda i: (i, 0)),
                  pl.BlockSpec((halo, 2 * D_FF), lambda i: (hn(i), 0)),
                  pl.BlockSpec((tm, D_FF), lambda i: (i, 0)),
                  pl.BlockSpec((halo, D_FF), lambda i: (hn(i), 0)),
                  pl.BlockSpec((None, F_CONV, 2 * D_FF), lambda i: (layer, 0, 0))],
        out_specs=[pl.BlockSpec((tm, 2 * D_FF), lambda i: (i, 0)),
                   pl.BlockSpec((F_CONV, 2 * D_FF), lambda i: (0, 0))],
        out_shape=[jax.ShapeDtypeStruct((T, 2 * D_FF), BF16),
                   jax.ShapeDtypeStruct((F_CONV, 2 * D_FF), F32)],
        compiler_params=_cparams(("arbitrary",), 56),
    )(up, upc, upc, da, da, conv_w)


def _local_step(x, tgt, small, plan):
    T = x.shape[0]
    tm_e = _pick(T, 256)
    tm_n = _pick(T, 512)
    tm = _pick(T, 1024)
    tm_r = _pick(T, 2048)
    tt = _pick(T, 512)
    nm = small["norm_mix"].reshape(2, 1, D_MODEL)
    nf = small["norm_ffn"].reshape(2, 1, D_MODEL)
    ngf = small["norm_final"].reshape(1, D_MODEL)
    b_s = small["a_b_s"].reshape(A_HEADS, CHUNK, 1)
    w_s = small["a_w_s"].reshape(A_HEADS, CHUNK, CHUNK)
    b_conv_w = small["b_conv_w"].reshape(B_CONV, D_B)
    sg = {}
    wt, cm = plan.weight, plan.comm

    h_m0 = _rmsnorm_fwd(x, nm, layer=0, tm=tm_n, name="norm_mix0")
    z_ab = _mm_nn(h_m0, wt("ab_w_in", 0), layer=0, tm=tm, tn=512, out_dtype=BF16, name="ab_in", comm=cm("ab_in"))
    yab, cb = _mixer_ab_fwd(z_ab, small["a_ln_g"], small["a_ln_b"], w_s, b_s, b_conv_w, small["b_conv_b"],
                            small["b_ln_g"], small["b_ln_b"], tm=tm_e, name="mixer_ab", comm=cm("mixer_ab"))
    x1 = _mm_nn(yab, wt("ab_w_out", 0), layer=0, tm=tm, tn=512, residual=x, name="ab_out", comm=cm("ab_out"))

    def ffn_fwd(xin, layer):
        h = _rmsnorm_fwd(xin, nf, layer=layer, tm=tm_n, name=f"norm_ffn{layer}")
        up = _mm_nn(h, wt("f_w_up", layer), layer=0, tm=tm, tn=1408, out_dtype=BF16, name=f"ffn_up{layer}",
                    comm=cm(f"ffn_up{layer}"))
        a, upc = _ffn_act_fwd(up, small["f_conv_w"], layer=layer, tm=tm_e, name=f"ffn_act{layer}",
                              comm=cm(f"ffn_act{layer}"))
        xout = _mm_nn(a, wt("f_w_down", layer), layer=0, tm=tm, tn=512, residual=xin, name=f"ffn_down{layer}",
                      comm=cm(f"ffn_down{layer}"))
        return h, up, upc, a, xout

    h_f0, up0, upc0, a0, x2 = ffn_fwd(x1, 0)
    h_m1 = _rmsnorm_fwd(x2, nm, layer=1, tm=tm_n, name="norm_mix1")
    z_c = _mm_nn(h_m1, wt("c_w_in", 0), layer=0, tm=tm, tn=768, out_dtype=BF16, name="c_in", comm=cm("c_in"))
    r = _mixer_c_fwd(z_c, small["c_conv_w"], tm=tm_e, name="mixer_c", comm=cm("mixer_c"))
    x3 = _mm_nn(r, wt("c_w_out", 0), layer=0, tm=tm, tn=512, residual=x2, name="c_out", comm=cm("c_out"))
    h_f1, up1, upc1, a1, x4 = ffn_fwd(x3, 1)
    loss, dx, sg["norm_final"] = _loss_head(x4, tgt, ngf, tm=tm_n, name="loss_head")

    def ffn_bwd(dx, xin, h, up, upc, a, layer):
        da = _mm_nt(dx, wt("f_w_down", layer), layer=0, tm=tm, tn=1408, out_dtype=BF16,
                    name=f"ffn_down_dx{layer}", comm=cm(f"ffn_down_dx{layer}"))
        plan.grad_ready("f_w_down", layer, _mm_tn(a, dx, shards=None, tk=1408, tn=1024, tt=tt,
                                                  name=f"ffn_down_dw{layer}", comm=cm(f"ffn_down_dw{layer}")))
        dup, dcw = _ffn_act_bwd(up, upc, da, small["f_conv_w"], layer=layer, tm=tm_e, name=f"ffn_act_bwd{layer}",
                                comm=cm(f"ffn_act_bwd{layer}"))
        dh = _mm_nt(dup, wt("f_w_up", layer), layer=0, tm=tm_r, tn=None, name=f"ffn_up_dx{layer}",
                    comm=cm(f"ffn_up_dx{layer}"))
        plan.grad_ready("f_w_up", layer, _mm_tn(h, dup, shards=N_CHIPS, tk=1024, tn=1408, tt=tt,
                                                name=f"ffn_up_dw{layer}", comm=cm(f"ffn_up_dw{layer}")))
        dxin, dg = _rmsnorm_bwd(xin, nf, dh, dx, layer=layer, tm=tm_n, name=f"norm_ffn_bwd{layer}",
                                comm=cm(f"norm_ffn_bwd{layer}"))
        return dxin, dg, dcw

    dx, dnf1, dfc1 = ffn_bwd(dx, x3, h_f1, up1, upc1, a1, 1)
    dr = _mm_nt(dx, wt("c_w_out", 0), layer=0, tm=tm, tn=512, out_dtype=BF16, name="c_out_dx", comm=cm("c_out_dx"))
    plan.grad_ready("c_w_out", 0, _mm_tn(r, dx, shards=None, tk=1024, tn=1024, tt=tt, name="c_out_dw",
                                         comm=cm("c_out_dw")))
    dz_c, dccw = _mixer_c_bwd(z_c, dr, small["c_conv_w"], tm=tm_e, name="mixer_c_bwd", comm=cm("mixer_c_bwd"))
    sg["c_conv_w"] = dccw.reshape(1, C_CONV, D_MODEL)
    dh = _mm_nt(dz_c, wt("c_w_in", 0), layer=0, tm=tm_r, tn=None, name="c_in_dx", comm=cm("c_in_dx"))
    plan.grad_ready("c_w_in", 0, _mm_tn(h_m1, dz_c, shards=N_CHIPS, tk=1024, tn=768, tt=tt, name="c_in_dw",
                                        comm=cm("c_in_dw")))
    dx, dnm1 = _rmsnorm_bwd(x2, nm, dh, dx, layer=1, tm=tm_n, name="norm_mix_bwd1", comm=cm("norm_mix_bwd1"))
    dx, dnf0, dfc0 = ffn_bwd(dx, x1, h_f0, up0, upc0, a0, 0)
    dyab = _mm_nt(dx, wt("ab_w_out", 0), layer=0, tm=tm, tn=512, out_dtype=BF16, name="ab_out_dx",
                  comm=cm("ab_out_dx"))
    plan.grad_ready("ab_w_out", 0, _mm_tn(yab, dx, shards=None, tk=1024, tn=1024, tt=tt, name="ab_out_dw",
                                          comm=cm("ab_out_dw")))
    (dza, dcb, sg["a_ln_g"], sg["a_ln_b"], dws, dbs, sg["b_ln_g"], sg["b_ln_b"]) = _mixer_ab_bwd_pre(
        z_ab, cb, dyab, small["a_ln_g"], small["a_ln_b"], w_s, b_s, small["b_ln_g"], small["b_ln_b"],
        tm=tm_e, name="mixer_ab_bwd", comm=cm("mixer_ab_bwd"))
    dzb, dbcw, sg["b_conv_b"] = _mixer_b_conv_bwd(z_ab, dcb, b_conv_w, tm=tm_e, name="mixer_b_conv_bwd",
                                                  comm=cm("mixer_b_conv_bwd"))
    sg["a_w_s"] = dws.reshape(1, A_HEADS, CHUNK, CHUNK)
    sg["a_b_s"] = dbs.reshape(1, A_HEADS, CHUNK)
    sg["b_conv_w"] = dbcw.reshape(1, B_CONV, D_B)
    dz_ab = jnp.concatenate([dza, dzb], axis=1)
    plan.grad_ready("ab_w_in", 0, _mm_tn(h_m0, dz_ab, shards=N_CHIPS, tk=1024, tn=512, tt=tt, name="ab_in_dw",
                                         comm=cm("ab_in_dw")))
    dh = _mm_nt(dz_ab, wt("ab_w_in", 0), layer=0, tm=tm_r, tn=None, name="ab_in_dx", comm=cm("ab_in_dx"))
    dx, dnm0 = _rmsnorm_bwd(x, nm, dh, dx, layer=0, tm=tm_n, name="norm_mix_bwd0", comm=cm("norm_mix_bwd0"))

    sg["norm_mix"] = [dnm0, dnm1]
    sg["norm_ffn"] = [dnf0, dnf1]
    sg["f_conv_w"] = [dfc0, dfc1]
    return loss, dx, sg


BLOCK_BYTES = 3 * 1024 * 1024


BF16_SUBLANES = 16


def _row_tile(rows, row_bytes, step=SUBLANES):
    best = None
    for tr in range(step, rows + 1, step):
        if rows % tr == 0 and tr * row_bytes <= BLOCK_BYTES:
            best = tr
    if best is None:
        raise ValueError(f"no row tile for {rows}")
    return best


def _place_scalars():
    x, y, c = lax.axis_index("x"), lax.axis_index("y"), lax.axis_index("c")
    return jnp.stack([c, 2 * x + y, 2 * (1 - x) + y, 2 * x + (1 - y), 2 * (1 - x) + (1 - y)]).astype(jnp.int32)


def _cast_into_slot(w, place, *, layer, name):
    L, rows, cols = w.shape
    tr = _row_tile(rows, cols * 4, BF16_SUBLANES)

    def body(place_ref, w_ref, o_ref):
        o_ref[...] = w_ref[...].astype(BF16)

    return pl.pallas_call(
        body, name=name,
        grid_spec=pltpu.PrefetchScalarGridSpec(
            num_scalar_prefetch=1, grid=(rows // tr,),
            in_specs=[pl.BlockSpec((None, tr, cols), lambda i, p: (layer, i, 0))],
            out_specs=pl.BlockSpec((None, None, tr, cols), lambda i, p: (0, p[1], i, 0))),
        out_shape=jax.ShapeDtypeStruct((1, N_CHIPS, rows, cols), BF16),
        compiler_params=_cparams(("parallel",), 32),
    )(place, w)


def _pair_sum(g, theirs, place, *, name):
    S, rows, cols = g.shape
    half = rows // 2
    tr = _row_tile(half, cols * 4, BF16_SUBLANES)
    nb = half // tr

    def body(place_ref, g_ref, t_ref, o_ref):
        o_ref[...] = (g_ref[...] + t_ref[...]).astype(BF16)

    spec = pl.BlockSpec((None, tr, cols), lambda s, i, p: (s, i, 0))
    return pl.pallas_call(
        body, name=name,
        grid_spec=pltpu.PrefetchScalarGridSpec(
            num_scalar_prefetch=1, grid=(S, nb),
            in_specs=[pl.BlockSpec((None, tr, cols), lambda s, i, p: (s, p[0] * nb + i, 0)), spec],
            out_specs=spec),
        out_shape=jax.ShapeDtypeStruct((S, half, cols), BF16),
        compiler_params=_cparams(("parallel", "parallel"), 32),
    )(place, g, theirs)


def _chip_sum(p, r, g_prev, place, *, layer, shape, name):
    L, rows, cols = shape
    half = rows // 2
    tr = _row_tile(half, cols * 4, BF16_SUBLANES)
    nb = half // tr

    def body(place_ref, p_ref, r_ref, *rest):
        o_ref = rest[-1]
        mine = p_ref[...].astype(F32)
        peers = [r_ref[j].astype(F32) for j in range(3)]
        acc = None
        for s in range(N_CHIPS):
            term = jnp.where(place_ref[1] == s, mine,
                             jnp.where(place_ref[2] == s, peers[0],
                                       jnp.where(place_ref[3] == s, peers[1], peers[2])))
            acc = term if acc is None else acc + term
        o_ref[...] = acc

    in_specs = [pl.BlockSpec((None, tr, cols), lambda i, pr: (pr[1], i, 0)),
                pl.BlockSpec((3, tr, cols), lambda i, pr: (0, i, 0))]
    args = [place, p, r]
    aliases = {}
    if g_prev is not None:
        in_specs.append(ANY)
        args.append(g_prev)
        aliases = {3: 0}
    return pl.pallas_call(
        body, name=name,
        grid_spec=pltpu.PrefetchScalarGridSpec(
            num_scalar_prefetch=1, grid=(nb,), in_specs=in_specs,
            out_specs=pl.BlockSpec((None, tr, cols), lambda i, pr: (layer, pr[0] * nb + i, 0))),
        out_shape=jax.ShapeDtypeStruct(shape, F32), input_output_aliases=aliases,
        compiler_params=_cparams(("parallel",), 32),
    )(*args)


def _adamw_math(w, g, m, v):
    m2 = ADAM_B1 * m + (1.0 - ADAM_B1) * g
    v2 = ADAM_B2 * v + (1.0 - ADAM_B2) * (g * g)
    m_hat = m2 / (1.0 - ADAM_B1 ** ADAM_STEP)
    v_hat = v2 / (1.0 - ADAM_B2 ** ADAM_STEP)
    delta = -ADAM_LR * (m_hat / (jnp.sqrt(v_hat) + ADAM_EPS) + ADAM_WD * w)
    return delta, m2, v2


def _adamw(w, g, m, v, *, name):
    L, rows, cols = w.shape
    tr = _row_tile(rows, cols * 4)

    def body(w_ref, g_ref, m_ref, v_ref, d_ref, m2_ref, v2_ref):
        d, m2, v2 = _adamw_math(w_ref[...], g_ref[...], m_ref[...], v_ref[...])
        d_ref[...] = d
        m2_ref[...] = m2
        v2_ref[...] = v2

    spec = pl.BlockSpec((None, tr, cols), lambda l, i: (l, i, 0))
    shape = jax.ShapeDtypeStruct(w.shape, F32)
    return pl.pallas_call(
        body, name=name, grid=(L, rows // tr), in_specs=[spec] * 4, out_specs=[spec] * 3,
        out_shape=[shape] * 3,
        compiler_params=_cparams(("parallel", "parallel"), 48),
    )(w, g, m, v)


def _exchange_packs(pack, *, reduce, name):
    R = pack.shape[0]
    ndev = 2 * N_CHIPS

    def body(p_ref, o_ref, *scratch):
        if reduce:
            buf, send, recv = scratch
        else:
            buf = o_ref
            send, recv = scratch
        x, y, c = lax.axis_index("x"), lax.axis_index("y"), lax.axis_index("c")
        me = 4 * x + 2 * y + c
        buf[me] = p_ref[...]
        sends = []
        for q in range(1, ndev):
            qx, qy, qc = (q >> 2) & 1, (q >> 1) & 1, q & 1
            peer = (x ^ qx, y ^ qy, c ^ qc)
            rc = _remote(p_ref, buf.at[me], send.at[q - 1], recv.at[q - 1], peer)
            rc.start()
            sends.append(rc)
        for q in range(1, ndev):
            qx, qy, qc = (q >> 2) & 1, (q >> 1) & 1, q & 1
            slot = buf.at[4 * (x ^ qx) + 2 * (y ^ qy) + (c ^ qc)]
            _remote(slot, slot, send.at[q - 1], recv.at[q - 1], (x ^ qx, y ^ qy, c ^ qc)).wait_recv()
        for rc in sends:
            rc.wait_send()
        if reduce:
            acc = buf[0]
            for d in range(1, ndev):
                acc = acc + buf[d]
            o_ref[...] = acc

    vm = pl.BlockSpec(memory_space=pltpu.VMEM)
    sems = [pltpu.SemaphoreType.DMA((ndev - 1,)), pltpu.SemaphoreType.DMA((ndev - 1,))]
    if reduce:
        out_shape = jax.ShapeDtypeStruct((R, LANES), F32)
        scratch = [pltpu.VMEM((ndev, R, LANES), F32)] + sems
    else:
        out_shape = jax.ShapeDtypeStruct((ndev, R, LANES), F32)
        scratch = sems
    return pl.pallas_call(
        body, name=name, in_specs=[vm], out_specs=vm, out_shape=out_shape, scratch_shapes=scratch,
        compiler_params=pltpu.CompilerParams(vmem_limit_bytes=VMEM_BYTES_MAX),
    )(pack)


PACK_UNIT = SUBLANES * LANES


def _pack(arrays):
    flat, sizes = [], []
    for a in arrays:
        pieces = a if isinstance(a, (list, tuple)) else [a]
        v = jnp.concatenate([p.reshape(-1) for p in pieces]) if len(pieces) > 1 else pieces[0].reshape(-1)
        size = v.shape[0]
        padded = -(-size // PACK_UNIT) * PACK_UNIT
        flat.append(jnp.pad(v, (0, padded - size)))
        sizes.append((size, padded))
    return jnp.concatenate(flat).reshape(-1, LANES), sizes


def _unpack(pack, sizes, shapes):
    v = pack.reshape(-1)
    out, off = [], 0
    for (size, padded), shape in zip(sizes, shapes):
        out.append(v[off:off + size].reshape(shape))
        off += padded
    return out


BIG = ("ab_w_in", "ab_w_out", "c_w_in", "c_w_out", "f_w_up", "f_w_down")
COL_SHARDED = ("ab_w_in", "c_w_in", "f_w_up")
SMALL_REPLICATED = ("norm_mix", "norm_ffn", "norm_final", "a_ln_g", "a_ln_b", "a_w_s", "a_b_s",
                    "b_conv_b", "b_ln_g", "b_ln_b")
SMALL_SHARDED = ("b_conv_w", "c_conv_w", "f_conv_w")
SMALL = SMALL_REPLICATED + SMALL_SHARDED
ALL_WEIGHTS = ("norm_mix", "norm_ffn", "norm_final", "ab_w_in", "a_ln_g", "a_ln_b", "a_w_s", "a_b_s",
               "b_conv_w", "b_conv_b", "b_ln_g", "b_ln_b", "ab_w_out", "c_w_in", "c_conv_w", "c_w_out",
               "f_w_up", "f_conv_w", "f_w_down")


SCHEDULE = {
    "ab_in": [("gi", "f_w_up", 0, 0, 2), ("gi", "ab_w_out", 0)],
    "mixer_ab": [("gd", "f_w_up", 0, 0, 2), ("gd", "ab_w_out", 0), ("gi", "f_w_up", 0, 1, 2)],
    "ab_out": [("gd", "f_w_up", 0, 1, 2)],
    "ffn_up0": [("gi", "f_w_down", 0), ("gi", "c_w_in", 0, 0, 2)],
    "ffn_act0": [("gd", "f_w_down", 0), ("gd", "c_w_in", 0, 0, 2), ("gi", "c_w_in", 0, 1, 2),
                 ("gi", "f_w_up", 1, 0, 2)],
    "ffn_down0": [("gd", "c_w_in", 0, 1, 2), ("gd", "f_w_up", 1, 0, 2), ("gi", "f_w_up", 1, 1, 2)],
    "c_in": [("gd", "f_w_up", 1, 1, 2), ("gi", "c_w_out", 0), ("gi", "f_w_down", 1, 0, 2)],
    "mixer_c": [("gd", "c_w_out", 0), ("gd", "f_w_down", 1, 0, 2), ("gi", "f_w_down", 1, 1, 2)],
    "c_out": [("gd", "f_w_down", 1, 1, 2)],
    "ffn_act_bwd1": [("px", "f_w_down", 1)],
    "ffn_up_dx1": [("cx", "f_w_down", 1)],
    "norm_ffn_bwd1": [("px", "f_w_up", 1)],
    "c_out_dx": [("cx", "f_w_up", 1, 0, 4)],
    "c_out_dw": [("cx", "f_w_up", 1, 1, 4)],
    "mixer_c_bwd": [("cx", "f_w_up", 1, 2, 4), ("px", "c_w_out", 0)],
    "c_in_dx": [("cx", "f_w_up", 1, 3, 4), ("cx", "c_w_out", 0)],
    "norm_mix_bwd1": [("px", "c_w_in", 0)],
    "ffn_down_dx0": [("cx", "c_w_in", 0)],
    "ffn_act_bwd0": [("px", "f_w_down", 0)],
    "ffn_up_dx0": [("cx", "f_w_down", 0)],
    "norm_ffn_bwd0": [("px", "f_w_up", 0)],
    "ab_out_dx": [("cx", "f_w_up", 0, 0, 4)],
    "ab_out_dw": [("cx", "f_w_up", 0, 1, 4)],
    "mixer_ab_bwd": [("cx", "f_w_up", 0, 2, 4), ("px", "ab_w_out", 0)],
    "mixer_b_conv_bwd": [("cx", "f_w_up", 0, 3, 4), ("cx", "ab_w_out", 0)],
    "ab_in_dx": [("px", "ab_w_in", 0)],
    "norm_mix_bwd0": [("cx", "ab_w_in", 0)],
}


class _Plan:
    def __init__(self, shapes, place):
        self.shapes, self.place, self.bufs = shapes, place, {}

    def weight(self, name, layer):
        g = self.bufs[f"w:{name}:{layer}"]
        if name in COL_SHARDED:
            return g
        _, S, rows, cols = g.shape
        return g.reshape(1, S * rows, cols)

    def grad_ready(self, name, layer, g):
        _, rows, cols = self.shapes[name]
        hbm = lambda a: pltpu.with_memory_space_constraint(a, pltpu.HBM)
        self.bufs[f"g:{name}:{layer}"] = g.reshape(N_CHIPS, rows, cols)
        self.bufs[f"t:{name}:{layer}"] = hbm(lax.empty((N_CHIPS, rows // 2, cols), F32))
        self.bufs[f"l:{name}:{layer}"] = hbm(lax.empty((3, rows // 2, cols), BF16))

    def job(self, kind, name, layer, part=0, parts=1):
        _, rows, cols = self.shapes[name]
        key = f"{name}:{layer}"
        if kind == "gi":
            return _job_gather_ici("w:" + key, rows, part, parts)
        if kind == "gd":
            return _job_gather_d2d("w:" + key, rows, part, parts)
        if kind == "px":
            return _job_pair_exchange("g:" + key, "t:" + key, rows)
        if kind == "cx":
            if "p:" + key not in self.bufs:
                self.bufs["p:" + key] = _pair_sum(self.bufs["g:" + key], self.bufs["t:" + key], self.place,
                                                  name=f"pair_sum_{name}{layer}")
            nr = rows // 2 // parts
            return _job_chip_exchange("p:" + key, "l:" + key, part * nr, nr)
        if kind == "ps":
            return _job_pair_share("G:" + name, layer, rows)
        raise ValueError(kind)

    def comm(self, call):
        specs = SCHEDULE.get(call)
        return None if specs is None else _Comm(self, [self.job(*spec) for spec in specs])


def _step(x, tgt, w, m, v):
    chip = 2 * lax.axis_index("x") + lax.axis_index("y")
    place = _place_scalars()
    plan = _Plan({n: w[n].shape for n in BIG}, place)
    items = [(n, l) for n in BIG for l in range(w[n].shape[0])]

    for n, l in items:
        plan.bufs[f"w:{n}:{l}"] = _cast_into_slot(w[n], place, layer=l, name=f"cast_{n}{l}")
    _comm_only(plan, [[plan.job("gi", "ab_w_in", 0)], [plan.job("gd", "ab_w_in", 0)]], name="gather_first")
    conv_pack, conv_sizes = _pack([w[n] for n in SMALL_SHARDED])
    conv_all = _exchange_packs(conv_pack, reduce=False, name="gather_conv_weights")
    conv_shapes = [w[n].shape for n in SMALL_SHARDED]
    per_chip = [_unpack(conv_all[2 * s], conv_sizes, conv_shapes) for s in range(N_CHIPS)]
    small = {n: w[n] for n in SMALL_REPLICATED}
    for idx, n in enumerate(SMALL_SHARDED):
        small[n] = jnp.concatenate([per_chip[s][idx] for s in range(N_CHIPS)], axis=-1)

    loss, dx, sg = _local_step(x, tgt, small, plan)

    for n, l in items:
        plan.bufs["G:" + n] = _chip_sum(plan.bufs[f"p:{n}:{l}"], plan.bufs[f"l:{n}:{l}"], plan.bufs.get("G:" + n),
                                        place, layer=l, shape=w[n].shape, name=f"chip_sum_{n}{l}")
    _comm_only(plan, [[plan.job("ps", n, l) for n, l in items]], name="reduce_pair_share")
    grads_big = [plan.bufs["G:" + n] for n in BIG]

    g_pack, g_sizes = _pack([sg[n] for n in SMALL])
    g_sum = _exchange_packs(g_pack, reduce=True, name="allreduce_small_grads")
    full_shapes = [small[n].shape for n in SMALL]
    g_small = dict(zip(SMALL, _unpack(g_sum, g_sizes, full_shapes)))
    for n in SMALL_SHARDED:
        width = w[n].shape[-1]
        g_small[n] = lax.dynamic_slice_in_dim(g_small[n], chip * width, width, axis=g_small[n].ndim - 1)

    grad, delta, new_m, new_v = {}, {}, {}, {}
    for n, g in zip(BIG, grads_big):
        grad[n] = g
        delta[n], new_m[n], new_v[n] = _adamw(w[n], g, m[n], v[n], name=f"adamw_{n}")
    shapes = [w[n].shape for n in SMALL]
    wp, sizes = _pack([w[n] for n in SMALL])
    gp, _ = _pack([g_small[n] for n in SMALL])
    mp, _ = _pack([m[n] for n in SMALL])
    vp, _ = _pack([v[n] for n in SMALL])
    R = wp.shape[0]
    dp, m2p, v2p = _adamw(wp.reshape(1, R, LANES), gp.reshape(1, R, LANES), mp.reshape(1, R, LANES),
                          vp.reshape(1, R, LANES), name="adamw_small")
    for n, d_, m_, v_ in zip(SMALL, _unpack(dp, sizes, shapes), _unpack(m2p, sizes, shapes),
                             _unpack(v2p, sizes, shapes)):
        grad[n] = g_small[n]
        delta[n], new_m[n], new_v[n] = d_, m_, v_
    return loss, dx, grad, delta, new_m, new_v


def kernel(x, norm_mix, norm_ffn, norm_final, ab_w_in, a_ln_g, a_ln_b, a_w_s, a_b_s, b_conv_w, b_conv_b, b_ln_g, b_ln_b, ab_w_out, c_w_in, c_conv_w, c_w_out, f_w_up, f_conv_w, f_w_down, loss_target, m_norm_mix, m_norm_ffn, m_norm_final, m_ab_w_in, m_a_ln_g, m_a_ln_b, m_a_w_s, m_a_b_s, m_b_conv_w, m_b_conv_b, m_b_ln_g, m_b_ln_b, m_ab_w_out, m_c_w_in, m_c_conv_w, m_c_w_out, m_f_w_up, m_f_conv_w, m_f_w_down, v_norm_mix, v_norm_ffn, v_norm_final, v_ab_w_in, v_a_ln_g, v_a_ln_b, v_a_w_s, v_a_b_s, v_b_conv_w, v_b_conv_b, v_b_ln_g, v_b_ln_b, v_ab_w_out, v_c_w_in, v_c_conv_w, v_c_w_out, v_f_w_up, v_f_conv_w, v_f_w_down):
    given = dict(locals())
    w = {n: given[n] for n in ALL_WEIGHTS}
    m = {n: given["m_" + n] for n in ALL_WEIGHTS}
    v = {n: given["v_" + n] for n in ALL_WEIGHTS}
    T = x.shape[1]
    loss, dx, grad, delta, new_m, new_v = _step(x.reshape(T, D_MODEL), loss_target.reshape(T, D_MODEL), w, m, v)
    loss = lax.psum(loss[0, 0], ("x", "y", "c"))
    out = [loss, dx.reshape(x.shape)]
    for d in (grad, delta, new_m, new_v):
        out += [d[n] for n in ALL_WEIGHTS]
    return tuple(out)
```

```python
import functools
import math

import jax
import jax.numpy as jnp
from jax import lax
from jax.experimental import pallas as pl
from jax.experimental.pallas import tpu as pltpu

F32 = jnp.float32
BF16 = jnp.bfloat16

EPS = 1e-6
D_MODEL = 1024
CHUNK = 128
HEAD_DIM = 128
A_HEADS = 4
D_A = 512
D_B = 512
B_CONV = 31
C_CONV = 3
D_FF = 2816
F_CONV = 3
N_CHIPS = 4

ADAM_LR = 0.001
ADAM_B1 = 0.9
ADAM_B2 = 0.999
ADAM_EPS = 1e-08
ADAM_WD = 0.01
ADAM_STEP = 10

SUBLANES = 8
LANES = 128
HALO_SHORT = 16
HALO_LONG = 32
VMEM_BYTES_MAX = 60000 * 1024

INV_SQRT2 = 1.0 / math.sqrt(2.0)
INV_SQRT_2PI = 1.0 / math.sqrt(2.0 * math.pi)

MESH = pl.DeviceIdType.MESH


def _cparams(sem, vmem_mb):
    del vmem_mb
    return pltpu.CompilerParams(dimension_semantics=sem, vmem_limit_bytes=VMEM_BYTES_MAX)


def _pick(total, pref):
    for c in (2048, 1024, 512, 256, 128):
        if c <= pref and total % c == 0:
            return c
    raise ValueError(f"no tile for {total}")


def _sigmoid(x):
    return jax.nn.sigmoid(x)


def _silu(x):
    return x * _sigmoid(x)


def _dsilu(x):
    s = _sigmoid(x)
    return s * (1.0 + x * (1.0 - s))


def _gelu(x):
    return 0.5 * x * (1.0 + lax.erf(x * INV_SQRT2))


def _dgelu(x):
    return 0.5 * (1.0 + lax.erf(x * INV_SQRT2)) + x * jnp.exp(-0.5 * x * x) * INV_SQRT_2PI


def _ln_stats(x):
    mu = jnp.mean(x, axis=-1, keepdims=True)
    xc = x - mu
    var = jnp.mean(xc * xc, axis=-1, keepdims=True)
    r = lax.rsqrt(var + EPS)
    return xc * r, r


def _ln_bwd(dy, xh, r, g):
    dxh = dy * g
    m1 = jnp.mean(dxh, axis=-1, keepdims=True)
    m2 = jnp.mean(dxh * xh, axis=-1, keepdims=True)
    return r * (dxh - m1 - xh * m2)


def _rowsum(x):
    return jnp.sum(x, axis=0, keepdims=True)


ANY = pl.BlockSpec(memory_space=pltpu.HBM)


def _place():
    x, y, c = lax.axis_index("x"), lax.axis_index("y"), lax.axis_index("c")
    peers = [(1 - x, y), (x, 1 - y), (1 - x, 1 - y)]
    return x, y, c, 2 * x + y, (x, y, 1 - c), peers


def _half(rows, which):
    return pl.ds(which * (rows // 2), rows // 2)


def _remote(src, dst, send_sem, recv_sem, device):
    return pltpu.make_async_remote_copy(src_ref=src, dst_ref=dst, send_sem=send_sem, recv_sem=recv_sem,
                                        device_id=device, device_id_type=MESH)


class _Job:
    def __init__(self, reads, writes, ncopies, copies):
        self.reads, self.writes, self.ncopies, self.copies = reads, writes, ncopies, copies


def _share(rows, which, part, parts):
    nr = rows // 2 // parts
    return pl.ds(which * (rows // 2) + part * nr, nr)


def _job_gather_ici(name, rows, part, parts):
    def copies(src, dst, sem):
        x, y, c, k, sib, peers = _place()
        mine_rows = _share(rows, c, part, parts)
        out = []
        for j, (px, py) in enumerate(peers):
            mine = src[name].at[0, k, mine_rows]
            out.append((_remote(mine, dst[name].at[0, k, mine_rows], sem(j, 0), sem(j, 1), (px, py, c)),
                        _remote(mine, dst[name].at[0, 2 * px + py, mine_rows], sem(j, 0), sem(j, 1), (px, py, c))))
        return out
    return _Job([], [name], 3, copies)


def _job_gather_d2d(name, rows, part, parts):
    def copies(src, dst, sem):
        x, y, c, k, sib, peers = _place()
        out = []
        for j, (px, py) in enumerate(peers):
            landed = src[name].at[0, 2 * px + py, _share(rows, c, part, parts)]
            out.append((_remote(landed, dst[name].at[0, 2 * px + py, _share(rows, c, part, parts)],
                                sem(j, 0), sem(j, 1), sib),
                        _remote(landed, dst[name].at[0, 2 * px + py, _share(rows, 1 - c, part, parts)],
                                sem(j, 0), sem(j, 1), sib)))
        return out
    return _Job([], [name], 3, copies)


def _job_pair_exchange(gname, tname, rows):
    def copies(src, dst, sem):
        x, y, c, k, sib, peers = _place()
        cp = _remote(src[gname].at[:, _half(rows, 1 - c), :], dst[tname], sem(0, 0), sem(0, 1), sib)
        return [(cp, cp)]
    return _Job([gname], [tname], 1, copies)


def _job_chip_exchange(pname, lname, r0, nr):
    def copies(src, dst, sem):
        x, y, c, k, sib, peers = _place()
        out = []
        for j, (px, py) in enumerate(peers):
            cp = _remote(src[pname].at[2 * px + py, pl.ds(r0, nr)], dst[lname].at[j, pl.ds(r0, nr)],
                         sem(j, 0), sem(j, 1), (px, py, c))
            out.append((cp, cp))
        return out
    return _Job([pname], [lname], 3, copies)


def _job_pair_share(name, layer, rows):
    def copies(src, dst, sem):
        x, y, c, k, sib, peers = _place()
        mine = src[name].at[layer, _half(rows, c)]
        return [(_remote(mine, dst[name].at[layer, _half(rows, c)], sem(0, 0), sem(0, 1), sib),
                 _remote(mine, dst[name].at[layer, _half(rows, 1 - c)], sem(0, 0), sem(0, 1), sib))]
    return _Job([], [name], 1, copies)


class _Comm:
    def __init__(self, plan, jobs):
        self.plan, self.jobs = plan, jobs
        self.writes, self.reads = [], []
        for job in jobs:
            for n in job.writes:
                if n not in self.writes:
                    self.writes.append(n)
        for job in jobs:
            for n in job.reads:
                if n not in self.writes and n not in self.reads:
                    self.reads.append(n)
        self.ncopies = sum(job.ncopies for job in jobs)

    def descriptors(self, src, dst, sems, base):
        out = []
        for job in self.jobs:
            sem = lambda j, which, base=base: sems.at[base + j, which]
            out += job.copies(src, dst, sem)
            base += job.ncopies
        return out

    def start(self, src, dst, sems, base=0):
        for first, _ in self.descriptors(src, dst, sems, base):
            first.start()

    def finish(self, src, dst, sems, base=0):
        for _, landed in self.descriptors(src, dst, sems, base):
            landed.wait()


def _comm_operands(comm):
    bufs = comm.plan.bufs
    shapes = [jax.ShapeDtypeStruct(bufs[n].shape, bufs[n].dtype) for n in comm.writes]
    return [bufs[n] for n in comm.reads] + [bufs[n] for n in comm.writes], shapes


def _pallas(comm, body, *, name, grid, in_specs, out_specs, out_shape, compiler_params, scratch_shapes=(),
            aliases=None):
    aliases = dict(aliases or {})
    if comm is None:
        return pl.pallas_call(body, name=name, grid=grid, in_specs=in_specs, out_specs=out_specs,
                              out_shape=out_shape, scratch_shapes=list(scratch_shapes),
                              input_output_aliases=aliases, compiler_params=compiler_params)
    single = not isinstance(out_shape, (list, tuple))
    base_specs = [out_specs] if single else list(out_specs)
    base_shape = [out_shape] if single else list(out_shape)
    nb, nr, nw, nbo, nsc = len(in_specs), len(comm.reads), len(comm.writes), len(base_specs), len(scratch_shapes)

    def wrapped(*refs):
        base_in, rd, wr_in = refs[:nb], refs[nb:nb + nr], refs[nb + nr:nb + nr + nw]
        o0 = nb + nr + nw
        base_out, wr_out = refs[o0:o0 + nbo], refs[o0 + nbo:o0 + nbo + nw]
        scratch, sems = refs[o0 + nbo + nw:o0 + nbo + nw + nsc], refs[-1]
        src = dict(zip(comm.reads, rd))
        src.update(zip(comm.writes, wr_in))
        dst = dict(zip(comm.writes, wr_out))
        first = functools.reduce(jnp.logical_and, [pl.program_id(a) == 0 for a in range(len(grid))])
        last = functools.reduce(jnp.logical_and,
                                [pl.program_id(a) == pl.num_programs(a) - 1 for a in range(len(grid))])

        @pl.when(first)
        def _():
            comm.start(src, dst, sems)
        body(*base_in, *base_out, *scratch)

        @pl.when(last)
        def _():
            comm.finish(src, dst, sems)

    operands, shapes = _comm_operands(comm)
    call = pl.pallas_call(
        wrapped, name=name, grid=grid, in_specs=list(in_specs) + [ANY] * (nr + nw),
        out_specs=base_specs + [ANY] * nw, out_shape=base_shape + shapes,
        input_output_aliases={**aliases, **{nb + nr + q: nbo + q for q in range(nw)}},
        scratch_shapes=list(scratch_shapes) + [pltpu.SemaphoreType.DMA((comm.ncopies, 2))],
        compiler_params=compiler_params)

    def run(*args):
        outs = call(*args, *operands)
        for q, n in enumerate(comm.writes):
            comm.plan.bufs[n] = outs[nbo + q]
        return outs[0] if single else list(outs[:nbo])

    return run


def _comm_only(plan, phases, *, name):
    comms = [_Comm(plan, jobs) for jobs in phases]
    both = _Comm(plan, [job for jobs in phases for job in jobs])
    nr, nw = len(both.reads), len(both.writes)

    def body(*refs):
        rd, wr_in, wr_out, sems = refs[:nr], refs[nr:nr + nw], refs[nr + nw:nr + 2 * nw], refs[-1]
        src = dict(zip(both.reads, rd))
        src.update(zip(both.writes, wr_in))
        dst = dict(zip(both.writes, wr_out))
        base = 0
        for comm in comms:
            comm.start(src, dst, sems, base)
            comm.finish(src, dst, sems, base)
            base += comm.ncopies

    operands, shapes = _comm_operands(both)
    outs = pl.pallas_call(
        body, name=name, in_specs=[ANY] * (nr + nw), out_specs=[ANY] * nw, out_shape=shapes,
        input_output_aliases={nr + q: q for q in range(nw)},
        scratch_shapes=[pltpu.SemaphoreType.DMA((both.ncopies, 2))],
    )(*operands)
    for q, n in enumerate(both.writes):
        plan.bufs[n] = outs[q]


def _mm_nn(a, w, *, layer, tm, tn, residual=None, out_dtype=F32, name, comm=None):
    T, K = a.shape
    if w.ndim == 4:
        _, S, _, n4 = w.shape
        N = S * n4
        bps = n4 // tn
        w_spec = pl.BlockSpec((None, None, K, tn), lambda j, i: (layer, j // bps, 0, j % bps))
    else:
        N = w.shape[2]
        w_spec = pl.BlockSpec((None, K, tn), lambda j, i: (layer, 0, j))
    in_specs = [pl.BlockSpec((tm, K), lambda j, i: (i, 0)), w_spec]
    args = [a, w]
    if residual is not None:
        in_specs.append(pl.BlockSpec((tm, tn), lambda j, i: (i, j)))
        args.append(residual)

    def body(*refs):
        a_ref, w_ref, o_ref = refs[0], refs[1], refs[-1]
        acc = jnp.dot(a_ref[...].astype(BF16), w_ref[...], preferred_element_type=F32)
        if residual is not None:
            acc = refs[2][...] + acc
        o_ref[...] = acc.astype(out_dtype)

    return _pallas(
        comm, body, name=name, grid=(N // tn, T // tm), in_specs=in_specs,
        out_specs=pl.BlockSpec((tm, tn), lambda j, i: (i, j)),
        out_shape=jax.ShapeDtypeStruct((T, N), out_dtype),
        compiler_params=_cparams(("parallel", "parallel"), 48),
    )(*args)


def _mm_nt(dy, w, *, layer, tm, tn, name, out_dtype=F32, comm=None):
    T = dy.shape[0]
    nt_dims = (((1,), (1,)), ((), ()))
    if w.ndim == 4:
        _, S, K, n4 = w.shape

        def body(dy_ref, w_ref, o_ref):
            @pl.when(pl.program_id(1) == 0)
            def _():
                o_ref[...] = jnp.zeros_like(o_ref)
            o_ref[...] += lax.dot_general(dy_ref[...].astype(BF16), w_ref[...], nt_dims,
                                          preferred_element_type=F32)

        return _pallas(
            comm, body, name=name, grid=(T // tm, S),
            in_specs=[pl.BlockSpec((tm, n4), lambda i, s: (i, s)),
                      pl.BlockSpec((None, None, K, n4), lambda i, s: (layer, s, 0, 0))],
            out_specs=pl.BlockSpec((tm, K), lambda i, s: (i, 0)),
            out_shape=jax.ShapeDtypeStruct((T, K), F32),
            compiler_params=_cparams(("parallel", "arbitrary"), 48),
        )(dy, w)
    _, R, N = w.shape

    def body2(dy_ref, w_ref, o_ref):
        o_ref[...] = lax.dot_general(dy_ref[...].astype(BF16), w_ref[...], nt_dims,
                                     preferred_element_type=F32).astype(out_dtype)

    return _pallas(
        comm, body2, name=name, grid=(R // tn, T // tm),
        in_specs=[pl.BlockSpec((tm, N), lambda j, i: (i, 0)),
                  pl.BlockSpec((None, tn, N), lambda j, i: (layer, j, 0))],
        out_specs=pl.BlockSpec((tm, tn), lambda j, i: (i, j)),
        out_shape=jax.ShapeDtypeStruct((T, R), out_dtype),
        compiler_params=_cparams(("parallel", "parallel"), 48),
    )(dy, w)


def _mm_tn(a, dy, *, shards, tk, tn, tt, name, comm=None):
    T, K = a.shape
    N = dy.shape[1]
    tn_dims = (((0,), (0,)), ((), ()))

    def body(a_ref, dy_ref, o_ref):
        @pl.when(pl.program_id(2) == 0)
        def _():
            o_ref[...] = jnp.zeros_like(o_ref)
        o_ref[...] += lax.dot_general(a_ref[...].astype(BF16), dy_ref[...].astype(BF16), tn_dims,
                                      preferred_element_type=F32)

    if shards is None:
        out_spec = pl.BlockSpec((tk, tn), lambda k, n, t: (k, n))
        out_shape = jax.ShapeDtypeStruct((K, N), F32)
    else:
        n4 = N // shards
        bps = n4 // tn
        out_spec = pl.BlockSpec((None, tk, tn), lambda k, n, t: (n // bps, k, n % bps))
        out_shape = jax.ShapeDtypeStruct((shards, K, n4), F32)
    return _pallas(
        comm, body, name=name, grid=(K // tk, N // tn, T // tt),
        in_specs=[pl.BlockSpec((tt, tk), lambda k, n, t: (t, k)),
                  pl.BlockSpec((tt, tn), lambda k, n, t: (t, n))],
        out_specs=out_spec, out_shape=out_shape,
        compiler_params=_cparams(("parallel", "parallel", "arbitrary"), 48),
    )(a, dy)


def _rmsnorm_fwd(x, g, *, layer, tm, name, comm=None):
    T, D = x.shape

    def body(x_ref, g_ref, h_ref):
        xf = x_ref[...]
        r = lax.rsqrt(jnp.mean(xf * xf, axis=-1, keepdims=True) + EPS)
        h_ref[...] = (xf * r * g_ref[...]).astype(BF16)

    return _pallas(
        comm, body, name=name, grid=(T // tm,),
        in_specs=[pl.BlockSpec((tm, D), lambda i: (i, 0)),
                  pl.BlockSpec((None, 1, D), lambda i: (layer, 0, 0))],
        out_specs=pl.BlockSpec((tm, D), lambda i: (i, 0)),
        out_shape=jax.ShapeDtypeStruct((T, D), BF16),
        compiler_params=_cparams(("parallel",), 32),
    )(x, g)


def _rmsnorm_bwd(x, g, dh, dres, *, layer, tm, name, comm=None):
    T, D = x.shape

    def body(x_ref, g_ref, dh_ref, dres_ref, dx_ref, dg_ref):
        @pl.when(pl.program_id(0) == 0)
        def _():
            dg_ref[...] = jnp.zeros_like(dg_ref)
        xf = x_ref[...]
        r = lax.rsqrt(jnp.mean(xf * xf, axis=-1, keepdims=True) + EPS)
        xh = xf * r
        dh = dh_ref[...]
        dg_ref[...] += _rowsum(dh * xh)
        dxh = dh * g_ref[...]
        dx_ref[...] = dres_ref[...] + r * (dxh - xh * jnp.mean(dxh * xh, axis=-1, keepdims=True))

    return _pallas(
        comm, body, name=name, grid=(T // tm,),
        in_specs=[pl.BlockSpec((tm, D), lambda i: (i, 0)),
                  pl.BlockSpec((None, 1, D), lambda i: (layer, 0, 0)),
                  pl.BlockSpec((tm, D), lambda i: (i, 0)),
                  pl.BlockSpec((tm, D), lambda i: (i, 0))],
        out_specs=[pl.BlockSpec((tm, D), lambda i: (i, 0)),
                   pl.BlockSpec((1, D), lambda i: (0, 0))],
        out_shape=[jax.ShapeDtypeStruct((T, D), F32), jax.ShapeDtypeStruct((1, D), F32)],
        compiler_params=_cparams(("arbitrary",), 40),
    )(x, g, dh, dres)


def _loss_head(x, tgt, g, *, tm, name, comm=None):
    T, D = x.shape

    def body(x_ref, t_ref, g_ref, loss_ref, dx_ref, dg_ref):
        @pl.when(pl.program_id(0) == 0)
        def _():
            dg_ref[...] = jnp.zeros_like(dg_ref)
            loss_ref[...] = jnp.zeros_like(loss_ref)
        xf = x_ref[...]
        gg = g_ref[...]
        r = lax.rsqrt(jnp.mean(xf * xf, axis=-1, keepdims=True) + EPS)
        xh = xf * r
        err = xh * gg - t_ref[...]
        row = jnp.mean(err * err, axis=-1, keepdims=True)
        loss_ref[...] += 0.5 * jnp.sum(row, axis=0, keepdims=True)
        dy = err * (1.0 / D)
        dg_ref[...] += _rowsum(dy * xh)
        dxh = dy * gg
        dx_ref[...] = r * (dxh - xh * jnp.mean(dxh * xh, axis=-1, keepdims=True))

    return _pallas(
        comm, body, name=name, grid=(T // tm,),
        in_specs=[pl.BlockSpec((tm, D), lambda i: (i, 0)),
                  pl.BlockSpec((tm, D), lambda i: (i, 0)),
                  pl.BlockSpec((1, D), lambda i: (0, 0))],
        out_specs=[pl.BlockSpec((1, 1), lambda i: (0, 0)),
                   pl.BlockSpec((tm, D), lambda i: (i, 0)),
                   pl.BlockSpec((1, D), lambda i: (0, 0))],
        out_shape=[jax.ShapeDtypeStruct((1, 1), F32), jax.ShapeDtypeStruct((T, D), F32),
                   jax.ShapeDtypeStruct((1, D), F32)],
        compiler_params=_cparams(("arbitrary",), 40),
    )(x, tgt, g)


CONV_ROWS = 64
CONV_COLS = 256


def _halo_prev_index(tm, halo):
    per = tm // halo
    return lambda i: jnp.maximum(i * per - 1, 0)


def _halo_next_index(tm, halo, total):
    per = tm // halo
    last = total // halo - 1
    return lambda i: jnp.minimum((i + 1) * per, last)


def _causal_mask():
    t = lax.broadcasted_iota(jnp.int32, (CHUNK, CHUNK), 0)
    s = lax.broadcasted_iota(jnp.int32, (CHUNK, CHUNK), 1)
    return s <= t


def _mixer_ab_fwd(z, a_ln_g, a_ln_b, w_s, b_s, conv_w, conv_b, b_ln_g, b_ln_b, *, tm, name, comm=None):
    T = z.shape[0]
    nchunk = tm // CHUNK
    halo = HALO_LONG

    def body(za_ref, zb_ref, zh_ref, alg_ref, alb_ref, ws_ref, bs_ref, cw_ref, cbias_ref,
             blg_ref, blb_ref, y_ref, cb_ref, ext_ref):
        i = pl.program_id(0)
        gu = _gelu(za_ref[:, :D_A].astype(F32))
        gv = _gelu(za_ref[:, D_A:].astype(F32))
        xh, _ = _ln_stats(gv)
        lv = (xh * alg_ref[...] + alb_ref[...]).astype(BF16)
        mask = _causal_mask()
        for h in range(A_HEADS):
            wm = jnp.where(mask, ws_ref[h], 0.0).astype(BF16)
            cols = slice(h * HEAD_DIM, (h + 1) * HEAD_DIM)
            for c in range(nchunk):
                rows = slice(c * CHUNK, (c + 1) * CHUNK)
                mixed = jnp.dot(wm, lv[rows, cols], preferred_element_type=F32) + bs_ref[h]
                y_ref[rows, cols] = (gu[rows, cols] * mixed).astype(BF16)
        ext_ref[halo:halo + tm, :] = zb_ref[:, :D_B].astype(F32) * _sigmoid(zb_ref[:, D_B:].astype(F32))
        prev = zh_ref[:, :D_B].astype(F32) * _sigmoid(zh_ref[:, D_B:].astype(F32))
        ext_ref[0:halo, :] = jnp.where(i > 0, prev, 0.0)
        for rb in range(tm // CONV_ROWS):
            for cb in range(D_B // CONV_COLS):
                cs = slice(cb * CONV_COLS, (cb + 1) * CONV_COLS)
                window = ext_ref[rb * CONV_ROWS:rb * CONV_ROWS + CONV_ROWS + halo, cs]
                acc = jnp.zeros((CONV_ROWS, CONV_COLS), F32)
                for k in range(B_CONV):
                    shifted = _rows_after(window, halo - (B_CONV - 1) + k)[:CONV_ROWS]
                    acc = acc + cw_ref[k:k + 1, cs] * shifted
                cb_ref[rb * CONV_ROWS:(rb + 1) * CONV_ROWS, cs] = acc + cbias_ref[:, cs]
        xhb, _ = _ln_stats(cb_ref[...])
        y_ref[:, D_A:] = _silu(xhb * blg_ref[...] + blb_ref[...]).astype(BF16)

    row = lambda i: (i, 0)
    par = lambda i: (0, 0)
    return _pallas(
        comm, body, name=name, grid=(T // tm,),
        in_specs=[pl.BlockSpec((tm, 2 * D_A), lambda i: (i, 0)),
                  pl.BlockSpec((tm, 2 * D_B), lambda i: (i, 1)),
                  pl.BlockSpec((halo, 2 * D_B), lambda i: (_halo_prev_index(tm, halo)(i), 1)),
                  pl.BlockSpec((1, D_A), par), pl.BlockSpec((1, D_A), par),
                  pl.BlockSpec((A_HEADS, CHUNK, CHUNK), lambda i: (0, 0, 0)),
                  pl.BlockSpec((A_HEADS, CHUNK, 1), lambda i: (0, 0, 0)),
                  pl.BlockSpec((B_CONV, D_B), par), pl.BlockSpec((1, D_B), par),
                  pl.BlockSpec((1, D_B), par), pl.BlockSpec((1, D_B), par)],
        out_specs=[pl.BlockSpec((tm, D_A + D_B), row), pl.BlockSpec((tm, D_B), row)],
        out_shape=[jax.ShapeDtypeStruct((T, D_A + D_B), BF16), jax.ShapeDtypeStruct((T, D_B), F32)],
        scratch_shapes=[pltpu.VMEM((halo + tm, D_B), F32)],
        compiler_params=_cparams(("parallel",), 40),
    )(z, z, z, a_ln_g, a_ln_b, w_s, b_s, conv_w, conv_b, b_ln_g, b_ln_b)


def _mixer_ab_bwd_pre(z, cb, dy, a_ln_g, a_ln_b, w_s, b_s, b_ln_g, b_ln_b, *, tm, name, comm=None):
    T = z.shape[0]
    nchunk = tm // CHUNK
    tn_dims = (((0,), (0,)), ((), ()))
    nt_dims = (((1,), (1,)), ((), ()))

    def body(za_ref, cb_ref, dy_ref, alg_ref, alb_ref, ws_ref, bs_ref, blg_ref, blb_ref,
             dza_ref, dcb_ref, dalg_ref, dalb_ref, dws_ref, dbs_ref, dblg_ref, dblb_ref,
             dlv_ref):
        @pl.when(pl.program_id(0) == 0)
        def _():
            for ref in (dalg_ref, dalb_ref, dws_ref, dbs_ref, dblg_ref, dblb_ref):
                ref[...] = jnp.zeros_like(ref)
        ua = za_ref[:, :D_A].astype(F32)
        va = za_ref[:, D_A:].astype(F32)
        gu = _gelu(ua)
        gv = _gelu(va)
        xh, r = _ln_stats(gv)
        alg = alg_ref[...]
        lv = (xh * alg + alb_ref[...]).astype(BF16)
        dya = dy_ref[:, :D_A].astype(F32)
        mask = _causal_mask()
        for h in range(A_HEADS):
            wm = jnp.where(mask, ws_ref[h], 0.0).astype(BF16)
            cols = slice(h * HEAD_DIM, (h + 1) * HEAD_DIM)
            dwm = jnp.zeros((CHUNK, CHUNK), F32)
            dbs = jnp.zeros((CHUNK, 1), F32)
            for c in range(nchunk):
                rows = slice(c * CHUNK, (c + 1) * CHUNK)
                lvb = lv[rows, cols]
                mixed = jnp.dot(wm, lvb, preferred_element_type=F32) + bs_ref[h]
                dyb = dya[rows, cols]
                dza_ref[rows, cols] = (dyb * mixed * _dgelu(ua[rows, cols])).astype(BF16)
                dmixed = dyb * gu[rows, cols]
                dmb = dmixed.astype(BF16)
                dlv_ref[rows, cols] = lax.dot_general(wm, dmb, tn_dims, preferred_element_type=F32)
                dwm = dwm + lax.dot_general(dmb, lvb, nt_dims, preferred_element_type=F32)
                dbs = dbs + jnp.sum(dmixed, axis=1, keepdims=True)
            dws_ref[h] += jnp.where(mask, dwm, 0.0)
            dbs_ref[h] += dbs
        dlv = dlv_ref[...]
        dalg_ref[...] += _rowsum(dlv * xh)
        dalb_ref[...] += _rowsum(dlv)
        dgv = _ln_bwd(dlv, xh, r, alg)
        dza_ref[:, D_A:] = (dgv * _dgelu(va)).astype(BF16)
        xhb, rb = _ln_stats(cb_ref[...])
        blg = blg_ref[...]
        lb = xhb * blg + blb_ref[...]
        dlb = dy_ref[:, D_A:].astype(F32) * _dsilu(lb)
        dblg_ref[...] += _rowsum(dlb * xhb)
        dblb_ref[...] += _rowsum(dlb)
        dcb_ref[...] = _ln_bwd(dlb, xhb, rb, blg)

    row = lambda i: (i, 0)
    par = lambda i: (0, 0)
    par3 = lambda i: (0, 0, 0)
    return _pallas(
        comm, body, name=name, grid=(T // tm,),
        in_specs=[pl.BlockSpec((tm, 2 * D_A), row), pl.BlockSpec((tm, D_B), row),
                  pl.BlockSpec((tm, D_A + D_B), row),
                  pl.BlockSpec((1, D_A), par), pl.BlockSpec((1, D_A), par),
                  pl.BlockSpec((A_HEADS, CHUNK, CHUNK), par3),
                  pl.BlockSpec((A_HEADS, CHUNK, 1), par3),
                  pl.BlockSpec((1, D_B), par), pl.BlockSpec((1, D_B), par)],
        out_specs=[pl.BlockSpec((tm, 2 * D_A), row), pl.BlockSpec((tm, D_B), row),
                   pl.BlockSpec((1, D_A), par), pl.BlockSpec((1, D_A), par),
                   pl.BlockSpec((A_HEADS, CHUNK, CHUNK), par3),
                   pl.BlockSpec((A_HEADS, CHUNK, 1), par3),
                   pl.BlockSpec((1, D_B), par), pl.BlockSpec((1, D_B), par)],
        out_shape=[jax.ShapeDtypeStruct((T, 2 * D_A + 2 * D_B), BF16), jax.ShapeDtypeStruct((T, D_B), F32),
                   jax.ShapeDtypeStruct((1, D_A), F32), jax.ShapeDtypeStruct((1, D_A), F32),
                   jax.ShapeDtypeStruct((A_HEADS, CHUNK, CHUNK), F32),
                   jax.ShapeDtypeStruct((A_HEADS, CHUNK, 1), F32),
                   jax.ShapeDtypeStruct((1, D_B), F32), jax.ShapeDtypeStruct((1, D_B), F32)],
        scratch_shapes=[pltpu.VMEM((tm, D_A), F32)],
        compiler_params=_cparams(("arbitrary",), 40),
    )(z, cb, dy, a_ln_g, a_ln_b, w_s, b_s, b_ln_g, b_ln_b)


def _mixer_b_conv_bwd(z, dcb, conv_w, dz, *, tm, name, comm=None):
    T = z.shape[0]
    halo = HALO_LONG

    def body(zb_ref, dcb_ref, dcn_ref, cw_ref, dz_in_ref, dzb_ref, dcw_ref, dbias_ref, dext_ref):
        i = pl.program_id(0)
        last = pl.num_programs(0) - 1

        @pl.when(i == 0)
        def _():
            dcw_ref[...] = jnp.zeros_like(dcw_ref)
            dbias_ref[...] = jnp.zeros_like(dbias_ref)
        dcb = dcb_ref[...]
        dext_ref[0:tm, :] = dcb
        dext_ref[tm:tm + halo, :] = jnp.where(i < last, dcn_ref[...], 0.0)
        dbias_ref[...] += _rowsum(dcb)
        for rb in range(tm // CONV_ROWS):
            for cb in range(D_B // CONV_COLS):
                cs = slice(cb * CONV_COLS, (cb + 1) * CONV_COLS)
                gcs = slice(D_B + cb * CONV_COLS, D_B + (cb + 1) * CONV_COLS)
                rs = slice(rb * CONV_ROWS, (rb + 1) * CONV_ROWS)
                xbb = zb_ref[rs, cs].astype(F32)
                sgb = _sigmoid(zb_ref[rs, gcs].astype(F32))
                yb0 = xbb * sgb
                window = dext_ref[rb * CONV_ROWS:rb * CONV_ROWS + CONV_ROWS + halo, cs]
                acc = jnp.zeros((CONV_ROWS, CONV_COLS), F32)
                for k in range(B_CONV):
                    shifted = _rows_after(window, (B_CONV - 1) - k)[:CONV_ROWS]
                    acc = acc + cw_ref[k:k + 1, cs] * shifted
                    dcw_ref[k:k + 1, cs] += _rowsum(shifted * yb0)
                dzb_ref[rs, cs] = (acc * sgb).astype(BF16)
                dzb_ref[rs, gcs] = (acc * xbb * sgb * (1.0 - sgb)).astype(BF16)

    row = lambda i: (i, 0)
    par = lambda i: (0, 0)
    return _pallas(
        comm, body, name=name, grid=(T // tm,),
        in_specs=[pl.BlockSpec((tm, 2 * D_B), lambda i: (i, 1)),
                  pl.BlockSpec((tm, D_B), row),
                  pl.BlockSpec((halo, D_B), lambda i: (_halo_next_index(tm, halo, T)(i), 0)),
                  pl.BlockSpec((B_CONV, D_B), par), pl.BlockSpec(memory_space=pl.ANY)],
        out_specs=[pl.BlockSpec((tm, 2 * D_B), lambda i: (i, 1)), pl.BlockSpec((B_CONV, D_B), par),
                   pl.BlockSpec((1, D_B), par)],
        out_shape=[jax.ShapeDtypeStruct(dz.shape, BF16), jax.ShapeDtypeStruct((B_CONV, D_B), F32),
                   jax.ShapeDtypeStruct((1, D_B), F32)],
        scratch_shapes=[pltpu.VMEM((tm + halo, D_B), F32)], aliases={4: 0},
        compiler_params=_cparams(("arbitrary",), 40),
    )(z, dcb, dcb, conv_w, dz)


def _rows_before(x, a):
    return x if a == 0 else pltpu.roll(x, a, axis=0)


def _rows_after(x, a):
    return x if a == 0 else pltpu.roll(x, x.shape[0] - a, axis=0)


def _conv3(w_ref, x, halo, cs):
    acc = w_ref[2:3, cs] * x[halo:]
    acc = acc + w_ref[1:2, cs] * _rows_before(x, 1)[halo:]
    return acc + w_ref[0:1, cs] * _rows_before(x, 2)[halo:]


def _mixer_c_fwd(z, conv_w, *, tm, name, comm=None):
    T = z.shape[0]
    D = D_MODEL
    halo = HALO_SHORT
    W = CONV_COLS

    def body(bg_ref, cg_ref, xv_ref, cgh_ref, xvh_ref, w_ref, r_ref):
        i = pl.program_id(0)
        for cb in range(D // W):
            cs = slice(cb * W, (cb + 1) * W)
            prev = jnp.where(i > 0, cgh_ref[:, cs].astype(F32) * xvh_ref[:, cs].astype(F32), 0.0)
            p = jnp.concatenate([prev, cg_ref[:, cs].astype(F32) * xv_ref[:, cs].astype(F32)], axis=0)
            r_ref[:, cs] = (bg_ref[:, cs].astype(F32) * _conv3(w_ref, p, halo, cs)).astype(BF16)

    hp = _halo_prev_index(tm, halo)
    return _pallas(
        comm, body, name=name, grid=(T // tm,),
        in_specs=[pl.BlockSpec((tm, D), lambda i: (i, 0)), pl.BlockSpec((tm, D), lambda i: (i, 1)),
                  pl.BlockSpec((tm, D), lambda i: (i, 2)),
                  pl.BlockSpec((halo, D), lambda i: (hp(i), 1)),
                  pl.BlockSpec((halo, D), lambda i: (hp(i), 2)),
                  pl.BlockSpec((None, C_CONV, D), lambda i: (0, 0, 0))],
        out_specs=pl.BlockSpec((tm, D), lambda i: (i, 0)),
        out_shape=jax.ShapeDtypeStruct((T, D), BF16),
        compiler_params=_cparams(("parallel",), 40),
    )(z, z, z, z, z, conv_w)


def _mixer_c_bwd(z, dr, conv_w, *, tm, name, comm=None):
    T = z.shape[0]
    D = D_MODEL
    halo = HALO_SHORT
    W = CONV_COLS

    def body(bg_ref, cg_ref, xv_ref, cgh_ref, xvh_ref, bgn_ref, dr_ref, drn_ref, w_ref, dz_ref, dw_ref):
        i = pl.program_id(0)
        last = pl.num_programs(0) - 1

        @pl.when(i == 0)
        def _():
            dw_ref[...] = jnp.zeros_like(dw_ref)
        for cb in range(D // W):
            cs = slice(cb * W, (cb + 1) * W)
            cg = cg_ref[:, cs].astype(F32)
            xv = xv_ref[:, cs].astype(F32)
            dr = dr_ref[:, cs].astype(F32)
            p = cg * xv
            prev = jnp.where(i > 0, cgh_ref[:, cs].astype(F32) * xvh_ref[:, cs].astype(F32), 0.0)
            q = _conv3(w_ref, jnp.concatenate([prev, p], axis=0), halo, cs)
            dz_ref[:, cs] = (dr * q).astype(BF16)
            nxt = jnp.where(i < last, drn_ref[:, cs].astype(F32) * bgn_ref[:, cs].astype(F32), 0.0)
            dq = jnp.concatenate([dr * bg_ref[:, cs].astype(F32), nxt], axis=0)
            dp = None
            for k in range(C_CONV):
                shifted = _rows_after(dq, 2 - k)[:tm]
                term = w_ref[k:k + 1, cs] * shifted
                dp = term if dp is None else dp + term
                dw_ref[k:k + 1, cs] += _rowsum(shifted * p)
            dz_ref[:, D + cb * W:D + (cb + 1) * W] = (dp * xv).astype(BF16)
            dz_ref[:, 2 * D + cb * W:2 * D + (cb + 1) * W] = (dp * cg).astype(BF16)

    hp = _halo_prev_index(tm, halo)
    hn = _halo_next_index(tm, halo, T)
    return _pallas(
        comm, body, name=name, grid=(T // tm,),
        in_specs=[pl.BlockSpec((tm, D), lambda i: (i, 0)), pl.BlockSpec((tm, D), lambda i: (i, 1)),
                  pl.BlockSpec((tm, D), lambda i: (i, 2)),
                  pl.BlockSpec((halo, D), lambda i: (hp(i), 1)),
                  pl.BlockSpec((halo, D), lambda i: (hp(i), 2)),
                  pl.BlockSpec((halo, D), lambda i: (hn(i), 0)),
                  pl.BlockSpec((tm, D), lambda i: (i, 0)),
                  pl.BlockSpec((halo, D), lambda i: (hn(i), 0)),
                  pl.BlockSpec((None, C_CONV, D), lambda i: (0, 0, 0))],
        out_specs=[pl.BlockSpec((tm, 3 * D), lambda i: (i, 0)),
                   pl.BlockSpec((C_CONV, D), lambda i: (0, 0))],
        out_shape=[jax.ShapeDtypeStruct((T, 3 * D), BF16), jax.ShapeDtypeStruct((C_CONV, D), F32)],
        compiler_params=_cparams(("arbitrary",), 48),
    )(z, z, z, z, z, z, dr, dr, conv_w)


FFN_COLS = 128


def _ffn_act_fwd(up, conv_w, *, layer, tm, name, comm=None):
    T = up.shape[0]
    halo = HALO_SHORT
    W = FFN_COLS

    def body(up_ref, uph_ref, w_ref, a_ref, upc_ref):
        i = pl.program_id(0)
        def conv(cs):
            prev = jnp.where(i > 0, uph_ref[:, cs], jnp.zeros((halo, W), BF16))
            return _conv3(w_ref, jnp.concatenate([prev, up_ref[:, cs]], axis=0).astype(F32), halo, cs)

        for cb in range(D_FF // W):
            gs = slice(cb * W, (cb + 1) * W)
            vs = slice(D_FF + cb * W, D_FF + (cb + 1) * W)
            g = conv(gs)
            v = conv(vs)
            upc_ref[:, gs] = g.astype(BF16)
            upc_ref[:, vs] = v.astype(BF16)
            a_ref[:, gs] = (_silu(g) * v).astype(BF16)

    return _pallas(
        comm, body, name=name, grid=(T // tm,),
        in_specs=[pl.BlockSpec((tm, 2 * D_FF), lambda i: (i, 0)),
                  pl.BlockSpec((halo, 2 * D_FF), lambda i: (_halo_prev_index(tm, halo)(i), 0)),
                  pl.BlockSpec((None, F_CONV, 2 * D_FF), lambda i: (layer, 0, 0))],
        out_specs=[pl.BlockSpec((tm, D_FF), lambda i: (i, 0)),
                   pl.BlockSpec((tm, 2 * D_FF), lambda i: (i, 0))],
        out_shape=[jax.ShapeDtypeStruct((T, D_FF), BF16), jax.ShapeDtypeStruct((T, 2 * D_FF), BF16)],
        compiler_params=_cparams(("parallel",), 48),
    )(up, up, conv_w)


def _ffn_act_bwd(up, upc, da, conv_w, *, layer, tm, name, comm=None):
    T = up.shape[0]
    halo = HALO_SHORT
    W = FFN_COLS

    def body(up_ref, upc_ref, upcn_ref, da_ref, dan_ref, w_ref, dup_ref, dw_ref):
        i = pl.program_id(0)
        last = pl.num_programs(0) - 1

        @pl.when(i == 0)
        def _():
            dw_ref[...] = jnp.zeros_like(dw_ref)
        live = jnp.where(i < last, 1.0, 0.0)
        for cb in range(D_FF // W):
            gs = slice(cb * W, (cb + 1) * W)
            vs = slice(D_FF + cb * W, D_FF + (cb + 1) * W)
            g = jnp.concatenate([upc_ref[:, gs], upcn_ref[:, gs]], axis=0).astype(F32)
            v = jnp.concatenate([upc_ref[:, vs], upcn_ref[:, vs]], axis=0).astype(F32)
            da = jnp.concatenate([da_ref[:, gs].astype(F32), dan_ref[:, gs].astype(F32) * live], axis=0)
            s = _sigmoid(g)
            silu = g * s
            grads = (da * v * (s * (1.0 + g * (1.0 - s))), da * silu)
            for cs, d in zip((gs, vs), grads):
                u = up_ref[:, cs].astype(F32)
                acc = None
                for k in range(F_CONV):
                    shifted = _rows_after(d, 2 - k)[:tm]
                    term = w_ref[k:k + 1, cs] * shifted
                    acc = term if acc is None else acc + term
                    dw_ref[k:k + 1, cs] += _rowsum(shifted * u)
                dup_ref[:, cs] = acc.astype(BF16)

    hn = _halo_next_index(tm, halo, T)
    return _pallas(
        comm, body, name=name, grid=(T // tm,),
        in_specs=[pl.BlockSpec((tm, 2 * D_FF), lambda i: (i, 0)),
                  pl.BlockSpec((tm, 2 * D_FF), lambda i: (i, 0)),
                  pl.BlockSpec((halo, 2 * D_FF), lambda i: (hn(i), 0)),
                  pl.BlockSpec((tm, D_FF), lambda i: (i, 0)),
                  pl.BlockSpec((halo, D_FF), lambda i: (hn(i), 0)),
                  pl.BlockSpec((None, F_CONV, 2 * D_FF), lambda i: (layer, 0, 0))],
        out_specs=[pl.BlockSpec((tm, 2 * D_FF), lambda i: (i, 0)),
                   pl.BlockSpec((F_CONV, 2 * D_FF), lambda i: (0, 0))],
        out_shape=[jax.ShapeDtypeStruct((T, 2 * D_FF), BF16),
                   jax.ShapeDtypeStruct((F_CONV, 2 * D_FF), F32)],
        compiler_params=_cparams(("arbitrary",), 56),
    )(up, upc, upc, da, da, conv_w)


def _local_step(x, tgt, small, plan):
    T = x.shape[0]
    tm_e = _pick(T, 256)
    tm_n = _pick(T, 512)
    tm = _pick(T, 1024)
    tm_r = _pick(T, 2048)
    tt = _pick(T, 512)
    nm = small["norm_mix"].reshape(2, 1, D_MODEL)
    nf = small["norm_ffn"].reshape(2, 1, D_MODEL)
    ngf = small["norm_final"].reshape(1, D_MODEL)
    b_s = small["a_b_s"].reshape(A_HEADS, CHUNK, 1)
    w_s = small["a_w_s"].reshape(A_HEADS, CHUNK, CHUNK)
    b_conv_w = small["b_conv_w"].reshape(B_CONV, D_B)
    sg = {}
    wt, cm = plan.weight, plan.comm

    h_m0 = _rmsnorm_fwd(x, nm, layer=0, tm=tm_n, name="norm_mix0")
    z_ab = _mm_nn(h_m0, wt("ab_w_in", 0), layer=0, tm=tm, tn=512, out_dtype=BF16, name="ab_in", comm=cm("ab_in"))
    yab, cb = _mixer_ab_fwd(z_ab, small["a_ln_g"], small["a_ln_b"], w_s, b_s, b_conv_w, small["b_conv_b"],
                            small["b_ln_g"], small["b_ln_b"], tm=tm_e, name="mixer_ab", comm=cm("mixer_ab"))
    x1 = _mm_nn(yab, wt("ab_w_out", 0), layer=0, tm=tm, tn=512, residual=x, name="ab_out", comm=cm("ab_out"))

    def ffn_fwd(xin, layer):
        h = _rmsnorm_fwd(xin, nf, layer=layer, tm=tm_n, name=f"norm_ffn{layer}", comm=cm(f"norm_ffn{layer}"))
        up = _mm_nn(h, wt("f_w_up", layer), layer=0, tm=tm, tn=1408, out_dtype=BF16, name=f"ffn_up{layer}",
                    comm=cm(f"ffn_up{layer}"))
        a, upc = _ffn_act_fwd(up, small["f_conv_w"], layer=layer, tm=tm_e, name=f"ffn_act{layer}",
                              comm=cm(f"ffn_act{layer}"))
        xout = _mm_nn(a, wt("f_w_down", layer), layer=0, tm=tm, tn=512, residual=xin, name=f"ffn_down{layer}",
                      comm=cm(f"ffn_down{layer}"))
        return h, up, upc, a, xout

    h_f0, up0, upc0, a0, x2 = ffn_fwd(x1, 0)
    h_m1 = _rmsnorm_fwd(x2, nm, layer=1, tm=tm_n, name="norm_mix1")
    z_c = _mm_nn(h_m1, wt("c_w_in", 0), layer=0, tm=tm, tn=768, out_dtype=BF16, name="c_in", comm=cm("c_in"))
    r = _mixer_c_fwd(z_c, small["c_conv_w"], tm=tm_e, name="mixer_c", comm=cm("mixer_c"))
    x3 = _mm_nn(r, wt("c_w_out", 0), layer=0, tm=tm, tn=512, residual=x2, name="c_out", comm=cm("c_out"))
    h_f1, up1, upc1, a1, x4 = ffn_fwd(x3, 1)
    loss, dx, sg["norm_final"] = _loss_head(x4, tgt, ngf, tm=tm_n, name="loss_head")

    def ffn_bwd(dx, xin, h, up, upc, a, layer):
        da = _mm_nt(dx, wt("f_w_down", layer), layer=0, tm=tm, tn=1408, out_dtype=BF16,
                    name=f"ffn_down_dx{layer}", comm=cm(f"ffn_down_dx{layer}"))
        plan.grad_ready("f_w_down", layer, _mm_tn(a, dx, shards=None, tk=1408, tn=1024, tt=tt,
                                                  name=f"ffn_down_dw{layer}", comm=cm(f"ffn_down_dw{layer}")))
        dup, dcw = _ffn_act_bwd(up, upc, da, small["f_conv_w"], layer=layer, tm=tm_e, name=f"ffn_act_bwd{layer}",
                                comm=cm(f"ffn_act_bwd{layer}"))
        dh = _mm_nt(dup, wt("f_w_up", layer), layer=0, tm=tm_r, tn=None, name=f"ffn_up_dx{layer}",
                    comm=cm(f"ffn_up_dx{layer}"))
        plan.grad_ready("f_w_up", layer, _mm_tn(h, dup, shards=N_CHIPS, tk=1024, tn=1408, tt=tt,
                                                name=f"ffn_up_dw{layer}", comm=cm(f"ffn_up_dw{layer}")))
        dxin, dg = _rmsnorm_bwd(xin, nf, dh, dx, layer=layer, tm=tm_n, name=f"norm_ffn_bwd{layer}",
                                comm=cm(f"norm_ffn_bwd{layer}"))
        return dxin, dg, dcw

    dx, dnf1, dfc1 = ffn_bwd(dx, x3, h_f1, up1, upc1, a1, 1)
    dr = _mm_nt(dx, wt("c_w_out", 0), layer=0, tm=tm, tn=512, out_dtype=BF16, name="c_out_dx", comm=cm("c_out_dx"))
    plan.grad_ready("c_w_out", 0, _mm_tn(r, dx, shards=None, tk=1024, tn=1024, tt=tt, name="c_out_dw",
                                         comm=cm("c_out_dw")))
    dz_c, dccw = _mixer_c_bwd(z_c, dr, small["c_conv_w"], tm=tm_e, name="mixer_c_bwd", comm=cm("mixer_c_bwd"))
    sg["c_conv_w"] = dccw.reshape(1, C_CONV, D_MODEL)
    dh = _mm_nt(dz_c, wt("c_w_in", 0), layer=0, tm=tm_r, tn=None, name="c_in_dx", comm=cm("c_in_dx"))
    plan.grad_ready("c_w_in", 0, _mm_tn(h_m1, dz_c, shards=N_CHIPS, tk=1024, tn=768, tt=tt, name="c_in_dw",
                                        comm=cm("c_in_dw")))
    dx, dnm1 = _rmsnorm_bwd(x2, nm, dh, dx, layer=1, tm=tm_n, name="norm_mix_bwd1", comm=cm("norm_mix_bwd1"))
    dx, dnf0, dfc0 = ffn_bwd(dx, x1, h_f0, up0, upc0, a0, 0)
    dyab = _mm_nt(dx, wt("ab_w_out", 0), layer=0, tm=tm, tn=512, out_dtype=BF16, name="ab_out_dx",
                  comm=cm("ab_out_dx"))
    plan.grad_ready("ab_w_out", 0, _mm_tn(yab, dx, shards=None, tk=1024, tn=1024, tt=tt, name="ab_out_dw",
                                          comm=cm("ab_out_dw")))
    (dza, dcb, sg["a_ln_g"], sg["a_ln_b"], dws, dbs, sg["b_ln_g"], sg["b_ln_b"]) = _mixer_ab_bwd_pre(
        z_ab, cb, dyab, small["a_ln_g"], small["a_ln_b"], w_s, b_s, small["b_ln_g"], small["b_ln_b"],
        tm=tm_e, name="mixer_ab_bwd", comm=cm("mixer_ab_bwd"))
    dz_ab, dbcw, sg["b_conv_b"] = _mixer_b_conv_bwd(z_ab, dcb, b_conv_w, dza, tm=tm_e, name="mixer_b_conv_bwd",
                                                    comm=cm("mixer_b_conv_bwd"))
    sg["a_w_s"] = dws.reshape(1, A_HEADS, CHUNK, CHUNK)
    sg["a_b_s"] = dbs.reshape(1, A_HEADS, CHUNK)
    sg["b_conv_w"] = dbcw.reshape(1, B_CONV, D_B)
    plan.grad_ready("ab_w_in", 0, _mm_tn(h_m0, dz_ab, shards=N_CHIPS, tk=1024, tn=512, tt=tt, name="ab_in_dw",
                                         comm=cm("ab_in_dw")))
    dh = _mm_nt(dz_ab, wt("ab_w_in", 0), layer=0, tm=tm_r, tn=None, name="ab_in_dx", comm=cm("ab_in_dx"))
    dx, dnm0 = _rmsnorm_bwd(x, nm, dh, dx, layer=0, tm=tm_n, name="norm_mix_bwd0", comm=cm("norm_mix_bwd0"))

    sg["norm_mix"] = [dnm0, dnm1]
    sg["norm_ffn"] = [dnf0, dnf1]
    sg["f_conv_w"] = [dfc0, dfc1]
    return loss, dx, sg


BLOCK_BYTES = 3 * 1024 * 1024


BF16_SUBLANES = 16


def _row_tile(rows, row_bytes, step=SUBLANES):
    best = None
    for tr in range(step, rows + 1, step):
        if rows % tr == 0 and tr * row_bytes <= BLOCK_BYTES:
            best = tr
    if best is None:
        raise ValueError(f"no row tile for {rows}")
    return best


def _place_scalars():
    x, y, c = lax.axis_index("x"), lax.axis_index("y"), lax.axis_index("c")
    return jnp.stack([c, 2 * x + y, 2 * (1 - x) + y, 2 * x + (1 - y), 2 * (1 - x) + (1 - y)]).astype(jnp.int32)


def _cast_into_slot(w, place, *, layer, name):
    L, rows, cols = w.shape
    tr = _row_tile(rows, cols * 4, BF16_SUBLANES)

    def body(place_ref, w_ref, o_ref):
        o_ref[...] = w_ref[...].astype(BF16)

    return pl.pallas_call(
        body, name=name,
        grid_spec=pltpu.PrefetchScalarGridSpec(
            num_scalar_prefetch=1, grid=(rows // tr,),
            in_specs=[pl.BlockSpec((None, tr, cols), lambda i, p: (layer, i, 0))],
            out_specs=pl.BlockSpec((None, None, tr, cols), lambda i, p: (0, p[1], i, 0))),
        out_shape=jax.ShapeDtypeStruct((1, N_CHIPS, rows, cols), BF16),
        compiler_params=_cparams(("parallel",), 32),
    )(place, w)


def _pair_sum(g, theirs, place, *, name):
    S, rows, cols = g.shape
    half = rows // 2
    tr = _row_tile(half, cols * 4, BF16_SUBLANES)
    nb = half // tr

    def body(place_ref, g_ref, t_ref, o_ref):
        o_ref[...] = (g_ref[...] + t_ref[...]).astype(BF16)

    spec = pl.BlockSpec((None, tr, cols), lambda s, i, p: (s, i, 0))
    return pl.pallas_call(
        body, name=name,
        grid_spec=pltpu.PrefetchScalarGridSpec(
            num_scalar_prefetch=1, grid=(S, nb),
            in_specs=[pl.BlockSpec((None, tr, cols), lambda s, i, p: (s, p[0] * nb + i, 0)), spec],
            out_specs=spec),
        out_shape=jax.ShapeDtypeStruct((S, half, cols), BF16),
        compiler_params=_cparams(("parallel", "parallel"), 32),
    )(place, g, theirs)


def _chip_sum(p, r, g_prev, place, *, layer, shape, name):
    L, rows, cols = shape
    half = rows // 2
    tr = _row_tile(half, cols * 4, BF16_SUBLANES)
    nb = half // tr

    def body(place_ref, p_ref, r_ref, *rest):
        o_ref = rest[-1]
        mine = p_ref[...].astype(F32)
        peers = [r_ref[j].astype(F32) for j in range(3)]
        acc = None
        for s in range(N_CHIPS):
            term = jnp.where(place_ref[1] == s, mine,
                             jnp.where(place_ref[2] == s, peers[0],
                                       jnp.where(place_ref[3] == s, peers[1], peers[2])))
            acc = term if acc is None else acc + term
        o_ref[...] = acc

    in_specs = [pl.BlockSpec((None, tr, cols), lambda i, pr: (pr[1], i, 0)),
                pl.BlockSpec((3, tr, cols), lambda i, pr: (0, i, 0))]
    args = [place, p, r]
    aliases = {}
    if g_prev is not None:
        in_specs.append(ANY)
        args.append(g_prev)
        aliases = {3: 0}
    return pl.pallas_call(
        body, name=name,
        grid_spec=pltpu.PrefetchScalarGridSpec(
            num_scalar_prefetch=1, grid=(nb,), in_specs=in_specs,
            out_specs=pl.BlockSpec((None, tr, cols), lambda i, pr: (layer, pr[0] * nb + i, 0))),
        out_shape=jax.ShapeDtypeStruct(shape, F32), input_output_aliases=aliases,
        compiler_params=_cparams(("parallel",), 32),
    )(*args)


def _adamw_math(w, g, m, v):
    m2 = ADAM_B1 * m + (1.0 - ADAM_B1) * g
    v2 = ADAM_B2 * v + (1.0 - ADAM_B2) * (g * g)
    m_hat = m2 / (1.0 - ADAM_B1 ** ADAM_STEP)
    v_hat = v2 / (1.0 - ADAM_B2 ** ADAM_STEP)
    delta = -ADAM_LR * (m_hat / (jnp.sqrt(v_hat) + ADAM_EPS) + ADAM_WD * w)
    return delta, m2, v2


def _adamw(w, g, m, v, *, name):
    L, rows, cols = w.shape
    tr = _row_tile(rows, cols * 4)

    def body(w_ref, g_ref, m_ref, v_ref, d_ref, m2_ref, v2_ref):
        d, m2, v2 = _adamw_math(w_ref[...], g_ref[...], m_ref[...], v_ref[...])
        d_ref[...] = d
        m2_ref[...] = m2
        v2_ref[...] = v2

    spec = pl.BlockSpec((None, tr, cols), lambda l, i: (l, i, 0))
    shape = jax.ShapeDtypeStruct(w.shape, F32)
    return pl.pallas_call(
        body, name=name, grid=(L, rows // tr), in_specs=[spec] * 4, out_specs=[spec] * 3,
        out_shape=[shape] * 3,
        compiler_params=_cparams(("parallel", "parallel"), 48),
    )(w, g, m, v)


def _exchange_packs(pack, *, reduce, name):
    R = pack.shape[0]
    ndev = 2 * N_CHIPS

    def body(p_ref, o_ref, *scratch):
        if reduce:
            buf, send, recv = scratch
        else:
            buf = o_ref
            send, recv = scratch
        x, y, c = lax.axis_index("x"), lax.axis_index("y"), lax.axis_index("c")
        me = 4 * x + 2 * y + c
        buf[me] = p_ref[...]
        sends = []
        for q in range(1, ndev):
            qx, qy, qc = (q >> 2) & 1, (q >> 1) & 1, q & 1
            peer = (x ^ qx, y ^ qy, c ^ qc)
            rc = _remote(p_ref, buf.at[me], send.at[q - 1], recv.at[q - 1], peer)
            rc.start()
            sends.append(rc)
        for q in range(1, ndev):
            qx, qy, qc = (q >> 2) & 1, (q >> 1) & 1, q & 1
            slot = buf.at[4 * (x ^ qx) + 2 * (y ^ qy) + (c ^ qc)]
            _remote(slot, slot, send.at[q - 1], recv.at[q - 1], (x ^ qx, y ^ qy, c ^ qc)).wait_recv()
        for rc in sends:
            rc.wait_send()
        if reduce:
            acc = buf[0]
            for d in range(1, ndev):
                acc = acc + buf[d]
            o_ref[...] = acc

    vm = pl.BlockSpec(memory_space=pltpu.VMEM)
    sems = [pltpu.SemaphoreType.DMA((ndev - 1,)), pltpu.SemaphoreType.DMA((ndev - 1,))]
    if reduce:
        out_shape = jax.ShapeDtypeStruct((R, LANES), F32)
        scratch = [pltpu.VMEM((ndev, R, LANES), F32)] + sems
    else:
        out_shape = jax.ShapeDtypeStruct((ndev, R, LANES), F32)
        scratch = sems
    return pl.pallas_call(
        body, name=name, in_specs=[vm], out_specs=vm, out_shape=out_shape, scratch_shapes=scratch,
        compiler_params=pltpu.CompilerParams(vmem_limit_bytes=VMEM_BYTES_MAX),
    )(pack)


PACK_UNIT = SUBLANES * LANES


def _pack(arrays):
    flat, sizes = [], []
    for a in arrays:
        pieces = a if isinstance(a, (list, tuple)) else [a]
        v = jnp.concatenate([p.reshape(-1) for p in pieces]) if len(pieces) > 1 else pieces[0].reshape(-1)
        size = v.shape[0]
        padded = -(-size // PACK_UNIT) * PACK_UNIT
        flat.append(jnp.pad(v, (0, padded - size)))
        sizes.append((size, padded))
    return jnp.concatenate(flat).reshape(-1, LANES), sizes


def _unpack(pack, sizes, shapes):
    v = pack.reshape(-1)
    out, off = [], 0
    for (size, padded), shape in zip(sizes, shapes):
        out.append(v[off:off + size].reshape(shape))
        off += padded
    return out


BIG = ("ab_w_in", "ab_w_out", "c_w_in", "c_w_out", "f_w_up", "f_w_down")
COL_SHARDED = ("ab_w_in", "c_w_in", "f_w_up")
SMALL_REPLICATED = ("norm_mix", "norm_ffn", "norm_final", "a_ln_g", "a_ln_b", "a_w_s", "a_b_s",
                    "b_conv_b", "b_ln_g", "b_ln_b")
SMALL_SHARDED = ("b_conv_w", "c_conv_w", "f_conv_w")
SMALL = SMALL_REPLICATED + SMALL_SHARDED
ALL_WEIGHTS = ("norm_mix", "norm_ffn", "norm_final", "ab_w_in", "a_ln_g", "a_ln_b", "a_w_s", "a_b_s",
               "b_conv_w", "b_conv_b", "b_ln_g", "b_ln_b", "ab_w_out", "c_w_in", "c_conv_w", "c_w_out",
               "f_w_up", "f_conv_w", "f_w_down")


SCHEDULE = {
    "ab_in": [("gi", "f_w_up", 0, 0, 4), ("gi", "ab_w_out", 0)],
    "mixer_ab": [("gd", "f_w_up", 0, 0, 4), ("gd", "ab_w_out", 0), ("gi", "f_w_up", 0, 1, 4),
                 ("gi", "f_w_up", 0, 2, 4)],
    "ab_out": [("gd", "f_w_up", 0, 1, 4), ("gd", "f_w_up", 0, 2, 4), ("gi", "f_w_up", 0, 3, 4)],
    "norm_ffn0": [("gd", "f_w_up", 0, 3, 4)],
    "ffn_up0": [("gi", "f_w_down", 0), ("gi", "c_w_in", 0, 0, 2)],
    "ffn_act0": [("gd", "f_w_down", 0), ("gd", "c_w_in", 0, 0, 2), ("gi", "c_w_in", 0, 1, 2),
                 ("gi", "f_w_up", 1, 0, 4), ("gi", "f_w_up", 1, 1, 4)],
    "ffn_down0": [("gd", "c_w_in", 0, 1, 2), ("gd", "f_w_up", 1, 0, 4), ("gd", "f_w_up", 1, 1, 4),
                  ("gi", "f_w_up", 1, 2, 4)],
    "c_in": [("gd", "f_w_up", 1, 2, 4), ("gi", "f_w_up", 1, 3, 4), ("gi", "c_w_out", 0)],
    "mixer_c": [("gd", "f_w_up", 1, 3, 4), ("gd", "c_w_out", 0), ("gi", "f_w_down", 1, 0, 2)],
    "c_out": [("gd", "f_w_down", 1, 0, 2), ("gi", "f_w_down", 1, 1, 2)],
    "ffn_up1": [("gd", "f_w_down", 1, 1, 2)],
    "ffn_act_bwd1": [("px", "f_w_down", 1)],
    "ffn_up_dx1": [("cx", "f_w_down", 1)],
    "mixer_c_bwd": [("px", "f_w_up", 1), ("px", "c_w_out", 0)],
    "c_in_dx": [("cx", "f_w_up", 1, 0, 2)],
    "c_in_dw": [("cx", "f_w_up", 1, 1, 2), ("cx", "c_w_out", 0)],
    "norm_mix_bwd1": [("px", "c_w_in", 0)],
    "ffn_down_dx0": [("cx", "c_w_in", 0, 0, 2)],
    "ffn_down_dw0": [("cx", "c_w_in", 0, 1, 2)],
    "ffn_act_bwd0": [("px", "f_w_down", 0)],
    "ffn_up_dx0": [("cx", "f_w_down", 0)],
    "mixer_ab_bwd": [("px", "f_w_up", 0), ("px", "ab_w_out", 0)],
    "mixer_b_conv_bwd": [("cx", "f_w_up", 0, 0, 2), ("cx", "ab_w_out", 0)],
    "ab_in_dw": [("cx", "f_w_up", 0, 1, 2)],
    "ab_in_dx": [("px", "ab_w_in", 0)],
    "norm_mix_bwd0": [("cx", "ab_w_in", 0)],
}


class _Plan:
    def __init__(self, shapes, place):
        self.shapes, self.place, self.bufs = shapes, place, {}

    def weight(self, name, layer):
        g = self.bufs[f"w:{name}:{layer}"]
        if name in COL_SHARDED:
            return g
        _, S, rows, cols = g.shape
        return g.reshape(1, S * rows, cols)

    def grad_ready(self, name, layer, g):
        _, rows, cols = self.shapes[name]
        hbm = lambda a: pltpu.with_memory_space_constraint(a, pltpu.HBM)
        self.bufs[f"g:{name}:{layer}"] = g.reshape(N_CHIPS, rows, cols)
        self.bufs[f"t:{name}:{layer}"] = hbm(lax.empty((N_CHIPS, rows // 2, cols), F32))
        self.bufs[f"l:{name}:{layer}"] = hbm(lax.empty((3, rows // 2, cols), BF16))

    def job(self, kind, name, layer, part=0, parts=1):
        _, rows, cols = self.shapes[name]
        key = f"{name}:{layer}"
        if kind == "gi":
            return _job_gather_ici("w:" + key, rows, part, parts)
        if kind == "gd":
            return _job_gather_d2d("w:" + key, rows, part, parts)
        if kind == "px":
            return _job_pair_exchange("g:" + key, "t:" + key, rows)
        if kind == "cx":
            if "p:" + key not in self.bufs:
                self.bufs["p:" + key] = _pair_sum(self.bufs["g:" + key], self.bufs["t:" + key], self.place,
                                                  name=f"pair_sum_{name}{layer}")
            nr = rows // 2 // parts
            return _job_chip_exchange("p:" + key, "l:" + key, part * nr, nr)
        if kind == "ps":
            return _job_pair_share("G:" + name, layer, rows)
        raise ValueError(kind)

    def comm(self, call):
        specs = SCHEDULE.get(call)
        return None if specs is None else _Comm(self, [self.job(*spec) for spec in specs])


def _step(x, tgt, w, m, v):
    chip = 2 * lax.axis_index("x") + lax.axis_index("y")
    place = _place_scalars()
    plan = _Plan({n: w[n].shape for n in BIG}, place)
    items = [(n, l) for n in BIG for l in range(w[n].shape[0])]

    for n, l in items:
        plan.bufs[f"w:{n}:{l}"] = _cast_into_slot(w[n], place, layer=l, name=f"cast_{n}{l}")
    _comm_only(plan, [[plan.job("gi", "ab_w_in", 0)], [plan.job("gd", "ab_w_in", 0)]], name="gather_first")
    conv_pack, conv_sizes = _pack([w[n] for n in SMALL_SHARDED])
    conv_all = _exchange_packs(conv_pack, reduce=False, name="gather_conv_weights")
    conv_shapes = [w[n].shape for n in SMALL_SHARDED]
    per_chip = [_unpack(conv_all[2 * s], conv_sizes, conv_shapes) for s in range(N_CHIPS)]
    small = {n: w[n] for n in SMALL_REPLICATED}
    for idx, n in enumerate(SMALL_SHARDED):
        small[n] = jnp.concatenate([per_chip[s][idx] for s in range(N_CHIPS)], axis=-1)

    loss, dx, sg = _local_step(x, tgt, small, plan)

    for n, l in items:
        plan.bufs["G:" + n] = _chip_sum(plan.bufs[f"p:{n}:{l}"], plan.bufs[f"l:{n}:{l}"], plan.bufs.get("G:" + n),
                                        place, layer=l, shape=w[n].shape, name=f"chip_sum_{n}{l}")
    _comm_only(plan, [[plan.job("ps", n, l) for n, l in items]], name="reduce_pair_share")
    grads_big = [plan.bufs["G:" + n] for n in BIG]

    g_pack, g_sizes = _pack([sg[n] for n in SMALL])
    g_sum = _exchange_packs(g_pack, reduce=True, name="allreduce_small_grads")
    full_shapes = [small[n].shape for n in SMALL]
    g_small = dict(zip(SMALL, _unpack(g_sum, g_sizes, full_shapes)))
    for n in SMALL_SHARDED:
        width = w[n].shape[-1]
        g_small[n] = lax.dynamic_slice_in_dim(g_small[n], chip * width, width, axis=g_small[n].ndim - 1)

    grad, delta, new_m, new_v = {}, {}, {}, {}
    for n, g in zip(BIG, grads_big):
        grad[n] = g
        delta[n], new_m[n], new_v[n] = _adamw(w[n], g, m[n], v[n], name=f"adamw_{n}")
    shapes = [w[n].shape for n in SMALL]
    wp, sizes = _pack([w[n] for n in SMALL])
    gp, _ = _pack([g_small[n] for n in SMALL])
    mp, _ = _pack([m[n] for n in SMALL])
    vp, _ = _pack([v[n] for n in SMALL])
    R = wp.shape[0]
    dp, m2p, v2p = _adamw(wp.reshape(1, R, LANES), gp.reshape(1, R, LANES), mp.reshape(1, R, LANES),
                          vp.reshape(1, R, LANES), name="adamw_small")
    for n, d_, m_, v_ in zip(SMALL, _unpack(dp, sizes, shapes), _unpack(m2p, sizes, shapes),
                             _unpack(v2p, sizes, shapes)):
        grad[n] = g_small[n]
        delta[n], new_m[n], new_v[n] = d_, m_, v_
    return loss, dx, grad, delta, new_m, new_v


def kernel(x, norm_mix, norm_ffn, norm_final, ab_w_in, a_ln_g, a_ln_b, a_w_s, a_b_s, b_conv_w, b_conv_b, b_ln_g, b_ln_b, ab_w_out, c_w_in, c_conv_w, c_w_out, f_w_up, f_conv_w, f_w_down, loss_target, m_norm_mix, m_norm_ffn, m_norm_final, m_ab_w_in, m_a_ln_g, m_a_ln_b, m_a_w_s, m_a_b_s, m_b_conv_w, m_b_conv_b, m_b_ln_g, m_b_ln_b, m_ab_w_out, m_c_w_in, m_c_conv_w, m_c_w_out, m_f_w_up, m_f_conv_w, m_f_w_down, v_norm_mix, v_norm_ffn, v_norm_final, v_ab_w_in, v_a_ln_g, v_a_ln_b, v_a_w_s, v_a_b_s, v_b_conv_w, v_b_conv_b, v_b_ln_g, v_b_ln_b, v_ab_w_out, v_c_w_in, v_c_conv_w, v_c_w_out, v_f_w_up, v_f_conv_w, v_f_w_down):
    given = dict(locals())
    w = {n: given[n] for n in ALL_WEIGHTS}
    m = {n: given["m_" + n] for n in ALL_WEIGHTS}
    v = {n: given["v_" + n] for n in ALL_WEIGHTS}
    T = x.shape[1]
    loss, dx, grad, delta, new_m, new_v = _step(x.reshape(T, D_MODEL), loss_target.reshape(T, D_MODEL), w, m, v)
    loss = lax.psum(loss[0, 0], ("x", "y", "c"))
    out = [loss, dx.reshape(x.shape)]
    for d in (grad, delta, new_m, new_v):
        out += [d[n] for n in ALL_WEIGHTS]
    return tuple(out)
```

```python
import functools
import math

import jax
import jax.numpy as jnp
from jax import lax
from jax.experimental import pallas as pl
from jax.experimental.pallas import tpu as pltpu

F32 = jnp.float32
BF16 = jnp.bfloat16

EPS = 1e-6
D_MODEL = 1024
CHUNK = 128
HEAD_DIM = 128
A_HEADS = 4
D_A = 512
D_B = 512
B_CONV = 31
C_CONV = 3
D_FF = 2816
F_CONV = 3
N_CHIPS = 4

ADAM_LR = 0.001
ADAM_B1 = 0.9
ADAM_B2 = 0.999
ADAM_EPS = 1e-08
ADAM_WD = 0.01
ADAM_STEP = 10

SUBLANES = 8
LANES = 128
HALO_SHORT = 16
HALO_LONG = 32
VMEM_BYTES_MAX = 60000 * 1024

INV_SQRT2 = 1.0 / math.sqrt(2.0)
INV_SQRT_2PI = 1.0 / math.sqrt(2.0 * math.pi)

MESH = pl.DeviceIdType.MESH


def _cparams(sem, vmem_mb):
    del vmem_mb
    return pltpu.CompilerParams(dimension_semantics=sem, vmem_limit_bytes=VMEM_BYTES_MAX)


def _pick(total, pref):
    for c in (2048, 1024, 512, 256, 128):
        if c <= pref and total % c == 0:
            return c
    raise ValueError(f"no tile for {total}")


def _sigmoid(x):
    return jax.nn.sigmoid(x)


def _silu(x):
    return x * _sigmoid(x)


def _dsilu(x):
    s = _sigmoid(x)
    return s * (1.0 + x * (1.0 - s))


def _gelu(x):
    return 0.5 * x * (1.0 + lax.erf(x * INV_SQRT2))


def _dgelu(x):
    return 0.5 * (1.0 + lax.erf(x * INV_SQRT2)) + x * jnp.exp(-0.5 * x * x) * INV_SQRT_2PI


def _ln_stats(x):
    mu = jnp.mean(x, axis=-1, keepdims=True)
    xc = x - mu
    var = jnp.mean(xc * xc, axis=-1, keepdims=True)
    r = lax.rsqrt(var + EPS)
    return xc * r, r


def _ln_bwd(dy, xh, r, g):
    dxh = dy * g
    m1 = jnp.mean(dxh, axis=-1, keepdims=True)
    m2 = jnp.mean(dxh * xh, axis=-1, keepdims=True)
    return r * (dxh - m1 - xh * m2)


def _rowsum(x):
    return jnp.sum(x, axis=0, keepdims=True)


ANY = pl.BlockSpec(memory_space=pltpu.HBM)


def _place():
    x, y, c = lax.axis_index("x"), lax.axis_index("y"), lax.axis_index("c")
    peers = [(1 - x, y), (x, 1 - y), (1 - x, 1 - y)]
    return x, y, c, 2 * x + y, (x, y, 1 - c), peers


def _half(rows, which):
    return pl.ds(which * (rows // 2), rows // 2)


def _remote(src, dst, send_sem, recv_sem, device):
    return pltpu.make_async_remote_copy(src_ref=src, dst_ref=dst, send_sem=send_sem, recv_sem=recv_sem,
                                        device_id=device, device_id_type=MESH)


class _Job:
    def __init__(self, reads, writes, ncopies, copies):
        self.reads, self.writes, self.ncopies, self.copies = reads, writes, ncopies, copies


def _share(rows, which, part, parts):
    nr = rows // 2 // parts
    return pl.ds(which * (rows // 2) + part * nr, nr)


def _job_gather_ici(name, rows, part, parts):
    def copies(src, dst, sem):
        x, y, c, k, sib, peers = _place()
        mine_rows = _share(rows, c, part, parts)
        out = []
        for j, (px, py) in enumerate(peers):
            mine = src[name].at[0, k, mine_rows]
            out.append((_remote(mine, dst[name].at[0, k, mine_rows], sem(j, 0), sem(j, 1), (px, py, c)),
                        _remote(mine, dst[name].at[0, 2 * px + py, mine_rows], sem(j, 0), sem(j, 1), (px, py, c))))
        return out
    return _Job([], [name], 3, copies)


def _job_gather_d2d(name, rows, part, parts):
    def copies(src, dst, sem):
        x, y, c, k, sib, peers = _place()
        out = []
        for j, (px, py) in enumerate(peers):
            landed = src[name].at[0, 2 * px + py, _share(rows, c, part, parts)]
            out.append((_remote(landed, dst[name].at[0, 2 * px + py, _share(rows, c, part, parts)],
                                sem(j, 0), sem(j, 1), sib),
                        _remote(landed, dst[name].at[0, 2 * px + py, _share(rows, 1 - c, part, parts)],
                                sem(j, 0), sem(j, 1), sib)))
        return out
    return _Job([], [name], 3, copies)


def _job_pair_exchange(gname, tname, rows):
    def copies(src, dst, sem):
        x, y, c, k, sib, peers = _place()
        cp = _remote(src[gname].at[:, _half(rows, 1 - c), :], dst[tname], sem(0, 0), sem(0, 1), sib)
        return [(cp, cp)]
    return _Job([gname], [tname], 1, copies)


def _job_chip_exchange(pname, lname, r0, nr):
    def copies(src, dst, sem):
        x, y, c, k, sib, peers = _place()
        out = []
        for j, (px, py) in enumerate(peers):
            cp = _remote(src[pname].at[2 * px + py, pl.ds(r0, nr)], dst[lname].at[j, pl.ds(r0, nr)],
                         sem(j, 0), sem(j, 1), (px, py, c))
            out.append((cp, cp))
        return out
    return _Job([pname], [lname], 3, copies)


def _job_pair_share(name, layer, rows):
    def copies(src, dst, sem):
        x, y, c, k, sib, peers = _place()
        mine = src[name].at[layer, _half(rows, c)]
        return [(_remote(mine, dst[name].at[layer, _half(rows, c)], sem(0, 0), sem(0, 1), sib),
                 _remote(mine, dst[name].at[layer, _half(rows, 1 - c)], sem(0, 0), sem(0, 1), sib))]
    return _Job([], [name], 1, copies)


class _Comm:
    def __init__(self, plan, jobs):
        self.plan, self.jobs = plan, jobs
        self.writes, self.reads = [], []
        for job in jobs:
            for n in job.writes:
                if n not in self.writes:
                    self.writes.append(n)
        for job in jobs:
            for n in job.reads:
                if n not in self.writes and n not in self.reads:
                    self.reads.append(n)
        self.ncopies = sum(job.ncopies for job in jobs)

    def descriptors(self, src, dst, sems, base):
        out = []
        for job in self.jobs:
            sem = lambda j, which, base=base: sems.at[base + j, which]
            out += job.copies(src, dst, sem)
            base += job.ncopies
        return out

    def start(self, src, dst, sems, base=0):
        for first, _ in self.descriptors(src, dst, sems, base):
            first.start()

    def finish(self, src, dst, sems, base=0):
        for _, landed in self.descriptors(src, dst, sems, base):
            landed.wait()


def _comm_operands(comm):
    bufs = comm.plan.bufs
    shapes = [jax.ShapeDtypeStruct(bufs[n].shape, bufs[n].dtype) for n in comm.writes]
    return [bufs[n] for n in comm.reads] + [bufs[n] for n in comm.writes], shapes


def _pallas(comm, body, *, name, grid, in_specs, out_specs, out_shape, compiler_params, scratch_shapes=(),
            aliases=None):
    aliases = dict(aliases or {})
    if comm is None:
        return pl.pallas_call(body, name=name, grid=grid, in_specs=in_specs, out_specs=out_specs,
                              out_shape=out_shape, scratch_shapes=list(scratch_shapes),
                              input_output_aliases=aliases, compiler_params=compiler_params)
    single = not isinstance(out_shape, (list, tuple))
    base_specs = [out_specs] if single else list(out_specs)
    base_shape = [out_shape] if single else list(out_shape)
    nb, nr, nw, nbo, nsc = len(in_specs), len(comm.reads), len(comm.writes), len(base_specs), len(scratch_shapes)

    def wrapped(*refs):
        base_in, rd, wr_in = refs[:nb], refs[nb:nb + nr], refs[nb + nr:nb + nr + nw]
        o0 = nb + nr + nw
        base_out, wr_out = refs[o0:o0 + nbo], refs[o0 + nbo:o0 + nbo + nw]
        scratch, sems = refs[o0 + nbo + nw:o0 + nbo + nw + nsc], refs[-1]
        src = dict(zip(comm.reads, rd))
        src.update(zip(comm.writes, wr_in))
        dst = dict(zip(comm.writes, wr_out))
        first = functools.reduce(jnp.logical_and, [pl.program_id(a) == 0 for a in range(len(grid))])
        last = functools.reduce(jnp.logical_and,
                                [pl.program_id(a) == pl.num_programs(a) - 1 for a in range(len(grid))])

        @pl.when(first)
        def _():
            comm.start(src, dst, sems)
        body(*base_in, *base_out, *scratch)

        @pl.when(last)
        def _():
            comm.finish(src, dst, sems)

    operands, shapes = _comm_operands(comm)
    call = pl.pallas_call(
        wrapped, name=name, grid=grid, in_specs=list(in_specs) + [ANY] * (nr + nw),
        out_specs=base_specs + [ANY] * nw, out_shape=base_shape + shapes,
        input_output_aliases={**aliases, **{nb + nr + q: nbo + q for q in range(nw)}},
        scratch_shapes=list(scratch_shapes) + [pltpu.SemaphoreType.DMA((comm.ncopies, 2))],
        compiler_params=compiler_params)

    def run(*args):
        outs = call(*args, *operands)
        for q, n in enumerate(comm.writes):
            comm.plan.bufs[n] = outs[nbo + q]
        return outs[0] if single else list(outs[:nbo])

    return run


def _comm_only(plan, phases, *, name):
    comms = [_Comm(plan, jobs) for jobs in phases]
    both = _Comm(plan, [job for jobs in phases for job in jobs])
    nr, nw = len(both.reads), len(both.writes)

    def body(*refs):
        rd, wr_in, wr_out, sems = refs[:nr], refs[nr:nr + nw], refs[nr + nw:nr + 2 * nw], refs[-1]
        src = dict(zip(both.reads, rd))
        src.update(zip(both.writes, wr_in))
        dst = dict(zip(both.writes, wr_out))
        base = 0
        for comm in comms:
            comm.start(src, dst, sems, base)
            comm.finish(src, dst, sems, base)
            base += comm.ncopies

    operands, shapes = _comm_operands(both)
    outs = pl.pallas_call(
        body, name=name, in_specs=[ANY] * (nr + nw), out_specs=[ANY] * nw, out_shape=shapes,
        input_output_aliases={nr + q: q for q in range(nw)},
        scratch_shapes=[pltpu.SemaphoreType.DMA((both.ncopies, 2))],
    )(*operands)
    for q, n in enumerate(both.writes):
        plan.bufs[n] = outs[q]


def _mm_nn(a, w, *, layer, tm, tn, residual=None, norm=None, out_dtype=F32, name, comm=None):
    T, K = a.shape
    if w.ndim == 4:
        _, S, _, n4 = w.shape
        N = S * n4
        bps = n4 // tn
        w_spec = pl.BlockSpec((None, None, K, tn), lambda j, i: (layer, j // bps, 0, j % bps))
    else:
        N = w.shape[2]
        w_spec = pl.BlockSpec((None, K, tn), lambda j, i: (layer, 0, j))
    in_specs = [pl.BlockSpec((tm, K), lambda j, i: (i, 0)), w_spec]
    args = [a, w]
    if residual is not None:
        in_specs.append(pl.BlockSpec((tm, tn), lambda j, i: (i, j)))
        args.append(residual)
    out_specs = pl.BlockSpec((tm, tn), lambda j, i: (i, j))
    out_shape = jax.ShapeDtypeStruct((T, N), out_dtype)
    if norm is not None:
        assert tn == N
        g, norm_layer = norm
        in_specs.append(pl.BlockSpec((None, 1, N), lambda j, i: (norm_layer, 0, 0)))
        args.append(g)
        out_specs = [out_specs, pl.BlockSpec((tm, tn), lambda j, i: (i, j))]
        out_shape = [out_shape, jax.ShapeDtypeStruct((T, N), BF16)]

    def body(*refs):
        a_ref, w_ref = refs[0], refs[1]
        acc = jnp.dot(a_ref[...].astype(BF16), w_ref[...], preferred_element_type=F32)
        if residual is not None:
            acc = refs[2][...] + acc
        if norm is None:
            refs[-1][...] = acc.astype(out_dtype)
        else:
            refs[-2][...] = acc.astype(out_dtype)
            r = lax.rsqrt(jnp.mean(acc * acc, axis=-1, keepdims=True) + EPS)
            refs[-1][...] = (acc * r * refs[-3][...]).astype(BF16)

    return _pallas(
        comm, body, name=name, grid=(N // tn, T // tm), in_specs=in_specs,
        out_specs=out_specs, out_shape=out_shape,
        compiler_params=_cparams(("parallel", "parallel"), 48),
    )(*args)


def _mm_nt(dy, w, *, layer, tm, tn, name, out_dtype=F32, comm=None):
    T = dy.shape[0]
    nt_dims = (((1,), (1,)), ((), ()))
    if w.ndim == 4:
        _, S, K, n4 = w.shape

        def body(dy_ref, w_ref, o_ref):
            @pl.when(pl.program_id(1) == 0)
            def _():
                o_ref[...] = jnp.zeros_like(o_ref)
            o_ref[...] += lax.dot_general(dy_ref[...].astype(BF16), w_ref[...], nt_dims,
                                          preferred_element_type=F32)

        return _pallas(
            comm, body, name=name, grid=(T // tm, S),
            in_specs=[pl.BlockSpec((tm, n4), lambda i, s: (i, s)),
                      pl.BlockSpec((None, None, K, n4), lambda i, s: (layer, s, 0, 0))],
            out_specs=pl.BlockSpec((tm, K), lambda i, s: (i, 0)),
            out_shape=jax.ShapeDtypeStruct((T, K), F32),
            compiler_params=_cparams(("parallel", "arbitrary"), 48),
        )(dy, w)
    _, R, N = w.shape

    def body2(dy_ref, w_ref, o_ref):
        o_ref[...] = lax.dot_general(dy_ref[...].astype(BF16), w_ref[...], nt_dims,
                                     preferred_element_type=F32).astype(out_dtype)

    return _pallas(
        comm, body2, name=name, grid=(R // tn, T // tm),
        in_specs=[pl.BlockSpec((tm, N), lambda j, i: (i, 0)),
                  pl.BlockSpec((None, tn, N), lambda j, i: (layer, j, 0))],
        out_specs=pl.BlockSpec((tm, tn), lambda j, i: (i, j)),
        out_shape=jax.ShapeDtypeStruct((T, R), out_dtype),
        compiler_params=_cparams(("parallel", "parallel"), 48),
    )(dy, w)


def _mm_tn(a, dy, *, shards, tk, tn, tt, name, comm=None):
    T, K = a.shape
    N = dy.shape[1]
    tn_dims = (((0,), (0,)), ((), ()))

    def body(a_ref, dy_ref, o_ref):
        @pl.when(pl.program_id(2) == 0)
        def _():
            o_ref[...] = jnp.zeros_like(o_ref)
        o_ref[...] += lax.dot_general(a_ref[...].astype(BF16), dy_ref[...].astype(BF16), tn_dims,
                                      preferred_element_type=F32)

    if shards is None:
        out_spec = pl.BlockSpec((tk, tn), lambda k, n, t: (k, n))
        out_shape = jax.ShapeDtypeStruct((K, N), F32)
    else:
        n4 = N // shards
        bps = n4 // tn
        out_spec = pl.BlockSpec((None, tk, tn), lambda k, n, t: (n // bps, k, n % bps))
        out_shape = jax.ShapeDtypeStruct((shards, K, n4), F32)
    return _pallas(
        comm, body, name=name, grid=(K // tk, N // tn, T // tt),
        in_specs=[pl.BlockSpec((tt, tk), lambda k, n, t: (t, k)),
                  pl.BlockSpec((tt, tn), lambda k, n, t: (t, n))],
        out_specs=out_spec, out_shape=out_shape,
        compiler_params=_cparams(("parallel", "parallel", "arbitrary"), 48),
    )(a, dy)


def _rmsnorm_fwd(x, g, *, layer, tm, name, comm=None):
    T, D = x.shape

    def body(x_ref, g_ref, h_ref):
        xf = x_ref[...]
        r = lax.rsqrt(jnp.mean(xf * xf, axis=-1, keepdims=True) + EPS)
        h_ref[...] = (xf * r * g_ref[...]).astype(BF16)

    return _pallas(
        comm, body, name=name, grid=(T // tm,),
        in_specs=[pl.BlockSpec((tm, D), lambda i: (i, 0)),
                  pl.BlockSpec((None, 1, D), lambda i: (layer, 0, 0))],
        out_specs=pl.BlockSpec((tm, D), lambda i: (i, 0)),
        out_shape=jax.ShapeDtypeStruct((T, D), BF16),
        compiler_params=_cparams(("parallel",), 32),
    )(x, g)


def _rmsnorm_bwd(x, g, dh, dres, *, layer, tm, name, comm=None):
    T, D = x.shape

    def body(x_ref, g_ref, dh_ref, dres_ref, dx_ref, dg_ref):
        @pl.when(pl.program_id(0) == 0)
        def _():
            dg_ref[...] = jnp.zeros_like(dg_ref)
        xf = x_ref[...]
        r = lax.rsqrt(jnp.mean(xf * xf, axis=-1, keepdims=True) + EPS)
        xh = xf * r
        dh = dh_ref[...]
        dg_ref[...] += _rowsum(dh * xh)
        dxh = dh * g_ref[...]
        dx_ref[...] = dres_ref[...] + r * (dxh - xh * jnp.mean(dxh * xh, axis=-1, keepdims=True))

    return _pallas(
        comm, body, name=name, grid=(T // tm,),
        in_specs=[pl.BlockSpec((tm, D), lambda i: (i, 0)),
                  pl.BlockSpec((None, 1, D), lambda i: (layer, 0, 0)),
                  pl.BlockSpec((tm, D), lambda i: (i, 0)),
                  pl.BlockSpec((tm, D), lambda i: (i, 0))],
        out_specs=[pl.BlockSpec((tm, D), lambda i: (i, 0)),
                   pl.BlockSpec((1, D), lambda i: (0, 0))],
        out_shape=[jax.ShapeDtypeStruct((T, D), F32), jax.ShapeDtypeStruct((1, D), F32)],
        compiler_params=_cparams(("arbitrary",), 40),
    )(x, g, dh, dres)


def _loss_head(x, tgt, g, *, tm, name, comm=None):
    T, D = x.shape

    def body(x_ref, t_ref, g_ref, loss_ref, dx_ref, dg_ref):
        @pl.when(pl.program_id(0) == 0)
        def _():
            dg_ref[...] = jnp.zeros_like(dg_ref)
            loss_ref[...] = jnp.zeros_like(loss_ref)
        xf = x_ref[...]
        gg = g_ref[...]
        r = lax.rsqrt(jnp.mean(xf * xf, axis=-1, keepdims=True) + EPS)
        xh = xf * r
        err = xh * gg - t_ref[...]
        row = jnp.mean(err * err, axis=-1, keepdims=True)
        loss_ref[...] += 0.5 * jnp.sum(row, axis=0, keepdims=True)
        dy = err * (1.0 / D)
        dg_ref[...] += _rowsum(dy * xh)
        dxh = dy * gg
        dx_ref[...] = r * (dxh - xh * jnp.mean(dxh * xh, axis=-1, keepdims=True))

    return _pallas(
        comm, body, name=name, grid=(T // tm,),
        in_specs=[pl.BlockSpec((tm, D), lambda i: (i, 0)),
                  pl.BlockSpec((tm, D), lambda i: (i, 0)),
                  pl.BlockSpec((1, D), lambda i: (0, 0))],
        out_specs=[pl.BlockSpec((1, 1), lambda i: (0, 0)),
                   pl.BlockSpec((tm, D), lambda i: (i, 0)),
                   pl.BlockSpec((1, D), lambda i: (0, 0))],
        out_shape=[jax.ShapeDtypeStruct((1, 1), F32), jax.ShapeDtypeStruct((T, D), F32),
                   jax.ShapeDtypeStruct((1, D), F32)],
        compiler_params=_cparams(("arbitrary",), 40),
    )(x, tgt, g)


CONV_ROWS = 64
CONV_COLS = 256


def _halo_prev_index(tm, halo):
    per = tm // halo
    return lambda i: jnp.maximum(i * per - 1, 0)


def _halo_next_index(tm, halo, total):
    per = tm // halo
    last = total // halo - 1
    return lambda i: jnp.minimum((i + 1) * per, last)


def _causal_mask():
    t = lax.broadcasted_iota(jnp.int32, (CHUNK, CHUNK), 0)
    s = lax.broadcasted_iota(jnp.int32, (CHUNK, CHUNK), 1)
    return s <= t


def _mixer_ab_fwd(z, a_ln_g, a_ln_b, w_s, b_s, conv_w, conv_b, b_ln_g, b_ln_b, *, tm, name, comm=None):
    T = z.shape[0]
    nchunk = tm // CHUNK
    halo = HALO_LONG

    def body(za_ref, zb_ref, zh_ref, alg_ref, alb_ref, ws_ref, bs_ref, cw_ref, cbias_ref,
             blg_ref, blb_ref, y_ref, cb_ref, ext_ref):
        i = pl.program_id(0)
        gu = _gelu(za_ref[:, :D_A].astype(F32))
        gv = _gelu(za_ref[:, D_A:].astype(F32))
        xh, _ = _ln_stats(gv)
        lv = (xh * alg_ref[...] + alb_ref[...]).astype(BF16)
        mask = _causal_mask()
        for h in range(A_HEADS):
            wm = jnp.where(mask, ws_ref[h], 0.0).astype(BF16)
            cols = slice(h * HEAD_DIM, (h + 1) * HEAD_DIM)
            for c in range(nchunk):
                rows = slice(c * CHUNK, (c + 1) * CHUNK)
                mixed = jnp.dot(wm, lv[rows, cols], preferred_element_type=F32) + bs_ref[h]
                y_ref[rows, cols] = (gu[rows, cols] * mixed).astype(BF16)
        ext_ref[halo:halo + tm, :] = zb_ref[:, :D_B].astype(F32) * _sigmoid(zb_ref[:, D_B:].astype(F32))
        prev = zh_ref[:, :D_B].astype(F32) * _sigmoid(zh_ref[:, D_B:].astype(F32))
        ext_ref[0:halo, :] = jnp.where(i > 0, prev, 0.0)
        for rb in range(tm // CONV_ROWS):
            for cb in range(D_B // CONV_COLS):
                cs = slice(cb * CONV_COLS, (cb + 1) * CONV_COLS)
                window = ext_ref[rb * CONV_ROWS:rb * CONV_ROWS + CONV_ROWS + halo, cs]
                acc = jnp.zeros((CONV_ROWS, CONV_COLS), F32)
                for k in range(B_CONV):
                    shifted = _rows_after(window, halo - (B_CONV - 1) + k)[:CONV_ROWS]
                    acc = acc + cw_ref[k:k + 1, cs] * shifted
                cb_ref[rb * CONV_ROWS:(rb + 1) * CONV_ROWS, cs] = acc + cbias_ref[:, cs]
        xhb, _ = _ln_stats(cb_ref[...])
        y_ref[:, D_A:] = _silu(xhb * blg_ref[...] + blb_ref[...]).astype(BF16)

    row = lambda i: (i, 0)
    par = lambda i: (0, 0)
    return _pallas(
        comm, body, name=name, grid=(T // tm,),
        in_specs=[pl.BlockSpec((tm, 2 * D_A), lambda i: (i, 0)),
                  pl.BlockSpec((tm, 2 * D_B), lambda i: (i, 1)),
                  pl.BlockSpec((halo, 2 * D_B), lambda i: (_halo_prev_index(tm, halo)(i), 1)),
                  pl.BlockSpec((1, D_A), par), pl.BlockSpec((1, D_A), par),
                  pl.BlockSpec((A_HEADS, CHUNK, CHUNK), lambda i: (0, 0, 0)),
                  pl.BlockSpec((A_HEADS, CHUNK, 1), lambda i: (0, 0, 0)),
                  pl.BlockSpec((B_CONV, D_B), par), pl.BlockSpec((1, D_B), par),
                  pl.BlockSpec((1, D_B), par), pl.BlockSpec((1, D_B), par)],
        out_specs=[pl.BlockSpec((tm, D_A + D_B), row), pl.BlockSpec((tm, D_B), row)],
        out_shape=[jax.ShapeDtypeStruct((T, D_A + D_B), BF16), jax.ShapeDtypeStruct((T, D_B), F32)],
        scratch_shapes=[pltpu.VMEM((halo + tm, D_B), F32)],
        compiler_params=_cparams(("parallel",), 40),
    )(z, z, z, a_ln_g, a_ln_b, w_s, b_s, conv_w, conv_b, b_ln_g, b_ln_b)


def _mixer_ab_bwd_pre(z, cb, dy, a_ln_g, a_ln_b, w_s, b_s, b_ln_g, b_ln_b, *, tm, name, comm=None):
    T = z.shape[0]
    nchunk = tm // CHUNK
    tn_dims = (((0,), (0,)), ((), ()))
    nt_dims = (((1,), (1,)), ((), ()))

    def body(za_ref, cb_ref, dy_ref, alg_ref, alb_ref, ws_ref, bs_ref, blg_ref, blb_ref,
             dza_ref, dcb_ref, dalg_ref, dalb_ref, dws_ref, dbs_ref, dblg_ref, dblb_ref,
             dlv_ref):
        @pl.when(pl.program_id(0) == 0)
        def _():
            for ref in (dalg_ref, dalb_ref, dws_ref, dbs_ref, dblg_ref, dblb_ref):
                ref[...] = jnp.zeros_like(ref)
        ua = za_ref[:, :D_A].astype(F32)
        va = za_ref[:, D_A:].astype(F32)
        gu = _gelu(ua)
        gv = _gelu(va)
        xh, r = _ln_stats(gv)
        alg = alg_ref[...]
        lv = (xh * alg + alb_ref[...]).astype(BF16)
        dya = dy_ref[:, :D_A].astype(F32)
        mask = _causal_mask()
        for h in range(A_HEADS):
            wm = jnp.where(mask, ws_ref[h], 0.0).astype(BF16)
            cols = slice(h * HEAD_DIM, (h + 1) * HEAD_DIM)
            dwm = jnp.zeros((CHUNK, CHUNK), F32)
            dbs = jnp.zeros((CHUNK, 1), F32)
            for c in range(nchunk):
                rows = slice(c * CHUNK, (c + 1) * CHUNK)
                lvb = lv[rows, cols]
                mixed = jnp.dot(wm, lvb, preferred_element_type=F32) + bs_ref[h]
                dyb = dya[rows, cols]
                dza_ref[rows, cols] = (dyb * mixed * _dgelu(ua[rows, cols])).astype(BF16)
                dmixed = dyb * gu[rows, cols]
                dmb = dmixed.astype(BF16)
                dlv_ref[rows, cols] = lax.dot_general(wm, dmb, tn_dims, preferred_element_type=F32)
                dwm = dwm + lax.dot_general(dmb, lvb, nt_dims, preferred_element_type=F32)
                dbs = dbs + jnp.sum(dmixed, axis=1, keepdims=True)
            dws_ref[h] += jnp.where(mask, dwm, 0.0)
            dbs_ref[h] += dbs
        dlv = dlv_ref[...]
        dalg_ref[...] += _rowsum(dlv * xh)
        dalb_ref[...] += _rowsum(dlv)
        dgv = _ln_bwd(dlv, xh, r, alg)
        dza_ref[:, D_A:] = (dgv * _dgelu(va)).astype(BF16)
        xhb, rb = _ln_stats(cb_ref[...])
        blg = blg_ref[...]
        lb = xhb * blg + blb_ref[...]
        dlb = dy_ref[:, D_A:].astype(F32) * _dsilu(lb)
        dblg_ref[...] += _rowsum(dlb * xhb)
        dblb_ref[...] += _rowsum(dlb)
        dcb_ref[...] = _ln_bwd(dlb, xhb, rb, blg)

    row = lambda i: (i, 0)
    par = lambda i: (0, 0)
    par3 = lambda i: (0, 0, 0)
    return _pallas(
        comm, body, name=name, grid=(T // tm,),
        in_specs=[pl.BlockSpec((tm, 2 * D_A), row), pl.BlockSpec((tm, D_B), row),
                  pl.BlockSpec((tm, D_A + D_B), row),
                  pl.BlockSpec((1, D_A), par), pl.BlockSpec((1, D_A), par),
                  pl.BlockSpec((A_HEADS, CHUNK, CHUNK), par3),
                  pl.BlockSpec((A_HEADS, CHUNK, 1), par3),
                  pl.BlockSpec((1, D_B), par), pl.BlockSpec((1, D_B), par)],
        out_specs=[pl.BlockSpec((tm, 2 * D_A), row), pl.BlockSpec((tm, D_B), row),
                   pl.BlockSpec((1, D_A), par), pl.BlockSpec((1, D_A), par),
                   pl.BlockSpec((A_HEADS, CHUNK, CHUNK), par3),
                   pl.BlockSpec((A_HEADS, CHUNK, 1), par3),
                   pl.BlockSpec((1, D_B), par), pl.BlockSpec((1, D_B), par)],
        out_shape=[jax.ShapeDtypeStruct((T, 2 * D_A + 2 * D_B), BF16), jax.ShapeDtypeStruct((T, D_B), F32),
                   jax.ShapeDtypeStruct((1, D_A), F32), jax.ShapeDtypeStruct((1, D_A), F32),
                   jax.ShapeDtypeStruct((A_HEADS, CHUNK, CHUNK), F32),
                   jax.ShapeDtypeStruct((A_HEADS, CHUNK, 1), F32),
                   jax.ShapeDtypeStruct((1, D_B), F32), jax.ShapeDtypeStruct((1, D_B), F32)],
        scratch_shapes=[pltpu.VMEM((tm, D_A), F32)],
        compiler_params=_cparams(("arbitrary",), 40),
    )(z, cb, dy, a_ln_g, a_ln_b, w_s, b_s, b_ln_g, b_ln_b)


def _mixer_b_conv_bwd(z, dcb, conv_w, dz, *, tm, name, comm=None):
    T = z.shape[0]
    halo = HALO_LONG

    def body(zb_ref, dcb_ref, dcn_ref, cw_ref, dz_in_ref, dzb_ref, dcw_ref, dbias_ref, dext_ref):
        i = pl.program_id(0)
        last = pl.num_programs(0) - 1

        @pl.when(i == 0)
        def _():
            dcw_ref[...] = jnp.zeros_like(dcw_ref)
            dbias_ref[...] = jnp.zeros_like(dbias_ref)
        dcb = dcb_ref[...]
        dext_ref[0:tm, :] = dcb
        dext_ref[tm:tm + halo, :] = jnp.where(i < last, dcn_ref[...], 0.0)
        dbias_ref[...] += _rowsum(dcb)
        for rb in range(tm // CONV_ROWS):
            for cb in range(D_B // CONV_COLS):
                cs = slice(cb * CONV_COLS, (cb + 1) * CONV_COLS)
                gcs = slice(D_B + cb * CONV_COLS, D_B + (cb + 1) * CONV_COLS)
                rs = slice(rb * CONV_ROWS, (rb + 1) * CONV_ROWS)
                xbb = zb_ref[rs, cs].astype(F32)
                sgb = _sigmoid(zb_ref[rs, gcs].astype(F32))
                yb0 = xbb * sgb
                window = dext_ref[rb * CONV_ROWS:rb * CONV_ROWS + CONV_ROWS + halo, cs]
                acc = jnp.zeros((CONV_ROWS, CONV_COLS), F32)
                for k in range(B_CONV):
                    shifted = _rows_after(window, (B_CONV - 1) - k)[:CONV_ROWS]
                    acc = acc + cw_ref[k:k + 1, cs] * shifted
                    dcw_ref[k:k + 1, cs] += _rowsum(shifted * yb0)
                dzb_ref[rs, cs] = (acc * sgb).astype(BF16)
                dzb_ref[rs, gcs] = (acc * xbb * sgb * (1.0 - sgb)).astype(BF16)

    row = lambda i: (i, 0)
    par = lambda i: (0, 0)
    return _pallas(
        comm, body, name=name, grid=(T // tm,),
        in_specs=[pl.BlockSpec((tm, 2 * D_B), lambda i: (i, 1)),
                  pl.BlockSpec((tm, D_B), row),
                  pl.BlockSpec((halo, D_B), lambda i: (_halo_next_index(tm, halo, T)(i), 0)),
                  pl.BlockSpec((B_CONV, D_B), par), pl.BlockSpec(memory_space=pl.ANY)],
        out_specs=[pl.BlockSpec((tm, 2 * D_B), lambda i: (i, 1)), pl.BlockSpec((B_CONV, D_B), par),
                   pl.BlockSpec((1, D_B), par)],
        out_shape=[jax.ShapeDtypeStruct(dz.shape, BF16), jax.ShapeDtypeStruct((B_CONV, D_B), F32),
                   jax.ShapeDtypeStruct((1, D_B), F32)],
        scratch_shapes=[pltpu.VMEM((tm + halo, D_B), F32)], aliases={4: 0},
        compiler_params=_cparams(("arbitrary",), 40),
    )(z, dcb, dcb, conv_w, dz)


def _rows_before(x, a):
    return x if a == 0 else pltpu.roll(x, a, axis=0)


def _rows_after(x, a):
    return x if a == 0 else pltpu.roll(x, x.shape[0] - a, axis=0)


def _conv3(w_ref, x, halo, cs):
    acc = w_ref[2:3, cs] * x[halo:]
    acc = acc + w_ref[1:2, cs] * _rows_before(x, 1)[halo:]
    return acc + w_ref[0:1, cs] * _rows_before(x, 2)[halo:]


def _mixer_c_fwd(z, conv_w, *, tm, name, comm=None):
    T = z.shape[0]
    D = D_MODEL
    halo = HALO_SHORT
    W = CONV_COLS

    def body(bg_ref, cg_ref, xv_ref, cgh_ref, xvh_ref, w_ref, r_ref):
        i = pl.program_id(0)
        for cb in range(D // W):
            cs = slice(cb * W, (cb + 1) * W)
            prev = jnp.where(i > 0, cgh_ref[:, cs].astype(F32) * xvh_ref[:, cs].astype(F32), 0.0)
            p = jnp.concatenate([prev, cg_ref[:, cs].astype(F32) * xv_ref[:, cs].astype(F32)], axis=0)
            r_ref[:, cs] = (bg_ref[:, cs].astype(F32) * _conv3(w_ref, p, halo, cs)).astype(BF16)

    hp = _halo_prev_index(tm, halo)
    return _pallas(
        comm, body, name=name, grid=(T // tm,),
        in_specs=[pl.BlockSpec((tm, D), lambda i: (i, 0)), pl.BlockSpec((tm, D), lambda i: (i, 1)),
                  pl.BlockSpec((tm, D), lambda i: (i, 2)),
                  pl.BlockSpec((halo, D), lambda i: (hp(i), 1)),
                  pl.BlockSpec((halo, D), lambda i: (hp(i), 2)),
                  pl.BlockSpec((None, C_CONV, D), lambda i: (0, 0, 0))],
        out_specs=pl.BlockSpec((tm, D), lambda i: (i, 0)),
        out_shape=jax.ShapeDtypeStruct((T, D), BF16),
        compiler_params=_cparams(("parallel",), 40),
    )(z, z, z, z, z, conv_w)


def _mixer_c_bwd(z, dr, conv_w, *, tm, name, comm=None):
    T = z.shape[0]
    D = D_MODEL
    halo = HALO_SHORT
    W = CONV_COLS

    def body(bg_ref, cg_ref, xv_ref, cgh_ref, xvh_ref, bgn_ref, dr_ref, drn_ref, w_ref, dz_ref, dw_ref):
        i = pl.program_id(0)
        last = pl.num_programs(0) - 1

        @pl.when(i == 0)
        def _():
            dw_ref[...] = jnp.zeros_like(dw_ref)
        for cb in range(D // W):
            cs = slice(cb * W, (cb + 1) * W)
            cg = cg_ref[:, cs].astype(F32)
            xv = xv_ref[:, cs].astype(F32)
            dr = dr_ref[:, cs].astype(F32)
            p = cg * xv
            prev = jnp.where(i > 0, cgh_ref[:, cs].astype(F32) * xvh_ref[:, cs].astype(F32), 0.0)
            q = _conv3(w_ref, jnp.concatenate([prev, p], axis=0), halo, cs)
            dz_ref[:, cs] = (dr * q).astype(BF16)
            nxt = jnp.where(i < last, drn_ref[:, cs].astype(F32) * bgn_ref[:, cs].astype(F32), 0.0)
            dq = jnp.concatenate([dr * bg_ref[:, cs].astype(F32), nxt], axis=0)
            dp = None
            for k in range(C_CONV):
                shifted = _rows_after(dq, 2 - k)[:tm]
                term = w_ref[k:k + 1, cs] * shifted
                dp = term if dp is None else dp + term
                dw_ref[k:k + 1, cs] += _rowsum(shifted * p)
            dz_ref[:, D + cb * W:D + (cb + 1) * W] = (dp * xv).astype(BF16)
            dz_ref[:, 2 * D + cb * W:2 * D + (cb + 1) * W] = (dp * cg).astype(BF16)

    hp = _halo_prev_index(tm, halo)
    hn = _halo_next_index(tm, halo, T)
    return _pallas(
        comm, body, name=name, grid=(T // tm,),
        in_specs=[pl.BlockSpec((tm, D), lambda i: (i, 0)), pl.BlockSpec((tm, D), lambda i: (i, 1)),
                  pl.BlockSpec((tm, D), lambda i: (i, 2)),
                  pl.BlockSpec((halo, D), lambda i: (hp(i), 1)),
                  pl.BlockSpec((halo, D), lambda i: (hp(i), 2)),
                  pl.BlockSpec((halo, D), lambda i: (hn(i), 0)),
                  pl.BlockSpec((tm, D), lambda i: (i, 0)),
                  pl.BlockSpec((halo, D), lambda i: (hn(i), 0)),
                  pl.BlockSpec((None, C_CONV, D), lambda i: (0, 0, 0))],
        out_specs=[pl.BlockSpec((tm, 3 * D), lambda i: (i, 0)),
                   pl.BlockSpec((C_CONV, D), lambda i: (0, 0))],
        out_shape=[jax.ShapeDtypeStruct((T, 3 * D), BF16), jax.ShapeDtypeStruct((C_CONV, D), F32)],
        compiler_params=_cparams(("arbitrary",), 48),
    )(z, z, z, z, z, z, dr, dr, conv_w)


FFN_COLS = 128


def _ffn_act_fwd(up, conv_w, *, layer, tm, name, comm=None):
    T = up.shape[0]
    halo = HALO_SHORT
    W = FFN_COLS

    def body(up_ref, uph_ref, w_ref, a_ref, upc_ref):
        i = pl.program_id(0)
        def conv(cs):
            prev = jnp.where(i > 0, uph_ref[:, cs], jnp.zeros((halo, W), BF16))
            return _conv3(w_ref, jnp.concatenate([prev, up_ref[:, cs]], axis=0).astype(F32), halo, cs)

        for cb in range(D_FF // W):
            gs = slice(cb * W, (cb + 1) * W)
            vs = slice(D_FF + cb * W, D_FF + (cb + 1) * W)
            g = conv(gs)
            v = conv(vs)
            upc_ref[:, gs] = g.astype(BF16)
            upc_ref[:, vs] = v.astype(BF16)
            a_ref[:, gs] = (_silu(g) * v).astype(BF16)

    return _pallas(
        comm, body, name=name, grid=(T // tm,),
        in_specs=[pl.BlockSpec((tm, 2 * D_FF), lambda i: (i, 0)),
                  pl.BlockSpec((halo, 2 * D_FF), lambda i: (_halo_prev_index(tm, halo)(i), 0)),
                  pl.BlockSpec((None, F_CONV, 2 * D_FF), lambda i: (layer, 0, 0))],
        out_specs=[pl.BlockSpec((tm, D_FF), lambda i: (i, 0)),
                   pl.BlockSpec((tm, 2 * D_FF), lambda i: (i, 0))],
        out_shape=[jax.ShapeDtypeStruct((T, D_FF), BF16), jax.ShapeDtypeStruct((T, 2 * D_FF), BF16)],
        compiler_params=_cparams(("parallel",), 48),
    )(up, up, conv_w)


def _ffn_act_bwd(up, upc, da, conv_w, *, layer, tm, name, comm=None):
    T = up.shape[0]
    halo = HALO_SHORT
    W = FFN_COLS

    def body(up_ref, upc_ref, upcn_ref, da_ref, dan_ref, w_ref, dup_ref, dw_ref):
        i = pl.program_id(0)
        last = pl.num_programs(0) - 1

        @pl.when(i == 0)
        def _():
            dw_ref[...] = jnp.zeros_like(dw_ref)
        live = jnp.where(i < last, 1.0, 0.0)
        for cb in range(D_FF // W):
            gs = slice(cb * W, (cb + 1) * W)
            vs = slice(D_FF + cb * W, D_FF + (cb + 1) * W)
            g = jnp.concatenate([upc_ref[:, gs], upcn_ref[:, gs]], axis=0).astype(F32)
            v = jnp.concatenate([upc_ref[:, vs], upcn_ref[:, vs]], axis=0).astype(F32)
            da = jnp.concatenate([da_ref[:, gs].astype(F32), dan_ref[:, gs].astype(F32) * live], axis=0)
            s = _sigmoid(g)
            silu = g * s
            grads = (da * v * (s * (1.0 + g * (1.0 - s))), da * silu)
            for cs, d in zip((gs, vs), grads):
                u = up_ref[:, cs].astype(F32)
                acc = None
                for k in range(F_CONV):
                    shifted = _rows_after(d, 2 - k)[:tm]
                    term = w_ref[k:k + 1, cs] * shifted
                    acc = term if acc is None else acc + term
                    dw_ref[k:k + 1, cs] += _rowsum(shifted * u)
                dup_ref[:, cs] = acc.astype(BF16)

    hn = _halo_next_index(tm, halo, T)
    return _pallas(
        comm, body, name=name, grid=(T // tm,),
        in_specs=[pl.BlockSpec((tm, 2 * D_FF), lambda i: (i, 0)),
                  pl.BlockSpec((tm, 2 * D_FF), lambda i: (i, 0)),
                  pl.BlockSpec((halo, 2 * D_FF), lambda i: (hn(i), 0)),
                  pl.BlockSpec((tm, D_FF), lambda i: (i, 0)),
                  pl.BlockSpec((halo, D_FF), lambda i: (hn(i), 0)),
                  pl.BlockSpec((None, F_CONV, 2 * D_FF), lambda i: (layer, 0, 0))],
        out_specs=[pl.BlockSpec((tm, 2 * D_FF), lambda i: (i, 0)),
                   pl.BlockSpec((F_CONV, 2 * D_FF), lambda i: (0, 0))],
        out_shape=[jax.ShapeDtypeStruct((T, 2 * D_FF), BF16),
                   jax.ShapeDtypeStruct((F_CONV, 2 * D_FF), F32)],
        compiler_params=_cparams(("arbitrary",), 56),
    )(up, upc, upc, da, da, conv_w)


def _local_step(x, tgt, small, plan):
    T = x.shape[0]
    tm_e = _pick(T, 256)
    tm_n = _pick(T, 512)
    tm = _pick(T, 1024)
    tm_r = _pick(T, 2048)
    tt = _pick(T, 512)
    nm = small["norm_mix"].reshape(2, 1, D_MODEL)
    nf = small["norm_ffn"].reshape(2, 1, D_MODEL)
    ngf = small["norm_final"].reshape(1, D_MODEL)
    b_s = small["a_b_s"].reshape(A_HEADS, CHUNK, 1)
    w_s = small["a_w_s"].reshape(A_HEADS, CHUNK, CHUNK)
    b_conv_w = small["b_conv_w"].reshape(B_CONV, D_B)
    sg = {}
    wt, cm = plan.weight, plan.comm

    h_m0 = _rmsnorm_fwd(x, nm, layer=0, tm=tm_n, name="norm_mix0")
    z_ab = _mm_nn(h_m0, wt("ab_w_in", 0), layer=0, tm=tm, tn=512, out_dtype=BF16, name="ab_in", comm=cm("ab_in"))
    yab, cb = _mixer_ab_fwd(z_ab, small["a_ln_g"], small["a_ln_b"], w_s, b_s, b_conv_w, small["b_conv_b"],
                            small["b_ln_g"], small["b_ln_b"], tm=tm_e, name="mixer_ab", comm=cm("mixer_ab"))
    x1, h_f0 = _mm_nn(yab, wt("ab_w_out", 0), layer=0, tm=tm, tn=D_MODEL, residual=x, norm=(nf, 0),
                      name="ab_out", comm=cm("ab_out"))

    def ffn_fwd(xin, h, layer, norm):
        up = _mm_nn(h, wt("f_w_up", layer), layer=0, tm=tm, tn=1408, out_dtype=BF16, name=f"ffn_up{layer}",
                    comm=cm(f"ffn_up{layer}"))
        a, upc = _ffn_act_fwd(up, small["f_conv_w"], layer=layer, tm=tm_e, name=f"ffn_act{layer}",
                              comm=cm(f"ffn_act{layer}"))
        out = _mm_nn(a, wt("f_w_down", layer), layer=0, tm=tm, tn=D_MODEL, residual=xin, norm=norm,
                     name=f"ffn_down{layer}", comm=cm(f"ffn_down{layer}"))
        return up, upc, a, out

    up0, upc0, a0, (x2, h_m1) = ffn_fwd(x1, h_f0, 0, (nm, 1))
    z_c = _mm_nn(h_m1, wt("c_w_in", 0), layer=0, tm=tm, tn=768, out_dtype=BF16, name="c_in", comm=cm("c_in"))
    r = _mixer_c_fwd(z_c, small["c_conv_w"], tm=tm_e, name="mixer_c", comm=cm("mixer_c"))
    x3, h_f1 = _mm_nn(r, wt("c_w_out", 0), layer=0, tm=tm, tn=D_MODEL, residual=x2, norm=(nf, 1),
                      name="c_out", comm=cm("c_out"))
    up1, upc1, a1, x4 = ffn_fwd(x3, h_f1, 1, None)
    loss, dx, sg["norm_final"] = _loss_head(x4, tgt, ngf, tm=tm_n, name="loss_head")

    def ffn_bwd(dx, xin, h, up, upc, a, layer):
        da = _mm_nt(dx, wt("f_w_down", layer), layer=0, tm=tm, tn=1408, out_dtype=BF16,
                    name=f"ffn_down_dx{layer}", comm=cm(f"ffn_down_dx{layer}"))
        plan.grad_ready("f_w_down", layer, _mm_tn(a, dx, shards=None, tk=1408, tn=1024, tt=tt,
                                                  name=f"ffn_down_dw{layer}", comm=cm(f"ffn_down_dw{layer}")))
        dup, dcw = _ffn_act_bwd(up, upc, da, small["f_conv_w"], layer=layer, tm=tm_e, name=f"ffn_act_bwd{layer}",
                                comm=cm(f"ffn_act_bwd{layer}"))
        dh = _mm_nt(dup, wt("f_w_up", layer), layer=0, tm=tm_r, tn=None, name=f"ffn_up_dx{layer}",
                    comm=cm(f"ffn_up_dx{layer}"))
        plan.grad_ready("f_w_up", layer, _mm_tn(h, dup, shards=N_CHIPS, tk=1024, tn=1408, tt=tt,
                                                name=f"ffn_up_dw{layer}", comm=cm(f"ffn_up_dw{layer}")))
        dxin, dg = _rmsnorm_bwd(xin, nf, dh, dx, layer=layer, tm=tm_n, name=f"norm_ffn_bwd{layer}",
                                comm=cm(f"norm_ffn_bwd{layer}"))
        return dxin, dg, dcw

    dx, dnf1, dfc1 = ffn_bwd(dx, x3, h_f1, up1, upc1, a1, 1)
    dr = _mm_nt(dx, wt("c_w_out", 0), layer=0, tm=tm, tn=512, out_dtype=BF16, name="c_out_dx", comm=cm("c_out_dx"))
    plan.grad_ready("c_w_out", 0, _mm_tn(r, dx, shards=None, tk=1024, tn=1024, tt=tt, name="c_out_dw",
                                         comm=cm("c_out_dw")))
    dz_c, dccw = _mixer_c_bwd(z_c, dr, small["c_conv_w"], tm=tm_e, name="mixer_c_bwd", comm=cm("mixer_c_bwd"))
    sg["c_conv_w"] = dccw.reshape(1, C_CONV, D_MODEL)
    dh = _mm_nt(dz_c, wt("c_w_in", 0), layer=0, tm=tm_r, tn=None, name="c_in_dx", comm=cm("c_in_dx"))
    plan.grad_ready("c_w_in", 0, _mm_tn(h_m1, dz_c, shards=N_CHIPS, tk=1024, tn=768, tt=tt, name="c_in_dw",
                                        comm=cm("c_in_dw")))
    dx, dnm1 = _rmsnorm_bwd(x2, nm, dh, dx, layer=1, tm=tm_n, name="norm_mix_bwd1", comm=cm("norm_mix_bwd1"))
    dx, dnf0, dfc0 = ffn_bwd(dx, x1, h_f0, up0, upc0, a0, 0)
    dyab = _mm_nt(dx, wt("ab_w_out", 0), layer=0, tm=tm, tn=512, out_dtype=BF16, name="ab_out_dx",
                  comm=cm("ab_out_dx"))
    plan.grad_ready("ab_w_out", 0, _mm_tn(yab, dx, shards=None, tk=1024, tn=1024, tt=tt, name="ab_out_dw",
                                          comm=cm("ab_out_dw")))
    (dza, dcb, sg["a_ln_g"], sg["a_ln_b"], dws, dbs, sg["b_ln_g"], sg["b_ln_b"]) = _mixer_ab_bwd_pre(
        z_ab, cb, dyab, small["a_ln_g"], small["a_ln_b"], w_s, b_s, small["b_ln_g"], small["b_ln_b"],
        tm=tm_e, name="mixer_ab_bwd", comm=cm("mixer_ab_bwd"))
    dz_ab, dbcw, sg["b_conv_b"] = _mixer_b_conv_bwd(z_ab, dcb, b_conv_w, dza, tm=tm_e, name="mixer_b_conv_bwd",
                                                    comm=cm("mixer_b_conv_bwd"))
    sg["a_w_s"] = dws.reshape(1, A_HEADS, CHUNK, CHUNK)
    sg["a_b_s"] = dbs.reshape(1, A_HEADS, CHUNK)
    sg["b_conv_w"] = dbcw.reshape(1, B_CONV, D_B)
    plan.grad_ready("ab_w_in", 0, _mm_tn(h_m0, dz_ab, shards=N_CHIPS, tk=1024, tn=512, tt=tt, name="ab_in_dw",
                                         comm=cm("ab_in_dw")))
    dh = _mm_nt(dz_ab, wt("ab_w_in", 0), layer=0, tm=tm_r, tn=None, name="ab_in_dx", comm=cm("ab_in_dx"))
    dx, dnm0 = _rmsnorm_bwd(x, nm, dh, dx, layer=0, tm=tm_n, name="norm_mix_bwd0", comm=cm("norm_mix_bwd0"))

    sg["norm_mix"] = [dnm0, dnm1]
    sg["norm_ffn"] = [dnf0, dnf1]
    sg["f_conv_w"] = [dfc0, dfc1]
    return loss, dx, sg


BLOCK_BYTES = 3 * 1024 * 1024


BF16_SUBLANES = 16


def _row_tile(rows, row_bytes, step=SUBLANES):
    best = None
    for tr in range(step, rows + 1, step):
        if rows % tr == 0 and tr * row_bytes <= BLOCK_BYTES:
            best = tr
    if best is None:
        raise ValueError(f"no row tile for {rows}")
    return best


def _place_scalars():
    x, y, c = lax.axis_index("x"), lax.axis_index("y"), lax.axis_index("c")
    return jnp.stack([c, 2 * x + y, 2 * (1 - x) + y, 2 * x + (1 - y), 2 * (1 - x) + (1 - y)]).astype(jnp.int32)


def _cast_into_slot(w, place, *, layer, name):
    L, rows, cols = w.shape
    tr = _row_tile(rows, cols * 4, BF16_SUBLANES)

    def body(place_ref, w_ref, o_ref):
        o_ref[...] = w_ref[...].astype(BF16)

    return pl.pallas_call(
        body, name=name,
        grid_spec=pltpu.PrefetchScalarGridSpec(
            num_scalar_prefetch=1, grid=(rows // tr,),
            in_specs=[pl.BlockSpec((None, tr, cols), lambda i, p: (layer, i, 0))],
            out_specs=pl.BlockSpec((None, None, tr, cols), lambda i, p: (0, p[1], i, 0))),
        out_shape=jax.ShapeDtypeStruct((1, N_CHIPS, rows, cols), BF16),
        compiler_params=_cparams(("parallel",), 32),
    )(place, w)


def _pair_sum(g, theirs, place, *, name):
    S, rows, cols = g.shape
    half = rows // 2
    tr = _row_tile(half, cols * 4, BF16_SUBLANES)
    nb = half // tr

    def body(place_ref, g_ref, t_ref, o_ref):
        o_ref[...] = (g_ref[...] + t_ref[...]).astype(BF16)

    spec = pl.BlockSpec((None, tr, cols), lambda s, i, p: (s, i, 0))
    return pl.pallas_call(
        body, name=name,
        grid_spec=pltpu.PrefetchScalarGridSpec(
            num_scalar_prefetch=1, grid=(S, nb),
            in_specs=[pl.BlockSpec((None, tr, cols), lambda s, i, p: (s, p[0] * nb + i, 0)), spec],
            out_specs=spec),
        out_shape=jax.ShapeDtypeStruct((S, half, cols), BF16),
        compiler_params=_cparams(("parallel", "parallel"), 32),
    )(place, g, theirs)


def _chip_sum(p, r, g_prev, place, *, layer, shape, name):
    L, rows, cols = shape
    half = rows // 2
    tr = _row_tile(half, cols * 4, BF16_SUBLANES)
    nb = half // tr

    def body(place_ref, p_ref, r_ref, *rest):
        o_ref = rest[-1]
        mine = p_ref[...].astype(F32)
        peers = [r_ref[j].astype(F32) for j in range(3)]
        acc = None
        for s in range(N_CHIPS):
            term = jnp.where(place_ref[1] == s, mine,
                             jnp.where(place_ref[2] == s, peers[0],
                                       jnp.where(place_ref[3] == s, peers[1], peers[2])))
            acc = term if acc is None else acc + term
        o_ref[...] = acc

    in_specs = [pl.BlockSpec((None, tr, cols), lambda i, pr: (pr[1], i, 0)),
                pl.BlockSpec((3, tr, cols), lambda i, pr: (0, i, 0))]
    args = [place, p, r]
    aliases = {}
    if g_prev is not None:
        in_specs.append(ANY)
        args.append(g_prev)
        aliases = {3: 0}
    return pl.pallas_call(
        body, name=name,
        grid_spec=pltpu.PrefetchScalarGridSpec(
            num_scalar_prefetch=1, grid=(nb,), in_specs=in_specs,
            out_specs=pl.BlockSpec((None, tr, cols), lambda i, pr: (layer, pr[0] * nb + i, 0))),
        out_shape=jax.ShapeDtypeStruct(shape, F32), input_output_aliases=aliases,
        compiler_params=_cparams(("parallel",), 32),
    )(*args)


def _adamw_math(w, g, m, v):
    m2 = ADAM_B1 * m + (1.0 - ADAM_B1) * g
    v2 = ADAM_B2 * v + (1.0 - ADAM_B2) * (g * g)
    m_hat = m2 / (1.0 - ADAM_B1 ** ADAM_STEP)
    v_hat = v2 / (1.0 - ADAM_B2 ** ADAM_STEP)
    delta = -ADAM_LR * (m_hat / (jnp.sqrt(v_hat) + ADAM_EPS) + ADAM_WD * w)
    return delta, m2, v2


def _adamw(w, g, m, v, *, name):
    L, rows, cols = w.shape
    tr = _row_tile(rows, cols * 4)

    def body(w_ref, g_ref, m_ref, v_ref, d_ref, m2_ref, v2_ref):
        d, m2, v2 = _adamw_math(w_ref[...], g_ref[...], m_ref[...], v_ref[...])
        d_ref[...] = d
        m2_ref[...] = m2
        v2_ref[...] = v2

    spec = pl.BlockSpec((None, tr, cols), lambda l, i: (l, i, 0))
    shape = jax.ShapeDtypeStruct(w.shape, F32)
    return pl.pallas_call(
        body, name=name, grid=(L, rows // tr), in_specs=[spec] * 4, out_specs=[spec] * 3,
        out_shape=[shape] * 3,
        compiler_params=_cparams(("parallel", "parallel"), 48),
    )(w, g, m, v)


def _exchange_packs(pack, *, reduce, name):
    R = pack.shape[0]
    ndev = 2 * N_CHIPS

    def body(p_ref, o_ref, *scratch):
        if reduce:
            buf, send, recv = scratch
        else:
            buf = o_ref
            send, recv = scratch
        x, y, c = lax.axis_index("x"), lax.axis_index("y"), lax.axis_index("c")
        me = 4 * x + 2 * y + c
        buf[me] = p_ref[...]
        sends = []
        for q in range(1, ndev):
            qx, qy, qc = (q >> 2) & 1, (q >> 1) & 1, q & 1
            peer = (x ^ qx, y ^ qy, c ^ qc)
            rc = _remote(p_ref, buf.at[me], send.at[q - 1], recv.at[q - 1], peer)
            rc.start()
            sends.append(rc)
        for q in range(1, ndev):
            qx, qy, qc = (q >> 2) & 1, (q >> 1) & 1, q & 1
            slot = buf.at[4 * (x ^ qx) + 2 * (y ^ qy) + (c ^ qc)]
            _remote(slot, slot, send.at[q - 1], recv.at[q - 1], (x ^ qx, y ^ qy, c ^ qc)).wait_recv()
        for rc in sends:
            rc.wait_send()
        if reduce:
            acc = buf[0]
            for d in range(1, ndev):
                acc = acc + buf[d]
            o_ref[...] = acc

    vm = pl.BlockSpec(memory_space=pltpu.VMEM)
    sems = [pltpu.SemaphoreType.DMA((ndev - 1,)), pltpu.SemaphoreType.DMA((ndev - 1,))]
    if reduce:
        out_shape = jax.ShapeDtypeStruct((R, LANES), F32)
        scratch = [pltpu.VMEM((ndev, R, LANES), F32)] + sems
    else:
        out_shape = jax.ShapeDtypeStruct((ndev, R, LANES), F32)
        scratch = sems
    return pl.pallas_call(
        body, name=name, in_specs=[vm], out_specs=vm, out_shape=out_shape, scratch_shapes=scratch,
        compiler_params=pltpu.CompilerParams(vmem_limit_bytes=VMEM_BYTES_MAX),
    )(pack)


PACK_UNIT = SUBLANES * LANES


def _pack(arrays):
    flat, sizes = [], []
    for a in arrays:
        pieces = a if isinstance(a, (list, tuple)) else [a]
        v = jnp.concatenate([p.reshape(-1) for p in pieces]) if len(pieces) > 1 else pieces[0].reshape(-1)
        size = v.shape[0]
        padded = -(-size // PACK_UNIT) * PACK_UNIT
        flat.append(jnp.pad(v, (0, padded - size)))
        sizes.append((size, padded))
    return jnp.concatenate(flat).reshape(-1, LANES), sizes


def _unpack(pack, sizes, shapes):
    v = pack.reshape(-1)
    out, off = [], 0
    for (size, padded), shape in zip(sizes, shapes):
        out.append(v[off:off + size].reshape(shape))
        off += padded
    return out


BIG = ("ab_w_in", "ab_w_out", "c_w_in", "c_w_out", "f_w_up", "f_w_down")
COL_SHARDED = ("ab_w_in", "c_w_in", "f_w_up")
SMALL_REPLICATED = ("norm_mix", "norm_ffn", "norm_final", "a_ln_g", "a_ln_b", "a_w_s", "a_b_s",
                    "b_conv_b", "b_ln_g", "b_ln_b")
SMALL_SHARDED = ("b_conv_w", "c_conv_w", "f_conv_w")
SMALL = SMALL_REPLICATED + SMALL_SHARDED
ALL_WEIGHTS = ("norm_mix", "norm_ffn", "norm_final", "ab_w_in", "a_ln_g", "a_ln_b", "a_w_s", "a_b_s",
               "b_conv_w", "b_conv_b", "b_ln_g", "b_ln_b", "ab_w_out", "c_w_in", "c_conv_w", "c_w_out",
               "f_w_up", "f_conv_w", "f_w_down")


SCHEDULE = {
    "ab_in": [("gi", "f_w_up", 0, 0, 4), ("gi", "ab_w_out", 0)],
    "mixer_ab": [("gd", "f_w_up", 0, 0, 4), ("gd", "ab_w_out", 0), ("gi", "f_w_up", 0, 1, 4),
                 ("gi", "f_w_up", 0, 2, 4), ("gi", "f_w_up", 0, 3, 4)],
    "ab_out": [("gd", "f_w_up", 0, 1, 4), ("gd", "f_w_up", 0, 2, 4), ("gd", "f_w_up", 0, 3, 4)],
    "ffn_up0": [("gi", "f_w_down", 0), ("gi", "c_w_in", 0, 0, 2)],
    "ffn_act0": [("gd", "f_w_down", 0), ("gd", "c_w_in", 0, 0, 2), ("gi", "c_w_in", 0, 1, 2),
                 ("gi", "f_w_up", 1, 0, 4), ("gi", "f_w_up", 1, 1, 4)],
    "ffn_down0": [("gd", "c_w_in", 0, 1, 2), ("gd", "f_w_up", 1, 0, 4), ("gd", "f_w_up", 1, 1, 4),
                  ("gi", "f_w_up", 1, 2, 4)],
    "c_in": [("gd", "f_w_up", 1, 2, 4), ("gi", "f_w_up", 1, 3, 4), ("gi", "c_w_out", 0)],
    "mixer_c": [("gd", "f_w_up", 1, 3, 4), ("gd", "c_w_out", 0), ("gi", "f_w_down", 1, 0, 2)],
    "c_out": [("gd", "f_w_down", 1, 0, 2), ("gi", "f_w_down", 1, 1, 2)],
    "ffn_up1": [("gd", "f_w_down", 1, 1, 2)],
    "ffn_act_bwd1": [("px", "f_w_down", 1)],
    "ffn_up_dx1": [("cx", "f_w_down", 1)],
    "mixer_c_bwd": [("px", "f_w_up", 1), ("px", "c_w_out", 0)],
    "c_in_dx": [("cx", "f_w_up", 1, 0, 2)],
    "c_in_dw": [("cx", "f_w_up", 1, 1, 2), ("cx", "c_w_out", 0)],
    "norm_mix_bwd1": [("px", "c_w_in", 0)],
    "ffn_down_dx0": [("cx", "c_w_in", 0, 0, 2)],
    "ffn_down_dw0": [("cx", "c_w_in", 0, 1, 2)],
    "ffn_act_bwd0": [("px", "f_w_down", 0)],
    "ffn_up_dx0": [("cx", "f_w_down", 0)],
    "mixer_ab_bwd": [("px", "f_w_up", 0), ("px", "ab_w_out", 0)],
    "mixer_b_conv_bwd": [("cx", "f_w_up", 0, 0, 2), ("cx", "ab_w_out", 0)],
    "ab_in_dw": [("cx", "f_w_up", 0, 1, 2)],
    "ab_in_dx": [("px", "ab_w_in", 0)],
    "norm_mix_bwd0": [("cx", "ab_w_in", 0)],
}


class _Plan:
    def __init__(self, shapes, place):
        self.shapes, self.place, self.bufs = shapes, place, {}

    def weight(self, name, layer):
        g = self.bufs[f"w:{name}:{layer}"]
        if name in COL_SHARDED:
            return g
        _, S, rows, cols = g.shape
        return g.reshape(1, S * rows, cols)

    def grad_ready(self, name, layer, g):
        _, rows, cols = self.shapes[name]
        hbm = lambda a: pltpu.with_memory_space_constraint(a, pltpu.HBM)
        self.bufs[f"g:{name}:{layer}"] = g.reshape(N_CHIPS, rows, cols)
        self.bufs[f"t:{name}:{layer}"] = hbm(lax.empty((N_CHIPS, rows // 2, cols), F32))
        self.bufs[f"l:{name}:{layer}"] = hbm(lax.empty((3, rows // 2, cols), BF16))

    def job(self, kind, name, layer, part=0, parts=1):
        _, rows, cols = self.shapes[name]
        key = f"{name}:{layer}"
        if kind == "gi":
            return _job_gather_ici("w:" + key, rows, part, parts)
        if kind == "gd":
            return _job_gather_d2d("w:" + key, rows, part, parts)
        if kind == "px":
            return _job_pair_exchange("g:" + key, "t:" + key, rows)
        if kind == "cx":
            if "p:" + key not in self.bufs:
                self.bufs["p:" + key] = _pair_sum(self.bufs["g:" + key], self.bufs["t:" + key], self.place,
                                                  name=f"pair_sum_{name}{layer}")
            nr = rows // 2 // parts
            return _job_chip_exchange("p:" + key, "l:" + key, part * nr, nr)
        if kind == "ps":
            return _job_pair_share("G:" + name, layer, rows)
        raise ValueError(kind)

    def comm(self, call):
        specs = SCHEDULE.get(call)
        return None if specs is None else _Comm(self, [self.job(*spec) for spec in specs])


def _step(x, tgt, w, m, v):
    chip = 2 * lax.axis_index("x") + lax.axis_index("y")
    place = _place_scalars()
    plan = _Plan({n: w[n].shape for n in BIG}, place)
    items = [(n, l) for n in BIG for l in range(w[n].shape[0])]

    for n, l in items:
        plan.bufs[f"w:{n}:{l}"] = _cast_into_slot(w[n], place, layer=l, name=f"cast_{n}{l}")
    _comm_only(plan, [[plan.job("gi", "ab_w_in", 0)], [plan.job("gd", "ab_w_in", 0)]], name="gather_first")
    conv_pack, conv_sizes = _pack([w[n] for n in SMALL_SHARDED])
    conv_all = _exchange_packs(conv_pack, reduce=False, name="gather_conv_weights")
    conv_shapes = [w[n].shape for n in SMALL_SHARDED]
    per_chip = [_unpack(conv_all[2 * s], conv_sizes, conv_shapes) for s in range(N_CHIPS)]
    small = {n: w[n] for n in SMALL_REPLICATED}
    for idx, n in enumerate(SMALL_SHARDED):
        small[n] = jnp.concatenate([per_chip[s][idx] for s in range(N_CHIPS)], axis=-1)

    loss, dx, sg = _local_step(x, tgt, small, plan)

    for n, l in items:
        plan.bufs["G:" + n] = _chip_sum(plan.bufs[f"p:{n}:{l}"], plan.bufs[f"l:{n}:{l}"], plan.bufs.get("G:" + n),
                                        place, layer=l, shape=w[n].shape, name=f"chip_sum_{n}{l}")
    _comm_only(plan, [[plan.job("ps", n, l) for n, l in items]], name="reduce_pair_share")
    grads_big = [plan.bufs["G:" + n] for n in BIG]

    g_pack, g_sizes = _pack([sg[n] for n in SMALL])
    g_sum = _exchange_packs(g_pack, reduce=True, name="allreduce_small_grads")
    full_shapes = [small[n].shape for n in SMALL]
    g_small = dict(zip(SMALL, _unpack(g_sum, g_sizes, full_shapes)))
    for n in SMALL_SHARDED:
        width = w[n].shape[-1]
        g_small[n] = lax.dynamic_slice_in_dim(g_small[n], chip * width, width, axis=g_small[n].ndim - 1)

    grad, delta, new_m, new_v = {}, {}, {}, {}
    for n, g in zip(BIG, grads_big):
        grad[n] = g
        delta[n], new_m[n], new_v[n] = _adamw(w[n], g, m[n], v[n], name=f"adamw_{n}")
    shapes = [w[n].shape for n in SMALL]
    wp, sizes = _pack([w[n] for n in SMALL])
    gp, _ = _pack([g_small[n] for n in SMALL])
    mp, _ = _pack([m[n] for n in SMALL])
    vp, _ = _pack([v[n] for n in SMALL])
    R = wp.shape[0]
    dp, m2p, v2p = _adamw(wp.reshape(1, R, LANES), gp.reshape(1, R, LANES), mp.reshape(1, R, LANES),
                          vp.reshape(1, R, LANES), name="adamw_small")
    for n, d_, m_, v_ in zip(SMALL, _unpack(dp, sizes, shapes), _unpack(m2p, sizes, shapes),
                             _unpack(v2p, sizes, shapes)):
        grad[n] = g_small[n]
        delta[n], new_m[n], new_v[n] = d_, m_, v_
    return loss, dx, grad, delta, new_m, new_v


def kernel(x, norm_mix, norm_ffn, norm_final, ab_w_in, a_ln_g, a_ln_b, a_w_s, a_b_s, b_conv_w, b_conv_b, b_ln_g, b_ln_b, ab_w_out, c_w_in, c_conv_w, c_w_out, f_w_up, f_conv_w, f_w_down, loss_target, m_norm_mix, m_norm_ffn, m_norm_final, m_ab_w_in, m_a_ln_g, m_a_ln_b, m_a_w_s, m_a_b_s, m_b_conv_w, m_b_conv_b, m_b_ln_g, m_b_ln_b, m_ab_w_out, m_c_w_in, m_c_conv_w, m_c_w_out, m_f_w_up, m_f_conv_w, m_f_w_down, v_norm_mix, v_norm_ffn, v_norm_final, v_ab_w_in, v_a_ln_g, v_a_ln_b, v_a_w_s, v_a_b_s, v_b_conv_w, v_b_conv_b, v_b_ln_g, v_b_ln_b, v_ab_w_out, v_c_w_in, v_c_conv_w, v_c_w_out, v_f_w_up, v_f_conv_w, v_f_w_down):
    given = dict(locals())
    w = {n: given[n] for n in ALL_WEIGHTS}
    m = {n: given["m_" + n] for n in ALL_WEIGHTS}
    v = {n: given["v_" + n] for n in ALL_WEIGHTS}
    T = x.shape[1]
    loss, dx, grad, delta, new_m, new_v = _step(x.reshape(T, D_MODEL), loss_target.reshape(T, D_MODEL), w, m, v)
    loss = lax.psum(loss[0, 0], ("x", "y", "c"))
    out = [loss, dx.reshape(x.shape)]
    for d in (grad, delta, new_m, new_v):
        out += [d[n] for n in ALL_WEIGHTS]
    return tuple(out)
```

```python
import functools
import math

import jax
import jax.numpy as jnp
from jax import lax
from jax.experimental import pallas as pl
from jax.experimental.pallas import tpu as pltpu

F32 = jnp.float32
BF16 = jnp.bfloat16

EPS = 1e-6
D_MODEL = 1024
CHUNK = 128
HEAD_DIM = 128
A_HEADS = 4
D_A = 512
D_B = 512
B_CONV = 31
C_CONV = 3
D_FF = 2816
F_CONV = 3
N_CHIPS = 4

ADAM_LR = 0.001
ADAM_B1 = 0.9
ADAM_B2 = 0.999
ADAM_EPS = 1e-08
ADAM_WD = 0.01
ADAM_STEP = 10

SUBLANES = 8
LANES = 128
HALO_SHORT = 16
HALO_LONG = 32
VMEM_BYTES_MAX = 60000 * 1024

INV_SQRT2 = 1.0 / math.sqrt(2.0)
INV_SQRT_2PI = 1.0 / math.sqrt(2.0 * math.pi)

MESH = pl.DeviceIdType.MESH


def _cparams(sem, vmem_mb):
    del vmem_mb
    return pltpu.CompilerParams(dimension_semantics=sem, vmem_limit_bytes=VMEM_BYTES_MAX)


def _pick(total, pref):
    for c in (2048, 1024, 512, 256, 128):
        if c <= pref and total % c == 0:
            return c
    raise ValueError(f"no tile for {total}")


def _sigmoid(x):
    return jax.nn.sigmoid(x)


def _silu(x):
    return x * _sigmoid(x)


def _dsilu(x):
    s = _sigmoid(x)
    return s * (1.0 + x * (1.0 - s))


def _gelu(x):
    return 0.5 * x * (1.0 + lax.erf(x * INV_SQRT2))


def _dgelu(x):
    return 0.5 * (1.0 + lax.erf(x * INV_SQRT2)) + x * jnp.exp(-0.5 * x * x) * INV_SQRT_2PI


def _ln_stats(x):
    mu = jnp.mean(x, axis=-1, keepdims=True)
    xc = x - mu
    var = jnp.mean(xc * xc, axis=-1, keepdims=True)
    r = lax.rsqrt(var + EPS)
    return xc * r, r


def _ln_bwd(dy, xh, r, g):
    dxh = dy * g
    m1 = jnp.mean(dxh, axis=-1, keepdims=True)
    m2 = jnp.mean(dxh * xh, axis=-1, keepdims=True)
    return r * (dxh - m1 - xh * m2)


def _rowsum(x):
    return jnp.sum(x, axis=0, keepdims=True)


TRANSPOSE_COLS = 256


def _store_transposed(t_ref, x, col0=0):
    cols = x.shape[1]
    step = min(TRANSPOSE_COLS, cols)
    for c0 in range(0, cols, step):
        t_ref[col0 + c0:col0 + c0 + step, :] = x[:, c0:c0 + step].T.astype(BF16)


ANY = pl.BlockSpec(memory_space=pltpu.HBM)


def _place():
    x, y, c = lax.axis_index("x"), lax.axis_index("y"), lax.axis_index("c")
    peers = [(1 - x, y), (x, 1 - y), (1 - x, 1 - y)]
    return x, y, c, 2 * x + y, (x, y, 1 - c), peers


def _half(rows, which):
    return pl.ds(which * (rows // 2), rows // 2)


def _remote(src, dst, send_sem, recv_sem, device):
    return pltpu.make_async_remote_copy(src_ref=src, dst_ref=dst, send_sem=send_sem, recv_sem=recv_sem,
                                        device_id=device, device_id_type=MESH)


class _Job:
    def __init__(self, reads, writes, ncopies, copies):
        self.reads, self.writes, self.ncopies, self.copies = reads, writes, ncopies, copies


def _share(rows, which, part, parts):
    nr = rows // 2 // parts
    return pl.ds(which * (rows // 2) + part * nr, nr)


def _job_gather_ici(name, rows, part, parts):
    def copies(src, dst, sem):
        x, y, c, k, sib, peers = _place()
        mine_rows = _share(rows, c, part, parts)
        out = []
        for j, (px, py) in enumerate(peers):
            mine = src[name].at[0, k, mine_rows]
            out.append((_remote(mine, dst[name].at[0, k, mine_rows], sem(j, 0), sem(j, 1), (px, py, c)),
                        _remote(mine, dst[name].at[0, 2 * px + py, mine_rows], sem(j, 0), sem(j, 1), (px, py, c))))
        return out
    return _Job([], [name], 3, copies)


def _job_gather_d2d(name, rows, part, parts):
    def copies(src, dst, sem):
        x, y, c, k, sib, peers = _place()
        out = []
        for j, (px, py) in enumerate(peers):
            landed = src[name].at[0, 2 * px + py, _share(rows, c, part, parts)]
            out.append((_remote(landed, dst[name].at[0, 2 * px + py, _share(rows, c, part, parts)],
                                sem(j, 0), sem(j, 1), sib),
                        _remote(landed, dst[name].at[0, 2 * px + py, _share(rows, 1 - c, part, parts)],
                                sem(j, 0), sem(j, 1), sib)))
        return out
    return _Job([], [name], 3, copies)


def _job_pair_exchange(gname, tname, rows):
    def copies(src, dst, sem):
        x, y, c, k, sib, peers = _place()
        cp = _remote(src[gname].at[:, _half(rows, 1 - c), :], dst[tname], sem(0, 0), sem(0, 1), sib)
        return [(cp, cp)]
    return _Job([gname], [tname], 1, copies)


def _job_chip_exchange(pname, lname, r0, nr):
    def copies(src, dst, sem):
        x, y, c, k, sib, peers = _place()
        out = []
        for j, (px, py) in enumerate(peers):
            cp = _remote(src[pname].at[2 * px + py, pl.ds(r0, nr)], dst[lname].at[j, pl.ds(r0, nr)],
                         sem(j, 0), sem(j, 1), (px, py, c))
            out.append((cp, cp))
        return out
    return _Job([pname], [lname], 3, copies)


def _job_pair_share(name, layer, rows):
    def copies(src, dst, sem):
        x, y, c, k, sib, peers = _place()
        mine = src[name].at[layer, _half(rows, c)]
        return [(_remote(mine, dst[name].at[layer, _half(rows, c)], sem(0, 0), sem(0, 1), sib),
                 _remote(mine, dst[name].at[layer, _half(rows, 1 - c)], sem(0, 0), sem(0, 1), sib))]
    return _Job([], [name], 1, copies)


class _Comm:
    def __init__(self, plan, jobs):
        self.plan, self.jobs = plan, jobs
        self.writes, self.reads = [], []
        for job in jobs:
            for n in job.writes:
                if n not in self.writes:
                    self.writes.append(n)
        for job in jobs:
            for n in job.reads:
                if n not in self.writes and n not in self.reads:
                    self.reads.append(n)
        self.ncopies = sum(job.ncopies for job in jobs)

    def descriptors(self, src, dst, sems, base):
        out = []
        for job in self.jobs:
            sem = lambda j, which, base=base: sems.at[base + j, which]
            out += job.copies(src, dst, sem)
            base += job.ncopies
        return out

    def start(self, src, dst, sems, base=0):
        for first, _ in self.descriptors(src, dst, sems, base):
            first.start()

    def finish(self, src, dst, sems, base=0):
        for _, landed in self.descriptors(src, dst, sems, base):
            landed.wait()


def _comm_operands(comm):
    bufs = comm.plan.bufs
    shapes = [jax.ShapeDtypeStruct(bufs[n].shape, bufs[n].dtype) for n in comm.writes]
    return [bufs[n] for n in comm.reads] + [bufs[n] for n in comm.writes], shapes


def _pallas(comm, body, *, name, grid, in_specs, out_specs, out_shape, compiler_params, scratch_shapes=(),
            aliases=None):
    aliases = dict(aliases or {})
    if comm is None:
        return pl.pallas_call(body, name=name, grid=grid, in_specs=in_specs, out_specs=out_specs,
                              out_shape=out_shape, scratch_shapes=list(scratch_shapes),
                              input_output_aliases=aliases, compiler_params=compiler_params)
    single = not isinstance(out_shape, (list, tuple))
    base_specs = [out_specs] if single else list(out_specs)
    base_shape = [out_shape] if single else list(out_shape)
    nb, nr, nw, nbo, nsc = len(in_specs), len(comm.reads), len(comm.writes), len(base_specs), len(scratch_shapes)

    def wrapped(*refs):
        base_in, rd, wr_in = refs[:nb], refs[nb:nb + nr], refs[nb + nr:nb + nr + nw]
        o0 = nb + nr + nw
        base_out, wr_out = refs[o0:o0 + nbo], refs[o0 + nbo:o0 + nbo + nw]
        scratch, sems = refs[o0 + nbo + nw:o0 + nbo + nw + nsc], refs[-1]
        src = dict(zip(comm.reads, rd))
        src.update(zip(comm.writes, wr_in))
        dst = dict(zip(comm.writes, wr_out))
        first = functools.reduce(jnp.logical_and, [pl.program_id(a) == 0 for a in range(len(grid))])
        last = functools.reduce(jnp.logical_and,
                                [pl.program_id(a) == pl.num_programs(a) - 1 for a in range(len(grid))])

        @pl.when(first)
        def _():
            comm.start(src, dst, sems)
        body(*base_in, *base_out, *scratch)

        @pl.when(last)
        def _():
            comm.finish(src, dst, sems)

    operands, shapes = _comm_operands(comm)
    call = pl.pallas_call(
        wrapped, name=name, grid=grid, in_specs=list(in_specs) + [ANY] * (nr + nw),
        out_specs=base_specs + [ANY] * nw, out_shape=base_shape + shapes,
        input_output_aliases={**aliases, **{nb + nr + q: nbo + q for q in range(nw)}},
        scratch_shapes=list(scratch_shapes) + [pltpu.SemaphoreType.DMA((comm.ncopies, 2))],
        compiler_params=compiler_params)

    def run(*args):
        outs = call(*args, *operands)
        for q, n in enumerate(comm.writes):
            comm.plan.bufs[n] = outs[nbo + q]
        return outs[0] if single else list(outs[:nbo])

    return run


def _comm_only(plan, phases, *, name):
    comms = [_Comm(plan, jobs) for jobs in phases]
    both = _Comm(plan, [job for jobs in phases for job in jobs])
    nr, nw = len(both.reads), len(both.writes)

    def body(*refs):
        rd, wr_in, wr_out, sems = refs[:nr], refs[nr:nr + nw], refs[nr + nw:nr + 2 * nw], refs[-1]
        src = dict(zip(both.reads, rd))
        src.update(zip(both.writes, wr_in))
        dst = dict(zip(both.writes, wr_out))
        base = 0
        for comm in comms:
            comm.start(src, dst, sems, base)
            comm.finish(src, dst, sems, base)
            base += comm.ncopies

    operands, shapes = _comm_operands(both)
    outs = pl.pallas_call(
        body, name=name, in_specs=[ANY] * (nr + nw), out_specs=[ANY] * nw, out_shape=shapes,
        input_output_aliases={nr + q: q for q in range(nw)},
        scratch_shapes=[pltpu.SemaphoreType.DMA((both.ncopies, 2))],
    )(*operands)
    for q, n in enumerate(both.writes):
        plan.bufs[n] = outs[q]


def _mm_nn(a, w, *, layer, tm, tn, residual=None, norm=None, out_dtype=F32, name, comm=None):
    T, K = a.shape
    if w.ndim == 4:
        _, S, _, n4 = w.shape
        N = S * n4
        bps = n4 // tn
        w_spec = pl.BlockSpec((None, None, K, tn), lambda j, i: (layer, j // bps, 0, j % bps))
    else:
        N = w.shape[2]
        w_spec = pl.BlockSpec((None, K, tn), lambda j, i: (layer, 0, j))
    in_specs = [pl.BlockSpec((tm, K), lambda j, i: (i, 0)), w_spec]
    args = [a, w]
    if residual is not None:
        in_specs.append(pl.BlockSpec((tm, tn), lambda j, i: (i, j)))
        args.append(residual)
    out_specs = pl.BlockSpec((tm, tn), lambda j, i: (i, j))
    out_shape = jax.ShapeDtypeStruct((T, N), out_dtype)
    if norm is not None:
        assert tn == N
        g, norm_layer = norm
        in_specs.append(pl.BlockSpec((None, 1, N), lambda j, i: (norm_layer, 0, 0)))
        args.append(g)
        out_specs = [out_specs, pl.BlockSpec((tm, tn), lambda j, i: (i, j)),
                     pl.BlockSpec((tn, tm), lambda j, i: (j, i))]
        out_shape = [out_shape, jax.ShapeDtypeStruct((T, N), BF16), jax.ShapeDtypeStruct((N, T), BF16)]

    def body(*refs):
        a_ref, w_ref = refs[0], refs[1]
        acc = jnp.dot(a_ref[...].astype(BF16), w_ref[...], preferred_element_type=F32)
        if residual is not None:
            acc = refs[2][...] + acc
        if norm is None:
            refs[-1][...] = acc.astype(out_dtype)
        else:
            refs[-3][...] = acc.astype(out_dtype)
            r = lax.rsqrt(jnp.mean(acc * acc, axis=-1, keepdims=True) + EPS)
            h = acc * r * refs[-4][...]
            refs[-2][...] = h.astype(BF16)
            _store_transposed(refs[-1], h)

    return _pallas(
        comm, body, name=name, grid=(N // tn, T // tm), in_specs=in_specs,
        out_specs=out_specs, out_shape=out_shape,
        compiler_params=_cparams(("parallel", "parallel"), 48),
    )(*args)


def _mm_nt(dy, w, *, layer, tm, tn, name, out_dtype=F32, comm=None):
    T = dy.shape[0]
    nt_dims = (((1,), (1,)), ((), ()))
    if w.ndim == 4:
        _, S, K, n4 = w.shape

        def body(dy_ref, w_ref, o_ref):
            @pl.when(pl.program_id(1) == 0)
            def _():
                o_ref[...] = jnp.zeros_like(o_ref)
            o_ref[...] += lax.dot_general(dy_ref[...].astype(BF16), w_ref[...], nt_dims,
                                          preferred_element_type=F32)

        return _pallas(
            comm, body, name=name, grid=(T // tm, S),
            in_specs=[pl.BlockSpec((tm, n4), lambda i, s: (i, s)),
                      pl.BlockSpec((None, None, K, n4), lambda i, s: (layer, s, 0, 0))],
            out_specs=pl.BlockSpec((tm, K), lambda i, s: (i, 0)),
            out_shape=jax.ShapeDtypeStruct((T, K), F32),
            compiler_params=_cparams(("parallel", "arbitrary"), 48),
        )(dy, w)
    _, R, N = w.shape

    def body2(dy_ref, w_ref, o_ref):
        o_ref[...] = lax.dot_general(dy_ref[...].astype(BF16), w_ref[...], nt_dims,
                                     preferred_element_type=F32).astype(out_dtype)

    return _pallas(
        comm, body2, name=name, grid=(R // tn, T // tm),
        in_specs=[pl.BlockSpec((tm, N), lambda j, i: (i, 0)),
                  pl.BlockSpec((None, tn, N), lambda j, i: (layer, j, 0))],
        out_specs=pl.BlockSpec((tm, tn), lambda j, i: (i, j)),
        out_shape=jax.ShapeDtypeStruct((T, R), out_dtype),
        compiler_params=_cparams(("parallel", "parallel"), 48),
    )(dy, w)


def _mm_tn(at, dy, *, shards, tk, tn, tt, name, comm=None):
    K, T = at.shape
    N = dy.shape[1]

    def body(a_ref, dy_ref, o_ref):
        @pl.when(pl.program_id(2) == 0)
        def _():
            o_ref[...] = jnp.zeros_like(o_ref)
        o_ref[...] += jnp.dot(a_ref[...], dy_ref[...].astype(BF16), preferred_element_type=F32)

    if shards is None:
        out_spec = pl.BlockSpec((tk, tn), lambda k, n, t: (k, n))
        out_shape = jax.ShapeDtypeStruct((K, N), F32)
    else:
        n4 = N // shards
        bps = n4 // tn
        out_spec = pl.BlockSpec((None, tk, tn), lambda k, n, t: (n // bps, k, n % bps))
        out_shape = jax.ShapeDtypeStruct((shards, K, n4), F32)
    return _pallas(
        comm, body, name=name, grid=(K // tk, N // tn, T // tt),
        in_specs=[pl.BlockSpec((tk, tt), lambda k, n, t: (k, t)),
                  pl.BlockSpec((tt, tn), lambda k, n, t: (t, n))],
        out_specs=out_spec, out_shape=out_shape,
        compiler_params=_cparams(("parallel", "parallel", "arbitrary"), 48),
    )(at, dy)


def _rmsnorm_fwd(x, g, *, layer, tm, name, comm=None):
    T, D = x.shape

    def body(x_ref, g_ref, h_ref, ht_ref):
        xf = x_ref[...]
        r = lax.rsqrt(jnp.mean(xf * xf, axis=-1, keepdims=True) + EPS)
        h = xf * r * g_ref[...]
        h_ref[...] = h.astype(BF16)
        _store_transposed(ht_ref, h)

    return _pallas(
        comm, body, name=name, grid=(T // tm,),
        in_specs=[pl.BlockSpec((tm, D), lambda i: (i, 0)),
                  pl.BlockSpec((None, 1, D), lambda i: (layer, 0, 0))],
        out_specs=[pl.BlockSpec((tm, D), lambda i: (i, 0)), pl.BlockSpec((D, tm), lambda i: (0, i))],
        out_shape=[jax.ShapeDtypeStruct((T, D), BF16), jax.ShapeDtypeStruct((D, T), BF16)],
        compiler_params=_cparams(("parallel",), 32),
    )(x, g)


def _rmsnorm_bwd(x, g, dh, dres, *, layer, tm, name, comm=None):
    T, D = x.shape

    def body(x_ref, g_ref, dh_ref, dres_ref, dx_ref, dg_ref):
        @pl.when(pl.program_id(0) == 0)
        def _():
            dg_ref[...] = jnp.zeros_like(dg_ref)
        xf = x_ref[...]
        r = lax.rsqrt(jnp.mean(xf * xf, axis=-1, keepdims=True) + EPS)
        xh = xf * r
        dh = dh_ref[...]
        dg_ref[...] += _rowsum(dh * xh)
        dxh = dh * g_ref[...]
        dx_ref[...] = dres_ref[...] + r * (dxh - xh * jnp.mean(dxh * xh, axis=-1, keepdims=True))

    return _pallas(
        comm, body, name=name, grid=(T // tm,),
        in_specs=[pl.BlockSpec((tm, D), lambda i: (i, 0)),
                  pl.BlockSpec((None, 1, D), lambda i: (layer, 0, 0)),
                  pl.BlockSpec((tm, D), lambda i: (i, 0)),
                  pl.BlockSpec((tm, D), lambda i: (i, 0))],
        out_specs=[pl.BlockSpec((tm, D), lambda i: (i, 0)),
                   pl.BlockSpec((1, D), lambda i: (0, 0))],
        out_shape=[jax.ShapeDtypeStruct((T, D), F32), jax.ShapeDtypeStruct((1, D), F32)],
        compiler_params=_cparams(("arbitrary",), 40),
    )(x, g, dh, dres)


def _loss_head(x, tgt, g, *, tm, name, comm=None):
    T, D = x.shape

    def body(x_ref, t_ref, g_ref, loss_ref, dx_ref, dg_ref):
        @pl.when(pl.program_id(0) == 0)
        def _():
            dg_ref[...] = jnp.zeros_like(dg_ref)
            loss_ref[...] = jnp.zeros_like(loss_ref)
        xf = x_ref[...]
        gg = g_ref[...]
        r = lax.rsqrt(jnp.mean(xf * xf, axis=-1, keepdims=True) + EPS)
        xh = xf * r
        err = xh * gg - t_ref[...]
        row = jnp.mean(err * err, axis=-1, keepdims=True)
        loss_ref[...] += 0.5 * jnp.sum(row, axis=0, keepdims=True)
        dy = err * (1.0 / D)
        dg_ref[...] += _rowsum(dy * xh)
        dxh = dy * gg
        dx_ref[...] = r * (dxh - xh * jnp.mean(dxh * xh, axis=-1, keepdims=True))

    return _pallas(
        comm, body, name=name, grid=(T // tm,),
        in_specs=[pl.BlockSpec((tm, D), lambda i: (i, 0)),
                  pl.BlockSpec((tm, D), lambda i: (i, 0)),
                  pl.BlockSpec((1, D), lambda i: (0, 0))],
        out_specs=[pl.BlockSpec((1, 1), lambda i: (0, 0)),
                   pl.BlockSpec((tm, D), lambda i: (i, 0)),
                   pl.BlockSpec((1, D), lambda i: (0, 0))],
        out_shape=[jax.ShapeDtypeStruct((1, 1), F32), jax.ShapeDtypeStruct((T, D), F32),
                   jax.ShapeDtypeStruct((1, D), F32)],
        compiler_params=_cparams(("arbitrary",), 40),
    )(x, tgt, g)


CONV_ROWS = 64
CONV_COLS = 256


def _halo_prev_index(tm, halo):
    per = tm // halo
    return lambda i: jnp.maximum(i * per - 1, 0)


def _halo_next_index(tm, halo, total):
    per = tm // halo
    last = total // halo - 1
    return lambda i: jnp.minimum((i + 1) * per, last)


def _causal_mask():
    t = lax.broadcasted_iota(jnp.int32, (CHUNK, CHUNK), 0)
    s = lax.broadcasted_iota(jnp.int32, (CHUNK, CHUNK), 1)
    return s <= t


def _mixer_ab_fwd(z, a_ln_g, a_ln_b, w_s, b_s, conv_w, conv_b, b_ln_g, b_ln_b, *, tm, name, comm=None):
    T = z.shape[0]
    nchunk = tm // CHUNK
    halo = HALO_LONG

    def body(za_ref, zb_ref, zh_ref, alg_ref, alb_ref, ws_ref, bs_ref, cw_ref, cbias_ref,
             blg_ref, blb_ref, y_ref, yt_ref, cb_ref, ext_ref):
        i = pl.program_id(0)
        gu = _gelu(za_ref[:, :D_A].astype(F32))
        gv = _gelu(za_ref[:, D_A:].astype(F32))
        xh, _ = _ln_stats(gv)
        lv = (xh * alg_ref[...] + alb_ref[...]).astype(BF16)
        mask = _causal_mask()
        for h in range(A_HEADS):
            wm = jnp.where(mask, ws_ref[h], 0.0).astype(BF16)
            cols = slice(h * HEAD_DIM, (h + 1) * HEAD_DIM)
            for c in range(nchunk):
                rows = slice(c * CHUNK, (c + 1) * CHUNK)
                mixed = jnp.dot(wm, lv[rows, cols], preferred_element_type=F32) + bs_ref[h]
                ya = gu[rows, cols] * mixed
                y_ref[rows, cols] = ya.astype(BF16)
                yt_ref[cols, rows] = ya.T.astype(BF16)
        ext_ref[halo:halo + tm, :] = zb_ref[:, :D_B].astype(F32) * _sigmoid(zb_ref[:, D_B:].astype(F32))
        prev = zh_ref[:, :D_B].astype(F32) * _sigmoid(zh_ref[:, D_B:].astype(F32))
        ext_ref[0:halo, :] = jnp.where(i > 0, prev, 0.0)
        for rb in range(tm // CONV_ROWS):
            for cb in range(D_B // CONV_COLS):
                cs = slice(cb * CONV_COLS, (cb + 1) * CONV_COLS)
                window = ext_ref[rb * CONV_ROWS:rb * CONV_ROWS + CONV_ROWS + halo, cs]
                acc = jnp.zeros((CONV_ROWS, CONV_COLS), F32)
                for k in range(B_CONV):
                    shifted = _rows_after(window, halo - (B_CONV - 1) + k)[:CONV_ROWS]
                    acc = acc + cw_ref[k:k + 1, cs] * shifted
                cb_ref[rb * CONV_ROWS:(rb + 1) * CONV_ROWS, cs] = acc + cbias_ref[:, cs]
        xhb, _ = _ln_stats(cb_ref[...])
        yb = _silu(xhb * blg_ref[...] + blb_ref[...])
        y_ref[:, D_A:] = yb.astype(BF16)
        _store_transposed(yt_ref, yb, D_A)

    row = lambda i: (i, 0)
    par = lambda i: (0, 0)
    return _pallas(
        comm, body, name=name, grid=(T // tm,),
        in_specs=[pl.BlockSpec((tm, 2 * D_A), lambda i: (i, 0)),
                  pl.BlockSpec((tm, 2 * D_B), lambda i: (i, 1)),
                  pl.BlockSpec((halo, 2 * D_B), lambda i: (_halo_prev_index(tm, halo)(i), 1)),
                  pl.BlockSpec((1, D_A), par), pl.BlockSpec((1, D_A), par),
                  pl.BlockSpec((A_HEADS, CHUNK, CHUNK), lambda i: (0, 0, 0)),
                  pl.BlockSpec((A_HEADS, CHUNK, 1), lambda i: (0, 0, 0)),
                  pl.BlockSpec((B_CONV, D_B), par), pl.BlockSpec((1, D_B), par),
                  pl.BlockSpec((1, D_B), par), pl.BlockSpec((1, D_B), par)],
        out_specs=[pl.BlockSpec((tm, D_A + D_B), row), pl.BlockSpec((D_A + D_B, tm), lambda i: (0, i)),
                   pl.BlockSpec((tm, D_B), row)],
        out_shape=[jax.ShapeDtypeStruct((T, D_A + D_B), BF16), jax.ShapeDtypeStruct((D_A + D_B, T), BF16),
                   jax.ShapeDtypeStruct((T, D_B), F32)],
        scratch_shapes=[pltpu.VMEM((halo + tm, D_B), F32)],
        compiler_params=_cparams(("parallel",), 40),
    )(z, z, z, a_ln_g, a_ln_b, w_s, b_s, conv_w, conv_b, b_ln_g, b_ln_b)


def _mixer_ab_bwd_pre(z, cb, dy, a_ln_g, a_ln_b, w_s, b_s, b_ln_g, b_ln_b, *, tm, name, comm=None):
    T = z.shape[0]
    nchunk = tm // CHUNK
    tn_dims = (((0,), (0,)), ((), ()))
    nt_dims = (((1,), (1,)), ((), ()))

    def body(za_ref, cb_ref, dy_ref, alg_ref, alb_ref, ws_ref, bs_ref, blg_ref, blb_ref,
             dza_ref, dcb_ref, dalg_ref, dalb_ref, dws_ref, dbs_ref, dblg_ref, dblb_ref,
             dlv_ref):
        @pl.when(pl.program_id(0) == 0)
        def _():
            for ref in (dalg_ref, dalb_ref, dws_ref, dbs_ref, dblg_ref, dblb_ref):
                ref[...] = jnp.zeros_like(ref)
        ua = za_ref[:, :D_A].astype(F32)
        va = za_ref[:, D_A:].astype(F32)
        gu = _gelu(ua)
        gv = _gelu(va)
        xh, r = _ln_stats(gv)
        alg = alg_ref[...]
        lv = (xh * alg + alb_ref[...]).astype(BF16)
        dya = dy_ref[:, :D_A].astype(F32)
        mask = _causal_mask()
        for h in range(A_HEADS):
            wm = jnp.where(mask, ws_ref[h], 0.0).astype(BF16)
            cols = slice(h * HEAD_DIM, (h + 1) * HEAD_DIM)
            dwm = jnp.zeros((CHUNK, CHUNK), F32)
            dbs = jnp.zeros((CHUNK, 1), F32)
            for c in range(nchunk):
                rows = slice(c * CHUNK, (c + 1) * CHUNK)
                lvb = lv[rows, cols]
                mixed = jnp.dot(wm, lvb, preferred_element_type=F32) + bs_ref[h]
                dyb = dya[rows, cols]
                dza_ref[rows, cols] = (dyb * mixed * _dgelu(ua[rows, cols])).astype(BF16)
                dmixed = dyb * gu[rows, cols]
                dmb = dmixed.astype(BF16)
                dlv_ref[rows, cols] = lax.dot_general(wm, dmb, tn_dims, preferred_element_type=F32)
                dwm = dwm + lax.dot_general(dmb, lvb, nt_dims, preferred_element_type=F32)
                dbs = dbs + jnp.sum(dmixed, axis=1, keepdims=True)
            dws_ref[h] += jnp.where(mask, dwm, 0.0)
            dbs_ref[h] += dbs
        dlv = dlv_ref[...]
        dalg_ref[...] += _rowsum(dlv * xh)
        dalb_ref[...] += _rowsum(dlv)
        dgv = _ln_bwd(dlv, xh, r, alg)
        dza_ref[:, D_A:] = (dgv * _dgelu(va)).astype(BF16)
        xhb, rb = _ln_stats(cb_ref[...])
        blg = blg_ref[...]
        lb = xhb * blg + blb_ref[...]
        dlb = dy_ref[:, D_A:].astype(F32) * _dsilu(lb)
        dblg_ref[...] += _rowsum(dlb * xhb)
        dblb_ref[...] += _rowsum(dlb)
        dcb_ref[...] = _ln_bwd(dlb, xhb, rb, blg)

    row = lambda i: (i, 0)
    par = lambda i: (0, 0)
    par3 = lambda i: (0, 0, 0)
    return _pallas(
        comm, body, name=name, grid=(T // tm,),
        in_specs=[pl.BlockSpec((tm, 2 * D_A), row), pl.BlockSpec((tm, D_B), row),
                  pl.BlockSpec((tm, D_A + D_B), row),
                  pl.BlockSpec((1, D_A), par), pl.BlockSpec((1, D_A), par),
                  pl.BlockSpec((A_HEADS, CHUNK, CHUNK), par3),
                  pl.BlockSpec((A_HEADS, CHUNK, 1), par3),
                  pl.BlockSpec((1, D_B), par), pl.BlockSpec((1, D_B), par)],
        out_specs=[pl.BlockSpec((tm, 2 * D_A), row), pl.BlockSpec((tm, D_B), row),
                   pl.BlockSpec((1, D_A), par), pl.BlockSpec((1, D_A), par),
                   pl.BlockSpec((A_HEADS, CHUNK, CHUNK), par3),
                   pl.BlockSpec((A_HEADS, CHUNK, 1), par3),
                   pl.BlockSpec((1, D_B), par), pl.BlockSpec((1, D_B), par)],
        out_shape=[jax.ShapeDtypeStruct((T, 2 * D_A + 2 * D_B), BF16), jax.ShapeDtypeStruct((T, D_B), F32),
                   jax.ShapeDtypeStruct((1, D_A), F32), jax.ShapeDtypeStruct((1, D_A), F32),
                   jax.ShapeDtypeStruct((A_HEADS, CHUNK, CHUNK), F32),
                   jax.ShapeDtypeStruct((A_HEADS, CHUNK, 1), F32),
                   jax.ShapeDtypeStruct((1, D_B), F32), jax.ShapeDtypeStruct((1, D_B), F32)],
        scratch_shapes=[pltpu.VMEM((tm, D_A), F32)],
        compiler_params=_cparams(("arbitrary",), 40),
    )(z, cb, dy, a_ln_g, a_ln_b, w_s, b_s, b_ln_g, b_ln_b)


def _mixer_b_conv_bwd(z, dcb, conv_w, dz, *, tm, name, comm=None):
    T = z.shape[0]
    halo = HALO_LONG

    def body(zb_ref, dcb_ref, dcn_ref, cw_ref, dz_in_ref, dzb_ref, dcw_ref, dbias_ref, dext_ref):
        i = pl.program_id(0)
        last = pl.num_programs(0) - 1

        @pl.when(i == 0)
        def _():
            dcw_ref[...] = jnp.zeros_like(dcw_ref)
            dbias_ref[...] = jnp.zeros_like(dbias_ref)
        dcb = dcb_ref[...]
        dext_ref[0:tm, :] = dcb
        dext_ref[tm:tm + halo, :] = jnp.where(i < last, dcn_ref[...], 0.0)
        dbias_ref[...] += _rowsum(dcb)
        for rb in range(tm // CONV_ROWS):
            for cb in range(D_B // CONV_COLS):
                cs = slice(cb * CONV_COLS, (cb + 1) * CONV_COLS)
                gcs = slice(D_B + cb * CONV_COLS, D_B + (cb + 1) * CONV_COLS)
                rs = slice(rb * CONV_ROWS, (rb + 1) * CONV_ROWS)
                xbb = zb_ref[rs, cs].astype(F32)
                sgb = _sigmoid(zb_ref[rs, gcs].astype(F32))
                yb0 = xbb * sgb
                window = dext_ref[rb * CONV_ROWS:rb * CONV_ROWS + CONV_ROWS + halo, cs]
                acc = jnp.zeros((CONV_ROWS, CONV_COLS), F32)
                for k in range(B_CONV):
                    shifted = _rows_after(window, (B_CONV - 1) - k)[:CONV_ROWS]
                    acc = acc + cw_ref[k:k + 1, cs] * shifted
                    dcw_ref[k:k + 1, cs] += _rowsum(shifted * yb0)
                dzb_ref[rs, cs] = (acc * sgb).astype(BF16)
                dzb_ref[rs, gcs] = (acc * xbb * sgb * (1.0 - sgb)).astype(BF16)

    row = lambda i: (i, 0)
    par = lambda i: (0, 0)
    return _pallas(
        comm, body, name=name, grid=(T // tm,),
        in_specs=[pl.BlockSpec((tm, 2 * D_B), lambda i: (i, 1)),
                  pl.BlockSpec((tm, D_B), row),
                  pl.BlockSpec((halo, D_B), lambda i: (_halo_next_index(tm, halo, T)(i), 0)),
                  pl.BlockSpec((B_CONV, D_B), par), pl.BlockSpec(memory_space=pl.ANY)],
        out_specs=[pl.BlockSpec((tm, 2 * D_B), lambda i: (i, 1)), pl.BlockSpec((B_CONV, D_B), par),
                   pl.BlockSpec((1, D_B), par)],
        out_shape=[jax.ShapeDtypeStruct(dz.shape, BF16), jax.ShapeDtypeStruct((B_CONV, D_B), F32),
                   jax.ShapeDtypeStruct((1, D_B), F32)],
        scratch_shapes=[pltpu.VMEM((tm + halo, D_B), F32)], aliases={4: 0},
        compiler_params=_cparams(("arbitrary",), 40),
    )(z, dcb, dcb, conv_w, dz)


def _rows_before(x, a):
    return x if a == 0 else pltpu.roll(x, a, axis=0)


def _rows_after(x, a):
    return x if a == 0 else pltpu.roll(x, x.shape[0] - a, axis=0)


def _conv3(w_ref, x, halo, cs):
    acc = w_ref[2:3, cs] * x[halo:]
    acc = acc + w_ref[1:2, cs] * _rows_before(x, 1)[halo:]
    return acc + w_ref[0:1, cs] * _rows_before(x, 2)[halo:]


def _mixer_c_fwd(z, conv_w, *, tm, name, comm=None):
    T = z.shape[0]
    D = D_MODEL
    halo = HALO_SHORT
    W = CONV_COLS

    def body(bg_ref, cg_ref, xv_ref, cgh_ref, xvh_ref, w_ref, r_ref, rt_ref):
        i = pl.program_id(0)
        for cb in range(D // W):
            cs = slice(cb * W, (cb + 1) * W)
            prev = jnp.where(i > 0, cgh_ref[:, cs].astype(F32) * xvh_ref[:, cs].astype(F32), 0.0)
            p = jnp.concatenate([prev, cg_ref[:, cs].astype(F32) * xv_ref[:, cs].astype(F32)], axis=0)
            r = bg_ref[:, cs].astype(F32) * _conv3(w_ref, p, halo, cs)
            r_ref[:, cs] = r.astype(BF16)
            _store_transposed(rt_ref, r, cb * W)

    hp = _halo_prev_index(tm, halo)
    return _pallas(
        comm, body, name=name, grid=(T // tm,),
        in_specs=[pl.BlockSpec((tm, D), lambda i: (i, 0)), pl.BlockSpec((tm, D), lambda i: (i, 1)),
                  pl.BlockSpec((tm, D), lambda i: (i, 2)),
                  pl.BlockSpec((halo, D), lambda i: (hp(i), 1)),
                  pl.BlockSpec((halo, D), lambda i: (hp(i), 2)),
                  pl.BlockSpec((None, C_CONV, D), lambda i: (0, 0, 0))],
        out_specs=[pl.BlockSpec((tm, D), lambda i: (i, 0)), pl.BlockSpec((D, tm), lambda i: (0, i))],
        out_shape=[jax.ShapeDtypeStruct((T, D), BF16), jax.ShapeDtypeStruct((D, T), BF16)],
        compiler_params=_cparams(("parallel",), 40),
    )(z, z, z, z, z, conv_w)


def _mixer_c_bwd(z, dr, conv_w, *, tm, name, comm=None):
    T = z.shape[0]
    D = D_MODEL
    halo = HALO_SHORT
    W = CONV_COLS

    def body(bg_ref, cg_ref, xv_ref, cgh_ref, xvh_ref, bgn_ref, dr_ref, drn_ref, w_ref, dz_ref, dw_ref):
        i = pl.program_id(0)
        last = pl.num_programs(0) - 1

        @pl.when(i == 0)
        def _():
            dw_ref[...] = jnp.zeros_like(dw_ref)
        for cb in range(D // W):
            cs = slice(cb * W, (cb + 1) * W)
            cg = cg_ref[:, cs].astype(F32)
            xv = xv_ref[:, cs].astype(F32)
            dr = dr_ref[:, cs].astype(F32)
            p = cg * xv
            prev = jnp.where(i > 0, cgh_ref[:, cs].astype(F32) * xvh_ref[:, cs].astype(F32), 0.0)
            q = _conv3(w_ref, jnp.concatenate([prev, p], axis=0), halo, cs)
            dz_ref[:, cs] = (dr * q).astype(BF16)
            nxt = jnp.where(i < last, drn_ref[:, cs].astype(F32) * bgn_ref[:, cs].astype(F32), 0.0)
            dq = jnp.concatenate([dr * bg_ref[:, cs].astype(F32), nxt], axis=0)
            dp = None
            for k in range(C_CONV):
                shifted = _rows_after(dq, 2 - k)[:tm]
                term = w_ref[k:k + 1, cs] * shifted
                dp = term if dp is None else dp + term
                dw_ref[k:k + 1, cs] += _rowsum(shifted * p)
            dz_ref[:, D + cb * W:D + (cb + 1) * W] = (dp * xv).astype(BF16)
            dz_ref[:, 2 * D + cb * W:2 * D + (cb + 1) * W] = (dp * cg).astype(BF16)

    hp = _halo_prev_index(tm, halo)
    hn = _halo_next_index(tm, halo, T)
    return _pallas(
        comm, body, name=name, grid=(T // tm,),
        in_specs=[pl.BlockSpec((tm, D), lambda i: (i, 0)), pl.BlockSpec((tm, D), lambda i: (i, 1)),
                  pl.BlockSpec((tm, D), lambda i: (i, 2)),
                  pl.BlockSpec((halo, D), lambda i: (hp(i), 1)),
                  pl.BlockSpec((halo, D), lambda i: (hp(i), 2)),
                  pl.BlockSpec((halo, D), lambda i: (hn(i), 0)),
                  pl.BlockSpec((tm, D), lambda i: (i, 0)),
                  pl.BlockSpec((halo, D), lambda i: (hn(i), 0)),
                  pl.BlockSpec((None, C_CONV, D), lambda i: (0, 0, 0))],
        out_specs=[pl.BlockSpec((tm, 3 * D), lambda i: (i, 0)),
                   pl.BlockSpec((C_CONV, D), lambda i: (0, 0))],
        out_shape=[jax.ShapeDtypeStruct((T, 3 * D), BF16), jax.ShapeDtypeStruct((C_CONV, D), F32)],
        compiler_params=_cparams(("arbitrary",), 48),
    )(z, z, z, z, z, z, dr, dr, conv_w)


FFN_COLS = 128


def _ffn_act_fwd(up, conv_w, *, layer, tm, name, comm=None):
    T = up.shape[0]
    halo = HALO_SHORT
    W = FFN_COLS

    def body(up_ref, uph_ref, w_ref, a_ref, at_ref, upc_ref):
        i = pl.program_id(0)

        def conv(cs):
            prev = jnp.where(i > 0, uph_ref[:, cs], jnp.zeros((halo, W), BF16))
            return _conv3(w_ref, jnp.concatenate([prev, up_ref[:, cs]], axis=0).astype(F32), halo, cs)

        for cb in range(D_FF // W):
            gs = slice(cb * W, (cb + 1) * W)
            vs = slice(D_FF + cb * W, D_FF + (cb + 1) * W)
            g = conv(gs)
            v = conv(vs)
            upc_ref[:, gs] = g.astype(BF16)
            upc_ref[:, vs] = v.astype(BF16)
            a = _silu(g) * v
            a_ref[:, gs] = a.astype(BF16)
            _store_transposed(at_ref, a, cb * W)

    return _pallas(
        comm, body, name=name, grid=(T // tm,),
        in_specs=[pl.BlockSpec((tm, 2 * D_FF), lambda i: (i, 0)),
                  pl.BlockSpec((halo, 2 * D_FF), lambda i: (_halo_prev_index(tm, halo)(i), 0)),
                  pl.BlockSpec((None, F_CONV, 2 * D_FF), lambda i: (layer, 0, 0))],
        out_specs=[pl.BlockSpec((tm, D_FF), lambda i: (i, 0)), pl.BlockSpec((D_FF, tm), lambda i: (0, i)),
                   pl.BlockSpec((tm, 2 * D_FF), lambda i: (i, 0))],
        out_shape=[jax.ShapeDtypeStruct((T, D_FF), BF16), jax.ShapeDtypeStruct((D_FF, T), BF16),
                   jax.ShapeDtypeStruct((T, 2 * D_FF), BF16)],
        compiler_params=_cparams(("parallel",), 48),
    )(up, up, conv_w)


def _ffn_act_bwd(up, upc, da, conv_w, *, layer, tm, name, comm=None):
    T = up.shape[0]
    halo = HALO_SHORT
    W = FFN_COLS

    def body(up_ref, upc_ref, upcn_ref, da_ref, dan_ref, w_ref, dup_ref, dw_ref):
        i = pl.program_id(0)
        last = pl.num_programs(0) - 1

        @pl.when(i == 0)
        def _():
            dw_ref[...] = jnp.zeros_like(dw_ref)
        live = jnp.where(i < last, 1.0, 0.0)
        for cb in range(D_FF // W):
            gs = slice(cb * W, (cb + 1) * W)
            vs = slice(D_FF + cb * W, D_FF + (cb + 1) * W)
            g = jnp.concatenate([upc_ref[:, gs], upcn_ref[:, gs]], axis=0).astype(F32)
            v = jnp.concatenate([upc_ref[:, vs], upcn_ref[:, vs]], axis=0).astype(F32)
            da = jnp.concatenate([da_ref[:, gs].astype(F32), dan_ref[:, gs].astype(F32) * live], axis=0)
            s = _sigmoid(g)
            silu = g * s
            grads = (da * v * (s * (1.0 + g * (1.0 - s))), da * silu)
            for cs, d in zip((gs, vs), grads):
                u = up_ref[:, cs].astype(F32)
                acc = None
                for k in range(F_CONV):
                    shifted = _rows_after(d, 2 - k)[:tm]
                    term = w_ref[k:k + 1, cs] * shifted
                    acc = term if acc is None else acc + term
                    dw_ref[k:k + 1, cs] += _rowsum(shifted * u)
                dup_ref[:, cs] = acc.astype(BF16)

    hn = _halo_next_index(tm, halo, T)
    return _pallas(
        comm, body, name=name, grid=(T // tm,),
        in_specs=[pl.BlockSpec((tm, 2 * D_FF), lambda i: (i, 0)),
                  pl.BlockSpec((tm, 2 * D_FF), lambda i: (i, 0)),
                  pl.BlockSpec((halo, 2 * D_FF), lambda i: (hn(i), 0)),
                  pl.BlockSpec((tm, D_FF), lambda i: (i, 0)),
                  pl.BlockSpec((halo, D_FF), lambda i: (hn(i), 0)),
                  pl.BlockSpec((None, F_CONV, 2 * D_FF), lambda i: (layer, 0, 0))],
        out_specs=[pl.BlockSpec((tm, 2 * D_FF), lambda i: (i, 0)),
                   pl.BlockSpec((F_CONV, 2 * D_FF), lambda i: (0, 0))],
        out_shape=[jax.ShapeDtypeStruct((T, 2 * D_FF), BF16),
                   jax.ShapeDtypeStruct((F_CONV, 2 * D_FF), F32)],
        compiler_params=_cparams(("arbitrary",), 56),
    )(up, upc, upc, da, da, conv_w)


def _local_step(x, tgt, small, plan):
    T = x.shape[0]
    tm_e = _pick(T, 256)
    tm_n = _pick(T, 512)
    tm = _pick(T, 1024)
    tm_r = _pick(T, 2048)
    tt = _pick(T, 2048)
    nm = small["norm_mix"].reshape(2, 1, D_MODEL)
    nf = small["norm_ffn"].reshape(2, 1, D_MODEL)
    ngf = small["norm_final"].reshape(1, D_MODEL)
    b_s = small["a_b_s"].reshape(A_HEADS, CHUNK, 1)
    w_s = small["a_w_s"].reshape(A_HEADS, CHUNK, CHUNK)
    b_conv_w = small["b_conv_w"].reshape(B_CONV, D_B)
    sg = {}
    wt, cm = plan.weight, plan.comm

    h_m0, h_m0_t = _rmsnorm_fwd(x, nm, layer=0, tm=tm_n, name="norm_mix0")
    z_ab = _mm_nn(h_m0, wt("ab_w_in", 0), layer=0, tm=tm, tn=512, out_dtype=BF16, name="ab_in", comm=cm("ab_in"))
    yab, yab_t, cb = _mixer_ab_fwd(z_ab, small["a_ln_g"], small["a_ln_b"], w_s, b_s, b_conv_w, small["b_conv_b"],
                                   small["b_ln_g"], small["b_ln_b"], tm=tm_e, name="mixer_ab", comm=cm("mixer_ab"))
    x1, h_f0, h_f0_t = _mm_nn(yab, wt("ab_w_out", 0), layer=0, tm=tm, tn=D_MODEL, residual=x, norm=(nf, 0),
                              name="ab_out", comm=cm("ab_out"))

    def ffn_fwd(xin, h, layer, norm):
        up = _mm_nn(h, wt("f_w_up", layer), layer=0, tm=tm, tn=1408, out_dtype=BF16, name=f"ffn_up{layer}",
                    comm=cm(f"ffn_up{layer}"))
        a, a_t, upc = _ffn_act_fwd(up, small["f_conv_w"], layer=layer, tm=tm_e, name=f"ffn_act{layer}",
                                   comm=cm(f"ffn_act{layer}"))
        out = _mm_nn(a, wt("f_w_down", layer), layer=0, tm=tm, tn=D_MODEL, residual=xin, norm=norm,
                     name=f"ffn_down{layer}", comm=cm(f"ffn_down{layer}"))
        return up, upc, a_t, out

    up0, upc0, a0_t, (x2, h_m1, h_m1_t) = ffn_fwd(x1, h_f0, 0, (nm, 1))
    z_c = _mm_nn(h_m1, wt("c_w_in", 0), layer=0, tm=tm, tn=768, out_dtype=BF16, name="c_in", comm=cm("c_in"))
    r, r_t = _mixer_c_fwd(z_c, small["c_conv_w"], tm=tm_e, name="mixer_c", comm=cm("mixer_c"))
    x3, h_f1, h_f1_t = _mm_nn(r, wt("c_w_out", 0), layer=0, tm=tm, tn=D_MODEL, residual=x2, norm=(nf, 1),
                              name="c_out", comm=cm("c_out"))
    up1, upc1, a1_t, x4 = ffn_fwd(x3, h_f1, 1, None)
    loss, dx, sg["norm_final"] = _loss_head(x4, tgt, ngf, tm=tm_n, name="loss_head")

    def ffn_bwd(dx, xin, h_t, up, upc, a_t, layer):
        da = _mm_nt(dx, wt("f_w_down", layer), layer=0, tm=tm, tn=1408, out_dtype=BF16,
                    name=f"ffn_down_dx{layer}", comm=cm(f"ffn_down_dx{layer}"))
        plan.grad_ready("f_w_down", layer, _mm_tn(a_t, dx, shards=None, tk=1408, tn=1024, tt=tt,
                                                  name=f"ffn_down_dw{layer}", comm=cm(f"ffn_down_dw{layer}")))
        dup, dcw = _ffn_act_bwd(up, upc, da, small["f_conv_w"], layer=layer, tm=tm_e, name=f"ffn_act_bwd{layer}",
                                comm=cm(f"ffn_act_bwd{layer}"))
        dh = _mm_nt(dup, wt("f_w_up", layer), layer=0, tm=tm_r, tn=None, name=f"ffn_up_dx{layer}",
                    comm=cm(f"ffn_up_dx{layer}"))
        plan.grad_ready("f_w_up", layer, _mm_tn(h_t, dup, shards=N_CHIPS, tk=1024, tn=1408, tt=tt,
                                                name=f"ffn_up_dw{layer}", comm=cm(f"ffn_up_dw{layer}")))
        dxin, dg = _rmsnorm_bwd(xin, nf, dh, dx, layer=layer, tm=tm_n, name=f"norm_ffn_bwd{layer}",
                                comm=cm(f"norm_ffn_bwd{layer}"))
        return dxin, dg, dcw

    dx, dnf1, dfc1 = ffn_bwd(dx, x3, h_f1_t, up1, upc1, a1_t, 1)
    dr = _mm_nt(dx, wt("c_w_out", 0), layer=0, tm=tm, tn=512, out_dtype=BF16, name="c_out_dx", comm=cm("c_out_dx"))
    plan.grad_ready("c_w_out", 0, _mm_tn(r_t, dx, shards=None, tk=1024, tn=1024, tt=tt, name="c_out_dw",
                                         comm=cm("c_out_dw")))
    dz_c, dccw = _mixer_c_bwd(z_c, dr, small["c_conv_w"], tm=tm_e, name="mixer_c_bwd", comm=cm("mixer_c_bwd"))
    sg["c_conv_w"] = dccw.reshape(1, C_CONV, D_MODEL)
    dh = _mm_nt(dz_c, wt("c_w_in", 0), layer=0, tm=tm_r, tn=None, name="c_in_dx", comm=cm("c_in_dx"))
    plan.grad_ready("c_w_in", 0, _mm_tn(h_m1_t, dz_c, shards=N_CHIPS, tk=1024, tn=768, tt=tt, name="c_in_dw",
                                        comm=cm("c_in_dw")))
    dx, dnm1 = _rmsnorm_bwd(x2, nm, dh, dx, layer=1, tm=tm_n, name="norm_mix_bwd1", comm=cm("norm_mix_bwd1"))
    dx, dnf0, dfc0 = ffn_bwd(dx, x1, h_f0_t, up0, upc0, a0_t, 0)
    dyab = _mm_nt(dx, wt("ab_w_out", 0), layer=0, tm=tm, tn=512, out_dtype=BF16, name="ab_out_dx",
                  comm=cm("ab_out_dx"))
    plan.grad_ready("ab_w_out", 0, _mm_tn(yab_t, dx, shards=None, tk=1024, tn=1024, tt=tt, name="ab_out_dw",
                                          comm=cm("ab_out_dw")))
    (dza, dcb, sg["a_ln_g"], sg["a_ln_b"], dws, dbs, sg["b_ln_g"], sg["b_ln_b"]) = _mixer_ab_bwd_pre(
        z_ab, cb, dyab, small["a_ln_g"], small["a_ln_b"], w_s, b_s, small["b_ln_g"], small["b_ln_b"],
        tm=tm_e, name="mixer_ab_bwd", comm=cm("mixer_ab_bwd"))
    dz_ab, dbcw, sg["b_conv_b"] = _mixer_b_conv_bwd(z_ab, dcb, b_conv_w, dza, tm=tm_e, name="mixer_b_conv_bwd",
                                                    comm=cm("mixer_b_conv_bwd"))
    sg["a_w_s"] = dws.reshape(1, A_HEADS, CHUNK, CHUNK)
    sg["a_b_s"] = dbs.reshape(1, A_HEADS, CHUNK)
    sg["b_conv_w"] = dbcw.reshape(1, B_CONV, D_B)
    plan.grad_ready("ab_w_in", 0, _mm_tn(h_m0_t, dz_ab, shards=N_CHIPS, tk=1024, tn=512, tt=tt, name="ab_in_dw",
                                         comm=cm("ab_in_dw")))
    dh = _mm_nt(dz_ab, wt("ab_w_in", 0), layer=0, tm=tm_r, tn=None, name="ab_in_dx", comm=cm("ab_in_dx"))
    dx, dnm0 = _rmsnorm_bwd(x, nm, dh, dx, layer=0, tm=tm_n, name="norm_mix_bwd0", comm=cm("norm_mix_bwd0"))

    sg["norm_mix"] = [dnm0, dnm1]
    sg["norm_ffn"] = [dnf0, dnf1]
    sg["f_conv_w"] = [dfc0, dfc1]
    return loss, dx, sg


BLOCK_BYTES = 3 * 1024 * 1024


BF16_SUBLANES = 16


def _row_tile(rows, row_bytes, step=SUBLANES):
    best = None
    for tr in range(step, rows + 1, step):
        if rows % tr == 0 and tr * row_bytes <= BLOCK_BYTES:
            best = tr
    if best is None:
        raise ValueError(f"no row tile for {rows}")
    return best


def _place_scalars():
    x, y, c = lax.axis_index("x"), lax.axis_index("y"), lax.axis_index("c")
    return jnp.stack([c, 2 * x + y, 2 * (1 - x) + y, 2 * x + (1 - y), 2 * (1 - x) + (1 - y)]).astype(jnp.int32)


def _cast_into_slot(w, place, *, layer, name):
    L, rows, cols = w.shape
    tr = _row_tile(rows, cols * 4, BF16_SUBLANES)

    def body(place_ref, w_ref, o_ref):
        o_ref[...] = w_ref[...].astype(BF16)

    return pl.pallas_call(
        body, name=name,
        grid_spec=pltpu.PrefetchScalarGridSpec(
            num_scalar_prefetch=1, grid=(rows // tr,),
            in_specs=[pl.BlockSpec((None, tr, cols), lambda i, p: (layer, i, 0))],
            out_specs=pl.BlockSpec((None, None, tr, cols), lambda i, p: (0, p[1], i, 0))),
        out_shape=jax.ShapeDtypeStruct((1, N_CHIPS, rows, cols), BF16),
        compiler_params=_cparams(("parallel",), 32),
    )(place, w)


def _pair_sum(g, theirs, place, *, name):
    S, rows, cols = g.shape
    half = rows // 2
    tr = _row_tile(half, cols * 4, BF16_SUBLANES)
    nb = half // tr

    def body(place_ref, g_ref, t_ref, o_ref):
        o_ref[...] = (g_ref[...] + t_ref[...]).astype(BF16)

    spec = pl.BlockSpec((None, tr, cols), lambda s, i, p: (s, i, 0))
    return pl.pallas_call(
        body, name=name,
        grid_spec=pltpu.PrefetchScalarGridSpec(
            num_scalar_prefetch=1, grid=(S, nb),
            in_specs=[pl.BlockSpec((None, tr, cols), lambda s, i, p: (s, p[0] * nb + i, 0)), spec],
            out_specs=spec),
        out_shape=jax.ShapeDtypeStruct((S, half, cols), BF16),
        compiler_params=_cparams(("parallel", "parallel"), 32),
    )(place, g, theirs)


def _chip_sum(p, r, g_prev, place, *, layer, shape, name):
    L, rows, cols = shape
    half = rows // 2
    tr = _row_tile(half, cols * 4, BF16_SUBLANES)
    nb = half // tr

    def body(place_ref, p_ref, r_ref, *rest):
        o_ref = rest[-1]
        mine = p_ref[...].astype(F32)
        peers = [r_ref[j].astype(F32) for j in range(3)]
        acc = None
        for s in range(N_CHIPS):
            term = jnp.where(place_ref[1] == s, mine,
                             jnp.where(place_ref[2] == s, peers[0],
                                       jnp.where(place_ref[3] == s, peers[1], peers[2])))
            acc = term if acc is None else acc + term
        o_ref[...] = acc

    in_specs = [pl.BlockSpec((None, tr, cols), lambda i, pr: (pr[1], i, 0)),
                pl.BlockSpec((3, tr, cols), lambda i, pr: (0, i, 0))]
    args = [place, p, r]
    aliases = {}
    if g_prev is not None:
        in_specs.append(ANY)
        args.append(g_prev)
        aliases = {3: 0}
    return pl.pallas_call(
        body, name=name,
        grid_spec=pltpu.PrefetchScalarGridSpec(
            num_scalar_prefetch=1, grid=(nb,), in_specs=in_specs,
            out_specs=pl.BlockSpec((None, tr, cols), lambda i, pr: (layer, pr[0] * nb + i, 0))),
        out_shape=jax.ShapeDtypeStruct(shape, F32), input_output_aliases=aliases,
        compiler_params=_cparams(("parallel",), 32),
    )(*args)


def _adamw_math(w, g, m, v):
    m2 = ADAM_B1 * m + (1.0 - ADAM_B1) * g
    v2 = ADAM_B2 * v + (1.0 - ADAM_B2) * (g * g)
    m_hat = m2 / (1.0 - ADAM_B1 ** ADAM_STEP)
    v_hat = v2 / (1.0 - ADAM_B2 ** ADAM_STEP)
    delta = -ADAM_LR * (m_hat / (jnp.sqrt(v_hat) + ADAM_EPS) + ADAM_WD * w)
    return delta, m2, v2


def _adamw(w, g, m, v, *, name):
    L, rows, cols = w.shape
    tr = _row_tile(rows, cols * 4)

    def body(w_ref, g_ref, m_ref, v_ref, d_ref, m2_ref, v2_ref):
        d, m2, v2 = _adamw_math(w_ref[...], g_ref[...], m_ref[...], v_ref[...])
        d_ref[...] = d
        m2_ref[...] = m2
        v2_ref[...] = v2

    spec = pl.BlockSpec((None, tr, cols), lambda l, i: (l, i, 0))
    shape = jax.ShapeDtypeStruct(w.shape, F32)
    return pl.pallas_call(
        body, name=name, grid=(L, rows // tr), in_specs=[spec] * 4, out_specs=[spec] * 3,
        out_shape=[shape] * 3,
        compiler_params=_cparams(("parallel", "parallel"), 48),
    )(w, g, m, v)


def _exchange_packs(pack, *, reduce, name):
    R = pack.shape[0]
    ndev = 2 * N_CHIPS

    def body(p_ref, o_ref, *scratch):
        if reduce:
            buf, send, recv = scratch
        else:
            buf = o_ref
            send, recv = scratch
        x, y, c = lax.axis_index("x"), lax.axis_index("y"), lax.axis_index("c")
        me = 4 * x + 2 * y + c
        buf[me] = p_ref[...]
        sends = []
        for q in range(1, ndev):
            qx, qy, qc = (q >> 2) & 1, (q >> 1) & 1, q & 1
            peer = (x ^ qx, y ^ qy, c ^ qc)
            rc = _remote(p_ref, buf.at[me], send.at[q - 1], recv.at[q - 1], peer)
            rc.start()
            sends.append(rc)
        for q in range(1, ndev):
            qx, qy, qc = (q >> 2) & 1, (q >> 1) & 1, q & 1
            slot = buf.at[4 * (x ^ qx) + 2 * (y ^ qy) + (c ^ qc)]
            _remote(slot, slot, send.at[q - 1], recv.at[q - 1], (x ^ qx, y ^ qy, c ^ qc)).wait_recv()
        for rc in sends:
            rc.wait_send()
        if reduce:
            acc = buf[0]
            for d in range(1, ndev):
                acc = acc + buf[d]
            o_ref[...] = acc

    vm = pl.BlockSpec(memory_space=pltpu.VMEM)
    sems = [pltpu.SemaphoreType.DMA((ndev - 1,)), pltpu.SemaphoreType.DMA((ndev - 1,))]
    if reduce:
        out_shape = jax.ShapeDtypeStruct((R, LANES), F32)
        scratch = [pltpu.VMEM((ndev, R, LANES), F32)] + sems
    else:
        out_shape = jax.ShapeDtypeStruct((ndev, R, LANES), F32)
        scratch = sems
    return pl.pallas_call(
        body, name=name, in_specs=[vm], out_specs=vm, out_shape=out_shape, scratch_shapes=scratch,
        compiler_params=pltpu.CompilerParams(vmem_limit_bytes=VMEM_BYTES_MAX),
    )(pack)


PACK_UNIT = SUBLANES * LANES


def _pack(arrays):
    flat, sizes = [], []
    for a in arrays:
        pieces = a if isinstance(a, (list, tuple)) else [a]
        v = jnp.concatenate([p.reshape(-1) for p in pieces]) if len(pieces) > 1 else pieces[0].reshape(-1)
        size = v.shape[0]
        padded = -(-size // PACK_UNIT) * PACK_UNIT
        flat.append(jnp.pad(v, (0, padded - size)))
        sizes.append((size, padded))
    return jnp.concatenate(flat).reshape(-1, LANES), sizes


def _unpack(pack, sizes, shapes):
    v = pack.reshape(-1)
    out, off = [], 0
    for (size, padded), shape in zip(sizes, shapes):
        out.append(v[off:off + size].reshape(shape))
        off += padded
    return out


BIG = ("ab_w_in", "ab_w_out", "c_w_in", "c_w_out", "f_w_up", "f_w_down")
COL_SHARDED = ("ab_w_in", "c_w_in", "f_w_up")
SMALL_REPLICATED = ("norm_mix", "norm_ffn", "norm_final", "a_ln_g", "a_ln_b", "a_w_s", "a_b_s",
                    "b_conv_b", "b_ln_g", "b_ln_b")
SMALL_SHARDED = ("b_conv_w", "c_conv_w", "f_conv_w")
SMALL = SMALL_REPLICATED + SMALL_SHARDED
ALL_WEIGHTS = ("norm_mix", "norm_ffn", "norm_final", "ab_w_in", "a_ln_g", "a_ln_b", "a_w_s", "a_b_s",
               "b_conv_w", "b_conv_b", "b_ln_g", "b_ln_b", "ab_w_out", "c_w_in", "c_conv_w", "c_w_out",
               "f_w_up", "f_conv_w", "f_w_down")


SCHEDULE = {
    "ab_in": [("gi", "f_w_up", 0, 0, 4), ("gi", "ab_w_out", 0)],
    "mixer_ab": [("gd", "f_w_up", 0, 0, 4), ("gd", "ab_w_out", 0), ("gi", "f_w_up", 0, 1, 4),
                 ("gi", "f_w_up", 0, 2, 4), ("gi", "f_w_up", 0, 3, 4)],
    "ab_out": [("gd", "f_w_up", 0, 1, 4), ("gd", "f_w_up", 0, 2, 4), ("gd", "f_w_up", 0, 3, 4)],
    "ffn_up0": [("gi", "f_w_down", 0), ("gi", "c_w_in", 0, 0, 2)],
    "ffn_act0": [("gd", "f_w_down", 0), ("gd", "c_w_in", 0, 0, 2), ("gi", "c_w_in", 0, 1, 2),
                 ("gi", "f_w_up", 1, 0, 4), ("gi", "f_w_up", 1, 1, 4)],
    "ffn_down0": [("gd", "c_w_in", 0, 1, 2), ("gd", "f_w_up", 1, 0, 4), ("gd", "f_w_up", 1, 1, 4),
                  ("gi", "f_w_up", 1, 2, 4)],
    "c_in": [("gd", "f_w_up", 1, 2, 4), ("gi", "f_w_up", 1, 3, 4), ("gi", "c_w_out", 0)],
    "mixer_c": [("gd", "f_w_up", 1, 3, 4), ("gd", "c_w_out", 0), ("gi", "f_w_down", 1, 0, 2)],
    "c_out": [("gd", "f_w_down", 1, 0, 2), ("gi", "f_w_down", 1, 1, 2)],
    "ffn_up1": [("gd", "f_w_down", 1, 1, 2)],
    "ffn_act_bwd1": [("px", "f_w_down", 1)],
    "ffn_up_dx1": [("cx", "f_w_down", 1)],
    "mixer_c_bwd": [("px", "f_w_up", 1), ("px", "c_w_out", 0)],
    "c_in_dx": [("cx", "f_w_up", 1, 0, 2)],
    "c_in_dw": [("cx", "f_w_up", 1, 1, 2), ("cx", "c_w_out", 0)],
    "norm_mix_bwd1": [("px", "c_w_in", 0)],
    "ffn_down_dx0": [("cx", "c_w_in", 0, 0, 2)],
    "ffn_down_dw0": [("cx", "c_w_in", 0, 1, 2)],
    "ffn_act_bwd0": [("px", "f_w_down", 0)],
    "ffn_up_dx0": [("cx", "f_w_down", 0)],
    "mixer_ab_bwd": [("px", "f_w_up", 0), ("px", "ab_w_out", 0)],
    "mixer_b_conv_bwd": [("cx", "f_w_up", 0, 0, 2), ("cx", "ab_w_out", 0)],
    "ab_in_dw": [("cx", "f_w_up", 0, 1, 2)],
    "ab_in_dx": [("px", "ab_w_in", 0)],
    "norm_mix_bwd0": [("cx", "ab_w_in", 0)],
}


class _Plan:
    def __init__(self, shapes, place):
        self.shapes, self.place, self.bufs = shapes, place, {}

    def weight(self, name, layer):
        g = self.bufs[f"w:{name}:{layer}"]
        if name in COL_SHARDED:
            return g
        _, S, rows, cols = g.shape
        return g.reshape(1, S * rows, cols)

    def grad_ready(self, name, layer, g):
        _, rows, cols = self.shapes[name]
        hbm = lambda a: pltpu.with_memory_space_constraint(a, pltpu.HBM)
        self.bufs[f"g:{name}:{layer}"] = g.reshape(N_CHIPS, rows, cols)
        self.bufs[f"t:{name}:{layer}"] = hbm(lax.empty((N_CHIPS, rows // 2, cols), F32))
        self.bufs[f"l:{name}:{layer}"] = hbm(lax.empty((3, rows // 2, cols), BF16))

    def job(self, kind, name, layer, part=0, parts=1):
        _, rows, cols = self.shapes[name]
        key = f"{name}:{layer}"
        if kind == "gi":
            return _job_gather_ici("w:" + key, rows, part, parts)
        if kind == "gd":
            return _job_gather_d2d("w:" + key, rows, part, parts)
        if kind == "px":
            return _job_pair_exchange("g:" + key, "t:" + key, rows)
        if kind == "cx":
            if "p:" + key not in self.bufs:
                self.bufs["p:" + key] = _pair_sum(self.bufs["g:" + key], self.bufs["t:" + key], self.place,
                                                  name=f"pair_sum_{name}{layer}")
            nr = rows // 2 // parts
            return _job_chip_exchange("p:" + key, "l:" + key, part * nr, nr)
        if kind == "ps":
            return _job_pair_share("G:" + name, layer, rows)
        raise ValueError(kind)

    def comm(self, call):
        specs = SCHEDULE.get(call)
        return None if specs is None else _Comm(self, [self.job(*spec) for spec in specs])


def _step(x, tgt, w, m, v):
    chip = 2 * lax.axis_index("x") + lax.axis_index("y")
    place = _place_scalars()
    plan = _Plan({n: w[n].shape for n in BIG}, place)
    items = [(n, l) for n in BIG for l in range(w[n].shape[0])]

    for n, l in items:
        plan.bufs[f"w:{n}:{l}"] = _cast_into_slot(w[n], place, layer=l, name=f"cast_{n}{l}")
    _comm_only(plan, [[plan.job("gi", "ab_w_in", 0)], [plan.job("gd", "ab_w_in", 0)]], name="gather_first")
    conv_pack, conv_sizes = _pack([w[n] for n in SMALL_SHARDED])
    conv_all = _exchange_packs(conv_pack, reduce=False, name="gather_conv_weights")
    conv_shapes = [w[n].shape for n in SMALL_SHARDED]
    per_chip = [_unpack(conv_all[2 * s], conv_sizes, conv_shapes) for s in range(N_CHIPS)]
    small = {n: w[n] for n in SMALL_REPLICATED}
    for idx, n in enumerate(SMALL_SHARDED):
        small[n] = jnp.concatenate([per_chip[s][idx] for s in range(N_CHIPS)], axis=-1)

    loss, dx, sg = _local_step(x, tgt, small, plan)

    for n, l in items:
        plan.bufs["G:" + n] = _chip_sum(plan.bufs[f"p:{n}:{l}"], plan.bufs[f"l:{n}:{l}"], plan.bufs.get("G:" + n),
                                        place, layer=l, shape=w[n].shape, name=f"chip_sum_{n}{l}")
    _comm_only(plan, [[plan.job("ps", n, l) for n, l in items]], name="reduce_pair_share")
    grads_big = [plan.bufs["G:" + n] for n in BIG]

    g_pack, g_sizes = _pack([sg[n] for n in SMALL])
    g_sum = _exchange_packs(g_pack, reduce=True, name="allreduce_small_grads")
    full_shapes = [small[n].shape for n in SMALL]
    g_small = dict(zip(SMALL, _unpack(g_sum, g_sizes, full_shapes)))
    for n in SMALL_SHARDED:
        width = w[n].shape[-1]
        g_small[n] = lax.dynamic_slice_in_dim(g_small[n], chip * width, width, axis=g_small[n].ndim - 1)

    grad, delta, new_m, new_v = {}, {}, {}, {}
    for n, g in zip(BIG, grads_big):
        grad[n] = g
        delta[n], new_m[n], new_v[n] = _adamw(w[n], g, m[n], v[n], name=f"adamw_{n}")
    shapes = [w[n].shape for n in SMALL]
    wp, sizes = _pack([w[n] for n in SMALL])
    gp, _ = _pack([g_small[n] for n in SMALL])
    mp, _ = _pack([m[n] for n in SMALL])
    vp, _ = _pack([v[n] for n in SMALL])
    R = wp.shape[0]
    dp, m2p, v2p = _adamw(wp.reshape(1, R, LANES), gp.reshape(1, R, LANES), mp.reshape(1, R, LANES),
                          vp.reshape(1, R, LANES), name="adamw_small")
    for n, d_, m_, v_ in zip(SMALL, _unpack(dp, sizes, shapes), _unpack(m2p, sizes, shapes),
                             _unpack(v2p, sizes, shapes)):
        grad[n] = g_small[n]
        delta[n], new_m[n], new_v[n] = d_, m_, v_
    return loss, dx, grad, delta, new_m, new_v


def kernel(x, norm_mix, norm_ffn, norm_final, ab_w_in, a_ln_g, a_ln_b, a_w_s, a_b_s, b_conv_w, b_conv_b, b_ln_g, b_ln_b, ab_w_out, c_w_in, c_conv_w, c_w_out, f_w_up, f_conv_w, f_w_down, loss_target, m_norm_mix, m_norm_ffn, m_norm_final, m_ab_w_in, m_a_ln_g, m_a_ln_b, m_a_w_s, m_a_b_s, m_b_conv_w, m_b_conv_b, m_b_ln_g, m_b_ln_b, m_ab_w_out, m_c_w_in, m_c_conv_w, m_c_w_out, m_f_w_up, m_f_conv_w, m_f_w_down, v_norm_mix, v_norm_ffn, v_norm_final, v_ab_w_in, v_a_ln_g, v_a_ln_b, v_a_w_s, v_a_b_s, v_b_conv_w, v_b_conv_b, v_b_ln_g, v_b_ln_b, v_ab_w_out, v_c_w_in, v_c_conv_w, v_c_w_out, v_f_w_up, v_f_conv_w, v_f_w_down):
    given = dict(locals())
    w = {n: given[n] for n in ALL_WEIGHTS}
    m = {n: given["m_" + n] for n in ALL_WEIGHTS}
    v = {n: given["v_" + n] for n in ALL_WEIGHTS}
    T = x.shape[1]
    loss, dx, grad, delta, new_m, new_v = _step(x.reshape(T, D_MODEL), loss_target.reshape(T, D_MODEL), w, m, v)
    loss = lax.psum(loss[0, 0], ("x", "y", "c"))
    out = [loss, dx.reshape(x.shape)]
    for d in (grad, delta, new_m, new_v):
        out += [d[n] for n in ALL_WEIGHTS]
    return tuple(out)
```

```python
import functools
import math

import jax
import jax.numpy as jnp
from jax import lax
from jax.experimental import pallas as pl
from jax.experimental.pallas import tpu as pltpu

F32 = jnp.float32
BF16 = jnp.bfloat16

EPS = 1e-6
D_MODEL = 1024
CHUNK = 128
HEAD_DIM = 128
A_HEADS = 4
D_A = 512
D_B = 512
B_CONV = 31
C_CONV = 3
D_FF = 2816
F_CONV = 3
N_CHIPS = 4

ADAM_LR = 0.001
ADAM_B1 = 0.9
ADAM_B2 = 0.999
ADAM_EPS = 1e-08
ADAM_WD = 0.01
ADAM_STEP = 10

SUBLANES = 8
LANES = 128
HALO_SHORT = 16
HALO_LONG = 32
VMEM_BYTES_MAX = 60000 * 1024

INV_SQRT2 = 1.0 / math.sqrt(2.0)
INV_SQRT_2PI = 1.0 / math.sqrt(2.0 * math.pi)

MESH = pl.DeviceIdType.MESH


def _cparams(sem, vmem_mb):
    del vmem_mb
    return pltpu.CompilerParams(dimension_semantics=sem, vmem_limit_bytes=VMEM_BYTES_MAX)


def _pick(total, pref):
    for c in (2048, 1024, 512, 256, 128):
        if c <= pref and total % c == 0:
            return c
    raise ValueError(f"no tile for {total}")


def _sigmoid(x):
    return jax.nn.sigmoid(x)


def _silu(x):
    return x * _sigmoid(x)


def _dsilu(x):
    s = _sigmoid(x)
    return s * (1.0 + x * (1.0 - s))


def _gelu(x):
    return 0.5 * x * (1.0 + lax.erf(x * INV_SQRT2))


def _dgelu(x):
    return 0.5 * (1.0 + lax.erf(x * INV_SQRT2)) + x * jnp.exp(-0.5 * x * x) * INV_SQRT_2PI


def _ln_stats(x):
    mu = jnp.mean(x, axis=-1, keepdims=True)
    xc = x - mu
    var = jnp.mean(xc * xc, axis=-1, keepdims=True)
    r = lax.rsqrt(var + EPS)
    return xc * r, r


def _ln_bwd(dy, xh, r, g):
    dxh = dy * g
    m1 = jnp.mean(dxh, axis=-1, keepdims=True)
    m2 = jnp.mean(dxh * xh, axis=-1, keepdims=True)
    return r * (dxh - m1 - xh * m2)


def _rowsum(x):
    return jnp.sum(x, axis=0, keepdims=True)


ANY = pl.BlockSpec(memory_space=pltpu.HBM)


def _place():
    x, y, c = lax.axis_index("x"), lax.axis_index("y"), lax.axis_index("c")
    peers = [(1 - x, y), (x, 1 - y), (1 - x, 1 - y)]
    return x, y, c, 2 * x + y, (x, y, 1 - c), peers


def _half(rows, which):
    return pl.ds(which * (rows // 2), rows // 2)


def _remote(src, dst, send_sem, recv_sem, device):
    return pltpu.make_async_remote_copy(src_ref=src, dst_ref=dst, send_sem=send_sem, recv_sem=recv_sem,
                                        device_id=device, device_id_type=MESH)


class _Job:
    def __init__(self, reads, writes, ncopies, copies):
        self.reads, self.writes, self.ncopies, self.copies = reads, writes, ncopies, copies


def _share(rows, which, part, parts):
    nr = rows // 2 // parts
    return pl.ds(which * (rows // 2) + part * nr, nr)


def _job_gather_ici(name, rows, part, parts):
    def copies(src, dst, sem):
        x, y, c, k, sib, peers = _place()
        mine_rows = _share(rows, c, part, parts)
        out = []
        for j, (px, py) in enumerate(peers):
            mine = src[name].at[0, k, mine_rows]
            out.append((_remote(mine, dst[name].at[0, k, mine_rows], sem(j, 0), sem(j, 1), (px, py, c)),
                        _remote(mine, dst[name].at[0, 2 * px + py, mine_rows], sem(j, 0), sem(j, 1), (px, py, c))))
        return out
    return _Job([], [name], 3, copies)


def _job_gather_d2d(name, rows, part, parts):
    def copies(src, dst, sem):
        x, y, c, k, sib, peers = _place()
        out = []
        for j, (px, py) in enumerate(peers):
            landed = src[name].at[0, 2 * px + py, _share(rows, c, part, parts)]
            out.append((_remote(landed, dst[name].at[0, 2 * px + py, _share(rows, c, part, parts)],
                                sem(j, 0), sem(j, 1), sib),
                        _remote(landed, dst[name].at[0, 2 * px + py, _share(rows, 1 - c, part, parts)],
                                sem(j, 0), sem(j, 1), sib)))
        return out
    return _Job([], [name], 3, copies)


def _job_pair_exchange(gname, tname, rows):
    def copies(src, dst, sem):
        x, y, c, k, sib, peers = _place()
        cp = _remote(src[gname].at[:, _half(rows, 1 - c), :], dst[tname], sem(0, 0), sem(0, 1), sib)
        return [(cp, cp)]
    return _Job([gname], [tname], 1, copies)


def _job_chip_exchange(pname, lname, r0, nr):
    def copies(src, dst, sem):
        x, y, c, k, sib, peers = _place()
        out = []
        for j, (px, py) in enumerate(peers):
            cp = _remote(src[pname].at[2 * px + py, pl.ds(r0, nr)], dst[lname].at[j, pl.ds(r0, nr)],
                         sem(j, 0), sem(j, 1), (px, py, c))
            out.append((cp, cp))
        return out
    return _Job([pname], [lname], 3, copies)


def _job_pair_share(name, layer, rows):
    def copies(src, dst, sem):
        x, y, c, k, sib, peers = _place()
        mine = src[name].at[layer, _half(rows, c)]
        return [(_remote(mine, dst[name].at[layer, _half(rows, c)], sem(0, 0), sem(0, 1), sib),
                 _remote(mine, dst[name].at[layer, _half(rows, 1 - c)], sem(0, 0), sem(0, 1), sib))]
    return _Job([], [name], 1, copies)


class _Comm:
    def __init__(self, plan, jobs):
        self.plan, self.jobs = plan, jobs
        self.writes, self.reads = [], []
        for job in jobs:
            for n in job.writes:
                if n not in self.writes:
                    self.writes.append(n)
        for job in jobs:
            for n in job.reads:
                if n not in self.writes and n not in self.reads:
                    self.reads.append(n)
        self.ncopies = sum(job.ncopies for job in jobs)

    def descriptors(self, src, dst, sems, base):
        out = []
        for job in self.jobs:
            sem = lambda j, which, base=base: sems.at[base + j, which]
            out += job.copies(src, dst, sem)
            base += job.ncopies
        return out

    def start(self, src, dst, sems, base=0):
        for first, _ in self.descriptors(src, dst, sems, base):
            first.start()

    def finish(self, src, dst, sems, base=0):
        for _, landed in self.descriptors(src, dst, sems, base):
            landed.wait()


def _comm_operands(comm):
    bufs = comm.plan.bufs
    shapes = [jax.ShapeDtypeStruct(bufs[n].shape, bufs[n].dtype) for n in comm.writes]
    return [bufs[n] for n in comm.reads] + [bufs[n] for n in comm.writes], shapes


def _pallas(comm, body, *, name, grid, in_specs, out_specs, out_shape, compiler_params, scratch_shapes=(),
            aliases=None):
    aliases = dict(aliases or {})
    if comm is None:
        return pl.pallas_call(body, name=name, grid=grid, in_specs=in_specs, out_specs=out_specs,
                              out_shape=out_shape, scratch_shapes=list(scratch_shapes),
                              input_output_aliases=aliases, compiler_params=compiler_params)
    single = not isinstance(out_shape, (list, tuple))
    base_specs = [out_specs] if single else list(out_specs)
    base_shape = [out_shape] if single else list(out_shape)
    nb, nr, nw, nbo, nsc = len(in_specs), len(comm.reads), len(comm.writes), len(base_specs), len(scratch_shapes)

    def wrapped(*refs):
        base_in, rd, wr_in = refs[:nb], refs[nb:nb + nr], refs[nb + nr:nb + nr + nw]
        o0 = nb + nr + nw
        base_out, wr_out = refs[o0:o0 + nbo], refs[o0 + nbo:o0 + nbo + nw]
        scratch, sems = refs[o0 + nbo + nw:o0 + nbo + nw + nsc], refs[-1]
        src = dict(zip(comm.reads, rd))
        src.update(zip(comm.writes, wr_in))
        dst = dict(zip(comm.writes, wr_out))
        first = functools.reduce(jnp.logical_and, [pl.program_id(a) == 0 for a in range(len(grid))])
        last = functools.reduce(jnp.logical_and,
                                [pl.program_id(a) == pl.num_programs(a) - 1 for a in range(len(grid))])

        @pl.when(first)
        def _():
            comm.start(src, dst, sems)
        body(*base_in, *base_out, *scratch)

        @pl.when(last)
        def _():
            comm.finish(src, dst, sems)

    operands, shapes = _comm_operands(comm)
    call = pl.pallas_call(
        wrapped, name=name, grid=grid, in_specs=list(in_specs) + [ANY] * (nr + nw),
        out_specs=base_specs + [ANY] * nw, out_shape=base_shape + shapes,
        input_output_aliases={**aliases, **{nb + nr + q: nbo + q for q in range(nw)}},
        scratch_shapes=list(scratch_shapes) + [pltpu.SemaphoreType.DMA((comm.ncopies, 2))],
        compiler_params=compiler_params)

    def run(*args):
        outs = call(*args, *operands)
        for q, n in enumerate(comm.writes):
            comm.plan.bufs[n] = outs[nbo + q]
        return outs[0] if single else list(outs[:nbo])

    return run


def _comm_only(plan, phases, *, name):
    comms = [_Comm(plan, jobs) for jobs in phases]
    both = _Comm(plan, [job for jobs in phases for job in jobs])
    nr, nw = len(both.reads), len(both.writes)

    def body(*refs):
        rd, wr_in, wr_out, sems = refs[:nr], refs[nr:nr + nw], refs[nr + nw:nr + 2 * nw], refs[-1]
        src = dict(zip(both.reads, rd))
        src.update(zip(both.writes, wr_in))
        dst = dict(zip(both.writes, wr_out))
        base = 0
        for comm in comms:
            comm.start(src, dst, sems, base)
            comm.finish(src, dst, sems, base)
            base += comm.ncopies

    operands, shapes = _comm_operands(both)
    outs = pl.pallas_call(
        body, name=name, in_specs=[ANY] * (nr + nw), out_specs=[ANY] * nw, out_shape=shapes,
        input_output_aliases={nr + q: q for q in range(nw)},
        scratch_shapes=[pltpu.SemaphoreType.DMA((both.ncopies, 2))],
    )(*operands)
    for q, n in enumerate(both.writes):
        plan.bufs[n] = outs[q]


def _mm_nn(a, w, *, layer, tm, tn, residual=None, norm=None, out_dtype=F32, name, comm=None):
    T, K = a.shape
    if w.ndim == 4:
        _, S, _, n4 = w.shape
        N = S * n4
        bps = n4 // tn
        w_spec = pl.BlockSpec((None, None, K, tn), lambda j, i: (layer, j // bps, 0, j % bps))
    else:
        N = w.shape[2]
        w_spec = pl.BlockSpec((None, K, tn), lambda j, i: (layer, 0, j))
    in_specs = [pl.BlockSpec((tm, K), lambda j, i: (i, 0)), w_spec]
    args = [a, w]
    if residual is not None:
        in_specs.append(pl.BlockSpec((tm, tn), lambda j, i: (i, j)))
        args.append(residual)
    out_specs = pl.BlockSpec((tm, tn), lambda j, i: (i, j))
    out_shape = jax.ShapeDtypeStruct((T, N), out_dtype)
    if norm is not None:
        assert tn == N
        g, norm_layer = norm
        in_specs.append(pl.BlockSpec((None, 1, N), lambda j, i: (norm_layer, 0, 0)))
        args.append(g)
        out_specs = [out_specs, pl.BlockSpec((tm, tn), lambda j, i: (i, j))]
        out_shape = [out_shape, jax.ShapeDtypeStruct((T, N), BF16)]

    def body(*refs):
        a_ref, w_ref = refs[0], refs[1]
        acc = jnp.dot(a_ref[...].astype(BF16), w_ref[...], preferred_element_type=F32)
        if residual is not None:
            acc = refs[2][...] + acc
        if norm is None:
            refs[-1][...] = acc.astype(out_dtype)
        else:
            refs[-2][...] = acc.astype(out_dtype)
            r = lax.rsqrt(jnp.mean(acc * acc, axis=-1, keepdims=True) + EPS)
            refs[-1][...] = (acc * r * refs[-3][...]).astype(BF16)

    return _pallas(
        comm, body, name=name, grid=(N // tn, T // tm), in_specs=in_specs,
        out_specs=out_specs, out_shape=out_shape,
        compiler_params=_cparams(("parallel", "parallel"), 48),
    )(*args)


def _mm_nt(dy, w, *, layer, tm, tn, name, out_dtype=F32, comm=None):
    T = dy.shape[0]
    nt_dims = (((1,), (1,)), ((), ()))
    if w.ndim == 4:
        _, S, K, n4 = w.shape

        def body(dy_ref, w_ref, o_ref):
            @pl.when(pl.program_id(1) == 0)
            def _():
                o_ref[...] = jnp.zeros_like(o_ref)
            o_ref[...] += lax.dot_general(dy_ref[...].astype(BF16), w_ref[...], nt_dims,
                                          preferred_element_type=F32)

        return _pallas(
            comm, body, name=name, grid=(T // tm, S),
            in_specs=[pl.BlockSpec((tm, n4), lambda i, s: (i, s)),
                      pl.BlockSpec((None, None, K, n4), lambda i, s: (layer, s, 0, 0))],
            out_specs=pl.BlockSpec((tm, K), lambda i, s: (i, 0)),
            out_shape=jax.ShapeDtypeStruct((T, K), F32),
            compiler_params=_cparams(("parallel", "arbitrary"), 48),
        )(dy, w)
    _, R, N = w.shape

    def body2(dy_ref, w_ref, o_ref):
        o_ref[...] = lax.dot_general(dy_ref[...].astype(BF16), w_ref[...], nt_dims,
                                     preferred_element_type=F32).astype(out_dtype)

    return _pallas(
        comm, body2, name=name, grid=(R // tn, T // tm),
        in_specs=[pl.BlockSpec((tm, N), lambda j, i: (i, 0)),
                  pl.BlockSpec((None, tn, N), lambda j, i: (layer, j, 0))],
        out_specs=pl.BlockSpec((tm, tn), lambda j, i: (i, j)),
        out_shape=jax.ShapeDtypeStruct((T, R), out_dtype),
        compiler_params=_cparams(("parallel", "parallel"), 48),
    )(dy, w)


def _mm_tn(a, dy, *, shards, tk, tn, tt, name, comm=None):
    T, K = a.shape
    N = dy.shape[1]
    tn_dims = (((0,), (0,)), ((), ()))

    def body(a_ref, dy_ref, o_ref):
        @pl.when(pl.program_id(2) == 0)
        def _():
            o_ref[...] = jnp.zeros_like(o_ref)
        o_ref[...] += lax.dot_general(a_ref[...].astype(BF16), dy_ref[...].astype(BF16), tn_dims,
                                      preferred_element_type=F32)

    if shards is None:
        out_spec = pl.BlockSpec((tk, tn), lambda k, n, t: (k, n))
        out_shape = jax.ShapeDtypeStruct((K, N), F32)
    else:
        n4 = N // shards
        bps = n4 // tn
        out_spec = pl.BlockSpec((None, tk, tn), lambda k, n, t: (n // bps, k, n % bps))
        out_shape = jax.ShapeDtypeStruct((shards, K, n4), F32)
    return _pallas(
        comm, body, name=name, grid=(K // tk, N // tn, T // tt),
        in_specs=[pl.BlockSpec((tt, tk), lambda k, n, t: (t, k)),
                  pl.BlockSpec((tt, tn), lambda k, n, t: (t, n))],
        out_specs=out_spec, out_shape=out_shape,
        compiler_params=_cparams(("parallel", "parallel", "arbitrary"), 48),
    )(a, dy)


def _rmsnorm_fwd(x, g, *, layer, tm, name, comm=None):
    T, D = x.shape

    def body(x_ref, g_ref, h_ref):
        xf = x_ref[...]
        r = lax.rsqrt(jnp.mean(xf * xf, axis=-1, keepdims=True) + EPS)
        h_ref[...] = (xf * r * g_ref[...]).astype(BF16)

    return _pallas(
        comm, body, name=name, grid=(T // tm,),
        in_specs=[pl.BlockSpec((tm, D), lambda i: (i, 0)),
                  pl.BlockSpec((None, 1, D), lambda i: (layer, 0, 0))],
        out_specs=pl.BlockSpec((tm, D), lambda i: (i, 0)),
        out_shape=jax.ShapeDtypeStruct((T, D), BF16),
        compiler_params=_cparams(("parallel",), 32),
    )(x, g)


def _rmsnorm_bwd(x, g, dh, dres, *, layer, tm, name, comm=None):
    T, D = x.shape

    def body(x_ref, g_ref, dh_ref, dres_ref, dx_ref, dg_ref):
        @pl.when(pl.program_id(0) == 0)
        def _():
            dg_ref[...] = jnp.zeros_like(dg_ref)
        xf = x_ref[...]
        r = lax.rsqrt(jnp.mean(xf * xf, axis=-1, keepdims=True) + EPS)
        xh = xf * r
        dh = dh_ref[...]
        dg_ref[...] += _rowsum(dh * xh)
        dxh = dh * g_ref[...]
        dx_ref[...] = dres_ref[...] + r * (dxh - xh * jnp.mean(dxh * xh, axis=-1, keepdims=True))

    return _pallas(
        comm, body, name=name, grid=(T // tm,),
        in_specs=[pl.BlockSpec((tm, D), lambda i: (i, 0)),
                  pl.BlockSpec((None, 1, D), lambda i: (layer, 0, 0)),
                  pl.BlockSpec((tm, D), lambda i: (i, 0)),
                  pl.BlockSpec((tm, D), lambda i: (i, 0))],
        out_specs=[pl.BlockSpec((tm, D), lambda i: (i, 0)),
                   pl.BlockSpec((1, D), lambda i: (0, 0))],
        out_shape=[jax.ShapeDtypeStruct((T, D), F32), jax.ShapeDtypeStruct((1, D), F32)],
        compiler_params=_cparams(("arbitrary",), 40),
    )(x, g, dh, dres)


def _loss_head(x, tgt, g, *, tm, name, comm=None):
    T, D = x.shape

    def body(x_ref, t_ref, g_ref, loss_ref, dx_ref, dg_ref):
        @pl.when(pl.program_id(0) == 0)
        def _():
            dg_ref[...] = jnp.zeros_like(dg_ref)
            loss_ref[...] = jnp.zeros_like(loss_ref)
        xf = x_ref[...]
        gg = g_ref[...]
        r = lax.rsqrt(jnp.mean(xf * xf, axis=-1, keepdims=True) + EPS)
        xh = xf * r
        err = xh * gg - t_ref[...]
        row = jnp.mean(err * err, axis=-1, keepdims=True)
        loss_ref[...] += 0.5 * jnp.sum(row, axis=0, keepdims=True)
        dy = err * (1.0 / D)
        dg_ref[...] += _rowsum(dy * xh)
        dxh = dy * gg
        dx_ref[...] = r * (dxh - xh * jnp.mean(dxh * xh, axis=-1, keepdims=True))

    return _pallas(
        comm, body, name=name, grid=(T // tm,),
        in_specs=[pl.BlockSpec((tm, D), lambda i: (i, 0)),
                  pl.BlockSpec((tm, D), lambda i: (i, 0)),
                  pl.BlockSpec((1, D), lambda i: (0, 0))],
        out_specs=[pl.BlockSpec((1, 1), lambda i: (0, 0)),
                   pl.BlockSpec((tm, D), lambda i: (i, 0)),
                   pl.BlockSpec((1, D), lambda i: (0, 0))],
        out_shape=[jax.ShapeDtypeStruct((1, 1), F32), jax.ShapeDtypeStruct((T, D), F32),
                   jax.ShapeDtypeStruct((1, D), F32)],
        compiler_params=_cparams(("arbitrary",), 40),
    )(x, tgt, g)


CONV_ROWS = 64
CONV_COLS = 256


def _halo_prev_index(tm, halo):
    per = tm // halo
    return lambda i: jnp.maximum(i * per - 1, 0)


def _halo_next_index(tm, halo, total):
    per = tm // halo
    last = total // halo - 1
    return lambda i: jnp.minimum((i + 1) * per, last)


def _causal_mask():
    t = lax.broadcasted_iota(jnp.int32, (CHUNK, CHUNK), 0)
    s = lax.broadcasted_iota(jnp.int32, (CHUNK, CHUNK), 1)
    return s <= t


def _mixer_ab_fwd(z, a_ln_g, a_ln_b, w_s, b_s, conv_w, conv_b, b_ln_g, b_ln_b, *, tm, name, comm=None):
    T = z.shape[0]
    nchunk = tm // CHUNK
    halo = HALO_LONG

    def body(za_ref, zb_ref, zh_ref, alg_ref, alb_ref, ws_ref, bs_ref, cw_ref, cbias_ref,
             blg_ref, blb_ref, y_ref, cb_ref, ext_ref):
        i = pl.program_id(0)
        gu = _gelu(za_ref[:, :D_A].astype(F32))
        gv = _gelu(za_ref[:, D_A:].astype(F32))
        xh, _ = _ln_stats(gv)
        lv = (xh * alg_ref[...] + alb_ref[...]).astype(BF16)
        mask = _causal_mask()
        for h in range(A_HEADS):
            wm = jnp.where(mask, ws_ref[h], 0.0).astype(BF16)
            cols = slice(h * HEAD_DIM, (h + 1) * HEAD_DIM)
            for c in range(nchunk):
                rows = slice(c * CHUNK, (c + 1) * CHUNK)
                mixed = jnp.dot(wm, lv[rows, cols], preferred_element_type=F32) + bs_ref[h]
                y_ref[rows, cols] = (gu[rows, cols] * mixed).astype(BF16)
        ext_ref[halo:halo + tm, :] = zb_ref[:, :D_B].astype(F32) * _sigmoid(zb_ref[:, D_B:].astype(F32))
        prev = zh_ref[:, :D_B].astype(F32) * _sigmoid(zh_ref[:, D_B:].astype(F32))
        ext_ref[0:halo, :] = jnp.where(i > 0, prev, 0.0)
        for rb in range(tm // CONV_ROWS):
            for cb in range(D_B // CONV_COLS):
                cs = slice(cb * CONV_COLS, (cb + 1) * CONV_COLS)
                window = ext_ref[rb * CONV_ROWS:rb * CONV_ROWS + CONV_ROWS + halo, cs]
                acc = jnp.zeros((CONV_ROWS, CONV_COLS), F32)
                for k in range(B_CONV):
                    shifted = _rows_after(window, halo - (B_CONV - 1) + k)[:CONV_ROWS]
                    acc = acc + cw_ref[k:k + 1, cs] * shifted
                cb_ref[rb * CONV_ROWS:(rb + 1) * CONV_ROWS, cs] = acc + cbias_ref[:, cs]
        xhb, _ = _ln_stats(cb_ref[...])
        y_ref[:, D_A:] = _silu(xhb * blg_ref[...] + blb_ref[...]).astype(BF16)

    row = lambda i: (i, 0)
    par = lambda i: (0, 0)
    return _pallas(
        comm, body, name=name, grid=(T // tm,),
        in_specs=[pl.BlockSpec((tm, 2 * D_A), lambda i: (i, 0)),
                  pl.BlockSpec((tm, 2 * D_B), lambda i: (i, 1)),
                  pl.BlockSpec((halo, 2 * D_B), lambda i: (_halo_prev_index(tm, halo)(i), 1)),
                  pl.BlockSpec((1, D_A), par), pl.BlockSpec((1, D_A), par),
                  pl.BlockSpec((A_HEADS, CHUNK, CHUNK), lambda i: (0, 0, 0)),
                  pl.BlockSpec((A_HEADS, CHUNK, 1), lambda i: (0, 0, 0)),
                  pl.BlockSpec((B_CONV, D_B), par), pl.BlockSpec((1, D_B), par),
                  pl.BlockSpec((1, D_B), par), pl.BlockSpec((1, D_B), par)],
        out_specs=[pl.BlockSpec((tm, D_A + D_B), row), pl.BlockSpec((tm, D_B), row)],
        out_shape=[jax.ShapeDtypeStruct((T, D_A + D_B), BF16), jax.ShapeDtypeStruct((T, D_B), F32)],
        scratch_shapes=[pltpu.VMEM((halo + tm, D_B), F32)],
        compiler_params=_cparams(("parallel",), 40),
    )(z, z, z, a_ln_g, a_ln_b, w_s, b_s, conv_w, conv_b, b_ln_g, b_ln_b)


def _mixer_ab_bwd_pre(z, cb, dy, a_ln_g, a_ln_b, w_s, b_s, b_ln_g, b_ln_b, *, tm, name, comm=None):
    T = z.shape[0]
    nchunk = tm // CHUNK
    tn_dims = (((0,), (0,)), ((), ()))
    nt_dims = (((1,), (1,)), ((), ()))

    def body(za_ref, cb_ref, dy_ref, alg_ref, alb_ref, ws_ref, bs_ref, blg_ref, blb_ref,
             dza_ref, dcb_ref, dalg_ref, dalb_ref, dws_ref, dbs_ref, dblg_ref, dblb_ref,
             dlv_ref):
        @pl.when(pl.program_id(0) == 0)
        def _():
            for ref in (dalg_ref, dalb_ref, dws_ref, dbs_ref, dblg_ref, dblb_ref):
                ref[...] = jnp.zeros_like(ref)
        ua = za_ref[:, :D_A].astype(F32)
        va = za_ref[:, D_A:].astype(F32)
        gu = _gelu(ua)
        gv = _gelu(va)
        xh, r = _ln_stats(gv)
        alg = alg_ref[...]
        lv = (xh * alg + alb_ref[...]).astype(BF16)
        dya = dy_ref[:, :D_A].astype(F32)
        mask = _causal_mask()
        for h in range(A_HEADS):
            wm = jnp.where(mask, ws_ref[h], 0.0).astype(BF16)
            cols = slice(h * HEAD_DIM, (h + 1) * HEAD_DIM)
            dwm = jnp.zeros((CHUNK, CHUNK), F32)
            dbs = jnp.zeros((CHUNK, 1), F32)
            for c in range(nchunk):
                rows = slice(c * CHUNK, (c + 1) * CHUNK)
                lvb = lv[rows, cols]
                mixed = jnp.dot(wm, lvb, preferred_element_type=F32) + bs_ref[h]
                dyb = dya[rows, cols]
                dza_ref[rows, cols] = (dyb * mixed * _dgelu(ua[rows, cols])).astype(BF16)
                dmixed = dyb * gu[rows, cols]
                dmb = dmixed.astype(BF16)
                dlv_ref[rows, cols] = lax.dot_general(wm, dmb, tn_dims, preferred_element_type=F32)
                dwm = dwm + lax.dot_general(dmb, lvb, nt_dims, preferred_element_type=F32)
                dbs = dbs + jnp.sum(dmixed, axis=1, keepdims=True)
            dws_ref[h] += jnp.where(mask, dwm, 0.0)
            dbs_ref[h] += dbs
        dlv = dlv_ref[...]
        dalg_ref[...] += _rowsum(dlv * xh)
        dalb_ref[...] += _rowsum(dlv)
        dgv = _ln_bwd(dlv, xh, r, alg)
        dza_ref[:, D_A:] = (dgv * _dgelu(va)).astype(BF16)
        xhb, rb = _ln_stats(cb_ref[...])
        blg = blg_ref[...]
        lb = xhb * blg + blb_ref[...]
        dlb = dy_ref[:, D_A:].astype(F32) * _dsilu(lb)
        dblg_ref[...] += _rowsum(dlb * xhb)
        dblb_ref[...] += _rowsum(dlb)
        dcb_ref[...] = _ln_bwd(dlb, xhb, rb, blg)

    row = lambda i: (i, 0)
    par = lambda i: (0, 0)
    par3 = lambda i: (0, 0, 0)
    return _pallas(
        comm, body, name=name, grid=(T // tm,),
        in_specs=[pl.BlockSpec((tm, 2 * D_A), row), pl.BlockSpec((tm, D_B), row),
                  pl.BlockSpec((tm, D_A + D_B), row),
                  pl.BlockSpec((1, D_A), par), pl.BlockSpec((1, D_A), par),
                  pl.BlockSpec((A_HEADS, CHUNK, CHUNK), par3),
                  pl.BlockSpec((A_HEADS, CHUNK, 1), par3),
                  pl.BlockSpec((1, D_B), par), pl.BlockSpec((1, D_B), par)],
        out_specs=[pl.BlockSpec((tm, 2 * D_A), row), pl.BlockSpec((tm, D_B), row),
                   pl.BlockSpec((1, D_A), par), pl.BlockSpec((1, D_A), par),
                   pl.BlockSpec((A_HEADS, CHUNK, CHUNK), par3),
                   pl.BlockSpec((A_HEADS, CHUNK, 1), par3),
                   pl.BlockSpec((1, D_B), par), pl.BlockSpec((1, D_B), par)],
        out_shape=[jax.ShapeDtypeStruct((T, 2 * D_A + 2 * D_B), BF16), jax.ShapeDtypeStruct((T, D_B), F32),
                   jax.ShapeDtypeStruct((1, D_A), F32), jax.ShapeDtypeStruct((1, D_A), F32),
                   jax.ShapeDtypeStruct((A_HEADS, CHUNK, CHUNK), F32),
                   jax.ShapeDtypeStruct((A_HEADS, CHUNK, 1), F32),
                   jax.ShapeDtypeStruct((1, D_B), F32), jax.ShapeDtypeStruct((1, D_B), F32)],
        scratch_shapes=[pltpu.VMEM((tm, D_A), F32)],
        compiler_params=_cparams(("arbitrary",), 40),
    )(z, cb, dy, a_ln_g, a_ln_b, w_s, b_s, b_ln_g, b_ln_b)


def _mixer_b_conv_bwd(z, dcb, conv_w, dz, *, tm, name, comm=None):
    T = z.shape[0]
    halo = HALO_LONG

    def body(zb_ref, dcb_ref, dcn_ref, cw_ref, dz_in_ref, dzb_ref, dcw_ref, dbias_ref, dext_ref):
        i = pl.program_id(0)
        last = pl.num_programs(0) - 1

        @pl.when(i == 0)
        def _():
            dcw_ref[...] = jnp.zeros_like(dcw_ref)
            dbias_ref[...] = jnp.zeros_like(dbias_ref)
        dcb = dcb_ref[...]
        dext_ref[0:tm, :] = dcb
        dext_ref[tm:tm + halo, :] = jnp.where(i < last, dcn_ref[...], 0.0)
        dbias_ref[...] += _rowsum(dcb)
        for rb in range(tm // CONV_ROWS):
            for cb in range(D_B // CONV_COLS):
                cs = slice(cb * CONV_COLS, (cb + 1) * CONV_COLS)
                gcs = slice(D_B + cb * CONV_COLS, D_B + (cb + 1) * CONV_COLS)
                rs = slice(rb * CONV_ROWS, (rb + 1) * CONV_ROWS)
                xbb = zb_ref[rs, cs].astype(F32)
                sgb = _sigmoid(zb_ref[rs, gcs].astype(F32))
                yb0 = xbb * sgb
                window = dext_ref[rb * CONV_ROWS:rb * CONV_ROWS + CONV_ROWS + halo, cs]
                acc = jnp.zeros((CONV_ROWS, CONV_COLS), F32)
                for k in range(B_CONV):
                    shifted = _rows_after(window, (B_CONV - 1) - k)[:CONV_ROWS]
                    acc = acc + cw_ref[k:k + 1, cs] * shifted
                    dcw_ref[k:k + 1, cs] += _rowsum(shifted * yb0)
                dzb_ref[rs, cs] = (acc * sgb).astype(BF16)
                dzb_ref[rs, gcs] = (acc * xbb * sgb * (1.0 - sgb)).astype(BF16)

    row = lambda i: (i, 0)
    par = lambda i: (0, 0)
    return _pallas(
        comm, body, name=name, grid=(T // tm,),
        in_specs=[pl.BlockSpec((tm, 2 * D_B), lambda i: (i, 1)),
                  pl.BlockSpec((tm, D_B), row),
                  pl.BlockSpec((halo, D_B), lambda i: (_halo_next_index(tm, halo, T)(i), 0)),
                  pl.BlockSpec((B_CONV, D_B), par), pl.BlockSpec(memory_space=pl.ANY)],
        out_specs=[pl.BlockSpec((tm, 2 * D_B), lambda i: (i, 1)), pl.BlockSpec((B_CONV, D_B), par),
                   pl.BlockSpec((1, D_B), par)],
        out_shape=[jax.ShapeDtypeStruct(dz.shape, BF16), jax.ShapeDtypeStruct((B_CONV, D_B), F32),
                   jax.ShapeDtypeStruct((1, D_B), F32)],
        scratch_shapes=[pltpu.VMEM((tm + halo, D_B), F32)], aliases={4: 0},
        compiler_params=_cparams(("arbitrary",), 40),
    )(z, dcb, dcb, conv_w, dz)


def _rows_before(x, a):
    return x if a == 0 else pltpu.roll(x, a, axis=0)


def _rows_after(x, a):
    return x if a == 0 else pltpu.roll(x, x.shape[0] - a, axis=0)


def _conv3(w_ref, x, halo, cs):
    acc = w_ref[2:3, cs] * x[halo:]
    acc = acc + w_ref[1:2, cs] * _rows_before(x, 1)[halo:]
    return acc + w_ref[0:1, cs] * _rows_before(x, 2)[halo:]


def _mixer_c_fwd(z, conv_w, *, tm, name, comm=None):
    T = z.shape[0]
    D = D_MODEL
    halo = HALO_SHORT
    W = CONV_COLS

    def body(bg_ref, cg_ref, xv_ref, cgh_ref, xvh_ref, w_ref, r_ref):
        i = pl.program_id(0)
        for cb in range(D // W):
            cs = slice(cb * W, (cb + 1) * W)
            prev = jnp.where(i > 0, cgh_ref[:, cs].astype(F32) * xvh_ref[:, cs].astype(F32), 0.0)
            p = jnp.concatenate([prev, cg_ref[:, cs].astype(F32) * xv_ref[:, cs].astype(F32)], axis=0)
            r_ref[:, cs] = (bg_ref[:, cs].astype(F32) * _conv3(w_ref, p, halo, cs)).astype(BF16)

    hp = _halo_prev_index(tm, halo)
    return _pallas(
        comm, body, name=name, grid=(T // tm,),
        in_specs=[pl.BlockSpec((tm, D), lambda i: (i, 0)), pl.BlockSpec((tm, D), lambda i: (i, 1)),
                  pl.BlockSpec((tm, D), lambda i: (i, 2)),
                  pl.BlockSpec((halo, D), lambda i: (hp(i), 1)),
                  pl.BlockSpec((halo, D), lambda i: (hp(i), 2)),
                  pl.BlockSpec((None, C_CONV, D), lambda i: (0, 0, 0))],
        out_specs=pl.BlockSpec((tm, D), lambda i: (i, 0)),
        out_shape=jax.ShapeDtypeStruct((T, D), BF16),
        compiler_params=_cparams(("parallel",), 40),
    )(z, z, z, z, z, conv_w)


def _mixer_c_bwd(z, dr, conv_w, *, tm, name, comm=None):
    T = z.shape[0]
    D = D_MODEL
    halo = HALO_SHORT
    W = CONV_COLS

    def body(bg_ref, cg_ref, xv_ref, cgh_ref, xvh_ref, bgn_ref, dr_ref, drn_ref, w_ref, dz_ref, dw_ref):
        i = pl.program_id(0)
        last = pl.num_programs(0) - 1

        @pl.when(i == 0)
        def _():
            dw_ref[...] = jnp.zeros_like(dw_ref)
        for cb in range(D // W):
            cs = slice(cb * W, (cb + 1) * W)
            cg = cg_ref[:, cs].astype(F32)
            xv = xv_ref[:, cs].astype(F32)
            dr = dr_ref[:, cs].astype(F32)
            p = cg * xv
            prev = jnp.where(i > 0, cgh_ref[:, cs].astype(F32) * xvh_ref[:, cs].astype(F32), 0.0)
            q = _conv3(w_ref, jnp.concatenate([prev, p], axis=0), halo, cs)
            dz_ref[:, cs] = (dr * q).astype(BF16)
            nxt = jnp.where(i < last, drn_ref[:, cs].astype(F32) * bgn_ref[:, cs].astype(F32), 0.0)
            dq = jnp.concatenate([dr * bg_ref[:, cs].astype(F32), nxt], axis=0)
            dp = None
            for k in range(C_CONV):
                shifted = _rows_after(dq, 2 - k)[:tm]
                term = w_ref[k:k + 1, cs] * shifted
                dp = term if dp is None else dp + term
                dw_ref[k:k + 1, cs] += _rowsum(shifted * p)
            dz_ref[:, D + cb * W:D + (cb + 1) * W] = (dp * xv).astype(BF16)
            dz_ref[:, 2 * D + cb * W:2 * D + (cb + 1) * W] = (dp * cg).astype(BF16)

    hp = _halo_prev_index(tm, halo)
    hn = _halo_next_index(tm, halo, T)
    return _pallas(
        comm, body, name=name, grid=(T // tm,),
        in_specs=[pl.BlockSpec((tm, D), lambda i: (i, 0)), pl.BlockSpec((tm, D), lambda i: (i, 1)),
                  pl.BlockSpec((tm, D), lambda i: (i, 2)),
                  pl.BlockSpec((halo, D), lambda i: (hp(i), 1)),
                  pl.BlockSpec((halo, D), lambda i: (hp(i), 2)),
                  pl.BlockSpec((halo, D), lambda i: (hn(i), 0)),
                  pl.BlockSpec((tm, D), lambda i: (i, 0)),
                  pl.BlockSpec((halo, D), lambda i: (hn(i), 0)),
                  pl.BlockSpec((None, C_CONV, D), lambda i: (0, 0, 0))],
        out_specs=[pl.BlockSpec((tm, 3 * D), lambda i: (i, 0)),
                   pl.BlockSpec((C_CONV, D), lambda i: (0, 0))],
        out_shape=[jax.ShapeDtypeStruct((T, 3 * D), BF16), jax.ShapeDtypeStruct((C_CONV, D), F32)],
        compiler_params=_cparams(("arbitrary",), 48),
    )(z, z, z, z, z, z, dr, dr, conv_w)


FFN_COLS = 128


def _ffn_act_fwd(up, conv_w, *, layer, tm, name, comm=None):
    T = up.shape[0]
    halo = HALO_SHORT
    W = FFN_COLS

    def body(up_ref, uph_ref, w_ref, a_ref, upc_ref):
        i = pl.program_id(0)

        def conv(cs):
            prev = jnp.where(i > 0, uph_ref[:, cs], jnp.zeros((halo, W), BF16))
            return _conv3(w_ref, jnp.concatenate([prev, up_ref[:, cs]], axis=0).astype(F32), halo, cs)

        for cb in range(D_FF // W):
            gs = slice(cb * W, (cb + 1) * W)
            vs = slice(D_FF + cb * W, D_FF + (cb + 1) * W)
            g = conv(gs)
            v = conv(vs)
            upc_ref[:, gs] = g.astype(BF16)
            upc_ref[:, vs] = v.astype(BF16)
            a_ref[:, gs] = (_silu(g) * v).astype(BF16)

    return _pallas(
        comm, body, name=name, grid=(T // tm,),
        in_specs=[pl.BlockSpec((tm, 2 * D_FF), lambda i: (i, 0)),
                  pl.BlockSpec((halo, 2 * D_FF), lambda i: (_halo_prev_index(tm, halo)(i), 0)),
                  pl.BlockSpec((None, F_CONV, 2 * D_FF), lambda i: (layer, 0, 0))],
        out_specs=[pl.BlockSpec((tm, D_FF), lambda i: (i, 0)),
                   pl.BlockSpec((tm, 2 * D_FF), lambda i: (i, 0))],
        out_shape=[jax.ShapeDtypeStruct((T, D_FF), BF16), jax.ShapeDtypeStruct((T, 2 * D_FF), BF16)],
        compiler_params=_cparams(("parallel",), 48),
    )(up, up, conv_w)


def _ffn_act_bwd(up, upc, da, conv_w, *, layer, tm, name, comm=None):
    T = up.shape[0]
    halo = HALO_SHORT
    W = FFN_COLS

    def body(up_ref, upc_ref, upcn_ref, da_ref, dan_ref, w_ref, dup_ref, dw_ref):
        i = pl.program_id(0)
        last = pl.num_programs(0) - 1

        @pl.when(i == 0)
        def _():
            dw_ref[...] = jnp.zeros_like(dw_ref)
        live = jnp.where(i < last, 1.0, 0.0)
        for cb in range(D_FF // W):
            gs = slice(cb * W, (cb + 1) * W)
            vs = slice(D_FF + cb * W, D_FF + (cb + 1) * W)
            g = jnp.concatenate([upc_ref[:, gs], upcn_ref[:, gs]], axis=0).astype(F32)
            v = jnp.concatenate([upc_ref[:, vs], upcn_ref[:, vs]], axis=0).astype(F32)
            da = jnp.concatenate([da_ref[:, gs].astype(F32), dan_ref[:, gs].astype(F32) * live], axis=0)
            s = _sigmoid(g)
            silu = g * s
            grads = (da * v * (s * (1.0 + g * (1.0 - s))), da * silu)
            for cs, d in zip((gs, vs), grads):
                u = up_ref[:, cs].astype(F32)
                acc = None
                for k in range(F_CONV):
                    shifted = _rows_after(d, 2 - k)[:tm]
                    term = w_ref[k:k + 1, cs] * shifted
                    acc = term if acc is None else acc + term
                    dw_ref[k:k + 1, cs] += _rowsum(shifted * u)
                dup_ref[:, cs] = acc.astype(BF16)

    hn = _halo_next_index(tm, halo, T)
    return _pallas(
        comm, body, name=name, grid=(T // tm,),
        in_specs=[pl.BlockSpec((tm, 2 * D_FF), lambda i: (i, 0)),
                  pl.BlockSpec((tm, 2 * D_FF), lambda i: (i, 0)),
                  pl.BlockSpec((halo, 2 * D_FF), lambda i: (hn(i), 0)),
                  pl.BlockSpec((tm, D_FF), lambda i: (i, 0)),
                  pl.BlockSpec((halo, D_FF), lambda i: (hn(i), 0)),
                  pl.BlockSpec((None, F_CONV, 2 * D_FF), lambda i: (layer, 0, 0))],
        out_specs=[pl.BlockSpec((tm, 2 * D_FF), lambda i: (i, 0)),
                   pl.BlockSpec((F_CONV, 2 * D_FF), lambda i: (0, 0))],
        out_shape=[jax.ShapeDtypeStruct((T, 2 * D_FF), BF16),
                   jax.ShapeDtypeStruct((F_CONV, 2 * D_FF), F32)],
        compiler_params=_cparams(("arbitrary",), 56),
    )(up, upc, upc, da, da, conv_w)


def _local_step(x, tgt, small, plan):
    T = x.shape[0]
    tm_e = _pick(T, 256)
    tm_n = _pick(T, 512)
    tm = _pick(T, 1024)
    tm_r = _pick(T, 2048)
    tt = _pick(T, 2048)
    nm = small["norm_mix"].reshape(2, 1, D_MODEL)
    nf = small["norm_ffn"].reshape(2, 1, D_MODEL)
    ngf = small["norm_final"].reshape(1, D_MODEL)
    b_s = small["a_b_s"].reshape(A_HEADS, CHUNK, 1)
    w_s = small["a_w_s"].reshape(A_HEADS, CHUNK, CHUNK)
    b_conv_w = small["b_conv_w"].reshape(B_CONV, D_B)
    sg = {}
    wt, cm = plan.weight, plan.comm

    h_m0 = _rmsnorm_fwd(x, nm, layer=0, tm=tm_n, name="norm_mix0")
    z_ab = _mm_nn(h_m0, wt("ab_w_in", 0), layer=0, tm=tm, tn=512, out_dtype=BF16, name="ab_in", comm=cm("ab_in"))
    yab, cb = _mixer_ab_fwd(z_ab, small["a_ln_g"], small["a_ln_b"], w_s, b_s, b_conv_w, small["b_conv_b"],
                            small["b_ln_g"], small["b_ln_b"], tm=tm_e, name="mixer_ab", comm=cm("mixer_ab"))
    x1, h_f0 = _mm_nn(yab, wt("ab_w_out", 0), layer=0, tm=tm, tn=D_MODEL, residual=x, norm=(nf, 0),
                      name="ab_out", comm=cm("ab_out"))

    def ffn_fwd(xin, h, layer, norm):
        up = _mm_nn(h, wt("f_w_up", layer), layer=0, tm=tm, tn=1408, out_dtype=BF16, name=f"ffn_up{layer}",
                    comm=cm(f"ffn_up{layer}"))
        a, upc = _ffn_act_fwd(up, small["f_conv_w"], layer=layer, tm=tm_e, name=f"ffn_act{layer}",
                              comm=cm(f"ffn_act{layer}"))
        out = _mm_nn(a, wt("f_w_down", layer), layer=0, tm=tm, tn=D_MODEL, residual=xin, norm=norm,
                     name=f"ffn_down{layer}", comm=cm(f"ffn_down{layer}"))
        return up, upc, a, out

    up0, upc0, a0, (x2, h_m1) = ffn_fwd(x1, h_f0, 0, (nm, 1))
    z_c = _mm_nn(h_m1, wt("c_w_in", 0), layer=0, tm=tm, tn=768, out_dtype=BF16, name="c_in", comm=cm("c_in"))
    r = _mixer_c_fwd(z_c, small["c_conv_w"], tm=tm_e, name="mixer_c", comm=cm("mixer_c"))
    x3, h_f1 = _mm_nn(r, wt("c_w_out", 0), layer=0, tm=tm, tn=D_MODEL, residual=x2, norm=(nf, 1),
                      name="c_out", comm=cm("c_out"))
    up1, upc1, a1, x4 = ffn_fwd(x3, h_f1, 1, None)
    loss, dx, sg["norm_final"] = _loss_head(x4, tgt, ngf, tm=tm_n, name="loss_head")

    def ffn_bwd(dx, xin, h, up, upc, a, layer):
        da = _mm_nt(dx, wt("f_w_down", layer), layer=0, tm=tm, tn=1408, out_dtype=BF16,
                    name=f"ffn_down_dx{layer}", comm=cm(f"ffn_down_dx{layer}"))
        plan.grad_ready("f_w_down", layer, _mm_tn(a, dx, shards=None, tk=1408, tn=1024, tt=tt,
                                                  name=f"ffn_down_dw{layer}", comm=cm(f"ffn_down_dw{layer}")))
        dup, dcw = _ffn_act_bwd(up, upc, da, small["f_conv_w"], layer=layer, tm=tm_e, name=f"ffn_act_bwd{layer}",
                                comm=cm(f"ffn_act_bwd{layer}"))
        dh = _mm_nt(dup, wt("f_w_up", layer), layer=0, tm=tm_r, tn=None, name=f"ffn_up_dx{layer}",
                    comm=cm(f"ffn_up_dx{layer}"))
        plan.grad_ready("f_w_up", layer, _mm_tn(h, dup, shards=N_CHIPS, tk=1024, tn=1408, tt=tt,
                                                name=f"ffn_up_dw{layer}", comm=cm(f"ffn_up_dw{layer}")))
        dxin, dg = _rmsnorm_bwd(xin, nf, dh, dx, layer=layer, tm=tm_n, name=f"norm_ffn_bwd{layer}",
                                comm=cm(f"norm_ffn_bwd{layer}"))
        return dxin, dg, dcw

    dx, dnf1, dfc1 = ffn_bwd(dx, x3, h_f1, up1, upc1, a1, 1)
    dr = _mm_nt(dx, wt("c_w_out", 0), layer=0, tm=tm, tn=512, out_dtype=BF16, name="c_out_dx", comm=cm("c_out_dx"))
    plan.grad_ready("c_w_out", 0, _mm_tn(r, dx, shards=None, tk=1024, tn=1024, tt=tt, name="c_out_dw",
                                         comm=cm("c_out_dw")))
    dz_c, dccw = _mixer_c_bwd(z_c, dr, small["c_conv_w"], tm=tm_e, name="mixer_c_bwd", comm=cm("mixer_c_bwd"))
    sg["c_conv_w"] = dccw.reshape(1, C_CONV, D_MODEL)
    dh = _mm_nt(dz_c, wt("c_w_in", 0), layer=0, tm=tm_r, tn=None, name="c_in_dx", comm=cm("c_in_dx"))
    plan.grad_ready("c_w_in", 0, _mm_tn(h_m1, dz_c, shards=N_CHIPS, tk=1024, tn=768, tt=tt, name="c_in_dw",
                                        comm=cm("c_in_dw")))
    dx, dnm1 = _rmsnorm_bwd(x2, nm, dh, dx, layer=1, tm=tm_n, name="norm_mix_bwd1", comm=cm("norm_mix_bwd1"))
    dx, dnf0, dfc0 = ffn_bwd(dx, x1, h_f0, up0, upc0, a0, 0)
    dyab = _mm_nt(dx, wt("ab_w_out", 0), layer=0, tm=tm, tn=512, out_dtype=BF16, name="ab_out_dx",
                  comm=cm("ab_out_dx"))
    plan.grad_ready("ab_w_out", 0, _mm_tn(yab, dx, shards=None, tk=1024, tn=1024, tt=tt, name="ab_out_dw",
                                          comm=cm("ab_out_dw")))
    (dza, dcb, sg["a_ln_g"], sg["a_ln_b"], dws, dbs, sg["b_ln_g"], sg["b_ln_b"]) = _mixer_ab_bwd_pre(
        z_ab, cb, dyab, small["a_ln_g"], small["a_ln_b"], w_s, b_s, small["b_ln_g"], small["b_ln_b"],
        tm=tm_e, name="mixer_ab_bwd", comm=cm("mixer_ab_bwd"))
    dz_ab, dbcw, sg["b_conv_b"] = _mixer_b_conv_bwd(z_ab, dcb, b_conv_w, dza, tm=tm_e, name="mixer_b_conv_bwd",
                                                    comm=cm("mixer_b_conv_bwd"))
    sg["a_w_s"] = dws.reshape(1, A_HEADS, CHUNK, CHUNK)
    sg["a_b_s"] = dbs.reshape(1, A_HEADS, CHUNK)
    sg["b_conv_w"] = dbcw.reshape(1, B_CONV, D_B)
    plan.grad_ready("ab_w_in", 0, _mm_tn(h_m0, dz_ab, shards=N_CHIPS, tk=1024, tn=512, tt=tt, name="ab_in_dw",
                                         comm=cm("ab_in_dw")))
    dh = _mm_nt(dz_ab, wt("ab_w_in", 0), layer=0, tm=tm_r, tn=None, name="ab_in_dx", comm=cm("ab_in_dx"))
    dx, dnm0 = _rmsnorm_bwd(x, nm, dh, dx, layer=0, tm=tm_n, name="norm_mix_bwd0", comm=cm("norm_mix_bwd0"))

    sg["norm_mix"] = [dnm0, dnm1]
    sg["norm_ffn"] = [dnf0, dnf1]
    sg["f_conv_w"] = [dfc0, dfc1]
    return loss, dx, sg


BLOCK_BYTES = 3 * 1024 * 1024


BF16_SUBLANES = 16


def _row_tile(rows, row_bytes, step=SUBLANES):
    best = None
    for tr in range(step, rows + 1, step):
        if rows % tr == 0 and tr * row_bytes <= BLOCK_BYTES:
            best = tr
    if best is None:
        raise ValueError(f"no row tile for {rows}")
    return best


def _place_scalars():
    x, y, c = lax.axis_index("x"), lax.axis_index("y"), lax.axis_index("c")
    return jnp.stack([c, 2 * x + y, 2 * (1 - x) + y, 2 * x + (1 - y), 2 * (1 - x) + (1 - y)]).astype(jnp.int32)


def _cast_into_slot(w, place, *, layer, name):
    L, rows, cols = w.shape
    tr = _row_tile(rows, cols * 4, BF16_SUBLANES)

    def body(place_ref, w_ref, o_ref):
        o_ref[...] = w_ref[...].astype(BF16)

    return pl.pallas_call(
        body, name=name,
        grid_spec=pltpu.PrefetchScalarGridSpec(
            num_scalar_prefetch=1, grid=(rows // tr,),
            in_specs=[pl.BlockSpec((None, tr, cols), lambda i, p: (layer, i, 0))],
            out_specs=pl.BlockSpec((None, None, tr, cols), lambda i, p: (0, p[1], i, 0))),
        out_shape=jax.ShapeDtypeStruct((1, N_CHIPS, rows, cols), BF16),
        compiler_params=_cparams(("parallel",), 32),
    )(place, w)


def _pair_sum(g, theirs, place, *, name):
    S, rows, cols = g.shape
    half = rows // 2
    tr = _row_tile(half, cols * 4, BF16_SUBLANES)
    nb = half // tr

    def body(place_ref, g_ref, t_ref, o_ref):
        o_ref[...] = (g_ref[...] + t_ref[...]).astype(BF16)

    spec = pl.BlockSpec((None, tr, cols), lambda s, i, p: (s, i, 0))
    return pl.pallas_call(
        body, name=name,
        grid_spec=pltpu.PrefetchScalarGridSpec(
            num_scalar_prefetch=1, grid=(S, nb),
            in_specs=[pl.BlockSpec((None, tr, cols), lambda s, i, p: (s, p[0] * nb + i, 0)), spec],
            out_specs=spec),
        out_shape=jax.ShapeDtypeStruct((S, half, cols), BF16),
        compiler_params=_cparams(("parallel", "parallel"), 32),
    )(place, g, theirs)


def _chip_sum(p, r, g_prev, place, *, layer, shape, name):
    L, rows, cols = shape
    half = rows // 2
    tr = _row_tile(half, cols * 4, BF16_SUBLANES)
    nb = half // tr

    def body(place_ref, p_ref, r_ref, *rest):
        o_ref = rest[-1]
        mine = p_ref[...].astype(F32)
        peers = [r_ref[j].astype(F32) for j in range(3)]
        acc = None
        for s in range(N_CHIPS):
            term = jnp.where(place_ref[1] == s, mine,
                             jnp.where(place_ref[2] == s, peers[0],
                                       jnp.where(place_ref[3] == s, peers[1], peers[2])))
            acc = term if acc is None else acc + term
        o_ref[...] = acc

    in_specs = [pl.BlockSpec((None, tr, cols), lambda i, pr: (pr[1], i, 0)),
                pl.BlockSpec((3, tr, cols), lambda i, pr: (0, i, 0))]
    args = [place, p, r]
    aliases = {}
    if g_prev is not None:
        in_specs.append(ANY)
        args.append(g_prev)
        aliases = {3: 0}
    return pl.pallas_call(
        body, name=name,
        grid_spec=pltpu.PrefetchScalarGridSpec(
            num_scalar_prefetch=1, grid=(nb,), in_specs=in_specs,
            out_specs=pl.BlockSpec((None, tr, cols), lambda i, pr: (layer, pr[0] * nb + i, 0))),
        out_shape=jax.ShapeDtypeStruct(shape, F32), input_output_aliases=aliases,
        compiler_params=_cparams(("parallel",), 32),
    )(*args)


def _adamw_math(w, g, m, v):
    m2 = ADAM_B1 * m + (1.0 - ADAM_B1) * g
    v2 = ADAM_B2 * v + (1.0 - ADAM_B2) * (g * g)
    m_hat = m2 / (1.0 - ADAM_B1 ** ADAM_STEP)
    v_hat = v2 / (1.0 - ADAM_B2 ** ADAM_STEP)
    delta = -ADAM_LR * (m_hat / (jnp.sqrt(v_hat) + ADAM_EPS) + ADAM_WD * w)
    return delta, m2, v2


def _adamw(w, g, m, v, *, name):
    L, rows, cols = w.shape
    tr = _row_tile(rows, cols * 4)

    def body(w_ref, g_ref, m_ref, v_ref, d_ref, m2_ref, v2_ref):
        d, m2, v2 = _adamw_math(w_ref[...], g_ref[...], m_ref[...], v_ref[...])
        d_ref[...] = d
        m2_ref[...] = m2
        v2_ref[...] = v2

    spec = pl.BlockSpec((None, tr, cols), lambda l, i: (l, i, 0))
    shape = jax.ShapeDtypeStruct(w.shape, F32)
    return pl.pallas_call(
        body, name=name, grid=(L, rows // tr), in_specs=[spec] * 4, out_specs=[spec] * 3,
        out_shape=[shape] * 3,
        compiler_params=_cparams(("parallel", "parallel"), 48),
    )(w, g, m, v)


def _exchange_packs(pack, *, reduce, name):
    R = pack.shape[0]
    ndev = 2 * N_CHIPS

    def body(p_ref, o_ref, *scratch):
        if reduce:
            buf, send, recv = scratch
        else:
            buf = o_ref
            send, recv = scratch
        x, y, c = lax.axis_index("x"), lax.axis_index("y"), lax.axis_index("c")
        me = 4 * x + 2 * y + c
        buf[me] = p_ref[...]
        sends = []
        for q in range(1, ndev):
            qx, qy, qc = (q >> 2) & 1, (q >> 1) & 1, q & 1
            peer = (x ^ qx, y ^ qy, c ^ qc)
            rc = _remote(p_ref, buf.at[me], send.at[q - 1], recv.at[q - 1], peer)
            rc.start()
            sends.append(rc)
        for q in range(1, ndev):
            qx, qy, qc = (q >> 2) & 1, (q >> 1) & 1, q & 1
            slot = buf.at[4 * (x ^ qx) + 2 * (y ^ qy) + (c ^ qc)]
            _remote(slot, slot, send.at[q - 1], recv.at[q - 1], (x ^ qx, y ^ qy, c ^ qc)).wait_recv()
        for rc in sends:
            rc.wait_send()
        if reduce:
            acc = buf[0]
            for d in range(1, ndev):
                acc = acc + buf[d]
            o_ref[...] = acc

    vm = pl.BlockSpec(memory_space=pltpu.VMEM)
    sems = [pltpu.SemaphoreType.DMA((ndev - 1,)), pltpu.SemaphoreType.DMA((ndev - 1,))]
    if reduce:
        out_shape = jax.ShapeDtypeStruct((R, LANES), F32)
        scratch = [pltpu.VMEM((ndev, R, LANES), F32)] + sems
    else:
        out_shape = jax.ShapeDtypeStruct((ndev, R, LANES), F32)
        scratch = sems
    return pl.pallas_call(
        body, name=name, in_specs=[vm], out_specs=vm, out_shape=out_shape, scratch_shapes=scratch,
        compiler_params=pltpu.CompilerParams(vmem_limit_bytes=VMEM_BYTES_MAX),
    )(pack)


PACK_UNIT = SUBLANES * LANES


def _pack(arrays):
    flat, sizes = [], []
    for a in arrays:
        pieces = a if isinstance(a, (list, tuple)) else [a]
        v = jnp.concatenate([p.reshape(-1) for p in pieces]) if len(pieces) > 1 else pieces[0].reshape(-1)
        size = v.shape[0]
        padded = -(-size // PACK_UNIT) * PACK_UNIT
        flat.append(jnp.pad(v, (0, padded - size)))
        sizes.append((size, padded))
    return jnp.concatenate(flat).reshape(-1, LANES), sizes


def _unpack(pack, sizes, shapes):
    v = pack.reshape(-1)
    out, off = [], 0
    for (size, padded), shape in zip(sizes, shapes):
        out.append(v[off:off + size].reshape(shape))
        off += padded
    return out


BIG = ("ab_w_in", "ab_w_out", "c_w_in", "c_w_out", "f_w_up", "f_w_down")
COL_SHARDED = ("ab_w_in", "c_w_in", "f_w_up")
SMALL_REPLICATED = ("norm_mix", "norm_ffn", "norm_final", "a_ln_g", "a_ln_b", "a_w_s", "a_b_s",
                    "b_conv_b", "b_ln_g", "b_ln_b")
SMALL_SHARDED = ("b_conv_w", "c_conv_w", "f_conv_w")
SMALL = SMALL_REPLICATED + SMALL_SHARDED
ALL_WEIGHTS = ("norm_mix", "norm_ffn", "norm_final", "ab_w_in", "a_ln_g", "a_ln_b", "a_w_s", "a_b_s",
               "b_conv_w", "b_conv_b", "b_ln_g", "b_ln_b", "ab_w_out", "c_w_in", "c_conv_w", "c_w_out",
               "f_w_up", "f_conv_w", "f_w_down")


SCHEDULE = {
    "ab_in": [("gi", "f_w_up", 0, 0, 4), ("gi", "ab_w_out", 0)],
    "mixer_ab": [("gd", "f_w_up", 0, 0, 4), ("gd", "ab_w_out", 0), ("gi", "f_w_up", 0, 1, 4),
                 ("gi", "f_w_up", 0, 2, 4), ("gi", "f_w_up", 0, 3, 4)],
    "ab_out": [("gd", "f_w_up", 0, 1, 4), ("gd", "f_w_up", 0, 2, 4), ("gd", "f_w_up", 0, 3, 4)],
    "ffn_up0": [("gi", "f_w_down", 0), ("gi", "c_w_in", 0, 0, 2)],
    "ffn_act0": [("gd", "f_w_down", 0), ("gd", "c_w_in", 0, 0, 2), ("gi", "c_w_in", 0, 1, 2),
                 ("gi", "f_w_up", 1, 0, 4), ("gi", "f_w_up", 1, 1, 4)],
    "ffn_down0": [("gd", "c_w_in", 0, 1, 2), ("gd", "f_w_up", 1, 0, 4), ("gd", "f_w_up", 1, 1, 4),
                  ("gi", "f_w_up", 1, 2, 4)],
    "c_in": [("gd", "f_w_up", 1, 2, 4), ("gi", "f_w_up", 1, 3, 4), ("gi", "c_w_out", 0)],
    "mixer_c": [("gd", "f_w_up", 1, 3, 4), ("gd", "c_w_out", 0), ("gi", "f_w_down", 1, 0, 2)],
    "c_out": [("gd", "f_w_down", 1, 0, 2), ("gi", "f_w_down", 1, 1, 2)],
    "ffn_up1": [("gd", "f_w_down", 1, 1, 2)],
    "ffn_act_bwd1": [("px", "f_w_down", 1)],
    "ffn_up_dx1": [("cx", "f_w_down", 1)],
    "mixer_c_bwd": [("px", "f_w_up", 1), ("px", "c_w_out", 0)],
    "c_in_dx": [("cx", "f_w_up", 1, 0, 2)],
    "c_in_dw": [("cx", "f_w_up", 1, 1, 2), ("cx", "c_w_out", 0)],
    "norm_mix_bwd1": [("px", "c_w_in", 0)],
    "ffn_down_dx0": [("cx", "c_w_in", 0, 0, 2)],
    "ffn_down_dw0": [("cx", "c_w_in", 0, 1, 2)],
    "ffn_act_bwd0": [("px", "f_w_down", 0)],
    "ffn_up_dx0": [("cx", "f_w_down", 0)],
    "mixer_ab_bwd": [("px", "f_w_up", 0), ("px", "ab_w_out", 0)],
    "mixer_b_conv_bwd": [("cx", "f_w_up", 0, 0, 2), ("cx", "ab_w_out", 0)],
    "ab_in_dw": [("cx", "f_w_up", 0, 1, 2)],
    "ab_in_dx": [("px", "ab_w_in", 0)],
    "norm_mix_bwd0": [("cx", "ab_w_in", 0)],
}


class _Plan:
    def __init__(self, shapes, place):
        self.shapes, self.place, self.bufs = shapes, place, {}

    def weight(self, name, layer):
        g = self.bufs[f"w:{name}:{layer}"]
        if name in COL_SHARDED:
            return g
        _, S, rows, cols = g.shape
        return g.reshape(1, S * rows, cols)

    def grad_ready(self, name, layer, g):
        _, rows, cols = self.shapes[name]
        hbm = lambda a: pltpu.with_memory_space_constraint(a, pltpu.HBM)
        self.bufs[f"g:{name}:{layer}"] = g.reshape(N_CHIPS, rows, cols)
        self.bufs[f"t:{name}:{layer}"] = hbm(lax.empty((N_CHIPS, rows // 2, cols), F32))
        self.bufs[f"l:{name}:{layer}"] = hbm(lax.empty((3, rows // 2, cols), BF16))

    def job(self, kind, name, layer, part=0, parts=1):
        _, rows, cols = self.shapes[name]
        key = f"{name}:{layer}"
        if kind == "gi":
            return _job_gather_ici("w:" + key, rows, part, parts)
        if kind == "gd":
            return _job_gather_d2d("w:" + key, rows, part, parts)
        if kind == "px":
            return _job_pair_exchange("g:" + key, "t:" + key, rows)
        if kind == "cx":
            if "p:" + key not in self.bufs:
                self.bufs["p:" + key] = _pair_sum(self.bufs["g:" + key], self.bufs["t:" + key], self.place,
                                                  name=f"pair_sum_{name}{layer}")
            nr = rows // 2 // parts
            return _job_chip_exchange("p:" + key, "l:" + key, part * nr, nr)
        if kind == "ps":
            return _job_pair_share("G:" + name, layer, rows)
        raise ValueError(kind)

    def comm(self, call):
        specs = SCHEDULE.get(call)
        return None if specs is None else _Comm(self, [self.job(*spec) for spec in specs])


def _step(x, tgt, w, m, v):
    chip = 2 * lax.axis_index("x") + lax.axis_index("y")
    place = _place_scalars()
    plan = _Plan({n: w[n].shape for n in BIG}, place)
    items = [(n, l) for n in BIG for l in range(w[n].shape[0])]

    for n, l in items:
        plan.bufs[f"w:{n}:{l}"] = _cast_into_slot(w[n], place, layer=l, name=f"cast_{n}{l}")
    _comm_only(plan, [[plan.job("gi", "ab_w_in", 0)], [plan.job("gd", "ab_w_in", 0)]], name="gather_first")
    conv_pack, conv_sizes = _pack([w[n] for n in SMALL_SHARDED])
    conv_all = _exchange_packs(conv_pack, reduce=False, name="gather_conv_weights")
    conv_shapes = [w[n].shape for n in SMALL_SHARDED]
    per_chip = [_unpack(conv_all[2 * s], conv_sizes, conv_shapes) for s in range(N_CHIPS)]
    small = {n: w[n] for n in SMALL_REPLICATED}
    for idx, n in enumerate(SMALL_SHARDED):
        small[n] = jnp.concatenate([per_chip[s][idx] for s in range(N_CHIPS)], axis=-1)

    loss, dx, sg = _local_step(x, tgt, small, plan)

    for n, l in items:
        plan.bufs["G:" + n] = _chip_sum(plan.bufs[f"p:{n}:{l}"], plan.bufs[f"l:{n}:{l}"], plan.bufs.get("G:" + n),
                                        place, layer=l, shape=w[n].shape, name=f"chip_sum_{n}{l}")
    _comm_only(plan, [[plan.job("ps", n, l) for n, l in items]], name="reduce_pair_share")
    grads_big = [plan.bufs["G:" + n] for n in BIG]

    g_pack, g_sizes = _pack([sg[n] for n in SMALL])
    g_sum = _exchange_packs(g_pack, reduce=True, name="allreduce_small_grads")
    full_shapes = [small[n].shape for n in SMALL]
    g_small = dict(zip(SMALL, _unpack(g_sum, g_sizes, full_shapes)))
    for n in SMALL_SHARDED:
        width = w[n].shape[-1]
        g_small[n] = lax.dynamic_slice_in_dim(g_small[n], chip * width, width, axis=g_small[n].ndim - 1)

    grad, delta, new_m, new_v = {}, {}, {}, {}
    for n, g in zip(BIG, grads_big):
        grad[n] = g
        delta[n], new_m[n], new_v[n] = _adamw(w[n], g, m[n], v[n], name=f"adamw_{n}")
    shapes = [w[n].shape for n in SMALL]
    wp, sizes = _pack([w[n] for n in SMALL])
    gp, _ = _pack([g_small[n] for n in SMALL])
    mp, _ = _pack([m[n] for n in SMALL])
    vp, _ = _pack([v[n] for n in SMALL])
    R = wp.shape[0]
    dp, m2p, v2p = _adamw(wp.reshape(1, R, LANES), gp.reshape(1, R, LANES), mp.reshape(1, R, LANES),
                          vp.reshape(1, R, LANES), name="adamw_small")
    for n, d_, m_, v_ in zip(SMALL, _unpack(dp, sizes, shapes), _unpack(m2p, sizes, shapes),
                             _unpack(v2p, sizes, shapes)):
        grad[n] = g_small[n]
        delta[n], new_m[n], new_v[n] = d_, m_, v_
    return loss, dx, grad, delta, new_m, new_v


def kernel(x, norm_mix, norm_ffn, norm_final, ab_w_in, a_ln_g, a_ln_b, a_w_s, a_b_s, b_conv_w, b_conv_b, b_ln_g, b_ln_b, ab_w_out, c_w_in, c_conv_w, c_w_out, f_w_up, f_conv_w, f_w_down, loss_target, m_norm_mix, m_norm_ffn, m_norm_final, m_ab_w_in, m_a_ln_g, m_a_ln_b, m_a_w_s, m_a_b_s, m_b_conv_w, m_b_conv_b, m_b_ln_g, m_b_ln_b, m_ab_w_out, m_c_w_in, m_c_conv_w, m_c_w_out, m_f_w_up, m_f_conv_w, m_f_w_down, v_norm_mix, v_norm_ffn, v_norm_final, v_ab_w_in, v_a_ln_g, v_a_ln_b, v_a_w_s, v_a_b_s, v_b_conv_w, v_b_conv_b, v_b_ln_g, v_b_ln_b, v_ab_w_out, v_c_w_in, v_c_conv_w, v_c_w_out, v_f_w_up, v_f_conv_w, v_f_w_down):
    given = dict(locals())
    w = {n: given[n] for n in ALL_WEIGHTS}
    m = {n: given["m_" + n] for n in ALL_WEIGHTS}
    v = {n: given["v_" + n] for n in ALL_WEIGHTS}
    T = x.shape[1]
    loss, dx, grad, delta, new_m, new_v = _step(x.reshape(T, D_MODEL), loss_target.reshape(T, D_MODEL), w, m, v)
    loss = lax.psum(loss[0, 0], ("x", "y", "c"))
    out = [loss, dx.reshape(x.shape)]
    for d in (grad, delta, new_m, new_v):
        out += [d[n] for n in ALL_WEIGHTS]
    return tuple(out)
```

```python
import functools
import math

import jax
import jax.numpy as jnp
from jax import lax
from jax.experimental import pallas as pl
from jax.experimental.pallas import tpu as pltpu

F32 = jnp.float32
BF16 = jnp.bfloat16

EPS = 1e-6
D_MODEL = 1024
CHUNK = 128
HEAD_DIM = 128
A_HEADS = 4
D_A = 512
D_B = 512
B_CONV = 31
C_CONV = 3
D_FF = 2816
F_CONV = 3
N_CHIPS = 4

ADAM_LR = 0.001
ADAM_B1 = 0.9
ADAM_B2 = 0.999
ADAM_EPS = 1e-08
ADAM_WD = 0.01
ADAM_STEP = 10

SUBLANES = 8
LANES = 128
HALO_SHORT = 16
HALO_LONG = 32
VMEM_BYTES_MAX = 60000 * 1024

INV_SQRT2 = 1.0 / math.sqrt(2.0)
INV_SQRT_2PI = 1.0 / math.sqrt(2.0 * math.pi)

MESH = pl.DeviceIdType.MESH


def _cparams(sem, vmem_mb):
    del vmem_mb
    return pltpu.CompilerParams(dimension_semantics=sem, vmem_limit_bytes=VMEM_BYTES_MAX)


def _pick(total, pref):
    for c in (2048, 1024, 512, 256, 128):
        if c <= pref and total % c == 0:
            return c
    raise ValueError(f"no tile for {total}")


def _sigmoid(x):
    return jax.nn.sigmoid(x)


def _silu(x):
    return x * _sigmoid(x)


def _dsilu(x):
    s = _sigmoid(x)
    return s * (1.0 + x * (1.0 - s))


def _gelu(x):
    return 0.5 * x * (1.0 + lax.erf(x * INV_SQRT2))


def _dgelu(x):
    return 0.5 * (1.0 + lax.erf(x * INV_SQRT2)) + x * jnp.exp(-0.5 * x * x) * INV_SQRT_2PI


def _ln_stats(x):
    mu = jnp.mean(x, axis=-1, keepdims=True)
    xc = x - mu
    var = jnp.mean(xc * xc, axis=-1, keepdims=True)
    r = lax.rsqrt(var + EPS)
    return xc * r, r


def _ln_bwd(dy, xh, r, g):
    dxh = dy * g
    m1 = jnp.mean(dxh, axis=-1, keepdims=True)
    m2 = jnp.mean(dxh * xh, axis=-1, keepdims=True)
    return r * (dxh - m1 - xh * m2)


def _rowsum(x):
    return jnp.sum(x, axis=0, keepdims=True)


ANY = pl.BlockSpec(memory_space=pltpu.HBM)


def _place():
    x, y, c = lax.axis_index("x"), lax.axis_index("y"), lax.axis_index("c")
    peers = [(1 - x, y), (x, 1 - y), (1 - x, 1 - y)]
    return x, y, c, 2 * x + y, (x, y, 1 - c), peers


def _half(rows, which):
    return pl.ds(which * (rows // 2), rows // 2)


def _remote(src, dst, send_sem, recv_sem, device):
    return pltpu.make_async_remote_copy(src_ref=src, dst_ref=dst, send_sem=send_sem, recv_sem=recv_sem,
                                        device_id=device, device_id_type=MESH)


class _Job:
    def __init__(self, reads, writes, ncopies, copies):
        self.reads, self.writes, self.ncopies, self.copies = reads, writes, ncopies, copies


def _share(rows, which, part, parts):
    nr = rows // 2 // parts
    return pl.ds(which * (rows // 2) + part * nr, nr)


def _job_gather_ici(name, rows, part, parts):
    def copies(src, dst, sem):
        x, y, c, k, sib, peers = _place()
        mine_rows = _share(rows, c, part, parts)
        out = []
        for j, (px, py) in enumerate(peers):
            mine = src[name].at[0, k, mine_rows]
            out.append((_remote(mine, dst[name].at[0, k, mine_rows], sem(j, 0), sem(j, 1), (px, py, c)),
                        _remote(mine, dst[name].at[0, 2 * px + py, mine_rows], sem(j, 0), sem(j, 1), (px, py, c))))
        return out
    return _Job([], [name], 3, copies)


def _job_gather_d2d(name, rows, part, parts):
    def copies(src, dst, sem):
        x, y, c, k, sib, peers = _place()
        out = []
        for j, (px, py) in enumerate(peers):
            landed = src[name].at[0, 2 * px + py, _share(rows, c, part, parts)]
            out.append((_remote(landed, dst[name].at[0, 2 * px + py, _share(rows, c, part, parts)],
                                sem(j, 0), sem(j, 1), sib),
                        _remote(landed, dst[name].at[0, 2 * px + py, _share(rows, 1 - c, part, parts)],
                                sem(j, 0), sem(j, 1), sib)))
        return out
    return _Job([], [name], 3, copies)


def _job_pair_exchange(gname, tname, rows):
    def copies(src, dst, sem):
        x, y, c, k, sib, peers = _place()
        cp = _remote(src[gname].at[:, _half(rows, 1 - c), :], dst[tname], sem(0, 0), sem(0, 1), sib)
        return [(cp, cp)]
    return _Job([gname], [tname], 1, copies)


def _job_chip_exchange(pname, lname, r0, nr):
    def copies(src, dst, sem):
        x, y, c, k, sib, peers = _place()
        out = []
        for j, (px, py) in enumerate(peers):
            cp = _remote(src[pname].at[2 * px + py, pl.ds(r0, nr)], dst[lname].at[j, pl.ds(r0, nr)],
                         sem(j, 0), sem(j, 1), (px, py, c))
            out.append((cp, cp))
        return out
    return _Job([pname], [lname], 3, copies)


def _job_pair_share(name, layer, rows):
    def copies(src, dst, sem):
        x, y, c, k, sib, peers = _place()
        mine = src[name].at[layer, _half(rows, c)]
        return [(_remote(mine, dst[name].at[layer, _half(rows, c)], sem(0, 0), sem(0, 1), sib),
                 _remote(mine, dst[name].at[layer, _half(rows, 1 - c)], sem(0, 0), sem(0, 1), sib))]
    return _Job([], [name], 1, copies)


class _Comm:
    def __init__(self, plan, jobs):
        self.plan, self.jobs = plan, jobs
        self.writes, self.reads = [], []
        for job in jobs:
            for n in job.writes:
                if n not in self.writes:
                    self.writes.append(n)
        for job in jobs:
            for n in job.reads:
                if n not in self.writes and n not in self.reads:
                    self.reads.append(n)
        self.ncopies = sum(job.ncopies for job in jobs)

    def descriptors(self, src, dst, sems, base):
        out = []
        for job in self.jobs:
            sem = lambda j, which, base=base: sems.at[base + j, which]
            out += job.copies(src, dst, sem)
            base += job.ncopies
        return out

    def start(self, src, dst, sems, base=0):
        for first, _ in self.descriptors(src, dst, sems, base):
            first.start()

    def finish(self, src, dst, sems, base=0):
        for _, landed in self.descriptors(src, dst, sems, base):
            landed.wait()


def _comm_operands(comm):
    bufs = comm.plan.bufs
    shapes = [jax.ShapeDtypeStruct(bufs[n].shape, bufs[n].dtype) for n in comm.writes]
    return [bufs[n] for n in comm.reads] + [bufs[n] for n in comm.writes], shapes


def _pallas(comm, body, *, name, grid, in_specs, out_specs, out_shape, compiler_params, scratch_shapes=(),
            aliases=None):
    aliases = dict(aliases or {})
    if comm is None:
        return pl.pallas_call(body, name=name, grid=grid, in_specs=in_specs, out_specs=out_specs,
                              out_shape=out_shape, scratch_shapes=list(scratch_shapes),
                              input_output_aliases=aliases, compiler_params=compiler_params)
    single = not isinstance(out_shape, (list, tuple))
    base_specs = [out_specs] if single else list(out_specs)
    base_shape = [out_shape] if single else list(out_shape)
    nb, nr, nw, nbo, nsc = len(in_specs), len(comm.reads), len(comm.writes), len(base_specs), len(scratch_shapes)

    def wrapped(*refs):
        base_in, rd, wr_in = refs[:nb], refs[nb:nb + nr], refs[nb + nr:nb + nr + nw]
        o0 = nb + nr + nw
        base_out, wr_out = refs[o0:o0 + nbo], refs[o0 + nbo:o0 + nbo + nw]
        scratch, sems = refs[o0 + nbo + nw:o0 + nbo + nw + nsc], refs[-1]
        src = dict(zip(comm.reads, rd))
        src.update(zip(comm.writes, wr_in))
        dst = dict(zip(comm.writes, wr_out))
        first = functools.reduce(jnp.logical_and, [pl.program_id(a) == 0 for a in range(len(grid))])
        last = functools.reduce(jnp.logical_and,
                                [pl.program_id(a) == pl.num_programs(a) - 1 for a in range(len(grid))])

        @pl.when(first)
        def _():
            comm.start(src, dst, sems)
        body(*base_in, *base_out, *scratch)

        @pl.when(last)
        def _():
            comm.finish(src, dst, sems)

    operands, shapes = _comm_operands(comm)
    call = pl.pallas_call(
        wrapped, name=name, grid=grid, in_specs=list(in_specs) + [ANY] * (nr + nw),
        out_specs=base_specs + [ANY] * nw, out_shape=base_shape + shapes,
        input_output_aliases={**aliases, **{nb + nr + q: nbo + q for q in range(nw)}},
        scratch_shapes=list(scratch_shapes) + [pltpu.SemaphoreType.DMA((comm.ncopies, 2))],
        compiler_params=compiler_params)

    def run(*args):
        outs = call(*args, *operands)
        for q, n in enumerate(comm.writes):
            comm.plan.bufs[n] = outs[nbo + q]
        return outs[0] if single else list(outs[:nbo])

    return run


def _comm_only(plan, phases, *, name):
    comms = [_Comm(plan, jobs) for jobs in phases]
    both = _Comm(plan, [job for jobs in phases for job in jobs])
    nr, nw = len(both.reads), len(both.writes)

    def body(*refs):
        rd, wr_in, wr_out, sems = refs[:nr], refs[nr:nr + nw], refs[nr + nw:nr + 2 * nw], refs[-1]
        src = dict(zip(both.reads, rd))
        src.update(zip(both.writes, wr_in))
        dst = dict(zip(both.writes, wr_out))
        base = 0
        for comm in comms:
            comm.start(src, dst, sems, base)
            comm.finish(src, dst, sems, base)
            base += comm.ncopies

    operands, shapes = _comm_operands(both)
    outs = pl.pallas_call(
        body, name=name, in_specs=[ANY] * (nr + nw), out_specs=[ANY] * nw, out_shape=shapes,
        input_output_aliases={nr + q: q for q in range(nw)},
        scratch_shapes=[pltpu.SemaphoreType.DMA((both.ncopies, 2))],
    )(*operands)
    for q, n in enumerate(both.writes):
        plan.bufs[n] = outs[q]


def _mm_nn(a, w, *, layer, tm, tn, residual=None, norm=None, out_dtype=F32, name, comm=None):
    T, K = a.shape
    if w.ndim == 4:
        _, S, _, n4 = w.shape
        N = S * n4
        bps = n4 // tn
        w_spec = pl.BlockSpec((None, None, K, tn), lambda j, i: (layer, j // bps, 0, j % bps))
    else:
        N = w.shape[2]
        w_spec = pl.BlockSpec((None, K, tn), lambda j, i: (layer, 0, j))
    in_specs = [pl.BlockSpec((tm, K), lambda j, i: (i, 0)), w_spec]
    args = [a, w]
    if residual is not None:
        in_specs.append(pl.BlockSpec((tm, tn), lambda j, i: (i, j)))
        args.append(residual)
    out_specs = pl.BlockSpec((tm, tn), lambda j, i: (i, j))
    out_shape = jax.ShapeDtypeStruct((T, N), out_dtype)
    if norm is not None:
        assert tn == N
        g, norm_layer = norm
        in_specs.append(pl.BlockSpec((None, 1, N), lambda j, i: (norm_layer, 0, 0)))
        args.append(g)
        out_specs = [out_specs, pl.BlockSpec((tm, tn), lambda j, i: (i, j))]
        out_shape = [out_shape, jax.ShapeDtypeStruct((T, N), BF16)]

    def body(*refs):
        a_ref, w_ref = refs[0], refs[1]
        acc = jnp.dot(a_ref[...].astype(BF16), w_ref[...], preferred_element_type=F32)
        if residual is not None:
            acc = refs[2][...] + acc
        if norm is None:
            refs[-1][...] = acc.astype(out_dtype)
        else:
            refs[-2][...] = acc.astype(out_dtype)
            r = lax.rsqrt(jnp.mean(acc * acc, axis=-1, keepdims=True) + EPS)
            refs[-1][...] = (acc * r * refs[-3][...]).astype(BF16)

    return _pallas(
        comm, body, name=name, grid=(N // tn, T // tm), in_specs=in_specs,
        out_specs=out_specs, out_shape=out_shape,
        compiler_params=_cparams(("parallel", "parallel"), 48),
    )(*args)


def _mm_nt(dy, w, *, layer, tm, tn, name, out_dtype=F32, comm=None):
    T = dy.shape[0]
    nt_dims = (((1,), (1,)), ((), ()))
    if w.ndim == 4:
        _, S, K, n4 = w.shape

        def body(dy_ref, w_ref, o_ref):
            @pl.when(pl.program_id(1) == 0)
            def _():
                o_ref[...] = jnp.zeros_like(o_ref)
            o_ref[...] += lax.dot_general(dy_ref[...].astype(BF16), w_ref[...], nt_dims,
                                          preferred_element_type=F32)

        return _pallas(
            comm, body, name=name, grid=(T // tm, S),
            in_specs=[pl.BlockSpec((tm, n4), lambda i, s: (i, s)),
                      pl.BlockSpec((None, None, K, n4), lambda i, s: (layer, s, 0, 0))],
            out_specs=pl.BlockSpec((tm, K), lambda i, s: (i, 0)),
            out_shape=jax.ShapeDtypeStruct((T, K), F32),
            compiler_params=_cparams(("parallel", "arbitrary"), 48),
        )(dy, w)
    _, R, N = w.shape

    def body2(dy_ref, w_ref, o_ref):
        o_ref[...] = lax.dot_general(dy_ref[...].astype(BF16), w_ref[...], nt_dims,
                                     preferred_element_type=F32).astype(out_dtype)

    return _pallas(
        comm, body2, name=name, grid=(R // tn, T // tm),
        in_specs=[pl.BlockSpec((tm, N), lambda j, i: (i, 0)),
                  pl.BlockSpec((None, tn, N), lambda j, i: (layer, j, 0))],
        out_specs=pl.BlockSpec((tm, tn), lambda j, i: (i, j)),
        out_shape=jax.ShapeDtypeStruct((T, R), out_dtype),
        compiler_params=_cparams(("parallel", "parallel"), 48),
    )(dy, w)


def _mm_tn(a, dy, *, shards, tk, tn, tt, name, comm=None):
    T, K = a.shape
    N = dy.shape[1]
    tn_dims = (((0,), (0,)), ((), ()))

    def body(a_ref, dy_ref, o_ref):
        @pl.when(pl.program_id(2) == 0)
        def _():
            o_ref[...] = jnp.zeros_like(o_ref)
        o_ref[...] += lax.dot_general(a_ref[...].astype(BF16), dy_ref[...].astype(BF16), tn_dims,
                                      preferred_element_type=F32)

    if shards is None:
        out_spec = pl.BlockSpec((tk, tn), lambda k, n, t: (k, n))
        out_shape = jax.ShapeDtypeStruct((K, N), F32)
    else:
        n4 = N // shards
        bps = n4 // tn
        out_spec = pl.BlockSpec((None, tk, tn), lambda k, n, t: (n // bps, k, n % bps))
        out_shape = jax.ShapeDtypeStruct((shards, K, n4), F32)
    return _pallas(
        comm, body, name=name, grid=(K // tk, N // tn, T // tt),
        in_specs=[pl.BlockSpec((tt, tk), lambda k, n, t: (t, k)),
                  pl.BlockSpec((tt, tn), lambda k, n, t: (t, n))],
        out_specs=out_spec, out_shape=out_shape,
        compiler_params=_cparams(("parallel", "parallel", "arbitrary"), 48),
    )(a, dy)


def _rmsnorm_fwd(x, g, *, layer, tm, name, comm=None):
    T, D = x.shape

    def body(x_ref, g_ref, h_ref):
        xf = x_ref[...]
        r = lax.rsqrt(jnp.mean(xf * xf, axis=-1, keepdims=True) + EPS)
        h_ref[...] = (xf * r * g_ref[...]).astype(BF16)

    return _pallas(
        comm, body, name=name, grid=(T // tm,),
        in_specs=[pl.BlockSpec((tm, D), lambda i: (i, 0)),
                  pl.BlockSpec((None, 1, D), lambda i: (layer, 0, 0))],
        out_specs=pl.BlockSpec((tm, D), lambda i: (i, 0)),
        out_shape=jax.ShapeDtypeStruct((T, D), BF16),
        compiler_params=_cparams(("parallel",), 32),
    )(x, g)


def _rmsnorm_bwd(x, g, dh, dres, *, layer, tm, name, comm=None):
    T, D = x.shape

    def body(x_ref, g_ref, dh_ref, dres_ref, dx_ref, dg_ref):
        @pl.when(pl.program_id(0) == 0)
        def _():
            dg_ref[...] = jnp.zeros_like(dg_ref)
        xf = x_ref[...]
        r = lax.rsqrt(jnp.mean(xf * xf, axis=-1, keepdims=True) + EPS)
        xh = xf * r
        dh = dh_ref[...]
        dg_ref[...] += _rowsum(dh * xh)
        dxh = dh * g_ref[...]
        dx_ref[...] = dres_ref[...] + r * (dxh - xh * jnp.mean(dxh * xh, axis=-1, keepdims=True))

    return _pallas(
        comm, body, name=name, grid=(T // tm,),
        in_specs=[pl.BlockSpec((tm, D), lambda i: (i, 0)),
                  pl.BlockSpec((None, 1, D), lambda i: (layer, 0, 0)),
                  pl.BlockSpec((tm, D), lambda i: (i, 0)),
                  pl.BlockSpec((tm, D), lambda i: (i, 0))],
        out_specs=[pl.BlockSpec((tm, D), lambda i: (i, 0)),
                   pl.BlockSpec((1, D), lambda i: (0, 0))],
        out_shape=[jax.ShapeDtypeStruct((T, D), F32), jax.ShapeDtypeStruct((1, D), F32)],
        compiler_params=_cparams(("arbitrary",), 40),
    )(x, g, dh, dres)


def _rmsnorm_bwd_math(xf, g, dh, dres):
    r = lax.rsqrt(jnp.mean(xf * xf, axis=-1, keepdims=True) + EPS)
    xh = xf * r
    dxh = dh * g
    dx = dres + r * (dxh - xh * jnp.mean(dxh * xh, axis=-1, keepdims=True))
    return dx, _rowsum(dh * xh)


def _mm_nt_norm(dy, w, x, g, dres, *, g_layer, tm, name, comm=None):
    T = dy.shape[0]
    _, S, K, n4 = w.shape
    nt_dims = (((1,), (1,)), ((), ()))

    def body(dy_ref, w_ref, x_ref, g_ref, dres_ref, dx_ref, dg_ref):
        @pl.when(pl.program_id(0) == 0)
        def _():
            dg_ref[...] = jnp.zeros_like(dg_ref)
        dh = None
        for s in range(S):
            part = lax.dot_general(dy_ref[:, s * n4:(s + 1) * n4].astype(BF16), w_ref[s], nt_dims,
                                   preferred_element_type=F32)
            dh = part if dh is None else dh + part
        dx, dg = _rmsnorm_bwd_math(x_ref[...], g_ref[...], dh, dres_ref[...])
        dx_ref[...] = dx
        dg_ref[...] += dg

    row = lambda i: (i, 0)
    return _pallas(
        comm, body, name=name, grid=(T // tm,),
        in_specs=[pl.BlockSpec((tm, S * n4), row),
                  pl.BlockSpec((None, S, K, n4), lambda i: (0, 0, 0, 0)),
                  pl.BlockSpec((tm, K), row),
                  pl.BlockSpec((None, 1, K), lambda i: (g_layer, 0, 0)),
                  pl.BlockSpec((tm, K), row)],
        out_specs=[pl.BlockSpec((tm, K), row), pl.BlockSpec((1, K), lambda i: (0, 0))],
        out_shape=[jax.ShapeDtypeStruct((T, K), F32), jax.ShapeDtypeStruct((1, K), F32)],
        compiler_params=_cparams(("arbitrary",), 56),
    )(dy, w, x, g, dres)


def _loss_head(x, tgt, g, *, tm, name, comm=None):
    T, D = x.shape

    def body(x_ref, t_ref, g_ref, loss_ref, dx_ref, dg_ref):
        @pl.when(pl.program_id(0) == 0)
        def _():
            dg_ref[...] = jnp.zeros_like(dg_ref)
            loss_ref[...] = jnp.zeros_like(loss_ref)
        xf = x_ref[...]
        gg = g_ref[...]
        r = lax.rsqrt(jnp.mean(xf * xf, axis=-1, keepdims=True) + EPS)
        xh = xf * r
        err = xh * gg - t_ref[...]
        row = jnp.mean(err * err, axis=-1, keepdims=True)
        loss_ref[...] += 0.5 * jnp.sum(row, axis=0, keepdims=True)
        dy = err * (1.0 / D)
        dg_ref[...] += _rowsum(dy * xh)
        dxh = dy * gg
        dx_ref[...] = r * (dxh - xh * jnp.mean(dxh * xh, axis=-1, keepdims=True))

    return _pallas(
        comm, body, name=name, grid=(T // tm,),
        in_specs=[pl.BlockSpec((tm, D), lambda i: (i, 0)),
                  pl.BlockSpec((tm, D), lambda i: (i, 0)),
                  pl.BlockSpec((1, D), lambda i: (0, 0))],
        out_specs=[pl.BlockSpec((1, 1), lambda i: (0, 0)),
                   pl.BlockSpec((tm, D), lambda i: (i, 0)),
                   pl.BlockSpec((1, D), lambda i: (0, 0))],
        out_shape=[jax.ShapeDtypeStruct((1, 1), F32), jax.ShapeDtypeStruct((T, D), F32),
                   jax.ShapeDtypeStruct((1, D), F32)],
        compiler_params=_cparams(("arbitrary",), 40),
    )(x, tgt, g)


CONV_ROWS = 64
CONV_COLS = 256


def _halo_prev_index(tm, halo):
    per = tm // halo
    return lambda i: jnp.maximum(i * per - 1, 0)


def _halo_next_index(tm, halo, total):
    per = tm // halo
    last = total // halo - 1
    return lambda i: jnp.minimum((i + 1) * per, last)


def _causal_mask():
    t = lax.broadcasted_iota(jnp.int32, (CHUNK, CHUNK), 0)
    s = lax.broadcasted_iota(jnp.int32, (CHUNK, CHUNK), 1)
    return s <= t


def _mixer_ab_fwd(z, a_ln_g, a_ln_b, w_s, b_s, conv_w, conv_b, b_ln_g, b_ln_b, *, tm, name, comm=None):
    T = z.shape[0]
    nchunk = tm // CHUNK
    halo = HALO_LONG

    def body(za_ref, zb_ref, zh_ref, alg_ref, alb_ref, ws_ref, bs_ref, cw_ref, cbias_ref,
             blg_ref, blb_ref, y_ref, cb_ref, ext_ref):
        i = pl.program_id(0)
        gu = _gelu(za_ref[:, :D_A].astype(F32))
        gv = _gelu(za_ref[:, D_A:].astype(F32))
        xh, _ = _ln_stats(gv)
        lv = (xh * alg_ref[...] + alb_ref[...]).astype(BF16)
        mask = _causal_mask()
        for h in range(A_HEADS):
            wm = jnp.where(mask, ws_ref[h], 0.0).astype(BF16)
            cols = slice(h * HEAD_DIM, (h + 1) * HEAD_DIM)
            for c in range(nchunk):
                rows = slice(c * CHUNK, (c + 1) * CHUNK)
                mixed = jnp.dot(wm, lv[rows, cols], preferred_element_type=F32) + bs_ref[h]
                y_ref[rows, cols] = (gu[rows, cols] * mixed).astype(BF16)
        ext_ref[halo:halo + tm, :] = zb_ref[:, :D_B].astype(F32) * _sigmoid(zb_ref[:, D_B:].astype(F32))
        prev = zh_ref[:, :D_B].astype(F32) * _sigmoid(zh_ref[:, D_B:].astype(F32))
        ext_ref[0:halo, :] = jnp.where(i > 0, prev, 0.0)
        for rb in range(tm // CONV_ROWS):
            for cb in range(D_B // CONV_COLS):
                cs = slice(cb * CONV_COLS, (cb + 1) * CONV_COLS)
                window = ext_ref[rb * CONV_ROWS:rb * CONV_ROWS + CONV_ROWS + halo, cs]
                acc = jnp.zeros((CONV_ROWS, CONV_COLS), F32)
                for k in range(B_CONV):
                    shifted = _rows_after(window, halo - (B_CONV - 1) + k)[:CONV_ROWS]
                    acc = acc + cw_ref[k:k + 1, cs] * shifted
                cb_ref[rb * CONV_ROWS:(rb + 1) * CONV_ROWS, cs] = acc + cbias_ref[:, cs]
        xhb, _ = _ln_stats(cb_ref[...])
        y_ref[:, D_A:] = _silu(xhb * blg_ref[...] + blb_ref[...]).astype(BF16)

    row = lambda i: (i, 0)
    par = lambda i: (0, 0)
    return _pallas(
        comm, body, name=name, grid=(T // tm,),
        in_specs=[pl.BlockSpec((tm, 2 * D_A), lambda i: (i, 0)),
                  pl.BlockSpec((tm, 2 * D_B), lambda i: (i, 1)),
                  pl.BlockSpec((halo, 2 * D_B), lambda i: (_halo_prev_index(tm, halo)(i), 1)),
                  pl.BlockSpec((1, D_A), par), pl.BlockSpec((1, D_A), par),
                  pl.BlockSpec((A_HEADS, CHUNK, CHUNK), lambda i: (0, 0, 0)),
                  pl.BlockSpec((A_HEADS, CHUNK, 1), lambda i: (0, 0, 0)),
                  pl.BlockSpec((B_CONV, D_B), par), pl.BlockSpec((1, D_B), par),
                  pl.BlockSpec((1, D_B), par), pl.BlockSpec((1, D_B), par)],
        out_specs=[pl.BlockSpec((tm, D_A + D_B), row), pl.BlockSpec((tm, D_B), row)],
        out_shape=[jax.ShapeDtypeStruct((T, D_A + D_B), BF16), jax.ShapeDtypeStruct((T, D_B), F32)],
        scratch_shapes=[pltpu.VMEM((halo + tm, D_B), F32)],
        compiler_params=_cparams(("parallel",), 40),
    )(z, z, z, a_ln_g, a_ln_b, w_s, b_s, conv_w, conv_b, b_ln_g, b_ln_b)


def _mixer_ab_bwd_pre(z, cb, dy, a_ln_g, a_ln_b, w_s, b_s, b_ln_g, b_ln_b, *, tm, name, comm=None):
    T = z.shape[0]
    nchunk = tm // CHUNK
    tn_dims = (((0,), (0,)), ((), ()))
    nt_dims = (((1,), (1,)), ((), ()))

    def body(za_ref, cb_ref, dy_ref, alg_ref, alb_ref, ws_ref, bs_ref, blg_ref, blb_ref,
             dza_ref, dcb_ref, dalg_ref, dalb_ref, dws_ref, dbs_ref, dblg_ref, dblb_ref,
             dlv_ref):
        @pl.when(pl.program_id(0) == 0)
        def _():
            for ref in (dalg_ref, dalb_ref, dws_ref, dbs_ref, dblg_ref, dblb_ref):
                ref[...] = jnp.zeros_like(ref)
        ua = za_ref[:, :D_A].astype(F32)
        va = za_ref[:, D_A:].astype(F32)
        gu = _gelu(ua)
        gv = _gelu(va)
        xh, r = _ln_stats(gv)
        alg = alg_ref[...]
        lv = (xh * alg + alb_ref[...]).astype(BF16)
        dya = dy_ref[:, :D_A].astype(F32)
        mask = _causal_mask()
        for h in range(A_HEADS):
            wm = jnp.where(mask, ws_ref[h], 0.0).astype(BF16)
            cols = slice(h * HEAD_DIM, (h + 1) * HEAD_DIM)
            dwm = jnp.zeros((CHUNK, CHUNK), F32)
            dbs = jnp.zeros((CHUNK, 1), F32)
            for c in range(nchunk):
                rows = slice(c * CHUNK, (c + 1) * CHUNK)
                lvb = lv[rows, cols]
                mixed = jnp.dot(wm, lvb, preferred_element_type=F32) + bs_ref[h]
                dyb = dya[rows, cols]
                dza_ref[rows, cols] = (dyb * mixed * _dgelu(ua[rows, cols])).astype(BF16)
                dmixed = dyb * gu[rows, cols]
                dmb = dmixed.astype(BF16)
                dlv_ref[rows, cols] = lax.dot_general(wm, dmb, tn_dims, preferred_element_type=F32)
                dwm = dwm + lax.dot_general(dmb, lvb, nt_dims, preferred_element_type=F32)
                dbs = dbs + jnp.sum(dmixed, axis=1, keepdims=True)
            dws_ref[h] += jnp.where(mask, dwm, 0.0)
            dbs_ref[h] += dbs
        dlv = dlv_ref[...]
        dalg_ref[...] += _rowsum(dlv * xh)
        dalb_ref[...] += _rowsum(dlv)
        dgv = _ln_bwd(dlv, xh, r, alg)
        dza_ref[:, D_A:] = (dgv * _dgelu(va)).astype(BF16)
        xhb, rb = _ln_stats(cb_ref[...])
        blg = blg_ref[...]
        lb = xhb * blg + blb_ref[...]
        dlb = dy_ref[:, D_A:].astype(F32) * _dsilu(lb)
        dblg_ref[...] += _rowsum(dlb * xhb)
        dblb_ref[...] += _rowsum(dlb)
        dcb_ref[...] = _ln_bwd(dlb, xhb, rb, blg)

    row = lambda i: (i, 0)
    par = lambda i: (0, 0)
    par3 = lambda i: (0, 0, 0)
    return _pallas(
        comm, body, name=name, grid=(T // tm,),
        in_specs=[pl.BlockSpec((tm, 2 * D_A), row), pl.BlockSpec((tm, D_B), row),
                  pl.BlockSpec((tm, D_A + D_B), row),
                  pl.BlockSpec((1, D_A), par), pl.BlockSpec((1, D_A), par),
                  pl.BlockSpec((A_HEADS, CHUNK, CHUNK), par3),
                  pl.BlockSpec((A_HEADS, CHUNK, 1), par3),
                  pl.BlockSpec((1, D_B), par), pl.BlockSpec((1, D_B), par)],
        out_specs=[pl.BlockSpec((tm, 2 * D_A), row), pl.BlockSpec((tm, D_B), row),
                   pl.BlockSpec((1, D_A), par), pl.BlockSpec((1, D_A), par),
                   pl.BlockSpec((A_HEADS, CHUNK, CHUNK), par3),
                   pl.BlockSpec((A_HEADS, CHUNK, 1), par3),
                   pl.BlockSpec((1, D_B), par), pl.BlockSpec((1, D_B), par)],
        out_shape=[jax.ShapeDtypeStruct((T, 2 * D_A + 2 * D_B), BF16), jax.ShapeDtypeStruct((T, D_B), F32),
                   jax.ShapeDtypeStruct((1, D_A), F32), jax.ShapeDtypeStruct((1, D_A), F32),
                   jax.ShapeDtypeStruct((A_HEADS, CHUNK, CHUNK), F32),
                   jax.ShapeDtypeStruct((A_HEADS, CHUNK, 1), F32),
                   jax.ShapeDtypeStruct((1, D_B), F32), jax.ShapeDtypeStruct((1, D_B), F32)],
        scratch_shapes=[pltpu.VMEM((tm, D_A), F32)],
        compiler_params=_cparams(("arbitrary",), 40),
    )(z, cb, dy, a_ln_g, a_ln_b, w_s, b_s, b_ln_g, b_ln_b)


def _mixer_b_conv_bwd(z, dcb, conv_w, dz, *, tm, name, comm=None):
    T = z.shape[0]
    halo = HALO_LONG

    def body(zb_ref, dcb_ref, dcn_ref, cw_ref, dz_in_ref, dzb_ref, dcw_ref, dbias_ref, dext_ref):
        i = pl.program_id(0)
        last = pl.num_programs(0) - 1

        @pl.when(i == 0)
        def _():
            dcw_ref[...] = jnp.zeros_like(dcw_ref)
            dbias_ref[...] = jnp.zeros_like(dbias_ref)
        dcb = dcb_ref[...]
        dext_ref[0:tm, :] = dcb
        dext_ref[tm:tm + halo, :] = jnp.where(i < last, dcn_ref[...], 0.0)
        dbias_ref[...] += _rowsum(dcb)
        for rb in range(tm // CONV_ROWS):
            for cb in range(D_B // CONV_COLS):
                cs = slice(cb * CONV_COLS, (cb + 1) * CONV_COLS)
                gcs = slice(D_B + cb * CONV_COLS, D_B + (cb + 1) * CONV_COLS)
                rs = slice(rb * CONV_ROWS, (rb + 1) * CONV_ROWS)
                xbb = zb_ref[rs, cs].astype(F32)
                sgb = _sigmoid(zb_ref[rs, gcs].astype(F32))
                yb0 = xbb * sgb
                window = dext_ref[rb * CONV_ROWS:rb * CONV_ROWS + CONV_ROWS + halo, cs]
                acc = jnp.zeros((CONV_ROWS, CONV_COLS), F32)
                for k in range(B_CONV):
                    shifted = _rows_after(window, (B_CONV - 1) - k)[:CONV_ROWS]
                    acc = acc + cw_ref[k:k + 1, cs] * shifted
                    dcw_ref[k:k + 1, cs] += _rowsum(shifted * yb0)
                dzb_ref[rs, cs] = (acc * sgb).astype(BF16)
                dzb_ref[rs, gcs] = (acc * xbb * sgb * (1.0 - sgb)).astype(BF16)

    row = lambda i: (i, 0)
    par = lambda i: (0, 0)
    return _pallas(
        comm, body, name=name, grid=(T // tm,),
        in_specs=[pl.BlockSpec((tm, 2 * D_B), lambda i: (i, 1)),
                  pl.BlockSpec((tm, D_B), row),
                  pl.BlockSpec((halo, D_B), lambda i: (_halo_next_index(tm, halo, T)(i), 0)),
                  pl.BlockSpec((B_CONV, D_B), par), pl.BlockSpec(memory_space=pl.ANY)],
        out_specs=[pl.BlockSpec((tm, 2 * D_B), lambda i: (i, 1)), pl.BlockSpec((B_CONV, D_B), par),
                   pl.BlockSpec((1, D_B), par)],
        out_shape=[jax.ShapeDtypeStruct(dz.shape, BF16), jax.ShapeDtypeStruct((B_CONV, D_B), F32),
                   jax.ShapeDtypeStruct((1, D_B), F32)],
        scratch_shapes=[pltpu.VMEM((tm + halo, D_B), F32)], aliases={4: 0},
        compiler_params=_cparams(("arbitrary",), 40),
    )(z, dcb, dcb, conv_w, dz)


def _rows_before(x, a):
    return x if a == 0 else pltpu.roll(x, a, axis=0)


def _rows_after(x, a):
    return x if a == 0 else pltpu.roll(x, x.shape[0] - a, axis=0)


def _conv3(w_ref, x, halo, cs):
    acc = w_ref[2:3, cs] * x[halo:]
    acc = acc + w_ref[1:2, cs] * _rows_before(x, 1)[halo:]
    return acc + w_ref[0:1, cs] * _rows_before(x, 2)[halo:]


def _mixer_c_fwd(z, conv_w, *, tm, name, comm=None):
    T = z.shape[0]
    D = D_MODEL
    halo = HALO_SHORT
    W = CONV_COLS

    def body(bg_ref, cg_ref, xv_ref, cgh_ref, xvh_ref, w_ref, r_ref):
        i = pl.program_id(0)
        for cb in range(D // W):
            cs = slice(cb * W, (cb + 1) * W)
            prev = jnp.where(i > 0, cgh_ref[:, cs].astype(F32) * xvh_ref[:, cs].astype(F32), 0.0)
            p = jnp.concatenate([prev, cg_ref[:, cs].astype(F32) * xv_ref[:, cs].astype(F32)], axis=0)
            r_ref[:, cs] = (bg_ref[:, cs].astype(F32) * _conv3(w_ref, p, halo, cs)).astype(BF16)

    hp = _halo_prev_index(tm, halo)
    return _pallas(
        comm, body, name=name, grid=(T // tm,),
        in_specs=[pl.BlockSpec((tm, D), lambda i: (i, 0)), pl.BlockSpec((tm, D), lambda i: (i, 1)),
                  pl.BlockSpec((tm, D), lambda i: (i, 2)),
                  pl.BlockSpec((halo, D), lambda i: (hp(i), 1)),
                  pl.BlockSpec((halo, D), lambda i: (hp(i), 2)),
                  pl.BlockSpec((None, C_CONV, D), lambda i: (0, 0, 0))],
        out_specs=pl.BlockSpec((tm, D), lambda i: (i, 0)),
        out_shape=jax.ShapeDtypeStruct((T, D), BF16),
        compiler_params=_cparams(("parallel",), 40),
    )(z, z, z, z, z, conv_w)


def _mixer_c_bwd(z, dr, conv_w, *, tm, name, comm=None):
    T = z.shape[0]
    D = D_MODEL
    halo = HALO_SHORT
    W = CONV_COLS

    def body(bg_ref, cg_ref, xv_ref, cgh_ref, xvh_ref, bgn_ref, dr_ref, drn_ref, w_ref, dz_ref, dw_ref):
        i = pl.program_id(0)
        last = pl.num_programs(0) - 1

        @pl.when(i == 0)
        def _():
            dw_ref[...] = jnp.zeros_like(dw_ref)
        for cb in range(D // W):
            cs = slice(cb * W, (cb + 1) * W)
            cg = cg_ref[:, cs].astype(F32)
            xv = xv_ref[:, cs].astype(F32)
            dr = dr_ref[:, cs].astype(F32)
            p = cg * xv
            prev = jnp.where(i > 0, cgh_ref[:, cs].astype(F32) * xvh_ref[:, cs].astype(F32), 0.0)
            q = _conv3(w_ref, jnp.concatenate([prev, p], axis=0), halo, cs)
            dz_ref[:, cs] = (dr * q).astype(BF16)
            nxt = jnp.where(i < last, drn_ref[:, cs].astype(F32) * bgn_ref[:, cs].astype(F32), 0.0)
            dq = jnp.concatenate([dr * bg_ref[:, cs].astype(F32), nxt], axis=0)
            dp = None
            for k in range(C_CONV):
                shifted = _rows_after(dq, 2 - k)[:tm]
                term = w_ref[k:k + 1, cs] * shifted
                dp = term if dp is None else dp + term
                dw_ref[k:k + 1, cs] += _rowsum(shifted * p)
            dz_ref[:, D + cb * W:D + (cb + 1) * W] = (dp * xv).astype(BF16)
            dz_ref[:, 2 * D + cb * W:2 * D + (cb + 1) * W] = (dp * cg).astype(BF16)

    hp = _halo_prev_index(tm, halo)
    hn = _halo_next_index(tm, halo, T)
    return _pallas(
        comm, body, name=name, grid=(T // tm,),
        in_specs=[pl.BlockSpec((tm, D), lambda i: (i, 0)), pl.BlockSpec((tm, D), lambda i: (i, 1)),
                  pl.BlockSpec((tm, D), lambda i: (i, 2)),
                  pl.BlockSpec((halo, D), lambda i: (hp(i), 1)),
                  pl.BlockSpec((halo, D), lambda i: (hp(i), 2)),
                  pl.BlockSpec((halo, D), lambda i: (hn(i), 0)),
                  pl.BlockSpec((tm, D), lambda i: (i, 0)),
                  pl.BlockSpec((halo, D), lambda i: (hn(i), 0)),
                  pl.BlockSpec((None, C_CONV, D), lambda i: (0, 0, 0))],
        out_specs=[pl.BlockSpec((tm, 3 * D), lambda i: (i, 0)),
                   pl.BlockSpec((C_CONV, D), lambda i: (0, 0))],
        out_shape=[jax.ShapeDtypeStruct((T, 3 * D), BF16), jax.ShapeDtypeStruct((C_CONV, D), F32)],
        compiler_params=_cparams(("arbitrary",), 48),
    )(z, z, z, z, z, z, dr, dr, conv_w)


FFN_COLS = 128


def _ffn_act_fwd(up, conv_w, *, layer, tm, name, comm=None):
    T = up.shape[0]
    halo = HALO_SHORT
    W = FFN_COLS

    def body(up_ref, uph_ref, w_ref, a_ref, upc_ref):
        i = pl.program_id(0)

        def conv(cs):
            prev = jnp.where(i > 0, uph_ref[:, cs], jnp.zeros((halo, W), BF16))
            return _conv3(w_ref, jnp.concatenate([prev, up_ref[:, cs]], axis=0).astype(F32), halo, cs)

        for cb in range(D_FF // W):
            gs = slice(cb * W, (cb + 1) * W)
            vs = slice(D_FF + cb * W, D_FF + (cb + 1) * W)
            g = conv(gs)
            v = conv(vs)
            upc_ref[:, gs] = g.astype(BF16)
            upc_ref[:, vs] = v.astype(BF16)
            a_ref[:, gs] = (_silu(g) * v).astype(BF16)

    return _pallas(
        comm, body, name=name, grid=(T // tm,),
        in_specs=[pl.BlockSpec((tm, 2 * D_FF), lambda i: (i, 0)),
                  pl.BlockSpec((halo, 2 * D_FF), lambda i: (_halo_prev_index(tm, halo)(i), 0)),
                  pl.BlockSpec((None, F_CONV, 2 * D_FF), lambda i: (layer, 0, 0))],
        out_specs=[pl.BlockSpec((tm, D_FF), lambda i: (i, 0)),
                   pl.BlockSpec((tm, 2 * D_FF), lambda i: (i, 0))],
        out_shape=[jax.ShapeDtypeStruct((T, D_FF), BF16), jax.ShapeDtypeStruct((T, 2 * D_FF), BF16)],
        compiler_params=_cparams(("parallel",), 48),
    )(up, up, conv_w)


def _ffn_act_bwd(up, upc, da, conv_w, *, layer, tm, name, comm=None):
    T = up.shape[0]
    halo = HALO_SHORT
    W = FFN_COLS

    def body(up_ref, upc_ref, upcn_ref, da_ref, dan_ref, w_ref, dup_ref, dw_ref):
        i = pl.program_id(0)
        last = pl.num_programs(0) - 1

        @pl.when(i == 0)
        def _():
            dw_ref[...] = jnp.zeros_like(dw_ref)
        live = jnp.where(i < last, 1.0, 0.0)
        for cb in range(D_FF // W):
            gs = slice(cb * W, (cb + 1) * W)
            vs = slice(D_FF + cb * W, D_FF + (cb + 1) * W)
            g = jnp.concatenate([upc_ref[:, gs], upcn_ref[:, gs]], axis=0).astype(F32)
            v = jnp.concatenate([upc_ref[:, vs], upcn_ref[:, vs]], axis=0).astype(F32)
            da = jnp.concatenate([da_ref[:, gs].astype(F32), dan_ref[:, gs].astype(F32) * live], axis=0)
            s = _sigmoid(g)
            silu = g * s
            grads = (da * v * (s * (1.0 + g * (1.0 - s))), da * silu)
            for cs, d in zip((gs, vs), grads):
                u = up_ref[:, cs].astype(F32)
                acc = None
                for k in range(F_CONV):
                    shifted = _rows_after(d, 2 - k)[:tm]
                    term = w_ref[k:k + 1, cs] * shifted
                    acc = term if acc is None else acc + term
                    dw_ref[k:k + 1, cs] += _rowsum(shifted * u)
                dup_ref[:, cs] = acc.astype(BF16)

    hn = _halo_next_index(tm, halo, T)
    return _pallas(
        comm, body, name=name, grid=(T // tm,),
        in_specs=[pl.BlockSpec((tm, 2 * D_FF), lambda i: (i, 0)),
                  pl.BlockSpec((tm, 2 * D_FF), lambda i: (i, 0)),
                  pl.BlockSpec((halo, 2 * D_FF), lambda i: (hn(i), 0)),
                  pl.BlockSpec((tm, D_FF), lambda i: (i, 0)),
                  pl.BlockSpec((halo, D_FF), lambda i: (hn(i), 0)),
                  pl.BlockSpec((None, F_CONV, 2 * D_FF), lambda i: (layer, 0, 0))],
        out_specs=[pl.BlockSpec((tm, 2 * D_FF), lambda i: (i, 0)),
                   pl.BlockSpec((F_CONV, 2 * D_FF), lambda i: (0, 0))],
        out_shape=[jax.ShapeDtypeStruct((T, 2 * D_FF), BF16),
                   jax.ShapeDtypeStruct((F_CONV, 2 * D_FF), F32)],
        compiler_params=_cparams(("arbitrary",), 56),
    )(up, upc, upc, da, da, conv_w)


def _local_step(x, tgt, small, plan):
    T = x.shape[0]
    tm_e = _pick(T, 256)
    tm_n = _pick(T, 512)
    tm = _pick(T, 1024)
    tm_r = _pick(T, 2048)
    tm_f = _pick(T, 512)
    tt = _pick(T, 2048)
    nm = small["norm_mix"].reshape(2, 1, D_MODEL)
    nf = small["norm_ffn"].reshape(2, 1, D_MODEL)
    ngf = small["norm_final"].reshape(1, D_MODEL)
    b_s = small["a_b_s"].reshape(A_HEADS, CHUNK, 1)
    w_s = small["a_w_s"].reshape(A_HEADS, CHUNK, CHUNK)
    b_conv_w = small["b_conv_w"].reshape(B_CONV, D_B)
    sg = {}
    wt, cm = plan.weight, plan.comm

    h_m0 = _rmsnorm_fwd(x, nm, layer=0, tm=tm_n, name="norm_mix0")
    z_ab = _mm_nn(h_m0, wt("ab_w_in", 0), layer=0, tm=tm, tn=512, out_dtype=BF16, name="ab_in", comm=cm("ab_in"))
    yab, cb = _mixer_ab_fwd(z_ab, small["a_ln_g"], small["a_ln_b"], w_s, b_s, b_conv_w, small["b_conv_b"],
                            small["b_ln_g"], small["b_ln_b"], tm=tm_e, name="mixer_ab", comm=cm("mixer_ab"))
    x1, h_f0 = _mm_nn(yab, wt("ab_w_out", 0), layer=0, tm=tm, tn=D_MODEL, residual=x, norm=(nf, 0),
                      name="ab_out", comm=cm("ab_out"))

    def ffn_fwd(xin, h, layer, norm):
        up = _mm_nn(h, wt("f_w_up", layer), layer=0, tm=tm, tn=1408, out_dtype=BF16, name=f"ffn_up{layer}",
                    comm=cm(f"ffn_up{layer}"))
        a, upc = _ffn_act_fwd(up, small["f_conv_w"], layer=layer, tm=tm_e, name=f"ffn_act{layer}",
                              comm=cm(f"ffn_act{layer}"))
        out = _mm_nn(a, wt("f_w_down", layer), layer=0, tm=tm, tn=D_MODEL, residual=xin, norm=norm,
                     name=f"ffn_down{layer}", comm=cm(f"ffn_down{layer}"))
        return up, upc, a, out

    up0, upc0, a0, (x2, h_m1) = ffn_fwd(x1, h_f0, 0, (nm, 1))
    z_c = _mm_nn(h_m1, wt("c_w_in", 0), layer=0, tm=tm, tn=768, out_dtype=BF16, name="c_in", comm=cm("c_in"))
    r = _mixer_c_fwd(z_c, small["c_conv_w"], tm=tm_e, name="mixer_c", comm=cm("mixer_c"))
    x3, h_f1 = _mm_nn(r, wt("c_w_out", 0), layer=0, tm=tm, tn=D_MODEL, residual=x2, norm=(nf, 1),
                      name="c_out", comm=cm("c_out"))
    up1, upc1, a1, x4 = ffn_fwd(x3, h_f1, 1, None)
    loss, dx, sg["norm_final"] = _loss_head(x4, tgt, ngf, tm=tm_n, name="loss_head")

    def ffn_bwd(dx, xin, h, up, upc, a, layer):
        da = _mm_nt(dx, wt("f_w_down", layer), layer=0, tm=tm, tn=1408, out_dtype=BF16,
                    name=f"ffn_down_dx{layer}", comm=cm(f"ffn_down_dx{layer}"))
        plan.grad_ready("f_w_down", layer, _mm_tn(a, dx, shards=None, tk=1408, tn=1024, tt=tt,
                                                  name=f"ffn_down_dw{layer}", comm=cm(f"ffn_down_dw{layer}")))
        dup, dcw = _ffn_act_bwd(up, upc, da, small["f_conv_w"], layer=layer, tm=tm_e, name=f"ffn_act_bwd{layer}",
                                comm=cm(f"ffn_act_bwd{layer}"))
        dxin, dg = _mm_nt_norm(dup, wt("f_w_up", layer), xin, nf, dx, g_layer=layer, tm=tm_f,
                               name=f"ffn_up_dx{layer}", comm=cm(f"ffn_up_dx{layer}"))
        plan.grad_ready("f_w_up", layer, _mm_tn(h, dup, shards=N_CHIPS, tk=1024, tn=1408, tt=tt,
                                                name=f"ffn_up_dw{layer}", comm=cm(f"ffn_up_dw{layer}")))
        return dxin, dg, dcw

    dx, dnf1, dfc1 = ffn_bwd(dx, x3, h_f1, up1, upc1, a1, 1)
    dr = _mm_nt(dx, wt("c_w_out", 0), layer=0, tm=tm, tn=512, out_dtype=BF16, name="c_out_dx", comm=cm("c_out_dx"))
    plan.grad_ready("c_w_out", 0, _mm_tn(r, dx, shards=None, tk=1024, tn=1024, tt=tt, name="c_out_dw",
                                         comm=cm("c_out_dw")))
    dz_c, dccw = _mixer_c_bwd(z_c, dr, small["c_conv_w"], tm=tm_e, name="mixer_c_bwd", comm=cm("mixer_c_bwd"))
    sg["c_conv_w"] = dccw.reshape(1, C_CONV, D_MODEL)
    plan.grad_ready("c_w_in", 0, _mm_tn(h_m1, dz_c, shards=N_CHIPS, tk=1024, tn=768, tt=tt, name="c_in_dw",
                                        comm=cm("c_in_dw")))
    dx, dnm1 = _mm_nt_norm(dz_c, wt("c_w_in", 0), x2, nm, dx, g_layer=1, tm=tm_f, name="c_in_dx",
                           comm=cm("c_in_dx"))
    dx, dnf0, dfc0 = ffn_bwd(dx, x1, h_f0, up0, upc0, a0, 0)
    dyab = _mm_nt(dx, wt("ab_w_out", 0), layer=0, tm=tm, tn=512, out_dtype=BF16, name="ab_out_dx",
                  comm=cm("ab_out_dx"))
    plan.grad_ready("ab_w_out", 0, _mm_tn(yab, dx, shards=None, tk=1024, tn=1024, tt=tt, name="ab_out_dw",
                                          comm=cm("ab_out_dw")))
    (dza, dcb, sg["a_ln_g"], sg["a_ln_b"], dws, dbs, sg["b_ln_g"], sg["b_ln_b"]) = _mixer_ab_bwd_pre(
        z_ab, cb, dyab, small["a_ln_g"], small["a_ln_b"], w_s, b_s, small["b_ln_g"], small["b_ln_b"],
        tm=tm_e, name="mixer_ab_bwd", comm=cm("mixer_ab_bwd"))
    dz_ab, dbcw, sg["b_conv_b"] = _mixer_b_conv_bwd(z_ab, dcb, b_conv_w, dza, tm=tm_e, name="mixer_b_conv_bwd",
                                                    comm=cm("mixer_b_conv_bwd"))
    sg["a_w_s"] = dws.reshape(1, A_HEADS, CHUNK, CHUNK)
    sg["a_b_s"] = dbs.reshape(1, A_HEADS, CHUNK)
    sg["b_conv_w"] = dbcw.reshape(1, B_CONV, D_B)
    plan.grad_ready("ab_w_in", 0, _mm_tn(h_m0, dz_ab, shards=N_CHIPS, tk=1024, tn=512, tt=tt, name="ab_in_dw",
                                         comm=cm("ab_in_dw")))
    dh = _mm_nt(dz_ab, wt("ab_w_in", 0), layer=0, tm=tm_r, tn=None, name="ab_in_dx", comm=cm("ab_in_dx"))
    dx, dnm0 = _rmsnorm_bwd(x, nm, dh, dx, layer=0, tm=tm_n, name="norm_mix_bwd0", comm=cm("norm_mix_bwd0"))

    sg["norm_mix"] = [dnm0, dnm1]
    sg["norm_ffn"] = [dnf0, dnf1]
    sg["f_conv_w"] = [dfc0, dfc1]
    return loss, dx, sg


BLOCK_BYTES = 3 * 1024 * 1024


BF16_SUBLANES = 16


def _row_tile(rows, row_bytes, step=SUBLANES):
    best = None
    for tr in range(step, rows + 1, step):
        if rows % tr == 0 and tr * row_bytes <= BLOCK_BYTES:
            best = tr
    if best is None:
        raise ValueError(f"no row tile for {rows}")
    return best


def _place_scalars():
    x, y, c = lax.axis_index("x"), lax.axis_index("y"), lax.axis_index("c")
    return jnp.stack([c, 2 * x + y, 2 * (1 - x) + y, 2 * x + (1 - y), 2 * (1 - x) + (1 - y)]).astype(jnp.int32)


def _cast_into_slot(w, place, *, layer, name):
    L, rows, cols = w.shape
    tr = _row_tile(rows, cols * 4, BF16_SUBLANES)

    def body(place_ref, w_ref, o_ref):
        o_ref[...] = w_ref[...].astype(BF16)

    return pl.pallas_call(
        body, name=name,
        grid_spec=pltpu.PrefetchScalarGridSpec(
            num_scalar_prefetch=1, grid=(rows // tr,),
            in_specs=[pl.BlockSpec((None, tr, cols), lambda i, p: (layer, i, 0))],
            out_specs=pl.BlockSpec((None, None, tr, cols), lambda i, p: (0, p[1], i, 0))),
        out_shape=jax.ShapeDtypeStruct((1, N_CHIPS, rows, cols), BF16),
        compiler_params=_cparams(("parallel",), 32),
    )(place, w)


def _pair_sum(g, theirs, place, *, name):
    S, rows, cols = g.shape
    half = rows // 2
    tr = _row_tile(half, cols * 4, BF16_SUBLANES)
    nb = half // tr

    def body(place_ref, g_ref, t_ref, o_ref):
        o_ref[...] = (g_ref[...] + t_ref[...]).astype(BF16)

    spec = pl.BlockSpec((None, tr, cols), lambda s, i, p: (s, i, 0))
    return pl.pallas_call(
        body, name=name,
        grid_spec=pltpu.PrefetchScalarGridSpec(
            num_scalar_prefetch=1, grid=(S, nb),
            in_specs=[pl.BlockSpec((None, tr, cols), lambda s, i, p: (s, p[0] * nb + i, 0)), spec],
            out_specs=spec),
        out_shape=jax.ShapeDtypeStruct((S, half, cols), BF16),
        compiler_params=_cparams(("parallel", "parallel"), 32),
    )(place, g, theirs)


def _chip_sum(p, r, g_prev, place, *, layer, shape, name):
    L, rows, cols = shape
    half = rows // 2
    tr = _row_tile(half, cols * 4, BF16_SUBLANES)
    nb = half // tr

    def body(place_ref, p_ref, r_ref, *rest):
        o_ref = rest[-1]
        mine = p_ref[...].astype(F32)
        peers = [r_ref[j].astype(F32) for j in range(3)]
        acc = None
        for s in range(N_CHIPS):
            term = jnp.where(place_ref[1] == s, mine,
                             jnp.where(place_ref[2] == s, peers[0],
                                       jnp.where(place_ref[3] == s, peers[1], peers[2])))
            acc = term if acc is None else acc + term
        o_ref[...] = acc

    in_specs = [pl.BlockSpec((None, tr, cols), lambda i, pr: (pr[1], i, 0)),
                pl.BlockSpec((3, tr, cols), lambda i, pr: (0, i, 0))]
    args = [place, p, r]
    aliases = {}
    if g_prev is not None:
        in_specs.append(ANY)
        args.append(g_prev)
        aliases = {3: 0}
    return pl.pallas_call(
        body, name=name,
        grid_spec=pltpu.PrefetchScalarGridSpec(
            num_scalar_prefetch=1, grid=(nb,), in_specs=in_specs,
            out_specs=pl.BlockSpec((None, tr, cols), lambda i, pr: (layer, pr[0] * nb + i, 0))),
        out_shape=jax.ShapeDtypeStruct(shape, F32), input_output_aliases=aliases,
        compiler_params=_cparams(("parallel",), 32),
    )(*args)


def _adamw_math(w, g, m, v):
    m2 = ADAM_B1 * m + (1.0 - ADAM_B1) * g
    v2 = ADAM_B2 * v + (1.0 - ADAM_B2) * (g * g)
    m_hat = m2 / (1.0 - ADAM_B1 ** ADAM_STEP)
    v_hat = v2 / (1.0 - ADAM_B2 ** ADAM_STEP)
    delta = -ADAM_LR * (m_hat / (jnp.sqrt(v_hat) + ADAM_EPS) + ADAM_WD * w)
    return delta, m2, v2


def _adamw(w, g, m, v, *, name):
    L, rows, cols = w.shape
    tr = _row_tile(rows, cols * 4)

    def body(w_ref, g_ref, m_ref, v_ref, d_ref, m2_ref, v2_ref):
        d, m2, v2 = _adamw_math(w_ref[...], g_ref[...], m_ref[...], v_ref[...])
        d_ref[...] = d
        m2_ref[...] = m2
        v2_ref[...] = v2

    spec = pl.BlockSpec((None, tr, cols), lambda l, i: (l, i, 0))
    shape = jax.ShapeDtypeStruct(w.shape, F32)
    return pl.pallas_call(
        body, name=name, grid=(L, rows // tr), in_specs=[spec] * 4, out_specs=[spec] * 3,
        out_shape=[shape] * 3,
        compiler_params=_cparams(("parallel", "parallel"), 48),
    )(w, g, m, v)


def _exchange_packs(pack, *, reduce, name):
    R = pack.shape[0]
    ndev = 2 * N_CHIPS

    def body(p_ref, o_ref, *scratch):
        if reduce:
            buf, send, recv = scratch
        else:
            buf = o_ref
            send, recv = scratch
        x, y, c = lax.axis_index("x"), lax.axis_index("y"), lax.axis_index("c")
        me = 4 * x + 2 * y + c
        buf[me] = p_ref[...]
        sends = []
        for q in range(1, ndev):
            qx, qy, qc = (q >> 2) & 1, (q >> 1) & 1, q & 1
            peer = (x ^ qx, y ^ qy, c ^ qc)
            rc = _remote(p_ref, buf.at[me], send.at[q - 1], recv.at[q - 1], peer)
            rc.start()
            sends.append(rc)
        for q in range(1, ndev):
            qx, qy, qc = (q >> 2) & 1, (q >> 1) & 1, q & 1
            slot = buf.at[4 * (x ^ qx) + 2 * (y ^ qy) + (c ^ qc)]
            _remote(slot, slot, send.at[q - 1], recv.at[q - 1], (x ^ qx, y ^ qy, c ^ qc)).wait_recv()
        for rc in sends:
            rc.wait_send()
        if reduce:
            acc = buf[0]
            for d in range(1, ndev):
                acc = acc + buf[d]
            o_ref[...] = acc

    vm = pl.BlockSpec(memory_space=pltpu.VMEM)
    sems = [pltpu.SemaphoreType.DMA((ndev - 1,)), pltpu.SemaphoreType.DMA((ndev - 1,))]
    if reduce:
        out_shape = jax.ShapeDtypeStruct((R, LANES), F32)
        scratch = [pltpu.VMEM((ndev, R, LANES), F32)] + sems
    else:
        out_shape = jax.ShapeDtypeStruct((ndev, R, LANES), F32)
        scratch = sems
    return pl.pallas_call(
        body, name=name, in_specs=[vm], out_specs=vm, out_shape=out_shape, scratch_shapes=scratch,
        compiler_params=pltpu.CompilerParams(vmem_limit_bytes=VMEM_BYTES_MAX),
    )(pack)


PACK_UNIT = SUBLANES * LANES


def _pack(arrays):
    flat, sizes = [], []
    for a in arrays:
        pieces = a if isinstance(a, (list, tuple)) else [a]
        v = jnp.concatenate([p.reshape(-1) for p in pieces]) if len(pieces) > 1 else pieces[0].reshape(-1)
        size = v.shape[0]
        padded = -(-size // PACK_UNIT) * PACK_UNIT
        flat.append(jnp.pad(v, (0, padded - size)))
        sizes.append((size, padded))
    return jnp.concatenate(flat).reshape(-1, LANES), sizes


def _unpack(pack, sizes, shapes):
    v = pack.reshape(-1)
    out, off = [], 0
    for (size, padded), shape in zip(sizes, shapes):
        out.append(v[off:off + size].reshape(shape))
        off += padded
    return out


BIG = ("ab_w_in", "ab_w_out", "c_w_in", "c_w_out", "f_w_up", "f_w_down")
COL_SHARDED = ("ab_w_in", "c_w_in", "f_w_up")
SMALL_REPLICATED = ("norm_mix", "norm_ffn", "norm_final", "a_ln_g", "a_ln_b", "a_w_s", "a_b_s",
                    "b_conv_b", "b_ln_g", "b_ln_b")
SMALL_SHARDED = ("b_conv_w", "c_conv_w", "f_conv_w")
SMALL = SMALL_REPLICATED + SMALL_SHARDED
ALL_WEIGHTS = ("norm_mix", "norm_ffn", "norm_final", "ab_w_in", "a_ln_g", "a_ln_b", "a_w_s", "a_b_s",
               "b_conv_w", "b_conv_b", "b_ln_g", "b_ln_b", "ab_w_out", "c_w_in", "c_conv_w", "c_w_out",
               "f_w_up", "f_conv_w", "f_w_down")


SCHEDULE = {
    "ab_in": [("gi", "f_w_up", 0, 0, 4), ("gi", "ab_w_out", 0)],
    "mixer_ab": [("gd", "f_w_up", 0, 0, 4), ("gd", "ab_w_out", 0), ("gi", "f_w_up", 0, 1, 4),
                 ("gi", "f_w_up", 0, 2, 4), ("gi", "f_w_up", 0, 3, 4)],
    "ab_out": [("gd", "f_w_up", 0, 1, 4), ("gd", "f_w_up", 0, 2, 4), ("gd", "f_w_up", 0, 3, 4)],
    "ffn_up0": [("gi", "f_w_down", 0), ("gi", "c_w_in", 0, 0, 2)],
    "ffn_act0": [("gd", "f_w_down", 0), ("gd", "c_w_in", 0, 0, 2), ("gi", "c_w_in", 0, 1, 2),
                 ("gi", "f_w_up", 1, 0, 4), ("gi", "f_w_up", 1, 1, 4)],
    "ffn_down0": [("gd", "c_w_in", 0, 1, 2), ("gd", "f_w_up", 1, 0, 4), ("gd", "f_w_up", 1, 1, 4),
                  ("gi", "f_w_up", 1, 2, 4)],
    "c_in": [("gd", "f_w_up", 1, 2, 4), ("gi", "f_w_up", 1, 3, 4), ("gi", "c_w_out", 0)],
    "mixer_c": [("gd", "f_w_up", 1, 3, 4), ("gd", "c_w_out", 0), ("gi", "f_w_down", 1, 0, 2)],
    "c_out": [("gd", "f_w_down", 1, 0, 2), ("gi", "f_w_down", 1, 1, 2)],
    "ffn_up1": [("gd", "f_w_down", 1, 1, 2)],
    "ffn_act_bwd1": [("px", "f_w_down", 1)],
    "ffn_up_dx1": [("cx", "f_w_down", 1)],
    "mixer_c_bwd": [("px", "f_w_up", 1), ("px", "c_w_out", 0)],
    "c_in_dw": [("cx", "f_w_up", 1, 0, 2), ("cx", "c_w_out", 0)],
    "c_in_dx": [("cx", "f_w_up", 1, 1, 2), ("px", "c_w_in", 0)],
    "ffn_down_dx0": [("cx", "c_w_in", 0, 0, 2)],
    "ffn_down_dw0": [("cx", "c_w_in", 0, 1, 2)],
    "ffn_act_bwd0": [("px", "f_w_down", 0)],
    "ffn_up_dx0": [("cx", "f_w_down", 0)],
    "mixer_ab_bwd": [("px", "f_w_up", 0), ("px", "ab_w_out", 0)],
    "mixer_b_conv_bwd": [("cx", "f_w_up", 0, 0, 2), ("cx", "ab_w_out", 0)],
    "ab_in_dw": [("cx", "f_w_up", 0, 1, 2)],
    "ab_in_dx": [("px", "ab_w_in", 0)],
    "norm_mix_bwd0": [("cx", "ab_w_in", 0)],
}


class _Plan:
    def __init__(self, shapes, place):
        self.shapes, self.place, self.bufs = shapes, place, {}

    def weight(self, name, layer):
        g = self.bufs[f"w:{name}:{layer}"]
        if name in COL_SHARDED:
            return g
        _, S, rows, cols = g.shape
        return g.reshape(1, S * rows, cols)

    def grad_ready(self, name, layer, g):
        _, rows, cols = self.shapes[name]
        hbm = lambda a: pltpu.with_memory_space_constraint(a, pltpu.HBM)
        self.bufs[f"g:{name}:{layer}"] = g.reshape(N_CHIPS, rows, cols)
        self.bufs[f"t:{name}:{layer}"] = hbm(lax.empty((N_CHIPS, rows // 2, cols), F32))
        self.bufs[f"l:{name}:{layer}"] = hbm(lax.empty((3, rows // 2, cols), BF16))

    def job(self, kind, name, layer, part=0, parts=1):
        _, rows, cols = self.shapes[name]
        key = f"{name}:{layer}"
        if kind == "gi":
            return _job_gather_ici("w:" + key, rows, part, parts)
        if kind == "gd":
            return _job_gather_d2d("w:" + key, rows, part, parts)
        if kind == "px":
            return _job_pair_exchange("g:" + key, "t:" + key, rows)
        if kind == "cx":
            if "p:" + key not in self.bufs:
                self.bufs["p:" + key] = _pair_sum(self.bufs["g:" + key], self.bufs["t:" + key], self.place,
                                                  name=f"pair_sum_{name}{layer}")
            nr = rows // 2 // parts
            return _job_chip_exchange("p:" + key, "l:" + key, part * nr, nr)
        if kind == "ps":
            return _job_pair_share("G:" + name, layer, rows)
        raise ValueError(kind)

    def comm(self, call):
        specs = SCHEDULE.get(call)
        return None if specs is None else _Comm(self, [self.job(*spec) for spec in specs])


def _step(x, tgt, w, m, v):
    chip = 2 * lax.axis_index("x") + lax.axis_index("y")
    place = _place_scalars()
    plan = _Plan({n: w[n].shape for n in BIG}, place)
    items = [(n, l) for n in BIG for l in range(w[n].shape[0])]

    for n, l in items:
        plan.bufs[f"w:{n}:{l}"] = _cast_into_slot(w[n], place, layer=l, name=f"cast_{n}{l}")
    _comm_only(plan, [[plan.job("gi", "ab_w_in", 0)], [plan.job("gd", "ab_w_in", 0)]], name="gather_first")
    conv_pack, conv_sizes = _pack([w[n] for n in SMALL_SHARDED])
    conv_all = _exchange_packs(conv_pack, reduce=False, name="gather_conv_weights")
    conv_shapes = [w[n].shape for n in SMALL_SHARDED]
    per_chip = [_unpack(conv_all[2 * s], conv_sizes, conv_shapes) for s in range(N_CHIPS)]
    small = {n: w[n] for n in SMALL_REPLICATED}
    for idx, n in enumerate(SMALL_SHARDED):
        small[n] = jnp.concatenate([per_chip[s][idx] for s in range(N_CHIPS)], axis=-1)

    loss, dx, sg = _local_step(x, tgt, small, plan)

    for n, l in items:
        plan.bufs["G:" + n] = _chip_sum(plan.bufs[f"p:{n}:{l}"], plan.bufs[f"l:{n}:{l}"], plan.bufs.get("G:" + n),
                                        place, layer=l, shape=w[n].shape, name=f"chip_sum_{n}{l}")
    _comm_only(plan, [[plan.job("ps", n, l) for n, l in items]], name="reduce_pair_share")
    grads_big = [plan.bufs["G:" + n] for n in BIG]

    g_pack, g_sizes = _pack([sg[n] for n in SMALL])
    g_sum = _exchange_packs(g_pack, reduce=True, name="allreduce_small_grads")
    full_shapes = [small[n].shape for n in SMALL]
    g_small = dict(zip(SMALL, _unpack(g_sum, g_sizes, full_shapes)))
    for n in SMALL_SHARDED:
        width = w[n].shape[-1]
        g_small[n] = lax.dynamic_slice_in_dim(g_small[n], chip * width, width, axis=g_small[n].ndim - 1)

    grad, delta, new_m, new_v = {}, {}, {}, {}
    for n, g in zip(BIG, grads_big):
        grad[n] = g
        delta[n], new_m[n], new_v[n] = _adamw(w[n], g, m[n], v[n], name=f"adamw_{n}")
    shapes = [w[n].shape for n in SMALL]
    wp, sizes = _pack([w[n] for n in SMALL])
    gp, _ = _pack([g_small[n] for n in SMALL])
    mp, _ = _pack([m[n] for n in SMALL])
    vp, _ = _pack([v[n] for n in SMALL])
    R = wp.shape[0]
    dp, m2p, v2p = _adamw(wp.reshape(1, R, LANES), gp.reshape(1, R, LANES), mp.reshape(1, R, LANES),
                          vp.reshape(1, R, LANES), name="adamw_small")
    for n, d_, m_, v_ in zip(SMALL, _unpack(dp, sizes, shapes), _unpack(m2p, sizes, shapes),
                             _unpack(v2p, sizes, shapes)):
        grad[n] = g_small[n]
        delta[n], new_m[n], new_v[n] = d_, m_, v_
    return loss, dx, grad, delta, new_m, new_v


def kernel(x, norm_mix, norm_ffn, norm_final, ab_w_in, a_ln_g, a_ln_b, a_w_s, a_b_s, b_conv_w, b_conv_b, b_ln_g, b_ln_b, ab_w_out, c_w_in, c_conv_w, c_w_out, f_w_up, f_conv_w, f_w_down, loss_target, m_norm_mix, m_norm_ffn, m_norm_final, m_ab_w_in, m_a_ln_g, m_a_ln_b, m_a_w_s, m_a_b_s, m_b_conv_w, m_b_conv_b, m_b_ln_g, m_b_ln_b, m_ab_w_out, m_c_w_in, m_c_conv_w, m_c_w_out, m_f_w_up, m_f_conv_w, m_f_w_down, v_norm_mix, v_norm_ffn, v_norm_final, v_ab_w_in, v_a_ln_g, v_a_ln_b, v_a_w_s, v_a_b_s, v_b_conv_w, v_b_conv_b, v_b_ln_g, v_b_ln_b, v_ab_w_out, v_c_w_in, v_c_conv_w, v_c_w_out, v_f_w_up, v_f_conv_w, v_f_w_down):
    given = dict(locals())
    w = {n: given[n] for n in ALL_WEIGHTS}
    m = {n: given["m_" + n] for n in ALL_WEIGHTS}
    v = {n: given["v_" + n] for n in ALL_WEIGHTS}
    T = x.shape[1]
    loss, dx, grad, delta, new_m, new_v = _step(x.reshape(T, D_MODEL), loss_target.reshape(T, D_MODEL), w, m, v)
    loss = lax.psum(loss[0, 0], ("x", "y", "c"))
    out = [loss, dx.reshape(x.shape)]
    for d in (grad, delta, new_m, new_v):
        out += [d[n] for n in ALL_WEIGHTS]
    return tuple(out)
```

```python
import functools
import math

import jax
import jax.numpy as jnp
from jax import lax
from jax.experimental import pallas as pl
from jax.experimental.pallas import tpu as pltpu

F32 = jnp.float32
BF16 = jnp.bfloat16

EPS = 1e-6
D_MODEL = 1024
CHUNK = 128
HEAD_DIM = 128
A_HEADS = 4
D_A = 512
D_B = 512
B_CONV = 31
C_CONV = 3
D_FF = 2816
F_CONV = 3
N_CHIPS = 4

ADAM_LR = 0.001
ADAM_B1 = 0.9
ADAM_B2 = 0.999
ADAM_EPS = 1e-08
ADAM_WD = 0.01
ADAM_STEP = 10

SUBLANES = 8
LANES = 128
HALO_SHORT = 16
HALO_LONG = 32
VMEM_BYTES_MAX = 60000 * 1024

INV_SQRT2 = 1.0 / math.sqrt(2.0)
INV_SQRT_2PI = 1.0 / math.sqrt(2.0 * math.pi)

MESH = pl.DeviceIdType.MESH


def _cparams(sem, vmem_mb):
    del vmem_mb
    return pltpu.CompilerParams(dimension_semantics=sem, vmem_limit_bytes=VMEM_BYTES_MAX)


def _pick(total, pref):
    for c in (2048, 1024, 512, 256, 128):
        if c <= pref and total % c == 0:
            return c
    raise ValueError(f"no tile for {total}")


def _sigmoid(x):
    return jax.nn.sigmoid(x)


def _silu(x):
    return x * _sigmoid(x)


def _dsilu(x):
    s = _sigmoid(x)
    return s * (1.0 + x * (1.0 - s))


def _gelu(x):
    return 0.5 * x * (1.0 + lax.erf(x * INV_SQRT2))


def _dgelu(x):
    return 0.5 * (1.0 + lax.erf(x * INV_SQRT2)) + x * jnp.exp(-0.5 * x * x) * INV_SQRT_2PI


def _ln_stats(x):
    mu = jnp.mean(x, axis=-1, keepdims=True)
    xc = x - mu
    var = jnp.mean(xc * xc, axis=-1, keepdims=True)
    r = lax.rsqrt(var + EPS)
    return xc * r, r


def _ln_bwd(dy, xh, r, g):
    dxh = dy * g
    m1 = jnp.mean(dxh, axis=-1, keepdims=True)
    m2 = jnp.mean(dxh * xh, axis=-1, keepdims=True)
    return r * (dxh - m1 - xh * m2)


def _rowsum(x):
    return jnp.sum(x, axis=0, keepdims=True)


ANY = pl.BlockSpec(memory_space=pltpu.HBM)


def _place():
    x, y, c = lax.axis_index("x"), lax.axis_index("y"), lax.axis_index("c")
    peers = [(1 - x, y), (x, 1 - y), (1 - x, 1 - y)]
    return x, y, c, 2 * x + y, (x, y, 1 - c), peers


def _half(rows, which):
    return pl.ds(which * (rows // 2), rows // 2)


def _remote(src, dst, send_sem, recv_sem, device):
    return pltpu.make_async_remote_copy(src_ref=src, dst_ref=dst, send_sem=send_sem, recv_sem=recv_sem,
                                        device_id=device, device_id_type=MESH)


class _Job:
    def __init__(self, reads, writes, ncopies, copies):
        self.reads, self.writes, self.ncopies, self.copies = reads, writes, ncopies, copies


def _share(rows, which, part, parts):
    nr = rows // 2 // parts
    return pl.ds(which * (rows // 2) + part * nr, nr)


def _job_gather_ici(name, rows, part, parts):
    def copies(src, dst, sem):
        x, y, c, k, sib, peers = _place()
        mine_rows = _share(rows, c, part, parts)
        out = []
        for j, (px, py) in enumerate(peers):
            mine = src[name].at[0, k, mine_rows]
            out.append((_remote(mine, dst[name].at[0, k, mine_rows], sem(j, 0), sem(j, 1), (px, py, c)),
                        _remote(mine, dst[name].at[0, 2 * px + py, mine_rows], sem(j, 0), sem(j, 1), (px, py, c))))
        return out
    return _Job([], [name], 3, copies)


def _job_gather_d2d(name, rows, part, parts):
    def copies(src, dst, sem):
        x, y, c, k, sib, peers = _place()
        out = []
        for j, (px, py) in enumerate(peers):
            landed = src[name].at[0, 2 * px + py, _share(rows, c, part, parts)]
            out.append((_remote(landed, dst[name].at[0, 2 * px + py, _share(rows, c, part, parts)],
                                sem(j, 0), sem(j, 1), sib),
                        _remote(landed, dst[name].at[0, 2 * px + py, _share(rows, 1 - c, part, parts)],
                                sem(j, 0), sem(j, 1), sib)))
        return out
    return _Job([], [name], 3, copies)


def _job_pair_exchange(gname, tname, rows):
    def copies(src, dst, sem):
        x, y, c, k, sib, peers = _place()
        cp = _remote(src[gname].at[:, _half(rows, 1 - c), :], dst[tname], sem(0, 0), sem(0, 1), sib)
        return [(cp, cp)]
    return _Job([gname], [tname], 1, copies)


def _job_chip_exchange(pname, lname, r0, nr):
    def copies(src, dst, sem):
        x, y, c, k, sib, peers = _place()
        out = []
        for j, (px, py) in enumerate(peers):
            cp = _remote(src[pname].at[2 * px + py, pl.ds(r0, nr)], dst[lname].at[j, pl.ds(r0, nr)],
                         sem(j, 0), sem(j, 1), (px, py, c))
            out.append((cp, cp))
        return out
    return _Job([pname], [lname], 3, copies)


def _job_pair_share(name, layer, rows):
    def copies(src, dst, sem):
        x, y, c, k, sib, peers = _place()
        mine = src[name].at[layer, _half(rows, c)]
        return [(_remote(mine, dst[name].at[layer, _half(rows, c)], sem(0, 0), sem(0, 1), sib),
                 _remote(mine, dst[name].at[layer, _half(rows, 1 - c)], sem(0, 0), sem(0, 1), sib))]
    return _Job([], [name], 1, copies)


class _Comm:
    def __init__(self, plan, jobs):
        self.plan, self.jobs = plan, jobs
        self.writes, self.reads = [], []
        for job in jobs:
            for n in job.writes:
                if n not in self.writes:
                    self.writes.append(n)
        for job in jobs:
            for n in job.reads:
                if n not in self.writes and n not in self.reads:
                    self.reads.append(n)
        self.ncopies = sum(job.ncopies for job in jobs)

    def descriptors(self, src, dst, sems, base):
        out = []
        for job in self.jobs:
            sem = lambda j, which, base=base: sems.at[base + j, which]
            out += job.copies(src, dst, sem)
            base += job.ncopies
        return out

    def start(self, src, dst, sems, base=0):
        for first, _ in self.descriptors(src, dst, sems, base):
            first.start()

    def finish(self, src, dst, sems, base=0):
        for _, landed in self.descriptors(src, dst, sems, base):
            landed.wait()


def _comm_operands(comm):
    bufs = comm.plan.bufs
    shapes = [jax.ShapeDtypeStruct(bufs[n].shape, bufs[n].dtype) for n in comm.writes]
    return [bufs[n] for n in comm.reads] + [bufs[n] for n in comm.writes], shapes


def _pallas(comm, body, *, name, grid, in_specs, out_specs, out_shape, compiler_params, scratch_shapes=(),
            aliases=None):
    aliases = dict(aliases or {})
    if comm is None:
        return pl.pallas_call(body, name=name, grid=grid, in_specs=in_specs, out_specs=out_specs,
                              out_shape=out_shape, scratch_shapes=list(scratch_shapes),
                              input_output_aliases=aliases, compiler_params=compiler_params)
    single = not isinstance(out_shape, (list, tuple))
    base_specs = [out_specs] if single else list(out_specs)
    base_shape = [out_shape] if single else list(out_shape)
    nb, nr, nw, nbo, nsc = len(in_specs), len(comm.reads), len(comm.writes), len(base_specs), len(scratch_shapes)

    def wrapped(*refs):
        base_in, rd, wr_in = refs[:nb], refs[nb:nb + nr], refs[nb + nr:nb + nr + nw]
        o0 = nb + nr + nw
        base_out, wr_out = refs[o0:o0 + nbo], refs[o0 + nbo:o0 + nbo + nw]
        scratch, sems = refs[o0 + nbo + nw:o0 + nbo + nw + nsc], refs[-1]
        src = dict(zip(comm.reads, rd))
        src.update(zip(comm.writes, wr_in))
        dst = dict(zip(comm.writes, wr_out))
        first = functools.reduce(jnp.logical_and, [pl.program_id(a) == 0 for a in range(len(grid))])
        last = functools.reduce(jnp.logical_and,
                                [pl.program_id(a) == pl.num_programs(a) - 1 for a in range(len(grid))])

        @pl.when(first)
        def _():
            comm.start(src, dst, sems)
        body(*base_in, *base_out, *scratch)

        @pl.when(last)
        def _():
            comm.finish(src, dst, sems)

    operands, shapes = _comm_operands(comm)
    call = pl.pallas_call(
        wrapped, name=name, grid=grid, in_specs=list(in_specs) + [ANY] * (nr + nw),
        out_specs=base_specs + [ANY] * nw, out_shape=base_shape + shapes,
        input_output_aliases={**aliases, **{nb + nr + q: nbo + q for q in range(nw)}},
        scratch_shapes=list(scratch_shapes) + [pltpu.SemaphoreType.DMA((comm.ncopies, 2))],
        compiler_params=compiler_params)

    def run(*args):
        outs = call(*args, *operands)
        for q, n in enumerate(comm.writes):
            comm.plan.bufs[n] = outs[nbo + q]
        return outs[0] if single else list(outs[:nbo])

    return run


def _comm_only(plan, phases, *, name):
    comms = [_Comm(plan, jobs) for jobs in phases]
    both = _Comm(plan, [job for jobs in phases for job in jobs])
    nr, nw = len(both.reads), len(both.writes)

    def body(*refs):
        rd, wr_in, wr_out, sems = refs[:nr], refs[nr:nr + nw], refs[nr + nw:nr + 2 * nw], refs[-1]
        src = dict(zip(both.reads, rd))
        src.update(zip(both.writes, wr_in))
        dst = dict(zip(both.writes, wr_out))
        base = 0
        for comm in comms:
            comm.start(src, dst, sems, base)
            comm.finish(src, dst, sems, base)
            base += comm.ncopies

    operands, shapes = _comm_operands(both)
    outs = pl.pallas_call(
        body, name=name, in_specs=[ANY] * (nr + nw), out_specs=[ANY] * nw, out_shape=shapes,
        input_output_aliases={nr + q: q for q in range(nw)},
        scratch_shapes=[pltpu.SemaphoreType.DMA((both.ncopies, 2))],
    )(*operands)
    for q, n in enumerate(both.writes):
        plan.bufs[n] = outs[q]


def _mm_nn(a, w, *, layer, tm, tn, residual=None, norm=None, out_dtype=F32, name, comm=None):
    T, K = a.shape
    if w.ndim == 4:
        _, S, _, n4 = w.shape
        N = S * n4
        bps = n4 // tn
        w_spec = pl.BlockSpec((None, None, K, tn), lambda j, i: (layer, j // bps, 0, j % bps))
    else:
        N = w.shape[2]
        w_spec = pl.BlockSpec((None, K, tn), lambda j, i: (layer, 0, j))
    in_specs = [pl.BlockSpec((tm, K), lambda j, i: (i, 0)), w_spec]
    args = [a, w]
    if residual is not None:
        in_specs.append(pl.BlockSpec((tm, tn), lambda j, i: (i, j)))
        args.append(residual)
    out_specs = pl.BlockSpec((tm, tn), lambda j, i: (i, j))
    out_shape = jax.ShapeDtypeStruct((T, N), out_dtype)
    if norm is not None:
        assert tn == N
        g, norm_layer = norm
        in_specs.append(pl.BlockSpec((None, 1, N), lambda j, i: (norm_layer, 0, 0)))
        args.append(g)
        out_specs = [out_specs, pl.BlockSpec((tm, tn), lambda j, i: (i, j))]
        out_shape = [out_shape, jax.ShapeDtypeStruct((T, N), BF16)]

    def body(*refs):
        a_ref, w_ref = refs[0], refs[1]
        acc = jnp.dot(a_ref[...].astype(BF16), w_ref[...], preferred_element_type=F32)
        if residual is not None:
            acc = refs[2][...] + acc
        if norm is None:
            refs[-1][...] = acc.astype(out_dtype)
        else:
            refs[-2][...] = acc.astype(out_dtype)
            r = lax.rsqrt(jnp.mean(acc * acc, axis=-1, keepdims=True) + EPS)
            refs[-1][...] = (acc * r * refs[-3][...]).astype(BF16)

    return _pallas(
        comm, body, name=name, grid=(N // tn, T // tm), in_specs=in_specs,
        out_specs=out_specs, out_shape=out_shape,
        compiler_params=_cparams(("parallel", "parallel"), 48),
    )(*args)


def _mm_nt(dy, w, *, layer, tm, tn, name, out_dtype=F32, comm=None):
    T = dy.shape[0]
    nt_dims = (((1,), (1,)), ((), ()))
    _, R, N = w.shape

    def body2(dy_ref, w_ref, o_ref):
        o_ref[...] = lax.dot_general(dy_ref[...].astype(BF16), w_ref[...], nt_dims,
                                     preferred_element_type=F32).astype(out_dtype)

    return _pallas(
        comm, body2, name=name, grid=(R // tn, T // tm),
        in_specs=[pl.BlockSpec((tm, N), lambda j, i: (i, 0)),
                  pl.BlockSpec((None, tn, N), lambda j, i: (layer, j, 0))],
        out_specs=pl.BlockSpec((tm, tn), lambda j, i: (i, j)),
        out_shape=jax.ShapeDtypeStruct((T, R), out_dtype),
        compiler_params=_cparams(("parallel", "parallel"), 48),
    )(dy, w)


def _mm_tn(a, dy, *, shards, tk, tn, tt, name, comm=None):
    T, K = a.shape
    N = dy.shape[1]
    tn_dims = (((0,), (0,)), ((), ()))

    def body(a_ref, dy_ref, o_ref):
        @pl.when(pl.program_id(2) == 0)
        def _():
            o_ref[...] = jnp.zeros_like(o_ref)
        o_ref[...] += lax.dot_general(a_ref[...].astype(BF16), dy_ref[...].astype(BF16), tn_dims,
                                      preferred_element_type=F32)

    if shards is None:
        out_spec = pl.BlockSpec((tk, tn), lambda k, n, t: (k, n))
        out_shape = jax.ShapeDtypeStruct((K, N), F32)
    else:
        n4 = N // shards
        bps = n4 // tn
        out_spec = pl.BlockSpec((None, tk, tn), lambda k, n, t: (n // bps, k, n % bps))
        out_shape = jax.ShapeDtypeStruct((shards, K, n4), F32)
    return _pallas(
        comm, body, name=name, grid=(K // tk, N // tn, T // tt),
        in_specs=[pl.BlockSpec((tt, tk), lambda k, n, t: (t, k)),
                  pl.BlockSpec((tt, tn), lambda k, n, t: (t, n))],
        out_specs=out_spec, out_shape=out_shape,
        compiler_params=_cparams(("parallel", "parallel", "arbitrary"), 48),
    )(a, dy)


def _rmsnorm_fwd(x, g, *, layer, tm, name, comm=None):
    T, D = x.shape

    def body(x_ref, g_ref, h_ref):
        xf = x_ref[...]
        r = lax.rsqrt(jnp.mean(xf * xf, axis=-1, keepdims=True) + EPS)
        h_ref[...] = (xf * r * g_ref[...]).astype(BF16)

    return _pallas(
        comm, body, name=name, grid=(T // tm,),
        in_specs=[pl.BlockSpec((tm, D), lambda i: (i, 0)),
                  pl.BlockSpec((None, 1, D), lambda i: (layer, 0, 0))],
        out_specs=pl.BlockSpec((tm, D), lambda i: (i, 0)),
        out_shape=jax.ShapeDtypeStruct((T, D), BF16),
        compiler_params=_cparams(("parallel",), 32),
    )(x, g)


def _rmsnorm_bwd_math(xf, g, dh, dres):
    r = lax.rsqrt(jnp.mean(xf * xf, axis=-1, keepdims=True) + EPS)
    xh = xf * r
    dxh = dh * g
    dx = dres + r * (dxh - xh * jnp.mean(dxh * xh, axis=-1, keepdims=True))
    return dx, _rowsum(dh * xh)


def _mm_nt_norm(dy, w, x, g, dres, *, g_layer, tm, name, comm=None):
    T = dy.shape[0]
    _, S, K, n4 = w.shape
    nt_dims = (((1,), (1,)), ((), ()))

    def body(dy_ref, w_ref, x_ref, g_ref, dres_ref, dx_ref, dg_ref):
        @pl.when(pl.program_id(0) == 0)
        def _():
            dg_ref[...] = jnp.zeros_like(dg_ref)
        dh = None
        for s in range(S):
            part = lax.dot_general(dy_ref[:, s * n4:(s + 1) * n4].astype(BF16), w_ref[s], nt_dims,
                                   preferred_element_type=F32)
            dh = part if dh is None else dh + part
        dx, dg = _rmsnorm_bwd_math(x_ref[...], g_ref[...], dh, dres_ref[...])
        dx_ref[...] = dx
        dg_ref[...] += dg

    row = lambda i: (i, 0)
    return _pallas(
        comm, body, name=name, grid=(T // tm,),
        in_specs=[pl.BlockSpec((tm, S * n4), row),
                  pl.BlockSpec((None, S, K, n4), lambda i: (0, 0, 0, 0)),
                  pl.BlockSpec((tm, K), row),
                  pl.BlockSpec((None, 1, K), lambda i: (g_layer, 0, 0)),
                  pl.BlockSpec((tm, K), row)],
        out_specs=[pl.BlockSpec((tm, K), row), pl.BlockSpec((1, K), lambda i: (0, 0))],
        out_shape=[jax.ShapeDtypeStruct((T, K), F32), jax.ShapeDtypeStruct((1, K), F32)],
        compiler_params=_cparams(("arbitrary",), 56),
    )(dy, w, x, g, dres)


def _loss_head(x, tgt, g, *, tm, name, comm=None):
    T, D = x.shape

    def body(x_ref, t_ref, g_ref, loss_ref, dx_ref, dg_ref):
        @pl.when(pl.program_id(0) == 0)
        def _():
            dg_ref[...] = jnp.zeros_like(dg_ref)
            loss_ref[...] = jnp.zeros_like(loss_ref)
        xf = x_ref[...]
        gg = g_ref[...]
        r = lax.rsqrt(jnp.mean(xf * xf, axis=-1, keepdims=True) + EPS)
        xh = xf * r
        err = xh * gg - t_ref[...]
        row = jnp.mean(err * err, axis=-1, keepdims=True)
        loss_ref[...] += 0.5 * jnp.sum(row, axis=0, keepdims=True)
        dy = err * (1.0 / D)
        dg_ref[...] += _rowsum(dy * xh)
        dxh = dy * gg
        dx_ref[...] = r * (dxh - xh * jnp.mean(dxh * xh, axis=-1, keepdims=True))

    return _pallas(
        comm, body, name=name, grid=(T // tm,),
        in_specs=[pl.BlockSpec((tm, D), lambda i: (i, 0)),
                  pl.BlockSpec((tm, D), lambda i: (i, 0)),
                  pl.BlockSpec((1, D), lambda i: (0, 0))],
        out_specs=[pl.BlockSpec((1, 1), lambda i: (0, 0)),
                   pl.BlockSpec((tm, D), lambda i: (i, 0)),
                   pl.BlockSpec((1, D), lambda i: (0, 0))],
        out_shape=[jax.ShapeDtypeStruct((1, 1), F32), jax.ShapeDtypeStruct((T, D), F32),
                   jax.ShapeDtypeStruct((1, D), F32)],
        compiler_params=_cparams(("arbitrary",), 40),
    )(x, tgt, g)


CONV_ROWS = 64
CONV_COLS = 256


def _halo_prev_index(tm, halo):
    per = tm // halo
    return lambda i: jnp.maximum(i * per - 1, 0)


def _halo_next_index(tm, halo, total):
    per = tm // halo
    last = total // halo - 1
    return lambda i: jnp.minimum((i + 1) * per, last)


def _causal_mask():
    t = lax.broadcasted_iota(jnp.int32, (CHUNK, CHUNK), 0)
    s = lax.broadcasted_iota(jnp.int32, (CHUNK, CHUNK), 1)
    return s <= t


def _mixer_ab_fwd(z, a_ln_g, a_ln_b, w_s, b_s, conv_w, conv_b, b_ln_g, b_ln_b, *, tm, name, comm=None):
    T = z.shape[0]
    nchunk = tm // CHUNK
    halo = HALO_LONG

    def body(za_ref, zb_ref, zh_ref, alg_ref, alb_ref, ws_ref, bs_ref, cw_ref, cbias_ref,
             blg_ref, blb_ref, y_ref, cb_ref, ext_ref):
        i = pl.program_id(0)
        gu = _gelu(za_ref[:, :D_A].astype(F32))
        gv = _gelu(za_ref[:, D_A:].astype(F32))
        xh, _ = _ln_stats(gv)
        lv = (xh * alg_ref[...] + alb_ref[...]).astype(BF16)
        mask = _causal_mask()
        for h in range(A_HEADS):
            wm = jnp.where(mask, ws_ref[h], 0.0).astype(BF16)
            cols = slice(h * HEAD_DIM, (h + 1) * HEAD_DIM)
            for c in range(nchunk):
                rows = slice(c * CHUNK, (c + 1) * CHUNK)
                mixed = jnp.dot(wm, lv[rows, cols], preferred_element_type=F32) + bs_ref[h]
                y_ref[rows, cols] = (gu[rows, cols] * mixed).astype(BF16)
        ext_ref[halo:halo + tm, :] = zb_ref[:, :D_B].astype(F32) * _sigmoid(zb_ref[:, D_B:].astype(F32))
        prev = zh_ref[:, :D_B].astype(F32) * _sigmoid(zh_ref[:, D_B:].astype(F32))
        ext_ref[0:halo, :] = jnp.where(i > 0, prev, 0.0)
        for rb in range(tm // CONV_ROWS):
            for cb in range(D_B // CONV_COLS):
                cs = slice(cb * CONV_COLS, (cb + 1) * CONV_COLS)
                window = ext_ref[rb * CONV_ROWS:rb * CONV_ROWS + CONV_ROWS + halo, cs]
                acc = jnp.zeros((CONV_ROWS, CONV_COLS), F32)
                for k in range(B_CONV):
                    shifted = _rows_after(window, halo - (B_CONV - 1) + k)[:CONV_ROWS]
                    acc = acc + cw_ref[k:k + 1, cs] * shifted
                cb_ref[rb * CONV_ROWS:(rb + 1) * CONV_ROWS, cs] = acc + cbias_ref[:, cs]
        xhb, _ = _ln_stats(cb_ref[...])
        y_ref[:, D_A:] = _silu(xhb * blg_ref[...] + blb_ref[...]).astype(BF16)

    row = lambda i: (i, 0)
    par = lambda i: (0, 0)
    return _pallas(
        comm, body, name=name, grid=(T // tm,),
        in_specs=[pl.BlockSpec((tm, 2 * D_A), lambda i: (i, 0)),
                  pl.BlockSpec((tm, 2 * D_B), lambda i: (i, 1)),
                  pl.BlockSpec((halo, 2 * D_B), lambda i: (_halo_prev_index(tm, halo)(i), 1)),
                  pl.BlockSpec((1, D_A), par), pl.BlockSpec((1, D_A), par),
                  pl.BlockSpec((A_HEADS, CHUNK, CHUNK), lambda i: (0, 0, 0)),
                  pl.BlockSpec((A_HEADS, CHUNK, 1), lambda i: (0, 0, 0)),
                  pl.BlockSpec((B_CONV, D_B), par), pl.BlockSpec((1, D_B), par),
                  pl.BlockSpec((1, D_B), par), pl.BlockSpec((1, D_B), par)],
        out_specs=[pl.BlockSpec((tm, D_A + D_B), row), pl.BlockSpec((tm, D_B), row)],
        out_shape=[jax.ShapeDtypeStruct((T, D_A + D_B), BF16), jax.ShapeDtypeStruct((T, D_B), F32)],
        scratch_shapes=[pltpu.VMEM((halo + tm, D_B), F32)],
        compiler_params=_cparams(("parallel",), 40),
    )(z, z, z, a_ln_g, a_ln_b, w_s, b_s, conv_w, conv_b, b_ln_g, b_ln_b)


def _mixer_ab_bwd_pre(z, cb, dy, a_ln_g, a_ln_b, w_s, b_s, b_ln_g, b_ln_b, *, tm, name, comm=None):
    T = z.shape[0]
    nchunk = tm // CHUNK
    tn_dims = (((0,), (0,)), ((), ()))
    nt_dims = (((1,), (1,)), ((), ()))

    def body(za_ref, cb_ref, dy_ref, alg_ref, alb_ref, ws_ref, bs_ref, blg_ref, blb_ref,
             dza_ref, dcb_ref, dalg_ref, dalb_ref, dws_ref, dbs_ref, dblg_ref, dblb_ref,
             dlv_ref):
        @pl.when(pl.program_id(0) == 0)
        def _():
            for ref in (dalg_ref, dalb_ref, dws_ref, dbs_ref, dblg_ref, dblb_ref):
                ref[...] = jnp.zeros_like(ref)
        ua = za_ref[:, :D_A].astype(F32)
        va = za_ref[:, D_A:].astype(F32)
        gu = _gelu(ua)
        gv = _gelu(va)
        xh, r = _ln_stats(gv)
        alg = alg_ref[...]
        lv = (xh * alg + alb_ref[...]).astype(BF16)
        dya = dy_ref[:, :D_A].astype(F32)
        mask = _causal_mask()
        for h in range(A_HEADS):
            wm = jnp.where(mask, ws_ref[h], 0.0).astype(BF16)
            cols = slice(h * HEAD_DIM, (h + 1) * HEAD_DIM)
            dwm = jnp.zeros((CHUNK, CHUNK), F32)
            dbs = jnp.zeros((CHUNK, 1), F32)
            for c in range(nchunk):
                rows = slice(c * CHUNK, (c + 1) * CHUNK)
                lvb = lv[rows, cols]
                mixed = jnp.dot(wm, lvb, preferred_element_type=F32) + bs_ref[h]
                dyb = dya[rows, cols]
                dza_ref[rows, cols] = (dyb * mixed * _dgelu(ua[rows, cols])).astype(BF16)
                dmixed = dyb * gu[rows, cols]
                dmb = dmixed.astype(BF16)
                dlv_ref[rows, cols] = lax.dot_general(wm, dmb, tn_dims, preferred_element_type=F32)
                dwm = dwm + lax.dot_general(dmb, lvb, nt_dims, preferred_element_type=F32)
                dbs = dbs + jnp.sum(dmixed, axis=1, keepdims=True)
            dws_ref[h] += jnp.where(mask, dwm, 0.0)
            dbs_ref[h] += dbs
        dlv = dlv_ref[...]
        dalg_ref[...] += _rowsum(dlv * xh)
        dalb_ref[...] += _rowsum(dlv)
        dgv = _ln_bwd(dlv, xh, r, alg)
        dza_ref[:, D_A:] = (dgv * _dgelu(va)).astype(BF16)
        xhb, rb = _ln_stats(cb_ref[...])
        blg = blg_ref[...]
        lb = xhb * blg + blb_ref[...]
        dlb = dy_ref[:, D_A:].astype(F32) * _dsilu(lb)
        dblg_ref[...] += _rowsum(dlb * xhb)
        dblb_ref[...] += _rowsum(dlb)
        dcb_ref[...] = _ln_bwd(dlb, xhb, rb, blg)

    row = lambda i: (i, 0)
    par = lambda i: (0, 0)
    par3 = lambda i: (0, 0, 0)
    return _pallas(
        comm, body, name=name, grid=(T // tm,),
        in_specs=[pl.BlockSpec((tm, 2 * D_A), row), pl.BlockSpec((tm, D_B), row),
                  pl.BlockSpec((tm, D_A + D_B), row),
                  pl.BlockSpec((1, D_A), par), pl.BlockSpec((1, D_A), par),
                  pl.BlockSpec((A_HEADS, CHUNK, CHUNK), par3),
                  pl.BlockSpec((A_HEADS, CHUNK, 1), par3),
                  pl.BlockSpec((1, D_B), par), pl.BlockSpec((1, D_B), par)],
        out_specs=[pl.BlockSpec((tm, 2 * D_A), row), pl.BlockSpec((tm, D_B), row),
                   pl.BlockSpec((1, D_A), par), pl.BlockSpec((1, D_A), par),
                   pl.BlockSpec((A_HEADS, CHUNK, CHUNK), par3),
                   pl.BlockSpec((A_HEADS, CHUNK, 1), par3),
                   pl.BlockSpec((1, D_B), par), pl.BlockSpec((1, D_B), par)],
        out_shape=[jax.ShapeDtypeStruct((T, 2 * D_A + 2 * D_B), BF16), jax.ShapeDtypeStruct((T, D_B), F32),
                   jax.ShapeDtypeStruct((1, D_A), F32), jax.ShapeDtypeStruct((1, D_A), F32),
                   jax.ShapeDtypeStruct((A_HEADS, CHUNK, CHUNK), F32),
                   jax.ShapeDtypeStruct((A_HEADS, CHUNK, 1), F32),
                   jax.ShapeDtypeStruct((1, D_B), F32), jax.ShapeDtypeStruct((1, D_B), F32)],
        scratch_shapes=[pltpu.VMEM((tm, D_A), F32)],
        compiler_params=_cparams(("arbitrary",), 40),
    )(z, cb, dy, a_ln_g, a_ln_b, w_s, b_s, b_ln_g, b_ln_b)


def _mixer_b_conv_bwd(z, dcb, conv_w, dz, *, tm, name, comm=None):
    T = z.shape[0]
    halo = HALO_LONG

    def body(zb_ref, dcb_ref, dcn_ref, cw_ref, dz_in_ref, dzb_ref, dcw_ref, dbias_ref, dext_ref):
        i = pl.program_id(0)
        last = pl.num_programs(0) - 1

        @pl.when(i == 0)
        def _():
            dcw_ref[...] = jnp.zeros_like(dcw_ref)
            dbias_ref[...] = jnp.zeros_like(dbias_ref)
        dcb = dcb_ref[...]
        dext_ref[0:tm, :] = dcb
        dext_ref[tm:tm + halo, :] = jnp.where(i < last, dcn_ref[...], 0.0)
        dbias_ref[...] += _rowsum(dcb)
        for rb in range(tm // CONV_ROWS):
            for cb in range(D_B // CONV_COLS):
                cs = slice(cb * CONV_COLS, (cb + 1) * CONV_COLS)
                gcs = slice(D_B + cb * CONV_COLS, D_B + (cb + 1) * CONV_COLS)
                rs = slice(rb * CONV_ROWS, (rb + 1) * CONV_ROWS)
                xbb = zb_ref[rs, cs].astype(F32)
                sgb = _sigmoid(zb_ref[rs, gcs].astype(F32))
                yb0 = xbb * sgb
                window = dext_ref[rb * CONV_ROWS:rb * CONV_ROWS + CONV_ROWS + halo, cs]
                acc = jnp.zeros((CONV_ROWS, CONV_COLS), F32)
                for k in range(B_CONV):
                    shifted = _rows_after(window, (B_CONV - 1) - k)[:CONV_ROWS]
                    acc = acc + cw_ref[k:k + 1, cs] * shifted
                    dcw_ref[k:k + 1, cs] += _rowsum(shifted * yb0)
                dzb_ref[rs, cs] = (acc * sgb).astype(BF16)
                dzb_ref[rs, gcs] = (acc * xbb * sgb * (1.0 - sgb)).astype(BF16)

    row = lambda i: (i, 0)
    par = lambda i: (0, 0)
    return _pallas(
        comm, body, name=name, grid=(T // tm,),
        in_specs=[pl.BlockSpec((tm, 2 * D_B), lambda i: (i, 1)),
                  pl.BlockSpec((tm, D_B), row),
                  pl.BlockSpec((halo, D_B), lambda i: (_halo_next_index(tm, halo, T)(i), 0)),
                  pl.BlockSpec((B_CONV, D_B), par), pl.BlockSpec(memory_space=pl.ANY)],
        out_specs=[pl.BlockSpec((tm, 2 * D_B), lambda i: (i, 1)), pl.BlockSpec((B_CONV, D_B), par),
                   pl.BlockSpec((1, D_B), par)],
        out_shape=[jax.ShapeDtypeStruct(dz.shape, BF16), jax.ShapeDtypeStruct((B_CONV, D_B), F32),
                   jax.ShapeDtypeStruct((1, D_B), F32)],
        scratch_shapes=[pltpu.VMEM((tm + halo, D_B), F32)], aliases={4: 0},
        compiler_params=_cparams(("arbitrary",), 40),
    )(z, dcb, dcb, conv_w, dz)


def _rows_before(x, a):
    return x if a == 0 else pltpu.roll(x, a, axis=0)


def _rows_after(x, a):
    return x if a == 0 else pltpu.roll(x, x.shape[0] - a, axis=0)


def _conv3(w_ref, x, halo, cs):
    acc = w_ref[2:3, cs] * x[halo:]
    acc = acc + w_ref[1:2, cs] * _rows_before(x, 1)[halo:]
    return acc + w_ref[0:1, cs] * _rows_before(x, 2)[halo:]


def _mixer_c_fwd(z, conv_w, *, tm, name, comm=None):
    T = z.shape[0]
    D = D_MODEL
    halo = HALO_SHORT
    W = CONV_COLS

    def body(bg_ref, cg_ref, xv_ref, cgh_ref, xvh_ref, w_ref, r_ref):
        i = pl.program_id(0)
        for cb in range(D // W):
            cs = slice(cb * W, (cb + 1) * W)
            prev = jnp.where(i > 0, cgh_ref[:, cs].astype(F32) * xvh_ref[:, cs].astype(F32), 0.0)
            p = jnp.concatenate([prev, cg_ref[:, cs].astype(F32) * xv_ref[:, cs].astype(F32)], axis=0)
            r_ref[:, cs] = (bg_ref[:, cs].astype(F32) * _conv3(w_ref, p, halo, cs)).astype(BF16)

    hp = _halo_prev_index(tm, halo)
    return _pallas(
        comm, body, name=name, grid=(T // tm,),
        in_specs=[pl.BlockSpec((tm, D), lambda i: (i, 0)), pl.BlockSpec((tm, D), lambda i: (i, 1)),
                  pl.BlockSpec((tm, D), lambda i: (i, 2)),
                  pl.BlockSpec((halo, D), lambda i: (hp(i), 1)),
                  pl.BlockSpec((halo, D), lambda i: (hp(i), 2)),
                  pl.BlockSpec((None, C_CONV, D), lambda i: (0, 0, 0))],
        out_specs=pl.BlockSpec((tm, D), lambda i: (i, 0)),
        out_shape=jax.ShapeDtypeStruct((T, D), BF16),
        compiler_params=_cparams(("parallel",), 40),
    )(z, z, z, z, z, conv_w)


def _mixer_c_bwd(z, dr, conv_w, *, tm, name, comm=None):
    T = z.shape[0]
    D = D_MODEL
    halo = HALO_SHORT
    W = CONV_COLS

    def body(bg_ref, cg_ref, xv_ref, cgh_ref, xvh_ref, bgn_ref, dr_ref, drn_ref, w_ref, dz_ref, dw_ref):
        i = pl.program_id(0)
        last = pl.num_programs(0) - 1

        @pl.when(i == 0)
        def _():
            dw_ref[...] = jnp.zeros_like(dw_ref)
        for cb in range(D // W):
            cs = slice(cb * W, (cb + 1) * W)
            cg = cg_ref[:, cs].astype(F32)
            xv = xv_ref[:, cs].astype(F32)
            dr = dr_ref[:, cs].astype(F32)
            p = cg * xv
            prev = jnp.where(i > 0, cgh_ref[:, cs].astype(F32) * xvh_ref[:, cs].astype(F32), 0.0)
            q = _conv3(w_ref, jnp.concatenate([prev, p], axis=0), halo, cs)
            dz_ref[:, cs] = (dr * q).astype(BF16)
            nxt = jnp.where(i < last, drn_ref[:, cs].astype(F32) * bgn_ref[:, cs].astype(F32), 0.0)
            dq = jnp.concatenate([dr * bg_ref[:, cs].astype(F32), nxt], axis=0)
            dp = None
            for k in range(C_CONV):
                shifted = _rows_after(dq, 2 - k)[:tm]
                term = w_ref[k:k + 1, cs] * shifted
                dp = term if dp is None else dp + term
                dw_ref[k:k + 1, cs] += _rowsum(shifted * p)
            dz_ref[:, D + cb * W:D + (cb + 1) * W] = (dp * xv).astype(BF16)
            dz_ref[:, 2 * D + cb * W:2 * D + (cb + 1) * W] = (dp * cg).astype(BF16)

    hp = _halo_prev_index(tm, halo)
    hn = _halo_next_index(tm, halo, T)
    return _pallas(
        comm, body, name=name, grid=(T // tm,),
        in_specs=[pl.BlockSpec((tm, D), lambda i: (i, 0)), pl.BlockSpec((tm, D), lambda i: (i, 1)),
                  pl.BlockSpec((tm, D), lambda i: (i, 2)),
                  pl.BlockSpec((halo, D), lambda i: (hp(i), 1)),
                  pl.BlockSpec((halo, D), lambda i: (hp(i), 2)),
                  pl.BlockSpec((halo, D), lambda i: (hn(i), 0)),
                  pl.BlockSpec((tm, D), lambda i: (i, 0)),
                  pl.BlockSpec((halo, D), lambda i: (hn(i), 0)),
                  pl.BlockSpec((None, C_CONV, D), lambda i: (0, 0, 0))],
        out_specs=[pl.BlockSpec((tm, 3 * D), lambda i: (i, 0)),
                   pl.BlockSpec((C_CONV, D), lambda i: (0, 0))],
        out_shape=[jax.ShapeDtypeStruct((T, 3 * D), BF16), jax.ShapeDtypeStruct((C_CONV, D), F32)],
        compiler_params=_cparams(("arbitrary",), 48),
    )(z, z, z, z, z, z, dr, dr, conv_w)


FFN_COLS = 128


def _ffn_act_fwd(up, conv_w, *, layer, tm, name, comm=None):
    T = up.shape[0]
    halo = HALO_SHORT
    W = FFN_COLS

    def body(up_ref, uph_ref, w_ref, a_ref, upc_ref):
        i = pl.program_id(0)

        def conv(cs):
            prev = jnp.where(i > 0, uph_ref[:, cs], jnp.zeros((halo, W), BF16))
            return _conv3(w_ref, jnp.concatenate([prev, up_ref[:, cs]], axis=0).astype(F32), halo, cs)

        for cb in range(D_FF // W):
            gs = slice(cb * W, (cb + 1) * W)
            vs = slice(D_FF + cb * W, D_FF + (cb + 1) * W)
            g = conv(gs)
            v = conv(vs)
            upc_ref[:, gs] = g.astype(BF16)
            upc_ref[:, vs] = v.astype(BF16)
            a_ref[:, gs] = (_silu(g) * v).astype(BF16)

    return _pallas(
        comm, body, name=name, grid=(T // tm,),
        in_specs=[pl.BlockSpec((tm, 2 * D_FF), lambda i: (i, 0)),
                  pl.BlockSpec((halo, 2 * D_FF), lambda i: (_halo_prev_index(tm, halo)(i), 0)),
                  pl.BlockSpec((None, F_CONV, 2 * D_FF), lambda i: (layer, 0, 0))],
        out_specs=[pl.BlockSpec((tm, D_FF), lambda i: (i, 0)),
                   pl.BlockSpec((tm, 2 * D_FF), lambda i: (i, 0))],
        out_shape=[jax.ShapeDtypeStruct((T, D_FF), BF16), jax.ShapeDtypeStruct((T, 2 * D_FF), BF16)],
        compiler_params=_cparams(("parallel",), 48),
    )(up, up, conv_w)


def _ffn_act_bwd(up, upc, da, conv_w, *, layer, tm, name, comm=None):
    T = up.shape[0]
    halo = HALO_SHORT
    W = FFN_COLS

    def body(up_ref, upc_ref, upcn_ref, da_ref, dan_ref, w_ref, dup_ref, dw_ref):
        i = pl.program_id(0)
        last = pl.num_programs(0) - 1

        @pl.when(i == 0)
        def _():
            dw_ref[...] = jnp.zeros_like(dw_ref)
        live = jnp.where(i < last, 1.0, 0.0)
        for cb in range(D_FF // W):
            gs = slice(cb * W, (cb + 1) * W)
            vs = slice(D_FF + cb * W, D_FF + (cb + 1) * W)
            g = jnp.concatenate([upc_ref[:, gs], upcn_ref[:, gs]], axis=0).astype(F32)
            v = jnp.concatenate([upc_ref[:, vs], upcn_ref[:, vs]], axis=0).astype(F32)
            da = jnp.concatenate([da_ref[:, gs].astype(F32), dan_ref[:, gs].astype(F32) * live], axis=0)
            s = _sigmoid(g)
            silu = g * s
            grads = (da * v * (s * (1.0 + g * (1.0 - s))), da * silu)
            for cs, d in zip((gs, vs), grads):
                u = up_ref[:, cs].astype(F32)
                acc = None
                for k in range(F_CONV):
                    shifted = _rows_after(d, 2 - k)[:tm]
                    term = w_ref[k:k + 1, cs] * shifted
                    acc = term if acc is None else acc + term
                    dw_ref[k:k + 1, cs] += _rowsum(shifted * u)
                dup_ref[:, cs] = acc.astype(BF16)

    hn = _halo_next_index(tm, halo, T)
    return _pallas(
        comm, body, name=name, grid=(T // tm,),
        in_specs=[pl.BlockSpec((tm, 2 * D_FF), lambda i: (i, 0)),
                  pl.BlockSpec((tm, 2 * D_FF), lambda i: (i, 0)),
                  pl.BlockSpec((halo, 2 * D_FF), lambda i: (hn(i), 0)),
                  pl.BlockSpec((tm, D_FF), lambda i: (i, 0)),
                  pl.BlockSpec((halo, D_FF), lambda i: (hn(i), 0)),
                  pl.BlockSpec((None, F_CONV, 2 * D_FF), lambda i: (layer, 0, 0))],
        out_specs=[pl.BlockSpec((tm, 2 * D_FF), lambda i: (i, 0)),
                   pl.BlockSpec((F_CONV, 2 * D_FF), lambda i: (0, 0))],
        out_shape=[jax.ShapeDtypeStruct((T, 2 * D_FF), BF16),
                   jax.ShapeDtypeStruct((F_CONV, 2 * D_FF), F32)],
        compiler_params=_cparams(("arbitrary",), 56),
    )(up, upc, upc, da, da, conv_w)


def _local_step(x, tgt, small, plan):
    T = x.shape[0]
    tm_e = _pick(T, 256)
    tm_n = _pick(T, 512)
    tm = _pick(T, 1024)
    tm_f = _pick(T, 512)
    tt = _pick(T, 2048)
    nm = small["norm_mix"].reshape(2, 1, D_MODEL)
    nf = small["norm_ffn"].reshape(2, 1, D_MODEL)
    ngf = small["norm_final"].reshape(1, D_MODEL)
    b_s = small["a_b_s"].reshape(A_HEADS, CHUNK, 1)
    w_s = small["a_w_s"].reshape(A_HEADS, CHUNK, CHUNK)
    b_conv_w = small["b_conv_w"].reshape(B_CONV, D_B)
    sg = {}
    wt, cm = plan.weight, plan.comm

    h_m0 = _rmsnorm_fwd(x, nm, layer=0, tm=tm_n, name="norm_mix0")
    z_ab = _mm_nn(h_m0, wt("ab_w_in", 0), layer=0, tm=tm, tn=512, out_dtype=BF16, name="ab_in", comm=cm("ab_in"))
    yab, cb = _mixer_ab_fwd(z_ab, small["a_ln_g"], small["a_ln_b"], w_s, b_s, b_conv_w, small["b_conv_b"],
                            small["b_ln_g"], small["b_ln_b"], tm=tm_e, name="mixer_ab", comm=cm("mixer_ab"))
    x1, h_f0 = _mm_nn(yab, wt("ab_w_out", 0), layer=0, tm=tm, tn=D_MODEL, residual=x, norm=(nf, 0),
                      name="ab_out", comm=cm("ab_out"))

    def ffn_fwd(xin, h, layer, norm):
        up = _mm_nn(h, wt("f_w_up", layer), layer=0, tm=tm, tn=1408, out_dtype=BF16, name=f"ffn_up{layer}",
                    comm=cm(f"ffn_up{layer}"))
        a, upc = _ffn_act_fwd(up, small["f_conv_w"], layer=layer, tm=tm_e, name=f"ffn_act{layer}",
                              comm=cm(f"ffn_act{layer}"))
        out = _mm_nn(a, wt("f_w_down", layer), layer=0, tm=tm, tn=D_MODEL, residual=xin, norm=norm,
                     name=f"ffn_down{layer}", comm=cm(f"ffn_down{layer}"))
        return up, upc, a, out

    up0, upc0, a0, (x2, h_m1) = ffn_fwd(x1, h_f0, 0, (nm, 1))
    z_c = _mm_nn(h_m1, wt("c_w_in", 0), layer=0, tm=tm, tn=768, out_dtype=BF16, name="c_in", comm=cm("c_in"))
    r = _mixer_c_fwd(z_c, small["c_conv_w"], tm=tm_e, name="mixer_c", comm=cm("mixer_c"))
    x3, h_f1 = _mm_nn(r, wt("c_w_out", 0), layer=0, tm=tm, tn=D_MODEL, residual=x2, norm=(nf, 1),
                      name="c_out", comm=cm("c_out"))
    up1, upc1, a1, x4 = ffn_fwd(x3, h_f1, 1, None)
    loss, dx, sg["norm_final"] = _loss_head(x4, tgt, ngf, tm=tm_n, name="loss_head")

    def ffn_bwd(dx, xin, h, up, upc, a, layer):
        da = _mm_nt(dx, wt("f_w_down", layer), layer=0, tm=tm, tn=1408, out_dtype=BF16,
                    name=f"ffn_down_dx{layer}", comm=cm(f"ffn_down_dx{layer}"))
        plan.grad_ready("f_w_down", layer, _mm_tn(a, dx, shards=None, tk=1408, tn=1024, tt=tt,
                                                  name=f"ffn_down_dw{layer}", comm=cm(f"ffn_down_dw{layer}")))
        dup, dcw = _ffn_act_bwd(up, upc, da, small["f_conv_w"], layer=layer, tm=tm_e, name=f"ffn_act_bwd{layer}",
                                comm=cm(f"ffn_act_bwd{layer}"))
        dxin, dg = _mm_nt_norm(dup, wt("f_w_up", layer), xin, nf, dx, g_layer=layer, tm=tm_f,
                               name=f"ffn_up_dx{layer}", comm=cm(f"ffn_up_dx{layer}"))
        plan.grad_ready("f_w_up", layer, _mm_tn(h, dup, shards=N_CHIPS, tk=1024, tn=1408, tt=tt,
                                                name=f"ffn_up_dw{layer}", comm=cm(f"ffn_up_dw{layer}")))
        return dxin, dg, dcw

    dx, dnf1, dfc1 = ffn_bwd(dx, x3, h_f1, up1, upc1, a1, 1)
    dr = _mm_nt(dx, wt("c_w_out", 0), layer=0, tm=tm, tn=512, out_dtype=BF16, name="c_out_dx", comm=cm("c_out_dx"))
    plan.grad_ready("c_w_out", 0, _mm_tn(r, dx, shards=None, tk=1024, tn=1024, tt=tt, name="c_out_dw",
                                         comm=cm("c_out_dw")))
    dz_c, dccw = _mixer_c_bwd(z_c, dr, small["c_conv_w"], tm=tm_e, name="mixer_c_bwd", comm=cm("mixer_c_bwd"))
    sg["c_conv_w"] = dccw.reshape(1, C_CONV, D_MODEL)
    plan.grad_ready("c_w_in", 0, _mm_tn(h_m1, dz_c, shards=N_CHIPS, tk=1024, tn=768, tt=tt, name="c_in_dw",
                                        comm=cm("c_in_dw")))
    dx, dnm1 = _mm_nt_norm(dz_c, wt("c_w_in", 0), x2, nm, dx, g_layer=1, tm=tm_f, name="c_in_dx",
                           comm=cm("c_in_dx"))
    dx, dnf0, dfc0 = ffn_bwd(dx, x1, h_f0, up0, upc0, a0, 0)
    dyab = _mm_nt(dx, wt("ab_w_out", 0), layer=0, tm=tm, tn=512, out_dtype=BF16, name="ab_out_dx",
                  comm=cm("ab_out_dx"))
    plan.grad_ready("ab_w_out", 0, _mm_tn(yab, dx, shards=None, tk=1024, tn=1024, tt=tt, name="ab_out_dw",
                                          comm=cm("ab_out_dw")))
    (dza, dcb, sg["a_ln_g"], sg["a_ln_b"], dws, dbs, sg["b_ln_g"], sg["b_ln_b"]) = _mixer_ab_bwd_pre(
        z_ab, cb, dyab, small["a_ln_g"], small["a_ln_b"], w_s, b_s, small["b_ln_g"], small["b_ln_b"],
        tm=tm_e, name="mixer_ab_bwd", comm=cm("mixer_ab_bwd"))
    dz_ab, dbcw, sg["b_conv_b"] = _mixer_b_conv_bwd(z_ab, dcb, b_conv_w, dza, tm=tm_e, name="mixer_b_conv_bwd",
                                                    comm=cm("mixer_b_conv_bwd"))
    sg["a_w_s"] = dws.reshape(1, A_HEADS, CHUNK, CHUNK)
    sg["a_b_s"] = dbs.reshape(1, A_HEADS, CHUNK)
    sg["b_conv_w"] = dbcw.reshape(1, B_CONV, D_B)
    plan.grad_ready("ab_w_in", 0, _mm_tn(h_m0, dz_ab, shards=N_CHIPS, tk=1024, tn=512, tt=tt, name="ab_in_dw",
                                         comm=cm("ab_in_dw")))
    dx, dnm0 = _mm_nt_norm(dz_ab, wt("ab_w_in", 0), x, nm, dx, g_layer=0, tm=tm_f, name="ab_in_dx",
                           comm=cm("ab_in_dx"))

    sg["norm_mix"] = [dnm0, dnm1]
    sg["norm_ffn"] = [dnf0, dnf1]
    sg["f_conv_w"] = [dfc0, dfc1]
    return loss, dx, sg


BLOCK_BYTES = 3 * 1024 * 1024


BF16_SUBLANES = 16


def _row_tile(rows, row_bytes, step=SUBLANES):
    best = None
    for tr in range(step, rows + 1, step):
        if rows % tr == 0 and tr * row_bytes <= BLOCK_BYTES:
            best = tr
    if best is None:
        raise ValueError(f"no row tile for {rows}")
    return best


def _place_scalars():
    x, y, c = lax.axis_index("x"), lax.axis_index("y"), lax.axis_index("c")
    return jnp.stack([c, 2 * x + y, 2 * (1 - x) + y, 2 * x + (1 - y), 2 * (1 - x) + (1 - y)]).astype(jnp.int32)


def _cast_into_slot(w, place, *, layer, name):
    L, rows, cols = w.shape
    tr = _row_tile(rows, cols * 4, BF16_SUBLANES)

    def body(place_ref, w_ref, o_ref):
        o_ref[...] = w_ref[...].astype(BF16)

    return pl.pallas_call(
        body, name=name,
        grid_spec=pltpu.PrefetchScalarGridSpec(
            num_scalar_prefetch=1, grid=(rows // tr,),
            in_specs=[pl.BlockSpec((None, tr, cols), lambda i, p: (layer, i, 0))],
            out_specs=pl.BlockSpec((None, None, tr, cols), lambda i, p: (0, p[1], i, 0))),
        out_shape=jax.ShapeDtypeStruct((1, N_CHIPS, rows, cols), BF16),
        compiler_params=_cparams(("parallel",), 32),
    )(place, w)


def _pair_sum(g, theirs, place, *, name):
    S, rows, cols = g.shape
    half = rows // 2
    tr = _row_tile(half, cols * 4, BF16_SUBLANES)
    nb = half // tr

    def body(place_ref, g_ref, t_ref, o_ref):
        o_ref[...] = (g_ref[...] + t_ref[...]).astype(BF16)

    spec = pl.BlockSpec((None, tr, cols), lambda s, i, p: (s, i, 0))
    return pl.pallas_call(
        body, name=name,
        grid_spec=pltpu.PrefetchScalarGridSpec(
            num_scalar_prefetch=1, grid=(S, nb),
            in_specs=[pl.BlockSpec((None, tr, cols), lambda s, i, p: (s, p[0] * nb + i, 0)), spec],
            out_specs=spec),
        out_shape=jax.ShapeDtypeStruct((S, half, cols), BF16),
        compiler_params=_cparams(("parallel", "parallel"), 32),
    )(place, g, theirs)


def _chip_sum(p, r, g_prev, place, *, layer, shape, name):
    L, rows, cols = shape
    half = rows // 2
    tr = _row_tile(half, cols * 4, BF16_SUBLANES)
    nb = half // tr

    def body(place_ref, p_ref, r_ref, *rest):
        o_ref = rest[-1]
        mine = p_ref[...].astype(F32)
        peers = [r_ref[j].astype(F32) for j in range(3)]
        acc = None
        for s in range(N_CHIPS):
            term = jnp.where(place_ref[1] == s, mine,
                             jnp.where(place_ref[2] == s, peers[0],
                                       jnp.where(place_ref[3] == s, peers[1], peers[2])))
            acc = term if acc is None else acc + term
        o_ref[...] = acc

    in_specs = [pl.BlockSpec((None, tr, cols), lambda i, pr: (pr[1], i, 0)),
                pl.BlockSpec((3, tr, cols), lambda i, pr: (0, i, 0))]
    args = [place, p, r]
    aliases = {}
    if g_prev is not None:
        in_specs.append(ANY)
        args.append(g_prev)
        aliases = {3: 0}
    return pl.pallas_call(
        body, name=name,
        grid_spec=pltpu.PrefetchScalarGridSpec(
            num_scalar_prefetch=1, grid=(nb,), in_specs=in_specs,
            out_specs=pl.BlockSpec((None, tr, cols), lambda i, pr: (layer, pr[0] * nb + i, 0))),
        out_shape=jax.ShapeDtypeStruct(shape, F32), input_output_aliases=aliases,
        compiler_params=_cparams(("parallel",), 32),
    )(*args)


def _adamw_math(w, g, m, v):
    m2 = ADAM_B1 * m + (1.0 - ADAM_B1) * g
    v2 = ADAM_B2 * v + (1.0 - ADAM_B2) * (g * g)
    m_hat = m2 / (1.0 - ADAM_B1 ** ADAM_STEP)
    v_hat = v2 / (1.0 - ADAM_B2 ** ADAM_STEP)
    delta = -ADAM_LR * (m_hat / (jnp.sqrt(v_hat) + ADAM_EPS) + ADAM_WD * w)
    return delta, m2, v2


def _adamw(w, g, m, v, *, name):
    L, rows, cols = w.shape
    tr = _row_tile(rows, cols * 4)

    def body(w_ref, g_ref, m_ref, v_ref, d_ref, m2_ref, v2_ref):
        d, m2, v2 = _adamw_math(w_ref[...], g_ref[...], m_ref[...], v_ref[...])
        d_ref[...] = d
        m2_ref[...] = m2
        v2_ref[...] = v2

    spec = pl.BlockSpec((None, tr, cols), lambda l, i: (l, i, 0))
    shape = jax.ShapeDtypeStruct(w.shape, F32)
    return pl.pallas_call(
        body, name=name, grid=(L, rows // tr), in_specs=[spec] * 4, out_specs=[spec] * 3,
        out_shape=[shape] * 3,
        compiler_params=_cparams(("parallel", "parallel"), 48),
    )(w, g, m, v)


def _gather_packs(pack, *, name):
    R = pack.shape[0]
    ndev = 2 * N_CHIPS

    def body(p_ref, buf, send, recv):
        x, y, c = lax.axis_index("x"), lax.axis_index("y"), lax.axis_index("c")
        me = 4 * x + 2 * y + c
        buf[me] = p_ref[...]
        sends = []
        for q in range(1, ndev):
            qx, qy, qc = (q >> 2) & 1, (q >> 1) & 1, q & 1
            peer = (x ^ qx, y ^ qy, c ^ qc)
            rc = _remote(p_ref, buf.at[me], send.at[q - 1], recv.at[q - 1], peer)
            rc.start()
            sends.append(rc)
        for q in range(1, ndev):
            qx, qy, qc = (q >> 2) & 1, (q >> 1) & 1, q & 1
            slot = buf.at[4 * (x ^ qx) + 2 * (y ^ qy) + (c ^ qc)]
            _remote(slot, slot, send.at[q - 1], recv.at[q - 1], (x ^ qx, y ^ qy, c ^ qc)).wait_recv()
        for rc in sends:
            rc.wait_send()

    vm = pl.BlockSpec(memory_space=pltpu.VMEM)
    return pl.pallas_call(
        body, name=name, in_specs=[vm], out_specs=vm, out_shape=jax.ShapeDtypeStruct((ndev, R, LANES), F32),
        scratch_shapes=[pltpu.SemaphoreType.DMA((ndev - 1,)), pltpu.SemaphoreType.DMA((ndev - 1,))],
        compiler_params=pltpu.CompilerParams(vmem_limit_bytes=VMEM_BYTES_MAX),
    )(pack)


def _allreduce_pack(pack, *, name):
    R = pack.shape[0]
    half = R // 2

    def body(p_ref, o_ref, sib_ref, chip_ref, parts_ref, sems):
        x, y, c, k, sib, peers = _place()
        swap = _remote(p_ref, sib_ref, sems.at[0, 0], sems.at[0, 1], sib)
        swap.start()
        swap.wait()
        chip_ref[...] = p_ref[...] + sib_ref[...]
        mine = chip_ref.at[pl.ds(pl.multiple_of(c * half, SUBLANES), half)]
        sends = [_remote(mine, parts_ref.at[j], sems.at[1 + j, 0], sems.at[1 + j, 1], (px, py, c))
                 for j, (px, py) in enumerate(peers)]
        for rc in sends:
            rc.start()
        for rc in sends:
            rc.wait()
        own = mine[...]
        others = [parts_ref[j] for j in range(3)]
        acc = None
        for s in range(N_CHIPS):
            term = own
            for j, (px, py) in enumerate(peers):
                term = jnp.where(2 * px + py == s, others[j], term)
            acc = term if acc is None else acc + term
        done = o_ref.at[pl.ds(pl.multiple_of(c * half, SUBLANES), half)]
        done[...] = acc
        theirs = o_ref.at[pl.ds(pl.multiple_of((1 - c) * half, SUBLANES), half)]
        share = _remote(done, done, sems.at[4, 0], sems.at[4, 1], sib)
        share.start()
        _remote(done, theirs, sems.at[4, 0], sems.at[4, 1], sib).wait()

    vm = pl.BlockSpec(memory_space=pltpu.VMEM)
    return pl.pallas_call(
        body, name=name, in_specs=[vm], out_specs=vm, out_shape=jax.ShapeDtypeStruct((R, LANES), F32),
        scratch_shapes=[pltpu.VMEM((R, LANES), F32), pltpu.VMEM((R, LANES), F32),
                        pltpu.VMEM((3, half, LANES), F32), pltpu.SemaphoreType.DMA((5, 2))],
        compiler_params=pltpu.CompilerParams(vmem_limit_bytes=VMEM_BYTES_MAX),
    )(pack)


PACK_UNIT = SUBLANES * LANES


def _pack(arrays):
    flat, sizes = [], []
    for a in arrays:
        pieces = a if isinstance(a, (list, tuple)) else [a]
        v = jnp.concatenate([p.reshape(-1) for p in pieces]) if len(pieces) > 1 else pieces[0].reshape(-1)
        size = v.shape[0]
        padded = -(-size // PACK_UNIT) * PACK_UNIT
        flat.append(jnp.pad(v, (0, padded - size)))
        sizes.append((size, padded))
    total = sum(p for _, p in sizes)
    if (total // PACK_UNIT) % 2:
        flat.append(jnp.zeros((PACK_UNIT,), F32))
    return jnp.concatenate(flat).reshape(-1, LANES), sizes


def _unpack(pack, sizes, shapes):
    v = pack.reshape(-1)
    out, off = [], 0
    for (size, padded), shape in zip(sizes, shapes):
        out.append(v[off:off + size].reshape(shape))
        off += padded
    return out


BIG = ("ab_w_in", "ab_w_out", "c_w_in", "c_w_out", "f_w_up", "f_w_down")
COL_SHARDED = ("ab_w_in", "c_w_in", "f_w_up")
SMALL_REPLICATED = ("norm_mix", "norm_ffn", "norm_final", "a_ln_g", "a_ln_b", "a_w_s", "a_b_s",
                    "b_conv_b", "b_ln_g", "b_ln_b")
SMALL_SHARDED = ("b_conv_w", "c_conv_w", "f_conv_w")
SMALL = SMALL_REPLICATED + SMALL_SHARDED
ALL_WEIGHTS = ("norm_mix", "norm_ffn", "norm_final", "ab_w_in", "a_ln_g", "a_ln_b", "a_w_s", "a_b_s",
               "b_conv_w", "b_conv_b", "b_ln_g", "b_ln_b", "ab_w_out", "c_w_in", "c_conv_w", "c_w_out",
               "f_w_up", "f_conv_w", "f_w_down")


SCHEDULE = {
    "ab_in": [("gi", "f_w_up", 0, 0, 4), ("gi", "ab_w_out", 0)],
    "mixer_ab": [("gd", "f_w_up", 0, 0, 4), ("gd", "ab_w_out", 0), ("gi", "f_w_up", 0, 1, 4),
                 ("gi", "f_w_up", 0, 2, 4), ("gi", "f_w_up", 0, 3, 4)],
    "ab_out": [("gd", "f_w_up", 0, 1, 4), ("gd", "f_w_up", 0, 2, 4), ("gd", "f_w_up", 0, 3, 4)],
    "ffn_up0": [("gi", "f_w_down", 0), ("gi", "c_w_in", 0, 0, 2)],
    "ffn_act0": [("gd", "f_w_down", 0), ("gd", "c_w_in", 0, 0, 2), ("gi", "c_w_in", 0, 1, 2),
                 ("gi", "f_w_up", 1, 0, 4), ("gi", "f_w_up", 1, 1, 4)],
    "ffn_down0": [("gd", "c_w_in", 0, 1, 2), ("gd", "f_w_up", 1, 0, 4), ("gd", "f_w_up", 1, 1, 4),
                  ("gi", "f_w_up", 1, 2, 4)],
    "c_in": [("gd", "f_w_up", 1, 2, 4), ("gi", "f_w_up", 1, 3, 4), ("gi", "c_w_out", 0)],
    "mixer_c": [("gd", "f_w_up", 1, 3, 4), ("gd", "c_w_out", 0), ("gi", "f_w_down", 1, 0, 2)],
    "c_out": [("gd", "f_w_down", 1, 0, 2), ("gi", "f_w_down", 1, 1, 2)],
    "ffn_up1": [("gd", "f_w_down", 1, 1, 2)],
    "ffn_act_bwd1": [("px", "f_w_down", 1)],
    "ffn_up_dx1": [("cx", "f_w_down", 1)],
    "mixer_c_bwd": [("px", "f_w_up", 1), ("px", "c_w_out", 0)],
    "c_in_dw": [("cx", "f_w_up", 1, 0, 2), ("cx", "c_w_out", 0)],
    "c_in_dx": [("cx", "f_w_up", 1, 1, 2), ("px", "c_w_in", 0)],
    "ffn_down_dx0": [("cx", "c_w_in", 0, 0, 2)],
    "ffn_down_dw0": [("cx", "c_w_in", 0, 1, 2)],
    "ffn_act_bwd0": [("px", "f_w_down", 0)],
    "ffn_up_dx0": [("cx", "f_w_down", 0)],
    "mixer_ab_bwd": [("px", "f_w_up", 0), ("px", "ab_w_out", 0)],
    "mixer_b_conv_bwd": [("cx", "f_w_up", 0, 0, 2), ("cx", "ab_w_out", 0)],
    "ab_in_dw": [("cx", "f_w_up", 0, 1, 2)],
    "ab_in_dx": [("px", "ab_w_in", 0)],
}


class _Plan:
    def __init__(self, shapes, place):
        self.shapes, self.place, self.bufs = shapes, place, {}

    def weight(self, name, layer):
        g = self.bufs[f"w:{name}:{layer}"]
        if name in COL_SHARDED:
            return g
        _, S, rows, cols = g.shape
        return g.reshape(1, S * rows, cols)

    def grad_ready(self, name, layer, g):
        _, rows, cols = self.shapes[name]
        hbm = lambda a: pltpu.with_memory_space_constraint(a, pltpu.HBM)
        self.bufs[f"g:{name}:{layer}"] = g.reshape(N_CHIPS, rows, cols)
        self.bufs[f"t:{name}:{layer}"] = hbm(lax.empty((N_CHIPS, rows // 2, cols), F32))
        self.bufs[f"l:{name}:{layer}"] = hbm(lax.empty((3, rows // 2, cols), BF16))

    def job(self, kind, name, layer, part=0, parts=1):
        _, rows, cols = self.shapes[name]
        key = f"{name}:{layer}"
        if kind == "gi":
            return _job_gather_ici("w:" + key, rows, part, parts)
        if kind == "gd":
            return _job_gather_d2d("w:" + key, rows, part, parts)
        if kind == "px":
            return _job_pair_exchange("g:" + key, "t:" + key, rows)
        if kind == "cx":
            if "p:" + key not in self.bufs:
                self.bufs["p:" + key] = _pair_sum(self.bufs["g:" + key], self.bufs["t:" + key], self.place,
                                                  name=f"pair_sum_{name}{layer}")
            nr = rows // 2 // parts
            return _job_chip_exchange("p:" + key, "l:" + key, part * nr, nr)
        if kind == "ps":
            return _job_pair_share("G:" + name, layer, rows)
        raise ValueError(kind)

    def comm(self, call):
        specs = SCHEDULE.get(call)
        return None if specs is None else _Comm(self, [self.job(*spec) for spec in specs])


def _step(x, tgt, w, m, v):
    chip = 2 * lax.axis_index("x") + lax.axis_index("y")
    place = _place_scalars()
    plan = _Plan({n: w[n].shape for n in BIG}, place)
    items = [(n, l) for n in BIG for l in range(w[n].shape[0])]

    for n, l in items:
        plan.bufs[f"w:{n}:{l}"] = _cast_into_slot(w[n], place, layer=l, name=f"cast_{n}{l}")
    _comm_only(plan, [[plan.job("gi", "ab_w_in", 0)], [plan.job("gd", "ab_w_in", 0)]], name="gather_first")
    conv_pack, conv_sizes = _pack([w[n] for n in SMALL_SHARDED])
    conv_all = _gather_packs(conv_pack, name="gather_conv_weights")
    conv_shapes = [w[n].shape for n in SMALL_SHARDED]
    per_chip = [_unpack(conv_all[2 * s], conv_sizes, conv_shapes) for s in range(N_CHIPS)]
    small = {n: w[n] for n in SMALL_REPLICATED}
    for idx, n in enumerate(SMALL_SHARDED):
        small[n] = jnp.concatenate([per_chip[s][idx] for s in range(N_CHIPS)], axis=-1)

    loss, dx, sg = _local_step(x, tgt, small, plan)

    _comm_only(plan, [[plan.job("cx", "ab_w_in", 0)]], name="reduce_last")
    for n, l in items:
        plan.bufs["G:" + n] = _chip_sum(plan.bufs[f"p:{n}:{l}"], plan.bufs[f"l:{n}:{l}"], plan.bufs.get("G:" + n),
                                        place, layer=l, shape=w[n].shape, name=f"chip_sum_{n}{l}")
    _comm_only(plan, [[plan.job("ps", n, l) for n, l in items]], name="reduce_pair_share")
    grads_big = [plan.bufs["G:" + n] for n in BIG]

    g_pack, g_sizes = _pack([sg[n] for n in SMALL])
    g_sum = _allreduce_pack(g_pack, name="allreduce_small_grads")
    full_shapes = [small[n].shape for n in SMALL]
    g_small = dict(zip(SMALL, _unpack(g_sum, g_sizes, full_shapes)))
    for n in SMALL_SHARDED:
        width = w[n].shape[-1]
        g_small[n] = lax.dynamic_slice_in_dim(g_small[n], chip * width, width, axis=g_small[n].ndim - 1)

    grad, delta, new_m, new_v = {}, {}, {}, {}
    for n, g in zip(BIG, grads_big):
        grad[n] = g
        delta[n], new_m[n], new_v[n] = _adamw(w[n], g, m[n], v[n], name=f"adamw_{n}")
    shapes = [w[n].shape for n in SMALL]
    wp, sizes = _pack([w[n] for n in SMALL])
    gp, _ = _pack([g_small[n] for n in SMALL])
    mp, _ = _pack([m[n] for n in SMALL])
    vp, _ = _pack([v[n] for n in SMALL])
    R = wp.shape[0]
    dp, m2p, v2p = _adamw(wp.reshape(1, R, LANES), gp.reshape(1, R, LANES), mp.reshape(1, R, LANES),
                          vp.reshape(1, R, LANES), name="adamw_small")
    for n, d_, m_, v_ in zip(SMALL, _unpack(dp, sizes, shapes), _unpack(m2p, sizes, shapes),
                             _unpack(v2p, sizes, shapes)):
        grad[n] = g_small[n]
        delta[n], new_m[n], new_v[n] = d_, m_, v_
    return loss, dx, grad, delta, new_m, new_v


def kernel(x, norm_mix, norm_ffn, norm_final, ab_w_in, a_ln_g, a_ln_b, a_w_s, a_b_s, b_conv_w, b_conv_b, b_ln_g, b_ln_b, ab_w_out, c_w_in, c_conv_w, c_w_out, f_w_up, f_conv_w, f_w_down, loss_target, m_norm_mix, m_norm_ffn, m_norm_final, m_ab_w_in, m_a_ln_g, m_a_ln_b, m_a_w_s, m_a_b_s, m_b_conv_w, m_b_conv_b, m_b_ln_g, m_b_ln_b, m_ab_w_out, m_c_w_in, m_c_conv_w, m_c_w_out, m_f_w_up, m_f_conv_w, m_f_w_down, v_norm_mix, v_norm_ffn, v_norm_final, v_ab_w_in, v_a_ln_g, v_a_ln_b, v_a_w_s, v_a_b_s, v_b_conv_w, v_b_conv_b, v_b_ln_g, v_b_ln_b, v_ab_w_out, v_c_w_in, v_c_conv_w, v_c_w_out, v_f_w_up, v_f_conv_w, v_f_w_down):
    given = dict(locals())
    w = {n: given[n] for n in ALL_WEIGHTS}
    m = {n: given["m_" + n] for n in ALL_WEIGHTS}
    v = {n: given["v_" + n] for n in ALL_WEIGHTS}
    T = x.shape[1]
    loss, dx, grad, delta, new_m, new_v = _step(x.reshape(T, D_MODEL), loss_target.reshape(T, D_MODEL), w, m, v)
    loss = lax.psum(loss[0, 0], ("x", "y", "c"))
    out = [loss, dx.reshape(x.shape)]
    for d in (grad, delta, new_m, new_v):
        out += [d[n] for n in ALL_WEIGHTS]
    return tuple(out)
```

```python
import functools
import math

import jax
import jax.numpy as jnp
from jax import lax
from jax.experimental import pallas as pl
from jax.experimental.pallas import tpu as pltpu

F32 = jnp.float32
BF16 = jnp.bfloat16

EPS = 1e-6
D_MODEL = 1024
CHUNK = 128
HEAD_DIM = 128
A_HEADS = 4
D_A = 512
D_B = 512
B_CONV = 31
C_CONV = 3
D_FF = 2816
F_CONV = 3
N_CHIPS = 4

ADAM_LR = 0.001
ADAM_B1 = 0.9
ADAM_B2 = 0.999
ADAM_EPS = 1e-08
ADAM_WD = 0.01
ADAM_STEP = 10

SUBLANES = 8
LANES = 128
HALO_SHORT = 16
HALO_LONG = 32
VMEM_BYTES_MAX = 60000 * 1024

INV_SQRT2 = 1.0 / math.sqrt(2.0)
INV_SQRT_2PI = 1.0 / math.sqrt(2.0 * math.pi)

MESH = pl.DeviceIdType.MESH


def _cparams(sem, vmem_mb):
    del vmem_mb
    return pltpu.CompilerParams(dimension_semantics=sem, vmem_limit_bytes=VMEM_BYTES_MAX)


def _pick(total, pref):
    for c in (2048, 1024, 512, 256, 128):
        if c <= pref and total % c == 0:
            return c
    raise ValueError(f"no tile for {total}")


def _sigmoid(x):
    return jax.nn.sigmoid(x)


def _silu(x):
    return x * _sigmoid(x)


def _dsilu(x):
    s = _sigmoid(x)
    return s * (1.0 + x * (1.0 - s))


def _gelu(x):
    return 0.5 * x * (1.0 + lax.erf(x * INV_SQRT2))


def _dgelu(x):
    return 0.5 * (1.0 + lax.erf(x * INV_SQRT2)) + x * jnp.exp(-0.5 * x * x) * INV_SQRT_2PI


def _ln_stats(x):
    mu = jnp.mean(x, axis=-1, keepdims=True)
    xc = x - mu
    var = jnp.mean(xc * xc, axis=-1, keepdims=True)
    r = lax.rsqrt(var + EPS)
    return xc * r, r


def _ln_bwd(dy, xh, r, g):
    dxh = dy * g
    m1 = jnp.mean(dxh, axis=-1, keepdims=True)
    m2 = jnp.mean(dxh * xh, axis=-1, keepdims=True)
    return r * (dxh - m1 - xh * m2)


def _rowsum(x):
    return jnp.sum(x, axis=0, keepdims=True)


ANY = pl.BlockSpec(memory_space=pltpu.HBM)


def _place():
    x, y, c = lax.axis_index("x"), lax.axis_index("y"), lax.axis_index("c")
    peers = [(1 - x, y), (x, 1 - y), (1 - x, 1 - y)]
    return x, y, c, 2 * x + y, (x, y, 1 - c), peers


def _half(rows, which):
    return pl.ds(which * (rows // 2), rows // 2)


def _remote(src, dst, send_sem, recv_sem, device):
    return pltpu.make_async_remote_copy(src_ref=src, dst_ref=dst, send_sem=send_sem, recv_sem=recv_sem,
                                        device_id=device, device_id_type=MESH)


class _Job:
    def __init__(self, reads, writes, ncopies, copies):
        self.reads, self.writes, self.ncopies, self.copies = reads, writes, ncopies, copies


def _share(rows, which, part, parts):
    nr = rows // 2 // parts
    return pl.ds(which * (rows // 2) + part * nr, nr)


def _job_gather_ici(name, rows, part, parts):
    def copies(src, dst, sem):
        x, y, c, k, sib, peers = _place()
        mine_rows = _share(rows, c, part, parts)
        out = []
        for j, (px, py) in enumerate(peers):
            mine = src[name].at[0, k, mine_rows]
            out.append((_remote(mine, dst[name].at[0, k, mine_rows], sem(j, 0), sem(j, 1), (px, py, c)),
                        _remote(mine, dst[name].at[0, 2 * px + py, mine_rows], sem(j, 0), sem(j, 1), (px, py, c))))
        return out
    return _Job([], [name], 3, copies)


def _job_gather_d2d(name, rows, part, parts):
    def copies(src, dst, sem):
        x, y, c, k, sib, peers = _place()
        out = []
        for j, (px, py) in enumerate(peers):
            landed = src[name].at[0, 2 * px + py, _share(rows, c, part, parts)]
            out.append((_remote(landed, dst[name].at[0, 2 * px + py, _share(rows, c, part, parts)],
                                sem(j, 0), sem(j, 1), sib),
                        _remote(landed, dst[name].at[0, 2 * px + py, _share(rows, 1 - c, part, parts)],
                                sem(j, 0), sem(j, 1), sib)))
        return out
    return _Job([], [name], 3, copies)


def _job_pair_exchange(gname, tname, rows):
    def copies(src, dst, sem):
        x, y, c, k, sib, peers = _place()
        cp = _remote(src[gname].at[:, _half(rows, 1 - c), :], dst[tname], sem(0, 0), sem(0, 1), sib)
        return [(cp, cp)]
    return _Job([gname], [tname], 1, copies)


def _job_chip_exchange(pname, lname, r0, nr):
    def copies(src, dst, sem):
        x, y, c, k, sib, peers = _place()
        out = []
        for j, (px, py) in enumerate(peers):
            cp = _remote(src[pname].at[2 * px + py, pl.ds(r0, nr)], dst[lname].at[j, pl.ds(r0, nr)],
                         sem(j, 0), sem(j, 1), (px, py, c))
            out.append((cp, cp))
        return out
    return _Job([pname], [lname], 3, copies)


def _job_pair_share(name, layer, rows):
    def copies(src, dst, sem):
        x, y, c, k, sib, peers = _place()
        mine = src[name].at[layer, _half(rows, c)]
        return [(_remote(mine, dst[name].at[layer, _half(rows, c)], sem(0, 0), sem(0, 1), sib),
                 _remote(mine, dst[name].at[layer, _half(rows, 1 - c)], sem(0, 0), sem(0, 1), sib))]
    return _Job([], [name], 1, copies)


class _Comm:
    def __init__(self, plan, jobs):
        self.plan, self.jobs = plan, jobs
        self.writes, self.reads = [], []
        for job in jobs:
            for n in job.writes:
                if n not in self.writes:
                    self.writes.append(n)
        for job in jobs:
            for n in job.reads:
                if n not in self.writes and n not in self.reads:
                    self.reads.append(n)
        self.ncopies = sum(job.ncopies for job in jobs)

    def descriptors(self, src, dst, sems, base):
        out = []
        for job in self.jobs:
            sem = lambda j, which, base=base: sems.at[base + j, which]
            out += job.copies(src, dst, sem)
            base += job.ncopies
        return out

    def start(self, src, dst, sems, base=0):
        for first, _ in self.descriptors(src, dst, sems, base):
            first.start()

    def finish(self, src, dst, sems, base=0):
        for _, landed in self.descriptors(src, dst, sems, base):
            landed.wait()


def _comm_operands(comm):
    bufs = comm.plan.bufs
    shapes = [jax.ShapeDtypeStruct(bufs[n].shape, bufs[n].dtype) for n in comm.writes]
    return [bufs[n] for n in comm.reads] + [bufs[n] for n in comm.writes], shapes


def _pallas(comm, body, *, name, grid, in_specs, out_specs, out_shape, compiler_params, scratch_shapes=(),
            aliases=None):
    aliases = dict(aliases or {})
    if comm is None:
        return pl.pallas_call(body, name=name, grid=grid, in_specs=in_specs, out_specs=out_specs,
                              out_shape=out_shape, scratch_shapes=list(scratch_shapes),
                              input_output_aliases=aliases, compiler_params=compiler_params)
    single = not isinstance(out_shape, (list, tuple))
    base_specs = [out_specs] if single else list(out_specs)
    base_shape = [out_shape] if single else list(out_shape)
    nb, nr, nw, nbo, nsc = len(in_specs), len(comm.reads), len(comm.writes), len(base_specs), len(scratch_shapes)

    def wrapped(*refs):
        base_in, rd, wr_in = refs[:nb], refs[nb:nb + nr], refs[nb + nr:nb + nr + nw]
        o0 = nb + nr + nw
        base_out, wr_out = refs[o0:o0 + nbo], refs[o0 + nbo:o0 + nbo + nw]
        scratch, sems = refs[o0 + nbo + nw:o0 + nbo + nw + nsc], refs[-1]
        src = dict(zip(comm.reads, rd))
        src.update(zip(comm.writes, wr_in))
        dst = dict(zip(comm.writes, wr_out))
        first = functools.reduce(jnp.logical_and, [pl.program_id(a) == 0 for a in range(len(grid))])
        last = functools.reduce(jnp.logical_and,
                                [pl.program_id(a) == pl.num_programs(a) - 1 for a in range(len(grid))])

        @pl.when(first)
        def _():
            comm.start(src, dst, sems)
        body(*base_in, *base_out, *scratch)

        @pl.when(last)
        def _():
            comm.finish(src, dst, sems)

    operands, shapes = _comm_operands(comm)
    call = pl.pallas_call(
        wrapped, name=name, grid=grid, in_specs=list(in_specs) + [ANY] * (nr + nw),
        out_specs=base_specs + [ANY] * nw, out_shape=base_shape + shapes,
        input_output_aliases={**aliases, **{nb + nr + q: nbo + q for q in range(nw)}},
        scratch_shapes=list(scratch_shapes) + [pltpu.SemaphoreType.DMA((comm.ncopies, 2))],
        compiler_params=compiler_params)

    def run(*args):
        outs = call(*args, *operands)
        for q, n in enumerate(comm.writes):
            comm.plan.bufs[n] = outs[nbo + q]
        return outs[0] if single else list(outs[:nbo])

    return run


def _comm_only(plan, phases, *, name):
    comms = [_Comm(plan, jobs) for jobs in phases]
    both = _Comm(plan, [job for jobs in phases for job in jobs])
    nr, nw = len(both.reads), len(both.writes)

    def body(*refs):
        rd, wr_in, wr_out, sems = refs[:nr], refs[nr:nr + nw], refs[nr + nw:nr + 2 * nw], refs[-1]
        src = dict(zip(both.reads, rd))
        src.update(zip(both.writes, wr_in))
        dst = dict(zip(both.writes, wr_out))
        base = 0
        for comm in comms:
            comm.start(src, dst, sems, base)
            comm.finish(src, dst, sems, base)
            base += comm.ncopies

    operands, shapes = _comm_operands(both)
    outs = pl.pallas_call(
        body, name=name, in_specs=[ANY] * (nr + nw), out_specs=[ANY] * nw, out_shape=shapes,
        input_output_aliases={nr + q: q for q in range(nw)},
        scratch_shapes=[pltpu.SemaphoreType.DMA((both.ncopies, 2))],
    )(*operands)
    for q, n in enumerate(both.writes):
        plan.bufs[n] = outs[q]


def _mm_nn(a, w, *, layer, tm, tn, residual=None, norm=None, out_dtype=F32, name, comm=None):
    T, K = a.shape
    if w.ndim == 4:
        _, S, _, n4 = w.shape
        N = S * n4
        bps = n4 // tn
        w_spec = pl.BlockSpec((None, None, K, tn), lambda j, i: (layer, j // bps, 0, j % bps))
    else:
        N = w.shape[2]
        w_spec = pl.BlockSpec((None, K, tn), lambda j, i: (layer, 0, j))
    in_specs = [pl.BlockSpec((tm, K), lambda j, i: (i, 0)), w_spec]
    args = [a, w]
    if residual is not None:
        in_specs.append(pl.BlockSpec((tm, tn), lambda j, i: (i, j)))
        args.append(residual)
    out_specs = pl.BlockSpec((tm, tn), lambda j, i: (i, j))
    out_shape = jax.ShapeDtypeStruct((T, N), out_dtype)
    if norm is not None:
        assert tn == N
        g, norm_layer = norm
        in_specs.append(pl.BlockSpec((None, 1, N), lambda j, i: (norm_layer, 0, 0)))
        args.append(g)
        out_specs = [out_specs, pl.BlockSpec((tm, tn), lambda j, i: (i, j))]
        out_shape = [out_shape, jax.ShapeDtypeStruct((T, N), BF16)]

    def body(*refs):
        a_ref, w_ref = refs[0], refs[1]
        acc = jnp.dot(a_ref[...].astype(BF16), w_ref[...], preferred_element_type=F32)
        if residual is not None:
            acc = refs[2][...] + acc
        if norm is None:
            refs[-1][...] = acc.astype(out_dtype)
        else:
            refs[-2][...] = acc.astype(out_dtype)
            r = lax.rsqrt(jnp.mean(acc * acc, axis=-1, keepdims=True) + EPS)
            refs[-1][...] = (acc * r * refs[-3][...]).astype(BF16)

    return _pallas(
        comm, body, name=name, grid=(N // tn, T // tm), in_specs=in_specs,
        out_specs=out_specs, out_shape=out_shape,
        compiler_params=_cparams(("parallel", "parallel"), 48),
    )(*args)


def _mm_nt(dy, w, *, layer, tm, tn, name, out_dtype=F32, comm=None):
    T = dy.shape[0]
    nt_dims = (((1,), (1,)), ((), ()))
    _, R, N = w.shape

    def body2(dy_ref, w_ref, o_ref):
        o_ref[...] = lax.dot_general(dy_ref[...].astype(BF16), w_ref[...], nt_dims,
                                     preferred_element_type=F32).astype(out_dtype)

    return _pallas(
        comm, body2, name=name, grid=(R // tn, T // tm),
        in_specs=[pl.BlockSpec((tm, N), lambda j, i: (i, 0)),
                  pl.BlockSpec((None, tn, N), lambda j, i: (layer, j, 0))],
        out_specs=pl.BlockSpec((tm, tn), lambda j, i: (i, j)),
        out_shape=jax.ShapeDtypeStruct((T, R), out_dtype),
        compiler_params=_cparams(("parallel", "parallel"), 48),
    )(dy, w)


def _mm_tn(a, dy, *, shards, tk, tn, tt, name, comm=None):
    T, K = a.shape
    N = dy.shape[1]
    tn_dims = (((0,), (0,)), ((), ()))

    def body(a_ref, dy_ref, o_ref):
        @pl.when(pl.program_id(2) == 0)
        def _():
            o_ref[...] = jnp.zeros_like(o_ref)
        o_ref[...] += lax.dot_general(a_ref[...].astype(BF16), dy_ref[...].astype(BF16), tn_dims,
                                      preferred_element_type=F32)

    if shards is None:
        out_spec = pl.BlockSpec((tk, tn), lambda k, n, t: (k, n))
        out_shape = jax.ShapeDtypeStruct((K, N), F32)
    else:
        n4 = N // shards
        bps = n4 // tn
        out_spec = pl.BlockSpec((None, tk, tn), lambda k, n, t: (n // bps, k, n % bps))
        out_shape = jax.ShapeDtypeStruct((shards, K, n4), F32)
    return _pallas(
        comm, body, name=name, grid=(K // tk, N // tn, T // tt),
        in_specs=[pl.BlockSpec((tt, tk), lambda k, n, t: (t, k)),
                  pl.BlockSpec((tt, tn), lambda k, n, t: (t, n))],
        out_specs=out_spec, out_shape=out_shape,
        compiler_params=_cparams(("parallel", "parallel", "arbitrary"), 48),
    )(a, dy)


def _rmsnorm_fwd(x, g, *, layer, tm, name, comm=None):
    T, D = x.shape

    def body(x_ref, g_ref, h_ref):
        xf = x_ref[...]
        r = lax.rsqrt(jnp.mean(xf * xf, axis=-1, keepdims=True) + EPS)
        h_ref[...] = (xf * r * g_ref[...]).astype(BF16)

    return _pallas(
        comm, body, name=name, grid=(T // tm,),
        in_specs=[pl.BlockSpec((tm, D), lambda i: (i, 0)),
                  pl.BlockSpec((None, 1, D), lambda i: (layer, 0, 0))],
        out_specs=pl.BlockSpec((tm, D), lambda i: (i, 0)),
        out_shape=jax.ShapeDtypeStruct((T, D), BF16),
        compiler_params=_cparams(("parallel",), 32),
    )(x, g)


def _rmsnorm_bwd_math(xf, g, dh, dres):
    r = lax.rsqrt(jnp.mean(xf * xf, axis=-1, keepdims=True) + EPS)
    xh = xf * r
    dxh = dh * g
    dx = dres + r * (dxh - xh * jnp.mean(dxh * xh, axis=-1, keepdims=True))
    return dx, _rowsum(dh * xh)


def _mm_nt_norm(dy, w, x, g, dres, *, g_layer, tm, name, comm=None):
    T = dy.shape[0]
    _, S, K, n4 = w.shape
    nt_dims = (((1,), (1,)), ((), ()))

    def body(dy_ref, w_ref, x_ref, g_ref, dres_ref, dx_ref, dg_ref):
        @pl.when(pl.program_id(0) == 0)
        def _():
            dg_ref[...] = jnp.zeros_like(dg_ref)
        dh = None
        for s in range(S):
            part = lax.dot_general(dy_ref[:, s * n4:(s + 1) * n4].astype(BF16), w_ref[s], nt_dims,
                                   preferred_element_type=F32)
            dh = part if dh is None else dh + part
        dx, dg = _rmsnorm_bwd_math(x_ref[...], g_ref[...], dh, dres_ref[...])
        dx_ref[...] = dx
        dg_ref[...] += dg

    row = lambda i: (i, 0)
    return _pallas(
        comm, body, name=name, grid=(T // tm,),
        in_specs=[pl.BlockSpec((tm, S * n4), row),
                  pl.BlockSpec((None, S, K, n4), lambda i: (0, 0, 0, 0)),
                  pl.BlockSpec((tm, K), row),
                  pl.BlockSpec((None, 1, K), lambda i: (g_layer, 0, 0)),
                  pl.BlockSpec((tm, K), row)],
        out_specs=[pl.BlockSpec((tm, K), row), pl.BlockSpec((1, K), lambda i: (0, 0))],
        out_shape=[jax.ShapeDtypeStruct((T, K), F32), jax.ShapeDtypeStruct((1, K), F32)],
        compiler_params=_cparams(("arbitrary",), 56),
    )(dy, w, x, g, dres)


def _loss_head(x, tgt, g, *, tm, name, comm=None):
    T, D = x.shape

    def body(x_ref, t_ref, g_ref, loss_ref, dx_ref, dg_ref):
        @pl.when(pl.program_id(0) == 0)
        def _():
            dg_ref[...] = jnp.zeros_like(dg_ref)
            loss_ref[...] = jnp.zeros_like(loss_ref)
        xf = x_ref[...]
        gg = g_ref[...]
        r = lax.rsqrt(jnp.mean(xf * xf, axis=-1, keepdims=True) + EPS)
        xh = xf * r
        err = xh * gg - t_ref[...]
        row = jnp.mean(err * err, axis=-1, keepdims=True)
        loss_ref[...] += 0.5 * jnp.sum(row, axis=0, keepdims=True)
        dy = err * (1.0 / D)
        dg_ref[...] += _rowsum(dy * xh)
        dxh = dy * gg
        dx_ref[...] = r * (dxh - xh * jnp.mean(dxh * xh, axis=-1, keepdims=True))

    return _pallas(
        comm, body, name=name, grid=(T // tm,),
        in_specs=[pl.BlockSpec((tm, D), lambda i: (i, 0)),
                  pl.BlockSpec((tm, D), lambda i: (i, 0)),
                  pl.BlockSpec((1, D), lambda i: (0, 0))],
        out_specs=[pl.BlockSpec((1, 1), lambda i: (0, 0)),
                   pl.BlockSpec((tm, D), lambda i: (i, 0)),
                   pl.BlockSpec((1, D), lambda i: (0, 0))],
        out_shape=[jax.ShapeDtypeStruct((1, 1), F32), jax.ShapeDtypeStruct((T, D), F32),
                   jax.ShapeDtypeStruct((1, D), F32)],
        compiler_params=_cparams(("arbitrary",), 40),
    )(x, tgt, g)


CONV_ROWS = 64
CONV_COLS = 256


def _halo_prev_index(tm, halo):
    per = tm // halo
    return lambda i: jnp.maximum(i * per - 1, 0)


def _halo_next_index(tm, halo, total):
    per = tm // halo
    last = total // halo - 1
    return lambda i: jnp.minimum((i + 1) * per, last)


def _causal_mask():
    t = lax.broadcasted_iota(jnp.int32, (CHUNK, CHUNK), 0)
    s = lax.broadcasted_iota(jnp.int32, (CHUNK, CHUNK), 1)
    return s <= t


def _mixer_ab_fwd(z, a_ln_g, a_ln_b, w_s, b_s, conv_w, conv_b, b_ln_g, b_ln_b, *, tm, name, comm=None):
    T = z.shape[0]
    nchunk = tm // CHUNK
    halo = HALO_LONG

    def body(za_ref, zb_ref, zh_ref, alg_ref, alb_ref, ws_ref, bs_ref, cw_ref, cbias_ref,
             blg_ref, blb_ref, y_ref, cb_ref, ext_ref):
        i = pl.program_id(0)
        gu = _gelu(za_ref[:, :D_A].astype(F32))
        gv = _gelu(za_ref[:, D_A:].astype(F32))
        xh, _ = _ln_stats(gv)
        lv = (xh * alg_ref[...] + alb_ref[...]).astype(BF16)
        mask = _causal_mask()
        for h in range(A_HEADS):
            wm = jnp.where(mask, ws_ref[h], 0.0).astype(BF16)
            cols = slice(h * HEAD_DIM, (h + 1) * HEAD_DIM)
            for c in range(nchunk):
                rows = slice(c * CHUNK, (c + 1) * CHUNK)
                mixed = jnp.dot(wm, lv[rows, cols], preferred_element_type=F32) + bs_ref[h]
                y_ref[rows, cols] = (gu[rows, cols] * mixed).astype(BF16)
        ext_ref[halo:halo + tm, :] = zb_ref[:, :D_B].astype(F32) * _sigmoid(zb_ref[:, D_B:].astype(F32))
        prev = zh_ref[:, :D_B].astype(F32) * _sigmoid(zh_ref[:, D_B:].astype(F32))
        ext_ref[0:halo, :] = jnp.where(i > 0, prev, 0.0)
        for rb in range(tm // CONV_ROWS):
            for cb in range(D_B // CONV_COLS):
                cs = slice(cb * CONV_COLS, (cb + 1) * CONV_COLS)
                window = ext_ref[rb * CONV_ROWS:rb * CONV_ROWS + CONV_ROWS + halo, cs]
                acc = jnp.zeros((CONV_ROWS, CONV_COLS), F32)
                for k in range(B_CONV):
                    shifted = _rows_after(window, halo - (B_CONV - 1) + k)[:CONV_ROWS]
                    acc = acc + cw_ref[k:k + 1, cs] * shifted
                cb_ref[rb * CONV_ROWS:(rb + 1) * CONV_ROWS, cs] = acc + cbias_ref[:, cs]
        xhb, _ = _ln_stats(cb_ref[...])
        y_ref[:, D_A:] = _silu(xhb * blg_ref[...] + blb_ref[...]).astype(BF16)

    row = lambda i: (i, 0)
    par = lambda i: (0, 0)
    return _pallas(
        comm, body, name=name, grid=(T // tm,),
        in_specs=[pl.BlockSpec((tm, 2 * D_A), lambda i: (i, 0)),
                  pl.BlockSpec((tm, 2 * D_B), lambda i: (i, 1)),
                  pl.BlockSpec((halo, 2 * D_B), lambda i: (_halo_prev_index(tm, halo)(i), 1)),
                  pl.BlockSpec((1, D_A), par), pl.BlockSpec((1, D_A), par),
                  pl.BlockSpec((A_HEADS, CHUNK, CHUNK), lambda i: (0, 0, 0)),
                  pl.BlockSpec((A_HEADS, CHUNK, 1), lambda i: (0, 0, 0)),
                  pl.BlockSpec((B_CONV, D_B), par), pl.BlockSpec((1, D_B), par),
                  pl.BlockSpec((1, D_B), par), pl.BlockSpec((1, D_B), par)],
        out_specs=[pl.BlockSpec((tm, D_A + D_B), row), pl.BlockSpec((tm, D_B), row)],
        out_shape=[jax.ShapeDtypeStruct((T, D_A + D_B), BF16), jax.ShapeDtypeStruct((T, D_B), F32)],
        scratch_shapes=[pltpu.VMEM((halo + tm, D_B), F32)],
        compiler_params=_cparams(("parallel",), 40),
    )(z, z, z, a_ln_g, a_ln_b, w_s, b_s, conv_w, conv_b, b_ln_g, b_ln_b)


def _mixer_ab_bwd_pre(z, cb, dy, a_ln_g, a_ln_b, w_s, b_s, b_ln_g, b_ln_b, *, tm, name, comm=None):
    T = z.shape[0]
    nchunk = tm // CHUNK
    tn_dims = (((0,), (0,)), ((), ()))
    nt_dims = (((1,), (1,)), ((), ()))

    def body(za_ref, cb_ref, dy_ref, alg_ref, alb_ref, ws_ref, bs_ref, blg_ref, blb_ref,
             dza_ref, dcb_ref, dalg_ref, dalb_ref, dws_ref, dbs_ref, dblg_ref, dblb_ref,
             dlv_ref):
        @pl.when(pl.program_id(0) == 0)
        def _():
            for ref in (dalg_ref, dalb_ref, dws_ref, dbs_ref, dblg_ref, dblb_ref):
                ref[...] = jnp.zeros_like(ref)
        ua = za_ref[:, :D_A].astype(F32)
        va = za_ref[:, D_A:].astype(F32)
        gu = _gelu(ua)
        gv = _gelu(va)
        xh, r = _ln_stats(gv)
        alg = alg_ref[...]
        lv = (xh * alg + alb_ref[...]).astype(BF16)
        dya = dy_ref[:, :D_A].astype(F32)
        mask = _causal_mask()
        for h in range(A_HEADS):
            wm = jnp.where(mask, ws_ref[h], 0.0).astype(BF16)
            cols = slice(h * HEAD_DIM, (h + 1) * HEAD_DIM)
            dwm = jnp.zeros((CHUNK, CHUNK), F32)
            dbs = jnp.zeros((CHUNK, 1), F32)
            for c in range(nchunk):
                rows = slice(c * CHUNK, (c + 1) * CHUNK)
                lvb = lv[rows, cols]
                mixed = jnp.dot(wm, lvb, preferred_element_type=F32) + bs_ref[h]
                dyb = dya[rows, cols]
                dza_ref[rows, cols] = (dyb * mixed * _dgelu(ua[rows, cols])).astype(BF16)
                dmixed = dyb * gu[rows, cols]
                dmb = dmixed.astype(BF16)
                dlv_ref[rows, cols] = lax.dot_general(wm, dmb, tn_dims, preferred_element_type=F32)
                dwm = dwm + lax.dot_general(dmb, lvb, nt_dims, preferred_element_type=F32)
                dbs = dbs + jnp.sum(dmixed, axis=1, keepdims=True)
            dws_ref[h] += jnp.where(mask, dwm, 0.0)
            dbs_ref[h] += dbs
        dlv = dlv_ref[...]
        dalg_ref[...] += _rowsum(dlv * xh)
        dalb_ref[...] += _rowsum(dlv)
        dgv = _ln_bwd(dlv, xh, r, alg)
        dza_ref[:, D_A:] = (dgv * _dgelu(va)).astype(BF16)
        xhb, rb = _ln_stats(cb_ref[...])
        blg = blg_ref[...]
        lb = xhb * blg + blb_ref[...]
        dlb = dy_ref[:, D_A:].astype(F32) * _dsilu(lb)
        dblg_ref[...] += _rowsum(dlb * xhb)
        dblb_ref[...] += _rowsum(dlb)
        dcb_ref[...] = _ln_bwd(dlb, xhb, rb, blg)

    row = lambda i: (i, 0)
    par = lambda i: (0, 0)
    par3 = lambda i: (0, 0, 0)
    return _pallas(
        comm, body, name=name, grid=(T // tm,),
        in_specs=[pl.BlockSpec((tm, 2 * D_A), row), pl.BlockSpec((tm, D_B), row),
                  pl.BlockSpec((tm, D_A + D_B), row),
                  pl.BlockSpec((1, D_A), par), pl.BlockSpec((1, D_A), par),
                  pl.BlockSpec((A_HEADS, CHUNK, CHUNK), par3),
                  pl.BlockSpec((A_HEADS, CHUNK, 1), par3),
                  pl.BlockSpec((1, D_B), par), pl.BlockSpec((1, D_B), par)],
        out_specs=[pl.BlockSpec((tm, 2 * D_A), row), pl.BlockSpec((tm, D_B), row),
                   pl.BlockSpec((1, D_A), par), pl.BlockSpec((1, D_A), par),
                   pl.BlockSpec((A_HEADS, CHUNK, CHUNK), par3),
                   pl.BlockSpec((A_HEADS, CHUNK, 1), par3),
                   pl.BlockSpec((1, D_B), par), pl.BlockSpec((1, D_B), par)],
        out_shape=[jax.ShapeDtypeStruct((T, 2 * D_A + 2 * D_B), BF16), jax.ShapeDtypeStruct((T, D_B), F32),
                   jax.ShapeDtypeStruct((1, D_A), F32), jax.ShapeDtypeStruct((1, D_A), F32),
                   jax.ShapeDtypeStruct((A_HEADS, CHUNK, CHUNK), F32),
                   jax.ShapeDtypeStruct((A_HEADS, CHUNK, 1), F32),
                   jax.ShapeDtypeStruct((1, D_B), F32), jax.ShapeDtypeStruct((1, D_B), F32)],
        scratch_shapes=[pltpu.VMEM((tm, D_A), F32)],
        compiler_params=_cparams(("arbitrary",), 40),
    )(z, cb, dy, a_ln_g, a_ln_b, w_s, b_s, b_ln_g, b_ln_b)


def _mixer_b_conv_bwd(z, dcb, conv_w, dz, *, tm, name, comm=None):
    T = z.shape[0]
    halo = HALO_LONG

    def body(zb_ref, dcb_ref, dcn_ref, cw_ref, dz_in_ref, dzb_ref, dcw_ref, dbias_ref, dext_ref):
        i = pl.program_id(0)
        last = pl.num_programs(0) - 1

        @pl.when(i == 0)
        def _():
            dcw_ref[...] = jnp.zeros_like(dcw_ref)
            dbias_ref[...] = jnp.zeros_like(dbias_ref)
        dcb = dcb_ref[...]
        dext_ref[0:tm, :] = dcb
        dext_ref[tm:tm + halo, :] = jnp.where(i < last, dcn_ref[...], 0.0)
        dbias_ref[...] += _rowsum(dcb)
        for rb in range(tm // CONV_ROWS):
            for cb in range(D_B // CONV_COLS):
                cs = slice(cb * CONV_COLS, (cb + 1) * CONV_COLS)
                gcs = slice(D_B + cb * CONV_COLS, D_B + (cb + 1) * CONV_COLS)
                rs = slice(rb * CONV_ROWS, (rb + 1) * CONV_ROWS)
                xbb = zb_ref[rs, cs].astype(F32)
                sgb = _sigmoid(zb_ref[rs, gcs].astype(F32))
                yb0 = xbb * sgb
                window = dext_ref[rb * CONV_ROWS:rb * CONV_ROWS + CONV_ROWS + halo, cs]
                acc = jnp.zeros((CONV_ROWS, CONV_COLS), F32)
                for k in range(B_CONV):
                    shifted = _rows_after(window, (B_CONV - 1) - k)[:CONV_ROWS]
                    acc = acc + cw_ref[k:k + 1, cs] * shifted
                    dcw_ref[k:k + 1, cs] += _rowsum(shifted * yb0)
                dzb_ref[rs, cs] = (acc * sgb).astype(BF16)
                dzb_ref[rs, gcs] = (acc * xbb * sgb * (1.0 - sgb)).astype(BF16)

    row = lambda i: (i, 0)
    par = lambda i: (0, 0)
    return _pallas(
        comm, body, name=name, grid=(T // tm,),
        in_specs=[pl.BlockSpec((tm, 2 * D_B), lambda i: (i, 1)),
                  pl.BlockSpec((tm, D_B), row),
                  pl.BlockSpec((halo, D_B), lambda i: (_halo_next_index(tm, halo, T)(i), 0)),
                  pl.BlockSpec((B_CONV, D_B), par), pl.BlockSpec(memory_space=pl.ANY)],
        out_specs=[pl.BlockSpec((tm, 2 * D_B), lambda i: (i, 1)), pl.BlockSpec((B_CONV, D_B), par),
                   pl.BlockSpec((1, D_B), par)],
        out_shape=[jax.ShapeDtypeStruct(dz.shape, BF16), jax.ShapeDtypeStruct((B_CONV, D_B), F32),
                   jax.ShapeDtypeStruct((1, D_B), F32)],
        scratch_shapes=[pltpu.VMEM((tm + halo, D_B), F32)], aliases={4: 0},
        compiler_params=_cparams(("arbitrary",), 40),
    )(z, dcb, dcb, conv_w, dz)


def _rows_before(x, a):
    return x if a == 0 else pltpu.roll(x, a, axis=0)


def _rows_after(x, a):
    return x if a == 0 else pltpu.roll(x, x.shape[0] - a, axis=0)


def _conv3(w_ref, x, halo, cs):
    acc = w_ref[2:3, cs] * x[halo:]
    acc = acc + w_ref[1:2, cs] * _rows_before(x, 1)[halo:]
    return acc + w_ref[0:1, cs] * _rows_before(x, 2)[halo:]


def _mixer_c_fwd(z, conv_w, *, tm, name, comm=None):
    T = z.shape[0]
    D = D_MODEL
    halo = HALO_SHORT
    W = CONV_COLS

    def body(bg_ref, cg_ref, xv_ref, cgh_ref, xvh_ref, w_ref, r_ref):
        i = pl.program_id(0)
        for cb in range(D // W):
            cs = slice(cb * W, (cb + 1) * W)
            prev = jnp.where(i > 0, cgh_ref[:, cs].astype(F32) * xvh_ref[:, cs].astype(F32), 0.0)
            p = jnp.concatenate([prev, cg_ref[:, cs].astype(F32) * xv_ref[:, cs].astype(F32)], axis=0)
            r_ref[:, cs] = (bg_ref[:, cs].astype(F32) * _conv3(w_ref, p, halo, cs)).astype(BF16)

    hp = _halo_prev_index(tm, halo)
    return _pallas(
        comm, body, name=name, grid=(T // tm,),
        in_specs=[pl.BlockSpec((tm, D), lambda i: (i, 0)), pl.BlockSpec((tm, D), lambda i: (i, 1)),
                  pl.BlockSpec((tm, D), lambda i: (i, 2)),
                  pl.BlockSpec((halo, D), lambda i: (hp(i), 1)),
                  pl.BlockSpec((halo, D), lambda i: (hp(i), 2)),
                  pl.BlockSpec((None, C_CONV, D), lambda i: (0, 0, 0))],
        out_specs=pl.BlockSpec((tm, D), lambda i: (i, 0)),
        out_shape=jax.ShapeDtypeStruct((T, D), BF16),
        compiler_params=_cparams(("parallel",), 40),
    )(z, z, z, z, z, conv_w)


def _mixer_c_bwd(z, dr, conv_w, *, tm, name, comm=None):
    T = z.shape[0]
    D = D_MODEL
    halo = HALO_SHORT
    W = CONV_COLS

    def body(bg_ref, cg_ref, xv_ref, cgh_ref, xvh_ref, bgn_ref, dr_ref, drn_ref, w_ref, dz_ref, dw_ref):
        i = pl.program_id(0)
        last = pl.num_programs(0) - 1

        @pl.when(i == 0)
        def _():
            dw_ref[...] = jnp.zeros_like(dw_ref)
        for cb in range(D // W):
            cs = slice(cb * W, (cb + 1) * W)
            cg = cg_ref[:, cs].astype(F32)
            xv = xv_ref[:, cs].astype(F32)
            dr = dr_ref[:, cs].astype(F32)
            p = cg * xv
            prev = jnp.where(i > 0, cgh_ref[:, cs].astype(F32) * xvh_ref[:, cs].astype(F32), 0.0)
            q = _conv3(w_ref, jnp.concatenate([prev, p], axis=0), halo, cs)
            dz_ref[:, cs] = (dr * q).astype(BF16)
            nxt = jnp.where(i < last, drn_ref[:, cs].astype(F32) * bgn_ref[:, cs].astype(F32), 0.0)
            dq = jnp.concatenate([dr * bg_ref[:, cs].astype(F32), nxt], axis=0)
            dp = None
            for k in range(C_CONV):
                shifted = _rows_after(dq, 2 - k)[:tm]
                term = w_ref[k:k + 1, cs] * shifted
                dp = term if dp is None else dp + term
                dw_ref[k:k + 1, cs] += _rowsum(shifted * p)
            dz_ref[:, D + cb * W:D + (cb + 1) * W] = (dp * xv).astype(BF16)
            dz_ref[:, 2 * D + cb * W:2 * D + (cb + 1) * W] = (dp * cg).astype(BF16)

    hp = _halo_prev_index(tm, halo)
    hn = _halo_next_index(tm, halo, T)
    return _pallas(
        comm, body, name=name, grid=(T // tm,),
        in_specs=[pl.BlockSpec((tm, D), lambda i: (i, 0)), pl.BlockSpec((tm, D), lambda i: (i, 1)),
                  pl.BlockSpec((tm, D), lambda i: (i, 2)),
                  pl.BlockSpec((halo, D), lambda i: (hp(i), 1)),
                  pl.BlockSpec((halo, D), lambda i: (hp(i), 2)),
                  pl.BlockSpec((halo, D), lambda i: (hn(i), 0)),
                  pl.BlockSpec((tm, D), lambda i: (i, 0)),
                  pl.BlockSpec((halo, D), lambda i: (hn(i), 0)),
                  pl.BlockSpec((None, C_CONV, D), lambda i: (0, 0, 0))],
        out_specs=[pl.BlockSpec((tm, 3 * D), lambda i: (i, 0)),
                   pl.BlockSpec((C_CONV, D), lambda i: (0, 0))],
        out_shape=[jax.ShapeDtypeStruct((T, 3 * D), BF16), jax.ShapeDtypeStruct((C_CONV, D), F32)],
        compiler_params=_cparams(("arbitrary",), 48),
    )(z, z, z, z, z, z, dr, dr, conv_w)


FFN_COLS = 128


def _ffn_act_fwd(up, conv_w, *, layer, tm, name, comm=None):
    T = up.shape[0]
    halo = HALO_SHORT
    W = FFN_COLS

    def body(up_ref, uph_ref, w_ref, a_ref, upc_ref):
        i = pl.program_id(0)

        def conv(cs):
            prev = jnp.where(i > 0, uph_ref[:, cs], jnp.zeros((halo, W), BF16))
            return _conv3(w_ref, jnp.concatenate([prev, up_ref[:, cs]], axis=0).astype(F32), halo, cs)

        for cb in range(D_FF // W):
            gs = slice(cb * W, (cb + 1) * W)
            vs = slice(D_FF + cb * W, D_FF + (cb + 1) * W)
            g = conv(gs)
            v = conv(vs)
            upc_ref[:, gs] = g.astype(BF16)
            upc_ref[:, vs] = v.astype(BF16)
            a_ref[:, gs] = (_silu(g) * v).astype(BF16)

    return _pallas(
        comm, body, name=name, grid=(T // tm,),
        in_specs=[pl.BlockSpec((tm, 2 * D_FF), lambda i: (i, 0)),
                  pl.BlockSpec((halo, 2 * D_FF), lambda i: (_halo_prev_index(tm, halo)(i), 0)),
                  pl.BlockSpec((None, F_CONV, 2 * D_FF), lambda i: (layer, 0, 0))],
        out_specs=[pl.BlockSpec((tm, D_FF), lambda i: (i, 0)),
                   pl.BlockSpec((tm, 2 * D_FF), lambda i: (i, 0))],
        out_shape=[jax.ShapeDtypeStruct((T, D_FF), BF16), jax.ShapeDtypeStruct((T, 2 * D_FF), BF16)],
        compiler_params=_cparams(("parallel",), 48),
    )(up, up, conv_w)


def _ffn_act_bwd(up, upc, da, conv_w, *, layer, tm, name, comm=None):
    T = up.shape[0]
    halo = HALO_SHORT
    W = FFN_COLS

    def body(up_ref, upc_ref, upcn_ref, da_ref, dan_ref, w_ref, dup_ref, dw_ref):
        i = pl.program_id(0)
        last = pl.num_programs(0) - 1

        @pl.when(i == 0)
        def _():
            dw_ref[...] = jnp.zeros_like(dw_ref)
        live = jnp.where(i < last, 1.0, 0.0)
        for cb in range(D_FF // W):
            gs = slice(cb * W, (cb + 1) * W)
            vs = slice(D_FF + cb * W, D_FF + (cb + 1) * W)
            g = jnp.concatenate([upc_ref[:, gs], upcn_ref[:, gs]], axis=0).astype(F32)
            v = jnp.concatenate([upc_ref[:, vs], upcn_ref[:, vs]], axis=0).astype(F32)
            da = jnp.concatenate([da_ref[:, gs].astype(F32), dan_ref[:, gs].astype(F32) * live], axis=0)
            s = _sigmoid(g)
            silu = g * s
            grads = (da * v * (s * (1.0 + g * (1.0 - s))), da * silu)
            for cs, d in zip((gs, vs), grads):
                u = up_ref[:, cs].astype(F32)
                acc = None
                for k in range(F_CONV):
                    shifted = _rows_after(d, 2 - k)[:tm]
                    term = w_ref[k:k + 1, cs] * shifted
                    acc = term if acc is None else acc + term
                    dw_ref[k:k + 1, cs] += _rowsum(shifted * u)
                dup_ref[:, cs] = acc.astype(BF16)

    hn = _halo_next_index(tm, halo, T)
    return _pallas(
        comm, body, name=name, grid=(T // tm,),
        in_specs=[pl.BlockSpec((tm, 2 * D_FF), lambda i: (i, 0)),
                  pl.BlockSpec((tm, 2 * D_FF), lambda i: (i, 0)),
                  pl.BlockSpec((halo, 2 * D_FF), lambda i: (hn(i), 0)),
                  pl.BlockSpec((tm, D_FF), lambda i: (i, 0)),
                  pl.BlockSpec((halo, D_FF), lambda i: (hn(i), 0)),
                  pl.BlockSpec((None, F_CONV, 2 * D_FF), lambda i: (layer, 0, 0))],
        out_specs=[pl.BlockSpec((tm, 2 * D_FF), lambda i: (i, 0)),
                   pl.BlockSpec((F_CONV, 2 * D_FF), lambda i: (0, 0))],
        out_shape=[jax.ShapeDtypeStruct((T, 2 * D_FF), BF16),
                   jax.ShapeDtypeStruct((F_CONV, 2 * D_FF), F32)],
        compiler_params=_cparams(("arbitrary",), 56),
    )(up, upc, upc, da, da, conv_w)


def _local_step(x, tgt, small, plan):
    T = x.shape[0]
    tm_e = _pick(T, 256)
    tm_n = _pick(T, 512)
    tm = _pick(T, 1024)
    tm_f = _pick(T, 512)
    tt = _pick(T, 2048)
    nm = small["norm_mix"].reshape(2, 1, D_MODEL)
    nf = small["norm_ffn"].reshape(2, 1, D_MODEL)
    ngf = small["norm_final"].reshape(1, D_MODEL)
    b_s = small["a_b_s"].reshape(A_HEADS, CHUNK, 1)
    w_s = small["a_w_s"].reshape(A_HEADS, CHUNK, CHUNK)
    b_conv_w = small["b_conv_w"].reshape(B_CONV, D_B)
    sg = {}
    wt, cm = plan.weight, plan.comm

    h_m0 = _rmsnorm_fwd(x, nm, layer=0, tm=tm_n, name="norm_mix0")
    z_ab = _mm_nn(h_m0, wt("ab_w_in", 0), layer=0, tm=tm, tn=512, out_dtype=BF16, name="ab_in", comm=cm("ab_in"))
    yab, cb = _mixer_ab_fwd(z_ab, small["a_ln_g"], small["a_ln_b"], w_s, b_s, b_conv_w, small["b_conv_b"],
                            small["b_ln_g"], small["b_ln_b"], tm=tm_e, name="mixer_ab", comm=cm("mixer_ab"))
    x1, h_f0 = _mm_nn(yab, wt("ab_w_out", 0), layer=0, tm=tm, tn=D_MODEL, residual=x, norm=(nf, 0),
                      name="ab_out", comm=cm("ab_out"))

    def ffn_fwd(xin, h, layer, norm):
        up = _mm_nn(h, wt("f_w_up", layer), layer=0, tm=tm, tn=1408, out_dtype=BF16, name=f"ffn_up{layer}",
                    comm=cm(f"ffn_up{layer}"))
        a, upc = _ffn_act_fwd(up, small["f_conv_w"], layer=layer, tm=tm_e, name=f"ffn_act{layer}",
                              comm=cm(f"ffn_act{layer}"))
        out = _mm_nn(a, wt("f_w_down", layer), layer=0, tm=tm, tn=D_MODEL, residual=xin, norm=norm,
                     name=f"ffn_down{layer}", comm=cm(f"ffn_down{layer}"))
        return up, upc, a, out

    up0, upc0, a0, (x2, h_m1) = ffn_fwd(x1, h_f0, 0, (nm, 1))
    z_c = _mm_nn(h_m1, wt("c_w_in", 0), layer=0, tm=tm, tn=768, out_dtype=BF16, name="c_in", comm=cm("c_in"))
    r = _mixer_c_fwd(z_c, small["c_conv_w"], tm=tm_e, name="mixer_c", comm=cm("mixer_c"))
    x3, h_f1 = _mm_nn(r, wt("c_w_out", 0), layer=0, tm=tm, tn=D_MODEL, residual=x2, norm=(nf, 1),
                      name="c_out", comm=cm("c_out"))
    up1, upc1, a1, x4 = ffn_fwd(x3, h_f1, 1, None)
    loss, dx, sg["norm_final"] = _loss_head(x4, tgt, ngf, tm=tm_n, name="loss_head")

    def ffn_bwd(dx, xin, h, up, upc, a, layer):
        da = _mm_nt(dx, wt("f_w_down", layer), layer=0, tm=tm, tn=1408, out_dtype=BF16,
                    name=f"ffn_down_dx{layer}", comm=cm(f"ffn_down_dx{layer}"))
        plan.grad_ready("f_w_down", layer, _mm_tn(a, dx, shards=None, tk=1408, tn=1024, tt=tt,
                                                  name=f"ffn_down_dw{layer}", comm=cm(f"ffn_down_dw{layer}")))
        dup, dcw = _ffn_act_bwd(up, upc, da, small["f_conv_w"], layer=layer, tm=tm_e, name=f"ffn_act_bwd{layer}",
                                comm=cm(f"ffn_act_bwd{layer}"))
        dxin, dg = _mm_nt_norm(dup, wt("f_w_up", layer), xin, nf, dx, g_layer=layer, tm=tm_f,
                               name=f"ffn_up_dx{layer}", comm=cm(f"ffn_up_dx{layer}"))
        plan.grad_ready("f_w_up", layer, _mm_tn(h, dup, shards=N_CHIPS, tk=1024, tn=1408, tt=tt,
                                                name=f"ffn_up_dw{layer}", comm=cm(f"ffn_up_dw{layer}")))
        return dxin, dg, dcw

    dx, dnf1, dfc1 = ffn_bwd(dx, x3, h_f1, up1, upc1, a1, 1)
    dr = _mm_nt(dx, wt("c_w_out", 0), layer=0, tm=tm, tn=512, out_dtype=BF16, name="c_out_dx", comm=cm("c_out_dx"))
    plan.grad_ready("c_w_out", 0, _mm_tn(r, dx, shards=None, tk=1024, tn=1024, tt=tt, name="c_out_dw",
                                         comm=cm("c_out_dw")))
    dz_c, dccw = _mixer_c_bwd(z_c, dr, small["c_conv_w"], tm=tm_e, name="mixer_c_bwd", comm=cm("mixer_c_bwd"))
    sg["c_conv_w"] = dccw.reshape(1, C_CONV, D_MODEL)
    plan.grad_ready("c_w_in", 0, _mm_tn(h_m1, dz_c, shards=N_CHIPS, tk=1024, tn=768, tt=tt, name="c_in_dw",
                                        comm=cm("c_in_dw")))
    dx, dnm1 = _mm_nt_norm(dz_c, wt("c_w_in", 0), x2, nm, dx, g_layer=1, tm=tm_f, name="c_in_dx",
                           comm=cm("c_in_dx"))
    dx, dnf0, dfc0 = ffn_bwd(dx, x1, h_f0, up0, upc0, a0, 0)
    dyab = _mm_nt(dx, wt("ab_w_out", 0), layer=0, tm=tm, tn=512, out_dtype=BF16, name="ab_out_dx",
                  comm=cm("ab_out_dx"))
    plan.grad_ready("ab_w_out", 0, _mm_tn(yab, dx, shards=None, tk=1024, tn=1024, tt=tt, name="ab_out_dw",
                                          comm=cm("ab_out_dw")))
    (dza, dcb, sg["a_ln_g"], sg["a_ln_b"], dws, dbs, sg["b_ln_g"], sg["b_ln_b"]) = _mixer_ab_bwd_pre(
        z_ab, cb, dyab, small["a_ln_g"], small["a_ln_b"], w_s, b_s, small["b_ln_g"], small["b_ln_b"],
        tm=tm_e, name="mixer_ab_bwd", comm=cm("mixer_ab_bwd"))
    dz_ab, dbcw, sg["b_conv_b"] = _mixer_b_conv_bwd(z_ab, dcb, b_conv_w, dza, tm=tm_e, name="mixer_b_conv_bwd",
                                                    comm=cm("mixer_b_conv_bwd"))
    sg["a_w_s"] = dws.reshape(1, A_HEADS, CHUNK, CHUNK)
    sg["a_b_s"] = dbs.reshape(1, A_HEADS, CHUNK)
    sg["b_conv_w"] = dbcw.reshape(1, B_CONV, D_B)
    plan.grad_ready("ab_w_in", 0, _mm_tn(h_m0, dz_ab, shards=N_CHIPS, tk=1024, tn=512, tt=tt, name="ab_in_dw",
                                         comm=cm("ab_in_dw")))
    dx, dnm0 = _mm_nt_norm(dz_ab, wt("ab_w_in", 0), x, nm, dx, g_layer=0, tm=tm_f, name="ab_in_dx",
                           comm=cm("ab_in_dx"))

    sg["norm_mix"] = [dnm0, dnm1]
    sg["norm_ffn"] = [dnf0, dnf1]
    sg["f_conv_w"] = [dfc0, dfc1]
    return loss, dx, sg


BLOCK_BYTES = 3 * 1024 * 1024


BF16_SUBLANES = 16


def _row_tile(rows, row_bytes, step=SUBLANES):
    best = None
    for tr in range(step, rows + 1, step):
        if rows % tr == 0 and tr * row_bytes <= BLOCK_BYTES:
            best = tr
    if best is None:
        raise ValueError(f"no row tile for {rows}")
    return best


def _place_scalars():
    x, y, c = lax.axis_index("x"), lax.axis_index("y"), lax.axis_index("c")
    return jnp.stack([c, 2 * x + y, 2 * (1 - x) + y, 2 * x + (1 - y), 2 * (1 - x) + (1 - y)]).astype(jnp.int32)


def _cast_into_slot(w, place, *, layer, name):
    L, rows, cols = w.shape
    tr = _row_tile(rows, cols * 4, BF16_SUBLANES)

    def body(place_ref, w_ref, o_ref):
        o_ref[...] = w_ref[...].astype(BF16)

    return pl.pallas_call(
        body, name=name,
        grid_spec=pltpu.PrefetchScalarGridSpec(
            num_scalar_prefetch=1, grid=(rows // tr,),
            in_specs=[pl.BlockSpec((None, tr, cols), lambda i, p: (layer, i, 0))],
            out_specs=pl.BlockSpec((None, None, tr, cols), lambda i, p: (0, p[1], i, 0))),
        out_shape=jax.ShapeDtypeStruct((1, N_CHIPS, rows, cols), BF16),
        compiler_params=_cparams(("parallel",), 32),
    )(place, w)


def _pair_sum(g, theirs, place, *, name):
    S, rows, cols = g.shape
    half = rows // 2
    tr = _row_tile(half, cols * 4, BF16_SUBLANES)
    nb = half // tr

    def body(place_ref, g_ref, t_ref, o_ref):
        o_ref[...] = (g_ref[...] + t_ref[...]).astype(BF16)

    spec = pl.BlockSpec((None, tr, cols), lambda s, i, p: (s, i, 0))
    return pl.pallas_call(
        body, name=name,
        grid_spec=pltpu.PrefetchScalarGridSpec(
            num_scalar_prefetch=1, grid=(S, nb),
            in_specs=[pl.BlockSpec((None, tr, cols), lambda s, i, p: (s, p[0] * nb + i, 0)), spec],
            out_specs=spec),
        out_shape=jax.ShapeDtypeStruct((S, half, cols), BF16),
        compiler_params=_cparams(("parallel", "parallel"), 32),
    )(place, g, theirs)


def _chip_sum(p, r, g_prev, place, *, layer, shape, name):
    L, rows, cols = shape
    half = rows // 2
    tr = _row_tile(half, cols * 4, BF16_SUBLANES)
    nb = half // tr

    def body(place_ref, p_ref, r_ref, *rest):
        o_ref = rest[-1]
        mine = p_ref[...].astype(F32)
        peers = [r_ref[j].astype(F32) for j in range(3)]
        acc = None
        for s in range(N_CHIPS):
            term = jnp.where(place_ref[1] == s, mine,
                             jnp.where(place_ref[2] == s, peers[0],
                                       jnp.where(place_ref[3] == s, peers[1], peers[2])))
            acc = term if acc is None else acc + term
        o_ref[...] = acc

    in_specs = [pl.BlockSpec((None, tr, cols), lambda i, pr: (pr[1], i, 0)),
                pl.BlockSpec((3, tr, cols), lambda i, pr: (0, i, 0))]
    args = [place, p, r]
    aliases = {}
    if g_prev is not None:
        in_specs.append(ANY)
        args.append(g_prev)
        aliases = {3: 0}
    return pl.pallas_call(
        body, name=name,
        grid_spec=pltpu.PrefetchScalarGridSpec(
            num_scalar_prefetch=1, grid=(nb,), in_specs=in_specs,
            out_specs=pl.BlockSpec((None, tr, cols), lambda i, pr: (layer, pr[0] * nb + i, 0))),
        out_shape=jax.ShapeDtypeStruct(shape, F32), input_output_aliases=aliases,
        compiler_params=_cparams(("parallel",), 32),
    )(*args)


def _adamw_math(w, g, m, v):
    m2 = ADAM_B1 * m + (1.0 - ADAM_B1) * g
    v2 = ADAM_B2 * v + (1.0 - ADAM_B2) * (g * g)
    m_hat = m2 / (1.0 - ADAM_B1 ** ADAM_STEP)
    v_hat = v2 / (1.0 - ADAM_B2 ** ADAM_STEP)
    delta = -ADAM_LR * (m_hat / (jnp.sqrt(v_hat) + ADAM_EPS) + ADAM_WD * w)
    return delta, m2, v2


def _adamw(w, g, m, v, *, name):
    L, rows, cols = w.shape
    tr = _row_tile(rows, cols * 4)

    def body(w_ref, g_ref, m_ref, v_ref, d_ref, m2_ref, v2_ref):
        d, m2, v2 = _adamw_math(w_ref[...], g_ref[...], m_ref[...], v_ref[...])
        d_ref[...] = d
        m2_ref[...] = m2
        v2_ref[...] = v2

    spec = pl.BlockSpec((None, tr, cols), lambda l, i: (l, i, 0))
    shape = jax.ShapeDtypeStruct(w.shape, F32)
    return pl.pallas_call(
        body, name=name, grid=(L, rows // tr), in_specs=[spec] * 4, out_specs=[spec] * 3,
        out_shape=[shape] * 3,
        compiler_params=_cparams(("parallel", "parallel"), 48),
    )(w, g, m, v)


def _gather_packs(pack, *, name):
    R = pack.shape[0]
    ndev = 2 * N_CHIPS

    def body(p_ref, buf, send, recv):
        x, y, c = lax.axis_index("x"), lax.axis_index("y"), lax.axis_index("c")
        me = 4 * x + 2 * y + c
        buf[me] = p_ref[...]
        sends = []
        for q in range(1, ndev):
            qx, qy, qc = (q >> 2) & 1, (q >> 1) & 1, q & 1
            peer = (x ^ qx, y ^ qy, c ^ qc)
            rc = _remote(p_ref, buf.at[me], send.at[q - 1], recv.at[q - 1], peer)
            rc.start()
            sends.append(rc)
        for q in range(1, ndev):
            qx, qy, qc = (q >> 2) & 1, (q >> 1) & 1, q & 1
            slot = buf.at[4 * (x ^ qx) + 2 * (y ^ qy) + (c ^ qc)]
            _remote(slot, slot, send.at[q - 1], recv.at[q - 1], (x ^ qx, y ^ qy, c ^ qc)).wait_recv()
        for rc in sends:
            rc.wait_send()

    vm = pl.BlockSpec(memory_space=pltpu.VMEM)
    return pl.pallas_call(
        body, name=name, in_specs=[vm], out_specs=vm, out_shape=jax.ShapeDtypeStruct((ndev, R, LANES), F32),
        scratch_shapes=[pltpu.SemaphoreType.DMA((ndev - 1,)), pltpu.SemaphoreType.DMA((ndev - 1,))],
        compiler_params=pltpu.CompilerParams(vmem_limit_bytes=VMEM_BYTES_MAX),
    )(pack)


def _allreduce_pack(pack, *, name, comm):
    R = pack.shape[0]
    half = R // 2
    nr, nw = len(comm.reads), len(comm.writes)

    def body(*refs):
        p_ref, rd, wr_in = refs[0], refs[1:1 + nr], refs[1 + nr:1 + nr + nw]
        o_ref, wr_out = refs[1 + nr + nw], refs[2 + nr + nw:2 + nr + 2 * nw]
        sib_ref, chip_ref, parts_ref, sems, comm_sems = refs[2 + nr + 2 * nw:]
        src = dict(zip(comm.reads, rd))
        src.update(zip(comm.writes, wr_in))
        dst = dict(zip(comm.writes, wr_out))
        comm.start(src, dst, comm_sems)
        x, y, c, k, sib, peers = _place()
        swap = _remote(p_ref, sib_ref, sems.at[0, 0], sems.at[0, 1], sib)
        swap.start()
        swap.wait()
        chip_ref[...] = p_ref[...] + sib_ref[...]
        mine = chip_ref.at[pl.ds(pl.multiple_of(c * half, SUBLANES), half)]
        sends = [_remote(mine, parts_ref.at[j], sems.at[1 + j, 0], sems.at[1 + j, 1], (px, py, c))
                 for j, (px, py) in enumerate(peers)]
        for rc in sends:
            rc.start()
        for rc in sends:
            rc.wait()
        own = mine[...]
        others = [parts_ref[j] for j in range(3)]
        acc = None
        for s in range(N_CHIPS):
            term = own
            for j, (px, py) in enumerate(peers):
                term = jnp.where(2 * px + py == s, others[j], term)
            acc = term if acc is None else acc + term
        done = o_ref.at[pl.ds(pl.multiple_of(c * half, SUBLANES), half)]
        done[...] = acc
        theirs = o_ref.at[pl.ds(pl.multiple_of((1 - c) * half, SUBLANES), half)]
        share = _remote(done, done, sems.at[4, 0], sems.at[4, 1], sib)
        share.start()
        _remote(done, theirs, sems.at[4, 0], sems.at[4, 1], sib).wait()
        comm.finish(src, dst, comm_sems)

    vm = pl.BlockSpec(memory_space=pltpu.VMEM)
    operands, shapes = _comm_operands(comm)
    outs = pl.pallas_call(
        body, name=name, in_specs=[vm] + [ANY] * (nr + nw), out_specs=[vm] + [ANY] * nw,
        out_shape=[jax.ShapeDtypeStruct((R, LANES), F32)] + shapes,
        input_output_aliases={1 + nr + q: 1 + q for q in range(nw)},
        scratch_shapes=[pltpu.VMEM((R, LANES), F32), pltpu.VMEM((R, LANES), F32),
                        pltpu.VMEM((3, half, LANES), F32), pltpu.SemaphoreType.DMA((5, 2)),
                        pltpu.SemaphoreType.DMA((comm.ncopies, 2))],
        compiler_params=pltpu.CompilerParams(vmem_limit_bytes=VMEM_BYTES_MAX),
    )(pack, *operands)
    for q, n in enumerate(comm.writes):
        comm.plan.bufs[n] = outs[1 + q]
    return outs[0]


PACK_UNIT = SUBLANES * LANES


def _pack(arrays):
    flat, sizes = [], []
    for a in arrays:
        pieces = a if isinstance(a, (list, tuple)) else [a]
        v = jnp.concatenate([p.reshape(-1) for p in pieces]) if len(pieces) > 1 else pieces[0].reshape(-1)
        size = v.shape[0]
        padded = -(-size // PACK_UNIT) * PACK_UNIT
        flat.append(jnp.pad(v, (0, padded - size)))
        sizes.append((size, padded))
    total = sum(p for _, p in sizes)
    if (total // PACK_UNIT) % 2:
        flat.append(jnp.zeros((PACK_UNIT,), F32))
    return jnp.concatenate(flat).reshape(-1, LANES), sizes


def _unpack(pack, sizes, shapes):
    v = pack.reshape(-1)
    out, off = [], 0
    for (size, padded), shape in zip(sizes, shapes):
        out.append(v[off:off + size].reshape(shape))
        off += padded
    return out


BIG = ("ab_w_in", "ab_w_out", "c_w_in", "c_w_out", "f_w_up", "f_w_down")
COL_SHARDED = ("ab_w_in", "c_w_in", "f_w_up")
SMALL_REPLICATED = ("norm_mix", "norm_ffn", "norm_final", "a_ln_g", "a_ln_b", "a_w_s", "a_b_s",
                    "b_conv_b", "b_ln_g", "b_ln_b")
SMALL_SHARDED = ("b_conv_w", "c_conv_w", "f_conv_w")
SMALL = SMALL_REPLICATED + SMALL_SHARDED
ALL_WEIGHTS = ("norm_mix", "norm_ffn", "norm_final", "ab_w_in", "a_ln_g", "a_ln_b", "a_w_s", "a_b_s",
               "b_conv_w", "b_conv_b", "b_ln_g", "b_ln_b", "ab_w_out", "c_w_in", "c_conv_w", "c_w_out",
               "f_w_up", "f_conv_w", "f_w_down")


SCHEDULE = {
    "ab_in": [("gi", "f_w_up", 0, 0, 4), ("gi", "ab_w_out", 0)],
    "mixer_ab": [("gd", "f_w_up", 0, 0, 4), ("gd", "ab_w_out", 0), ("gi", "f_w_up", 0, 1, 4),
                 ("gi", "f_w_up", 0, 2, 4), ("gi", "f_w_up", 0, 3, 4)],
    "ab_out": [("gd", "f_w_up", 0, 1, 4), ("gd", "f_w_up", 0, 2, 4), ("gd", "f_w_up", 0, 3, 4)],
    "ffn_up0": [("gi", "f_w_down", 0), ("gi", "c_w_in", 0, 0, 2)],
    "ffn_act0": [("gd", "f_w_down", 0), ("gd", "c_w_in", 0, 0, 2), ("gi", "c_w_in", 0, 1, 2),
                 ("gi", "f_w_up", 1, 0, 4), ("gi", "f_w_up", 1, 1, 4)],
    "ffn_down0": [("gd", "c_w_in", 0, 1, 2), ("gd", "f_w_up", 1, 0, 4), ("gd", "f_w_up", 1, 1, 4),
                  ("gi", "f_w_up", 1, 2, 4)],
    "c_in": [("gd", "f_w_up", 1, 2, 4), ("gi", "f_w_up", 1, 3, 4), ("gi", "c_w_out", 0)],
    "mixer_c": [("gd", "f_w_up", 1, 3, 4), ("gd", "c_w_out", 0), ("gi", "f_w_down", 1, 0, 2)],
    "c_out": [("gd", "f_w_down", 1, 0, 2), ("gi", "f_w_down", 1, 1, 2)],
    "ffn_up1": [("gd", "f_w_down", 1, 1, 2)],
    "ffn_act_bwd1": [("px", "f_w_down", 1)],
    "ffn_up_dx1": [("cx", "f_w_down", 1)],
    "mixer_c_bwd": [("px", "f_w_up", 1), ("px", "c_w_out", 0)],
    "c_in_dw": [("cx", "f_w_up", 1, 0, 2), ("cx", "c_w_out", 0)],
    "c_in_dx": [("cx", "f_w_up", 1, 1, 2), ("px", "c_w_in", 0)],
    "ffn_down_dx0": [("cx", "c_w_in", 0, 0, 2)],
    "ffn_down_dw0": [("cx", "c_w_in", 0, 1, 2)],
    "ffn_act_bwd0": [("px", "f_w_down", 0)],
    "ffn_up_dx0": [("cx", "f_w_down", 0)],
    "mixer_ab_bwd": [("px", "f_w_up", 0), ("px", "ab_w_out", 0)],
    "mixer_b_conv_bwd": [("cx", "f_w_up", 0, 0, 2), ("cx", "ab_w_out", 0)],
    "ab_in_dw": [("cx", "f_w_up", 0, 1, 2)],
    "ab_in_dx": [("px", "ab_w_in", 0)],
}


class _Plan:
    def __init__(self, shapes, place):
        self.shapes, self.place, self.bufs = shapes, place, {}

    def weight(self, name, layer):
        g = self.bufs[f"w:{name}:{layer}"]
        if name in COL_SHARDED:
            return g
        _, S, rows, cols = g.shape
        return g.reshape(1, S * rows, cols)

    def grad_ready(self, name, layer, g):
        _, rows, cols = self.shapes[name]
        hbm = lambda a: pltpu.with_memory_space_constraint(a, pltpu.HBM)
        self.bufs[f"g:{name}:{layer}"] = g.reshape(N_CHIPS, rows, cols)
        self.bufs[f"t:{name}:{layer}"] = hbm(lax.empty((N_CHIPS, rows // 2, cols), F32))
        self.bufs[f"l:{name}:{layer}"] = hbm(lax.empty((3, rows // 2, cols), BF16))

    def job(self, kind, name, layer, part=0, parts=1):
        _, rows, cols = self.shapes[name]
        key = f"{name}:{layer}"
        if kind == "gi":
            return _job_gather_ici("w:" + key, rows, part, parts)
        if kind == "gd":
            return _job_gather_d2d("w:" + key, rows, part, parts)
        if kind == "px":
            return _job_pair_exchange("g:" + key, "t:" + key, rows)
        if kind == "cx":
            if "p:" + key not in self.bufs:
                self.bufs["p:" + key] = _pair_sum(self.bufs["g:" + key], self.bufs["t:" + key], self.place,
                                                  name=f"pair_sum_{name}{layer}")
            nr = rows // 2 // parts
            return _job_chip_exchange("p:" + key, "l:" + key, part * nr, nr)
        if kind == "ps":
            return _job_pair_share("G:" + name, layer, rows)
        raise ValueError(kind)

    def comm(self, call):
        specs = SCHEDULE.get(call)
        return None if specs is None else _Comm(self, [self.job(*spec) for spec in specs])


def _step(x, tgt, w, m, v):
    chip = 2 * lax.axis_index("x") + lax.axis_index("y")
    place = _place_scalars()
    plan = _Plan({n: w[n].shape for n in BIG}, place)
    items = [(n, l) for n in BIG for l in range(w[n].shape[0])]

    for n, l in items:
        plan.bufs[f"w:{n}:{l}"] = _cast_into_slot(w[n], place, layer=l, name=f"cast_{n}{l}")
    _comm_only(plan, [[plan.job("gi", "ab_w_in", 0)], [plan.job("gd", "ab_w_in", 0)]], name="gather_first")
    conv_pack, conv_sizes = _pack([w[n] for n in SMALL_SHARDED])
    conv_all = _gather_packs(conv_pack, name="gather_conv_weights")
    conv_shapes = [w[n].shape for n in SMALL_SHARDED]
    per_chip = [_unpack(conv_all[2 * s], conv_sizes, conv_shapes) for s in range(N_CHIPS)]
    small = {n: w[n] for n in SMALL_REPLICATED}
    for idx, n in enumerate(SMALL_SHARDED):
        small[n] = jnp.concatenate([per_chip[s][idx] for s in range(N_CHIPS)], axis=-1)

    loss, dx, sg = _local_step(x, tgt, small, plan)

    g_pack, g_sizes = _pack([sg[n] for n in SMALL])
    g_sum = _allreduce_pack(g_pack, name="allreduce_small_grads",
                            comm=_Comm(plan, [plan.job("cx", "ab_w_in", 0)]))
    full_shapes = [small[n].shape for n in SMALL]
    g_small = dict(zip(SMALL, _unpack(g_sum, g_sizes, full_shapes)))
    for n in SMALL_SHARDED:
        width = w[n].shape[-1]
        g_small[n] = lax.dynamic_slice_in_dim(g_small[n], chip * width, width, axis=g_small[n].ndim - 1)

    for n, l in items:
        plan.bufs["G:" + n] = _chip_sum(plan.bufs[f"p:{n}:{l}"], plan.bufs[f"l:{n}:{l}"], plan.bufs.get("G:" + n),
                                        place, layer=l, shape=w[n].shape, name=f"chip_sum_{n}{l}")
    _comm_only(plan, [[plan.job("ps", n, l) for n, l in items]], name="reduce_pair_share")
    grads_big = [plan.bufs["G:" + n] for n in BIG]

    grad, delta, new_m, new_v = {}, {}, {}, {}
    for n, g in zip(BIG, grads_big):
        grad[n] = g
        delta[n], new_m[n], new_v[n] = _adamw(w[n], g, m[n], v[n], name=f"adamw_{n}")
    shapes = [w[n].shape for n in SMALL]
    wp, sizes = _pack([w[n] for n in SMALL])
    gp, _ = _pack([g_small[n] for n in SMALL])
    mp, _ = _pack([m[n] for n in SMALL])
    vp, _ = _pack([v[n] for n in SMALL])
    R = wp.shape[0]
    dp, m2p, v2p = _adamw(wp.reshape(1, R, LANES), gp.reshape(1, R, LANES), mp.reshape(1, R, LANES),
                          vp.reshape(1, R, LANES), name="adamw_small")
    for n, d_, m_, v_ in zip(SMALL, _unpack(dp, sizes, shapes), _unpack(m2p, sizes, shapes),
                             _unpack(v2p, sizes, shapes)):
        grad[n] = g_small[n]
        delta[n], new_m[n], new_v[n] = d_, m_, v_
    return loss, dx, grad, delta, new_m, new_v


def kernel(x, norm_mix, norm_ffn, norm_final, ab_w_in, a_ln_g, a_ln_b, a_w_s, a_b_s, b_conv_w, b_conv_b, b_ln_g, b_ln_b, ab_w_out, c_w_in, c_conv_w, c_w_out, f_w_up, f_conv_w, f_w_down, loss_target, m_norm_mix, m_norm_ffn, m_norm_final, m_ab_w_in, m_a_ln_g, m_a_ln_b, m_a_w_s, m_a_b_s, m_b_conv_w, m_b_conv_b, m_b_ln_g, m_b_ln_b, m_ab_w_out, m_c_w_in, m_c_conv_w, m_c_w_out, m_f_w_up, m_f_conv_w, m_f_w_down, v_norm_mix, v_norm_ffn, v_norm_final, v_ab_w_in, v_a_ln_g, v_a_ln_b, v_a_w_s, v_a_b_s, v_b_conv_w, v_b_conv_b, v_b_ln_g, v_b_ln_b, v_ab_w_out, v_c_w_in, v_c_conv_w, v_c_w_out, v_f_w_up, v_f_conv_w, v_f_w_down):
    given = dict(locals())
    w = {n: given[n] for n in ALL_WEIGHTS}
    m = {n: given["m_" + n] for n in ALL_WEIGHTS}
    v = {n: given["v_" + n] for n in ALL_WEIGHTS}
    T = x.shape[1]
    loss, dx, grad, delta, new_m, new_v = _step(x.reshape(T, D_MODEL), loss_target.reshape(T, D_MODEL), w, m, v)
    loss = lax.psum(loss[0, 0], ("x", "y", "c"))
    out = [loss, dx.reshape(x.shape)]
    for d in (grad, delta, new_m, new_v):
        out += [d[n] for n in ALL_WEIGHTS]
    return tuple(out)
```

```python
import functools
import math

import jax
import jax.numpy as jnp
from jax import lax
from jax.experimental import pallas as pl
from jax.experimental.pallas import tpu as pltpu

F32 = jnp.float32
BF16 = jnp.bfloat16

EPS = 1e-6
D_MODEL = 1024
CHUNK = 128
HEAD_DIM = 128
A_HEADS = 4
D_A = 512
D_B = 512
B_CONV = 31
C_CONV = 3
D_FF = 2816
F_CONV = 3
N_CHIPS = 4

ADAM_LR = 0.001
ADAM_B1 = 0.9
ADAM_B2 = 0.999
ADAM_EPS = 1e-08
ADAM_WD = 0.01
ADAM_STEP = 10

SUBLANES = 8
LANES = 128
HALO_SHORT = 16
HALO_LONG = 32
VMEM_BYTES_MAX = 60000 * 1024

INV_SQRT2 = 1.0 / math.sqrt(2.0)
INV_SQRT_2PI = 1.0 / math.sqrt(2.0 * math.pi)

MESH = pl.DeviceIdType.MESH


def _cparams(sem, vmem_mb):
    del vmem_mb
    return pltpu.CompilerParams(dimension_semantics=sem, vmem_limit_bytes=VMEM_BYTES_MAX)


def _pick(total, pref):
    for c in (2048, 1024, 512, 256, 128):
        if c <= pref and total % c == 0:
            return c
    raise ValueError(f"no tile for {total}")


def _sigmoid(x):
    return jax.nn.sigmoid(x)


def _silu(x):
    return x * _sigmoid(x)


def _dsilu(x):
    s = _sigmoid(x)
    return s * (1.0 + x * (1.0 - s))


def _gelu(x):
    return 0.5 * x * (1.0 + lax.erf(x * INV_SQRT2))


def _dgelu(x):
    return 0.5 * (1.0 + lax.erf(x * INV_SQRT2)) + x * jnp.exp(-0.5 * x * x) * INV_SQRT_2PI


def _ln_stats(x):
    mu = jnp.mean(x, axis=-1, keepdims=True)
    xc = x - mu
    var = jnp.mean(xc * xc, axis=-1, keepdims=True)
    r = lax.rsqrt(var + EPS)
    return xc * r, r


def _ln_bwd(dy, xh, r, g):
    dxh = dy * g
    m1 = jnp.mean(dxh, axis=-1, keepdims=True)
    m2 = jnp.mean(dxh * xh, axis=-1, keepdims=True)
    return r * (dxh - m1 - xh * m2)


def _rowsum(x):
    return jnp.sum(x, axis=0, keepdims=True)


ANY = pl.BlockSpec(memory_space=pltpu.HBM)


def _place():
    x, y, c = lax.axis_index("x"), lax.axis_index("y"), lax.axis_index("c")
    peers = [(1 - x, y), (x, 1 - y), (1 - x, 1 - y)]
    return x, y, c, 2 * x + y, (x, y, 1 - c), peers


def _half(rows, which):
    return pl.ds(which * (rows // 2), rows // 2)


def _remote(src, dst, send_sem, recv_sem, device):
    return pltpu.make_async_remote_copy(src_ref=src, dst_ref=dst, send_sem=send_sem, recv_sem=recv_sem,
                                        device_id=device, device_id_type=MESH)


class _Job:
    def __init__(self, reads, writes, ncopies, copies):
        self.reads, self.writes, self.ncopies, self.copies = reads, writes, ncopies, copies


def _share(rows, which, part, parts):
    nr = rows // 2 // parts
    return pl.ds(which * (rows // 2) + part * nr, nr)


def _job_gather_ici(name, rows, part, parts):
    def copies(src, dst, sem):
        x, y, c, k, sib, peers = _place()
        mine_rows = _share(rows, c, part, parts)
        out = []
        for j, (px, py) in enumerate(peers):
            mine = src[name].at[0, k, mine_rows]
            out.append((_remote(mine, dst[name].at[0, k, mine_rows], sem(j, 0), sem(j, 1), (px, py, c)),
                        _remote(mine, dst[name].at[0, 2 * px + py, mine_rows], sem(j, 0), sem(j, 1), (px, py, c))))
        return out
    return _Job([], [name], 3, copies)


def _job_gather_d2d(name, rows, part, parts):
    def copies(src, dst, sem):
        x, y, c, k, sib, peers = _place()
        out = []
        for j, (px, py) in enumerate(peers):
            landed = src[name].at[0, 2 * px + py, _share(rows, c, part, parts)]
            out.append((_remote(landed, dst[name].at[0, 2 * px + py, _share(rows, c, part, parts)],
                                sem(j, 0), sem(j, 1), sib),
                        _remote(landed, dst[name].at[0, 2 * px + py, _share(rows, 1 - c, part, parts)],
                                sem(j, 0), sem(j, 1), sib)))
        return out
    return _Job([], [name], 3, copies)


def _job_chip_gather(sname, dname):
    def copies(src, dst, sem):
        x, y, c, k, sib, peers = _place()
        return [(_remote(src[sname], dst[dname].at[k], sem(j, 0), sem(j, 1), (px, py, c)),
                 _remote(src[sname], dst[dname].at[2 * px + py], sem(j, 0), sem(j, 1), (px, py, c)))
                for j, (px, py) in enumerate(peers)]
    return _Job([sname], [dname], 3, copies)


def _job_pair_exchange(gname, tname, rows):
    def copies(src, dst, sem):
        x, y, c, k, sib, peers = _place()
        cp = _remote(src[gname].at[:, _half(rows, 1 - c), :], dst[tname], sem(0, 0), sem(0, 1), sib)
        return [(cp, cp)]
    return _Job([gname], [tname], 1, copies)


def _job_chip_exchange(pname, lname, r0, nr):
    def copies(src, dst, sem):
        x, y, c, k, sib, peers = _place()
        out = []
        for j, (px, py) in enumerate(peers):
            cp = _remote(src[pname].at[2 * px + py, pl.ds(r0, nr)], dst[lname].at[j, pl.ds(r0, nr)],
                         sem(j, 0), sem(j, 1), (px, py, c))
            out.append((cp, cp))
        return out
    return _Job([pname], [lname], 3, copies)


def _job_pair_share(name, layer, rows):
    def copies(src, dst, sem):
        x, y, c, k, sib, peers = _place()
        mine = src[name].at[layer, _half(rows, c)]
        return [(_remote(mine, dst[name].at[layer, _half(rows, c)], sem(0, 0), sem(0, 1), sib),
                 _remote(mine, dst[name].at[layer, _half(rows, 1 - c)], sem(0, 0), sem(0, 1), sib))]
    return _Job([], [name], 1, copies)


class _Comm:
    def __init__(self, plan, jobs):
        self.plan, self.jobs = plan, jobs
        self.writes, self.reads = [], []
        for job in jobs:
            for n in job.writes:
                if n not in self.writes:
                    self.writes.append(n)
        for job in jobs:
            for n in job.reads:
                if n not in self.writes and n not in self.reads:
                    self.reads.append(n)
        self.ncopies = sum(job.ncopies for job in jobs)

    def descriptors(self, src, dst, sems, base):
        out = []
        for job in self.jobs:
            sem = lambda j, which, base=base: sems.at[base + j, which]
            out += job.copies(src, dst, sem)
            base += job.ncopies
        return out

    def start(self, src, dst, sems, base=0):
        for first, _ in self.descriptors(src, dst, sems, base):
            first.start()

    def finish(self, src, dst, sems, base=0):
        for _, landed in self.descriptors(src, dst, sems, base):
            landed.wait()


def _comm_operands(comm):
    bufs = comm.plan.bufs
    shapes = [jax.ShapeDtypeStruct(bufs[n].shape, bufs[n].dtype) for n in comm.writes]
    return [bufs[n] for n in comm.reads] + [bufs[n] for n in comm.writes], shapes


def _pallas(comm, body, *, name, grid, in_specs, out_specs, out_shape, compiler_params, scratch_shapes=(),
            aliases=None):
    aliases = dict(aliases or {})
    if comm is None:
        return pl.pallas_call(body, name=name, grid=grid, in_specs=in_specs, out_specs=out_specs,
                              out_shape=out_shape, scratch_shapes=list(scratch_shapes),
                              input_output_aliases=aliases, compiler_params=compiler_params)
    single = not isinstance(out_shape, (list, tuple))
    base_specs = [out_specs] if single else list(out_specs)
    base_shape = [out_shape] if single else list(out_shape)
    nb, nr, nw, nbo, nsc = len(in_specs), len(comm.reads), len(comm.writes), len(base_specs), len(scratch_shapes)

    def wrapped(*refs):
        base_in, rd, wr_in = refs[:nb], refs[nb:nb + nr], refs[nb + nr:nb + nr + nw]
        o0 = nb + nr + nw
        base_out, wr_out = refs[o0:o0 + nbo], refs[o0 + nbo:o0 + nbo + nw]
        scratch, sems = refs[o0 + nbo + nw:o0 + nbo + nw + nsc], refs[-1]
        src = dict(zip(comm.reads, rd))
        src.update(zip(comm.writes, wr_in))
        dst = dict(zip(comm.writes, wr_out))
        first = functools.reduce(jnp.logical_and, [pl.program_id(a) == 0 for a in range(len(grid))])
        last = functools.reduce(jnp.logical_and,
                                [pl.program_id(a) == pl.num_programs(a) - 1 for a in range(len(grid))])

        @pl.when(first)
        def _():
            comm.start(src, dst, sems)
        body(*base_in, *base_out, *scratch)

        @pl.when(last)
        def _():
            comm.finish(src, dst, sems)

    operands, shapes = _comm_operands(comm)
    call = pl.pallas_call(
        wrapped, name=name, grid=grid, in_specs=list(in_specs) + [ANY] * (nr + nw),
        out_specs=base_specs + [ANY] * nw, out_shape=base_shape + shapes,
        input_output_aliases={**aliases, **{nb + nr + q: nbo + q for q in range(nw)}},
        scratch_shapes=list(scratch_shapes) + [pltpu.SemaphoreType.DMA((comm.ncopies, 2))],
        compiler_params=compiler_params)

    def run(*args):
        outs = call(*args, *operands)
        for q, n in enumerate(comm.writes):
            comm.plan.bufs[n] = outs[nbo + q]
        return outs[0] if single else list(outs[:nbo])

    return run


def _comm_only(plan, phases, *, name):
    comms = [_Comm(plan, jobs) for jobs in phases]
    both = _Comm(plan, [job for jobs in phases for job in jobs])
    nr, nw = len(both.reads), len(both.writes)

    def body(*refs):
        rd, wr_in, wr_out, sems = refs[:nr], refs[nr:nr + nw], refs[nr + nw:nr + 2 * nw], refs[-1]
        src = dict(zip(both.reads, rd))
        src.update(zip(both.writes, wr_in))
        dst = dict(zip(both.writes, wr_out))
        base = 0
        for comm in comms:
            comm.start(src, dst, sems, base)
            comm.finish(src, dst, sems, base)
            base += comm.ncopies

    operands, shapes = _comm_operands(both)
    outs = pl.pallas_call(
        body, name=name, in_specs=[ANY] * (nr + nw), out_specs=[ANY] * nw, out_shape=shapes,
        input_output_aliases={nr + q: q for q in range(nw)},
        scratch_shapes=[pltpu.SemaphoreType.DMA((both.ncopies, 2))],
    )(*operands)
    for q, n in enumerate(both.writes):
        plan.bufs[n] = outs[q]


def _mm_nn(a, w, *, layer, tm, tn, residual=None, norm=None, out_dtype=F32, name, comm=None):
    T, K = a.shape
    if w.ndim == 4:
        _, S, _, n4 = w.shape
        N = S * n4
        bps = n4 // tn
        w_spec = pl.BlockSpec((None, None, K, tn), lambda j, i: (layer, j // bps, 0, j % bps))
    else:
        N = w.shape[2]
        w_spec = pl.BlockSpec((None, K, tn), lambda j, i: (layer, 0, j))
    in_specs = [pl.BlockSpec((tm, K), lambda j, i: (i, 0)), w_spec]
    args = [a, w]
    if residual is not None:
        in_specs.append(pl.BlockSpec((tm, tn), lambda j, i: (i, j)))
        args.append(residual)
    out_specs = pl.BlockSpec((tm, tn), lambda j, i: (i, j))
    out_shape = jax.ShapeDtypeStruct((T, N), out_dtype)
    if norm is not None:
        assert tn == N
        g, norm_layer = norm
        in_specs.append(pl.BlockSpec((None, 1, N), lambda j, i: (norm_layer, 0, 0)))
        args.append(g)
        out_specs = [out_specs, pl.BlockSpec((tm, tn), lambda j, i: (i, j))]
        out_shape = [out_shape, jax.ShapeDtypeStruct((T, N), BF16)]

    def body(*refs):
        a_ref, w_ref = refs[0], refs[1]
        acc = jnp.dot(a_ref[...].astype(BF16), w_ref[...], preferred_element_type=F32)
        if residual is not None:
            acc = refs[2][...] + acc
        if norm is None:
            refs[-1][...] = acc.astype(out_dtype)
        else:
            refs[-2][...] = acc.astype(out_dtype)
            r = lax.rsqrt(jnp.mean(acc * acc, axis=-1, keepdims=True) + EPS)
            refs[-1][...] = (acc * r * refs[-3][...]).astype(BF16)

    return _pallas(
        comm, body, name=name, grid=(N // tn, T // tm), in_specs=in_specs,
        out_specs=out_specs, out_shape=out_shape,
        compiler_params=_cparams(("parallel", "parallel"), 48),
    )(*args)


def _mm_nt(dy, w, *, layer, tm, tn, name, out_dtype=F32, comm=None):
    T = dy.shape[0]
    nt_dims = (((1,), (1,)), ((), ()))
    _, R, N = w.shape

    def body2(dy_ref, w_ref, o_ref):
        o_ref[...] = lax.dot_general(dy_ref[...].astype(BF16), w_ref[...], nt_dims,
                                     preferred_element_type=F32).astype(out_dtype)

    return _pallas(
        comm, body2, name=name, grid=(R // tn, T // tm),
        in_specs=[pl.BlockSpec((tm, N), lambda j, i: (i, 0)),
                  pl.BlockSpec((None, tn, N), lambda j, i: (layer, j, 0))],
        out_specs=pl.BlockSpec((tm, tn), lambda j, i: (i, j)),
        out_shape=jax.ShapeDtypeStruct((T, R), out_dtype),
        compiler_params=_cparams(("parallel", "parallel"), 48),
    )(dy, w)


def _mm_tn(a, dy, *, shards, tk, tn, tt, name, comm=None):
    T, K = a.shape
    N = dy.shape[1]
    tn_dims = (((0,), (0,)), ((), ()))

    def body(a_ref, dy_ref, o_ref):
        @pl.when(pl.program_id(2) == 0)
        def _():
            o_ref[...] = jnp.zeros_like(o_ref)
        o_ref[...] += lax.dot_general(a_ref[...].astype(BF16), dy_ref[...].astype(BF16), tn_dims,
                                      preferred_element_type=F32)

    if shards is None:
        out_spec = pl.BlockSpec((tk, tn), lambda k, n, t: (k, n))
        out_shape = jax.ShapeDtypeStruct((K, N), F32)
    else:
        n4 = N // shards
        bps = n4 // tn
        out_spec = pl.BlockSpec((None, tk, tn), lambda k, n, t: (n // bps, k, n % bps))
        out_shape = jax.ShapeDtypeStruct((shards, K, n4), F32)
    return _pallas(
        comm, body, name=name, grid=(K // tk, N // tn, T // tt),
        in_specs=[pl.BlockSpec((tt, tk), lambda k, n, t: (t, k)),
                  pl.BlockSpec((tt, tn), lambda k, n, t: (t, n))],
        out_specs=out_spec, out_shape=out_shape,
        compiler_params=_cparams(("parallel", "parallel", "arbitrary"), 48),
    )(a, dy)


def _rmsnorm_fwd(x, g, *, layer, tm, name, comm=None):
    T, D = x.shape

    def body(x_ref, g_ref, h_ref):
        xf = x_ref[...]
        r = lax.rsqrt(jnp.mean(xf * xf, axis=-1, keepdims=True) + EPS)
        h_ref[...] = (xf * r * g_ref[...]).astype(BF16)

    return _pallas(
        comm, body, name=name, grid=(T // tm,),
        in_specs=[pl.BlockSpec((tm, D), lambda i: (i, 0)),
                  pl.BlockSpec((None, 1, D), lambda i: (layer, 0, 0))],
        out_specs=pl.BlockSpec((tm, D), lambda i: (i, 0)),
        out_shape=jax.ShapeDtypeStruct((T, D), BF16),
        compiler_params=_cparams(("parallel",), 32),
    )(x, g)


def _rmsnorm_bwd_math(xf, g, dh, dres):
    r = lax.rsqrt(jnp.mean(xf * xf, axis=-1, keepdims=True) + EPS)
    xh = xf * r
    dxh = dh * g
    dx = dres + r * (dxh - xh * jnp.mean(dxh * xh, axis=-1, keepdims=True))
    return dx, _rowsum(dh * xh)


def _mm_nt_norm(dy, w, x, g, dres, *, g_layer, tm, name, comm=None):
    T = dy.shape[0]
    _, S, K, n4 = w.shape
    nt_dims = (((1,), (1,)), ((), ()))

    def body(dy_ref, w_ref, x_ref, g_ref, dres_ref, dx_ref, dg_ref):
        @pl.when(pl.program_id(0) == 0)
        def _():
            dg_ref[...] = jnp.zeros_like(dg_ref)
        dh = None
        for s in range(S):
            part = lax.dot_general(dy_ref[:, s * n4:(s + 1) * n4].astype(BF16), w_ref[s], nt_dims,
                                   preferred_element_type=F32)
            dh = part if dh is None else dh + part
        dx, dg = _rmsnorm_bwd_math(x_ref[...], g_ref[...], dh, dres_ref[...])
        dx_ref[...] = dx
        dg_ref[...] += dg

    row = lambda i: (i, 0)
    return _pallas(
        comm, body, name=name, grid=(T // tm,),
        in_specs=[pl.BlockSpec((tm, S * n4), row),
                  pl.BlockSpec((None, S, K, n4), lambda i: (0, 0, 0, 0)),
                  pl.BlockSpec((tm, K), row),
                  pl.BlockSpec((None, 1, K), lambda i: (g_layer, 0, 0)),
                  pl.BlockSpec((tm, K), row)],
        out_specs=[pl.BlockSpec((tm, K), row), pl.BlockSpec((1, K), lambda i: (0, 0))],
        out_shape=[jax.ShapeDtypeStruct((T, K), F32), jax.ShapeDtypeStruct((1, K), F32)],
        compiler_params=_cparams(("arbitrary",), 56),
    )(dy, w, x, g, dres)


def _loss_head(x, tgt, g, *, tm, name, comm=None):
    T, D = x.shape

    def body(x_ref, t_ref, g_ref, loss_ref, dx_ref, dg_ref):
        @pl.when(pl.program_id(0) == 0)
        def _():
            dg_ref[...] = jnp.zeros_like(dg_ref)
            loss_ref[...] = jnp.zeros_like(loss_ref)
        xf = x_ref[...]
        gg = g_ref[...]
        r = lax.rsqrt(jnp.mean(xf * xf, axis=-1, keepdims=True) + EPS)
        xh = xf * r
        err = xh * gg - t_ref[...]
        row = jnp.mean(err * err, axis=-1, keepdims=True)
        loss_ref[...] += 0.5 * jnp.sum(row, axis=0, keepdims=True)
        dy = err * (1.0 / D)
        dg_ref[...] += _rowsum(dy * xh)
        dxh = dy * gg
        dx_ref[...] = r * (dxh - xh * jnp.mean(dxh * xh, axis=-1, keepdims=True))

    return _pallas(
        comm, body, name=name, grid=(T // tm,),
        in_specs=[pl.BlockSpec((tm, D), lambda i: (i, 0)),
                  pl.BlockSpec((tm, D), lambda i: (i, 0)),
                  pl.BlockSpec((1, D), lambda i: (0, 0))],
        out_specs=[pl.BlockSpec((1, 1), lambda i: (0, 0)),
                   pl.BlockSpec((tm, D), lambda i: (i, 0)),
                   pl.BlockSpec((1, D), lambda i: (0, 0))],
        out_shape=[jax.ShapeDtypeStruct((1, 1), F32), jax.ShapeDtypeStruct((T, D), F32),
                   jax.ShapeDtypeStruct((1, D), F32)],
        compiler_params=_cparams(("arbitrary",), 40),
    )(x, tgt, g)


CONV_ROWS = 64
CONV_COLS = 256


def _halo_prev_index(tm, halo):
    per = tm // halo
    return lambda i: jnp.maximum(i * per - 1, 0)


def _halo_next_index(tm, halo, total):
    per = tm // halo
    last = total // halo - 1
    return lambda i: jnp.minimum((i + 1) * per, last)


def _causal_mask():
    t = lax.broadcasted_iota(jnp.int32, (CHUNK, CHUNK), 0)
    s = lax.broadcasted_iota(jnp.int32, (CHUNK, CHUNK), 1)
    return s <= t


def _mixer_ab_fwd(z, a_ln_g, a_ln_b, w_s, b_s, conv_w, conv_b, b_ln_g, b_ln_b, *, tm, name, comm=None):
    T = z.shape[0]
    nchunk = tm // CHUNK
    halo = HALO_LONG

    def body(za_ref, zb_ref, zh_ref, alg_ref, alb_ref, ws_ref, bs_ref, cw_ref, cbias_ref,
             blg_ref, blb_ref, y_ref, cb_ref, ext_ref):
        i = pl.program_id(0)
        gu = _gelu(za_ref[:, :D_A].astype(F32))
        gv = _gelu(za_ref[:, D_A:].astype(F32))
        xh, _ = _ln_stats(gv)
        lv = (xh * alg_ref[...] + alb_ref[...]).astype(BF16)
        mask = _causal_mask()
        for h in range(A_HEADS):
            wm = jnp.where(mask, ws_ref[h], 0.0).astype(BF16)
            cols = slice(h * HEAD_DIM, (h + 1) * HEAD_DIM)
            for c in range(nchunk):
                rows = slice(c * CHUNK, (c + 1) * CHUNK)
                mixed = jnp.dot(wm, lv[rows, cols], preferred_element_type=F32) + bs_ref[h]
                y_ref[rows, cols] = (gu[rows, cols] * mixed).astype(BF16)
        ext_ref[halo:halo + tm, :] = zb_ref[:, :D_B].astype(F32) * _sigmoid(zb_ref[:, D_B:].astype(F32))
        prev = zh_ref[:, :D_B].astype(F32) * _sigmoid(zh_ref[:, D_B:].astype(F32))
        ext_ref[0:halo, :] = jnp.where(i > 0, prev, 0.0)
        for rb in range(tm // CONV_ROWS):
            for cb in range(D_B // CONV_COLS):
                cs = slice(cb * CONV_COLS, (cb + 1) * CONV_COLS)
                window = ext_ref[rb * CONV_ROWS:rb * CONV_ROWS + CONV_ROWS + halo, cs]
                acc = jnp.zeros((CONV_ROWS, CONV_COLS), F32)
                for k in range(B_CONV):
                    shifted = _rows_after(window, halo - (B_CONV - 1) + k)[:CONV_ROWS]
                    acc = acc + cw_ref[k:k + 1, cs] * shifted
                cb_ref[rb * CONV_ROWS:(rb + 1) * CONV_ROWS, cs] = acc + cbias_ref[:, cs]
        xhb, _ = _ln_stats(cb_ref[...])
        y_ref[:, D_A:] = _silu(xhb * blg_ref[...] + blb_ref[...]).astype(BF16)

    row = lambda i: (i, 0)
    par = lambda i: (0, 0)
    return _pallas(
        comm, body, name=name, grid=(T // tm,),
        in_specs=[pl.BlockSpec((tm, 2 * D_A), lambda i: (i, 0)),
                  pl.BlockSpec((tm, 2 * D_B), lambda i: (i, 1)),
                  pl.BlockSpec((halo, 2 * D_B), lambda i: (_halo_prev_index(tm, halo)(i), 1)),
                  pl.BlockSpec((1, D_A), par), pl.BlockSpec((1, D_A), par),
                  pl.BlockSpec((A_HEADS, CHUNK, CHUNK), lambda i: (0, 0, 0)),
                  pl.BlockSpec((A_HEADS, CHUNK, 1), lambda i: (0, 0, 0)),
                  pl.BlockSpec((B_CONV, D_B), par), pl.BlockSpec((1, D_B), par),
                  pl.BlockSpec((1, D_B), par), pl.BlockSpec((1, D_B), par)],
        out_specs=[pl.BlockSpec((tm, D_A + D_B), row), pl.BlockSpec((tm, D_B), row)],
        out_shape=[jax.ShapeDtypeStruct((T, D_A + D_B), BF16), jax.ShapeDtypeStruct((T, D_B), F32)],
        scratch_shapes=[pltpu.VMEM((halo + tm, D_B), F32)],
        compiler_params=_cparams(("parallel",), 40),
    )(z, z, z, a_ln_g, a_ln_b, w_s, b_s, conv_w, conv_b, b_ln_g, b_ln_b)


def _mixer_ab_bwd_pre(z, cb, dy, a_ln_g, a_ln_b, w_s, b_s, b_ln_g, b_ln_b, *, tm, name, comm=None):
    T = z.shape[0]
    nchunk = tm // CHUNK
    tn_dims = (((0,), (0,)), ((), ()))
    nt_dims = (((1,), (1,)), ((), ()))

    def body(za_ref, cb_ref, dy_ref, alg_ref, alb_ref, ws_ref, bs_ref, blg_ref, blb_ref,
             dza_ref, dcb_ref, dalg_ref, dalb_ref, dws_ref, dbs_ref, dblg_ref, dblb_ref,
             dlv_ref):
        @pl.when(pl.program_id(0) == 0)
        def _():
            for ref in (dalg_ref, dalb_ref, dws_ref, dbs_ref, dblg_ref, dblb_ref):
                ref[...] = jnp.zeros_like(ref)
        ua = za_ref[:, :D_A].astype(F32)
        va = za_ref[:, D_A:].astype(F32)
        gu = _gelu(ua)
        gv = _gelu(va)
        xh, r = _ln_stats(gv)
        alg = alg_ref[...]
        lv = (xh * alg + alb_ref[...]).astype(BF16)
        dya = dy_ref[:, :D_A].astype(F32)
        mask = _causal_mask()
        for h in range(A_HEADS):
            wm = jnp.where(mask, ws_ref[h], 0.0).astype(BF16)
            cols = slice(h * HEAD_DIM, (h + 1) * HEAD_DIM)
            dwm = jnp.zeros((CHUNK, CHUNK), F32)
            dbs = jnp.zeros((CHUNK, 1), F32)
            for c in range(nchunk):
                rows = slice(c * CHUNK, (c + 1) * CHUNK)
                lvb = lv[rows, cols]
                mixed = jnp.dot(wm, lvb, preferred_element_type=F32) + bs_ref[h]
                dyb = dya[rows, cols]
                dza_ref[rows, cols] = (dyb * mixed * _dgelu(ua[rows, cols])).astype(BF16)
                dmixed = dyb * gu[rows, cols]
                dmb = dmixed.astype(BF16)
                dlv_ref[rows, cols] = lax.dot_general(wm, dmb, tn_dims, preferred_element_type=F32)
                dwm = dwm + lax.dot_general(dmb, lvb, nt_dims, preferred_element_type=F32)
                dbs = dbs + jnp.sum(dmixed, axis=1, keepdims=True)
            dws_ref[h] += jnp.where(mask, dwm, 0.0)
            dbs_ref[h] += dbs
        dlv = dlv_ref[...]
        dalg_ref[...] += _rowsum(dlv * xh)
        dalb_ref[...] += _rowsum(dlv)
        dgv = _ln_bwd(dlv, xh, r, alg)
        dza_ref[:, D_A:] = (dgv * _dgelu(va)).astype(BF16)
        xhb, rb = _ln_stats(cb_ref[...])
        blg = blg_ref[...]
        lb = xhb * blg + blb_ref[...]
        dlb = dy_ref[:, D_A:].astype(F32) * _dsilu(lb)
        dblg_ref[...] += _rowsum(dlb * xhb)
        dblb_ref[...] += _rowsum(dlb)
        dcb_ref[...] = _ln_bwd(dlb, xhb, rb, blg)

    row = lambda i: (i, 0)
    par = lambda i: (0, 0)
    par3 = lambda i: (0, 0, 0)
    return _pallas(
        comm, body, name=name, grid=(T // tm,),
        in_specs=[pl.BlockSpec((tm, 2 * D_A), row), pl.BlockSpec((tm, D_B), row),
                  pl.BlockSpec((tm, D_A + D_B), row),
                  pl.BlockSpec((1, D_A), par), pl.BlockSpec((1, D_A), par),
                  pl.BlockSpec((A_HEADS, CHUNK, CHUNK), par3),
                  pl.BlockSpec((A_HEADS, CHUNK, 1), par3),
                  pl.BlockSpec((1, D_B), par), pl.BlockSpec((1, D_B), par)],
        out_specs=[pl.BlockSpec((tm, 2 * D_A), row), pl.BlockSpec((tm, D_B), row),
                   pl.BlockSpec((1, D_A), par), pl.BlockSpec((1, D_A), par),
                   pl.BlockSpec((A_HEADS, CHUNK, CHUNK), par3),
                   pl.BlockSpec((A_HEADS, CHUNK, 1), par3),
                   pl.BlockSpec((1, D_B), par), pl.BlockSpec((1, D_B), par)],
        out_shape=[jax.ShapeDtypeStruct((T, 2 * D_A + 2 * D_B), BF16), jax.ShapeDtypeStruct((T, D_B), F32),
                   jax.ShapeDtypeStruct((1, D_A), F32), jax.ShapeDtypeStruct((1, D_A), F32),
                   jax.ShapeDtypeStruct((A_HEADS, CHUNK, CHUNK), F32),
                   jax.ShapeDtypeStruct((A_HEADS, CHUNK, 1), F32),
                   jax.ShapeDtypeStruct((1, D_B), F32), jax.ShapeDtypeStruct((1, D_B), F32)],
        scratch_shapes=[pltpu.VMEM((tm, D_A), F32)],
        compiler_params=_cparams(("arbitrary",), 40),
    )(z, cb, dy, a_ln_g, a_ln_b, w_s, b_s, b_ln_g, b_ln_b)


def _mixer_b_conv_bwd(z, dcb, conv_w, dz, *, tm, name, comm=None):
    T = z.shape[0]
    halo = HALO_LONG

    def body(zb_ref, dcb_ref, dcn_ref, cw_ref, dz_in_ref, dzb_ref, dcw_ref, dbias_ref, dext_ref):
        i = pl.program_id(0)
        last = pl.num_programs(0) - 1

        @pl.when(i == 0)
        def _():
            dcw_ref[...] = jnp.zeros_like(dcw_ref)
            dbias_ref[...] = jnp.zeros_like(dbias_ref)
        dcb = dcb_ref[...]
        dext_ref[0:tm, :] = dcb
        dext_ref[tm:tm + halo, :] = jnp.where(i < last, dcn_ref[...], 0.0)
        dbias_ref[...] += _rowsum(dcb)
        for rb in range(tm // CONV_ROWS):
            for cb in range(D_B // CONV_COLS):
                cs = slice(cb * CONV_COLS, (cb + 1) * CONV_COLS)
                gcs = slice(D_B + cb * CONV_COLS, D_B + (cb + 1) * CONV_COLS)
                rs = slice(rb * CONV_ROWS, (rb + 1) * CONV_ROWS)
                xbb = zb_ref[rs, cs].astype(F32)
                sgb = _sigmoid(zb_ref[rs, gcs].astype(F32))
                yb0 = xbb * sgb
                window = dext_ref[rb * CONV_ROWS:rb * CONV_ROWS + CONV_ROWS + halo, cs]
                acc = jnp.zeros((CONV_ROWS, CONV_COLS), F32)
                for k in range(B_CONV):
                    shifted = _rows_after(window, (B_CONV - 1) - k)[:CONV_ROWS]
                    acc = acc + cw_ref[k:k + 1, cs] * shifted
                    dcw_ref[k:k + 1, cs] += _rowsum(shifted * yb0)
                dzb_ref[rs, cs] = (acc * sgb).astype(BF16)
                dzb_ref[rs, gcs] = (acc * xbb * sgb * (1.0 - sgb)).astype(BF16)

    row = lambda i: (i, 0)
    par = lambda i: (0, 0)
    return _pallas(
        comm, body, name=name, grid=(T // tm,),
        in_specs=[pl.BlockSpec((tm, 2 * D_B), lambda i: (i, 1)),
                  pl.BlockSpec((tm, D_B), row),
                  pl.BlockSpec((halo, D_B), lambda i: (_halo_next_index(tm, halo, T)(i), 0)),
                  pl.BlockSpec((B_CONV, D_B), par), pl.BlockSpec(memory_space=pl.ANY)],
        out_specs=[pl.BlockSpec((tm, 2 * D_B), lambda i: (i, 1)), pl.BlockSpec((B_CONV, D_B), par),
                   pl.BlockSpec((1, D_B), par)],
        out_shape=[jax.ShapeDtypeStruct(dz.shape, BF16), jax.ShapeDtypeStruct((B_CONV, D_B), F32),
                   jax.ShapeDtypeStruct((1, D_B), F32)],
        scratch_shapes=[pltpu.VMEM((tm + halo, D_B), F32)], aliases={4: 0},
        compiler_params=_cparams(("arbitrary",), 40),
    )(z, dcb, dcb, conv_w, dz)


def _rows_before(x, a):
    return x if a == 0 else pltpu.roll(x, a, axis=0)


def _rows_after(x, a):
    return x if a == 0 else pltpu.roll(x, x.shape[0] - a, axis=0)


def _conv3(w_ref, x, halo, cs):
    acc = w_ref[2:3, cs] * x[halo:]
    acc = acc + w_ref[1:2, cs] * _rows_before(x, 1)[halo:]
    return acc + w_ref[0:1, cs] * _rows_before(x, 2)[halo:]


def _mixer_c_fwd(z, conv_w, *, tm, name, comm=None):
    T = z.shape[0]
    D = D_MODEL
    halo = HALO_SHORT
    W = CONV_COLS

    def body(bg_ref, cg_ref, xv_ref, cgh_ref, xvh_ref, w_ref, r_ref):
        i = pl.program_id(0)
        for cb in range(D // W):
            cs = slice(cb * W, (cb + 1) * W)
            prev = jnp.where(i > 0, cgh_ref[:, cs].astype(F32) * xvh_ref[:, cs].astype(F32), 0.0)
            p = jnp.concatenate([prev, cg_ref[:, cs].astype(F32) * xv_ref[:, cs].astype(F32)], axis=0)
            r_ref[:, cs] = (bg_ref[:, cs].astype(F32) * _conv3(w_ref, p, halo, cs)).astype(BF16)

    hp = _halo_prev_index(tm, halo)
    return _pallas(
        comm, body, name=name, grid=(T // tm,),
        in_specs=[pl.BlockSpec((tm, D), lambda i: (i, 0)), pl.BlockSpec((tm, D), lambda i: (i, 1)),
                  pl.BlockSpec((tm, D), lambda i: (i, 2)),
                  pl.BlockSpec((halo, D), lambda i: (hp(i), 1)),
                  pl.BlockSpec((halo, D), lambda i: (hp(i), 2)),
                  pl.BlockSpec((None, C_CONV, D), lambda i: (0, 0, 0))],
        out_specs=pl.BlockSpec((tm, D), lambda i: (i, 0)),
        out_shape=jax.ShapeDtypeStruct((T, D), BF16),
        compiler_params=_cparams(("parallel",), 40),
    )(z, z, z, z, z, conv_w)


def _mixer_c_bwd(z, dr, conv_w, *, tm, name, comm=None):
    T = z.shape[0]
    D = D_MODEL
    halo = HALO_SHORT
    W = CONV_COLS

    def body(bg_ref, cg_ref, xv_ref, cgh_ref, xvh_ref, bgn_ref, dr_ref, drn_ref, w_ref, dz_ref, dw_ref):
        i = pl.program_id(0)
        last = pl.num_programs(0) - 1

        @pl.when(i == 0)
        def _():
            dw_ref[...] = jnp.zeros_like(dw_ref)
        for cb in range(D // W):
            cs = slice(cb * W, (cb + 1) * W)
            cg = cg_ref[:, cs].astype(F32)
            xv = xv_ref[:, cs].astype(F32)
            dr = dr_ref[:, cs].astype(F32)
            p = cg * xv
            prev = jnp.where(i > 0, cgh_ref[:, cs].astype(F32) * xvh_ref[:, cs].astype(F32), 0.0)
            q = _conv3(w_ref, jnp.concatenate([prev, p], axis=0), halo, cs)
            dz_ref[:, cs] = (dr * q).astype(BF16)
            nxt = jnp.where(i < last, drn_ref[:, cs].astype(F32) * bgn_ref[:, cs].astype(F32), 0.0)
            dq = jnp.concatenate([dr * bg_ref[:, cs].astype(F32), nxt], axis=0)
            dp = None
            for k in range(C_CONV):
                shifted = _rows_after(dq, 2 - k)[:tm]
                term = w_ref[k:k + 1, cs] * shifted
                dp = term if dp is None else dp + term
                dw_ref[k:k + 1, cs] += _rowsum(shifted * p)
            dz_ref[:, D + cb * W:D + (cb + 1) * W] = (dp * xv).astype(BF16)
            dz_ref[:, 2 * D + cb * W:2 * D + (cb + 1) * W] = (dp * cg).astype(BF16)

    hp = _halo_prev_index(tm, halo)
    hn = _halo_next_index(tm, halo, T)
    return _pallas(
        comm, body, name=name, grid=(T // tm,),
        in_specs=[pl.BlockSpec((tm, D), lambda i: (i, 0)), pl.BlockSpec((tm, D), lambda i: (i, 1)),
                  pl.BlockSpec((tm, D), lambda i: (i, 2)),
                  pl.BlockSpec((halo, D), lambda i: (hp(i), 1)),
                  pl.BlockSpec((halo, D), lambda i: (hp(i), 2)),
                  pl.BlockSpec((halo, D), lambda i: (hn(i), 0)),
                  pl.BlockSpec((tm, D), lambda i: (i, 0)),
                  pl.BlockSpec((halo, D), lambda i: (hn(i), 0)),
                  pl.BlockSpec((None, C_CONV, D), lambda i: (0, 0, 0))],
        out_specs=[pl.BlockSpec((tm, 3 * D), lambda i: (i, 0)),
                   pl.BlockSpec((C_CONV, D), lambda i: (0, 0))],
        out_shape=[jax.ShapeDtypeStruct((T, 3 * D), BF16), jax.ShapeDtypeStruct((C_CONV, D), F32)],
        compiler_params=_cparams(("arbitrary",), 48),
    )(z, z, z, z, z, z, dr, dr, conv_w)


FFN_COLS = 128


def _ffn_act_fwd(up, conv_w, *, layer, tm, name, comm=None):
    T = up.shape[0]
    halo = HALO_SHORT
    W = FFN_COLS

    def body(up_ref, uph_ref, w_ref, a_ref, upc_ref):
        i = pl.program_id(0)

        def conv(cs):
            prev = jnp.where(i > 0, uph_ref[:, cs], jnp.zeros((halo, W), BF16))
            return _conv3(w_ref, jnp.concatenate([prev, up_ref[:, cs]], axis=0).astype(F32), halo, cs)

        for cb in range(D_FF // W):
            gs = slice(cb * W, (cb + 1) * W)
            vs = slice(D_FF + cb * W, D_FF + (cb + 1) * W)
            g = conv(gs)
            v = conv(vs)
            upc_ref[:, gs] = g.astype(BF16)
            upc_ref[:, vs] = v.astype(BF16)
            a_ref[:, gs] = (_silu(g) * v).astype(BF16)

    return _pallas(
        comm, body, name=name, grid=(T // tm,),
        in_specs=[pl.BlockSpec((tm, 2 * D_FF), lambda i: (i, 0)),
                  pl.BlockSpec((halo, 2 * D_FF), lambda i: (_halo_prev_index(tm, halo)(i), 0)),
                  pl.BlockSpec((None, F_CONV, 2 * D_FF), lambda i: (layer, 0, 0))],
        out_specs=[pl.BlockSpec((tm, D_FF), lambda i: (i, 0)),
                   pl.BlockSpec((tm, 2 * D_FF), lambda i: (i, 0))],
        out_shape=[jax.ShapeDtypeStruct((T, D_FF), BF16), jax.ShapeDtypeStruct((T, 2 * D_FF), BF16)],
        compiler_params=_cparams(("parallel",), 48),
    )(up, up, conv_w)


def _ffn_act_bwd(up, upc, da, conv_w, *, layer, tm, name, comm=None):
    T = up.shape[0]
    halo = HALO_SHORT
    W = FFN_COLS

    def body(up_ref, upc_ref, upcn_ref, da_ref, dan_ref, w_ref, dup_ref, dw_ref):
        i = pl.program_id(0)
        last = pl.num_programs(0) - 1

        @pl.when(i == 0)
        def _():
            dw_ref[...] = jnp.zeros_like(dw_ref)
        live = jnp.where(i < last, 1.0, 0.0)
        for cb in range(D_FF // W):
            gs = slice(cb * W, (cb + 1) * W)
            vs = slice(D_FF + cb * W, D_FF + (cb + 1) * W)
            g = jnp.concatenate([upc_ref[:, gs], upcn_ref[:, gs]], axis=0).astype(F32)
            v = jnp.concatenate([upc_ref[:, vs], upcn_ref[:, vs]], axis=0).astype(F32)
            da = jnp.concatenate([da_ref[:, gs].astype(F32), dan_ref[:, gs].astype(F32) * live], axis=0)
            s = _sigmoid(g)
            silu = g * s
            grads = (da * v * (s * (1.0 + g * (1.0 - s))), da * silu)
            for cs, d in zip((gs, vs), grads):
                u = up_ref[:, cs].astype(F32)
                acc = None
                for k in range(F_CONV):
                    shifted = _rows_after(d, 2 - k)[:tm]
                    term = w_ref[k:k + 1, cs] * shifted
                    acc = term if acc is None else acc + term
                    dw_ref[k:k + 1, cs] += _rowsum(shifted * u)
                dup_ref[:, cs] = acc.astype(BF16)

    hn = _halo_next_index(tm, halo, T)
    return _pallas(
        comm, body, name=name, grid=(T // tm,),
        in_specs=[pl.BlockSpec((tm, 2 * D_FF), lambda i: (i, 0)),
                  pl.BlockSpec((tm, 2 * D_FF), lambda i: (i, 0)),
                  pl.BlockSpec((halo, 2 * D_FF), lambda i: (hn(i), 0)),
                  pl.BlockSpec((tm, D_FF), lambda i: (i, 0)),
                  pl.BlockSpec((halo, D_FF), lambda i: (hn(i), 0)),
                  pl.BlockSpec((None, F_CONV, 2 * D_FF), lambda i: (layer, 0, 0))],
        out_specs=[pl.BlockSpec((tm, 2 * D_FF), lambda i: (i, 0)),
                   pl.BlockSpec((F_CONV, 2 * D_FF), lambda i: (0, 0))],
        out_shape=[jax.ShapeDtypeStruct((T, 2 * D_FF), BF16),
                   jax.ShapeDtypeStruct((F_CONV, 2 * D_FF), F32)],
        compiler_params=_cparams(("arbitrary",), 56),
    )(up, upc, upc, da, da, conv_w)


def _local_step(x, tgt, small, plan):
    T = x.shape[0]
    tm_e = _pick(T, 256)
    tm_n = _pick(T, 512)
    tm = _pick(T, 1024)
    tm_f = _pick(T, 512)
    tt = _pick(T, 2048)
    nm = small["norm_mix"].reshape(2, 1, D_MODEL)
    nf = small["norm_ffn"].reshape(2, 1, D_MODEL)
    ngf = small["norm_final"].reshape(1, D_MODEL)
    b_s = small["a_b_s"].reshape(A_HEADS, CHUNK, 1)
    w_s = small["a_w_s"].reshape(A_HEADS, CHUNK, CHUNK)
    b_conv_w = small["b_conv_w"].reshape(B_CONV, D_B)
    sg = {}
    wt, cm = plan.weight, plan.comm

    h_m0 = _rmsnorm_fwd(x, nm, layer=0, tm=tm_n, name="norm_mix0")
    z_ab = _mm_nn(h_m0, wt("ab_w_in", 0), layer=0, tm=tm, tn=512, out_dtype=BF16, name="ab_in", comm=cm("ab_in"))
    yab, cb = _mixer_ab_fwd(z_ab, small["a_ln_g"], small["a_ln_b"], w_s, b_s, b_conv_w, small["b_conv_b"],
                            small["b_ln_g"], small["b_ln_b"], tm=tm_e, name="mixer_ab", comm=cm("mixer_ab"))
    x1, h_f0 = _mm_nn(yab, wt("ab_w_out", 0), layer=0, tm=tm, tn=D_MODEL, residual=x, norm=(nf, 0),
                      name="ab_out", comm=cm("ab_out"))

    def ffn_fwd(xin, h, layer, norm):
        up = _mm_nn(h, wt("f_w_up", layer), layer=0, tm=tm, tn=1408, out_dtype=BF16, name=f"ffn_up{layer}",
                    comm=cm(f"ffn_up{layer}"))
        a, upc = _ffn_act_fwd(up, small["f_conv_w"], layer=layer, tm=tm_e, name=f"ffn_act{layer}",
                              comm=cm(f"ffn_act{layer}"))
        out = _mm_nn(a, wt("f_w_down", layer), layer=0, tm=tm, tn=D_MODEL, residual=xin, norm=norm,
                     name=f"ffn_down{layer}", comm=cm(f"ffn_down{layer}"))
        return up, upc, a, out

    up0, upc0, a0, (x2, h_m1) = ffn_fwd(x1, h_f0, 0, (nm, 1))
    z_c = _mm_nn(h_m1, wt("c_w_in", 0), layer=0, tm=tm, tn=768, out_dtype=BF16, name="c_in", comm=cm("c_in"))
    r = _mixer_c_fwd(z_c, small["c_conv_w"], tm=tm_e, name="mixer_c", comm=cm("mixer_c"))
    x3, h_f1 = _mm_nn(r, wt("c_w_out", 0), layer=0, tm=tm, tn=D_MODEL, residual=x2, norm=(nf, 1),
                      name="c_out", comm=cm("c_out"))
    up1, upc1, a1, x4 = ffn_fwd(x3, h_f1, 1, None)
    loss, dx, sg["norm_final"] = _loss_head(x4, tgt, ngf, tm=tm_n, name="loss_head")

    def ffn_bwd(dx, xin, h, up, upc, a, layer):
        da = _mm_nt(dx, wt("f_w_down", layer), layer=0, tm=tm, tn=1408, out_dtype=BF16,
                    name=f"ffn_down_dx{layer}", comm=cm(f"ffn_down_dx{layer}"))
        plan.grad_ready("f_w_down", layer, _mm_tn(a, dx, shards=None, tk=1408, tn=1024, tt=tt,
                                                  name=f"ffn_down_dw{layer}", comm=cm(f"ffn_down_dw{layer}")))
        dup, dcw = _ffn_act_bwd(up, upc, da, small["f_conv_w"], layer=layer, tm=tm_e, name=f"ffn_act_bwd{layer}",
                                comm=cm(f"ffn_act_bwd{layer}"))
        dxin, dg = _mm_nt_norm(dup, wt("f_w_up", layer), xin, nf, dx, g_layer=layer, tm=tm_f,
                               name=f"ffn_up_dx{layer}", comm=cm(f"ffn_up_dx{layer}"))
        plan.grad_ready("f_w_up", layer, _mm_tn(h, dup, shards=N_CHIPS, tk=1024, tn=1408, tt=tt,
                                                name=f"ffn_up_dw{layer}", comm=cm(f"ffn_up_dw{layer}")))
        return dxin, dg, dcw

    dx, dnf1, dfc1 = ffn_bwd(dx, x3, h_f1, up1, upc1, a1, 1)
    dr = _mm_nt(dx, wt("c_w_out", 0), layer=0, tm=tm, tn=512, out_dtype=BF16, name="c_out_dx", comm=cm("c_out_dx"))
    plan.grad_ready("c_w_out", 0, _mm_tn(r, dx, shards=None, tk=1024, tn=1024, tt=tt, name="c_out_dw",
                                         comm=cm("c_out_dw")))
    dz_c, dccw = _mixer_c_bwd(z_c, dr, small["c_conv_w"], tm=tm_e, name="mixer_c_bwd", comm=cm("mixer_c_bwd"))
    sg["c_conv_w"] = dccw.reshape(1, C_CONV, D_MODEL)
    plan.grad_ready("c_w_in", 0, _mm_tn(h_m1, dz_c, shards=N_CHIPS, tk=1024, tn=768, tt=tt, name="c_in_dw",
                                        comm=cm("c_in_dw")))
    dx, dnm1 = _mm_nt_norm(dz_c, wt("c_w_in", 0), x2, nm, dx, g_layer=1, tm=tm_f, name="c_in_dx",
                           comm=cm("c_in_dx"))
    dx, dnf0, dfc0 = ffn_bwd(dx, x1, h_f0, up0, upc0, a0, 0)
    dyab = _mm_nt(dx, wt("ab_w_out", 0), layer=0, tm=tm, tn=512, out_dtype=BF16, name="ab_out_dx",
                  comm=cm("ab_out_dx"))
    plan.grad_ready("ab_w_out", 0, _mm_tn(yab, dx, shards=None, tk=1024, tn=1024, tt=tt, name="ab_out_dw",
                                          comm=cm("ab_out_dw")))
    (dza, dcb, sg["a_ln_g"], sg["a_ln_b"], dws, dbs, sg["b_ln_g"], sg["b_ln_b"]) = _mixer_ab_bwd_pre(
        z_ab, cb, dyab, small["a_ln_g"], small["a_ln_b"], w_s, b_s, small["b_ln_g"], small["b_ln_b"],
        tm=tm_e, name="mixer_ab_bwd", comm=cm("mixer_ab_bwd"))
    dz_ab, dbcw, sg["b_conv_b"] = _mixer_b_conv_bwd(z_ab, dcb, b_conv_w, dza, tm=tm_e, name="mixer_b_conv_bwd",
                                                    comm=cm("mixer_b_conv_bwd"))
    sg["a_w_s"] = dws.reshape(1, A_HEADS, CHUNK, CHUNK)
    sg["a_b_s"] = dbs.reshape(1, A_HEADS, CHUNK)
    sg["b_conv_w"] = dbcw.reshape(1, B_CONV, D_B)
    plan.grad_ready("ab_w_in", 0, _mm_tn(h_m0, dz_ab, shards=N_CHIPS, tk=1024, tn=512, tt=tt, name="ab_in_dw",
                                         comm=cm("ab_in_dw")))
    dx, dnm0 = _mm_nt_norm(dz_ab, wt("ab_w_in", 0), x, nm, dx, g_layer=0, tm=tm_f, name="ab_in_dx",
                           comm=cm("ab_in_dx"))

    sg["norm_mix"] = [dnm0, dnm1]
    sg["norm_ffn"] = [dnf0, dnf1]
    sg["f_conv_w"] = [dfc0, dfc1]
    return loss, dx, sg


BLOCK_BYTES = 3 * 1024 * 1024


BF16_SUBLANES = 16


def _row_tile(rows, row_bytes, step=SUBLANES):
    best = None
    for tr in range(step, rows + 1, step):
        if rows % tr == 0 and tr * row_bytes <= BLOCK_BYTES:
            best = tr
    if best is None:
        raise ValueError(f"no row tile for {rows}")
    return best


def _place_scalars():
    x, y, c = lax.axis_index("x"), lax.axis_index("y"), lax.axis_index("c")
    return jnp.stack([c, 2 * x + y, 2 * (1 - x) + y, 2 * x + (1 - y), 2 * (1 - x) + (1 - y)]).astype(jnp.int32)


def _cast_into_slot(w, place, *, layer, name):
    L, rows, cols = w.shape
    tr = _row_tile(rows, cols * 4, BF16_SUBLANES)

    def body(place_ref, w_ref, o_ref):
        o_ref[...] = w_ref[...].astype(BF16)

    return pl.pallas_call(
        body, name=name,
        grid_spec=pltpu.PrefetchScalarGridSpec(
            num_scalar_prefetch=1, grid=(rows // tr,),
            in_specs=[pl.BlockSpec((None, tr, cols), lambda i, p: (layer, i, 0))],
            out_specs=pl.BlockSpec((None, None, tr, cols), lambda i, p: (0, p[1], i, 0))),
        out_shape=jax.ShapeDtypeStruct((1, N_CHIPS, rows, cols), BF16),
        compiler_params=_cparams(("parallel",), 32),
    )(place, w)


def _pair_sum(g, theirs, place, *, name):
    S, rows, cols = g.shape
    half = rows // 2
    tr = _row_tile(half, cols * 4, BF16_SUBLANES)
    nb = half // tr

    def body(place_ref, g_ref, t_ref, o_ref):
        o_ref[...] = (g_ref[...] + t_ref[...]).astype(BF16)

    spec = pl.BlockSpec((None, tr, cols), lambda s, i, p: (s, i, 0))
    return pl.pallas_call(
        body, name=name,
        grid_spec=pltpu.PrefetchScalarGridSpec(
            num_scalar_prefetch=1, grid=(S, nb),
            in_specs=[pl.BlockSpec((None, tr, cols), lambda s, i, p: (s, p[0] * nb + i, 0)), spec],
            out_specs=spec),
        out_shape=jax.ShapeDtypeStruct((S, half, cols), BF16),
        compiler_params=_cparams(("parallel", "parallel"), 32),
    )(place, g, theirs)


def _chip_sum(p, r, g_prev, place, *, layer, shape, name):
    L, rows, cols = shape
    half = rows // 2
    tr = _row_tile(half, cols * 4, BF16_SUBLANES)
    nb = half // tr

    def body(place_ref, p_ref, r_ref, *rest):
        o_ref = rest[-1]
        mine = p_ref[...].astype(F32)
        peers = [r_ref[j].astype(F32) for j in range(3)]
        acc = None
        for s in range(N_CHIPS):
            term = jnp.where(place_ref[1] == s, mine,
                             jnp.where(place_ref[2] == s, peers[0],
                                       jnp.where(place_ref[3] == s, peers[1], peers[2])))
            acc = term if acc is None else acc + term
        o_ref[...] = acc

    in_specs = [pl.BlockSpec((None, tr, cols), lambda i, pr: (pr[1], i, 0)),
                pl.BlockSpec((3, tr, cols), lambda i, pr: (0, i, 0))]
    args = [place, p, r]
    aliases = {}
    if g_prev is not None:
        in_specs.append(ANY)
        args.append(g_prev)
        aliases = {3: 0}
    return pl.pallas_call(
        body, name=name,
        grid_spec=pltpu.PrefetchScalarGridSpec(
            num_scalar_prefetch=1, grid=(nb,), in_specs=in_specs,
            out_specs=pl.BlockSpec((None, tr, cols), lambda i, pr: (layer, pr[0] * nb + i, 0))),
        out_shape=jax.ShapeDtypeStruct(shape, F32), input_output_aliases=aliases,
        compiler_params=_cparams(("parallel",), 32),
    )(*args)


def _adamw_math(w, g, m, v):
    m2 = ADAM_B1 * m + (1.0 - ADAM_B1) * g
    v2 = ADAM_B2 * v + (1.0 - ADAM_B2) * (g * g)
    m_hat = m2 / (1.0 - ADAM_B1 ** ADAM_STEP)
    v_hat = v2 / (1.0 - ADAM_B2 ** ADAM_STEP)
    delta = -ADAM_LR * (m_hat / (jnp.sqrt(v_hat) + ADAM_EPS) + ADAM_WD * w)
    return delta, m2, v2


def _adamw(w, g, m, v, *, name):
    L, rows, cols = w.shape
    tr = _row_tile(rows, cols * 4)

    def body(w_ref, g_ref, m_ref, v_ref, d_ref, m2_ref, v2_ref):
        d, m2, v2 = _adamw_math(w_ref[...], g_ref[...], m_ref[...], v_ref[...])
        d_ref[...] = d
        m2_ref[...] = m2
        v2_ref[...] = v2

    spec = pl.BlockSpec((None, tr, cols), lambda l, i: (l, i, 0))
    shape = jax.ShapeDtypeStruct(w.shape, F32)
    return pl.pallas_call(
        body, name=name, grid=(L, rows // tr), in_specs=[spec] * 4, out_specs=[spec] * 3,
        out_shape=[shape] * 3,
        compiler_params=_cparams(("parallel", "parallel"), 48),
    )(w, g, m, v)


def _allreduce_pack(pack, *, name, comm):
    R = pack.shape[0]
    half = R // 2
    nr, nw = len(comm.reads), len(comm.writes)

    def body(*refs):
        p_ref, rd, wr_in = refs[0], refs[1:1 + nr], refs[1 + nr:1 + nr + nw]
        o_ref, wr_out = refs[1 + nr + nw], refs[2 + nr + nw:2 + nr + 2 * nw]
        sib_ref, chip_ref, parts_ref, sems, comm_sems = refs[2 + nr + 2 * nw:]
        src = dict(zip(comm.reads, rd))
        src.update(zip(comm.writes, wr_in))
        dst = dict(zip(comm.writes, wr_out))
        comm.start(src, dst, comm_sems)
        x, y, c, k, sib, peers = _place()
        swap = _remote(p_ref, sib_ref, sems.at[0, 0], sems.at[0, 1], sib)
        swap.start()
        swap.wait()
        chip_ref[...] = p_ref[...] + sib_ref[...]
        mine = chip_ref.at[pl.ds(pl.multiple_of(c * half, SUBLANES), half)]
        sends = [_remote(mine, parts_ref.at[j], sems.at[1 + j, 0], sems.at[1 + j, 1], (px, py, c))
                 for j, (px, py) in enumerate(peers)]
        for rc in sends:
            rc.start()
        for rc in sends:
            rc.wait()
        own = mine[...]
        others = [parts_ref[j] for j in range(3)]
        acc = None
        for s in range(N_CHIPS):
            term = own
            for j, (px, py) in enumerate(peers):
                term = jnp.where(2 * px + py == s, others[j], term)
            acc = term if acc is None else acc + term
        done = o_ref.at[pl.ds(pl.multiple_of(c * half, SUBLANES), half)]
        done[...] = acc
        theirs = o_ref.at[pl.ds(pl.multiple_of((1 - c) * half, SUBLANES), half)]
        share = _remote(done, done, sems.at[4, 0], sems.at[4, 1], sib)
        share.start()
        _remote(done, theirs, sems.at[4, 0], sems.at[4, 1], sib).wait()
        comm.finish(src, dst, comm_sems)

    vm = pl.BlockSpec(memory_space=pltpu.VMEM)
    operands, shapes = _comm_operands(comm)
    outs = pl.pallas_call(
        body, name=name, in_specs=[vm] + [ANY] * (nr + nw), out_specs=[vm] + [ANY] * nw,
        out_shape=[jax.ShapeDtypeStruct((R, LANES), F32)] + shapes,
        input_output_aliases={1 + nr + q: 1 + q for q in range(nw)},
        scratch_shapes=[pltpu.VMEM((R, LANES), F32), pltpu.VMEM((R, LANES), F32),
                        pltpu.VMEM((3, half, LANES), F32), pltpu.SemaphoreType.DMA((5, 2)),
                        pltpu.SemaphoreType.DMA((comm.ncopies, 2))],
        compiler_params=pltpu.CompilerParams(vmem_limit_bytes=VMEM_BYTES_MAX),
    )(pack, *operands)
    for q, n in enumerate(comm.writes):
        comm.plan.bufs[n] = outs[1 + q]
    return outs[0]


PACK_UNIT = SUBLANES * LANES


def _pack(arrays):
    flat, sizes = [], []
    for a in arrays:
        pieces = a if isinstance(a, (list, tuple)) else [a]
        v = jnp.concatenate([p.reshape(-1) for p in pieces]) if len(pieces) > 1 else pieces[0].reshape(-1)
        size = v.shape[0]
        padded = -(-size // PACK_UNIT) * PACK_UNIT
        flat.append(jnp.pad(v, (0, padded - size)))
        sizes.append((size, padded))
    total = sum(p for _, p in sizes)
    if (total // PACK_UNIT) % 2:
        flat.append(jnp.zeros((PACK_UNIT,), F32))
    return jnp.concatenate(flat).reshape(-1, LANES), sizes


def _unpack(pack, sizes, shapes):
    v = pack.reshape(-1)
    out, off = [], 0
    for (size, padded), shape in zip(sizes, shapes):
        out.append(v[off:off + size].reshape(shape))
        off += padded
    return out


BIG = ("ab_w_in", "ab_w_out", "c_w_in", "c_w_out", "f_w_up", "f_w_down")
COL_SHARDED = ("ab_w_in", "c_w_in", "f_w_up")
SMALL_REPLICATED = ("norm_mix", "norm_ffn", "norm_final", "a_ln_g", "a_ln_b", "a_w_s", "a_b_s",
                    "b_conv_b", "b_ln_g", "b_ln_b")
SMALL_SHARDED = ("b_conv_w", "c_conv_w", "f_conv_w")
SMALL = SMALL_REPLICATED + SMALL_SHARDED
ALL_WEIGHTS = ("norm_mix", "norm_ffn", "norm_final", "ab_w_in", "a_ln_g", "a_ln_b", "a_w_s", "a_b_s",
               "b_conv_w", "b_conv_b", "b_ln_g", "b_ln_b", "ab_w_out", "c_w_in", "c_conv_w", "c_w_out",
               "f_w_up", "f_conv_w", "f_w_down")


SCHEDULE = {
    "ab_in": [("gi", "f_w_up", 0, 0, 4), ("gi", "ab_w_out", 0)],
    "mixer_ab": [("gd", "f_w_up", 0, 0, 4), ("gd", "ab_w_out", 0), ("gi", "f_w_up", 0, 1, 4),
                 ("gi", "f_w_up", 0, 2, 4), ("gi", "f_w_up", 0, 3, 4)],
    "ab_out": [("gd", "f_w_up", 0, 1, 4), ("gd", "f_w_up", 0, 2, 4), ("gd", "f_w_up", 0, 3, 4)],
    "ffn_up0": [("gi", "f_w_down", 0), ("gi", "c_w_in", 0, 0, 2)],
    "ffn_act0": [("gd", "f_w_down", 0), ("gd", "c_w_in", 0, 0, 2), ("gi", "c_w_in", 0, 1, 2),
                 ("gi", "f_w_up", 1, 0, 4), ("gi", "f_w_up", 1, 1, 4)],
    "ffn_down0": [("gd", "c_w_in", 0, 1, 2), ("gd", "f_w_up", 1, 0, 4), ("gd", "f_w_up", 1, 1, 4),
                  ("gi", "f_w_up", 1, 2, 4)],
    "c_in": [("gd", "f_w_up", 1, 2, 4), ("gi", "f_w_up", 1, 3, 4), ("gi", "c_w_out", 0)],
    "mixer_c": [("gd", "f_w_up", 1, 3, 4), ("gd", "c_w_out", 0), ("gi", "f_w_down", 1, 0, 2)],
    "c_out": [("gd", "f_w_down", 1, 0, 2), ("gi", "f_w_down", 1, 1, 2)],
    "ffn_up1": [("gd", "f_w_down", 1, 1, 2)],
    "ffn_act_bwd1": [("px", "f_w_down", 1)],
    "ffn_up_dx1": [("cx", "f_w_down", 1)],
    "mixer_c_bwd": [("px", "f_w_up", 1), ("px", "c_w_out", 0)],
    "c_in_dw": [("cx", "f_w_up", 1, 0, 2), ("cx", "c_w_out", 0)],
    "c_in_dx": [("cx", "f_w_up", 1, 1, 2), ("px", "c_w_in", 0)],
    "ffn_down_dx0": [("cx", "c_w_in", 0, 0, 2)],
    "ffn_down_dw0": [("cx", "c_w_in", 0, 1, 2)],
    "ffn_act_bwd0": [("px", "f_w_down", 0)],
    "ffn_up_dx0": [("cx", "f_w_down", 0)],
    "mixer_ab_bwd": [("px", "f_w_up", 0), ("px", "ab_w_out", 0)],
    "mixer_b_conv_bwd": [("cx", "f_w_up", 0, 0, 2), ("cx", "ab_w_out", 0)],
    "ab_in_dw": [("cx", "f_w_up", 0, 1, 2)],
    "ab_in_dx": [("px", "ab_w_in", 0)],
}


class _Plan:
    def __init__(self, shapes, place):
        self.shapes, self.place, self.bufs = shapes, place, {}

    def weight(self, name, layer):
        g = self.bufs[f"w:{name}:{layer}"]
        if name in COL_SHARDED:
            return g
        _, S, rows, cols = g.shape
        return g.reshape(1, S * rows, cols)

    def grad_ready(self, name, layer, g):
        _, rows, cols = self.shapes[name]
        hbm = lambda a: pltpu.with_memory_space_constraint(a, pltpu.HBM)
        self.bufs[f"g:{name}:{layer}"] = g.reshape(N_CHIPS, rows, cols)
        self.bufs[f"t:{name}:{layer}"] = hbm(lax.empty((N_CHIPS, rows // 2, cols), F32))
        self.bufs[f"l:{name}:{layer}"] = hbm(lax.empty((3, rows // 2, cols), BF16))

    def job(self, kind, name, layer, part=0, parts=1):
        _, rows, cols = self.shapes[name]
        key = f"{name}:{layer}"
        if kind == "gi":
            return _job_gather_ici("w:" + key, rows, part, parts)
        if kind == "gd":
            return _job_gather_d2d("w:" + key, rows, part, parts)
        if kind == "px":
            return _job_pair_exchange("g:" + key, "t:" + key, rows)
        if kind == "cx":
            if "p:" + key not in self.bufs:
                self.bufs["p:" + key] = _pair_sum(self.bufs["g:" + key], self.bufs["t:" + key], self.place,
                                                  name=f"pair_sum_{name}{layer}")
            nr = rows // 2 // parts
            return _job_chip_exchange("p:" + key, "l:" + key, part * nr, nr)
        if kind == "ps":
            return _job_pair_share("G:" + name, layer, rows)
        raise ValueError(kind)

    def comm(self, call):
        specs = SCHEDULE.get(call)
        return None if specs is None else _Comm(self, [self.job(*spec) for spec in specs])


def _step(x, tgt, w, m, v):
    chip = 2 * lax.axis_index("x") + lax.axis_index("y")
    place = _place_scalars()
    plan = _Plan({n: w[n].shape for n in BIG}, place)
    items = [(n, l) for n in BIG for l in range(w[n].shape[0])]

    for n, l in items:
        plan.bufs[f"w:{n}:{l}"] = _cast_into_slot(w[n], place, layer=l, name=f"cast_{n}{l}")
    conv_pack, conv_sizes = _pack([w[n] for n in SMALL_SHARDED])
    hbm = lambda a: pltpu.with_memory_space_constraint(a, pltpu.HBM)
    plan.bufs["conv:mine"] = hbm(conv_pack)
    plan.bufs["conv:all"] = hbm(lax.empty((N_CHIPS,) + conv_pack.shape, F32))
    _comm_only(plan, [[plan.job("gi", "ab_w_in", 0), _job_chip_gather("conv:mine", "conv:all")],
                      [plan.job("gd", "ab_w_in", 0)]], name="gather_first")
    conv_shapes = [w[n].shape for n in SMALL_SHARDED]
    per_chip = [_unpack(plan.bufs["conv:all"][s], conv_sizes, conv_shapes) for s in range(N_CHIPS)]
    small = {n: w[n] for n in SMALL_REPLICATED}
    for idx, n in enumerate(SMALL_SHARDED):
        small[n] = jnp.concatenate([jnp.where(chip == s, w[n], per_chip[s][idx]) for s in range(N_CHIPS)], axis=-1)

    loss, dx, sg = _local_step(x, tgt, small, plan)

    g_pack, g_sizes = _pack([sg[n] for n in SMALL] + [loss])
    g_sum = _allreduce_pack(g_pack, name="allreduce_small_grads",
                            comm=_Comm(plan, [plan.job("cx", "ab_w_in", 0)]))
    full_shapes = [small[n].shape for n in SMALL]
    *summed, loss = _unpack(g_sum, g_sizes, full_shapes + [(1, 1)])
    g_small = dict(zip(SMALL, summed))
    for n in SMALL_SHARDED:
        width = w[n].shape[-1]
        g_small[n] = lax.dynamic_slice_in_dim(g_small[n], chip * width, width, axis=g_small[n].ndim - 1)

    for n, l in items:
        plan.bufs["G:" + n] = _chip_sum(plan.bufs[f"p:{n}:{l}"], plan.bufs[f"l:{n}:{l}"], plan.bufs.get("G:" + n),
                                        place, layer=l, shape=w[n].shape, name=f"chip_sum_{n}{l}")
    _comm_only(plan, [[plan.job("ps", n, l) for n, l in items]], name="reduce_pair_share")
    grads_big = [plan.bufs["G:" + n] for n in BIG]

    grad, delta, new_m, new_v = {}, {}, {}, {}
    for n, g in zip(BIG, grads_big):
        grad[n] = g
        delta[n], new_m[n], new_v[n] = _adamw(w[n], g, m[n], v[n], name=f"adamw_{n}")
    shapes = [w[n].shape for n in SMALL]
    wp, sizes = _pack([w[n] for n in SMALL])
    gp, _ = _pack([g_small[n] for n in SMALL])
    mp, _ = _pack([m[n] for n in SMALL])
    vp, _ = _pack([v[n] for n in SMALL])
    R = wp.shape[0]
    dp, m2p, v2p = _adamw(wp.reshape(1, R, LANES), gp.reshape(1, R, LANES), mp.reshape(1, R, LANES),
                          vp.reshape(1, R, LANES), name="adamw_small")
    for n, d_, m_, v_ in zip(SMALL, _unpack(dp, sizes, shapes), _unpack(m2p, sizes, shapes),
                             _unpack(v2p, sizes, shapes)):
        grad[n] = g_small[n]
        delta[n], new_m[n], new_v[n] = d_, m_, v_
    return loss, dx, grad, delta, new_m, new_v


def kernel(x, norm_mix, norm_ffn, norm_final, ab_w_in, a_ln_g, a_ln_b, a_w_s, a_b_s, b_conv_w, b_conv_b, b_ln_g, b_ln_b, ab_w_out, c_w_in, c_conv_w, c_w_out, f_w_up, f_conv_w, f_w_down, loss_target, m_norm_mix, m_norm_ffn, m_norm_final, m_ab_w_in, m_a_ln_g, m_a_ln_b, m_a_w_s, m_a_b_s, m_b_conv_w, m_b_conv_b, m_b_ln_g, m_b_ln_b, m_ab_w_out, m_c_w_in, m_c_conv_w, m_c_w_out, m_f_w_up, m_f_conv_w, m_f_w_down, v_norm_mix, v_norm_ffn, v_norm_final, v_ab_w_in, v_a_ln_g, v_a_ln_b, v_a_w_s, v_a_b_s, v_b_conv_w, v_b_conv_b, v_b_ln_g, v_b_ln_b, v_ab_w_out, v_c_w_in, v_c_conv_w, v_c_w_out, v_f_w_up, v_f_conv_w, v_f_w_down):
    given = dict(locals())
    w = {n: given[n] for n in ALL_WEIGHTS}
    m = {n: given["m_" + n] for n in ALL_WEIGHTS}
    v = {n: given["v_" + n] for n in ALL_WEIGHTS}
    T = x.shape[1]
    loss, dx, grad, delta, new_m, new_v = _step(x.reshape(T, D_MODEL), loss_target.reshape(T, D_MODEL), w, m, v)
    out = [loss[0, 0], dx.reshape(x.shape)]
    for d in (grad, delta, new_m, new_v):
        out += [d[n] for n in ALL_WEIGHTS]
    return tuple(out)
```

```python
import functools
import math

import jax
import jax.numpy as jnp
from jax import lax
from jax.experimental import pallas as pl
from jax.experimental.pallas import tpu as pltpu

F32 = jnp.float32
BF16 = jnp.bfloat16

EPS = 1e-6
D_MODEL = 1024
CHUNK = 128
HEAD_DIM = 128
A_HEADS = 4
D_A = 512
D_B = 512
B_CONV = 31
C_CONV = 3
D_FF = 2816
F_CONV = 3
N_CHIPS = 4

ADAM_LR = 0.001
ADAM_B1 = 0.9
ADAM_B2 = 0.999
ADAM_EPS = 1e-08
ADAM_WD = 0.01
ADAM_STEP = 10

SUBLANES = 8
LANES = 128
HALO_SHORT = 16
HALO_LONG = 32
VMEM_BYTES_MAX = 60000 * 1024

INV_SQRT2 = 1.0 / math.sqrt(2.0)
INV_SQRT_2PI = 1.0 / math.sqrt(2.0 * math.pi)

MESH = pl.DeviceIdType.MESH


def _cparams(sem, vmem_mb):
    del vmem_mb
    return pltpu.CompilerParams(dimension_semantics=sem, vmem_limit_bytes=VMEM_BYTES_MAX)


def _pick(total, pref):
    for c in (2048, 1024, 512, 256, 128):
        if c <= pref and total % c == 0:
            return c
    raise ValueError(f"no tile for {total}")


def _sigmoid(x):
    return jax.nn.sigmoid(x)


def _silu(x):
    return x * _sigmoid(x)


def _dsilu(x):
    s = _sigmoid(x)
    return s * (1.0 + x * (1.0 - s))


def _gelu(x):
    return 0.5 * x * (1.0 + lax.erf(x * INV_SQRT2))


def _dgelu(x):
    return 0.5 * (1.0 + lax.erf(x * INV_SQRT2)) + x * jnp.exp(-0.5 * x * x) * INV_SQRT_2PI


def _ln_stats(x):
    mu = jnp.mean(x, axis=-1, keepdims=True)
    xc = x - mu
    var = jnp.mean(xc * xc, axis=-1, keepdims=True)
    r = lax.rsqrt(var + EPS)
    return xc * r, r


def _ln_bwd(dy, xh, r, g):
    dxh = dy * g
    m1 = jnp.mean(dxh, axis=-1, keepdims=True)
    m2 = jnp.mean(dxh * xh, axis=-1, keepdims=True)
    return r * (dxh - m1 - xh * m2)


def _rowsum(x):
    return jnp.sum(x, axis=0, keepdims=True)


ANY = pl.BlockSpec(memory_space=pltpu.HBM)


def _place():
    x, y, c = lax.axis_index("x"), lax.axis_index("y"), lax.axis_index("c")
    peers = [(1 - x, y), (x, 1 - y), (1 - x, 1 - y)]
    return x, y, c, 2 * x + y, (x, y, 1 - c), peers


def _half(rows, which):
    return pl.ds(which * (rows // 2), rows // 2)


def _remote(src, dst, send_sem, recv_sem, device):
    return pltpu.make_async_remote_copy(src_ref=src, dst_ref=dst, send_sem=send_sem, recv_sem=recv_sem,
                                        device_id=device, device_id_type=MESH)


class _Job:
    def __init__(self, reads, writes, ncopies, copies):
        self.reads, self.writes, self.ncopies, self.copies = reads, writes, ncopies, copies


def _share(rows, which, part, parts):
    nr = rows // 2 // parts
    return pl.ds(which * (rows // 2) + part * nr, nr)


def _job_gather_ici(name, rows, part, parts):
    def copies(src, dst, sem):
        x, y, c, k, sib, peers = _place()
        mine_rows = _share(rows, c, part, parts)
        out = []
        for j, (px, py) in enumerate(peers):
            mine = src[name].at[0, k, mine_rows]
            out.append((_remote(mine, dst[name].at[0, k, mine_rows], sem(j, 0), sem(j, 1), (px, py, c)),
                        _remote(mine, dst[name].at[0, 2 * px + py, mine_rows], sem(j, 0), sem(j, 1), (px, py, c))))
        return out
    return _Job([], [name], 3, copies)


def _job_gather_d2d(name, rows, part, parts):
    def copies(src, dst, sem):
        x, y, c, k, sib, peers = _place()
        out = []
        for j, (px, py) in enumerate(peers):
            landed = src[name].at[0, 2 * px + py, _share(rows, c, part, parts)]
            out.append((_remote(landed, dst[name].at[0, 2 * px + py, _share(rows, c, part, parts)],
                                sem(j, 0), sem(j, 1), sib),
                        _remote(landed, dst[name].at[0, 2 * px + py, _share(rows, 1 - c, part, parts)],
                                sem(j, 0), sem(j, 1), sib)))
        return out
    return _Job([], [name], 3, copies)


def _job_chip_gather(sname, dname):
    def copies(src, dst, sem):
        x, y, c, k, sib, peers = _place()
        return [(_remote(src[sname], dst[dname].at[k], sem(j, 0), sem(j, 1), (px, py, c)),
                 _remote(src[sname], dst[dname].at[2 * px + py], sem(j, 0), sem(j, 1), (px, py, c)))
                for j, (px, py) in enumerate(peers)]
    return _Job([sname], [dname], 3, copies)


def _job_pair_exchange(gname, tname, rows):
    def copies(src, dst, sem):
        x, y, c, k, sib, peers = _place()
        cp = _remote(src[gname].at[:, _half(rows, 1 - c), :], dst[tname], sem(0, 0), sem(0, 1), sib)
        return [(cp, cp)]
    return _Job([gname], [tname], 1, copies)


def _job_chip_exchange(pname, lname, r0, nr):
    def copies(src, dst, sem):
        x, y, c, k, sib, peers = _place()
        out = []
        for j, (px, py) in enumerate(peers):
            cp = _remote(src[pname].at[2 * px + py, pl.ds(r0, nr)], dst[lname].at[j, pl.ds(r0, nr)],
                         sem(j, 0), sem(j, 1), (px, py, c))
            out.append((cp, cp))
        return out
    return _Job([pname], [lname], 3, copies)


def _job_pair_share(name, layer, rows):
    def copies(src, dst, sem):
        x, y, c, k, sib, peers = _place()
        mine = src[name].at[layer, _half(rows, c)]
        return [(_remote(mine, dst[name].at[layer, _half(rows, c)], sem(0, 0), sem(0, 1), sib),
                 _remote(mine, dst[name].at[layer, _half(rows, 1 - c)], sem(0, 0), sem(0, 1), sib))]
    return _Job([], [name], 1, copies)


class _Comm:
    def __init__(self, plan, jobs):
        self.plan, self.jobs = plan, jobs
        self.writes, self.reads = [], []
        for job in jobs:
            for n in job.writes:
                if n not in self.writes:
                    self.writes.append(n)
        for job in jobs:
            for n in job.reads:
                if n not in self.writes and n not in self.reads:
                    self.reads.append(n)
        self.ncopies = sum(job.ncopies for job in jobs)

    def descriptors(self, src, dst, sems, base):
        out = []
        for job in self.jobs:
            sem = lambda j, which, base=base: sems.at[base + j, which]
            out += job.copies(src, dst, sem)
            base += job.ncopies
        return out

    def start(self, src, dst, sems, base=0):
        for first, _ in self.descriptors(src, dst, sems, base):
            first.start()

    def finish(self, src, dst, sems, base=0):
        for _, landed in self.descriptors(src, dst, sems, base):
            landed.wait()


def _comm_operands(comm):
    bufs = comm.plan.bufs
    shapes = [jax.ShapeDtypeStruct(bufs[n].shape, bufs[n].dtype) for n in comm.writes]
    return [bufs[n] for n in comm.reads] + [bufs[n] for n in comm.writes], shapes


def _pallas(comm, body, *, name, grid, in_specs, out_specs, out_shape, compiler_params, scratch_shapes=(),
            aliases=None):
    aliases = dict(aliases or {})
    if comm is None:
        return pl.pallas_call(body, name=name, grid=grid, in_specs=in_specs, out_specs=out_specs,
                              out_shape=out_shape, scratch_shapes=list(scratch_shapes),
                              input_output_aliases=aliases, compiler_params=compiler_params)
    single = not isinstance(out_shape, (list, tuple))
    base_specs = [out_specs] if single else list(out_specs)
    base_shape = [out_shape] if single else list(out_shape)
    nb, nr, nw, nbo, nsc = len(in_specs), len(comm.reads), len(comm.writes), len(base_specs), len(scratch_shapes)

    def wrapped(*refs):
        base_in, rd, wr_in = refs[:nb], refs[nb:nb + nr], refs[nb + nr:nb + nr + nw]
        o0 = nb + nr + nw
        base_out, wr_out = refs[o0:o0 + nbo], refs[o0 + nbo:o0 + nbo + nw]
        scratch, sems = refs[o0 + nbo + nw:o0 + nbo + nw + nsc], refs[-1]
        src = dict(zip(comm.reads, rd))
        src.update(zip(comm.writes, wr_in))
        dst = dict(zip(comm.writes, wr_out))
        first = functools.reduce(jnp.logical_and, [pl.program_id(a) == 0 for a in range(len(grid))])
        last = functools.reduce(jnp.logical_and,
                                [pl.program_id(a) == pl.num_programs(a) - 1 for a in range(len(grid))])

        @pl.when(first)
        def _():
            comm.start(src, dst, sems)
        body(*base_in, *base_out, *scratch)

        @pl.when(last)
        def _():
            comm.finish(src, dst, sems)

    operands, shapes = _comm_operands(comm)
    call = pl.pallas_call(
        wrapped, name=name, grid=grid, in_specs=list(in_specs) + [ANY] * (nr + nw),
        out_specs=base_specs + [ANY] * nw, out_shape=base_shape + shapes,
        input_output_aliases={**aliases, **{nb + nr + q: nbo + q for q in range(nw)}},
        scratch_shapes=list(scratch_shapes) + [pltpu.SemaphoreType.DMA((comm.ncopies, 2))],
        compiler_params=compiler_params)

    def run(*args):
        outs = call(*args, *operands)
        for q, n in enumerate(comm.writes):
            comm.plan.bufs[n] = outs[nbo + q]
        return outs[0] if single else list(outs[:nbo])

    return run


def _comm_only(plan, phases, *, name):
    comms = [_Comm(plan, jobs) for jobs in phases]
    both = _Comm(plan, [job for jobs in phases for job in jobs])
    nr, nw = len(both.reads), len(both.writes)

    def body(*refs):
        rd, wr_in, wr_out, sems = refs[:nr], refs[nr:nr + nw], refs[nr + nw:nr + 2 * nw], refs[-1]
        src = dict(zip(both.reads, rd))
        src.update(zip(both.writes, wr_in))
        dst = dict(zip(both.writes, wr_out))
        base = 0
        for comm in comms:
            comm.start(src, dst, sems, base)
            comm.finish(src, dst, sems, base)
            base += comm.ncopies

    operands, shapes = _comm_operands(both)
    outs = pl.pallas_call(
        body, name=name, in_specs=[ANY] * (nr + nw), out_specs=[ANY] * nw, out_shape=shapes,
        input_output_aliases={nr + q: q for q in range(nw)},
        scratch_shapes=[pltpu.SemaphoreType.DMA((both.ncopies, 2))],
    )(*operands)
    for q, n in enumerate(both.writes):
        plan.bufs[n] = outs[q]


def _mm_nn(a, w, *, layer, tm, tn, residual=None, norm=None, out_dtype=F32, name, comm=None):
    T, K = a.shape
    if w.ndim == 4:
        _, S, _, n4 = w.shape
        N = S * n4
        bps = n4 // tn
        w_spec = pl.BlockSpec((None, None, K, tn), lambda j, i: (layer, j // bps, 0, j % bps))
    else:
        N = w.shape[2]
        w_spec = pl.BlockSpec((None, K, tn), lambda j, i: (layer, 0, j))
    in_specs = [pl.BlockSpec((tm, K), lambda j, i: (i, 0)), w_spec]
    args = [a, w]
    if residual is not None:
        in_specs.append(pl.BlockSpec((tm, tn), lambda j, i: (i, j)))
        args.append(residual)
    out_specs = pl.BlockSpec((tm, tn), lambda j, i: (i, j))
    out_shape = jax.ShapeDtypeStruct((T, N), out_dtype)
    if norm is not None:
        assert tn == N
        g, norm_layer = norm
        in_specs.append(pl.BlockSpec((None, 1, N), lambda j, i: (norm_layer, 0, 0)))
        args.append(g)
        out_specs = [out_specs, pl.BlockSpec((tm, tn), lambda j, i: (i, j))]
        out_shape = [out_shape, jax.ShapeDtypeStruct((T, N), BF16)]

    def body(*refs):
        a_ref, w_ref = refs[0], refs[1]
        acc = jnp.dot(a_ref[...].astype(BF16), w_ref[...], preferred_element_type=F32)
        if residual is not None:
            acc = refs[2][...] + acc
        if norm is None:
            refs[-1][...] = acc.astype(out_dtype)
        else:
            refs[-2][...] = acc.astype(out_dtype)
            r = lax.rsqrt(jnp.mean(acc * acc, axis=-1, keepdims=True) + EPS)
            refs[-1][...] = (acc * r * refs[-3][...]).astype(BF16)

    return _pallas(
        comm, body, name=name, grid=(N // tn, T // tm), in_specs=in_specs,
        out_specs=out_specs, out_shape=out_shape,
        compiler_params=_cparams(("parallel", "parallel"), 48),
    )(*args)


def _mm_nt(dy, w, *, layer, tm, tn, name, out_dtype=F32, comm=None):
    T = dy.shape[0]
    nt_dims = (((1,), (1,)), ((), ()))
    _, R, N = w.shape

    def body2(dy_ref, w_ref, o_ref):
        o_ref[...] = lax.dot_general(dy_ref[...].astype(BF16), w_ref[...], nt_dims,
                                     preferred_element_type=F32).astype(out_dtype)

    return _pallas(
        comm, body2, name=name, grid=(R // tn, T // tm),
        in_specs=[pl.BlockSpec((tm, N), lambda j, i: (i, 0)),
                  pl.BlockSpec((None, tn, N), lambda j, i: (layer, j, 0))],
        out_specs=pl.BlockSpec((tm, tn), lambda j, i: (i, j)),
        out_shape=jax.ShapeDtypeStruct((T, R), out_dtype),
        compiler_params=_cparams(("parallel", "parallel"), 48),
    )(dy, w)


def _mm_tn(a, dy, *, shards, tk, tn, tt, name, comm=None):
    T, K = a.shape
    N = dy.shape[1]
    tn_dims = (((0,), (0,)), ((), ()))

    def body(a_ref, dy_ref, o_ref):
        @pl.when(pl.program_id(2) == 0)
        def _():
            o_ref[...] = jnp.zeros_like(o_ref)
        o_ref[...] += lax.dot_general(a_ref[...].astype(BF16), dy_ref[...].astype(BF16), tn_dims,
                                      preferred_element_type=F32)

    if shards is None:
        out_spec = pl.BlockSpec((tk, tn), lambda k, n, t: (k, n))
        out_shape = jax.ShapeDtypeStruct((K, N), F32)
    else:
        n4 = N // shards
        bps = n4 // tn
        out_spec = pl.BlockSpec((None, tk, tn), lambda k, n, t: (n // bps, k, n % bps))
        out_shape = jax.ShapeDtypeStruct((shards, K, n4), F32)
    return _pallas(
        comm, body, name=name, grid=(K // tk, N // tn, T // tt),
        in_specs=[pl.BlockSpec((tt, tk), lambda k, n, t: (t, k)),
                  pl.BlockSpec((tt, tn), lambda k, n, t: (t, n))],
        out_specs=out_spec, out_shape=out_shape,
        compiler_params=_cparams(("parallel", "parallel", "arbitrary"), 48),
    )(a, dy)


def _rmsnorm_fwd(x, g, *, layer, tm, name, comm=None):
    T, D = x.shape

    def body(x_ref, g_ref, h_ref):
        xf = x_ref[...]
        r = lax.rsqrt(jnp.mean(xf * xf, axis=-1, keepdims=True) + EPS)
        h_ref[...] = (xf * r * g_ref[...]).astype(BF16)

    return _pallas(
        comm, body, name=name, grid=(T // tm,),
        in_specs=[pl.BlockSpec((tm, D), lambda i: (i, 0)),
                  pl.BlockSpec((None, 1, D), lambda i: (layer, 0, 0))],
        out_specs=pl.BlockSpec((tm, D), lambda i: (i, 0)),
        out_shape=jax.ShapeDtypeStruct((T, D), BF16),
        compiler_params=_cparams(("parallel",), 32),
    )(x, g)


def _rmsnorm_bwd_math(xf, g, dh, dres):
    r = lax.rsqrt(jnp.mean(xf * xf, axis=-1, keepdims=True) + EPS)
    xh = xf * r
    dxh = dh * g
    dx = dres + r * (dxh - xh * jnp.mean(dxh * xh, axis=-1, keepdims=True))
    return dx, _rowsum(dh * xh)


def _mm_nt_norm(dy, w, x, g, dres, *, g_layer, tm, name, comm=None):
    T = dy.shape[0]
    _, S, K, n4 = w.shape
    nt_dims = (((1,), (1,)), ((), ()))

    def body(dy_ref, w_ref, x_ref, g_ref, dres_ref, dx_ref, dg_ref):
        @pl.when(pl.program_id(0) == 0)
        def _():
            dg_ref[...] = jnp.zeros_like(dg_ref)
        dh = None
        for s in range(S):
            part = lax.dot_general(dy_ref[:, s * n4:(s + 1) * n4].astype(BF16), w_ref[s], nt_dims,
                                   preferred_element_type=F32)
            dh = part if dh is None else dh + part
        dx, dg = _rmsnorm_bwd_math(x_ref[...], g_ref[...], dh, dres_ref[...])
        dx_ref[...] = dx
        dg_ref[...] += dg

    row = lambda i: (i, 0)
    return _pallas(
        comm, body, name=name, grid=(T // tm,),
        in_specs=[pl.BlockSpec((tm, S * n4), row),
                  pl.BlockSpec((None, S, K, n4), lambda i: (0, 0, 0, 0)),
                  pl.BlockSpec((tm, K), row),
                  pl.BlockSpec((None, 1, K), lambda i: (g_layer, 0, 0)),
                  pl.BlockSpec((tm, K), row)],
        out_specs=[pl.BlockSpec((tm, K), row), pl.BlockSpec((1, K), lambda i: (0, 0))],
        out_shape=[jax.ShapeDtypeStruct((T, K), F32), jax.ShapeDtypeStruct((1, K), F32)],
        compiler_params=_cparams(("arbitrary",), 56),
    )(dy, w, x, g, dres)


def _loss_head(x, tgt, g, *, tm, name, comm=None):
    T, D = x.shape

    def body(x_ref, t_ref, g_ref, loss_ref, dx_ref, dg_ref):
        @pl.when(pl.program_id(0) == 0)
        def _():
            dg_ref[...] = jnp.zeros_like(dg_ref)
            loss_ref[...] = jnp.zeros_like(loss_ref)
        xf = x_ref[...]
        gg = g_ref[...]
        r = lax.rsqrt(jnp.mean(xf * xf, axis=-1, keepdims=True) + EPS)
        xh = xf * r
        err = xh * gg - t_ref[...]
        row = jnp.mean(err * err, axis=-1, keepdims=True)
        loss_ref[...] += 0.5 * jnp.sum(row, axis=0, keepdims=True)
        dy = err * (1.0 / D)
        dg_ref[...] += _rowsum(dy * xh)
        dxh = dy * gg
        dx_ref[...] = r * (dxh - xh * jnp.mean(dxh * xh, axis=-1, keepdims=True))

    return _pallas(
        comm, body, name=name, grid=(T // tm,),
        in_specs=[pl.BlockSpec((tm, D), lambda i: (i, 0)),
                  pl.BlockSpec((tm, D), lambda i: (i, 0)),
                  pl.BlockSpec((1, D), lambda i: (0, 0))],
        out_specs=[pl.BlockSpec((1, 1), lambda i: (0, 0)),
                   pl.BlockSpec((tm, D), lambda i: (i, 0)),
                   pl.BlockSpec((1, D), lambda i: (0, 0))],
        out_shape=[jax.ShapeDtypeStruct((1, 1), F32), jax.ShapeDtypeStruct((T, D), F32),
                   jax.ShapeDtypeStruct((1, D), F32)],
        compiler_params=_cparams(("arbitrary",), 40),
    )(x, tgt, g)


CONV_ROWS = 64
CONV_COLS = 256


def _halo_prev_index(tm, halo):
    per = tm // halo
    return lambda i: jnp.maximum(i * per - 1, 0)


def _halo_next_index(tm, halo, total):
    per = tm // halo
    last = total // halo - 1
    return lambda i: jnp.minimum((i + 1) * per, last)


def _causal_mask():
    t = lax.broadcasted_iota(jnp.int32, (CHUNK, CHUNK), 0)
    s = lax.broadcasted_iota(jnp.int32, (CHUNK, CHUNK), 1)
    return s <= t


def _mixer_ab_fwd(z, a_ln_g, a_ln_b, w_s, b_s, conv_w, conv_b, b_ln_g, b_ln_b, *, tm, name, comm=None):
    T = z.shape[0]
    nchunk = tm // CHUNK
    halo = HALO_LONG

    def body(za_ref, zb_ref, zh_ref, alg_ref, alb_ref, ws_ref, bs_ref, cw_ref, cbias_ref,
             blg_ref, blb_ref, y_ref, cb_ref, ext_ref):
        i = pl.program_id(0)
        gu = _gelu(za_ref[:, :D_A].astype(F32))
        gv = _gelu(za_ref[:, D_A:].astype(F32))
        xh, _ = _ln_stats(gv)
        lv = (xh * alg_ref[...] + alb_ref[...]).astype(BF16)
        mask = _causal_mask()
        for h in range(A_HEADS):
            wm = jnp.where(mask, ws_ref[h], 0.0).astype(BF16)
            cols = slice(h * HEAD_DIM, (h + 1) * HEAD_DIM)
            for c in range(nchunk):
                rows = slice(c * CHUNK, (c + 1) * CHUNK)
                mixed = jnp.dot(wm, lv[rows, cols], preferred_element_type=F32) + bs_ref[h]
                y_ref[rows, cols] = (gu[rows, cols] * mixed).astype(BF16)
        ext_ref[halo:halo + tm, :] = zb_ref[:, :D_B].astype(F32) * _sigmoid(zb_ref[:, D_B:].astype(F32))
        prev = zh_ref[:, :D_B].astype(F32) * _sigmoid(zh_ref[:, D_B:].astype(F32))
        ext_ref[0:halo, :] = jnp.where(i > 0, prev, 0.0)
        for rb in range(tm // CONV_ROWS):
            for cb in range(D_B // CONV_COLS):
                cs = slice(cb * CONV_COLS, (cb + 1) * CONV_COLS)
                window = ext_ref[rb * CONV_ROWS:rb * CONV_ROWS + CONV_ROWS + halo, cs]
                acc = jnp.zeros((CONV_ROWS, CONV_COLS), F32)
                for k in range(B_CONV):
                    shifted = _rows_after(window, halo - (B_CONV - 1) + k)[:CONV_ROWS]
                    acc = acc + cw_ref[k:k + 1, cs] * shifted
                cb_ref[rb * CONV_ROWS:(rb + 1) * CONV_ROWS, cs] = acc + cbias_ref[:, cs]
        xhb, _ = _ln_stats(cb_ref[...])
        y_ref[:, D_A:] = _silu(xhb * blg_ref[...] + blb_ref[...]).astype(BF16)

    row = lambda i: (i, 0)
    par = lambda i: (0, 0)
    return _pallas(
        comm, body, name=name, grid=(T // tm,),
        in_specs=[pl.BlockSpec((tm, 2 * D_A), lambda i: (i, 0)),
                  pl.BlockSpec((tm, 2 * D_B), lambda i: (i, 1)),
                  pl.BlockSpec((halo, 2 * D_B), lambda i: (_halo_prev_index(tm, halo)(i), 1)),
                  pl.BlockSpec((1, D_A), par), pl.BlockSpec((1, D_A), par),
                  pl.BlockSpec((A_HEADS, CHUNK, CHUNK), lambda i: (0, 0, 0)),
                  pl.BlockSpec((A_HEADS, CHUNK, 1), lambda i: (0, 0, 0)),
                  pl.BlockSpec((B_CONV, D_B), par), pl.BlockSpec((1, D_B), par),
                  pl.BlockSpec((1, D_B), par), pl.BlockSpec((1, D_B), par)],
        out_specs=[pl.BlockSpec((tm, D_A + D_B), row), pl.BlockSpec((tm, D_B), row)],
        out_shape=[jax.ShapeDtypeStruct((T, D_A + D_B), BF16), jax.ShapeDtypeStruct((T, D_B), F32)],
        scratch_shapes=[pltpu.VMEM((halo + tm, D_B), F32)],
        compiler_params=_cparams(("parallel",), 40),
    )(z, z, z, a_ln_g, a_ln_b, w_s, b_s, conv_w, conv_b, b_ln_g, b_ln_b)


def _mixer_ab_bwd_pre(z, cb, dy, a_ln_g, a_ln_b, w_s, b_s, b_ln_g, b_ln_b, *, tm, name, comm=None):
    T = z.shape[0]
    nchunk = tm // CHUNK
    tn_dims = (((0,), (0,)), ((), ()))
    nt_dims = (((1,), (1,)), ((), ()))

    def body(za_ref, cb_ref, dy_ref, alg_ref, alb_ref, ws_ref, bs_ref, blg_ref, blb_ref,
             dza_ref, dcb_ref, dalg_ref, dalb_ref, dws_ref, dbs_ref, dblg_ref, dblb_ref,
             dlv_ref):
        @pl.when(pl.program_id(0) == 0)
        def _():
            for ref in (dalg_ref, dalb_ref, dws_ref, dbs_ref, dblg_ref, dblb_ref):
                ref[...] = jnp.zeros_like(ref)
        ua = za_ref[:, :D_A].astype(F32)
        va = za_ref[:, D_A:].astype(F32)
        gu = _gelu(ua)
        gv = _gelu(va)
        xh, r = _ln_stats(gv)
        alg = alg_ref[...]
        lv = (xh * alg + alb_ref[...]).astype(BF16)
        dya = dy_ref[:, :D_A].astype(F32)
        mask = _causal_mask()
        for h in range(A_HEADS):
            wm = jnp.where(mask, ws_ref[h], 0.0).astype(BF16)
            cols = slice(h * HEAD_DIM, (h + 1) * HEAD_DIM)
            dwm = jnp.zeros((CHUNK, CHUNK), F32)
            dbs = jnp.zeros((CHUNK, 1), F32)
            for c in range(nchunk):
                rows = slice(c * CHUNK, (c + 1) * CHUNK)
                lvb = lv[rows, cols]
                mixed = jnp.dot(wm, lvb, preferred_element_type=F32) + bs_ref[h]
                dyb = dya[rows, cols]
                dza_ref[rows, cols] = (dyb * mixed * _dgelu(ua[rows, cols])).astype(BF16)
                dmixed = dyb * gu[rows, cols]
                dmb = dmixed.astype(BF16)
                dlv_ref[rows, cols] = lax.dot_general(wm, dmb, tn_dims, preferred_element_type=F32)
                dwm = dwm + lax.dot_general(dmb, lvb, nt_dims, preferred_element_type=F32)
                dbs = dbs + jnp.sum(dmixed, axis=1, keepdims=True)
            dws_ref[h] += jnp.where(mask, dwm, 0.0)
            dbs_ref[h] += dbs
        dlv = dlv_ref[...]
        dalg_ref[...] += _rowsum(dlv * xh)
        dalb_ref[...] += _rowsum(dlv)
        dgv = _ln_bwd(dlv, xh, r, alg)
        dza_ref[:, D_A:] = (dgv * _dgelu(va)).astype(BF16)
        xhb, rb = _ln_stats(cb_ref[...])
        blg = blg_ref[...]
        lb = xhb * blg + blb_ref[...]
        dlb = dy_ref[:, D_A:].astype(F32) * _dsilu(lb)
        dblg_ref[...] += _rowsum(dlb * xhb)
        dblb_ref[...] += _rowsum(dlb)
        dcb_ref[...] = _ln_bwd(dlb, xhb, rb, blg)

    row = lambda i: (i, 0)
    par = lambda i: (0, 0)
    par3 = lambda i: (0, 0, 0)
    return _pallas(
        comm, body, name=name, grid=(T // tm,),
        in_specs=[pl.BlockSpec((tm, 2 * D_A), row), pl.BlockSpec((tm, D_B), row),
                  pl.BlockSpec((tm, D_A + D_B), row),
                  pl.BlockSpec((1, D_A), par), pl.BlockSpec((1, D_A), par),
                  pl.BlockSpec((A_HEADS, CHUNK, CHUNK), par3),
                  pl.BlockSpec((A_HEADS, CHUNK, 1), par3),
                  pl.BlockSpec((1, D_B), par), pl.BlockSpec((1, D_B), par)],
        out_specs=[pl.BlockSpec((tm, 2 * D_A), row), pl.BlockSpec((tm, D_B), row),
                   pl.BlockSpec((1, D_A), par), pl.BlockSpec((1, D_A), par),
                   pl.BlockSpec((A_HEADS, CHUNK, CHUNK), par3),
                   pl.BlockSpec((A_HEADS, CHUNK, 1), par3),
                   pl.BlockSpec((1, D_B), par), pl.BlockSpec((1, D_B), par)],
        out_shape=[jax.ShapeDtypeStruct((T, 2 * D_A + 2 * D_B), BF16), jax.ShapeDtypeStruct((T, D_B), F32),
                   jax.ShapeDtypeStruct((1, D_A), F32), jax.ShapeDtypeStruct((1, D_A), F32),
                   jax.ShapeDtypeStruct((A_HEADS, CHUNK, CHUNK), F32),
                   jax.ShapeDtypeStruct((A_HEADS, CHUNK, 1), F32),
                   jax.ShapeDtypeStruct((1, D_B), F32), jax.ShapeDtypeStruct((1, D_B), F32)],
        scratch_shapes=[pltpu.VMEM((tm, D_A), F32)],
        compiler_params=_cparams(("arbitrary",), 40),
    )(z, cb, dy, a_ln_g, a_ln_b, w_s, b_s, b_ln_g, b_ln_b)


def _mixer_b_conv_bwd(z, dcb, conv_w, dz, *, tm, name, comm=None):
    T = z.shape[0]
    halo = HALO_LONG

    def body(zb_ref, dcb_ref, dcn_ref, cw_ref, dz_in_ref, dzb_ref, dcw_ref, dbias_ref, dext_ref):
        i = pl.program_id(0)
        last = pl.num_programs(0) - 1

        @pl.when(i == 0)
        def _():
            dcw_ref[...] = jnp.zeros_like(dcw_ref)
            dbias_ref[...] = jnp.zeros_like(dbias_ref)
        dcb = dcb_ref[...]
        dext_ref[0:tm, :] = dcb
        dext_ref[tm:tm + halo, :] = jnp.where(i < last, dcn_ref[...], 0.0)
        dbias_ref[...] += _rowsum(dcb)
        for rb in range(tm // CONV_ROWS):
            for cb in range(D_B // CONV_COLS):
                cs = slice(cb * CONV_COLS, (cb + 1) * CONV_COLS)
                gcs = slice(D_B + cb * CONV_COLS, D_B + (cb + 1) * CONV_COLS)
                rs = slice(rb * CONV_ROWS, (rb + 1) * CONV_ROWS)
                xbb = zb_ref[rs, cs].astype(F32)
                sgb = _sigmoid(zb_ref[rs, gcs].astype(F32))
                yb0 = xbb * sgb
                window = dext_ref[rb * CONV_ROWS:rb * CONV_ROWS + CONV_ROWS + halo, cs]
                acc = jnp.zeros((CONV_ROWS, CONV_COLS), F32)
                for k in range(B_CONV):
                    shifted = _rows_after(window, (B_CONV - 1) - k)[:CONV_ROWS]
                    acc = acc + cw_ref[k:k + 1, cs] * shifted
                    dcw_ref[k:k + 1, cs] += _rowsum(shifted * yb0)
                dzb_ref[rs, cs] = (acc * sgb).astype(BF16)
                dzb_ref[rs, gcs] = (acc * xbb * sgb * (1.0 - sgb)).astype(BF16)

    row = lambda i: (i, 0)
    par = lambda i: (0, 0)
    return _pallas(
        comm, body, name=name, grid=(T // tm,),
        in_specs=[pl.BlockSpec((tm, 2 * D_B), lambda i: (i, 1)),
                  pl.BlockSpec((tm, D_B), row),
                  pl.BlockSpec((halo, D_B), lambda i: (_halo_next_index(tm, halo, T)(i), 0)),
                  pl.BlockSpec((B_CONV, D_B), par), pl.BlockSpec(memory_space=pl.ANY)],
        out_specs=[pl.BlockSpec((tm, 2 * D_B), lambda i: (i, 1)), pl.BlockSpec((B_CONV, D_B), par),
                   pl.BlockSpec((1, D_B), par)],
        out_shape=[jax.ShapeDtypeStruct(dz.shape, BF16), jax.ShapeDtypeStruct((B_CONV, D_B), F32),
                   jax.ShapeDtypeStruct((1, D_B), F32)],
        scratch_shapes=[pltpu.VMEM((tm + halo, D_B), F32)], aliases={4: 0},
        compiler_params=_cparams(("arbitrary",), 40),
    )(z, dcb, dcb, conv_w, dz)


def _rows_before(x, a):
    return x if a == 0 else pltpu.roll(x, a, axis=0)


def _rows_after(x, a):
    return x if a == 0 else pltpu.roll(x, x.shape[0] - a, axis=0)


def _conv3(w_ref, x, halo, cs):
    acc = w_ref[2:3, cs] * x[halo:]
    acc = acc + w_ref[1:2, cs] * _rows_before(x, 1)[halo:]
    return acc + w_ref[0:1, cs] * _rows_before(x, 2)[halo:]


def _mixer_c_fwd(z, conv_w, *, tm, name, comm=None):
    T = z.shape[0]
    D = D_MODEL
    halo = HALO_SHORT
    W = CONV_COLS

    def body(bg_ref, cg_ref, xv_ref, cgh_ref, xvh_ref, w_ref, r_ref):
        i = pl.program_id(0)
        for cb in range(D // W):
            cs = slice(cb * W, (cb + 1) * W)
            prev = jnp.where(i > 0, cgh_ref[:, cs].astype(F32) * xvh_ref[:, cs].astype(F32), 0.0)
            p = jnp.concatenate([prev, cg_ref[:, cs].astype(F32) * xv_ref[:, cs].astype(F32)], axis=0)
            r_ref[:, cs] = (bg_ref[:, cs].astype(F32) * _conv3(w_ref, p, halo, cs)).astype(BF16)

    hp = _halo_prev_index(tm, halo)
    return _pallas(
        comm, body, name=name, grid=(T // tm,),
        in_specs=[pl.BlockSpec((tm, D), lambda i: (i, 0)), pl.BlockSpec((tm, D), lambda i: (i, 1)),
                  pl.BlockSpec((tm, D), lambda i: (i, 2)),
                  pl.BlockSpec((halo, D), lambda i: (hp(i), 1)),
                  pl.BlockSpec((halo, D), lambda i: (hp(i), 2)),
                  pl.BlockSpec((None, C_CONV, D), lambda i: (0, 0, 0))],
        out_specs=pl.BlockSpec((tm, D), lambda i: (i, 0)),
        out_shape=jax.ShapeDtypeStruct((T, D), BF16),
        compiler_params=_cparams(("parallel",), 40),
    )(z, z, z, z, z, conv_w)


def _mixer_c_bwd(z, dr, conv_w, *, tm, name, comm=None):
    T = z.shape[0]
    D = D_MODEL
    halo = HALO_SHORT
    W = CONV_COLS

    def body(bg_ref, cg_ref, xv_ref, cgh_ref, xvh_ref, bgn_ref, dr_ref, drn_ref, w_ref, dz_ref, dw_ref):
        i = pl.program_id(0)
        last = pl.num_programs(0) - 1

        @pl.when(i == 0)
        def _():
            dw_ref[...] = jnp.zeros_like(dw_ref)
        for cb in range(D // W):
            cs = slice(cb * W, (cb + 1) * W)
            cg = cg_ref[:, cs].astype(F32)
            xv = xv_ref[:, cs].astype(F32)
            dr = dr_ref[:, cs].astype(F32)
            p = cg * xv
            prev = jnp.where(i > 0, cgh_ref[:, cs].astype(F32) * xvh_ref[:, cs].astype(F32), 0.0)
            q = _conv3(w_ref, jnp.concatenate([prev, p], axis=0), halo, cs)
            dz_ref[:, cs] = (dr * q).astype(BF16)
            nxt = jnp.where(i < last, drn_ref[:, cs].astype(F32) * bgn_ref[:, cs].astype(F32), 0.0)
            dq = jnp.concatenate([dr * bg_ref[:, cs].astype(F32), nxt], axis=0)
            dp = None
            for k in range(C_CONV):
                shifted = _rows_after(dq, 2 - k)[:tm]
                term = w_ref[k:k + 1, cs] * shifted
                dp = term if dp is None else dp + term
                dw_ref[k:k + 1, cs] += _rowsum(shifted * p)
            dz_ref[:, D + cb * W:D + (cb + 1) * W] = (dp * xv).astype(BF16)
            dz_ref[:, 2 * D + cb * W:2 * D + (cb + 1) * W] = (dp * cg).astype(BF16)

    hp = _halo_prev_index(tm, halo)
    hn = _halo_next_index(tm, halo, T)
    return _pallas(
        comm, body, name=name, grid=(T // tm,),
        in_specs=[pl.BlockSpec((tm, D), lambda i: (i, 0)), pl.BlockSpec((tm, D), lambda i: (i, 1)),
                  pl.BlockSpec((tm, D), lambda i: (i, 2)),
                  pl.BlockSpec((halo, D), lambda i: (hp(i), 1)),
                  pl.BlockSpec((halo, D), lambda i: (hp(i), 2)),
                  pl.BlockSpec((halo, D), lambda i: (hn(i), 0)),
                  pl.BlockSpec((tm, D), lambda i: (i, 0)),
                  pl.BlockSpec((halo, D), lambda i: (hn(i), 0)),
                  pl.BlockSpec((None, C_CONV, D), lambda i: (0, 0, 0))],
        out_specs=[pl.BlockSpec((tm, 3 * D), lambda i: (i, 0)),
                   pl.BlockSpec((C_CONV, D), lambda i: (0, 0))],
        out_shape=[jax.ShapeDtypeStruct((T, 3 * D), BF16), jax.ShapeDtypeStruct((C_CONV, D), F32)],
        compiler_params=_cparams(("arbitrary",), 48),
    )(z, z, z, z, z, z, dr, dr, conv_w)


FFN_COLS = 128


def _ffn_act_fwd(up, conv_w, *, layer, tm, name, comm=None):
    T = up.shape[0]
    halo = HALO_SHORT
    W = FFN_COLS

    def body(up_ref, uph_ref, w_ref, a_ref, upc_ref):
        i = pl.program_id(0)

        def conv(cs):
            prev = jnp.where(i > 0, uph_ref[:, cs], jnp.zeros((halo, W), BF16))
            return _conv3(w_ref, jnp.concatenate([prev, up_ref[:, cs]], axis=0).astype(F32), halo, cs)

        for cb in range(D_FF // W):
            gs = slice(cb * W, (cb + 1) * W)
            vs = slice(D_FF + cb * W, D_FF + (cb + 1) * W)
            g = conv(gs)
            v = conv(vs)
            upc_ref[:, gs] = g.astype(BF16)
            upc_ref[:, vs] = v.astype(BF16)
            a_ref[:, gs] = (_silu(g) * v).astype(BF16)

    return _pallas(
        comm, body, name=name, grid=(T // tm,),
        in_specs=[pl.BlockSpec((tm, 2 * D_FF), lambda i: (i, 0)),
                  pl.BlockSpec((halo, 2 * D_FF), lambda i: (_halo_prev_index(tm, halo)(i), 0)),
                  pl.BlockSpec((None, F_CONV, 2 * D_FF), lambda i: (layer, 0, 0))],
        out_specs=[pl.BlockSpec((tm, D_FF), lambda i: (i, 0)),
                   pl.BlockSpec((tm, 2 * D_FF), lambda i: (i, 0))],
        out_shape=[jax.ShapeDtypeStruct((T, D_FF), BF16), jax.ShapeDtypeStruct((T, 2 * D_FF), BF16)],
        compiler_params=_cparams(("parallel",), 48),
    )(up, up, conv_w)


def _ffn_act_bwd(up, upc, da, conv_w, *, layer, tm, name, comm=None):
    T = up.shape[0]
    halo = HALO_SHORT
    W = FFN_COLS

    def body(up_ref, upc_ref, upcn_ref, da_ref, dan_ref, w_ref, dup_ref, dw_ref):
        i = pl.program_id(0)
        last = pl.num_programs(0) - 1

        @pl.when(i == 0)
        def _():
            dw_ref[...] = jnp.zeros_like(dw_ref)
        live = jnp.where(i < last, 1.0, 0.0)
        for cb in range(D_FF // W):
            gs = slice(cb * W, (cb + 1) * W)
            vs = slice(D_FF + cb * W, D_FF + (cb + 1) * W)
            g = jnp.concatenate([upc_ref[:, gs], upcn_ref[:, gs]], axis=0).astype(F32)
            v = jnp.concatenate([upc_ref[:, vs], upcn_ref[:, vs]], axis=0).astype(F32)
            da = jnp.concatenate([da_ref[:, gs].astype(F32), dan_ref[:, gs].astype(F32) * live], axis=0)
            s = _sigmoid(g)
            silu = g * s
            grads = (da * v * (s * (1.0 + g * (1.0 - s))), da * silu)
            for cs, d in zip((gs, vs), grads):
                u = up_ref[:, cs].astype(F32)
                acc = None
                for k in range(F_CONV):
                    shifted = _rows_after(d, 2 - k)[:tm]
                    term = w_ref[k:k + 1, cs] * shifted
                    acc = term if acc is None else acc + term
                    dw_ref[k:k + 1, cs] += _rowsum(shifted * u)
                dup_ref[:, cs] = acc.astype(BF16)

    hn = _halo_next_index(tm, halo, T)
    return _pallas(
        comm, body, name=name, grid=(T // tm,),
        in_specs=[pl.BlockSpec((tm, 2 * D_FF), lambda i: (i, 0)),
                  pl.BlockSpec((tm, 2 * D_FF), lambda i: (i, 0)),
                  pl.BlockSpec((halo, 2 * D_FF), lambda i: (hn(i), 0)),
                  pl.BlockSpec((tm, D_FF), lambda i: (i, 0)),
                  pl.BlockSpec((halo, D_FF), lambda i: (hn(i), 0)),
                  pl.BlockSpec((None, F_CONV, 2 * D_FF), lambda i: (layer, 0, 0))],
        out_specs=[pl.BlockSpec((tm, 2 * D_FF), lambda i: (i, 0)),
                   pl.BlockSpec((F_CONV, 2 * D_FF), lambda i: (0, 0))],
        out_shape=[jax.ShapeDtypeStruct((T, 2 * D_FF), BF16),
                   jax.ShapeDtypeStruct((F_CONV, 2 * D_FF), F32)],
        compiler_params=_cparams(("arbitrary",), 56),
    )(up, upc, upc, da, da, conv_w)


def _local_step(x, tgt, small, plan):
    T = x.shape[0]
    tm_e = _pick(T, 256)
    tm_a = _pick(T, 512)
    tm_b = _pick(T, 128)
    tm_n = _pick(T, 512)
    tm = _pick(T, 1024)
    tm_f = _pick(T, 512)
    tt = _pick(T, 2048)
    nm = small["norm_mix"].reshape(2, 1, D_MODEL)
    nf = small["norm_ffn"].reshape(2, 1, D_MODEL)
    ngf = small["norm_final"].reshape(1, D_MODEL)
    b_s = small["a_b_s"].reshape(A_HEADS, CHUNK, 1)
    w_s = small["a_w_s"].reshape(A_HEADS, CHUNK, CHUNK)
    b_conv_w = small["b_conv_w"].reshape(B_CONV, D_B)
    sg = {}
    wt, cm = plan.weight, plan.comm

    h_m0 = _rmsnorm_fwd(x, nm, layer=0, tm=tm_n, name="norm_mix0")
    z_ab = _mm_nn(h_m0, wt("ab_w_in", 0), layer=0, tm=tm, tn=512, out_dtype=BF16, name="ab_in", comm=cm("ab_in"))
    yab, cb = _mixer_ab_fwd(z_ab, small["a_ln_g"], small["a_ln_b"], w_s, b_s, b_conv_w, small["b_conv_b"],
                            small["b_ln_g"], small["b_ln_b"], tm=tm_e, name="mixer_ab", comm=cm("mixer_ab"))
    x1, h_f0 = _mm_nn(yab, wt("ab_w_out", 0), layer=0, tm=tm, tn=D_MODEL, residual=x, norm=(nf, 0),
                      name="ab_out", comm=cm("ab_out"))

    def ffn_fwd(xin, h, layer, norm):
        up = _mm_nn(h, wt("f_w_up", layer), layer=0, tm=tm, tn=1408, out_dtype=BF16, name=f"ffn_up{layer}",
                    comm=cm(f"ffn_up{layer}"))
        a, upc = _ffn_act_fwd(up, small["f_conv_w"], layer=layer, tm=tm_a, name=f"ffn_act{layer}",
                              comm=cm(f"ffn_act{layer}"))
        out = _mm_nn(a, wt("f_w_down", layer), layer=0, tm=tm, tn=D_MODEL, residual=xin, norm=norm,
                     name=f"ffn_down{layer}", comm=cm(f"ffn_down{layer}"))
        return up, upc, a, out

    up0, upc0, a0, (x2, h_m1) = ffn_fwd(x1, h_f0, 0, (nm, 1))
    z_c = _mm_nn(h_m1, wt("c_w_in", 0), layer=0, tm=tm, tn=768, out_dtype=BF16, name="c_in", comm=cm("c_in"))
    r = _mixer_c_fwd(z_c, small["c_conv_w"], tm=tm_e, name="mixer_c", comm=cm("mixer_c"))
    x3, h_f1 = _mm_nn(r, wt("c_w_out", 0), layer=0, tm=tm, tn=D_MODEL, residual=x2, norm=(nf, 1),
                      name="c_out", comm=cm("c_out"))
    up1, upc1, a1, x4 = ffn_fwd(x3, h_f1, 1, None)
    loss, dx, sg["norm_final"] = _loss_head(x4, tgt, ngf, tm=tm_n, name="loss_head")

    def ffn_bwd(dx, xin, h, up, upc, a, layer):
        da = _mm_nt(dx, wt("f_w_down", layer), layer=0, tm=tm, tn=1408, out_dtype=BF16,
                    name=f"ffn_down_dx{layer}", comm=cm(f"ffn_down_dx{layer}"))
        plan.grad_ready("f_w_down", layer, _mm_tn(a, dx, shards=None, tk=1408, tn=1024, tt=tt,
                                                  name=f"ffn_down_dw{layer}", comm=cm(f"ffn_down_dw{layer}")))
        dup, dcw = _ffn_act_bwd(up, upc, da, small["f_conv_w"], layer=layer, tm=tm_b, name=f"ffn_act_bwd{layer}",
                                comm=cm(f"ffn_act_bwd{layer}"))
        dxin, dg = _mm_nt_norm(dup, wt("f_w_up", layer), xin, nf, dx, g_layer=layer, tm=tm_f,
                               name=f"ffn_up_dx{layer}", comm=cm(f"ffn_up_dx{layer}"))
        plan.grad_ready("f_w_up", layer, _mm_tn(h, dup, shards=N_CHIPS, tk=1024, tn=1408, tt=tt,
                                                name=f"ffn_up_dw{layer}", comm=cm(f"ffn_up_dw{layer}")))
        return dxin, dg, dcw

    dx, dnf1, dfc1 = ffn_bwd(dx, x3, h_f1, up1, upc1, a1, 1)
    dr = _mm_nt(dx, wt("c_w_out", 0), layer=0, tm=tm, tn=512, out_dtype=BF16, name="c_out_dx", comm=cm("c_out_dx"))
    plan.grad_ready("c_w_out", 0, _mm_tn(r, dx, shards=None, tk=1024, tn=1024, tt=tt, name="c_out_dw",
                                         comm=cm("c_out_dw")))
    dz_c, dccw = _mixer_c_bwd(z_c, dr, small["c_conv_w"], tm=tm_e, name="mixer_c_bwd", comm=cm("mixer_c_bwd"))
    sg["c_conv_w"] = dccw.reshape(1, C_CONV, D_MODEL)
    plan.grad_ready("c_w_in", 0, _mm_tn(h_m1, dz_c, shards=N_CHIPS, tk=1024, tn=768, tt=tt, name="c_in_dw",
                                        comm=cm("c_in_dw")))
    dx, dnm1 = _mm_nt_norm(dz_c, wt("c_w_in", 0), x2, nm, dx, g_layer=1, tm=tm_f, name="c_in_dx",
                           comm=cm("c_in_dx"))
    dx, dnf0, dfc0 = ffn_bwd(dx, x1, h_f0, up0, upc0, a0, 0)
    dyab = _mm_nt(dx, wt("ab_w_out", 0), layer=0, tm=tm, tn=512, out_dtype=BF16, name="ab_out_dx",
                  comm=cm("ab_out_dx"))
    plan.grad_ready("ab_w_out", 0, _mm_tn(yab, dx, shards=None, tk=1024, tn=1024, tt=tt, name="ab_out_dw",
                                          comm=cm("ab_out_dw")))
    (dza, dcb, sg["a_ln_g"], sg["a_ln_b"], dws, dbs, sg["b_ln_g"], sg["b_ln_b"]) = _mixer_ab_bwd_pre(
        z_ab, cb, dyab, small["a_ln_g"], small["a_ln_b"], w_s, b_s, small["b_ln_g"], small["b_ln_b"],
        tm=tm_e, name="mixer_ab_bwd", comm=cm("mixer_ab_bwd"))
    dz_ab, dbcw, sg["b_conv_b"] = _mixer_b_conv_bwd(z_ab, dcb, b_conv_w, dza, tm=tm_e, name="mixer_b_conv_bwd",
                                                    comm=cm("mixer_b_conv_bwd"))
    sg["a_w_s"] = dws.reshape(1, A_HEADS, CHUNK, CHUNK)
    sg["a_b_s"] = dbs.reshape(1, A_HEADS, CHUNK)
    sg["b_conv_w"] = dbcw.reshape(1, B_CONV, D_B)
    plan.grad_ready("ab_w_in", 0, _mm_tn(h_m0, dz_ab, shards=N_CHIPS, tk=1024, tn=512, tt=tt, name="ab_in_dw",
                                         comm=cm("ab_in_dw")))
    dx, dnm0 = _mm_nt_norm(dz_ab, wt("ab_w_in", 0), x, nm, dx, g_layer=0, tm=tm_f, name="ab_in_dx",
                           comm=cm("ab_in_dx"))

    sg["norm_mix"] = [dnm0, dnm1]
    sg["norm_ffn"] = [dnf0, dnf1]
    sg["f_conv_w"] = [dfc0, dfc1]
    return loss, dx, sg


BLOCK_BYTES = 3 * 1024 * 1024


BF16_SUBLANES = 16


def _row_tile(rows, row_bytes, step=SUBLANES):
    best = None
    for tr in range(step, rows + 1, step):
        if rows % tr == 0 and tr * row_bytes <= BLOCK_BYTES:
            best = tr
    if best is None:
        raise ValueError(f"no row tile for {rows}")
    return best


def _place_scalars():
    x, y, c = lax.axis_index("x"), lax.axis_index("y"), lax.axis_index("c")
    return jnp.stack([c, 2 * x + y, 2 * (1 - x) + y, 2 * x + (1 - y), 2 * (1 - x) + (1 - y)]).astype(jnp.int32)


def _cast_into_slot(w, place, *, layer, name):
    L, rows, cols = w.shape
    tr = _row_tile(rows, cols * 4, BF16_SUBLANES)

    def body(place_ref, w_ref, o_ref):
        o_ref[...] = w_ref[...].astype(BF16)

    return pl.pallas_call(
        body, name=name,
        grid_spec=pltpu.PrefetchScalarGridSpec(
            num_scalar_prefetch=1, grid=(rows // tr,),
            in_specs=[pl.BlockSpec((None, tr, cols), lambda i, p: (layer, i, 0))],
            out_specs=pl.BlockSpec((None, None, tr, cols), lambda i, p: (0, p[1], i, 0))),
        out_shape=jax.ShapeDtypeStruct((1, N_CHIPS, rows, cols), BF16),
        compiler_params=_cparams(("parallel",), 32),
    )(place, w)


def _pair_sum(g, theirs, place, *, name):
    S, rows, cols = g.shape
    half = rows // 2
    tr = _row_tile(half, cols * 4, BF16_SUBLANES)
    nb = half // tr

    def body(place_ref, g_ref, t_ref, o_ref):
        o_ref[...] = (g_ref[...] + t_ref[...]).astype(BF16)

    spec = pl.BlockSpec((None, tr, cols), lambda s, i, p: (s, i, 0))
    return pl.pallas_call(
        body, name=name,
        grid_spec=pltpu.PrefetchScalarGridSpec(
            num_scalar_prefetch=1, grid=(S, nb),
            in_specs=[pl.BlockSpec((None, tr, cols), lambda s, i, p: (s, p[0] * nb + i, 0)), spec],
            out_specs=spec),
        out_shape=jax.ShapeDtypeStruct((S, half, cols), BF16),
        compiler_params=_cparams(("parallel", "parallel"), 32),
    )(place, g, theirs)


def _chip_sum(p, r, g_prev, place, *, layer, shape, name):
    L, rows, cols = shape
    half = rows // 2
    tr = _row_tile(half, cols * 4, BF16_SUBLANES)
    nb = half // tr

    def body(place_ref, p_ref, r_ref, *rest):
        o_ref = rest[-1]
        mine = p_ref[...].astype(F32)
        peers = [r_ref[j].astype(F32) for j in range(3)]
        acc = None
        for s in range(N_CHIPS):
            term = jnp.where(place_ref[1] == s, mine,
                             jnp.where(place_ref[2] == s, peers[0],
                                       jnp.where(place_ref[3] == s, peers[1], peers[2])))
            acc = term if acc is None else acc + term
        o_ref[...] = acc

    in_specs = [pl.BlockSpec((None, tr, cols), lambda i, pr: (pr[1], i, 0)),
                pl.BlockSpec((3, tr, cols), lambda i, pr: (0, i, 0))]
    args = [place, p, r]
    aliases = {}
    if g_prev is not None:
        in_specs.append(ANY)
        args.append(g_prev)
        aliases = {3: 0}
    return pl.pallas_call(
        body, name=name,
        grid_spec=pltpu.PrefetchScalarGridSpec(
            num_scalar_prefetch=1, grid=(nb,), in_specs=in_specs,
            out_specs=pl.BlockSpec((None, tr, cols), lambda i, pr: (layer, pr[0] * nb + i, 0))),
        out_shape=jax.ShapeDtypeStruct(shape, F32), input_output_aliases=aliases,
        compiler_params=_cparams(("parallel",), 32),
    )(*args)


def _adamw_math(w, g, m, v):
    m2 = ADAM_B1 * m + (1.0 - ADAM_B1) * g
    v2 = ADAM_B2 * v + (1.0 - ADAM_B2) * (g * g)
    m_hat = m2 / (1.0 - ADAM_B1 ** ADAM_STEP)
    v_hat = v2 / (1.0 - ADAM_B2 ** ADAM_STEP)
    delta = -ADAM_LR * (m_hat / (jnp.sqrt(v_hat) + ADAM_EPS) + ADAM_WD * w)
    return delta, m2, v2


def _adamw(w, g, m, v, *, name):
    L, rows, cols = w.shape
    tr = _row_tile(rows, cols * 4)

    def body(w_ref, g_ref, m_ref, v_ref, d_ref, m2_ref, v2_ref):
        d, m2, v2 = _adamw_math(w_ref[...], g_ref[...], m_ref[...], v_ref[...])
        d_ref[...] = d
        m2_ref[...] = m2
        v2_ref[...] = v2

    spec = pl.BlockSpec((None, tr, cols), lambda l, i: (l, i, 0))
    shape = jax.ShapeDtypeStruct(w.shape, F32)
    return pl.pallas_call(
        body, name=name, grid=(L, rows // tr), in_specs=[spec] * 4, out_specs=[spec] * 3,
        out_shape=[shape] * 3,
        compiler_params=_cparams(("parallel", "parallel"), 48),
    )(w, g, m, v)


def _allreduce_pack(pack, *, name, comm):
    R = pack.shape[0]
    half = R // 2
    nr, nw = len(comm.reads), len(comm.writes)

    def body(*refs):
        p_ref, rd, wr_in = refs[0], refs[1:1 + nr], refs[1 + nr:1 + nr + nw]
        o_ref, wr_out = refs[1 + nr + nw], refs[2 + nr + nw:2 + nr + 2 * nw]
        sib_ref, chip_ref, parts_ref, sems, comm_sems = refs[2 + nr + 2 * nw:]
        src = dict(zip(comm.reads, rd))
        src.update(zip(comm.writes, wr_in))
        dst = dict(zip(comm.writes, wr_out))
        comm.start(src, dst, comm_sems)
        x, y, c, k, sib, peers = _place()
        swap = _remote(p_ref, sib_ref, sems.at[0, 0], sems.at[0, 1], sib)
        swap.start()
        swap.wait()
        chip_ref[...] = p_ref[...] + sib_ref[...]
        mine = chip_ref.at[pl.ds(pl.multiple_of(c * half, SUBLANES), half)]
        sends = [_remote(mine, parts_ref.at[j], sems.at[1 + j, 0], sems.at[1 + j, 1], (px, py, c))
                 for j, (px, py) in enumerate(peers)]
        for rc in sends:
            rc.start()
        for rc in sends:
            rc.wait()
        own = mine[...]
        others = [parts_ref[j] for j in range(3)]
        acc = None
        for s in range(N_CHIPS):
            term = own
            for j, (px, py) in enumerate(peers):
                term = jnp.where(2 * px + py == s, others[j], term)
            acc = term if acc is None else acc + term
        done = o_ref.at[pl.ds(pl.multiple_of(c * half, SUBLANES), half)]
        done[...] = acc
        theirs = o_ref.at[pl.ds(pl.multiple_of((1 - c) * half, SUBLANES), half)]
        share = _remote(done, done, sems.at[4, 0], sems.at[4, 1], sib)
        share.start()
        _remote(done, theirs, sems.at[4, 0], sems.at[4, 1], sib).wait()
        comm.finish(src, dst, comm_sems)

    vm = pl.BlockSpec(memory_space=pltpu.VMEM)
    operands, shapes = _comm_operands(comm)
    outs = pl.pallas_call(
        body, name=name, in_specs=[vm] + [ANY] * (nr + nw), out_specs=[vm] + [ANY] * nw,
        out_shape=[jax.ShapeDtypeStruct((R, LANES), F32)] + shapes,
        input_output_aliases={1 + nr + q: 1 + q for q in range(nw)},
        scratch_shapes=[pltpu.VMEM((R, LANES), F32), pltpu.VMEM((R, LANES), F32),
                        pltpu.VMEM((3, half, LANES), F32), pltpu.SemaphoreType.DMA((5, 2)),
                        pltpu.SemaphoreType.DMA((comm.ncopies, 2))],
        compiler_params=pltpu.CompilerParams(vmem_limit_bytes=VMEM_BYTES_MAX),
    )(pack, *operands)
    for q, n in enumerate(comm.writes):
        comm.plan.bufs[n] = outs[1 + q]
    return outs[0]


PACK_UNIT = SUBLANES * LANES


def _pack(arrays):
    flat, sizes = [], []
    for a in arrays:
        pieces = a if isinstance(a, (list, tuple)) else [a]
        v = jnp.concatenate([p.reshape(-1) for p in pieces]) if len(pieces) > 1 else pieces[0].reshape(-1)
        size = v.shape[0]
        padded = -(-size // PACK_UNIT) * PACK_UNIT
        flat.append(jnp.pad(v, (0, padded - size)))
        sizes.append((size, padded))
    total = sum(p for _, p in sizes)
    if (total // PACK_UNIT) % 2:
        flat.append(jnp.zeros((PACK_UNIT,), F32))
    return jnp.concatenate(flat).reshape(-1, LANES), sizes


def _unpack(pack, sizes, shapes):
    v = pack.reshape(-1)
    out, off = [], 0
    for (size, padded), shape in zip(sizes, shapes):
        out.append(v[off:off + size].reshape(shape))
        off += padded
    return out


BIG = ("ab_w_in", "ab_w_out", "c_w_in", "c_w_out", "f_w_up", "f_w_down")
COL_SHARDED = ("ab_w_in", "c_w_in", "f_w_up")
SMALL_REPLICATED = ("norm_mix", "norm_ffn", "norm_final", "a_ln_g", "a_ln_b", "a_w_s", "a_b_s",
                    "b_conv_b", "b_ln_g", "b_ln_b")
SMALL_SHARDED = ("b_conv_w", "c_conv_w", "f_conv_w")
SMALL = SMALL_REPLICATED + SMALL_SHARDED
ALL_WEIGHTS = ("norm_mix", "norm_ffn", "norm_final", "ab_w_in", "a_ln_g", "a_ln_b", "a_w_s", "a_b_s",
               "b_conv_w", "b_conv_b", "b_ln_g", "b_ln_b", "ab_w_out", "c_w_in", "c_conv_w", "c_w_out",
               "f_w_up", "f_conv_w", "f_w_down")


SCHEDULE = {
    "ab_in": [("gi", "f_w_up", 0, 0, 4), ("gi", "ab_w_out", 0)],
    "mixer_ab": [("gd", "f_w_up", 0, 0, 4), ("gd", "ab_w_out", 0), ("gi", "f_w_up", 0, 1, 4),
                 ("gi", "f_w_up", 0, 2, 4), ("gi", "f_w_up", 0, 3, 4)],
    "ab_out": [("gd", "f_w_up", 0, 1, 4), ("gd", "f_w_up", 0, 2, 4), ("gd", "f_w_up", 0, 3, 4)],
    "ffn_up0": [("gi", "f_w_down", 0), ("gi", "c_w_in", 0, 0, 2)],
    "ffn_act0": [("gd", "f_w_down", 0), ("gd", "c_w_in", 0, 0, 2), ("gi", "c_w_in", 0, 1, 2),
                 ("gi", "f_w_up", 1, 0, 4), ("gi", "f_w_up", 1, 1, 4)],
    "ffn_down0": [("gd", "c_w_in", 0, 1, 2), ("gd", "f_w_up", 1, 0, 4), ("gd", "f_w_up", 1, 1, 4),
                  ("gi", "f_w_up", 1, 2, 4)],
    "c_in": [("gd", "f_w_up", 1, 2, 4), ("gi", "f_w_up", 1, 3, 4), ("gi", "c_w_out", 0)],
    "mixer_c": [("gd", "f_w_up", 1, 3, 4), ("gd", "c_w_out", 0), ("gi", "f_w_down", 1, 0, 2)],
    "c_out": [("gd", "f_w_down", 1, 0, 2), ("gi", "f_w_down", 1, 1, 2)],
    "ffn_up1": [("gd", "f_w_down", 1, 1, 2)],
    "ffn_act_bwd1": [("px", "f_w_down", 1)],
    "ffn_up_dx1": [("cx", "f_w_down", 1)],
    "mixer_c_bwd": [("px", "f_w_up", 1), ("px", "c_w_out", 0)],
    "c_in_dw": [("cx", "f_w_up", 1, 0, 2), ("cx", "c_w_out", 0)],
    "c_in_dx": [("cx", "f_w_up", 1, 1, 2), ("px", "c_w_in", 0)],
    "ffn_down_dx0": [("cx", "c_w_in", 0, 0, 2)],
    "ffn_down_dw0": [("cx", "c_w_in", 0, 1, 2)],
    "ffn_act_bwd0": [("px", "f_w_down", 0)],
    "ffn_up_dx0": [("cx", "f_w_down", 0)],
    "mixer_ab_bwd": [("px", "f_w_up", 0), ("px", "ab_w_out", 0)],
    "mixer_b_conv_bwd": [("cx", "f_w_up", 0, 0, 2), ("cx", "ab_w_out", 0)],
    "ab_in_dw": [("cx", "f_w_up", 0, 1, 2)],
    "ab_in_dx": [("px", "ab_w_in", 0)],
}


class _Plan:
    def __init__(self, shapes, place):
        self.shapes, self.place, self.bufs = shapes, place, {}

    def weight(self, name, layer):
        g = self.bufs[f"w:{name}:{layer}"]
        if name in COL_SHARDED:
            return g
        _, S, rows, cols = g.shape
        return g.reshape(1, S * rows, cols)

    def grad_ready(self, name, layer, g):
        _, rows, cols = self.shapes[name]
        hbm = lambda a: pltpu.with_memory_space_constraint(a, pltpu.HBM)
        self.bufs[f"g:{name}:{layer}"] = g.reshape(N_CHIPS, rows, cols)
        self.bufs[f"t:{name}:{layer}"] = hbm(lax.empty((N_CHIPS, rows // 2, cols), F32))
        self.bufs[f"l:{name}:{layer}"] = hbm(lax.empty((3, rows // 2, cols), BF16))

    def job(self, kind, name, layer, part=0, parts=1):
        _, rows, cols = self.shapes[name]
        key = f"{name}:{layer}"
        if kind == "gi":
            return _job_gather_ici("w:" + key, rows, part, parts)
        if kind == "gd":
            return _job_gather_d2d("w:" + key, rows, part, parts)
        if kind == "px":
            return _job_pair_exchange("g:" + key, "t:" + key, rows)
        if kind == "cx":
            if "p:" + key not in self.bufs:
                self.bufs["p:" + key] = _pair_sum(self.bufs["g:" + key], self.bufs["t:" + key], self.place,
                                                  name=f"pair_sum_{name}{layer}")
            nr = rows // 2 // parts
            return _job_chip_exchange("p:" + key, "l:" + key, part * nr, nr)
        if kind == "ps":
            return _job_pair_share("G:" + name, layer, rows)
        raise ValueError(kind)

    def comm(self, call):
        specs = SCHEDULE.get(call)
        return None if specs is None else _Comm(self, [self.job(*spec) for spec in specs])


def _step(x, tgt, w, m, v):
    chip = 2 * lax.axis_index("x") + lax.axis_index("y")
    place = _place_scalars()
    plan = _Plan({n: w[n].shape for n in BIG}, place)
    items = [(n, l) for n in BIG for l in range(w[n].shape[0])]

    for n, l in items:
        plan.bufs[f"w:{n}:{l}"] = _cast_into_slot(w[n], place, layer=l, name=f"cast_{n}{l}")
    conv_pack, conv_sizes = _pack([w[n] for n in SMALL_SHARDED])
    hbm = lambda a: pltpu.with_memory_space_constraint(a, pltpu.HBM)
    plan.bufs["conv:mine"] = hbm(conv_pack)
    plan.bufs["conv:all"] = hbm(lax.empty((N_CHIPS,) + conv_pack.shape, F32))
    _comm_only(plan, [[plan.job("gi", "ab_w_in", 0), _job_chip_gather("conv:mine", "conv:all")],
                      [plan.job("gd", "ab_w_in", 0)]], name="gather_first")
    conv_shapes = [w[n].shape for n in SMALL_SHARDED]
    per_chip = [_unpack(plan.bufs["conv:all"][s], conv_sizes, conv_shapes) for s in range(N_CHIPS)]
    small = {n: w[n] for n in SMALL_REPLICATED}
    for idx, n in enumerate(SMALL_SHARDED):
        small[n] = jnp.concatenate([jnp.where(chip == s, w[n], per_chip[s][idx]) for s in range(N_CHIPS)], axis=-1)

    loss, dx, sg = _local_step(x, tgt, small, plan)

    g_pack, g_sizes = _pack([sg[n] for n in SMALL] + [loss])
    g_sum = _allreduce_pack(g_pack, name="allreduce_small_grads",
                            comm=_Comm(plan, [plan.job("cx", "ab_w_in", 0)]))
    full_shapes = [small[n].shape for n in SMALL]
    *summed, loss = _unpack(g_sum, g_sizes, full_shapes + [(1, 1)])
    g_small = dict(zip(SMALL, summed))
    for n in SMALL_SHARDED:
        width = w[n].shape[-1]
        g_small[n] = lax.dynamic_slice_in_dim(g_small[n], chip * width, width, axis=g_small[n].ndim - 1)

    for n, l in items:
        plan.bufs["G:" + n] = _chip_sum(plan.bufs[f"p:{n}:{l}"], plan.bufs[f"l:{n}:{l}"], plan.bufs.get("G:" + n),
                                        place, layer=l, shape=w[n].shape, name=f"chip_sum_{n}{l}")
    _comm_only(plan, [[plan.job("ps", n, l) for n, l in items]], name="reduce_pair_share")
    grads_big = [plan.bufs["G:" + n] for n in BIG]

    grad, delta, new_m, new_v = {}, {}, {}, {}
    for n, g in zip(BIG, grads_big):
        grad[n] = g
        delta[n], new_m[n], new_v[n] = _adamw(w[n], g, m[n], v[n], name=f"adamw_{n}")
    shapes = [w[n].shape for n in SMALL]
    wp, sizes = _pack([w[n] for n in SMALL])
    gp, _ = _pack([g_small[n] for n in SMALL])
    mp, _ = _pack([m[n] for n in SMALL])
    vp, _ = _pack([v[n] for n in SMALL])
    R = wp.shape[0]
    dp, m2p, v2p = _adamw(wp.reshape(1, R, LANES), gp.reshape(1, R, LANES), mp.reshape(1, R, LANES),
                          vp.reshape(1, R, LANES), name="adamw_small")
    for n, d_, m_, v_ in zip(SMALL, _unpack(dp, sizes, shapes), _unpack(m2p, sizes, shapes),
                             _unpack(v2p, sizes, shapes)):
        grad[n] = g_small[n]
        delta[n], new_m[n], new_v[n] = d_, m_, v_
    return loss, dx, grad, delta, new_m, new_v


def kernel(x, norm_mix, norm_ffn, norm_final, ab_w_in, a_ln_g, a_ln_b, a_w_s, a_b_s, b_conv_w, b_conv_b, b_ln_g, b_ln_b, ab_w_out, c_w_in, c_conv_w, c_w_out, f_w_up, f_conv_w, f_w_down, loss_target, m_norm_mix, m_norm_ffn, m_norm_final, m_ab_w_in, m_a_ln_g, m_a_ln_b, m_a_w_s, m_a_b_s, m_b_conv_w, m_b_conv_b, m_b_ln_g, m_b_ln_b, m_ab_w_out, m_c_w_in, m_c_conv_w, m_c_w_out, m_f_w_up, m_f_conv_w, m_f_w_down, v_norm_mix, v_norm_ffn, v_norm_final, v_ab_w_in, v_a_ln_g, v_a_ln_b, v_a_w_s, v_a_b_s, v_b_conv_w, v_b_conv_b, v_b_ln_g, v_b_ln_b, v_ab_w_out, v_c_w_in, v_c_conv_w, v_c_w_out, v_f_w_up, v_f_conv_w, v_f_w_down):
    given = dict(locals())
    w = {n: given[n] for n in ALL_WEIGHTS}
    m = {n: given["m_" + n] for n in ALL_WEIGHTS}
    v = {n: given["v_" + n] for n in ALL_WEIGHTS}
    T = x.shape[1]
    loss, dx, grad, delta, new_m, new_v = _step(x.reshape(T, D_MODEL), loss_target.reshape(T, D_MODEL), w, m, v)
    out = [loss[0, 0], dx.reshape(x.shape)]
    for d in (grad, delta, new_m, new_v):
        out += [d[n] for n in ALL_WEIGHTS]
    return tuple(out)
```

```python
import functools
import math

import jax
import jax.numpy as jnp
from jax import lax
from jax.experimental import pallas as pl
from jax.experimental.pallas import tpu as pltpu

F32 = jnp.float32
BF16 = jnp.bfloat16

EPS = 1e-6
D_MODEL = 1024
CHUNK = 128
HEAD_DIM = 128
A_HEADS = 4
D_A = 512
D_B = 512
B_CONV = 31
C_CONV = 3
D_FF = 2816
F_CONV = 3
N_CHIPS = 4

ADAM_LR = 0.001
ADAM_B1 = 0.9
ADAM_B2 = 0.999
ADAM_EPS = 1e-08
ADAM_WD = 0.01
ADAM_STEP = 10

SUBLANES = 8
LANES = 128
HALO_SHORT = 16
HALO_LONG = 32
VMEM_BYTES_MAX = 60000 * 1024

INV_SQRT2 = 1.0 / math.sqrt(2.0)
INV_SQRT_2PI = 1.0 / math.sqrt(2.0 * math.pi)

MESH = pl.DeviceIdType.MESH


def _cparams(sem, vmem_mb):
    del vmem_mb
    return pltpu.CompilerParams(dimension_semantics=sem, vmem_limit_bytes=VMEM_BYTES_MAX)


def _pick(total, pref):
    for c in (2048, 1024, 512, 256, 128):
        if c <= pref and total % c == 0:
            return c
    raise ValueError(f"no tile for {total}")


def _sigmoid(x):
    return jax.nn.sigmoid(x)


def _silu(x):
    return x * _sigmoid(x)


def _dsilu(x):
    s = _sigmoid(x)
    return s * (1.0 + x * (1.0 - s))


def _gelu(x):
    return 0.5 * x * (1.0 + lax.erf(x * INV_SQRT2))


def _dgelu(x):
    return 0.5 * (1.0 + lax.erf(x * INV_SQRT2)) + x * jnp.exp(-0.5 * x * x) * INV_SQRT_2PI


def _ln_stats(x):
    mu = jnp.mean(x, axis=-1, keepdims=True)
    xc = x - mu
    var = jnp.mean(xc * xc, axis=-1, keepdims=True)
    r = lax.rsqrt(var + EPS)
    return xc * r, r


def _ln_bwd(dy, xh, r, g):
    dxh = dy * g
    m1 = jnp.mean(dxh, axis=-1, keepdims=True)
    m2 = jnp.mean(dxh * xh, axis=-1, keepdims=True)
    return r * (dxh - m1 - xh * m2)


def _rowsum(x):
    return jnp.sum(x, axis=0, keepdims=True)


ANY = pl.BlockSpec(memory_space=pltpu.HBM)


def _place():
    x, y, c = lax.axis_index("x"), lax.axis_index("y"), lax.axis_index("c")
    peers = [(1 - x, y), (x, 1 - y), (1 - x, 1 - y)]
    return x, y, c, 2 * x + y, (x, y, 1 - c), peers


def _half(rows, which):
    return pl.ds(which * (rows // 2), rows // 2)


def _remote(src, dst, send_sem, recv_sem, device):
    return pltpu.make_async_remote_copy(src_ref=src, dst_ref=dst, send_sem=send_sem, recv_sem=recv_sem,
                                        device_id=device, device_id_type=MESH)


class _Job:
    def __init__(self, reads, writes, ncopies, copies):
        self.reads, self.writes, self.ncopies, self.copies = reads, writes, ncopies, copies


def _share(rows, which, part, parts):
    nr = rows // 2 // parts
    return pl.ds(which * (rows // 2) + part * nr, nr)


def _slot(ref, chip, rows, cols):
    if ref.shape[1] == N_CHIPS:
        return ref.at[0, chip, rows]
    return ref.at[0, chip // 2, rows, pl.ds(pl.multiple_of((chip % 2) * cols, LANES), cols)]


def _job_gather_ici(name, rows, cols, part, parts):
    def copies(src, dst, sem):
        x, y, c, k, sib, peers = _place()
        mine_rows = _share(rows, c, part, parts)
        out = []
        for j, (px, py) in enumerate(peers):
            mine = _slot(src[name], k, mine_rows, cols)
            out.append((_remote(mine, _slot(dst[name], k, mine_rows, cols), sem(j, 0), sem(j, 1), (px, py, c)),
                        _remote(mine, _slot(dst[name], 2 * px + py, mine_rows, cols), sem(j, 0), sem(j, 1),
                                (px, py, c))))
        return out
    return _Job([], [name], 3, copies)


def _job_gather_d2d(name, rows, cols, part, parts):
    def copies(src, dst, sem):
        x, y, c, k, sib, peers = _place()
        out = []
        for j, (px, py) in enumerate(peers):
            mine_rows, their_rows = _share(rows, c, part, parts), _share(rows, 1 - c, part, parts)
            landed = _slot(src[name], 2 * px + py, mine_rows, cols)
            out.append((_remote(landed, _slot(dst[name], 2 * px + py, mine_rows, cols), sem(j, 0), sem(j, 1), sib),
                        _remote(landed, _slot(dst[name], 2 * px + py, their_rows, cols), sem(j, 0), sem(j, 1), sib)))
        return out
    return _Job([], [name], 3, copies)


def _job_chip_gather(sname, dname):
    def copies(src, dst, sem):
        x, y, c, k, sib, peers = _place()
        return [(_remote(src[sname], dst[dname].at[k], sem(j, 0), sem(j, 1), (px, py, c)),
                 _remote(src[sname], dst[dname].at[2 * px + py], sem(j, 0), sem(j, 1), (px, py, c)))
                for j, (px, py) in enumerate(peers)]
    return _Job([sname], [dname], 3, copies)


def _job_pair_exchange(gname, tname, rows):
    def copies(src, dst, sem):
        x, y, c, k, sib, peers = _place()
        cp = _remote(src[gname].at[:, _half(rows, 1 - c), :], dst[tname], sem(0, 0), sem(0, 1), sib)
        return [(cp, cp)]
    return _Job([gname], [tname], 1, copies)


def _job_chip_exchange(pname, lname, r0, nr):
    def copies(src, dst, sem):
        x, y, c, k, sib, peers = _place()
        out = []
        for j, (px, py) in enumerate(peers):
            cp = _remote(src[pname].at[2 * px + py, pl.ds(r0, nr)], dst[lname].at[j, pl.ds(r0, nr)],
                         sem(j, 0), sem(j, 1), (px, py, c))
            out.append((cp, cp))
        return out
    return _Job([pname], [lname], 3, copies)


def _job_pair_share(name, layer, rows):
    def copies(src, dst, sem):
        x, y, c, k, sib, peers = _place()
        mine = src[name].at[layer, _half(rows, c)]
        return [(_remote(mine, dst[name].at[layer, _half(rows, c)], sem(0, 0), sem(0, 1), sib),
                 _remote(mine, dst[name].at[layer, _half(rows, 1 - c)], sem(0, 0), sem(0, 1), sib))]
    return _Job([], [name], 1, copies)


class _Comm:
    def __init__(self, plan, jobs):
        self.plan, self.jobs = plan, jobs
        self.writes, self.reads = [], []
        for job in jobs:
            for n in job.writes:
                if n not in self.writes:
                    self.writes.append(n)
        for job in jobs:
            for n in job.reads:
                if n not in self.writes and n not in self.reads:
                    self.reads.append(n)
        self.ncopies = sum(job.ncopies for job in jobs)

    def descriptors(self, src, dst, sems, base):
        out = []
        for job in self.jobs:
            sem = lambda j, which, base=base: sems.at[base + j, which]
            out += job.copies(src, dst, sem)
            base += job.ncopies
        return out

    def start(self, src, dst, sems, base=0):
        for first, _ in self.descriptors(src, dst, sems, base):
            first.start()

    def finish(self, src, dst, sems, base=0):
        for _, landed in self.descriptors(src, dst, sems, base):
            landed.wait()


def _comm_operands(comm):
    bufs = comm.plan.bufs
    shapes = [jax.ShapeDtypeStruct(bufs[n].shape, bufs[n].dtype) for n in comm.writes]
    return [bufs[n] for n in comm.reads] + [bufs[n] for n in comm.writes], shapes


def _pallas(comm, body, *, name, grid, in_specs, out_specs, out_shape, compiler_params, scratch_shapes=(),
            aliases=None):
    aliases = dict(aliases or {})
    if comm is None:
        return pl.pallas_call(body, name=name, grid=grid, in_specs=in_specs, out_specs=out_specs,
                              out_shape=out_shape, scratch_shapes=list(scratch_shapes),
                              input_output_aliases=aliases, compiler_params=compiler_params)
    single = not isinstance(out_shape, (list, tuple))
    base_specs = [out_specs] if single else list(out_specs)
    base_shape = [out_shape] if single else list(out_shape)
    nb, nr, nw, nbo, nsc = len(in_specs), len(comm.reads), len(comm.writes), len(base_specs), len(scratch_shapes)

    def wrapped(*refs):
        base_in, rd, wr_in = refs[:nb], refs[nb:nb + nr], refs[nb + nr:nb + nr + nw]
        o0 = nb + nr + nw
        base_out, wr_out = refs[o0:o0 + nbo], refs[o0 + nbo:o0 + nbo + nw]
        scratch, sems = refs[o0 + nbo + nw:o0 + nbo + nw + nsc], refs[-1]
        src = dict(zip(comm.reads, rd))
        src.update(zip(comm.writes, wr_in))
        dst = dict(zip(comm.writes, wr_out))
        first = functools.reduce(jnp.logical_and, [pl.program_id(a) == 0 for a in range(len(grid))])
        last = functools.reduce(jnp.logical_and,
                                [pl.program_id(a) == pl.num_programs(a) - 1 for a in range(len(grid))])

        @pl.when(first)
        def _():
            comm.start(src, dst, sems)
        body(*base_in, *base_out, *scratch)

        @pl.when(last)
        def _():
            comm.finish(src, dst, sems)

    operands, shapes = _comm_operands(comm)
    call = pl.pallas_call(
        wrapped, name=name, grid=grid, in_specs=list(in_specs) + [ANY] * (nr + nw),
        out_specs=base_specs + [ANY] * nw, out_shape=base_shape + shapes,
        input_output_aliases={**aliases, **{nb + nr + q: nbo + q for q in range(nw)}},
        scratch_shapes=list(scratch_shapes) + [pltpu.SemaphoreType.DMA((comm.ncopies, 2))],
        compiler_params=compiler_params)

    def run(*args):
        outs = call(*args, *operands)
        for q, n in enumerate(comm.writes):
            comm.plan.bufs[n] = outs[nbo + q]
        return outs[0] if single else list(outs[:nbo])

    return run


def _comm_only(plan, phases, *, name):
    comms = [_Comm(plan, jobs) for jobs in phases]
    both = _Comm(plan, [job for jobs in phases for job in jobs])
    nr, nw = len(both.reads), len(both.writes)

    def body(*refs):
        rd, wr_in, wr_out, sems = refs[:nr], refs[nr:nr + nw], refs[nr + nw:nr + 2 * nw], refs[-1]
        src = dict(zip(both.reads, rd))
        src.update(zip(both.writes, wr_in))
        dst = dict(zip(both.writes, wr_out))
        base = 0
        for comm in comms:
            comm.start(src, dst, sems, base)
            comm.finish(src, dst, sems, base)
            base += comm.ncopies

    operands, shapes = _comm_operands(both)
    outs = pl.pallas_call(
        body, name=name, in_specs=[ANY] * (nr + nw), out_specs=[ANY] * nw, out_shape=shapes,
        input_output_aliases={nr + q: q for q in range(nw)},
        scratch_shapes=[pltpu.SemaphoreType.DMA((both.ncopies, 2))],
    )(*operands)
    for q, n in enumerate(both.writes):
        plan.bufs[n] = outs[q]


def _mm_nn(a, w, *, layer, tm, tn, residual=None, norm=None, out_dtype=F32, name, comm=None):
    T, K = a.shape
    if w.ndim == 4:
        _, S, _, n4 = w.shape
        N = S * n4
        bps = n4 // tn
        w_spec = pl.BlockSpec((None, None, K, tn), lambda j, i: (layer, j // bps, 0, j % bps))
    else:
        N = w.shape[2]
        w_spec = pl.BlockSpec((None, K, tn), lambda j, i: (layer, 0, j))
    in_specs = [pl.BlockSpec((tm, K), lambda j, i: (i, 0)), w_spec]
    args = [a, w]
    if residual is not None:
        in_specs.append(pl.BlockSpec((tm, tn), lambda j, i: (i, j)))
        args.append(residual)
    out_specs = pl.BlockSpec((tm, tn), lambda j, i: (i, j))
    out_shape = jax.ShapeDtypeStruct((T, N), out_dtype)
    if norm is not None:
        assert tn == N
        g, norm_layer = norm
        in_specs.append(pl.BlockSpec((None, 1, N), lambda j, i: (norm_layer, 0, 0)))
        args.append(g)
        out_specs = [out_specs, pl.BlockSpec((tm, tn), lambda j, i: (i, j))]
        out_shape = [out_shape, jax.ShapeDtypeStruct((T, N), BF16)]

    def body(*refs):
        a_ref, w_ref = refs[0], refs[1]
        acc = jnp.dot(a_ref[...].astype(BF16), w_ref[...], preferred_element_type=F32)
        if residual is not None:
            acc = refs[2][...] + acc
        if norm is None:
            refs[-1][...] = acc.astype(out_dtype)
        else:
            refs[-2][...] = acc.astype(out_dtype)
            r = lax.rsqrt(jnp.mean(acc * acc, axis=-1, keepdims=True) + EPS)
            refs[-1][...] = (acc * r * refs[-3][...]).astype(BF16)

    return _pallas(
        comm, body, name=name, grid=(N // tn, T // tm), in_specs=in_specs,
        out_specs=out_specs, out_shape=out_shape,
        compiler_params=_cparams(("parallel", "parallel"), 48),
    )(*args)


def _mm_nt(dy, w, *, layer, tm, tn, name, out_dtype=F32, comm=None):
    T = dy.shape[0]
    nt_dims = (((1,), (1,)), ((), ()))
    _, R, N = w.shape

    def body2(dy_ref, w_ref, o_ref):
        o_ref[...] = lax.dot_general(dy_ref[...].astype(BF16), w_ref[...], nt_dims,
                                     preferred_element_type=F32).astype(out_dtype)

    return _pallas(
        comm, body2, name=name, grid=(R // tn, T // tm),
        in_specs=[pl.BlockSpec((tm, N), lambda j, i: (i, 0)),
                  pl.BlockSpec((None, tn, N), lambda j, i: (layer, j, 0))],
        out_specs=pl.BlockSpec((tm, tn), lambda j, i: (i, j)),
        out_shape=jax.ShapeDtypeStruct((T, R), out_dtype),
        compiler_params=_cparams(("parallel", "parallel"), 48),
    )(dy, w)


def _mm_tn(a, dy, *, shards, tk, tn, tt, name, comm=None):
    T, K = a.shape
    N = dy.shape[1]
    tn_dims = (((0,), (0,)), ((), ()))
    n4 = N if shards is None else N // shards
    span = max(tn // n4, 1)

    def body(a_ref, dy_ref, o_ref):
        @pl.when(pl.program_id(2) == 0)
        def _():
            o_ref[...] = jnp.zeros_like(o_ref)
        r = lax.dot_general(a_ref[...].astype(BF16), dy_ref[...].astype(BF16), tn_dims,
                            preferred_element_type=F32)
        if span == 1:
            o_ref[...] += r
        else:
            for q in range(span):
                o_ref[q] += r[:, q * n4:(q + 1) * n4]

    if shards is None:
        out_spec = pl.BlockSpec((tk, tn), lambda k, n, t: (k, n))
        out_shape = jax.ShapeDtypeStruct((K, N), F32)
    elif span > 1:
        out_spec = pl.BlockSpec((span, tk, n4), lambda k, n, t: (n, k, 0))
        out_shape = jax.ShapeDtypeStruct((shards, K, n4), F32)
    else:
        bps = n4 // tn
        out_spec = pl.BlockSpec((None, tk, tn), lambda k, n, t: (n // bps, k, n % bps))
        out_shape = jax.ShapeDtypeStruct((shards, K, n4), F32)
    return _pallas(
        comm, body, name=name, grid=(K // tk, N // tn, T // tt),
        in_specs=[pl.BlockSpec((tt, tk), lambda k, n, t: (t, k)),
                  pl.BlockSpec((tt, tn), lambda k, n, t: (t, n))],
        out_specs=out_spec, out_shape=out_shape,
        compiler_params=_cparams(("parallel", "parallel", "arbitrary"), 48),
    )(a, dy)


def _rmsnorm_fwd(x, g, *, layer, tm, name, comm=None):
    T, D = x.shape

    def body(x_ref, g_ref, h_ref):
        xf = x_ref[...]
        r = lax.rsqrt(jnp.mean(xf * xf, axis=-1, keepdims=True) + EPS)
        h_ref[...] = (xf * r * g_ref[...]).astype(BF16)

    return _pallas(
        comm, body, name=name, grid=(T // tm,),
        in_specs=[pl.BlockSpec((tm, D), lambda i: (i, 0)),
                  pl.BlockSpec((None, 1, D), lambda i: (layer, 0, 0))],
        out_specs=pl.BlockSpec((tm, D), lambda i: (i, 0)),
        out_shape=jax.ShapeDtypeStruct((T, D), BF16),
        compiler_params=_cparams(("parallel",), 32),
    )(x, g)


def _rmsnorm_bwd_math(xf, g, dh, dres):
    r = lax.rsqrt(jnp.mean(xf * xf, axis=-1, keepdims=True) + EPS)
    xh = xf * r
    dxh = dh * g
    dx = dres + r * (dxh - xh * jnp.mean(dxh * xh, axis=-1, keepdims=True))
    return dx, _rowsum(dh * xh)


def _mm_nt_norm(dy, w, x, g, dres, *, g_layer, tm, name, comm=None):
    T = dy.shape[0]
    _, S, K, n4 = w.shape
    nt_dims = (((1,), (1,)), ((), ()))

    def body(dy_ref, w_ref, x_ref, g_ref, dres_ref, dx_ref, dg_ref):
        @pl.when(pl.program_id(0) == 0)
        def _():
            dg_ref[...] = jnp.zeros_like(dg_ref)
        dh = None
        for s in range(S):
            part = lax.dot_general(dy_ref[:, s * n4:(s + 1) * n4].astype(BF16), w_ref[s], nt_dims,
                                   preferred_element_type=F32)
            dh = part if dh is None else dh + part
        dx, dg = _rmsnorm_bwd_math(x_ref[...], g_ref[...], dh, dres_ref[...])
        dx_ref[...] = dx
        dg_ref[...] += dg

    row = lambda i: (i, 0)
    return _pallas(
        comm, body, name=name, grid=(T // tm,),
        in_specs=[pl.BlockSpec((tm, S * n4), row),
                  pl.BlockSpec((None, S, K, n4), lambda i: (0, 0, 0, 0)),
                  pl.BlockSpec((tm, K), row),
                  pl.BlockSpec((None, 1, K), lambda i: (g_layer, 0, 0)),
                  pl.BlockSpec((tm, K), row)],
        out_specs=[pl.BlockSpec((tm, K), row), pl.BlockSpec((1, K), lambda i: (0, 0))],
        out_shape=[jax.ShapeDtypeStruct((T, K), F32), jax.ShapeDtypeStruct((1, K), F32)],
        compiler_params=_cparams(("arbitrary",), 56),
    )(dy, w, x, g, dres)


def _loss_head(x, tgt, g, *, tm, name, comm=None):
    T, D = x.shape

    def body(x_ref, t_ref, g_ref, loss_ref, dx_ref, dg_ref):
        @pl.when(pl.program_id(0) == 0)
        def _():
            dg_ref[...] = jnp.zeros_like(dg_ref)
            loss_ref[...] = jnp.zeros_like(loss_ref)
        xf = x_ref[...]
        gg = g_ref[...]
        r = lax.rsqrt(jnp.mean(xf * xf, axis=-1, keepdims=True) + EPS)
        xh = xf * r
        err = xh * gg - t_ref[...]
        row = jnp.mean(err * err, axis=-1, keepdims=True)
        loss_ref[...] += 0.5 * jnp.sum(row, axis=0, keepdims=True)
        dy = err * (1.0 / D)
        dg_ref[...] += _rowsum(dy * xh)
        dxh = dy * gg
        dx_ref[...] = r * (dxh - xh * jnp.mean(dxh * xh, axis=-1, keepdims=True))

    return _pallas(
        comm, body, name=name, grid=(T // tm,),
        in_specs=[pl.BlockSpec((tm, D), lambda i: (i, 0)),
                  pl.BlockSpec((tm, D), lambda i: (i, 0)),
                  pl.BlockSpec((1, D), lambda i: (0, 0))],
        out_specs=[pl.BlockSpec((1, 1), lambda i: (0, 0)),
                   pl.BlockSpec((tm, D), lambda i: (i, 0)),
                   pl.BlockSpec((1, D), lambda i: (0, 0))],
        out_shape=[jax.ShapeDtypeStruct((1, 1), F32), jax.ShapeDtypeStruct((T, D), F32),
                   jax.ShapeDtypeStruct((1, D), F32)],
        compiler_params=_cparams(("arbitrary",), 40),
    )(x, tgt, g)


CONV_ROWS = 64
CONV_COLS = 256


def _halo_prev_index(tm, halo):
    per = tm // halo
    return lambda i: jnp.maximum(i * per - 1, 0)


def _halo_next_index(tm, halo, total):
    per = tm // halo
    last = total // halo - 1
    return lambda i: jnp.minimum((i + 1) * per, last)


def _causal_mask():
    t = lax.broadcasted_iota(jnp.int32, (CHUNK, CHUNK), 0)
    s = lax.broadcasted_iota(jnp.int32, (CHUNK, CHUNK), 1)
    return s <= t


def _mixer_ab_fwd(z, a_ln_g, a_ln_b, w_s, b_s, conv_w, conv_b, b_ln_g, b_ln_b, *, tm, name, comm=None):
    T = z.shape[0]
    nchunk = tm // CHUNK
    halo = HALO_LONG

    def body(za_ref, zb_ref, zh_ref, alg_ref, alb_ref, ws_ref, bs_ref, cw_ref, cbias_ref,
             blg_ref, blb_ref, y_ref, cb_ref, ext_ref):
        i = pl.program_id(0)
        gu = _gelu(za_ref[:, :D_A].astype(F32))
        gv = _gelu(za_ref[:, D_A:].astype(F32))
        xh, _ = _ln_stats(gv)
        lv = (xh * alg_ref[...] + alb_ref[...]).astype(BF16)
        mask = _causal_mask()
        for h in range(A_HEADS):
            wm = jnp.where(mask, ws_ref[h], 0.0).astype(BF16)
            cols = slice(h * HEAD_DIM, (h + 1) * HEAD_DIM)
            for c in range(nchunk):
                rows = slice(c * CHUNK, (c + 1) * CHUNK)
                mixed = jnp.dot(wm, lv[rows, cols], preferred_element_type=F32) + bs_ref[h]
                y_ref[rows, cols] = (gu[rows, cols] * mixed).astype(BF16)
        ext_ref[halo:halo + tm, :] = zb_ref[:, :D_B].astype(F32) * _sigmoid(zb_ref[:, D_B:].astype(F32))
        prev = zh_ref[:, :D_B].astype(F32) * _sigmoid(zh_ref[:, D_B:].astype(F32))
        ext_ref[0:halo, :] = jnp.where(i > 0, prev, 0.0)
        for rb in range(tm // CONV_ROWS):
            for cb in range(D_B // CONV_COLS):
                cs = slice(cb * CONV_COLS, (cb + 1) * CONV_COLS)
                window = ext_ref[rb * CONV_ROWS:rb * CONV_ROWS + CONV_ROWS + halo, cs]
                acc = jnp.zeros((CONV_ROWS, CONV_COLS), F32)
                for k in range(B_CONV):
                    shifted = _rows_after(window, halo - (B_CONV - 1) + k)[:CONV_ROWS]
                    acc = acc + cw_ref[k:k + 1, cs] * shifted
                cb_ref[rb * CONV_ROWS:(rb + 1) * CONV_ROWS, cs] = acc + cbias_ref[:, cs]
        xhb, _ = _ln_stats(cb_ref[...])
        y_ref[:, D_A:] = _silu(xhb * blg_ref[...] + blb_ref[...]).astype(BF16)

    row = lambda i: (i, 0)
    par = lambda i: (0, 0)
    return _pallas(
        comm, body, name=name, grid=(T // tm,),
        in_specs=[pl.BlockSpec((tm, 2 * D_A), lambda i: (i, 0)),
                  pl.BlockSpec((tm, 2 * D_B), lambda i: (i, 1)),
                  pl.BlockSpec((halo, 2 * D_B), lambda i: (_halo_prev_index(tm, halo)(i), 1)),
                  pl.BlockSpec((1, D_A), par), pl.BlockSpec((1, D_A), par),
                  pl.BlockSpec((A_HEADS, CHUNK, CHUNK), lambda i: (0, 0, 0)),
                  pl.BlockSpec((A_HEADS, CHUNK, 1), lambda i: (0, 0, 0)),
                  pl.BlockSpec((B_CONV, D_B), par), pl.BlockSpec((1, D_B), par),
                  pl.BlockSpec((1, D_B), par), pl.BlockSpec((1, D_B), par)],
        out_specs=[pl.BlockSpec((tm, D_A + D_B), row), pl.BlockSpec((tm, D_B), row)],
        out_shape=[jax.ShapeDtypeStruct((T, D_A + D_B), BF16), jax.ShapeDtypeStruct((T, D_B), F32)],
        scratch_shapes=[pltpu.VMEM((halo + tm, D_B), F32)],
        compiler_params=_cparams(("parallel",), 40),
    )(z, z, z, a_ln_g, a_ln_b, w_s, b_s, conv_w, conv_b, b_ln_g, b_ln_b)


def _mixer_ab_bwd_pre(z, cb, dy, a_ln_g, a_ln_b, w_s, b_s, b_ln_g, b_ln_b, *, tm, name, comm=None):
    T = z.shape[0]
    nchunk = tm // CHUNK
    tn_dims = (((0,), (0,)), ((), ()))
    nt_dims = (((1,), (1,)), ((), ()))

    def body(za_ref, cb_ref, dy_ref, alg_ref, alb_ref, ws_ref, bs_ref, blg_ref, blb_ref,
             dza_ref, dcb_ref, dalg_ref, dalb_ref, dws_ref, dbs_ref, dblg_ref, dblb_ref,
             dlv_ref):
        @pl.when(pl.program_id(0) == 0)
        def _():
            for ref in (dalg_ref, dalb_ref, dws_ref, dbs_ref, dblg_ref, dblb_ref):
                ref[...] = jnp.zeros_like(ref)
        ua = za_ref[:, :D_A].astype(F32)
        va = za_ref[:, D_A:].astype(F32)
        gu = _gelu(ua)
        gv = _gelu(va)
        xh, r = _ln_stats(gv)
        alg = alg_ref[...]
        lv = (xh * alg + alb_ref[...]).astype(BF16)
        dya = dy_ref[:, :D_A].astype(F32)
        mask = _causal_mask()
        for h in range(A_HEADS):
            wm = jnp.where(mask, ws_ref[h], 0.0).astype(BF16)
            cols = slice(h * HEAD_DIM, (h + 1) * HEAD_DIM)
            dwm = jnp.zeros((CHUNK, CHUNK), F32)
            dbs = jnp.zeros((CHUNK, 1), F32)
            for c in range(nchunk):
                rows = slice(c * CHUNK, (c + 1) * CHUNK)
                lvb = lv[rows, cols]
                mixed = jnp.dot(wm, lvb, preferred_element_type=F32) + bs_ref[h]
                dyb = dya[rows, cols]
                dza_ref[rows, cols] = (dyb * mixed * _dgelu(ua[rows, cols])).astype(BF16)
                dmixed = dyb * gu[rows, cols]
                dmb = dmixed.astype(BF16)
                dlv_ref[rows, cols] = lax.dot_general(wm, dmb, tn_dims, preferred_element_type=F32)
                dwm = dwm + lax.dot_general(dmb, lvb, nt_dims, preferred_element_type=F32)
                dbs = dbs + jnp.sum(dmixed, axis=1, keepdims=True)
            dws_ref[h] += jnp.where(mask, dwm, 0.0)
            dbs_ref[h] += dbs
        dlv = dlv_ref[...]
        dalg_ref[...] += _rowsum(dlv * xh)
        dalb_ref[...] += _rowsum(dlv)
        dgv = _ln_bwd(dlv, xh, r, alg)
        dza_ref[:, D_A:] = (dgv * _dgelu(va)).astype(BF16)
        xhb, rb = _ln_stats(cb_ref[...])
        blg = blg_ref[...]
        lb = xhb * blg + blb_ref[...]
        dlb = dy_ref[:, D_A:].astype(F32) * _dsilu(lb)
        dblg_ref[...] += _rowsum(dlb * xhb)
        dblb_ref[...] += _rowsum(dlb)
        dcb_ref[...] = _ln_bwd(dlb, xhb, rb, blg)

    row = lambda i: (i, 0)
    par = lambda i: (0, 0)
    par3 = lambda i: (0, 0, 0)
    return _pallas(
        comm, body, name=name, grid=(T // tm,),
        in_specs=[pl.BlockSpec((tm, 2 * D_A), row), pl.BlockSpec((tm, D_B), row),
                  pl.BlockSpec((tm, D_A + D_B), row),
                  pl.BlockSpec((1, D_A), par), pl.BlockSpec((1, D_A), par),
                  pl.BlockSpec((A_HEADS, CHUNK, CHUNK), par3),
                  pl.BlockSpec((A_HEADS, CHUNK, 1), par3),
                  pl.BlockSpec((1, D_B), par), pl.BlockSpec((1, D_B), par)],
        out_specs=[pl.BlockSpec((tm, 2 * D_A), row), pl.BlockSpec((tm, D_B), row),
                   pl.BlockSpec((1, D_A), par), pl.BlockSpec((1, D_A), par),
                   pl.BlockSpec((A_HEADS, CHUNK, CHUNK), par3),
                   pl.BlockSpec((A_HEADS, CHUNK, 1), par3),
                   pl.BlockSpec((1, D_B), par), pl.BlockSpec((1, D_B), par)],
        out_shape=[jax.ShapeDtypeStruct((T, 2 * D_A + 2 * D_B), BF16), jax.ShapeDtypeStruct((T, D_B), F32),
                   jax.ShapeDtypeStruct((1, D_A), F32), jax.ShapeDtypeStruct((1, D_A), F32),
                   jax.ShapeDtypeStruct((A_HEADS, CHUNK, CHUNK), F32),
                   jax.ShapeDtypeStruct((A_HEADS, CHUNK, 1), F32),
                   jax.ShapeDtypeStruct((1, D_B), F32), jax.ShapeDtypeStruct((1, D_B), F32)],
        scratch_shapes=[pltpu.VMEM((tm, D_A), F32)],
        compiler_params=_cparams(("arbitrary",), 40),
    )(z, cb, dy, a_ln_g, a_ln_b, w_s, b_s, b_ln_g, b_ln_b)


def _mixer_b_conv_bwd(z, dcb, conv_w, dz, *, tm, name, comm=None):
    T = z.shape[0]
    halo = HALO_LONG

    def body(zb_ref, dcb_ref, dcn_ref, cw_ref, dz_in_ref, dzb_ref, dcw_ref, dbias_ref, dext_ref):
        i = pl.program_id(0)
        last = pl.num_programs(0) - 1

        @pl.when(i == 0)
        def _():
            dcw_ref[...] = jnp.zeros_like(dcw_ref)
            dbias_ref[...] = jnp.zeros_like(dbias_ref)
        dcb = dcb_ref[...]
        dext_ref[0:tm, :] = dcb
        dext_ref[tm:tm + halo, :] = jnp.where(i < last, dcn_ref[...], 0.0)
        dbias_ref[...] += _rowsum(dcb)
        for rb in range(tm // CONV_ROWS):
            for cb in range(D_B // CONV_COLS):
                cs = slice(cb * CONV_COLS, (cb + 1) * CONV_COLS)
                gcs = slice(D_B + cb * CONV_COLS, D_B + (cb + 1) * CONV_COLS)
                rs = slice(rb * CONV_ROWS, (rb + 1) * CONV_ROWS)
                xbb = zb_ref[rs, cs].astype(F32)
                sgb = _sigmoid(zb_ref[rs, gcs].astype(F32))
                yb0 = xbb * sgb
                window = dext_ref[rb * CONV_ROWS:rb * CONV_ROWS + CONV_ROWS + halo, cs]
                acc = jnp.zeros((CONV_ROWS, CONV_COLS), F32)
                for k in range(B_CONV):
                    shifted = _rows_after(window, (B_CONV - 1) - k)[:CONV_ROWS]
                    acc = acc + cw_ref[k:k + 1, cs] * shifted
                    dcw_ref[k:k + 1, cs] += _rowsum(shifted * yb0)
                dzb_ref[rs, cs] = (acc * sgb).astype(BF16)
                dzb_ref[rs, gcs] = (acc * xbb * sgb * (1.0 - sgb)).astype(BF16)

    row = lambda i: (i, 0)
    par = lambda i: (0, 0)
    return _pallas(
        comm, body, name=name, grid=(T // tm,),
        in_specs=[pl.BlockSpec((tm, 2 * D_B), lambda i: (i, 1)),
                  pl.BlockSpec((tm, D_B), row),
                  pl.BlockSpec((halo, D_B), lambda i: (_halo_next_index(tm, halo, T)(i), 0)),
                  pl.BlockSpec((B_CONV, D_B), par), pl.BlockSpec(memory_space=pl.ANY)],
        out_specs=[pl.BlockSpec((tm, 2 * D_B), lambda i: (i, 1)), pl.BlockSpec((B_CONV, D_B), par),
                   pl.BlockSpec((1, D_B), par)],
        out_shape=[jax.ShapeDtypeStruct(dz.shape, BF16), jax.ShapeDtypeStruct((B_CONV, D_B), F32),
                   jax.ShapeDtypeStruct((1, D_B), F32)],
        scratch_shapes=[pltpu.VMEM((tm + halo, D_B), F32)], aliases={4: 0},
        compiler_params=_cparams(("arbitrary",), 40),
    )(z, dcb, dcb, conv_w, dz)


def _rows_before(x, a):
    return x if a == 0 else pltpu.roll(x, a, axis=0)


def _rows_after(x, a):
    return x if a == 0 else pltpu.roll(x, x.shape[0] - a, axis=0)


def _conv3(w_ref, x, halo, cs):
    acc = w_ref[2:3, cs] * x[halo:]
    acc = acc + w_ref[1:2, cs] * _rows_before(x, 1)[halo:]
    return acc + w_ref[0:1, cs] * _rows_before(x, 2)[halo:]


def _mixer_c_fwd(z, conv_w, *, tm, name, comm=None):
    T = z.shape[0]
    D = D_MODEL
    halo = HALO_SHORT
    W = CONV_COLS

    def body(bg_ref, cg_ref, xv_ref, cgh_ref, xvh_ref, w_ref, r_ref):
        i = pl.program_id(0)
        for cb in range(D // W):
            cs = slice(cb * W, (cb + 1) * W)
            prev = jnp.where(i > 0, cgh_ref[:, cs].astype(F32) * xvh_ref[:, cs].astype(F32), 0.0)
            p = jnp.concatenate([prev, cg_ref[:, cs].astype(F32) * xv_ref[:, cs].astype(F32)], axis=0)
            r_ref[:, cs] = (bg_ref[:, cs].astype(F32) * _conv3(w_ref, p, halo, cs)).astype(BF16)

    hp = _halo_prev_index(tm, halo)
    return _pallas(
        comm, body, name=name, grid=(T // tm,),
        in_specs=[pl.BlockSpec((tm, D), lambda i: (i, 0)), pl.BlockSpec((tm, D), lambda i: (i, 1)),
                  pl.BlockSpec((tm, D), lambda i: (i, 2)),
                  pl.BlockSpec((halo, D), lambda i: (hp(i), 1)),
                  pl.BlockSpec((halo, D), lambda i: (hp(i), 2)),
                  pl.BlockSpec((None, C_CONV, D), lambda i: (0, 0, 0))],
        out_specs=pl.BlockSpec((tm, D), lambda i: (i, 0)),
        out_shape=jax.ShapeDtypeStruct((T, D), BF16),
        compiler_params=_cparams(("parallel",), 40),
    )(z, z, z, z, z, conv_w)


def _mixer_c_bwd(z, dr, conv_w, *, tm, name, comm=None):
    T = z.shape[0]
    D = D_MODEL
    halo = HALO_SHORT
    W = CONV_COLS

    def body(bg_ref, cg_ref, xv_ref, cgh_ref, xvh_ref, bgn_ref, dr_ref, drn_ref, w_ref, dz_ref, dw_ref):
        i = pl.program_id(0)
        last = pl.num_programs(0) - 1

        @pl.when(i == 0)
        def _():
            dw_ref[...] = jnp.zeros_like(dw_ref)
        for cb in range(D // W):
            cs = slice(cb * W, (cb + 1) * W)
            cg = cg_ref[:, cs].astype(F32)
            xv = xv_ref[:, cs].astype(F32)
            dr = dr_ref[:, cs].astype(F32)
            p = cg * xv
            prev = jnp.where(i > 0, cgh_ref[:, cs].astype(F32) * xvh_ref[:, cs].astype(F32), 0.0)
            q = _conv3(w_ref, jnp.concatenate([prev, p], axis=0), halo, cs)
            dz_ref[:, cs] = (dr * q).astype(BF16)
            nxt = jnp.where(i < last, drn_ref[:, cs].astype(F32) * bgn_ref[:, cs].astype(F32), 0.0)
            dq = jnp.concatenate([dr * bg_ref[:, cs].astype(F32), nxt], axis=0)
            dp = None
            for k in range(C_CONV):
                shifted = _rows_after(dq, 2 - k)[:tm]
                term = w_ref[k:k + 1, cs] * shifted
                dp = term if dp is None else dp + term
                dw_ref[k:k + 1, cs] += _rowsum(shifted * p)
            dz_ref[:, D + cb * W:D + (cb + 1) * W] = (dp * xv).astype(BF16)
            dz_ref[:, 2 * D + cb * W:2 * D + (cb + 1) * W] = (dp * cg).astype(BF16)

    hp = _halo_prev_index(tm, halo)
    hn = _halo_next_index(tm, halo, T)
    return _pallas(
        comm, body, name=name, grid=(T // tm,),
        in_specs=[pl.BlockSpec((tm, D), lambda i: (i, 0)), pl.BlockSpec((tm, D), lambda i: (i, 1)),
                  pl.BlockSpec((tm, D), lambda i: (i, 2)),
                  pl.BlockSpec((halo, D), lambda i: (hp(i), 1)),
                  pl.BlockSpec((halo, D), lambda i: (hp(i), 2)),
                  pl.BlockSpec((halo, D), lambda i: (hn(i), 0)),
                  pl.BlockSpec((tm, D), lambda i: (i, 0)),
                  pl.BlockSpec((halo, D), lambda i: (hn(i), 0)),
                  pl.BlockSpec((None, C_CONV, D), lambda i: (0, 0, 0))],
        out_specs=[pl.BlockSpec((tm, 3 * D), lambda i: (i, 0)),
                   pl.BlockSpec((C_CONV, D), lambda i: (0, 0))],
        out_shape=[jax.ShapeDtypeStruct((T, 3 * D), BF16), jax.ShapeDtypeStruct((C_CONV, D), F32)],
        compiler_params=_cparams(("arbitrary",), 48),
    )(z, z, z, z, z, z, dr, dr, conv_w)


FFN_COLS = 128


def _ffn_act_fwd(up, conv_w, *, layer, tm, name, comm=None):
    T = up.shape[0]
    halo = HALO_SHORT
    W = FFN_COLS

    def body(up_ref, uph_ref, w_ref, a_ref, upc_ref):
        i = pl.program_id(0)

        def conv(cs):
            prev = jnp.where(i > 0, uph_ref[:, cs], jnp.zeros((halo, W), BF16))
            return _conv3(w_ref, jnp.concatenate([prev, up_ref[:, cs]], axis=0).astype(F32), halo, cs)

        for cb in range(D_FF // W):
            gs = slice(cb * W, (cb + 1) * W)
            vs = slice(D_FF + cb * W, D_FF + (cb + 1) * W)
            g = conv(gs)
            v = conv(vs)
            upc_ref[:, gs] = g.astype(BF16)
            upc_ref[:, vs] = v.astype(BF16)
            a_ref[:, gs] = (_silu(g) * v).astype(BF16)

    return _pallas(
        comm, body, name=name, grid=(T // tm,),
        in_specs=[pl.BlockSpec((tm, 2 * D_FF), lambda i: (i, 0)),
                  pl.BlockSpec((halo, 2 * D_FF), lambda i: (_halo_prev_index(tm, halo)(i), 0)),
                  pl.BlockSpec((None, F_CONV, 2 * D_FF), lambda i: (layer, 0, 0))],
        out_specs=[pl.BlockSpec((tm, D_FF), lambda i: (i, 0)),
                   pl.BlockSpec((tm, 2 * D_FF), lambda i: (i, 0))],
        out_shape=[jax.ShapeDtypeStruct((T, D_FF), BF16), jax.ShapeDtypeStruct((T, 2 * D_FF), BF16)],
        compiler_params=_cparams(("parallel",), 48),
    )(up, up, conv_w)


def _ffn_act_bwd(up, upc, da, conv_w, *, layer, tm, name, comm=None):
    T = up.shape[0]
    halo = HALO_SHORT
    W = FFN_COLS

    def body(up_ref, upc_ref, upcn_ref, da_ref, dan_ref, w_ref, dup_ref, dw_ref):
        i = pl.program_id(0)
        last = pl.num_programs(0) - 1

        @pl.when(i == 0)
        def _():
            dw_ref[...] = jnp.zeros_like(dw_ref)
        live = jnp.where(i < last, 1.0, 0.0)
        for cb in range(D_FF // W):
            gs = slice(cb * W, (cb + 1) * W)
            vs = slice(D_FF + cb * W, D_FF + (cb + 1) * W)
            g = jnp.concatenate([upc_ref[:, gs], upcn_ref[:, gs]], axis=0).astype(F32)
            v = jnp.concatenate([upc_ref[:, vs], upcn_ref[:, vs]], axis=0).astype(F32)
            da = jnp.concatenate([da_ref[:, gs].astype(F32), dan_ref[:, gs].astype(F32) * live], axis=0)
            s = _sigmoid(g)
            silu = g * s
            grads = (da * v * (s * (1.0 + g * (1.0 - s))), da * silu)
            for cs, d in zip((gs, vs), grads):
                u = up_ref[:, cs].astype(F32)
                acc = None
                for k in range(F_CONV):
                    shifted = _rows_after(d, 2 - k)[:tm]
                    term = w_ref[k:k + 1, cs] * shifted
                    acc = term if acc is None else acc + term
                    dw_ref[k:k + 1, cs] += _rowsum(shifted * u)
                dup_ref[:, cs] = acc.astype(BF16)

    hn = _halo_next_index(tm, halo, T)
    return _pallas(
        comm, body, name=name, grid=(T // tm,),
        in_specs=[pl.BlockSpec((tm, 2 * D_FF), lambda i: (i, 0)),
                  pl.BlockSpec((tm, 2 * D_FF), lambda i: (i, 0)),
                  pl.BlockSpec((halo, 2 * D_FF), lambda i: (hn(i), 0)),
                  pl.BlockSpec((tm, D_FF), lambda i: (i, 0)),
                  pl.BlockSpec((halo, D_FF), lambda i: (hn(i), 0)),
                  pl.BlockSpec((None, F_CONV, 2 * D_FF), lambda i: (layer, 0, 0))],
        out_specs=[pl.BlockSpec((tm, 2 * D_FF), lambda i: (i, 0)),
                   pl.BlockSpec((F_CONV, 2 * D_FF), lambda i: (0, 0))],
        out_shape=[jax.ShapeDtypeStruct((T, 2 * D_FF), BF16),
                   jax.ShapeDtypeStruct((F_CONV, 2 * D_FF), F32)],
        compiler_params=_cparams(("arbitrary",), 56),
    )(up, upc, upc, da, da, conv_w)


def _local_step(x, tgt, small, plan):
    T = x.shape[0]
    tm_e = _pick(T, 256)
    tm_a = _pick(T, 512)
    tm_b = _pick(T, 128)
    tm_n = _pick(T, 512)
    tm = _pick(T, 1024)
    tm_f = _pick(T, 512)
    tt = _pick(T, 2048)
    nm = small["norm_mix"].reshape(2, 1, D_MODEL)
    nf = small["norm_ffn"].reshape(2, 1, D_MODEL)
    ngf = small["norm_final"].reshape(1, D_MODEL)
    b_s = small["a_b_s"].reshape(A_HEADS, CHUNK, 1)
    w_s = small["a_w_s"].reshape(A_HEADS, CHUNK, CHUNK)
    b_conv_w = small["b_conv_w"].reshape(B_CONV, D_B)
    sg = {}
    wt, cm = plan.weight, plan.comm

    h_m0 = _rmsnorm_fwd(x, nm, layer=0, tm=tm_n, name="norm_mix0")
    z_ab = _mm_nn(h_m0, wt("ab_w_in", 0), layer=0, tm=tm, tn=512, out_dtype=BF16, name="ab_in", comm=cm("ab_in"))
    yab, cb = _mixer_ab_fwd(z_ab, small["a_ln_g"], small["a_ln_b"], w_s, b_s, b_conv_w, small["b_conv_b"],
                            small["b_ln_g"], small["b_ln_b"], tm=tm_e, name="mixer_ab", comm=cm("mixer_ab"))
    x1, h_f0 = _mm_nn(yab, wt("ab_w_out", 0), layer=0, tm=tm, tn=D_MODEL, residual=x, norm=(nf, 0),
                      name="ab_out", comm=cm("ab_out"))

    def ffn_fwd(xin, h, layer, norm):
        up = _mm_nn(h, wt("f_w_up", layer), layer=0, tm=tm, tn=2 * 1408, out_dtype=BF16, name=f"ffn_up{layer}",
                    comm=cm(f"ffn_up{layer}"))
        a, upc = _ffn_act_fwd(up, small["f_conv_w"], layer=layer, tm=tm_a, name=f"ffn_act{layer}",
                              comm=cm(f"ffn_act{layer}"))
        out = _mm_nn(a, wt("f_w_down", layer), layer=0, tm=tm, tn=D_MODEL, residual=xin, norm=norm,
                     name=f"ffn_down{layer}", comm=cm(f"ffn_down{layer}"))
        return up, upc, a, out

    up0, upc0, a0, (x2, h_m1) = ffn_fwd(x1, h_f0, 0, (nm, 1))
    z_c = _mm_nn(h_m1, wt("c_w_in", 0), layer=0, tm=tm, tn=768, out_dtype=BF16, name="c_in", comm=cm("c_in"))
    r = _mixer_c_fwd(z_c, small["c_conv_w"], tm=tm_e, name="mixer_c", comm=cm("mixer_c"))
    x3, h_f1 = _mm_nn(r, wt("c_w_out", 0), layer=0, tm=tm, tn=D_MODEL, residual=x2, norm=(nf, 1),
                      name="c_out", comm=cm("c_out"))
    up1, upc1, a1, x4 = ffn_fwd(x3, h_f1, 1, None)
    loss, dx, sg["norm_final"] = _loss_head(x4, tgt, ngf, tm=tm_n, name="loss_head")

    def ffn_bwd(dx, xin, h, up, upc, a, layer):
        da = _mm_nt(dx, wt("f_w_down", layer), layer=0, tm=tm, tn=1408, out_dtype=BF16,
                    name=f"ffn_down_dx{layer}", comm=cm(f"ffn_down_dx{layer}"))
        plan.grad_ready("f_w_down", layer, _mm_tn(a, dx, shards=None, tk=1408, tn=1024, tt=tt,
                                                  name=f"ffn_down_dw{layer}", comm=cm(f"ffn_down_dw{layer}")))
        dup, dcw = _ffn_act_bwd(up, upc, da, small["f_conv_w"], layer=layer, tm=tm_b, name=f"ffn_act_bwd{layer}",
                                comm=cm(f"ffn_act_bwd{layer}"))
        dxin, dg = _mm_nt_norm(dup, wt("f_w_up", layer), xin, nf, dx, g_layer=layer, tm=tm_f,
                               name=f"ffn_up_dx{layer}", comm=cm(f"ffn_up_dx{layer}"))
        plan.grad_ready("f_w_up", layer, _mm_tn(h, dup, shards=N_CHIPS, tk=512, tn=2 * 1408, tt=tt,
                                                name=f"ffn_up_dw{layer}", comm=cm(f"ffn_up_dw{layer}")))
        return dxin, dg, dcw

    dx, dnf1, dfc1 = ffn_bwd(dx, x3, h_f1, up1, upc1, a1, 1)
    dr = _mm_nt(dx, wt("c_w_out", 0), layer=0, tm=tm, tn=512, out_dtype=BF16, name="c_out_dx", comm=cm("c_out_dx"))
    plan.grad_ready("c_w_out", 0, _mm_tn(r, dx, shards=None, tk=1024, tn=1024, tt=tt, name="c_out_dw",
                                         comm=cm("c_out_dw")))
    dz_c, dccw = _mixer_c_bwd(z_c, dr, small["c_conv_w"], tm=tm_e, name="mixer_c_bwd", comm=cm("mixer_c_bwd"))
    sg["c_conv_w"] = dccw.reshape(1, C_CONV, D_MODEL)
    plan.grad_ready("c_w_in", 0, _mm_tn(h_m1, dz_c, shards=N_CHIPS, tk=1024, tn=768, tt=tt, name="c_in_dw",
                                        comm=cm("c_in_dw")))
    dx, dnm1 = _mm_nt_norm(dz_c, wt("c_w_in", 0), x2, nm, dx, g_layer=1, tm=tm_f, name="c_in_dx",
                           comm=cm("c_in_dx"))
    dx, dnf0, dfc0 = ffn_bwd(dx, x1, h_f0, up0, upc0, a0, 0)
    dyab = _mm_nt(dx, wt("ab_w_out", 0), layer=0, tm=tm, tn=512, out_dtype=BF16, name="ab_out_dx",
                  comm=cm("ab_out_dx"))
    plan.grad_ready("ab_w_out", 0, _mm_tn(yab, dx, shards=None, tk=1024, tn=1024, tt=tt, name="ab_out_dw",
                                          comm=cm("ab_out_dw")))
    (dza, dcb, sg["a_ln_g"], sg["a_ln_b"], dws, dbs, sg["b_ln_g"], sg["b_ln_b"]) = _mixer_ab_bwd_pre(
        z_ab, cb, dyab, small["a_ln_g"], small["a_ln_b"], w_s, b_s, small["b_ln_g"], small["b_ln_b"],
        tm=tm_e, name="mixer_ab_bwd", comm=cm("mixer_ab_bwd"))
    dz_ab, dbcw, sg["b_conv_b"] = _mixer_b_conv_bwd(z_ab, dcb, b_conv_w, dza, tm=tm_e, name="mixer_b_conv_bwd",
                                                    comm=cm("mixer_b_conv_bwd"))
    sg["a_w_s"] = dws.reshape(1, A_HEADS, CHUNK, CHUNK)
    sg["a_b_s"] = dbs.reshape(1, A_HEADS, CHUNK)
    sg["b_conv_w"] = dbcw.reshape(1, B_CONV, D_B)
    plan.grad_ready("ab_w_in", 0, _mm_tn(h_m0, dz_ab, shards=N_CHIPS, tk=1024, tn=512, tt=tt, name="ab_in_dw",
                                         comm=cm("ab_in_dw")))
    dx, dnm0 = _mm_nt_norm(dz_ab, wt("ab_w_in", 0), x, nm, dx, g_layer=0, tm=tm_f, name="ab_in_dx",
                           comm=cm("ab_in_dx"))

    sg["norm_mix"] = [dnm0, dnm1]
    sg["norm_ffn"] = [dnf0, dnf1]
    sg["f_conv_w"] = [dfc0, dfc1]
    return loss, dx, sg


BLOCK_BYTES = 3 * 1024 * 1024


BF16_SUBLANES = 16


def _row_tile(rows, row_bytes, step=SUBLANES):
    best = None
    for tr in range(step, rows + 1, step):
        if rows % tr == 0 and tr * row_bytes <= BLOCK_BYTES:
            best = tr
    if best is None:
        raise ValueError(f"no row tile for {rows}")
    return best


def _place_scalars():
    x, y, c = lax.axis_index("x"), lax.axis_index("y"), lax.axis_index("c")
    return jnp.stack([c, 2 * x + y, 2 * (1 - x) + y, 2 * x + (1 - y), 2 * (1 - x) + (1 - y)]).astype(jnp.int32)


def _cast_into_slot(w, place, *, layer, paired, name):
    L, rows, cols = w.shape
    tr = _row_tile(rows, cols * 4, BF16_SUBLANES)

    def body(place_ref, w_ref, o_ref):
        o_ref[...] = w_ref[...].astype(BF16)

    if paired:
        out_spec = pl.BlockSpec((None, None, tr, cols), lambda i, p: (0, p[1] // 2, i, p[1] % 2))
        out_shape = jax.ShapeDtypeStruct((1, N_CHIPS // 2, rows, 2 * cols), BF16)
    else:
        out_spec = pl.BlockSpec((None, None, tr, cols), lambda i, p: (0, p[1], i, 0))
        out_shape = jax.ShapeDtypeStruct((1, N_CHIPS, rows, cols), BF16)
    return pl.pallas_call(
        body, name=name,
        grid_spec=pltpu.PrefetchScalarGridSpec(
            num_scalar_prefetch=1, grid=(rows // tr,),
            in_specs=[pl.BlockSpec((None, tr, cols), lambda i, p: (layer, i, 0))],
            out_specs=out_spec),
        out_shape=out_shape,
        compiler_params=_cparams(("parallel",), 32),
    )(place, w)


def _pair_sum(g, theirs, place, *, name):
    S, rows, cols = g.shape
    half = rows // 2
    tr = _row_tile(half, cols * 4, BF16_SUBLANES)
    nb = half // tr

    def body(place_ref, g_ref, t_ref, o_ref):
        o_ref[...] = (g_ref[...] + t_ref[...]).astype(BF16)

    spec = pl.BlockSpec((None, tr, cols), lambda s, i, p: (s, i, 0))
    return pl.pallas_call(
        body, name=name,
        grid_spec=pltpu.PrefetchScalarGridSpec(
            num_scalar_prefetch=1, grid=(S, nb),
            in_specs=[pl.BlockSpec((None, tr, cols), lambda s, i, p: (s, p[0] * nb + i, 0)), spec],
            out_specs=spec),
        out_shape=jax.ShapeDtypeStruct((S, half, cols), BF16),
        compiler_params=_cparams(("parallel", "parallel"), 32),
    )(place, g, theirs)


def _chip_sum(p, r, g_prev, place, *, layer, shape, name):
    L, rows, cols = shape
    half = rows // 2
    tr = _row_tile(half, cols * 4, BF16_SUBLANES)
    nb = half // tr

    def body(place_ref, p_ref, r_ref, *rest):
        o_ref = rest[-1]
        mine = p_ref[...].astype(F32)
        peers = [r_ref[j].astype(F32) for j in range(3)]
        acc = None
        for s in range(N_CHIPS):
            term = jnp.where(place_ref[1] == s, mine,
                             jnp.where(place_ref[2] == s, peers[0],
                                       jnp.where(place_ref[3] == s, peers[1], peers[2])))
            acc = term if acc is None else acc + term
        o_ref[...] = acc

    in_specs = [pl.BlockSpec((None, tr, cols), lambda i, pr: (pr[1], i, 0)),
                pl.BlockSpec((3, tr, cols), lambda i, pr: (0, i, 0))]
    args = [place, p, r]
    aliases = {}
    if g_prev is not None:
        in_specs.append(ANY)
        args.append(g_prev)
        aliases = {3: 0}
    return pl.pallas_call(
        body, name=name,
        grid_spec=pltpu.PrefetchScalarGridSpec(
            num_scalar_prefetch=1, grid=(nb,), in_specs=in_specs,
            out_specs=pl.BlockSpec((None, tr, cols), lambda i, pr: (layer, pr[0] * nb + i, 0))),
        out_shape=jax.ShapeDtypeStruct(shape, F32), input_output_aliases=aliases,
        compiler_params=_cparams(("parallel",), 32),
    )(*args)


def _adamw_math(w, g, m, v):
    m2 = ADAM_B1 * m + (1.0 - ADAM_B1) * g
    v2 = ADAM_B2 * v + (1.0 - ADAM_B2) * (g * g)
    m_hat = m2 / (1.0 - ADAM_B1 ** ADAM_STEP)
    v_hat = v2 / (1.0 - ADAM_B2 ** ADAM_STEP)
    delta = -ADAM_LR * (m_hat / (jnp.sqrt(v_hat) + ADAM_EPS) + ADAM_WD * w)
    return delta, m2, v2


def _adamw(w, g, m, v, *, name):
    L, rows, cols = w.shape
    tr = _row_tile(rows, cols * 4)

    def body(w_ref, g_ref, m_ref, v_ref, d_ref, m2_ref, v2_ref):
        d, m2, v2 = _adamw_math(w_ref[...], g_ref[...], m_ref[...], v_ref[...])
        d_ref[...] = d
        m2_ref[...] = m2
        v2_ref[...] = v2

    spec = pl.BlockSpec((None, tr, cols), lambda l, i: (l, i, 0))
    shape = jax.ShapeDtypeStruct(w.shape, F32)
    return pl.pallas_call(
        body, name=name, grid=(L, rows // tr), in_specs=[spec] * 4, out_specs=[spec] * 3,
        out_shape=[shape] * 3,
        compiler_params=_cparams(("parallel", "parallel"), 48),
    )(w, g, m, v)


def _allreduce_pack(pack, *, name, comm):
    R = pack.shape[0]
    half = R // 2
    nr, nw = len(comm.reads), len(comm.writes)

    def body(*refs):
        p_ref, rd, wr_in = refs[0], refs[1:1 + nr], refs[1 + nr:1 + nr + nw]
        o_ref, wr_out = refs[1 + nr + nw], refs[2 + nr + nw:2 + nr + 2 * nw]
        sib_ref, chip_ref, parts_ref, sems, comm_sems = refs[2 + nr + 2 * nw:]
        src = dict(zip(comm.reads, rd))
        src.update(zip(comm.writes, wr_in))
        dst = dict(zip(comm.writes, wr_out))
        comm.start(src, dst, comm_sems)
        x, y, c, k, sib, peers = _place()
        swap = _remote(p_ref, sib_ref, sems.at[0, 0], sems.at[0, 1], sib)
        swap.start()
        swap.wait()
        chip_ref[...] = p_ref[...] + sib_ref[...]
        mine = chip_ref.at[pl.ds(pl.multiple_of(c * half, SUBLANES), half)]
        sends = [_remote(mine, parts_ref.at[j], sems.at[1 + j, 0], sems.at[1 + j, 1], (px, py, c))
                 for j, (px, py) in enumerate(peers)]
        for rc in sends:
            rc.start()
        for rc in sends:
            rc.wait()
        own = mine[...]
        others = [parts_ref[j] for j in range(3)]
        acc = None
        for s in range(N_CHIPS):
            term = own
            for j, (px, py) in enumerate(peers):
                term = jnp.where(2 * px + py == s, others[j], term)
            acc = term if acc is None else acc + term
        done = o_ref.at[pl.ds(pl.multiple_of(c * half, SUBLANES), half)]
        done[...] = acc
        theirs = o_ref.at[pl.ds(pl.multiple_of((1 - c) * half, SUBLANES), half)]
        share = _remote(done, done, sems.at[4, 0], sems.at[4, 1], sib)
        share.start()
        _remote(done, theirs, sems.at[4, 0], sems.at[4, 1], sib).wait()
        comm.finish(src, dst, comm_sems)

    vm = pl.BlockSpec(memory_space=pltpu.VMEM)
    operands, shapes = _comm_operands(comm)
    outs = pl.pallas_call(
        body, name=name, in_specs=[vm] + [ANY] * (nr + nw), out_specs=[vm] + [ANY] * nw,
        out_shape=[jax.ShapeDtypeStruct((R, LANES), F32)] + shapes,
        input_output_aliases={1 + nr + q: 1 + q for q in range(nw)},
        scratch_shapes=[pltpu.VMEM((R, LANES), F32), pltpu.VMEM((R, LANES), F32),
                        pltpu.VMEM((3, half, LANES), F32), pltpu.SemaphoreType.DMA((5, 2)),
                        pltpu.SemaphoreType.DMA((comm.ncopies, 2))],
        compiler_params=pltpu.CompilerParams(vmem_limit_bytes=VMEM_BYTES_MAX),
    )(pack, *operands)
    for q, n in enumerate(comm.writes):
        comm.plan.bufs[n] = outs[1 + q]
    return outs[0]


PACK_UNIT = SUBLANES * LANES


def _pack(arrays):
    flat, sizes = [], []
    for a in arrays:
        pieces = a if isinstance(a, (list, tuple)) else [a]
        v = jnp.concatenate([p.reshape(-1) for p in pieces]) if len(pieces) > 1 else pieces[0].reshape(-1)
        size = v.shape[0]
        padded = -(-size // PACK_UNIT) * PACK_UNIT
        flat.append(jnp.pad(v, (0, padded - size)))
        sizes.append((size, padded))
    total = sum(p for _, p in sizes)
    if (total // PACK_UNIT) % 2:
        flat.append(jnp.zeros((PACK_UNIT,), F32))
    return jnp.concatenate(flat).reshape(-1, LANES), sizes


def _unpack(pack, sizes, shapes):
    v = pack.reshape(-1)
    out, off = [], 0
    for (size, padded), shape in zip(sizes, shapes):
        out.append(v[off:off + size].reshape(shape))
        off += padded
    return out


BIG = ("ab_w_in", "ab_w_out", "c_w_in", "c_w_out", "f_w_up", "f_w_down")
COL_SHARDED = ("ab_w_in", "c_w_in", "f_w_up")
PAIRED = ("f_w_up",)
SMALL_REPLICATED = ("norm_mix", "norm_ffn", "norm_final", "a_ln_g", "a_ln_b", "a_w_s", "a_b_s",
                    "b_conv_b", "b_ln_g", "b_ln_b")
SMALL_SHARDED = ("b_conv_w", "c_conv_w", "f_conv_w")
SMALL = SMALL_REPLICATED + SMALL_SHARDED
ALL_WEIGHTS = ("norm_mix", "norm_ffn", "norm_final", "ab_w_in", "a_ln_g", "a_ln_b", "a_w_s", "a_b_s",
               "b_conv_w", "b_conv_b", "b_ln_g", "b_ln_b", "ab_w_out", "c_w_in", "c_conv_w", "c_w_out",
               "f_w_up", "f_conv_w", "f_w_down")


SCHEDULE = {
    "ab_in": [("gi", "f_w_up", 0, 0, 4), ("gi", "ab_w_out", 0)],
    "mixer_ab": [("gd", "f_w_up", 0, 0, 4), ("gd", "ab_w_out", 0), ("gi", "f_w_up", 0, 1, 4),
                 ("gi", "f_w_up", 0, 2, 4), ("gi", "f_w_up", 0, 3, 4)],
    "ab_out": [("gd", "f_w_up", 0, 1, 4), ("gd", "f_w_up", 0, 2, 4), ("gd", "f_w_up", 0, 3, 4)],
    "ffn_up0": [("gi", "f_w_down", 0), ("gi", "c_w_in", 0, 0, 2)],
    "ffn_act0": [("gd", "f_w_down", 0), ("gd", "c_w_in", 0, 0, 2), ("gi", "c_w_in", 0, 1, 2),
                 ("gi", "f_w_up", 1, 0, 4), ("gi", "f_w_up", 1, 1, 4)],
    "ffn_down0": [("gd", "c_w_in", 0, 1, 2), ("gd", "f_w_up", 1, 0, 4), ("gd", "f_w_up", 1, 1, 4),
                  ("gi", "f_w_up", 1, 2, 4)],
    "c_in": [("gd", "f_w_up", 1, 2, 4), ("gi", "f_w_up", 1, 3, 4), ("gi", "c_w_out", 0)],
    "mixer_c": [("gd", "f_w_up", 1, 3, 4), ("gd", "c_w_out", 0), ("gi", "f_w_down", 1, 0, 2)],
    "c_out": [("gd", "f_w_down", 1, 0, 2), ("gi", "f_w_down", 1, 1, 2)],
    "ffn_up1": [("gd", "f_w_down", 1, 1, 2)],
    "ffn_act_bwd1": [("px", "f_w_down", 1)],
    "ffn_up_dx1": [("cx", "f_w_down", 1)],
    "mixer_c_bwd": [("px", "f_w_up", 1), ("px", "c_w_out", 0)],
    "c_in_dw": [("cx", "f_w_up", 1, 0, 2), ("cx", "c_w_out", 0)],
    "c_in_dx": [("cx", "f_w_up", 1, 1, 2), ("px", "c_w_in", 0)],
    "ffn_down_dx0": [("cx", "c_w_in", 0, 0, 2)],
    "ffn_down_dw0": [("cx", "c_w_in", 0, 1, 2)],
    "ffn_act_bwd0": [("px", "f_w_down", 0)],
    "ffn_up_dx0": [("cx", "f_w_down", 0)],
    "mixer_ab_bwd": [("px", "f_w_up", 0), ("px", "ab_w_out", 0)],
    "mixer_b_conv_bwd": [("cx", "f_w_up", 0, 0, 2), ("cx", "ab_w_out", 0)],
    "ab_in_dw": [("cx", "f_w_up", 0, 1, 2)],
    "ab_in_dx": [("px", "ab_w_in", 0)],
}


class _Plan:
    def __init__(self, shapes, place):
        self.shapes, self.place, self.bufs = shapes, place, {}

    def weight(self, name, layer):
        g = self.bufs[f"w:{name}:{layer}"]
        if name in COL_SHARDED:
            return g
        _, S, rows, cols = g.shape
        return g.reshape(1, S * rows, cols)

    def grad_ready(self, name, layer, g):
        _, rows, cols = self.shapes[name]
        hbm = lambda a: pltpu.with_memory_space_constraint(a, pltpu.HBM)
        self.bufs[f"g:{name}:{layer}"] = g.reshape(N_CHIPS, rows, cols)
        self.bufs[f"t:{name}:{layer}"] = hbm(lax.empty((N_CHIPS, rows // 2, cols), F32))
        self.bufs[f"l:{name}:{layer}"] = hbm(lax.empty((3, rows // 2, cols), BF16))

    def job(self, kind, name, layer, part=0, parts=1):
        _, rows, cols = self.shapes[name]
        key = f"{name}:{layer}"
        if kind == "gi":
            return _job_gather_ici("w:" + key, rows, cols, part, parts)
        if kind == "gd":
            return _job_gather_d2d("w:" + key, rows, cols, part, parts)
        if kind == "px":
            return _job_pair_exchange("g:" + key, "t:" + key, rows)
        if kind == "cx":
            if "p:" + key not in self.bufs:
                self.bufs["p:" + key] = _pair_sum(self.bufs["g:" + key], self.bufs["t:" + key], self.place,
                                                  name=f"pair_sum_{name}{layer}")
            nr = rows // 2 // parts
            return _job_chip_exchange("p:" + key, "l:" + key, part * nr, nr)
        if kind == "ps":
            return _job_pair_share("G:" + name, layer, rows)
        raise ValueError(kind)

    def comm(self, call):
        specs = SCHEDULE.get(call)
        return None if specs is None else _Comm(self, [self.job(*spec) for spec in specs])


def _step(x, tgt, w, m, v):
    chip = 2 * lax.axis_index("x") + lax.axis_index("y")
    place = _place_scalars()
    plan = _Plan({n: w[n].shape for n in BIG}, place)
    items = [(n, l) for n in BIG for l in range(w[n].shape[0])]

    for n, l in items:
        plan.bufs[f"w:{n}:{l}"] = _cast_into_slot(w[n], place, layer=l, paired=n in PAIRED, name=f"cast_{n}{l}")
    conv_pack, conv_sizes = _pack([w[n] for n in SMALL_SHARDED])
    hbm = lambda a: pltpu.with_memory_space_constraint(a, pltpu.HBM)
    plan.bufs["conv:mine"] = hbm(conv_pack)
    plan.bufs["conv:all"] = hbm(lax.empty((N_CHIPS,) + conv_pack.shape, F32))
    _comm_only(plan, [[plan.job("gi", "ab_w_in", 0), _job_chip_gather("conv:mine", "conv:all")],
                      [plan.job("gd", "ab_w_in", 0)]], name="gather_first")
    conv_shapes = [w[n].shape for n in SMALL_SHARDED]
    per_chip = [_unpack(plan.bufs["conv:all"][s], conv_sizes, conv_shapes) for s in range(N_CHIPS)]
    small = {n: w[n] for n in SMALL_REPLICATED}
    for idx, n in enumerate(SMALL_SHARDED):
        small[n] = jnp.concatenate([jnp.where(chip == s, w[n], per_chip[s][idx]) for s in range(N_CHIPS)], axis=-1)

    loss, dx, sg = _local_step(x, tgt, small, plan)

    g_pack, g_sizes = _pack([sg[n] for n in SMALL] + [loss])
    g_sum = _allreduce_pack(g_pack, name="allreduce_small_grads",
                            comm=_Comm(plan, [plan.job("cx", "ab_w_in", 0)]))
    full_shapes = [small[n].shape for n in SMALL]
    *summed, loss = _unpack(g_sum, g_sizes, full_shapes + [(1, 1)])
    g_small = dict(zip(SMALL, summed))
    for n in SMALL_SHARDED:
        width = w[n].shape[-1]
        g_small[n] = lax.dynamic_slice_in_dim(g_small[n], chip * width, width, axis=g_small[n].ndim - 1)

    for n, l in items:
        plan.bufs["G:" + n] = _chip_sum(plan.bufs[f"p:{n}:{l}"], plan.bufs[f"l:{n}:{l}"], plan.bufs.get("G:" + n),
                                        place, layer=l, shape=w[n].shape, name=f"chip_sum_{n}{l}")
    _comm_only(plan, [[plan.job("ps", n, l) for n, l in items]], name="reduce_pair_share")
    grads_big = [plan.bufs["G:" + n] for n in BIG]

    grad, delta, new_m, new_v = {}, {}, {}, {}
    for n, g in zip(BIG, grads_big):
        grad[n] = g
        delta[n], new_m[n], new_v[n] = _adamw(w[n], g, m[n], v[n], name=f"adamw_{n}")
    shapes = [w[n].shape for n in SMALL]
    wp, sizes = _pack([w[n] for n in SMALL])
    gp, _ = _pack([g_small[n] for n in SMALL])
    mp, _ = _pack([m[n] for n in SMALL])
    vp, _ = _pack([v[n] for n in SMALL])
    R = wp.shape[0]
    dp, m2p, v2p = _adamw(wp.reshape(1, R, LANES), gp.reshape(1, R, LANES), mp.reshape(1, R, LANES),
                          vp.reshape(1, R, LANES), name="adamw_small")
    for n, d_, m_, v_ in zip(SMALL, _unpack(dp, sizes, shapes), _unpack(m2p, sizes, shapes),
                             _unpack(v2p, sizes, shapes)):
        grad[n] = g_small[n]
        delta[n], new_m[n], new_v[n] = d_, m_, v_
    return loss, dx, grad, delta, new_m, new_v


def kernel(x, norm_mix, norm_ffn, norm_final, ab_w_in, a_ln_g, a_ln_b, a_w_s, a_b_s, b_conv_w, b_conv_b, b_ln_g, b_ln_b, ab_w_out, c_w_in, c_conv_w, c_w_out, f_w_up, f_conv_w, f_w_down, loss_target, m_norm_mix, m_norm_ffn, m_norm_final, m_ab_w_in, m_a_ln_g, m_a_ln_b, m_a_w_s, m_a_b_s, m_b_conv_w, m_b_conv_b, m_b_ln_g, m_b_ln_b, m_ab_w_out, m_c_w_in, m_c_conv_w, m_c_w_out, m_f_w_up, m_f_conv_w, m_f_w_down, v_norm_mix, v_norm_ffn, v_norm_final, v_ab_w_in, v_a_ln_g, v_a_ln_b, v_a_w_s, v_a_b_s, v_b_conv_w, v_b_conv_b, v_b_ln_g, v_b_ln_b, v_ab_w_out, v_c_w_in, v_c_conv_w, v_c_w_out, v_f_w_up, v_f_conv_w, v_f_w_down):
    given = dict(locals())
    w = {n: given[n] for n in ALL_WEIGHTS}
    m = {n: given["m_" + n] for n in ALL_WEIGHTS}
    v = {n: given["v_" + n] for n in ALL_WEIGHTS}
    T = x.shape[1]
    loss, dx, grad, delta, new_m, new_v = _step(x.reshape(T, D_MODEL), loss_target.reshape(T, D_MODEL), w, m, v)
    out = [loss[0, 0], dx.reshape(x.shape)]
    for d in (grad, delta, new_m, new_v):
        out += [d[n] for n in ALL_WEIGHTS]
    return tuple(out)
```

```python
import functools
import math

import jax
import jax.numpy as jnp
from jax import lax
from jax.experimental import pallas as pl
from jax.experimental.pallas import tpu as pltpu

F32 = jnp.float32
BF16 = jnp.bfloat16

EPS = 1e-6
D_MODEL = 1024
CHUNK = 128
HEAD_DIM = 128
A_HEADS = 4
D_A = 512
D_B = 512
B_CONV = 31
C_CONV = 3
D_FF = 2816
F_CONV = 3
N_CHIPS = 4

ADAM_LR = 0.001
ADAM_B1 = 0.9
ADAM_B2 = 0.999
ADAM_EPS = 1e-08
ADAM_WD = 0.01
ADAM_STEP = 10

SUBLANES = 8
LANES = 128
HALO_SHORT = 16
HALO_LONG = 32
VMEM_BYTES_MAX = 60000 * 1024

INV_SQRT2 = 1.0 / math.sqrt(2.0)
INV_SQRT_2PI = 1.0 / math.sqrt(2.0 * math.pi)

MESH = pl.DeviceIdType.MESH


def _cparams(sem, vmem_mb):
    del vmem_mb
    return pltpu.CompilerParams(dimension_semantics=sem, vmem_limit_bytes=VMEM_BYTES_MAX)


def _pick(total, pref):
    for c in (2048, 1024, 512, 256, 128):
        if c <= pref and total % c == 0:
            return c
    raise ValueError(f"no tile for {total}")


def _sigmoid(x):
    return jax.nn.sigmoid(x)


def _silu(x):
    return x * _sigmoid(x)


def _dsilu(x):
    s = _sigmoid(x)
    return s * (1.0 + x * (1.0 - s))


def _gelu(x):
    return 0.5 * x * (1.0 + lax.erf(x * INV_SQRT2))


def _dgelu(x):
    return 0.5 * (1.0 + lax.erf(x * INV_SQRT2)) + x * jnp.exp(-0.5 * x * x) * INV_SQRT_2PI


def _ln_stats(x):
    mu = jnp.mean(x, axis=-1, keepdims=True)
    xc = x - mu
    var = jnp.mean(xc * xc, axis=-1, keepdims=True)
    r = lax.rsqrt(var + EPS)
    return xc * r, r


def _ln_bwd(dy, xh, r, g):
    dxh = dy * g
    m1 = jnp.mean(dxh, axis=-1, keepdims=True)
    m2 = jnp.mean(dxh * xh, axis=-1, keepdims=True)
    return r * (dxh - m1 - xh * m2)


def _rowsum(x):
    return jnp.sum(x, axis=0, keepdims=True)


ANY = pl.BlockSpec(memory_space=pltpu.HBM)


def _place():
    x, y, c = lax.axis_index("x"), lax.axis_index("y"), lax.axis_index("c")
    peers = [(1 - x, y), (x, 1 - y), (1 - x, 1 - y)]
    return x, y, c, 2 * x + y, (x, y, 1 - c), peers


def _half(rows, which):
    return pl.ds(which * (rows // 2), rows // 2)


def _remote(src, dst, send_sem, recv_sem, device):
    return pltpu.make_async_remote_copy(src_ref=src, dst_ref=dst, send_sem=send_sem, recv_sem=recv_sem,
                                        device_id=device, device_id_type=MESH)


class _Job:
    def __init__(self, reads, writes, ncopies, copies):
        self.reads, self.writes, self.ncopies, self.copies = reads, writes, ncopies, copies


def _share(rows, which, part, parts):
    nr = rows // 2 // parts
    return pl.ds(which * (rows // 2) + part * nr, nr)


def _slot(ref, chip, rows, cols):
    if ref.shape[1] == N_CHIPS:
        return ref.at[0, chip, rows]
    return ref.at[0, chip // 2, rows, pl.ds(pl.multiple_of((chip % 2) * cols, LANES), cols)]


def _job_gather_ici(name, rows, cols, part, parts):
    def copies(src, dst, sem):
        x, y, c, k, sib, peers = _place()
        mine_rows = _share(rows, c, part, parts)
        out = []
        for j, (px, py) in enumerate(peers):
            mine = _slot(src[name], k, mine_rows, cols)
            out.append((_remote(mine, _slot(dst[name], k, mine_rows, cols), sem(j, 0), sem(j, 1), (px, py, c)),
                        _remote(mine, _slot(dst[name], 2 * px + py, mine_rows, cols), sem(j, 0), sem(j, 1),
                                (px, py, c))))
        return out
    return _Job([], [name], 3, copies)


def _job_gather_d2d(name, rows, cols, part, parts):
    def copies(src, dst, sem):
        x, y, c, k, sib, peers = _place()
        out = []
        for j, (px, py) in enumerate(peers):
            mine_rows, their_rows = _share(rows, c, part, parts), _share(rows, 1 - c, part, parts)
            landed = _slot(src[name], 2 * px + py, mine_rows, cols)
            out.append((_remote(landed, _slot(dst[name], 2 * px + py, mine_rows, cols), sem(j, 0), sem(j, 1), sib),
                        _remote(landed, _slot(dst[name], 2 * px + py, their_rows, cols), sem(j, 0), sem(j, 1), sib)))
        return out
    return _Job([], [name], 3, copies)


def _job_chip_gather(sname, dname):
    def copies(src, dst, sem):
        x, y, c, k, sib, peers = _place()
        return [(_remote(src[sname], dst[dname].at[k], sem(j, 0), sem(j, 1), (px, py, c)),
                 _remote(src[sname], dst[dname].at[2 * px + py], sem(j, 0), sem(j, 1), (px, py, c)))
                for j, (px, py) in enumerate(peers)]
    return _Job([sname], [dname], 3, copies)


def _job_pair_exchange(gname, tname, rows):
    def copies(src, dst, sem):
        x, y, c, k, sib, peers = _place()
        cp = _remote(src[gname].at[:, _half(rows, 1 - c), :], dst[tname], sem(0, 0), sem(0, 1), sib)
        return [(cp, cp)]
    return _Job([gname], [tname], 1, copies)


def _job_chip_exchange(pname, lname, r0, nr):
    def copies(src, dst, sem):
        x, y, c, k, sib, peers = _place()
        out = []
        for j, (px, py) in enumerate(peers):
            cp = _remote(src[pname].at[2 * px + py, pl.ds(r0, nr)], dst[lname].at[j, pl.ds(r0, nr)],
                         sem(j, 0), sem(j, 1), (px, py, c))
            out.append((cp, cp))
        return out
    return _Job([pname], [lname], 3, copies)


def _job_pair_share(name, layer, rows):
    def copies(src, dst, sem):
        x, y, c, k, sib, peers = _place()
        mine = src[name].at[layer, _half(rows, c)]
        return [(_remote(mine, dst[name].at[layer, _half(rows, c)], sem(0, 0), sem(0, 1), sib),
                 _remote(mine, dst[name].at[layer, _half(rows, 1 - c)], sem(0, 0), sem(0, 1), sib))]
    return _Job([], [name], 1, copies)


class _Comm:
    def __init__(self, plan, jobs):
        self.plan, self.jobs = plan, jobs
        self.writes, self.reads = [], []
        for job in jobs:
            for n in job.writes:
                if n not in self.writes:
                    self.writes.append(n)
        for job in jobs:
            for n in job.reads:
                if n not in self.writes and n not in self.reads:
                    self.reads.append(n)
        self.ncopies = sum(job.ncopies for job in jobs)

    def descriptors(self, src, dst, sems, base):
        out = []
        for job in self.jobs:
            sem = lambda j, which, base=base: sems.at[base + j, which]
            out += job.copies(src, dst, sem)
            base += job.ncopies
        return out

    def start(self, src, dst, sems, base=0):
        for first, _ in self.descriptors(src, dst, sems, base):
            first.start()

    def finish(self, src, dst, sems, base=0):
        for _, landed in self.descriptors(src, dst, sems, base):
            landed.wait()


def _comm_operands(comm):
    bufs = comm.plan.bufs
    shapes = [jax.ShapeDtypeStruct(bufs[n].shape, bufs[n].dtype) for n in comm.writes]
    return [bufs[n] for n in comm.reads] + [bufs[n] for n in comm.writes], shapes


def _pallas(comm, body, *, name, grid, in_specs, out_specs, out_shape, compiler_params, scratch_shapes=(),
            aliases=None):
    aliases = dict(aliases or {})
    if comm is None:
        return pl.pallas_call(body, name=name, grid=grid, in_specs=in_specs, out_specs=out_specs,
                              out_shape=out_shape, scratch_shapes=list(scratch_shapes),
                              input_output_aliases=aliases, compiler_params=compiler_params)
    single = not isinstance(out_shape, (list, tuple))
    base_specs = [out_specs] if single else list(out_specs)
    base_shape = [out_shape] if single else list(out_shape)
    nb, nr, nw, nbo, nsc = len(in_specs), len(comm.reads), len(comm.writes), len(base_specs), len(scratch_shapes)

    def wrapped(*refs):
        base_in, rd, wr_in = refs[:nb], refs[nb:nb + nr], refs[nb + nr:nb + nr + nw]
        o0 = nb + nr + nw
        base_out, wr_out = refs[o0:o0 + nbo], refs[o0 + nbo:o0 + nbo + nw]
        scratch, sems = refs[o0 + nbo + nw:o0 + nbo + nw + nsc], refs[-1]
        src = dict(zip(comm.reads, rd))
        src.update(zip(comm.writes, wr_in))
        dst = dict(zip(comm.writes, wr_out))
        first = functools.reduce(jnp.logical_and, [pl.program_id(a) == 0 for a in range(len(grid))])
        last = functools.reduce(jnp.logical_and,
                                [pl.program_id(a) == pl.num_programs(a) - 1 for a in range(len(grid))])

        @pl.when(first)
        def _():
            comm.start(src, dst, sems)
        body(*base_in, *base_out, *scratch)

        @pl.when(last)
        def _():
            comm.finish(src, dst, sems)

    operands, shapes = _comm_operands(comm)
    call = pl.pallas_call(
        wrapped, name=name, grid=grid, in_specs=list(in_specs) + [ANY] * (nr + nw),
        out_specs=base_specs + [ANY] * nw, out_shape=base_shape + shapes,
        input_output_aliases={**aliases, **{nb + nr + q: nbo + q for q in range(nw)}},
        scratch_shapes=list(scratch_shapes) + [pltpu.SemaphoreType.DMA((comm.ncopies, 2))],
        compiler_params=compiler_params)

    def run(*args):
        outs = call(*args, *operands)
        for q, n in enumerate(comm.writes):
            comm.plan.bufs[n] = outs[nbo + q]
        return outs[0] if single else list(outs[:nbo])

    return run


def _comm_only(plan, phases, *, name):
    comms = [_Comm(plan, jobs) for jobs in phases]
    both = _Comm(plan, [job for jobs in phases for job in jobs])
    nr, nw = len(both.reads), len(both.writes)

    def body(*refs):
        rd, wr_in, wr_out, sems = refs[:nr], refs[nr:nr + nw], refs[nr + nw:nr + 2 * nw], refs[-1]
        src = dict(zip(both.reads, rd))
        src.update(zip(both.writes, wr_in))
        dst = dict(zip(both.writes, wr_out))
        base = 0
        for comm in comms:
            comm.start(src, dst, sems, base)
            comm.finish(src, dst, sems, base)
            base += comm.ncopies

    operands, shapes = _comm_operands(both)
    outs = pl.pallas_call(
        body, name=name, in_specs=[ANY] * (nr + nw), out_specs=[ANY] * nw, out_shape=shapes,
        input_output_aliases={nr + q: q for q in range(nw)},
        scratch_shapes=[pltpu.SemaphoreType.DMA((both.ncopies, 2))],
    )(*operands)
    for q, n in enumerate(both.writes):
        plan.bufs[n] = outs[q]


def _mm_nn(a, w, *, layer, tm, tn, residual=None, norm=None, out_dtype=F32, name, comm=None):
    T, K = a.shape
    if w.ndim == 4:
        _, S, _, n4 = w.shape
        N = S * n4
        bps = n4 // tn
        w_spec = pl.BlockSpec((None, None, K, tn), lambda j, i: (layer, j // bps, 0, j % bps))
    else:
        N = w.shape[2]
        w_spec = pl.BlockSpec((None, K, tn), lambda j, i: (layer, 0, j))
    in_specs = [pl.BlockSpec((tm, K), lambda j, i: (i, 0)), w_spec]
    args = [a, w]
    if residual is not None:
        in_specs.append(pl.BlockSpec((tm, tn), lambda j, i: (i, j)))
        args.append(residual)
    out_specs = pl.BlockSpec((tm, tn), lambda j, i: (i, j))
    out_shape = jax.ShapeDtypeStruct((T, N), out_dtype)
    if norm is not None:
        assert tn == N
        g, norm_layer = norm
        in_specs.append(pl.BlockSpec((None, 1, N), lambda j, i: (norm_layer, 0, 0)))
        args.append(g)
        out_specs = [out_specs, pl.BlockSpec((tm, tn), lambda j, i: (i, j))]
        out_shape = [out_shape, jax.ShapeDtypeStruct((T, N), BF16)]

    def body(*refs):
        a_ref, w_ref = refs[0], refs[1]
        acc = jnp.dot(a_ref[...].astype(BF16), w_ref[...], preferred_element_type=F32)
        if residual is not None:
            acc = refs[2][...] + acc
        if norm is None:
            refs[-1][...] = acc.astype(out_dtype)
        else:
            refs[-2][...] = acc.astype(out_dtype)
            r = lax.rsqrt(jnp.mean(acc * acc, axis=-1, keepdims=True) + EPS)
            refs[-1][...] = (acc * r * refs[-3][...]).astype(BF16)

    return _pallas(
        comm, body, name=name, grid=(N // tn, T // tm), in_specs=in_specs,
        out_specs=out_specs, out_shape=out_shape,
        compiler_params=_cparams(("parallel", "parallel"), 48),
    )(*args)


def _mm_nt(dy, w, *, layer, tm, tn, name, out_dtype=F32, comm=None):
    T = dy.shape[0]
    nt_dims = (((1,), (1,)), ((), ()))
    _, R, N = w.shape

    def body2(dy_ref, w_ref, o_ref):
        o_ref[...] = lax.dot_general(dy_ref[...].astype(BF16), w_ref[...], nt_dims,
                                     preferred_element_type=F32).astype(out_dtype)

    return _pallas(
        comm, body2, name=name, grid=(R // tn, T // tm),
        in_specs=[pl.BlockSpec((tm, N), lambda j, i: (i, 0)),
                  pl.BlockSpec((None, tn, N), lambda j, i: (layer, j, 0))],
        out_specs=pl.BlockSpec((tm, tn), lambda j, i: (i, j)),
        out_shape=jax.ShapeDtypeStruct((T, R), out_dtype),
        compiler_params=_cparams(("parallel", "parallel"), 48),
    )(dy, w)


def _mm_tn(a, dy, *, shards, tk, tn, tt, name, comm=None):
    T, K = a.shape
    N = dy.shape[1]
    tn_dims = (((0,), (0,)), ((), ()))
    n4 = N if shards is None else N // shards
    span = max(tn // n4, 1)

    def body(a_ref, dy_ref, o_ref):
        @pl.when(pl.program_id(2) == 0)
        def _():
            o_ref[...] = jnp.zeros_like(o_ref)
        r = lax.dot_general(a_ref[...].astype(BF16), dy_ref[...].astype(BF16), tn_dims,
                            preferred_element_type=F32)
        if span == 1:
            o_ref[...] += r
        else:
            for q in range(span):
                o_ref[q] += r[:, q * n4:(q + 1) * n4]

    if shards is None:
        out_spec = pl.BlockSpec((tk, tn), lambda k, n, t: (k, n))
        out_shape = jax.ShapeDtypeStruct((K, N), F32)
    elif span > 1:
        out_spec = pl.BlockSpec((span, tk, n4), lambda k, n, t: (n, k, 0))
        out_shape = jax.ShapeDtypeStruct((shards, K, n4), F32)
    else:
        bps = n4 // tn
        out_spec = pl.BlockSpec((None, tk, tn), lambda k, n, t: (n // bps, k, n % bps))
        out_shape = jax.ShapeDtypeStruct((shards, K, n4), F32)
    return _pallas(
        comm, body, name=name, grid=(K // tk, N // tn, T // tt),
        in_specs=[pl.BlockSpec((tt, tk), lambda k, n, t: (t, k)),
                  pl.BlockSpec((tt, tn), lambda k, n, t: (t, n))],
        out_specs=out_spec, out_shape=out_shape,
        compiler_params=_cparams(("parallel", "parallel", "arbitrary"), 48),
    )(a, dy)


def _rmsnorm_fwd(x, g, *, layer, tm, name, comm=None):
    T, D = x.shape

    def body(x_ref, g_ref, h_ref):
        xf = x_ref[...]
        r = lax.rsqrt(jnp.mean(xf * xf, axis=-1, keepdims=True) + EPS)
        h_ref[...] = (xf * r * g_ref[...]).astype(BF16)

    return _pallas(
        comm, body, name=name, grid=(T // tm,),
        in_specs=[pl.BlockSpec((tm, D), lambda i: (i, 0)),
                  pl.BlockSpec((None, 1, D), lambda i: (layer, 0, 0))],
        out_specs=pl.BlockSpec((tm, D), lambda i: (i, 0)),
        out_shape=jax.ShapeDtypeStruct((T, D), BF16),
        compiler_params=_cparams(("parallel",), 32),
    )(x, g)


def _rmsnorm_bwd_math(xf, g, dh, dres):
    r = lax.rsqrt(jnp.mean(xf * xf, axis=-1, keepdims=True) + EPS)
    xh = xf * r
    dxh = dh * g
    dx = dres + r * (dxh - xh * jnp.mean(dxh * xh, axis=-1, keepdims=True))
    return dx, _rowsum(dh * xh)


def _mm_nt_norm(dy, w, x, g, dres, *, g_layer, tm, name, comm=None):
    T = dy.shape[0]
    _, S, K, n4 = w.shape
    nt_dims = (((1,), (1,)), ((), ()))

    def body(dy_ref, w_ref, x_ref, g_ref, dres_ref, dx_ref, dg_ref):
        @pl.when(pl.program_id(0) == 0)
        def _():
            dg_ref[...] = jnp.zeros_like(dg_ref)
        dh = None
        for s in range(S):
            part = lax.dot_general(dy_ref[:, s * n4:(s + 1) * n4].astype(BF16), w_ref[s], nt_dims,
                                   preferred_element_type=F32)
            dh = part if dh is None else dh + part
        dx, dg = _rmsnorm_bwd_math(x_ref[...], g_ref[...], dh, dres_ref[...])
        dx_ref[...] = dx
        dg_ref[...] += dg

    row = lambda i: (i, 0)
    return _pallas(
        comm, body, name=name, grid=(T // tm,),
        in_specs=[pl.BlockSpec((tm, S * n4), row),
                  pl.BlockSpec((None, S, K, n4), lambda i: (0, 0, 0, 0)),
                  pl.BlockSpec((tm, K), row),
                  pl.BlockSpec((None, 1, K), lambda i: (g_layer, 0, 0)),
                  pl.BlockSpec((tm, K), row)],
        out_specs=[pl.BlockSpec((tm, K), row), pl.BlockSpec((1, K), lambda i: (0, 0))],
        out_shape=[jax.ShapeDtypeStruct((T, K), F32), jax.ShapeDtypeStruct((1, K), F32)],
        compiler_params=_cparams(("arbitrary",), 56),
    )(dy, w, x, g, dres)


def _loss_head(x, tgt, g, *, tm, name, comm=None):
    T, D = x.shape

    def body(x_ref, t_ref, g_ref, loss_ref, dx_ref, dg_ref):
        @pl.when(pl.program_id(0) == 0)
        def _():
            dg_ref[...] = jnp.zeros_like(dg_ref)
            loss_ref[...] = jnp.zeros_like(loss_ref)
        xf = x_ref[...]
        gg = g_ref[...]
        r = lax.rsqrt(jnp.mean(xf * xf, axis=-1, keepdims=True) + EPS)
        xh = xf * r
        err = xh * gg - t_ref[...]
        row = jnp.mean(err * err, axis=-1, keepdims=True)
        loss_ref[...] += 0.5 * jnp.sum(row, axis=0, keepdims=True)
        dy = err * (1.0 / D)
        dg_ref[...] += _rowsum(dy * xh)
        dxh = dy * gg
        dx_ref[...] = r * (dxh - xh * jnp.mean(dxh * xh, axis=-1, keepdims=True))

    return _pallas(
        comm, body, name=name, grid=(T // tm,),
        in_specs=[pl.BlockSpec((tm, D), lambda i: (i, 0)),
                  pl.BlockSpec((tm, D), lambda i: (i, 0)),
                  pl.BlockSpec((1, D), lambda i: (0, 0))],
        out_specs=[pl.BlockSpec((1, 1), lambda i: (0, 0)),
                   pl.BlockSpec((tm, D), lambda i: (i, 0)),
                   pl.BlockSpec((1, D), lambda i: (0, 0))],
        out_shape=[jax.ShapeDtypeStruct((1, 1), F32), jax.ShapeDtypeStruct((T, D), F32),
                   jax.ShapeDtypeStruct((1, D), F32)],
        compiler_params=_cparams(("arbitrary",), 40),
    )(x, tgt, g)


CONV_ROWS = 64
CONV_COLS = 256


def _halo_prev_index(tm, halo):
    per = tm // halo
    return lambda i: jnp.maximum(i * per - 1, 0)


def _halo_next_index(tm, halo, total):
    per = tm // halo
    last = total // halo - 1
    return lambda i: jnp.minimum((i + 1) * per, last)


def _causal_mask():
    t = lax.broadcasted_iota(jnp.int32, (CHUNK, CHUNK), 0)
    s = lax.broadcasted_iota(jnp.int32, (CHUNK, CHUNK), 1)
    return s <= t


def _mixer_ab_fwd(z, a_ln_g, a_ln_b, w_s, b_s, conv_w, conv_b, b_ln_g, b_ln_b, *, tm, name, comm=None):
    T = z.shape[0]
    nchunk = tm // CHUNK
    halo = HALO_LONG

    def body(za_ref, zb_ref, zh_ref, alg_ref, alb_ref, ws_ref, bs_ref, cw_ref, cbias_ref,
             blg_ref, blb_ref, y_ref, cb_ref, ext_ref):
        i = pl.program_id(0)
        gu = _gelu(za_ref[:, :D_A].astype(F32))
        gv = _gelu(za_ref[:, D_A:].astype(F32))
        xh, _ = _ln_stats(gv)
        lv = (xh * alg_ref[...] + alb_ref[...]).astype(BF16)
        mask = _causal_mask()
        for h in range(A_HEADS):
            wm = jnp.where(mask, ws_ref[h], 0.0).astype(BF16)
            cols = slice(h * HEAD_DIM, (h + 1) * HEAD_DIM)
            for c in range(nchunk):
                rows = slice(c * CHUNK, (c + 1) * CHUNK)
                mixed = jnp.dot(wm, lv[rows, cols], preferred_element_type=F32) + bs_ref[h]
                y_ref[rows, cols] = (gu[rows, cols] * mixed).astype(BF16)
        ext_ref[halo:halo + tm, :] = zb_ref[:, :D_B].astype(F32) * _sigmoid(zb_ref[:, D_B:].astype(F32))
        prev = zh_ref[:, :D_B].astype(F32) * _sigmoid(zh_ref[:, D_B:].astype(F32))
        ext_ref[0:halo, :] = jnp.where(i > 0, prev, 0.0)
        for rb in range(tm // CONV_ROWS):
            for cb in range(D_B // CONV_COLS):
                cs = slice(cb * CONV_COLS, (cb + 1) * CONV_COLS)
                window = ext_ref[rb * CONV_ROWS:rb * CONV_ROWS + CONV_ROWS + halo, cs]
                acc = jnp.zeros((CONV_ROWS, CONV_COLS), F32)
                for k in range(B_CONV):
                    shifted = _rows_after(window, halo - (B_CONV - 1) + k)[:CONV_ROWS]
                    acc = acc + cw_ref[k:k + 1, cs] * shifted
                cb_ref[rb * CONV_ROWS:(rb + 1) * CONV_ROWS, cs] = acc + cbias_ref[:, cs]
        xhb, _ = _ln_stats(cb_ref[...])
        y_ref[:, D_A:] = _silu(xhb * blg_ref[...] + blb_ref[...]).astype(BF16)

    row = lambda i: (i, 0)
    par = lambda i: (0, 0)
    return _pallas(
        comm, body, name=name, grid=(T // tm,),
        in_specs=[pl.BlockSpec((tm, 2 * D_A), lambda i: (i, 0)),
                  pl.BlockSpec((tm, 2 * D_B), lambda i: (i, 1)),
                  pl.BlockSpec((halo, 2 * D_B), lambda i: (_halo_prev_index(tm, halo)(i), 1)),
                  pl.BlockSpec((1, D_A), par), pl.BlockSpec((1, D_A), par),
                  pl.BlockSpec((A_HEADS, CHUNK, CHUNK), lambda i: (0, 0, 0)),
                  pl.BlockSpec((A_HEADS, CHUNK, 1), lambda i: (0, 0, 0)),
                  pl.BlockSpec((B_CONV, D_B), par), pl.BlockSpec((1, D_B), par),
                  pl.BlockSpec((1, D_B), par), pl.BlockSpec((1, D_B), par)],
        out_specs=[pl.BlockSpec((tm, D_A + D_B), row), pl.BlockSpec((tm, D_B), row)],
        out_shape=[jax.ShapeDtypeStruct((T, D_A + D_B), BF16), jax.ShapeDtypeStruct((T, D_B), F32)],
        scratch_shapes=[pltpu.VMEM((halo + tm, D_B), F32)],
        compiler_params=_cparams(("parallel",), 40),
    )(z, z, z, a_ln_g, a_ln_b, w_s, b_s, conv_w, conv_b, b_ln_g, b_ln_b)


def _mixer_ab_bwd_pre(z, cb, dy, a_ln_g, a_ln_b, w_s, b_s, b_ln_g, b_ln_b, *, tm, name, comm=None):
    T = z.shape[0]
    nchunk = tm // CHUNK
    tn_dims = (((0,), (0,)), ((), ()))
    nt_dims = (((1,), (1,)), ((), ()))

    def body(za_ref, cb_ref, dy_ref, alg_ref, alb_ref, ws_ref, bs_ref, blg_ref, blb_ref,
             dza_ref, dcb_ref, dalg_ref, dalb_ref, dws_ref, dbs_ref, dblg_ref, dblb_ref,
             dlv_ref):
        @pl.when(pl.program_id(0) == 0)
        def _():
            for ref in (dalg_ref, dalb_ref, dws_ref, dbs_ref, dblg_ref, dblb_ref):
                ref[...] = jnp.zeros_like(ref)
        ua = za_ref[:, :D_A].astype(F32)
        va = za_ref[:, D_A:].astype(F32)
        gu = _gelu(ua)
        gv = _gelu(va)
        xh, r = _ln_stats(gv)
        alg = alg_ref[...]
        lv = (xh * alg + alb_ref[...]).astype(BF16)
        dya = dy_ref[:, :D_A].astype(F32)
        mask = _causal_mask()
        for h in range(A_HEADS):
            wm = jnp.where(mask, ws_ref[h], 0.0).astype(BF16)
            cols = slice(h * HEAD_DIM, (h + 1) * HEAD_DIM)
            dwm = jnp.zeros((CHUNK, CHUNK), F32)
            dbs = jnp.zeros((CHUNK, 1), F32)
            for c in range(nchunk):
                rows = slice(c * CHUNK, (c + 1) * CHUNK)
                lvb = lv[rows, cols]
                mixed = jnp.dot(wm, lvb, preferred_element_type=F32) + bs_ref[h]
                dyb = dya[rows, cols]
                dza_ref[rows, cols] = (dyb * mixed * _dgelu(ua[rows, cols])).astype(BF16)
                dmixed = dyb * gu[rows, cols]
                dmb = dmixed.astype(BF16)
                dlv_ref[rows, cols] = lax.dot_general(wm, dmb, tn_dims, preferred_element_type=F32)
                dwm = dwm + lax.dot_general(dmb, lvb, nt_dims, preferred_element_type=F32)
                dbs = dbs + jnp.sum(dmixed, axis=1, keepdims=True)
            dws_ref[h] += jnp.where(mask, dwm, 0.0)
            dbs_ref[h] += dbs
        dlv = dlv_ref[...]
        dalg_ref[...] += _rowsum(dlv * xh)
        dalb_ref[...] += _rowsum(dlv)
        dgv = _ln_bwd(dlv, xh, r, alg)
        dza_ref[:, D_A:] = (dgv * _dgelu(va)).astype(BF16)
        xhb, rb = _ln_stats(cb_ref[...])
        blg = blg_ref[...]
        lb = xhb * blg + blb_ref[...]
        dlb = dy_ref[:, D_A:].astype(F32) * _dsilu(lb)
        dblg_ref[...] += _rowsum(dlb * xhb)
        dblb_ref[...] += _rowsum(dlb)
        dcb_ref[...] = _ln_bwd(dlb, xhb, rb, blg)

    row = lambda i: (i, 0)
    par = lambda i: (0, 0)
    par3 = lambda i: (0, 0, 0)
    return _pallas(
        comm, body, name=name, grid=(T // tm,),
        in_specs=[pl.BlockSpec((tm, 2 * D_A), row), pl.BlockSpec((tm, D_B), row),
                  pl.BlockSpec((tm, D_A + D_B), row),
                  pl.BlockSpec((1, D_A), par), pl.BlockSpec((1, D_A), par),
                  pl.BlockSpec((A_HEADS, CHUNK, CHUNK), par3),
                  pl.BlockSpec((A_HEADS, CHUNK, 1), par3),
                  pl.BlockSpec((1, D_B), par), pl.BlockSpec((1, D_B), par)],
        out_specs=[pl.BlockSpec((tm, 2 * D_A), row), pl.BlockSpec((tm, D_B), row),
                   pl.BlockSpec((1, D_A), par), pl.BlockSpec((1, D_A), par),
                   pl.BlockSpec((A_HEADS, CHUNK, CHUNK), par3),
                   pl.BlockSpec((A_HEADS, CHUNK, 1), par3),
                   pl.BlockSpec((1, D_B), par), pl.BlockSpec((1, D_B), par)],
        out_shape=[jax.ShapeDtypeStruct((T, 2 * D_A + 2 * D_B), BF16), jax.ShapeDtypeStruct((T, D_B), F32),
                   jax.ShapeDtypeStruct((1, D_A), F32), jax.ShapeDtypeStruct((1, D_A), F32),
                   jax.ShapeDtypeStruct((A_HEADS, CHUNK, CHUNK), F32),
                   jax.ShapeDtypeStruct((A_HEADS, CHUNK, 1), F32),
                   jax.ShapeDtypeStruct((1, D_B), F32), jax.ShapeDtypeStruct((1, D_B), F32)],
        scratch_shapes=[pltpu.VMEM((tm, D_A), F32)],
        compiler_params=_cparams(("arbitrary",), 40),
    )(z, cb, dy, a_ln_g, a_ln_b, w_s, b_s, b_ln_g, b_ln_b)


def _mixer_b_conv_bwd(z, dcb, conv_w, dz, *, tm, name, comm=None):
    T = z.shape[0]
    halo = HALO_LONG

    def body(zb_ref, dcb_ref, dcn_ref, cw_ref, dz_in_ref, dzb_ref, dcw_ref, dbias_ref, dext_ref):
        i = pl.program_id(0)
        last = pl.num_programs(0) - 1

        @pl.when(i == 0)
        def _():
            dcw_ref[...] = jnp.zeros_like(dcw_ref)
            dbias_ref[...] = jnp.zeros_like(dbias_ref)
        dcb = dcb_ref[...]
        dext_ref[0:tm, :] = dcb
        dext_ref[tm:tm + halo, :] = jnp.where(i < last, dcn_ref[...], 0.0)
        dbias_ref[...] += _rowsum(dcb)
        for rb in range(tm // CONV_ROWS):
            for cb in range(D_B // CONV_COLS):
                cs = slice(cb * CONV_COLS, (cb + 1) * CONV_COLS)
                gcs = slice(D_B + cb * CONV_COLS, D_B + (cb + 1) * CONV_COLS)
                rs = slice(rb * CONV_ROWS, (rb + 1) * CONV_ROWS)
                xbb = zb_ref[rs, cs].astype(F32)
                sgb = _sigmoid(zb_ref[rs, gcs].astype(F32))
                yb0 = xbb * sgb
                window = dext_ref[rb * CONV_ROWS:rb * CONV_ROWS + CONV_ROWS + halo, cs]
                acc = jnp.zeros((CONV_ROWS, CONV_COLS), F32)
                for k in range(B_CONV):
                    shifted = _rows_after(window, (B_CONV - 1) - k)[:CONV_ROWS]
                    acc = acc + cw_ref[k:k + 1, cs] * shifted
                    dcw_ref[k:k + 1, cs] += _rowsum(shifted * yb0)
                dzb_ref[rs, cs] = (acc * sgb).astype(BF16)
                dzb_ref[rs, gcs] = (acc * xbb * sgb * (1.0 - sgb)).astype(BF16)

    row = lambda i: (i, 0)
    par = lambda i: (0, 0)
    return _pallas(
        comm, body, name=name, grid=(T // tm,),
        in_specs=[pl.BlockSpec((tm, 2 * D_B), lambda i: (i, 1)),
                  pl.BlockSpec((tm, D_B), row),
                  pl.BlockSpec((halo, D_B), lambda i: (_halo_next_index(tm, halo, T)(i), 0)),
                  pl.BlockSpec((B_CONV, D_B), par), pl.BlockSpec(memory_space=pl.ANY)],
        out_specs=[pl.BlockSpec((tm, 2 * D_B), lambda i: (i, 1)), pl.BlockSpec((B_CONV, D_B), par),
                   pl.BlockSpec((1, D_B), par)],
        out_shape=[jax.ShapeDtypeStruct(dz.shape, BF16), jax.ShapeDtypeStruct((B_CONV, D_B), F32),
                   jax.ShapeDtypeStruct((1, D_B), F32)],
        scratch_shapes=[pltpu.VMEM((tm + halo, D_B), F32)], aliases={4: 0},
        compiler_params=_cparams(("arbitrary",), 40),
    )(z, dcb, dcb, conv_w, dz)


def _rows_before(x, a):
    return x if a == 0 else pltpu.roll(x, a, axis=0)


def _rows_after(x, a):
    return x if a == 0 else pltpu.roll(x, x.shape[0] - a, axis=0)


def _conv3(w_ref, x, halo, cs):
    acc = w_ref[2:3, cs] * x[halo:]
    acc = acc + w_ref[1:2, cs] * _rows_before(x, 1)[halo:]
    return acc + w_ref[0:1, cs] * _rows_before(x, 2)[halo:]


def _mixer_c_fwd(z, conv_w, *, tm, name, comm=None):
    T = z.shape[0]
    D = D_MODEL
    halo = HALO_SHORT
    W = CONV_COLS

    def body(bg_ref, cg_ref, xv_ref, cgh_ref, xvh_ref, w_ref, r_ref):
        i = pl.program_id(0)
        for cb in range(D // W):
            cs = slice(cb * W, (cb + 1) * W)
            prev = jnp.where(i > 0, cgh_ref[:, cs].astype(F32) * xvh_ref[:, cs].astype(F32), 0.0)
            p = jnp.concatenate([prev, cg_ref[:, cs].astype(F32) * xv_ref[:, cs].astype(F32)], axis=0)
            r_ref[:, cs] = (bg_ref[:, cs].astype(F32) * _conv3(w_ref, p, halo, cs)).astype(BF16)

    hp = _halo_prev_index(tm, halo)
    return _pallas(
        comm, body, name=name, grid=(T // tm,),
        in_specs=[pl.BlockSpec((tm, D), lambda i: (i, 0)), pl.BlockSpec((tm, D), lambda i: (i, 1)),
                  pl.BlockSpec((tm, D), lambda i: (i, 2)),
                  pl.BlockSpec((halo, D), lambda i: (hp(i), 1)),
                  pl.BlockSpec((halo, D), lambda i: (hp(i), 2)),
                  pl.BlockSpec((None, C_CONV, D), lambda i: (0, 0, 0))],
        out_specs=pl.BlockSpec((tm, D), lambda i: (i, 0)),
        out_shape=jax.ShapeDtypeStruct((T, D), BF16),
        compiler_params=_cparams(("parallel",), 40),
    )(z, z, z, z, z, conv_w)


def _mixer_c_bwd(z, dr, conv_w, *, tm, name, comm=None):
    T = z.shape[0]
    D = D_MODEL
    halo = HALO_SHORT
    W = CONV_COLS

    def body(bg_ref, cg_ref, xv_ref, cgh_ref, xvh_ref, bgn_ref, dr_ref, drn_ref, w_ref, dz_ref, dw_ref):
        i = pl.program_id(0)
        last = pl.num_programs(0) - 1

        @pl.when(i == 0)
        def _():
            dw_ref[...] = jnp.zeros_like(dw_ref)
        for cb in range(D // W):
            cs = slice(cb * W, (cb + 1) * W)
            cg = cg_ref[:, cs].astype(F32)
            xv = xv_ref[:, cs].astype(F32)
            dr = dr_ref[:, cs].astype(F32)
            p = cg * xv
            prev = jnp.where(i > 0, cgh_ref[:, cs].astype(F32) * xvh_ref[:, cs].astype(F32), 0.0)
            q = _conv3(w_ref, jnp.concatenate([prev, p], axis=0), halo, cs)
            dz_ref[:, cs] = (dr * q).astype(BF16)
            nxt = jnp.where(i < last, drn_ref[:, cs].astype(F32) * bgn_ref[:, cs].astype(F32), 0.0)
            dq = jnp.concatenate([dr * bg_ref[:, cs].astype(F32), nxt], axis=0)
            dp = None
            for k in range(C_CONV):
                shifted = _rows_after(dq, 2 - k)[:tm]
                term = w_ref[k:k + 1, cs] * shifted
                dp = term if dp is None else dp + term
                dw_ref[k:k + 1, cs] += _rowsum(shifted * p)
            dz_ref[:, D + cb * W:D + (cb + 1) * W] = (dp * xv).astype(BF16)
            dz_ref[:, 2 * D + cb * W:2 * D + (cb + 1) * W] = (dp * cg).astype(BF16)

    hp = _halo_prev_index(tm, halo)
    hn = _halo_next_index(tm, halo, T)
    return _pallas(
        comm, body, name=name, grid=(T // tm,),
        in_specs=[pl.BlockSpec((tm, D), lambda i: (i, 0)), pl.BlockSpec((tm, D), lambda i: (i, 1)),
                  pl.BlockSpec((tm, D), lambda i: (i, 2)),
                  pl.BlockSpec((halo, D), lambda i: (hp(i), 1)),
                  pl.BlockSpec((halo, D), lambda i: (hp(i), 2)),
                  pl.BlockSpec((halo, D), lambda i: (hn(i), 0)),
                  pl.BlockSpec((tm, D), lambda i: (i, 0)),
                  pl.BlockSpec((halo, D), lambda i: (hn(i), 0)),
                  pl.BlockSpec((None, C_CONV, D), lambda i: (0, 0, 0))],
        out_specs=[pl.BlockSpec((tm, 3 * D), lambda i: (i, 0)),
                   pl.BlockSpec((C_CONV, D), lambda i: (0, 0))],
        out_shape=[jax.ShapeDtypeStruct((T, 3 * D), BF16), jax.ShapeDtypeStruct((C_CONV, D), F32)],
        compiler_params=_cparams(("arbitrary",), 48),
    )(z, z, z, z, z, z, dr, dr, conv_w)


FFN_COLS = 128


def _ffn_act_fwd(up, conv_w, *, layer, tm, name, comm=None):
    T = up.shape[0]
    halo = HALO_SHORT
    W = FFN_COLS

    def body(up_ref, uph_ref, w_ref, a_ref, upc_ref):
        i = pl.program_id(0)

        def conv(cs):
            prev = jnp.where(i > 0, uph_ref[:, cs], jnp.zeros((halo, W), BF16))
            return _conv3(w_ref, jnp.concatenate([prev, up_ref[:, cs]], axis=0).astype(F32), halo, cs)

        for cb in range(D_FF // W):
            gs = slice(cb * W, (cb + 1) * W)
            vs = slice(D_FF + cb * W, D_FF + (cb + 1) * W)
            g = conv(gs)
            v = conv(vs)
            upc_ref[:, gs] = g.astype(BF16)
            upc_ref[:, vs] = v.astype(BF16)
            a_ref[:, gs] = (_silu(g) * v).astype(BF16)

    return _pallas(
        comm, body, name=name, grid=(T // tm,),
        in_specs=[pl.BlockSpec((tm, 2 * D_FF), lambda i: (i, 0)),
                  pl.BlockSpec((halo, 2 * D_FF), lambda i: (_halo_prev_index(tm, halo)(i), 0)),
                  pl.BlockSpec((None, F_CONV, 2 * D_FF), lambda i: (layer, 0, 0))],
        out_specs=[pl.BlockSpec((tm, D_FF), lambda i: (i, 0)),
                   pl.BlockSpec((tm, 2 * D_FF), lambda i: (i, 0))],
        out_shape=[jax.ShapeDtypeStruct((T, D_FF), BF16), jax.ShapeDtypeStruct((T, 2 * D_FF), BF16)],
        compiler_params=_cparams(("parallel",), 48),
    )(up, up, conv_w)


def _ffn_act_bwd(up, upc, da, conv_w, *, layer, tm, name, comm=None):
    T = up.shape[0]
    halo = HALO_SHORT
    W = FFN_COLS

    def body(up_ref, upc_ref, upcn_ref, da_ref, dan_ref, w_ref, dup_ref, dw_ref):
        i = pl.program_id(0)
        last = pl.num_programs(0) - 1

        @pl.when(i == 0)
        def _():
            dw_ref[...] = jnp.zeros_like(dw_ref)
        live = jnp.where(i < last, 1.0, 0.0)
        for cb in range(D_FF // W):
            gs = slice(cb * W, (cb + 1) * W)
            vs = slice(D_FF + cb * W, D_FF + (cb + 1) * W)
            g = jnp.concatenate([upc_ref[:, gs], upcn_ref[:, gs]], axis=0).astype(F32)
            v = jnp.concatenate([upc_ref[:, vs], upcn_ref[:, vs]], axis=0).astype(F32)
            da = jnp.concatenate([da_ref[:, gs].astype(F32), dan_ref[:, gs].astype(F32) * live], axis=0)
            s = _sigmoid(g)
            silu = g * s
            grads = (da * v * (s * (1.0 + g * (1.0 - s))), da * silu)
            for cs, d in zip((gs, vs), grads):
                u = up_ref[:, cs].astype(F32)
                acc = None
                for k in range(F_CONV):
                    shifted = _rows_after(d, 2 - k)[:tm]
                    term = w_ref[k:k + 1, cs] * shifted
                    acc = term if acc is None else acc + term
                    dw_ref[k:k + 1, cs] += _rowsum(shifted * u)
                dup_ref[:, cs] = acc.astype(BF16)

    hn = _halo_next_index(tm, halo, T)
    return _pallas(
        comm, body, name=name, grid=(T // tm,),
        in_specs=[pl.BlockSpec((tm, 2 * D_FF), lambda i: (i, 0)),
                  pl.BlockSpec((tm, 2 * D_FF), lambda i: (i, 0)),
                  pl.BlockSpec((halo, 2 * D_FF), lambda i: (hn(i), 0)),
                  pl.BlockSpec((tm, D_FF), lambda i: (i, 0)),
                  pl.BlockSpec((halo, D_FF), lambda i: (hn(i), 0)),
                  pl.BlockSpec((None, F_CONV, 2 * D_FF), lambda i: (layer, 0, 0))],
        out_specs=[pl.BlockSpec((tm, 2 * D_FF), lambda i: (i, 0)),
                   pl.BlockSpec((F_CONV, 2 * D_FF), lambda i: (0, 0))],
        out_shape=[jax.ShapeDtypeStruct((T, 2 * D_FF), BF16),
                   jax.ShapeDtypeStruct((F_CONV, 2 * D_FF), F32)],
        compiler_params=_cparams(("arbitrary",), 56),
    )(up, upc, upc, da, da, conv_w)


def _local_step(x, tgt, small, plan):
    T = x.shape[0]
    tm_e = _pick(T, 256)
    tm_a = _pick(T, 512)
    tm_b = _pick(T, 128)
    tm_n = _pick(T, 512)
    tm = _pick(T, 1024)
    tm_f = _pick(T, 512)
    tt = _pick(T, 2048)
    nm = small["norm_mix"].reshape(2, 1, D_MODEL)
    nf = small["norm_ffn"].reshape(2, 1, D_MODEL)
    ngf = small["norm_final"].reshape(1, D_MODEL)
    b_s = small["a_b_s"].reshape(A_HEADS, CHUNK, 1)
    w_s = small["a_w_s"].reshape(A_HEADS, CHUNK, CHUNK)
    b_conv_w = small["b_conv_w"].reshape(B_CONV, D_B)
    sg = {}
    wt, cm = plan.weight, plan.comm

    h_m0 = _rmsnorm_fwd(x, nm, layer=0, tm=tm_n, name="norm_mix0")
    z_ab = _mm_nn(h_m0, wt("ab_w_in", 0), layer=0, tm=tm, tn=512, out_dtype=BF16, name="ab_in", comm=cm("ab_in"))
    yab, cb = _mixer_ab_fwd(z_ab, small["a_ln_g"], small["a_ln_b"], w_s, b_s, b_conv_w, small["b_conv_b"],
                            small["b_ln_g"], small["b_ln_b"], tm=tm_e, name="mixer_ab", comm=cm("mixer_ab"))
    x1, h_f0 = _mm_nn(yab, wt("ab_w_out", 0), layer=0, tm=tm, tn=D_MODEL, residual=x, norm=(nf, 0),
                      name="ab_out", comm=cm("ab_out"))

    def ffn_fwd(xin, h, layer, norm):
        up = _mm_nn(h, wt("f_w_up", layer), layer=0, tm=tm, tn=2 * 1408, out_dtype=BF16, name=f"ffn_up{layer}",
                    comm=cm(f"ffn_up{layer}"))
        a, upc = _ffn_act_fwd(up, small["f_conv_w"], layer=layer, tm=tm_a, name=f"ffn_act{layer}",
                              comm=cm(f"ffn_act{layer}"))
        out = _mm_nn(a, wt("f_w_down", layer), layer=0, tm=tm, tn=D_MODEL, residual=xin, norm=norm,
                     name=f"ffn_down{layer}", comm=cm(f"ffn_down{layer}"))
        return up, upc, a, out

    up0, upc0, a0, (x2, h_m1) = ffn_fwd(x1, h_f0, 0, (nm, 1))
    z_c = _mm_nn(h_m1, wt("c_w_in", 0), layer=0, tm=tm, tn=768, out_dtype=BF16, name="c_in", comm=cm("c_in"))
    r = _mixer_c_fwd(z_c, small["c_conv_w"], tm=tm_e, name="mixer_c", comm=cm("mixer_c"))
    x3, h_f1 = _mm_nn(r, wt("c_w_out", 0), layer=0, tm=tm, tn=D_MODEL, residual=x2, norm=(nf, 1),
                      name="c_out", comm=cm("c_out"))
    up1, upc1, a1, x4 = ffn_fwd(x3, h_f1, 1, None)
    loss, dx, sg["norm_final"] = _loss_head(x4, tgt, ngf, tm=tm_n, name="loss_head")

    def ffn_bwd(dx, xin, h, up, upc, a, layer):
        da = _mm_nt(dx, wt("f_w_down", layer), layer=0, tm=tm, tn=1408, out_dtype=BF16,
                    name=f"ffn_down_dx{layer}", comm=cm(f"ffn_down_dx{layer}"))
        plan.grad_ready("f_w_down", layer, _mm_tn(a, dx, shards=None, tk=1408, tn=1024, tt=tt,
                                                  name=f"ffn_down_dw{layer}", comm=cm(f"ffn_down_dw{layer}")))
        dup, dcw = _ffn_act_bwd(up, upc, da, small["f_conv_w"], layer=layer, tm=tm_b, name=f"ffn_act_bwd{layer}",
                                comm=cm(f"ffn_act_bwd{layer}"))
        dxin, dg = _mm_nt_norm(dup, wt("f_w_up", layer), xin, nf, dx, g_layer=layer, tm=tm_f,
                               name=f"ffn_up_dx{layer}", comm=cm(f"ffn_up_dx{layer}"))
        plan.grad_ready("f_w_up", layer, _mm_tn(h, dup, shards=N_CHIPS, tk=512, tn=2 * 1408, tt=tt,
                                                name=f"ffn_up_dw{layer}", comm=cm(f"ffn_up_dw{layer}")))
        return dxin, dg, dcw

    dx, dnf1, dfc1 = ffn_bwd(dx, x3, h_f1, up1, upc1, a1, 1)
    dr = _mm_nt(dx, wt("c_w_out", 0), layer=0, tm=tm, tn=512, out_dtype=BF16, name="c_out_dx", comm=cm("c_out_dx"))
    plan.grad_ready("c_w_out", 0, _mm_tn(r, dx, shards=None, tk=1024, tn=1024, tt=tt, name="c_out_dw",
                                         comm=cm("c_out_dw")))
    dz_c, dccw = _mixer_c_bwd(z_c, dr, small["c_conv_w"], tm=tm_e, name="mixer_c_bwd", comm=cm("mixer_c_bwd"))
    sg["c_conv_w"] = dccw.reshape(1, C_CONV, D_MODEL)
    plan.grad_ready("c_w_in", 0, _mm_tn(h_m1, dz_c, shards=N_CHIPS, tk=1024, tn=768, tt=tt, name="c_in_dw",
                                        comm=cm("c_in_dw")))
    dx, dnm1 = _mm_nt_norm(dz_c, wt("c_w_in", 0), x2, nm, dx, g_layer=1, tm=tm_f, name="c_in_dx",
                           comm=cm("c_in_dx"))
    dx, dnf0, dfc0 = ffn_bwd(dx, x1, h_f0, up0, upc0, a0, 0)
    dyab = _mm_nt(dx, wt("ab_w_out", 0), layer=0, tm=tm, tn=512, out_dtype=BF16, name="ab_out_dx",
                  comm=cm("ab_out_dx"))
    plan.grad_ready("ab_w_out", 0, _mm_tn(yab, dx, shards=None, tk=1024, tn=1024, tt=tt, name="ab_out_dw",
                                          comm=cm("ab_out_dw")))
    (dza, dcb, sg["a_ln_g"], sg["a_ln_b"], dws, dbs, sg["b_ln_g"], sg["b_ln_b"]) = _mixer_ab_bwd_pre(
        z_ab, cb, dyab, small["a_ln_g"], small["a_ln_b"], w_s, b_s, small["b_ln_g"], small["b_ln_b"],
        tm=tm_e, name="mixer_ab_bwd", comm=cm("mixer_ab_bwd"))
    dz_ab, dbcw, sg["b_conv_b"] = _mixer_b_conv_bwd(z_ab, dcb, b_conv_w, dza, tm=tm_e, name="mixer_b_conv_bwd",
                                                    comm=cm("mixer_b_conv_bwd"))
    sg["a_w_s"] = dws.reshape(1, A_HEADS, CHUNK, CHUNK)
    sg["a_b_s"] = dbs.reshape(1, A_HEADS, CHUNK)
    sg["b_conv_w"] = dbcw.reshape(1, B_CONV, D_B)
    plan.grad_ready("ab_w_in", 0, _mm_tn(h_m0, dz_ab, shards=N_CHIPS, tk=1024, tn=512, tt=tt, name="ab_in_dw",
                                         comm=cm("ab_in_dw")))
    dx, dnm0 = _mm_nt_norm(dz_ab, wt("ab_w_in", 0), x, nm, dx, g_layer=0, tm=tm_f, name="ab_in_dx",
                           comm=cm("ab_in_dx"))

    sg["norm_mix"] = [dnm0, dnm1]
    sg["norm_ffn"] = [dnf0, dnf1]
    sg["f_conv_w"] = [dfc0, dfc1]
    return loss, dx, sg


BLOCK_BYTES = 3 * 1024 * 1024


BF16_SUBLANES = 16


def _row_tile(rows, row_bytes, step=SUBLANES):
    best = None
    for tr in range(step, rows + 1, step):
        if rows % tr == 0 and tr * row_bytes <= BLOCK_BYTES:
            best = tr
    if best is None:
        raise ValueError(f"no row tile for {rows}")
    return best


def _place_scalars():
    x, y, c = lax.axis_index("x"), lax.axis_index("y"), lax.axis_index("c")
    return jnp.stack([c, 2 * x + y, 2 * (1 - x) + y, 2 * x + (1 - y), 2 * (1 - x) + (1 - y)]).astype(jnp.int32)


def _cast_into_slot(w, place, *, layer, paired, name):
    L, rows, cols = w.shape
    tr = _row_tile(rows, cols * 4, BF16_SUBLANES)

    def body(place_ref, w_ref, o_ref):
        o_ref[...] = w_ref[...].astype(BF16)

    if paired:
        out_spec = pl.BlockSpec((None, None, tr, cols), lambda i, p: (0, p[1] // 2, i, p[1] % 2))
        out_shape = jax.ShapeDtypeStruct((1, N_CHIPS // 2, rows, 2 * cols), BF16)
    else:
        out_spec = pl.BlockSpec((None, None, tr, cols), lambda i, p: (0, p[1], i, 0))
        out_shape = jax.ShapeDtypeStruct((1, N_CHIPS, rows, cols), BF16)
    return pl.pallas_call(
        body, name=name,
        grid_spec=pltpu.PrefetchScalarGridSpec(
            num_scalar_prefetch=1, grid=(rows // tr,),
            in_specs=[pl.BlockSpec((None, tr, cols), lambda i, p: (layer, i, 0))],
            out_specs=out_spec),
        out_shape=out_shape,
        compiler_params=_cparams(("parallel",), 32),
    )(place, w)


def _pair_sum(g, theirs, place, *, name):
    S, rows, cols = g.shape
    half = rows // 2
    tr = _row_tile(half, cols * 4, BF16_SUBLANES)
    nb = half // tr

    def body(place_ref, g_ref, t_ref, o_ref):
        o_ref[...] = (g_ref[...] + t_ref[...]).astype(BF16)

    spec = pl.BlockSpec((None, tr, cols), lambda s, i, p: (s, i, 0))
    return pl.pallas_call(
        body, name=name,
        grid_spec=pltpu.PrefetchScalarGridSpec(
            num_scalar_prefetch=1, grid=(S, nb),
            in_specs=[pl.BlockSpec((None, tr, cols), lambda s, i, p: (s, p[0] * nb + i, 0)), spec],
            out_specs=spec),
        out_shape=jax.ShapeDtypeStruct((S, half, cols), BF16),
        compiler_params=_cparams(("parallel", "parallel"), 32),
    )(place, g, theirs)


def _chip_sum(p, r, g_prev, place, *, layer, shape, name):
    L, rows, cols = shape
    half = rows // 2
    tr = _row_tile(half, cols * 4, BF16_SUBLANES)
    nb = half // tr

    def body(place_ref, p_ref, r_ref, *rest):
        o_ref = rest[-1]
        mine = p_ref[...].astype(F32)
        peers = [r_ref[j].astype(F32) for j in range(3)]
        acc = None
        for s in range(N_CHIPS):
            term = jnp.where(place_ref[1] == s, mine,
                             jnp.where(place_ref[2] == s, peers[0],
                                       jnp.where(place_ref[3] == s, peers[1], peers[2])))
            acc = term if acc is None else acc + term
        o_ref[...] = acc

    in_specs = [pl.BlockSpec((None, tr, cols), lambda i, pr: (pr[1], i, 0)),
                pl.BlockSpec((3, tr, cols), lambda i, pr: (0, i, 0))]
    args = [place, p, r]
    aliases = {}
    if g_prev is not None:
        in_specs.append(ANY)
        args.append(g_prev)
        aliases = {3: 0}
    return pl.pallas_call(
        body, name=name,
        grid_spec=pltpu.PrefetchScalarGridSpec(
            num_scalar_prefetch=1, grid=(nb,), in_specs=in_specs,
            out_specs=pl.BlockSpec((None, tr, cols), lambda i, pr: (layer, pr[0] * nb + i, 0))),
        out_shape=jax.ShapeDtypeStruct(shape, F32), input_output_aliases=aliases,
        compiler_params=_cparams(("parallel",), 32),
    )(*args)


def _adamw_math(w, g, m, v):
    m2 = ADAM_B1 * m + (1.0 - ADAM_B1) * g
    v2 = ADAM_B2 * v + (1.0 - ADAM_B2) * (g * g)
    m_hat = m2 / (1.0 - ADAM_B1 ** ADAM_STEP)
    v_hat = v2 / (1.0 - ADAM_B2 ** ADAM_STEP)
    delta = -ADAM_LR * (m_hat / (jnp.sqrt(v_hat) + ADAM_EPS) + ADAM_WD * w)
    return delta, m2, v2


def _adamw(w, g, m, v, *, name):
    L, rows, cols = w.shape
    tr = _row_tile(rows, cols * 4)

    def body(w_ref, g_ref, m_ref, v_ref, d_ref, m2_ref, v2_ref):
        d, m2, v2 = _adamw_math(w_ref[...], g_ref[...], m_ref[...], v_ref[...])
        d_ref[...] = d
        m2_ref[...] = m2
        v2_ref[...] = v2

    spec = pl.BlockSpec((None, tr, cols), lambda l, i: (l, i, 0))
    shape = jax.ShapeDtypeStruct(w.shape, F32)
    return pl.pallas_call(
        body, name=name, grid=(L, rows // tr), in_specs=[spec] * 4, out_specs=[spec] * 3,
        out_shape=[shape] * 3,
        compiler_params=_cparams(("parallel", "parallel"), 48),
    )(w, g, m, v)


def _allreduce_pack(pack, *, name, comm):
    R = pack.shape[0]
    half = R // 2
    nr, nw = len(comm.reads), len(comm.writes)

    def body(*refs):
        p_ref, rd, wr_in = refs[0], refs[1:1 + nr], refs[1 + nr:1 + nr + nw]
        o_ref, wr_out = refs[1 + nr + nw], refs[2 + nr + nw:2 + nr + 2 * nw]
        sib_ref, chip_ref, parts_ref, sems, comm_sems = refs[2 + nr + 2 * nw:]
        src = dict(zip(comm.reads, rd))
        src.update(zip(comm.writes, wr_in))
        dst = dict(zip(comm.writes, wr_out))
        comm.start(src, dst, comm_sems)
        x, y, c, k, sib, peers = _place()
        swap = _remote(p_ref, sib_ref, sems.at[0, 0], sems.at[0, 1], sib)
        swap.start()
        swap.wait()
        chip_ref[...] = p_ref[...] + sib_ref[...]
        mine = chip_ref.at[pl.ds(pl.multiple_of(c * half, SUBLANES), half)]
        sends = [_remote(mine, parts_ref.at[j], sems.at[1 + j, 0], sems.at[1 + j, 1], (px, py, c))
                 for j, (px, py) in enumerate(peers)]
        for rc in sends:
            rc.start()
        for rc in sends:
            rc.wait()
        own = mine[...]
        others = [parts_ref[j] for j in range(3)]
        acc = None
        for s in range(N_CHIPS):
            term = own
            for j, (px, py) in enumerate(peers):
                term = jnp.where(2 * px + py == s, others[j], term)
            acc = term if acc is None else acc + term
        done = o_ref.at[pl.ds(pl.multiple_of(c * half, SUBLANES), half)]
        done[...] = acc
        theirs = o_ref.at[pl.ds(pl.multiple_of((1 - c) * half, SUBLANES), half)]
        share = _remote(done, done, sems.at[4, 0], sems.at[4, 1], sib)
        share.start()
        _remote(done, theirs, sems.at[4, 0], sems.at[4, 1], sib).wait()
        comm.finish(src, dst, comm_sems)

    vm = pl.BlockSpec(memory_space=pltpu.VMEM)
    operands, shapes = _comm_operands(comm)
    outs = pl.pallas_call(
        body, name=name, in_specs=[vm] + [ANY] * (nr + nw), out_specs=[vm] + [ANY] * nw,
        out_shape=[jax.ShapeDtypeStruct((R, LANES), F32)] + shapes,
        input_output_aliases={1 + nr + q: 1 + q for q in range(nw)},
        scratch_shapes=[pltpu.VMEM((R, LANES), F32), pltpu.VMEM((R, LANES), F32),
                        pltpu.VMEM((3, half, LANES), F32), pltpu.SemaphoreType.DMA((5, 2)),
                        pltpu.SemaphoreType.DMA((comm.ncopies, 2))],
        compiler_params=pltpu.CompilerParams(vmem_limit_bytes=VMEM_BYTES_MAX),
    )(pack, *operands)
    for q, n in enumerate(comm.writes):
        comm.plan.bufs[n] = outs[1 + q]
    return outs[0]


PACK_UNIT = SUBLANES * LANES


def _pack(arrays):
    flat, sizes = [], []
    for a in arrays:
        pieces = a if isinstance(a, (list, tuple)) else [a]
        v = jnp.concatenate([p.reshape(-1) for p in pieces]) if len(pieces) > 1 else pieces[0].reshape(-1)
        size = v.shape[0]
        padded = -(-size // PACK_UNIT) * PACK_UNIT
        flat.append(jnp.pad(v, (0, padded - size)))
        sizes.append((size, padded))
    total = sum(p for _, p in sizes)
    if (total // PACK_UNIT) % 2:
        flat.append(jnp.zeros((PACK_UNIT,), F32))
    return jnp.concatenate(flat).reshape(-1, LANES), sizes


def _unpack(pack, sizes, shapes):
    v = pack.reshape(-1)
    out, off = [], 0
    for (size, padded), shape in zip(sizes, shapes):
        out.append(v[off:off + size].reshape(shape))
        off += padded
    return out


BIG = ("ab_w_in", "ab_w_out", "c_w_in", "c_w_out", "f_w_up", "f_w_down")
COL_SHARDED = ("ab_w_in", "c_w_in", "f_w_up")
PAIRED = ("f_w_up",)
SMALL_REPLICATED = ("norm_mix", "norm_ffn", "norm_final", "a_ln_g", "a_ln_b", "a_w_s", "a_b_s",
                    "b_conv_b", "b_ln_g", "b_ln_b")
SMALL_SHARDED = ("b_conv_w", "c_conv_w", "f_conv_w")
SMALL = SMALL_REPLICATED + SMALL_SHARDED
ALL_WEIGHTS = ("norm_mix", "norm_ffn", "norm_final", "ab_w_in", "a_ln_g", "a_ln_b", "a_w_s", "a_b_s",
               "b_conv_w", "b_conv_b", "b_ln_g", "b_ln_b", "ab_w_out", "c_w_in", "c_conv_w", "c_w_out",
               "f_w_up", "f_conv_w", "f_w_down")


SCHEDULE = {
    "ab_in": [("gi", "f_w_up", 0, 0, 4), ("gi", "ab_w_out", 0)],
    "mixer_ab": [("gd", "f_w_up", 0, 0, 4), ("gd", "ab_w_out", 0), ("gi", "f_w_up", 0, 1, 4),
                 ("gi", "f_w_up", 0, 2, 4), ("gi", "f_w_up", 0, 3, 4)],
    "ab_out": [("gd", "f_w_up", 0, 1, 4), ("gd", "f_w_up", 0, 2, 4), ("gd", "f_w_up", 0, 3, 4)],
    "ffn_up0": [("gi", "f_w_down", 0), ("gi", "c_w_in", 0, 0, 2)],
    "ffn_act0": [("gd", "f_w_down", 0), ("gd", "c_w_in", 0, 0, 2), ("gi", "c_w_in", 0, 1, 2),
                 ("gi", "f_w_up", 1, 0, 4)],
    "ffn_down0": [("gd", "c_w_in", 0, 1, 2), ("gd", "f_w_up", 1, 0, 4), ("gi", "f_w_up", 1, 1, 4),
                  ("gi", "f_w_up", 1, 2, 4)],
    "c_in": [("gd", "f_w_up", 1, 1, 4), ("gd", "f_w_up", 1, 2, 4), ("gi", "f_w_up", 1, 3, 4),
             ("gi", "c_w_out", 0)],
    "mixer_c": [("gd", "f_w_up", 1, 3, 4), ("gd", "c_w_out", 0), ("gi", "f_w_down", 1, 0, 2)],
    "c_out": [("gd", "f_w_down", 1, 0, 2), ("gi", "f_w_down", 1, 1, 2)],
    "ffn_up1": [("gd", "f_w_down", 1, 1, 2)],
    "ffn_act_bwd1": [("px", "f_w_down", 1)],
    "ffn_up_dx1": [("cx", "f_w_down", 1)],
    "c_out_dx": [("px", "f_w_up", 1)],
    "mixer_c_bwd": [("cx", "f_w_up", 1, 0, 4), ("px", "c_w_out", 0)],
    "c_in_dw": [("cx", "f_w_up", 1, 1, 4)],
    "c_in_dx": [("cx", "f_w_up", 1, 2, 4), ("px", "c_w_in", 0)],
    "ffn_down_dx0": [("cx", "f_w_up", 1, 3, 4), ("cx", "c_w_out", 0)],
    "ffn_down_dw0": [("cx", "c_w_in", 0, 0, 2)],
    "ffn_act_bwd0": [("cx", "c_w_in", 0, 1, 2), ("px", "f_w_down", 0)],
    "ffn_up_dx0": [("cx", "f_w_down", 0)],
    "ab_out_dx": [("px", "f_w_up", 0)],
    "mixer_ab_bwd": [("cx", "f_w_up", 0, 0, 4), ("px", "ab_w_out", 0)],
    "mixer_b_conv_bwd": [("cx", "f_w_up", 0, 1, 4), ("cx", "ab_w_out", 0)],
    "ab_in_dw": [("cx", "f_w_up", 0, 2, 4)],
    "ab_in_dx": [("cx", "f_w_up", 0, 3, 4), ("px", "ab_w_in", 0)],
}


class _Plan:
    def __init__(self, shapes, place):
        self.shapes, self.place, self.bufs = shapes, place, {}

    def weight(self, name, layer):
        g = self.bufs[f"w:{name}:{layer}"]
        if name in COL_SHARDED:
            return g
        _, S, rows, cols = g.shape
        return g.reshape(1, S * rows, cols)

    def grad_ready(self, name, layer, g):
        _, rows, cols = self.shapes[name]
        hbm = lambda a: pltpu.with_memory_space_constraint(a, pltpu.HBM)
        self.bufs[f"g:{name}:{layer}"] = g.reshape(N_CHIPS, rows, cols)
        self.bufs[f"t:{name}:{layer}"] = hbm(lax.empty((N_CHIPS, rows // 2, cols), F32))
        self.bufs[f"l:{name}:{layer}"] = hbm(lax.empty((3, rows // 2, cols), BF16))

    def job(self, kind, name, layer, part=0, parts=1):
        _, rows, cols = self.shapes[name]
        key = f"{name}:{layer}"
        if kind == "gi":
            return _job_gather_ici("w:" + key, rows, cols, part, parts)
        if kind == "gd":
            return _job_gather_d2d("w:" + key, rows, cols, part, parts)
        if kind == "px":
            return _job_pair_exchange("g:" + key, "t:" + key, rows)
        if kind == "cx":
            if "p:" + key not in self.bufs:
                self.bufs["p:" + key] = _pair_sum(self.bufs["g:" + key], self.bufs["t:" + key], self.place,
                                                  name=f"pair_sum_{name}{layer}")
            nr = rows // 2 // parts
            return _job_chip_exchange("p:" + key, "l:" + key, part * nr, nr)
        if kind == "ps":
            return _job_pair_share("G:" + name, layer, rows)
        raise ValueError(kind)

    def comm(self, call):
        specs = SCHEDULE.get(call)
        return None if specs is None else _Comm(self, [self.job(*spec) for spec in specs])


def _step(x, tgt, w, m, v):
    chip = 2 * lax.axis_index("x") + lax.axis_index("y")
    place = _place_scalars()
    plan = _Plan({n: w[n].shape for n in BIG}, place)
    items = [(n, l) for n in BIG for l in range(w[n].shape[0])]

    for n, l in items:
        plan.bufs[f"w:{n}:{l}"] = _cast_into_slot(w[n], place, layer=l, paired=n in PAIRED, name=f"cast_{n}{l}")
    conv_pack, conv_sizes = _pack([w[n] for n in SMALL_SHARDED])
    hbm = lambda a: pltpu.with_memory_space_constraint(a, pltpu.HBM)
    plan.bufs["conv:mine"] = hbm(conv_pack)
    plan.bufs["conv:all"] = hbm(lax.empty((N_CHIPS,) + conv_pack.shape, F32))
    _comm_only(plan, [[plan.job("gi", "ab_w_in", 0), _job_chip_gather("conv:mine", "conv:all")],
                      [plan.job("gd", "ab_w_in", 0)]], name="gather_first")
    conv_shapes = [w[n].shape for n in SMALL_SHARDED]
    per_chip = [_unpack(plan.bufs["conv:all"][s], conv_sizes, conv_shapes) for s in range(N_CHIPS)]
    small = {n: w[n] for n in SMALL_REPLICATED}
    for idx, n in enumerate(SMALL_SHARDED):
        small[n] = jnp.concatenate([jnp.where(chip == s, w[n], per_chip[s][idx]) for s in range(N_CHIPS)], axis=-1)

    loss, dx, sg = _local_step(x, tgt, small, plan)

    g_pack, g_sizes = _pack([sg[n] for n in SMALL] + [loss])
    g_sum = _allreduce_pack(g_pack, name="allreduce_small_grads",
                            comm=_Comm(plan, [plan.job("cx", "ab_w_in", 0)]))
    full_shapes = [small[n].shape for n in SMALL]
    *summed, loss = _unpack(g_sum, g_sizes, full_shapes + [(1, 1)])
    g_small = dict(zip(SMALL, summed))
    for n in SMALL_SHARDED:
        width = w[n].shape[-1]
        g_small[n] = lax.dynamic_slice_in_dim(g_small[n], chip * width, width, axis=g_small[n].ndim - 1)

    for n, l in items:
        plan.bufs["G:" + n] = _chip_sum(plan.bufs[f"p:{n}:{l}"], plan.bufs[f"l:{n}:{l}"], plan.bufs.get("G:" + n),
                                        place, layer=l, shape=w[n].shape, name=f"chip_sum_{n}{l}")
    _comm_only(plan, [[plan.job("ps", n, l) for n, l in items]], name="reduce_pair_share")
    grads_big = [plan.bufs["G:" + n] for n in BIG]

    grad, delta, new_m, new_v = {}, {}, {}, {}
    for n, g in zip(BIG, grads_big):
        grad[n] = g
        delta[n], new_m[n], new_v[n] = _adamw(w[n], g, m[n], v[n], name=f"adamw_{n}")
    shapes = [w[n].shape for n in SMALL]
    wp, sizes = _pack([w[n] for n in SMALL])
    gp, _ = _pack([g_small[n] for n in SMALL])
    mp, _ = _pack([m[n] for n in SMALL])
    vp, _ = _pack([v[n] for n in SMALL])
    R = wp.shape[0]
    dp, m2p, v2p = _adamw(wp.reshape(1, R, LANES), gp.reshape(1, R, LANES), mp.reshape(1, R, LANES),
                          vp.reshape(1, R, LANES), name="adamw_small")
    for n, d_, m_, v_ in zip(SMALL, _unpack(dp, sizes, shapes), _unpack(m2p, sizes, shapes),
                             _unpack(v2p, sizes, shapes)):
        grad[n] = g_small[n]
        delta[n], new_m[n], new_v[n] = d_, m_, v_
    return loss, dx, grad, delta, new_m, new_v


def kernel(x, norm_mix, norm_ffn, norm_final, ab_w_in, a_ln_g, a_ln_b, a_w_s, a_b_s, b_conv_w, b_conv_b, b_ln_g, b_ln_b, ab_w_out, c_w_in, c_conv_w, c_w_out, f_w_up, f_conv_w, f_w_down, loss_target, m_norm_mix, m_norm_ffn, m_norm_final, m_ab_w_in, m_a_ln_g, m_a_ln_b, m_a_w_s, m_a_b_s, m_b_conv_w, m_b_conv_b, m_b_ln_g, m_b_ln_b, m_ab_w_out, m_c_w_in, m_c_conv_w, m_c_w_out, m_f_w_up, m_f_conv_w, m_f_w_down, v_norm_mix, v_norm_ffn, v_norm_final, v_ab_w_in, v_a_ln_g, v_a_ln_b, v_a_w_s, v_a_b_s, v_b_conv_w, v_b_conv_b, v_b_ln_g, v_b_ln_b, v_ab_w_out, v_c_w_in, v_c_conv_w, v_c_w_out, v_f_w_up, v_f_conv_w, v_f_w_down):
    given = dict(locals())
    w = {n: given[n] for n in ALL_WEIGHTS}
    m = {n: given["m_" + n] for n in ALL_WEIGHTS}
    v = {n: given["v_" + n] for n in ALL_WEIGHTS}
    T = x.shape[1]
    loss, dx, grad, delta, new_m, new_v = _step(x.reshape(T, D_MODEL), loss_target.reshape(T, D_MODEL), w, m, v)
    out = [loss[0, 0], dx.reshape(x.shape)]
    for d in (grad, delta, new_m, new_v):
        out += [d[n] for n in ALL_WEIGHTS]
    return tuple(out)
```

```python
import functools
import math

import jax
import jax.numpy as jnp
from jax import lax
from jax.experimental import pallas as pl
from jax.experimental.pallas import tpu as pltpu

F32 = jnp.float32
BF16 = jnp.bfloat16

EPS = 1e-6
D_MODEL = 1024
CHUNK = 128
HEAD_DIM = 128
A_HEADS = 4
D_A = 512
D_B = 512
B_CONV = 31
C_CONV = 3
D_FF = 2816
F_CONV = 3
N_CHIPS = 4

ADAM_LR = 0.001
ADAM_B1 = 0.9
ADAM_B2 = 0.999
ADAM_EPS = 1e-08
ADAM_WD = 0.01
ADAM_STEP = 10

SUBLANES = 8
LANES = 128
HALO_SHORT = 16
HALO_LONG = 32
VMEM_BYTES_MAX = 60000 * 1024

INV_SQRT2 = 1.0 / math.sqrt(2.0)
INV_SQRT_2PI = 1.0 / math.sqrt(2.0 * math.pi)

MESH = pl.DeviceIdType.MESH


def _cparams(sem, vmem_mb):
    del vmem_mb
    return pltpu.CompilerParams(dimension_semantics=sem, vmem_limit_bytes=VMEM_BYTES_MAX)


def _pick(total, pref):
    for c in (2048, 1024, 512, 256, 128):
        if c <= pref and total % c == 0:
            return c
    raise ValueError(f"no tile for {total}")


def _sigmoid(x):
    return jax.nn.sigmoid(x)


def _silu(x):
    return x * _sigmoid(x)


def _dsilu(x):
    s = _sigmoid(x)
    return s * (1.0 + x * (1.0 - s))


def _gelu(x):
    return 0.5 * x * (1.0 + lax.erf(x * INV_SQRT2))


def _dgelu(x):
    return 0.5 * (1.0 + lax.erf(x * INV_SQRT2)) + x * jnp.exp(-0.5 * x * x) * INV_SQRT_2PI


def _ln_stats(x):
    mu = jnp.mean(x, axis=-1, keepdims=True)
    xc = x - mu
    var = jnp.mean(xc * xc, axis=-1, keepdims=True)
    r = lax.rsqrt(var + EPS)
    return xc * r, r


def _ln_bwd(dy, xh, r, g):
    dxh = dy * g
    m1 = jnp.mean(dxh, axis=-1, keepdims=True)
    m2 = jnp.mean(dxh * xh, axis=-1, keepdims=True)
    return r * (dxh - m1 - xh * m2)


def _rowsum(x):
    return jnp.sum(x, axis=0, keepdims=True)


ANY = pl.BlockSpec(memory_space=pltpu.HBM)


def _place():
    x, y, c = lax.axis_index("x"), lax.axis_index("y"), lax.axis_index("c")
    peers = [(1 - x, y), (x, 1 - y), (1 - x, 1 - y)]
    return x, y, c, 2 * x + y, (x, y, 1 - c), peers


def _half(rows, which):
    return pl.ds(which * (rows // 2), rows // 2)


def _remote(src, dst, send_sem, recv_sem, device):
    return pltpu.make_async_remote_copy(src_ref=src, dst_ref=dst, send_sem=send_sem, recv_sem=recv_sem,
                                        device_id=device, device_id_type=MESH)


class _Job:
    def __init__(self, reads, writes, ncopies, copies):
        self.reads, self.writes, self.ncopies, self.copies = reads, writes, ncopies, copies


def _share(rows, which, part, parts):
    nr = rows // 2 // parts
    return pl.ds(which * (rows // 2) + part * nr, nr)


def _slot(ref, chip, rows, cols):
    if ref.shape[1] == N_CHIPS:
        return ref.at[0, chip, rows]
    return ref.at[0, chip // 2, rows, pl.ds(pl.multiple_of((chip % 2) * cols, LANES), cols)]


def _job_gather_ici(name, rows, cols, part, parts):
    def copies(src, dst, sem):
        x, y, c, k, sib, peers = _place()
        mine_rows = _share(rows, c, part, parts)
        out = []
        for j, (px, py) in enumerate(peers):
            mine = _slot(src[name], k, mine_rows, cols)
            out.append((_remote(mine, _slot(dst[name], k, mine_rows, cols), sem(j, 0), sem(j, 1), (px, py, c)),
                        _remote(mine, _slot(dst[name], 2 * px + py, mine_rows, cols), sem(j, 0), sem(j, 1),
                                (px, py, c))))
        return out
    return _Job([], [name], 3, copies)


def _job_gather_d2d(name, rows, cols, part, parts):
    def copies(src, dst, sem):
        x, y, c, k, sib, peers = _place()
        out = []
        for j, (px, py) in enumerate(peers):
            mine_rows, their_rows = _share(rows, c, part, parts), _share(rows, 1 - c, part, parts)
            landed = _slot(src[name], 2 * px + py, mine_rows, cols)
            out.append((_remote(landed, _slot(dst[name], 2 * px + py, mine_rows, cols), sem(j, 0), sem(j, 1), sib),
                        _remote(landed, _slot(dst[name], 2 * px + py, their_rows, cols), sem(j, 0), sem(j, 1), sib)))
        return out
    return _Job([], [name], 3, copies)


def _job_chip_gather(sname, dname):
    def copies(src, dst, sem):
        x, y, c, k, sib, peers = _place()
        return [(_remote(src[sname], dst[dname].at[k], sem(j, 0), sem(j, 1), (px, py, c)),
                 _remote(src[sname], dst[dname].at[2 * px + py], sem(j, 0), sem(j, 1), (px, py, c)))
                for j, (px, py) in enumerate(peers)]
    return _Job([sname], [dname], 3, copies)


def _job_pair_exchange(gname, tname, rows, part, parts):
    nr = rows // 2 // parts

    def copies(src, dst, sem):
        x, y, c, k, sib, peers = _place()
        cp = _remote(src[gname].at[:, _share(rows, 1 - c, part, parts), :],
                     dst[tname].at[:, pl.ds(part * nr, nr), :], sem(0, 0), sem(0, 1), sib)
        return [(cp, cp)]
    return _Job([gname], [tname], 1, copies)


def _job_chip_exchange(pname, lname, r0, nr):
    def copies(src, dst, sem):
        x, y, c, k, sib, peers = _place()
        out = []
        for j, (px, py) in enumerate(peers):
            cp = _remote(src[pname].at[2 * px + py, pl.ds(r0, nr)], dst[lname].at[j, pl.ds(r0, nr)],
                         sem(j, 0), sem(j, 1), (px, py, c))
            out.append((cp, cp))
        return out
    return _Job([pname], [lname], 3, copies)


def _job_pair_share(name, layer, rows):
    def copies(src, dst, sem):
        x, y, c, k, sib, peers = _place()
        mine = src[name].at[layer, _half(rows, c)]
        return [(_remote(mine, dst[name].at[layer, _half(rows, c)], sem(0, 0), sem(0, 1), sib),
                 _remote(mine, dst[name].at[layer, _half(rows, 1 - c)], sem(0, 0), sem(0, 1), sib))]
    return _Job([], [name], 1, copies)


class _Comm:
    def __init__(self, plan, jobs):
        self.plan, self.jobs = plan, jobs
        self.writes, self.reads = [], []
        for job in jobs:
            for n in job.writes:
                if n not in self.writes:
                    self.writes.append(n)
        for job in jobs:
            for n in job.reads:
                if n not in self.writes and n not in self.reads:
                    self.reads.append(n)
        self.ncopies = sum(job.ncopies for job in jobs)

    def descriptors(self, src, dst, sems, base):
        out = []
        for job in self.jobs:
            sem = lambda j, which, base=base: sems.at[base + j, which]
            out += job.copies(src, dst, sem)
            base += job.ncopies
        return out

    def start(self, src, dst, sems, base=0):
        for first, _ in self.descriptors(src, dst, sems, base):
            first.start()

    def finish(self, src, dst, sems, base=0):
        for _, landed in self.descriptors(src, dst, sems, base):
            landed.wait()


def _comm_operands(comm):
    bufs = comm.plan.bufs
    shapes = [jax.ShapeDtypeStruct(bufs[n].shape, bufs[n].dtype) for n in comm.writes]
    return [bufs[n] for n in comm.reads] + [bufs[n] for n in comm.writes], shapes


def _pallas(comm, body, *, name, grid, in_specs, out_specs, out_shape, compiler_params, scratch_shapes=(),
            aliases=None):
    aliases = dict(aliases or {})
    if comm is None:
        return pl.pallas_call(body, name=name, grid=grid, in_specs=in_specs, out_specs=out_specs,
                              out_shape=out_shape, scratch_shapes=list(scratch_shapes),
                              input_output_aliases=aliases, compiler_params=compiler_params)
    single = not isinstance(out_shape, (list, tuple))
    base_specs = [out_specs] if single else list(out_specs)
    base_shape = [out_shape] if single else list(out_shape)
    nb, nr, nw, nbo, nsc = len(in_specs), len(comm.reads), len(comm.writes), len(base_specs), len(scratch_shapes)

    def wrapped(*refs):
        base_in, rd, wr_in = refs[:nb], refs[nb:nb + nr], refs[nb + nr:nb + nr + nw]
        o0 = nb + nr + nw
        base_out, wr_out = refs[o0:o0 + nbo], refs[o0 + nbo:o0 + nbo + nw]
        scratch, sems = refs[o0 + nbo + nw:o0 + nbo + nw + nsc], refs[-1]
        src = dict(zip(comm.reads, rd))
        src.update(zip(comm.writes, wr_in))
        dst = dict(zip(comm.writes, wr_out))
        first = functools.reduce(jnp.logical_and, [pl.program_id(a) == 0 for a in range(len(grid))])
        last = functools.reduce(jnp.logical_and,
                                [pl.program_id(a) == pl.num_programs(a) - 1 for a in range(len(grid))])

        @pl.when(first)
        def _():
            comm.start(src, dst, sems)
        body(*base_in, *base_out, *scratch)

        @pl.when(last)
        def _():
            comm.finish(src, dst, sems)

    operands, shapes = _comm_operands(comm)
    call = pl.pallas_call(
        wrapped, name=name, grid=grid, in_specs=list(in_specs) + [ANY] * (nr + nw),
        out_specs=base_specs + [ANY] * nw, out_shape=base_shape + shapes,
        input_output_aliases={**aliases, **{nb + nr + q: nbo + q for q in range(nw)}},
        scratch_shapes=list(scratch_shapes) + [pltpu.SemaphoreType.DMA((comm.ncopies, 2))],
        compiler_params=compiler_params)

    def run(*args):
        outs = call(*args, *operands)
        for q, n in enumerate(comm.writes):
            comm.plan.bufs[n] = outs[nbo + q]
        return outs[0] if single else list(outs[:nbo])

    return run


def _comm_only(plan, phases, *, name):
    comms = [_Comm(plan, jobs) for jobs in phases]
    both = _Comm(plan, [job for jobs in phases for job in jobs])
    nr, nw = len(both.reads), len(both.writes)

    def body(*refs):
        rd, wr_in, wr_out, sems = refs[:nr], refs[nr:nr + nw], refs[nr + nw:nr + 2 * nw], refs[-1]
        src = dict(zip(both.reads, rd))
        src.update(zip(both.writes, wr_in))
        dst = dict(zip(both.writes, wr_out))
        base = 0
        for comm in comms:
            comm.start(src, dst, sems, base)
            comm.finish(src, dst, sems, base)
            base += comm.ncopies

    operands, shapes = _comm_operands(both)
    outs = pl.pallas_call(
        body, name=name, in_specs=[ANY] * (nr + nw), out_specs=[ANY] * nw, out_shape=shapes,
        input_output_aliases={nr + q: q for q in range(nw)},
        scratch_shapes=[pltpu.SemaphoreType.DMA((both.ncopies, 2))],
    )(*operands)
    for q, n in enumerate(both.writes):
        plan.bufs[n] = outs[q]


def _mm_nn(a, w, *, layer, tm, tn, residual=None, norm=None, out_dtype=F32, name, comm=None):
    T, K = a.shape
    if w.ndim == 4:
        _, S, _, n4 = w.shape
        N = S * n4
        bps = n4 // tn
        w_spec = pl.BlockSpec((None, None, K, tn), lambda j, i: (layer, j // bps, 0, j % bps))
    else:
        N = w.shape[2]
        w_spec = pl.BlockSpec((None, K, tn), lambda j, i: (layer, 0, j))
    in_specs = [pl.BlockSpec((tm, K), lambda j, i: (i, 0)), w_spec]
    args = [a, w]
    if residual is not None:
        in_specs.append(pl.BlockSpec((tm, tn), lambda j, i: (i, j)))
        args.append(residual)
    out_specs = pl.BlockSpec((tm, tn), lambda j, i: (i, j))
    out_shape = jax.ShapeDtypeStruct((T, N), out_dtype)
    if norm is not None:
        assert tn == N
        g, norm_layer = norm
        in_specs.append(pl.BlockSpec((None, 1, N), lambda j, i: (norm_layer, 0, 0)))
        args.append(g)
        out_specs = [out_specs, pl.BlockSpec((tm, tn), lambda j, i: (i, j))]
        out_shape = [out_shape, jax.ShapeDtypeStruct((T, N), BF16)]

    def body(*refs):
        a_ref, w_ref = refs[0], refs[1]
        acc = jnp.dot(a_ref[...].astype(BF16), w_ref[...], preferred_element_type=F32)
        if residual is not None:
            acc = refs[2][...] + acc
        if norm is None:
            refs[-1][...] = acc.astype(out_dtype)
        else:
            refs[-2][...] = acc.astype(out_dtype)
            r = lax.rsqrt(jnp.mean(acc * acc, axis=-1, keepdims=True) + EPS)
            refs[-1][...] = (acc * r * refs[-3][...]).astype(BF16)

    return _pallas(
        comm, body, name=name, grid=(N // tn, T // tm), in_specs=in_specs,
        out_specs=out_specs, out_shape=out_shape,
        compiler_params=_cparams(("parallel", "parallel"), 48),
    )(*args)


def _mm_nt(dy, w, *, layer, tm, tn, name, out_dtype=F32, comm=None):
    T = dy.shape[0]
    nt_dims = (((1,), (1,)), ((), ()))
    _, R, N = w.shape

    def body2(dy_ref, w_ref, o_ref):
        o_ref[...] = lax.dot_general(dy_ref[...].astype(BF16), w_ref[...], nt_dims,
                                     preferred_element_type=F32).astype(out_dtype)

    return _pallas(
        comm, body2, name=name, grid=(R // tn, T // tm),
        in_specs=[pl.BlockSpec((tm, N), lambda j, i: (i, 0)),
                  pl.BlockSpec((None, tn, N), lambda j, i: (layer, j, 0))],
        out_specs=pl.BlockSpec((tm, tn), lambda j, i: (i, j)),
        out_shape=jax.ShapeDtypeStruct((T, R), out_dtype),
        compiler_params=_cparams(("parallel", "parallel"), 48),
    )(dy, w)


def _mm_tn(a, dy, *, shards, tk, tn, tt, name, comm=None):
    T, K = a.shape
    N = dy.shape[1]
    tn_dims = (((0,), (0,)), ((), ()))
    n4 = N if shards is None else N // shards
    span = max(tn // n4, 1)

    def body(a_ref, dy_ref, o_ref):
        @pl.when(pl.program_id(2) == 0)
        def _():
            o_ref[...] = jnp.zeros_like(o_ref)
        r = lax.dot_general(a_ref[...].astype(BF16), dy_ref[...].astype(BF16), tn_dims,
                            preferred_element_type=F32)
        if span == 1:
            o_ref[...] += r
        else:
            for q in range(span):
                o_ref[q] += r[:, q * n4:(q + 1) * n4]

    if shards is None:
        out_spec = pl.BlockSpec((tk, tn), lambda k, n, t: (k, n))
        out_shape = jax.ShapeDtypeStruct((K, N), F32)
    elif span > 1:
        out_spec = pl.BlockSpec((span, tk, n4), lambda k, n, t: (n, k, 0))
        out_shape = jax.ShapeDtypeStruct((shards, K, n4), F32)
    else:
        bps = n4 // tn
        out_spec = pl.BlockSpec((None, tk, tn), lambda k, n, t: (n // bps, k, n % bps))
        out_shape = jax.ShapeDtypeStruct((shards, K, n4), F32)
    return _pallas(
        comm, body, name=name, grid=(K // tk, N // tn, T // tt),
        in_specs=[pl.BlockSpec((tt, tk), lambda k, n, t: (t, k)),
                  pl.BlockSpec((tt, tn), lambda k, n, t: (t, n))],
        out_specs=out_spec, out_shape=out_shape,
        compiler_params=_cparams(("parallel", "parallel", "arbitrary"), 48),
    )(a, dy)


def _rmsnorm_fwd(x, g, *, layer, tm, name, comm=None):
    T, D = x.shape

    def body(x_ref, g_ref, h_ref):
        xf = x_ref[...]
        r = lax.rsqrt(jnp.mean(xf * xf, axis=-1, keepdims=True) + EPS)
        h_ref[...] = (xf * r * g_ref[...]).astype(BF16)

    return _pallas(
        comm, body, name=name, grid=(T // tm,),
        in_specs=[pl.BlockSpec((tm, D), lambda i: (i, 0)),
                  pl.BlockSpec((None, 1, D), lambda i: (layer, 0, 0))],
        out_specs=pl.BlockSpec((tm, D), lambda i: (i, 0)),
        out_shape=jax.ShapeDtypeStruct((T, D), BF16),
        compiler_params=_cparams(("parallel",), 32),
    )(x, g)


def _rmsnorm_bwd_math(xf, g, dh, dres):
    r = lax.rsqrt(jnp.mean(xf * xf, axis=-1, keepdims=True) + EPS)
    xh = xf * r
    dxh = dh * g
    dx = dres + r * (dxh - xh * jnp.mean(dxh * xh, axis=-1, keepdims=True))
    return dx, _rowsum(dh * xh)


def _mm_nt_norm(dy, w, x, g, dres, *, g_layer, tm, name, comm=None):
    T = dy.shape[0]
    _, S, K, n4 = w.shape
    nt_dims = (((1,), (1,)), ((), ()))

    def body(dy_ref, w_ref, x_ref, g_ref, dres_ref, dx_ref, dg_ref):
        @pl.when(pl.program_id(0) == 0)
        def _():
            dg_ref[...] = jnp.zeros_like(dg_ref)
        dh = None
        for s in range(S):
            part = lax.dot_general(dy_ref[:, s * n4:(s + 1) * n4].astype(BF16), w_ref[s], nt_dims,
                                   preferred_element_type=F32)
            dh = part if dh is None else dh + part
        dx, dg = _rmsnorm_bwd_math(x_ref[...], g_ref[...], dh, dres_ref[...])
        dx_ref[...] = dx
        dg_ref[...] += dg

    row = lambda i: (i, 0)
    return _pallas(
        comm, body, name=name, grid=(T // tm,),
        in_specs=[pl.BlockSpec((tm, S * n4), row),
                  pl.BlockSpec((None, S, K, n4), lambda i: (0, 0, 0, 0)),
                  pl.BlockSpec((tm, K), row),
                  pl.BlockSpec((None, 1, K), lambda i: (g_layer, 0, 0)),
                  pl.BlockSpec((tm, K), row)],
        out_specs=[pl.BlockSpec((tm, K), row), pl.BlockSpec((1, K), lambda i: (0, 0))],
        out_shape=[jax.ShapeDtypeStruct((T, K), F32), jax.ShapeDtypeStruct((1, K), F32)],
        compiler_params=_cparams(("arbitrary",), 56),
    )(dy, w, x, g, dres)


def _loss_head(x, tgt, g, *, tm, name, comm=None):
    T, D = x.shape

    def body(x_ref, t_ref, g_ref, loss_ref, dx_ref, dg_ref):
        @pl.when(pl.program_id(0) == 0)
        def _():
            dg_ref[...] = jnp.zeros_like(dg_ref)
            loss_ref[...] = jnp.zeros_like(loss_ref)
        xf = x_ref[...]
        gg = g_ref[...]
        r = lax.rsqrt(jnp.mean(xf * xf, axis=-1, keepdims=True) + EPS)
        xh = xf * r
        err = xh * gg - t_ref[...]
        row = jnp.mean(err * err, axis=-1, keepdims=True)
        loss_ref[...] += 0.5 * jnp.sum(row, axis=0, keepdims=True)
        dy = err * (1.0 / D)
        dg_ref[...] += _rowsum(dy * xh)
        dxh = dy * gg
        dx_ref[...] = r * (dxh - xh * jnp.mean(dxh * xh, axis=-1, keepdims=True))

    return _pallas(
        comm, body, name=name, grid=(T // tm,),
        in_specs=[pl.BlockSpec((tm, D), lambda i: (i, 0)),
                  pl.BlockSpec((tm, D), lambda i: (i, 0)),
                  pl.BlockSpec((1, D), lambda i: (0, 0))],
        out_specs=[pl.BlockSpec((1, 1), lambda i: (0, 0)),
                   pl.BlockSpec((tm, D), lambda i: (i, 0)),
                   pl.BlockSpec((1, D), lambda i: (0, 0))],
        out_shape=[jax.ShapeDtypeStruct((1, 1), F32), jax.ShapeDtypeStruct((T, D), F32),
                   jax.ShapeDtypeStruct((1, D), F32)],
        compiler_params=_cparams(("arbitrary",), 40),
    )(x, tgt, g)


CONV_ROWS = 64
CONV_COLS = 256


def _halo_prev_index(tm, halo):
    per = tm // halo
    return lambda i: jnp.maximum(i * per - 1, 0)


def _halo_next_index(tm, halo, total):
    per = tm // halo
    last = total // halo - 1
    return lambda i: jnp.minimum((i + 1) * per, last)


def _causal_mask():
    t = lax.broadcasted_iota(jnp.int32, (CHUNK, CHUNK), 0)
    s = lax.broadcasted_iota(jnp.int32, (CHUNK, CHUNK), 1)
    return s <= t


def _mixer_ab_fwd(z, a_ln_g, a_ln_b, w_s, b_s, conv_w, conv_b, b_ln_g, b_ln_b, *, tm, name, comm=None):
    T = z.shape[0]
    nchunk = tm // CHUNK
    halo = HALO_LONG

    def body(za_ref, zb_ref, zh_ref, alg_ref, alb_ref, ws_ref, bs_ref, cw_ref, cbias_ref,
             blg_ref, blb_ref, y_ref, cb_ref, ext_ref):
        i = pl.program_id(0)
        gu = _gelu(za_ref[:, :D_A].astype(F32))
        gv = _gelu(za_ref[:, D_A:].astype(F32))
        xh, _ = _ln_stats(gv)
        lv = (xh * alg_ref[...] + alb_ref[...]).astype(BF16)
        mask = _causal_mask()
        for h in range(A_HEADS):
            wm = jnp.where(mask, ws_ref[h], 0.0).astype(BF16)
            cols = slice(h * HEAD_DIM, (h + 1) * HEAD_DIM)
            for c in range(nchunk):
                rows = slice(c * CHUNK, (c + 1) * CHUNK)
                mixed = jnp.dot(wm, lv[rows, cols], preferred_element_type=F32) + bs_ref[h]
                y_ref[rows, cols] = (gu[rows, cols] * mixed).astype(BF16)
        ext_ref[halo:halo + tm, :] = zb_ref[:, :D_B].astype(F32) * _sigmoid(zb_ref[:, D_B:].astype(F32))
        prev = zh_ref[:, :D_B].astype(F32) * _sigmoid(zh_ref[:, D_B:].astype(F32))
        ext_ref[0:halo, :] = jnp.where(i > 0, prev, 0.0)
        for rb in range(tm // CONV_ROWS):
            for cb in range(D_B // CONV_COLS):
                cs = slice(cb * CONV_COLS, (cb + 1) * CONV_COLS)
                window = ext_ref[rb * CONV_ROWS:rb * CONV_ROWS + CONV_ROWS + halo, cs]
                acc = jnp.zeros((CONV_ROWS, CONV_COLS), F32)
                for k in range(B_CONV):
                    shifted = _rows_after(window, halo - (B_CONV - 1) + k)[:CONV_ROWS]
                    acc = acc + cw_ref[k:k + 1, cs] * shifted
                cb_ref[rb * CONV_ROWS:(rb + 1) * CONV_ROWS, cs] = acc + cbias_ref[:, cs]
        xhb, _ = _ln_stats(cb_ref[...])
        y_ref[:, D_A:] = _silu(xhb * blg_ref[...] + blb_ref[...]).astype(BF16)

    row = lambda i: (i, 0)
    par = lambda i: (0, 0)
    return _pallas(
        comm, body, name=name, grid=(T // tm,),
        in_specs=[pl.BlockSpec((tm, 2 * D_A), lambda i: (i, 0)),
                  pl.BlockSpec((tm, 2 * D_B), lambda i: (i, 1)),
                  pl.BlockSpec((halo, 2 * D_B), lambda i: (_halo_prev_index(tm, halo)(i), 1)),
                  pl.BlockSpec((1, D_A), par), pl.BlockSpec((1, D_A), par),
                  pl.BlockSpec((A_HEADS, CHUNK, CHUNK), lambda i: (0, 0, 0)),
                  pl.BlockSpec((A_HEADS, CHUNK, 1), lambda i: (0, 0, 0)),
                  pl.BlockSpec((B_CONV, D_B), par), pl.BlockSpec((1, D_B), par),
                  pl.BlockSpec((1, D_B), par), pl.BlockSpec((1, D_B), par)],
        out_specs=[pl.BlockSpec((tm, D_A + D_B), row), pl.BlockSpec((tm, D_B), row)],
        out_shape=[jax.ShapeDtypeStruct((T, D_A + D_B), BF16), jax.ShapeDtypeStruct((T, D_B), F32)],
        scratch_shapes=[pltpu.VMEM((halo + tm, D_B), F32)],
        compiler_params=_cparams(("parallel",), 40),
    )(z, z, z, a_ln_g, a_ln_b, w_s, b_s, conv_w, conv_b, b_ln_g, b_ln_b)


def _mixer_ab_bwd_pre(z, cb, dy, a_ln_g, a_ln_b, w_s, b_s, b_ln_g, b_ln_b, *, tm, name, comm=None):
    T = z.shape[0]
    nchunk = tm // CHUNK
    tn_dims = (((0,), (0,)), ((), ()))
    nt_dims = (((1,), (1,)), ((), ()))

    def body(za_ref, cb_ref, dy_ref, alg_ref, alb_ref, ws_ref, bs_ref, blg_ref, blb_ref,
             dza_ref, dcb_ref, dalg_ref, dalb_ref, dws_ref, dbs_ref, dblg_ref, dblb_ref,
             dlv_ref):
        @pl.when(pl.program_id(0) == 0)
        def _():
            for ref in (dalg_ref, dalb_ref, dws_ref, dbs_ref, dblg_ref, dblb_ref):
                ref[...] = jnp.zeros_like(ref)
        ua = za_ref[:, :D_A].astype(F32)
        va = za_ref[:, D_A:].astype(F32)
        gu = _gelu(ua)
        gv = _gelu(va)
        xh, r = _ln_stats(gv)
        alg = alg_ref[...]
        lv = (xh * alg + alb_ref[...]).astype(BF16)
        dya = dy_ref[:, :D_A].astype(F32)
        mask = _causal_mask()
        for h in range(A_HEADS):
            wm = jnp.where(mask, ws_ref[h], 0.0).astype(BF16)
            cols = slice(h * HEAD_DIM, (h + 1) * HEAD_DIM)
            dwm = jnp.zeros((CHUNK, CHUNK), F32)
            dbs = jnp.zeros((CHUNK, 1), F32)
            for c in range(nchunk):
                rows = slice(c * CHUNK, (c + 1) * CHUNK)
                lvb = lv[rows, cols]
                mixed = jnp.dot(wm, lvb, preferred_element_type=F32) + bs_ref[h]
                dyb = dya[rows, cols]
                dza_ref[rows, cols] = (dyb * mixed * _dgelu(ua[rows, cols])).astype(BF16)
                dmixed = dyb * gu[rows, cols]
                dmb = dmixed.astype(BF16)
                dlv_ref[rows, cols] = lax.dot_general(wm, dmb, tn_dims, preferred_element_type=F32)
                dwm = dwm + lax.dot_general(dmb, lvb, nt_dims, preferred_element_type=F32)
                dbs = dbs + jnp.sum(dmixed, axis=1, keepdims=True)
            dws_ref[h] += jnp.where(mask, dwm, 0.0)
            dbs_ref[h] += dbs
        dlv = dlv_ref[...]
        dalg_ref[...] += _rowsum(dlv * xh)
        dalb_ref[...] += _rowsum(dlv)
        dgv = _ln_bwd(dlv, xh, r, alg)
        dza_ref[:, D_A:] = (dgv * _dgelu(va)).astype(BF16)
        xhb, rb = _ln_stats(cb_ref[...])
        blg = blg_ref[...]
        lb = xhb * blg + blb_ref[...]
        dlb = dy_ref[:, D_A:].astype(F32) * _dsilu(lb)
        dblg_ref[...] += _rowsum(dlb * xhb)
        dblb_ref[...] += _rowsum(dlb)
        dcb_ref[...] = _ln_bwd(dlb, xhb, rb, blg)

    row = lambda i: (i, 0)
    par = lambda i: (0, 0)
    par3 = lambda i: (0, 0, 0)
    return _pallas(
        comm, body, name=name, grid=(T // tm,),
        in_specs=[pl.BlockSpec((tm, 2 * D_A), row), pl.BlockSpec((tm, D_B), row),
                  pl.BlockSpec((tm, D_A + D_B), row),
                  pl.BlockSpec((1, D_A), par), pl.BlockSpec((1, D_A), par),
                  pl.BlockSpec((A_HEADS, CHUNK, CHUNK), par3),
                  pl.BlockSpec((A_HEADS, CHUNK, 1), par3),
                  pl.BlockSpec((1, D_B), par), pl.BlockSpec((1, D_B), par)],
        out_specs=[pl.BlockSpec((tm, 2 * D_A), row), pl.BlockSpec((tm, D_B), row),
                   pl.BlockSpec((1, D_A), par), pl.BlockSpec((1, D_A), par),
                   pl.BlockSpec((A_HEADS, CHUNK, CHUNK), par3),
                   pl.BlockSpec((A_HEADS, CHUNK, 1), par3),
                   pl.BlockSpec((1, D_B), par), pl.BlockSpec((1, D_B), par)],
        out_shape=[jax.ShapeDtypeStruct((T, 2 * D_A + 2 * D_B), BF16), jax.ShapeDtypeStruct((T, D_B), F32),
                   jax.ShapeDtypeStruct((1, D_A), F32), jax.ShapeDtypeStruct((1, D_A), F32),
                   jax.ShapeDtypeStruct((A_HEADS, CHUNK, CHUNK), F32),
                   jax.ShapeDtypeStruct((A_HEADS, CHUNK, 1), F32),
                   jax.ShapeDtypeStruct((1, D_B), F32), jax.ShapeDtypeStruct((1, D_B), F32)],
        scratch_shapes=[pltpu.VMEM((tm, D_A), F32)],
        compiler_params=_cparams(("arbitrary",), 40),
    )(z, cb, dy, a_ln_g, a_ln_b, w_s, b_s, b_ln_g, b_ln_b)


def _mixer_b_conv_bwd(z, dcb, conv_w, dz, *, tm, name, comm=None):
    T = z.shape[0]
    halo = HALO_LONG

    def body(zb_ref, dcb_ref, dcn_ref, cw_ref, dz_in_ref, dzb_ref, dcw_ref, dbias_ref, dext_ref):
        i = pl.program_id(0)
        last = pl.num_programs(0) - 1

        @pl.when(i == 0)
        def _():
            dcw_ref[...] = jnp.zeros_like(dcw_ref)
            dbias_ref[...] = jnp.zeros_like(dbias_ref)
        dcb = dcb_ref[...]
        dext_ref[0:tm, :] = dcb
        dext_ref[tm:tm + halo, :] = jnp.where(i < last, dcn_ref[...], 0.0)
        dbias_ref[...] += _rowsum(dcb)
        for rb in range(tm // CONV_ROWS):
            for cb in range(D_B // CONV_COLS):
                cs = slice(cb * CONV_COLS, (cb + 1) * CONV_COLS)
                gcs = slice(D_B + cb * CONV_COLS, D_B + (cb + 1) * CONV_COLS)
                rs = slice(rb * CONV_ROWS, (rb + 1) * CONV_ROWS)
                xbb = zb_ref[rs, cs].astype(F32)
                sgb = _sigmoid(zb_ref[rs, gcs].astype(F32))
                yb0 = xbb * sgb
                window = dext_ref[rb * CONV_ROWS:rb * CONV_ROWS + CONV_ROWS + halo, cs]
                acc = jnp.zeros((CONV_ROWS, CONV_COLS), F32)
                for k in range(B_CONV):
                    shifted = _rows_after(window, (B_CONV - 1) - k)[:CONV_ROWS]
                    acc = acc + cw_ref[k:k + 1, cs] * shifted
                    dcw_ref[k:k + 1, cs] += _rowsum(shifted * yb0)
                dzb_ref[rs, cs] = (acc * sgb).astype(BF16)
                dzb_ref[rs, gcs] = (acc * xbb * sgb * (1.0 - sgb)).astype(BF16)

    row = lambda i: (i, 0)
    par = lambda i: (0, 0)
    return _pallas(
        comm, body, name=name, grid=(T // tm,),
        in_specs=[pl.BlockSpec((tm, 2 * D_B), lambda i: (i, 1)),
                  pl.BlockSpec((tm, D_B), row),
                  pl.BlockSpec((halo, D_B), lambda i: (_halo_next_index(tm, halo, T)(i), 0)),
                  pl.BlockSpec((B_CONV, D_B), par), pl.BlockSpec(memory_space=pl.ANY)],
        out_specs=[pl.BlockSpec((tm, 2 * D_B), lambda i: (i, 1)), pl.BlockSpec((B_CONV, D_B), par),
                   pl.BlockSpec((1, D_B), par)],
        out_shape=[jax.ShapeDtypeStruct(dz.shape, BF16), jax.ShapeDtypeStruct((B_CONV, D_B), F32),
                   jax.ShapeDtypeStruct((1, D_B), F32)],
        scratch_shapes=[pltpu.VMEM((tm + halo, D_B), F32)], aliases={4: 0},
        compiler_params=_cparams(("arbitrary",), 40),
    )(z, dcb, dcb, conv_w, dz)


def _rows_before(x, a):
    return x if a == 0 else pltpu.roll(x, a, axis=0)


def _rows_after(x, a):
    return x if a == 0 else pltpu.roll(x, x.shape[0] - a, axis=0)


def _conv3(w_ref, x, halo, cs):
    acc = w_ref[2:3, cs] * x[halo:]
    acc = acc + w_ref[1:2, cs] * _rows_before(x, 1)[halo:]
    return acc + w_ref[0:1, cs] * _rows_before(x, 2)[halo:]


def _mixer_c_fwd(z, conv_w, *, tm, name, comm=None):
    T = z.shape[0]
    D = D_MODEL
    halo = HALO_SHORT
    W = CONV_COLS

    def body(bg_ref, cg_ref, xv_ref, cgh_ref, xvh_ref, w_ref, r_ref):
        i = pl.program_id(0)
        for cb in range(D // W):
            cs = slice(cb * W, (cb + 1) * W)
            prev = jnp.where(i > 0, cgh_ref[:, cs].astype(F32) * xvh_ref[:, cs].astype(F32), 0.0)
            p = jnp.concatenate([prev, cg_ref[:, cs].astype(F32) * xv_ref[:, cs].astype(F32)], axis=0)
            r_ref[:, cs] = (bg_ref[:, cs].astype(F32) * _conv3(w_ref, p, halo, cs)).astype(BF16)

    hp = _halo_prev_index(tm, halo)
    return _pallas(
        comm, body, name=name, grid=(T // tm,),
        in_specs=[pl.BlockSpec((tm, D), lambda i: (i, 0)), pl.BlockSpec((tm, D), lambda i: (i, 1)),
                  pl.BlockSpec((tm, D), lambda i: (i, 2)),
                  pl.BlockSpec((halo, D), lambda i: (hp(i), 1)),
                  pl.BlockSpec((halo, D), lambda i: (hp(i), 2)),
                  pl.BlockSpec((None, C_CONV, D), lambda i: (0, 0, 0))],
        out_specs=pl.BlockSpec((tm, D), lambda i: (i, 0)),
        out_shape=jax.ShapeDtypeStruct((T, D), BF16),
        compiler_params=_cparams(("parallel",), 40),
    )(z, z, z, z, z, conv_w)


def _mixer_c_bwd(z, dr, conv_w, *, tm, name, comm=None):
    T = z.shape[0]
    D = D_MODEL
    halo = HALO_SHORT
    W = CONV_COLS

    def body(bg_ref, cg_ref, xv_ref, cgh_ref, xvh_ref, bgn_ref, dr_ref, drn_ref, w_ref, dz_ref, dw_ref):
        i = pl.program_id(0)
        last = pl.num_programs(0) - 1

        @pl.when(i == 0)
        def _():
            dw_ref[...] = jnp.zeros_like(dw_ref)
        for cb in range(D // W):
            cs = slice(cb * W, (cb + 1) * W)
            cg = cg_ref[:, cs].astype(F32)
            xv = xv_ref[:, cs].astype(F32)
            dr = dr_ref[:, cs].astype(F32)
            p = cg * xv
            prev = jnp.where(i > 0, cgh_ref[:, cs].astype(F32) * xvh_ref[:, cs].astype(F32), 0.0)
            q = _conv3(w_ref, jnp.concatenate([prev, p], axis=0), halo, cs)
            dz_ref[:, cs] = (dr * q).astype(BF16)
            nxt = jnp.where(i < last, drn_ref[:, cs].astype(F32) * bgn_ref[:, cs].astype(F32), 0.0)
            dq = jnp.concatenate([dr * bg_ref[:, cs].astype(F32), nxt], axis=0)
            dp = None
            for k in range(C_CONV):
                shifted = _rows_after(dq, 2 - k)[:tm]
                term = w_ref[k:k + 1, cs] * shifted
                dp = term if dp is None else dp + term
                dw_ref[k:k + 1, cs] += _rowsum(shifted * p)
            dz_ref[:, D + cb * W:D + (cb + 1) * W] = (dp * xv).astype(BF16)
            dz_ref[:, 2 * D + cb * W:2 * D + (cb + 1) * W] = (dp * cg).astype(BF16)

    hp = _halo_prev_index(tm, halo)
    hn = _halo_next_index(tm, halo, T)
    return _pallas(
        comm, body, name=name, grid=(T // tm,),
        in_specs=[pl.BlockSpec((tm, D), lambda i: (i, 0)), pl.BlockSpec((tm, D), lambda i: (i, 1)),
                  pl.BlockSpec((tm, D), lambda i: (i, 2)),
                  pl.BlockSpec((halo, D), lambda i: (hp(i), 1)),
                  pl.BlockSpec((halo, D), lambda i: (hp(i), 2)),
                  pl.BlockSpec((halo, D), lambda i: (hn(i), 0)),
                  pl.BlockSpec((tm, D), lambda i: (i, 0)),
                  pl.BlockSpec((halo, D), lambda i: (hn(i), 0)),
                  pl.BlockSpec((None, C_CONV, D), lambda i: (0, 0, 0))],
        out_specs=[pl.BlockSpec((tm, 3 * D), lambda i: (i, 0)),
                   pl.BlockSpec((C_CONV, D), lambda i: (0, 0))],
        out_shape=[jax.ShapeDtypeStruct((T, 3 * D), BF16), jax.ShapeDtypeStruct((C_CONV, D), F32)],
        compiler_params=_cparams(("arbitrary",), 48),
    )(z, z, z, z, z, z, dr, dr, conv_w)


FFN_COLS = 128


def _ffn_act_fwd(up, conv_w, *, layer, tm, name, comm=None):
    T = up.shape[0]
    halo = HALO_SHORT
    W = FFN_COLS

    def body(up_ref, uph_ref, w_ref, a_ref, upc_ref):
        i = pl.program_id(0)

        def conv(cs):
            prev = jnp.where(i > 0, uph_ref[:, cs], jnp.zeros((halo, W), BF16))
            return _conv3(w_ref, jnp.concatenate([prev, up_ref[:, cs]], axis=0).astype(F32), halo, cs)

        for cb in range(D_FF // W):
            gs = slice(cb * W, (cb + 1) * W)
            vs = slice(D_FF + cb * W, D_FF + (cb + 1) * W)
            g = conv(gs)
            v = conv(vs)
            upc_ref[:, gs] = g.astype(BF16)
            upc_ref[:, vs] = v.astype(BF16)
            a_ref[:, gs] = (_silu(g) * v).astype(BF16)

    return _pallas(
        comm, body, name=name, grid=(T // tm,),
        in_specs=[pl.BlockSpec((tm, 2 * D_FF), lambda i: (i, 0)),
                  pl.BlockSpec((halo, 2 * D_FF), lambda i: (_halo_prev_index(tm, halo)(i), 0)),
                  pl.BlockSpec((None, F_CONV, 2 * D_FF), lambda i: (layer, 0, 0))],
        out_specs=[pl.BlockSpec((tm, D_FF), lambda i: (i, 0)),
                   pl.BlockSpec((tm, 2 * D_FF), lambda i: (i, 0))],
        out_shape=[jax.ShapeDtypeStruct((T, D_FF), BF16), jax.ShapeDtypeStruct((T, 2 * D_FF), BF16)],
        compiler_params=_cparams(("parallel",), 48),
    )(up, up, conv_w)


def _ffn_act_bwd(up, upc, da, conv_w, *, layer, tm, name, comm=None):
    T = up.shape[0]
    halo = HALO_SHORT
    W = FFN_COLS

    def body(up_ref, upc_ref, upcn_ref, da_ref, dan_ref, w_ref, dup_ref, dw_ref):
        i = pl.program_id(0)
        last = pl.num_programs(0) - 1

        @pl.when(i == 0)
        def _():
            dw_ref[...] = jnp.zeros_like(dw_ref)
        live = jnp.where(i < last, 1.0, 0.0)
        for cb in range(D_FF // W):
            gs = slice(cb * W, (cb + 1) * W)
            vs = slice(D_FF + cb * W, D_FF + (cb + 1) * W)
            g = jnp.concatenate([upc_ref[:, gs], upcn_ref[:, gs]], axis=0).astype(F32)
            v = jnp.concatenate([upc_ref[:, vs], upcn_ref[:, vs]], axis=0).astype(F32)
            da = jnp.concatenate([da_ref[:, gs].astype(F32), dan_ref[:, gs].astype(F32) * live], axis=0)
            s = _sigmoid(g)
            silu = g * s
            grads = (da * v * (s * (1.0 + g * (1.0 - s))), da * silu)
            for cs, d in zip((gs, vs), grads):
                u = up_ref[:, cs].astype(F32)
                acc = None
                for k in range(F_CONV):
                    shifted = _rows_after(d, 2 - k)[:tm]
                    term = w_ref[k:k + 1, cs] * shifted
                    acc = term if acc is None else acc + term
                    dw_ref[k:k + 1, cs] += _rowsum(shifted * u)
                dup_ref[:, cs] = acc.astype(BF16)

    hn = _halo_next_index(tm, halo, T)
    return _pallas(
        comm, body, name=name, grid=(T // tm,),
        in_specs=[pl.BlockSpec((tm, 2 * D_FF), lambda i: (i, 0)),
                  pl.BlockSpec((tm, 2 * D_FF), lambda i: (i, 0)),
                  pl.BlockSpec((halo, 2 * D_FF), lambda i: (hn(i), 0)),
                  pl.BlockSpec((tm, D_FF), lambda i: (i, 0)),
                  pl.BlockSpec((halo, D_FF), lambda i: (hn(i), 0)),
                  pl.BlockSpec((None, F_CONV, 2 * D_FF), lambda i: (layer, 0, 0))],
        out_specs=[pl.BlockSpec((tm, 2 * D_FF), lambda i: (i, 0)),
                   pl.BlockSpec((F_CONV, 2 * D_FF), lambda i: (0, 0))],
        out_shape=[jax.ShapeDtypeStruct((T, 2 * D_FF), BF16),
                   jax.ShapeDtypeStruct((F_CONV, 2 * D_FF), F32)],
        compiler_params=_cparams(("arbitrary",), 56),
    )(up, upc, upc, da, da, conv_w)


def _local_step(x, tgt, small, plan):
    T = x.shape[0]
    tm_e = _pick(T, 256)
    tm_a = _pick(T, 512)
    tm_b = _pick(T, 128)
    tm_n = _pick(T, 512)
    tm = _pick(T, 1024)
    tm_f = _pick(T, 512)
    tt = _pick(T, 2048)
    nm = small["norm_mix"].reshape(2, 1, D_MODEL)
    nf = small["norm_ffn"].reshape(2, 1, D_MODEL)
    ngf = small["norm_final"].reshape(1, D_MODEL)
    b_s = small["a_b_s"].reshape(A_HEADS, CHUNK, 1)
    w_s = small["a_w_s"].reshape(A_HEADS, CHUNK, CHUNK)
    b_conv_w = small["b_conv_w"].reshape(B_CONV, D_B)
    sg = {}
    wt, cm = plan.weight, plan.comm

    h_m0 = _rmsnorm_fwd(x, nm, layer=0, tm=tm_n, name="norm_mix0")
    z_ab = _mm_nn(h_m0, wt("ab_w_in", 0), layer=0, tm=tm, tn=512, out_dtype=BF16, name="ab_in", comm=cm("ab_in"))
    yab, cb = _mixer_ab_fwd(z_ab, small["a_ln_g"], small["a_ln_b"], w_s, b_s, b_conv_w, small["b_conv_b"],
                            small["b_ln_g"], small["b_ln_b"], tm=tm_e, name="mixer_ab", comm=cm("mixer_ab"))
    x1, h_f0 = _mm_nn(yab, wt("ab_w_out", 0), layer=0, tm=tm, tn=D_MODEL, residual=x, norm=(nf, 0),
                      name="ab_out", comm=cm("ab_out"))

    def ffn_fwd(xin, h, layer, norm):
        up = _mm_nn(h, wt("f_w_up", layer), layer=0, tm=tm, tn=2 * 1408, out_dtype=BF16, name=f"ffn_up{layer}",
                    comm=cm(f"ffn_up{layer}"))
        a, upc = _ffn_act_fwd(up, small["f_conv_w"], layer=layer, tm=tm_a, name=f"ffn_act{layer}",
                              comm=cm(f"ffn_act{layer}"))
        out = _mm_nn(a, wt("f_w_down", layer), layer=0, tm=tm, tn=D_MODEL, residual=xin, norm=norm,
                     name=f"ffn_down{layer}", comm=cm(f"ffn_down{layer}"))
        return up, upc, a, out

    up0, upc0, a0, (x2, h_m1) = ffn_fwd(x1, h_f0, 0, (nm, 1))
    z_c = _mm_nn(h_m1, wt("c_w_in", 0), layer=0, tm=tm, tn=768, out_dtype=BF16, name="c_in", comm=cm("c_in"))
    r = _mixer_c_fwd(z_c, small["c_conv_w"], tm=tm_e, name="mixer_c", comm=cm("mixer_c"))
    x3, h_f1 = _mm_nn(r, wt("c_w_out", 0), layer=0, tm=tm, tn=D_MODEL, residual=x2, norm=(nf, 1),
                      name="c_out", comm=cm("c_out"))
    up1, upc1, a1, x4 = ffn_fwd(x3, h_f1, 1, None)
    loss, dx, sg["norm_final"] = _loss_head(x4, tgt, ngf, tm=tm_n, name="loss_head")

    def ffn_bwd(dx, xin, h, up, upc, a, layer):
        da = _mm_nt(dx, wt("f_w_down", layer), layer=0, tm=tm, tn=1408, out_dtype=BF16,
                    name=f"ffn_down_dx{layer}", comm=cm(f"ffn_down_dx{layer}"))
        plan.grad_ready("f_w_down", layer, _mm_tn(a, dx, shards=None, tk=1408, tn=1024, tt=tt,
                                                  name=f"ffn_down_dw{layer}", comm=cm(f"ffn_down_dw{layer}")))
        dup, dcw = _ffn_act_bwd(up, upc, da, small["f_conv_w"], layer=layer, tm=tm_b, name=f"ffn_act_bwd{layer}",
                                comm=cm(f"ffn_act_bwd{layer}"))
        dxin, dg = _mm_nt_norm(dup, wt("f_w_up", layer), xin, nf, dx, g_layer=layer, tm=tm_f,
                               name=f"ffn_up_dx{layer}", comm=cm(f"ffn_up_dx{layer}"))
        plan.grad_ready("f_w_up", layer, _mm_tn(h, dup, shards=N_CHIPS, tk=512, tn=2 * 1408, tt=tt,
                                                name=f"ffn_up_dw{layer}", comm=cm(f"ffn_up_dw{layer}")))
        return dxin, dg, dcw

    dx, dnf1, dfc1 = ffn_bwd(dx, x3, h_f1, up1, upc1, a1, 1)
    dr = _mm_nt(dx, wt("c_w_out", 0), layer=0, tm=tm, tn=512, out_dtype=BF16, name="c_out_dx", comm=cm("c_out_dx"))
    plan.grad_ready("c_w_out", 0, _mm_tn(r, dx, shards=None, tk=1024, tn=1024, tt=tt, name="c_out_dw",
                                         comm=cm("c_out_dw")))
    dz_c, dccw = _mixer_c_bwd(z_c, dr, small["c_conv_w"], tm=tm_e, name="mixer_c_bwd", comm=cm("mixer_c_bwd"))
    sg["c_conv_w"] = dccw.reshape(1, C_CONV, D_MODEL)
    plan.grad_ready("c_w_in", 0, _mm_tn(h_m1, dz_c, shards=N_CHIPS, tk=1024, tn=768, tt=tt, name="c_in_dw",
                                        comm=cm("c_in_dw")))
    dx, dnm1 = _mm_nt_norm(dz_c, wt("c_w_in", 0), x2, nm, dx, g_layer=1, tm=tm_f, name="c_in_dx",
                           comm=cm("c_in_dx"))
    dx, dnf0, dfc0 = ffn_bwd(dx, x1, h_f0, up0, upc0, a0, 0)
    dyab = _mm_nt(dx, wt("ab_w_out", 0), layer=0, tm=tm, tn=512, out_dtype=BF16, name="ab_out_dx",
                  comm=cm("ab_out_dx"))
    plan.grad_ready("ab_w_out", 0, _mm_tn(yab, dx, shards=None, tk=1024, tn=1024, tt=tt, name="ab_out_dw",
                                          comm=cm("ab_out_dw")))
    (dza, dcb, sg["a_ln_g"], sg["a_ln_b"], dws, dbs, sg["b_ln_g"], sg["b_ln_b"]) = _mixer_ab_bwd_pre(
        z_ab, cb, dyab, small["a_ln_g"], small["a_ln_b"], w_s, b_s, small["b_ln_g"], small["b_ln_b"],
        tm=tm_e, name="mixer_ab_bwd", comm=cm("mixer_ab_bwd"))
    dz_ab, dbcw, sg["b_conv_b"] = _mixer_b_conv_bwd(z_ab, dcb, b_conv_w, dza, tm=tm_e, name="mixer_b_conv_bwd",
                                                    comm=cm("mixer_b_conv_bwd"))
    sg["a_w_s"] = dws.reshape(1, A_HEADS, CHUNK, CHUNK)
    sg["a_b_s"] = dbs.reshape(1, A_HEADS, CHUNK)
    sg["b_conv_w"] = dbcw.reshape(1, B_CONV, D_B)
    plan.grad_ready("ab_w_in", 0, _mm_tn(h_m0, dz_ab, shards=N_CHIPS, tk=1024, tn=512, tt=tt, name="ab_in_dw",
                                         comm=cm("ab_in_dw")))
    dx, dnm0 = _mm_nt_norm(dz_ab, wt("ab_w_in", 0), x, nm, dx, g_layer=0, tm=tm_f, name="ab_in_dx",
                           comm=cm("ab_in_dx"))

    sg["norm_mix"] = [dnm0, dnm1]
    sg["norm_ffn"] = [dnf0, dnf1]
    sg["f_conv_w"] = [dfc0, dfc1]
    return loss, dx, sg


BLOCK_BYTES = 3 * 1024 * 1024


BF16_SUBLANES = 16


def _row_tile(rows, row_bytes, step=SUBLANES):
    best = None
    for tr in range(step, rows + 1, step):
        if rows % tr == 0 and tr * row_bytes <= BLOCK_BYTES:
            best = tr
    if best is None:
        raise ValueError(f"no row tile for {rows}")
    return best


def _place_scalars():
    x, y, c = lax.axis_index("x"), lax.axis_index("y"), lax.axis_index("c")
    return jnp.stack([c, 2 * x + y, 2 * (1 - x) + y, 2 * x + (1 - y), 2 * (1 - x) + (1 - y)]).astype(jnp.int32)


def _cast_into_slot(w, place, *, layer, paired, name):
    L, rows, cols = w.shape
    tr = _row_tile(rows, cols * 4, BF16_SUBLANES)

    def body(place_ref, w_ref, o_ref):
        o_ref[...] = w_ref[...].astype(BF16)

    if paired:
        out_spec = pl.BlockSpec((None, None, tr, cols), lambda i, p: (0, p[1] // 2, i, p[1] % 2))
        out_shape = jax.ShapeDtypeStruct((1, N_CHIPS // 2, rows, 2 * cols), BF16)
    else:
        out_spec = pl.BlockSpec((None, None, tr, cols), lambda i, p: (0, p[1], i, 0))
        out_shape = jax.ShapeDtypeStruct((1, N_CHIPS, rows, cols), BF16)
    return pl.pallas_call(
        body, name=name,
        grid_spec=pltpu.PrefetchScalarGridSpec(
            num_scalar_prefetch=1, grid=(rows // tr,),
            in_specs=[pl.BlockSpec((None, tr, cols), lambda i, p: (layer, i, 0))],
            out_specs=out_spec),
        out_shape=out_shape,
        compiler_params=_cparams(("parallel",), 32),
    )(place, w)


def _pair_sum(g, theirs, place, *, name):
    S, rows, cols = g.shape
    half = rows // 2
    tr = _row_tile(half, cols * 4, BF16_SUBLANES)
    nb = half // tr

    def body(place_ref, g_ref, t_ref, o_ref):
        o_ref[...] = (g_ref[...] + t_ref[...]).astype(BF16)

    spec = pl.BlockSpec((None, tr, cols), lambda s, i, p: (s, i, 0))
    return pl.pallas_call(
        body, name=name,
        grid_spec=pltpu.PrefetchScalarGridSpec(
            num_scalar_prefetch=1, grid=(S, nb),
            in_specs=[pl.BlockSpec((None, tr, cols), lambda s, i, p: (s, p[0] * nb + i, 0)), spec],
            out_specs=spec),
        out_shape=jax.ShapeDtypeStruct((S, half, cols), BF16),
        compiler_params=_cparams(("parallel", "parallel"), 32),
    )(place, g, theirs)


def _chip_sum(p, r, g_prev, place, *, layer, shape, name):
    L, rows, cols = shape
    half = rows // 2
    tr = _row_tile(half, cols * 4, BF16_SUBLANES)
    nb = half // tr

    def body(place_ref, p_ref, r_ref, *rest):
        o_ref = rest[-1]
        mine = p_ref[...].astype(F32)
        peers = [r_ref[j].astype(F32) for j in range(3)]
        acc = None
        for s in range(N_CHIPS):
            term = jnp.where(place_ref[1] == s, mine,
                             jnp.where(place_ref[2] == s, peers[0],
                                       jnp.where(place_ref[3] == s, peers[1], peers[2])))
            acc = term if acc is None else acc + term
        o_ref[...] = acc

    in_specs = [pl.BlockSpec((None, tr, cols), lambda i, pr: (pr[1], i, 0)),
                pl.BlockSpec((3, tr, cols), lambda i, pr: (0, i, 0))]
    args = [place, p, r]
    aliases = {}
    if g_prev is not None:
        in_specs.append(ANY)
        args.append(g_prev)
        aliases = {3: 0}
    return pl.pallas_call(
        body, name=name,
        grid_spec=pltpu.PrefetchScalarGridSpec(
            num_scalar_prefetch=1, grid=(nb,), in_specs=in_specs,
            out_specs=pl.BlockSpec((None, tr, cols), lambda i, pr: (layer, pr[0] * nb + i, 0))),
        out_shape=jax.ShapeDtypeStruct(shape, F32), input_output_aliases=aliases,
        compiler_params=_cparams(("parallel",), 32),
    )(*args)


def _adamw_math(w, g, m, v):
    m2 = ADAM_B1 * m + (1.0 - ADAM_B1) * g
    v2 = ADAM_B2 * v + (1.0 - ADAM_B2) * (g * g)
    m_hat = m2 / (1.0 - ADAM_B1 ** ADAM_STEP)
    v_hat = v2 / (1.0 - ADAM_B2 ** ADAM_STEP)
    delta = -ADAM_LR * (m_hat / (jnp.sqrt(v_hat) + ADAM_EPS) + ADAM_WD * w)
    return delta, m2, v2


def _adamw(w, g, m, v, *, name):
    L, rows, cols = w.shape
    tr = _row_tile(rows, cols * 4)

    def body(w_ref, g_ref, m_ref, v_ref, d_ref, m2_ref, v2_ref):
        d, m2, v2 = _adamw_math(w_ref[...], g_ref[...], m_ref[...], v_ref[...])
        d_ref[...] = d
        m2_ref[...] = m2
        v2_ref[...] = v2

    spec = pl.BlockSpec((None, tr, cols), lambda l, i: (l, i, 0))
    shape = jax.ShapeDtypeStruct(w.shape, F32)
    return pl.pallas_call(
        body, name=name, grid=(L, rows // tr), in_specs=[spec] * 4, out_specs=[spec] * 3,
        out_shape=[shape] * 3,
        compiler_params=_cparams(("parallel", "parallel"), 48),
    )(w, g, m, v)


def _allreduce_pack(pack, *, name, comm):
    R = pack.shape[0]
    half = R // 2
    nr, nw = len(comm.reads), len(comm.writes)

    def body(*refs):
        p_ref, rd, wr_in = refs[0], refs[1:1 + nr], refs[1 + nr:1 + nr + nw]
        o_ref, wr_out = refs[1 + nr + nw], refs[2 + nr + nw:2 + nr + 2 * nw]
        sib_ref, chip_ref, parts_ref, sems, comm_sems = refs[2 + nr + 2 * nw:]
        src = dict(zip(comm.reads, rd))
        src.update(zip(comm.writes, wr_in))
        dst = dict(zip(comm.writes, wr_out))
        comm.start(src, dst, comm_sems)
        x, y, c, k, sib, peers = _place()
        swap = _remote(p_ref, sib_ref, sems.at[0, 0], sems.at[0, 1], sib)
        swap.start()
        swap.wait()
        chip_ref[...] = p_ref[...] + sib_ref[...]
        mine = chip_ref.at[pl.ds(pl.multiple_of(c * half, SUBLANES), half)]
        sends = [_remote(mine, parts_ref.at[j], sems.at[1 + j, 0], sems.at[1 + j, 1], (px, py, c))
                 for j, (px, py) in enumerate(peers)]
        for rc in sends:
            rc.start()
        for rc in sends:
            rc.wait()
        own = mine[...]
        others = [parts_ref[j] for j in range(3)]
        acc = None
        for s in range(N_CHIPS):
            term = own
            for j, (px, py) in enumerate(peers):
                term = jnp.where(2 * px + py == s, others[j], term)
            acc = term if acc is None else acc + term
        done = o_ref.at[pl.ds(pl.multiple_of(c * half, SUBLANES), half)]
        done[...] = acc
        theirs = o_ref.at[pl.ds(pl.multiple_of((1 - c) * half, SUBLANES), half)]
        share = _remote(done, done, sems.at[4, 0], sems.at[4, 1], sib)
        share.start()
        _remote(done, theirs, sems.at[4, 0], sems.at[4, 1], sib).wait()
        comm.finish(src, dst, comm_sems)

    vm = pl.BlockSpec(memory_space=pltpu.VMEM)
    operands, shapes = _comm_operands(comm)
    outs = pl.pallas_call(
        body, name=name, in_specs=[vm] + [ANY] * (nr + nw), out_specs=[vm] + [ANY] * nw,
        out_shape=[jax.ShapeDtypeStruct((R, LANES), F32)] + shapes,
        input_output_aliases={1 + nr + q: 1 + q for q in range(nw)},
        scratch_shapes=[pltpu.VMEM((R, LANES), F32), pltpu.VMEM((R, LANES), F32),
                        pltpu.VMEM((3, half, LANES), F32), pltpu.SemaphoreType.DMA((5, 2)),
                        pltpu.SemaphoreType.DMA((comm.ncopies, 2))],
        compiler_params=pltpu.CompilerParams(vmem_limit_bytes=VMEM_BYTES_MAX),
    )(pack, *operands)
    for q, n in enumerate(comm.writes):
        comm.plan.bufs[n] = outs[1 + q]
    return outs[0]


PACK_UNIT = SUBLANES * LANES


def _pack(arrays):
    flat, sizes = [], []
    for a in arrays:
        pieces = a if isinstance(a, (list, tuple)) else [a]
        v = jnp.concatenate([p.reshape(-1) for p in pieces]) if len(pieces) > 1 else pieces[0].reshape(-1)
        size = v.shape[0]
        padded = -(-size // PACK_UNIT) * PACK_UNIT
        flat.append(jnp.pad(v, (0, padded - size)))
        sizes.append((size, padded))
    total = sum(p for _, p in sizes)
    if (total // PACK_UNIT) % 2:
        flat.append(jnp.zeros((PACK_UNIT,), F32))
    return jnp.concatenate(flat).reshape(-1, LANES), sizes


def _unpack(pack, sizes, shapes):
    v = pack.reshape(-1)
    out, off = [], 0
    for (size, padded), shape in zip(sizes, shapes):
        out.append(v[off:off + size].reshape(shape))
        off += padded
    return out


BIG = ("ab_w_in", "ab_w_out", "c_w_in", "c_w_out", "f_w_up", "f_w_down")
COL_SHARDED = ("ab_w_in", "c_w_in", "f_w_up")
PAIRED = ("f_w_up",)
SMALL_REPLICATED = ("norm_mix", "norm_ffn", "norm_final", "a_ln_g", "a_ln_b", "a_w_s", "a_b_s",
                    "b_conv_b", "b_ln_g", "b_ln_b")
SMALL_SHARDED = ("b_conv_w", "c_conv_w", "f_conv_w")
SMALL = SMALL_REPLICATED + SMALL_SHARDED
ALL_WEIGHTS = ("norm_mix", "norm_ffn", "norm_final", "ab_w_in", "a_ln_g", "a_ln_b", "a_w_s", "a_b_s",
               "b_conv_w", "b_conv_b", "b_ln_g", "b_ln_b", "ab_w_out", "c_w_in", "c_conv_w", "c_w_out",
               "f_w_up", "f_conv_w", "f_w_down")


SCHEDULE = {
    "ab_in": [("gi", "f_w_up", 0, 0, 4), ("gi", "ab_w_out", 0)],
    "mixer_ab": [("gd", "f_w_up", 0, 0, 4), ("gd", "ab_w_out", 0), ("gi", "f_w_up", 0, 1, 4),
                 ("gi", "f_w_up", 0, 2, 4), ("gi", "f_w_up", 0, 3, 4)],
    "ab_out": [("gd", "f_w_up", 0, 1, 4), ("gd", "f_w_up", 0, 2, 4), ("gd", "f_w_up", 0, 3, 4)],
    "ffn_up0": [("gi", "f_w_down", 0), ("gi", "c_w_in", 0, 0, 2)],
    "ffn_act0": [("gd", "f_w_down", 0), ("gd", "c_w_in", 0, 0, 2), ("gi", "c_w_in", 0, 1, 2),
                 ("gi", "f_w_up", 1, 0, 4)],
    "ffn_down0": [("gd", "c_w_in", 0, 1, 2), ("gd", "f_w_up", 1, 0, 4), ("gi", "f_w_up", 1, 1, 4),
                  ("gi", "f_w_up", 1, 2, 4)],
    "c_in": [("gd", "f_w_up", 1, 1, 4), ("gd", "f_w_up", 1, 2, 4), ("gi", "f_w_up", 1, 3, 4),
             ("gi", "c_w_out", 0)],
    "mixer_c": [("gd", "f_w_up", 1, 3, 4), ("gd", "c_w_out", 0), ("gi", "f_w_down", 1, 0, 2)],
    "c_out": [("gd", "f_w_down", 1, 0, 2), ("gi", "f_w_down", 1, 1, 2)],
    "ffn_up1": [("gd", "f_w_down", 1, 1, 2)],
    "ffn_act_bwd1": [("px", "f_w_down", 1)],
    "ffn_up_dx1": [("cx", "f_w_down", 1)],
    "c_out_dx": [("px", "f_w_up", 1, 0, 2)],
    "c_out_dw": [("px", "f_w_up", 1, 1, 2)],
    "mixer_c_bwd": [("cx", "f_w_up", 1, 0, 4), ("px", "c_w_out", 0)],
    "c_in_dw": [("cx", "f_w_up", 1, 1, 4)],
    "c_in_dx": [("cx", "f_w_up", 1, 2, 4), ("px", "c_w_in", 0)],
    "ffn_down_dx0": [("cx", "f_w_up", 1, 3, 4), ("cx", "c_w_out", 0)],
    "ffn_down_dw0": [("cx", "c_w_in", 0, 0, 2)],
    "ffn_act_bwd0": [("cx", "c_w_in", 0, 1, 2), ("px", "f_w_down", 0)],
    "ffn_up_dx0": [("cx", "f_w_down", 0)],
    "ffn_up_dw0": [("ps", "f_w_down", 1), ("ps", "f_w_up", 1)],
    "ab_out_dx": [("px", "f_w_up", 0, 0, 2)],
    "ab_out_dw": [("px", "f_w_up", 0, 1, 2)],
    "mixer_ab_bwd": [("cx", "f_w_up", 0, 0, 4), ("px", "ab_w_out", 0)],
    "mixer_b_conv_bwd": [("cx", "f_w_up", 0, 1, 4), ("cx", "ab_w_out", 0), ("ps", "c_w_out", 0),
                         ("ps", "c_w_in", 0), ("ps", "f_w_down", 0)],
    "ab_in_dw": [("cx", "f_w_up", 0, 2, 4)],
    "ab_in_dx": [("cx", "f_w_up", 0, 3, 4), ("px", "ab_w_in", 0)],
}


class _Plan:
    def __init__(self, shapes, place):
        self.shapes, self.place, self.bufs = shapes, place, {}
        self.summed, self.shared = set(), set()

    def weight(self, name, layer):
        g = self.bufs[f"w:{name}:{layer}"]
        if name in COL_SHARDED:
            return g
        _, S, rows, cols = g.shape
        return g.reshape(1, S * rows, cols)

    def grad_ready(self, name, layer, g):
        _, rows, cols = self.shapes[name]
        hbm = lambda a: pltpu.with_memory_space_constraint(a, pltpu.HBM)
        self.bufs[f"g:{name}:{layer}"] = g.reshape(N_CHIPS, rows, cols)
        self.bufs[f"t:{name}:{layer}"] = hbm(lax.empty((N_CHIPS, rows // 2, cols), F32))
        self.bufs[f"l:{name}:{layer}"] = hbm(lax.empty((3, rows // 2, cols), BF16))

    def job(self, kind, name, layer, part=0, parts=1):
        _, rows, cols = self.shapes[name]
        key = f"{name}:{layer}"
        if kind == "gi":
            return _job_gather_ici("w:" + key, rows, cols, part, parts)
        if kind == "gd":
            return _job_gather_d2d("w:" + key, rows, cols, part, parts)
        if kind == "px":
            return _job_pair_exchange("g:" + key, "t:" + key, rows, part, parts)
        if kind == "cx":
            if "p:" + key not in self.bufs:
                self.bufs["p:" + key] = _pair_sum(self.bufs["g:" + key], self.bufs["t:" + key], self.place,
                                                  name=f"pair_sum_{name}{layer}")
            nr = rows // 2 // parts
            return _job_chip_exchange("p:" + key, "l:" + key, part * nr, nr)
        if kind == "ps":
            self.chip_sum(name, layer)
            self.shared.add(key)
            return _job_pair_share("G:" + name, layer, rows)
        raise ValueError(kind)

    def chip_sum(self, name, layer):
        key = f"{name}:{layer}"
        if key not in self.summed:
            self.summed.add(key)
            self.bufs["G:" + name] = _chip_sum(self.bufs["p:" + key], self.bufs["l:" + key],
                                               self.bufs.get("G:" + name), self.place, layer=layer,
                                               shape=self.shapes[name], name=f"chip_sum_{name}{layer}")

    def comm(self, call):
        specs = SCHEDULE.get(call)
        return None if specs is None else _Comm(self, [self.job(*spec) for spec in specs])


def _step(x, tgt, w, m, v):
    chip = 2 * lax.axis_index("x") + lax.axis_index("y")
    place = _place_scalars()
    plan = _Plan({n: w[n].shape for n in BIG}, place)
    items = [(n, l) for n in BIG for l in range(w[n].shape[0])]

    for n, l in items:
        plan.bufs[f"w:{n}:{l}"] = _cast_into_slot(w[n], place, layer=l, paired=n in PAIRED, name=f"cast_{n}{l}")
    conv_pack, conv_sizes = _pack([w[n] for n in SMALL_SHARDED])
    hbm = lambda a: pltpu.with_memory_space_constraint(a, pltpu.HBM)
    plan.bufs["conv:mine"] = hbm(conv_pack)
    plan.bufs["conv:all"] = hbm(lax.empty((N_CHIPS,) + conv_pack.shape, F32))
    _comm_only(plan, [[plan.job("gi", "ab_w_in", 0), _job_chip_gather("conv:mine", "conv:all")],
                      [plan.job("gd", "ab_w_in", 0)]], name="gather_first")
    conv_shapes = [w[n].shape for n in SMALL_SHARDED]
    per_chip = [_unpack(plan.bufs["conv:all"][s], conv_sizes, conv_shapes) for s in range(N_CHIPS)]
    small = {n: w[n] for n in SMALL_REPLICATED}
    for idx, n in enumerate(SMALL_SHARDED):
        small[n] = jnp.concatenate([jnp.where(chip == s, w[n], per_chip[s][idx]) for s in range(N_CHIPS)], axis=-1)

    loss, dx, sg = _local_step(x, tgt, small, plan)

    g_pack, g_sizes = _pack([sg[n] for n in SMALL] + [loss])
    g_sum = _allreduce_pack(g_pack, name="allreduce_small_grads",
                            comm=_Comm(plan, [plan.job("cx", "ab_w_in", 0)]))
    full_shapes = [small[n].shape for n in SMALL]
    *summed, loss = _unpack(g_sum, g_sizes, full_shapes + [(1, 1)])
    g_small = dict(zip(SMALL, summed))
    for n in SMALL_SHARDED:
        width = w[n].shape[-1]
        g_small[n] = lax.dynamic_slice_in_dim(g_small[n], chip * width, width, axis=g_small[n].ndim - 1)

    _comm_only(plan, [[plan.job("ps", n, l) for n, l in items if f"{n}:{l}" not in plan.shared]],
               name="reduce_pair_share")
    grads_big = [plan.bufs["G:" + n] for n in BIG]

    grad, delta, new_m, new_v = {}, {}, {}, {}
    for n, g in zip(BIG, grads_big):
        grad[n] = g
        delta[n], new_m[n], new_v[n] = _adamw(w[n], g, m[n], v[n], name=f"adamw_{n}")
    shapes = [w[n].shape for n in SMALL]
    wp, sizes = _pack([w[n] for n in SMALL])
    gp, _ = _pack([g_small[n] for n in SMALL])
    mp, _ = _pack([m[n] for n in SMALL])
    vp, _ = _pack([v[n] for n in SMALL])
    R = wp.shape[0]
    dp, m2p, v2p = _adamw(wp.reshape(1, R, LANES), gp.reshape(1, R, LANES), mp.reshape(1, R, LANES),
                          vp.reshape(1, R, LANES), name="adamw_small")
    for n, d_, m_, v_ in zip(SMALL, _unpack(dp, sizes, shapes), _unpack(m2p, sizes, shapes),
                             _unpack(v2p, sizes, shapes)):
        grad[n] = g_small[n]
        delta[n], new_m[n], new_v[n] = d_, m_, v_
    return loss, dx, grad, delta, new_m, new_v


def kernel(x, norm_mix, norm_ffn, norm_final, ab_w_in, a_ln_g, a_ln_b, a_w_s, a_b_s, b_conv_w, b_conv_b, b_ln_g, b_ln_b, ab_w_out, c_w_in, c_conv_w, c_w_out, f_w_up, f_conv_w, f_w_down, loss_target, m_norm_mix, m_norm_ffn, m_norm_final, m_ab_w_in, m_a_ln_g, m_a_ln_b, m_a_w_s, m_a_b_s, m_b_conv_w, m_b_conv_b, m_b_ln_g, m_b_ln_b, m_ab_w_out, m_c_w_in, m_c_conv_w, m_c_w_out, m_f_w_up, m_f_conv_w, m_f_w_down, v_norm_mix, v_norm_ffn, v_norm_final, v_ab_w_in, v_a_ln_g, v_a_ln_b, v_a_w_s, v_a_b_s, v_b_conv_w, v_b_conv_b, v_b_ln_g, v_b_ln_b, v_ab_w_out, v_c_w_in, v_c_conv_w, v_c_w_out, v_f_w_up, v_f_conv_w, v_f_w_down):
    given = dict(locals())
    w = {n: given[n] for n in ALL_WEIGHTS}
    m = {n: given["m_" + n] for n in ALL_WEIGHTS}
    v = {n: given["v_" + n] for n in ALL_WEIGHTS}
    T = x.shape[1]
    loss, dx, grad, delta, new_m, new_v = _step(x.reshape(T, D_MODEL), loss_target.reshape(T, D_MODEL), w, m, v)
    out = [loss[0, 0], dx.reshape(x.shape)]
    for d in (grad, delta, new_m, new_v):
        out += [d[n] for n in ALL_WEIGHTS]
    return tuple(out)
```

```python
import functools
import math

import jax
import jax.numpy as jnp
from jax import lax
from jax.experimental import pallas as pl
from jax.experimental.pallas import tpu as pltpu

F32 = jnp.float32
BF16 = jnp.bfloat16

EPS = 1e-6
D_MODEL = 1024
CHUNK = 128
HEAD_DIM = 128
A_HEADS = 4
D_A = 512
D_B = 512
B_CONV = 31
C_CONV = 3
D_FF = 2816
F_CONV = 3
N_CHIPS = 4

ADAM_LR = 0.001
ADAM_B1 = 0.9
ADAM_B2 = 0.999
ADAM_EPS = 1e-08
ADAM_WD = 0.01
ADAM_STEP = 10

SUBLANES = 8
LANES = 128
HALO_SHORT = 16
HALO_LONG = 32
VMEM_BYTES_MAX = 60000 * 1024

INV_SQRT2 = 1.0 / math.sqrt(2.0)
INV_SQRT_2PI = 1.0 / math.sqrt(2.0 * math.pi)

MESH = pl.DeviceIdType.MESH


def _cparams(*sem):
    return pltpu.CompilerParams(dimension_semantics=sem, vmem_limit_bytes=VMEM_BYTES_MAX)


def _pick(total, pref):
    for c in (2048, 1024, 512, 256, 128):
        if c <= pref and total % c == 0:
            return c
    raise ValueError(f"no tile for {total}")


def _sigmoid(x):
    return jax.nn.sigmoid(x)


def _silu(x):
    return x * _sigmoid(x)


def _dsilu(x):
    s = _sigmoid(x)
    return s * (1.0 + x * (1.0 - s))


def _gelu(x):
    return 0.5 * x * (1.0 + lax.erf(x * INV_SQRT2))


def _dgelu(x):
    return 0.5 * (1.0 + lax.erf(x * INV_SQRT2)) + x * jnp.exp(-0.5 * x * x) * INV_SQRT_2PI


def _ln_stats(x):
    mu = jnp.mean(x, axis=-1, keepdims=True)
    xc = x - mu
    var = jnp.mean(xc * xc, axis=-1, keepdims=True)
    r = lax.rsqrt(var + EPS)
    return xc * r, r


def _ln_bwd(dy, xh, r, g):
    dxh = dy * g
    m1 = jnp.mean(dxh, axis=-1, keepdims=True)
    m2 = jnp.mean(dxh * xh, axis=-1, keepdims=True)
    return r * (dxh - m1 - xh * m2)


def _rowsum(x):
    return jnp.sum(x, axis=0, keepdims=True)


HBM_REF = pl.BlockSpec(memory_space=pltpu.HBM)


def _place():
    x, y, c = lax.axis_index("x"), lax.axis_index("y"), lax.axis_index("c")
    peers = [(1 - x, y), (x, 1 - y), (1 - x, 1 - y)]
    return x, y, c, 2 * x + y, (x, y, 1 - c), peers


def _half(rows, which):
    return pl.ds(which * (rows // 2), rows // 2)


def _remote(src, dst, send_sem, recv_sem, device):
    return pltpu.make_async_remote_copy(src_ref=src, dst_ref=dst, send_sem=send_sem, recv_sem=recv_sem,
                                        device_id=device, device_id_type=MESH)


class _Job:
    def __init__(self, reads, writes, ncopies, copies):
        self.reads, self.writes, self.ncopies, self.copies = reads, writes, ncopies, copies


def _share(rows, which, part, parts):
    nr = rows // 2 // parts
    return pl.ds(which * (rows // 2) + part * nr, nr)


def _slot(ref, chip, rows, cols):
    if ref.shape[1] == N_CHIPS:
        return ref.at[0, chip, rows]
    return ref.at[0, chip // 2, rows, pl.ds(pl.multiple_of((chip % 2) * cols, LANES), cols)]


def _job_gather_ici(name, rows, cols, part, parts):
    def copies(src, dst, sem):
        x, y, c, k, sib, peers = _place()
        mine_rows = _share(rows, c, part, parts)
        out = []
        for j, (px, py) in enumerate(peers):
            mine = _slot(src[name], k, mine_rows, cols)
            out.append((_remote(mine, _slot(dst[name], k, mine_rows, cols), sem(j, 0), sem(j, 1), (px, py, c)),
                        _remote(mine, _slot(dst[name], 2 * px + py, mine_rows, cols), sem(j, 0), sem(j, 1),
                                (px, py, c))))
        return out
    return _Job([], [name], 3, copies)


def _job_gather_d2d(name, rows, cols, part, parts):
    def copies(src, dst, sem):
        x, y, c, k, sib, peers = _place()
        out = []
        for j, (px, py) in enumerate(peers):
            mine_rows, their_rows = _share(rows, c, part, parts), _share(rows, 1 - c, part, parts)
            landed = _slot(src[name], 2 * px + py, mine_rows, cols)
            out.append((_remote(landed, _slot(dst[name], 2 * px + py, mine_rows, cols), sem(j, 0), sem(j, 1), sib),
                        _remote(landed, _slot(dst[name], 2 * px + py, their_rows, cols), sem(j, 0), sem(j, 1), sib)))
        return out
    return _Job([], [name], 3, copies)


def _job_chip_gather(sname, dname):
    def copies(src, dst, sem):
        x, y, c, k, sib, peers = _place()
        return [(_remote(src[sname], dst[dname].at[k], sem(j, 0), sem(j, 1), (px, py, c)),
                 _remote(src[sname], dst[dname].at[2 * px + py], sem(j, 0), sem(j, 1), (px, py, c)))
                for j, (px, py) in enumerate(peers)]
    return _Job([sname], [dname], 3, copies)


def _job_pair_exchange(gname, tname, rows, part, parts):
    nr = rows // 2 // parts

    def copies(src, dst, sem):
        x, y, c, k, sib, peers = _place()
        cp = _remote(src[gname].at[:, _share(rows, 1 - c, part, parts), :],
                     dst[tname].at[:, pl.ds(part * nr, nr), :], sem(0, 0), sem(0, 1), sib)
        return [(cp, cp)]
    return _Job([gname], [tname], 1, copies)


def _job_chip_exchange(pname, lname, r0, nr):
    def copies(src, dst, sem):
        x, y, c, k, sib, peers = _place()
        out = []
        for j, (px, py) in enumerate(peers):
            cp = _remote(src[pname].at[2 * px + py, pl.ds(r0, nr)], dst[lname].at[j, pl.ds(r0, nr)],
                         sem(j, 0), sem(j, 1), (px, py, c))
            out.append((cp, cp))
        return out
    return _Job([pname], [lname], 3, copies)


def _job_pair_share(name, layer, rows):
    def copies(src, dst, sem):
        x, y, c, k, sib, peers = _place()
        mine = src[name].at[layer, _half(rows, c)]
        return [(_remote(mine, dst[name].at[layer, _half(rows, c)], sem(0, 0), sem(0, 1), sib),
                 _remote(mine, dst[name].at[layer, _half(rows, 1 - c)], sem(0, 0), sem(0, 1), sib))]
    return _Job([], [name], 1, copies)


class _Comm:
    def __init__(self, plan, jobs):
        self.plan, self.jobs = plan, jobs
        self.writes, self.reads = [], []
        for job in jobs:
            for n in job.writes:
                if n not in self.writes:
                    self.writes.append(n)
        for job in jobs:
            for n in job.reads:
                if n not in self.writes and n not in self.reads:
                    self.reads.append(n)
        self.ncopies = sum(job.ncopies for job in jobs)

    def descriptors(self, src, dst, sems, base):
        out = []
        for job in self.jobs:
            sem = lambda j, which, base=base: sems.at[base + j, which]
            out += job.copies(src, dst, sem)
            base += job.ncopies
        return out

    def start(self, src, dst, sems, base=0):
        for first, _ in self.descriptors(src, dst, sems, base):
            first.start()

    def finish(self, src, dst, sems, base=0):
        for _, landed in self.descriptors(src, dst, sems, base):
            landed.wait()


def _comm_operands(comm):
    bufs = comm.plan.bufs
    shapes = [jax.ShapeDtypeStruct(bufs[n].shape, bufs[n].dtype) for n in comm.writes]
    return [bufs[n] for n in comm.reads] + [bufs[n] for n in comm.writes], shapes


def _pallas(comm, body, *, name, grid, in_specs, out_specs, out_shape, compiler_params, scratch_shapes=(),
            aliases=None):
    aliases = dict(aliases or {})
    if comm is None:
        return pl.pallas_call(body, name=name, grid=grid, in_specs=in_specs, out_specs=out_specs,
                              out_shape=out_shape, scratch_shapes=list(scratch_shapes),
                              input_output_aliases=aliases, compiler_params=compiler_params)
    single = not isinstance(out_shape, (list, tuple))
    base_specs = [out_specs] if single else list(out_specs)
    base_shape = [out_shape] if single else list(out_shape)
    nb, nr, nw, nbo, nsc = len(in_specs), len(comm.reads), len(comm.writes), len(base_specs), len(scratch_shapes)

    def wrapped(*refs):
        base_in, rd, wr_in = refs[:nb], refs[nb:nb + nr], refs[nb + nr:nb + nr + nw]
        o0 = nb + nr + nw
        base_out, wr_out = refs[o0:o0 + nbo], refs[o0 + nbo:o0 + nbo + nw]
        scratch, sems = refs[o0 + nbo + nw:o0 + nbo + nw + nsc], refs[-1]
        src = dict(zip(comm.reads, rd))
        src.update(zip(comm.writes, wr_in))
        dst = dict(zip(comm.writes, wr_out))
        first = functools.reduce(jnp.logical_and, [pl.program_id(a) == 0 for a in range(len(grid))])
        last = functools.reduce(jnp.logical_and,
                                [pl.program_id(a) == pl.num_programs(a) - 1 for a in range(len(grid))])

        @pl.when(first)
        def _():
            comm.start(src, dst, sems)
        body(*base_in, *base_out, *scratch)

        @pl.when(last)
        def _():
            comm.finish(src, dst, sems)

    operands, shapes = _comm_operands(comm)
    call = pl.pallas_call(
        wrapped, name=name, grid=grid, in_specs=list(in_specs) + [HBM_REF] * (nr + nw),
        out_specs=base_specs + [HBM_REF] * nw, out_shape=base_shape + shapes,
        input_output_aliases={**aliases, **{nb + nr + q: nbo + q for q in range(nw)}},
        scratch_shapes=list(scratch_shapes) + [pltpu.SemaphoreType.DMA((comm.ncopies, 2))],
        compiler_params=compiler_params)

    def run(*args):
        outs = call(*args, *operands)
        for q, n in enumerate(comm.writes):
            comm.plan.bufs[n] = outs[nbo + q]
        return outs[0] if single else list(outs[:nbo])

    return run


def _comm_only(plan, phases, *, name):
    comms = [_Comm(plan, jobs) for jobs in phases]
    both = _Comm(plan, [job for jobs in phases for job in jobs])
    nr, nw = len(both.reads), len(both.writes)

    def body(*refs):
        rd, wr_in, wr_out, sems = refs[:nr], refs[nr:nr + nw], refs[nr + nw:nr + 2 * nw], refs[-1]
        src = dict(zip(both.reads, rd))
        src.update(zip(both.writes, wr_in))
        dst = dict(zip(both.writes, wr_out))
        base = 0
        for comm in comms:
            comm.start(src, dst, sems, base)
            comm.finish(src, dst, sems, base)
            base += comm.ncopies

    operands, shapes = _comm_operands(both)
    outs = pl.pallas_call(
        body, name=name, in_specs=[HBM_REF] * (nr + nw), out_specs=[HBM_REF] * nw, out_shape=shapes,
        input_output_aliases={nr + q: q for q in range(nw)},
        scratch_shapes=[pltpu.SemaphoreType.DMA((both.ncopies, 2))],
    )(*operands)
    for q, n in enumerate(both.writes):
        plan.bufs[n] = outs[q]


def _mm_nn(a, w, *, layer, tm, tn, residual=None, norm=None, out_dtype=F32, name, comm=None):
    T, K = a.shape
    if w.ndim == 4:
        _, S, _, n4 = w.shape
        N = S * n4
        bps = n4 // tn
        w_spec = pl.BlockSpec((None, None, K, tn), lambda j, i: (layer, j // bps, 0, j % bps))
    else:
        N = w.shape[2]
        w_spec = pl.BlockSpec((None, K, tn), lambda j, i: (layer, 0, j))
    in_specs = [pl.BlockSpec((tm, K), lambda j, i: (i, 0)), w_spec]
    args = [a, w]
    if residual is not None:
        in_specs.append(pl.BlockSpec((tm, tn), lambda j, i: (i, j)))
        args.append(residual)
    out_specs = pl.BlockSpec((tm, tn), lambda j, i: (i, j))
    out_shape = jax.ShapeDtypeStruct((T, N), out_dtype)
    if norm is not None:
        assert tn == N
        g, norm_layer = norm
        in_specs.append(pl.BlockSpec((None, 1, N), lambda j, i: (norm_layer, 0, 0)))
        args.append(g)
        out_specs = [out_specs, pl.BlockSpec((tm, tn), lambda j, i: (i, j))]
        out_shape = [out_shape, jax.ShapeDtypeStruct((T, N), BF16)]

    def body(*refs):
        a_ref, w_ref = refs[0], refs[1]
        acc = jnp.dot(a_ref[...].astype(BF16), w_ref[...], preferred_element_type=F32)
        if residual is not None:
            acc = refs[2][...] + acc
        if norm is None:
            refs[-1][...] = acc.astype(out_dtype)
        else:
            refs[-2][...] = acc.astype(out_dtype)
            r = lax.rsqrt(jnp.mean(acc * acc, axis=-1, keepdims=True) + EPS)
            refs[-1][...] = (acc * r * refs[-3][...]).astype(BF16)

    return _pallas(
        comm, body, name=name, grid=(N // tn, T // tm), in_specs=in_specs,
        out_specs=out_specs, out_shape=out_shape,
        compiler_params=_cparams("parallel", "parallel"),
    )(*args)


def _norm_mm_nn(x, g, w, *, g_layer, tm, tn, name, comm=None):
    T, K = x.shape
    _, S, _, n4 = w.shape
    bps = n4 // tn

    def body(x_ref, g_ref, w_ref, h_ref, o_ref):
        @pl.when(pl.program_id(1) == 0)
        def _():
            xf = x_ref[...]
            r = lax.rsqrt(jnp.mean(xf * xf, axis=-1, keepdims=True) + EPS)
            h_ref[...] = (xf * r * g_ref[...]).astype(BF16)
        o_ref[...] = jnp.dot(h_ref[...], w_ref[...], preferred_element_type=F32).astype(BF16)

    return _pallas(
        comm, body, name=name, grid=(T // tm, S * bps),
        in_specs=[pl.BlockSpec((tm, K), lambda i, j: (i, 0)),
                  pl.BlockSpec((None, 1, K), lambda i, j: (g_layer, 0, 0)),
                  pl.BlockSpec((None, None, K, tn), lambda i, j: (0, j // bps, 0, j % bps))],
        out_specs=[pl.BlockSpec((tm, K), lambda i, j: (i, 0)), pl.BlockSpec((tm, tn), lambda i, j: (i, j))],
        out_shape=[jax.ShapeDtypeStruct((T, K), BF16), jax.ShapeDtypeStruct((T, S * n4), BF16)],
        compiler_params=_cparams("parallel", "arbitrary"),
    )(x, g, w)


def _mm_nt(dy, w, *, layer, tm, tn, name, out_dtype=F32, comm=None):
    T = dy.shape[0]
    nt_dims = (((1,), (1,)), ((), ()))
    _, R, N = w.shape

    def body2(dy_ref, w_ref, o_ref):
        o_ref[...] = lax.dot_general(dy_ref[...].astype(BF16), w_ref[...], nt_dims,
                                     preferred_element_type=F32).astype(out_dtype)

    return _pallas(
        comm, body2, name=name, grid=(R // tn, T // tm),
        in_specs=[pl.BlockSpec((tm, N), lambda j, i: (i, 0)),
                  pl.BlockSpec((None, tn, N), lambda j, i: (layer, j, 0))],
        out_specs=pl.BlockSpec((tm, tn), lambda j, i: (i, j)),
        out_shape=jax.ShapeDtypeStruct((T, R), out_dtype),
        compiler_params=_cparams("parallel", "parallel"),
    )(dy, w)


def _mm_tn(a, dy, *, shards, tk, tn, tt, name, comm=None):
    T, K = a.shape
    N = dy.shape[1]
    tn_dims = (((0,), (0,)), ((), ()))
    n4 = N if shards is None else N // shards
    span = max(tn // n4, 1)

    def body(a_ref, dy_ref, o_ref):
        @pl.when(pl.program_id(2) == 0)
        def _():
            o_ref[...] = jnp.zeros_like(o_ref)
        r = lax.dot_general(a_ref[...].astype(BF16), dy_ref[...].astype(BF16), tn_dims,
                            preferred_element_type=F32)
        if span == 1:
            o_ref[...] += r
        else:
            for q in range(span):
                o_ref[q] += r[:, q * n4:(q + 1) * n4]

    if shards is None:
        out_spec = pl.BlockSpec((tk, tn), lambda k, n, t: (k, n))
        out_shape = jax.ShapeDtypeStruct((K, N), F32)
    elif span > 1:
        out_spec = pl.BlockSpec((span, tk, n4), lambda k, n, t: (n, k, 0))
        out_shape = jax.ShapeDtypeStruct((shards, K, n4), F32)
    else:
        bps = n4 // tn
        out_spec = pl.BlockSpec((None, tk, tn), lambda k, n, t: (n // bps, k, n % bps))
        out_shape = jax.ShapeDtypeStruct((shards, K, n4), F32)
    return _pallas(
        comm, body, name=name, grid=(K // tk, N // tn, T // tt),
        in_specs=[pl.BlockSpec((tt, tk), lambda k, n, t: (t, k)),
                  pl.BlockSpec((tt, tn), lambda k, n, t: (t, n))],
        out_specs=out_spec, out_shape=out_shape,
        compiler_params=_cparams("parallel", "parallel", "arbitrary"),
    )(a, dy)


def _rmsnorm_bwd_math(xf, g, dh, dres):
    r = lax.rsqrt(jnp.mean(xf * xf, axis=-1, keepdims=True) + EPS)
    xh = xf * r
    dxh = dh * g
    dx = dres + r * (dxh - xh * jnp.mean(dxh * xh, axis=-1, keepdims=True))
    return dx, _rowsum(dh * xh)


def _mm_nt_norm(dy, w, x, g, dres, *, g_layer, tm, name, comm=None):
    T = dy.shape[0]
    _, S, K, n4 = w.shape
    nt_dims = (((1,), (1,)), ((), ()))

    def body(dy_ref, w_ref, x_ref, g_ref, dres_ref, dx_ref, dg_ref):
        @pl.when(pl.program_id(0) == 0)
        def _():
            dg_ref[...] = jnp.zeros_like(dg_ref)
        dh = None
        for s in range(S):
            part = lax.dot_general(dy_ref[:, s * n4:(s + 1) * n4].astype(BF16), w_ref[s], nt_dims,
                                   preferred_element_type=F32)
            dh = part if dh is None else dh + part
        dx, dg = _rmsnorm_bwd_math(x_ref[...], g_ref[...], dh, dres_ref[...])
        dx_ref[...] = dx
        dg_ref[...] += dg

    row = lambda i: (i, 0)
    return _pallas(
        comm, body, name=name, grid=(T // tm,),
        in_specs=[pl.BlockSpec((tm, S * n4), row),
                  pl.BlockSpec((None, S, K, n4), lambda i: (0, 0, 0, 0)),
                  pl.BlockSpec((tm, K), row),
                  pl.BlockSpec((None, 1, K), lambda i: (g_layer, 0, 0)),
                  pl.BlockSpec((tm, K), row)],
        out_specs=[pl.BlockSpec((tm, K), row), pl.BlockSpec((1, K), lambda i: (0, 0))],
        out_shape=[jax.ShapeDtypeStruct((T, K), F32), jax.ShapeDtypeStruct((1, K), F32)],
        compiler_params=_cparams("arbitrary"),
    )(dy, w, x, g, dres)


def _loss_head(x, tgt, g, *, tm, name, comm=None):
    T, D = x.shape

    def body(x_ref, t_ref, g_ref, loss_ref, dx_ref, dg_ref):
        @pl.when(pl.program_id(0) == 0)
        def _():
            dg_ref[...] = jnp.zeros_like(dg_ref)
            loss_ref[...] = jnp.zeros_like(loss_ref)
        xf = x_ref[...]
        gg = g_ref[...]
        r = lax.rsqrt(jnp.mean(xf * xf, axis=-1, keepdims=True) + EPS)
        xh = xf * r
        err = xh * gg - t_ref[...]
        row = jnp.mean(err * err, axis=-1, keepdims=True)
        loss_ref[...] += 0.5 * jnp.sum(row, axis=0, keepdims=True)
        dy = err * (1.0 / D)
        dg_ref[...] += _rowsum(dy * xh)
        dxh = dy * gg
        dx_ref[...] = r * (dxh - xh * jnp.mean(dxh * xh, axis=-1, keepdims=True))

    return _pallas(
        comm, body, name=name, grid=(T // tm,),
        in_specs=[pl.BlockSpec((tm, D), lambda i: (i, 0)),
                  pl.BlockSpec((tm, D), lambda i: (i, 0)),
                  pl.BlockSpec((1, D), lambda i: (0, 0))],
        out_specs=[pl.BlockSpec((1, 1), lambda i: (0, 0)),
                   pl.BlockSpec((tm, D), lambda i: (i, 0)),
                   pl.BlockSpec((1, D), lambda i: (0, 0))],
        out_shape=[jax.ShapeDtypeStruct((1, 1), F32), jax.ShapeDtypeStruct((T, D), F32),
                   jax.ShapeDtypeStruct((1, D), F32)],
        compiler_params=_cparams("arbitrary"),
    )(x, tgt, g)


CONV_ROWS = 64
CONV_COLS = 256


def _halo_prev_index(tm, halo):
    per = tm // halo
    return lambda i: jnp.maximum(i * per - 1, 0)


def _halo_next_index(tm, halo, total):
    per = tm // halo
    last = total // halo - 1
    return lambda i: jnp.minimum((i + 1) * per, last)


def _causal_mask():
    t = lax.broadcasted_iota(jnp.int32, (CHUNK, CHUNK), 0)
    s = lax.broadcasted_iota(jnp.int32, (CHUNK, CHUNK), 1)
    return s <= t


def _mixer_ab_fwd(z, a_ln_g, a_ln_b, w_s, b_s, conv_w, conv_b, b_ln_g, b_ln_b, *, tm, name, comm=None):
    T = z.shape[0]
    nchunk = tm // CHUNK
    halo = HALO_LONG

    def body(za_ref, zb_ref, zh_ref, alg_ref, alb_ref, ws_ref, bs_ref, cw_ref, cbias_ref,
             blg_ref, blb_ref, y_ref, cb_ref, ext_ref):
        i = pl.program_id(0)
        gu = _gelu(za_ref[:, :D_A].astype(F32))
        gv = _gelu(za_ref[:, D_A:].astype(F32))
        xh, _ = _ln_stats(gv)
        lv = (xh * alg_ref[...] + alb_ref[...]).astype(BF16)
        mask = _causal_mask()
        for h in range(A_HEADS):
            wm = jnp.where(mask, ws_ref[h], 0.0).astype(BF16)
            cols = slice(h * HEAD_DIM, (h + 1) * HEAD_DIM)
            for c in range(nchunk):
                rows = slice(c * CHUNK, (c + 1) * CHUNK)
                mixed = jnp.dot(wm, lv[rows, cols], preferred_element_type=F32) + bs_ref[h]
                y_ref[rows, cols] = (gu[rows, cols] * mixed).astype(BF16)
        ext_ref[halo:halo + tm, :] = zb_ref[:, :D_B].astype(F32) * _sigmoid(zb_ref[:, D_B:].astype(F32))
        prev = zh_ref[:, :D_B].astype(F32) * _sigmoid(zh_ref[:, D_B:].astype(F32))
        ext_ref[0:halo, :] = jnp.where(i > 0, prev, 0.0)
        for rb in range(tm // CONV_ROWS):
            for cb in range(D_B // CONV_COLS):
                cs = slice(cb * CONV_COLS, (cb + 1) * CONV_COLS)
                window = ext_ref[rb * CONV_ROWS:rb * CONV_ROWS + CONV_ROWS + halo, cs]
                acc = jnp.zeros((CONV_ROWS, CONV_COLS), F32)
                for k in range(B_CONV):
                    shifted = _rows_after(window, halo - (B_CONV - 1) + k)[:CONV_ROWS]
                    acc = acc + cw_ref[k:k + 1, cs] * shifted
                cb_ref[rb * CONV_ROWS:(rb + 1) * CONV_ROWS, cs] = acc + cbias_ref[:, cs]
        xhb, _ = _ln_stats(cb_ref[...])
        y_ref[:, D_A:] = _silu(xhb * blg_ref[...] + blb_ref[...]).astype(BF16)

    row = lambda i: (i, 0)
    par = lambda i: (0, 0)
    return _pallas(
        comm, body, name=name, grid=(T // tm,),
        in_specs=[pl.BlockSpec((tm, 2 * D_A), lambda i: (i, 0)),
                  pl.BlockSpec((tm, 2 * D_B), lambda i: (i, 1)),
                  pl.BlockSpec((halo, 2 * D_B), lambda i: (_halo_prev_index(tm, halo)(i), 1)),
                  pl.BlockSpec((1, D_A), par), pl.BlockSpec((1, D_A), par),
                  pl.BlockSpec((A_HEADS, CHUNK, CHUNK), lambda i: (0, 0, 0)),
                  pl.BlockSpec((A_HEADS, CHUNK, 1), lambda i: (0, 0, 0)),
                  pl.BlockSpec((B_CONV, D_B), par), pl.BlockSpec((1, D_B), par),
                  pl.BlockSpec((1, D_B), par), pl.BlockSpec((1, D_B), par)],
        out_specs=[pl.BlockSpec((tm, D_A + D_B), row), pl.BlockSpec((tm, D_B), row)],
        out_shape=[jax.ShapeDtypeStruct((T, D_A + D_B), BF16), jax.ShapeDtypeStruct((T, D_B), F32)],
        scratch_shapes=[pltpu.VMEM((halo + tm, D_B), F32)],
        compiler_params=_cparams("parallel"),
    )(z, z, z, a_ln_g, a_ln_b, w_s, b_s, conv_w, conv_b, b_ln_g, b_ln_b)


def _mixer_ab_bwd_pre(z, cb, dy, a_ln_g, a_ln_b, w_s, b_s, b_ln_g, b_ln_b, *, tm, name, comm=None):
    T = z.shape[0]
    nchunk = tm // CHUNK
    tn_dims = (((0,), (0,)), ((), ()))
    nt_dims = (((1,), (1,)), ((), ()))

    def body(za_ref, cb_ref, dy_ref, alg_ref, alb_ref, ws_ref, bs_ref, blg_ref, blb_ref,
             dza_ref, dcb_ref, dalg_ref, dalb_ref, dws_ref, dbs_ref, dblg_ref, dblb_ref,
             dlv_ref):
        @pl.when(pl.program_id(0) == 0)
        def _():
            for ref in (dalg_ref, dalb_ref, dws_ref, dbs_ref, dblg_ref, dblb_ref):
                ref[...] = jnp.zeros_like(ref)
        ua = za_ref[:, :D_A].astype(F32)
        va = za_ref[:, D_A:].astype(F32)
        gu = _gelu(ua)
        gv = _gelu(va)
        xh, r = _ln_stats(gv)
        alg = alg_ref[...]
        lv = (xh * alg + alb_ref[...]).astype(BF16)
        dya = dy_ref[:, :D_A].astype(F32)
        mask = _causal_mask()
        for h in range(A_HEADS):
            wm = jnp.where(mask, ws_ref[h], 0.0).astype(BF16)
            cols = slice(h * HEAD_DIM, (h + 1) * HEAD_DIM)
            dwm = jnp.zeros((CHUNK, CHUNK), F32)
            dbs = jnp.zeros((CHUNK, 1), F32)
            for c in range(nchunk):
                rows = slice(c * CHUNK, (c + 1) * CHUNK)
                lvb = lv[rows, cols]
                mixed = jnp.dot(wm, lvb, preferred_element_type=F32) + bs_ref[h]
                dyb = dya[rows, cols]
                dza_ref[rows, cols] = (dyb * mixed * _dgelu(ua[rows, cols])).astype(BF16)
                dmixed = dyb * gu[rows, cols]
                dmb = dmixed.astype(BF16)
                dlv_ref[rows, cols] = lax.dot_general(wm, dmb, tn_dims, preferred_element_type=F32)
                dwm = dwm + lax.dot_general(dmb, lvb, nt_dims, preferred_element_type=F32)
                dbs = dbs + jnp.sum(dmixed, axis=1, keepdims=True)
            dws_ref[h] += jnp.where(mask, dwm, 0.0)
            dbs_ref[h] += dbs
        dlv = dlv_ref[...]
        dalg_ref[...] += _rowsum(dlv * xh)
        dalb_ref[...] += _rowsum(dlv)
        dgv = _ln_bwd(dlv, xh, r, alg)
        dza_ref[:, D_A:] = (dgv * _dgelu(va)).astype(BF16)
        xhb, rb = _ln_stats(cb_ref[...])
        blg = blg_ref[...]
        lb = xhb * blg + blb_ref[...]
        dlb = dy_ref[:, D_A:].astype(F32) * _dsilu(lb)
        dblg_ref[...] += _rowsum(dlb * xhb)
        dblb_ref[...] += _rowsum(dlb)
        dcb_ref[...] = _ln_bwd(dlb, xhb, rb, blg)

    row = lambda i: (i, 0)
    par = lambda i: (0, 0)
    par3 = lambda i: (0, 0, 0)
    return _pallas(
        comm, body, name=name, grid=(T // tm,),
        in_specs=[pl.BlockSpec((tm, 2 * D_A), row), pl.BlockSpec((tm, D_B), row),
                  pl.BlockSpec((tm, D_A + D_B), row),
                  pl.BlockSpec((1, D_A), par), pl.BlockSpec((1, D_A), par),
                  pl.BlockSpec((A_HEADS, CHUNK, CHUNK), par3),
                  pl.BlockSpec((A_HEADS, CHUNK, 1), par3),
                  pl.BlockSpec((1, D_B), par), pl.BlockSpec((1, D_B), par)],
        out_specs=[pl.BlockSpec((tm, 2 * D_A), row), pl.BlockSpec((tm, D_B), row),
                   pl.BlockSpec((1, D_A), par), pl.BlockSpec((1, D_A), par),
                   pl.BlockSpec((A_HEADS, CHUNK, CHUNK), par3),
                   pl.BlockSpec((A_HEADS, CHUNK, 1), par3),
                   pl.BlockSpec((1, D_B), par), pl.BlockSpec((1, D_B), par)],
        out_shape=[jax.ShapeDtypeStruct((T, 2 * D_A + 2 * D_B), BF16), jax.ShapeDtypeStruct((T, D_B), F32),
                   jax.ShapeDtypeStruct((1, D_A), F32), jax.ShapeDtypeStruct((1, D_A), F32),
                   jax.ShapeDtypeStruct((A_HEADS, CHUNK, CHUNK), F32),
                   jax.ShapeDtypeStruct((A_HEADS, CHUNK, 1), F32),
                   jax.ShapeDtypeStruct((1, D_B), F32), jax.ShapeDtypeStruct((1, D_B), F32)],
        scratch_shapes=[pltpu.VMEM((tm, D_A), F32)],
        compiler_params=_cparams("arbitrary"),
    )(z, cb, dy, a_ln_g, a_ln_b, w_s, b_s, b_ln_g, b_ln_b)


def _mixer_b_conv_bwd(z, dcb, conv_w, dz, *, tm, name, comm=None):
    T = z.shape[0]
    halo = HALO_LONG

    def body(zb_ref, dcb_ref, dcn_ref, cw_ref, dz_in_ref, dzb_ref, dcw_ref, dbias_ref, dext_ref):
        i = pl.program_id(0)
        last = pl.num_programs(0) - 1

        @pl.when(i == 0)
        def _():
            dcw_ref[...] = jnp.zeros_like(dcw_ref)
            dbias_ref[...] = jnp.zeros_like(dbias_ref)
        dcb = dcb_ref[...]
        dext_ref[0:tm, :] = dcb
        dext_ref[tm:tm + halo, :] = jnp.where(i < last, dcn_ref[...], 0.0)
        dbias_ref[...] += _rowsum(dcb)
        for rb in range(tm // CONV_ROWS):
            for cb in range(D_B // CONV_COLS):
                cs = slice(cb * CONV_COLS, (cb + 1) * CONV_COLS)
                gcs = slice(D_B + cb * CONV_COLS, D_B + (cb + 1) * CONV_COLS)
                rs = slice(rb * CONV_ROWS, (rb + 1) * CONV_ROWS)
                xbb = zb_ref[rs, cs].astype(F32)
                sgb = _sigmoid(zb_ref[rs, gcs].astype(F32))
                yb0 = xbb * sgb
                window = dext_ref[rb * CONV_ROWS:rb * CONV_ROWS + CONV_ROWS + halo, cs]
                acc = jnp.zeros((CONV_ROWS, CONV_COLS), F32)
                for k in range(B_CONV):
                    shifted = _rows_after(window, (B_CONV - 1) - k)[:CONV_ROWS]
                    acc = acc + cw_ref[k:k + 1, cs] * shifted
                    dcw_ref[k:k + 1, cs] += _rowsum(shifted * yb0)
                dzb_ref[rs, cs] = (acc * sgb).astype(BF16)
                dzb_ref[rs, gcs] = (acc * xbb * sgb * (1.0 - sgb)).astype(BF16)

    row = lambda i: (i, 0)
    par = lambda i: (0, 0)
    return _pallas(
        comm, body, name=name, grid=(T // tm,),
        in_specs=[pl.BlockSpec((tm, 2 * D_B), lambda i: (i, 1)),
                  pl.BlockSpec((tm, D_B), row),
                  pl.BlockSpec((halo, D_B), lambda i: (_halo_next_index(tm, halo, T)(i), 0)),
                  pl.BlockSpec((B_CONV, D_B), par), pl.BlockSpec(memory_space=pl.ANY)],
        out_specs=[pl.BlockSpec((tm, 2 * D_B), lambda i: (i, 1)), pl.BlockSpec((B_CONV, D_B), par),
                   pl.BlockSpec((1, D_B), par)],
        out_shape=[jax.ShapeDtypeStruct(dz.shape, BF16), jax.ShapeDtypeStruct((B_CONV, D_B), F32),
                   jax.ShapeDtypeStruct((1, D_B), F32)],
        scratch_shapes=[pltpu.VMEM((tm + halo, D_B), F32)], aliases={4: 0},
        compiler_params=_cparams("arbitrary"),
    )(z, dcb, dcb, conv_w, dz)


def _rows_before(x, a):
    return x if a == 0 else pltpu.roll(x, a, axis=0)


def _rows_after(x, a):
    return x if a == 0 else pltpu.roll(x, x.shape[0] - a, axis=0)


def _conv3(w_ref, x, halo, cs):
    acc = w_ref[2:3, cs] * x[halo:]
    acc = acc + w_ref[1:2, cs] * _rows_before(x, 1)[halo:]
    return acc + w_ref[0:1, cs] * _rows_before(x, 2)[halo:]


def _mixer_c_fwd(z, conv_w, *, tm, name, comm=None):
    T = z.shape[0]
    D = D_MODEL
    halo = HALO_SHORT
    W = CONV_COLS

    def body(bg_ref, cg_ref, xv_ref, cgh_ref, xvh_ref, w_ref, r_ref):
        i = pl.program_id(0)
        for cb in range(D // W):
            cs = slice(cb * W, (cb + 1) * W)
            prev = jnp.where(i > 0, cgh_ref[:, cs].astype(F32) * xvh_ref[:, cs].astype(F32), 0.0)
            p = jnp.concatenate([prev, cg_ref[:, cs].astype(F32) * xv_ref[:, cs].astype(F32)], axis=0)
            r_ref[:, cs] = (bg_ref[:, cs].astype(F32) * _conv3(w_ref, p, halo, cs)).astype(BF16)

    hp = _halo_prev_index(tm, halo)
    return _pallas(
        comm, body, name=name, grid=(T // tm,),
        in_specs=[pl.BlockSpec((tm, D), lambda i: (i, 0)), pl.BlockSpec((tm, D), lambda i: (i, 1)),
                  pl.BlockSpec((tm, D), lambda i: (i, 2)),
                  pl.BlockSpec((halo, D), lambda i: (hp(i), 1)),
                  pl.BlockSpec((halo, D), lambda i: (hp(i), 2)),
                  pl.BlockSpec((None, C_CONV, D), lambda i: (0, 0, 0))],
        out_specs=pl.BlockSpec((tm, D), lambda i: (i, 0)),
        out_shape=jax.ShapeDtypeStruct((T, D), BF16),
        compiler_params=_cparams("parallel"),
    )(z, z, z, z, z, conv_w)


def _mixer_c_bwd(z, dr, conv_w, *, tm, name, comm=None):
    T = z.shape[0]
    D = D_MODEL
    halo = HALO_SHORT
    W = CONV_COLS

    def body(bg_ref, cg_ref, xv_ref, cgh_ref, xvh_ref, bgn_ref, dr_ref, drn_ref, w_ref, dz_ref, dw_ref):
        i = pl.program_id(0)
        last = pl.num_programs(0) - 1

        @pl.when(i == 0)
        def _():
            dw_ref[...] = jnp.zeros_like(dw_ref)
        for cb in range(D // W):
            cs = slice(cb * W, (cb + 1) * W)
            cg = cg_ref[:, cs].astype(F32)
            xv = xv_ref[:, cs].astype(F32)
            dr = dr_ref[:, cs].astype(F32)
            p = cg * xv
            prev = jnp.where(i > 0, cgh_ref[:, cs].astype(F32) * xvh_ref[:, cs].astype(F32), 0.0)
            q = _conv3(w_ref, jnp.concatenate([prev, p], axis=0), halo, cs)
            dz_ref[:, cs] = (dr * q).astype(BF16)
            nxt = jnp.where(i < last, drn_ref[:, cs].astype(F32) * bgn_ref[:, cs].astype(F32), 0.0)
            dq = jnp.concatenate([dr * bg_ref[:, cs].astype(F32), nxt], axis=0)
            dp = None
            for k in range(C_CONV):
                shifted = _rows_after(dq, 2 - k)[:tm]
                term = w_ref[k:k + 1, cs] * shifted
                dp = term if dp is None else dp + term
                dw_ref[k:k + 1, cs] += _rowsum(shifted * p)
            dz_ref[:, D + cb * W:D + (cb + 1) * W] = (dp * xv).astype(BF16)
            dz_ref[:, 2 * D + cb * W:2 * D + (cb + 1) * W] = (dp * cg).astype(BF16)

    hp = _halo_prev_index(tm, halo)
    hn = _halo_next_index(tm, halo, T)
    return _pallas(
        comm, body, name=name, grid=(T // tm,),
        in_specs=[pl.BlockSpec((tm, D), lambda i: (i, 0)), pl.BlockSpec((tm, D), lambda i: (i, 1)),
                  pl.BlockSpec((tm, D), lambda i: (i, 2)),
                  pl.BlockSpec((halo, D), lambda i: (hp(i), 1)),
                  pl.BlockSpec((halo, D), lambda i: (hp(i), 2)),
                  pl.BlockSpec((halo, D), lambda i: (hn(i), 0)),
                  pl.BlockSpec((tm, D), lambda i: (i, 0)),
                  pl.BlockSpec((halo, D), lambda i: (hn(i), 0)),
                  pl.BlockSpec((None, C_CONV, D), lambda i: (0, 0, 0))],
        out_specs=[pl.BlockSpec((tm, 3 * D), lambda i: (i, 0)),
                   pl.BlockSpec((C_CONV, D), lambda i: (0, 0))],
        out_shape=[jax.ShapeDtypeStruct((T, 3 * D), BF16), jax.ShapeDtypeStruct((C_CONV, D), F32)],
        compiler_params=_cparams("arbitrary"),
    )(z, z, z, z, z, z, dr, dr, conv_w)


FFN_COLS = 128


def _ffn_act_fwd(up, conv_w, *, layer, tm, name, comm=None):
    T = up.shape[0]
    halo = HALO_SHORT
    W = FFN_COLS

    def body(up_ref, uph_ref, w_ref, a_ref, upc_ref):
        i = pl.program_id(0)

        def conv(cs):
            prev = jnp.where(i > 0, uph_ref[:, cs], jnp.zeros((halo, W), BF16))
            return _conv3(w_ref, jnp.concatenate([prev, up_ref[:, cs]], axis=0).astype(F32), halo, cs)

        for cb in range(D_FF // W):
            gs = slice(cb * W, (cb + 1) * W)
            vs = slice(D_FF + cb * W, D_FF + (cb + 1) * W)
            g = conv(gs)
            v = conv(vs)
            upc_ref[:, gs] = g.astype(BF16)
            upc_ref[:, vs] = v.astype(BF16)
            a_ref[:, gs] = (_silu(g) * v).astype(BF16)

    return _pallas(
        comm, body, name=name, grid=(T // tm,),
        in_specs=[pl.BlockSpec((tm, 2 * D_FF), lambda i: (i, 0)),
                  pl.BlockSpec((halo, 2 * D_FF), lambda i: (_halo_prev_index(tm, halo)(i), 0)),
                  pl.BlockSpec((None, F_CONV, 2 * D_FF), lambda i: (layer, 0, 0))],
        out_specs=[pl.BlockSpec((tm, D_FF), lambda i: (i, 0)),
                   pl.BlockSpec((tm, 2 * D_FF), lambda i: (i, 0))],
        out_shape=[jax.ShapeDtypeStruct((T, D_FF), BF16), jax.ShapeDtypeStruct((T, 2 * D_FF), BF16)],
        compiler_params=_cparams("parallel"),
    )(up, up, conv_w)


def _ffn_act_bwd(up, upc, da, conv_w, *, layer, tm, name, comm=None):
    T = up.shape[0]
    halo = HALO_SHORT
    W = FFN_COLS

    def body(up_ref, upc_ref, upcn_ref, da_ref, dan_ref, w_ref, dup_ref, dw_ref):
        i = pl.program_id(0)
        last = pl.num_programs(0) - 1

        @pl.when(i == 0)
        def _():
            dw_ref[...] = jnp.zeros_like(dw_ref)
        live = jnp.where(i < last, 1.0, 0.0)
        for cb in range(D_FF // W):
            gs = slice(cb * W, (cb + 1) * W)
            vs = slice(D_FF + cb * W, D_FF + (cb + 1) * W)
            g = jnp.concatenate([upc_ref[:, gs], upcn_ref[:, gs]], axis=0).astype(F32)
            v = jnp.concatenate([upc_ref[:, vs], upcn_ref[:, vs]], axis=0).astype(F32)
            da = jnp.concatenate([da_ref[:, gs].astype(F32), dan_ref[:, gs].astype(F32) * live], axis=0)
            s = _sigmoid(g)
            silu = g * s
            grads = (da * v * (s * (1.0 + g * (1.0 - s))), da * silu)
            for cs, d in zip((gs, vs), grads):
                u = up_ref[:, cs].astype(F32)
                acc = None
                for k in range(F_CONV):
                    shifted = _rows_after(d, 2 - k)[:tm]
                    term = w_ref[k:k + 1, cs] * shifted
                    acc = term if acc is None else acc + term
                    dw_ref[k:k + 1, cs] += _rowsum(shifted * u)
                dup_ref[:, cs] = acc.astype(BF16)

    hn = _halo_next_index(tm, halo, T)
    return _pallas(
        comm, body, name=name, grid=(T // tm,),
        in_specs=[pl.BlockSpec((tm, 2 * D_FF), lambda i: (i, 0)),
                  pl.BlockSpec((tm, 2 * D_FF), lambda i: (i, 0)),
                  pl.BlockSpec((halo, 2 * D_FF), lambda i: (hn(i), 0)),
                  pl.BlockSpec((tm, D_FF), lambda i: (i, 0)),
                  pl.BlockSpec((halo, D_FF), lambda i: (hn(i), 0)),
                  pl.BlockSpec((None, F_CONV, 2 * D_FF), lambda i: (layer, 0, 0))],
        out_specs=[pl.BlockSpec((tm, 2 * D_FF), lambda i: (i, 0)),
                   pl.BlockSpec((F_CONV, 2 * D_FF), lambda i: (0, 0))],
        out_shape=[jax.ShapeDtypeStruct((T, 2 * D_FF), BF16),
                   jax.ShapeDtypeStruct((F_CONV, 2 * D_FF), F32)],
        compiler_params=_cparams("arbitrary"),
    )(up, upc, upc, da, da, conv_w)


def _local_step(x, tgt, small, plan):
    T = x.shape[0]
    tm_e = _pick(T, 256)
    tm_a = _pick(T, 512)
    tm_b = _pick(T, 128)
    tm_n = _pick(T, 512)
    tm = _pick(T, 1024)
    tm_f = _pick(T, 512)
    tt = _pick(T, 2048)
    nm = small["norm_mix"].reshape(2, 1, D_MODEL)
    nf = small["norm_ffn"].reshape(2, 1, D_MODEL)
    ngf = small["norm_final"].reshape(1, D_MODEL)
    b_s = small["a_b_s"].reshape(A_HEADS, CHUNK, 1)
    w_s = small["a_w_s"].reshape(A_HEADS, CHUNK, CHUNK)
    b_conv_w = small["b_conv_w"].reshape(B_CONV, D_B)
    sg = {}
    wt, cm = plan.weight, plan.comm

    h_m0, z_ab = _norm_mm_nn(x, nm, wt("ab_w_in", 0), g_layer=0, tm=tm, tn=512, name="ab_in", comm=cm("ab_in"))
    yab, cb = _mixer_ab_fwd(z_ab, small["a_ln_g"], small["a_ln_b"], w_s, b_s, b_conv_w, small["b_conv_b"],
                            small["b_ln_g"], small["b_ln_b"], tm=tm_e, name="mixer_ab", comm=cm("mixer_ab"))
    x1, h_f0 = _mm_nn(yab, wt("ab_w_out", 0), layer=0, tm=tm, tn=D_MODEL, residual=x, norm=(nf, 0),
                      name="ab_out", comm=cm("ab_out"))

    def ffn_fwd(xin, h, layer, norm):
        up = _mm_nn(h, wt("f_w_up", layer), layer=0, tm=tm, tn=2 * 1408, out_dtype=BF16, name=f"ffn_up{layer}",
                    comm=cm(f"ffn_up{layer}"))
        a, upc = _ffn_act_fwd(up, small["f_conv_w"], layer=layer, tm=tm_a, name=f"ffn_act{layer}",
                              comm=cm(f"ffn_act{layer}"))
        out = _mm_nn(a, wt("f_w_down", layer), layer=0, tm=tm, tn=D_MODEL, residual=xin, norm=norm,
                     name=f"ffn_down{layer}", comm=cm(f"ffn_down{layer}"))
        return up, upc, a, out

    up0, upc0, a0, (x2, h_m1) = ffn_fwd(x1, h_f0, 0, (nm, 1))
    z_c = _mm_nn(h_m1, wt("c_w_in", 0), layer=0, tm=tm, tn=768, out_dtype=BF16, name="c_in", comm=cm("c_in"))
    r = _mixer_c_fwd(z_c, small["c_conv_w"], tm=tm_e, name="mixer_c", comm=cm("mixer_c"))
    x3, h_f1 = _mm_nn(r, wt("c_w_out", 0), layer=0, tm=tm, tn=D_MODEL, residual=x2, norm=(nf, 1),
                      name="c_out", comm=cm("c_out"))
    up1, upc1, a1, x4 = ffn_fwd(x3, h_f1, 1, None)
    loss, dx, sg["norm_final"] = _loss_head(x4, tgt, ngf, tm=tm_n, name="loss_head")

    def ffn_bwd(dx, xin, h, up, upc, a, layer):
        da = _mm_nt(dx, wt("f_w_down", layer), layer=0, tm=tm, tn=1408, out_dtype=BF16,
                    name=f"ffn_down_dx{layer}", comm=cm(f"ffn_down_dx{layer}"))
        plan.grad_ready("f_w_down", layer, _mm_tn(a, dx, shards=None, tk=1408, tn=1024, tt=tt,
                                                  name=f"ffn_down_dw{layer}", comm=cm(f"ffn_down_dw{layer}")))
        dup, dcw = _ffn_act_bwd(up, upc, da, small["f_conv_w"], layer=layer, tm=tm_b, name=f"ffn_act_bwd{layer}",
                                comm=cm(f"ffn_act_bwd{layer}"))
        dxin, dg = _mm_nt_norm(dup, wt("f_w_up", layer), xin, nf, dx, g_layer=layer, tm=tm_f,
                               name=f"ffn_up_dx{layer}", comm=cm(f"ffn_up_dx{layer}"))
        plan.grad_ready("f_w_up", layer, _mm_tn(h, dup, shards=N_CHIPS, tk=512, tn=2 * 1408, tt=tt,
                                                name=f"ffn_up_dw{layer}", comm=cm(f"ffn_up_dw{layer}")))
        return dxin, dg, dcw

    dx, dnf1, dfc1 = ffn_bwd(dx, x3, h_f1, up1, upc1, a1, 1)
    dr = _mm_nt(dx, wt("c_w_out", 0), layer=0, tm=tm, tn=512, out_dtype=BF16, name="c_out_dx", comm=cm("c_out_dx"))
    plan.grad_ready("c_w_out", 0, _mm_tn(r, dx, shards=None, tk=1024, tn=1024, tt=tt, name="c_out_dw",
                                         comm=cm("c_out_dw")))
    dz_c, dccw = _mixer_c_bwd(z_c, dr, small["c_conv_w"], tm=tm_e, name="mixer_c_bwd", comm=cm("mixer_c_bwd"))
    sg["c_conv_w"] = dccw.reshape(1, C_CONV, D_MODEL)
    plan.grad_ready("c_w_in", 0, _mm_tn(h_m1, dz_c, shards=N_CHIPS, tk=1024, tn=768, tt=tt, name="c_in_dw",
                                        comm=cm("c_in_dw")))
    dx, dnm1 = _mm_nt_norm(dz_c, wt("c_w_in", 0), x2, nm, dx, g_layer=1, tm=tm_f, name="c_in_dx",
                           comm=cm("c_in_dx"))
    dx, dnf0, dfc0 = ffn_bwd(dx, x1, h_f0, up0, upc0, a0, 0)
    dyab = _mm_nt(dx, wt("ab_w_out", 0), layer=0, tm=tm, tn=512, out_dtype=BF16, name="ab_out_dx",
                  comm=cm("ab_out_dx"))
    plan.grad_ready("ab_w_out", 0, _mm_tn(yab, dx, shards=None, tk=1024, tn=1024, tt=tt, name="ab_out_dw",
                                          comm=cm("ab_out_dw")))
    (dza, dcb, sg["a_ln_g"], sg["a_ln_b"], dws, dbs, sg["b_ln_g"], sg["b_ln_b"]) = _mixer_ab_bwd_pre(
        z_ab, cb, dyab, small["a_ln_g"], small["a_ln_b"], w_s, b_s, small["b_ln_g"], small["b_ln_b"],
        tm=tm_e, name="mixer_ab_bwd", comm=cm("mixer_ab_bwd"))
    dz_ab, dbcw, sg["b_conv_b"] = _mixer_b_conv_bwd(z_ab, dcb, b_conv_w, dza, tm=tm_e, name="mixer_b_conv_bwd",
                                                    comm=cm("mixer_b_conv_bwd"))
    sg["a_w_s"] = dws.reshape(1, A_HEADS, CHUNK, CHUNK)
    sg["a_b_s"] = dbs.reshape(1, A_HEADS, CHUNK)
    sg["b_conv_w"] = dbcw.reshape(1, B_CONV, D_B)
    plan.grad_ready("ab_w_in", 0, _mm_tn(h_m0, dz_ab, shards=N_CHIPS, tk=1024, tn=512, tt=tt, name="ab_in_dw",
                                         comm=cm("ab_in_dw")))
    dx, dnm0 = _mm_nt_norm(dz_ab, wt("ab_w_in", 0), x, nm, dx, g_layer=0, tm=tm_f, name="ab_in_dx",
                           comm=cm("ab_in_dx"))

    sg["norm_mix"] = [dnm0, dnm1]
    sg["norm_ffn"] = [dnf0, dnf1]
    sg["f_conv_w"] = [dfc0, dfc1]
    return loss, dx, sg


BLOCK_BYTES = 3 * 1024 * 1024


BF16_SUBLANES = 16


def _row_tile(rows, row_bytes, step=SUBLANES):
    best = None
    for tr in range(step, rows + 1, step):
        if rows % tr == 0 and tr * row_bytes <= BLOCK_BYTES:
            best = tr
    if best is None:
        raise ValueError(f"no row tile for {rows}")
    return best


def _place_scalars():
    x, y, c = lax.axis_index("x"), lax.axis_index("y"), lax.axis_index("c")
    return jnp.stack([c, 2 * x + y, 2 * (1 - x) + y, 2 * x + (1 - y), 2 * (1 - x) + (1 - y)]).astype(jnp.int32)


def _cast_into_slot(w, place, *, layer, paired, name):
    L, rows, cols = w.shape
    tr = _row_tile(rows, cols * 4, BF16_SUBLANES)

    def body(place_ref, w_ref, o_ref):
        o_ref[...] = w_ref[...].astype(BF16)

    if paired:
        out_spec = pl.BlockSpec((None, None, tr, cols), lambda i, p: (0, p[1] // 2, i, p[1] % 2))
        out_shape = jax.ShapeDtypeStruct((1, N_CHIPS // 2, rows, 2 * cols), BF16)
    else:
        out_spec = pl.BlockSpec((None, None, tr, cols), lambda i, p: (0, p[1], i, 0))
        out_shape = jax.ShapeDtypeStruct((1, N_CHIPS, rows, cols), BF16)
    return pl.pallas_call(
        body, name=name,
        grid_spec=pltpu.PrefetchScalarGridSpec(
            num_scalar_prefetch=1, grid=(rows // tr,),
            in_specs=[pl.BlockSpec((None, tr, cols), lambda i, p: (layer, i, 0))],
            out_specs=out_spec),
        out_shape=out_shape,
        compiler_params=_cparams("parallel"),
    )(place, w)


def _pair_sum(g, theirs, place, *, name):
    S, rows, cols = g.shape
    half = rows // 2
    tr = _row_tile(half, cols * 4, BF16_SUBLANES)
    nb = half // tr

    def body(place_ref, g_ref, t_ref, o_ref):
        o_ref[...] = (g_ref[...] + t_ref[...]).astype(BF16)

    spec = pl.BlockSpec((None, tr, cols), lambda s, i, p: (s, i, 0))
    return pl.pallas_call(
        body, name=name,
        grid_spec=pltpu.PrefetchScalarGridSpec(
            num_scalar_prefetch=1, grid=(S, nb),
            in_specs=[pl.BlockSpec((None, tr, cols), lambda s, i, p: (s, p[0] * nb + i, 0)), spec],
            out_specs=spec),
        out_shape=jax.ShapeDtypeStruct((S, half, cols), BF16),
        compiler_params=_cparams("parallel", "parallel"),
    )(place, g, theirs)


def _chip_sum(p, r, g_prev, place, *, layer, shape, name):
    L, rows, cols = shape
    half = rows // 2
    tr = _row_tile(half, cols * 4, BF16_SUBLANES)
    nb = half // tr

    def body(place_ref, p_ref, r_ref, *rest):
        o_ref = rest[-1]
        mine = p_ref[...].astype(F32)
        peers = [r_ref[j].astype(F32) for j in range(3)]
        acc = None
        for s in range(N_CHIPS):
            term = jnp.where(place_ref[1] == s, mine,
                             jnp.where(place_ref[2] == s, peers[0],
                                       jnp.where(place_ref[3] == s, peers[1], peers[2])))
            acc = term if acc is None else acc + term
        o_ref[...] = acc

    in_specs = [pl.BlockSpec((None, tr, cols), lambda i, pr: (pr[1], i, 0)),
                pl.BlockSpec((3, tr, cols), lambda i, pr: (0, i, 0))]
    args = [place, p, r]
    aliases = {}
    if g_prev is not None:
        in_specs.append(HBM_REF)
        args.append(g_prev)
        aliases = {3: 0}
    return pl.pallas_call(
        body, name=name,
        grid_spec=pltpu.PrefetchScalarGridSpec(
            num_scalar_prefetch=1, grid=(nb,), in_specs=in_specs,
            out_specs=pl.BlockSpec((None, tr, cols), lambda i, pr: (layer, pr[0] * nb + i, 0))),
        out_shape=jax.ShapeDtypeStruct(shape, F32), input_output_aliases=aliases,
        compiler_params=_cparams("parallel"),
    )(*args)


def _adamw_math(w, g, m, v):
    m2 = ADAM_B1 * m + (1.0 - ADAM_B1) * g
    v2 = ADAM_B2 * v + (1.0 - ADAM_B2) * (g * g)
    m_hat = m2 / (1.0 - ADAM_B1 ** ADAM_STEP)
    v_hat = v2 / (1.0 - ADAM_B2 ** ADAM_STEP)
    delta = -ADAM_LR * (m_hat / (jnp.sqrt(v_hat) + ADAM_EPS) + ADAM_WD * w)
    return delta, m2, v2


def _adamw(w, g, m, v, *, name):
    L, rows, cols = w.shape
    tr = _row_tile(rows, cols * 4)

    def body(w_ref, g_ref, m_ref, v_ref, d_ref, m2_ref, v2_ref):
        d, m2, v2 = _adamw_math(w_ref[...], g_ref[...], m_ref[...], v_ref[...])
        d_ref[...] = d
        m2_ref[...] = m2
        v2_ref[...] = v2

    spec = pl.BlockSpec((None, tr, cols), lambda l, i: (l, i, 0))
    shape = jax.ShapeDtypeStruct(w.shape, F32)
    return pl.pallas_call(
        body, name=name, grid=(L, rows // tr), in_specs=[spec] * 4, out_specs=[spec] * 3,
        out_shape=[shape] * 3,
        compiler_params=_cparams("parallel", "parallel"),
    )(w, g, m, v)


def _allreduce_pack(pack, *, name, comm):
    R = pack.shape[0]
    half = R // 2
    nr, nw = len(comm.reads), len(comm.writes)

    def body(*refs):
        p_ref, rd, wr_in = refs[0], refs[1:1 + nr], refs[1 + nr:1 + nr + nw]
        o_ref, wr_out = refs[1 + nr + nw], refs[2 + nr + nw:2 + nr + 2 * nw]
        sib_ref, chip_ref, parts_ref, sems, comm_sems = refs[2 + nr + 2 * nw:]
        src = dict(zip(comm.reads, rd))
        src.update(zip(comm.writes, wr_in))
        dst = dict(zip(comm.writes, wr_out))
        comm.start(src, dst, comm_sems)
        x, y, c, k, sib, peers = _place()
        swap = _remote(p_ref, sib_ref, sems.at[0, 0], sems.at[0, 1], sib)
        swap.start()
        swap.wait()
        chip_ref[...] = p_ref[...] + sib_ref[...]
        mine = chip_ref.at[pl.ds(pl.multiple_of(c * half, SUBLANES), half)]
        sends = [_remote(mine, parts_ref.at[j], sems.at[1 + j, 0], sems.at[1 + j, 1], (px, py, c))
                 for j, (px, py) in enumerate(peers)]
        for rc in sends:
            rc.start()
        for rc in sends:
            rc.wait()
        own = mine[...]
        others = [parts_ref[j] for j in range(3)]
        acc = None
        for s in range(N_CHIPS):
            term = own
            for j, (px, py) in enumerate(peers):
                term = jnp.where(2 * px + py == s, others[j], term)
            acc = term if acc is None else acc + term
        done = o_ref.at[pl.ds(pl.multiple_of(c * half, SUBLANES), half)]
        done[...] = acc
        theirs = o_ref.at[pl.ds(pl.multiple_of((1 - c) * half, SUBLANES), half)]
        share = _remote(done, done, sems.at[4, 0], sems.at[4, 1], sib)
        share.start()
        _remote(done, theirs, sems.at[4, 0], sems.at[4, 1], sib).wait()
        comm.finish(src, dst, comm_sems)

    vm = pl.BlockSpec(memory_space=pltpu.VMEM)
    operands, shapes = _comm_operands(comm)
    outs = pl.pallas_call(
        body, name=name, in_specs=[vm] + [HBM_REF] * (nr + nw), out_specs=[vm] + [HBM_REF] * nw,
        out_shape=[jax.ShapeDtypeStruct((R, LANES), F32)] + shapes,
        input_output_aliases={1 + nr + q: 1 + q for q in range(nw)},
        scratch_shapes=[pltpu.VMEM((R, LANES), F32), pltpu.VMEM((R, LANES), F32),
                        pltpu.VMEM((3, half, LANES), F32), pltpu.SemaphoreType.DMA((5, 2)),
                        pltpu.SemaphoreType.DMA((comm.ncopies, 2))],
        compiler_params=pltpu.CompilerParams(vmem_limit_bytes=VMEM_BYTES_MAX),
    )(pack, *operands)
    for q, n in enumerate(comm.writes):
        comm.plan.bufs[n] = outs[1 + q]
    return outs[0]


PACK_UNIT = SUBLANES * LANES


def _pack(arrays):
    flat, sizes = [], []
    for a in arrays:
        pieces = a if isinstance(a, (list, tuple)) else [a]
        v = jnp.concatenate([p.reshape(-1) for p in pieces]) if len(pieces) > 1 else pieces[0].reshape(-1)
        size = v.shape[0]
        padded = -(-size // PACK_UNIT) * PACK_UNIT
        flat.append(jnp.pad(v, (0, padded - size)))
        sizes.append((size, padded))
    total = sum(p for _, p in sizes)
    if (total // PACK_UNIT) % 2:
        flat.append(jnp.zeros((PACK_UNIT,), F32))
    return jnp.concatenate(flat).reshape(-1, LANES), sizes


def _unpack(pack, sizes, shapes):
    v = pack.reshape(-1)
    out, off = [], 0
    for (size, padded), shape in zip(sizes, shapes):
        out.append(v[off:off + size].reshape(shape))
        off += padded
    return out


BIG = ("ab_w_in", "ab_w_out", "c_w_in", "c_w_out", "f_w_up", "f_w_down")
COL_SHARDED = ("ab_w_in", "c_w_in", "f_w_up")
PAIRED = ("f_w_up",)
SMALL_REPLICATED = ("norm_mix", "norm_ffn", "norm_final", "a_ln_g", "a_ln_b", "a_w_s", "a_b_s",
                    "b_conv_b", "b_ln_g", "b_ln_b")
SMALL_SHARDED = ("b_conv_w", "c_conv_w", "f_conv_w")
SMALL = SMALL_REPLICATED + SMALL_SHARDED
ALL_WEIGHTS = ("norm_mix", "norm_ffn", "norm_final", "ab_w_in", "a_ln_g", "a_ln_b", "a_w_s", "a_b_s",
               "b_conv_w", "b_conv_b", "b_ln_g", "b_ln_b", "ab_w_out", "c_w_in", "c_conv_w", "c_w_out",
               "f_w_up", "f_conv_w", "f_w_down")


SCHEDULE = {
    "ab_in": [("gi", "f_w_up", 0, 0, 4), ("gi", "ab_w_out", 0)],
    "mixer_ab": [("gd", "f_w_up", 0, 0, 4), ("gd", "ab_w_out", 0), ("gi", "f_w_up", 0, 1, 4),
                 ("gi", "f_w_up", 0, 2, 4), ("gi", "f_w_up", 0, 3, 4)],
    "ab_out": [("gd", "f_w_up", 0, 1, 4), ("gd", "f_w_up", 0, 2, 4), ("gd", "f_w_up", 0, 3, 4)],
    "ffn_up0": [("gi", "f_w_down", 0), ("gi", "c_w_in", 0, 0, 2)],
    "ffn_act0": [("gd", "f_w_down", 0), ("gd", "c_w_in", 0, 0, 2), ("gi", "c_w_in", 0, 1, 2),
                 ("gi", "f_w_up", 1, 0, 4)],
    "ffn_down0": [("gd", "c_w_in", 0, 1, 2), ("gd", "f_w_up", 1, 0, 4), ("gi", "f_w_up", 1, 1, 4),
                  ("gi", "f_w_up", 1, 2, 4)],
    "c_in": [("gd", "f_w_up", 1, 1, 4), ("gd", "f_w_up", 1, 2, 4), ("gi", "f_w_up", 1, 3, 4),
             ("gi", "c_w_out", 0)],
    "mixer_c": [("gd", "f_w_up", 1, 3, 4), ("gd", "c_w_out", 0), ("gi", "f_w_down", 1, 0, 2)],
    "c_out": [("gd", "f_w_down", 1, 0, 2), ("gi", "f_w_down", 1, 1, 2)],
    "ffn_up1": [("gd", "f_w_down", 1, 1, 2)],
    "ffn_act_bwd1": [("px", "f_w_down", 1)],
    "ffn_up_dx1": [("cx", "f_w_down", 1)],
    "c_out_dx": [("px", "f_w_up", 1, 0, 2)],
    "c_out_dw": [("px", "f_w_up", 1, 1, 2)],
    "mixer_c_bwd": [("cx", "f_w_up", 1, 0, 4), ("px", "c_w_out", 0)],
    "c_in_dw": [("cx", "f_w_up", 1, 1, 4)],
    "c_in_dx": [("cx", "f_w_up", 1, 2, 4), ("px", "c_w_in", 0)],
    "ffn_down_dx0": [("cx", "f_w_up", 1, 3, 4), ("cx", "c_w_out", 0)],
    "ffn_down_dw0": [("cx", "c_w_in", 0, 0, 2)],
    "ffn_act_bwd0": [("cx", "c_w_in", 0, 1, 2), ("px", "f_w_down", 0)],
    "ffn_up_dx0": [("cx", "f_w_down", 0)],
    "ffn_up_dw0": [("ps", "f_w_down", 1), ("ps", "f_w_up", 1)],
    "ab_out_dx": [("px", "f_w_up", 0, 0, 2)],
    "ab_out_dw": [("px", "f_w_up", 0, 1, 2)],
    "mixer_ab_bwd": [("cx", "f_w_up", 0, 0, 4), ("px", "ab_w_out", 0)],
    "mixer_b_conv_bwd": [("cx", "f_w_up", 0, 1, 4), ("cx", "ab_w_out", 0), ("ps", "c_w_out", 0),
                         ("ps", "c_w_in", 0), ("ps", "f_w_down", 0)],
    "ab_in_dw": [("cx", "f_w_up", 0, 2, 4)],
    "ab_in_dx": [("cx", "f_w_up", 0, 3, 4), ("px", "ab_w_in", 0)],
}


class _Plan:
    def __init__(self, shapes, place):
        self.shapes, self.place, self.bufs = shapes, place, {}
        self.summed, self.shared = set(), set()

    def weight(self, name, layer):
        g = self.bufs[f"w:{name}:{layer}"]
        if name in COL_SHARDED:
            return g
        _, S, rows, cols = g.shape
        return g.reshape(1, S * rows, cols)

    def grad_ready(self, name, layer, g):
        _, rows, cols = self.shapes[name]
        hbm = lambda a: pltpu.with_memory_space_constraint(a, pltpu.HBM)
        self.bufs[f"g:{name}:{layer}"] = g.reshape(N_CHIPS, rows, cols)
        self.bufs[f"t:{name}:{layer}"] = hbm(lax.empty((N_CHIPS, rows // 2, cols), F32))
        self.bufs[f"l:{name}:{layer}"] = hbm(lax.empty((3, rows // 2, cols), BF16))

    def job(self, kind, name, layer, part=0, parts=1):
        _, rows, cols = self.shapes[name]
        key = f"{name}:{layer}"
        if kind == "gi":
            return _job_gather_ici("w:" + key, rows, cols, part, parts)
        if kind == "gd":
            return _job_gather_d2d("w:" + key, rows, cols, part, parts)
        if kind == "px":
            return _job_pair_exchange("g:" + key, "t:" + key, rows, part, parts)
        if kind == "cx":
            if "p:" + key not in self.bufs:
                self.bufs["p:" + key] = _pair_sum(self.bufs["g:" + key], self.bufs["t:" + key], self.place,
                                                  name=f"pair_sum_{name}{layer}")
            nr = rows // 2 // parts
            return _job_chip_exchange("p:" + key, "l:" + key, part * nr, nr)
        if kind == "ps":
            self.chip_sum(name, layer)
            self.shared.add(key)
            return _job_pair_share("G:" + name, layer, rows)
        raise ValueError(kind)

    def chip_sum(self, name, layer):
        key = f"{name}:{layer}"
        if key not in self.summed:
            self.summed.add(key)
            self.bufs["G:" + name] = _chip_sum(self.bufs["p:" + key], self.bufs["l:" + key],
                                               self.bufs.get("G:" + name), self.place, layer=layer,
                                               shape=self.shapes[name], name=f"chip_sum_{name}{layer}")

    def comm(self, call):
        specs = SCHEDULE.get(call)
        return None if specs is None else _Comm(self, [self.job(*spec) for spec in specs])


def _step(x, tgt, w, m, v):
    chip = 2 * lax.axis_index("x") + lax.axis_index("y")
    place = _place_scalars()
    plan = _Plan({n: w[n].shape for n in BIG}, place)
    items = [(n, l) for n in BIG for l in range(w[n].shape[0])]

    for n, l in items:
        plan.bufs[f"w:{n}:{l}"] = _cast_into_slot(w[n], place, layer=l, paired=n in PAIRED, name=f"cast_{n}{l}")
    conv_pack, conv_sizes = _pack([w[n] for n in SMALL_SHARDED])
    hbm = lambda a: pltpu.with_memory_space_constraint(a, pltpu.HBM)
    plan.bufs["conv:mine"] = hbm(conv_pack)
    plan.bufs["conv:all"] = hbm(lax.empty((N_CHIPS,) + conv_pack.shape, F32))
    _comm_only(plan, [[plan.job("gi", "ab_w_in", 0), _job_chip_gather("conv:mine", "conv:all")],
                      [plan.job("gd", "ab_w_in", 0)]], name="gather_first")
    conv_shapes = [w[n].shape for n in SMALL_SHARDED]
    per_chip = [_unpack(plan.bufs["conv:all"][s], conv_sizes, conv_shapes) for s in range(N_CHIPS)]
    small = {n: w[n] for n in SMALL_REPLICATED}
    for idx, n in enumerate(SMALL_SHARDED):
        small[n] = jnp.concatenate([jnp.where(chip == s, w[n], per_chip[s][idx]) for s in range(N_CHIPS)], axis=-1)

    loss, dx, sg = _local_step(x, tgt, small, plan)

    g_pack, g_sizes = _pack([sg[n] for n in SMALL] + [loss])
    g_sum = _allreduce_pack(g_pack, name="allreduce_small_grads",
                            comm=_Comm(plan, [plan.job("cx", "ab_w_in", 0)]))
    full_shapes = [small[n].shape for n in SMALL]
    *summed, loss = _unpack(g_sum, g_sizes, full_shapes + [(1, 1)])
    g_small = dict(zip(SMALL, summed))
    for n in SMALL_SHARDED:
        width = w[n].shape[-1]
        g_small[n] = lax.dynamic_slice_in_dim(g_small[n], chip * width, width, axis=g_small[n].ndim - 1)

    _comm_only(plan, [[plan.job("ps", n, l) for n, l in items if f"{n}:{l}" not in plan.shared]],
               name="reduce_pair_share")
    grads_big = [plan.bufs["G:" + n] for n in BIG]

    grad, delta, new_m, new_v = {}, {}, {}, {}
    for n, g in zip(BIG, grads_big):
        grad[n] = g
        delta[n], new_m[n], new_v[n] = _adamw(w[n], g, m[n], v[n], name=f"adamw_{n}")
    shapes = [w[n].shape for n in SMALL]
    wp, sizes = _pack([w[n] for n in SMALL])
    gp, _ = _pack([g_small[n] for n in SMALL])
    mp, _ = _pack([m[n] for n in SMALL])
    vp, _ = _pack([v[n] for n in SMALL])
    R = wp.shape[0]
    dp, m2p, v2p = _adamw(wp.reshape(1, R, LANES), gp.reshape(1, R, LANES), mp.reshape(1, R, LANES),
                          vp.reshape(1, R, LANES), name="adamw_small")
    for n, d_, m_, v_ in zip(SMALL, _unpack(dp, sizes, shapes), _unpack(m2p, sizes, shapes),
                             _unpack(v2p, sizes, shapes)):
        grad[n] = g_small[n]
        delta[n], new_m[n], new_v[n] = d_, m_, v_
    return loss, dx, grad, delta, new_m, new_v


def kernel(x, norm_mix, norm_ffn, norm_final, ab_w_in, a_ln_g, a_ln_b, a_w_s, a_b_s, b_conv_w, b_conv_b, b_ln_g, b_ln_b, ab_w_out, c_w_in, c_conv_w, c_w_out, f_w_up, f_conv_w, f_w_down, loss_target, m_norm_mix, m_norm_ffn, m_norm_final, m_ab_w_in, m_a_ln_g, m_a_ln_b, m_a_w_s, m_a_b_s, m_b_conv_w, m_b_conv_b, m_b_ln_g, m_b_ln_b, m_ab_w_out, m_c_w_in, m_c_conv_w, m_c_w_out, m_f_w_up, m_f_conv_w, m_f_w_down, v_norm_mix, v_norm_ffn, v_norm_final, v_ab_w_in, v_a_ln_g, v_a_ln_b, v_a_w_s, v_a_b_s, v_b_conv_w, v_b_conv_b, v_b_ln_g, v_b_ln_b, v_ab_w_out, v_c_w_in, v_c_conv_w, v_c_w_out, v_f_w_up, v_f_conv_w, v_f_w_down):
    given = dict(locals())
    w = {n: given[n] for n in ALL_WEIGHTS}
    m = {n: given["m_" + n] for n in ALL_WEIGHTS}
    v = {n: given["v_" + n] for n in ALL_WEIGHTS}
    T = x.shape[1]
    loss, dx, grad, delta, new_m, new_v = _step(x.reshape(T, D_MODEL), loss_target.reshape(T, D_MODEL), w, m, v)
    out = [loss[0, 0], dx.reshape(x.shape)]
    for d in (grad, delta, new_m, new_v):
        out += [d[n] for n in ALL_WEIGHTS]
    return tuple(out)
```

```python
import functools
import math

import jax
import jax.numpy as jnp
from jax import lax
from jax.experimental import pallas as pl
from jax.experimental.pallas import tpu as pltpu

F32 = jnp.float32
BF16 = jnp.bfloat16

EPS = 1e-6
D_MODEL = 1024
CHUNK = 128
HEAD_DIM = 128
A_HEADS = 4
D_A = 512
D_B = 512
B_CONV = 31
C_CONV = 3
D_FF = 2816
F_CONV = 3
N_CHIPS = 4

ADAM_LR = 0.001
ADAM_B1 = 0.9
ADAM_B2 = 0.999
ADAM_EPS = 1e-08
ADAM_WD = 0.01
ADAM_STEP = 10

SUBLANES = 8
LANES = 128
HALO_SHORT = 16
HALO_LONG = 32
VMEM_BYTES_MAX = 60000 * 1024

INV_SQRT2 = 1.0 / math.sqrt(2.0)
INV_SQRT_2PI = 1.0 / math.sqrt(2.0 * math.pi)

MESH = pl.DeviceIdType.MESH


def _cparams(*sem):
    return pltpu.CompilerParams(dimension_semantics=sem, vmem_limit_bytes=VMEM_BYTES_MAX)


def _pick(total, pref):
    for c in (2048, 1024, 512, 256, 128):
        if c <= pref and total % c == 0:
            return c
    raise ValueError(f"no tile for {total}")


def _sigmoid(x):
    return jax.nn.sigmoid(x)


def _silu(x):
    return x * _sigmoid(x)


def _dsilu(x):
    s = _sigmoid(x)
    return s * (1.0 + x * (1.0 - s))


def _gelu(x):
    return 0.5 * x * (1.0 + lax.erf(x * INV_SQRT2))


def _dgelu(x):
    return 0.5 * (1.0 + lax.erf(x * INV_SQRT2)) + x * jnp.exp(-0.5 * x * x) * INV_SQRT_2PI


def _ln_stats(x):
    mu = jnp.mean(x, axis=-1, keepdims=True)
    xc = x - mu
    var = jnp.mean(xc * xc, axis=-1, keepdims=True)
    r = lax.rsqrt(var + EPS)
    return xc * r, r


def _ln_bwd(dy, xh, r, g):
    dxh = dy * g
    m1 = jnp.mean(dxh, axis=-1, keepdims=True)
    m2 = jnp.mean(dxh * xh, axis=-1, keepdims=True)
    return r * (dxh - m1 - xh * m2)


def _rowsum(x):
    return jnp.sum(x, axis=0, keepdims=True)


HBM_REF = pl.BlockSpec(memory_space=pltpu.HBM)


def _place():
    x, y, c = lax.axis_index("x"), lax.axis_index("y"), lax.axis_index("c")
    peers = [(1 - x, y), (x, 1 - y), (1 - x, 1 - y)]
    return x, y, c, 2 * x + y, (x, y, 1 - c), peers


def _half(rows, which):
    return pl.ds(which * (rows // 2), rows // 2)


def _remote(src, dst, send_sem, recv_sem, device):
    return pltpu.make_async_remote_copy(src_ref=src, dst_ref=dst, send_sem=send_sem, recv_sem=recv_sem,
                                        device_id=device, device_id_type=MESH)


class _Job:
    def __init__(self, reads, writes, ncopies, copies):
        self.reads, self.writes, self.ncopies, self.copies = reads, writes, ncopies, copies


def _share(rows, which, part, parts):
    nr = rows // 2 // parts
    return pl.ds(which * (rows // 2) + part * nr, nr)


def _slot(ref, chip, rows, cols):
    if ref.shape[1] == N_CHIPS:
        return ref.at[0, chip, rows]
    return ref.at[0, chip // 2, rows, pl.ds(pl.multiple_of((chip % 2) * cols, LANES), cols)]


def _job_gather_ici(name, rows, cols, part, parts):
    def copies(src, dst, sem):
        x, y, c, k, sib, peers = _place()
        mine_rows = _share(rows, c, part, parts)
        out = []
        for j, (px, py) in enumerate(peers):
            mine = _slot(src[name], k, mine_rows, cols)
            out.append((_remote(mine, _slot(dst[name], k, mine_rows, cols), sem(j, 0), sem(j, 1), (px, py, c)),
                        _remote(mine, _slot(dst[name], 2 * px + py, mine_rows, cols), sem(j, 0), sem(j, 1),
                                (px, py, c))))
        return out
    return _Job([], [name], 3, copies)


def _job_gather_d2d(name, rows, cols, part, parts):
    def copies(src, dst, sem):
        x, y, c, k, sib, peers = _place()
        out = []
        for j, (px, py) in enumerate(peers):
            mine_rows, their_rows = _share(rows, c, part, parts), _share(rows, 1 - c, part, parts)
            landed = _slot(src[name], 2 * px + py, mine_rows, cols)
            out.append((_remote(landed, _slot(dst[name], 2 * px + py, mine_rows, cols), sem(j, 0), sem(j, 1), sib),
                        _remote(landed, _slot(dst[name], 2 * px + py, their_rows, cols), sem(j, 0), sem(j, 1), sib)))
        return out
    return _Job([], [name], 3, copies)


def _job_chip_gather(sname, dname):
    def copies(src, dst, sem):
        x, y, c, k, sib, peers = _place()
        return [(_remote(src[sname], dst[dname].at[k], sem(j, 0), sem(j, 1), (px, py, c)),
                 _remote(src[sname], dst[dname].at[2 * px + py], sem(j, 0), sem(j, 1), (px, py, c)))
                for j, (px, py) in enumerate(peers)]
    return _Job([sname], [dname], 3, copies)


def _job_pair_exchange(gname, tname, rows, part, parts):
    nr = rows // 2 // parts

    def copies(src, dst, sem):
        x, y, c, k, sib, peers = _place()
        cp = _remote(src[gname].at[:, _share(rows, 1 - c, part, parts), :],
                     dst[tname].at[:, pl.ds(part * nr, nr), :], sem(0, 0), sem(0, 1), sib)
        return [(cp, cp)]
    return _Job([gname], [tname], 1, copies)


def _job_chip_exchange(pname, lname, r0, nr):
    def copies(src, dst, sem):
        x, y, c, k, sib, peers = _place()
        out = []
        for j, (px, py) in enumerate(peers):
            cp = _remote(src[pname].at[2 * px + py, pl.ds(r0, nr)], dst[lname].at[j, pl.ds(r0, nr)],
                         sem(j, 0), sem(j, 1), (px, py, c))
            out.append((cp, cp))
        return out
    return _Job([pname], [lname], 3, copies)


def _job_pair_share(name, layer, rows):
    def copies(src, dst, sem):
        x, y, c, k, sib, peers = _place()
        mine = src[name].at[layer, _half(rows, c)]
        return [(_remote(mine, dst[name].at[layer, _half(rows, c)], sem(0, 0), sem(0, 1), sib),
                 _remote(mine, dst[name].at[layer, _half(rows, 1 - c)], sem(0, 0), sem(0, 1), sib))]
    return _Job([], [name], 1, copies)


class _Comm:
    def __init__(self, plan, jobs):
        self.plan, self.jobs = plan, jobs
        self.writes, self.reads = [], []
        for job in jobs:
            for n in job.writes:
                if n not in self.writes:
                    self.writes.append(n)
        for job in jobs:
            for n in job.reads:
                if n not in self.writes and n not in self.reads:
                    self.reads.append(n)
        self.ncopies = sum(job.ncopies for job in jobs)

    def descriptors(self, src, dst, sems, base):
        out = []
        for job in self.jobs:
            sem = lambda j, which, base=base: sems.at[base + j, which]
            out += job.copies(src, dst, sem)
            base += job.ncopies
        return out

    def start(self, src, dst, sems, base=0):
        for first, _ in self.descriptors(src, dst, sems, base):
            first.start()

    def finish(self, src, dst, sems, base=0):
        for _, landed in self.descriptors(src, dst, sems, base):
            landed.wait()


def _comm_operands(comm):
    bufs = comm.plan.bufs
    shapes = [jax.ShapeDtypeStruct(bufs[n].shape, bufs[n].dtype) for n in comm.writes]
    return [bufs[n] for n in comm.reads] + [bufs[n] for n in comm.writes], shapes


def _pallas(comm, body, *, name, grid, in_specs, out_specs, out_shape, compiler_params, scratch_shapes=(),
            aliases=None):
    aliases = dict(aliases or {})
    if comm is None:
        return pl.pallas_call(body, name=name, grid=grid, in_specs=in_specs, out_specs=out_specs,
                              out_shape=out_shape, scratch_shapes=list(scratch_shapes),
                              input_output_aliases=aliases, compiler_params=compiler_params)
    single = not isinstance(out_shape, (list, tuple))
    base_specs = [out_specs] if single else list(out_specs)
    base_shape = [out_shape] if single else list(out_shape)
    nb, nr, nw, nbo, nsc = len(in_specs), len(comm.reads), len(comm.writes), len(base_specs), len(scratch_shapes)

    def wrapped(*refs):
        base_in, rd, wr_in = refs[:nb], refs[nb:nb + nr], refs[nb + nr:nb + nr + nw]
        o0 = nb + nr + nw
        base_out, wr_out = refs[o0:o0 + nbo], refs[o0 + nbo:o0 + nbo + nw]
        scratch, sems = refs[o0 + nbo + nw:o0 + nbo + nw + nsc], refs[-1]
        src = dict(zip(comm.reads, rd))
        src.update(zip(comm.writes, wr_in))
        dst = dict(zip(comm.writes, wr_out))
        first = functools.reduce(jnp.logical_and, [pl.program_id(a) == 0 for a in range(len(grid))])
        last = functools.reduce(jnp.logical_and,
                                [pl.program_id(a) == pl.num_programs(a) - 1 for a in range(len(grid))])

        @pl.when(first)
        def _():
            comm.start(src, dst, sems)
        body(*base_in, *base_out, *scratch)

        @pl.when(last)
        def _():
            comm.finish(src, dst, sems)

    operands, shapes = _comm_operands(comm)
    call = pl.pallas_call(
        wrapped, name=name, grid=grid, in_specs=list(in_specs) + [HBM_REF] * (nr + nw),
        out_specs=base_specs + [HBM_REF] * nw, out_shape=base_shape + shapes,
        input_output_aliases={**aliases, **{nb + nr + q: nbo + q for q in range(nw)}},
        scratch_shapes=list(scratch_shapes) + [pltpu.SemaphoreType.DMA((comm.ncopies, 2))],
        compiler_params=compiler_params)

    def run(*args):
        outs = call(*args, *operands)
        for q, n in enumerate(comm.writes):
            comm.plan.bufs[n] = outs[nbo + q]
        return outs[0] if single else list(outs[:nbo])

    return run


def _comm_only(plan, phases, *, name):
    comms = [_Comm(plan, jobs) for jobs in phases]
    both = _Comm(plan, [job for jobs in phases for job in jobs])
    nr, nw = len(both.reads), len(both.writes)

    def body(*refs):
        rd, wr_in, wr_out, sems = refs[:nr], refs[nr:nr + nw], refs[nr + nw:nr + 2 * nw], refs[-1]
        src = dict(zip(both.reads, rd))
        src.update(zip(both.writes, wr_in))
        dst = dict(zip(both.writes, wr_out))
        base = 0
        for comm in comms:
            comm.start(src, dst, sems, base)
            comm.finish(src, dst, sems, base)
            base += comm.ncopies

    operands, shapes = _comm_operands(both)
    outs = pl.pallas_call(
        body, name=name, in_specs=[HBM_REF] * (nr + nw), out_specs=[HBM_REF] * nw, out_shape=shapes,
        input_output_aliases={nr + q: q for q in range(nw)},
        scratch_shapes=[pltpu.SemaphoreType.DMA((both.ncopies, 2))],
    )(*operands)
    for q, n in enumerate(both.writes):
        plan.bufs[n] = outs[q]


def _mm_nn(a, w, *, layer, tm, tn, residual=None, norm=None, out_dtype=F32, name, comm=None):
    T, K = a.shape
    if w.ndim == 4:
        _, S, _, n4 = w.shape
        N = S * n4
        bps = n4 // tn
        w_spec = pl.BlockSpec((None, None, K, tn), lambda j, i: (layer, j // bps, 0, j % bps))
    else:
        N = w.shape[2]
        w_spec = pl.BlockSpec((None, K, tn), lambda j, i: (layer, 0, j))
    in_specs = [pl.BlockSpec((tm, K), lambda j, i: (i, 0)), w_spec]
    args = [a, w]
    if residual is not None:
        in_specs.append(pl.BlockSpec((tm, tn), lambda j, i: (i, j)))
        args.append(residual)
    out_specs = pl.BlockSpec((tm, tn), lambda j, i: (i, j))
    out_shape = jax.ShapeDtypeStruct((T, N), out_dtype)
    if norm is not None:
        assert tn == N
        g, norm_layer = norm
        in_specs.append(pl.BlockSpec((None, 1, N), lambda j, i: (norm_layer, 0, 0)))
        args.append(g)
        out_specs = [out_specs, pl.BlockSpec((tm, tn), lambda j, i: (i, j))]
        out_shape = [out_shape, jax.ShapeDtypeStruct((T, N), BF16)]

    def body(*refs):
        a_ref, w_ref = refs[0], refs[1]
        acc = jnp.dot(a_ref[...].astype(BF16), w_ref[...], preferred_element_type=F32)
        if residual is not None:
            acc = refs[2][...] + acc
        if norm is None:
            refs[-1][...] = acc.astype(out_dtype)
        else:
            refs[-2][...] = acc.astype(out_dtype)
            r = lax.rsqrt(jnp.mean(acc * acc, axis=-1, keepdims=True) + EPS)
            refs[-1][...] = (acc * r * refs[-3][...]).astype(BF16)

    return _pallas(
        comm, body, name=name, grid=(N // tn, T // tm), in_specs=in_specs,
        out_specs=out_specs, out_shape=out_shape,
        compiler_params=_cparams("parallel", "parallel"),
    )(*args)


def _norm_mm_nn(x, g, w, *, g_layer, tm, tn, name, comm=None):
    T, K = x.shape
    _, S, _, n4 = w.shape
    bps = n4 // tn

    def body(x_ref, g_ref, w_ref, h_ref, o_ref):
        @pl.when(pl.program_id(1) == 0)
        def _():
            xf = x_ref[...]
            r = lax.rsqrt(jnp.mean(xf * xf, axis=-1, keepdims=True) + EPS)
            h_ref[...] = (xf * r * g_ref[...]).astype(BF16)
        o_ref[...] = jnp.dot(h_ref[...], w_ref[...], preferred_element_type=F32).astype(BF16)

    return _pallas(
        comm, body, name=name, grid=(T // tm, S * bps),
        in_specs=[pl.BlockSpec((tm, K), lambda i, j: (i, 0)),
                  pl.BlockSpec((None, 1, K), lambda i, j: (g_layer, 0, 0)),
                  pl.BlockSpec((None, None, K, tn), lambda i, j: (0, j // bps, 0, j % bps))],
        out_specs=[pl.BlockSpec((tm, K), lambda i, j: (i, 0)), pl.BlockSpec((tm, tn), lambda i, j: (i, j))],
        out_shape=[jax.ShapeDtypeStruct((T, K), BF16), jax.ShapeDtypeStruct((T, S * n4), BF16)],
        compiler_params=_cparams("parallel", "arbitrary"),
    )(x, g, w)


def _mm_nt(dy, w, *, layer, tm, tn, name, out_dtype=F32, comm=None):
    T = dy.shape[0]
    nt_dims = (((1,), (1,)), ((), ()))
    _, R, N = w.shape

    def body2(dy_ref, w_ref, o_ref):
        o_ref[...] = lax.dot_general(dy_ref[...].astype(BF16), w_ref[...], nt_dims,
                                     preferred_element_type=F32).astype(out_dtype)

    return _pallas(
        comm, body2, name=name, grid=(R // tn, T // tm),
        in_specs=[pl.BlockSpec((tm, N), lambda j, i: (i, 0)),
                  pl.BlockSpec((None, tn, N), lambda j, i: (layer, j, 0))],
        out_specs=pl.BlockSpec((tm, tn), lambda j, i: (i, j)),
        out_shape=jax.ShapeDtypeStruct((T, R), out_dtype),
        compiler_params=_cparams("parallel", "parallel"),
    )(dy, w)


def _mm_tn(a, dy, *, shards, tk, tn, tt, name, comm=None):
    T, K = a.shape
    N = dy.shape[1]
    tn_dims = (((0,), (0,)), ((), ()))
    n4 = N if shards is None else N // shards
    span = max(tn // n4, 1)

    def body(a_ref, dy_ref, o_ref):
        @pl.when(pl.program_id(2) == 0)
        def _():
            o_ref[...] = jnp.zeros_like(o_ref)
        r = lax.dot_general(a_ref[...].astype(BF16), dy_ref[...].astype(BF16), tn_dims,
                            preferred_element_type=F32)
        if span == 1:
            o_ref[...] += r
        else:
            for q in range(span):
                o_ref[q] += r[:, q * n4:(q + 1) * n4]

    if shards is None:
        out_spec = pl.BlockSpec((tk, tn), lambda k, n, t: (k, n))
        out_shape = jax.ShapeDtypeStruct((K, N), F32)
    elif span > 1:
        out_spec = pl.BlockSpec((span, tk, n4), lambda k, n, t: (n, k, 0))
        out_shape = jax.ShapeDtypeStruct((shards, K, n4), F32)
    else:
        bps = n4 // tn
        out_spec = pl.BlockSpec((None, tk, tn), lambda k, n, t: (n // bps, k, n % bps))
        out_shape = jax.ShapeDtypeStruct((shards, K, n4), F32)
    return _pallas(
        comm, body, name=name, grid=(K // tk, N // tn, T // tt),
        in_specs=[pl.BlockSpec((tt, tk), lambda k, n, t: (t, k)),
                  pl.BlockSpec((tt, tn), lambda k, n, t: (t, n))],
        out_specs=out_spec, out_shape=out_shape,
        compiler_params=_cparams("parallel", "parallel", "arbitrary"),
    )(a, dy)


def _rmsnorm_bwd_math(xf, g, dh, dres):
    r = lax.rsqrt(jnp.mean(xf * xf, axis=-1, keepdims=True) + EPS)
    xh = xf * r
    dxh = dh * g
    dx = dres + r * (dxh - xh * jnp.mean(dxh * xh, axis=-1, keepdims=True))
    return dx, _rowsum(dh * xh)


def _mm_nt_norm(dy, w, x, g, dres, *, g_layer, tm, name, comm=None):
    T = dy.shape[0]
    _, S, K, n4 = w.shape
    nt_dims = (((1,), (1,)), ((), ()))

    def body(dy_ref, w_ref, x_ref, g_ref, dres_ref, dx_ref, dg_ref):
        @pl.when(pl.program_id(0) == 0)
        def _():
            dg_ref[...] = jnp.zeros_like(dg_ref)
        dh = None
        for s in range(S):
            part = lax.dot_general(dy_ref[:, s * n4:(s + 1) * n4].astype(BF16), w_ref[s], nt_dims,
                                   preferred_element_type=F32)
            dh = part if dh is None else dh + part
        dx, dg = _rmsnorm_bwd_math(x_ref[...], g_ref[...], dh, dres_ref[...])
        dx_ref[...] = dx
        dg_ref[...] += dg

    row = lambda i: (i, 0)
    return _pallas(
        comm, body, name=name, grid=(T // tm,),
        in_specs=[pl.BlockSpec((tm, S * n4), row),
                  pl.BlockSpec((None, S, K, n4), lambda i: (0, 0, 0, 0)),
                  pl.BlockSpec((tm, K), row),
                  pl.BlockSpec((None, 1, K), lambda i: (g_layer, 0, 0)),
                  pl.BlockSpec((tm, K), row)],
        out_specs=[pl.BlockSpec((tm, K), row), pl.BlockSpec((1, K), lambda i: (0, 0))],
        out_shape=[jax.ShapeDtypeStruct((T, K), F32), jax.ShapeDtypeStruct((1, K), F32)],
        compiler_params=_cparams("arbitrary"),
    )(dy, w, x, g, dres)


def _mm_nn_loss(a, w, residual, tgt, g, *, tm, name, comm=None):
    T, K = a.shape
    D = w.shape[2]

    def body(a_ref, w_ref, res_ref, t_ref, g_ref, loss_ref, dx_ref, dg_ref):
        @pl.when(pl.program_id(0) == 0)
        def _():
            dg_ref[...] = jnp.zeros_like(dg_ref)
            loss_ref[...] = jnp.zeros_like(loss_ref)
        xf = res_ref[...] + jnp.dot(a_ref[...], w_ref[...], preferred_element_type=F32)
        gg = g_ref[...]
        r = lax.rsqrt(jnp.mean(xf * xf, axis=-1, keepdims=True) + EPS)
        xh = xf * r
        err = xh * gg - t_ref[...]
        row = jnp.mean(err * err, axis=-1, keepdims=True)
        loss_ref[...] += 0.5 * jnp.sum(row, axis=0, keepdims=True)
        dy = err * (1.0 / D)
        dg_ref[...] += _rowsum(dy * xh)
        dxh = dy * gg
        dx_ref[...] = r * (dxh - xh * jnp.mean(dxh * xh, axis=-1, keepdims=True))

    row_spec = pl.BlockSpec((tm, D), lambda i: (i, 0))
    return _pallas(
        comm, body, name=name, grid=(T // tm,),
        in_specs=[pl.BlockSpec((tm, K), lambda i: (i, 0)), pl.BlockSpec((None, K, D), lambda i: (0, 0, 0)),
                  row_spec, row_spec, pl.BlockSpec((1, D), lambda i: (0, 0))],
        out_specs=[pl.BlockSpec((1, 1), lambda i: (0, 0)), row_spec, pl.BlockSpec((1, D), lambda i: (0, 0))],
        out_shape=[jax.ShapeDtypeStruct((1, 1), F32), jax.ShapeDtypeStruct((T, D), F32),
                   jax.ShapeDtypeStruct((1, D), F32)],
        compiler_params=_cparams("arbitrary"),
    )(a, w, residual, tgt, g)


CONV_ROWS = 64
CONV_COLS = 256


def _halo_prev_index(tm, halo):
    per = tm // halo
    return lambda i: jnp.maximum(i * per - 1, 0)


def _halo_next_index(tm, halo, total):
    per = tm // halo
    last = total // halo - 1
    return lambda i: jnp.minimum((i + 1) * per, last)


def _causal_mask():
    t = lax.broadcasted_iota(jnp.int32, (CHUNK, CHUNK), 0)
    s = lax.broadcasted_iota(jnp.int32, (CHUNK, CHUNK), 1)
    return s <= t


def _mixer_ab_fwd(z, a_ln_g, a_ln_b, w_s, b_s, conv_w, conv_b, b_ln_g, b_ln_b, *, tm, name, comm=None):
    T = z.shape[0]
    nchunk = tm // CHUNK
    halo = HALO_LONG

    def body(za_ref, zb_ref, zh_ref, alg_ref, alb_ref, ws_ref, bs_ref, cw_ref, cbias_ref,
             blg_ref, blb_ref, y_ref, cb_ref, ext_ref):
        i = pl.program_id(0)
        gu = _gelu(za_ref[:, :D_A].astype(F32))
        gv = _gelu(za_ref[:, D_A:].astype(F32))
        xh, _ = _ln_stats(gv)
        lv = (xh * alg_ref[...] + alb_ref[...]).astype(BF16)
        mask = _causal_mask()
        for h in range(A_HEADS):
            wm = jnp.where(mask, ws_ref[h], 0.0).astype(BF16)
            cols = slice(h * HEAD_DIM, (h + 1) * HEAD_DIM)
            for c in range(nchunk):
                rows = slice(c * CHUNK, (c + 1) * CHUNK)
                mixed = jnp.dot(wm, lv[rows, cols], preferred_element_type=F32) + bs_ref[h]
                y_ref[rows, cols] = (gu[rows, cols] * mixed).astype(BF16)
        ext_ref[halo:halo + tm, :] = zb_ref[:, :D_B].astype(F32) * _sigmoid(zb_ref[:, D_B:].astype(F32))
        prev = zh_ref[:, :D_B].astype(F32) * _sigmoid(zh_ref[:, D_B:].astype(F32))
        ext_ref[0:halo, :] = jnp.where(i > 0, prev, 0.0)
        for rb in range(tm // CONV_ROWS):
            for cb in range(D_B // CONV_COLS):
                cs = slice(cb * CONV_COLS, (cb + 1) * CONV_COLS)
                window = ext_ref[rb * CONV_ROWS:rb * CONV_ROWS + CONV_ROWS + halo, cs]
                acc = jnp.zeros((CONV_ROWS, CONV_COLS), F32)
                for k in range(B_CONV):
                    shifted = _rows_after(window, halo - (B_CONV - 1) + k)[:CONV_ROWS]
                    acc = acc + cw_ref[k:k + 1, cs] * shifted
                cb_ref[rb * CONV_ROWS:(rb + 1) * CONV_ROWS, cs] = acc + cbias_ref[:, cs]
        xhb, _ = _ln_stats(cb_ref[...])
        y_ref[:, D_A:] = _silu(xhb * blg_ref[...] + blb_ref[...]).astype(BF16)

    row = lambda i: (i, 0)
    par = lambda i: (0, 0)
    return _pallas(
        comm, body, name=name, grid=(T // tm,),
        in_specs=[pl.BlockSpec((tm, 2 * D_A), lambda i: (i, 0)),
                  pl.BlockSpec((tm, 2 * D_B), lambda i: (i, 1)),
                  pl.BlockSpec((halo, 2 * D_B), lambda i: (_halo_prev_index(tm, halo)(i), 1)),
                  pl.BlockSpec((1, D_A), par), pl.BlockSpec((1, D_A), par),
                  pl.BlockSpec((A_HEADS, CHUNK, CHUNK), lambda i: (0, 0, 0)),
                  pl.BlockSpec((A_HEADS, CHUNK, 1), lambda i: (0, 0, 0)),
                  pl.BlockSpec((B_CONV, D_B), par), pl.BlockSpec((1, D_B), par),
                  pl.BlockSpec((1, D_B), par), pl.BlockSpec((1, D_B), par)],
        out_specs=[pl.BlockSpec((tm, D_A + D_B), row), pl.BlockSpec((tm, D_B), row)],
        out_shape=[jax.ShapeDtypeStruct((T, D_A + D_B), BF16), jax.ShapeDtypeStruct((T, D_B), F32)],
        scratch_shapes=[pltpu.VMEM((halo + tm, D_B), F32)],
        compiler_params=_cparams("parallel"),
    )(z, z, z, a_ln_g, a_ln_b, w_s, b_s, conv_w, conv_b, b_ln_g, b_ln_b)


def _mixer_ab_bwd_pre(z, cb, dy, a_ln_g, a_ln_b, w_s, b_s, b_ln_g, b_ln_b, *, tm, name, comm=None):
    T = z.shape[0]
    nchunk = tm // CHUNK
    tn_dims = (((0,), (0,)), ((), ()))
    nt_dims = (((1,), (1,)), ((), ()))

    def body(za_ref, cb_ref, dy_ref, alg_ref, alb_ref, ws_ref, bs_ref, blg_ref, blb_ref,
             dza_ref, dcb_ref, dalg_ref, dalb_ref, dws_ref, dbs_ref, dblg_ref, dblb_ref,
             dlv_ref):
        @pl.when(pl.program_id(0) == 0)
        def _():
            for ref in (dalg_ref, dalb_ref, dws_ref, dbs_ref, dblg_ref, dblb_ref):
                ref[...] = jnp.zeros_like(ref)
        ua = za_ref[:, :D_A].astype(F32)
        va = za_ref[:, D_A:].astype(F32)
        gu = _gelu(ua)
        gv = _gelu(va)
        xh, r = _ln_stats(gv)
        alg = alg_ref[...]
        lv = (xh * alg + alb_ref[...]).astype(BF16)
        dya = dy_ref[:, :D_A].astype(F32)
        mask = _causal_mask()
        for h in range(A_HEADS):
            wm = jnp.where(mask, ws_ref[h], 0.0).astype(BF16)
            cols = slice(h * HEAD_DIM, (h + 1) * HEAD_DIM)
            dwm = jnp.zeros((CHUNK, CHUNK), F32)
            dbs = jnp.zeros((CHUNK, 1), F32)
            for c in range(nchunk):
                rows = slice(c * CHUNK, (c + 1) * CHUNK)
                lvb = lv[rows, cols]
                mixed = jnp.dot(wm, lvb, preferred_element_type=F32) + bs_ref[h]
                dyb = dya[rows, cols]
                dza_ref[rows, cols] = (dyb * mixed * _dgelu(ua[rows, cols])).astype(BF16)
                dmixed = dyb * gu[rows, cols]
                dmb = dmixed.astype(BF16)
                dlv_ref[rows, cols] = lax.dot_general(wm, dmb, tn_dims, preferred_element_type=F32)
                dwm = dwm + lax.dot_general(dmb, lvb, nt_dims, preferred_element_type=F32)
                dbs = dbs + jnp.sum(dmixed, axis=1, keepdims=True)
            dws_ref[h] += jnp.where(mask, dwm, 0.0)
            dbs_ref[h] += dbs
        dlv = dlv_ref[...]
        dalg_ref[...] += _rowsum(dlv * xh)
        dalb_ref[...] += _rowsum(dlv)
        dgv = _ln_bwd(dlv, xh, r, alg)
        dza_ref[:, D_A:] = (dgv * _dgelu(va)).astype(BF16)
        xhb, rb = _ln_stats(cb_ref[...])
        blg = blg_ref[...]
        lb = xhb * blg + blb_ref[...]
        dlb = dy_ref[:, D_A:].astype(F32) * _dsilu(lb)
        dblg_ref[...] += _rowsum(dlb * xhb)
        dblb_ref[...] += _rowsum(dlb)
        dcb_ref[...] = _ln_bwd(dlb, xhb, rb, blg)

    row = lambda i: (i, 0)
    par = lambda i: (0, 0)
    par3 = lambda i: (0, 0, 0)
    return _pallas(
        comm, body, name=name, grid=(T // tm,),
        in_specs=[pl.BlockSpec((tm, 2 * D_A), row), pl.BlockSpec((tm, D_B), row),
                  pl.BlockSpec((tm, D_A + D_B), row),
                  pl.BlockSpec((1, D_A), par), pl.BlockSpec((1, D_A), par),
                  pl.BlockSpec((A_HEADS, CHUNK, CHUNK), par3),
                  pl.BlockSpec((A_HEADS, CHUNK, 1), par3),
                  pl.BlockSpec((1, D_B), par), pl.BlockSpec((1, D_B), par)],
        out_specs=[pl.BlockSpec((tm, 2 * D_A), row), pl.BlockSpec((tm, D_B), row),
                   pl.BlockSpec((1, D_A), par), pl.BlockSpec((1, D_A), par),
                   pl.BlockSpec((A_HEADS, CHUNK, CHUNK), par3),
                   pl.BlockSpec((A_HEADS, CHUNK, 1), par3),
                   pl.BlockSpec((1, D_B), par), pl.BlockSpec((1, D_B), par)],
        out_shape=[jax.ShapeDtypeStruct((T, 2 * D_A + 2 * D_B), BF16), jax.ShapeDtypeStruct((T, D_B), F32),
                   jax.ShapeDtypeStruct((1, D_A), F32), jax.ShapeDtypeStruct((1, D_A), F32),
                   jax.ShapeDtypeStruct((A_HEADS, CHUNK, CHUNK), F32),
                   jax.ShapeDtypeStruct((A_HEADS, CHUNK, 1), F32),
                   jax.ShapeDtypeStruct((1, D_B), F32), jax.ShapeDtypeStruct((1, D_B), F32)],
        scratch_shapes=[pltpu.VMEM((tm, D_A), F32)],
        compiler_params=_cparams("arbitrary"),
    )(z, cb, dy, a_ln_g, a_ln_b, w_s, b_s, b_ln_g, b_ln_b)


def _mixer_b_conv_bwd(z, dcb, conv_w, dz, *, tm, name, comm=None):
    T = z.shape[0]
    halo = HALO_LONG

    def body(zb_ref, dcb_ref, dcn_ref, cw_ref, dz_in_ref, dzb_ref, dcw_ref, dbias_ref, dext_ref):
        i = pl.program_id(0)
        last = pl.num_programs(0) - 1

        @pl.when(i == 0)
        def _():
            dcw_ref[...] = jnp.zeros_like(dcw_ref)
            dbias_ref[...] = jnp.zeros_like(dbias_ref)
        dcb = dcb_ref[...]
        dext_ref[0:tm, :] = dcb
        dext_ref[tm:tm + halo, :] = jnp.where(i < last, dcn_ref[...], 0.0)
        dbias_ref[...] += _rowsum(dcb)
        for rb in range(tm // CONV_ROWS):
            for cb in range(D_B // CONV_COLS):
                cs = slice(cb * CONV_COLS, (cb + 1) * CONV_COLS)
                gcs = slice(D_B + cb * CONV_COLS, D_B + (cb + 1) * CONV_COLS)
                rs = slice(rb * CONV_ROWS, (rb + 1) * CONV_ROWS)
                xbb = zb_ref[rs, cs].astype(F32)
                sgb = _sigmoid(zb_ref[rs, gcs].astype(F32))
                yb0 = xbb * sgb
                window = dext_ref[rb * CONV_ROWS:rb * CONV_ROWS + CONV_ROWS + halo, cs]
                acc = jnp.zeros((CONV_ROWS, CONV_COLS), F32)
                for k in range(B_CONV):
                    shifted = _rows_after(window, (B_CONV - 1) - k)[:CONV_ROWS]
                    acc = acc + cw_ref[k:k + 1, cs] * shifted
                    dcw_ref[k:k + 1, cs] += _rowsum(shifted * yb0)
                dzb_ref[rs, cs] = (acc * sgb).astype(BF16)
                dzb_ref[rs, gcs] = (acc * xbb * sgb * (1.0 - sgb)).astype(BF16)

    row = lambda i: (i, 0)
    par = lambda i: (0, 0)
    return _pallas(
        comm, body, name=name, grid=(T // tm,),
        in_specs=[pl.BlockSpec((tm, 2 * D_B), lambda i: (i, 1)),
                  pl.BlockSpec((tm, D_B), row),
                  pl.BlockSpec((halo, D_B), lambda i: (_halo_next_index(tm, halo, T)(i), 0)),
                  pl.BlockSpec((B_CONV, D_B), par), pl.BlockSpec(memory_space=pl.ANY)],
        out_specs=[pl.BlockSpec((tm, 2 * D_B), lambda i: (i, 1)), pl.BlockSpec((B_CONV, D_B), par),
                   pl.BlockSpec((1, D_B), par)],
        out_shape=[jax.ShapeDtypeStruct(dz.shape, BF16), jax.ShapeDtypeStruct((B_CONV, D_B), F32),
                   jax.ShapeDtypeStruct((1, D_B), F32)],
        scratch_shapes=[pltpu.VMEM((tm + halo, D_B), F32)], aliases={4: 0},
        compiler_params=_cparams("arbitrary"),
    )(z, dcb, dcb, conv_w, dz)


def _rows_before(x, a):
    return x if a == 0 else pltpu.roll(x, a, axis=0)


def _rows_after(x, a):
    return x if a == 0 else pltpu.roll(x, x.shape[0] - a, axis=0)


def _conv3(w_ref, x, halo, cs):
    acc = w_ref[2:3, cs] * x[halo:]
    acc = acc + w_ref[1:2, cs] * _rows_before(x, 1)[halo:]
    return acc + w_ref[0:1, cs] * _rows_before(x, 2)[halo:]


def _mixer_c_fwd(z, conv_w, *, tm, name, comm=None):
    T = z.shape[0]
    D = D_MODEL
    halo = HALO_SHORT
    W = CONV_COLS

    def body(bg_ref, cg_ref, xv_ref, cgh_ref, xvh_ref, w_ref, r_ref):
        i = pl.program_id(0)
        for cb in range(D // W):
            cs = slice(cb * W, (cb + 1) * W)
            prev = jnp.where(i > 0, cgh_ref[:, cs].astype(F32) * xvh_ref[:, cs].astype(F32), 0.0)
            p = jnp.concatenate([prev, cg_ref[:, cs].astype(F32) * xv_ref[:, cs].astype(F32)], axis=0)
            r_ref[:, cs] = (bg_ref[:, cs].astype(F32) * _conv3(w_ref, p, halo, cs)).astype(BF16)

    hp = _halo_prev_index(tm, halo)
    return _pallas(
        comm, body, name=name, grid=(T // tm,),
        in_specs=[pl.BlockSpec((tm, D), lambda i: (i, 0)), pl.BlockSpec((tm, D), lambda i: (i, 1)),
                  pl.BlockSpec((tm, D), lambda i: (i, 2)),
                  pl.BlockSpec((halo, D), lambda i: (hp(i), 1)),
                  pl.BlockSpec((halo, D), lambda i: (hp(i), 2)),
                  pl.BlockSpec((None, C_CONV, D), lambda i: (0, 0, 0))],
        out_specs=pl.BlockSpec((tm, D), lambda i: (i, 0)),
        out_shape=jax.ShapeDtypeStruct((T, D), BF16),
        compiler_params=_cparams("parallel"),
    )(z, z, z, z, z, conv_w)


def _mixer_c_bwd(z, dr, conv_w, *, tm, name, comm=None):
    T = z.shape[0]
    D = D_MODEL
    halo = HALO_SHORT
    W = CONV_COLS

    def body(bg_ref, cg_ref, xv_ref, cgh_ref, xvh_ref, bgn_ref, dr_ref, drn_ref, w_ref, dz_ref, dw_ref):
        i = pl.program_id(0)
        last = pl.num_programs(0) - 1

        @pl.when(i == 0)
        def _():
            dw_ref[...] = jnp.zeros_like(dw_ref)
        for cb in range(D // W):
            cs = slice(cb * W, (cb + 1) * W)
            cg = cg_ref[:, cs].astype(F32)
            xv = xv_ref[:, cs].astype(F32)
            dr = dr_ref[:, cs].astype(F32)
            p = cg * xv
            prev = jnp.where(i > 0, cgh_ref[:, cs].astype(F32) * xvh_ref[:, cs].astype(F32), 0.0)
            q = _conv3(w_ref, jnp.concatenate([prev, p], axis=0), halo, cs)
            dz_ref[:, cs] = (dr * q).astype(BF16)
            nxt = jnp.where(i < last, drn_ref[:, cs].astype(F32) * bgn_ref[:, cs].astype(F32), 0.0)
            dq = jnp.concatenate([dr * bg_ref[:, cs].astype(F32), nxt], axis=0)
            dp = None
            for k in range(C_CONV):
                shifted = _rows_after(dq, 2 - k)[:tm]
                term = w_ref[k:k + 1, cs] * shifted
                dp = term if dp is None else dp + term
                dw_ref[k:k + 1, cs] += _rowsum(shifted * p)
            dz_ref[:, D + cb * W:D + (cb + 1) * W] = (dp * xv).astype(BF16)
            dz_ref[:, 2 * D + cb * W:2 * D + (cb + 1) * W] = (dp * cg).astype(BF16)

    hp = _halo_prev_index(tm, halo)
    hn = _halo_next_index(tm, halo, T)
    return _pallas(
        comm, body, name=name, grid=(T // tm,),
        in_specs=[pl.BlockSpec((tm, D), lambda i: (i, 0)), pl.BlockSpec((tm, D), lambda i: (i, 1)),
                  pl.BlockSpec((tm, D), lambda i: (i, 2)),
                  pl.BlockSpec((halo, D), lambda i: (hp(i), 1)),
                  pl.BlockSpec((halo, D), lambda i: (hp(i), 2)),
                  pl.BlockSpec((halo, D), lambda i: (hn(i), 0)),
                  pl.BlockSpec((tm, D), lambda i: (i, 0)),
                  pl.BlockSpec((halo, D), lambda i: (hn(i), 0)),
                  pl.BlockSpec((None, C_CONV, D), lambda i: (0, 0, 0))],
        out_specs=[pl.BlockSpec((tm, 3 * D), lambda i: (i, 0)),
                   pl.BlockSpec((C_CONV, D), lambda i: (0, 0))],
        out_shape=[jax.ShapeDtypeStruct((T, 3 * D), BF16), jax.ShapeDtypeStruct((C_CONV, D), F32)],
        compiler_params=_cparams("arbitrary"),
    )(z, z, z, z, z, z, dr, dr, conv_w)


FFN_COLS = 128


def _ffn_act_fwd(up, conv_w, *, layer, tm, name, comm=None):
    T = up.shape[0]
    halo = HALO_SHORT
    W = FFN_COLS

    def body(up_ref, uph_ref, w_ref, a_ref, upc_ref):
        i = pl.program_id(0)

        def conv(cs):
            prev = jnp.where(i > 0, uph_ref[:, cs], jnp.zeros((halo, W), BF16))
            return _conv3(w_ref, jnp.concatenate([prev, up_ref[:, cs]], axis=0).astype(F32), halo, cs)

        for cb in range(D_FF // W):
            gs = slice(cb * W, (cb + 1) * W)
            vs = slice(D_FF + cb * W, D_FF + (cb + 1) * W)
            g = conv(gs)
            v = conv(vs)
            upc_ref[:, gs] = g.astype(BF16)
            upc_ref[:, vs] = v.astype(BF16)
            a_ref[:, gs] = (_silu(g) * v).astype(BF16)

    return _pallas(
        comm, body, name=name, grid=(T // tm,),
        in_specs=[pl.BlockSpec((tm, 2 * D_FF), lambda i: (i, 0)),
                  pl.BlockSpec((halo, 2 * D_FF), lambda i: (_halo_prev_index(tm, halo)(i), 0)),
                  pl.BlockSpec((None, F_CONV, 2 * D_FF), lambda i: (layer, 0, 0))],
        out_specs=[pl.BlockSpec((tm, D_FF), lambda i: (i, 0)),
                   pl.BlockSpec((tm, 2 * D_FF), lambda i: (i, 0))],
        out_shape=[jax.ShapeDtypeStruct((T, D_FF), BF16), jax.ShapeDtypeStruct((T, 2 * D_FF), BF16)],
        compiler_params=_cparams("parallel"),
    )(up, up, conv_w)


def _ffn_act_bwd(up, upc, da, conv_w, *, layer, tm, name, comm=None):
    T = up.shape[0]
    halo = HALO_SHORT
    W = FFN_COLS

    def body(up_ref, upc_ref, upcn_ref, da_ref, dan_ref, w_ref, dup_ref, dw_ref):
        i = pl.program_id(0)
        last = pl.num_programs(0) - 1

        @pl.when(i == 0)
        def _():
            dw_ref[...] = jnp.zeros_like(dw_ref)
        live = jnp.where(i < last, 1.0, 0.0)
        for cb in range(D_FF // W):
            gs = slice(cb * W, (cb + 1) * W)
            vs = slice(D_FF + cb * W, D_FF + (cb + 1) * W)
            g = jnp.concatenate([upc_ref[:, gs], upcn_ref[:, gs]], axis=0).astype(F32)
            v = jnp.concatenate([upc_ref[:, vs], upcn_ref[:, vs]], axis=0).astype(F32)
            da = jnp.concatenate([da_ref[:, gs].astype(F32), dan_ref[:, gs].astype(F32) * live], axis=0)
            s = _sigmoid(g)
            silu = g * s
            grads = (da * v * (s * (1.0 + g * (1.0 - s))), da * silu)
            for cs, d in zip((gs, vs), grads):
                u = up_ref[:, cs].astype(F32)
                acc = None
                for k in range(F_CONV):
                    shifted = _rows_after(d, 2 - k)[:tm]
                    term = w_ref[k:k + 1, cs] * shifted
                    acc = term if acc is None else acc + term
                    dw_ref[k:k + 1, cs] += _rowsum(shifted * u)
                dup_ref[:, cs] = acc.astype(BF16)

    hn = _halo_next_index(tm, halo, T)
    return _pallas(
        comm, body, name=name, grid=(T // tm,),
        in_specs=[pl.BlockSpec((tm, 2 * D_FF), lambda i: (i, 0)),
                  pl.BlockSpec((tm, 2 * D_FF), lambda i: (i, 0)),
                  pl.BlockSpec((halo, 2 * D_FF), lambda i: (hn(i), 0)),
                  pl.BlockSpec((tm, D_FF), lambda i: (i, 0)),
                  pl.BlockSpec((halo, D_FF), lambda i: (hn(i), 0)),
                  pl.BlockSpec((None, F_CONV, 2 * D_FF), lambda i: (layer, 0, 0))],
        out_specs=[pl.BlockSpec((tm, 2 * D_FF), lambda i: (i, 0)),
                   pl.BlockSpec((F_CONV, 2 * D_FF), lambda i: (0, 0))],
        out_shape=[jax.ShapeDtypeStruct((T, 2 * D_FF), BF16),
                   jax.ShapeDtypeStruct((F_CONV, 2 * D_FF), F32)],
        compiler_params=_cparams("arbitrary"),
    )(up, upc, upc, da, da, conv_w)


def _local_step(x, tgt, small, plan):
    T = x.shape[0]
    tm_e = _pick(T, 256)
    tm_a = _pick(T, 512)
    tm_b = _pick(T, 128)
    tm = _pick(T, 1024)
    tm_f = _pick(T, 512)
    tt = _pick(T, 2048)
    nm = small["norm_mix"].reshape(2, 1, D_MODEL)
    nf = small["norm_ffn"].reshape(2, 1, D_MODEL)
    ngf = small["norm_final"].reshape(1, D_MODEL)
    b_s = small["a_b_s"].reshape(A_HEADS, CHUNK, 1)
    w_s = small["a_w_s"].reshape(A_HEADS, CHUNK, CHUNK)
    b_conv_w = small["b_conv_w"].reshape(B_CONV, D_B)
    sg = {}
    wt, cm = plan.weight, plan.comm

    h_m0, z_ab = _norm_mm_nn(x, nm, wt("ab_w_in", 0), g_layer=0, tm=tm, tn=512, name="ab_in", comm=cm("ab_in"))
    yab, cb = _mixer_ab_fwd(z_ab, small["a_ln_g"], small["a_ln_b"], w_s, b_s, b_conv_w, small["b_conv_b"],
                            small["b_ln_g"], small["b_ln_b"], tm=tm_e, name="mixer_ab", comm=cm("mixer_ab"))
    x1, h_f0 = _mm_nn(yab, wt("ab_w_out", 0), layer=0, tm=tm, tn=D_MODEL, residual=x, norm=(nf, 0),
                      name="ab_out", comm=cm("ab_out"))

    def ffn_fwd(xin, h, layer):
        up = _mm_nn(h, wt("f_w_up", layer), layer=0, tm=tm, tn=2 * 1408, out_dtype=BF16, name=f"ffn_up{layer}",
                    comm=cm(f"ffn_up{layer}"))
        a, upc = _ffn_act_fwd(up, small["f_conv_w"], layer=layer, tm=tm_a, name=f"ffn_act{layer}",
                              comm=cm(f"ffn_act{layer}"))
        if layer == 0:
            out = _mm_nn(a, wt("f_w_down", layer), layer=0, tm=tm, tn=D_MODEL, residual=xin, norm=(nm, 1),
                         name=f"ffn_down{layer}", comm=cm(f"ffn_down{layer}"))
        else:
            out = _mm_nn_loss(a, wt("f_w_down", layer), xin, tgt, ngf, tm=tm, name=f"ffn_down{layer}",
                              comm=cm(f"ffn_down{layer}"))
        return up, upc, a, out

    up0, upc0, a0, (x2, h_m1) = ffn_fwd(x1, h_f0, 0)
    z_c = _mm_nn(h_m1, wt("c_w_in", 0), layer=0, tm=tm, tn=768, out_dtype=BF16, name="c_in", comm=cm("c_in"))
    r = _mixer_c_fwd(z_c, small["c_conv_w"], tm=tm_e, name="mixer_c", comm=cm("mixer_c"))
    x3, h_f1 = _mm_nn(r, wt("c_w_out", 0), layer=0, tm=tm, tn=D_MODEL, residual=x2, norm=(nf, 1),
                      name="c_out", comm=cm("c_out"))
    up1, upc1, a1, (loss, dx, sg["norm_final"]) = ffn_fwd(x3, h_f1, 1)

    def ffn_bwd(dx, xin, h, up, upc, a, layer):
        da = _mm_nt(dx, wt("f_w_down", layer), layer=0, tm=tm, tn=1408, out_dtype=BF16,
                    name=f"ffn_down_dx{layer}", comm=cm(f"ffn_down_dx{layer}"))
        plan.grad_ready("f_w_down", layer, _mm_tn(a, dx, shards=None, tk=1408, tn=1024, tt=tt,
                                                  name=f"ffn_down_dw{layer}", comm=cm(f"ffn_down_dw{layer}")))
        dup, dcw = _ffn_act_bwd(up, upc, da, small["f_conv_w"], layer=layer, tm=tm_b, name=f"ffn_act_bwd{layer}",
                                comm=cm(f"ffn_act_bwd{layer}"))
        dxin, dg = _mm_nt_norm(dup, wt("f_w_up", layer), xin, nf, dx, g_layer=layer, tm=tm_f,
                               name=f"ffn_up_dx{layer}", comm=cm(f"ffn_up_dx{layer}"))
        plan.grad_ready("f_w_up", layer, _mm_tn(h, dup, shards=N_CHIPS, tk=512, tn=2 * 1408, tt=tt,
                                                name=f"ffn_up_dw{layer}", comm=cm(f"ffn_up_dw{layer}")))
        return dxin, dg, dcw

    dx, dnf1, dfc1 = ffn_bwd(dx, x3, h_f1, up1, upc1, a1, 1)
    dr = _mm_nt(dx, wt("c_w_out", 0), layer=0, tm=tm, tn=512, out_dtype=BF16, name="c_out_dx", comm=cm("c_out_dx"))
    plan.grad_ready("c_w_out", 0, _mm_tn(r, dx, shards=None, tk=1024, tn=1024, tt=tt, name="c_out_dw",
                                         comm=cm("c_out_dw")))
    dz_c, dccw = _mixer_c_bwd(z_c, dr, small["c_conv_w"], tm=tm_e, name="mixer_c_bwd", comm=cm("mixer_c_bwd"))
    sg["c_conv_w"] = dccw.reshape(1, C_CONV, D_MODEL)
    plan.grad_ready("c_w_in", 0, _mm_tn(h_m1, dz_c, shards=N_CHIPS, tk=1024, tn=768, tt=tt, name="c_in_dw",
                                        comm=cm("c_in_dw")))
    dx, dnm1 = _mm_nt_norm(dz_c, wt("c_w_in", 0), x2, nm, dx, g_layer=1, tm=tm_f, name="c_in_dx",
                           comm=cm("c_in_dx"))
    dx, dnf0, dfc0 = ffn_bwd(dx, x1, h_f0, up0, upc0, a0, 0)
    dyab = _mm_nt(dx, wt("ab_w_out", 0), layer=0, tm=tm, tn=512, out_dtype=BF16, name="ab_out_dx",
                  comm=cm("ab_out_dx"))
    plan.grad_ready("ab_w_out", 0, _mm_tn(yab, dx, shards=None, tk=1024, tn=1024, tt=tt, name="ab_out_dw",
                                          comm=cm("ab_out_dw")))
    (dza, dcb, sg["a_ln_g"], sg["a_ln_b"], dws, dbs, sg["b_ln_g"], sg["b_ln_b"]) = _mixer_ab_bwd_pre(
        z_ab, cb, dyab, small["a_ln_g"], small["a_ln_b"], w_s, b_s, small["b_ln_g"], small["b_ln_b"],
        tm=tm_e, name="mixer_ab_bwd", comm=cm("mixer_ab_bwd"))
    dz_ab, dbcw, sg["b_conv_b"] = _mixer_b_conv_bwd(z_ab, dcb, b_conv_w, dza, tm=tm_e, name="mixer_b_conv_bwd",
                                                    comm=cm("mixer_b_conv_bwd"))
    sg["a_w_s"] = dws.reshape(1, A_HEADS, CHUNK, CHUNK)
    sg["a_b_s"] = dbs.reshape(1, A_HEADS, CHUNK)
    sg["b_conv_w"] = dbcw.reshape(1, B_CONV, D_B)
    plan.grad_ready("ab_w_in", 0, _mm_tn(h_m0, dz_ab, shards=N_CHIPS, tk=1024, tn=512, tt=tt, name="ab_in_dw",
                                         comm=cm("ab_in_dw")))
    dx, dnm0 = _mm_nt_norm(dz_ab, wt("ab_w_in", 0), x, nm, dx, g_layer=0, tm=tm_f, name="ab_in_dx",
                           comm=cm("ab_in_dx"))

    sg["norm_mix"] = [dnm0, dnm1]
    sg["norm_ffn"] = [dnf0, dnf1]
    sg["f_conv_w"] = [dfc0, dfc1]
    return loss, dx, sg


BLOCK_BYTES = 3 * 1024 * 1024


BF16_SUBLANES = 16


def _row_tile(rows, row_bytes, step=SUBLANES):
    best = None
    for tr in range(step, rows + 1, step):
        if rows % tr == 0 and tr * row_bytes <= BLOCK_BYTES:
            best = tr
    if best is None:
        raise ValueError(f"no row tile for {rows}")
    return best


def _place_scalars():
    x, y, c = lax.axis_index("x"), lax.axis_index("y"), lax.axis_index("c")
    return jnp.stack([c, 2 * x + y, 2 * (1 - x) + y, 2 * x + (1 - y), 2 * (1 - x) + (1 - y)]).astype(jnp.int32)


def _cast_into_slot(w, place, *, layer, paired, name):
    L, rows, cols = w.shape
    tr = _row_tile(rows, cols * 4, BF16_SUBLANES)

    def body(place_ref, w_ref, o_ref):
        o_ref[...] = w_ref[...].astype(BF16)

    if paired:
        out_spec = pl.BlockSpec((None, None, tr, cols), lambda i, p: (0, p[1] // 2, i, p[1] % 2))
        out_shape = jax.ShapeDtypeStruct((1, N_CHIPS // 2, rows, 2 * cols), BF16)
    else:
        out_spec = pl.BlockSpec((None, None, tr, cols), lambda i, p: (0, p[1], i, 0))
        out_shape = jax.ShapeDtypeStruct((1, N_CHIPS, rows, cols), BF16)
    return pl.pallas_call(
        body, name=name,
        grid_spec=pltpu.PrefetchScalarGridSpec(
            num_scalar_prefetch=1, grid=(rows // tr,),
            in_specs=[pl.BlockSpec((None, tr, cols), lambda i, p: (layer, i, 0))],
            out_specs=out_spec),
        out_shape=out_shape,
        compiler_params=_cparams("parallel"),
    )(place, w)


def _pair_sum(g, theirs, place, *, name):
    S, rows, cols = g.shape
    half = rows // 2
    tr = _row_tile(half, cols * 4, BF16_SUBLANES)
    nb = half // tr

    def body(place_ref, g_ref, t_ref, o_ref):
        o_ref[...] = (g_ref[...] + t_ref[...]).astype(BF16)

    spec = pl.BlockSpec((None, tr, cols), lambda s, i, p: (s, i, 0))
    return pl.pallas_call(
        body, name=name,
        grid_spec=pltpu.PrefetchScalarGridSpec(
            num_scalar_prefetch=1, grid=(S, nb),
            in_specs=[pl.BlockSpec((None, tr, cols), lambda s, i, p: (s, p[0] * nb + i, 0)), spec],
            out_specs=spec),
        out_shape=jax.ShapeDtypeStruct((S, half, cols), BF16),
        compiler_params=_cparams("parallel", "parallel"),
    )(place, g, theirs)


def _chip_sum(p, r, g_prev, place, *, layer, shape, name):
    L, rows, cols = shape
    half = rows // 2
    tr = _row_tile(half, cols * 4, BF16_SUBLANES)
    nb = half // tr

    def body(place_ref, p_ref, r_ref, *rest):
        o_ref = rest[-1]
        mine = p_ref[...].astype(F32)
        peers = [r_ref[j].astype(F32) for j in range(3)]
        acc = None
        for s in range(N_CHIPS):
            term = jnp.where(place_ref[1] == s, mine,
                             jnp.where(place_ref[2] == s, peers[0],
                                       jnp.where(place_ref[3] == s, peers[1], peers[2])))
            acc = term if acc is None else acc + term
        o_ref[...] = acc

    in_specs = [pl.BlockSpec((None, tr, cols), lambda i, pr: (pr[1], i, 0)),
                pl.BlockSpec((3, tr, cols), lambda i, pr: (0, i, 0))]
    args = [place, p, r]
    aliases = {}
    if g_prev is not None:
        in_specs.append(HBM_REF)
        args.append(g_prev)
        aliases = {3: 0}
    return pl.pallas_call(
        body, name=name,
        grid_spec=pltpu.PrefetchScalarGridSpec(
            num_scalar_prefetch=1, grid=(nb,), in_specs=in_specs,
            out_specs=pl.BlockSpec((None, tr, cols), lambda i, pr: (layer, pr[0] * nb + i, 0))),
        out_shape=jax.ShapeDtypeStruct(shape, F32), input_output_aliases=aliases,
        compiler_params=_cparams("parallel"),
    )(*args)


def _adamw_math(w, g, m, v):
    m2 = ADAM_B1 * m + (1.0 - ADAM_B1) * g
    v2 = ADAM_B2 * v + (1.0 - ADAM_B2) * (g * g)
    m_hat = m2 / (1.0 - ADAM_B1 ** ADAM_STEP)
    v_hat = v2 / (1.0 - ADAM_B2 ** ADAM_STEP)
    delta = -ADAM_LR * (m_hat / (jnp.sqrt(v_hat) + ADAM_EPS) + ADAM_WD * w)
    return delta, m2, v2


def _adamw(w, g, m, v, *, name):
    L, rows, cols = w.shape
    tr = _row_tile(rows, cols * 4)

    def body(w_ref, g_ref, m_ref, v_ref, d_ref, m2_ref, v2_ref):
        d, m2, v2 = _adamw_math(w_ref[...], g_ref[...], m_ref[...], v_ref[...])
        d_ref[...] = d
        m2_ref[...] = m2
        v2_ref[...] = v2

    spec = pl.BlockSpec((None, tr, cols), lambda l, i: (l, i, 0))
    shape = jax.ShapeDtypeStruct(w.shape, F32)
    return pl.pallas_call(
        body, name=name, grid=(L, rows // tr), in_specs=[spec] * 4, out_specs=[spec] * 3,
        out_shape=[shape] * 3,
        compiler_params=_cparams("parallel", "parallel"),
    )(w, g, m, v)


def _allreduce_pack(pack, *, name, comm):
    R = pack.shape[0]
    half = R // 2
    nr, nw = len(comm.reads), len(comm.writes)

    def body(*refs):
        p_ref, rd, wr_in = refs[0], refs[1:1 + nr], refs[1 + nr:1 + nr + nw]
        o_ref, wr_out = refs[1 + nr + nw], refs[2 + nr + nw:2 + nr + 2 * nw]
        sib_ref, chip_ref, parts_ref, sems, comm_sems = refs[2 + nr + 2 * nw:]
        src = dict(zip(comm.reads, rd))
        src.update(zip(comm.writes, wr_in))
        dst = dict(zip(comm.writes, wr_out))
        comm.start(src, dst, comm_sems)
        x, y, c, k, sib, peers = _place()
        swap = _remote(p_ref, sib_ref, sems.at[0, 0], sems.at[0, 1], sib)
        swap.start()
        swap.wait()
        chip_ref[...] = p_ref[...] + sib_ref[...]
        mine = chip_ref.at[pl.ds(pl.multiple_of(c * half, SUBLANES), half)]
        sends = [_remote(mine, parts_ref.at[j], sems.at[1 + j, 0], sems.at[1 + j, 1], (px, py, c))
                 for j, (px, py) in enumerate(peers)]
        for rc in sends:
            rc.start()
        for rc in sends:
            rc.wait()
        own = mine[...]
        others = [parts_ref[j] for j in range(3)]
        acc = None
        for s in range(N_CHIPS):
            term = own
            for j, (px, py) in enumerate(peers):
                term = jnp.where(2 * px + py == s, others[j], term)
            acc = term if acc is None else acc + term
        done = o_ref.at[pl.ds(pl.multiple_of(c * half, SUBLANES), half)]
        done[...] = acc
        theirs = o_ref.at[pl.ds(pl.multiple_of((1 - c) * half, SUBLANES), half)]
        share = _remote(done, done, sems.at[4, 0], sems.at[4, 1], sib)
        share.start()
        _remote(done, theirs, sems.at[4, 0], sems.at[4, 1], sib).wait()
        comm.finish(src, dst, comm_sems)

    vm = pl.BlockSpec(memory_space=pltpu.VMEM)
    operands, shapes = _comm_operands(comm)
    outs = pl.pallas_call(
        body, name=name, in_specs=[vm] + [HBM_REF] * (nr + nw), out_specs=[vm] + [HBM_REF] * nw,
        out_shape=[jax.ShapeDtypeStruct((R, LANES), F32)] + shapes,
        input_output_aliases={1 + nr + q: 1 + q for q in range(nw)},
        scratch_shapes=[pltpu.VMEM((R, LANES), F32), pltpu.VMEM((R, LANES), F32),
                        pltpu.VMEM((3, half, LANES), F32), pltpu.SemaphoreType.DMA((5, 2)),
                        pltpu.SemaphoreType.DMA((comm.ncopies, 2))],
        compiler_params=pltpu.CompilerParams(vmem_limit_bytes=VMEM_BYTES_MAX),
    )(pack, *operands)
    for q, n in enumerate(comm.writes):
        comm.plan.bufs[n] = outs[1 + q]
    return outs[0]


PACK_UNIT = SUBLANES * LANES


def _pack(arrays):
    flat, sizes = [], []
    for a in arrays:
        pieces = a if isinstance(a, (list, tuple)) else [a]
        v = jnp.concatenate([p.reshape(-1) for p in pieces]) if len(pieces) > 1 else pieces[0].reshape(-1)
        size = v.shape[0]
        padded = -(-size // PACK_UNIT) * PACK_UNIT
        flat.append(jnp.pad(v, (0, padded - size)))
        sizes.append((size, padded))
    total = sum(p for _, p in sizes)
    if (total // PACK_UNIT) % 2:
        flat.append(jnp.zeros((PACK_UNIT,), F32))
    return jnp.concatenate(flat).reshape(-1, LANES), sizes


def _unpack(pack, sizes, shapes):
    v = pack.reshape(-1)
    out, off = [], 0
    for (size, padded), shape in zip(sizes, shapes):
        out.append(v[off:off + size].reshape(shape))
        off += padded
    return out


BIG = ("ab_w_in", "ab_w_out", "c_w_in", "c_w_out", "f_w_up", "f_w_down")
COL_SHARDED = ("ab_w_in", "c_w_in", "f_w_up")
PAIRED = ("f_w_up",)
SMALL_REPLICATED = ("norm_mix", "norm_ffn", "norm_final", "a_ln_g", "a_ln_b", "a_w_s", "a_b_s",
                    "b_conv_b", "b_ln_g", "b_ln_b")
SMALL_SHARDED = ("b_conv_w", "c_conv_w", "f_conv_w")
SMALL = SMALL_REPLICATED + SMALL_SHARDED
ALL_WEIGHTS = ("norm_mix", "norm_ffn", "norm_final", "ab_w_in", "a_ln_g", "a_ln_b", "a_w_s", "a_b_s",
               "b_conv_w", "b_conv_b", "b_ln_g", "b_ln_b", "ab_w_out", "c_w_in", "c_conv_w", "c_w_out",
               "f_w_up", "f_conv_w", "f_w_down")


SCHEDULE = {
    "ab_in": [("gi", "f_w_up", 0, 0, 4), ("gi", "ab_w_out", 0)],
    "mixer_ab": [("gd", "f_w_up", 0, 0, 4), ("gd", "ab_w_out", 0), ("gi", "f_w_up", 0, 1, 4),
                 ("gi", "f_w_up", 0, 2, 4), ("gi", "f_w_up", 0, 3, 4)],
    "ab_out": [("gd", "f_w_up", 0, 1, 4), ("gd", "f_w_up", 0, 2, 4), ("gd", "f_w_up", 0, 3, 4)],
    "ffn_up0": [("gi", "f_w_down", 0), ("gi", "c_w_in", 0, 0, 2)],
    "ffn_act0": [("gd", "f_w_down", 0), ("gd", "c_w_in", 0, 0, 2), ("gi", "c_w_in", 0, 1, 2),
                 ("gi", "f_w_up", 1, 0, 4)],
    "ffn_down0": [("gd", "c_w_in", 0, 1, 2), ("gd", "f_w_up", 1, 0, 4), ("gi", "f_w_up", 1, 1, 4),
                  ("gi", "f_w_up", 1, 2, 4)],
    "c_in": [("gd", "f_w_up", 1, 1, 4), ("gd", "f_w_up", 1, 2, 4), ("gi", "f_w_up", 1, 3, 4),
             ("gi", "c_w_out", 0)],
    "mixer_c": [("gd", "f_w_up", 1, 3, 4), ("gd", "c_w_out", 0), ("gi", "f_w_down", 1, 0, 2)],
    "c_out": [("gd", "f_w_down", 1, 0, 2), ("gi", "f_w_down", 1, 1, 2)],
    "ffn_up1": [("gd", "f_w_down", 1, 1, 2)],
    "ffn_act_bwd1": [("px", "f_w_down", 1)],
    "ffn_up_dx1": [("cx", "f_w_down", 1)],
    "c_out_dx": [("px", "f_w_up", 1, 0, 2)],
    "c_out_dw": [("px", "f_w_up", 1, 1, 2)],
    "mixer_c_bwd": [("cx", "f_w_up", 1, 0, 4), ("px", "c_w_out", 0)],
    "c_in_dw": [("cx", "f_w_up", 1, 1, 4)],
    "c_in_dx": [("cx", "f_w_up", 1, 2, 4), ("px", "c_w_in", 0)],
    "ffn_down_dx0": [("cx", "f_w_up", 1, 3, 4), ("cx", "c_w_out", 0)],
    "ffn_down_dw0": [("cx", "c_w_in", 0, 0, 2)],
    "ffn_act_bwd0": [("cx", "c_w_in", 0, 1, 2), ("px", "f_w_down", 0)],
    "ffn_up_dx0": [("cx", "f_w_down", 0)],
    "ffn_up_dw0": [("ps", "f_w_down", 1), ("ps", "f_w_up", 1)],
    "ab_out_dx": [("px", "f_w_up", 0, 0, 2)],
    "ab_out_dw": [("px", "f_w_up", 0, 1, 2)],
    "mixer_ab_bwd": [("cx", "f_w_up", 0, 0, 4), ("px", "ab_w_out", 0)],
    "mixer_b_conv_bwd": [("cx", "f_w_up", 0, 1, 4), ("cx", "ab_w_out", 0), ("ps", "c_w_out", 0),
                         ("ps", "c_w_in", 0), ("ps", "f_w_down", 0)],
    "ab_in_dw": [("cx", "f_w_up", 0, 2, 4)],
    "ab_in_dx": [("cx", "f_w_up", 0, 3, 4), ("px", "ab_w_in", 0)],
}


class _Plan:
    def __init__(self, shapes, place):
        self.shapes, self.place, self.bufs = shapes, place, {}
        self.summed, self.shared = set(), set()

    def weight(self, name, layer):
        g = self.bufs[f"w:{name}:{layer}"]
        if name in COL_SHARDED:
            return g
        _, S, rows, cols = g.shape
        return g.reshape(1, S * rows, cols)

    def grad_ready(self, name, layer, g):
        _, rows, cols = self.shapes[name]
        hbm = lambda a: pltpu.with_memory_space_constraint(a, pltpu.HBM)
        self.bufs[f"g:{name}:{layer}"] = g.reshape(N_CHIPS, rows, cols)
        self.bufs[f"t:{name}:{layer}"] = hbm(lax.empty((N_CHIPS, rows // 2, cols), F32))
        self.bufs[f"l:{name}:{layer}"] = hbm(lax.empty((3, rows // 2, cols), BF16))

    def job(self, kind, name, layer, part=0, parts=1):
        _, rows, cols = self.shapes[name]
        key = f"{name}:{layer}"
        if kind == "gi":
            return _job_gather_ici("w:" + key, rows, cols, part, parts)
        if kind == "gd":
            return _job_gather_d2d("w:" + key, rows, cols, part, parts)
        if kind == "px":
            return _job_pair_exchange("g:" + key, "t:" + key, rows, part, parts)
        if kind == "cx":
            if "p:" + key not in self.bufs:
                self.bufs["p:" + key] = _pair_sum(self.bufs["g:" + key], self.bufs["t:" + key], self.place,
                                                  name=f"pair_sum_{name}{layer}")
            nr = rows // 2 // parts
            return _job_chip_exchange("p:" + key, "l:" + key, part * nr, nr)
        if kind == "ps":
            self.chip_sum(name, layer)
            self.shared.add(key)
            return _job_pair_share("G:" + name, layer, rows)
        raise ValueError(kind)

    def chip_sum(self, name, layer):
        key = f"{name}:{layer}"
        if key not in self.summed:
            self.summed.add(key)
            self.bufs["G:" + name] = _chip_sum(self.bufs["p:" + key], self.bufs["l:" + key],
                                               self.bufs.get("G:" + name), self.place, layer=layer,
                                               shape=self.shapes[name], name=f"chip_sum_{name}{layer}")

    def comm(self, call):
        specs = SCHEDULE.get(call)
        return None if specs is None else _Comm(self, [self.job(*spec) for spec in specs])


def _step(x, tgt, w, m, v):
    chip = 2 * lax.axis_index("x") + lax.axis_index("y")
    place = _place_scalars()
    plan = _Plan({n: w[n].shape for n in BIG}, place)
    items = [(n, l) for n in BIG for l in range(w[n].shape[0])]

    for n, l in items:
        plan.bufs[f"w:{n}:{l}"] = _cast_into_slot(w[n], place, layer=l, paired=n in PAIRED, name=f"cast_{n}{l}")
    conv_pack, conv_sizes = _pack([w[n] for n in SMALL_SHARDED])
    hbm = lambda a: pltpu.with_memory_space_constraint(a, pltpu.HBM)
    plan.bufs["conv:mine"] = hbm(conv_pack)
    plan.bufs["conv:all"] = hbm(lax.empty((N_CHIPS,) + conv_pack.shape, F32))
    _comm_only(plan, [[plan.job("gi", "ab_w_in", 0), _job_chip_gather("conv:mine", "conv:all")],
                      [plan.job("gd", "ab_w_in", 0)]], name="gather_first")
    conv_shapes = [w[n].shape for n in SMALL_SHARDED]
    per_chip = [_unpack(plan.bufs["conv:all"][s], conv_sizes, conv_shapes) for s in range(N_CHIPS)]
    small = {n: w[n] for n in SMALL_REPLICATED}
    for idx, n in enumerate(SMALL_SHARDED):
        small[n] = jnp.concatenate([jnp.where(chip == s, w[n], per_chip[s][idx]) for s in range(N_CHIPS)], axis=-1)

    loss, dx, sg = _local_step(x, tgt, small, plan)

    g_pack, g_sizes = _pack([sg[n] for n in SMALL] + [loss])
    g_sum = _allreduce_pack(g_pack, name="allreduce_small_grads",
                            comm=_Comm(plan, [plan.job("cx", "ab_w_in", 0)]))
    full_shapes = [small[n].shape for n in SMALL]
    *summed, loss = _unpack(g_sum, g_sizes, full_shapes + [(1, 1)])
    g_small = dict(zip(SMALL, summed))
    for n in SMALL_SHARDED:
        width = w[n].shape[-1]
        g_small[n] = lax.dynamic_slice_in_dim(g_small[n], chip * width, width, axis=g_small[n].ndim - 1)

    _comm_only(plan, [[plan.job("ps", n, l) for n, l in items if f"{n}:{l}" not in plan.shared]],
               name="reduce_pair_share")
    grads_big = [plan.bufs["G:" + n] for n in BIG]

    grad, delta, new_m, new_v = {}, {}, {}, {}
    for n, g in zip(BIG, grads_big):
        grad[n] = g
        delta[n], new_m[n], new_v[n] = _adamw(w[n], g, m[n], v[n], name=f"adamw_{n}")
    shapes = [w[n].shape for n in SMALL]
    wp, sizes = _pack([w[n] for n in SMALL])
    gp, _ = _pack([g_small[n] for n in SMALL])
    mp, _ = _pack([m[n] for n in SMALL])
    vp, _ = _pack([v[n] for n in SMALL])
    R = wp.shape[0]
    dp, m2p, v2p = _adamw(wp.reshape(1, R, LANES), gp.reshape(1, R, LANES), mp.reshape(1, R, LANES),
                          vp.reshape(1, R, LANES), name="adamw_small")
    for n, d_, m_, v_ in zip(SMALL, _unpack(dp, sizes, shapes), _unpack(m2p, sizes, shapes),
                             _unpack(v2p, sizes, shapes)):
        grad[n] = g_small[n]
        delta[n], new_m[n], new_v[n] = d_, m_, v_
    return loss, dx, grad, delta, new_m, new_v


def kernel(x, norm_mix, norm_ffn, norm_final, ab_w_in, a_ln_g, a_ln_b, a_w_s, a_b_s, b_conv_w, b_conv_b, b_ln_g, b_ln_b, ab_w_out, c_w_in, c_conv_w, c_w_out, f_w_up, f_conv_w, f_w_down, loss_target, m_norm_mix, m_norm_ffn, m_norm_final, m_ab_w_in, m_a_ln_g, m_a_ln_b, m_a_w_s, m_a_b_s, m_b_conv_w, m_b_conv_b, m_b_ln_g, m_b_ln_b, m_ab_w_out, m_c_w_in, m_c_conv_w, m_c_w_out, m_f_w_up, m_f_conv_w, m_f_w_down, v_norm_mix, v_norm_ffn, v_norm_final, v_ab_w_in, v_a_ln_g, v_a_ln_b, v_a_w_s, v_a_b_s, v_b_conv_w, v_b_conv_b, v_b_ln_g, v_b_ln_b, v_ab_w_out, v_c_w_in, v_c_conv_w, v_c_w_out, v_f_w_up, v_f_conv_w, v_f_w_down):
    given = dict(locals())
    w = {n: given[n] for n in ALL_WEIGHTS}
    m = {n: given["m_" + n] for n in ALL_WEIGHTS}
    v = {n: given["v_" + n] for n in ALL_WEIGHTS}
    T = x.shape[1]
    loss, dx, grad, delta, new_m, new_v = _step(x.reshape(T, D_MODEL), loss_target.reshape(T, D_MODEL), w, m, v)
    out = [loss[0, 0], dx.reshape(x.shape)]
    for d in (grad, delta, new_m, new_v):
        out += [d[n] for n in ALL_WEIGHTS]
    return tuple(out)
```

```python
import functools
import math

import jax
import jax.numpy as jnp
from jax import lax
from jax.experimental import pallas as pl
from jax.experimental.pallas import tpu as pltpu

F32 = jnp.float32
BF16 = jnp.bfloat16

EPS = 1e-6
D_MODEL = 1024
CHUNK = 128
HEAD_DIM = 128
A_HEADS = 4
D_A = 512
D_B = 512
B_CONV = 31
C_CONV = 3
D_FF = 2816
F_CONV = 3
N_CHIPS = 4

ADAM_LR = 0.001
ADAM_B1 = 0.9
ADAM_B2 = 0.999
ADAM_EPS = 1e-08
ADAM_WD = 0.01
ADAM_STEP = 10

SUBLANES = 8
LANES = 128
HALO_SHORT = 16
HALO_LONG = 32
VMEM_BYTES_MAX = 60000 * 1024

INV_SQRT2 = 1.0 / math.sqrt(2.0)
INV_SQRT_2PI = 1.0 / math.sqrt(2.0 * math.pi)

MESH = pl.DeviceIdType.MESH


def _cparams(*sem):
    return pltpu.CompilerParams(dimension_semantics=sem, vmem_limit_bytes=VMEM_BYTES_MAX)


def _pick(total, pref):
    for c in (2048, 1024, 512, 256, 128):
        if c <= pref and total % c == 0:
            return c
    raise ValueError(f"no tile for {total}")


def _sigmoid(x):
    return jax.nn.sigmoid(x)


def _silu(x):
    return x * _sigmoid(x)


def _dsilu(x):
    s = _sigmoid(x)
    return s * (1.0 + x * (1.0 - s))


def _gelu(x):
    return 0.5 * x * (1.0 + lax.erf(x * INV_SQRT2))


def _dgelu(x):
    return 0.5 * (1.0 + lax.erf(x * INV_SQRT2)) + x * jnp.exp(-0.5 * x * x) * INV_SQRT_2PI


def _ln_stats(x):
    mu = jnp.mean(x, axis=-1, keepdims=True)
    xc = x - mu
    var = jnp.mean(xc * xc, axis=-1, keepdims=True)
    r = lax.rsqrt(var + EPS)
    return xc * r, r


def _ln_bwd(dy, xh, r, g):
    dxh = dy * g
    m1 = jnp.mean(dxh, axis=-1, keepdims=True)
    m2 = jnp.mean(dxh * xh, axis=-1, keepdims=True)
    return r * (dxh - m1 - xh * m2)


def _rowsum(x):
    return jnp.sum(x, axis=0, keepdims=True)


HBM_REF = pl.BlockSpec(memory_space=pltpu.HBM)


def _place():
    x, y, c = lax.axis_index("x"), lax.axis_index("y"), lax.axis_index("c")
    peers = [(1 - x, y), (x, 1 - y), (1 - x, 1 - y)]
    return x, y, c, 2 * x + y, (x, y, 1 - c), peers


def _half(rows, which):
    return pl.ds(which * (rows // 2), rows // 2)


def _remote(src, dst, send_sem, recv_sem, device):
    return pltpu.make_async_remote_copy(src_ref=src, dst_ref=dst, send_sem=send_sem, recv_sem=recv_sem,
                                        device_id=device, device_id_type=MESH)


class _Job:
    def __init__(self, reads, writes, ncopies, copies):
        self.reads, self.writes, self.ncopies, self.copies = reads, writes, ncopies, copies


def _share(rows, which, part, parts):
    nr = rows // 2 // parts
    return pl.ds(which * (rows // 2) + part * nr, nr)


def _slot(ref, chip, rows, cols):
    if ref.shape[1] == N_CHIPS:
        return ref.at[0, chip, rows]
    return ref.at[0, chip // 2, rows, pl.ds(pl.multiple_of((chip % 2) * cols, LANES), cols)]


def _job_gather_ici(name, rows, cols, part, parts):
    def copies(src, dst, sem):
        x, y, c, k, sib, peers = _place()
        mine_rows = _share(rows, c, part, parts)
        out = []
        for j, (px, py) in enumerate(peers):
            mine = _slot(src[name], k, mine_rows, cols)
            out.append((_remote(mine, _slot(dst[name], k, mine_rows, cols), sem(j, 0), sem(j, 1), (px, py, c)),
                        _remote(mine, _slot(dst[name], 2 * px + py, mine_rows, cols), sem(j, 0), sem(j, 1),
                                (px, py, c))))
        return out
    return _Job([], [name], 3, copies)


def _job_gather_d2d(name, rows, cols, part, parts):
    def copies(src, dst, sem):
        x, y, c, k, sib, peers = _place()
        out = []
        for j, (px, py) in enumerate(peers):
            mine_rows, their_rows = _share(rows, c, part, parts), _share(rows, 1 - c, part, parts)
            landed = _slot(src[name], 2 * px + py, mine_rows, cols)
            out.append((_remote(landed, _slot(dst[name], 2 * px + py, mine_rows, cols), sem(j, 0), sem(j, 1), sib),
                        _remote(landed, _slot(dst[name], 2 * px + py, their_rows, cols), sem(j, 0), sem(j, 1), sib)))
        return out
    return _Job([], [name], 3, copies)


def _job_chip_gather(sname, dname):
    def copies(src, dst, sem):
        x, y, c, k, sib, peers = _place()
        return [(_remote(src[sname], dst[dname].at[k], sem(j, 0), sem(j, 1), (px, py, c)),
                 _remote(src[sname], dst[dname].at[2 * px + py], sem(j, 0), sem(j, 1), (px, py, c)))
                for j, (px, py) in enumerate(peers)]
    return _Job([sname], [dname], 3, copies)


def _job_pair_exchange(gname, tname, rows, part, parts):
    nr = rows // 2 // parts

    def copies(src, dst, sem):
        x, y, c, k, sib, peers = _place()
        cp = _remote(src[gname].at[:, _share(rows, 1 - c, part, parts), :],
                     dst[tname].at[:, pl.ds(part * nr, nr), :], sem(0, 0), sem(0, 1), sib)
        return [(cp, cp)]
    return _Job([gname], [tname], 1, copies)


def _job_chip_exchange(pname, lname, r0, nr):
    def copies(src, dst, sem):
        x, y, c, k, sib, peers = _place()
        out = []
        for j, (px, py) in enumerate(peers):
            cp = _remote(src[pname].at[2 * px + py, pl.ds(r0, nr)], dst[lname].at[j, pl.ds(r0, nr)],
                         sem(j, 0), sem(j, 1), (px, py, c))
            out.append((cp, cp))
        return out
    return _Job([pname], [lname], 3, copies)


def _job_pair_share(name, layer, rows):
    def copies(src, dst, sem):
        x, y, c, k, sib, peers = _place()
        mine = src[name].at[layer, _half(rows, c)]
        return [(_remote(mine, dst[name].at[layer, _half(rows, c)], sem(0, 0), sem(0, 1), sib),
                 _remote(mine, dst[name].at[layer, _half(rows, 1 - c)], sem(0, 0), sem(0, 1), sib))]
    return _Job([], [name], 1, copies)


class _Comm:
    def __init__(self, plan, jobs):
        self.plan, self.jobs = plan, jobs
        self.writes, self.reads = [], []
        for job in jobs:
            for n in job.writes:
                if n not in self.writes:
                    self.writes.append(n)
        for job in jobs:
            for n in job.reads:
                if n not in self.writes and n not in self.reads:
                    self.reads.append(n)
        self.ncopies = sum(job.ncopies for job in jobs)

    def descriptors(self, src, dst, sems, base):
        out = []
        for job in self.jobs:
            sem = lambda j, which, base=base: sems.at[base + j, which]
            out += job.copies(src, dst, sem)
            base += job.ncopies
        return out

    def start(self, src, dst, sems, base=0):
        for first, _ in self.descriptors(src, dst, sems, base):
            first.start()

    def finish(self, src, dst, sems, base=0):
        for _, landed in self.descriptors(src, dst, sems, base):
            landed.wait()


def _comm_operands(comm):
    bufs = comm.plan.bufs
    shapes = [jax.ShapeDtypeStruct(bufs[n].shape, bufs[n].dtype) for n in comm.writes]
    return [bufs[n] for n in comm.reads] + [bufs[n] for n in comm.writes], shapes


def _pallas(comm, body, *, name, grid, in_specs, out_specs, out_shape, compiler_params, scratch_shapes=(),
            aliases=None):
    aliases = dict(aliases or {})
    if comm is None:
        return pl.pallas_call(body, name=name, grid=grid, in_specs=in_specs, out_specs=out_specs,
                              out_shape=out_shape, scratch_shapes=list(scratch_shapes),
                              input_output_aliases=aliases, compiler_params=compiler_params)
    single = not isinstance(out_shape, (list, tuple))
    base_specs = [out_specs] if single else list(out_specs)
    base_shape = [out_shape] if single else list(out_shape)
    nb, nr, nw, nbo, nsc = len(in_specs), len(comm.reads), len(comm.writes), len(base_specs), len(scratch_shapes)

    def wrapped(*refs):
        base_in, rd, wr_in = refs[:nb], refs[nb:nb + nr], refs[nb + nr:nb + nr + nw]
        o0 = nb + nr + nw
        base_out, wr_out = refs[o0:o0 + nbo], refs[o0 + nbo:o0 + nbo + nw]
        scratch, sems = refs[o0 + nbo + nw:o0 + nbo + nw + nsc], refs[-1]
        src = dict(zip(comm.reads, rd))
        src.update(zip(comm.writes, wr_in))
        dst = dict(zip(comm.writes, wr_out))
        first = functools.reduce(jnp.logical_and, [pl.program_id(a) == 0 for a in range(len(grid))])
        last = functools.reduce(jnp.logical_and,
                                [pl.program_id(a) == pl.num_programs(a) - 1 for a in range(len(grid))])

        @pl.when(first)
        def _():
            comm.start(src, dst, sems)
        body(*base_in, *base_out, *scratch)

        @pl.when(last)
        def _():
            comm.finish(src, dst, sems)

    operands, shapes = _comm_operands(comm)
    call = pl.pallas_call(
        wrapped, name=name, grid=grid, in_specs=list(in_specs) + [HBM_REF] * (nr + nw),
        out_specs=base_specs + [HBM_REF] * nw, out_shape=base_shape + shapes,
        input_output_aliases={**aliases, **{nb + nr + q: nbo + q for q in range(nw)}},
        scratch_shapes=list(scratch_shapes) + [pltpu.SemaphoreType.DMA((comm.ncopies, 2))],
        compiler_params=compiler_params)

    def run(*args):
        outs = call(*args, *operands)
        for q, n in enumerate(comm.writes):
            comm.plan.bufs[n] = outs[nbo + q]
        return outs[0] if single else list(outs[:nbo])

    return run


def _comm_only(plan, phases, *, name):
    comms = [_Comm(plan, jobs) for jobs in phases]
    both = _Comm(plan, [job for jobs in phases for job in jobs])
    nr, nw = len(both.reads), len(both.writes)

    def body(*refs):
        rd, wr_in, wr_out, sems = refs[:nr], refs[nr:nr + nw], refs[nr + nw:nr + 2 * nw], refs[-1]
        src = dict(zip(both.reads, rd))
        src.update(zip(both.writes, wr_in))
        dst = dict(zip(both.writes, wr_out))
        base = 0
        for comm in comms:
            comm.start(src, dst, sems, base)
            comm.finish(src, dst, sems, base)
            base += comm.ncopies

    operands, shapes = _comm_operands(both)
    outs = pl.pallas_call(
        body, name=name, in_specs=[HBM_REF] * (nr + nw), out_specs=[HBM_REF] * nw, out_shape=shapes,
        input_output_aliases={nr + q: q for q in range(nw)},
        scratch_shapes=[pltpu.SemaphoreType.DMA((both.ncopies, 2))],
    )(*operands)
    for q, n in enumerate(both.writes):
        plan.bufs[n] = outs[q]


def _mm_nn(a, w, *, layer, tm, tn, residual=None, norm=None, out_dtype=F32, name, comm=None):
    T, K = a.shape
    if w.ndim == 4:
        _, S, _, n4 = w.shape
        N = S * n4
        bps = n4 // tn
        w_spec = pl.BlockSpec((None, None, K, tn), lambda j, i: (layer, j // bps, 0, j % bps))
    else:
        N = w.shape[2]
        w_spec = pl.BlockSpec((None, K, tn), lambda j, i: (layer, 0, j))
    in_specs = [pl.BlockSpec((tm, K), lambda j, i: (i, 0)), w_spec]
    args = [a, w]
    if residual is not None:
        in_specs.append(pl.BlockSpec((tm, tn), lambda j, i: (i, j)))
        args.append(residual)
    out_specs = pl.BlockSpec((tm, tn), lambda j, i: (i, j))
    out_shape = jax.ShapeDtypeStruct((T, N), out_dtype)
    if norm is not None:
        assert tn == N
        g, norm_layer = norm
        in_specs.append(pl.BlockSpec((None, 1, N), lambda j, i: (norm_layer, 0, 0)))
        args.append(g)
        out_specs = [out_specs, pl.BlockSpec((tm, tn), lambda j, i: (i, j))]
        out_shape = [out_shape, jax.ShapeDtypeStruct((T, N), BF16)]

    def body(*refs):
        a_ref, w_ref = refs[0], refs[1]
        acc = jnp.dot(a_ref[...].astype(BF16), w_ref[...], preferred_element_type=F32)
        if residual is not None:
            acc = refs[2][...] + acc
        if norm is None:
            refs[-1][...] = acc.astype(out_dtype)
        else:
            refs[-2][...] = acc.astype(out_dtype)
            r = lax.rsqrt(jnp.mean(acc * acc, axis=-1, keepdims=True) + EPS)
            refs[-1][...] = (acc * r * refs[-3][...]).astype(BF16)

    return _pallas(
        comm, body, name=name, grid=(N // tn, T // tm), in_specs=in_specs,
        out_specs=out_specs, out_shape=out_shape,
        compiler_params=_cparams("parallel", "parallel"),
    )(*args)


def _norm_mm_nn(x, g, w, *, g_layer, tm, tn, name, comm=None):
    T, K = x.shape
    _, S, _, n4 = w.shape
    bps = n4 // tn

    def body(x_ref, g_ref, w_ref, h_ref, o_ref):
        @pl.when(pl.program_id(1) == 0)
        def _():
            xf = x_ref[...]
            r = lax.rsqrt(jnp.mean(xf * xf, axis=-1, keepdims=True) + EPS)
            h_ref[...] = (xf * r * g_ref[...]).astype(BF16)
        o_ref[...] = jnp.dot(h_ref[...], w_ref[...], preferred_element_type=F32).astype(BF16)

    return _pallas(
        comm, body, name=name, grid=(T // tm, S * bps),
        in_specs=[pl.BlockSpec((tm, K), lambda i, j: (i, 0)),
                  pl.BlockSpec((None, 1, K), lambda i, j: (g_layer, 0, 0)),
                  pl.BlockSpec((None, None, K, tn), lambda i, j: (0, j // bps, 0, j % bps))],
        out_specs=[pl.BlockSpec((tm, K), lambda i, j: (i, 0)), pl.BlockSpec((tm, tn), lambda i, j: (i, j))],
        out_shape=[jax.ShapeDtypeStruct((T, K), BF16), jax.ShapeDtypeStruct((T, S * n4), BF16)],
        compiler_params=_cparams("parallel", "arbitrary"),
    )(x, g, w)


def _mm_nt(dy, w, *, layer, tm, tn, name, out_dtype=F32, comm=None):
    T = dy.shape[0]
    nt_dims = (((1,), (1,)), ((), ()))
    _, R, N = w.shape

    def body2(dy_ref, w_ref, o_ref):
        o_ref[...] = lax.dot_general(dy_ref[...].astype(BF16), w_ref[...], nt_dims,
                                     preferred_element_type=F32).astype(out_dtype)

    return _pallas(
        comm, body2, name=name, grid=(R // tn, T // tm),
        in_specs=[pl.BlockSpec((tm, N), lambda j, i: (i, 0)),
                  pl.BlockSpec((None, tn, N), lambda j, i: (layer, j, 0))],
        out_specs=pl.BlockSpec((tm, tn), lambda j, i: (i, j)),
        out_shape=jax.ShapeDtypeStruct((T, R), out_dtype),
        compiler_params=_cparams("parallel", "parallel"),
    )(dy, w)


def _mm_tn(a, dy, *, shards, tk, tn, tt, name, comm=None):
    T, K = a.shape
    N = dy.shape[1]
    tn_dims = (((0,), (0,)), ((), ()))
    n4 = N if shards is None else N // shards
    span = max(tn // n4, 1)

    def body(a_ref, dy_ref, o_ref):
        @pl.when(pl.program_id(2) == 0)
        def _():
            o_ref[...] = jnp.zeros_like(o_ref)
        r = lax.dot_general(a_ref[...].astype(BF16), dy_ref[...].astype(BF16), tn_dims,
                            preferred_element_type=F32)
        if span == 1:
            o_ref[...] += r
        else:
            for q in range(span):
                o_ref[q] += r[:, q * n4:(q + 1) * n4]

    if shards is None:
        out_spec = pl.BlockSpec((tk, tn), lambda k, n, t: (k, n))
        out_shape = jax.ShapeDtypeStruct((K, N), F32)
    elif span > 1:
        out_spec = pl.BlockSpec((span, tk, n4), lambda k, n, t: (n, k, 0))
        out_shape = jax.ShapeDtypeStruct((shards, K, n4), F32)
    else:
        bps = n4 // tn
        out_spec = pl.BlockSpec((None, tk, tn), lambda k, n, t: (n // bps, k, n % bps))
        out_shape = jax.ShapeDtypeStruct((shards, K, n4), F32)
    return _pallas(
        comm, body, name=name, grid=(K // tk, N // tn, T // tt),
        in_specs=[pl.BlockSpec((tt, tk), lambda k, n, t: (t, k)),
                  pl.BlockSpec((tt, tn), lambda k, n, t: (t, n))],
        out_specs=out_spec, out_shape=out_shape,
        compiler_params=_cparams("parallel", "parallel", "arbitrary"),
    )(a, dy)


def _rmsnorm_bwd_math(xf, g, dh, dres):
    r = lax.rsqrt(jnp.mean(xf * xf, axis=-1, keepdims=True) + EPS)
    xh = xf * r
    dxh = dh * g
    dx = dres + r * (dxh - xh * jnp.mean(dxh * xh, axis=-1, keepdims=True))
    return dx, _rowsum(dh * xh)


def _mm_nt_norm(dy, w, x, g, dres, *, g_layer, tm, name, comm=None):
    T = dy.shape[0]
    _, S, K, n4 = w.shape
    nt_dims = (((1,), (1,)), ((), ()))

    def body(dy_ref, w_ref, x_ref, g_ref, dres_ref, dx_ref, dg_ref):
        @pl.when(pl.program_id(0) == 0)
        def _():
            dg_ref[...] = jnp.zeros_like(dg_ref)
        dh = None
        for s in range(S):
            part = lax.dot_general(dy_ref[:, s * n4:(s + 1) * n4].astype(BF16), w_ref[s], nt_dims,
                                   preferred_element_type=F32)
            dh = part if dh is None else dh + part
        dx, dg = _rmsnorm_bwd_math(x_ref[...], g_ref[...], dh, dres_ref[...])
        dx_ref[...] = dx
        dg_ref[...] += dg

    row = lambda i: (i, 0)
    return _pallas(
        comm, body, name=name, grid=(T // tm,),
        in_specs=[pl.BlockSpec((tm, S * n4), row),
                  pl.BlockSpec((None, S, K, n4), lambda i: (0, 0, 0, 0)),
                  pl.BlockSpec((tm, K), row),
                  pl.BlockSpec((None, 1, K), lambda i: (g_layer, 0, 0)),
                  pl.BlockSpec((tm, K), row)],
        out_specs=[pl.BlockSpec((tm, K), row), pl.BlockSpec((1, K), lambda i: (0, 0))],
        out_shape=[jax.ShapeDtypeStruct((T, K), F32), jax.ShapeDtypeStruct((1, K), F32)],
        compiler_params=_cparams("arbitrary"),
    )(dy, w, x, g, dres)


def _mm_nn_loss(a, w, residual, tgt, g, *, tm, name, comm=None):
    T, K = a.shape
    D = w.shape[2]

    def body(a_ref, w_ref, res_ref, t_ref, g_ref, loss_ref, dx_ref, dg_ref):
        @pl.when(pl.program_id(0) == 0)
        def _():
            dg_ref[...] = jnp.zeros_like(dg_ref)
            loss_ref[...] = jnp.zeros_like(loss_ref)
        xf = res_ref[...] + jnp.dot(a_ref[...], w_ref[...], preferred_element_type=F32)
        gg = g_ref[...]
        r = lax.rsqrt(jnp.mean(xf * xf, axis=-1, keepdims=True) + EPS)
        xh = xf * r
        err = xh * gg - t_ref[...]
        row = jnp.mean(err * err, axis=-1, keepdims=True)
        loss_ref[...] += 0.5 * jnp.sum(row, axis=0, keepdims=True)
        dy = err * (1.0 / D)
        dg_ref[...] += _rowsum(dy * xh)
        dxh = dy * gg
        dx_ref[...] = r * (dxh - xh * jnp.mean(dxh * xh, axis=-1, keepdims=True))

    row_spec = pl.BlockSpec((tm, D), lambda i: (i, 0))
    return _pallas(
        comm, body, name=name, grid=(T // tm,),
        in_specs=[pl.BlockSpec((tm, K), lambda i: (i, 0)), pl.BlockSpec((None, K, D), lambda i: (0, 0, 0)),
                  row_spec, row_spec, pl.BlockSpec((1, D), lambda i: (0, 0))],
        out_specs=[pl.BlockSpec((1, 1), lambda i: (0, 0)), row_spec, pl.BlockSpec((1, D), lambda i: (0, 0))],
        out_shape=[jax.ShapeDtypeStruct((1, 1), F32), jax.ShapeDtypeStruct((T, D), F32),
                   jax.ShapeDtypeStruct((1, D), F32)],
        compiler_params=_cparams("arbitrary"),
    )(a, w, residual, tgt, g)


CONV_ROWS = 64
CONV_COLS = 256


def _halo_prev_index(tm, halo):
    per = tm // halo
    return lambda i: jnp.maximum(i * per - 1, 0)


def _halo_next_index(tm, halo, total):
    per = tm // halo
    last = total // halo - 1
    return lambda i: jnp.minimum((i + 1) * per, last)


def _causal_mask():
    t = lax.broadcasted_iota(jnp.int32, (CHUNK, CHUNK), 0)
    s = lax.broadcasted_iota(jnp.int32, (CHUNK, CHUNK), 1)
    return s <= t


def _mixer_ab_fwd(z, a_ln_g, a_ln_b, w_s, b_s, conv_w, conv_b, b_ln_g, b_ln_b, *, tm, name, comm=None):
    T = z.shape[0]
    nchunk = tm // CHUNK
    halo = HALO_LONG

    def body(za_ref, zb_ref, zh_ref, alg_ref, alb_ref, ws_ref, bs_ref, cw_ref, cbias_ref,
             blg_ref, blb_ref, y_ref, cb_ref, ext_ref):
        i = pl.program_id(0)
        gu = _gelu(za_ref[:, :D_A].astype(F32))
        gv = _gelu(za_ref[:, D_A:].astype(F32))
        xh, _ = _ln_stats(gv)
        lv = (xh * alg_ref[...] + alb_ref[...]).astype(BF16)
        mask = _causal_mask()
        for h in range(A_HEADS):
            wm = jnp.where(mask, ws_ref[h], 0.0).astype(BF16)
            cols = slice(h * HEAD_DIM, (h + 1) * HEAD_DIM)
            for c in range(nchunk):
                rows = slice(c * CHUNK, (c + 1) * CHUNK)
                mixed = jnp.dot(wm, lv[rows, cols], preferred_element_type=F32) + bs_ref[h]
                y_ref[rows, cols] = (gu[rows, cols] * mixed).astype(BF16)
        ext_ref[halo:halo + tm, :] = zb_ref[:, :D_B].astype(F32) * _sigmoid(zb_ref[:, D_B:].astype(F32))
        prev = zh_ref[:, :D_B].astype(F32) * _sigmoid(zh_ref[:, D_B:].astype(F32))
        ext_ref[0:halo, :] = jnp.where(i > 0, prev, 0.0)
        for rb in range(tm // CONV_ROWS):
            for cb in range(D_B // CONV_COLS):
                cs = slice(cb * CONV_COLS, (cb + 1) * CONV_COLS)
                window = ext_ref[rb * CONV_ROWS:rb * CONV_ROWS + CONV_ROWS + halo, cs]
                acc = jnp.zeros((CONV_ROWS, CONV_COLS), F32)
                for k in range(B_CONV):
                    shifted = _rows_after(window, halo - (B_CONV - 1) + k)[:CONV_ROWS]
                    acc = acc + cw_ref[k:k + 1, cs] * shifted
                cb_ref[rb * CONV_ROWS:(rb + 1) * CONV_ROWS, cs] = acc + cbias_ref[:, cs]
        xhb, _ = _ln_stats(cb_ref[...])
        y_ref[:, D_A:] = _silu(xhb * blg_ref[...] + blb_ref[...]).astype(BF16)

    row = lambda i: (i, 0)
    par = lambda i: (0, 0)
    return _pallas(
        comm, body, name=name, grid=(T // tm,),
        in_specs=[pl.BlockSpec((tm, 2 * D_A), lambda i: (i, 0)),
                  pl.BlockSpec((tm, 2 * D_B), lambda i: (i, 1)),
                  pl.BlockSpec((halo, 2 * D_B), lambda i: (_halo_prev_index(tm, halo)(i), 1)),
                  pl.BlockSpec((1, D_A), par), pl.BlockSpec((1, D_A), par),
                  pl.BlockSpec((A_HEADS, CHUNK, CHUNK), lambda i: (0, 0, 0)),
                  pl.BlockSpec((A_HEADS, CHUNK, 1), lambda i: (0, 0, 0)),
                  pl.BlockSpec((B_CONV, D_B), par), pl.BlockSpec((1, D_B), par),
                  pl.BlockSpec((1, D_B), par), pl.BlockSpec((1, D_B), par)],
        out_specs=[pl.BlockSpec((tm, D_A + D_B), row), pl.BlockSpec((tm, D_B), row)],
        out_shape=[jax.ShapeDtypeStruct((T, D_A + D_B), BF16), jax.ShapeDtypeStruct((T, D_B), F32)],
        scratch_shapes=[pltpu.VMEM((halo + tm, D_B), F32)],
        compiler_params=_cparams("parallel"),
    )(z, z, z, a_ln_g, a_ln_b, w_s, b_s, conv_w, conv_b, b_ln_g, b_ln_b)


def _mixer_ab_bwd_pre(z, cb, dy, a_ln_g, a_ln_b, w_s, b_s, b_ln_g, b_ln_b, *, tm, name, comm=None):
    T = z.shape[0]
    nchunk = tm // CHUNK
    tn_dims = (((0,), (0,)), ((), ()))
    nt_dims = (((1,), (1,)), ((), ()))

    def body(za_ref, cb_ref, dy_ref, alg_ref, alb_ref, ws_ref, bs_ref, blg_ref, blb_ref,
             dza_ref, dcb_ref, dalg_ref, dalb_ref, dws_ref, dbs_ref, dblg_ref, dblb_ref,
             dlv_ref):
        @pl.when(pl.program_id(0) == 0)
        def _():
            for ref in (dalg_ref, dalb_ref, dws_ref, dbs_ref, dblg_ref, dblb_ref):
                ref[...] = jnp.zeros_like(ref)
        ua = za_ref[:, :D_A].astype(F32)
        va = za_ref[:, D_A:].astype(F32)
        gu = _gelu(ua)
        gv = _gelu(va)
        xh, r = _ln_stats(gv)
        alg = alg_ref[...]
        lv = (xh * alg + alb_ref[...]).astype(BF16)
        dya = dy_ref[:, :D_A].astype(F32)
        mask = _causal_mask()
        for h in range(A_HEADS):
            wm = jnp.where(mask, ws_ref[h], 0.0).astype(BF16)
            cols = slice(h * HEAD_DIM, (h + 1) * HEAD_DIM)
            dwm = jnp.zeros((CHUNK, CHUNK), F32)
            dbs = jnp.zeros((CHUNK, 1), F32)
            for c in range(nchunk):
                rows = slice(c * CHUNK, (c + 1) * CHUNK)
                lvb = lv[rows, cols]
                mixed = jnp.dot(wm, lvb, preferred_element_type=F32) + bs_ref[h]
                dyb = dya[rows, cols]
                dza_ref[rows, cols] = (dyb * mixed * _dgelu(ua[rows, cols])).astype(BF16)
                dmixed = dyb * gu[rows, cols]
                dmb = dmixed.astype(BF16)
                dlv_ref[rows, cols] = lax.dot_general(wm, dmb, tn_dims, preferred_element_type=F32)
                dwm = dwm + lax.dot_general(dmb, lvb, nt_dims, preferred_element_type=F32)
                dbs = dbs + jnp.sum(dmixed, axis=1, keepdims=True)
            dws_ref[h] += jnp.where(mask, dwm, 0.0)
            dbs_ref[h] += dbs
        dlv = dlv_ref[...]
        dalg_ref[...] += _rowsum(dlv * xh)
        dalb_ref[...] += _rowsum(dlv)
        dgv = _ln_bwd(dlv, xh, r, alg)
        dza_ref[:, D_A:] = (dgv * _dgelu(va)).astype(BF16)
        xhb, rb = _ln_stats(cb_ref[...])
        blg = blg_ref[...]
        lb = xhb * blg + blb_ref[...]
        dlb = dy_ref[:, D_A:].astype(F32) * _dsilu(lb)
        dblg_ref[...] += _rowsum(dlb * xhb)
        dblb_ref[...] += _rowsum(dlb)
        dcb_ref[...] = _ln_bwd(dlb, xhb, rb, blg)

    row = lambda i: (i, 0)
    par = lambda i: (0, 0)
    par3 = lambda i: (0, 0, 0)
    return _pallas(
        comm, body, name=name, grid=(T // tm,),
        in_specs=[pl.BlockSpec((tm, 2 * D_A), row), pl.BlockSpec((tm, D_B), row),
                  pl.BlockSpec((tm, D_A + D_B), row),
                  pl.BlockSpec((1, D_A), par), pl.BlockSpec((1, D_A), par),
                  pl.BlockSpec((A_HEADS, CHUNK, CHUNK), par3),
                  pl.BlockSpec((A_HEADS, CHUNK, 1), par3),
                  pl.BlockSpec((1, D_B), par), pl.BlockSpec((1, D_B), par)],
        out_specs=[pl.BlockSpec((tm, 2 * D_A), row), pl.BlockSpec((tm, D_B), row),
                   pl.BlockSpec((1, D_A), par), pl.BlockSpec((1, D_A), par),
                   pl.BlockSpec((A_HEADS, CHUNK, CHUNK), par3),
                   pl.BlockSpec((A_HEADS, CHUNK, 1), par3),
                   pl.BlockSpec((1, D_B), par), pl.BlockSpec((1, D_B), par)],
        out_shape=[jax.ShapeDtypeStruct((T, 2 * D_A + 2 * D_B), BF16), jax.ShapeDtypeStruct((T, D_B), F32),
                   jax.ShapeDtypeStruct((1, D_A), F32), jax.ShapeDtypeStruct((1, D_A), F32),
                   jax.ShapeDtypeStruct((A_HEADS, CHUNK, CHUNK), F32),
                   jax.ShapeDtypeStruct((A_HEADS, CHUNK, 1), F32),
                   jax.ShapeDtypeStruct((1, D_B), F32), jax.ShapeDtypeStruct((1, D_B), F32)],
        scratch_shapes=[pltpu.VMEM((tm, D_A), F32)],
        compiler_params=_cparams("arbitrary"),
    )(z, cb, dy, a_ln_g, a_ln_b, w_s, b_s, b_ln_g, b_ln_b)


def _mixer_b_conv_bwd(z, dcb, conv_w, dz, *, tm, name, comm=None):
    T = z.shape[0]
    halo = HALO_LONG

    def body(zb_ref, dcb_ref, dcn_ref, cw_ref, dz_in_ref, dzb_ref, dcw_ref, dbias_ref, dext_ref):
        i = pl.program_id(0)
        last = pl.num_programs(0) - 1

        @pl.when(i == 0)
        def _():
            dcw_ref[...] = jnp.zeros_like(dcw_ref)
            dbias_ref[...] = jnp.zeros_like(dbias_ref)
        dcb = dcb_ref[...]
        dext_ref[0:tm, :] = dcb
        dext_ref[tm:tm + halo, :] = jnp.where(i < last, dcn_ref[...], 0.0)
        dbias_ref[...] += _rowsum(dcb)
        for rb in range(tm // CONV_ROWS):
            for cb in range(D_B // CONV_COLS):
                cs = slice(cb * CONV_COLS, (cb + 1) * CONV_COLS)
                gcs = slice(D_B + cb * CONV_COLS, D_B + (cb + 1) * CONV_COLS)
                rs = slice(rb * CONV_ROWS, (rb + 1) * CONV_ROWS)
                xbb = zb_ref[rs, cs].astype(F32)
                sgb = _sigmoid(zb_ref[rs, gcs].astype(F32))
                yb0 = xbb * sgb
                window = dext_ref[rb * CONV_ROWS:rb * CONV_ROWS + CONV_ROWS + halo, cs]
                acc = jnp.zeros((CONV_ROWS, CONV_COLS), F32)
                for k in range(B_CONV):
                    shifted = _rows_after(window, (B_CONV - 1) - k)[:CONV_ROWS]
                    acc = acc + cw_ref[k:k + 1, cs] * shifted
                    dcw_ref[k:k + 1, cs] += _rowsum(shifted * yb0)
                dzb_ref[rs, cs] = (acc * sgb).astype(BF16)
                dzb_ref[rs, gcs] = (acc * xbb * sgb * (1.0 - sgb)).astype(BF16)

    row = lambda i: (i, 0)
    par = lambda i: (0, 0)
    return _pallas(
        comm, body, name=name, grid=(T // tm,),
        in_specs=[pl.BlockSpec((tm, 2 * D_B), lambda i: (i, 1)),
                  pl.BlockSpec((tm, D_B), row),
                  pl.BlockSpec((halo, D_B), lambda i: (_halo_next_index(tm, halo, T)(i), 0)),
                  pl.BlockSpec((B_CONV, D_B), par), pl.BlockSpec(memory_space=pl.ANY)],
        out_specs=[pl.BlockSpec((tm, 2 * D_B), lambda i: (i, 1)), pl.BlockSpec((B_CONV, D_B), par),
                   pl.BlockSpec((1, D_B), par)],
        out_shape=[jax.ShapeDtypeStruct(dz.shape, BF16), jax.ShapeDtypeStruct((B_CONV, D_B), F32),
                   jax.ShapeDtypeStruct((1, D_B), F32)],
        scratch_shapes=[pltpu.VMEM((tm + halo, D_B), F32)], aliases={4: 0},
        compiler_params=_cparams("arbitrary"),
    )(z, dcb, dcb, conv_w, dz)


def _rows_before(x, a):
    return x if a == 0 else pltpu.roll(x, a, axis=0)


def _rows_after(x, a):
    return x if a == 0 else pltpu.roll(x, x.shape[0] - a, axis=0)


def _conv3(w_ref, x, halo, cs):
    acc = w_ref[2:3, cs] * x[halo:]
    acc = acc + w_ref[1:2, cs] * _rows_before(x, 1)[halo:]
    return acc + w_ref[0:1, cs] * _rows_before(x, 2)[halo:]


def _mixer_c_fwd(z, conv_w, *, tm, name, comm=None):
    T = z.shape[0]
    D = D_MODEL
    halo = HALO_SHORT
    W = CONV_COLS

    def body(bg_ref, cg_ref, xv_ref, cgh_ref, xvh_ref, w_ref, r_ref):
        i = pl.program_id(0)
        for cb in range(D // W):
            cs = slice(cb * W, (cb + 1) * W)
            prev = jnp.where(i > 0, cgh_ref[:, cs].astype(F32) * xvh_ref[:, cs].astype(F32), 0.0)
            p = jnp.concatenate([prev, cg_ref[:, cs].astype(F32) * xv_ref[:, cs].astype(F32)], axis=0)
            r_ref[:, cs] = (bg_ref[:, cs].astype(F32) * _conv3(w_ref, p, halo, cs)).astype(BF16)

    hp = _halo_prev_index(tm, halo)
    return _pallas(
        comm, body, name=name, grid=(T // tm,),
        in_specs=[pl.BlockSpec((tm, D), lambda i: (i, 0)), pl.BlockSpec((tm, D), lambda i: (i, 1)),
                  pl.BlockSpec((tm, D), lambda i: (i, 2)),
                  pl.BlockSpec((halo, D), lambda i: (hp(i), 1)),
                  pl.BlockSpec((halo, D), lambda i: (hp(i), 2)),
                  pl.BlockSpec((None, C_CONV, D), lambda i: (0, 0, 0))],
        out_specs=pl.BlockSpec((tm, D), lambda i: (i, 0)),
        out_shape=jax.ShapeDtypeStruct((T, D), BF16),
        compiler_params=_cparams("parallel"),
    )(z, z, z, z, z, conv_w)


def _mixer_c_bwd(z, dr, conv_w, *, tm, name, comm=None):
    T = z.shape[0]
    D = D_MODEL
    halo = HALO_SHORT
    W = CONV_COLS

    def body(bg_ref, cg_ref, xv_ref, cgh_ref, xvh_ref, bgn_ref, dr_ref, drn_ref, w_ref, dz_ref, dw_ref):
        i = pl.program_id(0)
        last = pl.num_programs(0) - 1

        @pl.when(i == 0)
        def _():
            dw_ref[...] = jnp.zeros_like(dw_ref)
        for cb in range(D // W):
            cs = slice(cb * W, (cb + 1) * W)
            cg = cg_ref[:, cs].astype(F32)
            xv = xv_ref[:, cs].astype(F32)
            dr = dr_ref[:, cs].astype(F32)
            p = cg * xv
            prev = jnp.where(i > 0, cgh_ref[:, cs].astype(F32) * xvh_ref[:, cs].astype(F32), 0.0)
            q = _conv3(w_ref, jnp.concatenate([prev, p], axis=0), halo, cs)
            dz_ref[:, cs] = (dr * q).astype(BF16)
            nxt = jnp.where(i < last, drn_ref[:, cs].astype(F32) * bgn_ref[:, cs].astype(F32), 0.0)
            dq = jnp.concatenate([dr * bg_ref[:, cs].astype(F32), nxt], axis=0)
            dp = None
            for k in range(C_CONV):
                shifted = _rows_after(dq, 2 - k)[:tm]
                term = w_ref[k:k + 1, cs] * shifted
                dp = term if dp is None else dp + term
                dw_ref[k:k + 1, cs] += _rowsum(shifted * p)
            dz_ref[:, D + cb * W:D + (cb + 1) * W] = (dp * xv).astype(BF16)
            dz_ref[:, 2 * D + cb * W:2 * D + (cb + 1) * W] = (dp * cg).astype(BF16)

    hp = _halo_prev_index(tm, halo)
    hn = _halo_next_index(tm, halo, T)
    return _pallas(
        comm, body, name=name, grid=(T // tm,),
        in_specs=[pl.BlockSpec((tm, D), lambda i: (i, 0)), pl.BlockSpec((tm, D), lambda i: (i, 1)),
                  pl.BlockSpec((tm, D), lambda i: (i, 2)),
                  pl.BlockSpec((halo, D), lambda i: (hp(i), 1)),
                  pl.BlockSpec((halo, D), lambda i: (hp(i), 2)),
                  pl.BlockSpec((halo, D), lambda i: (hn(i), 0)),
                  pl.BlockSpec((tm, D), lambda i: (i, 0)),
                  pl.BlockSpec((halo, D), lambda i: (hn(i), 0)),
                  pl.BlockSpec((None, C_CONV, D), lambda i: (0, 0, 0))],
        out_specs=[pl.BlockSpec((tm, 3 * D), lambda i: (i, 0)),
                   pl.BlockSpec((C_CONV, D), lambda i: (0, 0))],
        out_shape=[jax.ShapeDtypeStruct((T, 3 * D), BF16), jax.ShapeDtypeStruct((C_CONV, D), F32)],
        compiler_params=_cparams("arbitrary"),
    )(z, z, z, z, z, z, dr, dr, conv_w)


FFN_COLS = 128


def _ffn_act_fwd(up, conv_w, *, layer, tm, name, comm=None):
    T = up.shape[0]
    halo = HALO_SHORT
    W = FFN_COLS

    def body(up_ref, uph_ref, w_ref, a_ref, upc_ref):
        i = pl.program_id(0)

        def conv(cs):
            prev = jnp.where(i > 0, uph_ref[:, cs], jnp.zeros((halo, W), BF16))
            return _conv3(w_ref, jnp.concatenate([prev, up_ref[:, cs]], axis=0).astype(F32), halo, cs)

        for cb in range(D_FF // W):
            gs = slice(cb * W, (cb + 1) * W)
            vs = slice(D_FF + cb * W, D_FF + (cb + 1) * W)
            g = conv(gs)
            v = conv(vs)
            upc_ref[:, gs] = g.astype(BF16)
            upc_ref[:, vs] = v.astype(BF16)
            a_ref[:, gs] = (_silu(g) * v).astype(BF16)

    return _pallas(
        comm, body, name=name, grid=(T // tm,),
        in_specs=[pl.BlockSpec((tm, 2 * D_FF), lambda i: (i, 0)),
                  pl.BlockSpec((halo, 2 * D_FF), lambda i: (_halo_prev_index(tm, halo)(i), 0)),
                  pl.BlockSpec((None, F_CONV, 2 * D_FF), lambda i: (layer, 0, 0))],
        out_specs=[pl.BlockSpec((tm, D_FF), lambda i: (i, 0)),
                   pl.BlockSpec((tm, 2 * D_FF), lambda i: (i, 0))],
        out_shape=[jax.ShapeDtypeStruct((T, D_FF), BF16), jax.ShapeDtypeStruct((T, 2 * D_FF), BF16)],
        compiler_params=_cparams("parallel"),
    )(up, up, conv_w)


def _ffn_act_bwd(up, upc, da, conv_w, *, layer, tm, name, comm=None):
    T = up.shape[0]
    halo = HALO_SHORT
    W = FFN_COLS

    def body(up_ref, upc_ref, upcn_ref, da_ref, dan_ref, w_ref, dup_ref, dw_ref):
        i = pl.program_id(0)
        last = pl.num_programs(0) - 1

        @pl.when(i == 0)
        def _():
            dw_ref[...] = jnp.zeros_like(dw_ref)
        live = jnp.where(i < last, 1.0, 0.0)
        for cb in range(D_FF // W):
            gs = slice(cb * W, (cb + 1) * W)
            vs = slice(D_FF + cb * W, D_FF + (cb + 1) * W)
            g = jnp.concatenate([upc_ref[:, gs], upcn_ref[:, gs]], axis=0).astype(F32)
            v = jnp.concatenate([upc_ref[:, vs], upcn_ref[:, vs]], axis=0).astype(F32)
            da = jnp.concatenate([da_ref[:, gs].astype(F32), dan_ref[:, gs].astype(F32) * live], axis=0)
            s = _sigmoid(g)
            silu = g * s
            grads = (da * v * (s * (1.0 + g * (1.0 - s))), da * silu)
            for cs, d in zip((gs, vs), grads):
                u = up_ref[:, cs].astype(F32)
                acc = None
                for k in range(F_CONV):
                    shifted = _rows_after(d, 2 - k)[:tm]
                    term = w_ref[k:k + 1, cs] * shifted
                    acc = term if acc is None else acc + term
                    dw_ref[k:k + 1, cs] += _rowsum(shifted * u)
                dup_ref[:, cs] = acc.astype(BF16)

    hn = _halo_next_index(tm, halo, T)
    return _pallas(
        comm, body, name=name, grid=(T // tm,),
        in_specs=[pl.BlockSpec((tm, 2 * D_FF), lambda i: (i, 0)),
                  pl.BlockSpec((tm, 2 * D_FF), lambda i: (i, 0)),
                  pl.BlockSpec((halo, 2 * D_FF), lambda i: (hn(i), 0)),
                  pl.BlockSpec((tm, D_FF), lambda i: (i, 0)),
                  pl.BlockSpec((halo, D_FF), lambda i: (hn(i), 0)),
                  pl.BlockSpec((None, F_CONV, 2 * D_FF), lambda i: (layer, 0, 0))],
        out_specs=[pl.BlockSpec((tm, 2 * D_FF), lambda i: (i, 0)),
                   pl.BlockSpec((F_CONV, 2 * D_FF), lambda i: (0, 0))],
        out_shape=[jax.ShapeDtypeStruct((T, 2 * D_FF), BF16),
                   jax.ShapeDtypeStruct((F_CONV, 2 * D_FF), F32)],
        compiler_params=_cparams("arbitrary"),
    )(up, upc, upc, da, da, conv_w)


def _local_step(x, tgt, small, plan):
    T = x.shape[0]
    tm_e = _pick(T, 256)
    tm_a = _pick(T, 512)
    tm_b = _pick(T, 128)
    tm = _pick(T, 1024)
    tm_f = _pick(T, 512)
    tt = _pick(T, 2048)
    nm = small["norm_mix"].reshape(2, 1, D_MODEL)
    nf = small["norm_ffn"].reshape(2, 1, D_MODEL)
    ngf = small["norm_final"].reshape(1, D_MODEL)
    b_s = small["a_b_s"].reshape(A_HEADS, CHUNK, 1)
    w_s = small["a_w_s"].reshape(A_HEADS, CHUNK, CHUNK)
    b_conv_w = small["b_conv_w"].reshape(B_CONV, D_B)
    sg = {}
    wt, cm = plan.weight, plan.comm

    h_m0, z_ab = _norm_mm_nn(x, nm, wt("ab_w_in", 0), g_layer=0, tm=tm, tn=512, name="ab_in", comm=cm("ab_in"))
    yab, cb = _mixer_ab_fwd(z_ab, small["a_ln_g"], small["a_ln_b"], w_s, b_s, b_conv_w, small["b_conv_b"],
                            small["b_ln_g"], small["b_ln_b"], tm=tm_e, name="mixer_ab", comm=cm("mixer_ab"))
    x1, h_f0 = _mm_nn(yab, wt("ab_w_out", 0), layer=0, tm=tm, tn=D_MODEL, residual=x, norm=(nf, 0),
                      name="ab_out", comm=cm("ab_out"))

    def ffn_fwd(xin, h, layer):
        up = _mm_nn(h, wt("f_w_up", layer), layer=0, tm=tm, tn=2 * 1408, out_dtype=BF16, name=f"ffn_up{layer}",
                    comm=cm(f"ffn_up{layer}"))
        a, upc = _ffn_act_fwd(up, small["f_conv_w"], layer=layer, tm=tm_a, name=f"ffn_act{layer}",
                              comm=cm(f"ffn_act{layer}"))
        if layer == 0:
            out = _mm_nn(a, wt("f_w_down", layer), layer=0, tm=tm, tn=D_MODEL, residual=xin, norm=(nm, 1),
                         name=f"ffn_down{layer}", comm=cm(f"ffn_down{layer}"))
        else:
            out = _mm_nn_loss(a, wt("f_w_down", layer), xin, tgt, ngf, tm=tm, name=f"ffn_down{layer}",
                              comm=cm(f"ffn_down{layer}"))
        return up, upc, a, out

    up0, upc0, a0, (x2, h_m1) = ffn_fwd(x1, h_f0, 0)
    z_c = _mm_nn(h_m1, wt("c_w_in", 0), layer=0, tm=tm, tn=768, out_dtype=BF16, name="c_in", comm=cm("c_in"))
    r = _mixer_c_fwd(z_c, small["c_conv_w"], tm=tm_e, name="mixer_c", comm=cm("mixer_c"))
    x3, h_f1 = _mm_nn(r, wt("c_w_out", 0), layer=0, tm=tm, tn=D_MODEL, residual=x2, norm=(nf, 1),
                      name="c_out", comm=cm("c_out"))
    up1, upc1, a1, (loss, dx, sg["norm_final"]) = ffn_fwd(x3, h_f1, 1)

    def ffn_bwd(dx, xin, h, up, upc, a, layer):
        da = _mm_nt(dx, wt("f_w_down", layer), layer=0, tm=tm, tn=1408, out_dtype=BF16,
                    name=f"ffn_down_dx{layer}", comm=cm(f"ffn_down_dx{layer}"))
        plan.grad_ready("f_w_down", layer, _mm_tn(a, dx, shards=None, tk=1408, tn=1024, tt=tt,
                                                  name=f"ffn_down_dw{layer}", comm=cm(f"ffn_down_dw{layer}")))
        dup, dcw = _ffn_act_bwd(up, upc, da, small["f_conv_w"], layer=layer, tm=tm_b, name=f"ffn_act_bwd{layer}",
                                comm=cm(f"ffn_act_bwd{layer}"))
        dxin, dg = _mm_nt_norm(dup, wt("f_w_up", layer), xin, nf, dx, g_layer=layer, tm=tm_f,
                               name=f"ffn_up_dx{layer}", comm=cm(f"ffn_up_dx{layer}"))
        plan.grad_ready("f_w_up", layer, _mm_tn(h, dup, shards=N_CHIPS, tk=512, tn=2 * 1408, tt=tt,
                                                name=f"ffn_up_dw{layer}", comm=cm(f"ffn_up_dw{layer}")))
        return dxin, dg, dcw

    dx, dnf1, dfc1 = ffn_bwd(dx, x3, h_f1, up1, upc1, a1, 1)
    dr = _mm_nt(dx, wt("c_w_out", 0), layer=0, tm=tm, tn=512, out_dtype=BF16, name="c_out_dx", comm=cm("c_out_dx"))
    plan.grad_ready("c_w_out", 0, _mm_tn(r, dx, shards=None, tk=1024, tn=1024, tt=tt, name="c_out_dw",
                                         comm=cm("c_out_dw")))
    dz_c, dccw = _mixer_c_bwd(z_c, dr, small["c_conv_w"], tm=tm_e, name="mixer_c_bwd", comm=cm("mixer_c_bwd"))
    sg["c_conv_w"] = dccw.reshape(1, C_CONV, D_MODEL)
    plan.grad_ready("c_w_in", 0, _mm_tn(h_m1, dz_c, shards=N_CHIPS, tk=1024, tn=768, tt=tt, name="c_in_dw",
                                        comm=cm("c_in_dw")))
    dx, dnm1 = _mm_nt_norm(dz_c, wt("c_w_in", 0), x2, nm, dx, g_layer=1, tm=tm_f, name="c_in_dx",
                           comm=cm("c_in_dx"))
    dx, dnf0, dfc0 = ffn_bwd(dx, x1, h_f0, up0, upc0, a0, 0)
    dyab = _mm_nt(dx, wt("ab_w_out", 0), layer=0, tm=tm, tn=512, out_dtype=BF16, name="ab_out_dx",
                  comm=cm("ab_out_dx"))
    plan.grad_ready("ab_w_out", 0, _mm_tn(yab, dx, shards=None, tk=1024, tn=1024, tt=tt, name="ab_out_dw",
                                          comm=cm("ab_out_dw")))
    (dza, dcb, sg["a_ln_g"], sg["a_ln_b"], dws, dbs, sg["b_ln_g"], sg["b_ln_b"]) = _mixer_ab_bwd_pre(
        z_ab, cb, dyab, small["a_ln_g"], small["a_ln_b"], w_s, b_s, small["b_ln_g"], small["b_ln_b"],
        tm=tm_e, name="mixer_ab_bwd", comm=cm("mixer_ab_bwd"))
    dz_ab, dbcw, sg["b_conv_b"] = _mixer_b_conv_bwd(z_ab, dcb, b_conv_w, dza, tm=tm_e, name="mixer_b_conv_bwd",
                                                    comm=cm("mixer_b_conv_bwd"))
    sg["a_w_s"] = dws.reshape(1, A_HEADS, CHUNK, CHUNK)
    sg["a_b_s"] = dbs.reshape(1, A_HEADS, CHUNK)
    sg["b_conv_w"] = dbcw.reshape(1, B_CONV, D_B)
    plan.grad_ready("ab_w_in", 0, _mm_tn(h_m0, dz_ab, shards=N_CHIPS, tk=1024, tn=512, tt=tt, name="ab_in_dw",
                                         comm=cm("ab_in_dw")))
    dx, dnm0 = _mm_nt_norm(dz_ab, wt("ab_w_in", 0), x, nm, dx, g_layer=0, tm=tm_f, name="ab_in_dx",
                           comm=cm("ab_in_dx"))

    sg["norm_mix"] = [dnm0, dnm1]
    sg["norm_ffn"] = [dnf0, dnf1]
    sg["f_conv_w"] = [dfc0, dfc1]
    return loss, dx, sg


BLOCK_BYTES = 3 * 1024 * 1024


BF16_SUBLANES = 16


def _row_tile(rows, row_bytes, step=SUBLANES):
    best = None
    for tr in range(step, rows + 1, step):
        if rows % tr == 0 and tr * row_bytes <= BLOCK_BYTES:
            best = tr
    if best is None:
        raise ValueError(f"no row tile for {rows}")
    return best


def _place_scalars():
    x, y, c = lax.axis_index("x"), lax.axis_index("y"), lax.axis_index("c")
    return jnp.stack([c, 2 * x + y, 2 * (1 - x) + y, 2 * x + (1 - y), 2 * (1 - x) + (1 - y)]).astype(jnp.int32)


def _cast_into_slot(w, place, *, layer, paired, name):
    L, rows, cols = w.shape
    tr = _row_tile(rows, cols * 4, BF16_SUBLANES)

    def body(place_ref, w_ref, o_ref):
        o_ref[...] = w_ref[...].astype(BF16)

    if paired:
        out_spec = pl.BlockSpec((None, None, tr, cols), lambda i, p: (0, p[1] // 2, i, p[1] % 2))
        out_shape = jax.ShapeDtypeStruct((1, N_CHIPS // 2, rows, 2 * cols), BF16)
    else:
        out_spec = pl.BlockSpec((None, None, tr, cols), lambda i, p: (0, p[1], i, 0))
        out_shape = jax.ShapeDtypeStruct((1, N_CHIPS, rows, cols), BF16)
    return pl.pallas_call(
        body, name=name,
        grid_spec=pltpu.PrefetchScalarGridSpec(
            num_scalar_prefetch=1, grid=(rows // tr,),
            in_specs=[pl.BlockSpec((None, tr, cols), lambda i, p: (layer, i, 0))],
            out_specs=out_spec),
        out_shape=out_shape,
        compiler_params=_cparams("parallel"),
    )(place, w)


def _pair_sum(g, theirs, place, *, name):
    S, rows, cols = g.shape
    half = rows // 2
    tr = _row_tile(half, cols * 4, BF16_SUBLANES)
    nb = half // tr

    def body(place_ref, g_ref, t_ref, o_ref):
        o_ref[...] = (g_ref[...] + t_ref[...]).astype(BF16)

    spec = pl.BlockSpec((None, tr, cols), lambda s, i, p: (s, i, 0))
    return pl.pallas_call(
        body, name=name,
        grid_spec=pltpu.PrefetchScalarGridSpec(
            num_scalar_prefetch=1, grid=(S, nb),
            in_specs=[pl.BlockSpec((None, tr, cols), lambda s, i, p: (s, p[0] * nb + i, 0)), spec],
            out_specs=spec),
        out_shape=jax.ShapeDtypeStruct((S, half, cols), BF16),
        compiler_params=_cparams("parallel", "parallel"),
    )(place, g, theirs)


def _chip_sum(p, r, g_prev, place, *, layer, shape, name):
    L, rows, cols = shape
    half = rows // 2
    tr = _row_tile(half, cols * 4, BF16_SUBLANES)
    nb = half // tr

    def body(place_ref, p_ref, r_ref, *rest):
        o_ref = rest[-1]
        mine = p_ref[...].astype(F32)
        peers = [r_ref[j].astype(F32) for j in range(3)]
        acc = None
        for s in range(N_CHIPS):
            term = jnp.where(place_ref[1] == s, mine,
                             jnp.where(place_ref[2] == s, peers[0],
                                       jnp.where(place_ref[3] == s, peers[1], peers[2])))
            acc = term if acc is None else acc + term
        o_ref[...] = acc

    in_specs = [pl.BlockSpec((None, tr, cols), lambda i, pr: (pr[1], i, 0)),
                pl.BlockSpec((3, tr, cols), lambda i, pr: (0, i, 0))]
    args = [place, p, r]
    aliases = {}
    if g_prev is not None:
        in_specs.append(HBM_REF)
        args.append(g_prev)
        aliases = {3: 0}
    return pl.pallas_call(
        body, name=name,
        grid_spec=pltpu.PrefetchScalarGridSpec(
            num_scalar_prefetch=1, grid=(nb,), in_specs=in_specs,
            out_specs=pl.BlockSpec((None, tr, cols), lambda i, pr: (layer, pr[0] * nb + i, 0))),
        out_shape=jax.ShapeDtypeStruct(shape, F32), input_output_aliases=aliases,
        compiler_params=_cparams("parallel"),
    )(*args)


def _adamw_math(w, g, m, v):
    m2 = ADAM_B1 * m + (1.0 - ADAM_B1) * g
    v2 = ADAM_B2 * v + (1.0 - ADAM_B2) * (g * g)
    m_hat = m2 / (1.0 - ADAM_B1 ** ADAM_STEP)
    v_hat = v2 / (1.0 - ADAM_B2 ** ADAM_STEP)
    delta = -ADAM_LR * (m_hat / (jnp.sqrt(v_hat) + ADAM_EPS) + ADAM_WD * w)
    return delta, m2, v2


def _adamw(w, g, m, v, *, name):
    L, rows, cols = w.shape
    tr = _row_tile(rows, cols * 4)

    def body(w_ref, g_ref, m_ref, v_ref, go_ref, d_ref, m2_ref, v2_ref):
        g = g_ref[...]
        d, m2, v2 = _adamw_math(w_ref[...], g, m_ref[...], v_ref[...])
        go_ref[...] = g
        d_ref[...] = d
        m2_ref[...] = m2
        v2_ref[...] = v2

    spec = pl.BlockSpec((None, tr, cols), lambda l, i: (l, i, 0))
    shape = jax.ShapeDtypeStruct(w.shape, F32)
    return pl.pallas_call(
        body, name=name, grid=(L, rows // tr), in_specs=[spec] * 4, out_specs=[spec] * 4,
        out_shape=[shape] * 4,
        compiler_params=_cparams("parallel", "parallel"),
    )(w, g, m, v)


def _allreduce_pack(pack, *, name, comm):
    R = pack.shape[0]
    half = R // 2
    nr, nw = len(comm.reads), len(comm.writes)

    def body(*refs):
        p_ref, rd, wr_in = refs[0], refs[1:1 + nr], refs[1 + nr:1 + nr + nw]
        o_ref, wr_out = refs[1 + nr + nw], refs[2 + nr + nw:2 + nr + 2 * nw]
        sib_ref, chip_ref, parts_ref, sems, comm_sems = refs[2 + nr + 2 * nw:]
        src = dict(zip(comm.reads, rd))
        src.update(zip(comm.writes, wr_in))
        dst = dict(zip(comm.writes, wr_out))
        comm.start(src, dst, comm_sems)
        x, y, c, k, sib, peers = _place()
        swap = _remote(p_ref, sib_ref, sems.at[0, 0], sems.at[0, 1], sib)
        swap.start()
        swap.wait()
        chip_ref[...] = p_ref[...] + sib_ref[...]
        mine = chip_ref.at[pl.ds(pl.multiple_of(c * half, SUBLANES), half)]
        sends = [_remote(mine, parts_ref.at[j], sems.at[1 + j, 0], sems.at[1 + j, 1], (px, py, c))
                 for j, (px, py) in enumerate(peers)]
        for rc in sends:
            rc.start()
        for rc in sends:
            rc.wait()
        own = mine[...]
        others = [parts_ref[j] for j in range(3)]
        acc = None
        for s in range(N_CHIPS):
            term = own
            for j, (px, py) in enumerate(peers):
                term = jnp.where(2 * px + py == s, others[j], term)
            acc = term if acc is None else acc + term
        done = o_ref.at[pl.ds(pl.multiple_of(c * half, SUBLANES), half)]
        done[...] = acc
        theirs = o_ref.at[pl.ds(pl.multiple_of((1 - c) * half, SUBLANES), half)]
        share = _remote(done, done, sems.at[4, 0], sems.at[4, 1], sib)
        share.start()
        _remote(done, theirs, sems.at[4, 0], sems.at[4, 1], sib).wait()
        comm.finish(src, dst, comm_sems)

    vm = pl.BlockSpec(memory_space=pltpu.VMEM)
    operands, shapes = _comm_operands(comm)
    outs = pl.pallas_call(
        body, name=name, in_specs=[vm] + [HBM_REF] * (nr + nw), out_specs=[vm] + [HBM_REF] * nw,
        out_shape=[jax.ShapeDtypeStruct((R, LANES), F32)] + shapes,
        input_output_aliases={1 + nr + q: 1 + q for q in range(nw)},
        scratch_shapes=[pltpu.VMEM((R, LANES), F32), pltpu.VMEM((R, LANES), F32),
                        pltpu.VMEM((3, half, LANES), F32), pltpu.SemaphoreType.DMA((5, 2)),
                        pltpu.SemaphoreType.DMA((comm.ncopies, 2))],
        compiler_params=pltpu.CompilerParams(vmem_limit_bytes=VMEM_BYTES_MAX),
    )(pack, *operands)
    for q, n in enumerate(comm.writes):
        comm.plan.bufs[n] = outs[1 + q]
    return outs[0]


PACK_UNIT = SUBLANES * LANES


def _pack(arrays):
    flat, sizes = [], []
    for a in arrays:
        pieces = a if isinstance(a, (list, tuple)) else [a]
        v = jnp.concatenate([p.reshape(-1) for p in pieces]) if len(pieces) > 1 else pieces[0].reshape(-1)
        size = v.shape[0]
        padded = -(-size // PACK_UNIT) * PACK_UNIT
        flat.append(jnp.pad(v, (0, padded - size)))
        sizes.append((size, padded))
    total = sum(p for _, p in sizes)
    if (total // PACK_UNIT) % 2:
        flat.append(jnp.zeros((PACK_UNIT,), F32))
    return jnp.concatenate(flat).reshape(-1, LANES), sizes


def _unpack(pack, sizes, shapes):
    v = pack.reshape(-1)
    out, off = [], 0
    for (size, padded), shape in zip(sizes, shapes):
        out.append(v[off:off + size].reshape(shape))
        off += padded
    return out


BIG = ("ab_w_in", "ab_w_out", "c_w_in", "c_w_out", "f_w_up", "f_w_down")
COL_SHARDED = ("ab_w_in", "c_w_in", "f_w_up")
PAIRED = ("f_w_up",)
SMALL_REPLICATED = ("norm_mix", "norm_ffn", "norm_final", "a_ln_g", "a_ln_b", "a_w_s", "a_b_s",
                    "b_conv_b", "b_ln_g", "b_ln_b")
SMALL_SHARDED = ("b_conv_w", "c_conv_w", "f_conv_w")
SMALL = SMALL_REPLICATED + SMALL_SHARDED
ALL_WEIGHTS = ("norm_mix", "norm_ffn", "norm_final", "ab_w_in", "a_ln_g", "a_ln_b", "a_w_s", "a_b_s",
               "b_conv_w", "b_conv_b", "b_ln_g", "b_ln_b", "ab_w_out", "c_w_in", "c_conv_w", "c_w_out",
               "f_w_up", "f_conv_w", "f_w_down")


SCHEDULE = {
    "ab_in": [("gi", "f_w_up", 0, 0, 4), ("gi", "ab_w_out", 0)],
    "mixer_ab": [("gd", "f_w_up", 0, 0, 4), ("gd", "ab_w_out", 0), ("gi", "f_w_up", 0, 1, 4),
                 ("gi", "f_w_up", 0, 2, 4), ("gi", "f_w_up", 0, 3, 4)],
    "ab_out": [("gd", "f_w_up", 0, 1, 4), ("gd", "f_w_up", 0, 2, 4), ("gd", "f_w_up", 0, 3, 4)],
    "ffn_up0": [("gi", "f_w_down", 0), ("gi", "c_w_in", 0, 0, 2)],
    "ffn_act0": [("gd", "f_w_down", 0), ("gd", "c_w_in", 0, 0, 2), ("gi", "c_w_in", 0, 1, 2),
                 ("gi", "f_w_up", 1, 0, 4)],
    "ffn_down0": [("gd", "c_w_in", 0, 1, 2), ("gd", "f_w_up", 1, 0, 4), ("gi", "f_w_up", 1, 1, 4),
                  ("gi", "f_w_up", 1, 2, 4)],
    "c_in": [("gd", "f_w_up", 1, 1, 4), ("gd", "f_w_up", 1, 2, 4), ("gi", "f_w_up", 1, 3, 4),
             ("gi", "c_w_out", 0)],
    "mixer_c": [("gd", "f_w_up", 1, 3, 4), ("gd", "c_w_out", 0), ("gi", "f_w_down", 1, 0, 2)],
    "c_out": [("gd", "f_w_down", 1, 0, 2), ("gi", "f_w_down", 1, 1, 2)],
    "ffn_up1": [("gd", "f_w_down", 1, 1, 2)],
    "ffn_act_bwd1": [("px", "f_w_down", 1)],
    "ffn_up_dx1": [("cx", "f_w_down", 1)],
    "c_out_dx": [("px", "f_w_up", 1, 0, 2)],
    "c_out_dw": [("px", "f_w_up", 1, 1, 2)],
    "mixer_c_bwd": [("cx", "f_w_up", 1, 0, 4), ("px", "c_w_out", 0)],
    "c_in_dw": [("cx", "f_w_up", 1, 1, 4)],
    "c_in_dx": [("cx", "f_w_up", 1, 2, 4), ("px", "c_w_in", 0)],
    "ffn_down_dx0": [("cx", "f_w_up", 1, 3, 4), ("cx", "c_w_out", 0)],
    "ffn_down_dw0": [("cx", "c_w_in", 0, 0, 2)],
    "ffn_act_bwd0": [("cx", "c_w_in", 0, 1, 2), ("px", "f_w_down", 0)],
    "ffn_up_dx0": [("cx", "f_w_down", 0)],
    "ffn_up_dw0": [("ps", "f_w_down", 1), ("ps", "f_w_up", 1)],
    "ab_out_dx": [("px", "f_w_up", 0, 0, 2)],
    "ab_out_dw": [("px", "f_w_up", 0, 1, 2)],
    "mixer_ab_bwd": [("cx", "f_w_up", 0, 0, 4), ("px", "ab_w_out", 0)],
    "mixer_b_conv_bwd": [("cx", "f_w_up", 0, 1, 4), ("cx", "ab_w_out", 0), ("ps", "c_w_out", 0),
                         ("ps", "c_w_in", 0), ("ps", "f_w_down", 0)],
    "ab_in_dw": [("cx", "f_w_up", 0, 2, 4)],
    "ab_in_dx": [("cx", "f_w_up", 0, 3, 4), ("px", "ab_w_in", 0)],
}


class _Plan:
    def __init__(self, shapes, place):
        self.shapes, self.place, self.bufs = shapes, place, {}
        self.summed, self.shared = set(), set()

    def weight(self, name, layer):
        g = self.bufs[f"w:{name}:{layer}"]
        if name in COL_SHARDED:
            return g
        _, S, rows, cols = g.shape
        return g.reshape(1, S * rows, cols)

    def grad_ready(self, name, layer, g):
        _, rows, cols = self.shapes[name]
        hbm = lambda a: pltpu.with_memory_space_constraint(a, pltpu.HBM)
        self.bufs[f"g:{name}:{layer}"] = g.reshape(N_CHIPS, rows, cols)
        self.bufs[f"t:{name}:{layer}"] = hbm(lax.empty((N_CHIPS, rows // 2, cols), F32))
        self.bufs[f"l:{name}:{layer}"] = hbm(lax.empty((3, rows // 2, cols), BF16))

    def job(self, kind, name, layer, part=0, parts=1):
        _, rows, cols = self.shapes[name]
        key = f"{name}:{layer}"
        if kind == "gi":
            return _job_gather_ici("w:" + key, rows, cols, part, parts)
        if kind == "gd":
            return _job_gather_d2d("w:" + key, rows, cols, part, parts)
        if kind == "px":
            return _job_pair_exchange("g:" + key, "t:" + key, rows, part, parts)
        if kind == "cx":
            if "p:" + key not in self.bufs:
                self.bufs["p:" + key] = _pair_sum(self.bufs["g:" + key], self.bufs["t:" + key], self.place,
                                                  name=f"pair_sum_{name}{layer}")
            nr = rows // 2 // parts
            return _job_chip_exchange("p:" + key, "l:" + key, part * nr, nr)
        if kind == "ps":
            self.chip_sum(name, layer)
            self.shared.add(key)
            return _job_pair_share("G:" + name, layer, rows)
        raise ValueError(kind)

    def chip_sum(self, name, layer):
        key = f"{name}:{layer}"
        if key not in self.summed:
            self.summed.add(key)
            self.bufs["G:" + name] = _chip_sum(self.bufs["p:" + key], self.bufs["l:" + key],
                                               self.bufs.get("G:" + name), self.place, layer=layer,
                                               shape=self.shapes[name], name=f"chip_sum_{name}{layer}")

    def comm(self, call):
        specs = SCHEDULE.get(call)
        return None if specs is None else _Comm(self, [self.job(*spec) for spec in specs])


def _step(x, tgt, w, m, v):
    chip = 2 * lax.axis_index("x") + lax.axis_index("y")
    place = _place_scalars()
    plan = _Plan({n: w[n].shape for n in BIG}, place)
    items = [(n, l) for n in BIG for l in range(w[n].shape[0])]

    for n, l in items:
        plan.bufs[f"w:{n}:{l}"] = _cast_into_slot(w[n], place, layer=l, paired=n in PAIRED, name=f"cast_{n}{l}")
    conv_pack, conv_sizes = _pack([w[n] for n in SMALL_SHARDED])
    hbm = lambda a: pltpu.with_memory_space_constraint(a, pltpu.HBM)
    plan.bufs["conv:mine"] = hbm(conv_pack)
    plan.bufs["conv:all"] = hbm(lax.empty((N_CHIPS,) + conv_pack.shape, F32))
    _comm_only(plan, [[plan.job("gi", "ab_w_in", 0), _job_chip_gather("conv:mine", "conv:all")],
                      [plan.job("gd", "ab_w_in", 0)]], name="gather_first")
    conv_shapes = [w[n].shape for n in SMALL_SHARDED]
    per_chip = [_unpack(plan.bufs["conv:all"][s], conv_sizes, conv_shapes) for s in range(N_CHIPS)]
    small = {n: w[n] for n in SMALL_REPLICATED}
    for idx, n in enumerate(SMALL_SHARDED):
        small[n] = jnp.concatenate([jnp.where(chip == s, w[n], per_chip[s][idx]) for s in range(N_CHIPS)], axis=-1)

    loss, dx, sg = _local_step(x, tgt, small, plan)

    g_pack, g_sizes = _pack([sg[n] for n in SMALL] + [loss])
    g_sum = _allreduce_pack(g_pack, name="allreduce_small_grads",
                            comm=_Comm(plan, [plan.job("cx", "ab_w_in", 0)]))
    full_shapes = [small[n].shape for n in SMALL]
    *summed, loss = _unpack(g_sum, g_sizes, full_shapes + [(1, 1)])
    g_small = dict(zip(SMALL, summed))
    for n in SMALL_SHARDED:
        width = w[n].shape[-1]
        g_small[n] = lax.dynamic_slice_in_dim(g_small[n], chip * width, width, axis=g_small[n].ndim - 1)

    _comm_only(plan, [[plan.job("ps", n, l) for n, l in items if f"{n}:{l}" not in plan.shared]],
               name="reduce_pair_share")
    grads_big = [plan.bufs["G:" + n] for n in BIG]

    grad, delta, new_m, new_v = {}, {}, {}, {}
    for n, g in zip(BIG, grads_big):
        grad[n], delta[n], new_m[n], new_v[n] = _adamw(w[n], g, m[n], v[n], name=f"adamw_{n}")
    shapes = [w[n].shape for n in SMALL]
    wp, sizes = _pack([w[n] for n in SMALL])
    gp, _ = _pack([g_small[n] for n in SMALL])
    mp, _ = _pack([m[n] for n in SMALL])
    vp, _ = _pack([v[n] for n in SMALL])
    R = wp.shape[0]
    _, dp, m2p, v2p = _adamw(wp.reshape(1, R, LANES), gp.reshape(1, R, LANES), mp.reshape(1, R, LANES),
                             vp.reshape(1, R, LANES), name="adamw_small")
    for n, d_, m_, v_ in zip(SMALL, _unpack(dp, sizes, shapes), _unpack(m2p, sizes, shapes),
                             _unpack(v2p, sizes, shapes)):
        grad[n] = g_small[n]
        delta[n], new_m[n], new_v[n] = d_, m_, v_
    return loss, dx, grad, delta, new_m, new_v


def kernel(x, norm_mix, norm_ffn, norm_final, ab_w_in, a_ln_g, a_ln_b, a_w_s, a_b_s, b_conv_w, b_conv_b, b_ln_g, b_ln_b, ab_w_out, c_w_in, c_conv_w, c_w_out, f_w_up, f_conv_w, f_w_down, loss_target, m_norm_mix, m_norm_ffn, m_norm_final, m_ab_w_in, m_a_ln_g, m_a_ln_b, m_a_w_s, m_a_b_s, m_b_conv_w, m_b_conv_b, m_b_ln_g, m_b_ln_b, m_ab_w_out, m_c_w_in, m_c_conv_w, m_c_w_out, m_f_w_up, m_f_conv_w, m_f_w_down, v_norm_mix, v_norm_ffn, v_norm_final, v_ab_w_in, v_a_ln_g, v_a_ln_b, v_a_w_s, v_a_b_s, v_b_conv_w, v_b_conv_b, v_b_ln_g, v_b_ln_b, v_ab_w_out, v_c_w_in, v_c_conv_w, v_c_w_out, v_f_w_up, v_f_conv_w, v_f_w_down):
    given = dict(locals())
    w = {n: given[n] for n in ALL_WEIGHTS}
    m = {n: given["m_" + n] for n in ALL_WEIGHTS}
    v = {n: given["v_" + n] for n in ALL_WEIGHTS}
    T = x.shape[1]
    loss, dx, grad, delta, new_m, new_v = _step(x.reshape(T, D_MODEL), loss_target.reshape(T, D_MODEL), w, m, v)
    out = [loss[0, 0], dx.reshape(x.shape)]
    for d in (grad, delta, new_m, new_v):
        out += [d[n] for n in ALL_WEIGHTS]
    return tuple(out)
```

```python
import functools
import math

import jax
import jax.numpy as jnp
from jax import lax
from jax.experimental import pallas as pl
from jax.experimental.pallas import tpu as pltpu

F32 = jnp.float32
BF16 = jnp.bfloat16

EPS = 1e-6
D_MODEL = 1024
CHUNK = 128
HEAD_DIM = 128
A_HEADS = 4
D_A = 512
D_B = 512
B_CONV = 31
C_CONV = 3
D_FF = 2816
F_CONV = 3
N_CHIPS = 4

ADAM_LR = 0.001
ADAM_B1 = 0.9
ADAM_B2 = 0.999
ADAM_EPS = 1e-08
ADAM_WD = 0.01
ADAM_STEP = 10

SUBLANES = 8
LANES = 128
HALO_SHORT = 16
HALO_LONG = 32
VMEM_BYTES_MAX = 60000 * 1024

INV_SQRT2 = 1.0 / math.sqrt(2.0)
INV_SQRT_2PI = 1.0 / math.sqrt(2.0 * math.pi)

MESH = pl.DeviceIdType.MESH


def _cparams(*sem):
    return pltpu.CompilerParams(dimension_semantics=sem, vmem_limit_bytes=VMEM_BYTES_MAX)


def _pick(total, pref):
    for c in (2048, 1024, 512, 256, 128):
        if c <= pref and total % c == 0:
            return c
    raise ValueError(f"no tile for {total}")


def _sigmoid(x):
    return jax.nn.sigmoid(x)


def _silu(x):
    return x * _sigmoid(x)


def _dsilu(x):
    s = _sigmoid(x)
    return s * (1.0 + x * (1.0 - s))


def _gelu(x):
    return 0.5 * x * (1.0 + lax.erf(x * INV_SQRT2))


def _dgelu(x):
    return 0.5 * (1.0 + lax.erf(x * INV_SQRT2)) + x * jnp.exp(-0.5 * x * x) * INV_SQRT_2PI


def _ln_stats(x):
    mu = jnp.mean(x, axis=-1, keepdims=True)
    xc = x - mu
    var = jnp.mean(xc * xc, axis=-1, keepdims=True)
    r = lax.rsqrt(var + EPS)
    return xc * r, r


def _ln_bwd(dy, xh, r, g):
    dxh = dy * g
    m1 = jnp.mean(dxh, axis=-1, keepdims=True)
    m2 = jnp.mean(dxh * xh, axis=-1, keepdims=True)
    return r * (dxh - m1 - xh * m2)


def _rowsum(x):
    return jnp.sum(x, axis=0, keepdims=True)


HBM_REF = pl.BlockSpec(memory_space=pltpu.HBM)


def _place():
    x, y, c = lax.axis_index("x"), lax.axis_index("y"), lax.axis_index("c")
    peers = [(1 - x, y), (x, 1 - y), (1 - x, 1 - y)]
    return x, y, c, 2 * x + y, (x, y, 1 - c), peers


def _half(rows, which):
    return pl.ds(which * (rows // 2), rows // 2)


def _remote(src, dst, send_sem, recv_sem, device):
    return pltpu.make_async_remote_copy(src_ref=src, dst_ref=dst, send_sem=send_sem, recv_sem=recv_sem,
                                        device_id=device, device_id_type=MESH)


class _Job:
    def __init__(self, reads, writes, ncopies, copies):
        self.reads, self.writes, self.ncopies, self.copies = reads, writes, ncopies, copies


def _share(rows, which, part, parts):
    nr = rows // 2 // parts
    return pl.ds(which * (rows // 2) + part * nr, nr)


def _slot(ref, chip, rows, cols):
    if ref.shape[1] == N_CHIPS:
        return ref.at[0, chip, rows]
    return ref.at[0, chip // 2, rows, pl.ds(pl.multiple_of((chip % 2) * cols, LANES), cols)]


def _job_gather_ici(name, rows, cols, part, parts):
    def copies(src, dst, sem):
        x, y, c, k, sib, peers = _place()
        mine_rows = _share(rows, c, part, parts)
        out = []
        for j, (px, py) in enumerate(peers):
            mine = _slot(src[name], k, mine_rows, cols)
            out.append((_remote(mine, _slot(dst[name], k, mine_rows, cols), sem(j, 0), sem(j, 1), (px, py, c)),
                        _remote(mine, _slot(dst[name], 2 * px + py, mine_rows, cols), sem(j, 0), sem(j, 1),
                                (px, py, c))))
        return out
    return _Job([], [name], 3, copies)


def _job_gather_d2d(name, rows, cols, part, parts):
    def copies(src, dst, sem):
        x, y, c, k, sib, peers = _place()
        out = []
        for j, (px, py) in enumerate(peers):
            mine_rows, their_rows = _share(rows, c, part, parts), _share(rows, 1 - c, part, parts)
            landed = _slot(src[name], 2 * px + py, mine_rows, cols)
            out.append((_remote(landed, _slot(dst[name], 2 * px + py, mine_rows, cols), sem(j, 0), sem(j, 1), sib),
                        _remote(landed, _slot(dst[name], 2 * px + py, their_rows, cols), sem(j, 0), sem(j, 1), sib)))
        return out
    return _Job([], [name], 3, copies)


def _job_chip_gather(sname, dname):
    def copies(src, dst, sem):
        x, y, c, k, sib, peers = _place()
        return [(_remote(src[sname], dst[dname].at[k], sem(j, 0), sem(j, 1), (px, py, c)),
                 _remote(src[sname], dst[dname].at[2 * px + py], sem(j, 0), sem(j, 1), (px, py, c)))
                for j, (px, py) in enumerate(peers)]
    return _Job([sname], [dname], 3, copies)


def _job_pair_exchange(gname, tname, rows, part, parts):
    nr = rows // 2 // parts

    def copies(src, dst, sem):
        x, y, c, k, sib, peers = _place()
        cp = _remote(src[gname].at[:, _share(rows, 1 - c, part, parts), :],
                     dst[tname].at[:, pl.ds(part * nr, nr), :], sem(0, 0), sem(0, 1), sib)
        return [(cp, cp)]
    return _Job([gname], [tname], 1, copies)


def _job_chip_exchange(pname, lname, r0, nr):
    def copies(src, dst, sem):
        x, y, c, k, sib, peers = _place()
        out = []
        for j, (px, py) in enumerate(peers):
            cp = _remote(src[pname].at[2 * px + py, pl.ds(r0, nr)], dst[lname].at[j, pl.ds(r0, nr)],
                         sem(j, 0), sem(j, 1), (px, py, c))
            out.append((cp, cp))
        return out
    return _Job([pname], [lname], 3, copies)


def _job_pair_share(name, layer, rows):
    def copies(src, dst, sem):
        x, y, c, k, sib, peers = _place()
        mine = src[name].at[layer, _half(rows, c)]
        return [(_remote(mine, dst[name].at[layer, _half(rows, c)], sem(0, 0), sem(0, 1), sib),
                 _remote(mine, dst[name].at[layer, _half(rows, 1 - c)], sem(0, 0), sem(0, 1), sib))]
    return _Job([], [name], 1, copies)


class _Comm:
    def __init__(self, plan, jobs):
        self.plan, self.jobs = plan, jobs
        self.writes, self.reads = [], []
        for job in jobs:
            for n in job.writes:
                if n not in self.writes:
                    self.writes.append(n)
        for job in jobs:
            for n in job.reads:
                if n not in self.writes and n not in self.reads:
                    self.reads.append(n)
        self.ncopies = sum(job.ncopies for job in jobs)

    def descriptors(self, src, dst, sems, base):
        out = []
        for job in self.jobs:
            sem = lambda j, which, base=base: sems.at[base + j, which]
            out += job.copies(src, dst, sem)
            base += job.ncopies
        return out

    def start(self, src, dst, sems, base=0):
        for first, _ in self.descriptors(src, dst, sems, base):
            first.start()

    def finish(self, src, dst, sems, base=0):
        for _, landed in self.descriptors(src, dst, sems, base):
            landed.wait()


def _comm_operands(comm):
    bufs = comm.plan.bufs
    shapes = [jax.ShapeDtypeStruct(bufs[n].shape, bufs[n].dtype) for n in comm.writes]
    return [bufs[n] for n in comm.reads] + [bufs[n] for n in comm.writes], shapes


def _pallas(comm, body, *, name, grid, in_specs, out_specs, out_shape, compiler_params, scratch_shapes=(),
            aliases=None):
    aliases = dict(aliases or {})
    if comm is None:
        return pl.pallas_call(body, name=name, grid=grid, in_specs=in_specs, out_specs=out_specs,
                              out_shape=out_shape, scratch_shapes=list(scratch_shapes),
                              input_output_aliases=aliases, compiler_params=compiler_params)
    single = not isinstance(out_shape, (list, tuple))
    base_specs = [out_specs] if single else list(out_specs)
    base_shape = [out_shape] if single else list(out_shape)
    nb, nr, nw, nbo, nsc = len(in_specs), len(comm.reads), len(comm.writes), len(base_specs), len(scratch_shapes)

    def wrapped(*refs):
        base_in, rd, wr_in = refs[:nb], refs[nb:nb + nr], refs[nb + nr:nb + nr + nw]
        o0 = nb + nr + nw
        base_out, wr_out = refs[o0:o0 + nbo], refs[o0 + nbo:o0 + nbo + nw]
        scratch, sems = refs[o0 + nbo + nw:o0 + nbo + nw + nsc], refs[-1]
        src = dict(zip(comm.reads, rd))
        src.update(zip(comm.writes, wr_in))
        dst = dict(zip(comm.writes, wr_out))
        first = functools.reduce(jnp.logical_and, [pl.program_id(a) == 0 for a in range(len(grid))])
        last = functools.reduce(jnp.logical_and,
                                [pl.program_id(a) == pl.num_programs(a) - 1 for a in range(len(grid))])

        @pl.when(first)
        def _():
            comm.start(src, dst, sems)
        body(*base_in, *base_out, *scratch)

        @pl.when(last)
        def _():
            comm.finish(src, dst, sems)

    operands, shapes = _comm_operands(comm)
    call = pl.pallas_call(
        wrapped, name=name, grid=grid, in_specs=list(in_specs) + [HBM_REF] * (nr + nw),
        out_specs=base_specs + [HBM_REF] * nw, out_shape=base_shape + shapes,
        input_output_aliases={**aliases, **{nb + nr + q: nbo + q for q in range(nw)}},
        scratch_shapes=list(scratch_shapes) + [pltpu.SemaphoreType.DMA((comm.ncopies, 2))],
        compiler_params=compiler_params)

    def run(*args):
        outs = call(*args, *operands)
        for q, n in enumerate(comm.writes):
            comm.plan.bufs[n] = outs[nbo + q]
        return outs[0] if single else list(outs[:nbo])

    return run


def _comm_only(plan, phases, *, name):
    comms = [_Comm(plan, jobs) for jobs in phases]
    both = _Comm(plan, [job for jobs in phases for job in jobs])
    nr, nw = len(both.reads), len(both.writes)

    def body(*refs):
        rd, wr_in, wr_out, sems = refs[:nr], refs[nr:nr + nw], refs[nr + nw:nr + 2 * nw], refs[-1]
        src = dict(zip(both.reads, rd))
        src.update(zip(both.writes, wr_in))
        dst = dict(zip(both.writes, wr_out))
        base = 0
        for comm in comms:
            comm.start(src, dst, sems, base)
            comm.finish(src, dst, sems, base)
            base += comm.ncopies

    operands, shapes = _comm_operands(both)
    outs = pl.pallas_call(
        body, name=name, in_specs=[HBM_REF] * (nr + nw), out_specs=[HBM_REF] * nw, out_shape=shapes,
        input_output_aliases={nr + q: q for q in range(nw)},
        scratch_shapes=[pltpu.SemaphoreType.DMA((both.ncopies, 2))],
    )(*operands)
    for q, n in enumerate(both.writes):
        plan.bufs[n] = outs[q]


def _mm_nn(a, w, *, layer, tm, tn, residual=None, norm=None, out_dtype=F32, name, comm=None):
    T, K = a.shape
    if w.ndim == 4:
        _, S, _, n4 = w.shape
        N = S * n4
        bps = n4 // tn
        w_spec = pl.BlockSpec((None, None, K, tn), lambda j, i: (layer, j // bps, 0, j % bps))
    else:
        N = w.shape[2]
        w_spec = pl.BlockSpec((None, K, tn), lambda j, i: (layer, 0, j))
    in_specs = [pl.BlockSpec((tm, K), lambda j, i: (i, 0)), w_spec]
    args = [a, w]
    if residual is not None:
        in_specs.append(pl.BlockSpec((tm, tn), lambda j, i: (i, j)))
        args.append(residual)
    out_specs = pl.BlockSpec((tm, tn), lambda j, i: (i, j))
    out_shape = jax.ShapeDtypeStruct((T, N), out_dtype)
    if norm is not None:
        assert tn == N
        g, norm_layer = norm
        in_specs.append(pl.BlockSpec((None, 1, N), lambda j, i: (norm_layer, 0, 0)))
        args.append(g)
        out_specs = [out_specs, pl.BlockSpec((tm, tn), lambda j, i: (i, j))]
        out_shape = [out_shape, jax.ShapeDtypeStruct((T, N), BF16)]

    def body(*refs):
        a_ref, w_ref = refs[0], refs[1]
        acc = jnp.dot(a_ref[...].astype(BF16), w_ref[...], preferred_element_type=F32)
        if residual is not None:
            acc = refs[2][...] + acc
        if norm is None:
            refs[-1][...] = acc.astype(out_dtype)
        else:
            refs[-2][...] = acc.astype(out_dtype)
            r = lax.rsqrt(jnp.mean(acc * acc, axis=-1, keepdims=True) + EPS)
            refs[-1][...] = (acc * r * refs[-3][...]).astype(BF16)

    return _pallas(
        comm, body, name=name, grid=(N // tn, T // tm), in_specs=in_specs,
        out_specs=out_specs, out_shape=out_shape,
        compiler_params=_cparams("parallel", "parallel"),
    )(*args)


def _norm_mm_nn(x, g, w, *, g_layer, tm, tn, name, comm=None):
    T, K = x.shape
    _, S, _, n4 = w.shape
    bps = n4 // tn

    def body(x_ref, g_ref, w_ref, h_ref, o_ref):
        @pl.when(pl.program_id(1) == 0)
        def _():
            xf = x_ref[...]
            r = lax.rsqrt(jnp.mean(xf * xf, axis=-1, keepdims=True) + EPS)
            h_ref[...] = (xf * r * g_ref[...]).astype(BF16)
        o_ref[...] = jnp.dot(h_ref[...], w_ref[...], preferred_element_type=F32).astype(BF16)

    return _pallas(
        comm, body, name=name, grid=(T // tm, S * bps),
        in_specs=[pl.BlockSpec((tm, K), lambda i, j: (i, 0)),
                  pl.BlockSpec((None, 1, K), lambda i, j: (g_layer, 0, 0)),
                  pl.BlockSpec((None, None, K, tn), lambda i, j: (0, j // bps, 0, j % bps))],
        out_specs=[pl.BlockSpec((tm, K), lambda i, j: (i, 0)), pl.BlockSpec((tm, tn), lambda i, j: (i, j))],
        out_shape=[jax.ShapeDtypeStruct((T, K), BF16), jax.ShapeDtypeStruct((T, S * n4), BF16)],
        compiler_params=_cparams("parallel", "arbitrary"),
    )(x, g, w)


def _mm_nt(dy, w, *, layer, tm, tn, name, out_dtype=F32, comm=None):
    T = dy.shape[0]
    nt_dims = (((1,), (1,)), ((), ()))
    _, R, N = w.shape

    def body2(dy_ref, w_ref, o_ref):
        o_ref[...] = lax.dot_general(dy_ref[...].astype(BF16), w_ref[...], nt_dims,
                                     preferred_element_type=F32).astype(out_dtype)

    return _pallas(
        comm, body2, name=name, grid=(R // tn, T // tm),
        in_specs=[pl.BlockSpec((tm, N), lambda j, i: (i, 0)),
                  pl.BlockSpec((None, tn, N), lambda j, i: (layer, j, 0))],
        out_specs=pl.BlockSpec((tm, tn), lambda j, i: (i, j)),
        out_shape=jax.ShapeDtypeStruct((T, R), out_dtype),
        compiler_params=_cparams("parallel", "parallel"),
    )(dy, w)


def _mm_tn(a, dy, *, shards, tk, tn, tt, name, comm=None):
    T, K = a.shape
    N = dy.shape[1]
    tn_dims = (((0,), (0,)), ((), ()))
    n4 = N if shards is None else N // shards
    span = max(tn // n4, 1)

    def body(a_ref, dy_ref, o_ref):
        @pl.when(pl.program_id(2) == 0)
        def _():
            o_ref[...] = jnp.zeros_like(o_ref)
        r = lax.dot_general(a_ref[...].astype(BF16), dy_ref[...].astype(BF16), tn_dims,
                            preferred_element_type=F32)
        if span == 1:
            o_ref[...] += r
        else:
            for q in range(span):
                o_ref[q] += r[:, q * n4:(q + 1) * n4]

    if shards is None:
        out_spec = pl.BlockSpec((tk, tn), lambda k, n, t: (k, n))
        out_shape = jax.ShapeDtypeStruct((K, N), F32)
    elif span > 1:
        out_spec = pl.BlockSpec((span, tk, n4), lambda k, n, t: (n, k, 0))
        out_shape = jax.ShapeDtypeStruct((shards, K, n4), F32)
    else:
        bps = n4 // tn
        out_spec = pl.BlockSpec((None, tk, tn), lambda k, n, t: (n // bps, k, n % bps))
        out_shape = jax.ShapeDtypeStruct((shards, K, n4), F32)
    return _pallas(
        comm, body, name=name, grid=(K // tk, N // tn, T // tt),
        in_specs=[pl.BlockSpec((tt, tk), lambda k, n, t: (t, k)),
                  pl.BlockSpec((tt, tn), lambda k, n, t: (t, n))],
        out_specs=out_spec, out_shape=out_shape,
        compiler_params=_cparams("parallel", "parallel", "arbitrary"),
    )(a, dy)


def _rmsnorm_bwd_math(xf, g, dh, dres):
    r = lax.rsqrt(jnp.mean(xf * xf, axis=-1, keepdims=True) + EPS)
    xh = xf * r
    dxh = dh * g
    dx = dres + r * (dxh - xh * jnp.mean(dxh * xh, axis=-1, keepdims=True))
    return dx, _rowsum(dh * xh)


def _mm_nt_norm(dy, w, x, g, dres, *, g_layer, tm, name, comm=None):
    T = dy.shape[0]
    _, S, K, n4 = w.shape
    nt_dims = (((1,), (1,)), ((), ()))

    def body(dy_ref, w_ref, x_ref, g_ref, dres_ref, dx_ref, dg_ref):
        @pl.when(pl.program_id(0) == 0)
        def _():
            dg_ref[...] = jnp.zeros_like(dg_ref)
        dh = None
        for s in range(S):
            part = lax.dot_general(dy_ref[:, s * n4:(s + 1) * n4].astype(BF16), w_ref[s], nt_dims,
                                   preferred_element_type=F32)
            dh = part if dh is None else dh + part
        dx, dg = _rmsnorm_bwd_math(x_ref[...], g_ref[...], dh, dres_ref[...])
        dx_ref[...] = dx
        dg_ref[...] += dg

    row = lambda i: (i, 0)
    return _pallas(
        comm, body, name=name, grid=(T // tm,),
        in_specs=[pl.BlockSpec((tm, S * n4), row),
                  pl.BlockSpec((None, S, K, n4), lambda i: (0, 0, 0, 0)),
                  pl.BlockSpec((tm, K), row),
                  pl.BlockSpec((None, 1, K), lambda i: (g_layer, 0, 0)),
                  pl.BlockSpec((tm, K), row)],
        out_specs=[pl.BlockSpec((tm, K), row), pl.BlockSpec((1, K), lambda i: (0, 0))],
        out_shape=[jax.ShapeDtypeStruct((T, K), F32), jax.ShapeDtypeStruct((1, K), F32)],
        compiler_params=_cparams("arbitrary"),
    )(dy, w, x, g, dres)


def _mm_nn_loss(a, w, residual, tgt, g, *, tm, name, comm=None):
    T, K = a.shape
    D = w.shape[2]

    def body(a_ref, w_ref, res_ref, t_ref, g_ref, loss_ref, dx_ref, dg_ref):
        @pl.when(pl.program_id(0) == 0)
        def _():
            dg_ref[...] = jnp.zeros_like(dg_ref)
            loss_ref[...] = jnp.zeros_like(loss_ref)
        xf = res_ref[...] + jnp.dot(a_ref[...], w_ref[...], preferred_element_type=F32)
        gg = g_ref[...]
        r = lax.rsqrt(jnp.mean(xf * xf, axis=-1, keepdims=True) + EPS)
        xh = xf * r
        err = xh * gg - t_ref[...]
        row = jnp.mean(err * err, axis=-1, keepdims=True)
        loss_ref[...] += 0.5 * jnp.sum(row, axis=0, keepdims=True)
        dy = err * (1.0 / D)
        dg_ref[...] += _rowsum(dy * xh)
        dxh = dy * gg
        dx_ref[...] = r * (dxh - xh * jnp.mean(dxh * xh, axis=-1, keepdims=True))

    row_spec = pl.BlockSpec((tm, D), lambda i: (i, 0))
    return _pallas(
        comm, body, name=name, grid=(T // tm,),
        in_specs=[pl.BlockSpec((tm, K), lambda i: (i, 0)), pl.BlockSpec((None, K, D), lambda i: (0, 0, 0)),
                  row_spec, row_spec, pl.BlockSpec((1, D), lambda i: (0, 0))],
        out_specs=[pl.BlockSpec((1, 1), lambda i: (0, 0)), row_spec, pl.BlockSpec((1, D), lambda i: (0, 0))],
        out_shape=[jax.ShapeDtypeStruct((1, 1), F32), jax.ShapeDtypeStruct((T, D), F32),
                   jax.ShapeDtypeStruct((1, D), F32)],
        compiler_params=_cparams("arbitrary"),
    )(a, w, residual, tgt, g)


CONV_ROWS = 64
CONV_COLS = 256


def _halo_prev_index(tm, halo):
    per = tm // halo
    return lambda i: jnp.maximum(i * per - 1, 0)


def _halo_next_index(tm, halo, total):
    per = tm // halo
    last = total // halo - 1
    return lambda i: jnp.minimum((i + 1) * per, last)


def _causal_mask():
    t = lax.broadcasted_iota(jnp.int32, (CHUNK, CHUNK), 0)
    s = lax.broadcasted_iota(jnp.int32, (CHUNK, CHUNK), 1)
    return s <= t


def _mixer_ab_fwd(z, a_ln_g, a_ln_b, w_s, b_s, conv_w, conv_b, b_ln_g, b_ln_b, *, tm, name, comm=None):
    T = z.shape[0]
    nchunk = tm // CHUNK
    halo = HALO_LONG

    def body(za_ref, zb_ref, zh_ref, alg_ref, alb_ref, ws_ref, bs_ref, cw_ref, cbias_ref,
             blg_ref, blb_ref, y_ref, cb_ref, ext_ref):
        i = pl.program_id(0)
        gu = _gelu(za_ref[:, :D_A].astype(F32))
        gv = _gelu(za_ref[:, D_A:].astype(F32))
        xh, _ = _ln_stats(gv)
        lv = (xh * alg_ref[...] + alb_ref[...]).astype(BF16)
        mask = _causal_mask()
        for h in range(A_HEADS):
            wm = jnp.where(mask, ws_ref[h], 0.0).astype(BF16)
            cols = slice(h * HEAD_DIM, (h + 1) * HEAD_DIM)
            for c in range(nchunk):
                rows = slice(c * CHUNK, (c + 1) * CHUNK)
                mixed = jnp.dot(wm, lv[rows, cols], preferred_element_type=F32) + bs_ref[h]
                y_ref[rows, cols] = (gu[rows, cols] * mixed).astype(BF16)
        ext_ref[halo:halo + tm, :] = zb_ref[:, :D_B].astype(F32) * _sigmoid(zb_ref[:, D_B:].astype(F32))
        prev = zh_ref[:, :D_B].astype(F32) * _sigmoid(zh_ref[:, D_B:].astype(F32))
        ext_ref[0:halo, :] = jnp.where(i > 0, prev, 0.0)
        for rb in range(tm // CONV_ROWS):
            for cb in range(D_B // CONV_COLS):
                cs = slice(cb * CONV_COLS, (cb + 1) * CONV_COLS)
                window = ext_ref[rb * CONV_ROWS:rb * CONV_ROWS + CONV_ROWS + halo, cs]
                acc = jnp.zeros((CONV_ROWS, CONV_COLS), F32)
                for k in range(B_CONV):
                    shifted = _rows_after(window, halo - (B_CONV - 1) + k)[:CONV_ROWS]
                    acc = acc + cw_ref[k:k + 1, cs] * shifted
                cb_ref[rb * CONV_ROWS:(rb + 1) * CONV_ROWS, cs] = acc + cbias_ref[:, cs]
        xhb, _ = _ln_stats(cb_ref[...])
        y_ref[:, D_A:] = _silu(xhb * blg_ref[...] + blb_ref[...]).astype(BF16)

    row = lambda i: (i, 0)
    par = lambda i: (0, 0)
    return _pallas(
        comm, body, name=name, grid=(T // tm,),
        in_specs=[pl.BlockSpec((tm, 2 * D_A), lambda i: (i, 0)),
                  pl.BlockSpec((tm, 2 * D_B), lambda i: (i, 1)),
                  pl.BlockSpec((halo, 2 * D_B), lambda i: (_halo_prev_index(tm, halo)(i), 1)),
                  pl.BlockSpec((1, D_A), par), pl.BlockSpec((1, D_A), par),
                  pl.BlockSpec((A_HEADS, CHUNK, CHUNK), lambda i: (0, 0, 0)),
                  pl.BlockSpec((A_HEADS, CHUNK, 1), lambda i: (0, 0, 0)),
                  pl.BlockSpec((B_CONV, D_B), par), pl.BlockSpec((1, D_B), par),
                  pl.BlockSpec((1, D_B), par), pl.BlockSpec((1, D_B), par)],
        out_specs=[pl.BlockSpec((tm, D_A + D_B), row), pl.BlockSpec((tm, D_B), row)],
        out_shape=[jax.ShapeDtypeStruct((T, D_A + D_B), BF16), jax.ShapeDtypeStruct((T, D_B), F32)],
        scratch_shapes=[pltpu.VMEM((halo + tm, D_B), F32)],
        compiler_params=_cparams("parallel"),
    )(z, z, z, a_ln_g, a_ln_b, w_s, b_s, conv_w, conv_b, b_ln_g, b_ln_b)


def _mixer_ab_bwd_pre(z, cb, dy, a_ln_g, a_ln_b, w_s, b_s, b_ln_g, b_ln_b, *, tm, name, comm=None):
    T = z.shape[0]
    nchunk = tm // CHUNK
    tn_dims = (((0,), (0,)), ((), ()))
    nt_dims = (((1,), (1,)), ((), ()))

    def body(za_ref, cb_ref, dy_ref, alg_ref, alb_ref, ws_ref, bs_ref, blg_ref, blb_ref,
             dza_ref, dcb_ref, dalg_ref, dalb_ref, dws_ref, dbs_ref, dblg_ref, dblb_ref,
             dlv_ref):
        @pl.when(pl.program_id(0) == 0)
        def _():
            for ref in (dalg_ref, dalb_ref, dws_ref, dbs_ref, dblg_ref, dblb_ref):
                ref[...] = jnp.zeros_like(ref)
        ua = za_ref[:, :D_A].astype(F32)
        va = za_ref[:, D_A:].astype(F32)
        gu = _gelu(ua)
        gv = _gelu(va)
        xh, r = _ln_stats(gv)
        alg = alg_ref[...]
        lv = (xh * alg + alb_ref[...]).astype(BF16)
        dya = dy_ref[:, :D_A].astype(F32)
        mask = _causal_mask()
        for h in range(A_HEADS):
            wm = jnp.where(mask, ws_ref[h], 0.0).astype(BF16)
            cols = slice(h * HEAD_DIM, (h + 1) * HEAD_DIM)
            dwm = jnp.zeros((CHUNK, CHUNK), F32)
            dbs = jnp.zeros((CHUNK, 1), F32)
            for c in range(nchunk):
                rows = slice(c * CHUNK, (c + 1) * CHUNK)
                lvb = lv[rows, cols]
                mixed = jnp.dot(wm, lvb, preferred_element_type=F32) + bs_ref[h]
                dyb = dya[rows, cols]
                dza_ref[rows, cols] = (dyb * mixed * _dgelu(ua[rows, cols])).astype(BF16)
                dmixed = dyb * gu[rows, cols]
                dmb = dmixed.astype(BF16)
                dlv_ref[rows, cols] = lax.dot_general(wm, dmb, tn_dims, preferred_element_type=F32)
                dwm = dwm + lax.dot_general(dmb, lvb, nt_dims, preferred_element_type=F32)
                dbs = dbs + jnp.sum(dmixed, axis=1, keepdims=True)
            dws_ref[h] += jnp.where(mask, dwm, 0.0)
            dbs_ref[h] += dbs
        dlv = dlv_ref[...]
        dalg_ref[...] += _rowsum(dlv * xh)
        dalb_ref[...] += _rowsum(dlv)
        dgv = _ln_bwd(dlv, xh, r, alg)
        dza_ref[:, D_A:] = (dgv * _dgelu(va)).astype(BF16)
        xhb, rb = _ln_stats(cb_ref[...])
        blg = blg_ref[...]
        lb = xhb * blg + blb_ref[...]
        dlb = dy_ref[:, D_A:].astype(F32) * _dsilu(lb)
        dblg_ref[...] += _rowsum(dlb * xhb)
        dblb_ref[...] += _rowsum(dlb)
        dcb_ref[...] = _ln_bwd(dlb, xhb, rb, blg)

    row = lambda i: (i, 0)
    par = lambda i: (0, 0)
    par3 = lambda i: (0, 0, 0)
    return _pallas(
        comm, body, name=name, grid=(T // tm,),
        in_specs=[pl.BlockSpec((tm, 2 * D_A), row), pl.BlockSpec((tm, D_B), row),
                  pl.BlockSpec((tm, D_A + D_B), row),
                  pl.BlockSpec((1, D_A), par), pl.BlockSpec((1, D_A), par),
                  pl.BlockSpec((A_HEADS, CHUNK, CHUNK), par3),
                  pl.BlockSpec((A_HEADS, CHUNK, 1), par3),
                  pl.BlockSpec((1, D_B), par), pl.BlockSpec((1, D_B), par)],
        out_specs=[pl.BlockSpec((tm, 2 * D_A), row), pl.BlockSpec((tm, D_B), row),
                   pl.BlockSpec((1, D_A), par), pl.BlockSpec((1, D_A), par),
                   pl.BlockSpec((A_HEADS, CHUNK, CHUNK), par3),
                   pl.BlockSpec((A_HEADS, CHUNK, 1), par3),
                   pl.BlockSpec((1, D_B), par), pl.BlockSpec((1, D_B), par)],
        out_shape=[jax.ShapeDtypeStruct((T, 2 * D_A + 2 * D_B), BF16), jax.ShapeDtypeStruct((T, D_B), F32),
                   jax.ShapeDtypeStruct((1, D_A), F32), jax.ShapeDtypeStruct((1, D_A), F32),
                   jax.ShapeDtypeStruct((A_HEADS, CHUNK, CHUNK), F32),
                   jax.ShapeDtypeStruct((A_HEADS, CHUNK, 1), F32),
                   jax.ShapeDtypeStruct((1, D_B), F32), jax.ShapeDtypeStruct((1, D_B), F32)],
        scratch_shapes=[pltpu.VMEM((tm, D_A), F32)],
        compiler_params=_cparams("arbitrary"),
    )(z, cb, dy, a_ln_g, a_ln_b, w_s, b_s, b_ln_g, b_ln_b)


def _mixer_b_conv_bwd(z, dcb, conv_w, dz, *, tm, name, comm=None):
    T = z.shape[0]
    halo = HALO_LONG

    def body(zb_ref, dcb_ref, dcn_ref, cw_ref, dz_in_ref, dzb_ref, dcw_ref, dbias_ref, dext_ref):
        i = pl.program_id(0)
        last = pl.num_programs(0) - 1

        @pl.when(i == 0)
        def _():
            dcw_ref[...] = jnp.zeros_like(dcw_ref)
            dbias_ref[...] = jnp.zeros_like(dbias_ref)
        dcb = dcb_ref[...]
        dext_ref[0:tm, :] = dcb
        dext_ref[tm:tm + halo, :] = jnp.where(i < last, dcn_ref[...], 0.0)
        dbias_ref[...] += _rowsum(dcb)
        for rb in range(tm // CONV_ROWS):
            for cb in range(D_B // CONV_COLS):
                cs = slice(cb * CONV_COLS, (cb + 1) * CONV_COLS)
                gcs = slice(D_B + cb * CONV_COLS, D_B + (cb + 1) * CONV_COLS)
                rs = slice(rb * CONV_ROWS, (rb + 1) * CONV_ROWS)
                xbb = zb_ref[rs, cs].astype(F32)
                sgb = _sigmoid(zb_ref[rs, gcs].astype(F32))
                yb0 = xbb * sgb
                window = dext_ref[rb * CONV_ROWS:rb * CONV_ROWS + CONV_ROWS + halo, cs]
                acc = jnp.zeros((CONV_ROWS, CONV_COLS), F32)
                for k in range(B_CONV):
                    shifted = _rows_after(window, (B_CONV - 1) - k)[:CONV_ROWS]
                    acc = acc + cw_ref[k:k + 1, cs] * shifted
                    dcw_ref[k:k + 1, cs] += _rowsum(shifted * yb0)
                dzb_ref[rs, cs] = (acc * sgb).astype(BF16)
                dzb_ref[rs, gcs] = (acc * xbb * sgb * (1.0 - sgb)).astype(BF16)

    row = lambda i: (i, 0)
    par = lambda i: (0, 0)
    return _pallas(
        comm, body, name=name, grid=(T // tm,),
        in_specs=[pl.BlockSpec((tm, 2 * D_B), lambda i: (i, 1)),
                  pl.BlockSpec((tm, D_B), row),
                  pl.BlockSpec((halo, D_B), lambda i: (_halo_next_index(tm, halo, T)(i), 0)),
                  pl.BlockSpec((B_CONV, D_B), par), pl.BlockSpec(memory_space=pl.ANY)],
        out_specs=[pl.BlockSpec((tm, 2 * D_B), lambda i: (i, 1)), pl.BlockSpec((B_CONV, D_B), par),
                   pl.BlockSpec((1, D_B), par)],
        out_shape=[jax.ShapeDtypeStruct(dz.shape, BF16), jax.ShapeDtypeStruct((B_CONV, D_B), F32),
                   jax.ShapeDtypeStruct((1, D_B), F32)],
        scratch_shapes=[pltpu.VMEM((tm + halo, D_B), F32)], aliases={4: 0},
        compiler_params=_cparams("arbitrary"),
    )(z, dcb, dcb, conv_w, dz)


def _rows_before(x, a):
    return x if a == 0 else pltpu.roll(x, a, axis=0)


def _rows_after(x, a):
    return x if a == 0 else pltpu.roll(x, x.shape[0] - a, axis=0)


def _conv3(w_ref, x, halo, cs):
    acc = w_ref[2:3, cs] * x[halo:]
    acc = acc + w_ref[1:2, cs] * _rows_before(x, 1)[halo:]
    return acc + w_ref[0:1, cs] * _rows_before(x, 2)[halo:]


def _mixer_c_fwd(z, conv_w, *, tm, name, comm=None):
    T = z.shape[0]
    D = D_MODEL
    halo = HALO_SHORT
    W = CONV_COLS

    def body(bg_ref, cg_ref, xv_ref, cgh_ref, xvh_ref, w_ref, r_ref):
        i = pl.program_id(0)
        for cb in range(D // W):
            cs = slice(cb * W, (cb + 1) * W)
            prev = jnp.where(i > 0, cgh_ref[:, cs].astype(F32) * xvh_ref[:, cs].astype(F32), 0.0)
            p = jnp.concatenate([prev, cg_ref[:, cs].astype(F32) * xv_ref[:, cs].astype(F32)], axis=0)
            r_ref[:, cs] = (bg_ref[:, cs].astype(F32) * _conv3(w_ref, p, halo, cs)).astype(BF16)

    hp = _halo_prev_index(tm, halo)
    return _pallas(
        comm, body, name=name, grid=(T // tm,),
        in_specs=[pl.BlockSpec((tm, D), lambda i: (i, 0)), pl.BlockSpec((tm, D), lambda i: (i, 1)),
                  pl.BlockSpec((tm, D), lambda i: (i, 2)),
                  pl.BlockSpec((halo, D), lambda i: (hp(i), 1)),
                  pl.BlockSpec((halo, D), lambda i: (hp(i), 2)),
                  pl.BlockSpec((None, C_CONV, D), lambda i: (0, 0, 0))],
        out_specs=pl.BlockSpec((tm, D), lambda i: (i, 0)),
        out_shape=jax.ShapeDtypeStruct((T, D), BF16),
        compiler_params=_cparams("parallel"),
    )(z, z, z, z, z, conv_w)


def _mixer_c_bwd(z, dr, conv_w, *, tm, name, comm=None):
    T = z.shape[0]
    D = D_MODEL
    halo = HALO_SHORT
    W = CONV_COLS

    def body(bg_ref, cg_ref, xv_ref, cgh_ref, xvh_ref, bgn_ref, dr_ref, drn_ref, w_ref, dz_ref, dw_ref):
        i = pl.program_id(0)
        last = pl.num_programs(0) - 1

        @pl.when(i == 0)
        def _():
            dw_ref[...] = jnp.zeros_like(dw_ref)
        for cb in range(D // W):
            cs = slice(cb * W, (cb + 1) * W)
            cg = cg_ref[:, cs].astype(F32)
            xv = xv_ref[:, cs].astype(F32)
            dr = dr_ref[:, cs].astype(F32)
            p = cg * xv
            prev = jnp.where(i > 0, cgh_ref[:, cs].astype(F32) * xvh_ref[:, cs].astype(F32), 0.0)
            q = _conv3(w_ref, jnp.concatenate([prev, p], axis=0), halo, cs)
            dz_ref[:, cs] = (dr * q).astype(BF16)
            nxt = jnp.where(i < last, drn_ref[:, cs].astype(F32) * bgn_ref[:, cs].astype(F32), 0.0)
            dq = jnp.concatenate([dr * bg_ref[:, cs].astype(F32), nxt], axis=0)
            dp = None
            for k in range(C_CONV):
                shifted = _rows_after(dq, 2 - k)[:tm]
                term = w_ref[k:k + 1, cs] * shifted
                dp = term if dp is None else dp + term
                dw_ref[k:k + 1, cs] += _rowsum(shifted * p)
            dz_ref[:, D + cb * W:D + (cb + 1) * W] = (dp * xv).astype(BF16)
            dz_ref[:, 2 * D + cb * W:2 * D + (cb + 1) * W] = (dp * cg).astype(BF16)

    hp = _halo_prev_index(tm, halo)
    hn = _halo_next_index(tm, halo, T)
    return _pallas(
        comm, body, name=name, grid=(T // tm,),
        in_specs=[pl.BlockSpec((tm, D), lambda i: (i, 0)), pl.BlockSpec((tm, D), lambda i: (i, 1)),
                  pl.BlockSpec((tm, D), lambda i: (i, 2)),
                  pl.BlockSpec((halo, D), lambda i: (hp(i), 1)),
                  pl.BlockSpec((halo, D), lambda i: (hp(i), 2)),
                  pl.BlockSpec((halo, D), lambda i: (hn(i), 0)),
                  pl.BlockSpec((tm, D), lambda i: (i, 0)),
                  pl.BlockSpec((halo, D), lambda i: (hn(i), 0)),
                  pl.BlockSpec((None, C_CONV, D), lambda i: (0, 0, 0))],
        out_specs=[pl.BlockSpec((tm, 3 * D), lambda i: (i, 0)),
                   pl.BlockSpec((C_CONV, D), lambda i: (0, 0))],
        out_shape=[jax.ShapeDtypeStruct((T, 3 * D), BF16), jax.ShapeDtypeStruct((C_CONV, D), F32)],
        compiler_params=_cparams("arbitrary"),
    )(z, z, z, z, z, z, dr, dr, conv_w)


FFN_COLS = 128


def _ffn_act_fwd(up, conv_w, *, layer, tm, name, comm=None):
    T = up.shape[0]
    halo = HALO_SHORT
    W = FFN_COLS

    def body(up_ref, uph_ref, w_ref, a_ref, upc_ref):
        i = pl.program_id(0)

        def conv(cs):
            prev = jnp.where(i > 0, uph_ref[:, cs], jnp.zeros((halo, W), BF16))
            return _conv3(w_ref, jnp.concatenate([prev, up_ref[:, cs]], axis=0).astype(F32), halo, cs)

        for cb in range(D_FF // W):
            gs = slice(cb * W, (cb + 1) * W)
            vs = slice(D_FF + cb * W, D_FF + (cb + 1) * W)
            g = conv(gs)
            v = conv(vs)
            upc_ref[:, gs] = g.astype(BF16)
            upc_ref[:, vs] = v.astype(BF16)
            a_ref[:, gs] = (_silu(g) * v).astype(BF16)

    return _pallas(
        comm, body, name=name, grid=(T // tm,),
        in_specs=[pl.BlockSpec((tm, 2 * D_FF), lambda i: (i, 0)),
                  pl.BlockSpec((halo, 2 * D_FF), lambda i: (_halo_prev_index(tm, halo)(i), 0)),
                  pl.BlockSpec((None, F_CONV, 2 * D_FF), lambda i: (layer, 0, 0))],
        out_specs=[pl.BlockSpec((tm, D_FF), lambda i: (i, 0)),
                   pl.BlockSpec((tm, 2 * D_FF), lambda i: (i, 0))],
        out_shape=[jax.ShapeDtypeStruct((T, D_FF), BF16), jax.ShapeDtypeStruct((T, 2 * D_FF), BF16)],
        compiler_params=_cparams("parallel"),
    )(up, up, conv_w)


def _ffn_act_bwd(up, upc, da, conv_w, *, layer, tm, name, comm=None):
    T = up.shape[0]
    halo = HALO_SHORT
    W = FFN_COLS

    def body(up_ref, upc_ref, upcn_ref, da_ref, dan_ref, w_ref, dup_ref, dw_ref):
        i = pl.program_id(0)
        last = pl.num_programs(0) - 1

        @pl.when(i == 0)
        def _():
            dw_ref[...] = jnp.zeros_like(dw_ref)
        live = jnp.where(i < last, 1.0, 0.0)
        for cb in range(D_FF // W):
            gs = slice(cb * W, (cb + 1) * W)
            vs = slice(D_FF + cb * W, D_FF + (cb + 1) * W)
            g = jnp.concatenate([upc_ref[:, gs], upcn_ref[:, gs]], axis=0).astype(F32)
            v = jnp.concatenate([upc_ref[:, vs], upcn_ref[:, vs]], axis=0).astype(F32)
            da = jnp.concatenate([da_ref[:, gs].astype(F32), dan_ref[:, gs].astype(F32) * live], axis=0)
            s = _sigmoid(g)
            silu = g * s
            grads = (da * v * (s * (1.0 + g * (1.0 - s))), da * silu)
            for cs, d in zip((gs, vs), grads):
                u = up_ref[:, cs].astype(F32)
                acc = None
                for k in range(F_CONV):
                    shifted = _rows_after(d, 2 - k)[:tm]
                    term = w_ref[k:k + 1, cs] * shifted
                    acc = term if acc is None else acc + term
                    dw_ref[k:k + 1, cs] += _rowsum(shifted * u)
                dup_ref[:, cs] = acc.astype(BF16)

    hn = _halo_next_index(tm, halo, T)
    return _pallas(
        comm, body, name=name, grid=(T // tm,),
        in_specs=[pl.BlockSpec((tm, 2 * D_FF), lambda i: (i, 0)),
                  pl.BlockSpec((tm, 2 * D_FF), lambda i: (i, 0)),
                  pl.BlockSpec((halo, 2 * D_FF), lambda i: (hn(i), 0)),
                  pl.BlockSpec((tm, D_FF), lambda i: (i, 0)),
                  pl.BlockSpec((halo, D_FF), lambda i: (hn(i), 0)),
                  pl.BlockSpec((None, F_CONV, 2 * D_FF), lambda i: (layer, 0, 0))],
        out_specs=[pl.BlockSpec((tm, 2 * D_FF), lambda i: (i, 0)),
                   pl.BlockSpec((F_CONV, 2 * D_FF), lambda i: (0, 0))],
        out_shape=[jax.ShapeDtypeStruct((T, 2 * D_FF), BF16),
                   jax.ShapeDtypeStruct((F_CONV, 2 * D_FF), F32)],
        compiler_params=_cparams("arbitrary"),
    )(up, upc, upc, da, da, conv_w)


def _local_step(x, tgt, small, plan):
    T = x.shape[0]
    tm_e = _pick(T, 256)
    tm_a = _pick(T, 512)
    tm_b = _pick(T, 128)
    tm = _pick(T, 1024)
    tm_f = _pick(T, 512)
    tt = _pick(T, 2048)
    nm = small["norm_mix"].reshape(2, 1, D_MODEL)
    nf = small["norm_ffn"].reshape(2, 1, D_MODEL)
    ngf = small["norm_final"].reshape(1, D_MODEL)
    b_s = small["a_b_s"].reshape(A_HEADS, CHUNK, 1)
    w_s = small["a_w_s"].reshape(A_HEADS, CHUNK, CHUNK)
    b_conv_w = small["b_conv_w"].reshape(B_CONV, D_B)
    sg = {}
    wt, cm = plan.weight, plan.comm

    h_m0, z_ab = _norm_mm_nn(x, nm, wt("ab_w_in", 0), g_layer=0, tm=tm, tn=512, name="ab_in", comm=cm("ab_in"))
    yab, cb = _mixer_ab_fwd(z_ab, small["a_ln_g"], small["a_ln_b"], w_s, b_s, b_conv_w, small["b_conv_b"],
                            small["b_ln_g"], small["b_ln_b"], tm=tm_e, name="mixer_ab", comm=cm("mixer_ab"))
    x1, h_f0 = _mm_nn(yab, wt("ab_w_out", 0), layer=0, tm=tm, tn=D_MODEL, residual=x, norm=(nf, 0),
                      name="ab_out", comm=cm("ab_out"))

    def ffn_fwd(xin, h, layer):
        up = _mm_nn(h, wt("f_w_up", layer), layer=0, tm=tm, tn=2 * 1408, out_dtype=BF16, name=f"ffn_up{layer}",
                    comm=cm(f"ffn_up{layer}"))
        a, upc = _ffn_act_fwd(up, small["f_conv_w"], layer=layer, tm=tm_a, name=f"ffn_act{layer}",
                              comm=cm(f"ffn_act{layer}"))
        if layer == 0:
            out = _mm_nn(a, wt("f_w_down", layer), layer=0, tm=tm, tn=D_MODEL, residual=xin, norm=(nm, 1),
                         name=f"ffn_down{layer}", comm=cm(f"ffn_down{layer}"))
        else:
            out = _mm_nn_loss(a, wt("f_w_down", layer), xin, tgt, ngf, tm=tm, name=f"ffn_down{layer}",
                              comm=cm(f"ffn_down{layer}"))
        return up, upc, a, out

    up0, upc0, a0, (x2, h_m1) = ffn_fwd(x1, h_f0, 0)
    z_c = _mm_nn(h_m1, wt("c_w_in", 0), layer=0, tm=tm, tn=768, out_dtype=BF16, name="c_in", comm=cm("c_in"))
    r = _mixer_c_fwd(z_c, small["c_conv_w"], tm=tm_e, name="mixer_c", comm=cm("mixer_c"))
    x3, h_f1 = _mm_nn(r, wt("c_w_out", 0), layer=0, tm=tm, tn=D_MODEL, residual=x2, norm=(nf, 1),
                      name="c_out", comm=cm("c_out"))
    up1, upc1, a1, (loss, dx, sg["norm_final"]) = ffn_fwd(x3, h_f1, 1)

    def ffn_bwd(dx, xin, h, up, upc, a, layer):
        da = _mm_nt(dx, wt("f_w_down", layer), layer=0, tm=tm, tn=1408, out_dtype=BF16,
                    name=f"ffn_down_dx{layer}", comm=cm(f"ffn_down_dx{layer}"))
        plan.grad_ready("f_w_down", layer, _mm_tn(a, dx, shards=None, tk=1408, tn=1024, tt=tt,
                                                  name=f"ffn_down_dw{layer}", comm=cm(f"ffn_down_dw{layer}")))
        dup, dcw = _ffn_act_bwd(up, upc, da, small["f_conv_w"], layer=layer, tm=tm_b, name=f"ffn_act_bwd{layer}",
                                comm=cm(f"ffn_act_bwd{layer}"))
        dxin, dg = _mm_nt_norm(dup, wt("f_w_up", layer), xin, nf, dx, g_layer=layer, tm=tm_f,
                               name=f"ffn_up_dx{layer}", comm=cm(f"ffn_up_dx{layer}"))
        plan.grad_ready("f_w_up", layer, _mm_tn(h, dup, shards=N_CHIPS, tk=512, tn=2 * 1408, tt=tt,
                                                name=f"ffn_up_dw{layer}", comm=cm(f"ffn_up_dw{layer}")))
        return dxin, dg, dcw

    dx, dnf1, dfc1 = ffn_bwd(dx, x3, h_f1, up1, upc1, a1, 1)
    dr = _mm_nt(dx, wt("c_w_out", 0), layer=0, tm=tm, tn=512, out_dtype=BF16, name="c_out_dx", comm=cm("c_out_dx"))
    plan.grad_ready("c_w_out", 0, _mm_tn(r, dx, shards=None, tk=1024, tn=1024, tt=tt, name="c_out_dw",
                                         comm=cm("c_out_dw")))
    dz_c, dccw = _mixer_c_bwd(z_c, dr, small["c_conv_w"], tm=tm_e, name="mixer_c_bwd", comm=cm("mixer_c_bwd"))
    sg["c_conv_w"] = dccw.reshape(1, C_CONV, D_MODEL)
    plan.grad_ready("c_w_in", 0, _mm_tn(h_m1, dz_c, shards=N_CHIPS, tk=1024, tn=768, tt=tt, name="c_in_dw",
                                        comm=cm("c_in_dw")))
    dx, dnm1 = _mm_nt_norm(dz_c, wt("c_w_in", 0), x2, nm, dx, g_layer=1, tm=tm_f, name="c_in_dx",
                           comm=cm("c_in_dx"))
    dx, dnf0, dfc0 = ffn_bwd(dx, x1, h_f0, up0, upc0, a0, 0)
    dyab = _mm_nt(dx, wt("ab_w_out", 0), layer=0, tm=tm, tn=512, out_dtype=BF16, name="ab_out_dx",
                  comm=cm("ab_out_dx"))
    plan.grad_ready("ab_w_out", 0, _mm_tn(yab, dx, shards=None, tk=1024, tn=1024, tt=tt, name="ab_out_dw",
                                          comm=cm("ab_out_dw")))
    (dza, dcb, sg["a_ln_g"], sg["a_ln_b"], dws, dbs, sg["b_ln_g"], sg["b_ln_b"]) = _mixer_ab_bwd_pre(
        z_ab, cb, dyab, small["a_ln_g"], small["a_ln_b"], w_s, b_s, small["b_ln_g"], small["b_ln_b"],
        tm=tm_e, name="mixer_ab_bwd", comm=cm("mixer_ab_bwd"))
    dz_ab, dbcw, sg["b_conv_b"] = _mixer_b_conv_bwd(z_ab, dcb, b_conv_w, dza, tm=tm_e, name="mixer_b_conv_bwd",
                                                    comm=cm("mixer_b_conv_bwd"))
    sg["a_w_s"] = dws.reshape(1, A_HEADS, CHUNK, CHUNK)
    sg["a_b_s"] = dbs.reshape(1, A_HEADS, CHUNK)
    sg["b_conv_w"] = dbcw.reshape(1, B_CONV, D_B)
    plan.grad_ready("ab_w_in", 0, _mm_tn(h_m0, dz_ab, shards=N_CHIPS, tk=1024, tn=512, tt=tt, name="ab_in_dw",
                                         comm=cm("ab_in_dw")))
    dx, dnm0 = _mm_nt_norm(dz_ab, wt("ab_w_in", 0), x, nm, dx, g_layer=0, tm=tm_f, name="ab_in_dx",
                           comm=cm("ab_in_dx"))

    sg["norm_mix"] = [dnm0, dnm1]
    sg["norm_ffn"] = [dnf0, dnf1]
    sg["f_conv_w"] = [dfc0, dfc1]
    return loss, dx, sg


BLOCK_BYTES = 3 * 1024 * 1024


BF16_SUBLANES = 16


def _row_tile(rows, row_bytes, step=SUBLANES):
    best = None
    for tr in range(step, rows + 1, step):
        if rows % tr == 0 and tr * row_bytes <= BLOCK_BYTES:
            best = tr
    if best is None:
        raise ValueError(f"no row tile for {rows}")
    return best


def _place_scalars():
    x, y, c = lax.axis_index("x"), lax.axis_index("y"), lax.axis_index("c")
    return jnp.stack([c, 2 * x + y, 2 * (1 - x) + y, 2 * x + (1 - y), 2 * (1 - x) + (1 - y)]).astype(jnp.int32)


def _cast_into_slot(w, place, *, layer, paired, name):
    L, rows, cols = w.shape
    tr = _row_tile(rows, cols * 4, BF16_SUBLANES)

    def body(place_ref, w_ref, o_ref):
        o_ref[...] = w_ref[...].astype(BF16)

    if paired:
        out_spec = pl.BlockSpec((None, None, tr, cols), lambda i, p: (0, p[1] // 2, i, p[1] % 2))
        out_shape = jax.ShapeDtypeStruct((1, N_CHIPS // 2, rows, 2 * cols), BF16)
    else:
        out_spec = pl.BlockSpec((None, None, tr, cols), lambda i, p: (0, p[1], i, 0))
        out_shape = jax.ShapeDtypeStruct((1, N_CHIPS, rows, cols), BF16)
    return pl.pallas_call(
        body, name=name,
        grid_spec=pltpu.PrefetchScalarGridSpec(
            num_scalar_prefetch=1, grid=(rows // tr,),
            in_specs=[pl.BlockSpec((None, tr, cols), lambda i, p: (layer, i, 0))],
            out_specs=out_spec),
        out_shape=out_shape,
        compiler_params=_cparams("parallel"),
    )(place, w)


def _pair_sum(g, theirs, place, *, name):
    S, rows, cols = g.shape
    half = rows // 2
    tr = _row_tile(half, cols * 4, BF16_SUBLANES)
    nb = half // tr

    def body(place_ref, g_ref, t_ref, o_ref):
        o_ref[...] = (g_ref[...] + t_ref[...]).astype(BF16)

    spec = pl.BlockSpec((None, tr, cols), lambda s, i, p: (s, i, 0))
    return pl.pallas_call(
        body, name=name,
        grid_spec=pltpu.PrefetchScalarGridSpec(
            num_scalar_prefetch=1, grid=(S, nb),
            in_specs=[pl.BlockSpec((None, tr, cols), lambda s, i, p: (s, p[0] * nb + i, 0)), spec],
            out_specs=spec),
        out_shape=jax.ShapeDtypeStruct((S, half, cols), BF16),
        compiler_params=_cparams("parallel", "parallel"),
    )(place, g, theirs)


def _chip_sum(p, r, g_prev, place, *, layer, shape, name):
    L, rows, cols = shape
    half = rows // 2
    tr = _row_tile(half, cols * 4, BF16_SUBLANES)
    nb = half // tr

    def body(place_ref, p_ref, r_ref, *rest):
        o_ref = rest[-1]
        mine = p_ref[...].astype(F32)
        peers = [r_ref[j].astype(F32) for j in range(3)]
        acc = None
        for s in range(N_CHIPS):
            term = jnp.where(place_ref[1] == s, mine,
                             jnp.where(place_ref[2] == s, peers[0],
                                       jnp.where(place_ref[3] == s, peers[1], peers[2])))
            acc = term if acc is None else acc + term
        o_ref[...] = acc

    in_specs = [pl.BlockSpec((None, tr, cols), lambda i, pr: (pr[1], i, 0)),
                pl.BlockSpec((3, tr, cols), lambda i, pr: (0, i, 0))]
    args = [place, p, r]
    aliases = {}
    if g_prev is not None:
        in_specs.append(HBM_REF)
        args.append(g_prev)
        aliases = {3: 0}
    return pl.pallas_call(
        body, name=name,
        grid_spec=pltpu.PrefetchScalarGridSpec(
            num_scalar_prefetch=1, grid=(nb,), in_specs=in_specs,
            out_specs=pl.BlockSpec((None, tr, cols), lambda i, pr: (layer, pr[0] * nb + i, 0))),
        out_shape=jax.ShapeDtypeStruct(shape, F32), input_output_aliases=aliases,
        compiler_params=_cparams("parallel"),
    )(*args)


def _adamw_math(w, g, m, v):
    m2 = ADAM_B1 * m + (1.0 - ADAM_B1) * g
    v2 = ADAM_B2 * v + (1.0 - ADAM_B2) * (g * g)
    m_hat = m2 / (1.0 - ADAM_B1 ** ADAM_STEP)
    v_hat = v2 / (1.0 - ADAM_B2 ** ADAM_STEP)
    delta = -ADAM_LR * (m_hat / (jnp.sqrt(v_hat) + ADAM_EPS) + ADAM_WD * w)
    return delta, m2, v2


def _adamw(w, g, m, v, *, name):
    L, rows, cols = w.shape
    tr = _row_tile(rows, cols * 4)

    def body(w_ref, g_ref, m_ref, v_ref, go_ref, d_ref, m2_ref, v2_ref):
        g = g_ref[...]
        d, m2, v2 = _adamw_math(w_ref[...], g, m_ref[...], v_ref[...])
        go_ref[...] = g
        d_ref[...] = d
        m2_ref[...] = m2
        v2_ref[...] = v2

    spec = pl.BlockSpec((None, tr, cols), lambda l, i: (l, i, 0))
    shape = jax.ShapeDtypeStruct(w.shape, F32)
    return pl.pallas_call(
        body, name=name, grid=(L, rows // tr), in_specs=[spec] * 4, out_specs=[spec] * 4,
        out_shape=[shape] * 4,
        compiler_params=_cparams("parallel", "parallel"),
    )(w, g, m, v)


def _allreduce_pack(pack, *, name, comm):
    R = pack.shape[0]
    half = R // 2
    nr, nw = len(comm.reads), len(comm.writes)

    def body(*refs):
        p_ref, rd, wr_in = refs[0], refs[1:1 + nr], refs[1 + nr:1 + nr + nw]
        o_ref, wr_out = refs[1 + nr + nw], refs[2 + nr + nw:2 + nr + 2 * nw]
        sib_ref, chip_ref, parts_ref, sems, comm_sems = refs[2 + nr + 2 * nw:]
        src = dict(zip(comm.reads, rd))
        src.update(zip(comm.writes, wr_in))
        dst = dict(zip(comm.writes, wr_out))
        comm.start(src, dst, comm_sems)
        x, y, c, k, sib, peers = _place()
        swap = _remote(p_ref, sib_ref, sems.at[0, 0], sems.at[0, 1], sib)
        swap.start()
        swap.wait()
        chip_ref[...] = p_ref[...] + sib_ref[...]
        mine = chip_ref.at[pl.ds(pl.multiple_of(c * half, SUBLANES), half)]
        sends = [_remote(mine, parts_ref.at[j], sems.at[1 + j, 0], sems.at[1 + j, 1], (px, py, c))
                 for j, (px, py) in enumerate(peers)]
        for rc in sends:
            rc.start()
        for rc in sends:
            rc.wait()
        own = mine[...]
        others = [parts_ref[j] for j in range(3)]
        acc = None
        for s in range(N_CHIPS):
            term = own
            for j, (px, py) in enumerate(peers):
                term = jnp.where(2 * px + py == s, others[j], term)
            acc = term if acc is None else acc + term
        done = o_ref.at[pl.ds(pl.multiple_of(c * half, SUBLANES), half)]
        done[...] = acc
        theirs = o_ref.at[pl.ds(pl.multiple_of((1 - c) * half, SUBLANES), half)]
        share = _remote(done, done, sems.at[4, 0], sems.at[4, 1], sib)
        share.start()
        _remote(done, theirs, sems.at[4, 0], sems.at[4, 1], sib).wait()
        comm.finish(src, dst, comm_sems)

    vm = pl.BlockSpec(memory_space=pltpu.VMEM)
    operands, shapes = _comm_operands(comm)
    outs = pl.pallas_call(
        body, name=name, in_specs=[vm] + [HBM_REF] * (nr + nw), out_specs=[vm] + [HBM_REF] * nw,
        out_shape=[jax.ShapeDtypeStruct((R, LANES), F32)] + shapes,
        input_output_aliases={1 + nr + q: 1 + q for q in range(nw)},
        scratch_shapes=[pltpu.VMEM((R, LANES), F32), pltpu.VMEM((R, LANES), F32),
                        pltpu.VMEM((3, half, LANES), F32), pltpu.SemaphoreType.DMA((5, 2)),
                        pltpu.SemaphoreType.DMA((comm.ncopies, 2))],
        compiler_params=pltpu.CompilerParams(vmem_limit_bytes=VMEM_BYTES_MAX),
    )(pack, *operands)
    for q, n in enumerate(comm.writes):
        comm.plan.bufs[n] = outs[1 + q]
    return outs[0]


PACK_UNIT = SUBLANES * LANES


def _pack(arrays):
    flat, sizes = [], []
    for a in arrays:
        pieces = a if isinstance(a, (list, tuple)) else [a]
        v = jnp.concatenate([p.reshape(-1) for p in pieces]) if len(pieces) > 1 else pieces[0].reshape(-1)
        size = v.shape[0]
        padded = -(-size // PACK_UNIT) * PACK_UNIT
        flat.append(jnp.pad(v, (0, padded - size)))
        sizes.append((size, padded))
    total = sum(p for _, p in sizes)
    if (total // PACK_UNIT) % 2:
        flat.append(jnp.zeros((PACK_UNIT,), F32))
    return jnp.concatenate(flat).reshape(-1, LANES), sizes


def _unpack(pack, sizes, shapes):
    v = pack.reshape(-1)
    out, off = [], 0
    for (size, padded), shape in zip(sizes, shapes):
        out.append(v[off:off + size].reshape(shape))
        off += padded
    return out


BIG = ("ab_w_in", "ab_w_out", "c_w_in", "c_w_out", "f_w_up", "f_w_down")
COL_SHARDED = ("ab_w_in", "c_w_in", "f_w_up")
PAIRED = ("f_w_up",)
SMALL_REPLICATED = ("norm_mix", "norm_ffn", "norm_final", "a_ln_g", "a_ln_b", "a_w_s", "a_b_s",
                    "b_conv_b", "b_ln_g", "b_ln_b")
SMALL_SHARDED = ("b_conv_w", "c_conv_w", "f_conv_w")
SMALL = SMALL_REPLICATED + SMALL_SHARDED
ALL_WEIGHTS = ("norm_mix", "norm_ffn", "norm_final", "ab_w_in", "a_ln_g", "a_ln_b", "a_w_s", "a_b_s",
               "b_conv_w", "b_conv_b", "b_ln_g", "b_ln_b", "ab_w_out", "c_w_in", "c_conv_w", "c_w_out",
               "f_w_up", "f_conv_w", "f_w_down")


SCHEDULE = {
    "ab_in": [("gi", "f_w_up", 0, 0, 4), ("gi", "ab_w_out", 0)],
    "mixer_ab": [("gd", "f_w_up", 0, 0, 4), ("gd", "ab_w_out", 0), ("gi", "f_w_up", 0, 1, 4),
                 ("gi", "f_w_up", 0, 2, 4), ("gi", "f_w_up", 0, 3, 4)],
    "ab_out": [("gd", "f_w_up", 0, 1, 4), ("gd", "f_w_up", 0, 2, 4), ("gd", "f_w_up", 0, 3, 4)],
    "ffn_up0": [("gi", "f_w_down", 0), ("gi", "c_w_in", 0, 0, 2)],
    "ffn_act0": [("gd", "f_w_down", 0), ("gd", "c_w_in", 0, 0, 2), ("gi", "c_w_in", 0, 1, 2),
                 ("gi", "f_w_up", 1, 0, 4)],
    "ffn_down0": [("gd", "c_w_in", 0, 1, 2), ("gd", "f_w_up", 1, 0, 4), ("gi", "f_w_up", 1, 1, 4),
                  ("gi", "c_w_out", 0)],
    "c_in": [("gd", "f_w_up", 1, 1, 4), ("gd", "c_w_out", 0), ("gi", "f_w_up", 1, 2, 4),
             ("gi", "f_w_up", 1, 3, 4)],
    "mixer_c": [("gd", "f_w_up", 1, 2, 4), ("gd", "f_w_up", 1, 3, 4)],
    "ffn_up1": [("gi", "f_w_down", 1)],
    "ffn_act1": [("gd", "f_w_down", 1)],
    "ffn_act_bwd1": [("px", "f_w_down", 1)],
    "ffn_up_dx1": [("cx", "f_w_down", 1)],
    "c_out_dx": [("px", "f_w_up", 1, 0, 2)],
    "c_out_dw": [("px", "f_w_up", 1, 1, 2)],
    "mixer_c_bwd": [("cx", "f_w_up", 1, 0, 4), ("px", "c_w_out", 0)],
    "c_in_dw": [("cx", "f_w_up", 1, 1, 4)],
    "c_in_dx": [("cx", "f_w_up", 1, 2, 4), ("px", "c_w_in", 0)],
    "ffn_down_dx0": [("cx", "f_w_up", 1, 3, 4), ("cx", "c_w_out", 0)],
    "ffn_down_dw0": [("cx", "c_w_in", 0, 0, 2)],
    "ffn_act_bwd0": [("cx", "c_w_in", 0, 1, 2), ("px", "f_w_down", 0)],
    "ffn_up_dx0": [("cx", "f_w_down", 0)],
    "ffn_up_dw0": [("ps", "f_w_down", 1), ("ps", "f_w_up", 1)],
    "ab_out_dx": [("px", "f_w_up", 0, 0, 2)],
    "ab_out_dw": [("px", "f_w_up", 0, 1, 2)],
    "mixer_ab_bwd": [("cx", "f_w_up", 0, 0, 4), ("px", "ab_w_out", 0)],
    "mixer_b_conv_bwd": [("cx", "f_w_up", 0, 1, 4), ("cx", "ab_w_out", 0), ("ps", "c_w_out", 0),
                         ("ps", "c_w_in", 0), ("ps", "f_w_down", 0)],
    "ab_in_dw": [("cx", "f_w_up", 0, 2, 4)],
    "ab_in_dx": [("cx", "f_w_up", 0, 3, 4), ("px", "ab_w_in", 0)],
}


class _Plan:
    def __init__(self, shapes, place):
        self.shapes, self.place, self.bufs = shapes, place, {}
        self.summed, self.shared = set(), set()

    def weight(self, name, layer):
        g = self.bufs[f"w:{name}:{layer}"]
        if name in COL_SHARDED:
            return g
        _, S, rows, cols = g.shape
        return g.reshape(1, S * rows, cols)

    def grad_ready(self, name, layer, g):
        _, rows, cols = self.shapes[name]
        hbm = lambda a: pltpu.with_memory_space_constraint(a, pltpu.HBM)
        self.bufs[f"g:{name}:{layer}"] = g.reshape(N_CHIPS, rows, cols)
        self.bufs[f"t:{name}:{layer}"] = hbm(lax.empty((N_CHIPS, rows // 2, cols), F32))
        self.bufs[f"l:{name}:{layer}"] = hbm(lax.empty((3, rows // 2, cols), BF16))

    def job(self, kind, name, layer, part=0, parts=1):
        _, rows, cols = self.shapes[name]
        key = f"{name}:{layer}"
        if kind == "gi":
            return _job_gather_ici("w:" + key, rows, cols, part, parts)
        if kind == "gd":
            return _job_gather_d2d("w:" + key, rows, cols, part, parts)
        if kind == "px":
            return _job_pair_exchange("g:" + key, "t:" + key, rows, part, parts)
        if kind == "cx":
            if "p:" + key not in self.bufs:
                self.bufs["p:" + key] = _pair_sum(self.bufs["g:" + key], self.bufs["t:" + key], self.place,
                                                  name=f"pair_sum_{name}{layer}")
            nr = rows // 2 // parts
            return _job_chip_exchange("p:" + key, "l:" + key, part * nr, nr)
        if kind == "ps":
            self.chip_sum(name, layer)
            self.shared.add(key)
            return _job_pair_share("G:" + name, layer, rows)
        raise ValueError(kind)

    def chip_sum(self, name, layer):
        key = f"{name}:{layer}"
        if key not in self.summed:
            self.summed.add(key)
            self.bufs["G:" + name] = _chip_sum(self.bufs["p:" + key], self.bufs["l:" + key],
                                               self.bufs.get("G:" + name), self.place, layer=layer,
                                               shape=self.shapes[name], name=f"chip_sum_{name}{layer}")

    def comm(self, call):
        specs = SCHEDULE.get(call)
        return None if specs is None else _Comm(self, [self.job(*spec) for spec in specs])


def _step(x, tgt, w, m, v):
    chip = 2 * lax.axis_index("x") + lax.axis_index("y")
    place = _place_scalars()
    plan = _Plan({n: w[n].shape for n in BIG}, place)
    items = [(n, l) for n in BIG for l in range(w[n].shape[0])]

    for n, l in items:
        plan.bufs[f"w:{n}:{l}"] = _cast_into_slot(w[n], place, layer=l, paired=n in PAIRED, name=f"cast_{n}{l}")
    conv_pack, conv_sizes = _pack([w[n] for n in SMALL_SHARDED])
    hbm = lambda a: pltpu.with_memory_space_constraint(a, pltpu.HBM)
    plan.bufs["conv:mine"] = hbm(conv_pack)
    plan.bufs["conv:all"] = hbm(lax.empty((N_CHIPS,) + conv_pack.shape, F32))
    _comm_only(plan, [[plan.job("gi", "ab_w_in", 0), _job_chip_gather("conv:mine", "conv:all")],
                      [plan.job("gd", "ab_w_in", 0)]], name="gather_first")
    conv_shapes = [w[n].shape for n in SMALL_SHARDED]
    per_chip = [_unpack(plan.bufs["conv:all"][s], conv_sizes, conv_shapes) for s in range(N_CHIPS)]
    small = {n: w[n] for n in SMALL_REPLICATED}
    for idx, n in enumerate(SMALL_SHARDED):
        small[n] = jnp.concatenate([jnp.where(chip == s, w[n], per_chip[s][idx]) for s in range(N_CHIPS)], axis=-1)

    loss, dx, sg = _local_step(x, tgt, small, plan)

    g_pack, g_sizes = _pack([sg[n] for n in SMALL] + [loss])
    g_sum = _allreduce_pack(g_pack, name="allreduce_small_grads",
                            comm=_Comm(plan, [plan.job("cx", "ab_w_in", 0)]))
    full_shapes = [small[n].shape for n in SMALL]
    *summed, loss = _unpack(g_sum, g_sizes, full_shapes + [(1, 1)])
    g_small = dict(zip(SMALL, summed))
    for n in SMALL_SHARDED:
        width = w[n].shape[-1]
        g_small[n] = lax.dynamic_slice_in_dim(g_small[n], chip * width, width, axis=g_small[n].ndim - 1)

    _comm_only(plan, [[plan.job("ps", n, l) for n, l in items if f"{n}:{l}" not in plan.shared]],
               name="reduce_pair_share")
    grads_big = [plan.bufs["G:" + n] for n in BIG]

    grad, delta, new_m, new_v = {}, {}, {}, {}
    for n, g in zip(BIG, grads_big):
        grad[n], delta[n], new_m[n], new_v[n] = _adamw(w[n], g, m[n], v[n], name=f"adamw_{n}")
    shapes = [w[n].shape for n in SMALL]
    wp, sizes = _pack([w[n] for n in SMALL])
    gp, _ = _pack([g_small[n] for n in SMALL])
    mp, _ = _pack([m[n] for n in SMALL])
    vp, _ = _pack([v[n] for n in SMALL])
    R = wp.shape[0]
    _, dp, m2p, v2p = _adamw(wp.reshape(1, R, LANES), gp.reshape(1, R, LANES), mp.reshape(1, R, LANES),
                             vp.reshape(1, R, LANES), name="adamw_small")
    for n, d_, m_, v_ in zip(SMALL, _unpack(dp, sizes, shapes), _unpack(m2p, sizes, shapes),
                             _unpack(v2p, sizes, shapes)):
        grad[n] = g_small[n]
        delta[n], new_m[n], new_v[n] = d_, m_, v_
    return loss, dx, grad, delta, new_m, new_v


def kernel(x, norm_mix, norm_ffn, norm_final, ab_w_in, a_ln_g, a_ln_b, a_w_s, a_b_s, b_conv_w, b_conv_b, b_ln_g, b_ln_b, ab_w_out, c_w_in, c_conv_w, c_w_out, f_w_up, f_conv_w, f_w_down, loss_target, m_norm_mix, m_norm_ffn, m_norm_final, m_ab_w_in, m_a_ln_g, m_a_ln_b, m_a_w_s, m_a_b_s, m_b_conv_w, m_b_conv_b, m_b_ln_g, m_b_ln_b, m_ab_w_out, m_c_w_in, m_c_conv_w, m_c_w_out, m_f_w_up, m_f_conv_w, m_f_w_down, v_norm_mix, v_norm_ffn, v_norm_final, v_ab_w_in, v_a_ln_g, v_a_ln_b, v_a_w_s, v_a_b_s, v_b_conv_w, v_b_conv_b, v_b_ln_g, v_b_ln_b, v_ab_w_out, v_c_w_in, v_c_conv_w, v_c_w_out, v_f_w_up, v_f_conv_w, v_f_w_down):
    given = dict(locals())
    w = {n: given[n] for n in ALL_WEIGHTS}
    m = {n: given["m_" + n] for n in ALL_WEIGHTS}
    v = {n: given["v_" + n] for n in ALL_WEIGHTS}
    T = x.shape[1]
    loss, dx, grad, delta, new_m, new_v = _step(x.reshape(T, D_MODEL), loss_target.reshape(T, D_MODEL), w, m, v)
    out = [loss[0, 0], dx.reshape(x.shape)]
    for d in (grad, delta, new_m, new_v):
        out += [d[n] for n in ALL_WEIGHTS]
    return tuple(out)
```

```python
import functools
import math

import jax
import jax.numpy as jnp
from jax import lax
from jax.experimental import pallas as pl
from jax.experimental.pallas import tpu as pltpu

F32 = jnp.float32
BF16 = jnp.bfloat16

EPS = 1e-6
D_MODEL = 1024
CHUNK = 128
HEAD_DIM = 128
A_HEADS = 4
D_A = 512
D_B = 512
B_CONV = 31
C_CONV = 3
D_FF = 2816
F_CONV = 3
N_CHIPS = 4

ADAM_LR = 0.001
ADAM_B1 = 0.9
ADAM_B2 = 0.999
ADAM_EPS = 1e-08
ADAM_WD = 0.01
ADAM_STEP = 10

SUBLANES = 8
LANES = 128
HALO_SHORT = 16
HALO_LONG = 32
VMEM_BYTES_MAX = 60000 * 1024

INV_SQRT2 = 1.0 / math.sqrt(2.0)
INV_SQRT_2PI = 1.0 / math.sqrt(2.0 * math.pi)

MESH = pl.DeviceIdType.MESH


def _cparams(*sem):
    return pltpu.CompilerParams(dimension_semantics=sem, vmem_limit_bytes=VMEM_BYTES_MAX)


def _pick(total, pref):
    for c in (2048, 1024, 512, 256, 128):
        if c <= pref and total % c == 0:
            return c
    raise ValueError(f"no tile for {total}")


def _sigmoid(x):
    return jax.nn.sigmoid(x)


def _silu(x):
    return x * _sigmoid(x)


def _dsilu(x):
    s = _sigmoid(x)
    return s * (1.0 + x * (1.0 - s))


def _gelu(x):
    return 0.5 * x * (1.0 + lax.erf(x * INV_SQRT2))


def _dgelu(x):
    return 0.5 * (1.0 + lax.erf(x * INV_SQRT2)) + x * jnp.exp(-0.5 * x * x) * INV_SQRT_2PI


def _ln_stats(x):
    mu = jnp.mean(x, axis=-1, keepdims=True)
    xc = x - mu
    var = jnp.mean(xc * xc, axis=-1, keepdims=True)
    r = lax.rsqrt(var + EPS)
    return xc * r, r


def _ln_bwd(dy, xh, r, g):
    dxh = dy * g
    m1 = jnp.mean(dxh, axis=-1, keepdims=True)
    m2 = jnp.mean(dxh * xh, axis=-1, keepdims=True)
    return r * (dxh - m1 - xh * m2)


def _rowsum(x):
    return jnp.sum(x, axis=0, keepdims=True)


HBM_REF = pl.BlockSpec(memory_space=pltpu.HBM)


def _place():
    x, y, c = lax.axis_index("x"), lax.axis_index("y"), lax.axis_index("c")
    peers = [(1 - x, y), (x, 1 - y), (1 - x, 1 - y)]
    return x, y, c, 2 * x + y, (x, y, 1 - c), peers


def _half(rows, which):
    return pl.ds(which * (rows // 2), rows // 2)


def _remote(src, dst, send_sem, recv_sem, device):
    return pltpu.make_async_remote_copy(src_ref=src, dst_ref=dst, send_sem=send_sem, recv_sem=recv_sem,
                                        device_id=device, device_id_type=MESH)


class _Job:
    def __init__(self, reads, writes, ncopies, copies):
        self.reads, self.writes, self.ncopies, self.copies = reads, writes, ncopies, copies


def _share(rows, which, part, parts):
    nr = rows // 2 // parts
    return pl.ds(which * (rows // 2) + part * nr, nr)


def _slot(ref, chip, rows, cols):
    if ref.shape[1] == N_CHIPS:
        return ref.at[0, chip, rows]
    return ref.at[0, chip // 2, rows, pl.ds(pl.multiple_of((chip % 2) * cols, LANES), cols)]


def _job_gather_ici(name, rows, cols, part, parts):
    def copies(src, dst, sem):
        x, y, c, k, sib, peers = _place()
        mine_rows = _share(rows, c, part, parts)
        out = []
        for j, (px, py) in enumerate(peers):
            mine = _slot(src[name], k, mine_rows, cols)
            out.append((_remote(mine, _slot(dst[name], k, mine_rows, cols), sem(j, 0), sem(j, 1), (px, py, c)),
                        _remote(mine, _slot(dst[name], 2 * px + py, mine_rows, cols), sem(j, 0), sem(j, 1),
                                (px, py, c))))
        return out
    return _Job([], [name], 3, copies)


def _job_gather_d2d(name, rows, cols, part, parts):
    def copies(src, dst, sem):
        x, y, c, k, sib, peers = _place()
        out = []
        for j, (px, py) in enumerate(peers):
            mine_rows, their_rows = _share(rows, c, part, parts), _share(rows, 1 - c, part, parts)
            landed = _slot(src[name], 2 * px + py, mine_rows, cols)
            out.append((_remote(landed, _slot(dst[name], 2 * px + py, mine_rows, cols), sem(j, 0), sem(j, 1), sib),
                        _remote(landed, _slot(dst[name], 2 * px + py, their_rows, cols), sem(j, 0), sem(j, 1), sib)))
        return out
    return _Job([], [name], 3, copies)


def _job_chip_gather(sname, dname):
    def copies(src, dst, sem):
        x, y, c, k, sib, peers = _place()
        return [(_remote(src[sname], dst[dname].at[k], sem(j, 0), sem(j, 1), (px, py, c)),
                 _remote(src[sname], dst[dname].at[2 * px + py], sem(j, 0), sem(j, 1), (px, py, c)))
                for j, (px, py) in enumerate(peers)]
    return _Job([sname], [dname], 3, copies)


def _job_pair_exchange(gname, tname, rows, part, parts):
    nr = rows // 2 // parts

    def copies(src, dst, sem):
        x, y, c, k, sib, peers = _place()
        cp = _remote(src[gname].at[:, _share(rows, 1 - c, part, parts), :],
                     dst[tname].at[:, pl.ds(part * nr, nr), :], sem(0, 0), sem(0, 1), sib)
        return [(cp, cp)]
    return _Job([gname], [tname], 1, copies)


def _job_chip_exchange(pname, lname, r0, nr):
    def copies(src, dst, sem):
        x, y, c, k, sib, peers = _place()
        out = []
        for j, (px, py) in enumerate(peers):
            cp = _remote(src[pname].at[2 * px + py, pl.ds(r0, nr)], dst[lname].at[j, pl.ds(r0, nr)],
                         sem(j, 0), sem(j, 1), (px, py, c))
            out.append((cp, cp))
        return out
    return _Job([pname], [lname], 3, copies)


def _job_pair_share(name, layer, rows):
    def copies(src, dst, sem):
        x, y, c, k, sib, peers = _place()
        mine = src[name].at[layer, _half(rows, c)]
        return [(_remote(mine, dst[name].at[layer, _half(rows, c)], sem(0, 0), sem(0, 1), sib),
                 _remote(mine, dst[name].at[layer, _half(rows, 1 - c)], sem(0, 0), sem(0, 1), sib))]
    return _Job([], [name], 1, copies)


class _Comm:
    def __init__(self, plan, jobs):
        self.plan, self.jobs = plan, jobs
        self.writes, self.reads = [], []
        for job in jobs:
            for n in job.writes:
                if n not in self.writes:
                    self.writes.append(n)
        for job in jobs:
            for n in job.reads:
                if n not in self.writes and n not in self.reads:
                    self.reads.append(n)
        self.ncopies = sum(job.ncopies for job in jobs)

    def descriptors(self, src, dst, sems, base):
        out = []
        for job in self.jobs:
            sem = lambda j, which, base=base: sems.at[base + j, which]
            out += job.copies(src, dst, sem)
            base += job.ncopies
        return out

    def start(self, src, dst, sems, base=0):
        for first, _ in self.descriptors(src, dst, sems, base):
            first.start()

    def finish(self, src, dst, sems, base=0):
        for _, landed in self.descriptors(src, dst, sems, base):
            landed.wait()


def _comm_operands(comm):
    bufs = comm.plan.bufs
    shapes = [jax.ShapeDtypeStruct(bufs[n].shape, bufs[n].dtype) for n in comm.writes]
    return [bufs[n] for n in comm.reads] + [bufs[n] for n in comm.writes], shapes


def _pallas(comm, body, *, name, grid, in_specs, out_specs, out_shape, compiler_params, scratch_shapes=(),
            aliases=None):
    aliases = dict(aliases or {})
    if comm is None:
        return pl.pallas_call(body, name=name, grid=grid, in_specs=in_specs, out_specs=out_specs,
                              out_shape=out_shape, scratch_shapes=list(scratch_shapes),
                              input_output_aliases=aliases, compiler_params=compiler_params)
    single = not isinstance(out_shape, (list, tuple))
    base_specs = [out_specs] if single else list(out_specs)
    base_shape = [out_shape] if single else list(out_shape)
    nb, nr, nw, nbo, nsc = len(in_specs), len(comm.reads), len(comm.writes), len(base_specs), len(scratch_shapes)

    def wrapped(*refs):
        base_in, rd, wr_in = refs[:nb], refs[nb:nb + nr], refs[nb + nr:nb + nr + nw]
        o0 = nb + nr + nw
        base_out, wr_out = refs[o0:o0 + nbo], refs[o0 + nbo:o0 + nbo + nw]
        scratch, sems = refs[o0 + nbo + nw:o0 + nbo + nw + nsc], refs[-1]
        src = dict(zip(comm.reads, rd))
        src.update(zip(comm.writes, wr_in))
        dst = dict(zip(comm.writes, wr_out))
        first = functools.reduce(jnp.logical_and, [pl.program_id(a) == 0 for a in range(len(grid))])
        last = functools.reduce(jnp.logical_and,
                                [pl.program_id(a) == pl.num_programs(a) - 1 for a in range(len(grid))])

        @pl.when(first)
        def _():
            comm.start(src, dst, sems)
        body(*base_in, *base_out, *scratch)

        @pl.when(last)
        def _():
            comm.finish(src, dst, sems)

    operands, shapes = _comm_operands(comm)
    call = pl.pallas_call(
        wrapped, name=name, grid=grid, in_specs=list(in_specs) + [HBM_REF] * (nr + nw),
        out_specs=base_specs + [HBM_REF] * nw, out_shape=base_shape + shapes,
        input_output_aliases={**aliases, **{nb + nr + q: nbo + q for q in range(nw)}},
        scratch_shapes=list(scratch_shapes) + [pltpu.SemaphoreType.DMA((comm.ncopies, 2))],
        compiler_params=compiler_params)

    def run(*args):
        outs = call(*args, *operands)
        for q, n in enumerate(comm.writes):
            comm.plan.bufs[n] = outs[nbo + q]
        return outs[0] if single else list(outs[:nbo])

    return run


def _comm_only(plan, phases, *, name):
    comms = [_Comm(plan, jobs) for jobs in phases]
    both = _Comm(plan, [job for jobs in phases for job in jobs])
    nr, nw = len(both.reads), len(both.writes)

    def body(*refs):
        rd, wr_in, wr_out, sems = refs[:nr], refs[nr:nr + nw], refs[nr + nw:nr + 2 * nw], refs[-1]
        src = dict(zip(both.reads, rd))
        src.update(zip(both.writes, wr_in))
        dst = dict(zip(both.writes, wr_out))
        base = 0
        for comm in comms:
            comm.start(src, dst, sems, base)
            comm.finish(src, dst, sems, base)
            base += comm.ncopies

    operands, shapes = _comm_operands(both)
    outs = pl.pallas_call(
        body, name=name, in_specs=[HBM_REF] * (nr + nw), out_specs=[HBM_REF] * nw, out_shape=shapes,
        input_output_aliases={nr + q: q for q in range(nw)},
        scratch_shapes=[pltpu.SemaphoreType.DMA((both.ncopies, 2))],
    )(*operands)
    for q, n in enumerate(both.writes):
        plan.bufs[n] = outs[q]


def _mm_nn(a, w, *, layer, tm, tn, residual=None, norm=None, out_dtype=F32, name, comm=None):
    T, K = a.shape
    if w.ndim == 4:
        _, S, _, n4 = w.shape
        N = S * n4
        bps = n4 // tn
        w_spec = pl.BlockSpec((None, None, K, tn), lambda j, i: (layer, j // bps, 0, j % bps))
    else:
        N = w.shape[2]
        w_spec = pl.BlockSpec((None, K, tn), lambda j, i: (layer, 0, j))
    in_specs = [pl.BlockSpec((tm, K), lambda j, i: (i, 0)), w_spec]
    args = [a, w]
    if residual is not None:
        in_specs.append(pl.BlockSpec((tm, tn), lambda j, i: (i, j)))
        args.append(residual)
    out_specs = pl.BlockSpec((tm, tn), lambda j, i: (i, j))
    out_shape = jax.ShapeDtypeStruct((T, N), out_dtype)
    if norm is not None:
        assert tn == N
        g, norm_layer = norm
        in_specs.append(pl.BlockSpec((None, 1, N), lambda j, i: (norm_layer, 0, 0)))
        args.append(g)
        out_specs = [out_specs, pl.BlockSpec((tm, tn), lambda j, i: (i, j))]
        out_shape = [out_shape, jax.ShapeDtypeStruct((T, N), BF16)]

    def body(*refs):
        a_ref, w_ref = refs[0], refs[1]
        acc = jnp.dot(a_ref[...].astype(BF16), w_ref[...], preferred_element_type=F32)
        if residual is not None:
            acc = refs[2][...] + acc
        if norm is None:
            refs[-1][...] = acc.astype(out_dtype)
        else:
            refs[-2][...] = acc.astype(out_dtype)
            r = lax.rsqrt(jnp.mean(acc * acc, axis=-1, keepdims=True) + EPS)
            refs[-1][...] = (acc * r * refs[-3][...]).astype(BF16)

    return _pallas(
        comm, body, name=name, grid=(N // tn, T // tm), in_specs=in_specs,
        out_specs=out_specs, out_shape=out_shape,
        compiler_params=_cparams("parallel", "parallel"),
    )(*args)


def _norm_mm_nn(x, g, w, *, g_layer, tm, tn, name, comm=None):
    T, K = x.shape
    _, S, _, n4 = w.shape
    bps = n4 // tn

    def body(x_ref, g_ref, w_ref, h_ref, o_ref):
        @pl.when(pl.program_id(1) == 0)
        def _():
            xf = x_ref[...]
            r = lax.rsqrt(jnp.mean(xf * xf, axis=-1, keepdims=True) + EPS)
            h_ref[...] = (xf * r * g_ref[...]).astype(BF16)
        o_ref[...] = jnp.dot(h_ref[...], w_ref[...], preferred_element_type=F32).astype(BF16)

    return _pallas(
        comm, body, name=name, grid=(T // tm, S * bps),
        in_specs=[pl.BlockSpec((tm, K), lambda i, j: (i, 0)),
                  pl.BlockSpec((None, 1, K), lambda i, j: (g_layer, 0, 0)),
                  pl.BlockSpec((None, None, K, tn), lambda i, j: (0, j // bps, 0, j % bps))],
        out_specs=[pl.BlockSpec((tm, K), lambda i, j: (i, 0)), pl.BlockSpec((tm, tn), lambda i, j: (i, j))],
        out_shape=[jax.ShapeDtypeStruct((T, K), BF16), jax.ShapeDtypeStruct((T, S * n4), BF16)],
        compiler_params=_cparams("parallel", "arbitrary"),
    )(x, g, w)


def _mm_nt(dy, w, *, layer, tm, tn, name, out_dtype=F32, comm=None):
    T = dy.shape[0]
    nt_dims = (((1,), (1,)), ((), ()))
    _, R, N = w.shape

    def body2(dy_ref, w_ref, o_ref):
        o_ref[...] = lax.dot_general(dy_ref[...].astype(BF16), w_ref[...], nt_dims,
                                     preferred_element_type=F32).astype(out_dtype)

    return _pallas(
        comm, body2, name=name, grid=(R // tn, T // tm),
        in_specs=[pl.BlockSpec((tm, N), lambda j, i: (i, 0)),
                  pl.BlockSpec((None, tn, N), lambda j, i: (layer, j, 0))],
        out_specs=pl.BlockSpec((tm, tn), lambda j, i: (i, j)),
        out_shape=jax.ShapeDtypeStruct((T, R), out_dtype),
        compiler_params=_cparams("parallel", "parallel"),
    )(dy, w)


def _mm_tn(a, dy, *, shards, tk, tn, tt, name, comm=None):
    T, K = a.shape
    N = dy.shape[1]
    tn_dims = (((0,), (0,)), ((), ()))
    n4 = N if shards is None else N // shards
    span = max(tn // n4, 1)

    def body(a_ref, dy_ref, o_ref):
        @pl.when(pl.program_id(2) == 0)
        def _():
            o_ref[...] = jnp.zeros_like(o_ref)
        r = lax.dot_general(a_ref[...].astype(BF16), dy_ref[...].astype(BF16), tn_dims,
                            preferred_element_type=F32)
        if span == 1:
            o_ref[...] += r
        else:
            for q in range(span):
                o_ref[q] += r[:, q * n4:(q + 1) * n4]

    if shards is None:
        out_spec = pl.BlockSpec((tk, tn), lambda k, n, t: (k, n))
        out_shape = jax.ShapeDtypeStruct((K, N), F32)
    elif span > 1:
        out_spec = pl.BlockSpec((span, tk, n4), lambda k, n, t: (n, k, 0))
        out_shape = jax.ShapeDtypeStruct((shards, K, n4), F32)
    else:
        bps = n4 // tn
        out_spec = pl.BlockSpec((None, tk, tn), lambda k, n, t: (n // bps, k, n % bps))
        out_shape = jax.ShapeDtypeStruct((shards, K, n4), F32)
    return _pallas(
        comm, body, name=name, grid=(K // tk, N // tn, T // tt),
        in_specs=[pl.BlockSpec((tt, tk), lambda k, n, t: (t, k)),
                  pl.BlockSpec((tt, tn), lambda k, n, t: (t, n))],
        out_specs=out_spec, out_shape=out_shape,
        compiler_params=_cparams("parallel", "parallel", "arbitrary"),
    )(a, dy)


def _rmsnorm_bwd_math(xf, g, dh, dres):
    r = lax.rsqrt(jnp.mean(xf * xf, axis=-1, keepdims=True) + EPS)
    xh = xf * r
    dxh = dh * g
    dx = dres + r * (dxh - xh * jnp.mean(dxh * xh, axis=-1, keepdims=True))
    return dx, _rowsum(dh * xh)


def _mm_nt_norm(dy, w, x, g, dres, *, g_layer, tm, name, comm=None):
    T = dy.shape[0]
    _, S, K, n4 = w.shape
    nt_dims = (((1,), (1,)), ((), ()))

    def body(dy_ref, w_ref, x_ref, g_ref, dres_ref, dx_ref, dg_ref):
        @pl.when(pl.program_id(0) == 0)
        def _():
            dg_ref[...] = jnp.zeros_like(dg_ref)
        dh = None
        for s in range(S):
            part = lax.dot_general(dy_ref[:, s * n4:(s + 1) * n4].astype(BF16), w_ref[s], nt_dims,
                                   preferred_element_type=F32)
            dh = part if dh is None else dh + part
        dx, dg = _rmsnorm_bwd_math(x_ref[...], g_ref[...], dh, dres_ref[...])
        dx_ref[...] = dx
        dg_ref[...] += dg

    row = lambda i: (i, 0)
    return _pallas(
        comm, body, name=name, grid=(T // tm,),
        in_specs=[pl.BlockSpec((tm, S * n4), row),
                  pl.BlockSpec((None, S, K, n4), lambda i: (0, 0, 0, 0)),
                  pl.BlockSpec((tm, K), row),
                  pl.BlockSpec((None, 1, K), lambda i: (g_layer, 0, 0)),
                  pl.BlockSpec((tm, K), row)],
        out_specs=[pl.BlockSpec((tm, K), row), pl.BlockSpec((1, K), lambda i: (0, 0))],
        out_shape=[jax.ShapeDtypeStruct((T, K), F32), jax.ShapeDtypeStruct((1, K), F32)],
        compiler_params=_cparams("arbitrary"),
    )(dy, w, x, g, dres)


def _mm_nn_loss(a, w, residual, tgt, g, *, tm, name, comm=None):
    T, K = a.shape
    D = w.shape[2]

    def body(a_ref, w_ref, res_ref, t_ref, g_ref, loss_ref, dx_ref, dg_ref):
        @pl.when(pl.program_id(0) == 0)
        def _():
            dg_ref[...] = jnp.zeros_like(dg_ref)
            loss_ref[...] = jnp.zeros_like(loss_ref)
        xf = res_ref[...] + jnp.dot(a_ref[...], w_ref[...], preferred_element_type=F32)
        gg = g_ref[...]
        r = lax.rsqrt(jnp.mean(xf * xf, axis=-1, keepdims=True) + EPS)
        xh = xf * r
        err = xh * gg - t_ref[...]
        row = jnp.mean(err * err, axis=-1, keepdims=True)
        loss_ref[...] += 0.5 * jnp.sum(row, axis=0, keepdims=True)
        dy = err * (1.0 / D)
        dg_ref[...] += _rowsum(dy * xh)
        dxh = dy * gg
        dx_ref[...] = r * (dxh - xh * jnp.mean(dxh * xh, axis=-1, keepdims=True))

    row_spec = pl.BlockSpec((tm, D), lambda i: (i, 0))
    return _pallas(
        comm, body, name=name, grid=(T // tm,),
        in_specs=[pl.BlockSpec((tm, K), lambda i: (i, 0)), pl.BlockSpec((None, K, D), lambda i: (0, 0, 0)),
                  row_spec, row_spec, pl.BlockSpec((1, D), lambda i: (0, 0))],
        out_specs=[pl.BlockSpec((1, 1), lambda i: (0, 0)), row_spec, pl.BlockSpec((1, D), lambda i: (0, 0))],
        out_shape=[jax.ShapeDtypeStruct((1, 1), F32), jax.ShapeDtypeStruct((T, D), F32),
                   jax.ShapeDtypeStruct((1, D), F32)],
        compiler_params=_cparams("arbitrary"),
    )(a, w, residual, tgt, g)


CONV_ROWS = 64
CONV_COLS = 256


def _halo_prev_index(tm, halo):
    per = tm // halo
    return lambda i: jnp.maximum(i * per - 1, 0)


def _halo_next_index(tm, halo, total):
    per = tm // halo
    last = total // halo - 1
    return lambda i: jnp.minimum((i + 1) * per, last)


def _causal_mask():
    t = lax.broadcasted_iota(jnp.int32, (CHUNK, CHUNK), 0)
    s = lax.broadcasted_iota(jnp.int32, (CHUNK, CHUNK), 1)
    return s <= t


def _mixer_ab_fwd(z, a_ln_g, a_ln_b, w_s, b_s, conv_w, conv_b, b_ln_g, b_ln_b, *, tm, name, comm=None):
    T = z.shape[0]
    nchunk = tm // CHUNK
    halo = HALO_LONG

    def body(za_ref, zb_ref, zh_ref, alg_ref, alb_ref, ws_ref, bs_ref, cw_ref, cbias_ref,
             blg_ref, blb_ref, y_ref, cb_ref, ext_ref):
        i = pl.program_id(0)
        gu = _gelu(za_ref[:, :D_A].astype(F32))
        gv = _gelu(za_ref[:, D_A:].astype(F32))
        xh, _ = _ln_stats(gv)
        lv = (xh * alg_ref[...] + alb_ref[...]).astype(BF16)
        mask = _causal_mask()
        for h in range(A_HEADS):
            wm = jnp.where(mask, ws_ref[h], 0.0).astype(BF16)
            cols = slice(h * HEAD_DIM, (h + 1) * HEAD_DIM)
            for c in range(nchunk):
                rows = slice(c * CHUNK, (c + 1) * CHUNK)
                mixed = jnp.dot(wm, lv[rows, cols], preferred_element_type=F32) + bs_ref[h]
                y_ref[rows, cols] = (gu[rows, cols] * mixed).astype(BF16)
        ext_ref[halo:halo + tm, :] = zb_ref[:, :D_B].astype(F32) * _sigmoid(zb_ref[:, D_B:].astype(F32))
        prev = zh_ref[:, :D_B].astype(F32) * _sigmoid(zh_ref[:, D_B:].astype(F32))
        ext_ref[0:halo, :] = jnp.where(i > 0, prev, 0.0)
        for rb in range(tm // CONV_ROWS):
            for cb in range(D_B // CONV_COLS):
                cs = slice(cb * CONV_COLS, (cb + 1) * CONV_COLS)
                window = ext_ref[rb * CONV_ROWS:rb * CONV_ROWS + CONV_ROWS + halo, cs]
                acc = jnp.zeros((CONV_ROWS, CONV_COLS), F32)
                for k in range(B_CONV):
                    shifted = _rows_after(window, halo - (B_CONV - 1) + k)[:CONV_ROWS]
                    acc = acc + cw_ref[k:k + 1, cs] * shifted
                cb_ref[rb * CONV_ROWS:(rb + 1) * CONV_ROWS, cs] = acc + cbias_ref[:, cs]
        xhb, _ = _ln_stats(cb_ref[...])
        y_ref[:, D_A:] = _silu(xhb * blg_ref[...] + blb_ref[...]).astype(BF16)

    row = lambda i: (i, 0)
    par = lambda i: (0, 0)
    return _pallas(
        comm, body, name=name, grid=(T // tm,),
        in_specs=[pl.BlockSpec((tm, 2 * D_A), lambda i: (i, 0)),
                  pl.BlockSpec((tm, 2 * D_B), lambda i: (i, 1)),
                  pl.BlockSpec((halo, 2 * D_B), lambda i: (_halo_prev_index(tm, halo)(i), 1)),
                  pl.BlockSpec((1, D_A), par), pl.BlockSpec((1, D_A), par),
                  pl.BlockSpec((A_HEADS, CHUNK, CHUNK), lambda i: (0, 0, 0)),
                  pl.BlockSpec((A_HEADS, CHUNK, 1), lambda i: (0, 0, 0)),
                  pl.BlockSpec((B_CONV, D_B), par), pl.BlockSpec((1, D_B), par),
                  pl.BlockSpec((1, D_B), par), pl.BlockSpec((1, D_B), par)],
        out_specs=[pl.BlockSpec((tm, D_A + D_B), row), pl.BlockSpec((tm, D_B), row)],
        out_shape=[jax.ShapeDtypeStruct((T, D_A + D_B), BF16), jax.ShapeDtypeStruct((T, D_B), F32)],
        scratch_shapes=[pltpu.VMEM((halo + tm, D_B), F32)],
        compiler_params=_cparams("parallel"),
    )(z, z, z, a_ln_g, a_ln_b, w_s, b_s, conv_w, conv_b, b_ln_g, b_ln_b)


def _mixer_ab_bwd_pre(z, cb, dy, a_ln_g, a_ln_b, w_s, b_s, b_ln_g, b_ln_b, *, tm, name, comm=None):
    T = z.shape[0]
    nchunk = tm // CHUNK
    tn_dims = (((0,), (0,)), ((), ()))
    nt_dims = (((1,), (1,)), ((), ()))

    def body(za_ref, cb_ref, dy_ref, alg_ref, alb_ref, ws_ref, bs_ref, blg_ref, blb_ref,
             dza_ref, dcb_ref, dalg_ref, dalb_ref, dws_ref, dbs_ref, dblg_ref, dblb_ref,
             dlv_ref):
        @pl.when(pl.program_id(0) == 0)
        def _():
            for ref in (dalg_ref, dalb_ref, dws_ref, dbs_ref, dblg_ref, dblb_ref):
                ref[...] = jnp.zeros_like(ref)
        ua = za_ref[:, :D_A].astype(F32)
        va = za_ref[:, D_A:].astype(F32)
        gu = _gelu(ua)
        gv = _gelu(va)
        xh, r = _ln_stats(gv)
        alg = alg_ref[...]
        lv = (xh * alg + alb_ref[...]).astype(BF16)
        dya = dy_ref[:, :D_A].astype(F32)
        mask = _causal_mask()
        for h in range(A_HEADS):
            wm = jnp.where(mask, ws_ref[h], 0.0).astype(BF16)
            cols = slice(h * HEAD_DIM, (h + 1) * HEAD_DIM)
            dwm = jnp.zeros((CHUNK, CHUNK), F32)
            dbs = jnp.zeros((CHUNK, 1), F32)
            for c in range(nchunk):
                rows = slice(c * CHUNK, (c + 1) * CHUNK)
                lvb = lv[rows, cols]
                mixed = jnp.dot(wm, lvb, preferred_element_type=F32) + bs_ref[h]
                dyb = dya[rows, cols]
                dza_ref[rows, cols] = (dyb * mixed * _dgelu(ua[rows, cols])).astype(BF16)
                dmixed = dyb * gu[rows, cols]
                dmb = dmixed.astype(BF16)
                dlv_ref[rows, cols] = lax.dot_general(wm, dmb, tn_dims, preferred_element_type=F32)
                dwm = dwm + lax.dot_general(dmb, lvb, nt_dims, preferred_element_type=F32)
                dbs = dbs + jnp.sum(dmixed, axis=1, keepdims=True)
            dws_ref[h] += jnp.where(mask, dwm, 0.0)
            dbs_ref[h] += dbs
        dlv = dlv_ref[...]
        dalg_ref[...] += _rowsum(dlv * xh)
        dalb_ref[...] += _rowsum(dlv)
        dgv = _ln_bwd(dlv, xh, r, alg)
        dza_ref[:, D_A:] = (dgv * _dgelu(va)).astype(BF16)
        xhb, rb = _ln_stats(cb_ref[...])
        blg = blg_ref[...]
        lb = xhb * blg + blb_ref[...]
        dlb = dy_ref[:, D_A:].astype(F32) * _dsilu(lb)
        dblg_ref[...] += _rowsum(dlb * xhb)
        dblb_ref[...] += _rowsum(dlb)
        dcb_ref[...] = _ln_bwd(dlb, xhb, rb, blg)

    row = lambda i: (i, 0)
    par = lambda i: (0, 0)
    par3 = lambda i: (0, 0, 0)
    return _pallas(
        comm, body, name=name, grid=(T // tm,),
        in_specs=[pl.BlockSpec((tm, 2 * D_A), row), pl.BlockSpec((tm, D_B), row),
                  pl.BlockSpec((tm, D_A + D_B), row),
                  pl.BlockSpec((1, D_A), par), pl.BlockSpec((1, D_A), par),
                  pl.BlockSpec((A_HEADS, CHUNK, CHUNK), par3),
                  pl.BlockSpec((A_HEADS, CHUNK, 1), par3),
                  pl.BlockSpec((1, D_B), par), pl.BlockSpec((1, D_B), par)],
        out_specs=[pl.BlockSpec((tm, 2 * D_A), row), pl.BlockSpec((tm, D_B), row),
                   pl.BlockSpec((1, D_A), par), pl.BlockSpec((1, D_A), par),
                   pl.BlockSpec((A_HEADS, CHUNK, CHUNK), par3),
                   pl.BlockSpec((A_HEADS, CHUNK, 1), par3),
                   pl.BlockSpec((1, D_B), par), pl.BlockSpec((1, D_B), par)],
        out_shape=[jax.ShapeDtypeStruct((T, 2 * D_A + 2 * D_B), BF16), jax.ShapeDtypeStruct((T, D_B), F32),
                   jax.ShapeDtypeStruct((1, D_A), F32), jax.ShapeDtypeStruct((1, D_A), F32),
                   jax.ShapeDtypeStruct((A_HEADS, CHUNK, CHUNK), F32),
                   jax.ShapeDtypeStruct((A_HEADS, CHUNK, 1), F32),
                   jax.ShapeDtypeStruct((1, D_B), F32), jax.ShapeDtypeStruct((1, D_B), F32)],
        scratch_shapes=[pltpu.VMEM((tm, D_A), F32)],
        compiler_params=_cparams("arbitrary"),
    )(z, cb, dy, a_ln_g, a_ln_b, w_s, b_s, b_ln_g, b_ln_b)


def _mixer_b_conv_bwd(z, dcb, conv_w, dz, *, tm, name, comm=None):
    T = z.shape[0]
    halo = HALO_LONG

    def body(zb_ref, dcb_ref, dcn_ref, cw_ref, dz_in_ref, dzb_ref, dcw_ref, dbias_ref, dext_ref):
        i = pl.program_id(0)
        last = pl.num_programs(0) - 1

        @pl.when(i == 0)
        def _():
            dcw_ref[...] = jnp.zeros_like(dcw_ref)
            dbias_ref[...] = jnp.zeros_like(dbias_ref)
        dcb = dcb_ref[...]
        dext_ref[0:tm, :] = dcb
        dext_ref[tm:tm + halo, :] = jnp.where(i < last, dcn_ref[...], 0.0)
        dbias_ref[...] += _rowsum(dcb)
        for rb in range(tm // CONV_ROWS):
            for cb in range(D_B // CONV_COLS):
                cs = slice(cb * CONV_COLS, (cb + 1) * CONV_COLS)
                gcs = slice(D_B + cb * CONV_COLS, D_B + (cb + 1) * CONV_COLS)
                rs = slice(rb * CONV_ROWS, (rb + 1) * CONV_ROWS)
                xbb = zb_ref[rs, cs].astype(F32)
                sgb = _sigmoid(zb_ref[rs, gcs].astype(F32))
                yb0 = xbb * sgb
                window = dext_ref[rb * CONV_ROWS:rb * CONV_ROWS + CONV_ROWS + halo, cs]
                acc = jnp.zeros((CONV_ROWS, CONV_COLS), F32)
                for k in range(B_CONV):
                    shifted = _rows_after(window, (B_CONV - 1) - k)[:CONV_ROWS]
                    acc = acc + cw_ref[k:k + 1, cs] * shifted
                    dcw_ref[k:k + 1, cs] += _rowsum(shifted * yb0)
                dzb_ref[rs, cs] = (acc * sgb).astype(BF16)
                dzb_ref[rs, gcs] = (acc * xbb * sgb * (1.0 - sgb)).astype(BF16)

    row = lambda i: (i, 0)
    par = lambda i: (0, 0)
    return _pallas(
        comm, body, name=name, grid=(T // tm,),
        in_specs=[pl.BlockSpec((tm, 2 * D_B), lambda i: (i, 1)),
                  pl.BlockSpec((tm, D_B), row),
                  pl.BlockSpec((halo, D_B), lambda i: (_halo_next_index(tm, halo, T)(i), 0)),
                  pl.BlockSpec((B_CONV, D_B), par), pl.BlockSpec(memory_space=pl.ANY)],
        out_specs=[pl.BlockSpec((tm, 2 * D_B), lambda i: (i, 1)), pl.BlockSpec((B_CONV, D_B), par),
                   pl.BlockSpec((1, D_B), par)],
        out_shape=[jax.ShapeDtypeStruct(dz.shape, BF16), jax.ShapeDtypeStruct((B_CONV, D_B), F32),
                   jax.ShapeDtypeStruct((1, D_B), F32)],
        scratch_shapes=[pltpu.VMEM((tm + halo, D_B), F32)], aliases={4: 0},
        compiler_params=_cparams("arbitrary"),
    )(z, dcb, dcb, conv_w, dz)


def _rows_before(x, a):
    return x if a == 0 else pltpu.roll(x, a, axis=0)


def _rows_after(x, a):
    return x if a == 0 else pltpu.roll(x, x.shape[0] - a, axis=0)


def _conv3(w_ref, x, halo, cs):
    acc = w_ref[2:3, cs] * x[halo:]
    acc = acc + w_ref[1:2, cs] * _rows_before(x, 1)[halo:]
    return acc + w_ref[0:1, cs] * _rows_before(x, 2)[halo:]


def _mixer_c_fwd(z, conv_w, *, tm, name, comm=None):
    T = z.shape[0]
    D = D_MODEL
    halo = HALO_SHORT
    W = CONV_COLS

    def body(bg_ref, cg_ref, xv_ref, cgh_ref, xvh_ref, w_ref, r_ref):
        i = pl.program_id(0)
        for cb in range(D // W):
            cs = slice(cb * W, (cb + 1) * W)
            prev = jnp.where(i > 0, cgh_ref[:, cs].astype(F32) * xvh_ref[:, cs].astype(F32), 0.0)
            p = jnp.concatenate([prev, cg_ref[:, cs].astype(F32) * xv_ref[:, cs].astype(F32)], axis=0)
            r_ref[:, cs] = (bg_ref[:, cs].astype(F32) * _conv3(w_ref, p, halo, cs)).astype(BF16)

    hp = _halo_prev_index(tm, halo)
    return _pallas(
        comm, body, name=name, grid=(T // tm,),
        in_specs=[pl.BlockSpec((tm, D), lambda i: (i, 0)), pl.BlockSpec((tm, D), lambda i: (i, 1)),
                  pl.BlockSpec((tm, D), lambda i: (i, 2)),
                  pl.BlockSpec((halo, D), lambda i: (hp(i), 1)),
                  pl.BlockSpec((halo, D), lambda i: (hp(i), 2)),
                  pl.BlockSpec((None, C_CONV, D), lambda i: (0, 0, 0))],
        out_specs=pl.BlockSpec((tm, D), lambda i: (i, 0)),
        out_shape=jax.ShapeDtypeStruct((T, D), BF16),
        compiler_params=_cparams("parallel"),
    )(z, z, z, z, z, conv_w)


def _mixer_c_bwd(z, dr, conv_w, *, tm, name, comm=None):
    T = z.shape[0]
    D = D_MODEL
    halo = HALO_SHORT
    W = CONV_COLS

    def body(bg_ref, cg_ref, xv_ref, cgh_ref, xvh_ref, bgn_ref, dr_ref, drn_ref, w_ref, dz_ref, dw_ref):
        i = pl.program_id(0)
        last = pl.num_programs(0) - 1

        @pl.when(i == 0)
        def _():
            dw_ref[...] = jnp.zeros_like(dw_ref)
        for cb in range(D // W):
            cs = slice(cb * W, (cb + 1) * W)
            cg = cg_ref[:, cs].astype(F32)
            xv = xv_ref[:, cs].astype(F32)
            dr = dr_ref[:, cs].astype(F32)
            p = cg * xv
            prev = jnp.where(i > 0, cgh_ref[:, cs].astype(F32) * xvh_ref[:, cs].astype(F32), 0.0)
            q = _conv3(w_ref, jnp.concatenate([prev, p], axis=0), halo, cs)
            dz_ref[:, cs] = (dr * q).astype(BF16)
            nxt = jnp.where(i < last, drn_ref[:, cs].astype(F32) * bgn_ref[:, cs].astype(F32), 0.0)
            dq = jnp.concatenate([dr * bg_ref[:, cs].astype(F32), nxt], axis=0)
            dp = None
            for k in range(C_CONV):
                shifted = _rows_after(dq, 2 - k)[:tm]
                term = w_ref[k:k + 1, cs] * shifted
                dp = term if dp is None else dp + term
                dw_ref[k:k + 1, cs] += _rowsum(shifted * p)
            dz_ref[:, D + cb * W:D + (cb + 1) * W] = (dp * xv).astype(BF16)
            dz_ref[:, 2 * D + cb * W:2 * D + (cb + 1) * W] = (dp * cg).astype(BF16)

    hp = _halo_prev_index(tm, halo)
    hn = _halo_next_index(tm, halo, T)
    return _pallas(
        comm, body, name=name, grid=(T // tm,),
        in_specs=[pl.BlockSpec((tm, D), lambda i: (i, 0)), pl.BlockSpec((tm, D), lambda i: (i, 1)),
                  pl.BlockSpec((tm, D), lambda i: (i, 2)),
                  pl.BlockSpec((halo, D), lambda i: (hp(i), 1)),
                  pl.BlockSpec((halo, D), lambda i: (hp(i), 2)),
                  pl.BlockSpec((halo, D), lambda i: (hn(i), 0)),
                  pl.BlockSpec((tm, D), lambda i: (i, 0)),
                  pl.BlockSpec((halo, D), lambda i: (hn(i), 0)),
                  pl.BlockSpec((None, C_CONV, D), lambda i: (0, 0, 0))],
        out_specs=[pl.BlockSpec((tm, 3 * D), lambda i: (i, 0)),
                   pl.BlockSpec((C_CONV, D), lambda i: (0, 0))],
        out_shape=[jax.ShapeDtypeStruct((T, 3 * D), BF16), jax.ShapeDtypeStruct((C_CONV, D), F32)],
        compiler_params=_cparams("arbitrary"),
    )(z, z, z, z, z, z, dr, dr, conv_w)


FFN_COLS = 128


def _ffn_act_fwd(up, conv_w, *, layer, tm, name, comm=None):
    T = up.shape[0]
    halo = HALO_SHORT
    W = FFN_COLS

    def body(up_ref, uph_ref, w_ref, a_ref, upc_ref):
        i = pl.program_id(0)

        def conv(cs):
            prev = jnp.where(i > 0, uph_ref[:, cs], jnp.zeros((halo, W), BF16))
            return _conv3(w_ref, jnp.concatenate([prev, up_ref[:, cs]], axis=0).astype(F32), halo, cs)

        for cb in range(D_FF // W):
            gs = slice(cb * W, (cb + 1) * W)
            vs = slice(D_FF + cb * W, D_FF + (cb + 1) * W)
            g = conv(gs)
            v = conv(vs)
            upc_ref[:, gs] = g.astype(BF16)
            upc_ref[:, vs] = v.astype(BF16)
            a_ref[:, gs] = (_silu(g) * v).astype(BF16)

    return _pallas(
        comm, body, name=name, grid=(T // tm,),
        in_specs=[pl.BlockSpec((tm, 2 * D_FF), lambda i: (i, 0)),
                  pl.BlockSpec((halo, 2 * D_FF), lambda i: (_halo_prev_index(tm, halo)(i), 0)),
                  pl.BlockSpec((None, F_CONV, 2 * D_FF), lambda i: (layer, 0, 0))],
        out_specs=[pl.BlockSpec((tm, D_FF), lambda i: (i, 0)),
                   pl.BlockSpec((tm, 2 * D_FF), lambda i: (i, 0))],
        out_shape=[jax.ShapeDtypeStruct((T, D_FF), BF16), jax.ShapeDtypeStruct((T, 2 * D_FF), BF16)],
        compiler_params=_cparams("parallel"),
    )(up, up, conv_w)


def _ffn_act_bwd(up, upc, da, conv_w, *, layer, tm, name, comm=None):
    T = up.shape[0]
    halo = HALO_SHORT
    W = FFN_COLS

    def body(up_ref, upc_ref, upcn_ref, da_ref, dan_ref, w_ref, dup_ref, dw_ref):
        i = pl.program_id(0)
        last = pl.num_programs(0) - 1

        @pl.when(i == 0)
        def _():
            dw_ref[...] = jnp.zeros_like(dw_ref)
        live = jnp.where(i < last, 1.0, 0.0)
        for cb in range(D_FF // W):
            gs = slice(cb * W, (cb + 1) * W)
            vs = slice(D_FF + cb * W, D_FF + (cb + 1) * W)
            g = jnp.concatenate([upc_ref[:, gs], upcn_ref[:, gs]], axis=0).astype(F32)
            v = jnp.concatenate([upc_ref[:, vs], upcn_ref[:, vs]], axis=0).astype(F32)
            da = jnp.concatenate([da_ref[:, gs].astype(F32), dan_ref[:, gs].astype(F32) * live], axis=0)
            s = _sigmoid(g)
            silu = g * s
            grads = (da * v * (s * (1.0 + g * (1.0 - s))), da * silu)
            for cs, d in zip((gs, vs), grads):
                u = up_ref[:, cs].astype(F32)
                acc = None
                for k in range(F_CONV):
                    shifted = _rows_after(d, 2 - k)[:tm]
                    term = w_ref[k:k + 1, cs] * shifted
                    acc = term if acc is None else acc + term
                    dw_ref[k:k + 1, cs] += _rowsum(shifted * u)
                dup_ref[:, cs] = acc.astype(BF16)

    hn = _halo_next_index(tm, halo, T)
    return _pallas(
        comm, body, name=name, grid=(T // tm,),
        in_specs=[pl.BlockSpec((tm, 2 * D_FF), lambda i: (i, 0)),
                  pl.BlockSpec((tm, 2 * D_FF), lambda i: (i, 0)),
                  pl.BlockSpec((halo, 2 * D_FF), lambda i: (hn(i), 0)),
                  pl.BlockSpec((tm, D_FF), lambda i: (i, 0)),
                  pl.BlockSpec((halo, D_FF), lambda i: (hn(i), 0)),
                  pl.BlockSpec((None, F_CONV, 2 * D_FF), lambda i: (layer, 0, 0))],
        out_specs=[pl.BlockSpec((tm, 2 * D_FF), lambda i: (i, 0)),
                   pl.BlockSpec((F_CONV, 2 * D_FF), lambda i: (0, 0))],
        out_shape=[jax.ShapeDtypeStruct((T, 2 * D_FF), BF16),
                   jax.ShapeDtypeStruct((F_CONV, 2 * D_FF), F32)],
        compiler_params=_cparams("arbitrary"),
    )(up, upc, upc, da, da, conv_w)


def _local_step(x, tgt, small, plan):
    T = x.shape[0]
    tm_e = _pick(T, 256)
    tm_a = _pick(T, 512)
    tm_b = _pick(T, 128)
    tm = _pick(T, 1024)
    tm_f = _pick(T, 512)
    tt = _pick(T, 2048)
    nm = small["norm_mix"].reshape(2, 1, D_MODEL)
    nf = small["norm_ffn"].reshape(2, 1, D_MODEL)
    ngf = small["norm_final"].reshape(1, D_MODEL)
    b_s = small["a_b_s"].reshape(A_HEADS, CHUNK, 1)
    w_s = small["a_w_s"].reshape(A_HEADS, CHUNK, CHUNK)
    b_conv_w = small["b_conv_w"].reshape(B_CONV, D_B)
    sg = {}
    wt, cm = plan.weight, plan.comm

    h_m0, z_ab = _norm_mm_nn(x, nm, wt("ab_w_in", 0), g_layer=0, tm=tm, tn=512, name="ab_in", comm=cm("ab_in"))
    yab, cb = _mixer_ab_fwd(z_ab, small["a_ln_g"], small["a_ln_b"], w_s, b_s, b_conv_w, small["b_conv_b"],
                            small["b_ln_g"], small["b_ln_b"], tm=tm_e, name="mixer_ab", comm=cm("mixer_ab"))
    x1, h_f0 = _mm_nn(yab, wt("ab_w_out", 0), layer=0, tm=tm, tn=D_MODEL, residual=x, norm=(nf, 0),
                      name="ab_out", comm=cm("ab_out"))

    def ffn_fwd(xin, h, layer):
        up = _mm_nn(h, wt("f_w_up", layer), layer=0, tm=tm, tn=2 * 1408, out_dtype=BF16, name=f"ffn_up{layer}",
                    comm=cm(f"ffn_up{layer}"))
        a, upc = _ffn_act_fwd(up, small["f_conv_w"], layer=layer, tm=tm_a, name=f"ffn_act{layer}",
                              comm=cm(f"ffn_act{layer}"))
        if layer == 0:
            out = _mm_nn(a, wt("f_w_down", layer), layer=0, tm=tm, tn=D_MODEL, residual=xin, norm=(nm, 1),
                         name=f"ffn_down{layer}", comm=cm(f"ffn_down{layer}"))
        else:
            out = _mm_nn_loss(a, wt("f_w_down", layer), xin, tgt, ngf, tm=tm, name=f"ffn_down{layer}",
                              comm=cm(f"ffn_down{layer}"))
        return up, upc, a, out

    up0, upc0, a0, (x2, h_m1) = ffn_fwd(x1, h_f0, 0)
    z_c = _mm_nn(h_m1, wt("c_w_in", 0), layer=0, tm=tm, tn=768, out_dtype=BF16, name="c_in", comm=cm("c_in"))
    r = _mixer_c_fwd(z_c, small["c_conv_w"], tm=tm_e, name="mixer_c", comm=cm("mixer_c"))
    x3, h_f1 = _mm_nn(r, wt("c_w_out", 0), layer=0, tm=tm, tn=D_MODEL, residual=x2, norm=(nf, 1),
                      name="c_out", comm=cm("c_out"))
    up1, upc1, a1, (loss, dx, sg["norm_final"]) = ffn_fwd(x3, h_f1, 1)

    def ffn_bwd(dx, xin, h, up, upc, a, layer):
        da = _mm_nt(dx, wt("f_w_down", layer), layer=0, tm=tm, tn=1408, out_dtype=BF16,
                    name=f"ffn_down_dx{layer}", comm=cm(f"ffn_down_dx{layer}"))
        plan.grad_ready("f_w_down", layer, _mm_tn(a, dx, shards=None, tk=1408, tn=1024, tt=tt,
                                                  name=f"ffn_down_dw{layer}", comm=cm(f"ffn_down_dw{layer}")))
        dup, dcw = _ffn_act_bwd(up, upc, da, small["f_conv_w"], layer=layer, tm=tm_b, name=f"ffn_act_bwd{layer}",
                                comm=cm(f"ffn_act_bwd{layer}"))
        dxin, dg = _mm_nt_norm(dup, wt("f_w_up", layer), xin, nf, dx, g_layer=layer, tm=tm_f,
                               name=f"ffn_up_dx{layer}", comm=cm(f"ffn_up_dx{layer}"))
        plan.grad_ready("f_w_up", layer, _mm_tn(h, dup, shards=N_CHIPS, tk=512, tn=2 * 1408, tt=tt,
                                                name=f"ffn_up_dw{layer}", comm=cm(f"ffn_up_dw{layer}")))
        return dxin, dg, dcw

    dx, dnf1, dfc1 = ffn_bwd(dx, x3, h_f1, up1, upc1, a1, 1)
    dr = _mm_nt(dx, wt("c_w_out", 0), layer=0, tm=tm, tn=512, out_dtype=BF16, name="c_out_dx", comm=cm("c_out_dx"))
    plan.grad_ready("c_w_out", 0, _mm_tn(r, dx, shards=None, tk=1024, tn=1024, tt=tt, name="c_out_dw",
                                         comm=cm("c_out_dw")))
    dz_c, dccw = _mixer_c_bwd(z_c, dr, small["c_conv_w"], tm=tm_e, name="mixer_c_bwd", comm=cm("mixer_c_bwd"))
    sg["c_conv_w"] = dccw.reshape(1, C_CONV, D_MODEL)
    plan.grad_ready("c_w_in", 0, _mm_tn(h_m1, dz_c, shards=N_CHIPS, tk=1024, tn=768, tt=tt, name="c_in_dw",
                                        comm=cm("c_in_dw")))
    dx, dnm1 = _mm_nt_norm(dz_c, wt("c_w_in", 0), x2, nm, dx, g_layer=1, tm=tm_f, name="c_in_dx",
                           comm=cm("c_in_dx"))
    dx, dnf0, dfc0 = ffn_bwd(dx, x1, h_f0, up0, upc0, a0, 0)
    dyab = _mm_nt(dx, wt("ab_w_out", 0), layer=0, tm=tm, tn=512, out_dtype=BF16, name="ab_out_dx",
                  comm=cm("ab_out_dx"))
    plan.grad_ready("ab_w_out", 0, _mm_tn(yab, dx, shards=None, tk=1024, tn=1024, tt=tt, name="ab_out_dw",
                                          comm=cm("ab_out_dw")))
    (dza, dcb, sg["a_ln_g"], sg["a_ln_b"], dws, dbs, sg["b_ln_g"], sg["b_ln_b"]) = _mixer_ab_bwd_pre(
        z_ab, cb, dyab, small["a_ln_g"], small["a_ln_b"], w_s, b_s, small["b_ln_g"], small["b_ln_b"],
        tm=tm_e, name="mixer_ab_bwd", comm=cm("mixer_ab_bwd"))
    dz_ab, dbcw, sg["b_conv_b"] = _mixer_b_conv_bwd(z_ab, dcb, b_conv_w, dza, tm=tm_e, name="mixer_b_conv_bwd",
                                                    comm=cm("mixer_b_conv_bwd"))
    sg["a_w_s"] = dws.reshape(1, A_HEADS, CHUNK, CHUNK)
    sg["a_b_s"] = dbs.reshape(1, A_HEADS, CHUNK)
    sg["b_conv_w"] = dbcw.reshape(1, B_CONV, D_B)
    plan.grad_ready("ab_w_in", 0, _mm_tn(h_m0, dz_ab, shards=N_CHIPS, tk=1024, tn=512, tt=tt, name="ab_in_dw",
                                         comm=cm("ab_in_dw")))
    dx, dnm0 = _mm_nt_norm(dz_ab, wt("ab_w_in", 0), x, nm, dx, g_layer=0, tm=tm_f, name="ab_in_dx",
                           comm=cm("ab_in_dx"))

    sg["norm_mix"] = [dnm0, dnm1]
    sg["norm_ffn"] = [dnf0, dnf1]
    sg["f_conv_w"] = [dfc0, dfc1]
    return loss, dx, sg


BLOCK_BYTES = 3 * 1024 * 1024


BF16_SUBLANES = 16


def _row_tile(rows, row_bytes, step=SUBLANES):
    best = None
    for tr in range(step, rows + 1, step):
        if rows % tr == 0 and tr * row_bytes <= BLOCK_BYTES:
            best = tr
    if best is None:
        raise ValueError(f"no row tile for {rows}")
    return best


def _place_scalars():
    x, y, c = lax.axis_index("x"), lax.axis_index("y"), lax.axis_index("c")
    return jnp.stack([c, 2 * x + y, 2 * (1 - x) + y, 2 * x + (1 - y), 2 * (1 - x) + (1 - y)]).astype(jnp.int32)


def _cast_into_slot(w, place, *, layer, paired, name):
    L, rows, cols = w.shape
    tr = _row_tile(rows, cols * 4, BF16_SUBLANES)

    def body(place_ref, w_ref, o_ref):
        o_ref[...] = w_ref[...].astype(BF16)

    if paired:
        out_spec = pl.BlockSpec((None, None, tr, cols), lambda i, p: (0, p[1] // 2, i, p[1] % 2))
        out_shape = jax.ShapeDtypeStruct((1, N_CHIPS // 2, rows, 2 * cols), BF16)
    else:
        out_spec = pl.BlockSpec((None, None, tr, cols), lambda i, p: (0, p[1], i, 0))
        out_shape = jax.ShapeDtypeStruct((1, N_CHIPS, rows, cols), BF16)
    return pl.pallas_call(
        body, name=name,
        grid_spec=pltpu.PrefetchScalarGridSpec(
            num_scalar_prefetch=1, grid=(rows // tr,),
            in_specs=[pl.BlockSpec((None, tr, cols), lambda i, p: (layer, i, 0))],
            out_specs=out_spec),
        out_shape=out_shape,
        compiler_params=_cparams("parallel"),
    )(place, w)


def _pair_sum(g, theirs, place, *, name):
    S, rows, cols = g.shape
    half = rows // 2
    tr = _row_tile(half, cols * 4, BF16_SUBLANES)
    nb = half // tr

    def body(place_ref, g_ref, t_ref, o_ref):
        o_ref[...] = (g_ref[...] + t_ref[...]).astype(BF16)

    spec = pl.BlockSpec((None, tr, cols), lambda s, i, p: (s, i, 0))
    return pl.pallas_call(
        body, name=name,
        grid_spec=pltpu.PrefetchScalarGridSpec(
            num_scalar_prefetch=1, grid=(S, nb),
            in_specs=[pl.BlockSpec((None, tr, cols), lambda s, i, p: (s, p[0] * nb + i, 0)), spec],
            out_specs=spec),
        out_shape=jax.ShapeDtypeStruct((S, half, cols), BF16),
        compiler_params=_cparams("parallel", "parallel"),
    )(place, g, theirs)


def _chip_sum(p, r, g_prev, place, *, layer, shape, name):
    L, rows, cols = shape
    half = rows // 2
    tr = _row_tile(half, cols * 4, BF16_SUBLANES)
    nb = half // tr

    def body(place_ref, p_ref, r_ref, *rest):
        o_ref = rest[-1]
        mine = p_ref[...].astype(F32)
        peers = [r_ref[j].astype(F32) for j in range(3)]
        acc = None
        for s in range(N_CHIPS):
            term = jnp.where(place_ref[1] == s, mine,
                             jnp.where(place_ref[2] == s, peers[0],
                                       jnp.where(place_ref[3] == s, peers[1], peers[2])))
            acc = term if acc is None else acc + term
        o_ref[...] = acc

    in_specs = [pl.BlockSpec((None, tr, cols), lambda i, pr: (pr[1], i, 0)),
                pl.BlockSpec((3, tr, cols), lambda i, pr: (0, i, 0))]
    args = [place, p, r]
    aliases = {}
    if g_prev is not None:
        in_specs.append(HBM_REF)
        args.append(g_prev)
        aliases = {3: 0}
    return pl.pallas_call(
        body, name=name,
        grid_spec=pltpu.PrefetchScalarGridSpec(
            num_scalar_prefetch=1, grid=(nb,), in_specs=in_specs,
            out_specs=pl.BlockSpec((None, tr, cols), lambda i, pr: (layer, pr[0] * nb + i, 0))),
        out_shape=jax.ShapeDtypeStruct(shape, F32), input_output_aliases=aliases,
        compiler_params=_cparams("parallel"),
    )(*args)


def _adamw_math(w, g, m, v):
    m2 = ADAM_B1 * m + (1.0 - ADAM_B1) * g
    v2 = ADAM_B2 * v + (1.0 - ADAM_B2) * (g * g)
    m_hat = m2 / (1.0 - ADAM_B1 ** ADAM_STEP)
    v_hat = v2 / (1.0 - ADAM_B2 ** ADAM_STEP)
    delta = -ADAM_LR * (m_hat / (jnp.sqrt(v_hat) + ADAM_EPS) + ADAM_WD * w)
    return delta, m2, v2


def _adamw(w, g, m, v, *, name):
    L, rows, cols = w.shape
    tr = _row_tile(rows, cols * 4)

    def body(w_ref, g_ref, m_ref, v_ref, go_ref, d_ref, m2_ref, v2_ref):
        g = g_ref[...]
        d, m2, v2 = _adamw_math(w_ref[...], g, m_ref[...], v_ref[...])
        go_ref[...] = g
        d_ref[...] = d
        m2_ref[...] = m2
        v2_ref[...] = v2

    spec = pl.BlockSpec((None, tr, cols), lambda l, i: (l, i, 0))
    shape = jax.ShapeDtypeStruct(w.shape, F32)
    return pl.pallas_call(
        body, name=name, grid=(L, rows // tr), in_specs=[spec] * 4, out_specs=[spec] * 4,
        out_shape=[shape] * 4,
        compiler_params=_cparams("parallel", "parallel"),
    )(w, g, m, v)


def _allreduce_pack(pack, *, name, comm):
    R = pack.shape[0]
    half = R // 2
    nr, nw = len(comm.reads), len(comm.writes)

    def body(*refs):
        p_ref, rd, wr_in = refs[0], refs[1:1 + nr], refs[1 + nr:1 + nr + nw]
        o_ref, wr_out = refs[1 + nr + nw], refs[2 + nr + nw:2 + nr + 2 * nw]
        sib_ref, chip_ref, parts_ref, sems, comm_sems = refs[2 + nr + 2 * nw:]
        src = dict(zip(comm.reads, rd))
        src.update(zip(comm.writes, wr_in))
        dst = dict(zip(comm.writes, wr_out))
        comm.start(src, dst, comm_sems)
        x, y, c, k, sib, peers = _place()
        swap = _remote(p_ref, sib_ref, sems.at[0, 0], sems.at[0, 1], sib)
        swap.start()
        swap.wait()
        chip_ref[...] = p_ref[...] + sib_ref[...]
        mine = chip_ref.at[pl.ds(pl.multiple_of(c * half, SUBLANES), half)]
        sends = [_remote(mine, parts_ref.at[j], sems.at[1 + j, 0], sems.at[1 + j, 1], (px, py, c))
                 for j, (px, py) in enumerate(peers)]
        for rc in sends:
            rc.start()
        for rc in sends:
            rc.wait()
        own = mine[...]
        others = [parts_ref[j] for j in range(3)]
        acc = None
        for s in range(N_CHIPS):
            term = own
            for j, (px, py) in enumerate(peers):
                term = jnp.where(2 * px + py == s, others[j], term)
            acc = term if acc is None else acc + term
        done = o_ref.at[pl.ds(pl.multiple_of(c * half, SUBLANES), half)]
        done[...] = acc
        theirs = o_ref.at[pl.ds(pl.multiple_of((1 - c) * half, SUBLANES), half)]
        share = _remote(done, done, sems.at[4, 0], sems.at[4, 1], sib)
        share.start()
        _remote(done, theirs, sems.at[4, 0], sems.at[4, 1], sib).wait()
        comm.finish(src, dst, comm_sems)

    vm = pl.BlockSpec(memory_space=pltpu.VMEM)
    operands, shapes = _comm_operands(comm)
    outs = pl.pallas_call(
        body, name=name, in_specs=[vm] + [HBM_REF] * (nr + nw), out_specs=[vm] + [HBM_REF] * nw,
        out_shape=[jax.ShapeDtypeStruct((R, LANES), F32)] + shapes,
        input_output_aliases={1 + nr + q: 1 + q for q in range(nw)},
        scratch_shapes=[pltpu.VMEM((R, LANES), F32), pltpu.VMEM((R, LANES), F32),
                        pltpu.VMEM((3, half, LANES), F32), pltpu.SemaphoreType.DMA((5, 2)),
                        pltpu.SemaphoreType.DMA((comm.ncopies, 2))],
        compiler_params=pltpu.CompilerParams(vmem_limit_bytes=VMEM_BYTES_MAX),
    )(pack, *operands)
    for q, n in enumerate(comm.writes):
        comm.plan.bufs[n] = outs[1 + q]
    return outs[0]


PACK_UNIT = SUBLANES * LANES


def _pack(arrays):
    flat, sizes = [], []
    for a in arrays:
        pieces = a if isinstance(a, (list, tuple)) else [a]
        v = jnp.concatenate([p.reshape(-1) for p in pieces]) if len(pieces) > 1 else pieces[0].reshape(-1)
        size = v.shape[0]
        padded = -(-size // PACK_UNIT) * PACK_UNIT
        flat.append(jnp.pad(v, (0, padded - size)))
        sizes.append((size, padded))
    total = sum(p for _, p in sizes)
    if (total // PACK_UNIT) % 2:
        flat.append(jnp.zeros((PACK_UNIT,), F32))
    return jnp.concatenate(flat).reshape(-1, LANES), sizes


def _unpack(pack, sizes, shapes):
    v = pack.reshape(-1)
    out, off = [], 0
    for (size, padded), shape in zip(sizes, shapes):
        out.append(v[off:off + size].reshape(shape))
        off += padded
    return out


BIG = ("ab_w_in", "ab_w_out", "c_w_in", "c_w_out", "f_w_up", "f_w_down")
COL_SHARDED = ("ab_w_in", "c_w_in", "f_w_up")
PAIRED = ("f_w_up",)
SMALL_REPLICATED = ("norm_mix", "norm_ffn", "norm_final", "a_ln_g", "a_ln_b", "a_w_s", "a_b_s",
                    "b_conv_b", "b_ln_g", "b_ln_b")
SMALL_SHARDED = ("b_conv_w", "c_conv_w", "f_conv_w")
SMALL = SMALL_REPLICATED + SMALL_SHARDED
ALL_WEIGHTS = ("norm_mix", "norm_ffn", "norm_final", "ab_w_in", "a_ln_g", "a_ln_b", "a_w_s", "a_b_s",
               "b_conv_w", "b_conv_b", "b_ln_g", "b_ln_b", "ab_w_out", "c_w_in", "c_conv_w", "c_w_out",
               "f_w_up", "f_conv_w", "f_w_down")


SCHEDULE = {
    "ab_in": [("gi", "f_w_up", 0, 0, 4), ("gi", "ab_w_out", 0)],
    "mixer_ab": [("gd", "f_w_up", 0, 0, 4), ("gd", "ab_w_out", 0), ("gi", "f_w_up", 0, 1, 4),
                 ("gi", "f_w_up", 0, 2, 4), ("gi", "f_w_up", 0, 3, 4)],
    "ab_out": [("gd", "f_w_up", 0, 1, 4), ("gd", "f_w_up", 0, 2, 4), ("gd", "f_w_up", 0, 3, 4)],
    "ffn_up0": [("gi", "f_w_down", 0), ("gi", "c_w_in", 0, 0, 2)],
    "ffn_act0": [("gd", "f_w_down", 0), ("gd", "c_w_in", 0, 0, 2), ("gi", "c_w_in", 0, 1, 2),
                 ("gi", "f_w_up", 1, 0, 4)],
    "ffn_down0": [("gd", "c_w_in", 0, 1, 2), ("gd", "f_w_up", 1, 0, 4), ("gi", "f_w_up", 1, 1, 4),
                  ("gi", "c_w_out", 0)],
    "c_in": [("gd", "f_w_up", 1, 1, 4), ("gd", "c_w_out", 0), ("gi", "f_w_up", 1, 2, 4),
             ("gi", "f_w_up", 1, 3, 4)],
    "mixer_c": [("gd", "f_w_up", 1, 2, 4), ("gd", "f_w_up", 1, 3, 4)],
    "ffn_up1": [("gi", "f_w_down", 1)],
    "ffn_act1": [("gd", "f_w_down", 1)],
    "ffn_act_bwd1": [("px", "f_w_down", 1)],
    "ffn_up_dx1": [("cx", "f_w_down", 1)],
    "c_out_dx": [("px", "f_w_up", 1)],
    "mixer_c_bwd": [("cx", "f_w_up", 1, 0, 4), ("px", "c_w_out", 0)],
    "c_in_dx": [("cx", "f_w_up", 1, 1, 4), ("px", "c_w_in", 0)],
    "ffn_act_bwd0": [("cx", "f_w_up", 1, 2, 4), ("cx", "c_w_out", 0), ("cx", "c_w_in", 0),
                     ("px", "f_w_down", 0)],
    "ffn_up_dx0": [("cx", "f_w_down", 0), ("cx", "f_w_up", 1, 3, 4)],
    "ffn_up_dw0": [("ps", "f_w_down", 1), ("ps", "f_w_up", 1)],
    "ab_out_dx": [("px", "f_w_up", 0)],
    "mixer_ab_bwd": [("cx", "f_w_up", 0, 0, 4), ("px", "ab_w_out", 0)],
    "mixer_b_conv_bwd": [("cx", "f_w_up", 0, 1, 4), ("cx", "f_w_up", 0, 2, 4), ("cx", "ab_w_out", 0),
                         ("ps", "c_w_out", 0), ("ps", "c_w_in", 0), ("ps", "f_w_down", 0)],
    "ab_in_dx": [("cx", "f_w_up", 0, 3, 4), ("px", "ab_w_in", 0)],
}


class _Plan:
    def __init__(self, shapes, place):
        self.shapes, self.place, self.bufs = shapes, place, {}
        self.summed, self.shared = set(), set()

    def weight(self, name, layer):
        g = self.bufs[f"w:{name}:{layer}"]
        if name in COL_SHARDED:
            return g
        _, S, rows, cols = g.shape
        return g.reshape(1, S * rows, cols)

    def grad_ready(self, name, layer, g):
        _, rows, cols = self.shapes[name]
        hbm = lambda a: pltpu.with_memory_space_constraint(a, pltpu.HBM)
        self.bufs[f"g:{name}:{layer}"] = g.reshape(N_CHIPS, rows, cols)
        self.bufs[f"t:{name}:{layer}"] = hbm(lax.empty((N_CHIPS, rows // 2, cols), F32))
        self.bufs[f"l:{name}:{layer}"] = hbm(lax.empty((3, rows // 2, cols), BF16))

    def job(self, kind, name, layer, part=0, parts=1):
        _, rows, cols = self.shapes[name]
        key = f"{name}:{layer}"
        if kind == "gi":
            return _job_gather_ici("w:" + key, rows, cols, part, parts)
        if kind == "gd":
            return _job_gather_d2d("w:" + key, rows, cols, part, parts)
        if kind == "px":
            return _job_pair_exchange("g:" + key, "t:" + key, rows, part, parts)
        if kind == "cx":
            if "p:" + key not in self.bufs:
                self.bufs["p:" + key] = _pair_sum(self.bufs["g:" + key], self.bufs["t:" + key], self.place,
                                                  name=f"pair_sum_{name}{layer}")
            nr = rows // 2 // parts
            return _job_chip_exchange("p:" + key, "l:" + key, part * nr, nr)
        if kind == "ps":
            self.chip_sum(name, layer)
            self.shared.add(key)
            return _job_pair_share("G:" + name, layer, rows)
        raise ValueError(kind)

    def chip_sum(self, name, layer):
        key = f"{name}:{layer}"
        if key not in self.summed:
            self.summed.add(key)
            self.bufs["G:" + name] = _chip_sum(self.bufs["p:" + key], self.bufs["l:" + key],
                                               self.bufs.get("G:" + name), self.place, layer=layer,
                                               shape=self.shapes[name], name=f"chip_sum_{name}{layer}")

    def comm(self, call):
        specs = SCHEDULE.get(call)
        return None if specs is None else _Comm(self, [self.job(*spec) for spec in specs])


def _step(x, tgt, w, m, v):
    chip = 2 * lax.axis_index("x") + lax.axis_index("y")
    place = _place_scalars()
    plan = _Plan({n: w[n].shape for n in BIG}, place)
    items = [(n, l) for n in BIG for l in range(w[n].shape[0])]

    for n, l in items:
        plan.bufs[f"w:{n}:{l}"] = _cast_into_slot(w[n], place, layer=l, paired=n in PAIRED, name=f"cast_{n}{l}")
    conv_pack, conv_sizes = _pack([w[n] for n in SMALL_SHARDED])
    hbm = lambda a: pltpu.with_memory_space_constraint(a, pltpu.HBM)
    plan.bufs["conv:mine"] = hbm(conv_pack)
    plan.bufs["conv:all"] = hbm(lax.empty((N_CHIPS,) + conv_pack.shape, F32))
    _comm_only(plan, [[plan.job("gi", "ab_w_in", 0), _job_chip_gather("conv:mine", "conv:all")],
                      [plan.job("gd", "ab_w_in", 0)]], name="gather_first")
    conv_shapes = [w[n].shape for n in SMALL_SHARDED]
    per_chip = [_unpack(plan.bufs["conv:all"][s], conv_sizes, conv_shapes) for s in range(N_CHIPS)]
    small = {n: w[n] for n in SMALL_REPLICATED}
    for idx, n in enumerate(SMALL_SHARDED):
        small[n] = jnp.concatenate([jnp.where(chip == s, w[n], per_chip[s][idx]) for s in range(N_CHIPS)], axis=-1)

    loss, dx, sg = _local_step(x, tgt, small, plan)

    g_pack, g_sizes = _pack([sg[n] for n in SMALL] + [loss])
    g_sum = _allreduce_pack(g_pack, name="allreduce_small_grads",
                            comm=_Comm(plan, [plan.job("cx", "ab_w_in", 0)]))
    full_shapes = [small[n].shape for n in SMALL]
    *summed, loss = _unpack(g_sum, g_sizes, full_shapes + [(1, 1)])
    g_small = dict(zip(SMALL, summed))
    for n in SMALL_SHARDED:
        width = w[n].shape[-1]
        g_small[n] = lax.dynamic_slice_in_dim(g_small[n], chip * width, width, axis=g_small[n].ndim - 1)

    _comm_only(plan, [[plan.job("ps", n, l) for n, l in items if f"{n}:{l}" not in plan.shared]],
               name="reduce_pair_share")
    grads_big = [plan.bufs["G:" + n] for n in BIG]

    grad, delta, new_m, new_v = {}, {}, {}, {}
    for n, g in zip(BIG, grads_big):
        grad[n], delta[n], new_m[n], new_v[n] = _adamw(w[n], g, m[n], v[n], name=f"adamw_{n}")
    shapes = [w[n].shape for n in SMALL]
    wp, sizes = _pack([w[n] for n in SMALL])
    gp, _ = _pack([g_small[n] for n in SMALL])
    mp, _ = _pack([m[n] for n in SMALL])
    vp, _ = _pack([v[n] for n in SMALL])
    R = wp.shape[0]
    _, dp, m2p, v2p = _adamw(wp.reshape(1, R, LANES), gp.reshape(1, R, LANES), mp.reshape(1, R, LANES),
                             vp.reshape(1, R, LANES), name="adamw_small")
    for n, d_, m_, v_ in zip(SMALL, _unpack(dp, sizes, shapes), _unpack(m2p, sizes, shapes),
                             _unpack(v2p, sizes, shapes)):
        grad[n] = g_small[n]
        delta[n], new_m[n], new_v[n] = d_, m_, v_
    return loss, dx, grad, delta, new_m, new_v


def kernel(x, norm_mix, norm_ffn, norm_final, ab_w_in, a_ln_g, a_ln_b, a_w_s, a_b_s, b_conv_w, b_conv_b, b_ln_g, b_ln_b, ab_w_out, c_w_in, c_conv_w, c_w_out, f_w_up, f_conv_w, f_w_down, loss_target, m_norm_mix, m_norm_ffn, m_norm_final, m_ab_w_in, m_a_ln_g, m_a_ln_b, m_a_w_s, m_a_b_s, m_b_conv_w, m_b_conv_b, m_b_ln_g, m_b_ln_b, m_ab_w_out, m_c_w_in, m_c_conv_w, m_c_w_out, m_f_w_up, m_f_conv_w, m_f_w_down, v_norm_mix, v_norm_ffn, v_norm_final, v_ab_w_in, v_a_ln_g, v_a_ln_b, v_a_w_s, v_a_b_s, v_b_conv_w, v_b_conv_b, v_b_ln_g, v_b_ln_b, v_ab_w_out, v_c_w_in, v_c_conv_w, v_c_w_out, v_f_w_up, v_f_conv_w, v_f_w_down):
    given = dict(locals())
    w = {n: given[n] for n in ALL_WEIGHTS}
    m = {n: given["m_" + n] for n in ALL_WEIGHTS}
    v = {n: given["v_" + n] for n in ALL_WEIGHTS}
    T = x.shape[1]
    loss, dx, grad, delta, new_m, new_v = _step(x.reshape(T, D_MODEL), loss_target.reshape(T, D_MODEL), w, m, v)
    out = [loss[0, 0], dx.reshape(x.shape)]
    for d in (grad, delta, new_m, new_v):
        out += [d[n] for n in ALL_WEIGHTS]
    return tuple(out)
```

```python
import functools
import math

import jax
import jax.numpy as jnp
from jax import lax
from jax.experimental import pallas as pl
from jax.experimental.pallas import tpu as pltpu

F32 = jnp.float32
BF16 = jnp.bfloat16

EPS = 1e-6
D_MODEL = 1024
CHUNK = 128
HEAD_DIM = 128
A_HEADS = 4
D_A = 512
D_B = 512
B_CONV = 31
C_CONV = 3
D_FF = 2816
F_CONV = 3
N_CHIPS = 4

ADAM_LR = 0.001
ADAM_B1 = 0.9
ADAM_B2 = 0.999
ADAM_EPS = 1e-08
ADAM_WD = 0.01
ADAM_STEP = 10

SUBLANES = 8
LANES = 128
HALO_SHORT = 16
HALO_LONG = 32
VMEM_BYTES_MAX = 60000 * 1024

INV_SQRT2 = 1.0 / math.sqrt(2.0)
INV_SQRT_2PI = 1.0 / math.sqrt(2.0 * math.pi)

MESH = pl.DeviceIdType.MESH


def _cparams(*sem):
    return pltpu.CompilerParams(dimension_semantics=sem, vmem_limit_bytes=VMEM_BYTES_MAX)


def _pick(total, pref):
    for c in (2048, 1024, 512, 256, 128):
        if c <= pref and total % c == 0:
            return c
    raise ValueError(f"no tile for {total}")


def _sigmoid(x):
    return jax.nn.sigmoid(x)


def _silu(x):
    return x * _sigmoid(x)


def _dsilu(x):
    s = _sigmoid(x)
    return s * (1.0 + x * (1.0 - s))


def _gelu(x):
    return 0.5 * x * (1.0 + lax.erf(x * INV_SQRT2))


def _dgelu(x):
    return 0.5 * (1.0 + lax.erf(x * INV_SQRT2)) + x * jnp.exp(-0.5 * x * x) * INV_SQRT_2PI


def _ln_stats(x):
    mu = jnp.mean(x, axis=-1, keepdims=True)
    xc = x - mu
    var = jnp.mean(xc * xc, axis=-1, keepdims=True)
    r = lax.rsqrt(var + EPS)
    return xc * r, r


def _ln_bwd(dy, xh, r, g):
    dxh = dy * g
    m1 = jnp.mean(dxh, axis=-1, keepdims=True)
    m2 = jnp.mean(dxh * xh, axis=-1, keepdims=True)
    return r * (dxh - m1 - xh * m2)


def _rowsum(x):
    return jnp.sum(x, axis=0, keepdims=True)


HBM_REF = pl.BlockSpec(memory_space=pltpu.HBM)


def _place():
    x, y, c = lax.axis_index("x"), lax.axis_index("y"), lax.axis_index("c")
    peers = [(1 - x, y), (x, 1 - y), (1 - x, 1 - y)]
    return x, y, c, 2 * x + y, (x, y, 1 - c), peers


def _half(rows, which):
    return pl.ds(which * (rows // 2), rows // 2)


def _remote(src, dst, send_sem, recv_sem, device):
    return pltpu.make_async_remote_copy(src_ref=src, dst_ref=dst, send_sem=send_sem, recv_sem=recv_sem,
                                        device_id=device, device_id_type=MESH)


class _Job:
    def __init__(self, reads, writes, ncopies, copies):
        self.reads, self.writes, self.ncopies, self.copies = reads, writes, ncopies, copies


def _share(rows, which, part, parts):
    nr = rows // 2 // parts
    return pl.ds(which * (rows // 2) + part * nr, nr)


def _slot(ref, chip, rows, cols):
    if ref.shape[1] == N_CHIPS:
        return ref.at[0, chip, rows]
    return ref.at[0, chip // 2, rows, pl.ds(pl.multiple_of((chip % 2) * cols, LANES), cols)]


def _job_gather_ici(name, rows, cols, part, parts):
    def copies(src, dst, sem):
        x, y, c, k, sib, peers = _place()
        mine_rows = _share(rows, c, part, parts)
        out = []
        for j, (px, py) in enumerate(peers):
            mine = _slot(src[name], k, mine_rows, cols)
            out.append((_remote(mine, _slot(dst[name], k, mine_rows, cols), sem(j, 0), sem(j, 1), (px, py, c)),
                        _remote(mine, _slot(dst[name], 2 * px + py, mine_rows, cols), sem(j, 0), sem(j, 1),
                                (px, py, c))))
        return out
    return _Job([], [name], 3, copies)


def _job_gather_d2d(name, rows, cols, part, parts):
    def copies(src, dst, sem):
        x, y, c, k, sib, peers = _place()
        out = []
        for j, (px, py) in enumerate(peers):
            mine_rows, their_rows = _share(rows, c, part, parts), _share(rows, 1 - c, part, parts)
            landed = _slot(src[name], 2 * px + py, mine_rows, cols)
            out.append((_remote(landed, _slot(dst[name], 2 * px + py, mine_rows, cols), sem(j, 0), sem(j, 1), sib),
                        _remote(landed, _slot(dst[name], 2 * px + py, their_rows, cols), sem(j, 0), sem(j, 1), sib)))
        return out
    return _Job([], [name], 3, copies)


def _job_chip_gather(sname, dname):
    def copies(src, dst, sem):
        x, y, c, k, sib, peers = _place()
        return [(_remote(src[sname], dst[dname].at[k], sem(j, 0), sem(j, 1), (px, py, c)),
                 _remote(src[sname], dst[dname].at[2 * px + py], sem(j, 0), sem(j, 1), (px, py, c)))
                for j, (px, py) in enumerate(peers)]
    return _Job([sname], [dname], 3, copies)


def _job_pair_exchange(gname, tname, rows, part, parts):
    nr = rows // 2 // parts

    def copies(src, dst, sem):
        x, y, c, k, sib, peers = _place()
        cp = _remote(src[gname].at[:, _share(rows, 1 - c, part, parts), :],
                     dst[tname].at[:, pl.ds(part * nr, nr), :], sem(0, 0), sem(0, 1), sib)
        return [(cp, cp)]
    return _Job([gname], [tname], 1, copies)


def _job_chip_exchange(pname, lname, r0, nr):
    def copies(src, dst, sem):
        x, y, c, k, sib, peers = _place()
        out = []
        for j, (px, py) in enumerate(peers):
            cp = _remote(src[pname].at[2 * px + py, pl.ds(r0, nr)], dst[lname].at[j, pl.ds(r0, nr)],
                         sem(j, 0), sem(j, 1), (px, py, c))
            out.append((cp, cp))
        return out
    return _Job([pname], [lname], 3, copies)


def _job_pair_share(name, layer, rows):
    def copies(src, dst, sem):
        x, y, c, k, sib, peers = _place()
        mine = src[name].at[layer, _half(rows, c)]
        return [(_remote(mine, dst[name].at[layer, _half(rows, c)], sem(0, 0), sem(0, 1), sib),
                 _remote(mine, dst[name].at[layer, _half(rows, 1 - c)], sem(0, 0), sem(0, 1), sib))]
    return _Job([], [name], 1, copies)


class _Comm:
    def __init__(self, plan, jobs):
        self.plan, self.jobs = plan, jobs
        self.writes, self.reads = [], []
        for job in jobs:
            for n in job.writes:
                if n not in self.writes:
                    self.writes.append(n)
        for job in jobs:
            for n in job.reads:
                if n not in self.writes and n not in self.reads:
                    self.reads.append(n)
        self.ncopies = sum(job.ncopies for job in jobs)

    def descriptors(self, src, dst, sems, base):
        out = []
        for job in self.jobs:
            sem = lambda j, which, base=base: sems.at[base + j, which]
            out += job.copies(src, dst, sem)
            base += job.ncopies
        return out

    def start(self, src, dst, sems, base=0):
        for first, _ in self.descriptors(src, dst, sems, base):
            first.start()

    def finish(self, src, dst, sems, base=0):
        for _, landed in self.descriptors(src, dst, sems, base):
            landed.wait()


def _comm_operands(comm):
    bufs = comm.plan.bufs
    shapes = [jax.ShapeDtypeStruct(bufs[n].shape, bufs[n].dtype) for n in comm.writes]
    return [bufs[n] for n in comm.reads] + [bufs[n] for n in comm.writes], shapes


def _pallas(comm, body, *, name, grid, in_specs, out_specs, out_shape, compiler_params, scratch_shapes=(),
            aliases=None):
    aliases = dict(aliases or {})
    if comm is None:
        return pl.pallas_call(body, name=name, grid=grid, in_specs=in_specs, out_specs=out_specs,
                              out_shape=out_shape, scratch_shapes=list(scratch_shapes),
                              input_output_aliases=aliases, compiler_params=compiler_params)
    single = not isinstance(out_shape, (list, tuple))
    base_specs = [out_specs] if single else list(out_specs)
    base_shape = [out_shape] if single else list(out_shape)
    nb, nr, nw, nbo, nsc = len(in_specs), len(comm.reads), len(comm.writes), len(base_specs), len(scratch_shapes)

    def wrapped(*refs):
        base_in, rd, wr_in = refs[:nb], refs[nb:nb + nr], refs[nb + nr:nb + nr + nw]
        o0 = nb + nr + nw
        base_out, wr_out = refs[o0:o0 + nbo], refs[o0 + nbo:o0 + nbo + nw]
        scratch, sems = refs[o0 + nbo + nw:o0 + nbo + nw + nsc], refs[-1]
        src = dict(zip(comm.reads, rd))
        src.update(zip(comm.writes, wr_in))
        dst = dict(zip(comm.writes, wr_out))
        first = functools.reduce(jnp.logical_and, [pl.program_id(a) == 0 for a in range(len(grid))])
        last = functools.reduce(jnp.logical_and,
                                [pl.program_id(a) == pl.num_programs(a) - 1 for a in range(len(grid))])

        @pl.when(first)
        def _():
            comm.start(src, dst, sems)
        body(*base_in, *base_out, *scratch)

        @pl.when(last)
        def _():
            comm.finish(src, dst, sems)

    operands, shapes = _comm_operands(comm)
    call = pl.pallas_call(
        wrapped, name=name, grid=grid, in_specs=list(in_specs) + [HBM_REF] * (nr + nw),
        out_specs=base_specs + [HBM_REF] * nw, out_shape=base_shape + shapes,
        input_output_aliases={**aliases, **{nb + nr + q: nbo + q for q in range(nw)}},
        scratch_shapes=list(scratch_shapes) + [pltpu.SemaphoreType.DMA((comm.ncopies, 2))],
        compiler_params=compiler_params)

    def run(*args):
        outs = call(*args, *operands)
        for q, n in enumerate(comm.writes):
            comm.plan.bufs[n] = outs[nbo + q]
        return outs[0] if single else list(outs[:nbo])

    return run


def _comm_only(plan, phases, *, name):
    comms = [_Comm(plan, jobs) for jobs in phases]
    both = _Comm(plan, [job for jobs in phases for job in jobs])
    nr, nw = len(both.reads), len(both.writes)

    def body(*refs):
        rd, wr_in, wr_out, sems = refs[:nr], refs[nr:nr + nw], refs[nr + nw:nr + 2 * nw], refs[-1]
        src = dict(zip(both.reads, rd))
        src.update(zip(both.writes, wr_in))
        dst = dict(zip(both.writes, wr_out))
        base = 0
        for comm in comms:
            comm.start(src, dst, sems, base)
            comm.finish(src, dst, sems, base)
            base += comm.ncopies

    operands, shapes = _comm_operands(both)
    outs = pl.pallas_call(
        body, name=name, in_specs=[HBM_REF] * (nr + nw), out_specs=[HBM_REF] * nw, out_shape=shapes,
        input_output_aliases={nr + q: q for q in range(nw)},
        scratch_shapes=[pltpu.SemaphoreType.DMA((both.ncopies, 2))],
    )(*operands)
    for q, n in enumerate(both.writes):
        plan.bufs[n] = outs[q]


def _mm_nn(a, w, *, layer, tm, tn, residual=None, norm=None, out_dtype=F32, name, comm=None):
    T, K = a.shape
    if w.ndim == 4:
        _, S, _, n4 = w.shape
        N = S * n4
        bps = n4 // tn
        w_spec = pl.BlockSpec((None, None, K, tn), lambda j, i: (layer, j // bps, 0, j % bps))
    else:
        N = w.shape[2]
        w_spec = pl.BlockSpec((None, K, tn), lambda j, i: (layer, 0, j))
    in_specs = [pl.BlockSpec((tm, K), lambda j, i: (i, 0)), w_spec]
    args = [a, w]
    if residual is not None:
        in_specs.append(pl.BlockSpec((tm, tn), lambda j, i: (i, j)))
        args.append(residual)
    out_specs = pl.BlockSpec((tm, tn), lambda j, i: (i, j))
    out_shape = jax.ShapeDtypeStruct((T, N), out_dtype)
    if norm is not None:
        assert tn == N
        g, norm_layer = norm
        in_specs.append(pl.BlockSpec((None, 1, N), lambda j, i: (norm_layer, 0, 0)))
        args.append(g)
        out_specs = [out_specs, pl.BlockSpec((tm, tn), lambda j, i: (i, j))]
        out_shape = [out_shape, jax.ShapeDtypeStruct((T, N), BF16)]

    def body(*refs):
        a_ref, w_ref = refs[0], refs[1]
        acc = jnp.dot(a_ref[...].astype(BF16), w_ref[...], preferred_element_type=F32)
        if residual is not None:
            acc = refs[2][...] + acc
        if norm is None:
            refs[-1][...] = acc.astype(out_dtype)
        else:
            refs[-2][...] = acc.astype(out_dtype)
            r = lax.rsqrt(jnp.mean(acc * acc, axis=-1, keepdims=True) + EPS)
            refs[-1][...] = (acc * r * refs[-3][...]).astype(BF16)

    return _pallas(
        comm, body, name=name, grid=(N // tn, T // tm), in_specs=in_specs,
        out_specs=out_specs, out_shape=out_shape,
        compiler_params=_cparams("parallel", "parallel"),
    )(*args)


def _norm_mm_nn(x, g, w, *, g_layer, tm, tn, name, comm=None):
    T, K = x.shape
    _, S, _, n4 = w.shape
    bps = n4 // tn

    def body(x_ref, g_ref, w_ref, h_ref, o_ref):
        @pl.when(pl.program_id(1) == 0)
        def _():
            xf = x_ref[...]
            r = lax.rsqrt(jnp.mean(xf * xf, axis=-1, keepdims=True) + EPS)
            h_ref[...] = (xf * r * g_ref[...]).astype(BF16)
        o_ref[...] = jnp.dot(h_ref[...], w_ref[...], preferred_element_type=F32).astype(BF16)

    return _pallas(
        comm, body, name=name, grid=(T // tm, S * bps),
        in_specs=[pl.BlockSpec((tm, K), lambda i, j: (i, 0)),
                  pl.BlockSpec((None, 1, K), lambda i, j: (g_layer, 0, 0)),
                  pl.BlockSpec((None, None, K, tn), lambda i, j: (0, j // bps, 0, j % bps))],
        out_specs=[pl.BlockSpec((tm, K), lambda i, j: (i, 0)), pl.BlockSpec((tm, tn), lambda i, j: (i, j))],
        out_shape=[jax.ShapeDtypeStruct((T, K), BF16), jax.ShapeDtypeStruct((T, S * n4), BF16)],
        compiler_params=_cparams("parallel", "arbitrary"),
    )(x, g, w)


def _mm_nt(dy, w, *, layer, tm, tn, name, out_dtype=F32, comm=None):
    T = dy.shape[0]
    nt_dims = (((1,), (1,)), ((), ()))
    _, R, N = w.shape

    def body2(dy_ref, w_ref, o_ref):
        o_ref[...] = lax.dot_general(dy_ref[...].astype(BF16), w_ref[...], nt_dims,
                                     preferred_element_type=F32).astype(out_dtype)

    return _pallas(
        comm, body2, name=name, grid=(R // tn, T // tm),
        in_specs=[pl.BlockSpec((tm, N), lambda j, i: (i, 0)),
                  pl.BlockSpec((None, tn, N), lambda j, i: (layer, j, 0))],
        out_specs=pl.BlockSpec((tm, tn), lambda j, i: (i, j)),
        out_shape=jax.ShapeDtypeStruct((T, R), out_dtype),
        compiler_params=_cparams("parallel", "parallel"),
    )(dy, w)


def _mm_tn(a, dy, *, shards, tk, tn, tt, name, comm=None):
    T, K = a.shape
    N = dy.shape[1]
    tn_dims = (((0,), (0,)), ((), ()))
    n4 = N if shards is None else N // shards
    span = max(tn // n4, 1)

    def body(a_ref, dy_ref, o_ref):
        @pl.when(pl.program_id(2) == 0)
        def _():
            o_ref[...] = jnp.zeros_like(o_ref)
        r = lax.dot_general(a_ref[...].astype(BF16), dy_ref[...].astype(BF16), tn_dims,
                            preferred_element_type=F32)
        if span == 1:
            o_ref[...] += r
        else:
            for q in range(span):
                o_ref[q] += r[:, q * n4:(q + 1) * n4]

    if shards is None:
        out_spec = pl.BlockSpec((tk, tn), lambda k, n, t: (k, n))
        out_shape = jax.ShapeDtypeStruct((K, N), F32)
    elif span > 1:
        out_spec = pl.BlockSpec((span, tk, n4), lambda k, n, t: (n, k, 0))
        out_shape = jax.ShapeDtypeStruct((shards, K, n4), F32)
    else:
        bps = n4 // tn
        out_spec = pl.BlockSpec((None, tk, tn), lambda k, n, t: (n // bps, k, n % bps))
        out_shape = jax.ShapeDtypeStruct((shards, K, n4), F32)
    return _pallas(
        comm, body, name=name, grid=(K // tk, N // tn, T // tt),
        in_specs=[pl.BlockSpec((tt, tk), lambda k, n, t: (t, k)),
                  pl.BlockSpec((tt, tn), lambda k, n, t: (t, n))],
        out_specs=out_spec, out_shape=out_shape,
        compiler_params=_cparams("parallel", "parallel", "arbitrary"),
    )(a, dy)


def _rmsnorm_bwd_math(xf, g, dh, dres):
    r = lax.rsqrt(jnp.mean(xf * xf, axis=-1, keepdims=True) + EPS)
    xh = xf * r
    dxh = dh * g
    dx = dres + r * (dxh - xh * jnp.mean(dxh * xh, axis=-1, keepdims=True))
    return dx, _rowsum(dh * xh)


def _mm_nt_norm(dy, w, x, g, dres, *, g_layer, tm, name, comm=None):
    T = dy.shape[0]
    _, S, K, n4 = w.shape
    nt_dims = (((1,), (1,)), ((), ()))

    def body(dy_ref, w_ref, x_ref, g_ref, dres_ref, dx_ref, dg_ref):
        @pl.when(pl.program_id(0) == 0)
        def _():
            dg_ref[...] = jnp.zeros_like(dg_ref)
        dh = None
        for s in range(S):
            part = lax.dot_general(dy_ref[:, s * n4:(s + 1) * n4].astype(BF16), w_ref[s], nt_dims,
                                   preferred_element_type=F32)
            dh = part if dh is None else dh + part
        dx, dg = _rmsnorm_bwd_math(x_ref[...], g_ref[...], dh, dres_ref[...])
        dx_ref[...] = dx
        dg_ref[...] += dg

    row = lambda i: (i, 0)
    return _pallas(
        comm, body, name=name, grid=(T // tm,),
        in_specs=[pl.BlockSpec((tm, S * n4), row),
                  pl.BlockSpec((None, S, K, n4), lambda i: (0, 0, 0, 0)),
                  pl.BlockSpec((tm, K), row),
                  pl.BlockSpec((None, 1, K), lambda i: (g_layer, 0, 0)),
                  pl.BlockSpec((tm, K), row)],
        out_specs=[pl.BlockSpec((tm, K), row), pl.BlockSpec((1, K), lambda i: (0, 0))],
        out_shape=[jax.ShapeDtypeStruct((T, K), F32), jax.ShapeDtypeStruct((1, K), F32)],
        compiler_params=_cparams("arbitrary"),
    )(dy, w, x, g, dres)


def _mm_nn_loss(a, w, residual, tgt, g, *, tm, name, comm=None):
    T, K = a.shape
    D = w.shape[2]

    def body(a_ref, w_ref, res_ref, t_ref, g_ref, loss_ref, dx_ref, dg_ref):
        @pl.when(pl.program_id(0) == 0)
        def _():
            dg_ref[...] = jnp.zeros_like(dg_ref)
            loss_ref[...] = jnp.zeros_like(loss_ref)
        xf = res_ref[...] + jnp.dot(a_ref[...], w_ref[...], preferred_element_type=F32)
        gg = g_ref[...]
        r = lax.rsqrt(jnp.mean(xf * xf, axis=-1, keepdims=True) + EPS)
        xh = xf * r
        err = xh * gg - t_ref[...]
        row = jnp.mean(err * err, axis=-1, keepdims=True)
        loss_ref[...] += 0.5 * jnp.sum(row, axis=0, keepdims=True)
        dy = err * (1.0 / D)
        dg_ref[...] += _rowsum(dy * xh)
        dxh = dy * gg
        dx_ref[...] = r * (dxh - xh * jnp.mean(dxh * xh, axis=-1, keepdims=True))

    row_spec = pl.BlockSpec((tm, D), lambda i: (i, 0))
    return _pallas(
        comm, body, name=name, grid=(T // tm,),
        in_specs=[pl.BlockSpec((tm, K), lambda i: (i, 0)), pl.BlockSpec((None, K, D), lambda i: (0, 0, 0)),
                  row_spec, row_spec, pl.BlockSpec((1, D), lambda i: (0, 0))],
        out_specs=[pl.BlockSpec((1, 1), lambda i: (0, 0)), row_spec, pl.BlockSpec((1, D), lambda i: (0, 0))],
        out_shape=[jax.ShapeDtypeStruct((1, 1), F32), jax.ShapeDtypeStruct((T, D), F32),
                   jax.ShapeDtypeStruct((1, D), F32)],
        compiler_params=_cparams("arbitrary"),
    )(a, w, residual, tgt, g)


CONV_ROWS = 64
CONV_COLS = 256


def _halo_prev_index(tm, halo):
    per = tm // halo
    return lambda i: jnp.maximum(i * per - 1, 0)


def _halo_next_index(tm, halo, total):
    per = tm // halo
    last = total // halo - 1
    return lambda i: jnp.minimum((i + 1) * per, last)


def _causal_mask():
    t = lax.broadcasted_iota(jnp.int32, (CHUNK, CHUNK), 0)
    s = lax.broadcasted_iota(jnp.int32, (CHUNK, CHUNK), 1)
    return s <= t


def _mixer_ab_fwd(z, a_ln_g, a_ln_b, w_s, b_s, conv_w, conv_b, b_ln_g, b_ln_b, *, tm, name, comm=None):
    T = z.shape[0]
    nchunk = tm // CHUNK
    halo = HALO_LONG

    def body(za_ref, zb_ref, zh_ref, alg_ref, alb_ref, ws_ref, bs_ref, cw_ref, cbias_ref,
             blg_ref, blb_ref, y_ref, cb_ref, ext_ref):
        i = pl.program_id(0)
        gu = _gelu(za_ref[:, :D_A].astype(F32))
        gv = _gelu(za_ref[:, D_A:].astype(F32))
        xh, _ = _ln_stats(gv)
        lv = (xh * alg_ref[...] + alb_ref[...]).astype(BF16)
        mask = _causal_mask()
        for h in range(A_HEADS):
            wm = jnp.where(mask, ws_ref[h], 0.0).astype(BF16)
            cols = slice(h * HEAD_DIM, (h + 1) * HEAD_DIM)
            for c in range(nchunk):
                rows = slice(c * CHUNK, (c + 1) * CHUNK)
                mixed = jnp.dot(wm, lv[rows, cols], preferred_element_type=F32) + bs_ref[h]
                y_ref[rows, cols] = (gu[rows, cols] * mixed).astype(BF16)
        ext_ref[halo:halo + tm, :] = zb_ref[:, :D_B].astype(F32) * _sigmoid(zb_ref[:, D_B:].astype(F32))
        prev = zh_ref[:, :D_B].astype(F32) * _sigmoid(zh_ref[:, D_B:].astype(F32))
        ext_ref[0:halo, :] = jnp.where(i > 0, prev, 0.0)
        for rb in range(tm // CONV_ROWS):
            for cb in range(D_B // CONV_COLS):
                cs = slice(cb * CONV_COLS, (cb + 1) * CONV_COLS)
                window = ext_ref[rb * CONV_ROWS:rb * CONV_ROWS + CONV_ROWS + halo, cs]
                acc = jnp.zeros((CONV_ROWS, CONV_COLS), F32)
                for k in range(B_CONV):
                    shifted = _rows_after(window, halo - (B_CONV - 1) + k)[:CONV_ROWS]
                    acc = acc + cw_ref[k:k + 1, cs] * shifted
                cb_ref[rb * CONV_ROWS:(rb + 1) * CONV_ROWS, cs] = acc + cbias_ref[:, cs]
        xhb, _ = _ln_stats(cb_ref[...])
        y_ref[:, D_A:] = _silu(xhb * blg_ref[...] + blb_ref[...]).astype(BF16)

    row = lambda i: (i, 0)
    par = lambda i: (0, 0)
    return _pallas(
        comm, body, name=name, grid=(T // tm,),
        in_specs=[pl.BlockSpec((tm, 2 * D_A), lambda i: (i, 0)),
                  pl.BlockSpec((tm, 2 * D_B), lambda i: (i, 1)),
                  pl.BlockSpec((halo, 2 * D_B), lambda i: (_halo_prev_index(tm, halo)(i), 1)),
                  pl.BlockSpec((1, D_A), par), pl.BlockSpec((1, D_A), par),
                  pl.BlockSpec((A_HEADS, CHUNK, CHUNK), lambda i: (0, 0, 0)),
                  pl.BlockSpec((A_HEADS, CHUNK, 1), lambda i: (0, 0, 0)),
                  pl.BlockSpec((B_CONV, D_B), par), pl.BlockSpec((1, D_B), par),
                  pl.BlockSpec((1, D_B), par), pl.BlockSpec((1, D_B), par)],
        out_specs=[pl.BlockSpec((tm, D_A + D_B), row), pl.BlockSpec((tm, D_B), row)],
        out_shape=[jax.ShapeDtypeStruct((T, D_A + D_B), BF16), jax.ShapeDtypeStruct((T, D_B), F32)],
        scratch_shapes=[pltpu.VMEM((halo + tm, D_B), F32)],
        compiler_params=_cparams("parallel"),
    )(z, z, z, a_ln_g, a_ln_b, w_s, b_s, conv_w, conv_b, b_ln_g, b_ln_b)


def _mixer_ab_bwd_pre(z, cb, dy, a_ln_g, a_ln_b, w_s, b_s, b_ln_g, b_ln_b, *, tm, name, comm=None):
    T = z.shape[0]
    nchunk = tm // CHUNK
    tn_dims = (((0,), (0,)), ((), ()))
    nt_dims = (((1,), (1,)), ((), ()))

    def body(za_ref, cb_ref, dy_ref, alg_ref, alb_ref, ws_ref, bs_ref, blg_ref, blb_ref,
             dza_ref, dcb_ref, dalg_ref, dalb_ref, dws_ref, dbs_ref, dblg_ref, dblb_ref,
             dlv_ref):
        @pl.when(pl.program_id(0) == 0)
        def _():
            for ref in (dalg_ref, dalb_ref, dws_ref, dbs_ref, dblg_ref, dblb_ref):
                ref[...] = jnp.zeros_like(ref)
        ua = za_ref[:, :D_A].astype(F32)
        va = za_ref[:, D_A:].astype(F32)
        gu = _gelu(ua)
        gv = _gelu(va)
        xh, r = _ln_stats(gv)
        alg = alg_ref[...]
        lv = (xh * alg + alb_ref[...]).astype(BF16)
        dya = dy_ref[:, :D_A].astype(F32)
        mask = _causal_mask()
        for h in range(A_HEADS):
            wm = jnp.where(mask, ws_ref[h], 0.0).astype(BF16)
            cols = slice(h * HEAD_DIM, (h + 1) * HEAD_DIM)
            dwm = jnp.zeros((CHUNK, CHUNK), F32)
            dbs = jnp.zeros((CHUNK, 1), F32)
            for c in range(nchunk):
                rows = slice(c * CHUNK, (c + 1) * CHUNK)
                lvb = lv[rows, cols]
                mixed = jnp.dot(wm, lvb, preferred_element_type=F32) + bs_ref[h]
                dyb = dya[rows, cols]
                dza_ref[rows, cols] = (dyb * mixed * _dgelu(ua[rows, cols])).astype(BF16)
                dmixed = dyb * gu[rows, cols]
                dmb = dmixed.astype(BF16)
                dlv_ref[rows, cols] = lax.dot_general(wm, dmb, tn_dims, preferred_element_type=F32)
                dwm = dwm + lax.dot_general(dmb, lvb, nt_dims, preferred_element_type=F32)
                dbs = dbs + jnp.sum(dmixed, axis=1, keepdims=True)
            dws_ref[h] += jnp.where(mask, dwm, 0.0)
            dbs_ref[h] += dbs
        dlv = dlv_ref[...]
        dalg_ref[...] += _rowsum(dlv * xh)
        dalb_ref[...] += _rowsum(dlv)
        dgv = _ln_bwd(dlv, xh, r, alg)
        dza_ref[:, D_A:] = (dgv * _dgelu(va)).astype(BF16)
        xhb, rb = _ln_stats(cb_ref[...])
        blg = blg_ref[...]
        lb = xhb * blg + blb_ref[...]
        dlb = dy_ref[:, D_A:].astype(F32) * _dsilu(lb)
        dblg_ref[...] += _rowsum(dlb * xhb)
        dblb_ref[...] += _rowsum(dlb)
        dcb_ref[...] = _ln_bwd(dlb, xhb, rb, blg)

    row = lambda i: (i, 0)
    par = lambda i: (0, 0)
    par3 = lambda i: (0, 0, 0)
    return _pallas(
        comm, body, name=name, grid=(T // tm,),
        in_specs=[pl.BlockSpec((tm, 2 * D_A), row), pl.BlockSpec((tm, D_B), row),
                  pl.BlockSpec((tm, D_A + D_B), row),
                  pl.BlockSpec((1, D_A), par), pl.BlockSpec((1, D_A), par),
                  pl.BlockSpec((A_HEADS, CHUNK, CHUNK), par3),
                  pl.BlockSpec((A_HEADS, CHUNK, 1), par3),
                  pl.BlockSpec((1, D_B), par), pl.BlockSpec((1, D_B), par)],
        out_specs=[pl.BlockSpec((tm, 2 * D_A), row), pl.BlockSpec((tm, D_B), row),
                   pl.BlockSpec((1, D_A), par), pl.BlockSpec((1, D_A), par),
                   pl.BlockSpec((A_HEADS, CHUNK, CHUNK), par3),
                   pl.BlockSpec((A_HEADS, CHUNK, 1), par3),
                   pl.BlockSpec((1, D_B), par), pl.BlockSpec((1, D_B), par)],
        out_shape=[jax.ShapeDtypeStruct((T, 2 * D_A + 2 * D_B), BF16), jax.ShapeDtypeStruct((T, D_B), F32),
                   jax.ShapeDtypeStruct((1, D_A), F32), jax.ShapeDtypeStruct((1, D_A), F32),
                   jax.ShapeDtypeStruct((A_HEADS, CHUNK, CHUNK), F32),
                   jax.ShapeDtypeStruct((A_HEADS, CHUNK, 1), F32),
                   jax.ShapeDtypeStruct((1, D_B), F32), jax.ShapeDtypeStruct((1, D_B), F32)],
        scratch_shapes=[pltpu.VMEM((tm, D_A), F32)],
        compiler_params=_cparams("arbitrary"),
    )(z, cb, dy, a_ln_g, a_ln_b, w_s, b_s, b_ln_g, b_ln_b)


def _mixer_b_conv_bwd(z, dcb, conv_w, dz, *, tm, name, comm=None):
    T = z.shape[0]
    halo = HALO_LONG

    def body(zb_ref, dcb_ref, dcn_ref, cw_ref, dz_in_ref, dzb_ref, dcw_ref, dbias_ref, dext_ref):
        i = pl.program_id(0)
        last = pl.num_programs(0) - 1

        @pl.when(i == 0)
        def _():
            dcw_ref[...] = jnp.zeros_like(dcw_ref)
            dbias_ref[...] = jnp.zeros_like(dbias_ref)
        dcb = dcb_ref[...]
        dext_ref[0:tm, :] = dcb
        dext_ref[tm:tm + halo, :] = jnp.where(i < last, dcn_ref[...], 0.0)
        dbias_ref[...] += _rowsum(dcb)
        for rb in range(tm // CONV_ROWS):
            for cb in range(D_B // CONV_COLS):
                cs = slice(cb * CONV_COLS, (cb + 1) * CONV_COLS)
                gcs = slice(D_B + cb * CONV_COLS, D_B + (cb + 1) * CONV_COLS)
                rs = slice(rb * CONV_ROWS, (rb + 1) * CONV_ROWS)
                xbb = zb_ref[rs, cs].astype(F32)
                sgb = _sigmoid(zb_ref[rs, gcs].astype(F32))
                yb0 = xbb * sgb
                window = dext_ref[rb * CONV_ROWS:rb * CONV_ROWS + CONV_ROWS + halo, cs]
                acc = jnp.zeros((CONV_ROWS, CONV_COLS), F32)
                for k in range(B_CONV):
                    shifted = _rows_after(window, (B_CONV - 1) - k)[:CONV_ROWS]
                    acc = acc + cw_ref[k:k + 1, cs] * shifted
                    dcw_ref[k:k + 1, cs] += _rowsum(shifted * yb0)
                dzb_ref[rs, cs] = (acc * sgb).astype(BF16)
                dzb_ref[rs, gcs] = (acc * xbb * sgb * (1.0 - sgb)).astype(BF16)

    row = lambda i: (i, 0)
    par = lambda i: (0, 0)
    return _pallas(
        comm, body, name=name, grid=(T // tm,),
        in_specs=[pl.BlockSpec((tm, 2 * D_B), lambda i: (i, 1)),
                  pl.BlockSpec((tm, D_B), row),
                  pl.BlockSpec((halo, D_B), lambda i: (_halo_next_index(tm, halo, T)(i), 0)),
                  pl.BlockSpec((B_CONV, D_B), par), pl.BlockSpec(memory_space=pl.ANY)],
        out_specs=[pl.BlockSpec((tm, 2 * D_B), lambda i: (i, 1)), pl.BlockSpec((B_CONV, D_B), par),
                   pl.BlockSpec((1, D_B), par)],
        out_shape=[jax.ShapeDtypeStruct(dz.shape, BF16), jax.ShapeDtypeStruct((B_CONV, D_B), F32),
                   jax.ShapeDtypeStruct((1, D_B), F32)],
        scratch_shapes=[pltpu.VMEM((tm + halo, D_B), F32)], aliases={4: 0},
        compiler_params=_cparams("arbitrary"),
    )(z, dcb, dcb, conv_w, dz)


def _rows_before(x, a):
    return x if a == 0 else pltpu.roll(x, a, axis=0)


def _rows_after(x, a):
    return x if a == 0 else pltpu.roll(x, x.shape[0] - a, axis=0)


def _conv3(w_ref, x, halo, cs):
    acc = w_ref[2:3, cs] * x[halo:]
    acc = acc + w_ref[1:2, cs] * _rows_before(x, 1)[halo:]
    return acc + w_ref[0:1, cs] * _rows_before(x, 2)[halo:]


def _mixer_c_fwd(z, conv_w, *, tm, name, comm=None):
    T = z.shape[0]
    D = D_MODEL
    halo = HALO_SHORT
    W = CONV_COLS

    def body(bg_ref, cg_ref, xv_ref, cgh_ref, xvh_ref, w_ref, r_ref):
        i = pl.program_id(0)
        for cb in range(D // W):
            cs = slice(cb * W, (cb + 1) * W)
            prev = jnp.where(i > 0, cgh_ref[:, cs].astype(F32) * xvh_ref[:, cs].astype(F32), 0.0)
            p = jnp.concatenate([prev, cg_ref[:, cs].astype(F32) * xv_ref[:, cs].astype(F32)], axis=0)
            r_ref[:, cs] = (bg_ref[:, cs].astype(F32) * _conv3(w_ref, p, halo, cs)).astype(BF16)

    hp = _halo_prev_index(tm, halo)
    return _pallas(
        comm, body, name=name, grid=(T // tm,),
        in_specs=[pl.BlockSpec((tm, D), lambda i: (i, 0)), pl.BlockSpec((tm, D), lambda i: (i, 1)),
                  pl.BlockSpec((tm, D), lambda i: (i, 2)),
                  pl.BlockSpec((halo, D), lambda i: (hp(i), 1)),
                  pl.BlockSpec((halo, D), lambda i: (hp(i), 2)),
                  pl.BlockSpec((None, C_CONV, D), lambda i: (0, 0, 0))],
        out_specs=pl.BlockSpec((tm, D), lambda i: (i, 0)),
        out_shape=jax.ShapeDtypeStruct((T, D), BF16),
        compiler_params=_cparams("parallel"),
    )(z, z, z, z, z, conv_w)


def _mixer_c_bwd(z, dr, conv_w, *, tm, name, comm=None):
    T = z.shape[0]
    D = D_MODEL
    halo = HALO_SHORT
    W = CONV_COLS

    def body(bg_ref, cg_ref, xv_ref, cgh_ref, xvh_ref, bgn_ref, dr_ref, drn_ref, w_ref, dz_ref, dw_ref):
        i = pl.program_id(0)
        last = pl.num_programs(0) - 1

        @pl.when(i == 0)
        def _():
            dw_ref[...] = jnp.zeros_like(dw_ref)
        for cb in range(D // W):
            cs = slice(cb * W, (cb + 1) * W)
            cg = cg_ref[:, cs].astype(F32)
            xv = xv_ref[:, cs].astype(F32)
            dr = dr_ref[:, cs].astype(F32)
            p = cg * xv
            prev = jnp.where(i > 0, cgh_ref[:, cs].astype(F32) * xvh_ref[:, cs].astype(F32), 0.0)
            q = _conv3(w_ref, jnp.concatenate([prev, p], axis=0), halo, cs)
            dz_ref[:, cs] = (dr * q).astype(BF16)
            nxt = jnp.where(i < last, drn_ref[:, cs].astype(F32) * bgn_ref[:, cs].astype(F32), 0.0)
            dq = jnp.concatenate([dr * bg_ref[:, cs].astype(F32), nxt], axis=0)
            dp = None
            for k in range(C_CONV):
                shifted = _rows_after(dq, 2 - k)[:tm]
                term = w_ref[k:k + 1, cs] * shifted
                dp = term if dp is None else dp + term
                dw_ref[k:k + 1, cs] += _rowsum(shifted * p)
            dz_ref[:, D + cb * W:D + (cb + 1) * W] = (dp * xv).astype(BF16)
            dz_ref[:, 2 * D + cb * W:2 * D + (cb + 1) * W] = (dp * cg).astype(BF16)

    hp = _halo_prev_index(tm, halo)
    hn = _halo_next_index(tm, halo, T)
    return _pallas(
        comm, body, name=name, grid=(T // tm,),
        in_specs=[pl.BlockSpec((tm, D), lambda i: (i, 0)), pl.BlockSpec((tm, D), lambda i: (i, 1)),
                  pl.BlockSpec((tm, D), lambda i: (i, 2)),
                  pl.BlockSpec((halo, D), lambda i: (hp(i), 1)),
                  pl.BlockSpec((halo, D), lambda i: (hp(i), 2)),
                  pl.BlockSpec((halo, D), lambda i: (hn(i), 0)),
                  pl.BlockSpec((tm, D), lambda i: (i, 0)),
                  pl.BlockSpec((halo, D), lambda i: (hn(i), 0)),
                  pl.BlockSpec((None, C_CONV, D), lambda i: (0, 0, 0))],
        out_specs=[pl.BlockSpec((tm, 3 * D), lambda i: (i, 0)),
                   pl.BlockSpec((C_CONV, D), lambda i: (0, 0))],
        out_shape=[jax.ShapeDtypeStruct((T, 3 * D), BF16), jax.ShapeDtypeStruct((C_CONV, D), F32)],
        compiler_params=_cparams("arbitrary"),
    )(z, z, z, z, z, z, dr, dr, conv_w)


FFN_COLS = 128


def _ffn_act_fwd(up, conv_w, *, layer, tm, name, comm=None):
    T = up.shape[0]
    halo = HALO_SHORT
    W = FFN_COLS

    def body(up_ref, uph_ref, w_ref, a_ref, upc_ref):
        i = pl.program_id(0)

        def conv(cs):
            prev = jnp.where(i > 0, uph_ref[:, cs], jnp.zeros((halo, W), BF16))
            return _conv3(w_ref, jnp.concatenate([prev, up_ref[:, cs]], axis=0).astype(F32), halo, cs)

        for cb in range(D_FF // W):
            gs = slice(cb * W, (cb + 1) * W)
            vs = slice(D_FF + cb * W, D_FF + (cb + 1) * W)
            g = conv(gs)
            v = conv(vs)
            upc_ref[:, gs] = g.astype(BF16)
            upc_ref[:, vs] = v.astype(BF16)
            a_ref[:, gs] = (_silu(g) * v).astype(BF16)

    return _pallas(
        comm, body, name=name, grid=(T // tm,),
        in_specs=[pl.BlockSpec((tm, 2 * D_FF), lambda i: (i, 0)),
                  pl.BlockSpec((halo, 2 * D_FF), lambda i: (_halo_prev_index(tm, halo)(i), 0)),
                  pl.BlockSpec((None, F_CONV, 2 * D_FF), lambda i: (layer, 0, 0))],
        out_specs=[pl.BlockSpec((tm, D_FF), lambda i: (i, 0)),
                   pl.BlockSpec((tm, 2 * D_FF), lambda i: (i, 0))],
        out_shape=[jax.ShapeDtypeStruct((T, D_FF), BF16), jax.ShapeDtypeStruct((T, 2 * D_FF), BF16)],
        compiler_params=_cparams("parallel"),
    )(up, up, conv_w)


def _ffn_act_bwd(up, upc, da, conv_w, *, layer, tm, name, comm=None):
    T = up.shape[0]
    halo = HALO_SHORT
    W = FFN_COLS

    def body(up_ref, upc_ref, upcn_ref, da_ref, dan_ref, w_ref, dup_ref, dw_ref):
        i = pl.program_id(0)
        last = pl.num_programs(0) - 1

        @pl.when(i == 0)
        def _():
            dw_ref[...] = jnp.zeros_like(dw_ref)
        live = jnp.where(i < last, 1.0, 0.0)
        for cb in range(D_FF // W):
            gs = slice(cb * W, (cb + 1) * W)
            vs = slice(D_FF + cb * W, D_FF + (cb + 1) * W)
            g = jnp.concatenate([upc_ref[:, gs], upcn_ref[:, gs]], axis=0).astype(F32)
            v = jnp.concatenate([upc_ref[:, vs], upcn_ref[:, vs]], axis=0).astype(F32)
            da = jnp.concatenate([da_ref[:, gs].astype(F32), dan_ref[:, gs].astype(F32) * live], axis=0)
            s = _sigmoid(g)
            silu = g * s
            grads = (da * v * (s * (1.0 + g * (1.0 - s))), da * silu)
            for cs, d in zip((gs, vs), grads):
                u = up_ref[:, cs].astype(F32)
                acc = None
                for k in range(F_CONV):
                    shifted = _rows_after(d, 2 - k)[:tm]
                    term = w_ref[k:k + 1, cs] * shifted
                    acc = term if acc is None else acc + term
                    dw_ref[k:k + 1, cs] += _rowsum(shifted * u)
                dup_ref[:, cs] = acc.astype(BF16)

    hn = _halo_next_index(tm, halo, T)
    return _pallas(
        comm, body, name=name, grid=(T // tm,),
        in_specs=[pl.BlockSpec((tm, 2 * D_FF), lambda i: (i, 0)),
                  pl.BlockSpec((tm, 2 * D_FF), lambda i: (i, 0)),
                  pl.BlockSpec((halo, 2 * D_FF), lambda i: (hn(i), 0)),
                  pl.BlockSpec((tm, D_FF), lambda i: (i, 0)),
                  pl.BlockSpec((halo, D_FF), lambda i: (hn(i), 0)),
                  pl.BlockSpec((None, F_CONV, 2 * D_FF), lambda i: (layer, 0, 0))],
        out_specs=[pl.BlockSpec((tm, 2 * D_FF), lambda i: (i, 0)),
                   pl.BlockSpec((F_CONV, 2 * D_FF), lambda i: (0, 0))],
        out_shape=[jax.ShapeDtypeStruct((T, 2 * D_FF), BF16),
                   jax.ShapeDtypeStruct((F_CONV, 2 * D_FF), F32)],
        compiler_params=_cparams("arbitrary"),
    )(up, upc, upc, da, da, conv_w)


def _local_step(x, tgt, small, plan):
    T = x.shape[0]
    tm_e = _pick(T, 256)
    tm_a = _pick(T, 512)
    tm_b = _pick(T, 128)
    tm = _pick(T, 1024)
    tm_f = _pick(T, 512)
    tt = _pick(T, 2048)
    nm = small["norm_mix"].reshape(2, 1, D_MODEL)
    nf = small["norm_ffn"].reshape(2, 1, D_MODEL)
    ngf = small["norm_final"].reshape(1, D_MODEL)
    b_s = small["a_b_s"].reshape(A_HEADS, CHUNK, 1)
    w_s = small["a_w_s"].reshape(A_HEADS, CHUNK, CHUNK)
    b_conv_w = small["b_conv_w"].reshape(B_CONV, D_B)
    sg = {}
    wt, cm = plan.weight, plan.comm

    h_m0, z_ab = _norm_mm_nn(x, nm, wt("ab_w_in", 0), g_layer=0, tm=tm, tn=512, name="ab_in", comm=cm("ab_in"))
    yab, cb = _mixer_ab_fwd(z_ab, small["a_ln_g"], small["a_ln_b"], w_s, b_s, b_conv_w, small["b_conv_b"],
                            small["b_ln_g"], small["b_ln_b"], tm=tm_e, name="mixer_ab", comm=cm("mixer_ab"))
    x1, h_f0 = _mm_nn(yab, wt("ab_w_out", 0), layer=0, tm=tm, tn=D_MODEL, residual=x, norm=(nf, 0),
                      name="ab_out", comm=cm("ab_out"))

    def ffn_fwd(xin, h, layer):
        up = _mm_nn(h, wt("f_w_up", layer), layer=0, tm=tm, tn=2 * 1408, out_dtype=BF16, name=f"ffn_up{layer}",
                    comm=cm(f"ffn_up{layer}"))
        a, upc = _ffn_act_fwd(up, small["f_conv_w"], layer=layer, tm=tm_a, name=f"ffn_act{layer}",
                              comm=cm(f"ffn_act{layer}"))
        if layer == 0:
            out = _mm_nn(a, wt("f_w_down", layer), layer=0, tm=tm, tn=D_MODEL, residual=xin, norm=(nm, 1),
                         name=f"ffn_down{layer}", comm=cm(f"ffn_down{layer}"))
        else:
            out = _mm_nn_loss(a, wt("f_w_down", layer), xin, tgt, ngf, tm=tm, name=f"ffn_down{layer}",
                              comm=cm(f"ffn_down{layer}"))
        return up, upc, a, out

    up0, upc0, a0, (x2, h_m1) = ffn_fwd(x1, h_f0, 0)
    z_c = _mm_nn(h_m1, wt("c_w_in", 0), layer=0, tm=tm, tn=768, out_dtype=BF16, name="c_in", comm=cm("c_in"))
    r = _mixer_c_fwd(z_c, small["c_conv_w"], tm=tm_e, name="mixer_c", comm=cm("mixer_c"))
    x3, h_f1 = _mm_nn(r, wt("c_w_out", 0), layer=0, tm=tm, tn=D_MODEL, residual=x2, norm=(nf, 1),
                      name="c_out", comm=cm("c_out"))
    up1, upc1, a1, (loss, dx, sg["norm_final"]) = ffn_fwd(x3, h_f1, 1)

    def ffn_bwd(dx, xin, h, up, upc, a, layer):
        da = _mm_nt(dx, wt("f_w_down", layer), layer=0, tm=tm, tn=1408, out_dtype=BF16,
                    name=f"ffn_down_dx{layer}", comm=cm(f"ffn_down_dx{layer}"))
        plan.grad_ready("f_w_down", layer, _mm_tn(a, dx, shards=None, tk=1408, tn=1024, tt=tt,
                                                  name=f"ffn_down_dw{layer}", comm=cm(f"ffn_down_dw{layer}")))
        dup, dcw = _ffn_act_bwd(up, upc, da, small["f_conv_w"], layer=layer, tm=tm_b, name=f"ffn_act_bwd{layer}",
                                comm=cm(f"ffn_act_bwd{layer}"))
        dxin, dg = _mm_nt_norm(dup, wt("f_w_up", layer), xin, nf, dx, g_layer=layer, tm=tm_f,
                               name=f"ffn_up_dx{layer}", comm=cm(f"ffn_up_dx{layer}"))
        plan.grad_ready("f_w_up", layer, _mm_tn(h, dup, shards=N_CHIPS, tk=512, tn=2 * 1408, tt=tt,
                                                name=f"ffn_up_dw{layer}", comm=cm(f"ffn_up_dw{layer}")))
        return dxin, dg, dcw

    dx, dnf1, dfc1 = ffn_bwd(dx, x3, h_f1, up1, upc1, a1, 1)
    dr = _mm_nt(dx, wt("c_w_out", 0), layer=0, tm=tm, tn=512, out_dtype=BF16, name="c_out_dx", comm=cm("c_out_dx"))
    plan.grad_ready("c_w_out", 0, _mm_tn(r, dx, shards=None, tk=1024, tn=1024, tt=tt, name="c_out_dw",
                                         comm=cm("c_out_dw")))
    dz_c, dccw = _mixer_c_bwd(z_c, dr, small["c_conv_w"], tm=tm_e, name="mixer_c_bwd", comm=cm("mixer_c_bwd"))
    sg["c_conv_w"] = dccw.reshape(1, C_CONV, D_MODEL)
    plan.grad_ready("c_w_in", 0, _mm_tn(h_m1, dz_c, shards=N_CHIPS, tk=1024, tn=768, tt=tt, name="c_in_dw",
                                        comm=cm("c_in_dw")))
    dx, dnm1 = _mm_nt_norm(dz_c, wt("c_w_in", 0), x2, nm, dx, g_layer=1, tm=tm_f, name="c_in_dx",
                           comm=cm("c_in_dx"))
    dx, dnf0, dfc0 = ffn_bwd(dx, x1, h_f0, up0, upc0, a0, 0)
    dyab = _mm_nt(dx, wt("ab_w_out", 0), layer=0, tm=tm, tn=512, out_dtype=BF16, name="ab_out_dx",
                  comm=cm("ab_out_dx"))
    plan.grad_ready("ab_w_out", 0, _mm_tn(yab, dx, shards=None, tk=1024, tn=1024, tt=tt, name="ab_out_dw",
                                          comm=cm("ab_out_dw")))
    (dza, dcb, sg["a_ln_g"], sg["a_ln_b"], dws, dbs, sg["b_ln_g"], sg["b_ln_b"]) = _mixer_ab_bwd_pre(
        z_ab, cb, dyab, small["a_ln_g"], small["a_ln_b"], w_s, b_s, small["b_ln_g"], small["b_ln_b"],
        tm=tm_e, name="mixer_ab_bwd", comm=cm("mixer_ab_bwd"))
    dz_ab, dbcw, sg["b_conv_b"] = _mixer_b_conv_bwd(z_ab, dcb, b_conv_w, dza, tm=tm_e, name="mixer_b_conv_bwd",
                                                    comm=cm("mixer_b_conv_bwd"))
    sg["a_w_s"] = dws.reshape(1, A_HEADS, CHUNK, CHUNK)
    sg["a_b_s"] = dbs.reshape(1, A_HEADS, CHUNK)
    sg["b_conv_w"] = dbcw.reshape(1, B_CONV, D_B)
    plan.grad_ready("ab_w_in", 0, _mm_tn(h_m0, dz_ab, shards=N_CHIPS, tk=1024, tn=512, tt=tt, name="ab_in_dw",
                                         comm=cm("ab_in_dw")))
    dx, dnm0 = _mm_nt_norm(dz_ab, wt("ab_w_in", 0), x, nm, dx, g_layer=0, tm=tm_f, name="ab_in_dx",
                           comm=cm("ab_in_dx"))

    sg["norm_mix"] = [dnm0, dnm1]
    sg["norm_ffn"] = [dnf0, dnf1]
    sg["f_conv_w"] = [dfc0, dfc1]
    return loss, dx, sg


BLOCK_BYTES = 3 * 1024 * 1024


BF16_SUBLANES = 16


def _row_tile(rows, row_bytes, step=SUBLANES):
    best = None
    for tr in range(step, rows + 1, step):
        if rows % tr == 0 and tr * row_bytes <= BLOCK_BYTES:
            best = tr
    if best is None:
        raise ValueError(f"no row tile for {rows}")
    return best


def _place_scalars():
    x, y, c = lax.axis_index("x"), lax.axis_index("y"), lax.axis_index("c")
    return jnp.stack([c, 2 * x + y, 2 * (1 - x) + y, 2 * x + (1 - y), 2 * (1 - x) + (1 - y)]).astype(jnp.int32)


def _cast_into_slot(w, place, *, layer, paired, name):
    L, rows, cols = w.shape
    tr = _row_tile(rows, cols * 4, BF16_SUBLANES)

    def body(place_ref, w_ref, o_ref):
        o_ref[...] = w_ref[...].astype(BF16)

    if paired:
        out_spec = pl.BlockSpec((None, None, tr, cols), lambda i, p: (0, p[1] // 2, i, p[1] % 2))
        out_shape = jax.ShapeDtypeStruct((1, N_CHIPS // 2, rows, 2 * cols), BF16)
    else:
        out_spec = pl.BlockSpec((None, None, tr, cols), lambda i, p: (0, p[1], i, 0))
        out_shape = jax.ShapeDtypeStruct((1, N_CHIPS, rows, cols), BF16)
    return pl.pallas_call(
        body, name=name,
        grid_spec=pltpu.PrefetchScalarGridSpec(
            num_scalar_prefetch=1, grid=(rows // tr,),
            in_specs=[pl.BlockSpec((None, tr, cols), lambda i, p: (layer, i, 0))],
            out_specs=out_spec),
        out_shape=out_shape,
        compiler_params=_cparams("parallel"),
    )(place, w)


def _pair_sum(g, theirs, place, *, name):
    S, rows, cols = g.shape
    half = rows // 2
    tr = _row_tile(half, cols * 4, BF16_SUBLANES)
    nb = half // tr

    def body(place_ref, g_ref, t_ref, o_ref):
        o_ref[...] = (g_ref[...] + t_ref[...]).astype(BF16)

    spec = pl.BlockSpec((None, tr, cols), lambda s, i, p: (s, i, 0))
    return pl.pallas_call(
        body, name=name,
        grid_spec=pltpu.PrefetchScalarGridSpec(
            num_scalar_prefetch=1, grid=(S, nb),
            in_specs=[pl.BlockSpec((None, tr, cols), lambda s, i, p: (s, p[0] * nb + i, 0)), spec],
            out_specs=spec),
        out_shape=jax.ShapeDtypeStruct((S, half, cols), BF16),
        compiler_params=_cparams("parallel", "parallel"),
    )(place, g, theirs)


def _chip_sum(p, r, g_prev, place, *, layer, shape, name):
    L, rows, cols = shape
    half = rows // 2
    tr = _row_tile(half, cols * 4, BF16_SUBLANES)
    nb = half // tr

    def body(place_ref, p_ref, r_ref, *rest):
        o_ref = rest[-1]
        mine = p_ref[...].astype(F32)
        peers = [r_ref[j].astype(F32) for j in range(3)]
        acc = None
        for s in range(N_CHIPS):
            term = jnp.where(place_ref[1] == s, mine,
                             jnp.where(place_ref[2] == s, peers[0],
                                       jnp.where(place_ref[3] == s, peers[1], peers[2])))
            acc = term if acc is None else acc + term
        o_ref[...] = acc

    in_specs = [pl.BlockSpec((None, tr, cols), lambda i, pr: (pr[1], i, 0)),
                pl.BlockSpec((3, tr, cols), lambda i, pr: (0, i, 0))]
    args = [place, p, r]
    aliases = {}
    if g_prev is not None:
        in_specs.append(HBM_REF)
        args.append(g_prev)
        aliases = {3: 0}
    return pl.pallas_call(
        body, name=name,
        grid_spec=pltpu.PrefetchScalarGridSpec(
            num_scalar_prefetch=1, grid=(nb,), in_specs=in_specs,
            out_specs=pl.BlockSpec((None, tr, cols), lambda i, pr: (layer, pr[0] * nb + i, 0))),
        out_shape=jax.ShapeDtypeStruct(shape, F32), input_output_aliases=aliases,
        compiler_params=_cparams("parallel"),
    )(*args)


def _adamw_math(w, g, m, v):
    m2 = ADAM_B1 * m + (1.0 - ADAM_B1) * g
    v2 = ADAM_B2 * v + (1.0 - ADAM_B2) * (g * g)
    m_hat = m2 / (1.0 - ADAM_B1 ** ADAM_STEP)
    v_hat = v2 / (1.0 - ADAM_B2 ** ADAM_STEP)
    delta = -ADAM_LR * (m_hat / (jnp.sqrt(v_hat) + ADAM_EPS) + ADAM_WD * w)
    return delta, m2, v2


def _adamw(w, g, m, v, *, name):
    L, rows, cols = w.shape
    tr = _row_tile(rows, cols * 4)

    def body(w_ref, g_ref, m_ref, v_ref, go_ref, d_ref, m2_ref, v2_ref):
        g = g_ref[...]
        d, m2, v2 = _adamw_math(w_ref[...], g, m_ref[...], v_ref[...])
        go_ref[...] = g
        d_ref[...] = d
        m2_ref[...] = m2
        v2_ref[...] = v2

    spec = pl.BlockSpec((None, tr, cols), lambda l, i: (l, i, 0))
    shape = jax.ShapeDtypeStruct(w.shape, F32)
    return pl.pallas_call(
        body, name=name, grid=(L, rows // tr), in_specs=[spec] * 4, out_specs=[spec] * 4,
        out_shape=[shape] * 4,
        compiler_params=_cparams("parallel", "parallel"),
    )(w, g, m, v)


def _allreduce_pack(pack, *, name, comm):
    R = pack.shape[0]
    half = R // 2
    nr, nw = len(comm.reads), len(comm.writes)

    def body(*refs):
        p_ref, rd, wr_in = refs[0], refs[1:1 + nr], refs[1 + nr:1 + nr + nw]
        o_ref, wr_out = refs[1 + nr + nw], refs[2 + nr + nw:2 + nr + 2 * nw]
        sib_ref, chip_ref, parts_ref, sems, comm_sems = refs[2 + nr + 2 * nw:]
        src = dict(zip(comm.reads, rd))
        src.update(zip(comm.writes, wr_in))
        dst = dict(zip(comm.writes, wr_out))
        comm.start(src, dst, comm_sems)
        x, y, c, k, sib, peers = _place()
        swap = _remote(p_ref, sib_ref, sems.at[0, 0], sems.at[0, 1], sib)
        swap.start()
        swap.wait()
        chip_ref[...] = p_ref[...] + sib_ref[...]
        mine = chip_ref.at[pl.ds(pl.multiple_of(c * half, SUBLANES), half)]
        sends = [_remote(mine, parts_ref.at[j], sems.at[1 + j, 0], sems.at[1 + j, 1], (px, py, c))
                 for j, (px, py) in enumerate(peers)]
        for rc in sends:
            rc.start()
        for rc in sends:
            rc.wait()
        own = mine[...]
        others = [parts_ref[j] for j in range(3)]
        acc = None
        for s in range(N_CHIPS):
            term = own
            for j, (px, py) in enumerate(peers):
                term = jnp.where(2 * px + py == s, others[j], term)
            acc = term if acc is None else acc + term
        done = o_ref.at[pl.ds(pl.multiple_of(c * half, SUBLANES), half)]
        done[...] = acc
        theirs = o_ref.at[pl.ds(pl.multiple_of((1 - c) * half, SUBLANES), half)]
        share = _remote(done, done, sems.at[4, 0], sems.at[4, 1], sib)
        share.start()
        _remote(done, theirs, sems.at[4, 0], sems.at[4, 1], sib).wait()
        comm.finish(src, dst, comm_sems)

    vm = pl.BlockSpec(memory_space=pltpu.VMEM)
    operands, shapes = _comm_operands(comm)
    outs = pl.pallas_call(
        body, name=name, in_specs=[vm] + [HBM_REF] * (nr + nw), out_specs=[vm] + [HBM_REF] * nw,
        out_shape=[jax.ShapeDtypeStruct((R, LANES), F32)] + shapes,
        input_output_aliases={1 + nr + q: 1 + q for q in range(nw)},
        scratch_shapes=[pltpu.VMEM((R, LANES), F32), pltpu.VMEM((R, LANES), F32),
                        pltpu.VMEM((3, half, LANES), F32), pltpu.SemaphoreType.DMA((5, 2)),
                        pltpu.SemaphoreType.DMA((comm.ncopies, 2))],
        compiler_params=pltpu.CompilerParams(vmem_limit_bytes=VMEM_BYTES_MAX),
    )(pack, *operands)
    for q, n in enumerate(comm.writes):
        comm.plan.bufs[n] = outs[1 + q]
    return outs[0]


PACK_UNIT = SUBLANES * LANES


def _pack(arrays):
    flat, sizes = [], []
    for a in arrays:
        pieces = a if isinstance(a, (list, tuple)) else [a]
        v = jnp.concatenate([p.reshape(-1) for p in pieces]) if len(pieces) > 1 else pieces[0].reshape(-1)
        size = v.shape[0]
        padded = -(-size // PACK_UNIT) * PACK_UNIT
        flat.append(jnp.pad(v, (0, padded - size)))
        sizes.append((size, padded))
    total = sum(p for _, p in sizes)
    if (total // PACK_UNIT) % 2:
        flat.append(jnp.zeros((PACK_UNIT,), F32))
    return jnp.concatenate(flat).reshape(-1, LANES), sizes


def _unpack(pack, sizes, shapes):
    v = pack.reshape(-1)
    out, off = [], 0
    for (size, padded), shape in zip(sizes, shapes):
        out.append(v[off:off + size].reshape(shape))
        off += padded
    return out


BIG = ("ab_w_in", "ab_w_out", "c_w_in", "c_w_out", "f_w_up", "f_w_down")
COL_SHARDED = ("ab_w_in", "c_w_in", "f_w_up")
PAIRED = ("f_w_up",)
SMALL_REPLICATED = ("norm_mix", "norm_ffn", "norm_final", "a_ln_g", "a_ln_b", "a_w_s", "a_b_s",
                    "b_conv_b", "b_ln_g", "b_ln_b")
SMALL_SHARDED = ("b_conv_w", "c_conv_w", "f_conv_w")
SMALL = SMALL_REPLICATED + SMALL_SHARDED
ALL_WEIGHTS = ("norm_mix", "norm_ffn", "norm_final", "ab_w_in", "a_ln_g", "a_ln_b", "a_w_s", "a_b_s",
               "b_conv_w", "b_conv_b", "b_ln_g", "b_ln_b", "ab_w_out", "c_w_in", "c_conv_w", "c_w_out",
               "f_w_up", "f_conv_w", "f_w_down")


SCHEDULE = {
    "ab_in": [("gi", "f_w_up", 0, 0, 4), ("gi", "ab_w_out", 0)],
    "mixer_ab": [("gd", "f_w_up", 0, 0, 4), ("gd", "ab_w_out", 0), ("gi", "f_w_up", 0, 1, 4),
                 ("gi", "f_w_up", 0, 2, 4), ("gi", "f_w_up", 0, 3, 4)],
    "ab_out": [("gd", "f_w_up", 0, 1, 4), ("gd", "f_w_up", 0, 2, 4), ("gd", "f_w_up", 0, 3, 4)],
    "ffn_up0": [("gi", "f_w_down", 0), ("gi", "c_w_in", 0, 0, 2)],
    "ffn_act0": [("gd", "f_w_down", 0), ("gd", "c_w_in", 0, 0, 2), ("gi", "c_w_in", 0, 1, 2),
                 ("gi", "f_w_up", 1, 0, 4)],
    "ffn_down0": [("gd", "c_w_in", 0, 1, 2), ("gd", "f_w_up", 1, 0, 4), ("gi", "f_w_up", 1, 1, 4),
                  ("gi", "c_w_out", 0)],
    "c_in": [("gd", "f_w_up", 1, 1, 4), ("gd", "c_w_out", 0), ("gi", "f_w_up", 1, 2, 4),
             ("gi", "f_w_up", 1, 3, 4)],
    "mixer_c": [("gd", "f_w_up", 1, 2, 4), ("gd", "f_w_up", 1, 3, 4)],
    "ffn_up1": [("gi", "f_w_down", 1)],
    "ffn_act1": [("gd", "f_w_down", 1)],
    "ffn_act_bwd1": [("px", "f_w_down", 1)],
    "ffn_up_dx1": [("cx", "f_w_down", 1)],
    "c_out_dx": [("px", "f_w_up", 1)],
    "mixer_c_bwd": [("cx", "f_w_up", 1, 0, 4), ("px", "c_w_out", 0)],
    "c_in_dx": [("cx", "f_w_up", 1, 1, 4), ("px", "c_w_in", 0)],
    "ffn_act_bwd0": [("cx", "f_w_up", 1, 2, 4), ("cx", "c_w_out", 0), ("cx", "c_w_in", 0),
                     ("px", "f_w_down", 0), ("ps", "f_w_down", 1)],
    "ffn_up_dx0": [("cx", "f_w_down", 0), ("cx", "f_w_up", 1, 3, 4)],
    "ab_out_dx": [("px", "f_w_up", 0)],
    "mixer_ab_bwd": [("cx", "f_w_up", 0, 0, 4), ("px", "ab_w_out", 0)],
    "mixer_b_conv_bwd": [("cx", "f_w_up", 0, 1, 4), ("cx", "f_w_up", 0, 2, 4), ("cx", "ab_w_out", 0),
                         ("ps", "c_w_out", 0), ("ps", "c_w_in", 0), ("ps", "f_w_down", 0),
                         ("ps", "f_w_up", 1)],
    "ab_in_dx": [("cx", "f_w_up", 0, 3, 4), ("px", "ab_w_in", 0)],
}


class _Plan:
    def __init__(self, shapes, place):
        self.shapes, self.place, self.bufs = shapes, place, {}
        self.summed, self.shared = set(), set()

    def weight(self, name, layer):
        g = self.bufs[f"w:{name}:{layer}"]
        if name in COL_SHARDED:
            return g
        _, S, rows, cols = g.shape
        return g.reshape(1, S * rows, cols)

    def grad_ready(self, name, layer, g):
        _, rows, cols = self.shapes[name]
        hbm = lambda a: pltpu.with_memory_space_constraint(a, pltpu.HBM)
        self.bufs[f"g:{name}:{layer}"] = g.reshape(N_CHIPS, rows, cols)
        self.bufs[f"t:{name}:{layer}"] = hbm(lax.empty((N_CHIPS, rows // 2, cols), F32))
        self.bufs[f"l:{name}:{layer}"] = hbm(lax.empty((3, rows // 2, cols), BF16))

    def job(self, kind, name, layer, part=0, parts=1):
        _, rows, cols = self.shapes[name]
        key = f"{name}:{layer}"
        if kind == "gi":
            return _job_gather_ici("w:" + key, rows, cols, part, parts)
        if kind == "gd":
            return _job_gather_d2d("w:" + key, rows, cols, part, parts)
        if kind == "px":
            return _job_pair_exchange("g:" + key, "t:" + key, rows, part, parts)
        if kind == "cx":
            if "p:" + key not in self.bufs:
                self.bufs["p:" + key] = _pair_sum(self.bufs["g:" + key], self.bufs["t:" + key], self.place,
                                                  name=f"pair_sum_{name}{layer}")
            nr = rows // 2 // parts
            return _job_chip_exchange("p:" + key, "l:" + key, part * nr, nr)
        if kind == "ps":
            self.chip_sum(name, layer)
            self.shared.add(key)
            return _job_pair_share("G:" + name, layer, rows)
        raise ValueError(kind)

    def chip_sum(self, name, layer):
        key = f"{name}:{layer}"
        if key not in self.summed:
            self.summed.add(key)
            self.bufs["G:" + name] = _chip_sum(self.bufs["p:" + key], self.bufs["l:" + key],
                                               self.bufs.get("G:" + name), self.place, layer=layer,
                                               shape=self.shapes[name], name=f"chip_sum_{name}{layer}")

    def comm(self, call):
        specs = SCHEDULE.get(call)
        return None if specs is None else _Comm(self, [self.job(*spec) for spec in specs])


def _step(x, tgt, w, m, v):
    chip = 2 * lax.axis_index("x") + lax.axis_index("y")
    place = _place_scalars()
    plan = _Plan({n: w[n].shape for n in BIG}, place)
    items = [(n, l) for n in BIG for l in range(w[n].shape[0])]

    for n, l in items:
        plan.bufs[f"w:{n}:{l}"] = _cast_into_slot(w[n], place, layer=l, paired=n in PAIRED, name=f"cast_{n}{l}")
    conv_pack, conv_sizes = _pack([w[n] for n in SMALL_SHARDED])
    hbm = lambda a: pltpu.with_memory_space_constraint(a, pltpu.HBM)
    plan.bufs["conv:mine"] = hbm(conv_pack)
    plan.bufs["conv:all"] = hbm(lax.empty((N_CHIPS,) + conv_pack.shape, F32))
    _comm_only(plan, [[plan.job("gi", "ab_w_in", 0), _job_chip_gather("conv:mine", "conv:all")],
                      [plan.job("gd", "ab_w_in", 0)]], name="gather_first")
    conv_shapes = [w[n].shape for n in SMALL_SHARDED]
    per_chip = [_unpack(plan.bufs["conv:all"][s], conv_sizes, conv_shapes) for s in range(N_CHIPS)]
    small = {n: w[n] for n in SMALL_REPLICATED}
    for idx, n in enumerate(SMALL_SHARDED):
        small[n] = jnp.concatenate([jnp.where(chip == s, w[n], per_chip[s][idx]) for s in range(N_CHIPS)], axis=-1)

    loss, dx, sg = _local_step(x, tgt, small, plan)

    g_pack, g_sizes = _pack([sg[n] for n in SMALL] + [loss])
    g_sum = _allreduce_pack(g_pack, name="allreduce_small_grads",
                            comm=_Comm(plan, [plan.job("cx", "ab_w_in", 0)]))
    full_shapes = [small[n].shape for n in SMALL]
    *summed, loss = _unpack(g_sum, g_sizes, full_shapes + [(1, 1)])
    g_small = dict(zip(SMALL, summed))
    for n in SMALL_SHARDED:
        width = w[n].shape[-1]
        g_small[n] = lax.dynamic_slice_in_dim(g_small[n], chip * width, width, axis=g_small[n].ndim - 1)

    _comm_only(plan, [[plan.job("ps", n, l) for n, l in items if f"{n}:{l}" not in plan.shared]],
               name="reduce_pair_share")
    grads_big = [plan.bufs["G:" + n] for n in BIG]

    grad, delta, new_m, new_v = {}, {}, {}, {}
    for n, g in zip(BIG, grads_big):
        grad[n], delta[n], new_m[n], new_v[n] = _adamw(w[n], g, m[n], v[n], name=f"adamw_{n}")
    shapes = [w[n].shape for n in SMALL]
    wp, sizes = _pack([w[n] for n in SMALL])
    gp, _ = _pack([g_small[n] for n in SMALL])
    mp, _ = _pack([m[n] for n in SMALL])
    vp, _ = _pack([v[n] for n in SMALL])
    R = wp.shape[0]
    _, dp, m2p, v2p = _adamw(wp.reshape(1, R, LANES), gp.reshape(1, R, LANES), mp.reshape(1, R, LANES),
                             vp.reshape(1, R, LANES), name="adamw_small")
    for n, d_, m_, v_ in zip(SMALL, _unpack(dp, sizes, shapes), _unpack(m2p, sizes, shapes),
                             _unpack(v2p, sizes, shapes)):
        grad[n] = g_small[n]
        delta[n], new_m[n], new_v[n] = d_, m_, v_
    return loss, dx, grad, delta, new_m, new_v


def kernel(x, norm_mix, norm_ffn, norm_final, ab_w_in, a_ln_g, a_ln_b, a_w_s, a_b_s, b_conv_w, b_conv_b, b_ln_g, b_ln_b, ab_w_out, c_w_in, c_conv_w, c_w_out, f_w_up, f_conv_w, f_w_down, loss_target, m_norm_mix, m_norm_ffn, m_norm_final, m_ab_w_in, m_a_ln_g, m_a_ln_b, m_a_w_s, m_a_b_s, m_b_conv_w, m_b_conv_b, m_b_ln_g, m_b_ln_b, m_ab_w_out, m_c_w_in, m_c_conv_w, m_c_w_out, m_f_w_up, m_f_conv_w, m_f_w_down, v_norm_mix, v_norm_ffn, v_norm_final, v_ab_w_in, v_a_ln_g, v_a_ln_b, v_a_w_s, v_a_b_s, v_b_conv_w, v_b_conv_b, v_b_ln_g, v_b_ln_b, v_ab_w_out, v_c_w_in, v_c_conv_w, v_c_w_out, v_f_w_up, v_f_conv_w, v_f_w_down):
    given = dict(locals())
    w = {n: given[n] for n in ALL_WEIGHTS}
    m = {n: given["m_" + n] for n in ALL_WEIGHTS}
    v = {n: given["v_" + n] for n in ALL_WEIGHTS}
    T = x.shape[1]
    loss, dx, grad, delta, new_m, new_v = _step(x.reshape(T, D_MODEL), loss_target.reshape(T, D_MODEL), w, m, v)
    out = [loss[0, 0], dx.reshape(x.shape)]
    for d in (grad, delta, new_m, new_v):
        out += [d[n] for n in ALL_WEIGHTS]
    return tuple(out)
```

```python
import functools
import math

import jax
import jax.numpy as jnp
from jax import lax
from jax.experimental import pallas as pl
from jax.experimental.pallas import tpu as pltpu

F32 = jnp.float32
BF16 = jnp.bfloat16

EPS = 1e-6
D_MODEL = 1024
CHUNK = 128
HEAD_DIM = 128
A_HEADS = 4
D_A = 512
D_B = 512
B_CONV = 31
C_CONV = 3
D_FF = 2816
F_CONV = 3
N_CHIPS = 4

ADAM_LR = 0.001
ADAM_B1 = 0.9
ADAM_B2 = 0.999
ADAM_EPS = 1e-08
ADAM_WD = 0.01
ADAM_STEP = 10

SUBLANES = 8
LANES = 128
HALO_SHORT = 16
HALO_LONG = 32
VMEM_BYTES_MAX = 60000 * 1024

INV_SQRT2 = 1.0 / math.sqrt(2.0)
INV_SQRT_2PI = 1.0 / math.sqrt(2.0 * math.pi)

MESH = pl.DeviceIdType.MESH


def _cparams(*sem):
    return pltpu.CompilerParams(dimension_semantics=sem, vmem_limit_bytes=VMEM_BYTES_MAX)


def _pick(total, pref):
    for c in (2048, 1024, 512, 256, 128):
        if c <= pref and total % c == 0:
            return c
    raise ValueError(f"no tile for {total}")


def _sigmoid(x):
    return jax.nn.sigmoid(x)


def _silu(x):
    return x * _sigmoid(x)


def _dsilu(x):
    s = _sigmoid(x)
    return s * (1.0 + x * (1.0 - s))


def _gelu(x):
    return 0.5 * x * (1.0 + lax.erf(x * INV_SQRT2))


def _dgelu(x):
    return 0.5 * (1.0 + lax.erf(x * INV_SQRT2)) + x * jnp.exp(-0.5 * x * x) * INV_SQRT_2PI


def _ln_stats(x):
    mu = jnp.mean(x, axis=-1, keepdims=True)
    xc = x - mu
    var = jnp.mean(xc * xc, axis=-1, keepdims=True)
    r = lax.rsqrt(var + EPS)
    return xc * r, r


def _ln_bwd(dy, xh, r, g):
    dxh = dy * g
    m1 = jnp.mean(dxh, axis=-1, keepdims=True)
    m2 = jnp.mean(dxh * xh, axis=-1, keepdims=True)
    return r * (dxh - m1 - xh * m2)


def _rowsum(x):
    return jnp.sum(x, axis=0, keepdims=True)


HBM_REF = pl.BlockSpec(memory_space=pltpu.HBM)


def _place():
    x, y, c = lax.axis_index("x"), lax.axis_index("y"), lax.axis_index("c")
    peers = [(1 - x, y), (x, 1 - y), (1 - x, 1 - y)]
    return x, y, c, 2 * x + y, (x, y, 1 - c), peers


def _half(rows, which):
    return pl.ds(which * (rows // 2), rows // 2)


def _remote(src, dst, send_sem, recv_sem, device):
    return pltpu.make_async_remote_copy(src_ref=src, dst_ref=dst, send_sem=send_sem, recv_sem=recv_sem,
                                        device_id=device, device_id_type=MESH)


class _Job:
    def __init__(self, reads, writes, ncopies, copies):
        self.reads, self.writes, self.ncopies, self.copies = reads, writes, ncopies, copies


def _share(rows, which, part, parts):
    nr = rows // 2 // parts
    return pl.ds(which * (rows // 2) + part * nr, nr)


def _slot(ref, chip, rows, cols):
    if ref.shape[1] == N_CHIPS:
        return ref.at[0, chip, rows]
    return ref.at[0, chip // 2, rows, pl.ds(pl.multiple_of((chip % 2) * cols, LANES), cols)]


def _job_gather_ici(name, rows, cols, part, parts):
    def copies(src, dst, sem):
        x, y, c, k, sib, peers = _place()
        mine_rows = _share(rows, c, part, parts)
        out = []
        for j, (px, py) in enumerate(peers):
            mine = _slot(src[name], k, mine_rows, cols)
            out.append((_remote(mine, _slot(dst[name], k, mine_rows, cols), sem(j, 0), sem(j, 1), (px, py, c)),
                        _remote(mine, _slot(dst[name], 2 * px + py, mine_rows, cols), sem(j, 0), sem(j, 1),
                                (px, py, c))))
        return out
    return _Job([], [name], 3, copies)


def _job_gather_d2d(name, rows, cols, part, parts):
    def copies(src, dst, sem):
        x, y, c, k, sib, peers = _place()
        out = []
        for j, (px, py) in enumerate(peers):
            mine_rows, their_rows = _share(rows, c, part, parts), _share(rows, 1 - c, part, parts)
            landed = _slot(src[name], 2 * px + py, mine_rows, cols)
            out.append((_remote(landed, _slot(dst[name], 2 * px + py, mine_rows, cols), sem(j, 0), sem(j, 1), sib),
                        _remote(landed, _slot(dst[name], 2 * px + py, their_rows, cols), sem(j, 0), sem(j, 1), sib)))
        return out
    return _Job([], [name], 3, copies)


def _job_chip_gather(sname, dname):
    def copies(src, dst, sem):
        x, y, c, k, sib, peers = _place()
        return [(_remote(src[sname], dst[dname].at[k], sem(j, 0), sem(j, 1), (px, py, c)),
                 _remote(src[sname], dst[dname].at[2 * px + py], sem(j, 0), sem(j, 1), (px, py, c)))
                for j, (px, py) in enumerate(peers)]
    return _Job([sname], [dname], 3, copies)


def _job_pair_exchange(gname, tname, rows, part, parts):
    nr = rows // 2 // parts

    def copies(src, dst, sem):
        x, y, c, k, sib, peers = _place()
        cp = _remote(src[gname].at[:, _share(rows, 1 - c, part, parts), :],
                     dst[tname].at[:, pl.ds(part * nr, nr), :], sem(0, 0), sem(0, 1), sib)
        return [(cp, cp)]
    return _Job([gname], [tname], 1, copies)


def _job_chip_exchange(pname, lname, r0, nr):
    def copies(src, dst, sem):
        x, y, c, k, sib, peers = _place()
        out = []
        for j, (px, py) in enumerate(peers):
            cp = _remote(src[pname].at[2 * px + py, pl.ds(r0, nr)], dst[lname].at[j, pl.ds(r0, nr)],
                         sem(j, 0), sem(j, 1), (px, py, c))
            out.append((cp, cp))
        return out
    return _Job([pname], [lname], 3, copies)


def _job_pair_share(name, layer, rows):
    def copies(src, dst, sem):
        x, y, c, k, sib, peers = _place()
        mine = src[name].at[layer, _half(rows, c)]
        return [(_remote(mine, dst[name].at[layer, _half(rows, c)], sem(0, 0), sem(0, 1), sib),
                 _remote(mine, dst[name].at[layer, _half(rows, 1 - c)], sem(0, 0), sem(0, 1), sib))]
    return _Job([], [name], 1, copies)


class _Comm:
    def __init__(self, plan, jobs):
        self.plan, self.jobs = plan, jobs
        self.writes, self.reads = [], []
        for job in jobs:
            for n in job.writes:
                if n not in self.writes:
                    self.writes.append(n)
        for job in jobs:
            for n in job.reads:
                if n not in self.writes and n not in self.reads:
                    self.reads.append(n)
        self.ncopies = sum(job.ncopies for job in jobs)

    def descriptors(self, src, dst, sems, base):
        out = []
        for job in self.jobs:
            sem = lambda j, which, base=base: sems.at[base + j, which]
            out += job.copies(src, dst, sem)
            base += job.ncopies
        return out

    def start(self, src, dst, sems, base=0):
        for first, _ in self.descriptors(src, dst, sems, base):
            first.start()

    def finish(self, src, dst, sems, base=0):
        for _, landed in self.descriptors(src, dst, sems, base):
            landed.wait()


def _comm_operands(comm):
    bufs = comm.plan.bufs
    shapes = [jax.ShapeDtypeStruct(bufs[n].shape, bufs[n].dtype) for n in comm.writes]
    return [bufs[n] for n in comm.reads] + [bufs[n] for n in comm.writes], shapes


def _pallas(comm, body, *, name, grid, in_specs, out_specs, out_shape, compiler_params, scratch_shapes=(),
            aliases=None):
    aliases = dict(aliases or {})
    if comm is None:
        return pl.pallas_call(body, name=name, grid=grid, in_specs=in_specs, out_specs=out_specs,
                              out_shape=out_shape, scratch_shapes=list(scratch_shapes),
                              input_output_aliases=aliases, compiler_params=compiler_params)
    single = not isinstance(out_shape, (list, tuple))
    base_specs = [out_specs] if single else list(out_specs)
    base_shape = [out_shape] if single else list(out_shape)
    nb, nr, nw, nbo, nsc = len(in_specs), len(comm.reads), len(comm.writes), len(base_specs), len(scratch_shapes)

    def wrapped(*refs):
        base_in, rd, wr_in = refs[:nb], refs[nb:nb + nr], refs[nb + nr:nb + nr + nw]
        o0 = nb + nr + nw
        base_out, wr_out = refs[o0:o0 + nbo], refs[o0 + nbo:o0 + nbo + nw]
        scratch, sems = refs[o0 + nbo + nw:o0 + nbo + nw + nsc], refs[-1]
        src = dict(zip(comm.reads, rd))
        src.update(zip(comm.writes, wr_in))
        dst = dict(zip(comm.writes, wr_out))
        first = functools.reduce(jnp.logical_and, [pl.program_id(a) == 0 for a in range(len(grid))])
        last = functools.reduce(jnp.logical_and,
                                [pl.program_id(a) == pl.num_programs(a) - 1 for a in range(len(grid))])

        @pl.when(first)
        def _():
            comm.start(src, dst, sems)
        body(*base_in, *base_out, *scratch)

        @pl.when(last)
        def _():
            comm.finish(src, dst, sems)

    operands, shapes = _comm_operands(comm)
    call = pl.pallas_call(
        wrapped, name=name, grid=grid, in_specs=list(in_specs) + [HBM_REF] * (nr + nw),
        out_specs=base_specs + [HBM_REF] * nw, out_shape=base_shape + shapes,
        input_output_aliases={**aliases, **{nb + nr + q: nbo + q for q in range(nw)}},
        scratch_shapes=list(scratch_shapes) + [pltpu.SemaphoreType.DMA((comm.ncopies, 2))],
        compiler_params=compiler_params)

    def run(*args):
        outs = call(*args, *operands)
        for q, n in enumerate(comm.writes):
            comm.plan.bufs[n] = outs[nbo + q]
        return outs[0] if single else list(outs[:nbo])

    return run


def _comm_only(plan, phases, *, name):
    comms = [_Comm(plan, jobs) for jobs in phases]
    both = _Comm(plan, [job for jobs in phases for job in jobs])
    nr, nw = len(both.reads), len(both.writes)

    def body(*refs):
        rd, wr_in, wr_out, sems = refs[:nr], refs[nr:nr + nw], refs[nr + nw:nr + 2 * nw], refs[-1]
        src = dict(zip(both.reads, rd))
        src.update(zip(both.writes, wr_in))
        dst = dict(zip(both.writes, wr_out))
        base = 0
        for comm in comms:
            comm.start(src, dst, sems, base)
            comm.finish(src, dst, sems, base)
            base += comm.ncopies

    operands, shapes = _comm_operands(both)
    outs = pl.pallas_call(
        body, name=name, in_specs=[HBM_REF] * (nr + nw), out_specs=[HBM_REF] * nw, out_shape=shapes,
        input_output_aliases={nr + q: q for q in range(nw)},
        scratch_shapes=[pltpu.SemaphoreType.DMA((both.ncopies, 2))],
    )(*operands)
    for q, n in enumerate(both.writes):
        plan.bufs[n] = outs[q]


def _mm_nn(a, w, *, layer, tm, tn, residual=None, norm=None, out_dtype=F32, name, comm=None):
    T, K = a.shape
    if w.ndim == 4:
        _, S, _, n4 = w.shape
        N = S * n4
        bps = n4 // tn
        w_spec = pl.BlockSpec((None, None, K, tn), lambda j, i: (layer, j // bps, 0, j % bps))
    else:
        N = w.shape[2]
        w_spec = pl.BlockSpec((None, K, tn), lambda j, i: (layer, 0, j))
    in_specs = [pl.BlockSpec((tm, K), lambda j, i: (i, 0)), w_spec]
    args = [a, w]
    if residual is not None:
        in_specs.append(pl.BlockSpec((tm, tn), lambda j, i: (i, j)))
        args.append(residual)
    out_specs = pl.BlockSpec((tm, tn), lambda j, i: (i, j))
    out_shape = jax.ShapeDtypeStruct((T, N), out_dtype)
    if norm is not None:
        assert tn == N
        g, norm_layer = norm
        in_specs.append(pl.BlockSpec((None, 1, N), lambda j, i: (norm_layer, 0, 0)))
        args.append(g)
        out_specs = [out_specs, pl.BlockSpec((tm, tn), lambda j, i: (i, j))]
        out_shape = [out_shape, jax.ShapeDtypeStruct((T, N), BF16)]

    def body(*refs):
        a_ref, w_ref = refs[0], refs[1]
        acc = jnp.dot(a_ref[...].astype(BF16), w_ref[...], preferred_element_type=F32)
        if residual is not None:
            acc = refs[2][...] + acc
        if norm is None:
            refs[-1][...] = acc.astype(out_dtype)
        else:
            refs[-2][...] = acc.astype(out_dtype)
            r = lax.rsqrt(jnp.mean(acc * acc, axis=-1, keepdims=True) + EPS)
            refs[-1][...] = (acc * r * refs[-3][...]).astype(BF16)

    return _pallas(
        comm, body, name=name, grid=(N // tn, T // tm), in_specs=in_specs,
        out_specs=out_specs, out_shape=out_shape,
        compiler_params=_cparams("parallel", "parallel"),
    )(*args)


def _norm_mm_nn(x, g, w, *, g_layer, tm, tn, name, comm=None):
    T, K = x.shape
    _, S, _, n4 = w.shape
    bps = n4 // tn

    def body(x_ref, g_ref, w_ref, h_ref, o_ref):
        @pl.when(pl.program_id(1) == 0)
        def _():
            xf = x_ref[...]
            r = lax.rsqrt(jnp.mean(xf * xf, axis=-1, keepdims=True) + EPS)
            h_ref[...] = (xf * r * g_ref[...]).astype(BF16)
        o_ref[...] = jnp.dot(h_ref[...], w_ref[...], preferred_element_type=F32).astype(BF16)

    return _pallas(
        comm, body, name=name, grid=(T // tm, S * bps),
        in_specs=[pl.BlockSpec((tm, K), lambda i, j: (i, 0)),
                  pl.BlockSpec((None, 1, K), lambda i, j: (g_layer, 0, 0)),
                  pl.BlockSpec((None, None, K, tn), lambda i, j: (0, j // bps, 0, j % bps))],
        out_specs=[pl.BlockSpec((tm, K), lambda i, j: (i, 0)), pl.BlockSpec((tm, tn), lambda i, j: (i, j))],
        out_shape=[jax.ShapeDtypeStruct((T, K), BF16), jax.ShapeDtypeStruct((T, S * n4), BF16)],
        compiler_params=_cparams("parallel", "arbitrary"),
    )(x, g, w)


def _mm_nt(dy, w, *, layer, tm, tn, name, out_dtype=F32, comm=None):
    T = dy.shape[0]
    nt_dims = (((1,), (1,)), ((), ()))
    _, R, N = w.shape

    def body2(dy_ref, w_ref, o_ref):
        o_ref[...] = lax.dot_general(dy_ref[...].astype(BF16), w_ref[...], nt_dims,
                                     preferred_element_type=F32).astype(out_dtype)

    return _pallas(
        comm, body2, name=name, grid=(R // tn, T // tm),
        in_specs=[pl.BlockSpec((tm, N), lambda j, i: (i, 0)),
                  pl.BlockSpec((None, tn, N), lambda j, i: (layer, j, 0))],
        out_specs=pl.BlockSpec((tm, tn), lambda j, i: (i, j)),
        out_shape=jax.ShapeDtypeStruct((T, R), out_dtype),
        compiler_params=_cparams("parallel", "parallel"),
    )(dy, w)


def _mm_tn(a, dy, *, shards, tk, tn, tt, name, comm=None):
    T, K = a.shape
    N = dy.shape[1]
    tn_dims = (((0,), (0,)), ((), ()))
    n4 = N if shards is None else N // shards
    span = max(tn // n4, 1)

    def body(a_ref, dy_ref, o_ref):
        @pl.when(pl.program_id(2) == 0)
        def _():
            o_ref[...] = jnp.zeros_like(o_ref)
        r = lax.dot_general(a_ref[...].astype(BF16), dy_ref[...].astype(BF16), tn_dims,
                            preferred_element_type=F32)
        if span == 1:
            o_ref[...] += r
        else:
            for q in range(span):
                o_ref[q] += r[:, q * n4:(q + 1) * n4]

    if shards is None:
        out_spec = pl.BlockSpec((tk, tn), lambda k, n, t: (k, n))
        out_shape = jax.ShapeDtypeStruct((K, N), F32)
    elif span > 1:
        out_spec = pl.BlockSpec((span, tk, n4), lambda k, n, t: (n, k, 0))
        out_shape = jax.ShapeDtypeStruct((shards, K, n4), F32)
    else:
        bps = n4 // tn
        out_spec = pl.BlockSpec((None, tk, tn), lambda k, n, t: (n // bps, k, n % bps))
        out_shape = jax.ShapeDtypeStruct((shards, K, n4), F32)
    return _pallas(
        comm, body, name=name, grid=(K // tk, N // tn, T // tt),
        in_specs=[pl.BlockSpec((tt, tk), lambda k, n, t: (t, k)),
                  pl.BlockSpec((tt, tn), lambda k, n, t: (t, n))],
        out_specs=out_spec, out_shape=out_shape,
        compiler_params=_cparams("parallel", "parallel", "arbitrary"),
    )(a, dy)


def _rmsnorm_bwd_math(xf, g, dh, dres):
    r = lax.rsqrt(jnp.mean(xf * xf, axis=-1, keepdims=True) + EPS)
    xh = xf * r
    dxh = dh * g
    dx = dres + r * (dxh - xh * jnp.mean(dxh * xh, axis=-1, keepdims=True))
    return dx, _rowsum(dh * xh)


def _mm_nt_norm(dy, w, x, g, dres, *, g_layer, tm, name, comm=None):
    T = dy.shape[0]
    _, S, K, n4 = w.shape
    nt_dims = (((1,), (1,)), ((), ()))

    def body(dy_ref, w_ref, x_ref, g_ref, dres_ref, dx_ref, dg_ref):
        @pl.when(pl.program_id(0) == 0)
        def _():
            dg_ref[...] = jnp.zeros_like(dg_ref)
        dh = None
        for s in range(S):
            part = lax.dot_general(dy_ref[:, s * n4:(s + 1) * n4].astype(BF16), w_ref[s], nt_dims,
                                   preferred_element_type=F32)
            dh = part if dh is None else dh + part
        dx, dg = _rmsnorm_bwd_math(x_ref[...], g_ref[...], dh, dres_ref[...])
        dx_ref[...] = dx
        dg_ref[...] += dg

    row = lambda i: (i, 0)
    return _pallas(
        comm, body, name=name, grid=(T // tm,),
        in_specs=[pl.BlockSpec((tm, S * n4), row),
                  pl.BlockSpec((None, S, K, n4), lambda i: (0, 0, 0, 0)),
                  pl.BlockSpec((tm, K), row),
                  pl.BlockSpec((None, 1, K), lambda i: (g_layer, 0, 0)),
                  pl.BlockSpec((tm, K), row)],
        out_specs=[pl.BlockSpec((tm, K), row), pl.BlockSpec((1, K), lambda i: (0, 0))],
        out_shape=[jax.ShapeDtypeStruct((T, K), F32), jax.ShapeDtypeStruct((1, K), F32)],
        compiler_params=_cparams("arbitrary"),
    )(dy, w, x, g, dres)


def _mm_nn_loss(a, w, residual, tgt, g, *, tm, name, comm=None):
    T, K = a.shape
    D = w.shape[2]

    def body(a_ref, w_ref, res_ref, t_ref, g_ref, loss_ref, dx_ref, dg_ref):
        @pl.when(pl.program_id(0) == 0)
        def _():
            dg_ref[...] = jnp.zeros_like(dg_ref)
            loss_ref[...] = jnp.zeros_like(loss_ref)
        xf = res_ref[...] + jnp.dot(a_ref[...], w_ref[...], preferred_element_type=F32)
        gg = g_ref[...]
        r = lax.rsqrt(jnp.mean(xf * xf, axis=-1, keepdims=True) + EPS)
        xh = xf * r
        err = xh * gg - t_ref[...]
        row = jnp.mean(err * err, axis=-1, keepdims=True)
        loss_ref[...] += 0.5 * jnp.sum(row, axis=0, keepdims=True)
        dy = err * (1.0 / D)
        dg_ref[...] += _rowsum(dy * xh)
        dxh = dy * gg
        dx_ref[...] = r * (dxh - xh * jnp.mean(dxh * xh, axis=-1, keepdims=True))

    row_spec = pl.BlockSpec((tm, D), lambda i: (i, 0))
    return _pallas(
        comm, body, name=name, grid=(T // tm,),
        in_specs=[pl.BlockSpec((tm, K), lambda i: (i, 0)), pl.BlockSpec((None, K, D), lambda i: (0, 0, 0)),
                  row_spec, row_spec, pl.BlockSpec((1, D), lambda i: (0, 0))],
        out_specs=[pl.BlockSpec((1, 1), lambda i: (0, 0)), row_spec, pl.BlockSpec((1, D), lambda i: (0, 0))],
        out_shape=[jax.ShapeDtypeStruct((1, 1), F32), jax.ShapeDtypeStruct((T, D), F32),
                   jax.ShapeDtypeStruct((1, D), F32)],
        compiler_params=_cparams("arbitrary"),
    )(a, w, residual, tgt, g)


CONV_ROWS = 64
CONV_COLS = 256


def _halo_prev_index(tm, halo):
    per = tm // halo
    return lambda i: jnp.maximum(i * per - 1, 0)


def _halo_next_index(tm, halo, total):
    per = tm // halo
    last = total // halo - 1
    return lambda i: jnp.minimum((i + 1) * per, last)


def _causal_mask():
    t = lax.broadcasted_iota(jnp.int32, (CHUNK, CHUNK), 0)
    s = lax.broadcasted_iota(jnp.int32, (CHUNK, CHUNK), 1)
    return s <= t


def _mixer_ab_fwd(z, a_ln_g, a_ln_b, w_s, b_s, conv_w, conv_b, b_ln_g, b_ln_b, *, tm, name, comm=None):
    T = z.shape[0]
    nchunk = tm // CHUNK
    halo = HALO_LONG

    def body(za_ref, zb_ref, zh_ref, alg_ref, alb_ref, ws_ref, bs_ref, cw_ref, cbias_ref,
             blg_ref, blb_ref, y_ref, cb_ref, ext_ref):
        i = pl.program_id(0)
        gu = _gelu(za_ref[:, :D_A].astype(F32))
        gv = _gelu(za_ref[:, D_A:].astype(F32))
        xh, _ = _ln_stats(gv)
        lv = (xh * alg_ref[...] + alb_ref[...]).astype(BF16)
        mask = _causal_mask()
        for h in range(A_HEADS):
            wm = jnp.where(mask, ws_ref[h], 0.0).astype(BF16)
            cols = slice(h * HEAD_DIM, (h + 1) * HEAD_DIM)
            for c in range(nchunk):
                rows = slice(c * CHUNK, (c + 1) * CHUNK)
                mixed = jnp.dot(wm, lv[rows, cols], preferred_element_type=F32) + bs_ref[h]
                y_ref[rows, cols] = (gu[rows, cols] * mixed).astype(BF16)
        ext_ref[halo:halo + tm, :] = zb_ref[:, :D_B].astype(F32) * _sigmoid(zb_ref[:, D_B:].astype(F32))
        prev = zh_ref[:, :D_B].astype(F32) * _sigmoid(zh_ref[:, D_B:].astype(F32))
        ext_ref[0:halo, :] = jnp.where(i > 0, prev, 0.0)
        for rb in range(tm // CONV_ROWS):
            for cb in range(D_B // CONV_COLS):
                cs = slice(cb * CONV_COLS, (cb + 1) * CONV_COLS)
                window = ext_ref[rb * CONV_ROWS:rb * CONV_ROWS + CONV_ROWS + halo, cs]
                acc = jnp.zeros((CONV_ROWS, CONV_COLS), F32)
                for k in range(B_CONV):
                    shifted = _rows_after(window, halo - (B_CONV - 1) + k)[:CONV_ROWS]
                    acc = acc + cw_ref[k:k + 1, cs] * shifted
                cb_ref[rb * CONV_ROWS:(rb + 1) * CONV_ROWS, cs] = acc + cbias_ref[:, cs]
        xhb, _ = _ln_stats(cb_ref[...])
        y_ref[:, D_A:] = _silu(xhb * blg_ref[...] + blb_ref[...]).astype(BF16)

    row = lambda i: (i, 0)
    par = lambda i: (0, 0)
    return _pallas(
        comm, body, name=name, grid=(T // tm,),
        in_specs=[pl.BlockSpec((tm, 2 * D_A), lambda i: (i, 0)),
                  pl.BlockSpec((tm, 2 * D_B), lambda i: (i, 1)),
                  pl.BlockSpec((halo, 2 * D_B), lambda i: (_halo_prev_index(tm, halo)(i), 1)),
                  pl.BlockSpec((1, D_A), par), pl.BlockSpec((1, D_A), par),
                  pl.BlockSpec((A_HEADS, CHUNK, CHUNK), lambda i: (0, 0, 0)),
                  pl.BlockSpec((A_HEADS, CHUNK, 1), lambda i: (0, 0, 0)),
                  pl.BlockSpec((B_CONV, D_B), par), pl.BlockSpec((1, D_B), par),
                  pl.BlockSpec((1, D_B), par), pl.BlockSpec((1, D_B), par)],
        out_specs=[pl.BlockSpec((tm, D_A + D_B), row), pl.BlockSpec((tm, D_B), row)],
        out_shape=[jax.ShapeDtypeStruct((T, D_A + D_B), BF16), jax.ShapeDtypeStruct((T, D_B), F32)],
        scratch_shapes=[pltpu.VMEM((halo + tm, D_B), F32)],
        compiler_params=_cparams("parallel"),
    )(z, z, z, a_ln_g, a_ln_b, w_s, b_s, conv_w, conv_b, b_ln_g, b_ln_b)


def _mixer_ab_bwd_pre(z, cb, dy, a_ln_g, a_ln_b, w_s, b_s, b_ln_g, b_ln_b, *, tm, name, comm=None):
    T = z.shape[0]
    nchunk = tm // CHUNK
    tn_dims = (((0,), (0,)), ((), ()))
    nt_dims = (((1,), (1,)), ((), ()))

    def body(za_ref, cb_ref, dy_ref, alg_ref, alb_ref, ws_ref, bs_ref, blg_ref, blb_ref,
             dza_ref, dcb_ref, dalg_ref, dalb_ref, dws_ref, dbs_ref, dblg_ref, dblb_ref,
             dlv_ref):
        @pl.when(pl.program_id(0) == 0)
        def _():
            for ref in (dalg_ref, dalb_ref, dws_ref, dbs_ref, dblg_ref, dblb_ref):
                ref[...] = jnp.zeros_like(ref)
        ua = za_ref[:, :D_A].astype(F32)
        va = za_ref[:, D_A:].astype(F32)
        gu = _gelu(ua)
        gv = _gelu(va)
        xh, r = _ln_stats(gv)
        alg = alg_ref[...]
        lv = (xh * alg + alb_ref[...]).astype(BF16)
        dya = dy_ref[:, :D_A].astype(F32)
        mask = _causal_mask()
        for h in range(A_HEADS):
            wm = jnp.where(mask, ws_ref[h], 0.0).astype(BF16)
            cols = slice(h * HEAD_DIM, (h + 1) * HEAD_DIM)
            dwm = jnp.zeros((CHUNK, CHUNK), F32)
            dbs = jnp.zeros((CHUNK, 1), F32)
            for c in range(nchunk):
                rows = slice(c * CHUNK, (c + 1) * CHUNK)
                lvb = lv[rows, cols]
                mixed = jnp.dot(wm, lvb, preferred_element_type=F32) + bs_ref[h]
                dyb = dya[rows, cols]
                dza_ref[rows, cols] = (dyb * mixed * _dgelu(ua[rows, cols])).astype(BF16)
                dmixed = dyb * gu[rows, cols]
                dmb = dmixed.astype(BF16)
                dlv_ref[rows, cols] = lax.dot_general(wm, dmb, tn_dims, preferred_element_type=F32)
                dwm = dwm + lax.dot_general(dmb, lvb, nt_dims, preferred_element_type=F32)
                dbs = dbs + jnp.sum(dmixed, axis=1, keepdims=True)
            dws_ref[h] += jnp.where(mask, dwm, 0.0)
            dbs_ref[h] += dbs
        dlv = dlv_ref[...]
        dalg_ref[...] += _rowsum(dlv * xh)
        dalb_ref[...] += _rowsum(dlv)
        dgv = _ln_bwd(dlv, xh, r, alg)
        dza_ref[:, D_A:] = (dgv * _dgelu(va)).astype(BF16)
        xhb, rb = _ln_stats(cb_ref[...])
        blg = blg_ref[...]
        lb = xhb * blg + blb_ref[...]
        dlb = dy_ref[:, D_A:].astype(F32) * _dsilu(lb)
        dblg_ref[...] += _rowsum(dlb * xhb)
        dblb_ref[...] += _rowsum(dlb)
        dcb_ref[...] = _ln_bwd(dlb, xhb, rb, blg)

    row = lambda i: (i, 0)
    par = lambda i: (0, 0)
    par3 = lambda i: (0, 0, 0)
    return _pallas(
        comm, body, name=name, grid=(T // tm,),
        in_specs=[pl.BlockSpec((tm, 2 * D_A), row), pl.BlockSpec((tm, D_B), row),
                  pl.BlockSpec((tm, D_A + D_B), row),
                  pl.BlockSpec((1, D_A), par), pl.BlockSpec((1, D_A), par),
                  pl.BlockSpec((A_HEADS, CHUNK, CHUNK), par3),
                  pl.BlockSpec((A_HEADS, CHUNK, 1), par3),
                  pl.BlockSpec((1, D_B), par), pl.BlockSpec((1, D_B), par)],
        out_specs=[pl.BlockSpec((tm, 2 * D_A), row), pl.BlockSpec((tm, D_B), row),
                   pl.BlockSpec((1, D_A), par), pl.BlockSpec((1, D_A), par),
                   pl.BlockSpec((A_HEADS, CHUNK, CHUNK), par3),
                   pl.BlockSpec((A_HEADS, CHUNK, 1), par3),
                   pl.BlockSpec((1, D_B), par), pl.BlockSpec((1, D_B), par)],
        out_shape=[jax.ShapeDtypeStruct((T, 2 * D_A + 2 * D_B), BF16), jax.ShapeDtypeStruct((T, D_B), F32),
                   jax.ShapeDtypeStruct((1, D_A), F32), jax.ShapeDtypeStruct((1, D_A), F32),
                   jax.ShapeDtypeStruct((A_HEADS, CHUNK, CHUNK), F32),
                   jax.ShapeDtypeStruct((A_HEADS, CHUNK, 1), F32),
                   jax.ShapeDtypeStruct((1, D_B), F32), jax.ShapeDtypeStruct((1, D_B), F32)],
        scratch_shapes=[pltpu.VMEM((tm, D_A), F32)],
        compiler_params=_cparams("arbitrary"),
    )(z, cb, dy, a_ln_g, a_ln_b, w_s, b_s, b_ln_g, b_ln_b)


def _mixer_b_conv_bwd(z, dcb, conv_w, dz, *, tm, name, comm=None):
    T = z.shape[0]
    halo = HALO_LONG

    def body(zb_ref, dcb_ref, dcn_ref, cw_ref, dz_in_ref, dzb_ref, dcw_ref, dbias_ref, dext_ref):
        i = pl.program_id(0)
        last = pl.num_programs(0) - 1

        @pl.when(i == 0)
        def _():
            dcw_ref[...] = jnp.zeros_like(dcw_ref)
            dbias_ref[...] = jnp.zeros_like(dbias_ref)
        dcb = dcb_ref[...]
        dext_ref[0:tm, :] = dcb
        dext_ref[tm:tm + halo, :] = jnp.where(i < last, dcn_ref[...], 0.0)
        dbias_ref[...] += _rowsum(dcb)
        for rb in range(tm // CONV_ROWS):
            for cb in range(D_B // CONV_COLS):
                cs = slice(cb * CONV_COLS, (cb + 1) * CONV_COLS)
                gcs = slice(D_B + cb * CONV_COLS, D_B + (cb + 1) * CONV_COLS)
                rs = slice(rb * CONV_ROWS, (rb + 1) * CONV_ROWS)
                xbb = zb_ref[rs, cs].astype(F32)
                sgb = _sigmoid(zb_ref[rs, gcs].astype(F32))
                yb0 = xbb * sgb
                window = dext_ref[rb * CONV_ROWS:rb * CONV_ROWS + CONV_ROWS + halo, cs]
                acc = jnp.zeros((CONV_ROWS, CONV_COLS), F32)
                for k in range(B_CONV):
                    shifted = _rows_after(window, (B_CONV - 1) - k)[:CONV_ROWS]
                    acc = acc + cw_ref[k:k + 1, cs] * shifted
                    dcw_ref[k:k + 1, cs] += _rowsum(shifted * yb0)
                dzb_ref[rs, cs] = (acc * sgb).astype(BF16)
                dzb_ref[rs, gcs] = (acc * xbb * sgb * (1.0 - sgb)).astype(BF16)

    row = lambda i: (i, 0)
    par = lambda i: (0, 0)
    return _pallas(
        comm, body, name=name, grid=(T // tm,),
        in_specs=[pl.BlockSpec((tm, 2 * D_B), lambda i: (i, 1)),
                  pl.BlockSpec((tm, D_B), row),
                  pl.BlockSpec((halo, D_B), lambda i: (_halo_next_index(tm, halo, T)(i), 0)),
                  pl.BlockSpec((B_CONV, D_B), par), pl.BlockSpec(memory_space=pl.ANY)],
        out_specs=[pl.BlockSpec((tm, 2 * D_B), lambda i: (i, 1)), pl.BlockSpec((B_CONV, D_B), par),
                   pl.BlockSpec((1, D_B), par)],
        out_shape=[jax.ShapeDtypeStruct(dz.shape, BF16), jax.ShapeDtypeStruct((B_CONV, D_B), F32),
                   jax.ShapeDtypeStruct((1, D_B), F32)],
        scratch_shapes=[pltpu.VMEM((tm + halo, D_B), F32)], aliases={4: 0},
        compiler_params=_cparams("arbitrary"),
    )(z, dcb, dcb, conv_w, dz)


def _rows_before(x, a):
    return x if a == 0 else pltpu.roll(x, a, axis=0)


def _rows_after(x, a):
    return x if a == 0 else pltpu.roll(x, x.shape[0] - a, axis=0)


def _conv3(w_ref, x, halo, cs):
    acc = w_ref[2:3, cs] * x[halo:]
    acc = acc + w_ref[1:2, cs] * _rows_before(x, 1)[halo:]
    return acc + w_ref[0:1, cs] * _rows_before(x, 2)[halo:]


def _mixer_c_fwd(z, conv_w, *, tm, name, comm=None):
    T = z.shape[0]
    D = D_MODEL
    halo = HALO_SHORT
    W = CONV_COLS

    def body(bg_ref, cg_ref, xv_ref, cgh_ref, xvh_ref, w_ref, r_ref):
        i = pl.program_id(0)
        for cb in range(D // W):
            cs = slice(cb * W, (cb + 1) * W)
            prev = jnp.where(i > 0, cgh_ref[:, cs].astype(F32) * xvh_ref[:, cs].astype(F32), 0.0)
            p = jnp.concatenate([prev, cg_ref[:, cs].astype(F32) * xv_ref[:, cs].astype(F32)], axis=0)
            r_ref[:, cs] = (bg_ref[:, cs].astype(F32) * _conv3(w_ref, p, halo, cs)).astype(BF16)

    hp = _halo_prev_index(tm, halo)
    return _pallas(
        comm, body, name=name, grid=(T // tm,),
        in_specs=[pl.BlockSpec((tm, D), lambda i: (i, 0)), pl.BlockSpec((tm, D), lambda i: (i, 1)),
                  pl.BlockSpec((tm, D), lambda i: (i, 2)),
                  pl.BlockSpec((halo, D), lambda i: (hp(i), 1)),
                  pl.BlockSpec((halo, D), lambda i: (hp(i), 2)),
                  pl.BlockSpec((None, C_CONV, D), lambda i: (0, 0, 0))],
        out_specs=pl.BlockSpec((tm, D), lambda i: (i, 0)),
        out_shape=jax.ShapeDtypeStruct((T, D), BF16),
        compiler_params=_cparams("parallel"),
    )(z, z, z, z, z, conv_w)


def _mixer_c_bwd(z, dr, conv_w, *, tm, name, comm=None):
    T = z.shape[0]
    D = D_MODEL
    halo = HALO_SHORT
    W = CONV_COLS

    def body(bg_ref, cg_ref, xv_ref, cgh_ref, xvh_ref, bgn_ref, dr_ref, drn_ref, w_ref, dz_ref, dw_ref):
        i = pl.program_id(0)
        last = pl.num_programs(0) - 1

        @pl.when(i == 0)
        def _():
            dw_ref[...] = jnp.zeros_like(dw_ref)
        for cb in range(D // W):
            cs = slice(cb * W, (cb + 1) * W)
            cg = cg_ref[:, cs].astype(F32)
            xv = xv_ref[:, cs].astype(F32)
            dr = dr_ref[:, cs].astype(F32)
            p = cg * xv
            prev = jnp.where(i > 0, cgh_ref[:, cs].astype(F32) * xvh_ref[:, cs].astype(F32), 0.0)
            q = _conv3(w_ref, jnp.concatenate([prev, p], axis=0), halo, cs)
            dz_ref[:, cs] = (dr * q).astype(BF16)
            nxt = jnp.where(i < last, drn_ref[:, cs].astype(F32) * bgn_ref[:, cs].astype(F32), 0.0)
            dq = jnp.concatenate([dr * bg_ref[:, cs].astype(F32), nxt], axis=0)
            dp = None
            for k in range(C_CONV):
                shifted = _rows_after(dq, 2 - k)[:tm]
                term = w_ref[k:k + 1, cs] * shifted
                dp = term if dp is None else dp + term
                dw_ref[k:k + 1, cs] += _rowsum(shifted * p)
            dz_ref[:, D + cb * W:D + (cb + 1) * W] = (dp * xv).astype(BF16)
            dz_ref[:, 2 * D + cb * W:2 * D + (cb + 1) * W] = (dp * cg).astype(BF16)

    hp = _halo_prev_index(tm, halo)
    hn = _halo_next_index(tm, halo, T)
    return _pallas(
        comm, body, name=name, grid=(T // tm,),
        in_specs=[pl.BlockSpec((tm, D), lambda i: (i, 0)), pl.BlockSpec((tm, D), lambda i: (i, 1)),
                  pl.BlockSpec((tm, D), lambda i: (i, 2)),
                  pl.BlockSpec((halo, D), lambda i: (hp(i), 1)),
                  pl.BlockSpec((halo, D), lambda i: (hp(i), 2)),
                  pl.BlockSpec((halo, D), lambda i: (hn(i), 0)),
                  pl.BlockSpec((tm, D), lambda i: (i, 0)),
                  pl.BlockSpec((halo, D), lambda i: (hn(i), 0)),
                  pl.BlockSpec((None, C_CONV, D), lambda i: (0, 0, 0))],
        out_specs=[pl.BlockSpec((tm, 3 * D), lambda i: (i, 0)),
                   pl.BlockSpec((C_CONV, D), lambda i: (0, 0))],
        out_shape=[jax.ShapeDtypeStruct((T, 3 * D), BF16), jax.ShapeDtypeStruct((C_CONV, D), F32)],
        compiler_params=_cparams("arbitrary"),
    )(z, z, z, z, z, z, dr, dr, conv_w)


FFN_COLS = 128


def _ffn_act_fwd(up, conv_w, *, layer, tm, name, comm=None):
    T = up.shape[0]
    halo = HALO_SHORT
    W = FFN_COLS

    def body(up_ref, uph_ref, w_ref, a_ref, upc_ref):
        i = pl.program_id(0)

        def conv(cs):
            prev = jnp.where(i > 0, uph_ref[:, cs], jnp.zeros((halo, W), BF16))
            return _conv3(w_ref, jnp.concatenate([prev, up_ref[:, cs]], axis=0).astype(F32), halo, cs)

        for cb in range(D_FF // W):
            gs = slice(cb * W, (cb + 1) * W)
            vs = slice(D_FF + cb * W, D_FF + (cb + 1) * W)
            g = conv(gs)
            v = conv(vs)
            upc_ref[:, gs] = g.astype(BF16)
            upc_ref[:, vs] = v.astype(BF16)
            a_ref[:, gs] = (_silu(g) * v).astype(BF16)

    return _pallas(
        comm, body, name=name, grid=(T // tm,),
        in_specs=[pl.BlockSpec((tm, 2 * D_FF), lambda i: (i, 0)),
                  pl.BlockSpec((halo, 2 * D_FF), lambda i: (_halo_prev_index(tm, halo)(i), 0)),
                  pl.BlockSpec((None, F_CONV, 2 * D_FF), lambda i: (layer, 0, 0))],
        out_specs=[pl.BlockSpec((tm, D_FF), lambda i: (i, 0)),
                   pl.BlockSpec((tm, 2 * D_FF), lambda i: (i, 0))],
        out_shape=[jax.ShapeDtypeStruct((T, D_FF), BF16), jax.ShapeDtypeStruct((T, 2 * D_FF), BF16)],
        compiler_params=_cparams("parallel"),
    )(up, up, conv_w)


def _ffn_act_bwd(up, upc, da, conv_w, *, layer, tm, name, comm=None):
    T = up.shape[0]
    halo = HALO_SHORT
    W = FFN_COLS

    def body(up_ref, upc_ref, upcn_ref, da_ref, dan_ref, w_ref, dup_ref, dw_ref):
        i = pl.program_id(0)
        last = pl.num_programs(0) - 1

        @pl.when(i == 0)
        def _():
            dw_ref[...] = jnp.zeros_like(dw_ref)
        live = jnp.where(i < last, 1.0, 0.0)
        for cb in range(D_FF // W):
            gs = slice(cb * W, (cb + 1) * W)
            vs = slice(D_FF + cb * W, D_FF + (cb + 1) * W)
            g = jnp.concatenate([upc_ref[:, gs], upcn_ref[:, gs]], axis=0).astype(F32)
            v = jnp.concatenate([upc_ref[:, vs], upcn_ref[:, vs]], axis=0).astype(F32)
            da = jnp.concatenate([da_ref[:, gs].astype(F32), dan_ref[:, gs].astype(F32) * live], axis=0)
            s = _sigmoid(g)
            silu = g * s
            grads = (da * v * (s * (1.0 + g * (1.0 - s))), da * silu)
            for cs, d in zip((gs, vs), grads):
                u = up_ref[:, cs].astype(F32)
                acc = None
                for k in range(F_CONV):
                    shifted = _rows_after(d, 2 - k)[:tm]
                    term = w_ref[k:k + 1, cs] * shifted
                    acc = term if acc is None else acc + term
                    dw_ref[k:k + 1, cs] += _rowsum(shifted * u)
                dup_ref[:, cs] = acc.astype(BF16)

    hn = _halo_next_index(tm, halo, T)
    return _pallas(
        comm, body, name=name, grid=(T // tm,),
        in_specs=[pl.BlockSpec((tm, 2 * D_FF), lambda i: (i, 0)),
                  pl.BlockSpec((tm, 2 * D_FF), lambda i: (i, 0)),
                  pl.BlockSpec((halo, 2 * D_FF), lambda i: (hn(i), 0)),
                  pl.BlockSpec((tm, D_FF), lambda i: (i, 0)),
                  pl.BlockSpec((halo, D_FF), lambda i: (hn(i), 0)),
                  pl.BlockSpec((None, F_CONV, 2 * D_FF), lambda i: (layer, 0, 0))],
        out_specs=[pl.BlockSpec((tm, 2 * D_FF), lambda i: (i, 0)),
                   pl.BlockSpec((F_CONV, 2 * D_FF), lambda i: (0, 0))],
        out_shape=[jax.ShapeDtypeStruct((T, 2 * D_FF), BF16),
                   jax.ShapeDtypeStruct((F_CONV, 2 * D_FF), F32)],
        compiler_params=_cparams("arbitrary"),
    )(up, upc, upc, da, da, conv_w)


def _local_step(x, tgt, small, plan):
    T = x.shape[0]
    tm_e = _pick(T, 256)
    tm_a = _pick(T, 512)
    tm_b = _pick(T, 128)
    tm = _pick(T, 1024)
    tm_f = _pick(T, 512)
    tt = _pick(T, 2048)
    nm = small["norm_mix"].reshape(2, 1, D_MODEL)
    nf = small["norm_ffn"].reshape(2, 1, D_MODEL)
    ngf = small["norm_final"].reshape(1, D_MODEL)
    b_s = small["a_b_s"].reshape(A_HEADS, CHUNK, 1)
    w_s = small["a_w_s"].reshape(A_HEADS, CHUNK, CHUNK)
    b_conv_w = small["b_conv_w"].reshape(B_CONV, D_B)
    sg = {}
    wt, cm = plan.weight, plan.comm

    h_m0, z_ab = _norm_mm_nn(x, nm, wt("ab_w_in", 0), g_layer=0, tm=tm, tn=512, name="ab_in", comm=cm("ab_in"))
    yab, cb = _mixer_ab_fwd(z_ab, small["a_ln_g"], small["a_ln_b"], w_s, b_s, b_conv_w, small["b_conv_b"],
                            small["b_ln_g"], small["b_ln_b"], tm=tm_e, name="mixer_ab", comm=cm("mixer_ab"))
    x1, h_f0 = _mm_nn(yab, wt("ab_w_out", 0), layer=0, tm=tm, tn=D_MODEL, residual=x, norm=(nf, 0),
                      name="ab_out", comm=cm("ab_out"))

    def ffn_fwd(xin, h, layer):
        up = _mm_nn(h, wt("f_w_up", layer), layer=0, tm=tm, tn=2 * 1408, out_dtype=BF16, name=f"ffn_up{layer}",
                    comm=cm(f"ffn_up{layer}"))
        a, upc = _ffn_act_fwd(up, small["f_conv_w"], layer=layer, tm=tm_a, name=f"ffn_act{layer}",
                              comm=cm(f"ffn_act{layer}"))
        if layer == 0:
            out = _mm_nn(a, wt("f_w_down", layer), layer=0, tm=tm, tn=D_MODEL, residual=xin, norm=(nm, 1),
                         name=f"ffn_down{layer}", comm=cm(f"ffn_down{layer}"))
        else:
            out = _mm_nn_loss(a, wt("f_w_down", layer), xin, tgt, ngf, tm=tm, name=f"ffn_down{layer}",
                              comm=cm(f"ffn_down{layer}"))
        return up, upc, a, out

    up0, upc0, a0, (x2, h_m1) = ffn_fwd(x1, h_f0, 0)
    z_c = _mm_nn(h_m1, wt("c_w_in", 0), layer=0, tm=tm, tn=768, out_dtype=BF16, name="c_in", comm=cm("c_in"))
    r = _mixer_c_fwd(z_c, small["c_conv_w"], tm=tm_e, name="mixer_c", comm=cm("mixer_c"))
    x3, h_f1 = _mm_nn(r, wt("c_w_out", 0), layer=0, tm=tm, tn=D_MODEL, residual=x2, norm=(nf, 1),
                      name="c_out", comm=cm("c_out"))
    up1, upc1, a1, (loss, dx, sg["norm_final"]) = ffn_fwd(x3, h_f1, 1)

    def ffn_bwd(dx, xin, h, up, upc, a, layer):
        da = _mm_nt(dx, wt("f_w_down", layer), layer=0, tm=tm, tn=1408, out_dtype=BF16,
                    name=f"ffn_down_dx{layer}", comm=cm(f"ffn_down_dx{layer}"))
        plan.grad_ready("f_w_down", layer, _mm_tn(a, dx, shards=None, tk=1408, tn=1024, tt=tt,
                                                  name=f"ffn_down_dw{layer}", comm=cm(f"ffn_down_dw{layer}")))
        dup, dcw = _ffn_act_bwd(up, upc, da, small["f_conv_w"], layer=layer, tm=tm_b, name=f"ffn_act_bwd{layer}",
                                comm=cm(f"ffn_act_bwd{layer}"))
        plan.grad_ready("f_w_up", layer, _mm_tn(h, dup, shards=N_CHIPS, tk=512, tn=2 * 1408, tt=tt,
                                                name=f"ffn_up_dw{layer}", comm=cm(f"ffn_up_dw{layer}")))
        dxin, dg = _mm_nt_norm(dup, wt("f_w_up", layer), xin, nf, dx, g_layer=layer, tm=tm_f,
                               name=f"ffn_up_dx{layer}", comm=cm(f"ffn_up_dx{layer}"))
        return dxin, dg, dcw

    dx, dnf1, dfc1 = ffn_bwd(dx, x3, h_f1, up1, upc1, a1, 1)
    dr = _mm_nt(dx, wt("c_w_out", 0), layer=0, tm=tm, tn=512, out_dtype=BF16, name="c_out_dx", comm=cm("c_out_dx"))
    plan.grad_ready("c_w_out", 0, _mm_tn(r, dx, shards=None, tk=1024, tn=1024, tt=tt, name="c_out_dw",
                                         comm=cm("c_out_dw")))
    dz_c, dccw = _mixer_c_bwd(z_c, dr, small["c_conv_w"], tm=tm_e, name="mixer_c_bwd", comm=cm("mixer_c_bwd"))
    sg["c_conv_w"] = dccw.reshape(1, C_CONV, D_MODEL)
    plan.grad_ready("c_w_in", 0, _mm_tn(h_m1, dz_c, shards=N_CHIPS, tk=1024, tn=768, tt=tt, name="c_in_dw",
                                        comm=cm("c_in_dw")))
    dx, dnm1 = _mm_nt_norm(dz_c, wt("c_w_in", 0), x2, nm, dx, g_layer=1, tm=tm_f, name="c_in_dx",
                           comm=cm("c_in_dx"))
    dx, dnf0, dfc0 = ffn_bwd(dx, x1, h_f0, up0, upc0, a0, 0)
    dyab = _mm_nt(dx, wt("ab_w_out", 0), layer=0, tm=tm, tn=512, out_dtype=BF16, name="ab_out_dx",
                  comm=cm("ab_out_dx"))
    plan.grad_ready("ab_w_out", 0, _mm_tn(yab, dx, shards=None, tk=1024, tn=1024, tt=tt, name="ab_out_dw",
                                          comm=cm("ab_out_dw")))
    (dza, dcb, sg["a_ln_g"], sg["a_ln_b"], dws, dbs, sg["b_ln_g"], sg["b_ln_b"]) = _mixer_ab_bwd_pre(
        z_ab, cb, dyab, small["a_ln_g"], small["a_ln_b"], w_s, b_s, small["b_ln_g"], small["b_ln_b"],
        tm=tm_e, name="mixer_ab_bwd", comm=cm("mixer_ab_bwd"))
    dz_ab, dbcw, sg["b_conv_b"] = _mixer_b_conv_bwd(z_ab, dcb, b_conv_w, dza, tm=tm_e, name="mixer_b_conv_bwd",
                                                    comm=cm("mixer_b_conv_bwd"))
    sg["a_w_s"] = dws.reshape(1, A_HEADS, CHUNK, CHUNK)
    sg["a_b_s"] = dbs.reshape(1, A_HEADS, CHUNK)
    sg["b_conv_w"] = dbcw.reshape(1, B_CONV, D_B)
    plan.grad_ready("ab_w_in", 0, _mm_tn(h_m0, dz_ab, shards=N_CHIPS, tk=1024, tn=512, tt=tt, name="ab_in_dw",
                                         comm=cm("ab_in_dw")))
    dx, dnm0 = _mm_nt_norm(dz_ab, wt("ab_w_in", 0), x, nm, dx, g_layer=0, tm=tm_f, name="ab_in_dx",
                           comm=cm("ab_in_dx"))

    sg["norm_mix"] = [dnm0, dnm1]
    sg["norm_ffn"] = [dnf0, dnf1]
    sg["f_conv_w"] = [dfc0, dfc1]
    return loss, dx, sg


BLOCK_BYTES = 3 * 1024 * 1024


BF16_SUBLANES = 16


def _row_tile(rows, row_bytes, step=SUBLANES):
    best = None
    for tr in range(step, rows + 1, step):
        if rows % tr == 0 and tr * row_bytes <= BLOCK_BYTES:
            best = tr
    if best is None:
        raise ValueError(f"no row tile for {rows}")
    return best


def _place_scalars():
    x, y, c = lax.axis_index("x"), lax.axis_index("y"), lax.axis_index("c")
    return jnp.stack([c, 2 * x + y, 2 * (1 - x) + y, 2 * x + (1 - y), 2 * (1 - x) + (1 - y)]).astype(jnp.int32)


def _cast_into_slot(w, place, *, layer, paired, name):
    L, rows, cols = w.shape
    tr = _row_tile(rows, cols * 4, BF16_SUBLANES)

    def body(place_ref, w_ref, o_ref):
        o_ref[...] = w_ref[...].astype(BF16)

    if paired:
        out_spec = pl.BlockSpec((None, None, tr, cols), lambda i, p: (0, p[1] // 2, i, p[1] % 2))
        out_shape = jax.ShapeDtypeStruct((1, N_CHIPS // 2, rows, 2 * cols), BF16)
    else:
        out_spec = pl.BlockSpec((None, None, tr, cols), lambda i, p: (0, p[1], i, 0))
        out_shape = jax.ShapeDtypeStruct((1, N_CHIPS, rows, cols), BF16)
    return pl.pallas_call(
        body, name=name,
        grid_spec=pltpu.PrefetchScalarGridSpec(
            num_scalar_prefetch=1, grid=(rows // tr,),
            in_specs=[pl.BlockSpec((None, tr, cols), lambda i, p: (layer, i, 0))],
            out_specs=out_spec),
        out_shape=out_shape,
        compiler_params=_cparams("parallel"),
    )(place, w)


def _pair_sum(g, theirs, place, *, name):
    S, rows, cols = g.shape
    half = rows // 2
    tr = _row_tile(half, cols * 4, BF16_SUBLANES)
    nb = half // tr

    def body(place_ref, g_ref, t_ref, o_ref):
        o_ref[...] = (g_ref[...] + t_ref[...]).astype(BF16)

    spec = pl.BlockSpec((None, tr, cols), lambda s, i, p: (s, i, 0))
    return pl.pallas_call(
        body, name=name,
        grid_spec=pltpu.PrefetchScalarGridSpec(
            num_scalar_prefetch=1, grid=(S, nb),
            in_specs=[pl.BlockSpec((None, tr, cols), lambda s, i, p: (s, p[0] * nb + i, 0)), spec],
            out_specs=spec),
        out_shape=jax.ShapeDtypeStruct((S, half, cols), BF16),
        compiler_params=_cparams("parallel", "parallel"),
    )(place, g, theirs)


def _chip_sum(p, r, g_prev, place, *, layer, shape, name):
    L, rows, cols = shape
    half = rows // 2
    tr = _row_tile(half, cols * 4, BF16_SUBLANES)
    nb = half // tr

    def body(place_ref, p_ref, r_ref, *rest):
        o_ref = rest[-1]
        mine = p_ref[...].astype(F32)
        peers = [r_ref[j].astype(F32) for j in range(3)]
        acc = None
        for s in range(N_CHIPS):
            term = jnp.where(place_ref[1] == s, mine,
                             jnp.where(place_ref[2] == s, peers[0],
                                       jnp.where(place_ref[3] == s, peers[1], peers[2])))
            acc = term if acc is None else acc + term
        o_ref[...] = acc

    in_specs = [pl.BlockSpec((None, tr, cols), lambda i, pr: (pr[1], i, 0)),
                pl.BlockSpec((3, tr, cols), lambda i, pr: (0, i, 0))]
    args = [place, p, r]
    aliases = {}
    if g_prev is not None:
        in_specs.append(HBM_REF)
        args.append(g_prev)
        aliases = {3: 0}
    return pl.pallas_call(
        body, name=name,
        grid_spec=pltpu.PrefetchScalarGridSpec(
            num_scalar_prefetch=1, grid=(nb,), in_specs=in_specs,
            out_specs=pl.BlockSpec((None, tr, cols), lambda i, pr: (layer, pr[0] * nb + i, 0))),
        out_shape=jax.ShapeDtypeStruct(shape, F32), input_output_aliases=aliases,
        compiler_params=_cparams("parallel"),
    )(*args)


def _adamw_math(w, g, m, v):
    m2 = ADAM_B1 * m + (1.0 - ADAM_B1) * g
    v2 = ADAM_B2 * v + (1.0 - ADAM_B2) * (g * g)
    m_hat = m2 / (1.0 - ADAM_B1 ** ADAM_STEP)
    v_hat = v2 / (1.0 - ADAM_B2 ** ADAM_STEP)
    delta = -ADAM_LR * (m_hat / (jnp.sqrt(v_hat) + ADAM_EPS) + ADAM_WD * w)
    return delta, m2, v2


def _adamw(w, g, m, v, *, name):
    L, rows, cols = w.shape
    tr = _row_tile(rows, cols * 4)

    def body(w_ref, g_ref, m_ref, v_ref, go_ref, d_ref, m2_ref, v2_ref):
        g = g_ref[...]
        d, m2, v2 = _adamw_math(w_ref[...], g, m_ref[...], v_ref[...])
        go_ref[...] = g
        d_ref[...] = d
        m2_ref[...] = m2
        v2_ref[...] = v2

    spec = pl.BlockSpec((None, tr, cols), lambda l, i: (l, i, 0))
    shape = jax.ShapeDtypeStruct(w.shape, F32)
    return pl.pallas_call(
        body, name=name, grid=(L, rows // tr), in_specs=[spec] * 4, out_specs=[spec] * 4,
        out_shape=[shape] * 4,
        compiler_params=_cparams("parallel", "parallel"),
    )(w, g, m, v)


def _allreduce_pack(pack, *, name, comm):
    R = pack.shape[0]
    half = R // 2
    nr, nw = len(comm.reads), len(comm.writes)

    def body(*refs):
        p_ref, rd, wr_in = refs[0], refs[1:1 + nr], refs[1 + nr:1 + nr + nw]
        o_ref, wr_out = refs[1 + nr + nw], refs[2 + nr + nw:2 + nr + 2 * nw]
        sib_ref, chip_ref, parts_ref, sems, comm_sems = refs[2 + nr + 2 * nw:]
        src = dict(zip(comm.reads, rd))
        src.update(zip(comm.writes, wr_in))
        dst = dict(zip(comm.writes, wr_out))
        comm.start(src, dst, comm_sems)
        x, y, c, k, sib, peers = _place()
        swap = _remote(p_ref, sib_ref, sems.at[0, 0], sems.at[0, 1], sib)
        swap.start()
        swap.wait()
        chip_ref[...] = p_ref[...] + sib_ref[...]
        mine = chip_ref.at[pl.ds(pl.multiple_of(c * half, SUBLANES), half)]
        sends = [_remote(mine, parts_ref.at[j], sems.at[1 + j, 0], sems.at[1 + j, 1], (px, py, c))
                 for j, (px, py) in enumerate(peers)]
        for rc in sends:
            rc.start()
        for rc in sends:
            rc.wait()
        own = mine[...]
        others = [parts_ref[j] for j in range(3)]
        acc = None
        for s in range(N_CHIPS):
            term = own
            for j, (px, py) in enumerate(peers):
                term = jnp.where(2 * px + py == s, others[j], term)
            acc = term if acc is None else acc + term
        done = o_ref.at[pl.ds(pl.multiple_of(c * half, SUBLANES), half)]
        done[...] = acc
        theirs = o_ref.at[pl.ds(pl.multiple_of((1 - c) * half, SUBLANES), half)]
        share = _remote(done, done, sems.at[4, 0], sems.at[4, 1], sib)
        share.start()
        _remote(done, theirs, sems.at[4, 0], sems.at[4, 1], sib).wait()
        comm.finish(src, dst, comm_sems)

    vm = pl.BlockSpec(memory_space=pltpu.VMEM)
    operands, shapes = _comm_operands(comm)
    outs = pl.pallas_call(
        body, name=name, in_specs=[vm] + [HBM_REF] * (nr + nw), out_specs=[vm] + [HBM_REF] * nw,
        out_shape=[jax.ShapeDtypeStruct((R, LANES), F32)] + shapes,
        input_output_aliases={1 + nr + q: 1 + q for q in range(nw)},
        scratch_shapes=[pltpu.VMEM((R, LANES), F32), pltpu.VMEM((R, LANES), F32),
                        pltpu.VMEM((3, half, LANES), F32), pltpu.SemaphoreType.DMA((5, 2)),
                        pltpu.SemaphoreType.DMA((comm.ncopies, 2))],
        compiler_params=pltpu.CompilerParams(vmem_limit_bytes=VMEM_BYTES_MAX),
    )(pack, *operands)
    for q, n in enumerate(comm.writes):
        comm.plan.bufs[n] = outs[1 + q]
    return outs[0]


PACK_UNIT = SUBLANES * LANES


def _pack(arrays):
    flat, sizes = [], []
    for a in arrays:
        pieces = a if isinstance(a, (list, tuple)) else [a]
        v = jnp.concatenate([p.reshape(-1) for p in pieces]) if len(pieces) > 1 else pieces[0].reshape(-1)
        size = v.shape[0]
        padded = -(-size // PACK_UNIT) * PACK_UNIT
        flat.append(jnp.pad(v, (0, padded - size)))
        sizes.append((size, padded))
    total = sum(p for _, p in sizes)
    if (total // PACK_UNIT) % 2:
        flat.append(jnp.zeros((PACK_UNIT,), F32))
    return jnp.concatenate(flat).reshape(-1, LANES), sizes


def _unpack(pack, sizes, shapes):
    v = pack.reshape(-1)
    out, off = [], 0
    for (size, padded), shape in zip(sizes, shapes):
        out.append(v[off:off + size].reshape(shape))
        off += padded
    return out


BIG = ("ab_w_in", "ab_w_out", "c_w_in", "c_w_out", "f_w_up", "f_w_down")
COL_SHARDED = ("ab_w_in", "c_w_in", "f_w_up")
PAIRED = ("f_w_up",)
SMALL_REPLICATED = ("norm_mix", "norm_ffn", "norm_final", "a_ln_g", "a_ln_b", "a_w_s", "a_b_s",
                    "b_conv_b", "b_ln_g", "b_ln_b")
SMALL_SHARDED = ("b_conv_w", "c_conv_w", "f_conv_w")
SMALL = SMALL_REPLICATED + SMALL_SHARDED
ALL_WEIGHTS = ("norm_mix", "norm_ffn", "norm_final", "ab_w_in", "a_ln_g", "a_ln_b", "a_w_s", "a_b_s",
               "b_conv_w", "b_conv_b", "b_ln_g", "b_ln_b", "ab_w_out", "c_w_in", "c_conv_w", "c_w_out",
               "f_w_up", "f_conv_w", "f_w_down")


SCHEDULE = {
    "ab_in": [("gi", "f_w_up", 0, 0, 4), ("gi", "ab_w_out", 0)],
    "mixer_ab": [("gd", "f_w_up", 0, 0, 4), ("gd", "ab_w_out", 0), ("gi", "f_w_up", 0, 1, 4),
                 ("gi", "f_w_up", 0, 2, 4), ("gi", "f_w_up", 0, 3, 4)],
    "ab_out": [("gd", "f_w_up", 0, 1, 4), ("gd", "f_w_up", 0, 2, 4), ("gd", "f_w_up", 0, 3, 4)],
    "ffn_up0": [("gi", "f_w_down", 0), ("gi", "c_w_in", 0, 0, 2)],
    "ffn_act0": [("gd", "f_w_down", 0), ("gd", "c_w_in", 0, 0, 2), ("gi", "c_w_in", 0, 1, 2),
                 ("gi", "f_w_up", 1, 0, 4)],
    "ffn_down0": [("gd", "c_w_in", 0, 1, 2), ("gd", "f_w_up", 1, 0, 4), ("gi", "f_w_up", 1, 1, 4),
                  ("gi", "c_w_out", 0)],
    "c_in": [("gd", "f_w_up", 1, 1, 4), ("gd", "c_w_out", 0), ("gi", "f_w_up", 1, 2, 4),
             ("gi", "f_w_up", 1, 3, 4)],
    "mixer_c": [("gd", "f_w_up", 1, 2, 4), ("gd", "f_w_up", 1, 3, 4)],
    "ffn_up1": [("gi", "f_w_down", 1)],
    "ffn_act1": [("gd", "f_w_down", 1)],
    "ffn_act_bwd1": [("px", "f_w_down", 1)],
    "ffn_up_dx1": [("cx", "f_w_down", 1), ("px", "f_w_up", 1)],
    "mixer_c_bwd": [("cx", "f_w_up", 1, 0, 4), ("px", "c_w_out", 0)],
    "c_in_dx": [("cx", "f_w_up", 1, 1, 4), ("px", "c_w_in", 0)],
    "ffn_act_bwd0": [("cx", "f_w_up", 1, 2, 4), ("cx", "c_w_out", 0), ("cx", "c_w_in", 0),
                     ("px", "f_w_down", 0), ("ps", "f_w_down", 1)],
    "ffn_up_dx0": [("cx", "f_w_down", 0), ("cx", "f_w_up", 1, 3, 4), ("px", "f_w_up", 0)],
    "mixer_ab_bwd": [("cx", "f_w_up", 0, 0, 4), ("px", "ab_w_out", 0)],
    "mixer_b_conv_bwd": [("cx", "f_w_up", 0, 1, 4), ("cx", "f_w_up", 0, 2, 4), ("cx", "ab_w_out", 0),
                         ("ps", "c_w_out", 0), ("ps", "c_w_in", 0), ("ps", "f_w_down", 0),
                         ("ps", "f_w_up", 1)],
    "ab_in_dx": [("cx", "f_w_up", 0, 3, 4), ("px", "ab_w_in", 0)],
}


class _Plan:
    def __init__(self, shapes, place):
        self.shapes, self.place, self.bufs = shapes, place, {}
        self.summed, self.shared = set(), set()

    def weight(self, name, layer):
        g = self.bufs[f"w:{name}:{layer}"]
        if name in COL_SHARDED:
            return g
        _, S, rows, cols = g.shape
        return g.reshape(1, S * rows, cols)

    def grad_ready(self, name, layer, g):
        _, rows, cols = self.shapes[name]
        hbm = lambda a: pltpu.with_memory_space_constraint(a, pltpu.HBM)
        self.bufs[f"g:{name}:{layer}"] = g.reshape(N_CHIPS, rows, cols)
        self.bufs[f"t:{name}:{layer}"] = hbm(lax.empty((N_CHIPS, rows // 2, cols), F32))
        self.bufs[f"l:{name}:{layer}"] = hbm(lax.empty((3, rows // 2, cols), BF16))

    def job(self, kind, name, layer, part=0, parts=1):
        _, rows, cols = self.shapes[name]
        key = f"{name}:{layer}"
        if kind == "gi":
            return _job_gather_ici("w:" + key, rows, cols, part, parts)
        if kind == "gd":
            return _job_gather_d2d("w:" + key, rows, cols, part, parts)
        if kind == "px":
            return _job_pair_exchange("g:" + key, "t:" + key, rows, part, parts)
        if kind == "cx":
            if "p:" + key not in self.bufs:
                self.bufs["p:" + key] = _pair_sum(self.bufs["g:" + key], self.bufs["t:" + key], self.place,
                                                  name=f"pair_sum_{name}{layer}")
            nr = rows // 2 // parts
            return _job_chip_exchange("p:" + key, "l:" + key, part * nr, nr)
        if kind == "ps":
            self.chip_sum(name, layer)
            self.shared.add(key)
            return _job_pair_share("G:" + name, layer, rows)
        raise ValueError(kind)

    def chip_sum(self, name, layer):
        key = f"{name}:{layer}"
        if key not in self.summed:
            self.summed.add(key)
            self.bufs["G:" + name] = _chip_sum(self.bufs["p:" + key], self.bufs["l:" + key],
                                               self.bufs.get("G:" + name), self.place, layer=layer,
                                               shape=self.shapes[name], name=f"chip_sum_{name}{layer}")

    def comm(self, call):
        specs = SCHEDULE.get(call)
        return None if specs is None else _Comm(self, [self.job(*spec) for spec in specs])


def _step(x, tgt, w, m, v):
    chip = 2 * lax.axis_index("x") + lax.axis_index("y")
    place = _place_scalars()
    plan = _Plan({n: w[n].shape for n in BIG}, place)
    items = [(n, l) for n in BIG for l in range(w[n].shape[0])]

    for n, l in items:
        plan.bufs[f"w:{n}:{l}"] = _cast_into_slot(w[n], place, layer=l, paired=n in PAIRED, name=f"cast_{n}{l}")
    conv_pack, conv_sizes = _pack([w[n] for n in SMALL_SHARDED])
    hbm = lambda a: pltpu.with_memory_space_constraint(a, pltpu.HBM)
    plan.bufs["conv:mine"] = hbm(conv_pack)
    plan.bufs["conv:all"] = hbm(lax.empty((N_CHIPS,) + conv_pack.shape, F32))
    _comm_only(plan, [[plan.job("gi", "ab_w_in", 0), _job_chip_gather("conv:mine", "conv:all")],
                      [plan.job("gd", "ab_w_in", 0)]], name="gather_first")
    conv_shapes = [w[n].shape for n in SMALL_SHARDED]
    per_chip = [_unpack(plan.bufs["conv:all"][s], conv_sizes, conv_shapes) for s in range(N_CHIPS)]
    small = {n: w[n] for n in SMALL_REPLICATED}
    for idx, n in enumerate(SMALL_SHARDED):
        small[n] = jnp.concatenate([jnp.where(chip == s, w[n], per_chip[s][idx]) for s in range(N_CHIPS)], axis=-1)

    loss, dx, sg = _local_step(x, tgt, small, plan)

    g_pack, g_sizes = _pack([sg[n] for n in SMALL] + [loss])
    g_sum = _allreduce_pack(g_pack, name="allreduce_small_grads",
                            comm=_Comm(plan, [plan.job("cx", "ab_w_in", 0)]))
    full_shapes = [small[n].shape for n in SMALL]
    *summed, loss = _unpack(g_sum, g_sizes, full_shapes + [(1, 1)])
    g_small = dict(zip(SMALL, summed))
    for n in SMALL_SHARDED:
        width = w[n].shape[-1]
        g_small[n] = lax.dynamic_slice_in_dim(g_small[n], chip * width, width, axis=g_small[n].ndim - 1)

    _comm_only(plan, [[plan.job("ps", n, l) for n, l in items if f"{n}:{l}" not in plan.shared]],
               name="reduce_pair_share")
    grads_big = [plan.bufs["G:" + n] for n in BIG]

    grad, delta, new_m, new_v = {}, {}, {}, {}
    for n, g in zip(BIG, grads_big):
        grad[n], delta[n], new_m[n], new_v[n] = _adamw(w[n], g, m[n], v[n], name=f"adamw_{n}")
    shapes = [w[n].shape for n in SMALL]
    wp, sizes = _pack([w[n] for n in SMALL])
    gp, _ = _pack([g_small[n] for n in SMALL])
    mp, _ = _pack([m[n] for n in SMALL])
    vp, _ = _pack([v[n] for n in SMALL])
    R = wp.shape[0]
    _, dp, m2p, v2p = _adamw(wp.reshape(1, R, LANES), gp.reshape(1, R, LANES), mp.reshape(1, R, LANES),
                             vp.reshape(1, R, LANES), name="adamw_small")
    for n, d_, m_, v_ in zip(SMALL, _unpack(dp, sizes, shapes), _unpack(m2p, sizes, shapes),
                             _unpack(v2p, sizes, shapes)):
        grad[n] = g_small[n]
        delta[n], new_m[n], new_v[n] = d_, m_, v_
    return loss, dx, grad, delta, new_m, new_v


def kernel(x, norm_mix, norm_ffn, norm_final, ab_w_in, a_ln_g, a_ln_b, a_w_s, a_b_s, b_conv_w, b_conv_b, b_ln_g, b_ln_b, ab_w_out, c_w_in, c_conv_w, c_w_out, f_w_up, f_conv_w, f_w_down, loss_target, m_norm_mix, m_norm_ffn, m_norm_final, m_ab_w_in, m_a_ln_g, m_a_ln_b, m_a_w_s, m_a_b_s, m_b_conv_w, m_b_conv_b, m_b_ln_g, m_b_ln_b, m_ab_w_out, m_c_w_in, m_c_conv_w, m_c_w_out, m_f_w_up, m_f_conv_w, m_f_w_down, v_norm_mix, v_norm_ffn, v_norm_final, v_ab_w_in, v_a_ln_g, v_a_ln_b, v_a_w_s, v_a_b_s, v_b_conv_w, v_b_conv_b, v_b_ln_g, v_b_ln_b, v_ab_w_out, v_c_w_in, v_c_conv_w, v_c_w_out, v_f_w_up, v_f_conv_w, v_f_w_down):
    given = dict(locals())
    w = {n: given[n] for n in ALL_WEIGHTS}
    m = {n: given["m_" + n] for n in ALL_WEIGHTS}
    v = {n: given["v_" + n] for n in ALL_WEIGHTS}
    T = x.shape[1]
    loss, dx, grad, delta, new_m, new_v = _step(x.reshape(T, D_MODEL), loss_target.reshape(T, D_MODEL), w, m, v)
    out = [loss[0, 0], dx.reshape(x.shape)]
    for d in (grad, delta, new_m, new_v):
        out += [d[n] for n in ALL_WEIGHTS]
    return tuple(out)
```

```python
import functools
import math

import jax
import jax.numpy as jnp
from jax import lax
from jax.experimental import pallas as pl
from jax.experimental.pallas import tpu as pltpu

F32 = jnp.float32
BF16 = jnp.bfloat16

EPS = 1e-6
D_MODEL = 1024
CHUNK = 128
HEAD_DIM = 128
A_HEADS = 4
D_A = 512
D_B = 512
B_CONV = 31
C_CONV = 3
D_FF = 2816
F_CONV = 3
N_CHIPS = 4

ADAM_LR = 0.001
ADAM_B1 = 0.9
ADAM_B2 = 0.999
ADAM_EPS = 1e-08
ADAM_WD = 0.01
ADAM_STEP = 10

SUBLANES = 8
LANES = 128
HALO_SHORT = 16
HALO_LONG = 32
VMEM_BYTES_MAX = 60000 * 1024

INV_SQRT2 = 1.0 / math.sqrt(2.0)
INV_SQRT_2PI = 1.0 / math.sqrt(2.0 * math.pi)

MESH = pl.DeviceIdType.MESH


def _cparams(*sem):
    return pltpu.CompilerParams(dimension_semantics=sem, vmem_limit_bytes=VMEM_BYTES_MAX)


def _pick(total, pref):
    for c in (2048, 1024, 512, 256, 128):
        if c <= pref and total % c == 0:
            return c
    raise ValueError(f"no tile for {total}")


def _sigmoid(x):
    return jax.nn.sigmoid(x)


def _silu(x):
    return x * _sigmoid(x)


def _dsilu(x):
    s = _sigmoid(x)
    return s * (1.0 + x * (1.0 - s))


def _gelu(x):
    return 0.5 * x * (1.0 + lax.erf(x * INV_SQRT2))


def _dgelu(x):
    return 0.5 * (1.0 + lax.erf(x * INV_SQRT2)) + x * jnp.exp(-0.5 * x * x) * INV_SQRT_2PI


def _ln_stats(x):
    mu = jnp.mean(x, axis=-1, keepdims=True)
    xc = x - mu
    var = jnp.mean(xc * xc, axis=-1, keepdims=True)
    r = lax.rsqrt(var + EPS)
    return xc * r, r


def _ln_bwd(dy, xh, r, g):
    dxh = dy * g
    m1 = jnp.mean(dxh, axis=-1, keepdims=True)
    m2 = jnp.mean(dxh * xh, axis=-1, keepdims=True)
    return r * (dxh - m1 - xh * m2)


def _rowsum(x):
    return jnp.sum(x, axis=0, keepdims=True)


HBM_REF = pl.BlockSpec(memory_space=pltpu.HBM)


def _place():
    x, y, c = lax.axis_index("x"), lax.axis_index("y"), lax.axis_index("c")
    peers = [(1 - x, y), (x, 1 - y), (1 - x, 1 - y)]
    return x, y, c, 2 * x + y, (x, y, 1 - c), peers


def _half(rows, which):
    return pl.ds(which * (rows // 2), rows // 2)


def _remote(src, dst, send_sem, recv_sem, device):
    return pltpu.make_async_remote_copy(src_ref=src, dst_ref=dst, send_sem=send_sem, recv_sem=recv_sem,
                                        device_id=device, device_id_type=MESH)


class _Job:
    def __init__(self, reads, writes, ncopies, copies):
        self.reads, self.writes, self.ncopies, self.copies = reads, writes, ncopies, copies


def _share(rows, which, part, parts):
    nr = rows // 2 // parts
    return pl.ds(which * (rows // 2) + part * nr, nr)


def _slot(ref, chip, rows, cols):
    if ref.shape[1] == N_CHIPS:
        return ref.at[0, chip, rows]
    return ref.at[0, chip // 2, rows, pl.ds(pl.multiple_of((chip % 2) * cols, LANES), cols)]


def _job_gather_ici(name, rows, cols, part, parts):
    def copies(src, dst, sem):
        x, y, c, k, sib, peers = _place()
        mine_rows = _share(rows, c, part, parts)
        out = []
        for j, (px, py) in enumerate(peers):
            mine = _slot(src[name], k, mine_rows, cols)
            out.append((_remote(mine, _slot(dst[name], k, mine_rows, cols), sem(j, 0), sem(j, 1), (px, py, c)),
                        _remote(mine, _slot(dst[name], 2 * px + py, mine_rows, cols), sem(j, 0), sem(j, 1),
                                (px, py, c))))
        return out
    return _Job([], [name], 3, copies)


def _job_gather_d2d(name, rows, cols, part, parts):
    def copies(src, dst, sem):
        x, y, c, k, sib, peers = _place()
        out = []
        for j, (px, py) in enumerate(peers):
            mine_rows, their_rows = _share(rows, c, part, parts), _share(rows, 1 - c, part, parts)
            landed = _slot(src[name], 2 * px + py, mine_rows, cols)
            out.append((_remote(landed, _slot(dst[name], 2 * px + py, mine_rows, cols), sem(j, 0), sem(j, 1), sib),
                        _remote(landed, _slot(dst[name], 2 * px + py, their_rows, cols), sem(j, 0), sem(j, 1), sib)))
        return out
    return _Job([], [name], 3, copies)


def _job_chip_gather(sname, dname):
    def copies(src, dst, sem):
        x, y, c, k, sib, peers = _place()
        return [(_remote(src[sname], dst[dname].at[k], sem(j, 0), sem(j, 1), (px, py, c)),
                 _remote(src[sname], dst[dname].at[2 * px + py], sem(j, 0), sem(j, 1), (px, py, c)))
                for j, (px, py) in enumerate(peers)]
    return _Job([sname], [dname], 3, copies)


def _job_pair_exchange(gname, tname, rows, part, parts):
    nr = rows // 2 // parts

    def copies(src, dst, sem):
        x, y, c, k, sib, peers = _place()
        cp = _remote(src[gname].at[:, _share(rows, 1 - c, part, parts), :],
                     dst[tname].at[:, pl.ds(part * nr, nr), :], sem(0, 0), sem(0, 1), sib)
        return [(cp, cp)]
    return _Job([gname], [tname], 1, copies)


def _job_chip_exchange(pname, lname, r0, nr):
    def copies(src, dst, sem):
        x, y, c, k, sib, peers = _place()
        out = []
        for j, (px, py) in enumerate(peers):
            cp = _remote(src[pname].at[2 * px + py, pl.ds(r0, nr)], dst[lname].at[j, pl.ds(r0, nr)],
                         sem(j, 0), sem(j, 1), (px, py, c))
            out.append((cp, cp))
        return out
    return _Job([pname], [lname], 3, copies)


def _job_pair_share(name, layer, rows):
    def copies(src, dst, sem):
        x, y, c, k, sib, peers = _place()
        mine = src[name].at[layer, _half(rows, c)]
        return [(_remote(mine, dst[name].at[layer, _half(rows, c)], sem(0, 0), sem(0, 1), sib),
                 _remote(mine, dst[name].at[layer, _half(rows, 1 - c)], sem(0, 0), sem(0, 1), sib))]
    return _Job([], [name], 1, copies)


class _Comm:
    def __init__(self, plan, jobs, late=0):
        self.plan, self.jobs, self.late = plan, jobs, late
        self.writes, self.reads = [], []
        for job in jobs:
            for n in job.writes:
                if n not in self.writes:
                    self.writes.append(n)
        for job in jobs:
            for n in job.reads:
                if n not in self.writes and n not in self.reads:
                    self.reads.append(n)
        self.ncopies = sum(job.ncopies for job in jobs)

    def descriptors(self, src, dst, sems, base, stage):
        out = []
        first_late = len(self.jobs) - self.late
        for idx, job in enumerate(self.jobs):
            if stage is None or stage == int(idx >= first_late):
                sem = lambda j, which, base=base: sems.at[base + j, which]
                out += job.copies(src, dst, sem)
            base += job.ncopies
        return out

    def start(self, src, dst, sems, base=0, stage=None):
        for first, _ in self.descriptors(src, dst, sems, base, stage):
            first.start()

    def finish(self, src, dst, sems, base=0, stage=None):
        for _, landed in self.descriptors(src, dst, sems, base, stage):
            landed.wait()


def _comm_operands(comm):
    bufs = comm.plan.bufs
    shapes = [jax.ShapeDtypeStruct(bufs[n].shape, bufs[n].dtype) for n in comm.writes]
    return [bufs[n] for n in comm.reads] + [bufs[n] for n in comm.writes], shapes


def _pallas(comm, body, *, name, grid, in_specs, out_specs, out_shape, compiler_params, scratch_shapes=(),
            aliases=None):
    aliases = dict(aliases or {})
    if comm is None:
        return pl.pallas_call(body, name=name, grid=grid, in_specs=in_specs, out_specs=out_specs,
                              out_shape=out_shape, scratch_shapes=list(scratch_shapes),
                              input_output_aliases=aliases, compiler_params=compiler_params)
    single = not isinstance(out_shape, (list, tuple))
    base_specs = [out_specs] if single else list(out_specs)
    base_shape = [out_shape] if single else list(out_shape)
    nb, nr, nw, nbo, nsc = len(in_specs), len(comm.reads), len(comm.writes), len(base_specs), len(scratch_shapes)

    def wrapped(*refs):
        base_in, rd, wr_in = refs[:nb], refs[nb:nb + nr], refs[nb + nr:nb + nr + nw]
        o0 = nb + nr + nw
        base_out, wr_out = refs[o0:o0 + nbo], refs[o0 + nbo:o0 + nbo + nw]
        scratch, sems = refs[o0 + nbo + nw:o0 + nbo + nw + nsc], refs[-1]
        src = dict(zip(comm.reads, rd))
        src.update(zip(comm.writes, wr_in))
        dst = dict(zip(comm.writes, wr_out))
        first = functools.reduce(jnp.logical_and, [pl.program_id(a) == 0 for a in range(len(grid))])
        last = functools.reduce(jnp.logical_and,
                                [pl.program_id(a) == pl.num_programs(a) - 1 for a in range(len(grid))])

        if not comm.late:
            @pl.when(first)
            def _():
                comm.start(src, dst, sems)
            body(*base_in, *base_out, *scratch)

            @pl.when(last)
            def _():
                comm.finish(src, dst, sems)
            return
        step = functools.reduce(lambda acc, a: acc * grid[a] + pl.program_id(a), range(len(grid)), 0)

        @pl.when(first)
        def _():
            comm.start(src, dst, sems, stage=0)

        @pl.when(step == math.prod(grid) - LATE_STEPS)
        def _():
            comm.finish(src, dst, sems, stage=0)
            comm.start(src, dst, sems, stage=1)
        body(*base_in, *base_out, *scratch)

        @pl.when(last)
        def _():
            comm.finish(src, dst, sems, stage=1)

    operands, shapes = _comm_operands(comm)
    call = pl.pallas_call(
        wrapped, name=name, grid=grid, in_specs=list(in_specs) + [HBM_REF] * (nr + nw),
        out_specs=base_specs + [HBM_REF] * nw, out_shape=base_shape + shapes,
        input_output_aliases={**aliases, **{nb + nr + q: nbo + q for q in range(nw)}},
        scratch_shapes=list(scratch_shapes) + [pltpu.SemaphoreType.DMA((comm.ncopies, 2))],
        compiler_params=compiler_params)

    def run(*args):
        outs = call(*args, *operands)
        for q, n in enumerate(comm.writes):
            comm.plan.bufs[n] = outs[nbo + q]
        return outs[0] if single else list(outs[:nbo])

    return run


def _comm_only(plan, phases, *, name):
    comms = [_Comm(plan, jobs) for jobs in phases]
    both = _Comm(plan, [job for jobs in phases for job in jobs])
    nr, nw = len(both.reads), len(both.writes)

    def body(*refs):
        rd, wr_in, wr_out, sems = refs[:nr], refs[nr:nr + nw], refs[nr + nw:nr + 2 * nw], refs[-1]
        src = dict(zip(both.reads, rd))
        src.update(zip(both.writes, wr_in))
        dst = dict(zip(both.writes, wr_out))
        base = 0
        for comm in comms:
            comm.start(src, dst, sems, base)
            comm.finish(src, dst, sems, base)
            base += comm.ncopies

    operands, shapes = _comm_operands(both)
    outs = pl.pallas_call(
        body, name=name, in_specs=[HBM_REF] * (nr + nw), out_specs=[HBM_REF] * nw, out_shape=shapes,
        input_output_aliases={nr + q: q for q in range(nw)},
        scratch_shapes=[pltpu.SemaphoreType.DMA((both.ncopies, 2))],
    )(*operands)
    for q, n in enumerate(both.writes):
        plan.bufs[n] = outs[q]


def _mm_nn(a, w, *, layer, tm, tn, residual=None, norm=None, out_dtype=F32, name, comm=None):
    T, K = a.shape
    if w.ndim == 4:
        _, S, _, n4 = w.shape
        N = S * n4
        bps = n4 // tn
        w_spec = pl.BlockSpec((None, None, K, tn), lambda j, i: (layer, j // bps, 0, j % bps))
    else:
        N = w.shape[2]
        w_spec = pl.BlockSpec((None, K, tn), lambda j, i: (layer, 0, j))
    in_specs = [pl.BlockSpec((tm, K), lambda j, i: (i, 0)), w_spec]
    args = [a, w]
    if residual is not None:
        in_specs.append(pl.BlockSpec((tm, tn), lambda j, i: (i, j)))
        args.append(residual)
    out_specs = pl.BlockSpec((tm, tn), lambda j, i: (i, j))
    out_shape = jax.ShapeDtypeStruct((T, N), out_dtype)
    if norm is not None:
        assert tn == N
        g, norm_layer = norm
        in_specs.append(pl.BlockSpec((None, 1, N), lambda j, i: (norm_layer, 0, 0)))
        args.append(g)
        out_specs = [out_specs, pl.BlockSpec((tm, tn), lambda j, i: (i, j))]
        out_shape = [out_shape, jax.ShapeDtypeStruct((T, N), BF16)]

    def body(*refs):
        a_ref, w_ref = refs[0], refs[1]
        acc = jnp.dot(a_ref[...].astype(BF16), w_ref[...], preferred_element_type=F32)
        if residual is not None:
            acc = refs[2][...] + acc
        if norm is None:
            refs[-1][...] = acc.astype(out_dtype)
        else:
            refs[-2][...] = acc.astype(out_dtype)
            r = lax.rsqrt(jnp.mean(acc * acc, axis=-1, keepdims=True) + EPS)
            refs[-1][...] = (acc * r * refs[-3][...]).astype(BF16)

    return _pallas(
        comm, body, name=name, grid=(N // tn, T // tm), in_specs=in_specs,
        out_specs=out_specs, out_shape=out_shape,
        compiler_params=_cparams("parallel", "parallel"),
    )(*args)


def _norm_mm_nn(x, g, w, *, g_layer, tm, tn, name, comm=None):
    T, K = x.shape
    _, S, _, n4 = w.shape
    bps = n4 // tn

    def body(x_ref, g_ref, w_ref, h_ref, o_ref):
        @pl.when(pl.program_id(1) == 0)
        def _():
            xf = x_ref[...]
            r = lax.rsqrt(jnp.mean(xf * xf, axis=-1, keepdims=True) + EPS)
            h_ref[...] = (xf * r * g_ref[...]).astype(BF16)
        o_ref[...] = jnp.dot(h_ref[...], w_ref[...], preferred_element_type=F32).astype(BF16)

    return _pallas(
        comm, body, name=name, grid=(T // tm, S * bps),
        in_specs=[pl.BlockSpec((tm, K), lambda i, j: (i, 0)),
                  pl.BlockSpec((None, 1, K), lambda i, j: (g_layer, 0, 0)),
                  pl.BlockSpec((None, None, K, tn), lambda i, j: (0, j // bps, 0, j % bps))],
        out_specs=[pl.BlockSpec((tm, K), lambda i, j: (i, 0)), pl.BlockSpec((tm, tn), lambda i, j: (i, j))],
        out_shape=[jax.ShapeDtypeStruct((T, K), BF16), jax.ShapeDtypeStruct((T, S * n4), BF16)],
        compiler_params=_cparams("parallel", "arbitrary"),
    )(x, g, w)


def _mm_nt(dy, w, *, layer, tm, tn, name, out_dtype=F32, comm=None):
    T = dy.shape[0]
    nt_dims = (((1,), (1,)), ((), ()))
    _, R, N = w.shape

    def body2(dy_ref, w_ref, o_ref):
        o_ref[...] = lax.dot_general(dy_ref[...].astype(BF16), w_ref[...], nt_dims,
                                     preferred_element_type=F32).astype(out_dtype)

    return _pallas(
        comm, body2, name=name, grid=(R // tn, T // tm),
        in_specs=[pl.BlockSpec((tm, N), lambda j, i: (i, 0)),
                  pl.BlockSpec((None, tn, N), lambda j, i: (layer, j, 0))],
        out_specs=pl.BlockSpec((tm, tn), lambda j, i: (i, j)),
        out_shape=jax.ShapeDtypeStruct((T, R), out_dtype),
        compiler_params=_cparams("parallel", "parallel"),
    )(dy, w)


def _mm_tn(a, dy, *, shards, tk, tn, tt, name, comm=None):
    T, K = a.shape
    N = dy.shape[1]
    tn_dims = (((0,), (0,)), ((), ()))
    n4 = N if shards is None else N // shards
    span = max(tn // n4, 1)

    def body(a_ref, dy_ref, o_ref):
        @pl.when(pl.program_id(2) == 0)
        def _():
            o_ref[...] = jnp.zeros_like(o_ref)
        r = lax.dot_general(a_ref[...].astype(BF16), dy_ref[...].astype(BF16), tn_dims,
                            preferred_element_type=F32)
        if span == 1:
            o_ref[...] += r
        else:
            for q in range(span):
                o_ref[q] += r[:, q * n4:(q + 1) * n4]

    if shards is None:
        out_spec = pl.BlockSpec((tk, tn), lambda k, n, t: (k, n))
        out_shape = jax.ShapeDtypeStruct((K, N), F32)
    elif span > 1:
        out_spec = pl.BlockSpec((span, tk, n4), lambda k, n, t: (n, k, 0))
        out_shape = jax.ShapeDtypeStruct((shards, K, n4), F32)
    else:
        bps = n4 // tn
        out_spec = pl.BlockSpec((None, tk, tn), lambda k, n, t: (n // bps, k, n % bps))
        out_shape = jax.ShapeDtypeStruct((shards, K, n4), F32)
    return _pallas(
        comm, body, name=name, grid=(K // tk, N // tn, T // tt),
        in_specs=[pl.BlockSpec((tt, tk), lambda k, n, t: (t, k)),
                  pl.BlockSpec((tt, tn), lambda k, n, t: (t, n))],
        out_specs=out_spec, out_shape=out_shape,
        compiler_params=_cparams("parallel", "parallel", "arbitrary"),
    )(a, dy)


def _rmsnorm_bwd_math(xf, g, dh, dres):
    r = lax.rsqrt(jnp.mean(xf * xf, axis=-1, keepdims=True) + EPS)
    xh = xf * r
    dxh = dh * g
    dx = dres + r * (dxh - xh * jnp.mean(dxh * xh, axis=-1, keepdims=True))
    return dx, _rowsum(dh * xh)


def _mm_nt_norm(dy, w, x, g, dres, *, g_layer, tm, name, comm=None):
    T = dy.shape[0]
    _, S, K, n4 = w.shape
    nt_dims = (((1,), (1,)), ((), ()))

    def body(dy_ref, w_ref, x_ref, g_ref, dres_ref, dx_ref, dg_ref):
        @pl.when(pl.program_id(0) == 0)
        def _():
            dg_ref[...] = jnp.zeros_like(dg_ref)
        dh = None
        for s in range(S):
            part = lax.dot_general(dy_ref[:, s * n4:(s + 1) * n4].astype(BF16), w_ref[s], nt_dims,
                                   preferred_element_type=F32)
            dh = part if dh is None else dh + part
        dx, dg = _rmsnorm_bwd_math(x_ref[...], g_ref[...], dh, dres_ref[...])
        dx_ref[...] = dx
        dg_ref[...] += dg

    row = lambda i: (i, 0)
    return _pallas(
        comm, body, name=name, grid=(T // tm,),
        in_specs=[pl.BlockSpec((tm, S * n4), row),
                  pl.BlockSpec((None, S, K, n4), lambda i: (0, 0, 0, 0)),
                  pl.BlockSpec((tm, K), row),
                  pl.BlockSpec((None, 1, K), lambda i: (g_layer, 0, 0)),
                  pl.BlockSpec((tm, K), row)],
        out_specs=[pl.BlockSpec((tm, K), row), pl.BlockSpec((1, K), lambda i: (0, 0))],
        out_shape=[jax.ShapeDtypeStruct((T, K), F32), jax.ShapeDtypeStruct((1, K), F32)],
        compiler_params=_cparams("arbitrary"),
    )(dy, w, x, g, dres)


def _mm_nn_loss(a, w, residual, tgt, g, *, tm, name, comm=None):
    T, K = a.shape
    D = w.shape[2]

    def body(a_ref, w_ref, res_ref, t_ref, g_ref, loss_ref, dx_ref, dg_ref):
        @pl.when(pl.program_id(0) == 0)
        def _():
            dg_ref[...] = jnp.zeros_like(dg_ref)
            loss_ref[...] = jnp.zeros_like(loss_ref)
        xf = res_ref[...] + jnp.dot(a_ref[...], w_ref[...], preferred_element_type=F32)
        gg = g_ref[...]
        r = lax.rsqrt(jnp.mean(xf * xf, axis=-1, keepdims=True) + EPS)
        xh = xf * r
        err = xh * gg - t_ref[...]
        row = jnp.mean(err * err, axis=-1, keepdims=True)
        loss_ref[...] += 0.5 * jnp.sum(row, axis=0, keepdims=True)
        dy = err * (1.0 / D)
        dg_ref[...] += _rowsum(dy * xh)
        dxh = dy * gg
        dx_ref[...] = r * (dxh - xh * jnp.mean(dxh * xh, axis=-1, keepdims=True))

    row_spec = pl.BlockSpec((tm, D), lambda i: (i, 0))
    return _pallas(
        comm, body, name=name, grid=(T // tm,),
        in_specs=[pl.BlockSpec((tm, K), lambda i: (i, 0)), pl.BlockSpec((None, K, D), lambda i: (0, 0, 0)),
                  row_spec, row_spec, pl.BlockSpec((1, D), lambda i: (0, 0))],
        out_specs=[pl.BlockSpec((1, 1), lambda i: (0, 0)), row_spec, pl.BlockSpec((1, D), lambda i: (0, 0))],
        out_shape=[jax.ShapeDtypeStruct((1, 1), F32), jax.ShapeDtypeStruct((T, D), F32),
                   jax.ShapeDtypeStruct((1, D), F32)],
        compiler_params=_cparams("arbitrary"),
    )(a, w, residual, tgt, g)


CONV_ROWS = 64
CONV_COLS = 256


def _halo_prev_index(tm, halo):
    per = tm // halo
    return lambda i: jnp.maximum(i * per - 1, 0)


def _halo_next_index(tm, halo, total):
    per = tm // halo
    last = total // halo - 1
    return lambda i: jnp.minimum((i + 1) * per, last)


def _causal_mask():
    t = lax.broadcasted_iota(jnp.int32, (CHUNK, CHUNK), 0)
    s = lax.broadcasted_iota(jnp.int32, (CHUNK, CHUNK), 1)
    return s <= t


def _mixer_ab_fwd(z, a_ln_g, a_ln_b, w_s, b_s, conv_w, conv_b, b_ln_g, b_ln_b, *, tm, name, comm=None):
    T = z.shape[0]
    nchunk = tm // CHUNK
    halo = HALO_LONG

    def body(za_ref, zb_ref, zh_ref, alg_ref, alb_ref, ws_ref, bs_ref, cw_ref, cbias_ref,
             blg_ref, blb_ref, y_ref, cb_ref, ext_ref):
        i = pl.program_id(0)
        gu = _gelu(za_ref[:, :D_A].astype(F32))
        gv = _gelu(za_ref[:, D_A:].astype(F32))
        xh, _ = _ln_stats(gv)
        lv = (xh * alg_ref[...] + alb_ref[...]).astype(BF16)
        mask = _causal_mask()
        for h in range(A_HEADS):
            wm = jnp.where(mask, ws_ref[h], 0.0).astype(BF16)
            cols = slice(h * HEAD_DIM, (h + 1) * HEAD_DIM)
            for c in range(nchunk):
                rows = slice(c * CHUNK, (c + 1) * CHUNK)
                mixed = jnp.dot(wm, lv[rows, cols], preferred_element_type=F32) + bs_ref[h]
                y_ref[rows, cols] = (gu[rows, cols] * mixed).astype(BF16)
        ext_ref[halo:halo + tm, :] = zb_ref[:, :D_B].astype(F32) * _sigmoid(zb_ref[:, D_B:].astype(F32))
        prev = zh_ref[:, :D_B].astype(F32) * _sigmoid(zh_ref[:, D_B:].astype(F32))
        ext_ref[0:halo, :] = jnp.where(i > 0, prev, 0.0)
        for rb in range(tm // CONV_ROWS):
            for cb in range(D_B // CONV_COLS):
                cs = slice(cb * CONV_COLS, (cb + 1) * CONV_COLS)
                window = ext_ref[rb * CONV_ROWS:rb * CONV_ROWS + CONV_ROWS + halo, cs]
                acc = jnp.zeros((CONV_ROWS, CONV_COLS), F32)
                for k in range(B_CONV):
                    shifted = _rows_after(window, halo - (B_CONV - 1) + k)[:CONV_ROWS]
                    acc = acc + cw_ref[k:k + 1, cs] * shifted
                cb_ref[rb * CONV_ROWS:(rb + 1) * CONV_ROWS, cs] = acc + cbias_ref[:, cs]
        xhb, _ = _ln_stats(cb_ref[...])
        y_ref[:, D_A:] = _silu(xhb * blg_ref[...] + blb_ref[...]).astype(BF16)

    row = lambda i: (i, 0)
    par = lambda i: (0, 0)
    return _pallas(
        comm, body, name=name, grid=(T // tm,),
        in_specs=[pl.BlockSpec((tm, 2 * D_A), lambda i: (i, 0)),
                  pl.BlockSpec((tm, 2 * D_B), lambda i: (i, 1)),
                  pl.BlockSpec((halo, 2 * D_B), lambda i: (_halo_prev_index(tm, halo)(i), 1)),
                  pl.BlockSpec((1, D_A), par), pl.BlockSpec((1, D_A), par),
                  pl.BlockSpec((A_HEADS, CHUNK, CHUNK), lambda i: (0, 0, 0)),
                  pl.BlockSpec((A_HEADS, CHUNK, 1), lambda i: (0, 0, 0)),
                  pl.BlockSpec((B_CONV, D_B), par), pl.BlockSpec((1, D_B), par),
                  pl.BlockSpec((1, D_B), par), pl.BlockSpec((1, D_B), par)],
        out_specs=[pl.BlockSpec((tm, D_A + D_B), row), pl.BlockSpec((tm, D_B), row)],
        out_shape=[jax.ShapeDtypeStruct((T, D_A + D_B), BF16), jax.ShapeDtypeStruct((T, D_B), F32)],
        scratch_shapes=[pltpu.VMEM((halo + tm, D_B), F32)],
        compiler_params=_cparams("parallel"),
    )(z, z, z, a_ln_g, a_ln_b, w_s, b_s, conv_w, conv_b, b_ln_g, b_ln_b)


def _mixer_ab_bwd_pre(z, cb, dy, a_ln_g, a_ln_b, w_s, b_s, b_ln_g, b_ln_b, *, tm, name, comm=None):
    T = z.shape[0]
    nchunk = tm // CHUNK
    tn_dims = (((0,), (0,)), ((), ()))
    nt_dims = (((1,), (1,)), ((), ()))

    def body(za_ref, cb_ref, dy_ref, alg_ref, alb_ref, ws_ref, bs_ref, blg_ref, blb_ref,
             dza_ref, dcb_ref, dalg_ref, dalb_ref, dws_ref, dbs_ref, dblg_ref, dblb_ref,
             dlv_ref):
        @pl.when(pl.program_id(0) == 0)
        def _():
            for ref in (dalg_ref, dalb_ref, dws_ref, dbs_ref, dblg_ref, dblb_ref):
                ref[...] = jnp.zeros_like(ref)
        ua = za_ref[:, :D_A].astype(F32)
        va = za_ref[:, D_A:].astype(F32)
        gu = _gelu(ua)
        gv = _gelu(va)
        xh, r = _ln_stats(gv)
        alg = alg_ref[...]
        lv = (xh * alg + alb_ref[...]).astype(BF16)
        dya = dy_ref[:, :D_A].astype(F32)
        mask = _causal_mask()
        for h in range(A_HEADS):
            wm = jnp.where(mask, ws_ref[h], 0.0).astype(BF16)
            cols = slice(h * HEAD_DIM, (h + 1) * HEAD_DIM)
            dwm = jnp.zeros((CHUNK, CHUNK), F32)
            dbs = jnp.zeros((CHUNK, 1), F32)
            for c in range(nchunk):
                rows = slice(c * CHUNK, (c + 1) * CHUNK)
                lvb = lv[rows, cols]
                mixed = jnp.dot(wm, lvb, preferred_element_type=F32) + bs_ref[h]
                dyb = dya[rows, cols]
                dza_ref[rows, cols] = (dyb * mixed * _dgelu(ua[rows, cols])).astype(BF16)
                dmixed = dyb * gu[rows, cols]
                dmb = dmixed.astype(BF16)
                dlv_ref[rows, cols] = lax.dot_general(wm, dmb, tn_dims, preferred_element_type=F32)
                dwm = dwm + lax.dot_general(dmb, lvb, nt_dims, preferred_element_type=F32)
                dbs = dbs + jnp.sum(dmixed, axis=1, keepdims=True)
            dws_ref[h] += jnp.where(mask, dwm, 0.0)
            dbs_ref[h] += dbs
        dlv = dlv_ref[...]
        dalg_ref[...] += _rowsum(dlv * xh)
        dalb_ref[...] += _rowsum(dlv)
        dgv = _ln_bwd(dlv, xh, r, alg)
        dza_ref[:, D_A:] = (dgv * _dgelu(va)).astype(BF16)
        xhb, rb = _ln_stats(cb_ref[...])
        blg = blg_ref[...]
        lb = xhb * blg + blb_ref[...]
        dlb = dy_ref[:, D_A:].astype(F32) * _dsilu(lb)
        dblg_ref[...] += _rowsum(dlb * xhb)
        dblb_ref[...] += _rowsum(dlb)
        dcb_ref[...] = _ln_bwd(dlb, xhb, rb, blg)

    row = lambda i: (i, 0)
    par = lambda i: (0, 0)
    par3 = lambda i: (0, 0, 0)
    return _pallas(
        comm, body, name=name, grid=(T // tm,),
        in_specs=[pl.BlockSpec((tm, 2 * D_A), row), pl.BlockSpec((tm, D_B), row),
                  pl.BlockSpec((tm, D_A + D_B), row),
                  pl.BlockSpec((1, D_A), par), pl.BlockSpec((1, D_A), par),
                  pl.BlockSpec((A_HEADS, CHUNK, CHUNK), par3),
                  pl.BlockSpec((A_HEADS, CHUNK, 1), par3),
                  pl.BlockSpec((1, D_B), par), pl.BlockSpec((1, D_B), par)],
        out_specs=[pl.BlockSpec((tm, 2 * D_A), row), pl.BlockSpec((tm, D_B), row),
                   pl.BlockSpec((1, D_A), par), pl.BlockSpec((1, D_A), par),
                   pl.BlockSpec((A_HEADS, CHUNK, CHUNK), par3),
                   pl.BlockSpec((A_HEADS, CHUNK, 1), par3),
                   pl.BlockSpec((1, D_B), par), pl.BlockSpec((1, D_B), par)],
        out_shape=[jax.ShapeDtypeStruct((T, 2 * D_A + 2 * D_B), BF16), jax.ShapeDtypeStruct((T, D_B), F32),
                   jax.ShapeDtypeStruct((1, D_A), F32), jax.ShapeDtypeStruct((1, D_A), F32),
                   jax.ShapeDtypeStruct((A_HEADS, CHUNK, CHUNK), F32),
                   jax.ShapeDtypeStruct((A_HEADS, CHUNK, 1), F32),
                   jax.ShapeDtypeStruct((1, D_B), F32), jax.ShapeDtypeStruct((1, D_B), F32)],
        scratch_shapes=[pltpu.VMEM((tm, D_A), F32)],
        compiler_params=_cparams("arbitrary"),
    )(z, cb, dy, a_ln_g, a_ln_b, w_s, b_s, b_ln_g, b_ln_b)


def _mixer_b_conv_bwd(z, dcb, conv_w, dz, *, tm, name, comm=None):
    T = z.shape[0]
    halo = HALO_LONG

    def body(zb_ref, dcb_ref, dcn_ref, cw_ref, dz_in_ref, dzb_ref, dcw_ref, dbias_ref, dext_ref):
        i = pl.program_id(0)
        last = pl.num_programs(0) - 1

        @pl.when(i == 0)
        def _():
            dcw_ref[...] = jnp.zeros_like(dcw_ref)
            dbias_ref[...] = jnp.zeros_like(dbias_ref)
        dcb = dcb_ref[...]
        dext_ref[0:tm, :] = dcb
        dext_ref[tm:tm + halo, :] = jnp.where(i < last, dcn_ref[...], 0.0)
        dbias_ref[...] += _rowsum(dcb)
        for rb in range(tm // CONV_ROWS):
            for cb in range(D_B // CONV_COLS):
                cs = slice(cb * CONV_COLS, (cb + 1) * CONV_COLS)
                gcs = slice(D_B + cb * CONV_COLS, D_B + (cb + 1) * CONV_COLS)
                rs = slice(rb * CONV_ROWS, (rb + 1) * CONV_ROWS)
                xbb = zb_ref[rs, cs].astype(F32)
                sgb = _sigmoid(zb_ref[rs, gcs].astype(F32))
                yb0 = xbb * sgb
                window = dext_ref[rb * CONV_ROWS:rb * CONV_ROWS + CONV_ROWS + halo, cs]
                acc = jnp.zeros((CONV_ROWS, CONV_COLS), F32)
                for k in range(B_CONV):
                    shifted = _rows_after(window, (B_CONV - 1) - k)[:CONV_ROWS]
                    acc = acc + cw_ref[k:k + 1, cs] * shifted
                    dcw_ref[k:k + 1, cs] += _rowsum(shifted * yb0)
                dzb_ref[rs, cs] = (acc * sgb).astype(BF16)
                dzb_ref[rs, gcs] = (acc * xbb * sgb * (1.0 - sgb)).astype(BF16)

    row = lambda i: (i, 0)
    par = lambda i: (0, 0)
    return _pallas(
        comm, body, name=name, grid=(T // tm,),
        in_specs=[pl.BlockSpec((tm, 2 * D_B), lambda i: (i, 1)),
                  pl.BlockSpec((tm, D_B), row),
                  pl.BlockSpec((halo, D_B), lambda i: (_halo_next_index(tm, halo, T)(i), 0)),
                  pl.BlockSpec((B_CONV, D_B), par), pl.BlockSpec(memory_space=pl.ANY)],
        out_specs=[pl.BlockSpec((tm, 2 * D_B), lambda i: (i, 1)), pl.BlockSpec((B_CONV, D_B), par),
                   pl.BlockSpec((1, D_B), par)],
        out_shape=[jax.ShapeDtypeStruct(dz.shape, BF16), jax.ShapeDtypeStruct((B_CONV, D_B), F32),
                   jax.ShapeDtypeStruct((1, D_B), F32)],
        scratch_shapes=[pltpu.VMEM((tm + halo, D_B), F32)], aliases={4: 0},
        compiler_params=_cparams("arbitrary"),
    )(z, dcb, dcb, conv_w, dz)


def _rows_before(x, a):
    return x if a == 0 else pltpu.roll(x, a, axis=0)


def _rows_after(x, a):
    return x if a == 0 else pltpu.roll(x, x.shape[0] - a, axis=0)


def _conv3(w_ref, x, halo, cs):
    acc = w_ref[2:3, cs] * x[halo:]
    acc = acc + w_ref[1:2, cs] * _rows_before(x, 1)[halo:]
    return acc + w_ref[0:1, cs] * _rows_before(x, 2)[halo:]


def _mixer_c_fwd(z, conv_w, *, tm, name, comm=None):
    T = z.shape[0]
    D = D_MODEL
    halo = HALO_SHORT
    W = CONV_COLS

    def body(bg_ref, cg_ref, xv_ref, cgh_ref, xvh_ref, w_ref, r_ref):
        i = pl.program_id(0)
        for cb in range(D // W):
            cs = slice(cb * W, (cb + 1) * W)
            prev = jnp.where(i > 0, cgh_ref[:, cs].astype(F32) * xvh_ref[:, cs].astype(F32), 0.0)
            p = jnp.concatenate([prev, cg_ref[:, cs].astype(F32) * xv_ref[:, cs].astype(F32)], axis=0)
            r_ref[:, cs] = (bg_ref[:, cs].astype(F32) * _conv3(w_ref, p, halo, cs)).astype(BF16)

    hp = _halo_prev_index(tm, halo)
    return _pallas(
        comm, body, name=name, grid=(T // tm,),
        in_specs=[pl.BlockSpec((tm, D), lambda i: (i, 0)), pl.BlockSpec((tm, D), lambda i: (i, 1)),
                  pl.BlockSpec((tm, D), lambda i: (i, 2)),
                  pl.BlockSpec((halo, D), lambda i: (hp(i), 1)),
                  pl.BlockSpec((halo, D), lambda i: (hp(i), 2)),
                  pl.BlockSpec((None, C_CONV, D), lambda i: (0, 0, 0))],
        out_specs=pl.BlockSpec((tm, D), lambda i: (i, 0)),
        out_shape=jax.ShapeDtypeStruct((T, D), BF16),
        compiler_params=_cparams("parallel"),
    )(z, z, z, z, z, conv_w)


def _mixer_c_bwd(z, dr, conv_w, *, tm, name, comm=None):
    T = z.shape[0]
    D = D_MODEL
    halo = HALO_SHORT
    W = CONV_COLS

    def body(bg_ref, cg_ref, xv_ref, cgh_ref, xvh_ref, bgn_ref, dr_ref, drn_ref, w_ref, dz_ref, dw_ref):
        i = pl.program_id(0)
        last = pl.num_programs(0) - 1

        @pl.when(i == 0)
        def _():
            dw_ref[...] = jnp.zeros_like(dw_ref)
        for cb in range(D // W):
            cs = slice(cb * W, (cb + 1) * W)
            cg = cg_ref[:, cs].astype(F32)
            xv = xv_ref[:, cs].astype(F32)
            dr = dr_ref[:, cs].astype(F32)
            p = cg * xv
            prev = jnp.where(i > 0, cgh_ref[:, cs].astype(F32) * xvh_ref[:, cs].astype(F32), 0.0)
            q = _conv3(w_ref, jnp.concatenate([prev, p], axis=0), halo, cs)
            dz_ref[:, cs] = (dr * q).astype(BF16)
            nxt = jnp.where(i < last, drn_ref[:, cs].astype(F32) * bgn_ref[:, cs].astype(F32), 0.0)
            dq = jnp.concatenate([dr * bg_ref[:, cs].astype(F32), nxt], axis=0)
            dp = None
            for k in range(C_CONV):
                shifted = _rows_after(dq, 2 - k)[:tm]
                term = w_ref[k:k + 1, cs] * shifted
                dp = term if dp is None else dp + term
                dw_ref[k:k + 1, cs] += _rowsum(shifted * p)
            dz_ref[:, D + cb * W:D + (cb + 1) * W] = (dp * xv).astype(BF16)
            dz_ref[:, 2 * D + cb * W:2 * D + (cb + 1) * W] = (dp * cg).astype(BF16)

    hp = _halo_prev_index(tm, halo)
    hn = _halo_next_index(tm, halo, T)
    return _pallas(
        comm, body, name=name, grid=(T // tm,),
        in_specs=[pl.BlockSpec((tm, D), lambda i: (i, 0)), pl.BlockSpec((tm, D), lambda i: (i, 1)),
                  pl.BlockSpec((tm, D), lambda i: (i, 2)),
                  pl.BlockSpec((halo, D), lambda i: (hp(i), 1)),
                  pl.BlockSpec((halo, D), lambda i: (hp(i), 2)),
                  pl.BlockSpec((halo, D), lambda i: (hn(i), 0)),
                  pl.BlockSpec((tm, D), lambda i: (i, 0)),
                  pl.BlockSpec((halo, D), lambda i: (hn(i), 0)),
                  pl.BlockSpec((None, C_CONV, D), lambda i: (0, 0, 0))],
        out_specs=[pl.BlockSpec((tm, 3 * D), lambda i: (i, 0)),
                   pl.BlockSpec((C_CONV, D), lambda i: (0, 0))],
        out_shape=[jax.ShapeDtypeStruct((T, 3 * D), BF16), jax.ShapeDtypeStruct((C_CONV, D), F32)],
        compiler_params=_cparams("arbitrary"),
    )(z, z, z, z, z, z, dr, dr, conv_w)


FFN_COLS = 128


def _ffn_act_fwd(up, conv_w, *, layer, tm, name, comm=None):
    T = up.shape[0]
    halo = HALO_SHORT
    W = FFN_COLS

    def body(up_ref, uph_ref, w_ref, a_ref, upc_ref):
        i = pl.program_id(0)

        def conv(cs):
            prev = jnp.where(i > 0, uph_ref[:, cs], jnp.zeros((halo, W), BF16))
            return _conv3(w_ref, jnp.concatenate([prev, up_ref[:, cs]], axis=0).astype(F32), halo, cs)

        for cb in range(D_FF // W):
            gs = slice(cb * W, (cb + 1) * W)
            vs = slice(D_FF + cb * W, D_FF + (cb + 1) * W)
            g = conv(gs)
            v = conv(vs)
            upc_ref[:, gs] = g.astype(BF16)
            upc_ref[:, vs] = v.astype(BF16)
            a_ref[:, gs] = (_silu(g) * v).astype(BF16)

    return _pallas(
        comm, body, name=name, grid=(T // tm,),
        in_specs=[pl.BlockSpec((tm, 2 * D_FF), lambda i: (i, 0)),
                  pl.BlockSpec((halo, 2 * D_FF), lambda i: (_halo_prev_index(tm, halo)(i), 0)),
                  pl.BlockSpec((None, F_CONV, 2 * D_FF), lambda i: (layer, 0, 0))],
        out_specs=[pl.BlockSpec((tm, D_FF), lambda i: (i, 0)),
                   pl.BlockSpec((tm, 2 * D_FF), lambda i: (i, 0))],
        out_shape=[jax.ShapeDtypeStruct((T, D_FF), BF16), jax.ShapeDtypeStruct((T, 2 * D_FF), BF16)],
        compiler_params=_cparams("parallel"),
    )(up, up, conv_w)


def _ffn_act_bwd(up, upc, da, conv_w, *, layer, tm, name, comm=None):
    T = up.shape[0]
    halo = HALO_SHORT
    W = FFN_COLS

    def body(up_ref, upc_ref, upcn_ref, da_ref, dan_ref, w_ref, dup_ref, dw_ref):
        i = pl.program_id(0)
        last = pl.num_programs(0) - 1

        @pl.when(i == 0)
        def _():
            dw_ref[...] = jnp.zeros_like(dw_ref)
        live = jnp.where(i < last, 1.0, 0.0)
        for cb in range(D_FF // W):
            gs = slice(cb * W, (cb + 1) * W)
            vs = slice(D_FF + cb * W, D_FF + (cb + 1) * W)
            g = jnp.concatenate([upc_ref[:, gs], upcn_ref[:, gs]], axis=0).astype(F32)
            v = jnp.concatenate([upc_ref[:, vs], upcn_ref[:, vs]], axis=0).astype(F32)
            da = jnp.concatenate([da_ref[:, gs].astype(F32), dan_ref[:, gs].astype(F32) * live], axis=0)
            s = _sigmoid(g)
            silu = g * s
            grads = (da * v * (s * (1.0 + g * (1.0 - s))), da * silu)
            for cs, d in zip((gs, vs), grads):
                u = up_ref[:, cs].astype(F32)
                acc = None
                for k in range(F_CONV):
                    shifted = _rows_after(d, 2 - k)[:tm]
                    term = w_ref[k:k + 1, cs] * shifted
                    acc = term if acc is None else acc + term
                    dw_ref[k:k + 1, cs] += _rowsum(shifted * u)
                dup_ref[:, cs] = acc.astype(BF16)

    hn = _halo_next_index(tm, halo, T)
    return _pallas(
        comm, body, name=name, grid=(T // tm,),
        in_specs=[pl.BlockSpec((tm, 2 * D_FF), lambda i: (i, 0)),
                  pl.BlockSpec((tm, 2 * D_FF), lambda i: (i, 0)),
                  pl.BlockSpec((halo, 2 * D_FF), lambda i: (hn(i), 0)),
                  pl.BlockSpec((tm, D_FF), lambda i: (i, 0)),
                  pl.BlockSpec((halo, D_FF), lambda i: (hn(i), 0)),
                  pl.BlockSpec((None, F_CONV, 2 * D_FF), lambda i: (layer, 0, 0))],
        out_specs=[pl.BlockSpec((tm, 2 * D_FF), lambda i: (i, 0)),
                   pl.BlockSpec((F_CONV, 2 * D_FF), lambda i: (0, 0))],
        out_shape=[jax.ShapeDtypeStruct((T, 2 * D_FF), BF16),
                   jax.ShapeDtypeStruct((F_CONV, 2 * D_FF), F32)],
        compiler_params=_cparams("arbitrary"),
    )(up, upc, upc, da, da, conv_w)


def _local_step(x, tgt, small, plan):
    T = x.shape[0]
    tm_e = _pick(T, 256)
    tm_a = _pick(T, 512)
    tm_b = _pick(T, 128)
    tm = _pick(T, 1024)
    tm_f = _pick(T, 512)
    tt = _pick(T, 2048)
    nm = small["norm_mix"].reshape(2, 1, D_MODEL)
    nf = small["norm_ffn"].reshape(2, 1, D_MODEL)
    ngf = small["norm_final"].reshape(1, D_MODEL)
    b_s = small["a_b_s"].reshape(A_HEADS, CHUNK, 1)
    w_s = small["a_w_s"].reshape(A_HEADS, CHUNK, CHUNK)
    b_conv_w = small["b_conv_w"].reshape(B_CONV, D_B)
    sg = {}
    wt, cm = plan.weight, plan.comm

    h_m0, z_ab = _norm_mm_nn(x, nm, wt("ab_w_in", 0), g_layer=0, tm=tm, tn=512, name="ab_in", comm=cm("ab_in"))
    yab, cb = _mixer_ab_fwd(z_ab, small["a_ln_g"], small["a_ln_b"], w_s, b_s, b_conv_w, small["b_conv_b"],
                            small["b_ln_g"], small["b_ln_b"], tm=tm_e, name="mixer_ab", comm=cm("mixer_ab"))
    x1, h_f0 = _mm_nn(yab, wt("ab_w_out", 0), layer=0, tm=tm, tn=D_MODEL, residual=x, norm=(nf, 0),
                      name="ab_out", comm=cm("ab_out"))

    def ffn_fwd(xin, h, layer):
        up = _mm_nn(h, wt("f_w_up", layer), layer=0, tm=tm, tn=2 * 1408, out_dtype=BF16, name=f"ffn_up{layer}",
                    comm=cm(f"ffn_up{layer}"))
        a, upc = _ffn_act_fwd(up, small["f_conv_w"], layer=layer, tm=tm_a, name=f"ffn_act{layer}",
                              comm=cm(f"ffn_act{layer}"))
        if layer == 0:
            out = _mm_nn(a, wt("f_w_down", layer), layer=0, tm=tm, tn=D_MODEL, residual=xin, norm=(nm, 1),
                         name=f"ffn_down{layer}", comm=cm(f"ffn_down{layer}"))
        else:
            out = _mm_nn_loss(a, wt("f_w_down", layer), xin, tgt, ngf, tm=tm, name=f"ffn_down{layer}",
                              comm=cm(f"ffn_down{layer}"))
        return up, upc, a, out

    up0, upc0, a0, (x2, h_m1) = ffn_fwd(x1, h_f0, 0)
    z_c = _mm_nn(h_m1, wt("c_w_in", 0), layer=0, tm=tm, tn=768, out_dtype=BF16, name="c_in", comm=cm("c_in"))
    r = _mixer_c_fwd(z_c, small["c_conv_w"], tm=tm_e, name="mixer_c", comm=cm("mixer_c"))
    x3, h_f1 = _mm_nn(r, wt("c_w_out", 0), layer=0, tm=tm, tn=D_MODEL, residual=x2, norm=(nf, 1),
                      name="c_out", comm=cm("c_out"))
    up1, upc1, a1, (loss, dx, sg["norm_final"]) = ffn_fwd(x3, h_f1, 1)

    def ffn_bwd(dx, xin, h, up, upc, a, layer):
        da = _mm_nt(dx, wt("f_w_down", layer), layer=0, tm=tm, tn=1408, out_dtype=BF16,
                    name=f"ffn_down_dx{layer}", comm=cm(f"ffn_down_dx{layer}"))
        plan.grad_ready("f_w_down", layer, _mm_tn(a, dx, shards=None, tk=1408, tn=1024, tt=tt,
                                                  name=f"ffn_down_dw{layer}", comm=cm(f"ffn_down_dw{layer}")))
        dup, dcw = _ffn_act_bwd(up, upc, da, small["f_conv_w"], layer=layer, tm=tm_b, name=f"ffn_act_bwd{layer}",
                                comm=cm(f"ffn_act_bwd{layer}"))
        plan.grad_ready("f_w_up", layer, _mm_tn(h, dup, shards=N_CHIPS, tk=512, tn=2 * 1408, tt=tt,
                                                name=f"ffn_up_dw{layer}", comm=cm(f"ffn_up_dw{layer}")))
        dxin, dg = _mm_nt_norm(dup, wt("f_w_up", layer), xin, nf, dx, g_layer=layer, tm=tm_f,
                               name=f"ffn_up_dx{layer}", comm=cm(f"ffn_up_dx{layer}"))
        return dxin, dg, dcw

    dx, dnf1, dfc1 = ffn_bwd(dx, x3, h_f1, up1, upc1, a1, 1)
    dr = _mm_nt(dx, wt("c_w_out", 0), layer=0, tm=tm, tn=512, out_dtype=BF16, name="c_out_dx", comm=cm("c_out_dx"))
    plan.grad_ready("c_w_out", 0, _mm_tn(r, dx, shards=None, tk=1024, tn=1024, tt=tt, name="c_out_dw",
                                         comm=cm("c_out_dw")))
    dz_c, dccw = _mixer_c_bwd(z_c, dr, small["c_conv_w"], tm=tm_e, name="mixer_c_bwd", comm=cm("mixer_c_bwd"))
    sg["c_conv_w"] = dccw.reshape(1, C_CONV, D_MODEL)
    plan.grad_ready("c_w_in", 0, _mm_tn(h_m1, dz_c, shards=N_CHIPS, tk=1024, tn=768, tt=tt, name="c_in_dw",
                                        comm=cm("c_in_dw")))
    dx, dnm1 = _mm_nt_norm(dz_c, wt("c_w_in", 0), x2, nm, dx, g_layer=1, tm=tm_f, name="c_in_dx",
                           comm=cm("c_in_dx"))
    dx, dnf0, dfc0 = ffn_bwd(dx, x1, h_f0, up0, upc0, a0, 0)
    dyab = _mm_nt(dx, wt("ab_w_out", 0), layer=0, tm=tm, tn=512, out_dtype=BF16, name="ab_out_dx",
                  comm=cm("ab_out_dx"))
    plan.grad_ready("ab_w_out", 0, _mm_tn(yab, dx, shards=None, tk=1024, tn=1024, tt=tt, name="ab_out_dw",
                                          comm=cm("ab_out_dw")))
    (dza, dcb, sg["a_ln_g"], sg["a_ln_b"], dws, dbs, sg["b_ln_g"], sg["b_ln_b"]) = _mixer_ab_bwd_pre(
        z_ab, cb, dyab, small["a_ln_g"], small["a_ln_b"], w_s, b_s, small["b_ln_g"], small["b_ln_b"],
        tm=tm_e, name="mixer_ab_bwd", comm=cm("mixer_ab_bwd"))
    dz_ab, dbcw, sg["b_conv_b"] = _mixer_b_conv_bwd(z_ab, dcb, b_conv_w, dza, tm=tm_e, name="mixer_b_conv_bwd",
                                                    comm=cm("mixer_b_conv_bwd"))
    sg["a_w_s"] = dws.reshape(1, A_HEADS, CHUNK, CHUNK)
    sg["a_b_s"] = dbs.reshape(1, A_HEADS, CHUNK)
    sg["b_conv_w"] = dbcw.reshape(1, B_CONV, D_B)
    plan.grad_ready("ab_w_in", 0, _mm_tn(h_m0, dz_ab, shards=N_CHIPS, tk=1024, tn=512, tt=tt, name="ab_in_dw",
                                         comm=cm("ab_in_dw")))
    dx, dnm0 = _mm_nt_norm(dz_ab, wt("ab_w_in", 0), x, nm, dx, g_layer=0, tm=tm_f, name="ab_in_dx",
                           comm=cm("ab_in_dx"))

    sg["norm_mix"] = [dnm0, dnm1]
    sg["norm_ffn"] = [dnf0, dnf1]
    sg["f_conv_w"] = [dfc0, dfc1]
    return loss, dx, sg


BLOCK_BYTES = 3 * 1024 * 1024


BF16_SUBLANES = 16


def _row_tile(rows, row_bytes, step=SUBLANES):
    best = None
    for tr in range(step, rows + 1, step):
        if rows % tr == 0 and tr * row_bytes <= BLOCK_BYTES:
            best = tr
    if best is None:
        raise ValueError(f"no row tile for {rows}")
    return best


def _place_scalars():
    x, y, c = lax.axis_index("x"), lax.axis_index("y"), lax.axis_index("c")
    return jnp.stack([c, 2 * x + y, 2 * (1 - x) + y, 2 * x + (1 - y), 2 * (1 - x) + (1 - y)]).astype(jnp.int32)


def _cast_into_slot(w, place, *, layer, paired, name):
    L, rows, cols = w.shape
    tr = _row_tile(rows, cols * 4, BF16_SUBLANES)

    def body(place_ref, w_ref, o_ref):
        o_ref[...] = w_ref[...].astype(BF16)

    if paired:
        out_spec = pl.BlockSpec((None, None, tr, cols), lambda i, p: (0, p[1] // 2, i, p[1] % 2))
        out_shape = jax.ShapeDtypeStruct((1, N_CHIPS // 2, rows, 2 * cols), BF16)
    else:
        out_spec = pl.BlockSpec((None, None, tr, cols), lambda i, p: (0, p[1], i, 0))
        out_shape = jax.ShapeDtypeStruct((1, N_CHIPS, rows, cols), BF16)
    return pl.pallas_call(
        body, name=name,
        grid_spec=pltpu.PrefetchScalarGridSpec(
            num_scalar_prefetch=1, grid=(rows // tr,),
            in_specs=[pl.BlockSpec((None, tr, cols), lambda i, p: (layer, i, 0))],
            out_specs=out_spec),
        out_shape=out_shape,
        compiler_params=_cparams("parallel"),
    )(place, w)


def _pair_sum(g, theirs, place, *, name):
    S, rows, cols = g.shape
    half = rows // 2
    tr = _row_tile(half, cols * 4, BF16_SUBLANES)
    nb = half // tr

    def body(place_ref, g_ref, t_ref, o_ref):
        o_ref[...] = (g_ref[...] + t_ref[...]).astype(BF16)

    spec = pl.BlockSpec((None, tr, cols), lambda s, i, p: (s, i, 0))
    return pl.pallas_call(
        body, name=name,
        grid_spec=pltpu.PrefetchScalarGridSpec(
            num_scalar_prefetch=1, grid=(S, nb),
            in_specs=[pl.BlockSpec((None, tr, cols), lambda s, i, p: (s, p[0] * nb + i, 0)), spec],
            out_specs=spec),
        out_shape=jax.ShapeDtypeStruct((S, half, cols), BF16),
        compiler_params=_cparams("parallel", "parallel"),
    )(place, g, theirs)


def _chip_sum(p, r, g_prev, place, *, layer, shape, name):
    L, rows, cols = shape
    half = rows // 2
    tr = _row_tile(half, cols * 4, BF16_SUBLANES)
    nb = half // tr

    def body(place_ref, p_ref, r_ref, *rest):
        o_ref = rest[-1]
        mine = p_ref[...].astype(F32)
        peers = [r_ref[j].astype(F32) for j in range(3)]
        acc = None
        for s in range(N_CHIPS):
            term = jnp.where(place_ref[1] == s, mine,
                             jnp.where(place_ref[2] == s, peers[0],
                                       jnp.where(place_ref[3] == s, peers[1], peers[2])))
            acc = term if acc is None else acc + term
        o_ref[...] = acc

    in_specs = [pl.BlockSpec((None, tr, cols), lambda i, pr: (pr[1], i, 0)),
                pl.BlockSpec((3, tr, cols), lambda i, pr: (0, i, 0))]
    args = [place, p, r]
    aliases = {}
    if g_prev is not None:
        in_specs.append(HBM_REF)
        args.append(g_prev)
        aliases = {3: 0}
    return pl.pallas_call(
        body, name=name,
        grid_spec=pltpu.PrefetchScalarGridSpec(
            num_scalar_prefetch=1, grid=(nb,), in_specs=in_specs,
            out_specs=pl.BlockSpec((None, tr, cols), lambda i, pr: (layer, pr[0] * nb + i, 0))),
        out_shape=jax.ShapeDtypeStruct(shape, F32), input_output_aliases=aliases,
        compiler_params=_cparams("parallel"),
    )(*args)


def _adamw_math(w, g, m, v):
    m2 = ADAM_B1 * m + (1.0 - ADAM_B1) * g
    v2 = ADAM_B2 * v + (1.0 - ADAM_B2) * (g * g)
    m_hat = m2 / (1.0 - ADAM_B1 ** ADAM_STEP)
    v_hat = v2 / (1.0 - ADAM_B2 ** ADAM_STEP)
    delta = -ADAM_LR * (m_hat / (jnp.sqrt(v_hat) + ADAM_EPS) + ADAM_WD * w)
    return delta, m2, v2


def _adamw(w, g, m, v, *, name):
    L, rows, cols = w.shape
    tr = _row_tile(rows, cols * 4)

    def body(w_ref, g_ref, m_ref, v_ref, go_ref, d_ref, m2_ref, v2_ref):
        g = g_ref[...]
        d, m2, v2 = _adamw_math(w_ref[...], g, m_ref[...], v_ref[...])
        go_ref[...] = g
        d_ref[...] = d
        m2_ref[...] = m2
        v2_ref[...] = v2

    spec = pl.BlockSpec((None, tr, cols), lambda l, i: (l, i, 0))
    shape = jax.ShapeDtypeStruct(w.shape, F32)
    return pl.pallas_call(
        body, name=name, grid=(L, rows // tr), in_specs=[spec] * 4, out_specs=[spec] * 4,
        out_shape=[shape] * 4,
        compiler_params=_cparams("parallel", "parallel"),
    )(w, g, m, v)


def _allreduce_pack(pack, *, name, comm):
    R = pack.shape[0]
    half = R // 2
    nr, nw = len(comm.reads), len(comm.writes)

    def body(*refs):
        p_ref, rd, wr_in = refs[0], refs[1:1 + nr], refs[1 + nr:1 + nr + nw]
        o_ref, wr_out = refs[1 + nr + nw], refs[2 + nr + nw:2 + nr + 2 * nw]
        sib_ref, chip_ref, parts_ref, sems, comm_sems = refs[2 + nr + 2 * nw:]
        src = dict(zip(comm.reads, rd))
        src.update(zip(comm.writes, wr_in))
        dst = dict(zip(comm.writes, wr_out))
        comm.start(src, dst, comm_sems)
        x, y, c, k, sib, peers = _place()
        swap = _remote(p_ref, sib_ref, sems.at[0, 0], sems.at[0, 1], sib)
        swap.start()
        swap.wait()
        chip_ref[...] = p_ref[...] + sib_ref[...]
        mine = chip_ref.at[pl.ds(pl.multiple_of(c * half, SUBLANES), half)]
        sends = [_remote(mine, parts_ref.at[j], sems.at[1 + j, 0], sems.at[1 + j, 1], (px, py, c))
                 for j, (px, py) in enumerate(peers)]
        for rc in sends:
            rc.start()
        for rc in sends:
            rc.wait()
        own = mine[...]
        others = [parts_ref[j] for j in range(3)]
        acc = None
        for s in range(N_CHIPS):
            term = own
            for j, (px, py) in enumerate(peers):
                term = jnp.where(2 * px + py == s, others[j], term)
            acc = term if acc is None else acc + term
        done = o_ref.at[pl.ds(pl.multiple_of(c * half, SUBLANES), half)]
        done[...] = acc
        theirs = o_ref.at[pl.ds(pl.multiple_of((1 - c) * half, SUBLANES), half)]
        share = _remote(done, done, sems.at[4, 0], sems.at[4, 1], sib)
        share.start()
        _remote(done, theirs, sems.at[4, 0], sems.at[4, 1], sib).wait()
        comm.finish(src, dst, comm_sems)

    vm = pl.BlockSpec(memory_space=pltpu.VMEM)
    operands, shapes = _comm_operands(comm)
    outs = pl.pallas_call(
        body, name=name, in_specs=[vm] + [HBM_REF] * (nr + nw), out_specs=[vm] + [HBM_REF] * nw,
        out_shape=[jax.ShapeDtypeStruct((R, LANES), F32)] + shapes,
        input_output_aliases={1 + nr + q: 1 + q for q in range(nw)},
        scratch_shapes=[pltpu.VMEM((R, LANES), F32), pltpu.VMEM((R, LANES), F32),
                        pltpu.VMEM((3, half, LANES), F32), pltpu.SemaphoreType.DMA((5, 2)),
                        pltpu.SemaphoreType.DMA((comm.ncopies, 2))],
        compiler_params=pltpu.CompilerParams(vmem_limit_bytes=VMEM_BYTES_MAX),
    )(pack, *operands)
    for q, n in enumerate(comm.writes):
        comm.plan.bufs[n] = outs[1 + q]
    return outs[0]


PACK_UNIT = SUBLANES * LANES


def _pack(arrays):
    flat, sizes = [], []
    for a in arrays:
        pieces = a if isinstance(a, (list, tuple)) else [a]
        v = jnp.concatenate([p.reshape(-1) for p in pieces]) if len(pieces) > 1 else pieces[0].reshape(-1)
        size = v.shape[0]
        padded = -(-size // PACK_UNIT) * PACK_UNIT
        flat.append(jnp.pad(v, (0, padded - size)))
        sizes.append((size, padded))
    total = sum(p for _, p in sizes)
    if (total // PACK_UNIT) % 2:
        flat.append(jnp.zeros((PACK_UNIT,), F32))
    return jnp.concatenate(flat).reshape(-1, LANES), sizes


def _unpack(pack, sizes, shapes):
    v = pack.reshape(-1)
    out, off = [], 0
    for (size, padded), shape in zip(sizes, shapes):
        out.append(v[off:off + size].reshape(shape))
        off += padded
    return out


BIG = ("ab_w_in", "ab_w_out", "c_w_in", "c_w_out", "f_w_up", "f_w_down")
COL_SHARDED = ("ab_w_in", "c_w_in", "f_w_up")
PAIRED = ("f_w_up",)
SMALL_REPLICATED = ("norm_mix", "norm_ffn", "norm_final", "a_ln_g", "a_ln_b", "a_w_s", "a_b_s",
                    "b_conv_b", "b_ln_g", "b_ln_b")
SMALL_SHARDED = ("b_conv_w", "c_conv_w", "f_conv_w")
SMALL = SMALL_REPLICATED + SMALL_SHARDED
ALL_WEIGHTS = ("norm_mix", "norm_ffn", "norm_final", "ab_w_in", "a_ln_g", "a_ln_b", "a_w_s", "a_b_s",
               "b_conv_w", "b_conv_b", "b_ln_g", "b_ln_b", "ab_w_out", "c_w_in", "c_conv_w", "c_w_out",
               "f_w_up", "f_conv_w", "f_w_down")


LATE = "late"
LATE_STEPS = 2
SCHEDULE = {
    "ab_in": [("gi", "f_w_up", 0, 0, 4), ("gi", "ab_w_out", 0)],
    "mixer_ab": [("gd", "f_w_up", 0, 0, 4), ("gd", "ab_w_out", 0), ("gi", "f_w_up", 0, 1, 4),
                 ("gi", "f_w_up", 0, 2, 4), ("gi", "f_w_up", 0, 3, 4)],
    "ab_out": [("gd", "f_w_up", 0, 1, 4), ("gd", "f_w_up", 0, 2, 4), ("gd", "f_w_up", 0, 3, 4)],
    "ffn_up0": [("gi", "f_w_down", 0), ("gi", "c_w_in", 0, 0, 2)],
    "ffn_act0": [("gd", "f_w_down", 0), ("gd", "c_w_in", 0, 0, 2), ("gi", "c_w_in", 0, 1, 2),
                 ("gi", "f_w_up", 1, 0, 4)],
    "ffn_down0": [("gd", "c_w_in", 0, 1, 2), ("gd", "f_w_up", 1, 0, 4), ("gi", "f_w_up", 1, 1, 4),
                  ("gi", "c_w_out", 0)],
    "c_in": [("gd", "f_w_up", 1, 1, 4), ("gd", "c_w_out", 0), ("gi", "f_w_up", 1, 2, 4),
             ("gi", "f_w_up", 1, 3, 4), LATE, ("gd", "f_w_up", 1, 2, 4), ("gd", "f_w_up", 1, 3, 4)],
    "ffn_up1": [("gi", "f_w_down", 1), LATE, ("gd", "f_w_down", 1)],
    "ffn_act_bwd1": [("px", "f_w_down", 1)],
    "ffn_up_dx1": [("cx", "f_w_down", 1), ("px", "f_w_up", 1)],
    "mixer_c_bwd": [("cx", "f_w_up", 1, 0, 4), ("px", "c_w_out", 0)],
    "c_in_dx": [("cx", "f_w_up", 1, 1, 4), ("px", "c_w_in", 0)],
    "ffn_act_bwd0": [("cx", "f_w_up", 1, 2, 4), ("cx", "c_w_out", 0), ("cx", "c_w_in", 0),
                     ("px", "f_w_down", 0), ("ps", "f_w_down", 1)],
    "ffn_up_dx0": [("cx", "f_w_down", 0), ("cx", "f_w_up", 1, 3, 4), ("px", "f_w_up", 0)],
    "mixer_ab_bwd": [("cx", "f_w_up", 0, 0, 4), ("px", "ab_w_out", 0)],
    "mixer_b_conv_bwd": [("cx", "f_w_up", 0, 1, 4), ("cx", "f_w_up", 0, 2, 4), ("cx", "ab_w_out", 0),
                         ("ps", "c_w_out", 0), ("ps", "c_w_in", 0), ("ps", "f_w_down", 0),
                         ("ps", "f_w_up", 1)],
    "ab_in_dx": [("cx", "f_w_up", 0, 3, 4), ("px", "ab_w_in", 0)],
}


class _Plan:
    def __init__(self, shapes, place):
        self.shapes, self.place, self.bufs = shapes, place, {}
        self.summed, self.shared = set(), set()

    def weight(self, name, layer):
        g = self.bufs[f"w:{name}:{layer}"]
        if name in COL_SHARDED:
            return g
        _, S, rows, cols = g.shape
        return g.reshape(1, S * rows, cols)

    def grad_ready(self, name, layer, g):
        _, rows, cols = self.shapes[name]
        hbm = lambda a: pltpu.with_memory_space_constraint(a, pltpu.HBM)
        self.bufs[f"g:{name}:{layer}"] = g.reshape(N_CHIPS, rows, cols)
        self.bufs[f"t:{name}:{layer}"] = hbm(lax.empty((N_CHIPS, rows // 2, cols), F32))
        self.bufs[f"l:{name}:{layer}"] = hbm(lax.empty((3, rows // 2, cols), BF16))

    def job(self, kind, name, layer, part=0, parts=1):
        _, rows, cols = self.shapes[name]
        key = f"{name}:{layer}"
        if kind == "gi":
            return _job_gather_ici("w:" + key, rows, cols, part, parts)
        if kind == "gd":
            return _job_gather_d2d("w:" + key, rows, cols, part, parts)
        if kind == "px":
            return _job_pair_exchange("g:" + key, "t:" + key, rows, part, parts)
        if kind == "cx":
            if "p:" + key not in self.bufs:
                self.bufs["p:" + key] = _pair_sum(self.bufs["g:" + key], self.bufs["t:" + key], self.place,
                                                  name=f"pair_sum_{name}{layer}")
            nr = rows // 2 // parts
            return _job_chip_exchange("p:" + key, "l:" + key, part * nr, nr)
        if kind == "ps":
            self.chip_sum(name, layer)
            self.shared.add(key)
            return _job_pair_share("G:" + name, layer, rows)
        raise ValueError(kind)

    def chip_sum(self, name, layer):
        key = f"{name}:{layer}"
        if key not in self.summed:
            self.summed.add(key)
            self.bufs["G:" + name] = _chip_sum(self.bufs["p:" + key], self.bufs["l:" + key],
                                               self.bufs.get("G:" + name), self.place, layer=layer,
                                               shape=self.shapes[name], name=f"chip_sum_{name}{layer}")

    def comm(self, call):
        specs = SCHEDULE.get(call)
        if specs is None:
            return None
        late = len(specs) - 1 - specs.index(LATE) if LATE in specs else 0
        return _Comm(self, [self.job(*spec) for spec in specs if spec != LATE], late)


def _step(x, tgt, w, m, v):
    chip = 2 * lax.axis_index("x") + lax.axis_index("y")
    place = _place_scalars()
    plan = _Plan({n: w[n].shape for n in BIG}, place)
    items = [(n, l) for n in BIG for l in range(w[n].shape[0])]

    for n, l in items:
        plan.bufs[f"w:{n}:{l}"] = _cast_into_slot(w[n], place, layer=l, paired=n in PAIRED, name=f"cast_{n}{l}")
    conv_pack, conv_sizes = _pack([w[n] for n in SMALL_SHARDED])
    hbm = lambda a: pltpu.with_memory_space_constraint(a, pltpu.HBM)
    plan.bufs["conv:mine"] = hbm(conv_pack)
    plan.bufs["conv:all"] = hbm(lax.empty((N_CHIPS,) + conv_pack.shape, F32))
    _comm_only(plan, [[plan.job("gi", "ab_w_in", 0), _job_chip_gather("conv:mine", "conv:all")],
                      [plan.job("gd", "ab_w_in", 0)]], name="gather_first")
    conv_shapes = [w[n].shape for n in SMALL_SHARDED]
    per_chip = [_unpack(plan.bufs["conv:all"][s], conv_sizes, conv_shapes) for s in range(N_CHIPS)]
    small = {n: w[n] for n in SMALL_REPLICATED}
    for idx, n in enumerate(SMALL_SHARDED):
        small[n] = jnp.concatenate([jnp.where(chip == s, w[n], per_chip[s][idx]) for s in range(N_CHIPS)], axis=-1)

    loss, dx, sg = _local_step(x, tgt, small, plan)

    g_pack, g_sizes = _pack([sg[n] for n in SMALL] + [loss])
    g_sum = _allreduce_pack(g_pack, name="allreduce_small_grads",
                            comm=_Comm(plan, [plan.job("cx", "ab_w_in", 0)]))
    full_shapes = [small[n].shape for n in SMALL]
    *summed, loss = _unpack(g_sum, g_sizes, full_shapes + [(1, 1)])
    g_small = dict(zip(SMALL, summed))
    for n in SMALL_SHARDED:
        width = w[n].shape[-1]
        g_small[n] = lax.dynamic_slice_in_dim(g_small[n], chip * width, width, axis=g_small[n].ndim - 1)

    _comm_only(plan, [[plan.job("ps", n, l) for n, l in items if f"{n}:{l}" not in plan.shared]],
               name="reduce_pair_share")
    grads_big = [plan.bufs["G:" + n] for n in BIG]

    grad, delta, new_m, new_v = {}, {}, {}, {}
    for n, g in zip(BIG, grads_big):
        grad[n], delta[n], new_m[n], new_v[n] = _adamw(w[n], g, m[n], v[n], name=f"adamw_{n}")
    shapes = [w[n].shape for n in SMALL]
    wp, sizes = _pack([w[n] for n in SMALL])
    gp, _ = _pack([g_small[n] for n in SMALL])
    mp, _ = _pack([m[n] for n in SMALL])
    vp, _ = _pack([v[n] for n in SMALL])
    R = wp.shape[0]
    _, dp, m2p, v2p = _adamw(wp.reshape(1, R, LANES), gp.reshape(1, R, LANES), mp.reshape(1, R, LANES),
                             vp.reshape(1, R, LANES), name="adamw_small")
    for n, d_, m_, v_ in zip(SMALL, _unpack(dp, sizes, shapes), _unpack(m2p, sizes, shapes),
                             _unpack(v2p, sizes, shapes)):
        grad[n] = g_small[n]
        delta[n], new_m[n], new_v[n] = d_, m_, v_
    return loss, dx, grad, delta, new_m, new_v


def kernel(x, norm_mix, norm_ffn, norm_final, ab_w_in, a_ln_g, a_ln_b, a_w_s, a_b_s, b_conv_w, b_conv_b, b_ln_g, b_ln_b, ab_w_out, c_w_in, c_conv_w, c_w_out, f_w_up, f_conv_w, f_w_down, loss_target, m_norm_mix, m_norm_ffn, m_norm_final, m_ab_w_in, m_a_ln_g, m_a_ln_b, m_a_w_s, m_a_b_s, m_b_conv_w, m_b_conv_b, m_b_ln_g, m_b_ln_b, m_ab_w_out, m_c_w_in, m_c_conv_w, m_c_w_out, m_f_w_up, m_f_conv_w, m_f_w_down, v_norm_mix, v_norm_ffn, v_norm_final, v_ab_w_in, v_a_ln_g, v_a_ln_b, v_a_w_s, v_a_b_s, v_b_conv_w, v_b_conv_b, v_b_ln_g, v_b_ln_b, v_ab_w_out, v_c_w_in, v_c_conv_w, v_c_w_out, v_f_w_up, v_f_conv_w, v_f_w_down):
    given = dict(locals())
    w = {n: given[n] for n in ALL_WEIGHTS}
    m = {n: given["m_" + n] for n in ALL_WEIGHTS}
    v = {n: given["v_" + n] for n in ALL_WEIGHTS}
    T = x.shape[1]
    loss, dx, grad, delta, new_m, new_v = _step(x.reshape(T, D_MODEL), loss_target.reshape(T, D_MODEL), w, m, v)
    out = [loss[0, 0], dx.reshape(x.shape)]
    for d in (grad, delta, new_m, new_v):
        out += [d[n] for n in ALL_WEIGHTS]
    return tuple(out)
```

```python
import functools
import math

import jax
import jax.numpy as jnp
from jax import lax
from jax.experimental import pallas as pl
from jax.experimental.pallas import tpu as pltpu

F32 = jnp.float32
BF16 = jnp.bfloat16

EPS = 1e-6
D_MODEL = 1024
CHUNK = 128
HEAD_DIM = 128
A_HEADS = 4
D_A = 512
D_B = 512
B_CONV = 31
C_CONV = 3
D_FF = 2816
F_CONV = 3
N_CHIPS = 4

ADAM_LR = 0.001
ADAM_B1 = 0.9
ADAM_B2 = 0.999
ADAM_EPS = 1e-08
ADAM_WD = 0.01
ADAM_STEP = 10

SUBLANES = 8
LANES = 128
HALO_SHORT = 16
HALO_LONG = 32
VMEM_BYTES_MAX = 60000 * 1024

INV_SQRT2 = 1.0 / math.sqrt(2.0)
INV_SQRT_2PI = 1.0 / math.sqrt(2.0 * math.pi)

MESH = pl.DeviceIdType.MESH


def _cparams(*sem):
    return pltpu.CompilerParams(dimension_semantics=sem, vmem_limit_bytes=VMEM_BYTES_MAX)


def _pick(total, pref):
    for c in (2048, 1024, 512, 256, 128):
        if c <= pref and total % c == 0:
            return c
    raise ValueError(f"no tile for {total}")


def _sigmoid(x):
    return jax.nn.sigmoid(x)


def _silu(x):
    return x * _sigmoid(x)


def _dsilu(x):
    s = _sigmoid(x)
    return s * (1.0 + x * (1.0 - s))


def _gelu(x):
    return 0.5 * x * (1.0 + lax.erf(x * INV_SQRT2))


def _dgelu(x):
    return 0.5 * (1.0 + lax.erf(x * INV_SQRT2)) + x * jnp.exp(-0.5 * x * x) * INV_SQRT_2PI


def _ln_stats(x):
    mu = jnp.mean(x, axis=-1, keepdims=True)
    xc = x - mu
    var = jnp.mean(xc * xc, axis=-1, keepdims=True)
    r = lax.rsqrt(var + EPS)
    return xc * r, r


def _ln_bwd(dy, xh, r, g):
    dxh = dy * g
    m1 = jnp.mean(dxh, axis=-1, keepdims=True)
    m2 = jnp.mean(dxh * xh, axis=-1, keepdims=True)
    return r * (dxh - m1 - xh * m2)


def _rowsum(x):
    return jnp.sum(x, axis=0, keepdims=True)


HBM_REF = pl.BlockSpec(memory_space=pltpu.HBM)


def _place():
    x, y, c = lax.axis_index("x"), lax.axis_index("y"), lax.axis_index("c")
    peers = [(1 - x, y), (x, 1 - y), (1 - x, 1 - y)]
    return x, y, c, 2 * x + y, (x, y, 1 - c), peers


def _half(rows, which):
    return pl.ds(which * (rows // 2), rows // 2)


def _remote(src, dst, send_sem, recv_sem, device):
    return pltpu.make_async_remote_copy(src_ref=src, dst_ref=dst, send_sem=send_sem, recv_sem=recv_sem,
                                        device_id=device, device_id_type=MESH)


class _Job:
    def __init__(self, reads, writes, ncopies, copies):
        self.reads, self.writes, self.ncopies, self.copies = reads, writes, ncopies, copies


def _share(rows, which, part, parts):
    nr = rows // 2 // parts
    return pl.ds(which * (rows // 2) + part * nr, nr)


def _slot(ref, chip, rows, cols):
    if ref.shape[1] == N_CHIPS:
        return ref.at[0, chip, rows]
    return ref.at[0, chip // 2, rows, pl.ds(pl.multiple_of((chip % 2) * cols, LANES), cols)]


def _job_gather_ici(name, rows, cols, part, parts):
    def copies(src, dst, sem):
        x, y, c, k, sib, peers = _place()
        mine_rows = _share(rows, c, part, parts)
        out = []
        for j, (px, py) in enumerate(peers):
            mine = _slot(src[name], k, mine_rows, cols)
            out.append((_remote(mine, _slot(dst[name], k, mine_rows, cols), sem(j, 0), sem(j, 1), (px, py, c)),
                        _remote(mine, _slot(dst[name], 2 * px + py, mine_rows, cols), sem(j, 0), sem(j, 1),
                                (px, py, c))))
        return out
    return _Job([], [name], 3, copies)


def _job_gather_d2d(name, rows, cols, part, parts):
    def copies(src, dst, sem):
        x, y, c, k, sib, peers = _place()
        out = []
        for j, (px, py) in enumerate(peers):
            mine_rows, their_rows = _share(rows, c, part, parts), _share(rows, 1 - c, part, parts)
            landed = _slot(src[name], 2 * px + py, mine_rows, cols)
            out.append((_remote(landed, _slot(dst[name], 2 * px + py, mine_rows, cols), sem(j, 0), sem(j, 1), sib),
                        _remote(landed, _slot(dst[name], 2 * px + py, their_rows, cols), sem(j, 0), sem(j, 1), sib)))
        return out
    return _Job([], [name], 3, copies)


def _job_chip_gather(sname, dname):
    def copies(src, dst, sem):
        x, y, c, k, sib, peers = _place()
        return [(_remote(src[sname], dst[dname].at[k], sem(j, 0), sem(j, 1), (px, py, c)),
                 _remote(src[sname], dst[dname].at[2 * px + py], sem(j, 0), sem(j, 1), (px, py, c)))
                for j, (px, py) in enumerate(peers)]
    return _Job([sname], [dname], 3, copies)


def _job_pair_exchange(gname, tname, rows, part, parts):
    nr = rows // 2 // parts

    def copies(src, dst, sem):
        x, y, c, k, sib, peers = _place()
        cp = _remote(src[gname].at[:, _share(rows, 1 - c, part, parts), :],
                     dst[tname].at[:, pl.ds(part * nr, nr), :], sem(0, 0), sem(0, 1), sib)
        return [(cp, cp)]
    return _Job([gname], [tname], 1, copies)


def _job_chip_exchange(pname, lname, r0, nr):
    def copies(src, dst, sem):
        x, y, c, k, sib, peers = _place()
        out = []
        for j, (px, py) in enumerate(peers):
            cp = _remote(src[pname].at[2 * px + py, pl.ds(r0, nr)], dst[lname].at[j, pl.ds(r0, nr)],
                         sem(j, 0), sem(j, 1), (px, py, c))
            out.append((cp, cp))
        return out
    return _Job([pname], [lname], 3, copies)


def _job_pair_share(name, layer, rows):
    def copies(src, dst, sem):
        x, y, c, k, sib, peers = _place()
        mine = src[name].at[layer, _half(rows, c)]
        return [(_remote(mine, dst[name].at[layer, _half(rows, c)], sem(0, 0), sem(0, 1), sib),
                 _remote(mine, dst[name].at[layer, _half(rows, 1 - c)], sem(0, 0), sem(0, 1), sib))]
    return _Job([], [name], 1, copies)


class _Comm:
    def __init__(self, plan, jobs, late=0):
        self.plan, self.jobs, self.late = plan, jobs, late
        self.writes, self.reads = [], []
        for job in jobs:
            for n in job.writes:
                if n not in self.writes:
                    self.writes.append(n)
        for job in jobs:
            for n in job.reads:
                if n not in self.writes and n not in self.reads:
                    self.reads.append(n)
        self.ncopies = sum(job.ncopies for job in jobs)

    def descriptors(self, src, dst, sems, base, stage):
        out = []
        first_late = len(self.jobs) - self.late
        for idx, job in enumerate(self.jobs):
            if stage is None or stage == int(idx >= first_late):
                sem = lambda j, which, base=base: sems.at[base + j, which]
                out += job.copies(src, dst, sem)
            base += job.ncopies
        return out

    def start(self, src, dst, sems, base=0, stage=None):
        for first, _ in self.descriptors(src, dst, sems, base, stage):
            first.start()

    def finish(self, src, dst, sems, base=0, stage=None):
        for _, landed in self.descriptors(src, dst, sems, base, stage):
            landed.wait()


def _comm_operands(comm):
    bufs = comm.plan.bufs
    shapes = [jax.ShapeDtypeStruct(bufs[n].shape, bufs[n].dtype) for n in comm.writes]
    return [bufs[n] for n in comm.reads] + [bufs[n] for n in comm.writes], shapes


def _pallas(comm, body, *, name, grid, in_specs, out_specs, out_shape, compiler_params, scratch_shapes=(),
            aliases=None):
    aliases = dict(aliases or {})
    if comm is None:
        return pl.pallas_call(body, name=name, grid=grid, in_specs=in_specs, out_specs=out_specs,
                              out_shape=out_shape, scratch_shapes=list(scratch_shapes),
                              input_output_aliases=aliases, compiler_params=compiler_params)
    single = not isinstance(out_shape, (list, tuple))
    base_specs = [out_specs] if single else list(out_specs)
    base_shape = [out_shape] if single else list(out_shape)
    nb, nr, nw, nbo, nsc = len(in_specs), len(comm.reads), len(comm.writes), len(base_specs), len(scratch_shapes)

    def wrapped(*refs):
        base_in, rd, wr_in = refs[:nb], refs[nb:nb + nr], refs[nb + nr:nb + nr + nw]
        o0 = nb + nr + nw
        base_out, wr_out = refs[o0:o0 + nbo], refs[o0 + nbo:o0 + nbo + nw]
        scratch, sems = refs[o0 + nbo + nw:o0 + nbo + nw + nsc], refs[-1]
        src = dict(zip(comm.reads, rd))
        src.update(zip(comm.writes, wr_in))
        dst = dict(zip(comm.writes, wr_out))
        first = functools.reduce(jnp.logical_and, [pl.program_id(a) == 0 for a in range(len(grid))])
        last = functools.reduce(jnp.logical_and,
                                [pl.program_id(a) == pl.num_programs(a) - 1 for a in range(len(grid))])

        if not comm.late:
            @pl.when(first)
            def _():
                comm.start(src, dst, sems)
            body(*base_in, *base_out, *scratch)

            @pl.when(last)
            def _():
                comm.finish(src, dst, sems)
            return
        step = functools.reduce(lambda acc, a: acc * grid[a] + pl.program_id(a), range(len(grid)), 0)

        @pl.when(first)
        def _():
            comm.start(src, dst, sems, stage=0)

        @pl.when(step == math.prod(grid) - LATE_STEPS)
        def _():
            comm.finish(src, dst, sems, stage=0)
            comm.start(src, dst, sems, stage=1)
        body(*base_in, *base_out, *scratch)

        @pl.when(last)
        def _():
            comm.finish(src, dst, sems, stage=1)

    operands, shapes = _comm_operands(comm)
    call = pl.pallas_call(
        wrapped, name=name, grid=grid, in_specs=list(in_specs) + [HBM_REF] * (nr + nw),
        out_specs=base_specs + [HBM_REF] * nw, out_shape=base_shape + shapes,
        input_output_aliases={**aliases, **{nb + nr + q: nbo + q for q in range(nw)}},
        scratch_shapes=list(scratch_shapes) + [pltpu.SemaphoreType.DMA((comm.ncopies, 2))],
        compiler_params=compiler_params)

    def run(*args):
        outs = call(*args, *operands)
        for q, n in enumerate(comm.writes):
            comm.plan.bufs[n] = outs[nbo + q]
        return outs[0] if single else list(outs[:nbo])

    return run


def _comm_only(plan, phases, *, name):
    comms = [_Comm(plan, jobs) for jobs in phases]
    both = _Comm(plan, [job for jobs in phases for job in jobs])
    nr, nw = len(both.reads), len(both.writes)

    def body(*refs):
        rd, wr_in, wr_out, sems = refs[:nr], refs[nr:nr + nw], refs[nr + nw:nr + 2 * nw], refs[-1]
        src = dict(zip(both.reads, rd))
        src.update(zip(both.writes, wr_in))
        dst = dict(zip(both.writes, wr_out))
        base = 0
        for comm in comms:
            comm.start(src, dst, sems, base)
            comm.finish(src, dst, sems, base)
            base += comm.ncopies

    operands, shapes = _comm_operands(both)
    outs = pl.pallas_call(
        body, name=name, in_specs=[HBM_REF] * (nr + nw), out_specs=[HBM_REF] * nw, out_shape=shapes,
        input_output_aliases={nr + q: q for q in range(nw)},
        scratch_shapes=[pltpu.SemaphoreType.DMA((both.ncopies, 2))],
    )(*operands)
    for q, n in enumerate(both.writes):
        plan.bufs[n] = outs[q]


def _mm_nn(a, w, *, layer, tm, tn, residual=None, norm=None, out_dtype=F32, name, comm=None):
    T, K = a.shape
    if w.ndim == 4:
        _, S, _, n4 = w.shape
        N = S * n4
        bps = n4 // tn
        w_spec = pl.BlockSpec((None, None, K, tn), lambda j, i: (layer, j // bps, 0, j % bps))
    else:
        N = w.shape[2]
        w_spec = pl.BlockSpec((None, K, tn), lambda j, i: (layer, 0, j))
    in_specs = [pl.BlockSpec((tm, K), lambda j, i: (i, 0)), w_spec]
    args = [a, w]
    if residual is not None:
        in_specs.append(pl.BlockSpec((tm, tn), lambda j, i: (i, j)))
        args.append(residual)
    out_specs = pl.BlockSpec((tm, tn), lambda j, i: (i, j))
    out_shape = jax.ShapeDtypeStruct((T, N), out_dtype)
    if norm is not None:
        assert tn == N
        g, norm_layer = norm
        in_specs.append(pl.BlockSpec((None, 1, N), lambda j, i: (norm_layer, 0, 0)))
        args.append(g)
        out_specs = [out_specs, pl.BlockSpec((tm, tn), lambda j, i: (i, j))]
        out_shape = [out_shape, jax.ShapeDtypeStruct((T, N), BF16)]

    def body(*refs):
        a_ref, w_ref = refs[0], refs[1]
        acc = jnp.dot(a_ref[...].astype(BF16), w_ref[...], preferred_element_type=F32)
        if residual is not None:
            acc = refs[2][...] + acc
        if norm is None:
            refs[-1][...] = acc.astype(out_dtype)
        else:
            refs[-2][...] = acc.astype(out_dtype)
            r = lax.rsqrt(jnp.mean(acc * acc, axis=-1, keepdims=True) + EPS)
            refs[-1][...] = (acc * r * refs[-3][...]).astype(BF16)

    return _pallas(
        comm, body, name=name, grid=(N // tn, T // tm), in_specs=in_specs,
        out_specs=out_specs, out_shape=out_shape,
        compiler_params=_cparams("parallel", "parallel"),
    )(*args)


def _norm_mm_nn(x, g, w, *, g_layer, tm, tn, name, comm=None):
    T, K = x.shape
    _, S, _, n4 = w.shape
    bps = n4 // tn

    def body(x_ref, g_ref, w_ref, h_ref, o_ref):
        @pl.when(pl.program_id(1) == 0)
        def _():
            xf = x_ref[...]
            r = lax.rsqrt(jnp.mean(xf * xf, axis=-1, keepdims=True) + EPS)
            h_ref[...] = (xf * r * g_ref[...]).astype(BF16)
        o_ref[...] = jnp.dot(h_ref[...], w_ref[...], preferred_element_type=F32).astype(BF16)

    return _pallas(
        comm, body, name=name, grid=(T // tm, S * bps),
        in_specs=[pl.BlockSpec((tm, K), lambda i, j: (i, 0)),
                  pl.BlockSpec((None, 1, K), lambda i, j: (g_layer, 0, 0)),
                  pl.BlockSpec((None, None, K, tn), lambda i, j: (0, j // bps, 0, j % bps))],
        out_specs=[pl.BlockSpec((tm, K), lambda i, j: (i, 0)), pl.BlockSpec((tm, tn), lambda i, j: (i, j))],
        out_shape=[jax.ShapeDtypeStruct((T, K), BF16), jax.ShapeDtypeStruct((T, S * n4), BF16)],
        compiler_params=_cparams("parallel", "arbitrary"),
    )(x, g, w)


def _mm_nt(dy, w, *, layer, tm, tn, name, out_dtype=F32, comm=None):
    T = dy.shape[0]
    nt_dims = (((1,), (1,)), ((), ()))
    _, R, N = w.shape

    def body2(dy_ref, w_ref, o_ref):
        o_ref[...] = lax.dot_general(dy_ref[...].astype(BF16), w_ref[...], nt_dims,
                                     preferred_element_type=F32).astype(out_dtype)

    return _pallas(
        comm, body2, name=name, grid=(R // tn, T // tm),
        in_specs=[pl.BlockSpec((tm, N), lambda j, i: (i, 0)),
                  pl.BlockSpec((None, tn, N), lambda j, i: (layer, j, 0))],
        out_specs=pl.BlockSpec((tm, tn), lambda j, i: (i, j)),
        out_shape=jax.ShapeDtypeStruct((T, R), out_dtype),
        compiler_params=_cparams("parallel", "parallel"),
    )(dy, w)


def _mm_tn(a, dy, *, shards, tk, tn, tt, name, comm=None):
    T, K = a.shape
    N = dy.shape[1]
    tn_dims = (((0,), (0,)), ((), ()))
    n4 = N if shards is None else N // shards
    span = max(tn // n4, 1)

    def body(a_ref, dy_ref, o_ref):
        @pl.when(pl.program_id(2) == 0)
        def _():
            o_ref[...] = jnp.zeros_like(o_ref)
        r = lax.dot_general(a_ref[...].astype(BF16), dy_ref[...].astype(BF16), tn_dims,
                            preferred_element_type=F32)
        if span == 1:
            o_ref[...] += r
        else:
            for q in range(span):
                o_ref[q] += r[:, q * n4:(q + 1) * n4]

    if shards is None:
        out_spec = pl.BlockSpec((tk, tn), lambda k, n, t: (k, n))
        out_shape = jax.ShapeDtypeStruct((K, N), F32)
    elif span > 1:
        out_spec = pl.BlockSpec((span, tk, n4), lambda k, n, t: (n, k, 0))
        out_shape = jax.ShapeDtypeStruct((shards, K, n4), F32)
    else:
        bps = n4 // tn
        out_spec = pl.BlockSpec((None, tk, tn), lambda k, n, t: (n // bps, k, n % bps))
        out_shape = jax.ShapeDtypeStruct((shards, K, n4), F32)
    return _pallas(
        comm, body, name=name, grid=(K // tk, N // tn, T // tt),
        in_specs=[pl.BlockSpec((tt, tk), lambda k, n, t: (t, k)),
                  pl.BlockSpec((tt, tn), lambda k, n, t: (t, n))],
        out_specs=out_spec, out_shape=out_shape,
        compiler_params=_cparams("parallel", "parallel", "arbitrary"),
    )(a, dy)


def _rmsnorm_bwd_math(xf, g, dh, dres):
    r = lax.rsqrt(jnp.mean(xf * xf, axis=-1, keepdims=True) + EPS)
    xh = xf * r
    dxh = dh * g
    dx = dres + r * (dxh - xh * jnp.mean(dxh * xh, axis=-1, keepdims=True))
    return dx, _rowsum(dh * xh)


def _mm_nt_norm(dy, w, x, g, dres, *, g_layer, tm, name, comm=None):
    T = dy.shape[0]
    _, S, K, n4 = w.shape
    nt_dims = (((1,), (1,)), ((), ()))

    def body(dy_ref, w_ref, x_ref, g_ref, dres_ref, dx_ref, dg_ref):
        @pl.when(pl.program_id(0) == 0)
        def _():
            dg_ref[...] = jnp.zeros_like(dg_ref)
        dh = None
        for s in range(S):
            part = lax.dot_general(dy_ref[:, s * n4:(s + 1) * n4].astype(BF16), w_ref[s], nt_dims,
                                   preferred_element_type=F32)
            dh = part if dh is None else dh + part
        dx, dg = _rmsnorm_bwd_math(x_ref[...], g_ref[...], dh, dres_ref[...])
        dx_ref[...] = dx
        dg_ref[...] += dg

    row = lambda i: (i, 0)
    return _pallas(
        comm, body, name=name, grid=(T // tm,),
        in_specs=[pl.BlockSpec((tm, S * n4), row),
                  pl.BlockSpec((None, S, K, n4), lambda i: (0, 0, 0, 0)),
                  pl.BlockSpec((tm, K), row),
                  pl.BlockSpec((None, 1, K), lambda i: (g_layer, 0, 0)),
                  pl.BlockSpec((tm, K), row)],
        out_specs=[pl.BlockSpec((tm, K), row), pl.BlockSpec((1, K), lambda i: (0, 0))],
        out_shape=[jax.ShapeDtypeStruct((T, K), F32), jax.ShapeDtypeStruct((1, K), F32)],
        compiler_params=_cparams("arbitrary"),
    )(dy, w, x, g, dres)


def _mm_nn_loss(a, w, residual, tgt, g, *, tm, name, comm=None):
    T, K = a.shape
    D = w.shape[2]

    def body(a_ref, w_ref, res_ref, t_ref, g_ref, loss_ref, dx_ref, dg_ref):
        @pl.when(pl.program_id(0) == 0)
        def _():
            dg_ref[...] = jnp.zeros_like(dg_ref)
            loss_ref[...] = jnp.zeros_like(loss_ref)
        xf = res_ref[...] + jnp.dot(a_ref[...], w_ref[...], preferred_element_type=F32)
        gg = g_ref[...]
        r = lax.rsqrt(jnp.mean(xf * xf, axis=-1, keepdims=True) + EPS)
        xh = xf * r
        err = xh * gg - t_ref[...]
        row = jnp.mean(err * err, axis=-1, keepdims=True)
        loss_ref[...] += 0.5 * jnp.sum(row, axis=0, keepdims=True)
        dy = err * (1.0 / D)
        dg_ref[...] += _rowsum(dy * xh)
        dxh = dy * gg
        dx_ref[...] = r * (dxh - xh * jnp.mean(dxh * xh, axis=-1, keepdims=True))

    row_spec = pl.BlockSpec((tm, D), lambda i: (i, 0))
    return _pallas(
        comm, body, name=name, grid=(T // tm,),
        in_specs=[pl.BlockSpec((tm, K), lambda i: (i, 0)), pl.BlockSpec((None, K, D), lambda i: (0, 0, 0)),
                  row_spec, row_spec, pl.BlockSpec((1, D), lambda i: (0, 0))],
        out_specs=[pl.BlockSpec((1, 1), lambda i: (0, 0)), row_spec, pl.BlockSpec((1, D), lambda i: (0, 0))],
        out_shape=[jax.ShapeDtypeStruct((1, 1), F32), jax.ShapeDtypeStruct((T, D), F32),
                   jax.ShapeDtypeStruct((1, D), F32)],
        compiler_params=_cparams("arbitrary"),
    )(a, w, residual, tgt, g)


CONV_ROWS = 64
CONV_COLS = 256


def _halo_prev_index(tm, halo):
    per = tm // halo
    return lambda i: jnp.maximum(i * per - 1, 0)


def _halo_next_index(tm, halo, total):
    per = tm // halo
    last = total // halo - 1
    return lambda i: jnp.minimum((i + 1) * per, last)


def _causal_mask():
    t = lax.broadcasted_iota(jnp.int32, (CHUNK, CHUNK), 0)
    s = lax.broadcasted_iota(jnp.int32, (CHUNK, CHUNK), 1)
    return s <= t


def _mixer_ab_fwd(z, a_ln_g, a_ln_b, w_s, b_s, conv_w, conv_b, b_ln_g, b_ln_b, *, tm, name, comm=None):
    T = z.shape[0]
    nchunk = tm // CHUNK
    halo = HALO_LONG

    def body(za_ref, zb_ref, zh_ref, alg_ref, alb_ref, ws_ref, bs_ref, cw_ref, cbias_ref,
             blg_ref, blb_ref, y_ref, cb_ref, ext_ref):
        i = pl.program_id(0)
        gu = _gelu(za_ref[:, :D_A].astype(F32))
        gv = _gelu(za_ref[:, D_A:].astype(F32))
        xh, _ = _ln_stats(gv)
        lv = (xh * alg_ref[...] + alb_ref[...]).astype(BF16)
        mask = _causal_mask()
        for h in range(A_HEADS):
            wm = jnp.where(mask, ws_ref[h], 0.0).astype(BF16)
            cols = slice(h * HEAD_DIM, (h + 1) * HEAD_DIM)
            for c in range(nchunk):
                rows = slice(c * CHUNK, (c + 1) * CHUNK)
                mixed = jnp.dot(wm, lv[rows, cols], preferred_element_type=F32) + bs_ref[h]
                y_ref[rows, cols] = (gu[rows, cols] * mixed).astype(BF16)
        ext_ref[halo:halo + tm, :] = zb_ref[:, :D_B].astype(F32) * _sigmoid(zb_ref[:, D_B:].astype(F32))
        prev = zh_ref[:, :D_B].astype(F32) * _sigmoid(zh_ref[:, D_B:].astype(F32))
        ext_ref[0:halo, :] = jnp.where(i > 0, prev, 0.0)
        for rb in range(tm // CONV_ROWS):
            for cb in range(D_B // CONV_COLS):
                cs = slice(cb * CONV_COLS, (cb + 1) * CONV_COLS)
                window = ext_ref[rb * CONV_ROWS:rb * CONV_ROWS + CONV_ROWS + halo, cs]
                acc = jnp.zeros((CONV_ROWS, CONV_COLS), F32)
                for k in range(B_CONV):
                    shifted = _rows_after(window, halo - (B_CONV - 1) + k)[:CONV_ROWS]
                    acc = acc + cw_ref[k:k + 1, cs] * shifted
                cb_ref[rb * CONV_ROWS:(rb + 1) * CONV_ROWS, cs] = acc + cbias_ref[:, cs]
        xhb, _ = _ln_stats(cb_ref[...])
        y_ref[:, D_A:] = _silu(xhb * blg_ref[...] + blb_ref[...]).astype(BF16)

    row = lambda i: (i, 0)
    par = lambda i: (0, 0)
    return _pallas(
        comm, body, name=name, grid=(T // tm,),
        in_specs=[pl.BlockSpec((tm, 2 * D_A), lambda i: (i, 0)),
                  pl.BlockSpec((tm, 2 * D_B), lambda i: (i, 1)),
                  pl.BlockSpec((halo, 2 * D_B), lambda i: (_halo_prev_index(tm, halo)(i), 1)),
                  pl.BlockSpec((1, D_A), par), pl.BlockSpec((1, D_A), par),
                  pl.BlockSpec((A_HEADS, CHUNK, CHUNK), lambda i: (0, 0, 0)),
                  pl.BlockSpec((A_HEADS, CHUNK, 1), lambda i: (0, 0, 0)),
                  pl.BlockSpec((B_CONV, D_B), par), pl.BlockSpec((1, D_B), par),
                  pl.BlockSpec((1, D_B), par), pl.BlockSpec((1, D_B), par)],
        out_specs=[pl.BlockSpec((tm, D_A + D_B), row), pl.BlockSpec((tm, D_B), row)],
        out_shape=[jax.ShapeDtypeStruct((T, D_A + D_B), BF16), jax.ShapeDtypeStruct((T, D_B), F32)],
        scratch_shapes=[pltpu.VMEM((halo + tm, D_B), F32)],
        compiler_params=_cparams("parallel"),
    )(z, z, z, a_ln_g, a_ln_b, w_s, b_s, conv_w, conv_b, b_ln_g, b_ln_b)


def _mixer_ab_bwd_pre(z, cb, dy, a_ln_g, a_ln_b, w_s, b_s, b_ln_g, b_ln_b, *, tm, name, comm=None):
    T = z.shape[0]
    nchunk = tm // CHUNK
    tn_dims = (((0,), (0,)), ((), ()))
    nt_dims = (((1,), (1,)), ((), ()))

    def body(za_ref, cb_ref, dy_ref, alg_ref, alb_ref, ws_ref, bs_ref, blg_ref, blb_ref,
             dza_ref, dcb_ref, dalg_ref, dalb_ref, dws_ref, dbs_ref, dblg_ref, dblb_ref,
             dlv_ref):
        @pl.when(pl.program_id(0) == 0)
        def _():
            for ref in (dalg_ref, dalb_ref, dws_ref, dbs_ref, dblg_ref, dblb_ref):
                ref[...] = jnp.zeros_like(ref)
        ua = za_ref[:, :D_A].astype(F32)
        va = za_ref[:, D_A:].astype(F32)
        gu = _gelu(ua)
        gv = _gelu(va)
        xh, r = _ln_stats(gv)
        alg = alg_ref[...]
        lv = (xh * alg + alb_ref[...]).astype(BF16)
        dya = dy_ref[:, :D_A].astype(F32)
        mask = _causal_mask()
        for h in range(A_HEADS):
            wm = jnp.where(mask, ws_ref[h], 0.0).astype(BF16)
            cols = slice(h * HEAD_DIM, (h + 1) * HEAD_DIM)
            dwm = jnp.zeros((CHUNK, CHUNK), F32)
            dbs = jnp.zeros((CHUNK, 1), F32)
            for c in range(nchunk):
                rows = slice(c * CHUNK, (c + 1) * CHUNK)
                lvb = lv[rows, cols]
                mixed = jnp.dot(wm, lvb, preferred_element_type=F32) + bs_ref[h]
                dyb = dya[rows, cols]
                dza_ref[rows, cols] = (dyb * mixed * _dgelu(ua[rows, cols])).astype(BF16)
                dmixed = dyb * gu[rows, cols]
                dmb = dmixed.astype(BF16)
                dlv_ref[rows, cols] = lax.dot_general(wm, dmb, tn_dims, preferred_element_type=F32)
                dwm = dwm + lax.dot_general(dmb, lvb, nt_dims, preferred_element_type=F32)
                dbs = dbs + jnp.sum(dmixed, axis=1, keepdims=True)
            dws_ref[h] += jnp.where(mask, dwm, 0.0)
            dbs_ref[h] += dbs
        dlv = dlv_ref[...]
        dalg_ref[...] += _rowsum(dlv * xh)
        dalb_ref[...] += _rowsum(dlv)
        dgv = _ln_bwd(dlv, xh, r, alg)
        dza_ref[:, D_A:] = (dgv * _dgelu(va)).astype(BF16)
        xhb, rb = _ln_stats(cb_ref[...])
        blg = blg_ref[...]
        lb = xhb * blg + blb_ref[...]
        dlb = dy_ref[:, D_A:].astype(F32) * _dsilu(lb)
        dblg_ref[...] += _rowsum(dlb * xhb)
        dblb_ref[...] += _rowsum(dlb)
        dcb_ref[...] = _ln_bwd(dlb, xhb, rb, blg)

    row = lambda i: (i, 0)
    par = lambda i: (0, 0)
    par3 = lambda i: (0, 0, 0)
    return _pallas(
        comm, body, name=name, grid=(T // tm,),
        in_specs=[pl.BlockSpec((tm, 2 * D_A), row), pl.BlockSpec((tm, D_B), row),
                  pl.BlockSpec((tm, D_A + D_B), row),
                  pl.BlockSpec((1, D_A), par), pl.BlockSpec((1, D_A), par),
                  pl.BlockSpec((A_HEADS, CHUNK, CHUNK), par3),
                  pl.BlockSpec((A_HEADS, CHUNK, 1), par3),
                  pl.BlockSpec((1, D_B), par), pl.BlockSpec((1, D_B), par)],
        out_specs=[pl.BlockSpec((tm, 2 * D_A), row), pl.BlockSpec((tm, D_B), row),
                   pl.BlockSpec((1, D_A), par), pl.BlockSpec((1, D_A), par),
                   pl.BlockSpec((A_HEADS, CHUNK, CHUNK), par3),
                   pl.BlockSpec((A_HEADS, CHUNK, 1), par3),
                   pl.BlockSpec((1, D_B), par), pl.BlockSpec((1, D_B), par)],
        out_shape=[jax.ShapeDtypeStruct((T, 2 * D_A + 2 * D_B), BF16), jax.ShapeDtypeStruct((T, D_B), F32),
                   jax.ShapeDtypeStruct((1, D_A), F32), jax.ShapeDtypeStruct((1, D_A), F32),
                   jax.ShapeDtypeStruct((A_HEADS, CHUNK, CHUNK), F32),
                   jax.ShapeDtypeStruct((A_HEADS, CHUNK, 1), F32),
                   jax.ShapeDtypeStruct((1, D_B), F32), jax.ShapeDtypeStruct((1, D_B), F32)],
        scratch_shapes=[pltpu.VMEM((tm, D_A), F32)],
        compiler_params=_cparams("arbitrary"),
    )(z, cb, dy, a_ln_g, a_ln_b, w_s, b_s, b_ln_g, b_ln_b)


def _mixer_b_conv_bwd(z, dcb, conv_w, dz, *, tm, name, comm=None):
    T = z.shape[0]
    halo = HALO_LONG

    def body(zb_ref, dcb_ref, dcn_ref, cw_ref, dz_in_ref, dzb_ref, dcw_ref, dbias_ref, dext_ref):
        i = pl.program_id(0)
        last = pl.num_programs(0) - 1

        @pl.when(i == 0)
        def _():
            dcw_ref[...] = jnp.zeros_like(dcw_ref)
            dbias_ref[...] = jnp.zeros_like(dbias_ref)
        dcb = dcb_ref[...]
        dext_ref[0:tm, :] = dcb
        dext_ref[tm:tm + halo, :] = jnp.where(i < last, dcn_ref[...], 0.0)
        dbias_ref[...] += _rowsum(dcb)
        for rb in range(tm // CONV_ROWS):
            for cb in range(D_B // CONV_COLS):
                cs = slice(cb * CONV_COLS, (cb + 1) * CONV_COLS)
                gcs = slice(D_B + cb * CONV_COLS, D_B + (cb + 1) * CONV_COLS)
                rs = slice(rb * CONV_ROWS, (rb + 1) * CONV_ROWS)
                xbb = zb_ref[rs, cs].astype(F32)
                sgb = _sigmoid(zb_ref[rs, gcs].astype(F32))
                yb0 = xbb * sgb
                window = dext_ref[rb * CONV_ROWS:rb * CONV_ROWS + CONV_ROWS + halo, cs]
                acc = jnp.zeros((CONV_ROWS, CONV_COLS), F32)
                for k in range(B_CONV):
                    shifted = _rows_after(window, (B_CONV - 1) - k)[:CONV_ROWS]
                    acc = acc + cw_ref[k:k + 1, cs] * shifted
                    dcw_ref[k:k + 1, cs] += _rowsum(shifted * yb0)
                dzb_ref[rs, cs] = (acc * sgb).astype(BF16)
                dzb_ref[rs, gcs] = (acc * xbb * sgb * (1.0 - sgb)).astype(BF16)

    row = lambda i: (i, 0)
    par = lambda i: (0, 0)
    return _pallas(
        comm, body, name=name, grid=(T // tm,),
        in_specs=[pl.BlockSpec((tm, 2 * D_B), lambda i: (i, 1)),
                  pl.BlockSpec((tm, D_B), row),
                  pl.BlockSpec((halo, D_B), lambda i: (_halo_next_index(tm, halo, T)(i), 0)),
                  pl.BlockSpec((B_CONV, D_B), par), pl.BlockSpec(memory_space=pl.ANY)],
        out_specs=[pl.BlockSpec((tm, 2 * D_B), lambda i: (i, 1)), pl.BlockSpec((B_CONV, D_B), par),
                   pl.BlockSpec((1, D_B), par)],
        out_shape=[jax.ShapeDtypeStruct(dz.shape, BF16), jax.ShapeDtypeStruct((B_CONV, D_B), F32),
                   jax.ShapeDtypeStruct((1, D_B), F32)],
        scratch_shapes=[pltpu.VMEM((tm + halo, D_B), F32)], aliases={4: 0},
        compiler_params=_cparams("arbitrary"),
    )(z, dcb, dcb, conv_w, dz)


def _rows_before(x, a):
    return x if a == 0 else pltpu.roll(x, a, axis=0)


def _rows_after(x, a):
    return x if a == 0 else pltpu.roll(x, x.shape[0] - a, axis=0)


def _conv3(w_ref, x, halo, cs):
    acc = w_ref[2:3, cs] * x[halo:]
    acc = acc + w_ref[1:2, cs] * _rows_before(x, 1)[halo:]
    return acc + w_ref[0:1, cs] * _rows_before(x, 2)[halo:]


def _mixer_c_fwd(z, conv_w, *, tm, name, comm=None):
    T = z.shape[0]
    D = D_MODEL
    halo = HALO_SHORT
    W = CONV_COLS

    def body(bg_ref, cg_ref, xv_ref, cgh_ref, xvh_ref, w_ref, r_ref):
        i = pl.program_id(0)
        for cb in range(D // W):
            cs = slice(cb * W, (cb + 1) * W)
            prev = jnp.where(i > 0, cgh_ref[:, cs].astype(F32) * xvh_ref[:, cs].astype(F32), 0.0)
            p = jnp.concatenate([prev, cg_ref[:, cs].astype(F32) * xv_ref[:, cs].astype(F32)], axis=0)
            r_ref[:, cs] = (bg_ref[:, cs].astype(F32) * _conv3(w_ref, p, halo, cs)).astype(BF16)

    hp = _halo_prev_index(tm, halo)
    return _pallas(
        comm, body, name=name, grid=(T // tm,),
        in_specs=[pl.BlockSpec((tm, D), lambda i: (i, 0)), pl.BlockSpec((tm, D), lambda i: (i, 1)),
                  pl.BlockSpec((tm, D), lambda i: (i, 2)),
                  pl.BlockSpec((halo, D), lambda i: (hp(i), 1)),
                  pl.BlockSpec((halo, D), lambda i: (hp(i), 2)),
                  pl.BlockSpec((None, C_CONV, D), lambda i: (0, 0, 0))],
        out_specs=pl.BlockSpec((tm, D), lambda i: (i, 0)),
        out_shape=jax.ShapeDtypeStruct((T, D), BF16),
        compiler_params=_cparams("parallel"),
    )(z, z, z, z, z, conv_w)


def _mixer_c_bwd(z, dr, conv_w, *, tm, name, comm=None):
    T = z.shape[0]
    D = D_MODEL
    halo = HALO_SHORT
    W = CONV_COLS

    def body(bg_ref, cg_ref, xv_ref, cgh_ref, xvh_ref, bgn_ref, dr_ref, drn_ref, w_ref, dz_ref, dw_ref):
        i = pl.program_id(0)
        last = pl.num_programs(0) - 1

        @pl.when(i == 0)
        def _():
            dw_ref[...] = jnp.zeros_like(dw_ref)
        for cb in range(D // W):
            cs = slice(cb * W, (cb + 1) * W)
            cg = cg_ref[:, cs].astype(F32)
            xv = xv_ref[:, cs].astype(F32)
            dr = dr_ref[:, cs].astype(F32)
            p = cg * xv
            prev = jnp.where(i > 0, cgh_ref[:, cs].astype(F32) * xvh_ref[:, cs].astype(F32), 0.0)
            q = _conv3(w_ref, jnp.concatenate([prev, p], axis=0), halo, cs)
            dz_ref[:, cs] = (dr * q).astype(BF16)
            nxt = jnp.where(i < last, drn_ref[:, cs].astype(F32) * bgn_ref[:, cs].astype(F32), 0.0)
            dq = jnp.concatenate([dr * bg_ref[:, cs].astype(F32), nxt], axis=0)
            dp = None
            for k in range(C_CONV):
                shifted = _rows_after(dq, 2 - k)[:tm]
                term = w_ref[k:k + 1, cs] * shifted
                dp = term if dp is None else dp + term
                dw_ref[k:k + 1, cs] += _rowsum(shifted * p)
            dz_ref[:, D + cb * W:D + (cb + 1) * W] = (dp * xv).astype(BF16)
            dz_ref[:, 2 * D + cb * W:2 * D + (cb + 1) * W] = (dp * cg).astype(BF16)

    hp = _halo_prev_index(tm, halo)
    hn = _halo_next_index(tm, halo, T)
    return _pallas(
        comm, body, name=name, grid=(T // tm,),
        in_specs=[pl.BlockSpec((tm, D), lambda i: (i, 0)), pl.BlockSpec((tm, D), lambda i: (i, 1)),
                  pl.BlockSpec((tm, D), lambda i: (i, 2)),
                  pl.BlockSpec((halo, D), lambda i: (hp(i), 1)),
                  pl.BlockSpec((halo, D), lambda i: (hp(i), 2)),
                  pl.BlockSpec((halo, D), lambda i: (hn(i), 0)),
                  pl.BlockSpec((tm, D), lambda i: (i, 0)),
                  pl.BlockSpec((halo, D), lambda i: (hn(i), 0)),
                  pl.BlockSpec((None, C_CONV, D), lambda i: (0, 0, 0))],
        out_specs=[pl.BlockSpec((tm, 3 * D), lambda i: (i, 0)),
                   pl.BlockSpec((C_CONV, D), lambda i: (0, 0))],
        out_shape=[jax.ShapeDtypeStruct((T, 3 * D), BF16), jax.ShapeDtypeStruct((C_CONV, D), F32)],
        compiler_params=_cparams("arbitrary"),
    )(z, z, z, z, z, z, dr, dr, conv_w)


FFN_COLS = 128


def _ffn_act_fwd(up, conv_w, *, layer, tm, name, comm=None):
    T = up.shape[0]
    halo = HALO_SHORT
    W = FFN_COLS

    def body(up_ref, uph_ref, w_ref, a_ref, upc_ref):
        i = pl.program_id(0)

        def conv(cs):
            prev = jnp.where(i > 0, uph_ref[:, cs], jnp.zeros((halo, W), BF16))
            return _conv3(w_ref, jnp.concatenate([prev, up_ref[:, cs]], axis=0).astype(F32), halo, cs)

        for cb in range(D_FF // W):
            gs = slice(cb * W, (cb + 1) * W)
            vs = slice(D_FF + cb * W, D_FF + (cb + 1) * W)
            g = conv(gs)
            v = conv(vs)
            upc_ref[:, gs] = g.astype(BF16)
            upc_ref[:, vs] = v.astype(BF16)
            a_ref[:, gs] = (_silu(g) * v).astype(BF16)

    return _pallas(
        comm, body, name=name, grid=(T // tm,),
        in_specs=[pl.BlockSpec((tm, 2 * D_FF), lambda i: (i, 0)),
                  pl.BlockSpec((halo, 2 * D_FF), lambda i: (_halo_prev_index(tm, halo)(i), 0)),
                  pl.BlockSpec((None, F_CONV, 2 * D_FF), lambda i: (layer, 0, 0))],
        out_specs=[pl.BlockSpec((tm, D_FF), lambda i: (i, 0)),
                   pl.BlockSpec((tm, 2 * D_FF), lambda i: (i, 0))],
        out_shape=[jax.ShapeDtypeStruct((T, D_FF), BF16), jax.ShapeDtypeStruct((T, 2 * D_FF), BF16)],
        compiler_params=_cparams("parallel"),
    )(up, up, conv_w)


def _ffn_act_bwd(up, upc, da, conv_w, *, layer, tm, name, comm=None):
    T = up.shape[0]
    halo = HALO_SHORT
    W = FFN_COLS

    def body(up_ref, upc_ref, upcn_ref, da_ref, dan_ref, w_ref, dup_ref, dw_ref):
        i = pl.program_id(0)
        last = pl.num_programs(0) - 1

        @pl.when(i == 0)
        def _():
            dw_ref[...] = jnp.zeros_like(dw_ref)
        live = jnp.where(i < last, 1.0, 0.0)
        for cb in range(D_FF // W):
            gs = slice(cb * W, (cb + 1) * W)
            vs = slice(D_FF + cb * W, D_FF + (cb + 1) * W)
            g = jnp.concatenate([upc_ref[:, gs], upcn_ref[:, gs]], axis=0).astype(F32)
            v = jnp.concatenate([upc_ref[:, vs], upcn_ref[:, vs]], axis=0).astype(F32)
            da = jnp.concatenate([da_ref[:, gs].astype(F32), dan_ref[:, gs].astype(F32) * live], axis=0)
            s = _sigmoid(g)
            silu = g * s
            grads = (da * v * (s * (1.0 + g * (1.0 - s))), da * silu)
            for cs, d in zip((gs, vs), grads):
                u = up_ref[:, cs].astype(F32)
                acc = None
                for k in range(F_CONV):
                    shifted = _rows_after(d, 2 - k)[:tm]
                    term = w_ref[k:k + 1, cs] * shifted
                    acc = term if acc is None else acc + term
                    dw_ref[k:k + 1, cs] += _rowsum(shifted * u)
                dup_ref[:, cs] = acc.astype(BF16)

    hn = _halo_next_index(tm, halo, T)
    return _pallas(
        comm, body, name=name, grid=(T // tm,),
        in_specs=[pl.BlockSpec((tm, 2 * D_FF), lambda i: (i, 0)),
                  pl.BlockSpec((tm, 2 * D_FF), lambda i: (i, 0)),
                  pl.BlockSpec((halo, 2 * D_FF), lambda i: (hn(i), 0)),
                  pl.BlockSpec((tm, D_FF), lambda i: (i, 0)),
                  pl.BlockSpec((halo, D_FF), lambda i: (hn(i), 0)),
                  pl.BlockSpec((None, F_CONV, 2 * D_FF), lambda i: (layer, 0, 0))],
        out_specs=[pl.BlockSpec((tm, 2 * D_FF), lambda i: (i, 0)),
                   pl.BlockSpec((F_CONV, 2 * D_FF), lambda i: (0, 0))],
        out_shape=[jax.ShapeDtypeStruct((T, 2 * D_FF), BF16),
                   jax.ShapeDtypeStruct((F_CONV, 2 * D_FF), F32)],
        compiler_params=_cparams("arbitrary"),
    )(up, upc, upc, da, da, conv_w)


def _local_step(x, tgt, small, plan):
    T = x.shape[0]
    tm_e = _pick(T, 256)
    tm_a = _pick(T, 512)
    tm_b = _pick(T, 128)
    tm = _pick(T, 1024)
    tm_f = _pick(T, 512)
    tt = _pick(T, 2048)
    nm = small["norm_mix"].reshape(2, 1, D_MODEL)
    nf = small["norm_ffn"].reshape(2, 1, D_MODEL)
    ngf = small["norm_final"].reshape(1, D_MODEL)
    b_s = small["a_b_s"].reshape(A_HEADS, CHUNK, 1)
    w_s = small["a_w_s"].reshape(A_HEADS, CHUNK, CHUNK)
    b_conv_w = small["b_conv_w"].reshape(B_CONV, D_B)
    sg = {}
    wt, cm = plan.weight, plan.comm

    h_m0, z_ab = _norm_mm_nn(x, nm, wt("ab_w_in", 0), g_layer=0, tm=tm, tn=512, name="ab_in", comm=cm("ab_in"))
    yab, cb = _mixer_ab_fwd(z_ab, small["a_ln_g"], small["a_ln_b"], w_s, b_s, b_conv_w, small["b_conv_b"],
                            small["b_ln_g"], small["b_ln_b"], tm=tm_e, name="mixer_ab", comm=cm("mixer_ab"))
    x1, h_f0 = _mm_nn(yab, wt("ab_w_out", 0), layer=0, tm=tm, tn=D_MODEL, residual=x, norm=(nf, 0),
                      name="ab_out", comm=cm("ab_out"))

    def ffn_fwd(xin, h, layer):
        up = _mm_nn(h, wt("f_w_up", layer), layer=0, tm=tm, tn=2 * 1408, out_dtype=BF16, name=f"ffn_up{layer}",
                    comm=cm(f"ffn_up{layer}"))
        a, upc = _ffn_act_fwd(up, small["f_conv_w"], layer=layer, tm=tm_a, name=f"ffn_act{layer}",
                              comm=cm(f"ffn_act{layer}"))
        if layer == 0:
            out = _mm_nn(a, wt("f_w_down", layer), layer=0, tm=tm, tn=D_MODEL, residual=xin, norm=(nm, 1),
                         name=f"ffn_down{layer}", comm=cm(f"ffn_down{layer}"))
        else:
            out = _mm_nn_loss(a, wt("f_w_down", layer), xin, tgt, ngf, tm=tm, name=f"ffn_down{layer}",
                              comm=cm(f"ffn_down{layer}"))
        return up, upc, a, out

    up0, upc0, a0, (x2, h_m1) = ffn_fwd(x1, h_f0, 0)
    z_c = _mm_nn(h_m1, wt("c_w_in", 0), layer=0, tm=tm, tn=768, out_dtype=BF16, name="c_in", comm=cm("c_in"))
    r = _mixer_c_fwd(z_c, small["c_conv_w"], tm=tm_e, name="mixer_c", comm=cm("mixer_c"))
    x3, h_f1 = _mm_nn(r, wt("c_w_out", 0), layer=0, tm=tm, tn=D_MODEL, residual=x2, norm=(nf, 1),
                      name="c_out", comm=cm("c_out"))
    up1, upc1, a1, (loss, dx, sg["norm_final"]) = ffn_fwd(x3, h_f1, 1)

    def ffn_bwd(dx, xin, h, up, upc, a, layer):
        da = _mm_nt(dx, wt("f_w_down", layer), layer=0, tm=tm, tn=1408, out_dtype=BF16,
                    name=f"ffn_down_dx{layer}", comm=cm(f"ffn_down_dx{layer}"))
        plan.grad_ready("f_w_down", layer, _mm_tn(a, dx, shards=None, tk=1408, tn=1024, tt=tt,
                                                  name=f"ffn_down_dw{layer}", comm=cm(f"ffn_down_dw{layer}")))
        dup, dcw = _ffn_act_bwd(up, upc, da, small["f_conv_w"], layer=layer, tm=tm_b, name=f"ffn_act_bwd{layer}",
                                comm=cm(f"ffn_act_bwd{layer}"))
        plan.grad_ready("f_w_up", layer, _mm_tn(h, dup, shards=N_CHIPS, tk=512, tn=2 * 1408, tt=tt,
                                                name=f"ffn_up_dw{layer}", comm=cm(f"ffn_up_dw{layer}")))
        dxin, dg = _mm_nt_norm(dup, wt("f_w_up", layer), xin, nf, dx, g_layer=layer, tm=tm_f,
                               name=f"ffn_up_dx{layer}", comm=cm(f"ffn_up_dx{layer}"))
        return dxin, dg, dcw

    dx, dnf1, dfc1 = ffn_bwd(dx, x3, h_f1, up1, upc1, a1, 1)
    dr = _mm_nt(dx, wt("c_w_out", 0), layer=0, tm=tm, tn=512, out_dtype=BF16, name="c_out_dx", comm=cm("c_out_dx"))
    plan.grad_ready("c_w_out", 0, _mm_tn(r, dx, shards=None, tk=1024, tn=1024, tt=tt, name="c_out_dw",
                                         comm=cm("c_out_dw")))
    dz_c, dccw = _mixer_c_bwd(z_c, dr, small["c_conv_w"], tm=tm_e, name="mixer_c_bwd", comm=cm("mixer_c_bwd"))
    sg["c_conv_w"] = dccw.reshape(1, C_CONV, D_MODEL)
    plan.grad_ready("c_w_in", 0, _mm_tn(h_m1, dz_c, shards=N_CHIPS, tk=1024, tn=768, tt=tt, name="c_in_dw",
                                        comm=cm("c_in_dw")))
    dx, dnm1 = _mm_nt_norm(dz_c, wt("c_w_in", 0), x2, nm, dx, g_layer=1, tm=tm_f, name="c_in_dx",
                           comm=cm("c_in_dx"))
    dx, dnf0, dfc0 = ffn_bwd(dx, x1, h_f0, up0, upc0, a0, 0)
    dyab = _mm_nt(dx, wt("ab_w_out", 0), layer=0, tm=tm, tn=512, out_dtype=BF16, name="ab_out_dx",
                  comm=cm("ab_out_dx"))
    plan.grad_ready("ab_w_out", 0, _mm_tn(yab, dx, shards=None, tk=1024, tn=1024, tt=tt, name="ab_out_dw",
                                          comm=cm("ab_out_dw")))
    (dza, dcb, sg["a_ln_g"], sg["a_ln_b"], dws, dbs, sg["b_ln_g"], sg["b_ln_b"]) = _mixer_ab_bwd_pre(
        z_ab, cb, dyab, small["a_ln_g"], small["a_ln_b"], w_s, b_s, small["b_ln_g"], small["b_ln_b"],
        tm=tm_e, name="mixer_ab_bwd", comm=cm("mixer_ab_bwd"))
    dz_ab, dbcw, sg["b_conv_b"] = _mixer_b_conv_bwd(z_ab, dcb, b_conv_w, dza, tm=tm_e, name="mixer_b_conv_bwd",
                                                    comm=cm("mixer_b_conv_bwd"))
    sg["a_w_s"] = dws.reshape(1, A_HEADS, CHUNK, CHUNK)
    sg["a_b_s"] = dbs.reshape(1, A_HEADS, CHUNK)
    sg["b_conv_w"] = dbcw.reshape(1, B_CONV, D_B)
    plan.grad_ready("ab_w_in", 0, _mm_tn(h_m0, dz_ab, shards=N_CHIPS, tk=1024, tn=512, tt=tt, name="ab_in_dw",
                                         comm=cm("ab_in_dw")))
    dx, dnm0 = _mm_nt_norm(dz_ab, wt("ab_w_in", 0), x, nm, dx, g_layer=0, tm=tm_f, name="ab_in_dx",
                           comm=cm("ab_in_dx"))

    sg["norm_mix"] = [dnm0, dnm1]
    sg["norm_ffn"] = [dnf0, dnf1]
    sg["f_conv_w"] = [dfc0, dfc1]
    return loss, dx, sg


BLOCK_BYTES = 3 * 1024 * 1024


BF16_SUBLANES = 16


def _row_tile(rows, row_bytes, step=SUBLANES):
    best = None
    for tr in range(step, rows + 1, step):
        if rows % tr == 0 and tr * row_bytes <= BLOCK_BYTES:
            best = tr
    if best is None:
        raise ValueError(f"no row tile for {rows}")
    return best


def _place_scalars():
    x, y, c = lax.axis_index("x"), lax.axis_index("y"), lax.axis_index("c")
    return jnp.stack([c, 2 * x + y, 2 * (1 - x) + y, 2 * x + (1 - y), 2 * (1 - x) + (1 - y)]).astype(jnp.int32)


def _cast_into_slot(w, place, *, layer, paired, name):
    L, rows, cols = w.shape
    tr = _row_tile(rows, cols * 4, BF16_SUBLANES)

    def body(place_ref, w_ref, o_ref):
        o_ref[...] = w_ref[...].astype(BF16)

    if paired:
        out_spec = pl.BlockSpec((None, None, tr, cols), lambda i, p: (0, p[1] // 2, i, p[1] % 2))
        out_shape = jax.ShapeDtypeStruct((1, N_CHIPS // 2, rows, 2 * cols), BF16)
    else:
        out_spec = pl.BlockSpec((None, None, tr, cols), lambda i, p: (0, p[1], i, 0))
        out_shape = jax.ShapeDtypeStruct((1, N_CHIPS, rows, cols), BF16)
    return pl.pallas_call(
        body, name=name,
        grid_spec=pltpu.PrefetchScalarGridSpec(
            num_scalar_prefetch=1, grid=(rows // tr,),
            in_specs=[pl.BlockSpec((None, tr, cols), lambda i, p: (layer, i, 0))],
            out_specs=out_spec),
        out_shape=out_shape,
        compiler_params=_cparams("parallel"),
    )(place, w)


def _pair_sum(g, theirs, place, *, name):
    S, rows, cols = g.shape
    half = rows // 2
    tr = _row_tile(half, cols * 4, BF16_SUBLANES)
    nb = half // tr

    def body(place_ref, g_ref, t_ref, o_ref):
        o_ref[...] = (g_ref[...] + t_ref[...]).astype(BF16)

    spec = pl.BlockSpec((None, tr, cols), lambda s, i, p: (s, i, 0))
    return pl.pallas_call(
        body, name=name,
        grid_spec=pltpu.PrefetchScalarGridSpec(
            num_scalar_prefetch=1, grid=(S, nb),
            in_specs=[pl.BlockSpec((None, tr, cols), lambda s, i, p: (s, p[0] * nb + i, 0)), spec],
            out_specs=spec),
        out_shape=jax.ShapeDtypeStruct((S, half, cols), BF16),
        compiler_params=_cparams("parallel", "parallel"),
    )(place, g, theirs)


def _chip_sum(p, r, g_prev, place, *, layer, shape, name):
    L, rows, cols = shape
    half = rows // 2
    tr = _row_tile(half, cols * 4, BF16_SUBLANES)
    nb = half // tr

    def body(place_ref, p_ref, r_ref, *rest):
        o_ref = rest[-1]
        mine = p_ref[...].astype(F32)
        peers = [r_ref[j].astype(F32) for j in range(3)]
        acc = None
        for s in range(N_CHIPS):
            term = jnp.where(place_ref[1] == s, mine,
                             jnp.where(place_ref[2] == s, peers[0],
                                       jnp.where(place_ref[3] == s, peers[1], peers[2])))
            acc = term if acc is None else acc + term
        o_ref[...] = acc

    in_specs = [pl.BlockSpec((None, tr, cols), lambda i, pr: (pr[1], i, 0)),
                pl.BlockSpec((3, tr, cols), lambda i, pr: (0, i, 0))]
    args = [place, p, r]
    aliases = {}
    if g_prev is not None:
        in_specs.append(HBM_REF)
        args.append(g_prev)
        aliases = {3: 0}
    return pl.pallas_call(
        body, name=name,
        grid_spec=pltpu.PrefetchScalarGridSpec(
            num_scalar_prefetch=1, grid=(nb,), in_specs=in_specs,
            out_specs=pl.BlockSpec((None, tr, cols), lambda i, pr: (layer, pr[0] * nb + i, 0))),
        out_shape=jax.ShapeDtypeStruct(shape, F32), input_output_aliases=aliases,
        compiler_params=_cparams("parallel"),
    )(*args)


def _adamw_math(w, g, m, v):
    m2 = ADAM_B1 * m + (1.0 - ADAM_B1) * g
    v2 = ADAM_B2 * v + (1.0 - ADAM_B2) * (g * g)
    m_hat = m2 / (1.0 - ADAM_B1 ** ADAM_STEP)
    v_hat = v2 / (1.0 - ADAM_B2 ** ADAM_STEP)
    delta = -ADAM_LR * (m_hat / (jnp.sqrt(v_hat) + ADAM_EPS) + ADAM_WD * w)
    return delta, m2, v2


def _adamw(w, g, m, v, *, name):
    L, rows, cols = w.shape
    tr = _row_tile(rows, cols * 4)

    def body(w_ref, g_ref, m_ref, v_ref, go_ref, d_ref, m2_ref, v2_ref):
        g = g_ref[...]
        d, m2, v2 = _adamw_math(w_ref[...], g, m_ref[...], v_ref[...])
        go_ref[...] = g
        d_ref[...] = d
        m2_ref[...] = m2
        v2_ref[...] = v2

    spec = pl.BlockSpec((None, tr, cols), lambda l, i: (l, i, 0))
    shape = jax.ShapeDtypeStruct(w.shape, F32)
    return pl.pallas_call(
        body, name=name, grid=(L, rows // tr), in_specs=[spec] * 4, out_specs=[spec] * 4,
        out_shape=[shape] * 4,
        compiler_params=_cparams("parallel", "parallel"),
    )(w, g, m, v)


def _allreduce_pack(pack, *, name, comm):
    R = pack.shape[0]
    half = R // 2
    nr, nw = len(comm.reads), len(comm.writes)

    def body(*refs):
        p_ref, rd, wr_in = refs[0], refs[1:1 + nr], refs[1 + nr:1 + nr + nw]
        o_ref, wr_out = refs[1 + nr + nw], refs[2 + nr + nw:2 + nr + 2 * nw]
        sib_ref, chip_ref, parts_ref, sems, comm_sems = refs[2 + nr + 2 * nw:]
        src = dict(zip(comm.reads, rd))
        src.update(zip(comm.writes, wr_in))
        dst = dict(zip(comm.writes, wr_out))
        comm.start(src, dst, comm_sems)
        x, y, c, k, sib, peers = _place()
        swap = _remote(p_ref, sib_ref, sems.at[0, 0], sems.at[0, 1], sib)
        swap.start()
        swap.wait()
        chip_ref[...] = p_ref[...] + sib_ref[...]
        mine = chip_ref.at[pl.ds(pl.multiple_of(c * half, SUBLANES), half)]
        sends = [_remote(mine, parts_ref.at[j], sems.at[1 + j, 0], sems.at[1 + j, 1], (px, py, c))
                 for j, (px, py) in enumerate(peers)]
        for rc in sends:
            rc.start()
        for rc in sends:
            rc.wait()
        own = mine[...]
        others = [parts_ref[j] for j in range(3)]
        acc = None
        for s in range(N_CHIPS):
            term = own
            for j, (px, py) in enumerate(peers):
                term = jnp.where(2 * px + py == s, others[j], term)
            acc = term if acc is None else acc + term
        done = o_ref.at[pl.ds(pl.multiple_of(c * half, SUBLANES), half)]
        done[...] = acc
        theirs = o_ref.at[pl.ds(pl.multiple_of((1 - c) * half, SUBLANES), half)]
        share = _remote(done, done, sems.at[4, 0], sems.at[4, 1], sib)
        share.start()
        _remote(done, theirs, sems.at[4, 0], sems.at[4, 1], sib).wait()
        comm.finish(src, dst, comm_sems)

    vm = pl.BlockSpec(memory_space=pltpu.VMEM)
    operands, shapes = _comm_operands(comm)
    outs = pl.pallas_call(
        body, name=name, in_specs=[vm] + [HBM_REF] * (nr + nw), out_specs=[vm] + [HBM_REF] * nw,
        out_shape=[jax.ShapeDtypeStruct((R, LANES), F32)] + shapes,
        input_output_aliases={1 + nr + q: 1 + q for q in range(nw)},
        scratch_shapes=[pltpu.VMEM((R, LANES), F32), pltpu.VMEM((R, LANES), F32),
                        pltpu.VMEM((3, half, LANES), F32), pltpu.SemaphoreType.DMA((5, 2)),
                        pltpu.SemaphoreType.DMA((comm.ncopies, 2))],
        compiler_params=pltpu.CompilerParams(vmem_limit_bytes=VMEM_BYTES_MAX),
    )(pack, *operands)
    for q, n in enumerate(comm.writes):
        comm.plan.bufs[n] = outs[1 + q]
    return outs[0]


PACK_UNIT = SUBLANES * LANES


def _pack(arrays):
    flat, sizes = [], []
    for a in arrays:
        pieces = a if isinstance(a, (list, tuple)) else [a]
        v = jnp.concatenate([p.reshape(-1) for p in pieces]) if len(pieces) > 1 else pieces[0].reshape(-1)
        size = v.shape[0]
        padded = -(-size // PACK_UNIT) * PACK_UNIT
        flat.append(jnp.pad(v, (0, padded - size)))
        sizes.append((size, padded))
    total = sum(p for _, p in sizes)
    if (total // PACK_UNIT) % 2:
        flat.append(jnp.zeros((PACK_UNIT,), F32))
    return jnp.concatenate(flat).reshape(-1, LANES), sizes


def _unpack(pack, sizes, shapes):
    v = pack.reshape(-1)
    out, off = [], 0
    for (size, padded), shape in zip(sizes, shapes):
        out.append(v[off:off + size].reshape(shape))
        off += padded
    return out


BIG = ("ab_w_in", "ab_w_out", "c_w_in", "c_w_out", "f_w_up", "f_w_down")
COL_SHARDED = ("ab_w_in", "c_w_in", "f_w_up")
PAIRED = ("f_w_up",)
SMALL_REPLICATED = ("norm_mix", "norm_ffn", "norm_final", "a_ln_g", "a_ln_b", "a_w_s", "a_b_s",
                    "b_conv_b", "b_ln_g", "b_ln_b")
SMALL_SHARDED = ("b_conv_w", "c_conv_w", "f_conv_w")
SMALL = SMALL_REPLICATED + SMALL_SHARDED
ALL_WEIGHTS = ("norm_mix", "norm_ffn", "norm_final", "ab_w_in", "a_ln_g", "a_ln_b", "a_w_s", "a_b_s",
               "b_conv_w", "b_conv_b", "b_ln_g", "b_ln_b", "ab_w_out", "c_w_in", "c_conv_w", "c_w_out",
               "f_w_up", "f_conv_w", "f_w_down")


LATE = "late"
LATE_STEPS = 2
SCHEDULE = {
    "ab_in": [("gi", "f_w_up", 0, 0, 4), ("gi", "ab_w_out", 0)],
    "mixer_ab": [("gd", "f_w_up", 0, 0, 4), ("gd", "ab_w_out", 0), ("gi", "f_w_up", 0, 1, 4),
                 ("gi", "f_w_up", 0, 2, 4)],
    "ab_out": [("gd", "f_w_up", 0, 1, 4), ("gd", "f_w_up", 0, 2, 4), ("gi", "f_w_up", 0, 3, 4), LATE,
               ("gd", "f_w_up", 0, 3, 4)],
    "ffn_up0": [("gi", "f_w_down", 0), ("gi", "c_w_in", 0, 0, 2)],
    "ffn_act0": [("gd", "f_w_down", 0), ("gd", "c_w_in", 0, 0, 2), ("gi", "c_w_in", 0, 1, 2),
                 ("gi", "f_w_up", 1, 0, 4)],
    "ffn_down0": [("gd", "c_w_in", 0, 1, 2), ("gd", "f_w_up", 1, 0, 4), ("gi", "f_w_up", 1, 1, 4),
                  ("gi", "c_w_out", 0)],
    "c_in": [("gd", "f_w_up", 1, 1, 4), ("gd", "c_w_out", 0), ("gi", "f_w_up", 1, 2, 4),
             ("gi", "f_w_up", 1, 3, 4), LATE, ("gd", "f_w_up", 1, 2, 4), ("gd", "f_w_up", 1, 3, 4)],
    "ffn_up1": [("gi", "f_w_down", 1), LATE, ("gd", "f_w_down", 1)],
    "ffn_act_bwd1": [("px", "f_w_down", 1)],
    "ffn_up_dx1": [("cx", "f_w_down", 1), ("px", "f_w_up", 1)],
    "mixer_c_bwd": [("cx", "f_w_up", 1, 0, 4), ("px", "c_w_out", 0)],
    "c_in_dx": [("cx", "f_w_up", 1, 1, 4), ("px", "c_w_in", 0)],
    "ffn_act_bwd0": [("cx", "f_w_up", 1, 2, 4), ("cx", "c_w_out", 0), ("cx", "c_w_in", 0),
                     ("px", "f_w_down", 0), ("ps", "f_w_down", 1)],
    "ffn_up_dx0": [("cx", "f_w_down", 0), ("cx", "f_w_up", 1, 3, 4), ("px", "f_w_up", 0)],
    "mixer_ab_bwd": [("cx", "f_w_up", 0, 0, 4), ("px", "ab_w_out", 0)],
    "mixer_b_conv_bwd": [("cx", "f_w_up", 0, 1, 4), ("cx", "f_w_up", 0, 2, 4), ("cx", "ab_w_out", 0),
                         ("ps", "c_w_out", 0), ("ps", "c_w_in", 0), ("ps", "f_w_down", 0),
                         ("ps", "f_w_up", 1)],
    "ab_in_dx": [("cx", "f_w_up", 0, 3, 4), ("px", "ab_w_in", 0)],
}


class _Plan:
    def __init__(self, shapes, place):
        self.shapes, self.place, self.bufs = shapes, place, {}
        self.summed, self.shared = set(), set()

    def weight(self, name, layer):
        g = self.bufs[f"w:{name}:{layer}"]
        if name in COL_SHARDED:
            return g
        _, S, rows, cols = g.shape
        return g.reshape(1, S * rows, cols)

    def grad_ready(self, name, layer, g):
        _, rows, cols = self.shapes[name]
        hbm = lambda a: pltpu.with_memory_space_constraint(a, pltpu.HBM)
        self.bufs[f"g:{name}:{layer}"] = g.reshape(N_CHIPS, rows, cols)
        self.bufs[f"t:{name}:{layer}"] = hbm(lax.empty((N_CHIPS, rows // 2, cols), F32))
        self.bufs[f"l:{name}:{layer}"] = hbm(lax.empty((3, rows // 2, cols), BF16))

    def job(self, kind, name, layer, part=0, parts=1):
        _, rows, cols = self.shapes[name]
        key = f"{name}:{layer}"
        if kind == "gi":
            return _job_gather_ici("w:" + key, rows, cols, part, parts)
        if kind == "gd":
            return _job_gather_d2d("w:" + key, rows, cols, part, parts)
        if kind == "px":
            return _job_pair_exchange("g:" + key, "t:" + key, rows, part, parts)
        if kind == "cx":
            if "p:" + key not in self.bufs:
                self.bufs["p:" + key] = _pair_sum(self.bufs["g:" + key], self.bufs["t:" + key], self.place,
                                                  name=f"pair_sum_{name}{layer}")
            nr = rows // 2 // parts
            return _job_chip_exchange("p:" + key, "l:" + key, part * nr, nr)
        if kind == "ps":
            self.chip_sum(name, layer)
            self.shared.add(key)
            return _job_pair_share("G:" + name, layer, rows)
        raise ValueError(kind)

    def chip_sum(self, name, layer):
        key = f"{name}:{layer}"
        if key not in self.summed:
            self.summed.add(key)
            self.bufs["G:" + name] = _chip_sum(self.bufs["p:" + key], self.bufs["l:" + key],
                                               self.bufs.get("G:" + name), self.place, layer=layer,
                                               shape=self.shapes[name], name=f"chip_sum_{name}{layer}")

    def comm(self, call):
        specs = SCHEDULE.get(call)
        if specs is None:
            return None
        late = len(specs) - 1 - specs.index(LATE) if LATE in specs else 0
        return _Comm(self, [self.job(*spec) for spec in specs if spec != LATE], late)


def _step(x, tgt, w, m, v):
    chip = 2 * lax.axis_index("x") + lax.axis_index("y")
    place = _place_scalars()
    plan = _Plan({n: w[n].shape for n in BIG}, place)
    items = [(n, l) for n in BIG for l in range(w[n].shape[0])]

    for n, l in items:
        plan.bufs[f"w:{n}:{l}"] = _cast_into_slot(w[n], place, layer=l, paired=n in PAIRED, name=f"cast_{n}{l}")
    conv_pack, conv_sizes = _pack([w[n] for n in SMALL_SHARDED])
    hbm = lambda a: pltpu.with_memory_space_constraint(a, pltpu.HBM)
    plan.bufs["conv:mine"] = hbm(conv_pack)
    plan.bufs["conv:all"] = hbm(lax.empty((N_CHIPS,) + conv_pack.shape, F32))
    _comm_only(plan, [[plan.job("gi", "ab_w_in", 0), _job_chip_gather("conv:mine", "conv:all")],
                      [plan.job("gd", "ab_w_in", 0)]], name="gather_first")
    conv_shapes = [w[n].shape for n in SMALL_SHARDED]
    per_chip = [_unpack(plan.bufs["conv:all"][s], conv_sizes, conv_shapes) for s in range(N_CHIPS)]
    small = {n: w[n] for n in SMALL_REPLICATED}
    for idx, n in enumerate(SMALL_SHARDED):
        small[n] = jnp.concatenate([jnp.where(chip == s, w[n], per_chip[s][idx]) for s in range(N_CHIPS)], axis=-1)

    loss, dx, sg = _local_step(x, tgt, small, plan)

    g_pack, g_sizes = _pack([sg[n] for n in SMALL] + [loss])
    g_sum = _allreduce_pack(g_pack, name="allreduce_small_grads",
                            comm=_Comm(plan, [plan.job("cx", "ab_w_in", 0)]))
    full_shapes = [small[n].shape for n in SMALL]
    *summed, loss = _unpack(g_sum, g_sizes, full_shapes + [(1, 1)])
    g_small = dict(zip(SMALL, summed))
    for n in SMALL_SHARDED:
        width = w[n].shape[-1]
        g_small[n] = lax.dynamic_slice_in_dim(g_small[n], chip * width, width, axis=g_small[n].ndim - 1)

    _comm_only(plan, [[plan.job("ps", n, l) for n, l in items if f"{n}:{l}" not in plan.shared]],
               name="reduce_pair_share")
    grads_big = [plan.bufs["G:" + n] for n in BIG]

    grad, delta, new_m, new_v = {}, {}, {}, {}
    for n, g in zip(BIG, grads_big):
        grad[n], delta[n], new_m[n], new_v[n] = _adamw(w[n], g, m[n], v[n], name=f"adamw_{n}")
    shapes = [w[n].shape for n in SMALL]
    wp, sizes = _pack([w[n] for n in SMALL])
    gp, _ = _pack([g_small[n] for n in SMALL])
    mp, _ = _pack([m[n] for n in SMALL])
    vp, _ = _pack([v[n] for n in SMALL])
    R = wp.shape[0]
    _, dp, m2p, v2p = _adamw(wp.reshape(1, R, LANES), gp.reshape(1, R, LANES), mp.reshape(1, R, LANES),
                             vp.reshape(1, R, LANES), name="adamw_small")
    for n, d_, m_, v_ in zip(SMALL, _unpack(dp, sizes, shapes), _unpack(m2p, sizes, shapes),
                             _unpack(v2p, sizes, shapes)):
        grad[n] = g_small[n]
        delta[n], new_m[n], new_v[n] = d_, m_, v_
    return loss, dx, grad, delta, new_m, new_v


def kernel(x, norm_mix, norm_ffn, norm_final, ab_w_in, a_ln_g, a_ln_b, a_w_s, a_b_s, b_conv_w, b_conv_b, b_ln_g, b_ln_b, ab_w_out, c_w_in, c_conv_w, c_w_out, f_w_up, f_conv_w, f_w_down, loss_target, m_norm_mix, m_norm_ffn, m_norm_final, m_ab_w_in, m_a_ln_g, m_a_ln_b, m_a_w_s, m_a_b_s, m_b_conv_w, m_b_conv_b, m_b_ln_g, m_b_ln_b, m_ab_w_out, m_c_w_in, m_c_conv_w, m_c_w_out, m_f_w_up, m_f_conv_w, m_f_w_down, v_norm_mix, v_norm_ffn, v_norm_final, v_ab_w_in, v_a_ln_g, v_a_ln_b, v_a_w_s, v_a_b_s, v_b_conv_w, v_b_conv_b, v_b_ln_g, v_b_ln_b, v_ab_w_out, v_c_w_in, v_c_conv_w, v_c_w_out, v_f_w_up, v_f_conv_w, v_f_w_down):
    given = dict(locals())
    w = {n: given[n] for n in ALL_WEIGHTS}
    m = {n: given["m_" + n] for n in ALL_WEIGHTS}
    v = {n: given["v_" + n] for n in ALL_WEIGHTS}
    T = x.shape[1]
    loss, dx, grad, delta, new_m, new_v = _step(x.reshape(T, D_MODEL), loss_target.reshape(T, D_MODEL), w, m, v)
    out = [loss[0, 0], dx.reshape(x.shape)]
    for d in (grad, delta, new_m, new_v):
        out += [d[n] for n in ALL_WEIGHTS]
    return tuple(out)
```

```python
import functools
import math

import jax
import jax.numpy as jnp
from jax import lax
from jax.experimental import pallas as pl
from jax.experimental.pallas import tpu as pltpu

F32 = jnp.float32
BF16 = jnp.bfloat16

EPS = 1e-6
D_MODEL = 1024
CHUNK = 128
HEAD_DIM = 128
A_HEADS = 4
D_A = 512
D_B = 512
B_CONV = 31
C_CONV = 3
D_FF = 2816
F_CONV = 3
N_CHIPS = 4

ADAM_LR = 0.001
ADAM_B1 = 0.9
ADAM_B2 = 0.999
ADAM_EPS = 1e-08
ADAM_WD = 0.01
ADAM_STEP = 10

SUBLANES = 8
LANES = 128
HALO_SHORT = 16
HALO_LONG = 32
VMEM_BYTES_MAX = 60000 * 1024

INV_SQRT2 = 1.0 / math.sqrt(2.0)
INV_SQRT_2PI = 1.0 / math.sqrt(2.0 * math.pi)

MESH = pl.DeviceIdType.MESH


def _cparams(*sem):
    return pltpu.CompilerParams(dimension_semantics=sem, vmem_limit_bytes=VMEM_BYTES_MAX)


def _pick(total, pref):
    for c in (2048, 1024, 512, 256, 128):
        if c <= pref and total % c == 0:
            return c
    raise ValueError(f"no tile for {total}")


def _sigmoid(x):
    return jax.nn.sigmoid(x)


def _silu(x):
    return x * _sigmoid(x)


def _dsilu(x):
    s = _sigmoid(x)
    return s * (1.0 + x * (1.0 - s))


def _gelu(x):
    return 0.5 * x * (1.0 + lax.erf(x * INV_SQRT2))


def _dgelu(x):
    return 0.5 * (1.0 + lax.erf(x * INV_SQRT2)) + x * jnp.exp(-0.5 * x * x) * INV_SQRT_2PI


def _ln_stats(x):
    mu = jnp.mean(x, axis=-1, keepdims=True)
    xc = x - mu
    var = jnp.mean(xc * xc, axis=-1, keepdims=True)
    r = lax.rsqrt(var + EPS)
    return xc * r, r


def _ln_bwd(dy, xh, r, g):
    dxh = dy * g
    m1 = jnp.mean(dxh, axis=-1, keepdims=True)
    m2 = jnp.mean(dxh * xh, axis=-1, keepdims=True)
    return r * (dxh - m1 - xh * m2)


def _rowsum(x):
    return jnp.sum(x, axis=0, keepdims=True)


HBM_REF = pl.BlockSpec(memory_space=pltpu.HBM)


def _place():
    x, y, c = lax.axis_index("x"), lax.axis_index("y"), lax.axis_index("c")
    peers = [(1 - x, y), (x, 1 - y), (1 - x, 1 - y)]
    return x, y, c, 2 * x + y, (x, y, 1 - c), peers


def _half(rows, which):
    return pl.ds(which * (rows // 2), rows // 2)


def _remote(src, dst, send_sem, recv_sem, device):
    return pltpu.make_async_remote_copy(src_ref=src, dst_ref=dst, send_sem=send_sem, recv_sem=recv_sem,
                                        device_id=device, device_id_type=MESH)


class _Job:
    def __init__(self, reads, writes, ncopies, copies):
        self.reads, self.writes, self.ncopies, self.copies = reads, writes, ncopies, copies


def _share(rows, which, part, parts):
    nr = rows // 2 // parts
    return pl.ds(which * (rows // 2) + part * nr, nr)


def _slot(ref, chip, rows, cols):
    if ref.shape[1] == N_CHIPS:
        return ref.at[0, chip, rows]
    return ref.at[0, chip // 2, rows, pl.ds(pl.multiple_of((chip % 2) * cols, LANES), cols)]


def _job_gather_ici(name, rows, cols, part, parts):
    def copies(src, dst, sem):
        x, y, c, k, sib, peers = _place()
        mine_rows = _share(rows, c, part, parts)
        out = []
        for j, (px, py) in enumerate(peers):
            mine = _slot(src[name], k, mine_rows, cols)
            out.append((_remote(mine, _slot(dst[name], k, mine_rows, cols), sem(j, 0), sem(j, 1), (px, py, c)),
                        _remote(mine, _slot(dst[name], 2 * px + py, mine_rows, cols), sem(j, 0), sem(j, 1),
                                (px, py, c))))
        return out
    return _Job([], [name], 3, copies)


def _job_gather_d2d(name, rows, cols, part, parts):
    def copies(src, dst, sem):
        x, y, c, k, sib, peers = _place()
        out = []
        for j, (px, py) in enumerate(peers):
            mine_rows, their_rows = _share(rows, c, part, parts), _share(rows, 1 - c, part, parts)
            landed = _slot(src[name], 2 * px + py, mine_rows, cols)
            out.append((_remote(landed, _slot(dst[name], 2 * px + py, mine_rows, cols), sem(j, 0), sem(j, 1), sib),
                        _remote(landed, _slot(dst[name], 2 * px + py, their_rows, cols), sem(j, 0), sem(j, 1), sib)))
        return out
    return _Job([], [name], 3, copies)


def _job_chip_gather(sname, dname):
    def copies(src, dst, sem):
        x, y, c, k, sib, peers = _place()
        return [(_remote(src[sname], dst[dname].at[k], sem(j, 0), sem(j, 1), (px, py, c)),
                 _remote(src[sname], dst[dname].at[2 * px + py], sem(j, 0), sem(j, 1), (px, py, c)))
                for j, (px, py) in enumerate(peers)]
    return _Job([sname], [dname], 3, copies)


def _job_pair_exchange(gname, tname, rows, part, parts):
    nr = rows // 2 // parts

    def copies(src, dst, sem):
        x, y, c, k, sib, peers = _place()
        cp = _remote(src[gname].at[:, _share(rows, 1 - c, part, parts), :],
                     dst[tname].at[:, pl.ds(part * nr, nr), :], sem(0, 0), sem(0, 1), sib)
        return [(cp, cp)]
    return _Job([gname], [tname], 1, copies)


def _job_chip_exchange(pname, lname, r0, nr):
    def copies(src, dst, sem):
        x, y, c, k, sib, peers = _place()
        out = []
        for j, (px, py) in enumerate(peers):
            cp = _remote(src[pname].at[2 * px + py, pl.ds(r0, nr)], dst[lname].at[j, pl.ds(r0, nr)],
                         sem(j, 0), sem(j, 1), (px, py, c))
            out.append((cp, cp))
        return out
    return _Job([pname], [lname], 3, copies)


def _job_pair_share(name, layer, rows):
    def copies(src, dst, sem):
        x, y, c, k, sib, peers = _place()
        mine = src[name].at[layer, _half(rows, c)]
        return [(_remote(mine, dst[name].at[layer, _half(rows, c)], sem(0, 0), sem(0, 1), sib),
                 _remote(mine, dst[name].at[layer, _half(rows, 1 - c)], sem(0, 0), sem(0, 1), sib))]
    return _Job([], [name], 1, copies)


class _Comm:
    def __init__(self, plan, jobs, late=0, late_steps=0):
        self.plan, self.jobs, self.late, self.late_steps = plan, jobs, late, late_steps
        self.writes, self.reads = [], []
        for job in jobs:
            for n in job.writes:
                if n not in self.writes:
                    self.writes.append(n)
        for job in jobs:
            for n in job.reads:
                if n not in self.writes and n not in self.reads:
                    self.reads.append(n)
        self.ncopies = sum(job.ncopies for job in jobs)

    def descriptors(self, src, dst, sems, base, stage):
        out = []
        first_late = len(self.jobs) - self.late
        for idx, job in enumerate(self.jobs):
            if stage is None or stage == int(idx >= first_late):
                sem = lambda j, which, base=base: sems.at[base + j, which]
                out += job.copies(src, dst, sem)
            base += job.ncopies
        return out

    def start(self, src, dst, sems, base=0, stage=None):
        for first, _ in self.descriptors(src, dst, sems, base, stage):
            first.start()

    def finish(self, src, dst, sems, base=0, stage=None):
        for _, landed in self.descriptors(src, dst, sems, base, stage):
            landed.wait()


def _comm_operands(comm):
    bufs = comm.plan.bufs
    shapes = [jax.ShapeDtypeStruct(bufs[n].shape, bufs[n].dtype) for n in comm.writes]
    return [bufs[n] for n in comm.reads] + [bufs[n] for n in comm.writes], shapes


def _pallas(comm, body, *, name, grid, in_specs, out_specs, out_shape, compiler_params, scratch_shapes=(),
            aliases=None):
    aliases = dict(aliases or {})
    if comm is None:
        return pl.pallas_call(body, name=name, grid=grid, in_specs=in_specs, out_specs=out_specs,
                              out_shape=out_shape, scratch_shapes=list(scratch_shapes),
                              input_output_aliases=aliases, compiler_params=compiler_params)
    single = not isinstance(out_shape, (list, tuple))
    base_specs = [out_specs] if single else list(out_specs)
    base_shape = [out_shape] if single else list(out_shape)
    nb, nr, nw, nbo, nsc = len(in_specs), len(comm.reads), len(comm.writes), len(base_specs), len(scratch_shapes)

    def wrapped(*refs):
        base_in, rd, wr_in = refs[:nb], refs[nb:nb + nr], refs[nb + nr:nb + nr + nw]
        o0 = nb + nr + nw
        base_out, wr_out = refs[o0:o0 + nbo], refs[o0 + nbo:o0 + nbo + nw]
        scratch, sems = refs[o0 + nbo + nw:o0 + nbo + nw + nsc], refs[-1]
        src = dict(zip(comm.reads, rd))
        src.update(zip(comm.writes, wr_in))
        dst = dict(zip(comm.writes, wr_out))
        first = functools.reduce(jnp.logical_and, [pl.program_id(a) == 0 for a in range(len(grid))])
        last = functools.reduce(jnp.logical_and,
                                [pl.program_id(a) == pl.num_programs(a) - 1 for a in range(len(grid))])

        if not comm.late:
            @pl.when(first)
            def _():
                comm.start(src, dst, sems)
            body(*base_in, *base_out, *scratch)

            @pl.when(last)
            def _():
                comm.finish(src, dst, sems)
            return
        step = functools.reduce(lambda acc, a: acc * grid[a] + pl.program_id(a), range(len(grid)), 0)

        @pl.when(first)
        def _():
            comm.start(src, dst, sems, stage=0)

        @pl.when(step == math.prod(grid) - comm.late_steps)
        def _():
            comm.finish(src, dst, sems, stage=0)
            comm.start(src, dst, sems, stage=1)
        body(*base_in, *base_out, *scratch)

        @pl.when(last)
        def _():
            comm.finish(src, dst, sems, stage=1)

    operands, shapes = _comm_operands(comm)
    call = pl.pallas_call(
        wrapped, name=name, grid=grid, in_specs=list(in_specs) + [HBM_REF] * (nr + nw),
        out_specs=base_specs + [HBM_REF] * nw, out_shape=base_shape + shapes,
        input_output_aliases={**aliases, **{nb + nr + q: nbo + q for q in range(nw)}},
        scratch_shapes=list(scratch_shapes) + [pltpu.SemaphoreType.DMA((comm.ncopies, 2))],
        compiler_params=compiler_params)

    def run(*args):
        outs = call(*args, *operands)
        for q, n in enumerate(comm.writes):
            comm.plan.bufs[n] = outs[nbo + q]
        return outs[0] if single else list(outs[:nbo])

    return run


def _comm_only(plan, phases, *, name):
    comms = [_Comm(plan, jobs) for jobs in phases]
    both = _Comm(plan, [job for jobs in phases for job in jobs])
    nr, nw = len(both.reads), len(both.writes)

    def body(*refs):
        rd, wr_in, wr_out, sems = refs[:nr], refs[nr:nr + nw], refs[nr + nw:nr + 2 * nw], refs[-1]
        src = dict(zip(both.reads, rd))
        src.update(zip(both.writes, wr_in))
        dst = dict(zip(both.writes, wr_out))
        base = 0
        for comm in comms:
            comm.start(src, dst, sems, base)
            comm.finish(src, dst, sems, base)
            base += comm.ncopies

    operands, shapes = _comm_operands(both)
    outs = pl.pallas_call(
        body, name=name, in_specs=[HBM_REF] * (nr + nw), out_specs=[HBM_REF] * nw, out_shape=shapes,
        input_output_aliases={nr + q: q for q in range(nw)},
        scratch_shapes=[pltpu.SemaphoreType.DMA((both.ncopies, 2))],
    )(*operands)
    for q, n in enumerate(both.writes):
        plan.bufs[n] = outs[q]


def _mm_nn(a, w, *, layer, tm, tn, residual=None, norm=None, out_dtype=F32, name, comm=None):
    T, K = a.shape
    if w.ndim == 4:
        _, S, _, n4 = w.shape
        N = S * n4
        bps = n4 // tn
        w_spec = pl.BlockSpec((None, None, K, tn), lambda j, i: (layer, j // bps, 0, j % bps))
    else:
        N = w.shape[2]
        w_spec = pl.BlockSpec((None, K, tn), lambda j, i: (layer, 0, j))
    in_specs = [pl.BlockSpec((tm, K), lambda j, i: (i, 0)), w_spec]
    args = [a, w]
    if residual is not None:
        in_specs.append(pl.BlockSpec((tm, tn), lambda j, i: (i, j)))
        args.append(residual)
    out_specs = pl.BlockSpec((tm, tn), lambda j, i: (i, j))
    out_shape = jax.ShapeDtypeStruct((T, N), out_dtype)
    if norm is not None:
        assert tn == N
        g, norm_layer = norm
        in_specs.append(pl.BlockSpec((None, 1, N), lambda j, i: (norm_layer, 0, 0)))
        args.append(g)
        out_specs = [out_specs, pl.BlockSpec((tm, tn), lambda j, i: (i, j))]
        out_shape = [out_shape, jax.ShapeDtypeStruct((T, N), BF16)]

    def body(*refs):
        a_ref, w_ref = refs[0], refs[1]
        acc = jnp.dot(a_ref[...].astype(BF16), w_ref[...], preferred_element_type=F32)
        if residual is not None:
            acc = refs[2][...] + acc
        if norm is None:
            refs[-1][...] = acc.astype(out_dtype)
        else:
            refs[-2][...] = acc.astype(out_dtype)
            r = lax.rsqrt(jnp.mean(acc * acc, axis=-1, keepdims=True) + EPS)
            refs[-1][...] = (acc * r * refs[-3][...]).astype(BF16)

    return _pallas(
        comm, body, name=name, grid=(N // tn, T // tm), in_specs=in_specs,
        out_specs=out_specs, out_shape=out_shape,
        compiler_params=_cparams("parallel", "parallel"),
    )(*args)


def _norm_mm_nn(x, g, w, *, g_layer, tm, tn, name, comm=None):
    T, K = x.shape
    _, S, _, n4 = w.shape
    bps = n4 // tn

    def body(x_ref, g_ref, w_ref, h_ref, o_ref):
        @pl.when(pl.program_id(1) == 0)
        def _():
            xf = x_ref[...]
            r = lax.rsqrt(jnp.mean(xf * xf, axis=-1, keepdims=True) + EPS)
            h_ref[...] = (xf * r * g_ref[...]).astype(BF16)
        o_ref[...] = jnp.dot(h_ref[...], w_ref[...], preferred_element_type=F32).astype(BF16)

    return _pallas(
        comm, body, name=name, grid=(T // tm, S * bps),
        in_specs=[pl.BlockSpec((tm, K), lambda i, j: (i, 0)),
                  pl.BlockSpec((None, 1, K), lambda i, j: (g_layer, 0, 0)),
                  pl.BlockSpec((None, None, K, tn), lambda i, j: (0, j // bps, 0, j % bps))],
        out_specs=[pl.BlockSpec((tm, K), lambda i, j: (i, 0)), pl.BlockSpec((tm, tn), lambda i, j: (i, j))],
        out_shape=[jax.ShapeDtypeStruct((T, K), BF16), jax.ShapeDtypeStruct((T, S * n4), BF16)],
        compiler_params=_cparams("parallel", "arbitrary"),
    )(x, g, w)


def _mm_nt(dy, w, *, layer, tm, tn, name, out_dtype=F32, comm=None):
    T = dy.shape[0]
    nt_dims = (((1,), (1,)), ((), ()))
    _, R, N = w.shape

    def body2(dy_ref, w_ref, o_ref):
        o_ref[...] = lax.dot_general(dy_ref[...].astype(BF16), w_ref[...], nt_dims,
                                     preferred_element_type=F32).astype(out_dtype)

    return _pallas(
        comm, body2, name=name, grid=(R // tn, T // tm),
        in_specs=[pl.BlockSpec((tm, N), lambda j, i: (i, 0)),
                  pl.BlockSpec((None, tn, N), lambda j, i: (layer, j, 0))],
        out_specs=pl.BlockSpec((tm, tn), lambda j, i: (i, j)),
        out_shape=jax.ShapeDtypeStruct((T, R), out_dtype),
        compiler_params=_cparams("parallel", "parallel"),
    )(dy, w)


def _mm_tn(a, dy, *, shards, tk, tn, tt, name, comm=None):
    T, K = a.shape
    N = dy.shape[1]
    tn_dims = (((0,), (0,)), ((), ()))
    n4 = N if shards is None else N // shards
    span = max(tn // n4, 1)

    def body(a_ref, dy_ref, o_ref):
        @pl.when(pl.program_id(2) == 0)
        def _():
            o_ref[...] = jnp.zeros_like(o_ref)
        r = lax.dot_general(a_ref[...].astype(BF16), dy_ref[...].astype(BF16), tn_dims,
                            preferred_element_type=F32)
        if span == 1:
            o_ref[...] += r
        else:
            for q in range(span):
                o_ref[q] += r[:, q * n4:(q + 1) * n4]

    if shards is None:
        out_spec = pl.BlockSpec((tk, tn), lambda k, n, t: (k, n))
        out_shape = jax.ShapeDtypeStruct((K, N), F32)
    elif span > 1:
        out_spec = pl.BlockSpec((span, tk, n4), lambda k, n, t: (n, k, 0))
        out_shape = jax.ShapeDtypeStruct((shards, K, n4), F32)
    else:
        bps = n4 // tn
        out_spec = pl.BlockSpec((None, tk, tn), lambda k, n, t: (n // bps, k, n % bps))
        out_shape = jax.ShapeDtypeStruct((shards, K, n4), F32)
    return _pallas(
        comm, body, name=name, grid=(K // tk, N // tn, T // tt),
        in_specs=[pl.BlockSpec((tt, tk), lambda k, n, t: (t, k)),
                  pl.BlockSpec((tt, tn), lambda k, n, t: (t, n))],
        out_specs=out_spec, out_shape=out_shape,
        compiler_params=_cparams("parallel", "parallel", "arbitrary"),
    )(a, dy)


def _rmsnorm_bwd_math(xf, g, dh, dres):
    r = lax.rsqrt(jnp.mean(xf * xf, axis=-1, keepdims=True) + EPS)
    xh = xf * r
    dxh = dh * g
    dx = dres + r * (dxh - xh * jnp.mean(dxh * xh, axis=-1, keepdims=True))
    return dx, _rowsum(dh * xh)


def _mm_nt_norm(dy, w, x, g, dres, *, g_layer, tm, name, comm=None):
    T = dy.shape[0]
    _, S, K, n4 = w.shape
    nt_dims = (((1,), (1,)), ((), ()))

    def body(dy_ref, w_ref, x_ref, g_ref, dres_ref, dx_ref, dg_ref):
        @pl.when(pl.program_id(0) == 0)
        def _():
            dg_ref[...] = jnp.zeros_like(dg_ref)
        dh = None
        for s in range(S):
            part = lax.dot_general(dy_ref[:, s * n4:(s + 1) * n4].astype(BF16), w_ref[s], nt_dims,
                                   preferred_element_type=F32)
            dh = part if dh is None else dh + part
        dx, dg = _rmsnorm_bwd_math(x_ref[...], g_ref[...], dh, dres_ref[...])
        dx_ref[...] = dx
        dg_ref[...] += dg

    row = lambda i: (i, 0)
    return _pallas(
        comm, body, name=name, grid=(T // tm,),
        in_specs=[pl.BlockSpec((tm, S * n4), row),
                  pl.BlockSpec((None, S, K, n4), lambda i: (0, 0, 0, 0)),
                  pl.BlockSpec((tm, K), row),
                  pl.BlockSpec((None, 1, K), lambda i: (g_layer, 0, 0)),
                  pl.BlockSpec((tm, K), row)],
        out_specs=[pl.BlockSpec((tm, K), row), pl.BlockSpec((1, K), lambda i: (0, 0))],
        out_shape=[jax.ShapeDtypeStruct((T, K), F32), jax.ShapeDtypeStruct((1, K), F32)],
        compiler_params=_cparams("arbitrary"),
    )(dy, w, x, g, dres)


def _mm_nn_loss(a, w, residual, tgt, g, *, tm, name, comm=None):
    T, K = a.shape
    D = w.shape[2]

    def body(a_ref, w_ref, res_ref, t_ref, g_ref, loss_ref, dx_ref, dg_ref):
        @pl.when(pl.program_id(0) == 0)
        def _():
            dg_ref[...] = jnp.zeros_like(dg_ref)
            loss_ref[...] = jnp.zeros_like(loss_ref)
        xf = res_ref[...] + jnp.dot(a_ref[...], w_ref[...], preferred_element_type=F32)
        gg = g_ref[...]
        r = lax.rsqrt(jnp.mean(xf * xf, axis=-1, keepdims=True) + EPS)
        xh = xf * r
        err = xh * gg - t_ref[...]
        row = jnp.mean(err * err, axis=-1, keepdims=True)
        loss_ref[...] += 0.5 * jnp.sum(row, axis=0, keepdims=True)
        dy = err * (1.0 / D)
        dg_ref[...] += _rowsum(dy * xh)
        dxh = dy * gg
        dx_ref[...] = r * (dxh - xh * jnp.mean(dxh * xh, axis=-1, keepdims=True))

    row_spec = pl.BlockSpec((tm, D), lambda i: (i, 0))
    return _pallas(
        comm, body, name=name, grid=(T // tm,),
        in_specs=[pl.BlockSpec((tm, K), lambda i: (i, 0)), pl.BlockSpec((None, K, D), lambda i: (0, 0, 0)),
                  row_spec, row_spec, pl.BlockSpec((1, D), lambda i: (0, 0))],
        out_specs=[pl.BlockSpec((1, 1), lambda i: (0, 0)), row_spec, pl.BlockSpec((1, D), lambda i: (0, 0))],
        out_shape=[jax.ShapeDtypeStruct((1, 1), F32), jax.ShapeDtypeStruct((T, D), F32),
                   jax.ShapeDtypeStruct((1, D), F32)],
        compiler_params=_cparams("arbitrary"),
    )(a, w, residual, tgt, g)


CONV_ROWS = 64
CONV_COLS = 256


def _halo_prev_index(tm, halo):
    per = tm // halo
    return lambda i: jnp.maximum(i * per - 1, 0)


def _halo_next_index(tm, halo, total):
    per = tm // halo
    last = total // halo - 1
    return lambda i: jnp.minimum((i + 1) * per, last)


def _causal_mask():
    t = lax.broadcasted_iota(jnp.int32, (CHUNK, CHUNK), 0)
    s = lax.broadcasted_iota(jnp.int32, (CHUNK, CHUNK), 1)
    return s <= t


def _mixer_ab_fwd(z, a_ln_g, a_ln_b, w_s, b_s, conv_w, conv_b, b_ln_g, b_ln_b, *, tm, name, comm=None):
    T = z.shape[0]
    nchunk = tm // CHUNK
    halo = HALO_LONG

    def body(za_ref, zb_ref, zh_ref, alg_ref, alb_ref, ws_ref, bs_ref, cw_ref, cbias_ref,
             blg_ref, blb_ref, y_ref, cb_ref, ext_ref):
        i = pl.program_id(0)
        gu = _gelu(za_ref[:, :D_A].astype(F32))
        gv = _gelu(za_ref[:, D_A:].astype(F32))
        xh, _ = _ln_stats(gv)
        lv = (xh * alg_ref[...] + alb_ref[...]).astype(BF16)
        mask = _causal_mask()
        for h in range(A_HEADS):
            wm = jnp.where(mask, ws_ref[h], 0.0).astype(BF16)
            cols = slice(h * HEAD_DIM, (h + 1) * HEAD_DIM)
            for c in range(nchunk):
                rows = slice(c * CHUNK, (c + 1) * CHUNK)
                mixed = jnp.dot(wm, lv[rows, cols], preferred_element_type=F32) + bs_ref[h]
                y_ref[rows, cols] = (gu[rows, cols] * mixed).astype(BF16)
        ext_ref[halo:halo + tm, :] = zb_ref[:, :D_B].astype(F32) * _sigmoid(zb_ref[:, D_B:].astype(F32))
        prev = zh_ref[:, :D_B].astype(F32) * _sigmoid(zh_ref[:, D_B:].astype(F32))
        ext_ref[0:halo, :] = jnp.where(i > 0, prev, 0.0)
        for rb in range(tm // CONV_ROWS):
            for cb in range(D_B // CONV_COLS):
                cs = slice(cb * CONV_COLS, (cb + 1) * CONV_COLS)
                window = ext_ref[rb * CONV_ROWS:rb * CONV_ROWS + CONV_ROWS + halo, cs]
                acc = jnp.zeros((CONV_ROWS, CONV_COLS), F32)
                for k in range(B_CONV):
                    shifted = _rows_after(window, halo - (B_CONV - 1) + k)[:CONV_ROWS]
                    acc = acc + cw_ref[k:k + 1, cs] * shifted
                cb_ref[rb * CONV_ROWS:(rb + 1) * CONV_ROWS, cs] = acc + cbias_ref[:, cs]
        xhb, _ = _ln_stats(cb_ref[...])
        y_ref[:, D_A:] = _silu(xhb * blg_ref[...] + blb_ref[...]).astype(BF16)

    row = lambda i: (i, 0)
    par = lambda i: (0, 0)
    return _pallas(
        comm, body, name=name, grid=(T // tm,),
        in_specs=[pl.BlockSpec((tm, 2 * D_A), lambda i: (i, 0)),
                  pl.BlockSpec((tm, 2 * D_B), lambda i: (i, 1)),
                  pl.BlockSpec((halo, 2 * D_B), lambda i: (_halo_prev_index(tm, halo)(i), 1)),
                  pl.BlockSpec((1, D_A), par), pl.BlockSpec((1, D_A), par),
                  pl.BlockSpec((A_HEADS, CHUNK, CHUNK), lambda i: (0, 0, 0)),
                  pl.BlockSpec((A_HEADS, CHUNK, 1), lambda i: (0, 0, 0)),
                  pl.BlockSpec((B_CONV, D_B), par), pl.BlockSpec((1, D_B), par),
                  pl.BlockSpec((1, D_B), par), pl.BlockSpec((1, D_B), par)],
        out_specs=[pl.BlockSpec((tm, D_A + D_B), row), pl.BlockSpec((tm, D_B), row)],
        out_shape=[jax.ShapeDtypeStruct((T, D_A + D_B), BF16), jax.ShapeDtypeStruct((T, D_B), F32)],
        scratch_shapes=[pltpu.VMEM((halo + tm, D_B), F32)],
        compiler_params=_cparams("parallel"),
    )(z, z, z, a_ln_g, a_ln_b, w_s, b_s, conv_w, conv_b, b_ln_g, b_ln_b)


def _mixer_ab_bwd_pre(z, cb, dy, a_ln_g, a_ln_b, w_s, b_s, b_ln_g, b_ln_b, *, tm, name, comm=None):
    T = z.shape[0]
    nchunk = tm // CHUNK
    tn_dims = (((0,), (0,)), ((), ()))
    nt_dims = (((1,), (1,)), ((), ()))

    def body(za_ref, cb_ref, dy_ref, alg_ref, alb_ref, ws_ref, bs_ref, blg_ref, blb_ref,
             dza_ref, dcb_ref, dalg_ref, dalb_ref, dws_ref, dbs_ref, dblg_ref, dblb_ref,
             dlv_ref):
        @pl.when(pl.program_id(0) == 0)
        def _():
            for ref in (dalg_ref, dalb_ref, dws_ref, dbs_ref, dblg_ref, dblb_ref):
                ref[...] = jnp.zeros_like(ref)
        ua = za_ref[:, :D_A].astype(F32)
        va = za_ref[:, D_A:].astype(F32)
        gu = _gelu(ua)
        gv = _gelu(va)
        xh, r = _ln_stats(gv)
        alg = alg_ref[...]
        lv = (xh * alg + alb_ref[...]).astype(BF16)
        dya = dy_ref[:, :D_A].astype(F32)
        mask = _causal_mask()
        for h in range(A_HEADS):
            wm = jnp.where(mask, ws_ref[h], 0.0).astype(BF16)
            cols = slice(h * HEAD_DIM, (h + 1) * HEAD_DIM)
            dwm = jnp.zeros((CHUNK, CHUNK), F32)
            dbs = jnp.zeros((CHUNK, 1), F32)
            for c in range(nchunk):
                rows = slice(c * CHUNK, (c + 1) * CHUNK)
                lvb = lv[rows, cols]
                mixed = jnp.dot(wm, lvb, preferred_element_type=F32) + bs_ref[h]
                dyb = dya[rows, cols]
                dza_ref[rows, cols] = (dyb * mixed * _dgelu(ua[rows, cols])).astype(BF16)
                dmixed = dyb * gu[rows, cols]
                dmb = dmixed.astype(BF16)
                dlv_ref[rows, cols] = lax.dot_general(wm, dmb, tn_dims, preferred_element_type=F32)
                dwm = dwm + lax.dot_general(dmb, lvb, nt_dims, preferred_element_type=F32)
                dbs = dbs + jnp.sum(dmixed, axis=1, keepdims=True)
            dws_ref[h] += jnp.where(mask, dwm, 0.0)
            dbs_ref[h] += dbs
        dlv = dlv_ref[...]
        dalg_ref[...] += _rowsum(dlv * xh)
        dalb_ref[...] += _rowsum(dlv)
        dgv = _ln_bwd(dlv, xh, r, alg)
        dza_ref[:, D_A:] = (dgv * _dgelu(va)).astype(BF16)
        xhb, rb = _ln_stats(cb_ref[...])
        blg = blg_ref[...]
        lb = xhb * blg + blb_ref[...]
        dlb = dy_ref[:, D_A:].astype(F32) * _dsilu(lb)
        dblg_ref[...] += _rowsum(dlb * xhb)
        dblb_ref[...] += _rowsum(dlb)
        dcb_ref[...] = _ln_bwd(dlb, xhb, rb, blg)

    row = lambda i: (i, 0)
    par = lambda i: (0, 0)
    par3 = lambda i: (0, 0, 0)
    return _pallas(
        comm, body, name=name, grid=(T // tm,),
        in_specs=[pl.BlockSpec((tm, 2 * D_A), row), pl.BlockSpec((tm, D_B), row),
                  pl.BlockSpec((tm, D_A + D_B), row),
                  pl.BlockSpec((1, D_A), par), pl.BlockSpec((1, D_A), par),
                  pl.BlockSpec((A_HEADS, CHUNK, CHUNK), par3),
                  pl.BlockSpec((A_HEADS, CHUNK, 1), par3),
                  pl.BlockSpec((1, D_B), par), pl.BlockSpec((1, D_B), par)],
        out_specs=[pl.BlockSpec((tm, 2 * D_A), row), pl.BlockSpec((tm, D_B), row),
                   pl.BlockSpec((1, D_A), par), pl.BlockSpec((1, D_A), par),
                   pl.BlockSpec((A_HEADS, CHUNK, CHUNK), par3),
                   pl.BlockSpec((A_HEADS, CHUNK, 1), par3),
                   pl.BlockSpec((1, D_B), par), pl.BlockSpec((1, D_B), par)],
        out_shape=[jax.ShapeDtypeStruct((T, 2 * D_A + 2 * D_B), BF16), jax.ShapeDtypeStruct((T, D_B), F32),
                   jax.ShapeDtypeStruct((1, D_A), F32), jax.ShapeDtypeStruct((1, D_A), F32),
                   jax.ShapeDtypeStruct((A_HEADS, CHUNK, CHUNK), F32),
                   jax.ShapeDtypeStruct((A_HEADS, CHUNK, 1), F32),
                   jax.ShapeDtypeStruct((1, D_B), F32), jax.ShapeDtypeStruct((1, D_B), F32)],
        scratch_shapes=[pltpu.VMEM((tm, D_A), F32)],
        compiler_params=_cparams("arbitrary"),
    )(z, cb, dy, a_ln_g, a_ln_b, w_s, b_s, b_ln_g, b_ln_b)


def _mixer_b_conv_bwd(z, dcb, conv_w, dz, *, tm, name, comm=None):
    T = z.shape[0]
    halo = HALO_LONG

    def body(zb_ref, dcb_ref, dcn_ref, cw_ref, dz_in_ref, dzb_ref, dcw_ref, dbias_ref, dext_ref):
        i = pl.program_id(0)
        last = pl.num_programs(0) - 1

        @pl.when(i == 0)
        def _():
            dcw_ref[...] = jnp.zeros_like(dcw_ref)
            dbias_ref[...] = jnp.zeros_like(dbias_ref)
        dcb = dcb_ref[...]
        dext_ref[0:tm, :] = dcb
        dext_ref[tm:tm + halo, :] = jnp.where(i < last, dcn_ref[...], 0.0)
        dbias_ref[...] += _rowsum(dcb)
        for rb in range(tm // CONV_ROWS):
            for cb in range(D_B // CONV_COLS):
                cs = slice(cb * CONV_COLS, (cb + 1) * CONV_COLS)
                gcs = slice(D_B + cb * CONV_COLS, D_B + (cb + 1) * CONV_COLS)
                rs = slice(rb * CONV_ROWS, (rb + 1) * CONV_ROWS)
                xbb = zb_ref[rs, cs].astype(F32)
                sgb = _sigmoid(zb_ref[rs, gcs].astype(F32))
                yb0 = xbb * sgb
                window = dext_ref[rb * CONV_ROWS:rb * CONV_ROWS + CONV_ROWS + halo, cs]
                acc = jnp.zeros((CONV_ROWS, CONV_COLS), F32)
                for k in range(B_CONV):
                    shifted = _rows_after(window, (B_CONV - 1) - k)[:CONV_ROWS]
                    acc = acc + cw_ref[k:k + 1, cs] * shifted
                    dcw_ref[k:k + 1, cs] += _rowsum(shifted * yb0)
                dzb_ref[rs, cs] = (acc * sgb).astype(BF16)
                dzb_ref[rs, gcs] = (acc * xbb * sgb * (1.0 - sgb)).astype(BF16)

    row = lambda i: (i, 0)
    par = lambda i: (0, 0)
    return _pallas(
        comm, body, name=name, grid=(T // tm,),
        in_specs=[pl.BlockSpec((tm, 2 * D_B), lambda i: (i, 1)),
                  pl.BlockSpec((tm, D_B), row),
                  pl.BlockSpec((halo, D_B), lambda i: (_halo_next_index(tm, halo, T)(i), 0)),
                  pl.BlockSpec((B_CONV, D_B), par), pl.BlockSpec(memory_space=pl.ANY)],
        out_specs=[pl.BlockSpec((tm, 2 * D_B), lambda i: (i, 1)), pl.BlockSpec((B_CONV, D_B), par),
                   pl.BlockSpec((1, D_B), par)],
        out_shape=[jax.ShapeDtypeStruct(dz.shape, BF16), jax.ShapeDtypeStruct((B_CONV, D_B), F32),
                   jax.ShapeDtypeStruct((1, D_B), F32)],
        scratch_shapes=[pltpu.VMEM((tm + halo, D_B), F32)], aliases={4: 0},
        compiler_params=_cparams("arbitrary"),
    )(z, dcb, dcb, conv_w, dz)


def _rows_before(x, a):
    return x if a == 0 else pltpu.roll(x, a, axis=0)


def _rows_after(x, a):
    return x if a == 0 else pltpu.roll(x, x.shape[0] - a, axis=0)


def _conv3(w_ref, x, halo, cs):
    acc = w_ref[2:3, cs] * x[halo:]
    acc = acc + w_ref[1:2, cs] * _rows_before(x, 1)[halo:]
    return acc + w_ref[0:1, cs] * _rows_before(x, 2)[halo:]


def _mixer_c_fwd(z, conv_w, *, tm, name, comm=None):
    T = z.shape[0]
    D = D_MODEL
    halo = HALO_SHORT
    W = CONV_COLS

    def body(bg_ref, cg_ref, xv_ref, cgh_ref, xvh_ref, w_ref, r_ref):
        i = pl.program_id(0)
        for cb in range(D // W):
            cs = slice(cb * W, (cb + 1) * W)
            prev = jnp.where(i > 0, cgh_ref[:, cs].astype(F32) * xvh_ref[:, cs].astype(F32), 0.0)
            p = jnp.concatenate([prev, cg_ref[:, cs].astype(F32) * xv_ref[:, cs].astype(F32)], axis=0)
            r_ref[:, cs] = (bg_ref[:, cs].astype(F32) * _conv3(w_ref, p, halo, cs)).astype(BF16)

    hp = _halo_prev_index(tm, halo)
    return _pallas(
        comm, body, name=name, grid=(T // tm,),
        in_specs=[pl.BlockSpec((tm, D), lambda i: (i, 0)), pl.BlockSpec((tm, D), lambda i: (i, 1)),
                  pl.BlockSpec((tm, D), lambda i: (i, 2)),
                  pl.BlockSpec((halo, D), lambda i: (hp(i), 1)),
                  pl.BlockSpec((halo, D), lambda i: (hp(i), 2)),
                  pl.BlockSpec((None, C_CONV, D), lambda i: (0, 0, 0))],
        out_specs=pl.BlockSpec((tm, D), lambda i: (i, 0)),
        out_shape=jax.ShapeDtypeStruct((T, D), BF16),
        compiler_params=_cparams("parallel"),
    )(z, z, z, z, z, conv_w)


def _mixer_c_bwd(z, dr, conv_w, *, tm, name, comm=None):
    T = z.shape[0]
    D = D_MODEL
    halo = HALO_SHORT
    W = CONV_COLS

    def body(bg_ref, cg_ref, xv_ref, cgh_ref, xvh_ref, bgn_ref, dr_ref, drn_ref, w_ref, dz_ref, dw_ref):
        i = pl.program_id(0)
        last = pl.num_programs(0) - 1

        @pl.when(i == 0)
        def _():
            dw_ref[...] = jnp.zeros_like(dw_ref)
        for cb in range(D // W):
            cs = slice(cb * W, (cb + 1) * W)
            cg = cg_ref[:, cs].astype(F32)
            xv = xv_ref[:, cs].astype(F32)
            dr = dr_ref[:, cs].astype(F32)
            p = cg * xv
            prev = jnp.where(i > 0, cgh_ref[:, cs].astype(F32) * xvh_ref[:, cs].astype(F32), 0.0)
            q = _conv3(w_ref, jnp.concatenate([prev, p], axis=0), halo, cs)
            dz_ref[:, cs] = (dr * q).astype(BF16)
            nxt = jnp.where(i < last, drn_ref[:, cs].astype(F32) * bgn_ref[:, cs].astype(F32), 0.0)
            dq = jnp.concatenate([dr * bg_ref[:, cs].astype(F32), nxt], axis=0)
            dp = None
            for k in range(C_CONV):
                shifted = _rows_after(dq, 2 - k)[:tm]
                term = w_ref[k:k + 1, cs] * shifted
                dp = term if dp is None else dp + term
                dw_ref[k:k + 1, cs] += _rowsum(shifted * p)
            dz_ref[:, D + cb * W:D + (cb + 1) * W] = (dp * xv).astype(BF16)
            dz_ref[:, 2 * D + cb * W:2 * D + (cb + 1) * W] = (dp * cg).astype(BF16)

    hp = _halo_prev_index(tm, halo)
    hn = _halo_next_index(tm, halo, T)
    return _pallas(
        comm, body, name=name, grid=(T // tm,),
        in_specs=[pl.BlockSpec((tm, D), lambda i: (i, 0)), pl.BlockSpec((tm, D), lambda i: (i, 1)),
                  pl.BlockSpec((tm, D), lambda i: (i, 2)),
                  pl.BlockSpec((halo, D), lambda i: (hp(i), 1)),
                  pl.BlockSpec((halo, D), lambda i: (hp(i), 2)),
                  pl.BlockSpec((halo, D), lambda i: (hn(i), 0)),
                  pl.BlockSpec((tm, D), lambda i: (i, 0)),
                  pl.BlockSpec((halo, D), lambda i: (hn(i), 0)),
                  pl.BlockSpec((None, C_CONV, D), lambda i: (0, 0, 0))],
        out_specs=[pl.BlockSpec((tm, 3 * D), lambda i: (i, 0)),
                   pl.BlockSpec((C_CONV, D), lambda i: (0, 0))],
        out_shape=[jax.ShapeDtypeStruct((T, 3 * D), BF16), jax.ShapeDtypeStruct((C_CONV, D), F32)],
        compiler_params=_cparams("arbitrary"),
    )(z, z, z, z, z, z, dr, dr, conv_w)


FFN_COLS = 128


def _ffn_act_fwd(up, conv_w, *, layer, tm, name, comm=None):
    T = up.shape[0]
    halo = HALO_SHORT
    W = FFN_COLS

    def body(up_ref, uph_ref, w_ref, a_ref, upc_ref):
        i = pl.program_id(0)

        def conv(cs):
            prev = jnp.where(i > 0, uph_ref[:, cs], jnp.zeros((halo, W), BF16))
            return _conv3(w_ref, jnp.concatenate([prev, up_ref[:, cs]], axis=0).astype(F32), halo, cs)

        for cb in range(D_FF // W):
            gs = slice(cb * W, (cb + 1) * W)
            vs = slice(D_FF + cb * W, D_FF + (cb + 1) * W)
            g = conv(gs)
            v = conv(vs)
            upc_ref[:, gs] = g.astype(BF16)
            upc_ref[:, vs] = v.astype(BF16)
            a_ref[:, gs] = (_silu(g) * v).astype(BF16)

    return _pallas(
        comm, body, name=name, grid=(T // tm,),
        in_specs=[pl.BlockSpec((tm, 2 * D_FF), lambda i: (i, 0)),
                  pl.BlockSpec((halo, 2 * D_FF), lambda i: (_halo_prev_index(tm, halo)(i), 0)),
                  pl.BlockSpec((None, F_CONV, 2 * D_FF), lambda i: (layer, 0, 0))],
        out_specs=[pl.BlockSpec((tm, D_FF), lambda i: (i, 0)),
                   pl.BlockSpec((tm, 2 * D_FF), lambda i: (i, 0))],
        out_shape=[jax.ShapeDtypeStruct((T, D_FF), BF16), jax.ShapeDtypeStruct((T, 2 * D_FF), BF16)],
        compiler_params=_cparams("parallel"),
    )(up, up, conv_w)


def _ffn_act_bwd(up, upc, da, conv_w, *, layer, tm, name, comm=None):
    T = up.shape[0]
    halo = HALO_SHORT
    W = FFN_COLS

    def body(up_ref, upc_ref, upcn_ref, da_ref, dan_ref, w_ref, dup_ref, dw_ref):
        i = pl.program_id(0)
        last = pl.num_programs(0) - 1

        @pl.when(i == 0)
        def _():
            dw_ref[...] = jnp.zeros_like(dw_ref)
        live = jnp.where(i < last, 1.0, 0.0)
        for cb in range(D_FF // W):
            gs = slice(cb * W, (cb + 1) * W)
            vs = slice(D_FF + cb * W, D_FF + (cb + 1) * W)
            g = jnp.concatenate([upc_ref[:, gs], upcn_ref[:, gs]], axis=0).astype(F32)
            v = jnp.concatenate([upc_ref[:, vs], upcn_ref[:, vs]], axis=0).astype(F32)
            da = jnp.concatenate([da_ref[:, gs].astype(F32), dan_ref[:, gs].astype(F32) * live], axis=0)
            s = _sigmoid(g)
            silu = g * s
            grads = (da * v * (s * (1.0 + g * (1.0 - s))), da * silu)
            for cs, d in zip((gs, vs), grads):
                u = up_ref[:, cs].astype(F32)
                acc = None
                for k in range(F_CONV):
                    shifted = _rows_after(d, 2 - k)[:tm]
                    term = w_ref[k:k + 1, cs] * shifted
                    acc = term if acc is None else acc + term
                    dw_ref[k:k + 1, cs] += _rowsum(shifted * u)
                dup_ref[:, cs] = acc.astype(BF16)

    hn = _halo_next_index(tm, halo, T)
    return _pallas(
        comm, body, name=name, grid=(T // tm,),
        in_specs=[pl.BlockSpec((tm, 2 * D_FF), lambda i: (i, 0)),
                  pl.BlockSpec((tm, 2 * D_FF), lambda i: (i, 0)),
                  pl.BlockSpec((halo, 2 * D_FF), lambda i: (hn(i), 0)),
                  pl.BlockSpec((tm, D_FF), lambda i: (i, 0)),
                  pl.BlockSpec((halo, D_FF), lambda i: (hn(i), 0)),
                  pl.BlockSpec((None, F_CONV, 2 * D_FF), lambda i: (layer, 0, 0))],
        out_specs=[pl.BlockSpec((tm, 2 * D_FF), lambda i: (i, 0)),
                   pl.BlockSpec((F_CONV, 2 * D_FF), lambda i: (0, 0))],
        out_shape=[jax.ShapeDtypeStruct((T, 2 * D_FF), BF16),
                   jax.ShapeDtypeStruct((F_CONV, 2 * D_FF), F32)],
        compiler_params=_cparams("arbitrary"),
    )(up, upc, upc, da, da, conv_w)


def _local_step(x, tgt, small, plan):
    T = x.shape[0]
    tm_e = _pick(T, 256)
    tm_a = _pick(T, 512)
    tm_b = _pick(T, 128)
    tm = _pick(T, 1024)
    tm_f = _pick(T, 512)
    tt = _pick(T, 2048)
    nm = small["norm_mix"].reshape(2, 1, D_MODEL)
    nf = small["norm_ffn"].reshape(2, 1, D_MODEL)
    ngf = small["norm_final"].reshape(1, D_MODEL)
    b_s = small["a_b_s"].reshape(A_HEADS, CHUNK, 1)
    w_s = small["a_w_s"].reshape(A_HEADS, CHUNK, CHUNK)
    b_conv_w = small["b_conv_w"].reshape(B_CONV, D_B)
    sg = {}
    wt, cm = plan.weight, plan.comm

    h_m0, z_ab = _norm_mm_nn(x, nm, wt("ab_w_in", 0), g_layer=0, tm=tm, tn=512, name="ab_in", comm=cm("ab_in"))
    yab, cb = _mixer_ab_fwd(z_ab, small["a_ln_g"], small["a_ln_b"], w_s, b_s, b_conv_w, small["b_conv_b"],
                            small["b_ln_g"], small["b_ln_b"], tm=tm_e, name="mixer_ab", comm=cm("mixer_ab"))
    x1, h_f0 = _mm_nn(yab, wt("ab_w_out", 0), layer=0, tm=tm, tn=D_MODEL, residual=x, norm=(nf, 0),
                      name="ab_out", comm=cm("ab_out"))

    def ffn_fwd(xin, h, layer):
        up = _mm_nn(h, wt("f_w_up", layer), layer=0, tm=tm, tn=2 * 1408, out_dtype=BF16, name=f"ffn_up{layer}",
                    comm=cm(f"ffn_up{layer}"))
        a, upc = _ffn_act_fwd(up, small["f_conv_w"], layer=layer, tm=tm_a, name=f"ffn_act{layer}",
                              comm=cm(f"ffn_act{layer}"))
        if layer == 0:
            out = _mm_nn(a, wt("f_w_down", layer), layer=0, tm=tm, tn=D_MODEL, residual=xin, norm=(nm, 1),
                         name=f"ffn_down{layer}", comm=cm(f"ffn_down{layer}"))
        else:
            out = _mm_nn_loss(a, wt("f_w_down", layer), xin, tgt, ngf, tm=tm, name=f"ffn_down{layer}",
                              comm=cm(f"ffn_down{layer}"))
        return up, upc, a, out

    up0, upc0, a0, (x2, h_m1) = ffn_fwd(x1, h_f0, 0)
    z_c = _mm_nn(h_m1, wt("c_w_in", 0), layer=0, tm=tm, tn=768, out_dtype=BF16, name="c_in", comm=cm("c_in"))
    r = _mixer_c_fwd(z_c, small["c_conv_w"], tm=tm_e, name="mixer_c", comm=cm("mixer_c"))
    x3, h_f1 = _mm_nn(r, wt("c_w_out", 0), layer=0, tm=tm, tn=D_MODEL, residual=x2, norm=(nf, 1),
                      name="c_out", comm=cm("c_out"))
    up1, upc1, a1, (loss, dx, sg["norm_final"]) = ffn_fwd(x3, h_f1, 1)

    def ffn_bwd(dx, xin, h, up, upc, a, layer):
        da = _mm_nt(dx, wt("f_w_down", layer), layer=0, tm=tm, tn=1408, out_dtype=BF16,
                    name=f"ffn_down_dx{layer}", comm=cm(f"ffn_down_dx{layer}"))
        plan.grad_ready("f_w_down", layer, _mm_tn(a, dx, shards=None, tk=1408, tn=1024, tt=tt,
                                                  name=f"ffn_down_dw{layer}", comm=cm(f"ffn_down_dw{layer}")))
        dup, dcw = _ffn_act_bwd(up, upc, da, small["f_conv_w"], layer=layer, tm=tm_b, name=f"ffn_act_bwd{layer}",
                                comm=cm(f"ffn_act_bwd{layer}"))
        plan.grad_ready("f_w_up", layer, _mm_tn(h, dup, shards=N_CHIPS, tk=512, tn=2 * 1408, tt=tt,
                                                name=f"ffn_up_dw{layer}", comm=cm(f"ffn_up_dw{layer}")))
        dxin, dg = _mm_nt_norm(dup, wt("f_w_up", layer), xin, nf, dx, g_layer=layer, tm=tm_f,
                               name=f"ffn_up_dx{layer}", comm=cm(f"ffn_up_dx{layer}"))
        return dxin, dg, dcw

    dx, dnf1, dfc1 = ffn_bwd(dx, x3, h_f1, up1, upc1, a1, 1)
    dr = _mm_nt(dx, wt("c_w_out", 0), layer=0, tm=tm, tn=512, out_dtype=BF16, name="c_out_dx", comm=cm("c_out_dx"))
    plan.grad_ready("c_w_out", 0, _mm_tn(r, dx, shards=None, tk=1024, tn=1024, tt=tt, name="c_out_dw",
                                         comm=cm("c_out_dw")))
    dz_c, dccw = _mixer_c_bwd(z_c, dr, small["c_conv_w"], tm=tm_e, name="mixer_c_bwd", comm=cm("mixer_c_bwd"))
    sg["c_conv_w"] = dccw.reshape(1, C_CONV, D_MODEL)
    plan.grad_ready("c_w_in", 0, _mm_tn(h_m1, dz_c, shards=N_CHIPS, tk=1024, tn=768, tt=tt, name="c_in_dw",
                                        comm=cm("c_in_dw")))
    dx, dnm1 = _mm_nt_norm(dz_c, wt("c_w_in", 0), x2, nm, dx, g_layer=1, tm=tm_f, name="c_in_dx",
                           comm=cm("c_in_dx"))
    dx, dnf0, dfc0 = ffn_bwd(dx, x1, h_f0, up0, upc0, a0, 0)
    dyab = _mm_nt(dx, wt("ab_w_out", 0), layer=0, tm=tm, tn=512, out_dtype=BF16, name="ab_out_dx",
                  comm=cm("ab_out_dx"))
    plan.grad_ready("ab_w_out", 0, _mm_tn(yab, dx, shards=None, tk=1024, tn=1024, tt=tt, name="ab_out_dw",
                                          comm=cm("ab_out_dw")))
    (dza, dcb, sg["a_ln_g"], sg["a_ln_b"], dws, dbs, sg["b_ln_g"], sg["b_ln_b"]) = _mixer_ab_bwd_pre(
        z_ab, cb, dyab, small["a_ln_g"], small["a_ln_b"], w_s, b_s, small["b_ln_g"], small["b_ln_b"],
        tm=tm_e, name="mixer_ab_bwd", comm=cm("mixer_ab_bwd"))
    dz_ab, dbcw, sg["b_conv_b"] = _mixer_b_conv_bwd(z_ab, dcb, b_conv_w, dza, tm=tm_e, name="mixer_b_conv_bwd",
                                                    comm=cm("mixer_b_conv_bwd"))
    sg["a_w_s"] = dws.reshape(1, A_HEADS, CHUNK, CHUNK)
    sg["a_b_s"] = dbs.reshape(1, A_HEADS, CHUNK)
    sg["b_conv_w"] = dbcw.reshape(1, B_CONV, D_B)
    plan.grad_ready("ab_w_in", 0, _mm_tn(h_m0, dz_ab, shards=N_CHIPS, tk=1024, tn=512, tt=tt, name="ab_in_dw",
                                         comm=cm("ab_in_dw")))
    dx, dnm0 = _mm_nt_norm(dz_ab, wt("ab_w_in", 0), x, nm, dx, g_layer=0, tm=tm_f, name="ab_in_dx",
                           comm=cm("ab_in_dx"))

    sg["norm_mix"] = [dnm0, dnm1]
    sg["norm_ffn"] = [dnf0, dnf1]
    sg["f_conv_w"] = [dfc0, dfc1]
    return loss, dx, sg


BLOCK_BYTES = 3 * 1024 * 1024


BF16_SUBLANES = 16


def _row_tile(rows, row_bytes, step=SUBLANES):
    best = None
    for tr in range(step, rows + 1, step):
        if rows % tr == 0 and tr * row_bytes <= BLOCK_BYTES:
            best = tr
    if best is None:
        raise ValueError(f"no row tile for {rows}")
    return best


def _place_scalars():
    x, y, c = lax.axis_index("x"), lax.axis_index("y"), lax.axis_index("c")
    return jnp.stack([c, 2 * x + y, 2 * (1 - x) + y, 2 * x + (1 - y), 2 * (1 - x) + (1 - y)]).astype(jnp.int32)


def _cast_into_slot(w, place, *, layer, paired, name):
    L, rows, cols = w.shape
    tr = _row_tile(rows, cols * 4, BF16_SUBLANES)

    def body(place_ref, w_ref, o_ref):
        o_ref[...] = w_ref[...].astype(BF16)

    if paired:
        out_spec = pl.BlockSpec((None, None, tr, cols), lambda i, p: (0, p[1] // 2, i, p[1] % 2))
        out_shape = jax.ShapeDtypeStruct((1, N_CHIPS // 2, rows, 2 * cols), BF16)
    else:
        out_spec = pl.BlockSpec((None, None, tr, cols), lambda i, p: (0, p[1], i, 0))
        out_shape = jax.ShapeDtypeStruct((1, N_CHIPS, rows, cols), BF16)
    return pl.pallas_call(
        body, name=name,
        grid_spec=pltpu.PrefetchScalarGridSpec(
            num_scalar_prefetch=1, grid=(rows // tr,),
            in_specs=[pl.BlockSpec((None, tr, cols), lambda i, p: (layer, i, 0))],
            out_specs=out_spec),
        out_shape=out_shape,
        compiler_params=_cparams("parallel"),
    )(place, w)


def _pair_sum(g, theirs, place, *, name):
    S, rows, cols = g.shape
    half = rows // 2
    tr = _row_tile(half, cols * 4, BF16_SUBLANES)
    nb = half // tr

    def body(place_ref, g_ref, t_ref, o_ref):
        o_ref[...] = (g_ref[...] + t_ref[...]).astype(BF16)

    spec = pl.BlockSpec((None, tr, cols), lambda s, i, p: (s, i, 0))
    return pl.pallas_call(
        body, name=name,
        grid_spec=pltpu.PrefetchScalarGridSpec(
            num_scalar_prefetch=1, grid=(S, nb),
            in_specs=[pl.BlockSpec((None, tr, cols), lambda s, i, p: (s, p[0] * nb + i, 0)), spec],
            out_specs=spec),
        out_shape=jax.ShapeDtypeStruct((S, half, cols), BF16),
        compiler_params=_cparams("parallel", "parallel"),
    )(place, g, theirs)


def _chip_sum(p, r, g_prev, place, *, layer, shape, name):
    L, rows, cols = shape
    half = rows // 2
    tr = _row_tile(half, cols * 4, BF16_SUBLANES)
    nb = half // tr

    def body(place_ref, p_ref, r_ref, *rest):
        o_ref = rest[-1]
        mine = p_ref[...].astype(F32)
        peers = [r_ref[j].astype(F32) for j in range(3)]
        acc = None
        for s in range(N_CHIPS):
            term = jnp.where(place_ref[1] == s, mine,
                             jnp.where(place_ref[2] == s, peers[0],
                                       jnp.where(place_ref[3] == s, peers[1], peers[2])))
            acc = term if acc is None else acc + term
        o_ref[...] = acc

    in_specs = [pl.BlockSpec((None, tr, cols), lambda i, pr: (pr[1], i, 0)),
                pl.BlockSpec((3, tr, cols), lambda i, pr: (0, i, 0))]
    args = [place, p, r]
    aliases = {}
    if g_prev is not None:
        in_specs.append(HBM_REF)
        args.append(g_prev)
        aliases = {3: 0}
    return pl.pallas_call(
        body, name=name,
        grid_spec=pltpu.PrefetchScalarGridSpec(
            num_scalar_prefetch=1, grid=(nb,), in_specs=in_specs,
            out_specs=pl.BlockSpec((None, tr, cols), lambda i, pr: (layer, pr[0] * nb + i, 0))),
        out_shape=jax.ShapeDtypeStruct(shape, F32), input_output_aliases=aliases,
        compiler_params=_cparams("parallel"),
    )(*args)


def _adamw_math(w, g, m, v):
    m2 = ADAM_B1 * m + (1.0 - ADAM_B1) * g
    v2 = ADAM_B2 * v + (1.0 - ADAM_B2) * (g * g)
    m_hat = m2 / (1.0 - ADAM_B1 ** ADAM_STEP)
    v_hat = v2 / (1.0 - ADAM_B2 ** ADAM_STEP)
    delta = -ADAM_LR * (m_hat / (jnp.sqrt(v_hat) + ADAM_EPS) + ADAM_WD * w)
    return delta, m2, v2


def _adamw(w, g, m, v, *, name):
    L, rows, cols = w.shape
    tr = _row_tile(rows, cols * 4)

    def body(w_ref, g_ref, m_ref, v_ref, go_ref, d_ref, m2_ref, v2_ref):
        g = g_ref[...]
        d, m2, v2 = _adamw_math(w_ref[...], g, m_ref[...], v_ref[...])
        go_ref[...] = g
        d_ref[...] = d
        m2_ref[...] = m2
        v2_ref[...] = v2

    spec = pl.BlockSpec((None, tr, cols), lambda l, i: (l, i, 0))
    shape = jax.ShapeDtypeStruct(w.shape, F32)
    return pl.pallas_call(
        body, name=name, grid=(L, rows // tr), in_specs=[spec] * 4, out_specs=[spec] * 4,
        out_shape=[shape] * 4,
        compiler_params=_cparams("parallel", "parallel"),
    )(w, g, m, v)


def _allreduce_pack(pack, *, name, comm):
    R = pack.shape[0]
    half = R // 2
    nr, nw = len(comm.reads), len(comm.writes)

    def body(*refs):
        p_ref, rd, wr_in = refs[0], refs[1:1 + nr], refs[1 + nr:1 + nr + nw]
        o_ref, wr_out = refs[1 + nr + nw], refs[2 + nr + nw:2 + nr + 2 * nw]
        sib_ref, chip_ref, parts_ref, sems, comm_sems = refs[2 + nr + 2 * nw:]
        src = dict(zip(comm.reads, rd))
        src.update(zip(comm.writes, wr_in))
        dst = dict(zip(comm.writes, wr_out))
        comm.start(src, dst, comm_sems)
        x, y, c, k, sib, peers = _place()
        swap = _remote(p_ref, sib_ref, sems.at[0, 0], sems.at[0, 1], sib)
        swap.start()
        swap.wait()
        chip_ref[...] = p_ref[...] + sib_ref[...]
        mine = chip_ref.at[pl.ds(pl.multiple_of(c * half, SUBLANES), half)]
        sends = [_remote(mine, parts_ref.at[j], sems.at[1 + j, 0], sems.at[1 + j, 1], (px, py, c))
                 for j, (px, py) in enumerate(peers)]
        for rc in sends:
            rc.start()
        for rc in sends:
            rc.wait()
        own = mine[...]
        others = [parts_ref[j] for j in range(3)]
        acc = None
        for s in range(N_CHIPS):
            term = own
            for j, (px, py) in enumerate(peers):
                term = jnp.where(2 * px + py == s, others[j], term)
            acc = term if acc is None else acc + term
        done = o_ref.at[pl.ds(pl.multiple_of(c * half, SUBLANES), half)]
        done[...] = acc
        theirs = o_ref.at[pl.ds(pl.multiple_of((1 - c) * half, SUBLANES), half)]
        share = _remote(done, done, sems.at[4, 0], sems.at[4, 1], sib)
        share.start()
        _remote(done, theirs, sems.at[4, 0], sems.at[4, 1], sib).wait()
        comm.finish(src, dst, comm_sems)

    vm = pl.BlockSpec(memory_space=pltpu.VMEM)
    operands, shapes = _comm_operands(comm)
    outs = pl.pallas_call(
        body, name=name, in_specs=[vm] + [HBM_REF] * (nr + nw), out_specs=[vm] + [HBM_REF] * nw,
        out_shape=[jax.ShapeDtypeStruct((R, LANES), F32)] + shapes,
        input_output_aliases={1 + nr + q: 1 + q for q in range(nw)},
        scratch_shapes=[pltpu.VMEM((R, LANES), F32), pltpu.VMEM((R, LANES), F32),
                        pltpu.VMEM((3, half, LANES), F32), pltpu.SemaphoreType.DMA((5, 2)),
                        pltpu.SemaphoreType.DMA((comm.ncopies, 2))],
        compiler_params=pltpu.CompilerParams(vmem_limit_bytes=VMEM_BYTES_MAX),
    )(pack, *operands)
    for q, n in enumerate(comm.writes):
        comm.plan.bufs[n] = outs[1 + q]
    return outs[0]


PACK_UNIT = SUBLANES * LANES


def _pack(arrays):
    flat, sizes = [], []
    for a in arrays:
        pieces = a if isinstance(a, (list, tuple)) else [a]
        v = jnp.concatenate([p.reshape(-1) for p in pieces]) if len(pieces) > 1 else pieces[0].reshape(-1)
        size = v.shape[0]
        padded = -(-size // PACK_UNIT) * PACK_UNIT
        flat.append(jnp.pad(v, (0, padded - size)))
        sizes.append((size, padded))
    total = sum(p for _, p in sizes)
    if (total // PACK_UNIT) % 2:
        flat.append(jnp.zeros((PACK_UNIT,), F32))
    return jnp.concatenate(flat).reshape(-1, LANES), sizes


def _unpack(pack, sizes, shapes):
    v = pack.reshape(-1)
    out, off = [], 0
    for (size, padded), shape in zip(sizes, shapes):
        out.append(v[off:off + size].reshape(shape))
        off += padded
    return out


BIG = ("ab_w_in", "ab_w_out", "c_w_in", "c_w_out", "f_w_up", "f_w_down")
COL_SHARDED = ("ab_w_in", "c_w_in", "f_w_up")
PAIRED = ("f_w_up",)
SMALL_REPLICATED = ("norm_mix", "norm_ffn", "norm_final", "a_ln_g", "a_ln_b", "a_w_s", "a_b_s",
                    "b_conv_b", "b_ln_g", "b_ln_b")
SMALL_SHARDED = ("b_conv_w", "c_conv_w", "f_conv_w")
SMALL = SMALL_REPLICATED + SMALL_SHARDED
ALL_WEIGHTS = ("norm_mix", "norm_ffn", "norm_final", "ab_w_in", "a_ln_g", "a_ln_b", "a_w_s", "a_b_s",
               "b_conv_w", "b_conv_b", "b_ln_g", "b_ln_b", "ab_w_out", "c_w_in", "c_conv_w", "c_w_out",
               "f_w_up", "f_conv_w", "f_w_down")


LATE = "late"
SCHEDULE = {
    "ab_in": [("gi", "f_w_up", 0, 0, 4), ("gi", "ab_w_out", 0)],
    "mixer_ab": [("gd", "f_w_up", 0, 0, 4), ("gd", "ab_w_out", 0), ("gi", "f_w_up", 0, 1, 4),
                 ("gi", "f_w_up", 0, 2, 4)],
    "ab_out": [("gd", "f_w_up", 0, 1, 4), ("gd", "f_w_up", 0, 2, 4), ("gi", "f_w_up", 0, 3, 4), (LATE, 1),
               ("gd", "f_w_up", 0, 3, 4)],
    "ffn_up0": [("gi", "f_w_down", 0), ("gi", "c_w_in", 0, 0, 2)],
    "ffn_act0": [("gd", "f_w_down", 0), ("gd", "c_w_in", 0, 0, 2), ("gi", "c_w_in", 0, 1, 2),
                 ("gi", "f_w_up", 1, 0, 4)],
    "ffn_down0": [("gd", "c_w_in", 0, 1, 2), ("gd", "f_w_up", 1, 0, 4), ("gi", "f_w_up", 1, 1, 4),
                  ("gi", "c_w_out", 0)],
    "c_in": [("gd", "f_w_up", 1, 1, 4), ("gd", "c_w_out", 0), ("gi", "f_w_up", 1, 2, 4),
             ("gi", "f_w_up", 1, 3, 4), (LATE, 2), ("gd", "f_w_up", 1, 2, 4), ("gd", "f_w_up", 1, 3, 4)],
    "ffn_up1": [("gi", "f_w_down", 1), (LATE, 2), ("gd", "f_w_down", 1)],
    "ffn_act_bwd1": [("px", "f_w_down", 1)],
    "ffn_up_dx1": [("cx", "f_w_down", 1), ("px", "f_w_up", 1)],
    "mixer_c_bwd": [("cx", "f_w_up", 1, 0, 4), ("px", "c_w_out", 0)],
    "c_in_dx": [("cx", "f_w_up", 1, 1, 4), ("px", "c_w_in", 0)],
    "ffn_act_bwd0": [("cx", "f_w_up", 1, 2, 4), ("cx", "c_w_out", 0), ("cx", "c_w_in", 0),
                     ("px", "f_w_down", 0), ("ps", "f_w_down", 1)],
    "ffn_up_dx0": [("cx", "f_w_down", 0), ("cx", "f_w_up", 1, 3, 4), ("px", "f_w_up", 0)],
    "mixer_ab_bwd": [("cx", "f_w_up", 0, 0, 4), ("px", "ab_w_out", 0)],
    "mixer_b_conv_bwd": [("cx", "f_w_up", 0, 1, 4), ("cx", "f_w_up", 0, 2, 4), ("cx", "ab_w_out", 0),
                         ("ps", "c_w_out", 0), ("ps", "c_w_in", 0), ("ps", "f_w_down", 0),
                         ("ps", "f_w_up", 1)],
    "ab_in_dx": [("cx", "f_w_up", 0, 3, 4), ("px", "ab_w_in", 0)],
}


class _Plan:
    def __init__(self, shapes, place):
        self.shapes, self.place, self.bufs = shapes, place, {}
        self.summed, self.shared = set(), set()

    def weight(self, name, layer):
        g = self.bufs[f"w:{name}:{layer}"]
        if name in COL_SHARDED:
            return g
        _, S, rows, cols = g.shape
        return g.reshape(1, S * rows, cols)

    def grad_ready(self, name, layer, g):
        _, rows, cols = self.shapes[name]
        hbm = lambda a: pltpu.with_memory_space_constraint(a, pltpu.HBM)
        self.bufs[f"g:{name}:{layer}"] = g.reshape(N_CHIPS, rows, cols)
        self.bufs[f"t:{name}:{layer}"] = hbm(lax.empty((N_CHIPS, rows // 2, cols), F32))
        self.bufs[f"l:{name}:{layer}"] = hbm(lax.empty((3, rows // 2, cols), BF16))

    def job(self, kind, name, layer, part=0, parts=1):
        _, rows, cols = self.shapes[name]
        key = f"{name}:{layer}"
        if kind == "gi":
            return _job_gather_ici("w:" + key, rows, cols, part, parts)
        if kind == "gd":
            return _job_gather_d2d("w:" + key, rows, cols, part, parts)
        if kind == "px":
            return _job_pair_exchange("g:" + key, "t:" + key, rows, part, parts)
        if kind == "cx":
            if "p:" + key not in self.bufs:
                self.bufs["p:" + key] = _pair_sum(self.bufs["g:" + key], self.bufs["t:" + key], self.place,
                                                  name=f"pair_sum_{name}{layer}")
            nr = rows // 2 // parts
            return _job_chip_exchange("p:" + key, "l:" + key, part * nr, nr)
        if kind == "ps":
            self.chip_sum(name, layer)
            self.shared.add(key)
            return _job_pair_share("G:" + name, layer, rows)
        raise ValueError(kind)

    def chip_sum(self, name, layer):
        key = f"{name}:{layer}"
        if key not in self.summed:
            self.summed.add(key)
            self.bufs["G:" + name] = _chip_sum(self.bufs["p:" + key], self.bufs["l:" + key],
                                               self.bufs.get("G:" + name), self.place, layer=layer,
                                               shape=self.shapes[name], name=f"chip_sum_{name}{layer}")

    def comm(self, call):
        specs = SCHEDULE.get(call)
        if specs is None:
            return None
        marks = [i for i, spec in enumerate(specs) if spec[0] == LATE]
        late, late_steps = (len(specs) - 1 - marks[0], specs[marks[0]][1]) if marks else (0, 0)
        return _Comm(self, [self.job(*spec) for spec in specs if spec[0] != LATE], late, late_steps)


def _step(x, tgt, w, m, v):
    chip = 2 * lax.axis_index("x") + lax.axis_index("y")
    place = _place_scalars()
    plan = _Plan({n: w[n].shape for n in BIG}, place)
    items = [(n, l) for n in BIG for l in range(w[n].shape[0])]

    for n, l in items:
        plan.bufs[f"w:{n}:{l}"] = _cast_into_slot(w[n], place, layer=l, paired=n in PAIRED, name=f"cast_{n}{l}")
    conv_pack, conv_sizes = _pack([w[n] for n in SMALL_SHARDED])
    hbm = lambda a: pltpu.with_memory_space_constraint(a, pltpu.HBM)
    plan.bufs["conv:mine"] = hbm(conv_pack)
    plan.bufs["conv:all"] = hbm(lax.empty((N_CHIPS,) + conv_pack.shape, F32))
    _comm_only(plan, [[plan.job("gi", "ab_w_in", 0), _job_chip_gather("conv:mine", "conv:all")],
                      [plan.job("gd", "ab_w_in", 0)]], name="gather_first")
    conv_shapes = [w[n].shape for n in SMALL_SHARDED]
    per_chip = [_unpack(plan.bufs["conv:all"][s], conv_sizes, conv_shapes) for s in range(N_CHIPS)]
    small = {n: w[n] for n in SMALL_REPLICATED}
    for idx, n in enumerate(SMALL_SHARDED):
        small[n] = jnp.concatenate([jnp.where(chip == s, w[n], per_chip[s][idx]) for s in range(N_CHIPS)], axis=-1)

    loss, dx, sg = _local_step(x, tgt, small, plan)

    g_pack, g_sizes = _pack([sg[n] for n in SMALL] + [loss])
    g_sum = _allreduce_pack(g_pack, name="allreduce_small_grads",
                            comm=_Comm(plan, [plan.job("cx", "ab_w_in", 0)]))
    full_shapes = [small[n].shape for n in SMALL]
    *summed, loss = _unpack(g_sum, g_sizes, full_shapes + [(1, 1)])
    g_small = dict(zip(SMALL, summed))
    for n in SMALL_SHARDED:
        width = w[n].shape[-1]
        g_small[n] = lax.dynamic_slice_in_dim(g_small[n], chip * width, width, axis=g_small[n].ndim - 1)

    _comm_only(plan, [[plan.job("ps", n, l) for n, l in items if f"{n}:{l}" not in plan.shared]],
               name="reduce_pair_share")
    grads_big = [plan.bufs["G:" + n] for n in BIG]

    grad, delta, new_m, new_v = {}, {}, {}, {}
    for n, g in zip(BIG, grads_big):
        grad[n], delta[n], new_m[n], new_v[n] = _adamw(w[n], g, m[n], v[n], name=f"adamw_{n}")
    shapes = [w[n].shape for n in SMALL]
    wp, sizes = _pack([w[n] for n in SMALL])
    gp, _ = _pack([g_small[n] for n in SMALL])
    mp, _ = _pack([m[n] for n in SMALL])
    vp, _ = _pack([v[n] for n in SMALL])
    R = wp.shape[0]
    _, dp, m2p, v2p = _adamw(wp.reshape(1, R, LANES), gp.reshape(1, R, LANES), mp.reshape(1, R, LANES),
                             vp.reshape(1, R, LANES), name="adamw_small")
    for n, d_, m_, v_ in zip(SMALL, _unpack(dp, sizes, shapes), _unpack(m2p, sizes, shapes),
                             _unpack(v2p, sizes, shapes)):
        grad[n] = g_small[n]
        delta[n], new_m[n], new_v[n] = d_, m_, v_
    return loss, dx, grad, delta, new_m, new_v


def kernel(x, norm_mix, norm_ffn, norm_final, ab_w_in, a_ln_g, a_ln_b, a_w_s, a_b_s, b_conv_w, b_conv_b, b_ln_g, b_ln_b, ab_w_out, c_w_in, c_conv_w, c_w_out, f_w_up, f_conv_w, f_w_down, loss_target, m_norm_mix, m_norm_ffn, m_norm_final, m_ab_w_in, m_a_ln_g, m_a_ln_b, m_a_w_s, m_a_b_s, m_b_conv_w, m_b_conv_b, m_b_ln_g, m_b_ln_b, m_ab_w_out, m_c_w_in, m_c_conv_w, m_c_w_out, m_f_w_up, m_f_conv_w, m_f_w_down, v_norm_mix, v_norm_ffn, v_norm_final, v_ab_w_in, v_a_ln_g, v_a_ln_b, v_a_w_s, v_a_b_s, v_b_conv_w, v_b_conv_b, v_b_ln_g, v_b_ln_b, v_ab_w_out, v_c_w_in, v_c_conv_w, v_c_w_out, v_f_w_up, v_f_conv_w, v_f_w_down):
    given = dict(locals())
    w = {n: given[n] for n in ALL_WEIGHTS}
    m = {n: given["m_" + n] for n in ALL_WEIGHTS}
    v = {n: given["v_" + n] for n in ALL_WEIGHTS}
    T = x.shape[1]
    loss, dx, grad, delta, new_m, new_v = _step(x.reshape(T, D_MODEL), loss_target.reshape(T, D_MODEL), w, m, v)
    out = [loss[0, 0], dx.reshape(x.shape)]
    for d in (grad, delta, new_m, new_v):
        out += [d[n] for n in ALL_WEIGHTS]
    return tuple(out)
```

```python
import functools
import math

import jax
import jax.numpy as jnp
from jax import lax
from jax.experimental import pallas as pl
from jax.experimental.pallas import tpu as pltpu

F32 = jnp.float32
BF16 = jnp.bfloat16

EPS = 1e-6
D_MODEL = 1024
CHUNK = 128
HEAD_DIM = 128
A_HEADS = 4
D_A = 512
D_B = 512
B_CONV = 31
C_CONV = 3
D_FF = 2816
F_CONV = 3
N_CHIPS = 4

ADAM_LR = 0.001
ADAM_B1 = 0.9
ADAM_B2 = 0.999
ADAM_EPS = 1e-08
ADAM_WD = 0.01
ADAM_STEP = 10

SUBLANES = 8
LANES = 128
HALO_SHORT = 16
HALO_LONG = 32
VMEM_BYTES_MAX = 60000 * 1024

INV_SQRT2 = 1.0 / math.sqrt(2.0)
INV_SQRT_2PI = 1.0 / math.sqrt(2.0 * math.pi)

MESH = pl.DeviceIdType.MESH


def _cparams(*sem):
    return pltpu.CompilerParams(dimension_semantics=sem, vmem_limit_bytes=VMEM_BYTES_MAX)


def _pick(total, pref):
    for c in (2048, 1024, 512, 256, 128):
        if c <= pref and total % c == 0:
            return c
    raise ValueError(f"no tile for {total}")


def _sigmoid(x):
    return jax.nn.sigmoid(x)


def _silu(x):
    return x * _sigmoid(x)


def _dsilu(x):
    s = _sigmoid(x)
    return s * (1.0 + x * (1.0 - s))


def _gelu(x):
    return 0.5 * x * (1.0 + lax.erf(x * INV_SQRT2))


def _dgelu(x):
    return 0.5 * (1.0 + lax.erf(x * INV_SQRT2)) + x * jnp.exp(-0.5 * x * x) * INV_SQRT_2PI


def _ln_stats(x):
    mu = jnp.mean(x, axis=-1, keepdims=True)
    xc = x - mu
    var = jnp.mean(xc * xc, axis=-1, keepdims=True)
    r = lax.rsqrt(var + EPS)
    return xc * r, r


def _ln_bwd(dy, xh, r, g):
    dxh = dy * g
    m1 = jnp.mean(dxh, axis=-1, keepdims=True)
    m2 = jnp.mean(dxh * xh, axis=-1, keepdims=True)
    return r * (dxh - m1 - xh * m2)


def _rowsum(x):
    return jnp.sum(x, axis=0, keepdims=True)


HBM_REF = pl.BlockSpec(memory_space=pltpu.HBM)


def _place():
    x, y, c = lax.axis_index("x"), lax.axis_index("y"), lax.axis_index("c")
    peers = [(1 - x, y), (x, 1 - y), (1 - x, 1 - y)]
    return x, y, c, 2 * x + y, (x, y, 1 - c), peers


def _half(rows, which):
    return pl.ds(which * (rows // 2), rows // 2)


def _remote(src, dst, send_sem, recv_sem, device):
    return pltpu.make_async_remote_copy(src_ref=src, dst_ref=dst, send_sem=send_sem, recv_sem=recv_sem,
                                        device_id=device, device_id_type=MESH)


class _Job:
    def __init__(self, reads, writes, ncopies, copies):
        self.reads, self.writes, self.ncopies, self.copies = reads, writes, ncopies, copies


def _share(rows, which, part, parts):
    nr = rows // 2 // parts
    return pl.ds(which * (rows // 2) + part * nr, nr)


def _slot(ref, chip, rows, cols):
    if ref.shape[1] == N_CHIPS:
        return ref.at[0, chip, rows]
    return ref.at[0, chip // 2, rows, pl.ds(pl.multiple_of((chip % 2) * cols, LANES), cols)]


def _job_gather_ici(name, rows, cols, part, parts):
    def copies(src, dst, sem):
        x, y, c, k, sib, peers = _place()
        mine_rows = _share(rows, c, part, parts)
        out = []
        for j, (px, py) in enumerate(peers):
            mine = _slot(src[name], k, mine_rows, cols)
            out.append((_remote(mine, _slot(dst[name], k, mine_rows, cols), sem(j, 0), sem(j, 1), (px, py, c)),
                        _remote(mine, _slot(dst[name], 2 * px + py, mine_rows, cols), sem(j, 0), sem(j, 1),
                                (px, py, c))))
        return out
    return _Job([], [name], 3, copies)


def _job_gather_d2d(name, rows, cols, part, parts):
    def copies(src, dst, sem):
        x, y, c, k, sib, peers = _place()
        out = []
        for j, (px, py) in enumerate(peers):
            mine_rows, their_rows = _share(rows, c, part, parts), _share(rows, 1 - c, part, parts)
            landed = _slot(src[name], 2 * px + py, mine_rows, cols)
            out.append((_remote(landed, _slot(dst[name], 2 * px + py, mine_rows, cols), sem(j, 0), sem(j, 1), sib),
                        _remote(landed, _slot(dst[name], 2 * px + py, their_rows, cols), sem(j, 0), sem(j, 1), sib)))
        return out
    return _Job([], [name], 3, copies)


def _job_chip_gather(sname, dname):
    def copies(src, dst, sem):
        x, y, c, k, sib, peers = _place()
        return [(_remote(src[sname], dst[dname].at[k], sem(j, 0), sem(j, 1), (px, py, c)),
                 _remote(src[sname], dst[dname].at[2 * px + py], sem(j, 0), sem(j, 1), (px, py, c)))
                for j, (px, py) in enumerate(peers)]
    return _Job([sname], [dname], 3, copies)


def _job_pair_exchange(gname, tname, rows, part, parts):
    nr = rows // 2 // parts

    def copies(src, dst, sem):
        x, y, c, k, sib, peers = _place()
        cp = _remote(src[gname].at[:, _share(rows, 1 - c, part, parts), :],
                     dst[tname].at[:, pl.ds(part * nr, nr), :], sem(0, 0), sem(0, 1), sib)
        return [(cp, cp)]
    return _Job([gname], [tname], 1, copies)


def _job_chip_exchange(pname, lname, r0, nr):
    def copies(src, dst, sem):
        x, y, c, k, sib, peers = _place()
        out = []
        for j, (px, py) in enumerate(peers):
            cp = _remote(src[pname].at[2 * px + py, pl.ds(r0, nr)], dst[lname].at[j, pl.ds(r0, nr)],
                         sem(j, 0), sem(j, 1), (px, py, c))
            out.append((cp, cp))
        return out
    return _Job([pname], [lname], 3, copies)


def _job_pair_share(name, layer, rows):
    def copies(src, dst, sem):
        x, y, c, k, sib, peers = _place()
        mine = src[name].at[layer, _half(rows, c)]
        return [(_remote(mine, dst[name].at[layer, _half(rows, c)], sem(0, 0), sem(0, 1), sib),
                 _remote(mine, dst[name].at[layer, _half(rows, 1 - c)], sem(0, 0), sem(0, 1), sib))]
    return _Job([], [name], 1, copies)


class _Comm:
    def __init__(self, plan, jobs, late=0, late_steps=0):
        self.plan, self.jobs, self.late, self.late_steps = plan, jobs, late, late_steps
        self.writes, self.reads = [], []
        for job in jobs:
            for n in job.writes:
                if n not in self.writes:
                    self.writes.append(n)
        for job in jobs:
            for n in job.reads:
                if n not in self.writes and n not in self.reads:
                    self.reads.append(n)
        self.ncopies = sum(job.ncopies for job in jobs)

    def descriptors(self, src, dst, sems, base, stage):
        out = []
        first_late = len(self.jobs) - self.late
        for idx, job in enumerate(self.jobs):
            if stage is None or stage == int(idx >= first_late):
                sem = lambda j, which, base=base: sems.at[base + j, which]
                out += job.copies(src, dst, sem)
            base += job.ncopies
        return out

    def start(self, src, dst, sems, base=0, stage=None):
        for first, _ in self.descriptors(src, dst, sems, base, stage):
            first.start()

    def finish(self, src, dst, sems, base=0, stage=None):
        for _, landed in self.descriptors(src, dst, sems, base, stage):
            landed.wait()


def _comm_operands(comm):
    bufs = comm.plan.bufs
    shapes = [jax.ShapeDtypeStruct(bufs[n].shape, bufs[n].dtype) for n in comm.writes]
    return [bufs[n] for n in comm.reads] + [bufs[n] for n in comm.writes], shapes


def _pallas(comm, body, *, name, grid, in_specs, out_specs, out_shape, compiler_params, scratch_shapes=(),
            aliases=None):
    aliases = dict(aliases or {})
    if comm is None:
        return pl.pallas_call(body, name=name, grid=grid, in_specs=in_specs, out_specs=out_specs,
                              out_shape=out_shape, scratch_shapes=list(scratch_shapes),
                              input_output_aliases=aliases, compiler_params=compiler_params)
    single = not isinstance(out_shape, (list, tuple))
    base_specs = [out_specs] if single else list(out_specs)
    base_shape = [out_shape] if single else list(out_shape)
    nb, nr, nw, nbo, nsc = len(in_specs), len(comm.reads), len(comm.writes), len(base_specs), len(scratch_shapes)

    def wrapped(*refs):
        base_in, rd, wr_in = refs[:nb], refs[nb:nb + nr], refs[nb + nr:nb + nr + nw]
        o0 = nb + nr + nw
        base_out, wr_out = refs[o0:o0 + nbo], refs[o0 + nbo:o0 + nbo + nw]
        scratch, sems = refs[o0 + nbo + nw:o0 + nbo + nw + nsc], refs[-1]
        src = dict(zip(comm.reads, rd))
        src.update(zip(comm.writes, wr_in))
        dst = dict(zip(comm.writes, wr_out))
        first = functools.reduce(jnp.logical_and, [pl.program_id(a) == 0 for a in range(len(grid))])
        last = functools.reduce(jnp.logical_and,
                                [pl.program_id(a) == pl.num_programs(a) - 1 for a in range(len(grid))])

        if not comm.late:
            @pl.when(first)
            def _():
                comm.start(src, dst, sems)
            body(*base_in, *base_out, *scratch)

            @pl.when(last)
            def _():
                comm.finish(src, dst, sems)
            return
        step = functools.reduce(lambda acc, a: acc * grid[a] + pl.program_id(a), range(len(grid)), 0)

        @pl.when(first)
        def _():
            comm.start(src, dst, sems, stage=0)

        @pl.when(step == math.prod(grid) - comm.late_steps)
        def _():
            comm.finish(src, dst, sems, stage=0)
            comm.start(src, dst, sems, stage=1)
        body(*base_in, *base_out, *scratch)

        @pl.when(last)
        def _():
            comm.finish(src, dst, sems, stage=1)

    operands, shapes = _comm_operands(comm)
    call = pl.pallas_call(
        wrapped, name=name, grid=grid, in_specs=list(in_specs) + [HBM_REF] * (nr + nw),
        out_specs=base_specs + [HBM_REF] * nw, out_shape=base_shape + shapes,
        input_output_aliases={**aliases, **{nb + nr + q: nbo + q for q in range(nw)}},
        scratch_shapes=list(scratch_shapes) + [pltpu.SemaphoreType.DMA((comm.ncopies, 2))],
        compiler_params=compiler_params)

    def run(*args):
        outs = call(*args, *operands)
        for q, n in enumerate(comm.writes):
            comm.plan.bufs[n] = outs[nbo + q]
        return outs[0] if single else list(outs[:nbo])

    return run


def _comm_only(plan, phases, *, name):
    comms = [_Comm(plan, jobs) for jobs in phases]
    both = _Comm(plan, [job for jobs in phases for job in jobs])
    nr, nw = len(both.reads), len(both.writes)

    def body(*refs):
        rd, wr_in, wr_out, sems = refs[:nr], refs[nr:nr + nw], refs[nr + nw:nr + 2 * nw], refs[-1]
        src = dict(zip(both.reads, rd))
        src.update(zip(both.writes, wr_in))
        dst = dict(zip(both.writes, wr_out))
        base = 0
        for comm in comms:
            comm.start(src, dst, sems, base)
            comm.finish(src, dst, sems, base)
            base += comm.ncopies

    operands, shapes = _comm_operands(both)
    outs = pl.pallas_call(
        body, name=name, in_specs=[HBM_REF] * (nr + nw), out_specs=[HBM_REF] * nw, out_shape=shapes,
        input_output_aliases={nr + q: q for q in range(nw)},
        scratch_shapes=[pltpu.SemaphoreType.DMA((both.ncopies, 2))],
    )(*operands)
    for q, n in enumerate(both.writes):
        plan.bufs[n] = outs[q]


def _mm_nn(a, w, *, layer, tm, tn, residual=None, norm=None, out_dtype=F32, name, comm=None):
    T, K = a.shape
    if w.ndim == 4:
        _, S, _, n4 = w.shape
        N = S * n4
        bps = n4 // tn
        w_spec = pl.BlockSpec((None, None, K, tn), lambda j, i: (layer, j // bps, 0, j % bps))
    else:
        N = w.shape[2]
        w_spec = pl.BlockSpec((None, K, tn), lambda j, i: (layer, 0, j))
    in_specs = [pl.BlockSpec((tm, K), lambda j, i: (i, 0)), w_spec]
    args = [a, w]
    if residual is not None:
        in_specs.append(pl.BlockSpec((tm, tn), lambda j, i: (i, j)))
        args.append(residual)
    out_specs = pl.BlockSpec((tm, tn), lambda j, i: (i, j))
    out_shape = jax.ShapeDtypeStruct((T, N), out_dtype)
    if norm is not None:
        assert tn == N
        g, norm_layer = norm
        in_specs.append(pl.BlockSpec((None, 1, N), lambda j, i: (norm_layer, 0, 0)))
        args.append(g)
        out_specs = [out_specs, pl.BlockSpec((tm, tn), lambda j, i: (i, j))]
        out_shape = [out_shape, jax.ShapeDtypeStruct((T, N), BF16)]

    def body(*refs):
        a_ref, w_ref = refs[0], refs[1]
        acc = jnp.dot(a_ref[...].astype(BF16), w_ref[...], preferred_element_type=F32)
        if residual is not None:
            acc = refs[2][...] + acc
        if norm is None:
            refs[-1][...] = acc.astype(out_dtype)
        else:
            refs[-2][...] = acc.astype(out_dtype)
            r = lax.rsqrt(jnp.mean(acc * acc, axis=-1, keepdims=True) + EPS)
            refs[-1][...] = (acc * r * refs[-3][...]).astype(BF16)

    return _pallas(
        comm, body, name=name, grid=(N // tn, T // tm), in_specs=in_specs,
        out_specs=out_specs, out_shape=out_shape,
        compiler_params=_cparams("parallel", "parallel"),
    )(*args)


def _norm_mm_nn(x, g, w, *, g_layer, tm, tn, name, comm=None):
    T, K = x.shape
    _, S, _, n4 = w.shape
    bps = n4 // tn

    def body(x_ref, g_ref, w_ref, h_ref, o_ref):
        @pl.when(pl.program_id(1) == 0)
        def _():
            xf = x_ref[...]
            r = lax.rsqrt(jnp.mean(xf * xf, axis=-1, keepdims=True) + EPS)
            h_ref[...] = (xf * r * g_ref[...]).astype(BF16)
        o_ref[...] = jnp.dot(h_ref[...], w_ref[...], preferred_element_type=F32).astype(BF16)

    return _pallas(
        comm, body, name=name, grid=(T // tm, S * bps),
        in_specs=[pl.BlockSpec((tm, K), lambda i, j: (i, 0)),
                  pl.BlockSpec((None, 1, K), lambda i, j: (g_layer, 0, 0)),
                  pl.BlockSpec((None, None, K, tn), lambda i, j: (0, j // bps, 0, j % bps))],
        out_specs=[pl.BlockSpec((tm, K), lambda i, j: (i, 0)), pl.BlockSpec((tm, tn), lambda i, j: (i, j))],
        out_shape=[jax.ShapeDtypeStruct((T, K), BF16), jax.ShapeDtypeStruct((T, S * n4), BF16)],
        compiler_params=_cparams("parallel", "arbitrary"),
    )(x, g, w)


def _mm_nt(dy, w, *, layer, tm, tn, name, out_dtype=F32, comm=None):
    T = dy.shape[0]
    nt_dims = (((1,), (1,)), ((), ()))
    _, R, N = w.shape

    def body2(dy_ref, w_ref, o_ref):
        o_ref[...] = lax.dot_general(dy_ref[...].astype(BF16), w_ref[...], nt_dims,
                                     preferred_element_type=F32).astype(out_dtype)

    return _pallas(
        comm, body2, name=name, grid=(R // tn, T // tm),
        in_specs=[pl.BlockSpec((tm, N), lambda j, i: (i, 0)),
                  pl.BlockSpec((None, tn, N), lambda j, i: (layer, j, 0))],
        out_specs=pl.BlockSpec((tm, tn), lambda j, i: (i, j)),
        out_shape=jax.ShapeDtypeStruct((T, R), out_dtype),
        compiler_params=_cparams("parallel", "parallel"),
    )(dy, w)


def _mm_tn(a, dy, *, shards, tk, tn, tt, name, comm=None):
    T, K = a.shape
    N = dy.shape[1]
    tn_dims = (((0,), (0,)), ((), ()))
    n4 = N if shards is None else N // shards
    span = max(tn // n4, 1)

    def body(a_ref, dy_ref, o_ref):
        @pl.when(pl.program_id(2) == 0)
        def _():
            o_ref[...] = jnp.zeros_like(o_ref)
        r = lax.dot_general(a_ref[...].astype(BF16), dy_ref[...].astype(BF16), tn_dims,
                            preferred_element_type=F32)
        if span == 1:
            o_ref[...] += r
        else:
            for q in range(span):
                o_ref[q] += r[:, q * n4:(q + 1) * n4]

    if shards is None:
        out_spec = pl.BlockSpec((tk, tn), lambda k, n, t: (k, n))
        out_shape = jax.ShapeDtypeStruct((K, N), F32)
    elif span > 1:
        out_spec = pl.BlockSpec((span, tk, n4), lambda k, n, t: (n, k, 0))
        out_shape = jax.ShapeDtypeStruct((shards, K, n4), F32)
    else:
        bps = n4 // tn
        out_spec = pl.BlockSpec((None, tk, tn), lambda k, n, t: (n // bps, k, n % bps))
        out_shape = jax.ShapeDtypeStruct((shards, K, n4), F32)
    return _pallas(
        comm, body, name=name, grid=(K // tk, N // tn, T // tt),
        in_specs=[pl.BlockSpec((tt, tk), lambda k, n, t: (t, k)),
                  pl.BlockSpec((tt, tn), lambda k, n, t: (t, n))],
        out_specs=out_spec, out_shape=out_shape,
        compiler_params=_cparams("parallel", "parallel", "arbitrary"),
    )(a, dy)


def _rmsnorm_bwd_math(xf, g, dh, dres):
    r = lax.rsqrt(jnp.mean(xf * xf, axis=-1, keepdims=True) + EPS)
    xh = xf * r
    dxh = dh * g
    dx = dres + r * (dxh - xh * jnp.mean(dxh * xh, axis=-1, keepdims=True))
    return dx, _rowsum(dh * xh)


def _mm_nt_norm(dy, w, x, g, dres, *, g_layer, tm, name, comm=None):
    T = dy.shape[0]
    _, S, K, n4 = w.shape
    nt_dims = (((1,), (1,)), ((), ()))

    def body(dy_ref, w_ref, x_ref, g_ref, dres_ref, dx_ref, dg_ref):
        @pl.when(pl.program_id(0) == 0)
        def _():
            dg_ref[...] = jnp.zeros_like(dg_ref)
        dh = None
        for s in range(S):
            part = lax.dot_general(dy_ref[:, s * n4:(s + 1) * n4].astype(BF16), w_ref[s], nt_dims,
                                   preferred_element_type=F32)
            dh = part if dh is None else dh + part
        dx, dg = _rmsnorm_bwd_math(x_ref[...], g_ref[...], dh, dres_ref[...])
        dx_ref[...] = dx
        dg_ref[...] += dg

    row = lambda i: (i, 0)
    return _pallas(
        comm, body, name=name, grid=(T // tm,),
        in_specs=[pl.BlockSpec((tm, S * n4), row),
                  pl.BlockSpec((None, S, K, n4), lambda i: (0, 0, 0, 0)),
                  pl.BlockSpec((tm, K), row),
                  pl.BlockSpec((None, 1, K), lambda i: (g_layer, 0, 0)),
                  pl.BlockSpec((tm, K), row)],
        out_specs=[pl.BlockSpec((tm, K), row), pl.BlockSpec((1, K), lambda i: (0, 0))],
        out_shape=[jax.ShapeDtypeStruct((T, K), F32), jax.ShapeDtypeStruct((1, K), F32)],
        compiler_params=_cparams("arbitrary"),
    )(dy, w, x, g, dres)


def _mm_nn_loss(a, w, residual, tgt, g, *, tm, name, comm=None):
    T, K = a.shape
    D = w.shape[2]

    def body(a_ref, w_ref, res_ref, t_ref, g_ref, loss_ref, dx_ref, dg_ref):
        @pl.when(pl.program_id(0) == 0)
        def _():
            dg_ref[...] = jnp.zeros_like(dg_ref)
            loss_ref[...] = jnp.zeros_like(loss_ref)
        xf = res_ref[...] + jnp.dot(a_ref[...], w_ref[...], preferred_element_type=F32)
        gg = g_ref[...]
        r = lax.rsqrt(jnp.mean(xf * xf, axis=-1, keepdims=True) + EPS)
        xh = xf * r
        err = xh * gg - t_ref[...]
        row = jnp.mean(err * err, axis=-1, keepdims=True)
        loss_ref[...] += 0.5 * jnp.sum(row, axis=0, keepdims=True)
        dy = err * (1.0 / D)
        dg_ref[...] += _rowsum(dy * xh)
        dxh = dy * gg
        dx_ref[...] = r * (dxh - xh * jnp.mean(dxh * xh, axis=-1, keepdims=True))

    row_spec = pl.BlockSpec((tm, D), lambda i: (i, 0))
    return _pallas(
        comm, body, name=name, grid=(T // tm,),
        in_specs=[pl.BlockSpec((tm, K), lambda i: (i, 0)), pl.BlockSpec((None, K, D), lambda i: (0, 0, 0)),
                  row_spec, row_spec, pl.BlockSpec((1, D), lambda i: (0, 0))],
        out_specs=[pl.BlockSpec((1, 1), lambda i: (0, 0)), row_spec, pl.BlockSpec((1, D), lambda i: (0, 0))],
        out_shape=[jax.ShapeDtypeStruct((1, 1), F32), jax.ShapeDtypeStruct((T, D), F32),
                   jax.ShapeDtypeStruct((1, D), F32)],
        compiler_params=_cparams("arbitrary"),
    )(a, w, residual, tgt, g)


CONV_ROWS = 64
CONV_COLS = 256


def _halo_prev_index(tm, halo):
    per = tm // halo
    return lambda i: jnp.maximum(i * per - 1, 0)


def _halo_next_index(tm, halo, total):
    per = tm // halo
    last = total // halo - 1
    return lambda i: jnp.minimum((i + 1) * per, last)


def _causal_mask():
    t = lax.broadcasted_iota(jnp.int32, (CHUNK, CHUNK), 0)
    s = lax.broadcasted_iota(jnp.int32, (CHUNK, CHUNK), 1)
    return s <= t


def _mixer_ab_fwd(z, a_ln_g, a_ln_b, w_s, b_s, conv_w, conv_b, b_ln_g, b_ln_b, *, tm, name, comm=None):
    T = z.shape[0]
    nchunk = tm // CHUNK
    halo = HALO_LONG

    def body(za_ref, zb_ref, zh_ref, alg_ref, alb_ref, ws_ref, bs_ref, cw_ref, cbias_ref,
             blg_ref, blb_ref, y_ref, cb_ref, ext_ref):
        i = pl.program_id(0)
        gu = _gelu(za_ref[:, :D_A].astype(F32))
        gv = _gelu(za_ref[:, D_A:].astype(F32))
        xh, _ = _ln_stats(gv)
        lv = (xh * alg_ref[...] + alb_ref[...]).astype(BF16)
        mask = _causal_mask()
        for h in range(A_HEADS):
            wm = jnp.where(mask, ws_ref[h], 0.0).astype(BF16)
            cols = slice(h * HEAD_DIM, (h + 1) * HEAD_DIM)
            for c in range(nchunk):
                rows = slice(c * CHUNK, (c + 1) * CHUNK)
                mixed = jnp.dot(wm, lv[rows, cols], preferred_element_type=F32) + bs_ref[h]
                y_ref[rows, cols] = (gu[rows, cols] * mixed).astype(BF16)
        ext_ref[halo:halo + tm, :] = zb_ref[:, :D_B].astype(F32) * _sigmoid(zb_ref[:, D_B:].astype(F32))
        prev = zh_ref[:, :D_B].astype(F32) * _sigmoid(zh_ref[:, D_B:].astype(F32))
        ext_ref[0:halo, :] = jnp.where(i > 0, prev, 0.0)
        for rb in range(tm // CONV_ROWS):
            for cb in range(D_B // CONV_COLS):
                cs = slice(cb * CONV_COLS, (cb + 1) * CONV_COLS)
                window = ext_ref[rb * CONV_ROWS:rb * CONV_ROWS + CONV_ROWS + halo, cs]
                acc = jnp.zeros((CONV_ROWS, CONV_COLS), F32)
                for k in range(B_CONV):
                    shifted = _rows_after(window, halo - (B_CONV - 1) + k)[:CONV_ROWS]
                    acc = acc + cw_ref[k:k + 1, cs] * shifted
                cb_ref[rb * CONV_ROWS:(rb + 1) * CONV_ROWS, cs] = acc + cbias_ref[:, cs]
        xhb, _ = _ln_stats(cb_ref[...])
        y_ref[:, D_A:] = _silu(xhb * blg_ref[...] + blb_ref[...]).astype(BF16)

    row = lambda i: (i, 0)
    par = lambda i: (0, 0)
    return _pallas(
        comm, body, name=name, grid=(T // tm,),
        in_specs=[pl.BlockSpec((tm, 2 * D_A), lambda i: (i, 0)),
                  pl.BlockSpec((tm, 2 * D_B), lambda i: (i, 1)),
                  pl.BlockSpec((halo, 2 * D_B), lambda i: (_halo_prev_index(tm, halo)(i), 1)),
                  pl.BlockSpec((1, D_A), par), pl.BlockSpec((1, D_A), par),
                  pl.BlockSpec((A_HEADS, CHUNK, CHUNK), lambda i: (0, 0, 0)),
                  pl.BlockSpec((A_HEADS, CHUNK, 1), lambda i: (0, 0, 0)),
                  pl.BlockSpec((B_CONV, D_B), par), pl.BlockSpec((1, D_B), par),
                  pl.BlockSpec((1, D_B), par), pl.BlockSpec((1, D_B), par)],
        out_specs=[pl.BlockSpec((tm, D_A + D_B), row), pl.BlockSpec((tm, D_B), row)],
        out_shape=[jax.ShapeDtypeStruct((T, D_A + D_B), BF16), jax.ShapeDtypeStruct((T, D_B), F32)],
        scratch_shapes=[pltpu.VMEM((halo + tm, D_B), F32)],
        compiler_params=_cparams("parallel"),
    )(z, z, z, a_ln_g, a_ln_b, w_s, b_s, conv_w, conv_b, b_ln_g, b_ln_b)


def _mixer_ab_bwd_pre(z, cb, dy, a_ln_g, a_ln_b, w_s, b_s, b_ln_g, b_ln_b, *, tm, name, comm=None):
    T = z.shape[0]
    nchunk = tm // CHUNK
    tn_dims = (((0,), (0,)), ((), ()))
    nt_dims = (((1,), (1,)), ((), ()))

    def body(za_ref, cb_ref, dy_ref, alg_ref, alb_ref, ws_ref, bs_ref, blg_ref, blb_ref,
             dza_ref, dcb_ref, dalg_ref, dalb_ref, dws_ref, dbs_ref, dblg_ref, dblb_ref,
             dlv_ref):
        @pl.when(pl.program_id(0) == 0)
        def _():
            for ref in (dalg_ref, dalb_ref, dws_ref, dbs_ref, dblg_ref, dblb_ref):
                ref[...] = jnp.zeros_like(ref)
        ua = za_ref[:, :D_A].astype(F32)
        va = za_ref[:, D_A:].astype(F32)
        gu = _gelu(ua)
        gv = _gelu(va)
        xh, r = _ln_stats(gv)
        alg = alg_ref[...]
        lv = (xh * alg + alb_ref[...]).astype(BF16)
        dya = dy_ref[:, :D_A].astype(F32)
        mask = _causal_mask()
        for h in range(A_HEADS):
            wm = jnp.where(mask, ws_ref[h], 0.0).astype(BF16)
            cols = slice(h * HEAD_DIM, (h + 1) * HEAD_DIM)
            dwm = jnp.zeros((CHUNK, CHUNK), F32)
            dbs = jnp.zeros((CHUNK, 1), F32)
            for c in range(nchunk):
                rows = slice(c * CHUNK, (c + 1) * CHUNK)
                lvb = lv[rows, cols]
                mixed = jnp.dot(wm, lvb, preferred_element_type=F32) + bs_ref[h]
                dyb = dya[rows, cols]
                dza_ref[rows, cols] = (dyb * mixed * _dgelu(ua[rows, cols])).astype(BF16)
                dmixed = dyb * gu[rows, cols]
                dmb = dmixed.astype(BF16)
                dlv_ref[rows, cols] = lax.dot_general(wm, dmb, tn_dims, preferred_element_type=F32)
                dwm = dwm + lax.dot_general(dmb, lvb, nt_dims, preferred_element_type=F32)
                dbs = dbs + jnp.sum(dmixed, axis=1, keepdims=True)
            dws_ref[h] += jnp.where(mask, dwm, 0.0)
            dbs_ref[h] += dbs
        dlv = dlv_ref[...]
        dalg_ref[...] += _rowsum(dlv * xh)
        dalb_ref[...] += _rowsum(dlv)
        dgv = _ln_bwd(dlv, xh, r, alg)
        dza_ref[:, D_A:] = (dgv * _dgelu(va)).astype(BF16)
        xhb, rb = _ln_stats(cb_ref[...])
        blg = blg_ref[...]
        lb = xhb * blg + blb_ref[...]
        dlb = dy_ref[:, D_A:].astype(F32) * _dsilu(lb)
        dblg_ref[...] += _rowsum(dlb * xhb)
        dblb_ref[...] += _rowsum(dlb)
        dcb_ref[...] = _ln_bwd(dlb, xhb, rb, blg)

    row = lambda i: (i, 0)
    par = lambda i: (0, 0)
    par3 = lambda i: (0, 0, 0)
    return _pallas(
        comm, body, name=name, grid=(T // tm,),
        in_specs=[pl.BlockSpec((tm, 2 * D_A), row), pl.BlockSpec((tm, D_B), row),
                  pl.BlockSpec((tm, D_A + D_B), row),
                  pl.BlockSpec((1, D_A), par), pl.BlockSpec((1, D_A), par),
                  pl.BlockSpec((A_HEADS, CHUNK, CHUNK), par3),
                  pl.BlockSpec((A_HEADS, CHUNK, 1), par3),
                  pl.BlockSpec((1, D_B), par), pl.BlockSpec((1, D_B), par)],
        out_specs=[pl.BlockSpec((tm, 2 * D_A), row), pl.BlockSpec((tm, D_B), row),
                   pl.BlockSpec((1, D_A), par), pl.BlockSpec((1, D_A), par),
                   pl.BlockSpec((A_HEADS, CHUNK, CHUNK), par3),
                   pl.BlockSpec((A_HEADS, CHUNK, 1), par3),
                   pl.BlockSpec((1, D_B), par), pl.BlockSpec((1, D_B), par)],
        out_shape=[jax.ShapeDtypeStruct((T, 2 * D_A + 2 * D_B), BF16), jax.ShapeDtypeStruct((T, D_B), F32),
                   jax.ShapeDtypeStruct((1, D_A), F32), jax.ShapeDtypeStruct((1, D_A), F32),
                   jax.ShapeDtypeStruct((A_HEADS, CHUNK, CHUNK), F32),
                   jax.ShapeDtypeStruct((A_HEADS, CHUNK, 1), F32),
                   jax.ShapeDtypeStruct((1, D_B), F32), jax.ShapeDtypeStruct((1, D_B), F32)],
        scratch_shapes=[pltpu.VMEM((tm, D_A), F32)],
        compiler_params=_cparams("arbitrary"),
    )(z, cb, dy, a_ln_g, a_ln_b, w_s, b_s, b_ln_g, b_ln_b)


def _mixer_b_conv_bwd(z, dcb, conv_w, dz, *, tm, name, comm=None):
    T = z.shape[0]
    halo = HALO_LONG

    def body(zb_ref, dcb_ref, dcn_ref, cw_ref, dz_in_ref, dzb_ref, dcw_ref, dbias_ref, dext_ref):
        i = pl.program_id(0)
        last = pl.num_programs(0) - 1

        @pl.when(i == 0)
        def _():
            dcw_ref[...] = jnp.zeros_like(dcw_ref)
            dbias_ref[...] = jnp.zeros_like(dbias_ref)
        dcb = dcb_ref[...]
        dext_ref[0:tm, :] = dcb
        dext_ref[tm:tm + halo, :] = jnp.where(i < last, dcn_ref[...], 0.0)
        dbias_ref[...] += _rowsum(dcb)
        for rb in range(tm // CONV_ROWS):
            for cb in range(D_B // CONV_COLS):
                cs = slice(cb * CONV_COLS, (cb + 1) * CONV_COLS)
                gcs = slice(D_B + cb * CONV_COLS, D_B + (cb + 1) * CONV_COLS)
                rs = slice(rb * CONV_ROWS, (rb + 1) * CONV_ROWS)
                xbb = zb_ref[rs, cs].astype(F32)
                sgb = _sigmoid(zb_ref[rs, gcs].astype(F32))
                yb0 = xbb * sgb
                window = dext_ref[rb * CONV_ROWS:rb * CONV_ROWS + CONV_ROWS + halo, cs]
                acc = jnp.zeros((CONV_ROWS, CONV_COLS), F32)
                for k in range(B_CONV):
                    shifted = _rows_after(window, (B_CONV - 1) - k)[:CONV_ROWS]
                    acc = acc + cw_ref[k:k + 1, cs] * shifted
                    dcw_ref[k:k + 1, cs] += _rowsum(shifted * yb0)
                dzb_ref[rs, cs] = (acc * sgb).astype(BF16)
                dzb_ref[rs, gcs] = (acc * xbb * sgb * (1.0 - sgb)).astype(BF16)

    row = lambda i: (i, 0)
    par = lambda i: (0, 0)
    return _pallas(
        comm, body, name=name, grid=(T // tm,),
        in_specs=[pl.BlockSpec((tm, 2 * D_B), lambda i: (i, 1)),
                  pl.BlockSpec((tm, D_B), row),
                  pl.BlockSpec((halo, D_B), lambda i: (_halo_next_index(tm, halo, T)(i), 0)),
                  pl.BlockSpec((B_CONV, D_B), par), pl.BlockSpec(memory_space=pl.ANY)],
        out_specs=[pl.BlockSpec((tm, 2 * D_B), lambda i: (i, 1)), pl.BlockSpec((B_CONV, D_B), par),
                   pl.BlockSpec((1, D_B), par)],
        out_shape=[jax.ShapeDtypeStruct(dz.shape, BF16), jax.ShapeDtypeStruct((B_CONV, D_B), F32),
                   jax.ShapeDtypeStruct((1, D_B), F32)],
        scratch_shapes=[pltpu.VMEM((tm + halo, D_B), F32)], aliases={4: 0},
        compiler_params=_cparams("arbitrary"),
    )(z, dcb, dcb, conv_w, dz)


def _rows_before(x, a):
    return x if a == 0 else pltpu.roll(x, a, axis=0)


def _rows_after(x, a):
    return x if a == 0 else pltpu.roll(x, x.shape[0] - a, axis=0)


def _conv3(w_ref, x, halo, cs):
    acc = w_ref[2:3, cs] * x[halo:]
    acc = acc + w_ref[1:2, cs] * _rows_before(x, 1)[halo:]
    return acc + w_ref[0:1, cs] * _rows_before(x, 2)[halo:]


def _mixer_c_fwd(z, conv_w, *, tm, name, comm=None):
    T = z.shape[0]
    D = D_MODEL
    halo = HALO_SHORT
    W = CONV_COLS

    def body(bg_ref, cg_ref, xv_ref, cgh_ref, xvh_ref, w_ref, r_ref):
        i = pl.program_id(0)
        for cb in range(D // W):
            cs = slice(cb * W, (cb + 1) * W)
            prev = jnp.where(i > 0, cgh_ref[:, cs].astype(F32) * xvh_ref[:, cs].astype(F32), 0.0)
            p = jnp.concatenate([prev, cg_ref[:, cs].astype(F32) * xv_ref[:, cs].astype(F32)], axis=0)
            r_ref[:, cs] = (bg_ref[:, cs].astype(F32) * _conv3(w_ref, p, halo, cs)).astype(BF16)

    hp = _halo_prev_index(tm, halo)
    return _pallas(
        comm, body, name=name, grid=(T // tm,),
        in_specs=[pl.BlockSpec((tm, D), lambda i: (i, 0)), pl.BlockSpec((tm, D), lambda i: (i, 1)),
                  pl.BlockSpec((tm, D), lambda i: (i, 2)),
                  pl.BlockSpec((halo, D), lambda i: (hp(i), 1)),
                  pl.BlockSpec((halo, D), lambda i: (hp(i), 2)),
                  pl.BlockSpec((None, C_CONV, D), lambda i: (0, 0, 0))],
        out_specs=pl.BlockSpec((tm, D), lambda i: (i, 0)),
        out_shape=jax.ShapeDtypeStruct((T, D), BF16),
        compiler_params=_cparams("parallel"),
    )(z, z, z, z, z, conv_w)


def _mixer_c_bwd(z, dr, conv_w, *, tm, name, comm=None):
    T = z.shape[0]
    D = D_MODEL
    halo = HALO_SHORT
    W = CONV_COLS

    def body(bg_ref, cg_ref, xv_ref, cgh_ref, xvh_ref, bgn_ref, dr_ref, drn_ref, w_ref, dz_ref, dw_ref):
        i = pl.program_id(0)
        last = pl.num_programs(0) - 1

        @pl.when(i == 0)
        def _():
            dw_ref[...] = jnp.zeros_like(dw_ref)
        for cb in range(D // W):
            cs = slice(cb * W, (cb + 1) * W)
            cg = cg_ref[:, cs].astype(F32)
            xv = xv_ref[:, cs].astype(F32)
            dr = dr_ref[:, cs].astype(F32)
            p = cg * xv
            prev = jnp.where(i > 0, cgh_ref[:, cs].astype(F32) * xvh_ref[:, cs].astype(F32), 0.0)
            q = _conv3(w_ref, jnp.concatenate([prev, p], axis=0), halo, cs)
            dz_ref[:, cs] = (dr * q).astype(BF16)
            nxt = jnp.where(i < last, drn_ref[:, cs].astype(F32) * bgn_ref[:, cs].astype(F32), 0.0)
            dq = jnp.concatenate([dr * bg_ref[:, cs].astype(F32), nxt], axis=0)
            dp = None
            for k in range(C_CONV):
                shifted = _rows_after(dq, 2 - k)[:tm]
                term = w_ref[k:k + 1, cs] * shifted
                dp = term if dp is None else dp + term
                dw_ref[k:k + 1, cs] += _rowsum(shifted * p)
            dz_ref[:, D + cb * W:D + (cb + 1) * W] = (dp * xv).astype(BF16)
            dz_ref[:, 2 * D + cb * W:2 * D + (cb + 1) * W] = (dp * cg).astype(BF16)

    hp = _halo_prev_index(tm, halo)
    hn = _halo_next_index(tm, halo, T)
    return _pallas(
        comm, body, name=name, grid=(T // tm,),
        in_specs=[pl.BlockSpec((tm, D), lambda i: (i, 0)), pl.BlockSpec((tm, D), lambda i: (i, 1)),
                  pl.BlockSpec((tm, D), lambda i: (i, 2)),
                  pl.BlockSpec((halo, D), lambda i: (hp(i), 1)),
                  pl.BlockSpec((halo, D), lambda i: (hp(i), 2)),
                  pl.BlockSpec((halo, D), lambda i: (hn(i), 0)),
                  pl.BlockSpec((tm, D), lambda i: (i, 0)),
                  pl.BlockSpec((halo, D), lambda i: (hn(i), 0)),
                  pl.BlockSpec((None, C_CONV, D), lambda i: (0, 0, 0))],
        out_specs=[pl.BlockSpec((tm, 3 * D), lambda i: (i, 0)),
                   pl.BlockSpec((C_CONV, D), lambda i: (0, 0))],
        out_shape=[jax.ShapeDtypeStruct((T, 3 * D), BF16), jax.ShapeDtypeStruct((C_CONV, D), F32)],
        compiler_params=_cparams("arbitrary"),
    )(z, z, z, z, z, z, dr, dr, conv_w)


FFN_COLS = 128


def _ffn_act_fwd(up, conv_w, *, layer, tm, name, comm=None):
    T = up.shape[0]
    halo = HALO_SHORT
    W = FFN_COLS

    def body(up_ref, uph_ref, w_ref, a_ref, upc_ref):
        i = pl.program_id(0)

        def conv(cs):
            prev = jnp.where(i > 0, uph_ref[:, cs], jnp.zeros((halo, W), BF16))
            return _conv3(w_ref, jnp.concatenate([prev, up_ref[:, cs]], axis=0).astype(F32), halo, cs)

        for cb in range(D_FF // W):
            gs = slice(cb * W, (cb + 1) * W)
            vs = slice(D_FF + cb * W, D_FF + (cb + 1) * W)
            g = conv(gs)
            v = conv(vs)
            upc_ref[:, gs] = g.astype(BF16)
            upc_ref[:, vs] = v.astype(BF16)
            a_ref[:, gs] = (_silu(g) * v).astype(BF16)

    return _pallas(
        comm, body, name=name, grid=(T // tm,),
        in_specs=[pl.BlockSpec((tm, 2 * D_FF), lambda i: (i, 0)),
                  pl.BlockSpec((halo, 2 * D_FF), lambda i: (_halo_prev_index(tm, halo)(i), 0)),
                  pl.BlockSpec((None, F_CONV, 2 * D_FF), lambda i: (layer, 0, 0))],
        out_specs=[pl.BlockSpec((tm, D_FF), lambda i: (i, 0)),
                   pl.BlockSpec((tm, 2 * D_FF), lambda i: (i, 0))],
        out_shape=[jax.ShapeDtypeStruct((T, D_FF), BF16), jax.ShapeDtypeStruct((T, 2 * D_FF), BF16)],
        compiler_params=_cparams("parallel"),
    )(up, up, conv_w)


def _ffn_act_bwd(up, upc, da, conv_w, *, layer, tm, name, comm=None):
    T = up.shape[0]
    halo = HALO_SHORT
    W = FFN_COLS

    def body(up_ref, upc_ref, upcn_ref, da_ref, dan_ref, w_ref, dup_ref, dw_ref):
        i = pl.program_id(0)
        last = pl.num_programs(0) - 1

        @pl.when(i == 0)
        def _():
            dw_ref[...] = jnp.zeros_like(dw_ref)
        live = jnp.where(i < last, 1.0, 0.0)
        for cb in range(D_FF // W):
            gs = slice(cb * W, (cb + 1) * W)
            vs = slice(D_FF + cb * W, D_FF + (cb + 1) * W)
            g = jnp.concatenate([upc_ref[:, gs], upcn_ref[:, gs]], axis=0).astype(F32)
            v = jnp.concatenate([upc_ref[:, vs], upcn_ref[:, vs]], axis=0).astype(F32)
            da = jnp.concatenate([da_ref[:, gs].astype(F32), dan_ref[:, gs].astype(F32) * live], axis=0)
            s = _sigmoid(g)
            silu = g * s
            grads = (da * v * (s * (1.0 + g * (1.0 - s))), da * silu)
            for cs, d in zip((gs, vs), grads):
                u = up_ref[:, cs].astype(F32)
                acc = None
                for k in range(F_CONV):
                    shifted = _rows_after(d, 2 - k)[:tm]
                    term = w_ref[k:k + 1, cs] * shifted
                    acc = term if acc is None else acc + term
                    dw_ref[k:k + 1, cs] += _rowsum(shifted * u)
                dup_ref[:, cs] = acc.astype(BF16)

    hn = _halo_next_index(tm, halo, T)
    return _pallas(
        comm, body, name=name, grid=(T // tm,),
        in_specs=[pl.BlockSpec((tm, 2 * D_FF), lambda i: (i, 0)),
                  pl.BlockSpec((tm, 2 * D_FF), lambda i: (i, 0)),
                  pl.BlockSpec((halo, 2 * D_FF), lambda i: (hn(i), 0)),
                  pl.BlockSpec((tm, D_FF), lambda i: (i, 0)),
                  pl.BlockSpec((halo, D_FF), lambda i: (hn(i), 0)),
                  pl.BlockSpec((None, F_CONV, 2 * D_FF), lambda i: (layer, 0, 0))],
        out_specs=[pl.BlockSpec((tm, 2 * D_FF), lambda i: (i, 0)),
                   pl.BlockSpec((F_CONV, 2 * D_FF), lambda i: (0, 0))],
        out_shape=[jax.ShapeDtypeStruct((T, 2 * D_FF), BF16),
                   jax.ShapeDtypeStruct((F_CONV, 2 * D_FF), F32)],
        compiler_params=_cparams("arbitrary"),
    )(up, upc, upc, da, da, conv_w)


def _local_step(x, tgt, small, plan):
    T = x.shape[0]
    tm_e = _pick(T, 256)
    tm_a = _pick(T, 512)
    tm_b = _pick(T, 128)
    tm = _pick(T, 1024)
    tm_f = _pick(T, 512)
    tt = _pick(T, 2048)
    nm = small["norm_mix"].reshape(2, 1, D_MODEL)
    nf = small["norm_ffn"].reshape(2, 1, D_MODEL)
    ngf = small["norm_final"].reshape(1, D_MODEL)
    b_s = small["a_b_s"].reshape(A_HEADS, CHUNK, 1)
    w_s = small["a_w_s"].reshape(A_HEADS, CHUNK, CHUNK)
    b_conv_w = small["b_conv_w"].reshape(B_CONV, D_B)
    sg = {}
    wt, cm = plan.weight, plan.comm

    h_m0, z_ab = _norm_mm_nn(x, nm, wt("ab_w_in", 0), g_layer=0, tm=tm, tn=512, name="ab_in", comm=cm("ab_in"))
    yab, cb = _mixer_ab_fwd(z_ab, small["a_ln_g"], small["a_ln_b"], w_s, b_s, b_conv_w, small["b_conv_b"],
                            small["b_ln_g"], small["b_ln_b"], tm=tm_e, name="mixer_ab", comm=cm("mixer_ab"))
    x1, h_f0 = _mm_nn(yab, wt("ab_w_out", 0), layer=0, tm=tm, tn=D_MODEL, residual=x, norm=(nf, 0),
                      name="ab_out", comm=cm("ab_out"))

    def ffn_fwd(xin, h, layer):
        up = _mm_nn(h, wt("f_w_up", layer), layer=0, tm=tm, tn=2 * 1408, out_dtype=BF16, name=f"ffn_up{layer}",
                    comm=cm(f"ffn_up{layer}"))
        a, upc = _ffn_act_fwd(up, small["f_conv_w"], layer=layer, tm=tm_a, name=f"ffn_act{layer}",
                              comm=cm(f"ffn_act{layer}"))
        if layer == 0:
            out = _mm_nn(a, wt("f_w_down", layer), layer=0, tm=tm, tn=D_MODEL, residual=xin, norm=(nm, 1),
                         name=f"ffn_down{layer}", comm=cm(f"ffn_down{layer}"))
        else:
            out = _mm_nn_loss(a, wt("f_w_down", layer), xin, tgt, ngf, tm=tm, name=f"ffn_down{layer}",
                              comm=cm(f"ffn_down{layer}"))
        return up, upc, a, out

    up0, upc0, a0, (x2, h_m1) = ffn_fwd(x1, h_f0, 0)
    z_c = _mm_nn(h_m1, wt("c_w_in", 0), layer=0, tm=tm, tn=768, out_dtype=BF16, name="c_in", comm=cm("c_in"))
    r = _mixer_c_fwd(z_c, small["c_conv_w"], tm=tm_e, name="mixer_c", comm=cm("mixer_c"))
    x3, h_f1 = _mm_nn(r, wt("c_w_out", 0), layer=0, tm=tm, tn=D_MODEL, residual=x2, norm=(nf, 1),
                      name="c_out", comm=cm("c_out"))
    up1, upc1, a1, (loss, dx, sg["norm_final"]) = ffn_fwd(x3, h_f1, 1)

    def ffn_bwd(dx, xin, h, up, upc, a, layer):
        da = _mm_nt(dx, wt("f_w_down", layer), layer=0, tm=tm, tn=1408, out_dtype=BF16,
                    name=f"ffn_down_dx{layer}", comm=cm(f"ffn_down_dx{layer}"))
        plan.grad_ready("f_w_down", layer, _mm_tn(a, dx, shards=None, tk=1408, tn=1024, tt=tt,
                                                  name=f"ffn_down_dw{layer}", comm=cm(f"ffn_down_dw{layer}")))
        dup, dcw = _ffn_act_bwd(up, upc, da, small["f_conv_w"], layer=layer, tm=tm_b, name=f"ffn_act_bwd{layer}",
                                comm=cm(f"ffn_act_bwd{layer}"))
        plan.grad_ready("f_w_up", layer, _mm_tn(h, dup, shards=N_CHIPS, tk=512, tn=2 * 1408, tt=tt,
                                                name=f"ffn_up_dw{layer}", comm=cm(f"ffn_up_dw{layer}")))
        dxin, dg = _mm_nt_norm(dup, wt("f_w_up", layer), xin, nf, dx, g_layer=layer, tm=tm_f,
                               name=f"ffn_up_dx{layer}", comm=cm(f"ffn_up_dx{layer}"))
        return dxin, dg, dcw

    dx, dnf1, dfc1 = ffn_bwd(dx, x3, h_f1, up1, upc1, a1, 1)
    dr = _mm_nt(dx, wt("c_w_out", 0), layer=0, tm=tm, tn=512, out_dtype=BF16, name="c_out_dx", comm=cm("c_out_dx"))
    plan.grad_ready("c_w_out", 0, _mm_tn(r, dx, shards=None, tk=1024, tn=1024, tt=tt, name="c_out_dw",
                                         comm=cm("c_out_dw")))
    dz_c, dccw = _mixer_c_bwd(z_c, dr, small["c_conv_w"], tm=tm_e, name="mixer_c_bwd", comm=cm("mixer_c_bwd"))
    sg["c_conv_w"] = dccw.reshape(1, C_CONV, D_MODEL)
    plan.grad_ready("c_w_in", 0, _mm_tn(h_m1, dz_c, shards=N_CHIPS, tk=1024, tn=768, tt=tt, name="c_in_dw",
                                        comm=cm("c_in_dw")))
    dx, dnm1 = _mm_nt_norm(dz_c, wt("c_w_in", 0), x2, nm, dx, g_layer=1, tm=tm_f, name="c_in_dx",
                           comm=cm("c_in_dx"))
    dx, dnf0, dfc0 = ffn_bwd(dx, x1, h_f0, up0, upc0, a0, 0)
    dyab = _mm_nt(dx, wt("ab_w_out", 0), layer=0, tm=tm, tn=512, out_dtype=BF16, name="ab_out_dx",
                  comm=cm("ab_out_dx"))
    plan.grad_ready("ab_w_out", 0, _mm_tn(yab, dx, shards=None, tk=1024, tn=1024, tt=tt, name="ab_out_dw",
                                          comm=cm("ab_out_dw")))
    (dza, dcb, sg["a_ln_g"], sg["a_ln_b"], dws, dbs, sg["b_ln_g"], sg["b_ln_b"]) = _mixer_ab_bwd_pre(
        z_ab, cb, dyab, small["a_ln_g"], small["a_ln_b"], w_s, b_s, small["b_ln_g"], small["b_ln_b"],
        tm=tm_e, name="mixer_ab_bwd", comm=cm("mixer_ab_bwd"))
    dz_ab, dbcw, sg["b_conv_b"] = _mixer_b_conv_bwd(z_ab, dcb, b_conv_w, dza, tm=tm_e, name="mixer_b_conv_bwd",
                                                    comm=cm("mixer_b_conv_bwd"))
    sg["a_w_s"] = dws.reshape(1, A_HEADS, CHUNK, CHUNK)
    sg["a_b_s"] = dbs.reshape(1, A_HEADS, CHUNK)
    sg["b_conv_w"] = dbcw.reshape(1, B_CONV, D_B)
    plan.grad_ready("ab_w_in", 0, _mm_tn(h_m0, dz_ab, shards=N_CHIPS, tk=1024, tn=512, tt=tt, name="ab_in_dw",
                                         comm=cm("ab_in_dw")))
    dx, dnm0 = _mm_nt_norm(dz_ab, wt("ab_w_in", 0), x, nm, dx, g_layer=0, tm=tm_f, name="ab_in_dx",
                           comm=cm("ab_in_dx"))

    sg["norm_mix"] = [dnm0, dnm1]
    sg["norm_ffn"] = [dnf0, dnf1]
    sg["f_conv_w"] = [dfc0, dfc1]
    return loss, dx, sg


BLOCK_BYTES = 3 * 1024 * 1024


BF16_SUBLANES = 16


def _row_tile(rows, row_bytes, step=SUBLANES):
    best = None
    for tr in range(step, rows + 1, step):
        if rows % tr == 0 and tr * row_bytes <= BLOCK_BYTES:
            best = tr
    if best is None:
        raise ValueError(f"no row tile for {rows}")
    return best


def _place_scalars():
    x, y, c = lax.axis_index("x"), lax.axis_index("y"), lax.axis_index("c")
    return jnp.stack([c, 2 * x + y, 2 * (1 - x) + y, 2 * x + (1 - y), 2 * (1 - x) + (1 - y)]).astype(jnp.int32)


def _cast_into_slot(w, place, *, layer, paired, name):
    L, rows, cols = w.shape
    tr = _row_tile(rows, cols * 4, BF16_SUBLANES)

    def body(place_ref, w_ref, o_ref):
        o_ref[...] = w_ref[...].astype(BF16)

    if paired:
        out_spec = pl.BlockSpec((None, None, tr, cols), lambda i, p: (0, p[1] // 2, i, p[1] % 2))
        out_shape = jax.ShapeDtypeStruct((1, N_CHIPS // 2, rows, 2 * cols), BF16)
    else:
        out_spec = pl.BlockSpec((None, None, tr, cols), lambda i, p: (0, p[1], i, 0))
        out_shape = jax.ShapeDtypeStruct((1, N_CHIPS, rows, cols), BF16)
    return pl.pallas_call(
        body, name=name,
        grid_spec=pltpu.PrefetchScalarGridSpec(
            num_scalar_prefetch=1, grid=(rows // tr,),
            in_specs=[pl.BlockSpec((None, tr, cols), lambda i, p: (layer, i, 0))],
            out_specs=out_spec),
        out_shape=out_shape,
        compiler_params=_cparams("parallel"),
    )(place, w)


def _pair_sum(g, theirs, place, *, name):
    S, rows, cols = g.shape
    half = rows // 2
    tr = _row_tile(half, cols * 4, BF16_SUBLANES)
    nb = half // tr

    def body(place_ref, g_ref, t_ref, o_ref):
        o_ref[...] = (g_ref[...] + t_ref[...]).astype(BF16)

    spec = pl.BlockSpec((None, tr, cols), lambda s, i, p: (s, i, 0))
    return pl.pallas_call(
        body, name=name,
        grid_spec=pltpu.PrefetchScalarGridSpec(
            num_scalar_prefetch=1, grid=(S, nb),
            in_specs=[pl.BlockSpec((None, tr, cols), lambda s, i, p: (s, p[0] * nb + i, 0)), spec],
            out_specs=spec),
        out_shape=jax.ShapeDtypeStruct((S, half, cols), BF16),
        compiler_params=_cparams("parallel", "parallel"),
    )(place, g, theirs)


def _chip_sum(p, r, g_prev, place, *, layer, shape, name):
    L, rows, cols = shape
    half = rows // 2
    tr = _row_tile(half, cols * 4, BF16_SUBLANES)
    nb = half // tr

    def body(place_ref, p_ref, r_ref, *rest):
        o_ref = rest[-1]
        mine = p_ref[...].astype(F32)
        peers = [r_ref[j].astype(F32) for j in range(3)]
        acc = None
        for s in range(N_CHIPS):
            term = jnp.where(place_ref[1] == s, mine,
                             jnp.where(place_ref[2] == s, peers[0],
                                       jnp.where(place_ref[3] == s, peers[1], peers[2])))
            acc = term if acc is None else acc + term
        o_ref[...] = acc

    in_specs = [pl.BlockSpec((None, tr, cols), lambda i, pr: (pr[1], i, 0)),
                pl.BlockSpec((3, tr, cols), lambda i, pr: (0, i, 0))]
    args = [place, p, r]
    aliases = {}
    if g_prev is not None:
        in_specs.append(HBM_REF)
        args.append(g_prev)
        aliases = {3: 0}
    return pl.pallas_call(
        body, name=name,
        grid_spec=pltpu.PrefetchScalarGridSpec(
            num_scalar_prefetch=1, grid=(nb,), in_specs=in_specs,
            out_specs=pl.BlockSpec((None, tr, cols), lambda i, pr: (layer, pr[0] * nb + i, 0))),
        out_shape=jax.ShapeDtypeStruct(shape, F32), input_output_aliases=aliases,
        compiler_params=_cparams("parallel"),
    )(*args)


def _adamw_math(w, g, m, v):
    m2 = ADAM_B1 * m + (1.0 - ADAM_B1) * g
    v2 = ADAM_B2 * v + (1.0 - ADAM_B2) * (g * g)
    m_hat = m2 / (1.0 - ADAM_B1 ** ADAM_STEP)
    v_hat = v2 / (1.0 - ADAM_B2 ** ADAM_STEP)
    delta = -ADAM_LR * (m_hat / (jnp.sqrt(v_hat) + ADAM_EPS) + ADAM_WD * w)
    return delta, m2, v2


def _adamw(w, g, m, v, *, name):
    L, rows, cols = w.shape
    tr = _row_tile(rows, cols * 4)

    def body(w_ref, g_ref, m_ref, v_ref, go_ref, d_ref, m2_ref, v2_ref):
        g = g_ref[...]
        d, m2, v2 = _adamw_math(w_ref[...], g, m_ref[...], v_ref[...])
        go_ref[...] = g
        d_ref[...] = d
        m2_ref[...] = m2
        v2_ref[...] = v2

    spec = pl.BlockSpec((None, tr, cols), lambda l, i: (l, i, 0))
    shape = jax.ShapeDtypeStruct(w.shape, F32)
    return pl.pallas_call(
        body, name=name, grid=(L, rows // tr), in_specs=[spec] * 4, out_specs=[spec] * 4,
        out_shape=[shape] * 4,
        compiler_params=_cparams("parallel", "parallel"),
    )(w, g, m, v)


def _allreduce_pack(pack, *, name, comm):
    R = pack.shape[0]
    half = R // 2
    nr, nw = len(comm.reads), len(comm.writes)

    def body(*refs):
        p_ref, rd, wr_in = refs[0], refs[1:1 + nr], refs[1 + nr:1 + nr + nw]
        o_ref, wr_out = refs[1 + nr + nw], refs[2 + nr + nw:2 + nr + 2 * nw]
        sib_ref, chip_ref, parts_ref, sems, comm_sems = refs[2 + nr + 2 * nw:]
        src = dict(zip(comm.reads, rd))
        src.update(zip(comm.writes, wr_in))
        dst = dict(zip(comm.writes, wr_out))
        comm.start(src, dst, comm_sems)
        x, y, c, k, sib, peers = _place()
        swap = _remote(p_ref, sib_ref, sems.at[0, 0], sems.at[0, 1], sib)
        swap.start()
        swap.wait()
        chip_ref[...] = p_ref[...] + sib_ref[...]
        mine = chip_ref.at[pl.ds(pl.multiple_of(c * half, SUBLANES), half)]
        sends = [_remote(mine, parts_ref.at[j], sems.at[1 + j, 0], sems.at[1 + j, 1], (px, py, c))
                 for j, (px, py) in enumerate(peers)]
        for rc in sends:
            rc.start()
        for rc in sends:
            rc.wait()
        own = mine[...]
        others = [parts_ref[j] for j in range(3)]
        acc = None
        for s in range(N_CHIPS):
            term = own
            for j, (px, py) in enumerate(peers):
                term = jnp.where(2 * px + py == s, others[j], term)
            acc = term if acc is None else acc + term
        done = o_ref.at[pl.ds(pl.multiple_of(c * half, SUBLANES), half)]
        done[...] = acc
        theirs = o_ref.at[pl.ds(pl.multiple_of((1 - c) * half, SUBLANES), half)]
        share = _remote(done, done, sems.at[4, 0], sems.at[4, 1], sib)
        share.start()
        _remote(done, theirs, sems.at[4, 0], sems.at[4, 1], sib).wait()
        comm.finish(src, dst, comm_sems)

    vm = pl.BlockSpec(memory_space=pltpu.VMEM)
    operands, shapes = _comm_operands(comm)
    outs = pl.pallas_call(
        body, name=name, in_specs=[vm] + [HBM_REF] * (nr + nw), out_specs=[vm] + [HBM_REF] * nw,
        out_shape=[jax.ShapeDtypeStruct((R, LANES), F32)] + shapes,
        input_output_aliases={1 + nr + q: 1 + q for q in range(nw)},
        scratch_shapes=[pltpu.VMEM((R, LANES), F32), pltpu.VMEM((R, LANES), F32),
                        pltpu.VMEM((3, half, LANES), F32), pltpu.SemaphoreType.DMA((5, 2)),
                        pltpu.SemaphoreType.DMA((comm.ncopies, 2))],
        compiler_params=pltpu.CompilerParams(vmem_limit_bytes=VMEM_BYTES_MAX),
    )(pack, *operands)
    for q, n in enumerate(comm.writes):
        comm.plan.bufs[n] = outs[1 + q]
    return outs[0]


PACK_UNIT = SUBLANES * LANES


def _pack(arrays):
    flat, sizes = [], []
    for a in arrays:
        pieces = a if isinstance(a, (list, tuple)) else [a]
        v = jnp.concatenate([p.reshape(-1) for p in pieces]) if len(pieces) > 1 else pieces[0].reshape(-1)
        size = v.shape[0]
        padded = -(-size // PACK_UNIT) * PACK_UNIT
        flat.append(jnp.pad(v, (0, padded - size)))
        sizes.append((size, padded))
    total = sum(p for _, p in sizes)
    if (total // PACK_UNIT) % 2:
        flat.append(jnp.zeros((PACK_UNIT,), F32))
    return jnp.concatenate(flat).reshape(-1, LANES), sizes


def _unpack(pack, sizes, shapes):
    v = pack.reshape(-1)
    out, off = [], 0
    for (size, padded), shape in zip(sizes, shapes):
        out.append(v[off:off + size].reshape(shape))
        off += padded
    return out


BIG = ("ab_w_in", "ab_w_out", "c_w_in", "c_w_out", "f_w_up", "f_w_down")
COL_SHARDED = ("ab_w_in", "c_w_in", "f_w_up")
PAIRED = ("f_w_up",)
SMALL_REPLICATED = ("norm_mix", "norm_ffn", "norm_final", "a_ln_g", "a_ln_b", "a_w_s", "a_b_s",
                    "b_conv_b", "b_ln_g", "b_ln_b")
SMALL_SHARDED = ("b_conv_w", "c_conv_w", "f_conv_w")
SMALL = SMALL_REPLICATED + SMALL_SHARDED
ALL_WEIGHTS = ("norm_mix", "norm_ffn", "norm_final", "ab_w_in", "a_ln_g", "a_ln_b", "a_w_s", "a_b_s",
               "b_conv_w", "b_conv_b", "b_ln_g", "b_ln_b", "ab_w_out", "c_w_in", "c_conv_w", "c_w_out",
               "f_w_up", "f_conv_w", "f_w_down")


LATE = "late"
SCHEDULE = {
    "ab_in": [("gi", "f_w_up", 0, 0, 4), ("gi", "ab_w_out", 0)],
    "mixer_ab": [("gd", "f_w_up", 0, 0, 4), ("gd", "ab_w_out", 0), ("gi", "f_w_up", 0, 1, 4),
                 ("gi", "f_w_up", 0, 2, 4)],
    "ab_out": [("gd", "f_w_up", 0, 1, 4), ("gd", "f_w_up", 0, 2, 4), ("gi", "f_w_up", 0, 3, 4), (LATE, 1),
               ("gd", "f_w_up", 0, 3, 4)],
    "ffn_up0": [("gi", "f_w_down", 0), ("gi", "c_w_in", 0, 0, 2)],
    "ffn_act0": [("gd", "f_w_down", 0), ("gd", "c_w_in", 0, 0, 2), ("gi", "c_w_in", 0, 1, 2),
                 ("gi", "f_w_up", 1, 0, 4)],
    "ffn_down0": [("gd", "c_w_in", 0, 1, 2), ("gd", "f_w_up", 1, 0, 4), ("gi", "f_w_up", 1, 1, 4),
                  ("gi", "c_w_out", 0)],
    "c_in": [("gd", "f_w_up", 1, 1, 4), ("gd", "c_w_out", 0), ("gi", "f_w_up", 1, 2, 4),
             ("gi", "f_w_up", 1, 3, 4), (LATE, 1), ("gd", "f_w_up", 1, 2, 4), ("gd", "f_w_up", 1, 3, 4)],
    "ffn_up1": [("gi", "f_w_down", 1), (LATE, 2), ("gd", "f_w_down", 1)],
    "ffn_act_bwd1": [("px", "f_w_down", 1)],
    "ffn_up_dx1": [("cx", "f_w_down", 1), ("px", "f_w_up", 1)],
    "mixer_c_bwd": [("cx", "f_w_up", 1, 0, 4), ("px", "c_w_out", 0)],
    "c_in_dx": [("cx", "f_w_up", 1, 1, 4), ("px", "c_w_in", 0)],
    "ffn_act_bwd0": [("cx", "f_w_up", 1, 2, 4), ("cx", "c_w_out", 0), ("cx", "c_w_in", 0),
                     ("px", "f_w_down", 0), ("ps", "f_w_down", 1)],
    "ffn_up_dx0": [("cx", "f_w_down", 0), ("cx", "f_w_up", 1, 3, 4), ("px", "f_w_up", 0)],
    "mixer_ab_bwd": [("cx", "f_w_up", 0, 0, 4), ("px", "ab_w_out", 0)],
    "mixer_b_conv_bwd": [("cx", "f_w_up", 0, 1, 4), ("cx", "f_w_up", 0, 2, 4), ("cx", "ab_w_out", 0),
                         ("ps", "c_w_out", 0), ("ps", "c_w_in", 0), ("ps", "f_w_down", 0),
                         ("ps", "f_w_up", 1)],
    "ab_in_dx": [("cx", "f_w_up", 0, 3, 4), ("px", "ab_w_in", 0)],
}


class _Plan:
    def __init__(self, shapes, place):
        self.shapes, self.place, self.bufs = shapes, place, {}
        self.summed, self.shared = set(), set()

    def weight(self, name, layer):
        g = self.bufs[f"w:{name}:{layer}"]
        if name in COL_SHARDED:
            return g
        _, S, rows, cols = g.shape
        return g.reshape(1, S * rows, cols)

    def grad_ready(self, name, layer, g):
        _, rows, cols = self.shapes[name]
        hbm = lambda a: pltpu.with_memory_space_constraint(a, pltpu.HBM)
        self.bufs[f"g:{name}:{layer}"] = g.reshape(N_CHIPS, rows, cols)
        self.bufs[f"t:{name}:{layer}"] = hbm(lax.empty((N_CHIPS, rows // 2, cols), F32))
        self.bufs[f"l:{name}:{layer}"] = hbm(lax.empty((3, rows // 2, cols), BF16))

    def job(self, kind, name, layer, part=0, parts=1):
        _, rows, cols = self.shapes[name]
        key = f"{name}:{layer}"
        if kind == "gi":
            return _job_gather_ici("w:" + key, rows, cols, part, parts)
        if kind == "gd":
            return _job_gather_d2d("w:" + key, rows, cols, part, parts)
        if kind == "px":
            return _job_pair_exchange("g:" + key, "t:" + key, rows, part, parts)
        if kind == "cx":
            if "p:" + key not in self.bufs:
                self.bufs["p:" + key] = _pair_sum(self.bufs["g:" + key], self.bufs["t:" + key], self.place,
                                                  name=f"pair_sum_{name}{layer}")
            nr = rows // 2 // parts
            return _job_chip_exchange("p:" + key, "l:" + key, part * nr, nr)
        if kind == "ps":
            self.chip_sum(name, layer)
            self.shared.add(key)
            return _job_pair_share("G:" + name, layer, rows)
        raise ValueError(kind)

    def chip_sum(self, name, layer):
        key = f"{name}:{layer}"
        if key not in self.summed:
            self.summed.add(key)
            self.bufs["G:" + name] = _chip_sum(self.bufs["p:" + key], self.bufs["l:" + key],
                                               self.bufs.get("G:" + name), self.place, layer=layer,
                                               shape=self.shapes[name], name=f"chip_sum_{name}{layer}")

    def comm(self, call):
        specs = SCHEDULE.get(call)
        if specs is None:
            return None
        marks = [i for i, spec in enumerate(specs) if spec[0] == LATE]
        late, late_steps = (len(specs) - 1 - marks[0], specs[marks[0]][1]) if marks else (0, 0)
        return _Comm(self, [self.job(*spec) for spec in specs if spec[0] != LATE], late, late_steps)


def _step(x, tgt, w, m, v):
    chip = 2 * lax.axis_index("x") + lax.axis_index("y")
    place = _place_scalars()
    plan = _Plan({n: w[n].shape for n in BIG}, place)
    items = [(n, l) for n in BIG for l in range(w[n].shape[0])]

    for n, l in items:
        plan.bufs[f"w:{n}:{l}"] = _cast_into_slot(w[n], place, layer=l, paired=n in PAIRED, name=f"cast_{n}{l}")
    conv_pack, conv_sizes = _pack([w[n] for n in SMALL_SHARDED])
    hbm = lambda a: pltpu.with_memory_space_constraint(a, pltpu.HBM)
    plan.bufs["conv:mine"] = hbm(conv_pack)
    plan.bufs["conv:all"] = hbm(lax.empty((N_CHIPS,) + conv_pack.shape, F32))
    _comm_only(plan, [[plan.job("gi", "ab_w_in", 0), _job_chip_gather("conv:mine", "conv:all")],
                      [plan.job("gd", "ab_w_in", 0)]], name="gather_first")
    conv_shapes = [w[n].shape for n in SMALL_SHARDED]
    per_chip = [_unpack(plan.bufs["conv:all"][s], conv_sizes, conv_shapes) for s in range(N_CHIPS)]
    small = {n: w[n] for n in SMALL_REPLICATED}
    for idx, n in enumerate(SMALL_SHARDED):
        small[n] = jnp.concatenate([jnp.where(chip == s, w[n], per_chip[s][idx]) for s in range(N_CHIPS)], axis=-1)

    loss, dx, sg = _local_step(x, tgt, small, plan)

    g_pack, g_sizes = _pack([sg[n] for n in SMALL] + [loss])
    g_sum = _allreduce_pack(g_pack, name="allreduce_small_grads",
                            comm=_Comm(plan, [plan.job("cx", "ab_w_in", 0)]))
    full_shapes = [small[n].shape for n in SMALL]
    *summed, loss = _unpack(g_sum, g_sizes, full_shapes + [(1, 1)])
    g_small = dict(zip(SMALL, summed))
    for n in SMALL_SHARDED:
        width = w[n].shape[-1]
        g_small[n] = lax.dynamic_slice_in_dim(g_small[n], chip * width, width, axis=g_small[n].ndim - 1)

    _comm_only(plan, [[plan.job("ps", n, l) for n, l in items if f"{n}:{l}" not in plan.shared]],
               name="reduce_pair_share")
    grads_big = [plan.bufs["G:" + n] for n in BIG]

    grad, delta, new_m, new_v = {}, {}, {}, {}
    for n, g in zip(BIG, grads_big):
        grad[n], delta[n], new_m[n], new_v[n] = _adamw(w[n], g, m[n], v[n], name=f"adamw_{n}")
    shapes = [w[n].shape for n in SMALL]
    wp, sizes = _pack([w[n] for n in SMALL])
    gp, _ = _pack([g_small[n] for n in SMALL])
    mp, _ = _pack([m[n] for n in SMALL])
    vp, _ = _pack([v[n] for n in SMALL])
    R = wp.shape[0]
    _, dp, m2p, v2p = _adamw(wp.reshape(1, R, LANES), gp.reshape(1, R, LANES), mp.reshape(1, R, LANES),
                             vp.reshape(1, R, LANES), name="adamw_small")
    for n, d_, m_, v_ in zip(SMALL, _unpack(dp, sizes, shapes), _unpack(m2p, sizes, shapes),
                             _unpack(v2p, sizes, shapes)):
        grad[n] = g_small[n]
        delta[n], new_m[n], new_v[n] = d_, m_, v_
    return loss, dx, grad, delta, new_m, new_v


def kernel(x, norm_mix, norm_ffn, norm_final, ab_w_in, a_ln_g, a_ln_b, a_w_s, a_b_s, b_conv_w, b_conv_b, b_ln_g, b_ln_b, ab_w_out, c_w_in, c_conv_w, c_w_out, f_w_up, f_conv_w, f_w_down, loss_target, m_norm_mix, m_norm_ffn, m_norm_final, m_ab_w_in, m_a_ln_g, m_a_ln_b, m_a_w_s, m_a_b_s, m_b_conv_w, m_b_conv_b, m_b_ln_g, m_b_ln_b, m_ab_w_out, m_c_w_in, m_c_conv_w, m_c_w_out, m_f_w_up, m_f_conv_w, m_f_w_down, v_norm_mix, v_norm_ffn, v_norm_final, v_ab_w_in, v_a_ln_g, v_a_ln_b, v_a_w_s, v_a_b_s, v_b_conv_w, v_b_conv_b, v_b_ln_g, v_b_ln_b, v_ab_w_out, v_c_w_in, v_c_conv_w, v_c_w_out, v_f_w_up, v_f_conv_w, v_f_w_down):
    given = dict(locals())
    w = {n: given[n] for n in ALL_WEIGHTS}
    m = {n: given["m_" + n] for n in ALL_WEIGHTS}
    v = {n: given["v_" + n] for n in ALL_WEIGHTS}
    T = x.shape[1]
    loss, dx, grad, delta, new_m, new_v = _step(x.reshape(T, D_MODEL), loss_target.reshape(T, D_MODEL), w, m, v)
    out = [loss[0, 0], dx.reshape(x.shape)]
    for d in (grad, delta, new_m, new_v):
        out += [d[n] for n in ALL_WEIGHTS]
    return tuple(out)
```
